```python
import math
import jax, jax.numpy as jnp
from jax import lax
import numpy as np

D_MODEL = 1024
BATCH = 8
SEQ = 4096
DEPTH = 1

CHUNK = 64
Q_BLOCK = 128
HEAD_DIM = 64
N_RET_HEADS = 8
N_FOX_HEADS = 8
RET_WIDTH = N_RET_HEADS * HEAD_DIM
FOX_WIDTH = N_FOX_HEADS * HEAD_DIM
MIX_WIDTH = RET_WIDTH + FOX_WIDTH
N_MEM = 256
N_XATTN_HEADS = 4
XATTN_HEAD_DIM = D_MODEL // N_XATTN_HEADS
D_FF = -(-8 * D_MODEL // (3 * 256)) * 256
ROPE_BASE = 10000.0
EPS = 1e-6
NEG_INF = -1e30
IN_SIZES = (RET_WIDTH, RET_WIDTH, RET_WIDTH, RET_WIDTH, FOX_WIDTH, FOX_WIDTH, FOX_WIDTH, N_FOX_HEADS)
IN_WIDTH = sum(IN_SIZES)

kernel_name = "hymba_retention_fox_hybrid_block"


def _split_points(sizes):
    pts, acc = [], 0
    for s in sizes[:-1]:
        acc += s
        pts.append(acc)
    return pts


def rmsnorm(x, g):
    xf = x.astype(jnp.float32)
    y = xf * lax.rsqrt(jnp.mean(xf * xf, axis=-1, keepdims=True) + EPS)
    return (y * g.astype(jnp.float32)).astype(x.dtype)


def head_group_norm(x, g):
    xf = x.astype(jnp.float32)
    mu = jnp.mean(xf, axis=-1, keepdims=True)
    xc = xf - mu
    var = jnp.mean(xc * xc, axis=-1, keepdims=True)
    return (xc * lax.rsqrt(var + EPS) * g.astype(jnp.float32)).astype(x.dtype)


def rotary(x, pos):
    d = x.shape[-1]
    inv_freq = ROPE_BASE ** (-jnp.arange(0, d, 2, dtype=jnp.float32) / d)
    ang = pos[:, None] * inv_freq[None, :]
    cos = jnp.cos(ang)[:, None, :].astype(x.dtype)
    sin = jnp.sin(ang)[:, None, :].astype(x.dtype)
    x1, x2 = x[..., : d // 2], x[..., d // 2:]
    return jnp.concatenate([x1 * cos - x2 * sin, x1 * sin + x2 * cos], axis=-1)


def chunk_retention(q, k, v):
    B, T, H, d = q.shape
    dv = v.shape[-1]
    nc = T // CHUNK
    dt = q.dtype
    log_g = jnp.log(1.0 - 2.0 ** (-5.0 - jnp.arange(H, dtype=jnp.float32)))
    idx = jnp.arange(CHUNK, dtype=jnp.float32)
    intra_decay = jnp.exp(log_g[:, None, None] * jnp.abs(idx[:, None] - idx[None, :])).astype(dt)
    q_decay = jnp.exp(log_g[None, :] * (idx[:, None] + 1.0)).astype(dt)
    k_decay = jnp.exp(log_g[None, :] * (CHUNK - 1.0 - idx[:, None])).astype(dt)
    chunk_decay = jnp.exp(log_g * CHUNK).astype(dt)[:, None, None]

    qc = (q * (d ** -0.5)).reshape(B, nc, CHUNK, H, d)
    kc = k.reshape(B, nc, CHUNK, H, d)
    vc = v.reshape(B, nc, CHUNK, H, dv)

    scores = jnp.einsum('bnihd,bnjhd->bnhij', qc, kc) * intra_decay
    intra = jnp.einsum('bnhij,bnjhe->bnihe', scores, vc)

    kv = jnp.einsum('bnjhd,bnjhe->nbhde', kc * k_decay[:, :, None], vc)

    def step(state, kv_c):
        return state * chunk_decay + kv_c, state

    _, s_prev = lax.scan(step, jnp.zeros((B, H, d, dv), kv.dtype), kv)
    inter = jnp.einsum('bnihd,nbhde->bnihe', qc * q_decay[:, :, None], s_prev)
    return (intra + inter).reshape(B, T, H, dv)


def forgetting_attention(q, k, v, log_f):
    B, T, H, d = q.shape
    F = jnp.cumsum(log_f, axis=1).transpose(0, 2, 1)
    qh = (q * (d ** -0.5)).transpose(0, 2, 1, 3)
    kh = k.transpose(0, 2, 1, 3)
    vh = v.transpose(0, 2, 1, 3)
    qpos = jnp.arange(Q_BLOCK)
    outs = []
    for blk in range(T // Q_BLOCK):
        q0 = blk * Q_BLOCK
        kend = q0 + Q_BLOCK
        logits = jnp.einsum('bhqd,bhkd->bhqk', qh[:, :, q0:kend], kh[:, :, :kend]).astype(jnp.float32)
        logits = logits + F[:, :, q0:kend, None] - F[:, :, None, :kend]
        causal = (q0 + qpos)[:, None] >= jnp.arange(kend)[None, :]
        logits = jnp.where(causal, logits, NEG_INF)
        p = jax.nn.softmax(logits, axis=-1).astype(v.dtype)
        outs.append(jnp.einsum('bhqk,bhkd->bhqd', p, vh[:, :, :kend]))
    return jnp.concatenate(outs, axis=2).transpose(0, 2, 1, 3)


def memory_cross_attention(hn, mem, w_xq, w_xkv, g_mem, g_xq, g_xk, w_xo):
    B, T, _ = hn.shape
    M = mem.shape[1]
    q = (hn @ w_xq).reshape(B, T, N_XATTN_HEADS, XATTN_HEAD_DIM)
    q = rmsnorm(q, g_xq)
    kv = rmsnorm(mem, g_mem) @ w_xkv
    k, v = jnp.split(kv, 2, axis=-1)
    k = rmsnorm(k.reshape(B, M, N_XATTN_HEADS, XATTN_HEAD_DIM), g_xk)
    v = v.reshape(B, M, N_XATTN_HEADS, XATTN_HEAD_DIM)
    logits = jnp.einsum('bthd,bmhd->bhtm', q, k).astype(jnp.float32) * (XATTN_HEAD_DIM ** -0.5)
    p = jax.nn.softmax(logits, axis=-1).astype(v.dtype)
    o = jnp.einsum('bhtm,bmhd->bthd', p, v).reshape(B, T, D_MODEL)
    return o @ w_xo


def _fwd_setup_inputs(seed: int = 0) -> dict:
    key = jax.random.key(seed)
    ks = jax.random.split(key, 24)
    f32 = jnp.float32
    D = D_MODEL

    def w(k, shape, fan_in):
        return jax.random.normal(k, shape, f32) * fan_in ** -0.5

    def gain(k, shape):
        return 1.0 + 0.02 * jax.random.normal(k, shape, f32)

    return {
        "x": jax.random.normal(ks[0], (BATCH, SEQ, D), f32),
        "mem": jax.random.normal(ks[1], (BATCH, N_MEM, D), f32),
        "g_mix": gain(ks[2], (DEPTH, D)),
        "w_in": w(ks[3], (DEPTH, D, IN_WIDTH), D),
        "b_forget": jax.random.uniform(ks[4], (DEPTH, N_FOX_HEADS), f32, 1.0, 4.0),
        "g_ret_out": gain(ks[5], (DEPTH, N_RET_HEADS, HEAD_DIM)),
        "g_fox_q": gain(ks[6], (DEPTH, HEAD_DIM)),
        "g_fox_k": gain(ks[7], (DEPTH, HEAD_DIM)),
        "w_out": w(ks[8], (DEPTH, MIX_WIDTH, D), MIX_WIDTH),
        "g_xattn": gain(ks[9], (DEPTH, D)),
        "w_xq": w(ks[10], (DEPTH, D, D), D),
        "w_xkv": w(ks[11], (DEPTH, D, 2 * D), D),
        "g_mem": gain(ks[12], (DEPTH, D)),
        "g_xq": gain(ks[13], (DEPTH, XATTN_HEAD_DIM)),
        "g_xk": gain(ks[14], (DEPTH, XATTN_HEAD_DIM)),
        "w_xo": w(ks[15], (DEPTH, D, D), D),
        "g_ffn": gain(ks[16], (DEPTH, D)),
        "w_gate": w(ks[17], (DEPTH, D, D_FF), D),
        "w_up": w(ks[18], (DEPTH, D, D_FF), D),
        "w_down": w(ks[19], (DEPTH, D_FF, D), D_FF),
    }


def _fwd_reference(x, mem, g_mix, w_in, b_forget, g_ret_out, g_fox_q, g_fox_k, w_out,
              g_xattn, w_xq, w_xkv, g_mem, g_xq, g_xk, w_xo,
              g_ffn, w_gate, w_up, w_down):
    B, T, _ = x.shape
    pos = jnp.arange(T, dtype=jnp.float32)
    splits = _split_points(IN_SIZES)
    h = x
    for l in range(DEPTH):
        hn = rmsnorm(h, g_mix[l])
        proj = hn @ w_in[l]
        rq, rk, rv, rg, fq, fk, fv, ff = jnp.split(proj, splits, axis=-1)

        rq = rotary(rq.reshape(B, T, N_RET_HEADS, HEAD_DIM), pos)
        rk = rotary(rk.reshape(B, T, N_RET_HEADS, HEAD_DIM), pos)
        ret = chunk_retention(rq, rk, rv.reshape(B, T, N_RET_HEADS, HEAD_DIM))
        ret = head_group_norm(ret, g_ret_out[l]).reshape(B, T, RET_WIDTH)
        ret = jax.nn.silu(rg) * ret

        fq = rmsnorm(fq.reshape(B, T, N_FOX_HEADS, HEAD_DIM), g_fox_q[l])
        fk = rmsnorm(fk.reshape(B, T, N_FOX_HEADS, HEAD_DIM), g_fox_k[l])
        log_f = jax.nn.log_sigmoid(ff.astype(jnp.float32) + b_forget[l].astype(jnp.float32))
        fox = forgetting_attention(fq, fk, fv.reshape(B, T, N_FOX_HEADS, HEAD_DIM), log_f)
        fox = fox.reshape(B, T, FOX_WIDTH)

        h = h + jnp.concatenate([ret, fox], axis=-1) @ w_out[l]

        h = h + memory_cross_attention(rmsnorm(h, g_xattn[l]), mem, w_xq[l], w_xkv[l],
                                       g_mem[l], g_xq[l], g_xk[l], w_xo[l])

        hn = rmsnorm(h, g_ffn[l])
        h = h + (jax.nn.silu(hn @ w_gate[l]) * (hn @ w_up[l])) @ w_down[l]
    return h


import jax as _jax
import jax.numpy as _jnp

TWIN_FORMAT = 'train_step'
FWD_PARAMS = ['x', 'mem', 'g_mix', 'w_in', 'b_forget', 'g_ret_out', 'g_fox_q', 'g_fox_k', 'w_out', 'g_xattn', 'w_xq', 'w_xkv', 'g_mem', 'g_xq', 'g_xk', 'w_xo', 'g_ffn', 'w_gate', 'w_up', 'w_down']
TWIN_WEIGHTS = ['g_mix', 'w_in', 'b_forget', 'g_ret_out', 'g_fox_q', 'g_fox_k', 'w_out', 'g_xattn', 'w_xq', 'w_xkv', 'g_mem', 'g_xq', 'g_xk', 'w_xo', 'g_ffn', 'w_gate', 'w_up', 'w_down']
TWIN_DIFF_INPUT = 'x'
TWIN_INPUTS = ['x', 'mem', 'g_mix', 'w_in', 'b_forget', 'g_ret_out', 'g_fox_q', 'g_fox_k', 'w_out', 'g_xattn', 'w_xq', 'w_xkv', 'g_mem', 'g_xq', 'g_xk', 'w_xo', 'g_ffn', 'w_gate', 'w_up', 'w_down', 'loss_target', 'm_g_mix', 'm_w_in', 'm_b_forget', 'm_g_ret_out', 'm_g_fox_q', 'm_g_fox_k', 'm_w_out', 'm_g_xattn', 'm_w_xq', 'm_w_xkv', 'm_g_mem', 'm_g_xq', 'm_g_xk', 'm_w_xo', 'm_g_ffn', 'm_w_gate', 'm_w_up', 'm_w_down', 'v_g_mix', 'v_w_in', 'v_b_forget', 'v_g_ret_out', 'v_g_fox_q', 'v_g_fox_k', 'v_w_out', 'v_g_xattn', 'v_w_xq', 'v_w_xkv', 'v_g_mem', 'v_g_xq', 'v_g_xk', 'v_w_xo', 'v_g_ffn', 'v_w_gate', 'v_w_up', 'v_w_down']
TWIN_OUTPUTS = ['loss', 'grad_x', 'grad_g_mix', 'grad_w_in', 'grad_b_forget', 'grad_g_ret_out', 'grad_g_fox_q', 'grad_g_fox_k', 'grad_w_out', 'grad_g_xattn', 'grad_w_xq', 'grad_w_xkv', 'grad_g_mem', 'grad_g_xq', 'grad_g_xk', 'grad_w_xo', 'grad_g_ffn', 'grad_w_gate', 'grad_w_up', 'grad_w_down', 'delta_g_mix', 'delta_w_in', 'delta_b_forget', 'delta_g_ret_out', 'delta_g_fox_q', 'delta_g_fox_k', 'delta_w_out', 'delta_g_xattn', 'delta_w_xq', 'delta_w_xkv', 'delta_g_mem', 'delta_g_xq', 'delta_g_xk', 'delta_w_xo', 'delta_g_ffn', 'delta_w_gate', 'delta_w_up', 'delta_w_down', 'new_m_g_mix', 'new_m_w_in', 'new_m_b_forget', 'new_m_g_ret_out', 'new_m_g_fox_q', 'new_m_g_fox_k', 'new_m_w_out', 'new_m_g_xattn', 'new_m_w_xq', 'new_m_w_xkv', 'new_m_g_mem', 'new_m_g_xq', 'new_m_g_xk', 'new_m_w_xo', 'new_m_g_ffn', 'new_m_w_gate', 'new_m_w_up', 'new_m_w_down', 'new_v_g_mix', 'new_v_w_in', 'new_v_b_forget', 'new_v_g_ret_out', 'new_v_g_fox_q', 'new_v_g_fox_k', 'new_v_w_out', 'new_v_g_xattn', 'new_v_w_xq', 'new_v_w_xkv', 'new_v_g_mem', 'new_v_g_xq', 'new_v_g_xk', 'new_v_w_xo', 'new_v_g_ffn', 'new_v_w_gate', 'new_v_w_up', 'new_v_w_down']
TWIN_LEAF_KINDS = {'loss': 'loss', 'grad_x': 'grad_x', 'grad_g_mix': 'grad_w', 'grad_w_in': 'grad_w', 'grad_b_forget': 'grad_w', 'grad_g_ret_out': 'grad_w', 'grad_g_fox_q': 'grad_w', 'grad_g_fox_k': 'grad_w', 'grad_w_out': 'grad_w', 'grad_g_xattn': 'grad_w', 'grad_w_xq': 'grad_w', 'grad_w_xkv': 'grad_w', 'grad_g_mem': 'grad_w', 'grad_g_xq': 'grad_w', 'grad_g_xk': 'grad_w', 'grad_w_xo': 'grad_w', 'grad_g_ffn': 'grad_w', 'grad_w_gate': 'grad_w', 'grad_w_up': 'grad_w', 'grad_w_down': 'grad_w', 'delta_g_mix': 'delta_w', 'delta_w_in': 'delta_w', 'delta_b_forget': 'delta_w', 'delta_g_ret_out': 'delta_w', 'delta_g_fox_q': 'delta_w', 'delta_g_fox_k': 'delta_w', 'delta_w_out': 'delta_w', 'delta_g_xattn': 'delta_w', 'delta_w_xq': 'delta_w', 'delta_w_xkv': 'delta_w', 'delta_g_mem': 'delta_w', 'delta_g_xq': 'delta_w', 'delta_g_xk': 'delta_w', 'delta_w_xo': 'delta_w', 'delta_g_ffn': 'delta_w', 'delta_w_gate': 'delta_w', 'delta_w_up': 'delta_w', 'delta_w_down': 'delta_w', 'new_m_g_mix': 'new_m', 'new_m_w_in': 'new_m', 'new_m_b_forget': 'new_m', 'new_m_g_ret_out': 'new_m', 'new_m_g_fox_q': 'new_m', 'new_m_g_fox_k': 'new_m', 'new_m_w_out': 'new_m', 'new_m_g_xattn': 'new_m', 'new_m_w_xq': 'new_m', 'new_m_w_xkv': 'new_m', 'new_m_g_mem': 'new_m', 'new_m_g_xq': 'new_m', 'new_m_g_xk': 'new_m', 'new_m_w_xo': 'new_m', 'new_m_g_ffn': 'new_m', 'new_m_w_gate': 'new_m', 'new_m_w_up': 'new_m', 'new_m_w_down': 'new_m', 'new_v_g_mix': 'new_v', 'new_v_w_in': 'new_v', 'new_v_b_forget': 'new_v', 'new_v_g_ret_out': 'new_v', 'new_v_g_fox_q': 'new_v', 'new_v_g_fox_k': 'new_v', 'new_v_w_out': 'new_v', 'new_v_g_xattn': 'new_v', 'new_v_w_xq': 'new_v', 'new_v_w_xkv': 'new_v', 'new_v_g_mem': 'new_v', 'new_v_g_xq': 'new_v', 'new_v_g_xk': 'new_v', 'new_v_w_xo': 'new_v', 'new_v_g_ffn': 'new_v', 'new_v_w_gate': 'new_v', 'new_v_w_up': 'new_v', 'new_v_w_down': 'new_v'}


def _forward(args):
    return _fwd_reference(*[args[k] for k in FWD_PARAMS])


def _output_shape():
    def fwd():
        inp = _fwd_setup_inputs(0)
        return _fwd_reference(*[inp[k] for k in FWD_PARAMS])
    out = _jax.eval_shape(fwd)
    return out.shape, out.dtype

N_MICROBATCH = 1
ADAM_LR = 0.001
ADAM_B1 = 0.9
ADAM_B2 = 0.999
ADAM_EPS = 1e-08
ADAM_WD = 0.01
ADAM_STEP = 10
PER_EXAMPLE_BATCH_AXIS = {'x': 0, 'mem': 0, 'loss_target': 0}
SHARED_INPUTS = []
_WEIGHT_DTYPES = {'g_mix': _jnp.float32, 'w_in': _jnp.float32, 'b_forget': _jnp.float32, 'g_ret_out': _jnp.float32, 'g_fox_q': _jnp.float32, 'g_fox_k': _jnp.float32, 'w_out': _jnp.float32, 'g_xattn': _jnp.float32, 'w_xq': _jnp.float32, 'w_xkv': _jnp.float32, 'g_mem': _jnp.float32, 'g_xq': _jnp.float32, 'g_xk': _jnp.float32, 'w_xo': _jnp.float32, 'g_ffn': _jnp.float32, 'w_gate': _jnp.float32, 'w_up': _jnp.float32, 'w_down': _jnp.float32}
MOMENT_SCALE = {'g_mix': 9.385557e+00, 'w_in': 2.846208e-01, 'b_forget': 9.747584e+01, 'g_ret_out': 1.119193e+01, 'g_fox_q': 1.364736e+01, 'g_fox_k': 1.365711e+01, 'w_out': 3.086316e-01, 'g_xattn': 5.371601e-02, 'w_xq': 5.244121e-02, 'w_xkv': 9.351224e-02, 'g_mem': 3.637935e-01, 'g_xq': 1.293149e+00, 'g_xk': 1.293219e+00, 'w_xo': 1.156377e-01, 'g_ffn': 2.474170e+01, 'w_gate': 1.750909e-01, 'w_up': 1.886271e-01, 'w_down': 2.959447e-01}


def _to_microbatches(a, axis):
    t = _jnp.moveaxis(a, axis, 0)
    t = t.reshape((N_MICROBATCH, t.shape[0] // N_MICROBATCH) + t.shape[1:])
    return _jnp.moveaxis(t, 1, axis + 1)


def setup_inputs(seed: int = 0) -> dict:
    inp = _fwd_setup_inputs(seed)
    key = _jax.random.fold_in(_jax.random.key(seed), 7919)
    shape, _ = _output_shape()
    out = dict(inp)
    out["loss_target"] = _jax.random.normal(_jax.random.fold_in(key, 0), shape, _jnp.float32)
    for i, name in enumerate(TWIN_WEIGHTS):
        w = inp[name].astype(_jnp.float32)
        if MOMENT_SCALE is None:
            s = _jnp.sqrt(_jnp.mean(_jnp.square(w)) + 1e-30)
        else:
            s = MOMENT_SCALE[name]
        km, kv = _jax.random.split(_jax.random.fold_in(key, i + 1))
        out[name] = w
        out["m_" + name] = s * _jax.random.normal(km, w.shape, _jnp.float32)
        out["v_" + name] = (s * s) * _jax.random.uniform(kv, w.shape, _jnp.float32, 0.5, 1.5)
    if N_MICROBATCH > 1:
        for name, axis in PER_EXAMPLE_BATCH_AXIS.items():
            out[name] = _to_microbatches(out[name], axis)
    return {'x': out['x'], 'mem': out['mem'], 'g_mix': out['g_mix'], 'w_in': out['w_in'], 'b_forget': out['b_forget'], 'g_ret_out': out['g_ret_out'], 'g_fox_q': out['g_fox_q'], 'g_fox_k': out['g_fox_k'], 'w_out': out['w_out'], 'g_xattn': out['g_xattn'], 'w_xq': out['w_xq'], 'w_xkv': out['w_xkv'], 'g_mem': out['g_mem'], 'g_xq': out['g_xq'], 'g_xk': out['g_xk'], 'w_xo': out['w_xo'], 'g_ffn': out['g_ffn'], 'w_gate': out['w_gate'], 'w_up': out['w_up'], 'w_down': out['w_down'], 'loss_target': out['loss_target'], 'm_g_mix': out['m_g_mix'], 'm_w_in': out['m_w_in'], 'm_b_forget': out['m_b_forget'], 'm_g_ret_out': out['m_g_ret_out'], 'm_g_fox_q': out['m_g_fox_q'], 'm_g_fox_k': out['m_g_fox_k'], 'm_w_out': out['m_w_out'], 'm_g_xattn': out['m_g_xattn'], 'm_w_xq': out['m_w_xq'], 'm_w_xkv': out['m_w_xkv'], 'm_g_mem': out['m_g_mem'], 'm_g_xq': out['m_g_xq'], 'm_g_xk': out['m_g_xk'], 'm_w_xo': out['m_w_xo'], 'm_g_ffn': out['m_g_ffn'], 'm_w_gate': out['m_w_gate'], 'm_w_up': out['m_w_up'], 'm_w_down': out['m_w_down'], 'v_g_mix': out['v_g_mix'], 'v_w_in': out['v_w_in'], 'v_b_forget': out['v_b_forget'], 'v_g_ret_out': out['v_g_ret_out'], 'v_g_fox_q': out['v_g_fox_q'], 'v_g_fox_k': out['v_g_fox_k'], 'v_w_out': out['v_w_out'], 'v_g_xattn': out['v_g_xattn'], 'v_w_xq': out['v_w_xq'], 'v_w_xkv': out['v_w_xkv'], 'v_g_mem': out['v_g_mem'], 'v_g_xq': out['v_g_xq'], 'v_g_xk': out['v_g_xk'], 'v_w_xo': out['v_w_xo'], 'v_g_ffn': out['v_g_ffn'], 'v_w_gate': out['v_w_gate'], 'v_w_up': out['v_w_up'], 'v_w_down': out['v_w_down']}


def _loss(weights, diff, rest, loss_target):
    with _jax.named_scope("forward"):
        args = {**rest, TWIN_DIFF_INPUT: diff, **{k: w.astype(_WEIGHT_DTYPES[k]) for k, w in weights.items()}}
        y = _forward(args)
    with _jax.named_scope("loss_head"):
        err = _jnp.square(y.astype(_jnp.float32) - loss_target)
        return 0.5 * _jnp.sum(_jnp.mean(err, axis=-1)) if err.ndim else 0.5 * err


def _adamw(w, g, m, v):
    m = ADAM_B1 * m + (1.0 - ADAM_B1) * g
    v = ADAM_B2 * v + (1.0 - ADAM_B2) * _jnp.square(g)
    m_hat = m / (1.0 - ADAM_B1 ** ADAM_STEP)
    v_hat = v / (1.0 - ADAM_B2 ** ADAM_STEP)
    delta = -ADAM_LR * (m_hat / (_jnp.sqrt(v_hat) + ADAM_EPS) + ADAM_WD * w)
    return delta, m, v


def reference(x, mem, g_mix, w_in, b_forget, g_ret_out, g_fox_q, g_fox_k, w_out, g_xattn, w_xq, w_xkv, g_mem, g_xq, g_xk, w_xo, g_ffn, w_gate, w_up, w_down, loss_target, m_g_mix, m_w_in, m_b_forget, m_g_ret_out, m_g_fox_q, m_g_fox_k, m_w_out, m_g_xattn, m_w_xq, m_w_xkv, m_g_mem, m_g_xq, m_g_xk, m_w_xo, m_g_ffn, m_w_gate, m_w_up, m_w_down, v_g_mix, v_w_in, v_b_forget, v_g_ret_out, v_g_fox_q, v_g_fox_k, v_w_out, v_g_xattn, v_w_xq, v_w_xkv, v_g_mem, v_g_xq, v_g_xk, v_w_xo, v_g_ffn, v_w_gate, v_w_up, v_w_down):
    given = dict(x=x, mem=mem, g_mix=g_mix, w_in=w_in, b_forget=b_forget, g_ret_out=g_ret_out, g_fox_q=g_fox_q, g_fox_k=g_fox_k, w_out=w_out, g_xattn=g_xattn, w_xq=w_xq, w_xkv=w_xkv, g_mem=g_mem, g_xq=g_xq, g_xk=g_xk, w_xo=w_xo, g_ffn=g_ffn, w_gate=w_gate, w_up=w_up, w_down=w_down, loss_target=loss_target, m_g_mix=m_g_mix, m_w_in=m_w_in, m_b_forget=m_b_forget, m_g_ret_out=m_g_ret_out, m_g_fox_q=m_g_fox_q, m_g_fox_k=m_g_fox_k, m_w_out=m_w_out, m_g_xattn=m_g_xattn, m_w_xq=m_w_xq, m_w_xkv=m_w_xkv, m_g_mem=m_g_mem, m_g_xq=m_g_xq, m_g_xk=m_g_xk, m_w_xo=m_w_xo, m_g_ffn=m_g_ffn, m_w_gate=m_w_gate, m_w_up=m_w_up, m_w_down=m_w_down, v_g_mix=v_g_mix, v_w_in=v_w_in, v_b_forget=v_b_forget, v_g_ret_out=v_g_ret_out, v_g_fox_q=v_g_fox_q, v_g_fox_k=v_g_fox_k, v_w_out=v_w_out, v_g_xattn=v_g_xattn, v_w_xq=v_w_xq, v_w_xkv=v_w_xkv, v_g_mem=v_g_mem, v_g_xq=v_g_xq, v_g_xk=v_g_xk, v_w_xo=v_w_xo, v_g_ffn=v_g_ffn, v_w_gate=v_w_gate, v_w_up=v_w_up, v_w_down=v_w_down)
    weights = {n: given[n] for n in TWIN_WEIGHTS}
    shared = {n: given[n] for n in SHARED_INPUTS}
    per_example = {n: given[n] for n in ['x', 'mem']}
    grad_fn = _jax.value_and_grad(_loss, argnums=(0, 1))

    def one_microbatch(ex, loss_target):
        ex = dict(ex)
        diff = ex.pop(TWIN_DIFF_INPUT)
        return grad_fn(weights, diff, {**shared, **ex}, loss_target)

    if N_MICROBATCH == 1:
        loss, (grad_w, grad_x) = one_microbatch(per_example, given["loss_target"])
    else:
        def body(carry, xs):
            loss_sum, grad_sum = carry
            l_k, (gw_k, gx_k) = one_microbatch(xs[0], xs[1])
            with _jax.named_scope("update"):
                return (loss_sum + l_k, _jax.tree.map(_jnp.add, grad_sum, gw_k)), gx_k

        init = (_jnp.zeros((), _jnp.float32), _jax.tree.map(_jnp.zeros_like, weights))
        (loss, grad_w), grad_x = _jax.lax.scan(body, init, (per_example, given["loss_target"]))
    with _jax.named_scope("update"):
        delta_w, new_m, new_v = {}, {}, {}
        for n in TWIN_WEIGHTS:
            delta_w[n], new_m[n], new_v[n] = _adamw(weights[n], grad_w[n], given["m_" + n], given["v_" + n])
    return (loss, grad_x, *[grad_w[n] for n in TWIN_WEIGHTS], *[delta_w[n] for n in TWIN_WEIGHTS],
            *[new_m[n] for n in TWIN_WEIGHTS], *[new_v[n] for n in TWIN_WEIGHTS])
```

```python
import functools

import numpy as np
import jax
import jax.numpy as jnp
from jax import lax
from jax.experimental import pallas as pl
from jax.experimental.pallas import tpu as pltpu

F32 = jnp.float32
BF = jnp.bfloat16

D_MODEL = 1024
HEAD_DIM = 64
N_HEADS = 8
GROUP_W = 512
N_XH = 4
XHD = 256
D_FF = 2816
MAIN_W = 3584
IN_W = 3592
ROPE_BASE = 10000.0
EPS = 1e-6
NEG = -1e30
LANES = 128
RET_BLOCK = 256
REF_CHUNK = 64
ROW_TILE = 256
ATT_BLOCK = 256
SMALL_ROWS = 16
VMEM_LIMIT = 56 * 1024 * 1024

ADAM_LR = 0.001
ADAM_B1 = 0.9
ADAM_B2 = 0.999
ADAM_EPS = 1e-08
ADAM_WD = 0.01
ADAM_STEP = 10

MESH = pl.DeviceIdType.MESH
ANY = pl.BlockSpec(memory_space=pl.ANY)
VMEM_SPEC = pl.BlockSpec(memory_space=pltpu.VMEM)


def _cparams(sem=None, vmem=VMEM_LIMIT):
    return pltpu.CompilerParams(dimension_semantics=sem, vmem_limit_bytes=vmem)


def _dot(a, b):
    return jnp.dot(a.astype(BF), b.astype(BF), preferred_element_type=F32)


def _dot_nt(a, b):
    return lax.dot_general(a.astype(BF), b.astype(BF), (((1,), (1,)), ((), ())), preferred_element_type=F32)


def _dot_tn(a, b):
    return lax.dot_general(a.astype(BF), b.astype(BF), (((0,), (0,)), ((), ())), preferred_element_type=F32)


def _split3(x):
    hi = x.astype(BF)
    r = x - hi.astype(F32)
    mid = r.astype(BF)
    lo = (r - mid.astype(F32)).astype(BF)
    return hi, mid, lo


def _dot_exact(ind, x):
    hi, mid, lo = _split3(x)
    return (jnp.dot(ind, lo, preferred_element_type=F32) + jnp.dot(ind, mid, preferred_element_type=F32)
            + jnp.dot(ind, hi, preferred_element_type=F32))


def _dot_nt_exact(ind, x):
    hi, mid, lo = _split3(x)
    dn = (((1,), (1,)), ((), ()))
    return (lax.dot_general(ind, lo, dn, preferred_element_type=F32) + lax.dot_general(ind, mid, dn, preferred_element_type=F32)
            + lax.dot_general(ind, hi, dn, preferred_element_type=F32))


def _sigmoid(x):
    return 1.0 / (1.0 + jnp.exp(-x))


def _rms_fwd(x, g):
    r = lax.rsqrt(jnp.mean(x * x, axis=-1, keepdims=True) + EPS)
    return x * r * g


def _rms_bwd(x, g, dy):
    r = lax.rsqrt(jnp.mean(x * x, axis=-1, keepdims=True) + EPS)
    xh = x * r
    dg = jnp.sum(dy * xh, axis=0, keepdims=True)
    dxh = dy * g
    dx = r * (dxh - xh * jnp.mean(dxh * xh, axis=-1, keepdims=True))
    return dx, dg


def _group_mean64(x):
    lane = lax.broadcasted_iota(jnp.int32, x.shape, 1)
    lo = lane < HEAD_DIM
    s_lo = jnp.sum(jnp.where(lo, x, 0.0), axis=-1, keepdims=True)
    s_hi = jnp.sum(jnp.where(lo, 0.0, x), axis=-1, keepdims=True)
    return jnp.where(lo, s_lo, s_hi) * (1.0 / HEAD_DIM)


def _swap32(x):
    lane = lax.broadcasted_iota(jnp.int32, x.shape, 1)
    first = (lane % HEAD_DIM) < (HEAD_DIM // 2)
    return jnp.where(first, pltpu.roll(x, LANES - HEAD_DIM // 2, axis=1), pltpu.roll(x, HEAD_DIM // 2, axis=1))


def _chunks(w):
    return [slice(j * LANES, (j + 1) * LANES) for j in range(w // LANES)]


def _mem_kv_fwd(mem, g_mem, w_xkv, g_xk):
    m_tok = mem.shape[0]

    def body(mem_ref, gm_ref, w_ref, gk_ref, memn_ref, kraw_ref, kn_ref, v_ref):
        mn = _rms_fwd(mem_ref[...], gm_ref[...]).astype(BF)
        memn_ref[...] = mn
        kv = jnp.dot(mn, w_ref[...], preferred_element_type=F32)
        k = kv[:, :D_MODEL]
        kraw_ref[...] = k
        v_ref[...] = kv[:, D_MODEL:].astype(BF)
        for h in range(N_XH):
            sl = slice(h * XHD, (h + 1) * XHD)
            kn_ref[:, sl] = _rms_fwd(k[:, sl], gk_ref[...]).astype(BF)

    return pl.pallas_call(
        body, name="mem_kv_fwd",
        out_shape=(jax.ShapeDtypeStruct((m_tok, D_MODEL), BF), jax.ShapeDtypeStruct((m_tok, D_MODEL), F32),
                   jax.ShapeDtypeStruct((m_tok, D_MODEL), BF), jax.ShapeDtypeStruct((m_tok, D_MODEL), BF)),
        in_specs=[VMEM_SPEC] * 4, out_specs=(VMEM_SPEC,) * 4, compiler_params=_cparams(),
    )(mem, g_mem, w_xkv, g_xk)


def _in_proj_fwd(x, g_mix, w_main, w_ff, b_pad, cos_t, sin_t, gq_t, gk_t):
    t_len = x.shape[0]
    tm = min(ROW_TILE, t_len)
    n_t = t_len // tm

    def body(x_ref, g_ref, wm_ref, wf_ref, b_ref, cos_ref, sin_ref, gq_ref, gk_ref,
             n1_ref, proj_ref, rq_ref, rk_ref, fq_ref, fk_ref, z_ref, fc_ref, carry):
        i = pl.program_id(0)

        @pl.when(i == 0)
        def _():
            carry[...] = jnp.zeros_like(carry)

        n1 = _rms_fwd(x_ref[...], g_ref[...]).astype(BF)
        n1_ref[...] = n1
        proj = jnp.dot(n1, wm_ref[...], preferred_element_type=F32)
        proj_ref[...] = proj.astype(BF)
        c, s = cos_ref[...], sin_ref[...]
        for j, sl in enumerate(_chunks(GROUP_W)):
            q = proj[:, sl]
            rq_ref[:, sl] = ((q * c + _swap32(q) * s) * 0.125).astype(BF)
            k = proj[:, GROUP_W + j * LANES:GROUP_W + (j + 1) * LANES]
            rk_ref[:, sl] = (k * c + _swap32(k) * s).astype(BF)
            fq = proj[:, 4 * GROUP_W + j * LANES:4 * GROUP_W + (j + 1) * LANES]
            fq_ref[:, sl] = (fq * lax.rsqrt(_group_mean64(fq * fq) + EPS) * gq_ref[...] * 0.125).astype(BF)
            fk = proj[:, 5 * GROUP_W + j * LANES:5 * GROUP_W + (j + 1) * LANES]
            fk_ref[:, sl] = (fk * lax.rsqrt(_group_mean64(fk * fk) + EPS) * gk_ref[...]).astype(BF)
        z = jnp.dot(n1, wf_ref[...], preferred_element_type=F32) + b_ref[...]
        z_ref[...] = z
        lane = lax.broadcasted_iota(jnp.int32, z.shape, 1)
        lf = jnp.where(lane < N_HEADS, jnp.minimum(z, 0.0) - jnp.log(1.0 + jnp.exp(-jnp.abs(z))), 0.0)
        row = lax.broadcasted_iota(jnp.int32, (tm, tm), 0)
        col = lax.broadcasted_iota(jnp.int32, (tm, tm), 1)
        tri = (row >= col).astype(BF)
        fc = _dot_exact(tri, lf) + carry[0:1, :]
        fc_ref[...] = fc
        carry[...] = jnp.broadcast_to(fc[tm - 1:tm, :], carry.shape)

    row_spec = lambda w: pl.BlockSpec((tm, w), lambda i: (i, 0))
    full = lambda a: pl.BlockSpec(a.shape, lambda i: (0,) * a.ndim)
    return pl.pallas_call(
        body, name="in_proj_fwd", grid=(n_t,),
        out_shape=(jax.ShapeDtypeStruct((t_len, D_MODEL), BF), jax.ShapeDtypeStruct((t_len, MAIN_W), BF),
                   jax.ShapeDtypeStruct((t_len, GROUP_W), BF), jax.ShapeDtypeStruct((t_len, GROUP_W), BF),
                   jax.ShapeDtypeStruct((t_len, GROUP_W), BF), jax.ShapeDtypeStruct((t_len, GROUP_W), BF),
                   jax.ShapeDtypeStruct((t_len, LANES), F32), jax.ShapeDtypeStruct((t_len, LANES), F32)),
        in_specs=[row_spec(D_MODEL), full(g_mix), full(w_main), full(w_ff), full(b_pad), row_spec(LANES), row_spec(LANES),
                  full(gq_t), full(gk_t)],
        out_specs=(row_spec(D_MODEL), row_spec(MAIN_W), row_spec(GROUP_W), row_spec(GROUP_W), row_spec(GROUP_W),
                   row_spec(GROUP_W), row_spec(LANES), row_spec(LANES)),
        scratch_shapes=[pltpu.VMEM((8, LANES), F32)],
        compiler_params=_cparams(("arbitrary",)),
    )(x, g_mix, w_main, w_ff, b_pad, cos_t, sin_t, gq_t, gk_t)


def _decay_tables(c):
    h = np.arange(N_HEADS, dtype=np.float64)
    lg = np.log(1.0 - 2.0 ** (-5.0 - h)).astype(np.float32).astype(np.float64)
    t = np.arange(c)
    same_or_earlier = (t[None, :] // REF_CHUNK) <= (t[:, None] // REF_CHUNK)
    w = np.where(same_or_earlier[None], np.exp(lg[:, None, None] * np.abs(t[:, None] - t[None, :])[None]), 0.0)
    qd = np.exp(lg[:, None] * (t[None, :] + 1.0))
    kd = np.exp(lg[:, None] * (c - 1.0 - t[None, :]))
    cd = np.exp(lg * c)
    ones = np.ones((1, 1, HEAD_DIM))
    return (jnp.asarray(w, F32), jnp.asarray(qd[:, :, None] * ones, F32), jnp.asarray(kd[:, :, None] * ones, F32),
            jnp.asarray(cd[:, None, None] * np.ones((1, HEAD_DIM, HEAD_DIM)), F32))


def _retention_fwd(rq, rk, proj, g_ret, tables):
    t_len = rq.shape[0]
    c = min(RET_BLOCK, t_len)
    n_b = t_len // c
    wdec, qdec, kdec, cdec = tables
    v_col, g_col = 2 * GROUP_W // LANES, 3 * GROUP_W // LANES

    def body(q_ref, k_ref, v_ref, rg_ref, g_ref, w_ref, qd_ref, kd_ref, cd_ref, raw_ref, mix_ref, st_ref, state):
        i = pl.program_id(1)

        @pl.when(i == 0)
        def _():
            state[...] = jnp.zeros_like(state)

        q2, k2, v2 = q_ref[...], k_ref[...], v_ref[...]
        outs = []
        for hh in range(2):
            sl = slice(hh * HEAD_DIM, (hh + 1) * HEAD_DIM)
            q, k, v = q2[:, sl], k2[:, sl], v2[:, sl]
            sp = state[hh]
            st_ref[0, 0, hh] = sp
            a = _dot_nt(q, k) * w_ref[hh]
            o = _dot(a, v) + _dot(q.astype(F32) * qd_ref[hh], sp)
            state[hh] = sp * cd_ref[hh] + _dot_tn(k.astype(F32) * kd_ref[hh], v)
            outs.append(o)
        o2 = jnp.concatenate(outs, axis=-1)
        raw_ref[...] = o2
        xc = o2 - _group_mean64(o2)
        xh = xc * lax.rsqrt(_group_mean64(xc * xc) + EPS)
        gate = rg_ref[...].astype(F32)
        mix_ref[...] = (gate * _sigmoid(gate) * (xh * g_ref[0])).astype(BF)

    blk = lambda col0: pl.BlockSpec((c, LANES), lambda hp, i: (i, col0 + hp))
    tab = lambda a: pl.BlockSpec((2,) + a.shape[1:], lambda hp, i: (hp, 0, 0))
    return pl.pallas_call(
        body, name="retention_fwd", grid=(N_HEADS // 2, n_b),
        out_shape=(jax.ShapeDtypeStruct((t_len, GROUP_W), F32), jax.ShapeDtypeStruct((t_len, GROUP_W), BF),
                   jax.ShapeDtypeStruct((N_HEADS // 2, n_b, 2, HEAD_DIM, HEAD_DIM), F32)),
        in_specs=[blk(0), blk(0), blk(v_col), blk(g_col), pl.BlockSpec((1, 1, LANES), lambda hp, i: (hp, 0, 0)),
                  tab(wdec), tab(qdec), tab(kdec), tab(cdec)],
        out_specs=(blk(0), blk(0), pl.BlockSpec((1, 1, 2, HEAD_DIM, HEAD_DIM), lambda hp, i: (hp, i, 0, 0, 0))),
        scratch_shapes=[pltpu.VMEM((2, HEAD_DIM, HEAD_DIM), F32)],
        compiler_params=_cparams(("arbitrary", "arbitrary")),
    )(rq, rk, proj, proj, g_ret, wdec, qdec, kdec, cdec)


def _fox_fwd(fq, fk, proj, f_col, f_row):
    t_len = fq.shape[0]
    tq = min(ATT_BLOCK, t_len)
    n_q = t_len // tq
    v_col = 6 * GROUP_W // LANES

    def body(q_ref, k_ref, v_ref, fq_ref, fk_ref, o_ref, o32_ref, lse_ref):
        i = pl.program_id(1)
        q2 = q_ref[...]
        fq2 = fq_ref[0]
        row = i * tq + lax.broadcasted_iota(jnp.int32, (tq, tq), 0)
        col0 = lax.broadcasted_iota(jnp.int32, (tq, tq), 1)
        outs, lses = [], []
        for hh in range(2):
            sl = slice(hh * HEAD_DIM, (hh + 1) * HEAD_DIM)
            q = q2[:, sl]
            f_q = fq2[:, hh:hh + 1]

            def step(j, carry, sl=sl, q=q, f_q=f_q, hh=hh):
                m, l, acc = carry
                rows = pl.ds(pl.multiple_of(j * tq, tq), tq)
                k = k_ref[rows, :][:, sl]
                v = v_ref[rows, :][:, sl]
                s = _dot_nt(q, k) + f_q - fk_ref[0, hh:hh + 1, rows]
                s = jnp.where(row >= j * tq + col0, s, NEG)
                m_new = jnp.maximum(m, jnp.max(s, axis=-1, keepdims=True))
                alpha = jnp.exp(m - m_new)
                p = jnp.exp(s - m_new)
                l = l * alpha + jnp.sum(p, axis=-1, keepdims=True)
                acc = acc * alpha + _dot(p, v)
                return m_new, l, acc

            m, l, acc = lax.fori_loop(0, i + 1, step, (jnp.full((tq, 1), NEG, F32), jnp.zeros((tq, 1), F32),
                                                       jnp.zeros((tq, HEAD_DIM), F32)))
            outs.append(acc / l)
            lses.append(m + jnp.log(l))
        o2 = jnp.concatenate(outs, axis=-1)
        o32_ref[...] = o2
        o_ref[...] = o2.astype(BF)
        lse_ref[0] = jnp.concatenate(lses, axis=-1)

    return pl.pallas_call(
        body, name="fox_fwd", grid=(N_HEADS // 2, n_q),
        out_shape=(jax.ShapeDtypeStruct((t_len, GROUP_W), BF), jax.ShapeDtypeStruct((t_len, GROUP_W), F32),
                   jax.ShapeDtypeStruct((N_HEADS // 2, t_len, 2), F32)),
        in_specs=[pl.BlockSpec((tq, LANES), lambda hp, i: (i, hp)),
                  pl.BlockSpec((t_len, LANES), lambda hp, i: (0, hp)),
                  pl.BlockSpec((t_len, LANES), lambda hp, i: (0, v_col + hp)),
                  pl.BlockSpec((1, tq, 2), lambda hp, i: (hp, i, 0)),
                  pl.BlockSpec((1, 2, t_len), lambda hp, i: (hp, 0, 0))],
        out_specs=(pl.BlockSpec((tq, LANES), lambda hp, i: (i, hp)), pl.BlockSpec((tq, LANES), lambda hp, i: (i, hp)),
                   pl.BlockSpec((1, tq, 2), lambda hp, i: (hp, i, 0))),
        compiler_params=_cparams(("arbitrary", "arbitrary")),
    )(fq, fk, proj, f_col, f_row)


def _softmax_rows(s):
    p = jnp.exp(s - jnp.max(s, axis=-1, keepdims=True))
    return p / jnp.sum(p, axis=-1, keepdims=True)


def _attn_out_xattn_fwd(x, mix_r, mix_f, w_out, g_xattn, w_xq, g_xq, kn, v, w_xo):
    t_len = x.shape[0]
    tm = min(ROW_TILE, t_len)

    def body(x_ref, mr_ref, mf_ref, wo_ref, g_ref, wq_ref, gq_ref, kn_ref, v_ref, wxo_ref,
             h1_ref, hn_ref, qx_ref, o_ref, h2_ref):
        h1 = x_ref[...] + jnp.dot(mr_ref[...], wo_ref[:GROUP_W, :], preferred_element_type=F32) \
            + jnp.dot(mf_ref[...], wo_ref[GROUP_W:, :], preferred_element_type=F32)
        h1_ref[...] = h1
        hn = _rms_fwd(h1, g_ref[...]).astype(BF)
        hn_ref[...] = hn
        qx = jnp.dot(hn, wq_ref[...], preferred_element_type=F32).astype(BF)
        qx_ref[...] = qx
        for h in range(N_XH):
            sl = slice(h * XHD, (h + 1) * XHD)
            qn = _rms_fwd(qx[:, sl].astype(F32), gq_ref[...])
            p = _softmax_rows(_dot_nt(qn, kn_ref[:, sl]) * (XHD ** -0.5))
            o_ref[:, sl] = _dot(p, v_ref[:, sl]).astype(BF)
        h2_ref[...] = h1 + jnp.dot(o_ref[...], wxo_ref[...], preferred_element_type=F32)

    row_spec = lambda w: pl.BlockSpec((tm, w), lambda i: (i, 0))
    full = lambda a: pl.BlockSpec(a.shape, lambda i: (0,) * a.ndim)
    return pl.pallas_call(
        body, name="attn_out_xattn_fwd", grid=(t_len // tm,),
        out_shape=(jax.ShapeDtypeStruct((t_len, D_MODEL), F32), jax.ShapeDtypeStruct((t_len, D_MODEL), BF),
                   jax.ShapeDtypeStruct((t_len, D_MODEL), BF), jax.ShapeDtypeStruct((t_len, D_MODEL), BF),
                   jax.ShapeDtypeStruct((t_len, D_MODEL), F32)),
        in_specs=[row_spec(D_MODEL), row_spec(GROUP_W), row_spec(GROUP_W), full(w_out), full(g_xattn), full(w_xq), full(g_xq),
                  full(kn), full(v), full(w_xo)],
        out_specs=(row_spec(D_MODEL),) * 5,
        compiler_params=_cparams(("arbitrary",)),
    )(x, mix_r, mix_f, w_out, g_xattn, w_xq, g_xq, kn, v, w_xo)


def _ffn_loss_fwd(h2, g_ffn, w_gate, w_up, w_down, target):
    t_len = h2.shape[0]
    tm = min(ROW_TILE, t_len)

    def body(h2_ref, g_ref, wg_ref, wu_ref, wd_ref, tgt_ref, hn_ref, gate_ref, up_ref, act_ref, dh3_ref, loss_ref):
        @pl.when(pl.program_id(0) == 0)
        def _():
            loss_ref[...] = jnp.zeros_like(loss_ref)

        h2v = h2_ref[...]
        hn = _rms_fwd(h2v, g_ref[...]).astype(BF)
        hn_ref[...] = hn
        gate = jnp.dot(hn, wg_ref[...], preferred_element_type=F32)
        up = jnp.dot(hn, wu_ref[...], preferred_element_type=F32)
        gate_ref[...] = gate.astype(BF)
        up_ref[...] = up.astype(BF)
        act = (gate * _sigmoid(gate) * up).astype(BF)
        act_ref[...] = act
        diff = h2v + jnp.dot(act, wd_ref[...], preferred_element_type=F32) - tgt_ref[...]
        dh3_ref[...] = diff * (1.0 / D_MODEL)
        per_row = jnp.sum(diff * diff, axis=-1, keepdims=True) * (1.0 / D_MODEL)
        loss_ref[...] += 0.5 * jnp.sum(per_row, axis=0, keepdims=True)

    row_spec = lambda w: pl.BlockSpec((tm, w), lambda i: (i, 0))
    full = lambda a: pl.BlockSpec(a.shape, lambda i: (0,) * a.ndim, pipeline_mode=pl.Buffered(1))
    return pl.pallas_call(
        body, name="ffn_loss_fwd", grid=(t_len // tm,),
        out_shape=(jax.ShapeDtypeStruct((t_len, D_MODEL), BF), jax.ShapeDtypeStruct((t_len, D_FF), BF),
                   jax.ShapeDtypeStruct((t_len, D_FF), BF), jax.ShapeDtypeStruct((t_len, D_FF), BF),
                   jax.ShapeDtypeStruct((t_len, D_MODEL), F32), jax.ShapeDtypeStruct((8, LANES), F32)),
        in_specs=[row_spec(D_MODEL), full(g_ffn), full(w_gate), full(w_up), full(w_down), row_spec(D_MODEL)],
        out_specs=(row_spec(D_MODEL), row_spec(D_FF), row_spec(D_FF), row_spec(D_FF), row_spec(D_MODEL),
                   pl.BlockSpec((8, LANES), lambda i: (0, 0))),
        compiler_params=_cparams(("arbitrary",)),
    )(h2, g_ffn, w_gate, w_up, w_down, target)


def _ffn_bwd(dh3, gate, up, h2, g_ffn, w_gate, w_up, w_down):
    t_len = h2.shape[0]
    tm = min(ROW_TILE, t_len)

    def body(dh3_ref, gate_ref, up_ref, h2_ref, g_ref, wg_ref, wu_ref, wd_ref, dgate_ref, dup_ref, dh2_ref, dg_ref):
        @pl.when(pl.program_id(0) == 0)
        def _():
            dg_ref[...] = jnp.zeros_like(dg_ref)

        dh3v = dh3_ref[...]
        dact = _dot_nt(dh3v, wd_ref[...])
        g = gate_ref[...].astype(F32)
        sg = _sigmoid(g)
        dup = (dact * (g * sg)).astype(BF)
        dgate = (dact * up_ref[...].astype(F32) * (sg * (1.0 + g * (1.0 - sg)))).astype(BF)
        dup_ref[...] = dup
        dgate_ref[...] = dgate
        dhn = _dot_nt(dgate, wg_ref[...]) + _dot_nt(dup, wu_ref[...])
        dx, dg = _rms_bwd(h2_ref[...], g_ref[...], dhn)
        dh2_ref[...] = dh3v + dx
        dg_ref[...] += dg

    row_spec = lambda w: pl.BlockSpec((tm, w), lambda i: (i, 0))
    full = lambda a: pl.BlockSpec(a.shape, lambda i: (0,) * a.ndim, pipeline_mode=pl.Buffered(1))
    return pl.pallas_call(
        body, name="ffn_bwd", grid=(t_len // tm,),
        out_shape=(jax.ShapeDtypeStruct((t_len, D_FF), BF), jax.ShapeDtypeStruct((t_len, D_FF), BF),
                   jax.ShapeDtypeStruct((t_len, D_MODEL), F32), jax.ShapeDtypeStruct((1, D_MODEL), F32)),
        in_specs=[row_spec(D_MODEL), row_spec(D_FF), row_spec(D_FF), row_spec(D_MODEL), full(g_ffn), full(w_gate), full(w_up),
                  full(w_down)],
        out_specs=(row_spec(D_FF), row_spec(D_FF), row_spec(D_MODEL), pl.BlockSpec((1, D_MODEL), lambda i: (0, 0))),
        compiler_params=_cparams(("arbitrary",)),
    )(dh3, gate, up, h2, g_ffn, w_gate, w_up, w_down)


def _attn_out_xattn_bwd(dh2, h1, qx, kn, v, w_xo, w_xq, w_out, g_xattn, g_xq):
    t_len = h1.shape[0]
    tm = min(ROW_TILE, t_len)
    m_tok = kn.shape[0]

    def body(dh2_ref, h1_ref, qx_ref, kn_ref, v_ref, wxo_ref, wq_ref, wo_ref, g_ref, gq_ref,
             dqx_ref, dh1_ref, dmr_ref, dmf_ref, dkn_ref, dv_ref, dg_ref, dgq_ref, dqx_scr):
        @pl.when(pl.program_id(0) == 0)
        def _():
            dkn_ref[...] = jnp.zeros_like(dkn_ref)
            dv_ref[...] = jnp.zeros_like(dv_ref)
            dg_ref[...] = jnp.zeros_like(dg_ref)
            dgq_ref[...] = jnp.zeros_like(dgq_ref)

        dh2v = dh2_ref[...]
        do = _dot_nt(dh2v, wxo_ref[...])
        gq = gq_ref[...]
        dgq = jnp.zeros((1, XHD), F32)
        for h in range(N_XH):
            sl = slice(h * XHD, (h + 1) * XHD)
            qraw = qx_ref[:, sl].astype(F32)
            qn = _rms_fwd(qraw, gq)
            p = _softmax_rows(_dot_nt(qn, kn_ref[:, sl]) * (XHD ** -0.5))
            doh = do[:, sl]
            dv_ref[:, sl] += _dot_tn(p, doh)
            dp = _dot_nt(doh, v_ref[:, sl])
            ds = p * (dp - jnp.sum(dp * p, axis=-1, keepdims=True)) * (XHD ** -0.5)
            dqn = _dot(ds, kn_ref[:, sl])
            dkn_ref[:, sl] += _dot_tn(ds, qn)
            dx, dg_h = _rms_bwd(qraw, gq, dqn)
            dgq = dgq + dg_h
            dqx_scr[:, sl] = dx.astype(BF)
        dgq_ref[...] += dgq
        dqx = dqx_scr[...]
        dqx_ref[...] = dqx
        dhn = _dot_nt(dqx, wq_ref[...])
        dx, dg = _rms_bwd(h1_ref[...], g_ref[...], dhn)
        dg_ref[...] += dg
        dh1 = dh2v + dx
        dh1_ref[...] = dh1
        dmix = _dot_nt(dh1, wo_ref[...])
        dmr_ref[...] = dmix[:, :GROUP_W]
        dmf_ref[...] = dmix[:, GROUP_W:].astype(BF)

    row_spec = lambda w: pl.BlockSpec((tm, w), lambda i: (i, 0))
    full = lambda a: pl.BlockSpec(a.shape, lambda i: (0,) * a.ndim)
    acc = lambda r, c: pl.BlockSpec((r, c), lambda i: (0, 0))
    return pl.pallas_call(
        body, name="attn_out_xattn_bwd", grid=(t_len // tm,),
        out_shape=(jax.ShapeDtypeStruct((t_len, D_MODEL), BF), jax.ShapeDtypeStruct((t_len, D_MODEL), F32),
                   jax.ShapeDtypeStruct((t_len, GROUP_W), F32), jax.ShapeDtypeStruct((t_len, GROUP_W), BF),
                   jax.ShapeDtypeStruct((m_tok, D_MODEL), F32), jax.ShapeDtypeStruct((m_tok, D_MODEL), F32),
                   jax.ShapeDtypeStruct((1, D_MODEL), F32), jax.ShapeDtypeStruct((1, XHD), F32)),
        in_specs=[row_spec(D_MODEL), row_spec(D_MODEL), row_spec(D_MODEL), full(kn), full(v), full(w_xo), full(w_xq), full(w_out),
                  full(g_xattn), full(g_xq)],
        out_specs=(row_spec(D_MODEL), row_spec(D_MODEL), row_spec(GROUP_W), row_spec(GROUP_W), acc(m_tok, D_MODEL),
                   acc(m_tok, D_MODEL), acc(1, D_MODEL), acc(1, XHD)),
        scratch_shapes=[pltpu.VMEM((tm, D_MODEL), BF)],
        compiler_params=_cparams(("arbitrary",)),
    )(dh2, h1, qx, kn, v, w_xo, w_xq, w_out, g_xattn, g_xq)


def _mem_kv_bwd(dkn, dv, kraw, mem, memn, g_mem, g_xk, w_xkv):
    m_tok = mem.shape[0]

    def body(dkn_ref, dv_ref, kraw_ref, mem_ref, memn_ref, gm_ref, gk_ref, w_ref, dw_ref, dgm_ref, dgk_ref, dkv_scr):
        gk = gk_ref[...]
        dgk = jnp.zeros((1, XHD), F32)
        for h in range(N_XH):
            sl = slice(h * XHD, (h + 1) * XHD)
            dx, dg_h = _rms_bwd(kraw_ref[:, sl], gk, dkn_ref[:, sl])
            dgk = dgk + dg_h
            dkv_scr[:, sl] = dx.astype(BF)
        dgk_ref[...] = dgk
        dkv_scr[:, D_MODEL:] = dv_ref[...].astype(BF)
        dkv = dkv_scr[...]
        dw_ref[...] = _dot_tn(memn_ref[...], dkv)
        dmemn = _dot_nt(dkv, w_ref[...])
        mem_v = mem_ref[...]
        r = lax.rsqrt(jnp.mean(mem_v * mem_v, axis=-1, keepdims=True) + EPS)
        dgm_ref[...] = jnp.sum(dmemn * mem_v * r, axis=0, keepdims=True)

    return pl.pallas_call(
        body, name="mem_kv_bwd",
        out_shape=(jax.ShapeDtypeStruct((D_MODEL, 2 * D_MODEL), F32), jax.ShapeDtypeStruct((1, D_MODEL), F32),
                   jax.ShapeDtypeStruct((1, XHD), F32)),
        in_specs=[VMEM_SPEC] * 8, out_specs=(VMEM_SPEC,) * 3,
        scratch_shapes=[pltpu.VMEM((m_tok, 2 * D_MODEL), BF)],
        compiler_params=_cparams(),
    )(dkn, dv, kraw, mem, memn, g_mem, g_xk, w_xkv)


def _fox_bwd(fq, fk, proj, dmf, o32, lse_row, f_row, f_col):
    t_len = fq.shape[0]
    tb = min(ATT_BLOCK, t_len)
    n_b = t_len // tb
    v_col = 6 * GROUP_W // LANES

    def body(k_ref, v_ref, q_ref, do_ref, o_ref, lse_ref, fr_ref, fc_ref, dq_ref, dk_ref, dv_ref, df_ref, delta):
        j = pl.program_id(1)

        @pl.when(j == 0)
        def _():
            dq_ref[...] = jnp.zeros_like(dq_ref)
            dd = do_ref[...].astype(F32) * o_ref[...]
            hrow = lax.broadcasted_iota(jnp.int32, (8, LANES), 0)
            lane = lax.broadcasted_iota(jnp.int32, (8, LANES), 1)
            ind = ((lane // HEAD_DIM) == hrow).astype(BF)
            delta[...] = _dot_nt_exact(ind, dd)

        k2, v2 = k_ref[...], v_ref[...]
        krow = j * tb + lax.broadcasted_iota(jnp.int32, (tb, tb), 0)
        qcol0 = lax.broadcasted_iota(jnp.int32, (tb, tb), 1)
        dks, dvs, dfs = [], [], []
        for hh in range(2):
            sl = slice(hh * HEAD_DIM, (hh + 1) * HEAD_DIM)
            k, v = k2[:, sl], v2[:, sl]
            f_k = fc_ref[0][:, hh:hh + 1]

            def step(i, carry, sl=sl, k=k, v=v, f_k=f_k, hh=hh):
                dk, dv, df = carry
                rows = pl.ds(pl.multiple_of(i * tb, tb), tb)
                q = q_ref[rows, :][:, sl]
                do = do_ref[rows, :][:, sl]
                s_t = _dot_nt(k, q) + fr_ref[0, hh:hh + 1, rows] - f_k
                s_t = jnp.where(i * tb + qcol0 >= krow, s_t, NEG)
                p_t = jnp.exp(s_t - lse_ref[0, hh:hh + 1, rows])
                dv = dv + _dot(p_t, do)
                ds_t = p_t * (_dot_nt(v, do) - delta[hh:hh + 1, rows])
                dk = dk + _dot(ds_t, q)
                df = df - jnp.sum(ds_t, axis=-1, keepdims=True)
                dq_ref[rows, sl] += _dot_tn(ds_t, k)
                return dk, dv, df

            dk, dv, df = lax.fori_loop(j, n_b, step, (jnp.zeros((tb, HEAD_DIM), F32), jnp.zeros((tb, HEAD_DIM), F32),
                                                      jnp.zeros((tb, 1), F32)))
            dks.append(dk)
            dvs.append(dv)
            dfs.append(df)
        dk_ref[...] = jnp.concatenate(dks, axis=-1)
        dv_ref[...] = jnp.concatenate(dvs, axis=-1)
        df_ref[0] = jnp.concatenate(dfs, axis=-1)

    blk = lambda col0: pl.BlockSpec((tb, LANES), lambda hp, j: (j, col0 + hp))
    whole = pl.BlockSpec((t_len, LANES), lambda hp, j: (0, hp))
    rows2 = pl.BlockSpec((1, 2, t_len), lambda hp, j: (hp, 0, 0))
    cols2 = pl.BlockSpec((1, tb, 2), lambda hp, j: (hp, j, 0))
    return pl.pallas_call(
        body, name="fox_bwd", grid=(N_HEADS // 2, n_b),
        out_shape=(jax.ShapeDtypeStruct((t_len, GROUP_W), F32), jax.ShapeDtypeStruct((t_len, GROUP_W), F32),
                   jax.ShapeDtypeStruct((t_len, GROUP_W), F32), jax.ShapeDtypeStruct((N_HEADS // 2, t_len, 2), F32)),
        in_specs=[blk(0), blk(v_col), whole, whole, whole, rows2, rows2, cols2],
        out_specs=(whole, blk(0), blk(0), cols2),
        scratch_shapes=[pltpu.VMEM((8, t_len), F32)],
        compiler_params=_cparams(("arbitrary", "arbitrary")),
    )(fk, proj, fq, dmf, o32, lse_row, f_row, f_col)


def _retention_bwd(dmr, raw, proj, g_ret, rq, rk, states, tables):
    t_len = rq.shape[0]
    c = min(RET_BLOCK, t_len)
    n_b = t_len // c
    wdec, qdec, kdec, cdec = tables
    v_col, g_col = 2 * GROUP_W // LANES, 3 * GROUP_W // LANES

    def body(d_ref, raw_ref, rg_ref, g_ref, q_ref, k_ref, v_ref, st_ref, w_ref, qd_ref, kd_ref, cd_ref,
             dq_ref, dk_ref, dv_ref, drg_ref, dg_ref, gstate):
        @pl.when(pl.program_id(1) == 0)
        def _():
            gstate[...] = jnp.zeros_like(gstate)
            dg_ref[...] = jnp.zeros_like(dg_ref)

        d, raw_v, g = d_ref[...], raw_ref[...], g_ref[0]
        gate = rg_ref[...].astype(F32)
        xc = raw_v - _group_mean64(raw_v)
        r = lax.rsqrt(_group_mean64(xc * xc) + EPS)
        xh = xc * r
        sg = _sigmoid(gate)
        drg_ref[...] = d * (xh * g) * (sg * (1.0 + gate * (1.0 - sg)))
        dy = d * (gate * sg)
        dg_ref[0] += jnp.sum(dy * xh, axis=0, keepdims=True)
        dxh = dy * g
        do2 = r * (dxh - _group_mean64(dxh) - xh * _group_mean64(dxh * xh))
        q2, k2, v2 = q_ref[...], k_ref[...], v_ref[...]
        dqs, dks, dvs = [], [], []
        for hh in range(2):
            sl = slice(hh * HEAD_DIM, (hh + 1) * HEAD_DIM)
            q, k, v, do = q2[:, sl], k2[:, sl], v2[:, sl], do2[:, sl].astype(BF)
            w = w_ref[hh]
            a = _dot_nt(q, k) * w
            dm = _dot_nt(do, v) * w
            sp, gs = st_ref[0, 0, hh], gstate[hh]
            qd = q.astype(F32) * qd_ref[hh]
            kd = k.astype(F32) * kd_ref[hh]
            dqs.append(_dot(dm, k) + _dot_nt(do, sp) * qd_ref[hh])
            dks.append(_dot_tn(dm, q) + _dot_nt(v, gs) * kd_ref[hh])
            dvs.append(_dot_tn(a, do) + _dot(kd, gs))
            gstate[hh] = gs * cd_ref[hh] + _dot_tn(qd, do)
        dq_ref[...] = jnp.concatenate(dqs, axis=-1)
        dk_ref[...] = jnp.concatenate(dks, axis=-1)
        dv_ref[...] = jnp.concatenate(dvs, axis=-1)

    blk = lambda col0: pl.BlockSpec((c, LANES), lambda hp, i: (n_b - 1 - i, col0 + hp))
    tab = lambda a: pl.BlockSpec((2,) + a.shape[1:], lambda hp, i: (hp, 0, 0))
    gspec = pl.BlockSpec((1, 1, LANES), lambda hp, i: (hp, 0, 0))
    return pl.pallas_call(
        body, name="retention_bwd", grid=(N_HEADS // 2, n_b),
        out_shape=(jax.ShapeDtypeStruct((t_len, GROUP_W), F32),) * 4 + (jax.ShapeDtypeStruct((N_HEADS // 2, 1, LANES), F32),),
        in_specs=[blk(0), blk(0), blk(g_col), gspec, blk(0), blk(0), blk(v_col),
                  pl.BlockSpec((1, 1, 2, HEAD_DIM, HEAD_DIM), lambda hp, i: (hp, n_b - 1 - i, 0, 0, 0)),
                  tab(wdec), tab(qdec), tab(kdec), tab(cdec)],
        out_specs=(blk(0), blk(0), blk(0), blk(0), gspec),
        scratch_shapes=[pltpu.VMEM((2, HEAD_DIM, HEAD_DIM), F32)],
        compiler_params=_cparams(("arbitrary", "arbitrary")),
    )(dmr, raw, proj, g_ret, rq, rk, proj, states, wdec, qdec, kdec, cdec)


def _in_proj_bwd(x, g_mix, dh1, dq_r, dk_r, dv_r, drg, dq_f, dk_f, dv_f, df_col, proj, z, cos_t, sin_t, gq_t, gk_t, w_main, w_ff):
    t_len = x.shape[0]
    tm = min(ROW_TILE, t_len)
    n_t = t_len // tm

    def body(x_ref, g_ref, dh1_ref, dqr_ref, dkr_ref, dvr_ref, drg_ref, dqf_ref, dkf_ref, dvf_ref, df_ref, fq_ref, fk_ref, z_ref,
             cos_ref, sin_ref, gq_ref, gk_ref, wm_ref, wf_ref,
             dproj_ref, dz_ref, dx_ref, dg_ref, dgq_ref, dgk_ref, db_ref, carry, gq_acc, gk_acc):
        i = pl.program_id(0)

        @pl.when(i == 0)
        def _():
            carry[...] = jnp.zeros_like(carry)
            gq_acc[...] = jnp.zeros_like(gq_acc)
            gk_acc[...] = jnp.zeros_like(gk_acc)
            dg_ref[...] = jnp.zeros_like(dg_ref)
            db_ref[...] = jnp.zeros_like(db_ref)

        c, s = cos_ref[...], sin_ref[...]
        gq, gk = gq_ref[...], gk_ref[...]
        dgq = jnp.zeros((1, LANES), F32)
        dgk = jnp.zeros((1, LANES), F32)
        for sl in _chunks(GROUP_W):
            dy = dqr_ref[:, sl] * 0.125
            dproj_ref[:, sl] = (dy * c + _swap32(dy * s)).astype(BF)
            dy = dkr_ref[:, sl]
            dproj_ref[:, GROUP_W + sl.start:GROUP_W + sl.stop] = (dy * c + _swap32(dy * s)).astype(BF)
            dproj_ref[:, 2 * GROUP_W + sl.start:2 * GROUP_W + sl.stop] = dvr_ref[:, sl].astype(BF)
            dproj_ref[:, 3 * GROUP_W + sl.start:3 * GROUP_W + sl.stop] = drg_ref[:, sl].astype(BF)
            for src, dsrc, gain, off in ((fq_ref, dqf_ref, gq, 4), (fk_ref, dkf_ref, gk, 5)):
                xr = src[:, sl].astype(F32)
                r = lax.rsqrt(_group_mean64(xr * xr) + EPS)
                xh = xr * r
                dy = dsrc[:, sl] * (0.125 if off == 4 else 1.0)
                dgs = jnp.sum(dy * xh, axis=0, keepdims=True)
                if off == 4:
                    dgq = dgq + dgs
                else:
                    dgk = dgk + dgs
                dxh = dy * gain
                dproj_ref[:, off * GROUP_W + sl.start:off * GROUP_W + sl.stop] = \
                    (r * (dxh - xh * _group_mean64(dxh * xh))).astype(BF)
            dproj_ref[:, 6 * GROUP_W + sl.start:6 * GROUP_W + sl.stop] = dvf_ref[:, sl].astype(BF)
        gq_acc[...] += dgq
        gk_acc[...] += dgk
        row = lax.broadcasted_iota(jnp.int32, (tm, tm), 0)
        col = lax.broadcasted_iota(jnp.int32, (tm, tm), 1)
        dlf = _dot_exact((col >= row).astype(BF), df_ref[...]) + carry[0:1, :]
        carry[...] = jnp.broadcast_to(dlf[0:1, :], carry.shape)
        lane = lax.broadcasted_iota(jnp.int32, (tm, LANES), 1)
        dz = jnp.where(lane < N_HEADS, dlf / (1.0 + jnp.exp(z_ref[...])), 0.0)
        db_ref[...] += jnp.sum(dz, axis=0, keepdims=True)
        dz_bf = dz.astype(BF)
        dz_ref[...] = dz_bf
        dn1 = _dot_nt(dz_bf, wf_ref[...])
        for sec in range(MAIN_W // GROUP_W):
            sl = slice(sec * GROUP_W, (sec + 1) * GROUP_W)
            dn1 = dn1 + _dot_nt(dproj_ref[:, sl], wm_ref[:, sl])
        dx, dg = _rms_bwd(x_ref[...], g_ref[...], dn1)
        dx_ref[...] = dh1_ref[...] + dx
        dg_ref[...] += dg

        @pl.when(i == n_t - 1)
        def _():
            dgq_ref[...] = gq_acc[:, :HEAD_DIM] + gq_acc[:, HEAD_DIM:]
            dgk_ref[...] = gk_acc[:, :HEAD_DIM] + gk_acc[:, HEAD_DIM:]

    row_spec = lambda w, col=0: pl.BlockSpec((tm, w), lambda i: (n_t - 1 - i, col))
    full = lambda a: pl.BlockSpec(a.shape, lambda i: (0,) * a.ndim)
    acc = lambda r, c: pl.BlockSpec((r, c), lambda i: (0, 0))
    return pl.pallas_call(
        body, name="in_proj_bwd", grid=(n_t,),
        out_shape=(jax.ShapeDtypeStruct((t_len, MAIN_W), BF), jax.ShapeDtypeStruct((t_len, LANES), BF),
                   jax.ShapeDtypeStruct((t_len, D_MODEL), F32), jax.ShapeDtypeStruct((1, D_MODEL), F32),
                   jax.ShapeDtypeStruct((1, HEAD_DIM), F32), jax.ShapeDtypeStruct((1, HEAD_DIM), F32),
                   jax.ShapeDtypeStruct((1, LANES), F32)),
        in_specs=[row_spec(D_MODEL), full(g_mix), row_spec(D_MODEL)] + [row_spec(GROUP_W)] * 7
        + [row_spec(LANES), row_spec(GROUP_W, 4), row_spec(GROUP_W, 5), row_spec(LANES), row_spec(LANES), row_spec(LANES),
           full(gq_t), full(gk_t), full(w_main), full(w_ff)],
        out_specs=(row_spec(MAIN_W), row_spec(LANES), row_spec(D_MODEL), acc(1, D_MODEL), acc(1, HEAD_DIM), acc(1, HEAD_DIM),
                   acc(1, LANES)),
        scratch_shapes=[pltpu.VMEM((8, LANES), F32), pltpu.VMEM((1, LANES), F32), pltpu.VMEM((1, LANES), F32)],
        compiler_params=_cparams(("arbitrary",)),
    )(x, g_mix, dh1, dq_r, dk_r, dv_r, drg, dq_f, dk_f, dv_f, df_col, proj, proj, z, cos_t, sin_t, gq_t, gk_t, w_main, w_ff)


def _matmul_tn(a, b, name, bm=256, bk=512):
    t_len, m = a.shape
    n = b.shape[1]
    bm, bk = min(bm, m), min(bk, t_len)

    def body(a_ref, b_ref, o_ref):
        @pl.when(pl.program_id(1) == 0)
        def _():
            o_ref[...] = jnp.zeros_like(o_ref)

        o_ref[...] += _dot_tn(a_ref[...], b_ref[...])

    return pl.pallas_call(
        body, name=name, grid=(m // bm, t_len // bk),
        out_shape=jax.ShapeDtypeStruct((m, n), F32),
        in_specs=[pl.BlockSpec((bk, bm), lambda i, k: (k, i)), pl.BlockSpec((bk, n), lambda i, k: (k, 0))],
        out_specs=pl.BlockSpec((bm, n), lambda i, k: (i, 0)),
        compiler_params=_cparams(("arbitrary", "arbitrary")),
    )(a, b)


def _place():
    x, y, c = lax.axis_index("x"), lax.axis_index("y"), lax.axis_index("c")
    chips = [(1 - x, y), (x, 1 - y), (1 - x, 1 - y)]
    return x, y, c, chips


def _all_gather_weights(shards):
    n_w = len(shards)

    def body(*refs):
        ins, outs = refs[:n_w], refs[n_w:2 * n_w]
        send_sems, recv_sems, local_sems = refs[2 * n_w:]
        x, y, c, chips = _place()
        me_chip = 2 * x + y
        sibling = (x, y, 1 - c)

        def copy(w, k, slot, half, to, src=None):
            dst = outs[w].at[slot, half]
            return pltpu.make_async_remote_copy(src_ref=dst if src is None else src, dst_ref=dst,
                                                send_sem=send_sems.at[w, k], recv_sem=recv_sems.at[w, k],
                                                device_id=to, device_id_type=MESH)

        local = [pltpu.make_async_copy(ins[w], outs[w].at[me_chip], local_sems.at[w]) for w in range(n_w)]
        for cp in local:
            cp.start()
        first = [copy(w, j, me_chip, c, (*chip, c), src=ins[w].at[c]) for w in range(n_w) for j, chip in enumerate(chips)]
        for cp in first:
            cp.start()
        passed = []
        for w in range(n_w):
            for j, (px, py) in enumerate(chips):
                copy(w, j, 2 * px + py, c, (x, y, c)).wait_recv()
                fwd = copy(w, 3 + j, 2 * px + py, c, sibling)
                fwd.start()
                passed.append(fwd)
        for w in range(n_w):
            for j, (px, py) in enumerate(chips):
                copy(w, 3 + j, 2 * px + py, 1 - c, (x, y, c)).wait_recv()
        for cp in first + passed:
            cp.wait_send()
        for cp in local:
            cp.wait()

    return pl.pallas_call(
        body, name="all_gather_weights",
        out_shape=tuple(jax.ShapeDtypeStruct((4,) + s.shape, s.dtype) for s in shards),
        in_specs=[ANY] * n_w, out_specs=(ANY,) * n_w,
        scratch_shapes=[pltpu.SemaphoreType.DMA((n_w, 6)), pltpu.SemaphoreType.DMA((n_w, 6)), pltpu.SemaphoreType.DMA((n_w,))],
    )(*shards)


def _exchange_core_halves(grads):
    n_w = len(grads)

    def body(*refs):
        ins, mine, theirs = refs[:n_w], refs[n_w:2 * n_w], refs[2 * n_w:3 * n_w]
        send_sems, recv_sems, local_sems = refs[3 * n_w:]
        x, y, c, _ = _place()
        local = [pltpu.make_async_copy(ins[w].at[:, c], mine[w], local_sems.at[w]) for w in range(n_w)]
        remote = [pltpu.make_async_remote_copy(src_ref=ins[w].at[:, 1 - c], dst_ref=theirs[w], send_sem=send_sems.at[w],
                                               recv_sem=recv_sems.at[w], device_id=(x, y, 1 - c), device_id_type=MESH)
                  for w in range(n_w)]
        for cp in local + remote:
            cp.start()
        for cp in remote + local:
            cp.wait()

    half = tuple(jax.ShapeDtypeStruct((4,) + g.shape[2:], g.dtype) for g in grads)
    return pl.pallas_call(
        body, name="exchange_core_halves", out_shape=half + half,
        in_specs=[ANY] * n_w, out_specs=(ANY,) * (2 * n_w),
        scratch_shapes=[pltpu.SemaphoreType.DMA((n_w,)), pltpu.SemaphoreType.DMA((n_w,)), pltpu.SemaphoreType.DMA((n_w,))],
    )(*grads)


def _add_pairs(a, b, name):
    _, r, c = a.shape
    rb = 32 if r % 32 == 0 else r

    def body(a_ref, b_ref, o_ref, ob_ref):
        s = a_ref[...] + b_ref[...]
        o_ref[...] = s
        ob_ref[...] = s.astype(BF)

    spec = pl.BlockSpec((4, rb, c), lambda i: (0, i, 0))
    return pl.pallas_call(
        body, name=name, grid=(r // rb,),
        out_shape=(jax.ShapeDtypeStruct(a.shape, F32), jax.ShapeDtypeStruct(a.shape, BF)),
        in_specs=[spec, spec], out_specs=(spec, spec), compiler_params=_cparams(("arbitrary",)),
    )(a, b)


def _scatter_to_chips(sums_f32, sums_bf16):
    n_w = len(sums_f32)

    def body(*refs):
        f32s, bfs, own, got = refs[:n_w], refs[n_w:2 * n_w], refs[2 * n_w:3 * n_w], refs[3 * n_w:4 * n_w]
        send_sems, recv_sems, local_sems = refs[4 * n_w:]
        x, y, c, chips = _place()
        local = [pltpu.make_async_copy(f32s[w].at[2 * x + y], own[w], local_sems.at[w]) for w in range(n_w)]
        remote = [pltpu.make_async_remote_copy(src_ref=bfs[w].at[2 * px + py], dst_ref=got[w].at[j], send_sem=send_sems.at[w, j],
                                               recv_sem=recv_sems.at[w, j], device_id=(px, py, c), device_id_type=MESH)
                  for w in range(n_w) for j, (px, py) in enumerate(chips)]
        for cp in local + remote:
            cp.start()
        for cp in remote + local:
            cp.wait()

    return pl.pallas_call(
        body, name="scatter_to_chips",
        out_shape=tuple(jax.ShapeDtypeStruct(s.shape[1:], F32) for s in sums_f32)
        + tuple(jax.ShapeDtypeStruct((3,) + s.shape[1:], BF) for s in sums_bf16),
        in_specs=[ANY] * (2 * n_w), out_specs=(ANY,) * (2 * n_w),
        scratch_shapes=[pltpu.SemaphoreType.DMA((n_w, 3)), pltpu.SemaphoreType.DMA((n_w, 3)), pltpu.SemaphoreType.DMA((n_w,))],
    )(*sums_f32, *sums_bf16)


def _add_received(own, got, name):
    r, c = own.shape
    rb = 32 if r % 32 == 0 else r

    def body(o_ref, g_ref, out_ref):
        out_ref[...] = ((o_ref[...] + g_ref[0].astype(F32)) + g_ref[1].astype(F32)) + g_ref[2].astype(F32)

    return pl.pallas_call(
        body, name=name, grid=(r // rb,), out_shape=jax.ShapeDtypeStruct((r, c), F32),
        in_specs=[pl.BlockSpec((rb, c), lambda i: (i, 0)), pl.BlockSpec((3, rb, c), lambda i: (0, i, 0))],
        out_specs=pl.BlockSpec((rb, c), lambda i: (i, 0)), compiler_params=_cparams(("arbitrary",)),
    )(own, got)


def _share_with_sibling(halves):
    n_w = len(halves)

    def body(*refs):
        ins, outs = refs[:n_w], refs[n_w:2 * n_w]
        send_sems, recv_sems, local_sems = refs[2 * n_w:]
        x, y, c, _ = _place()
        local = [pltpu.make_async_copy(ins[w], outs[w].at[c], local_sems.at[w]) for w in range(n_w)]
        remote = [pltpu.make_async_remote_copy(src_ref=ins[w], dst_ref=outs[w].at[c], send_sem=send_sems.at[w],
                                               recv_sem=recv_sems.at[w], device_id=(x, y, 1 - c), device_id_type=MESH)
                  for w in range(n_w)]
        for cp in local + remote:
            cp.start()
        for cp in remote + local:
            cp.wait()

    return pl.pallas_call(
        body, name="share_with_sibling",
        out_shape=tuple(jax.ShapeDtypeStruct((2,) + h.shape, h.dtype) for h in halves),
        in_specs=[ANY] * n_w, out_specs=(ANY,) * n_w,
        scratch_shapes=[pltpu.SemaphoreType.DMA((n_w,)), pltpu.SemaphoreType.DMA((n_w,)), pltpu.SemaphoreType.DMA((n_w,))],
    )(*halves)


def _all_reduce_small(pack):
    r, c = pack.shape

    def body(p_ref, out_ref, slots, send_sems, recv_sems):
        x, y, cc, _ = _place()
        me = 4 * x + 2 * y + cc
        slots[me] = p_ref[...]
        copies = []
        for k in range(1, 8):
            dx, dy, dc = (k >> 2) & 1, (k >> 1) & 1, k & 1
            to = (1 - x if dx else x, 1 - y if dy else y, 1 - cc if dc else cc)
            cp = pltpu.make_async_remote_copy(src_ref=p_ref, dst_ref=slots.at[me], send_sem=send_sems.at[k - 1],
                                              recv_sem=recv_sems.at[k - 1], device_id=to, device_id_type=MESH)
            cp.start()
            copies.append(cp)
        for cp in copies:
            cp.wait()
        total = slots[0]
        for d in range(1, 8):
            total = total + slots[d]
        out_ref[...] = total

    return pl.pallas_call(
        body, name="all_reduce_small", out_shape=jax.ShapeDtypeStruct((r, c), F32),
        in_specs=[VMEM_SPEC], out_specs=VMEM_SPEC,
        scratch_shapes=[pltpu.VMEM((8, r, c), F32), pltpu.SemaphoreType.DMA((7,)), pltpu.SemaphoreType.DMA((7,))],
    )(pack)


def _adamw(w, g, m, v, name):
    r, c = w.shape
    rb = 64 if r % 64 == 0 else r
    c1 = 1.0 - ADAM_B1 ** ADAM_STEP
    c2 = 1.0 - ADAM_B2 ** ADAM_STEP

    def body(w_ref, g_ref, m_ref, v_ref, d_ref, nm_ref, nv_ref):
        gv = g_ref[...]
        nm = ADAM_B1 * m_ref[...] + (1.0 - ADAM_B1) * gv
        nv = ADAM_B2 * v_ref[...] + (1.0 - ADAM_B2) * (gv * gv)
        nm_ref[...] = nm
        nv_ref[...] = nv
        d_ref[...] = -ADAM_LR * ((nm / c1) / (jnp.sqrt(nv / c2) + ADAM_EPS) + ADAM_WD * w_ref[...])

    spec = pl.BlockSpec((rb, c), lambda i: (i, 0))
    return pl.pallas_call(
        body, name=name, grid=(r // rb,), out_shape=(jax.ShapeDtypeStruct((r, c), F32),) * 3,
        in_specs=[spec] * 4, out_specs=(spec,) * 3, compiler_params=_cparams(("arbitrary",)),
    )(w, g, m, v)


def _rope_tables(t_len):
    inv_freq = ROPE_BASE ** (-jnp.arange(0, HEAD_DIM, 2, dtype=F32) / HEAD_DIM)
    ang = jnp.arange(t_len, dtype=F32)[:, None] * inv_freq[None, :]
    cos, sin = jnp.cos(ang), jnp.sin(ang)
    cos_t = jnp.concatenate([cos, cos, cos, cos], axis=-1)
    sin_t = jnp.concatenate([-sin, sin, -sin, sin], axis=-1)
    return cos_t, sin_t


def _cols_to_shards(dw):
    r, n = dw.shape
    return jnp.transpose(dw.reshape(2, r // 2, 4, n // 4), (2, 0, 1, 3))


def _rows_to_shards(dw):
    r, n = dw.shape
    return dw.reshape(4, 2, r // 8, n)


def _pad_row(a, width=D_MODEL):
    a = a.reshape(1, -1)
    return jnp.pad(a, ((0, 0), (0, width - a.shape[1])))


def kernel(x, mem, g_mix, w_in, b_forget, g_ret_out, g_fox_q, g_fox_k, w_out, g_xattn, w_xq, w_xkv, g_mem, g_xq, g_xk, w_xo, g_ffn, w_gate, w_up, w_down, loss_target, m_g_mix, m_w_in, m_b_forget, m_g_ret_out, m_g_fox_q, m_g_fox_k, m_w_out, m_g_xattn, m_w_xq, m_w_xkv, m_g_mem, m_g_xq, m_g_xk, m_w_xo, m_g_ffn, m_w_gate, m_w_up, m_w_down, v_g_mix, v_w_in, v_b_forget, v_g_ret_out, v_g_fox_q, v_g_fox_k, v_w_out, v_g_xattn, v_w_xq, v_w_xkv, v_g_mem, v_g_xq, v_g_xk, v_w_xo, v_g_ffn, v_w_gate, v_w_up, v_w_down):
    big = {"w_in": (w_in, m_w_in, v_w_in), "w_out": (w_out, m_w_out, v_w_out), "w_xq": (w_xq, m_w_xq, v_w_xq),
           "w_xkv": (w_xkv, m_w_xkv, v_w_xkv), "w_xo": (w_xo, m_w_xo, v_w_xo), "w_gate": (w_gate, m_w_gate, v_w_gate),
           "w_up": (w_up, m_w_up, v_w_up), "w_down": (w_down, m_w_down, v_w_down)}
    big_names = list(big)
    col_sharded = ("w_in", "w_xkv", "w_gate", "w_up")

    shards = []
    for n in big_names:
        w = big[n][0][0]
        shards.append(w.astype(BF).reshape(2, w.shape[0] // 2, w.shape[1]))
    gathered = _all_gather_weights(shards)
    full = {}
    for n, g in zip(big_names, gathered):
        _, _, rh, cs = g.shape
        g = g.reshape(4, 2 * rh, cs)
        full[n] = jnp.transpose(g, (1, 0, 2)).reshape(2 * rh, 4 * cs) if n in col_sharded else g.reshape(8 * rh, cs)
    small_w = {"g_mix": g_mix, "b_forget": b_forget, "g_ret_out": g_ret_out, "g_fox_q": g_fox_q, "g_fox_k": g_fox_k,
               "g_xattn": g_xattn, "g_mem": g_mem, "g_xq": g_xq, "g_xk": g_xk, "g_ffn": g_ffn}
    m_small = {"g_mix": m_g_mix, "b_forget": m_b_forget, "g_ret_out": m_g_ret_out, "g_fox_q": m_g_fox_q, "g_fox_k": m_g_fox_k,
               "g_xattn": m_g_xattn, "g_mem": m_g_mem, "g_xq": m_g_xq, "g_xk": m_g_xk, "g_ffn": m_g_ffn}
    v_small = {"g_mix": v_g_mix, "b_forget": v_b_forget, "g_ret_out": v_g_ret_out, "g_fox_q": v_g_fox_q, "g_fox_k": v_g_fox_k,
               "g_xattn": v_g_xattn, "g_mem": v_g_mem, "g_xq": v_g_xq, "g_xk": v_g_xk, "g_ffn": v_g_ffn}
    loss_part, grad_x, dw, small_g = _local_step(x[0], mem[0], loss_target[0], full, small_w)
    return _reduce_and_update(big, big_names, col_sharded, dw, small_w, small_g, loss_part, grad_x, m_small, v_small)


def _local_step(xs, mems, tgt, full, small_w):
    g_mix, b_forget, g_ret_out, g_fox_q, g_fox_k = (small_w[n] for n in ("g_mix", "b_forget", "g_ret_out", "g_fox_q", "g_fox_k"))
    g_xattn, g_mem, g_xq, g_xk, g_ffn = (small_w[n] for n in ("g_xattn", "g_mem", "g_xq", "g_xk", "g_ffn"))
    w_main = full["w_in"][:, :MAIN_W]
    w_ff = jnp.pad(full["w_in"][:, MAIN_W:], ((0, 0), (0, LANES - (IN_W - MAIN_W))))
    t_len = xs.shape[0]
    cos_t, sin_t = _rope_tables(t_len)
    tables = _decay_tables(min(RET_BLOCK, t_len))
    gq_t = jnp.concatenate([g_fox_q, g_fox_q], axis=-1)
    gk_t = jnp.concatenate([g_fox_k, g_fox_k], axis=-1)
    b_pad = _pad_row(b_forget, LANES)
    g_ret = g_ret_out.reshape(N_HEADS // 2, 1, LANES)

    memn, kraw, kn, vmem = _mem_kv_fwd(mems, g_mem, full["w_xkv"], g_xk)
    n1, proj, rq, rk, fq, fk, z, fcum = _in_proj_fwd(xs, g_mix, w_main, w_ff, b_pad, cos_t, sin_t, gq_t, gk_t)
    f8 = fcum[:, :N_HEADS]
    f_col = jnp.transpose(f8.reshape(t_len, N_HEADS // 2, 2), (1, 0, 2))
    f_row = jnp.transpose(f8).reshape(N_HEADS // 2, 2, t_len)
    raw, mix_r, states = _retention_fwd(rq, rk, proj, g_ret, tables)
    mix_f, o32, lse = _fox_fwd(fq, fk, proj, f_col, f_row)
    h1, hn2, qx, o_x, h2 = _attn_out_xattn_fwd(xs, mix_r, mix_f, full["w_out"], g_xattn, full["w_xq"], g_xq, kn, vmem, full["w_xo"])
    hn3, gate, up, act, dh3, loss_part = _ffn_loss_fwd(h2, g_ffn, full["w_gate"], full["w_up"], full["w_down"], tgt)

    dgate, dup, dh2, dg_ffn = _ffn_bwd(dh3, gate, up, h2, g_ffn, full["w_gate"], full["w_up"], full["w_down"])
    dqx, dh1, dmr, dmf, dkn, dvm, dg_xattn, dg_xq = _attn_out_xattn_bwd(dh2, h1, qx, kn, vmem, full["w_xo"], full["w_xq"],
                                                                      full["w_out"], g_xattn, g_xq)
    dw_xkv, dg_mem, dg_xk = _mem_kv_bwd(dkn, dvm, kraw, mems, memn, g_mem, g_xk, full["w_xkv"])
    lse_row = jnp.transpose(lse, (0, 2, 1))
    dq_f, dk_f, dv_f, df = _fox_bwd(fq, fk, proj, dmf, o32, lse_row, f_row, f_col)
    dq_r, dk_r, dv_r, drg, dg_ret = _retention_bwd(dmr, raw, proj, g_ret, rq, rk, states, tables)
    df_col = jnp.pad(jnp.transpose(df, (1, 0, 2)).reshape(t_len, N_HEADS), ((0, 0), (0, LANES - N_HEADS)))
    dproj, dz, grad_x, dg_mix, dg_fq, dg_fk, db = _in_proj_bwd(xs, g_mix, dh1, dq_r, dk_r, dv_r, drg, dq_f, dk_f, dv_f, df_col,
                                                              proj, z, cos_t, sin_t, gq_t, gk_t, w_main, w_ff)

    dw = {
        "w_in": jnp.concatenate([_matmul_tn(n1, dproj, "dw_in_main"), _matmul_tn(n1, dz, "dw_in_ff")[:, :IN_W - MAIN_W]], axis=1),
        "w_out": jnp.concatenate([_matmul_tn(mix_r, dh1, "dw_out_ret"), _matmul_tn(mix_f, dh1, "dw_out_fox")], axis=0),
        "w_xq": _matmul_tn(hn2, dqx, "dw_xq"),
        "w_xkv": dw_xkv,
        "w_xo": _matmul_tn(o_x, dh2, "dw_xo"),
        "w_gate": _matmul_tn(hn3, dgate, "dw_gate"),
        "w_up": _matmul_tn(hn3, dup, "dw_up"),
        "w_down": _matmul_tn(act, dh3, "dw_down"),
    }
    small_g = {"g_mix": dg_mix, "b_forget": db[:, :N_HEADS], "g_ret_out": dg_ret, "g_fox_q": dg_fq, "g_fox_k": dg_fk,
               "g_xattn": dg_xattn, "g_mem": dg_mem, "g_xq": dg_xq, "g_xk": dg_xk, "g_ffn": dg_ffn}
    return loss_part, grad_x, dw, small_g


def _reduce_and_update(big, big_names, col_sharded, dw, small_w, small_g, loss_part, grad_x, m_small, v_small):
    parts =[_cols_to_shards(dw[n]) if n in col_sharded else _rows_to_shards(dw[n]) for n in big_names]
    ex = _exchange_core_halves(parts)
    mine, theirs = ex[:len(parts)], ex[len(parts):]
    sums = [_add_pairs(a, b, f"core_sum_{n}") for n, a, b in zip(big_names, mine, theirs)]
    sc = _scatter_to_chips([s[0] for s in sums], [s[1] for s in sums])
    own, got = sc[:len(parts)], sc[len(parts):]
    finals = [_add_received(o, g, f"chip_sum_{n}") for n, o, g in zip(big_names, own, got)]
    shared = _share_with_sibling(finals)
    grads, deltas, new_m, new_v = {}, {}, {}, {}
    for n, s in zip(big_names, shared):
        w, m, v = big[n]
        g = s.reshape(w.shape[1:])
        d, nm, nv = _adamw(w[0], g, m[0], v[0], f"adamw_{n}")
        grads[n], deltas[n], new_m[n], new_v[n] = g[None], d[None], nm[None], nv[None]

    small_names = list(small_w)
    pad_rows = SMALL_ROWS - len(small_names) - 1
    stack = lambda d: jnp.concatenate([_pad_row(d[n]) for n in small_names] + [jnp.zeros((pad_rows + 1, D_MODEL), F32)], axis=0)
    g_pack = jnp.concatenate([_pad_row(small_g[n]) for n in small_names] + [_pad_row(loss_part[0:1, 0:1])]
                             + [jnp.zeros((pad_rows, D_MODEL), F32)], axis=0)
    g_tot = _all_reduce_small(g_pack)
    d_s, m_s, v_s = _adamw(stack(small_w), g_tot, stack(m_small), stack(v_small), "adamw_small")
    for i, n in enumerate(small_names):
        shape = small_w[n].shape
        size = int(np.prod(shape))
        grads[n] = g_tot[i, :size].reshape(shape)
        deltas[n], new_m[n], new_v[n] = d_s[i, :size].reshape(shape), m_s[i, :size].reshape(shape), v_s[i, :size].reshape(shape)
    loss = g_tot[len(small_names), 0]

    order = ["g_mix", "w_in", "b_forget", "g_ret_out", "g_fox_q", "g_fox_k", "w_out", "g_xattn", "w_xq", "w_xkv", "g_mem", "g_xq",
             "g_xk", "w_xo", "g_ffn", "w_gate", "w_up", "w_down"]
    return (loss, grad_x[None], *[grads[n] for n in order], *[deltas[n] for n in order], *[new_m[n] for n in order],
            *[new_v[n] for n in order])
```

```python
import functools

import numpy as np
import jax
import jax.numpy as jnp
from jax import lax
from jax.experimental import pallas as pl
from jax.experimental.pallas import tpu as pltpu

F32 = jnp.float32
BF = jnp.bfloat16

D_MODEL = 1024
HEAD_DIM = 64
N_HEADS = 8
GROUP_W = 512
N_XH = 4
XHD = 256
D_FF = 2816
MAIN_W = 3584
IN_W = 3592
ROPE_BASE = 10000.0
EPS = 1e-6
NEG = -1e30
LANES = 128
RET_BLOCK = 256
REF_CHUNK = 64
ROW_TILE = 256
ATT_BLOCK = 256
SMALL_ROWS = 16
VMEM_LIMIT = 56 * 1024 * 1024

ADAM_LR = 0.001
ADAM_B1 = 0.9
ADAM_B2 = 0.999
ADAM_EPS = 1e-08
ADAM_WD = 0.01
ADAM_STEP = 10

MESH = pl.DeviceIdType.MESH
ANY = pl.BlockSpec(memory_space=pl.ANY)
VMEM_SPEC = pl.BlockSpec(memory_space=pltpu.VMEM)


def _cparams(sem=None, vmem=VMEM_LIMIT):
    return pltpu.CompilerParams(dimension_semantics=sem, vmem_limit_bytes=vmem)


def _dot(a, b):
    return jnp.dot(a.astype(BF), b.astype(BF), preferred_element_type=F32)


def _dot_nt(a, b):
    return lax.dot_general(a.astype(BF), b.astype(BF), (((1,), (1,)), ((), ())), preferred_element_type=F32)


def _dot_tn(a, b):
    return lax.dot_general(a.astype(BF), b.astype(BF), (((0,), (0,)), ((), ())), preferred_element_type=F32)


def _split3(x):
    hi = x.astype(BF)
    r = x - hi.astype(F32)
    mid = r.astype(BF)
    lo = (r - mid.astype(F32)).astype(BF)
    return hi, mid, lo


def _dot_exact(ind, x):
    hi, mid, lo = _split3(x)
    return (jnp.dot(ind, lo, preferred_element_type=F32) + jnp.dot(ind, mid, preferred_element_type=F32)
            + jnp.dot(ind, hi, preferred_element_type=F32))


def _dot_nt_exact(ind, x):
    hi, mid, lo = _split3(x)
    dn = (((1,), (1,)), ((), ()))
    return (lax.dot_general(ind, lo, dn, preferred_element_type=F32) + lax.dot_general(ind, mid, dn, preferred_element_type=F32)
            + lax.dot_general(ind, hi, dn, preferred_element_type=F32))


def _sigmoid(x):
    return 1.0 / (1.0 + jnp.exp(-x))


def _rms_fwd(x, g):
    r = lax.rsqrt(jnp.mean(x * x, axis=-1, keepdims=True) + EPS)
    return x * r * g


def _rms_bwd(x, g, dy):
    r = lax.rsqrt(jnp.mean(x * x, axis=-1, keepdims=True) + EPS)
    xh = x * r
    dg = jnp.sum(dy * xh, axis=0, keepdims=True)
    dxh = dy * g
    dx = r * (dxh - xh * jnp.mean(dxh * xh, axis=-1, keepdims=True))
    return dx, dg


def _group_mean64(x):
    lane = lax.broadcasted_iota(jnp.int32, x.shape, 1)
    lo = lane < HEAD_DIM
    s_lo = jnp.sum(jnp.where(lo, x, 0.0), axis=-1, keepdims=True)
    s_hi = jnp.sum(jnp.where(lo, 0.0, x), axis=-1, keepdims=True)
    return jnp.where(lo, s_lo, s_hi) * (1.0 / HEAD_DIM)


def _swap32(x):
    lane = lax.broadcasted_iota(jnp.int32, x.shape, 1)
    first = (lane % HEAD_DIM) < (HEAD_DIM // 2)
    return jnp.where(first, pltpu.roll(x, LANES - HEAD_DIM // 2, axis=1), pltpu.roll(x, HEAD_DIM // 2, axis=1))


def _chunks(w):
    return [slice(j * LANES, (j + 1) * LANES) for j in range(w // LANES)]


def _mem_kv_fwd(mem, g_mem, w_xkv, g_xk):
    m_tok = mem.shape[0]

    def body(mem_ref, gm_ref, w_ref, gk_ref, memn_ref, kraw_ref, kn_ref, v_ref):
        mn = _rms_fwd(mem_ref[...], gm_ref[...]).astype(BF)
        memn_ref[...] = mn
        kv = jnp.dot(mn, w_ref[...], preferred_element_type=F32)
        k = kv[:, :D_MODEL]
        kraw_ref[...] = k
        v_ref[...] = kv[:, D_MODEL:].astype(BF)
        for h in range(N_XH):
            sl = slice(h * XHD, (h + 1) * XHD)
            kn_ref[:, sl] = _rms_fwd(k[:, sl], gk_ref[...]).astype(BF)

    return pl.pallas_call(
        body, name="mem_kv_fwd",
        out_shape=(jax.ShapeDtypeStruct((m_tok, D_MODEL), BF), jax.ShapeDtypeStruct((m_tok, D_MODEL), F32),
                   jax.ShapeDtypeStruct((m_tok, D_MODEL), BF), jax.ShapeDtypeStruct((m_tok, D_MODEL), BF)),
        in_specs=[VMEM_SPEC] * 4, out_specs=(VMEM_SPEC,) * 4, compiler_params=_cparams(),
    )(mem, g_mem, w_xkv, g_xk)


def _in_proj_fwd(x, g_mix, w_main, w_ff, b_pad, cos_t, sin_t, gq_t, gk_t):
    t_len = x.shape[0]
    tm = min(ROW_TILE, t_len)
    n_t = t_len // tm

    def body(x_ref, g_ref, wm_ref, wf_ref, b_ref, cos_ref, sin_ref, gq_ref, gk_ref,
             n1_ref, proj_ref, rq_ref, rk_ref, fq_ref, fk_ref, z_ref, fc_ref, carry):
        i = pl.program_id(0)

        @pl.when(i == 0)
        def _():
            carry[...] = jnp.zeros_like(carry)

        n1 = _rms_fwd(x_ref[...], g_ref[...]).astype(BF)
        n1_ref[...] = n1
        proj = jnp.dot(n1, wm_ref[...], preferred_element_type=F32)
        proj_ref[...] = proj.astype(BF)
        c, s = cos_ref[...], sin_ref[...]
        for j, sl in enumerate(_chunks(GROUP_W)):
            q = proj[:, sl]
            rq_ref[:, sl] = ((q * c + _swap32(q) * s) * 0.125).astype(BF)
            k = proj[:, GROUP_W + j * LANES:GROUP_W + (j + 1) * LANES]
            rk_ref[:, sl] = (k * c + _swap32(k) * s).astype(BF)
            fq = proj[:, 4 * GROUP_W + j * LANES:4 * GROUP_W + (j + 1) * LANES]
            fq_ref[:, sl] = (fq * lax.rsqrt(_group_mean64(fq * fq) + EPS) * gq_ref[...] * 0.125).astype(BF)
            fk = proj[:, 5 * GROUP_W + j * LANES:5 * GROUP_W + (j + 1) * LANES]
            fk_ref[:, sl] = (fk * lax.rsqrt(_group_mean64(fk * fk) + EPS) * gk_ref[...]).astype(BF)
        z = jnp.dot(n1, wf_ref[...], preferred_element_type=F32) + b_ref[...]
        z_ref[...] = z
        lane = lax.broadcasted_iota(jnp.int32, z.shape, 1)
        lf = jnp.where(lane < N_HEADS, jnp.minimum(z, 0.0) - jnp.log(1.0 + jnp.exp(-jnp.abs(z))), 0.0)
        row = lax.broadcasted_iota(jnp.int32, (tm, tm), 0)
        col = lax.broadcasted_iota(jnp.int32, (tm, tm), 1)
        tri = (row >= col).astype(BF)
        fc = _dot_exact(tri, lf) + carry[0:1, :]
        fc_ref[...] = fc
        carry[...] = jnp.broadcast_to(fc[tm - 1:tm, :], carry.shape)

    row_spec = lambda w: pl.BlockSpec((tm, w), lambda i: (i, 0))
    full = lambda a: pl.BlockSpec(a.shape, lambda i: (0,) * a.ndim)
    return pl.pallas_call(
        body, name="in_proj_fwd", grid=(n_t,),
        out_shape=(jax.ShapeDtypeStruct((t_len, D_MODEL), BF), jax.ShapeDtypeStruct((t_len, MAIN_W), BF),
                   jax.ShapeDtypeStruct((t_len, GROUP_W), BF), jax.ShapeDtypeStruct((t_len, GROUP_W), BF),
                   jax.ShapeDtypeStruct((t_len, GROUP_W), BF), jax.ShapeDtypeStruct((t_len, GROUP_W), BF),
                   jax.ShapeDtypeStruct((t_len, LANES), F32), jax.ShapeDtypeStruct((t_len, LANES), F32)),
        in_specs=[row_spec(D_MODEL), full(g_mix), full(w_main), full(w_ff), full(b_pad), row_spec(LANES), row_spec(LANES),
                  full(gq_t), full(gk_t)],
        out_specs=(row_spec(D_MODEL), row_spec(MAIN_W), row_spec(GROUP_W), row_spec(GROUP_W), row_spec(GROUP_W),
                   row_spec(GROUP_W), row_spec(LANES), row_spec(LANES)),
        scratch_shapes=[pltpu.VMEM((8, LANES), F32)],
        compiler_params=_cparams(("arbitrary",)),
    )(x, g_mix, w_main, w_ff, b_pad, cos_t, sin_t, gq_t, gk_t)


def _decay_tables(c):
    h = np.arange(N_HEADS, dtype=np.float64)
    lg = np.log(1.0 - 2.0 ** (-5.0 - h)).astype(np.float32).astype(np.float64)
    t = np.arange(c)
    same_or_earlier = (t[None, :] // REF_CHUNK) <= (t[:, None] // REF_CHUNK)
    w = np.where(same_or_earlier[None], np.exp(lg[:, None, None] * np.abs(t[:, None] - t[None, :])[None]), 0.0)
    qd = np.exp(lg[:, None] * (t[None, :] + 1.0))
    kd = np.exp(lg[:, None] * (c - 1.0 - t[None, :]))
    cd = np.exp(lg * c)
    ones = np.ones((1, 1, HEAD_DIM))
    return (jnp.asarray(w, F32), jnp.asarray(qd[:, :, None] * ones, F32), jnp.asarray(kd[:, :, None] * ones, F32),
            jnp.asarray(cd[:, None, None] * np.ones((1, HEAD_DIM, HEAD_DIM)), F32))


def _retention_fwd(rq, rk, proj, g_ret, tables):
    t_len = rq.shape[0]
    c = min(RET_BLOCK, t_len)
    n_b = t_len // c
    wdec, qdec, kdec, cdec = tables
    v_col, g_col = 2 * GROUP_W // LANES, 3 * GROUP_W // LANES

    def body(q_ref, k_ref, v_ref, rg_ref, g_ref, w_ref, qd_ref, kd_ref, cd_ref, raw_ref, mix_ref, st_ref, state):
        i = pl.program_id(1)

        @pl.when(i == 0)
        def _():
            state[...] = jnp.zeros_like(state)

        q2, k2, v2 = q_ref[...], k_ref[...], v_ref[...]
        outs = []
        for hh in range(2):
            sl = slice(hh * HEAD_DIM, (hh + 1) * HEAD_DIM)
            q, k, v = q2[:, sl], k2[:, sl], v2[:, sl]
            sp = state[hh]
            st_ref[0, 0, hh] = sp
            a = _dot_nt(q, k) * w_ref[hh]
            o = _dot(a, v) + _dot(q.astype(F32) * qd_ref[hh], sp)
            state[hh] = sp * cd_ref[hh] + _dot_tn(k.astype(F32) * kd_ref[hh], v)
            outs.append(o)
        o2 = jnp.concatenate(outs, axis=-1)
        raw_ref[...] = o2
        xc = o2 - _group_mean64(o2)
        xh = xc * lax.rsqrt(_group_mean64(xc * xc) + EPS)
        gate = rg_ref[...].astype(F32)
        mix_ref[...] = (gate * _sigmoid(gate) * (xh * g_ref[0])).astype(BF)

    blk = lambda col0: pl.BlockSpec((c, LANES), lambda hp, i: (i, col0 + hp))
    tab = lambda a: pl.BlockSpec((2,) + a.shape[1:], lambda hp, i: (hp, 0, 0))
    return pl.pallas_call(
        body, name="retention_fwd", grid=(N_HEADS // 2, n_b),
        out_shape=(jax.ShapeDtypeStruct((t_len, GROUP_W), F32), jax.ShapeDtypeStruct((t_len, GROUP_W), BF),
                   jax.ShapeDtypeStruct((N_HEADS // 2, n_b, 2, HEAD_DIM, HEAD_DIM), F32)),
        in_specs=[blk(0), blk(0), blk(v_col), blk(g_col), pl.BlockSpec((1, 1, LANES), lambda hp, i: (hp, 0, 0)),
                  tab(wdec), tab(qdec), tab(kdec), tab(cdec)],
        out_specs=(blk(0), blk(0), pl.BlockSpec((1, 1, 2, HEAD_DIM, HEAD_DIM), lambda hp, i: (hp, i, 0, 0, 0))),
        scratch_shapes=[pltpu.VMEM((2, HEAD_DIM, HEAD_DIM), F32)],
        compiler_params=_cparams(("arbitrary", "arbitrary")),
    )(rq, rk, proj, proj, g_ret, wdec, qdec, kdec, cdec)


def _fox_fwd(fq, fk, proj, f_col, f_row):
    t_len = fq.shape[0]
    tq = min(ATT_BLOCK, t_len)
    n_q = t_len // tq
    v_col = 6 * GROUP_W // LANES

    def body(q_ref, k_ref, v_ref, fq_ref, fk_ref, o_ref, o32_ref, lse_ref):
        i = pl.program_id(1)
        q2 = q_ref[...]
        fq2 = fq_ref[0]
        row = i * tq + lax.broadcasted_iota(jnp.int32, (tq, tq), 0)
        col0 = lax.broadcasted_iota(jnp.int32, (tq, tq), 1)
        outs, lses = [], []
        for hh in range(2):
            sl = slice(hh * HEAD_DIM, (hh + 1) * HEAD_DIM)
            q = q2[:, sl]
            f_q = fq2[:, hh:hh + 1]

            def step(j, carry, sl=sl, q=q, f_q=f_q, hh=hh):
                m, l, acc = carry
                rows = pl.ds(pl.multiple_of(j * tq, tq), tq)
                k = k_ref[rows, :][:, sl]
                v = v_ref[rows, :][:, sl]
                s = _dot_nt(q, k) + f_q - fk_ref[0, hh:hh + 1, rows]
                s = jnp.where(row >= j * tq + col0, s, NEG)
                m_new = jnp.maximum(m, jnp.max(s, axis=-1, keepdims=True))
                alpha = jnp.exp(m - m_new)
                p = jnp.exp(s - m_new)
                l = l * alpha + jnp.sum(p, axis=-1, keepdims=True)
                acc = acc * alpha + _dot(p, v)
                return m_new, l, acc

            m, l, acc = lax.fori_loop(0, i + 1, step, (jnp.full((tq, 1), NEG, F32), jnp.zeros((tq, 1), F32),
                                                       jnp.zeros((tq, HEAD_DIM), F32)))
            outs.append(acc / l)
            lses.append(m + jnp.log(l))
        o2 = jnp.concatenate(outs, axis=-1)
        o32_ref[...] = o2
        o_ref[...] = o2.astype(BF)
        lse_ref[0] = jnp.concatenate(lses, axis=-1)

    return pl.pallas_call(
        body, name="fox_fwd", grid=(N_HEADS // 2, n_q),
        out_shape=(jax.ShapeDtypeStruct((t_len, GROUP_W), BF), jax.ShapeDtypeStruct((t_len, GROUP_W), F32),
                   jax.ShapeDtypeStruct((N_HEADS // 2, t_len, 2), F32)),
        in_specs=[pl.BlockSpec((tq, LANES), lambda hp, i: (i, hp)),
                  pl.BlockSpec((t_len, LANES), lambda hp, i: (0, hp)),
                  pl.BlockSpec((t_len, LANES), lambda hp, i: (0, v_col + hp)),
                  pl.BlockSpec((1, tq, 2), lambda hp, i: (hp, i, 0)),
                  pl.BlockSpec((1, 2, t_len), lambda hp, i: (hp, 0, 0))],
        out_specs=(pl.BlockSpec((tq, LANES), lambda hp, i: (i, hp)), pl.BlockSpec((tq, LANES), lambda hp, i: (i, hp)),
                   pl.BlockSpec((1, tq, 2), lambda hp, i: (hp, i, 0))),
        compiler_params=_cparams(("arbitrary", "arbitrary")),
    )(fq, fk, proj, f_col, f_row)


def _softmax_rows(s):
    p = jnp.exp(s - jnp.max(s, axis=-1, keepdims=True))
    return p / jnp.sum(p, axis=-1, keepdims=True)


def _attn_out_xattn_fwd(x, mix_r, mix_f, w_out, g_xattn, w_xq, g_xq, kn, v, w_xo):
    t_len = x.shape[0]
    tm = min(ROW_TILE, t_len)

    def body(x_ref, mr_ref, mf_ref, wo_ref, g_ref, wq_ref, gq_ref, kn_ref, v_ref, wxo_ref,
             h1_ref, hn_ref, qx_ref, o_ref, h2_ref):
        h1 = x_ref[...] + jnp.dot(mr_ref[...], wo_ref[:GROUP_W, :], preferred_element_type=F32) \
            + jnp.dot(mf_ref[...], wo_ref[GROUP_W:, :], preferred_element_type=F32)
        h1_ref[...] = h1
        hn = _rms_fwd(h1, g_ref[...]).astype(BF)
        hn_ref[...] = hn
        qx = jnp.dot(hn, wq_ref[...], preferred_element_type=F32).astype(BF)
        qx_ref[...] = qx
        for h in range(N_XH):
            sl = slice(h * XHD, (h + 1) * XHD)
            qn = _rms_fwd(qx[:, sl].astype(F32), gq_ref[...])
            p = _softmax_rows(_dot_nt(qn, kn_ref[:, sl]) * (XHD ** -0.5))
            o_ref[:, sl] = _dot(p, v_ref[:, sl]).astype(BF)
        h2_ref[...] = h1 + jnp.dot(o_ref[...], wxo_ref[...], preferred_element_type=F32)

    row_spec = lambda w: pl.BlockSpec((tm, w), lambda i: (i, 0))
    full = lambda a: pl.BlockSpec(a.shape, lambda i: (0,) * a.ndim)
    return pl.pallas_call(
        body, name="attn_out_xattn_fwd", grid=(t_len // tm,),
        out_shape=(jax.ShapeDtypeStruct((t_len, D_MODEL), F32), jax.ShapeDtypeStruct((t_len, D_MODEL), BF),
                   jax.ShapeDtypeStruct((t_len, D_MODEL), BF), jax.ShapeDtypeStruct((t_len, D_MODEL), BF),
                   jax.ShapeDtypeStruct((t_len, D_MODEL), F32)),
        in_specs=[row_spec(D_MODEL), row_spec(GROUP_W), row_spec(GROUP_W), full(w_out), full(g_xattn), full(w_xq), full(g_xq),
                  full(kn), full(v), full(w_xo)],
        out_specs=(row_spec(D_MODEL),) * 5,
        compiler_params=_cparams(("arbitrary",)),
    )(x, mix_r, mix_f, w_out, g_xattn, w_xq, g_xq, kn, v, w_xo)


def _ffn_loss_fwd(h2, g_ffn, w_gate, w_up, w_down, target):
    t_len = h2.shape[0]
    tm = min(ROW_TILE, t_len)

    def body(h2_ref, g_ref, wg_ref, wu_ref, wd_ref, tgt_ref, hn_ref, gate_ref, up_ref, act_ref, dh3_ref, loss_ref):
        @pl.when(pl.program_id(0) == 0)
        def _():
            loss_ref[...] = jnp.zeros_like(loss_ref)

        h2v = h2_ref[...]
        hn = _rms_fwd(h2v, g_ref[...]).astype(BF)
        hn_ref[...] = hn
        gate = jnp.dot(hn, wg_ref[...], preferred_element_type=F32)
        up = jnp.dot(hn, wu_ref[...], preferred_element_type=F32)
        gate_ref[...] = gate.astype(BF)
        up_ref[...] = up.astype(BF)
        act = (gate * _sigmoid(gate) * up).astype(BF)
        act_ref[...] = act
        diff = h2v + jnp.dot(act, wd_ref[...], preferred_element_type=F32) - tgt_ref[...]
        dh3_ref[...] = diff * (1.0 / D_MODEL)
        per_row = jnp.sum(diff * diff, axis=-1, keepdims=True) * (1.0 / D_MODEL)
        loss_ref[...] += 0.5 * jnp.sum(per_row, axis=0, keepdims=True)

    row_spec = lambda w: pl.BlockSpec((tm, w), lambda i: (i, 0))
    full = lambda a: pl.BlockSpec(a.shape, lambda i: (0,) * a.ndim, pipeline_mode=pl.Buffered(1))
    return pl.pallas_call(
        body, name="ffn_loss_fwd", grid=(t_len // tm,),
        out_shape=(jax.ShapeDtypeStruct((t_len, D_MODEL), BF), jax.ShapeDtypeStruct((t_len, D_FF), BF),
                   jax.ShapeDtypeStruct((t_len, D_FF), BF), jax.ShapeDtypeStruct((t_len, D_FF), BF),
                   jax.ShapeDtypeStruct((t_len, D_MODEL), F32), jax.ShapeDtypeStruct((8, LANES), F32)),
        in_specs=[row_spec(D_MODEL), full(g_ffn), full(w_gate), full(w_up), full(w_down), row_spec(D_MODEL)],
        out_specs=(row_spec(D_MODEL), row_spec(D_FF), row_spec(D_FF), row_spec(D_FF), row_spec(D_MODEL),
                   pl.BlockSpec((8, LANES), lambda i: (0, 0))),
        compiler_params=_cparams(("arbitrary",)),
    )(h2, g_ffn, w_gate, w_up, w_down, target)


def _ffn_bwd(dh3, gate, up, h2, g_ffn, w_gate, w_up, w_down):
    t_len = h2.shape[0]
    tm = min(ROW_TILE, t_len)

    def body(dh3_ref, gate_ref, up_ref, h2_ref, g_ref, wg_ref, wu_ref, wd_ref, dgate_ref, dup_ref, dh2_ref, dg_ref):
        @pl.when(pl.program_id(0) == 0)
        def _():
            dg_ref[...] = jnp.zeros_like(dg_ref)

        dh3v = dh3_ref[...]
        dact = _dot_nt(dh3v, wd_ref[...])
        g = gate_ref[...].astype(F32)
        sg = _sigmoid(g)
        dup = (dact * (g * sg)).astype(BF)
        dgate = (dact * up_ref[...].astype(F32) * (sg * (1.0 + g * (1.0 - sg)))).astype(BF)
        dup_ref[...] = dup
        dgate_ref[...] = dgate
        dhn = _dot_nt(dgate, wg_ref[...]) + _dot_nt(dup, wu_ref[...])
        dx, dg = _rms_bwd(h2_ref[...], g_ref[...], dhn)
        dh2_ref[...] = dh3v + dx
        dg_ref[...] += dg

    row_spec = lambda w: pl.BlockSpec((tm, w), lambda i: (i, 0))
    full = lambda a: pl.BlockSpec(a.shape, lambda i: (0,) * a.ndim, pipeline_mode=pl.Buffered(1))
    return pl.pallas_call(
        body, name="ffn_bwd", grid=(t_len // tm,),
        out_shape=(jax.ShapeDtypeStruct((t_len, D_FF), BF), jax.ShapeDtypeStruct((t_len, D_FF), BF),
                   jax.ShapeDtypeStruct((t_len, D_MODEL), F32), jax.ShapeDtypeStruct((1, D_MODEL), F32)),
        in_specs=[row_spec(D_MODEL), row_spec(D_FF), row_spec(D_FF), row_spec(D_MODEL), full(g_ffn), full(w_gate), full(w_up),
                  full(w_down)],
        out_specs=(row_spec(D_FF), row_spec(D_FF), row_spec(D_MODEL), pl.BlockSpec((1, D_MODEL), lambda i: (0, 0))),
        compiler_params=_cparams(("arbitrary",)),
    )(dh3, gate, up, h2, g_ffn, w_gate, w_up, w_down)


def _attn_out_xattn_bwd(dh2, h1, qx, kn, v, w_xo, w_xq, w_out, g_xattn, g_xq):
    t_len = h1.shape[0]
    tm = min(ROW_TILE, t_len)
    m_tok = kn.shape[0]

    def body(dh2_ref, h1_ref, qx_ref, kn_ref, v_ref, wxo_ref, wq_ref, wo_ref, g_ref, gq_ref,
             dqx_ref, dh1_ref, dmr_ref, dmf_ref, dkn_ref, dv_ref, dg_ref, dgq_ref, dqx_scr):
        @pl.when(pl.program_id(0) == 0)
        def _():
            dkn_ref[...] = jnp.zeros_like(dkn_ref)
            dv_ref[...] = jnp.zeros_like(dv_ref)
            dg_ref[...] = jnp.zeros_like(dg_ref)
            dgq_ref[...] = jnp.zeros_like(dgq_ref)

        dh2v = dh2_ref[...]
        do = _dot_nt(dh2v, wxo_ref[...])
        gq = gq_ref[...]
        dgq = jnp.zeros((1, XHD), F32)
        for h in range(N_XH):
            sl = slice(h * XHD, (h + 1) * XHD)
            qraw = qx_ref[:, sl].astype(F32)
            qn = _rms_fwd(qraw, gq)
            p = _softmax_rows(_dot_nt(qn, kn_ref[:, sl]) * (XHD ** -0.5))
            doh = do[:, sl]
            dv_ref[:, sl] += _dot_tn(p, doh)
            dp = _dot_nt(doh, v_ref[:, sl])
            ds = p * (dp - jnp.sum(dp * p, axis=-1, keepdims=True)) * (XHD ** -0.5)
            dqn = _dot(ds, kn_ref[:, sl])
            dkn_ref[:, sl] += _dot_tn(ds, qn)
            dx, dg_h = _rms_bwd(qraw, gq, dqn)
            dgq = dgq + dg_h
            dqx_scr[:, sl] = dx.astype(BF)
        dgq_ref[...] += dgq
        dqx = dqx_scr[...]
        dqx_ref[...] = dqx
        dhn = _dot_nt(dqx, wq_ref[...])
        dx, dg = _rms_bwd(h1_ref[...], g_ref[...], dhn)
        dg_ref[...] += dg
        dh1 = dh2v + dx
        dh1_ref[...] = dh1
        dmix = _dot_nt(dh1, wo_ref[...])
        dmr_ref[...] = dmix[:, :GROUP_W]
        dmf_ref[...] = dmix[:, GROUP_W:].astype(BF)

    row_spec = lambda w: pl.BlockSpec((tm, w), lambda i: (i, 0))
    full = lambda a: pl.BlockSpec(a.shape, lambda i: (0,) * a.ndim)
    acc = lambda r, c: pl.BlockSpec((r, c), lambda i: (0, 0))
    return pl.pallas_call(
        body, name="attn_out_xattn_bwd", grid=(t_len // tm,),
        out_shape=(jax.ShapeDtypeStruct((t_len, D_MODEL), BF), jax.ShapeDtypeStruct((t_len, D_MODEL), F32),
                   jax.ShapeDtypeStruct((t_len, GROUP_W), F32), jax.ShapeDtypeStruct((t_len, GROUP_W), BF),
                   jax.ShapeDtypeStruct((m_tok, D_MODEL), F32), jax.ShapeDtypeStruct((m_tok, D_MODEL), F32),
                   jax.ShapeDtypeStruct((1, D_MODEL), F32), jax.ShapeDtypeStruct((1, XHD), F32)),
        in_specs=[row_spec(D_MODEL), row_spec(D_MODEL), row_spec(D_MODEL), full(kn), full(v), full(w_xo), full(w_xq), full(w_out),
                  full(g_xattn), full(g_xq)],
        out_specs=(row_spec(D_MODEL), row_spec(D_MODEL), row_spec(GROUP_W), row_spec(GROUP_W), acc(m_tok, D_MODEL),
                   acc(m_tok, D_MODEL), acc(1, D_MODEL), acc(1, XHD)),
        scratch_shapes=[pltpu.VMEM((tm, D_MODEL), BF)],
        compiler_params=_cparams(("arbitrary",)),
    )(dh2, h1, qx, kn, v, w_xo, w_xq, w_out, g_xattn, g_xq)


def _mem_kv_bwd(dkn, dv, kraw, mem, memn, g_mem, g_xk, w_xkv):
    m_tok = mem.shape[0]

    def body(dkn_ref, dv_ref, kraw_ref, mem_ref, memn_ref, gm_ref, gk_ref, w_ref, dw_ref, dgm_ref, dgk_ref, dkv_scr):
        gk = gk_ref[...]
        dgk = jnp.zeros((1, XHD), F32)
        for h in range(N_XH):
            sl = slice(h * XHD, (h + 1) * XHD)
            dx, dg_h = _rms_bwd(kraw_ref[:, sl], gk, dkn_ref[:, sl])
            dgk = dgk + dg_h
            dkv_scr[:, sl] = dx.astype(BF)
        dgk_ref[...] = dgk
        dkv_scr[:, D_MODEL:] = dv_ref[...].astype(BF)
        dkv = dkv_scr[...]
        dw_ref[...] = _dot_tn(memn_ref[...], dkv)
        dmemn = _dot_nt(dkv, w_ref[...])
        mem_v = mem_ref[...]
        r = lax.rsqrt(jnp.mean(mem_v * mem_v, axis=-1, keepdims=True) + EPS)
        dgm_ref[...] = jnp.sum(dmemn * mem_v * r, axis=0, keepdims=True)

    return pl.pallas_call(
        body, name="mem_kv_bwd",
        out_shape=(jax.ShapeDtypeStruct((D_MODEL, 2 * D_MODEL), F32), jax.ShapeDtypeStruct((1, D_MODEL), F32),
                   jax.ShapeDtypeStruct((1, XHD), F32)),
        in_specs=[VMEM_SPEC] * 8, out_specs=(VMEM_SPEC,) * 3,
        scratch_shapes=[pltpu.VMEM((m_tok, 2 * D_MODEL), BF)],
        compiler_params=_cparams(),
    )(dkn, dv, kraw, mem, memn, g_mem, g_xk, w_xkv)


def _fox_bwd(fq, fk, proj, dmf, o32, lse_row, f_row, f_col):
    t_len = fq.shape[0]
    tb = min(ATT_BLOCK, t_len)
    n_b = t_len // tb
    v_col = 6 * GROUP_W // LANES

    def body(k_ref, v_ref, q_ref, do_ref, o_ref, lse_ref, fr_ref, fc_ref, dq_ref, dk_ref, dv_ref, df_ref, delta):
        j = pl.program_id(1)

        @pl.when(j == 0)
        def _():
            dq_ref[...] = jnp.zeros_like(dq_ref)
            dd = do_ref[...].astype(F32) * o_ref[...]
            hrow = lax.broadcasted_iota(jnp.int32, (8, LANES), 0)
            lane = lax.broadcasted_iota(jnp.int32, (8, LANES), 1)
            ind = ((lane // HEAD_DIM) == hrow).astype(BF)
            delta[...] = _dot_nt_exact(ind, dd)

        k2, v2 = k_ref[...], v_ref[...]
        krow = j * tb + lax.broadcasted_iota(jnp.int32, (tb, tb), 0)
        qcol0 = lax.broadcasted_iota(jnp.int32, (tb, tb), 1)
        dks, dvs, dfs = [], [], []
        for hh in range(2):
            sl = slice(hh * HEAD_DIM, (hh + 1) * HEAD_DIM)
            k, v = k2[:, sl], v2[:, sl]
            f_k = fc_ref[0][:, hh:hh + 1]

            def step(i, carry, sl=sl, k=k, v=v, f_k=f_k, hh=hh):
                dk, dv, df = carry
                rows = pl.ds(pl.multiple_of(i * tb, tb), tb)
                q = q_ref[rows, :][:, sl]
                do = do_ref[rows, :][:, sl]
                s_t = _dot_nt(k, q) + fr_ref[0, hh:hh + 1, rows] - f_k
                s_t = jnp.where(i * tb + qcol0 >= krow, s_t, NEG)
                p_t = jnp.exp(s_t - lse_ref[0, hh:hh + 1, rows])
                dv = dv + _dot(p_t, do)
                ds_t = p_t * (_dot_nt(v, do) - delta[hh:hh + 1, rows])
                dk = dk + _dot(ds_t, q)
                df = df - jnp.sum(ds_t, axis=-1, keepdims=True)
                dq_ref[rows, sl] += _dot_tn(ds_t, k)
                return dk, dv, df

            dk, dv, df = lax.fori_loop(j, n_b, step, (jnp.zeros((tb, HEAD_DIM), F32), jnp.zeros((tb, HEAD_DIM), F32),
                                                      jnp.zeros((tb, 1), F32)))
            dks.append(dk)
            dvs.append(dv)
            dfs.append(df)
        dk_ref[...] = jnp.concatenate(dks, axis=-1)
        dv_ref[...] = jnp.concatenate(dvs, axis=-1)
        df_ref[0] = jnp.concatenate(dfs, axis=-1)

    blk = lambda col0: pl.BlockSpec((tb, LANES), lambda hp, j: (j, col0 + hp))
    whole = pl.BlockSpec((t_len, LANES), lambda hp, j: (0, hp))
    rows2 = pl.BlockSpec((1, 2, t_len), lambda hp, j: (hp, 0, 0))
    cols2 = pl.BlockSpec((1, tb, 2), lambda hp, j: (hp, j, 0))
    return pl.pallas_call(
        body, name="fox_bwd", grid=(N_HEADS // 2, n_b),
        out_shape=(jax.ShapeDtypeStruct((t_len, GROUP_W), F32), jax.ShapeDtypeStruct((t_len, GROUP_W), F32),
                   jax.ShapeDtypeStruct((t_len, GROUP_W), F32), jax.ShapeDtypeStruct((N_HEADS // 2, t_len, 2), F32)),
        in_specs=[blk(0), blk(v_col), whole, whole, whole, rows2, rows2, cols2],
        out_specs=(whole, blk(0), blk(0), cols2),
        scratch_shapes=[pltpu.VMEM((8, t_len), F32)],
        compiler_params=_cparams(("arbitrary", "arbitrary")),
    )(fk, proj, fq, dmf, o32, lse_row, f_row, f_col)


def _retention_bwd(dmr, raw, proj, g_ret, rq, rk, states, tables):
    t_len = rq.shape[0]
    c = min(RET_BLOCK, t_len)
    n_b = t_len // c
    wdec, qdec, kdec, cdec = tables
    v_col, g_col = 2 * GROUP_W // LANES, 3 * GROUP_W // LANES

    def body(d_ref, raw_ref, rg_ref, g_ref, q_ref, k_ref, v_ref, st_ref, w_ref, qd_ref, kd_ref, cd_ref,
             dq_ref, dk_ref, dv_ref, drg_ref, dg_ref, gstate):
        @pl.when(pl.program_id(1) == 0)
        def _():
            gstate[...] = jnp.zeros_like(gstate)
            dg_ref[...] = jnp.zeros_like(dg_ref)

        d, raw_v, g = d_ref[...], raw_ref[...], g_ref[0]
        gate = rg_ref[...].astype(F32)
        xc = raw_v - _group_mean64(raw_v)
        r = lax.rsqrt(_group_mean64(xc * xc) + EPS)
        xh = xc * r
        sg = _sigmoid(gate)
        drg_ref[...] = d * (xh * g) * (sg * (1.0 + gate * (1.0 - sg)))
        dy = d * (gate * sg)
        dg_ref[0] += jnp.sum(dy * xh, axis=0, keepdims=True)
        dxh = dy * g
        do2 = r * (dxh - _group_mean64(dxh) - xh * _group_mean64(dxh * xh))
        q2, k2, v2 = q_ref[...], k_ref[...], v_ref[...]
        dqs, dks, dvs = [], [], []
        for hh in range(2):
            sl = slice(hh * HEAD_DIM, (hh + 1) * HEAD_DIM)
            q, k, v, do = q2[:, sl], k2[:, sl], v2[:, sl], do2[:, sl].astype(BF)
            w = w_ref[hh]
            a = _dot_nt(q, k) * w
            dm = _dot_nt(do, v) * w
            sp, gs = st_ref[0, 0, hh], gstate[hh]
            qd = q.astype(F32) * qd_ref[hh]
            kd = k.astype(F32) * kd_ref[hh]
            dqs.append(_dot(dm, k) + _dot_nt(do, sp) * qd_ref[hh])
            dks.append(_dot_tn(dm, q) + _dot_nt(v, gs) * kd_ref[hh])
            dvs.append(_dot_tn(a, do) + _dot(kd, gs))
            gstate[hh] = gs * cd_ref[hh] + _dot_tn(qd, do)
        dq_ref[...] = jnp.concatenate(dqs, axis=-1)
        dk_ref[...] = jnp.concatenate(dks, axis=-1)
        dv_ref[...] = jnp.concatenate(dvs, axis=-1)

    blk = lambda col0: pl.BlockSpec((c, LANES), lambda hp, i: (n_b - 1 - i, col0 + hp))
    tab = lambda a: pl.BlockSpec((2,) + a.shape[1:], lambda hp, i: (hp, 0, 0))
    gspec = pl.BlockSpec((1, 1, LANES), lambda hp, i: (hp, 0, 0))
    return pl.pallas_call(
        body, name="retention_bwd", grid=(N_HEADS // 2, n_b),
        out_shape=(jax.ShapeDtypeStruct((t_len, GROUP_W), F32),) * 4 + (jax.ShapeDtypeStruct((N_HEADS // 2, 1, LANES), F32),),
        in_specs=[blk(0), blk(0), blk(g_col), gspec, blk(0), blk(0), blk(v_col),
                  pl.BlockSpec((1, 1, 2, HEAD_DIM, HEAD_DIM), lambda hp, i: (hp, n_b - 1 - i, 0, 0, 0)),
                  tab(wdec), tab(qdec), tab(kdec), tab(cdec)],
        out_specs=(blk(0), blk(0), blk(0), blk(0), gspec),
        scratch_shapes=[pltpu.VMEM((2, HEAD_DIM, HEAD_DIM), F32)],
        compiler_params=_cparams(("arbitrary", "arbitrary")),
    )(dmr, raw, proj, g_ret, rq, rk, proj, states, wdec, qdec, kdec, cdec)


def _in_proj_bwd(x, g_mix, dh1, dq_r, dk_r, dv_r, drg, dq_f, dk_f, dv_f, df_col, proj, z, cos_t, sin_t, gq_t, gk_t, w_main, w_ff):
    t_len = x.shape[0]
    tm = min(ROW_TILE, t_len)
    n_t = t_len // tm

    def body(x_ref, g_ref, dh1_ref, dqr_ref, dkr_ref, dvr_ref, drg_ref, dqf_ref, dkf_ref, dvf_ref, df_ref, fq_ref, fk_ref, z_ref,
             cos_ref, sin_ref, gq_ref, gk_ref, wm_ref, wf_ref,
             dproj_ref, dz_ref, dx_ref, dg_ref, dgq_ref, dgk_ref, db_ref, carry, gq_acc, gk_acc):
        i = pl.program_id(0)

        @pl.when(i == 0)
        def _():
            carry[...] = jnp.zeros_like(carry)
            gq_acc[...] = jnp.zeros_like(gq_acc)
            gk_acc[...] = jnp.zeros_like(gk_acc)
            dg_ref[...] = jnp.zeros_like(dg_ref)
            db_ref[...] = jnp.zeros_like(db_ref)

        c, s = cos_ref[...], sin_ref[...]
        gq, gk = gq_ref[...], gk_ref[...]
        dgq = jnp.zeros((1, LANES), F32)
        dgk = jnp.zeros((1, LANES), F32)
        for sl in _chunks(GROUP_W):
            dy = dqr_ref[:, sl] * 0.125
            dproj_ref[:, sl] = (dy * c + _swap32(dy * s)).astype(BF)
            dy = dkr_ref[:, sl]
            dproj_ref[:, GROUP_W + sl.start:GROUP_W + sl.stop] = (dy * c + _swap32(dy * s)).astype(BF)
            dproj_ref[:, 2 * GROUP_W + sl.start:2 * GROUP_W + sl.stop] = dvr_ref[:, sl].astype(BF)
            dproj_ref[:, 3 * GROUP_W + sl.start:3 * GROUP_W + sl.stop] = drg_ref[:, sl].astype(BF)
            for src, dsrc, gain, off in ((fq_ref, dqf_ref, gq, 4), (fk_ref, dkf_ref, gk, 5)):
                xr = src[:, sl].astype(F32)
                r = lax.rsqrt(_group_mean64(xr * xr) + EPS)
                xh = xr * r
                dy = dsrc[:, sl] * (0.125 if off == 4 else 1.0)
                dgs = jnp.sum(dy * xh, axis=0, keepdims=True)
                if off == 4:
                    dgq = dgq + dgs
                else:
                    dgk = dgk + dgs
                dxh = dy * gain
                dproj_ref[:, off * GROUP_W + sl.start:off * GROUP_W + sl.stop] = \
                    (r * (dxh - xh * _group_mean64(dxh * xh))).astype(BF)
            dproj_ref[:, 6 * GROUP_W + sl.start:6 * GROUP_W + sl.stop] = dvf_ref[:, sl].astype(BF)
        gq_acc[...] += dgq
        gk_acc[...] += dgk
        row = lax.broadcasted_iota(jnp.int32, (tm, tm), 0)
        col = lax.broadcasted_iota(jnp.int32, (tm, tm), 1)
        dlf = _dot_exact((col >= row).astype(BF), df_ref[...]) + carry[0:1, :]
        carry[...] = jnp.broadcast_to(dlf[0:1, :], carry.shape)
        lane = lax.broadcasted_iota(jnp.int32, (tm, LANES), 1)
        dz = jnp.where(lane < N_HEADS, dlf / (1.0 + jnp.exp(z_ref[...])), 0.0)
        db_ref[...] += jnp.sum(dz, axis=0, keepdims=True)
        dz_bf = dz.astype(BF)
        dz_ref[...] = dz_bf
        dn1 = _dot_nt(dz_bf, wf_ref[...])
        for sec in range(MAIN_W // GROUP_W):
            sl = slice(sec * GROUP_W, (sec + 1) * GROUP_W)
            dn1 = dn1 + _dot_nt(dproj_ref[:, sl], wm_ref[:, sl])
        dx, dg = _rms_bwd(x_ref[...], g_ref[...], dn1)
        dx_ref[...] = dh1_ref[...] + dx
        dg_ref[...] += dg

        @pl.when(i == n_t - 1)
        def _():
            dgq_ref[...] = gq_acc[:, :HEAD_DIM] + gq_acc[:, HEAD_DIM:]
            dgk_ref[...] = gk_acc[:, :HEAD_DIM] + gk_acc[:, HEAD_DIM:]

    row_spec = lambda w, col=0: pl.BlockSpec((tm, w), lambda i: (n_t - 1 - i, col))
    full = lambda a: pl.BlockSpec(a.shape, lambda i: (0,) * a.ndim)
    acc = lambda r, c: pl.BlockSpec((r, c), lambda i: (0, 0))
    return pl.pallas_call(
        body, name="in_proj_bwd", grid=(n_t,),
        out_shape=(jax.ShapeDtypeStruct((t_len, MAIN_W), BF), jax.ShapeDtypeStruct((t_len, LANES), BF),
                   jax.ShapeDtypeStruct((t_len, D_MODEL), F32), jax.ShapeDtypeStruct((1, D_MODEL), F32),
                   jax.ShapeDtypeStruct((1, HEAD_DIM), F32), jax.ShapeDtypeStruct((1, HEAD_DIM), F32),
                   jax.ShapeDtypeStruct((1, LANES), F32)),
        in_specs=[row_spec(D_MODEL), full(g_mix), row_spec(D_MODEL)] + [row_spec(GROUP_W)] * 7
        + [row_spec(LANES), row_spec(GROUP_W, 4), row_spec(GROUP_W, 5), row_spec(LANES), row_spec(LANES), row_spec(LANES),
           full(gq_t), full(gk_t), full(w_main), full(w_ff)],
        out_specs=(row_spec(MAIN_W), row_spec(LANES), row_spec(D_MODEL), acc(1, D_MODEL), acc(1, HEAD_DIM), acc(1, HEAD_DIM),
                   acc(1, LANES)),
        scratch_shapes=[pltpu.VMEM((8, LANES), F32), pltpu.VMEM((1, LANES), F32), pltpu.VMEM((1, LANES), F32)],
        compiler_params=_cparams(("arbitrary",)),
    )(x, g_mix, dh1, dq_r, dk_r, dv_r, drg, dq_f, dk_f, dv_f, df_col, proj, proj, z, cos_t, sin_t, gq_t, gk_t, w_main, w_ff)


def _matmul_tn(a, b, name, bm=256, bk=512):
    t_len, m = a.shape
    n = b.shape[1]
    bm, bk = min(bm, m), min(bk, t_len)

    def body(a_ref, b_ref, o_ref):
        @pl.when(pl.program_id(1) == 0)
        def _():
            o_ref[...] = jnp.zeros_like(o_ref)

        o_ref[...] += _dot_tn(a_ref[...], b_ref[...])

    return pl.pallas_call(
        body, name=name, grid=(m // bm, t_len // bk),
        out_shape=jax.ShapeDtypeStruct((m, n), F32),
        in_specs=[pl.BlockSpec((bk, bm), lambda i, k: (k, i)), pl.BlockSpec((bk, n), lambda i, k: (k, 0))],
        out_specs=pl.BlockSpec((bm, n), lambda i, k: (i, 0)),
        compiler_params=_cparams(("arbitrary", "arbitrary")),
    )(a, b)


def _place():
    x, y, c = lax.axis_index("x"), lax.axis_index("y"), lax.axis_index("c")
    chips = [(1 - x, y), (x, 1 - y), (1 - x, 1 - y)]
    return x, y, c, chips


def _row_chunks(rows, limit):
    step = max(d for d in range(16, min(rows, limit) + 1, 16) if rows % d == 0)
    return [slice(i, i + step) for i in range(0, rows, step)]


ICI_CHUNK_ROWS = 128
D2D_CHUNK_ROWS = 64


def _all_gather_weights(shards):
    n_w = len(shards)

    def body(*refs):
        ins, outs = refs[:n_w], refs[n_w:2 * n_w]
        send_sems, recv_sems, local_sems = refs[2 * n_w:]
        x, y, c, chips = _place()
        me_chip = 2 * x + y
        sibling = (x, y, 1 - c)

        def copy(w, k, slot, half, to, rows=slice(None), src=None):
            dst = outs[w].at[slot, half, rows]
            return pltpu.make_async_remote_copy(src_ref=dst if src is None else src, dst_ref=dst,
                                                send_sem=send_sems.at[w, k], recv_sem=recv_sems.at[w, k],
                                                device_id=to, device_id_type=MESH)

        local = [pltpu.make_async_copy(ins[w], outs[w].at[me_chip], local_sems.at[w]) for w in range(n_w)]
        for cp in local:
            cp.start()
        for w in range(n_w):
            for j, chip in enumerate(chips):
                for rows in _row_chunks(ins[w].shape[1], ICI_CHUNK_ROWS):
                    copy(w, j, me_chip, c, (*chip, c), rows, src=ins[w].at[c, rows]).start()
        for w in range(n_w):
            for j, (px, py) in enumerate(chips):
                copy(w, j, 2 * px + py, c, (x, y, c)).wait_recv()
                for rows in _row_chunks(ins[w].shape[1], D2D_CHUNK_ROWS):
                    copy(w, 3 + j, 2 * px + py, c, sibling, rows).start()
        for w in range(n_w):
            for j, (px, py) in enumerate(chips):
                copy(w, 3 + j, 2 * px + py, 1 - c, (x, y, c)).wait_recv()
        for w in range(n_w):
            for j, (px, py) in enumerate(chips):
                copy(w, j, me_chip, c, (px, py, c), src=ins[w].at[c]).wait_send()
                copy(w, 3 + j, 2 * px + py, c, sibling).wait_send()
        for cp in local:
            cp.wait()

    return pl.pallas_call(
        body, name="all_gather_weights",
        out_shape=tuple(jax.ShapeDtypeStruct((4,) + s.shape, s.dtype) for s in shards),
        in_specs=[ANY] * n_w, out_specs=(ANY,) * n_w,
        scratch_shapes=[pltpu.SemaphoreType.DMA((n_w, 6)), pltpu.SemaphoreType.DMA((n_w, 6)), pltpu.SemaphoreType.DMA((n_w,))],
    )(*shards)


def _exchange_core_halves(grads):
    n_w = len(grads)

    def body(*refs):
        ins, mine, theirs = refs[:n_w], refs[n_w:2 * n_w], refs[2 * n_w:3 * n_w]
        send_sems, recv_sems, local_sems = refs[3 * n_w:]
        x, y, c, _ = _place()

        def remote(w, k=slice(None), rows=slice(None)):
            return pltpu.make_async_remote_copy(src_ref=ins[w].at[k, 1 - c, rows], dst_ref=theirs[w].at[k, rows],
                                                send_sem=send_sems.at[w], recv_sem=recv_sems.at[w], device_id=(x, y, 1 - c),
                                                device_id_type=MESH)

        def local(w, k=slice(None)):
            return pltpu.make_async_copy(ins[w].at[k, c], mine[w].at[k], local_sems.at[w])

        for w in range(n_w):
            for k in range(4):
                for rows in _row_chunks(ins[w].shape[2], D2D_CHUNK_ROWS):
                    remote(w, k, rows).start()
        for w in range(n_w):
            for k in range(4):
                local(w, k).start()
        for w in range(n_w):
            remote(w).wait()
        for w in range(n_w):
            local(w).wait()

    half =tuple(jax.ShapeDtypeStruct((4,) + g.shape[2:], g.dtype) for g in grads)
    return pl.pallas_call(
        body, name="exchange_core_halves", out_shape=half + half,
        in_specs=[ANY] * n_w, out_specs=(ANY,) * (2 * n_w),
        scratch_shapes=[pltpu.SemaphoreType.DMA((n_w,)), pltpu.SemaphoreType.DMA((n_w,)), pltpu.SemaphoreType.DMA((n_w,))],
    )(*grads)


def _add_pairs(a, b, name):
    _, r, c = a.shape
    rb = 32 if r % 32 == 0 else r

    def body(a_ref, b_ref, o_ref, ob_ref):
        s = a_ref[...] + b_ref[...]
        o_ref[...] = s
        ob_ref[...] = s.astype(BF)

    spec = pl.BlockSpec((4, rb, c), lambda i: (0, i, 0))
    return pl.pallas_call(
        body, name=name, grid=(r // rb,),
        out_shape=(jax.ShapeDtypeStruct(a.shape, F32), jax.ShapeDtypeStruct(a.shape, BF)),
        in_specs=[spec, spec], out_specs=(spec, spec), compiler_params=_cparams(("arbitrary",)),
    )(a, b)


def _scatter_to_chips(sums_f32, sums_bf16):
    n_w = len(sums_f32)

    def body(*refs):
        f32s, bfs, own, got = refs[:n_w], refs[n_w:2 * n_w], refs[2 * n_w:3 * n_w], refs[3 * n_w:4 * n_w]
        send_sems, recv_sems, local_sems = refs[4 * n_w:]
        x, y, c, chips = _place()

        def remote(w, j, px, py, rows=slice(None)):
            return pltpu.make_async_remote_copy(src_ref=bfs[w].at[2 * px + py, rows], dst_ref=got[w].at[j, rows],
                                                send_sem=send_sems.at[w, j], recv_sem=recv_sems.at[w, j], device_id=(px, py, c),
                                                device_id_type=MESH)

        local = [pltpu.make_async_copy(f32s[w].at[2 * x + y], own[w], local_sems.at[w]) for w in range(n_w)]
        for cp in local:
            cp.start()
        for w in range(n_w):
            for j, (px, py) in enumerate(chips):
                for rows in _row_chunks(bfs[w].shape[1], ICI_CHUNK_ROWS):
                    remote(w, j, px, py, rows).start()
        for w in range(n_w):
            for j, (px, py) in enumerate(chips):
                remote(w, j, px, py).wait()
        for cp in local:
            cp.wait()

    return pl.pallas_call(
        body, name="scatter_to_chips",
        out_shape=tuple(jax.ShapeDtypeStruct(s.shape[1:], F32) for s in sums_f32)
        + tuple(jax.ShapeDtypeStruct((3,) + s.shape[1:], BF) for s in sums_bf16),
        in_specs=[ANY] * (2 * n_w), out_specs=(ANY,) * (2 * n_w),
        scratch_shapes=[pltpu.SemaphoreType.DMA((n_w, 3)), pltpu.SemaphoreType.DMA((n_w, 3)), pltpu.SemaphoreType.DMA((n_w,))],
    )(*sums_f32, *sums_bf16)


def _add_received(own, got, name):
    r, c = own.shape
    rb = 32 if r % 32 == 0 else r

    def body(o_ref, g_ref, out_ref):
        out_ref[...] = ((o_ref[...] + g_ref[0].astype(F32)) + g_ref[1].astype(F32)) + g_ref[2].astype(F32)

    return pl.pallas_call(
        body, name=name, grid=(r // rb,), out_shape=jax.ShapeDtypeStruct((r, c), F32),
        in_specs=[pl.BlockSpec((rb, c), lambda i: (i, 0)), pl.BlockSpec((3, rb, c), lambda i: (0, i, 0))],
        out_specs=pl.BlockSpec((rb, c), lambda i: (i, 0)), compiler_params=_cparams(("arbitrary",)),
    )(own, got)


def _share_with_sibling(halves):
    n_w = len(halves)

    def body(*refs):
        ins, outs = refs[:n_w], refs[n_w:2 * n_w]
        send_sems, recv_sems, local_sems = refs[2 * n_w:]
        x, y, c, _ = _place()

        def remote(w, rows=slice(None)):
            return pltpu.make_async_remote_copy(src_ref=ins[w].at[rows], dst_ref=outs[w].at[c, rows], send_sem=send_sems.at[w],
                                                recv_sem=recv_sems.at[w], device_id=(x, y, 1 - c), device_id_type=MESH)

        local = [pltpu.make_async_copy(ins[w], outs[w].at[c], local_sems.at[w]) for w in range(n_w)]
        for w in range(n_w):
            for rows in _row_chunks(ins[w].shape[0], D2D_CHUNK_ROWS):
                remote(w, rows).start()
        for cp in local:
            cp.start()
        for w in range(n_w):
            remote(w).wait()
        for cp in local:
            cp.wait()

    return pl.pallas_call(
        body, name="share_with_sibling",
        out_shape=tuple(jax.ShapeDtypeStruct((2,) + h.shape, h.dtype) for h in halves),
        in_specs=[ANY] * n_w, out_specs=(ANY,) * n_w,
        scratch_shapes=[pltpu.SemaphoreType.DMA((n_w,)), pltpu.SemaphoreType.DMA((n_w,)), pltpu.SemaphoreType.DMA((n_w,))],
    )(*halves)


def _all_reduce_small(pack):
    r, c = pack.shape

    def body(p_ref, out_ref, slots, send_sems, recv_sems):
        x, y, cc, _ = _place()
        me = 4 * x + 2 * y + cc
        slots[me] = p_ref[...]
        copies = []
        for k in range(1, 8):
            dx, dy, dc = (k >> 2) & 1, (k >> 1) & 1, k & 1
            to = (1 - x if dx else x, 1 - y if dy else y, 1 - cc if dc else cc)
            cp = pltpu.make_async_remote_copy(src_ref=p_ref, dst_ref=slots.at[me], send_sem=send_sems.at[k - 1],
                                              recv_sem=recv_sems.at[k - 1], device_id=to, device_id_type=MESH)
            cp.start()
            copies.append(cp)
        for cp in copies:
            cp.wait()
        total = slots[0]
        for d in range(1, 8):
            total = total + slots[d]
        out_ref[...] = total

    return pl.pallas_call(
        body, name="all_reduce_small", out_shape=jax.ShapeDtypeStruct((r, c), F32),
        in_specs=[VMEM_SPEC], out_specs=VMEM_SPEC,
        scratch_shapes=[pltpu.VMEM((8, r, c), F32), pltpu.SemaphoreType.DMA((7,)), pltpu.SemaphoreType.DMA((7,))],
    )(pack)


def _adamw(w, g, m, v, name):
    r, c = w.shape
    rb = 64 if r % 64 == 0 else r
    c1 = 1.0 - ADAM_B1 ** ADAM_STEP
    c2 = 1.0 - ADAM_B2 ** ADAM_STEP

    def body(w_ref, g_ref, m_ref, v_ref, d_ref, nm_ref, nv_ref):
        gv = g_ref[...]
        nm = ADAM_B1 * m_ref[...] + (1.0 - ADAM_B1) * gv
        nv = ADAM_B2 * v_ref[...] + (1.0 - ADAM_B2) * (gv * gv)
        nm_ref[...] = nm
        nv_ref[...] = nv
        d_ref[...] = -ADAM_LR * ((nm / c1) / (jnp.sqrt(nv / c2) + ADAM_EPS) + ADAM_WD * w_ref[...])

    spec = pl.BlockSpec((rb, c), lambda i: (i, 0))
    return pl.pallas_call(
        body, name=name, grid=(r // rb,), out_shape=(jax.ShapeDtypeStruct((r, c), F32),) * 3,
        in_specs=[spec] * 4, out_specs=(spec,) * 3, compiler_params=_cparams(("arbitrary",)),
    )(w, g, m, v)


def _rope_tables(t_len):
    inv_freq = ROPE_BASE ** (-jnp.arange(0, HEAD_DIM, 2, dtype=F32) / HEAD_DIM)
    ang = jnp.arange(t_len, dtype=F32)[:, None] * inv_freq[None, :]
    cos, sin = jnp.cos(ang), jnp.sin(ang)
    cos_t = jnp.concatenate([cos, cos, cos, cos], axis=-1)
    sin_t = jnp.concatenate([-sin, sin, -sin, sin], axis=-1)
    return cos_t, sin_t


def _cols_to_shards(dw):
    r, n = dw.shape
    return jnp.transpose(dw.reshape(2, r // 2, 4, n // 4), (2, 0, 1, 3))


def _rows_to_shards(dw):
    r, n = dw.shape
    return dw.reshape(4, 2, r // 8, n)


def _pad_row(a, width=D_MODEL):
    a = a.reshape(1, -1)
    return jnp.pad(a, ((0, 0), (0, width - a.shape[1])))


def kernel(x, mem, g_mix, w_in, b_forget, g_ret_out, g_fox_q, g_fox_k, w_out, g_xattn, w_xq, w_xkv, g_mem, g_xq, g_xk, w_xo, g_ffn, w_gate, w_up, w_down, loss_target, m_g_mix, m_w_in, m_b_forget, m_g_ret_out, m_g_fox_q, m_g_fox_k, m_w_out, m_g_xattn, m_w_xq, m_w_xkv, m_g_mem, m_g_xq, m_g_xk, m_w_xo, m_g_ffn, m_w_gate, m_w_up, m_w_down, v_g_mix, v_w_in, v_b_forget, v_g_ret_out, v_g_fox_q, v_g_fox_k, v_w_out, v_g_xattn, v_w_xq, v_w_xkv, v_g_mem, v_g_xq, v_g_xk, v_w_xo, v_g_ffn, v_w_gate, v_w_up, v_w_down):
    big = {"w_in": (w_in, m_w_in, v_w_in), "w_out": (w_out, m_w_out, v_w_out), "w_xq": (w_xq, m_w_xq, v_w_xq),
           "w_xkv": (w_xkv, m_w_xkv, v_w_xkv), "w_xo": (w_xo, m_w_xo, v_w_xo), "w_gate": (w_gate, m_w_gate, v_w_gate),
           "w_up": (w_up, m_w_up, v_w_up), "w_down": (w_down, m_w_down, v_w_down)}
    big_names = list(big)
    col_sharded = ("w_in", "w_xkv", "w_gate", "w_up")

    shards = []
    for n in big_names:
        w = big[n][0][0]
        shards.append(w.astype(BF).reshape(2, w.shape[0] // 2, w.shape[1]))
    gathered = _all_gather_weights(shards)
    full = {}
    for n, g in zip(big_names, gathered):
        _, _, rh, cs = g.shape
        g = g.reshape(4, 2 * rh, cs)
        full[n] = jnp.transpose(g, (1, 0, 2)).reshape(2 * rh, 4 * cs) if n in col_sharded else g.reshape(8 * rh, cs)
    small_w = {"g_mix": g_mix, "b_forget": b_forget, "g_ret_out": g_ret_out, "g_fox_q": g_fox_q, "g_fox_k": g_fox_k,
               "g_xattn": g_xattn, "g_mem": g_mem, "g_xq": g_xq, "g_xk": g_xk, "g_ffn": g_ffn}
    m_small = {"g_mix": m_g_mix, "b_forget": m_b_forget, "g_ret_out": m_g_ret_out, "g_fox_q": m_g_fox_q, "g_fox_k": m_g_fox_k,
               "g_xattn": m_g_xattn, "g_mem": m_g_mem, "g_xq": m_g_xq, "g_xk": m_g_xk, "g_ffn": m_g_ffn}
    v_small = {"g_mix": v_g_mix, "b_forget": v_b_forget, "g_ret_out": v_g_ret_out, "g_fox_q": v_g_fox_q, "g_fox_k": v_g_fox_k,
               "g_xattn": v_g_xattn, "g_mem": v_g_mem, "g_xq": v_g_xq, "g_xk": v_g_xk, "g_ffn": v_g_ffn}
    loss_part, grad_x, dw, small_g = _local_step(x[0], mem[0], loss_target[0], full, small_w)
    return _reduce_and_update(big, big_names, col_sharded, dw, small_w, small_g, loss_part, grad_x, m_small, v_small)


def _local_step(xs, mems, tgt, full, small_w):
    g_mix, b_forget, g_ret_out, g_fox_q, g_fox_k = (small_w[n] for n in ("g_mix", "b_forget", "g_ret_out", "g_fox_q", "g_fox_k"))
    g_xattn, g_mem, g_xq, g_xk, g_ffn = (small_w[n] for n in ("g_xattn", "g_mem", "g_xq", "g_xk", "g_ffn"))
    w_main = full["w_in"][:, :MAIN_W]
    w_ff = jnp.pad(full["w_in"][:, MAIN_W:], ((0, 0), (0, LANES - (IN_W - MAIN_W))))
    t_len = xs.shape[0]
    cos_t, sin_t = _rope_tables(t_len)
    tables = _decay_tables(min(RET_BLOCK, t_len))
    gq_t = jnp.concatenate([g_fox_q, g_fox_q], axis=-1)
    gk_t = jnp.concatenate([g_fox_k, g_fox_k], axis=-1)
    b_pad = _pad_row(b_forget, LANES)
    g_ret = g_ret_out.reshape(N_HEADS // 2, 1, LANES)

    memn, kraw, kn, vmem = _mem_kv_fwd(mems, g_mem, full["w_xkv"], g_xk)
    n1, proj, rq, rk, fq, fk, z, fcum = _in_proj_fwd(xs, g_mix, w_main, w_ff, b_pad, cos_t, sin_t, gq_t, gk_t)
    f8 = fcum[:, :N_HEADS]
    f_col = jnp.transpose(f8.reshape(t_len, N_HEADS // 2, 2), (1, 0, 2))
    f_row = jnp.transpose(f8).reshape(N_HEADS // 2, 2, t_len)
    raw, mix_r, states = _retention_fwd(rq, rk, proj, g_ret, tables)
    mix_f, o32, lse = _fox_fwd(fq, fk, proj, f_col, f_row)
    h1, hn2, qx, o_x, h2 = _attn_out_xattn_fwd(xs, mix_r, mix_f, full["w_out"], g_xattn, full["w_xq"], g_xq, kn, vmem, full["w_xo"])
    hn3, gate, up, act, dh3, loss_part = _ffn_loss_fwd(h2, g_ffn, full["w_gate"], full["w_up"], full["w_down"], tgt)

    dgate, dup, dh2, dg_ffn = _ffn_bwd(dh3, gate, up, h2, g_ffn, full["w_gate"], full["w_up"], full["w_down"])
    dqx, dh1, dmr, dmf, dkn, dvm, dg_xattn, dg_xq = _attn_out_xattn_bwd(dh2, h1, qx, kn, vmem, full["w_xo"], full["w_xq"],
                                                                      full["w_out"], g_xattn, g_xq)
    dw_xkv, dg_mem, dg_xk = _mem_kv_bwd(dkn, dvm, kraw, mems, memn, g_mem, g_xk, full["w_xkv"])
    lse_row = jnp.transpose(lse, (0, 2, 1))
    dq_f, dk_f, dv_f, df = _fox_bwd(fq, fk, proj, dmf, o32, lse_row, f_row, f_col)
    dq_r, dk_r, dv_r, drg, dg_ret = _retention_bwd(dmr, raw, proj, g_ret, rq, rk, states, tables)
    df_col = jnp.pad(jnp.transpose(df, (1, 0, 2)).reshape(t_len, N_HEADS), ((0, 0), (0, LANES - N_HEADS)))
    dproj, dz, grad_x, dg_mix, dg_fq, dg_fk, db = _in_proj_bwd(xs, g_mix, dh1, dq_r, dk_r, dv_r, drg, dq_f, dk_f, dv_f, df_col,
                                                              proj, z, cos_t, sin_t, gq_t, gk_t, w_main, w_ff)

    dw = {
        "w_in": jnp.concatenate([_matmul_tn(n1, dproj, "dw_in_main"), _matmul_tn(n1, dz, "dw_in_ff")[:, :IN_W - MAIN_W]], axis=1),
        "w_out": jnp.concatenate([_matmul_tn(mix_r, dh1, "dw_out_ret"), _matmul_tn(mix_f, dh1, "dw_out_fox")], axis=0),
        "w_xq": _matmul_tn(hn2, dqx, "dw_xq"),
        "w_xkv": dw_xkv,
        "w_xo": _matmul_tn(o_x, dh2, "dw_xo"),
        "w_gate": _matmul_tn(hn3, dgate, "dw_gate"),
        "w_up": _matmul_tn(hn3, dup, "dw_up"),
        "w_down": _matmul_tn(act, dh3, "dw_down"),
    }
    small_g = {"g_mix": dg_mix, "b_forget": db[:, :N_HEADS], "g_ret_out": dg_ret, "g_fox_q": dg_fq, "g_fox_k": dg_fk,
               "g_xattn": dg_xattn, "g_mem": dg_mem, "g_xq": dg_xq, "g_xk": dg_xk, "g_ffn": dg_ffn}
    return loss_part, grad_x, dw, small_g


def _reduce_and_update(big, big_names, col_sharded, dw, small_w, small_g, loss_part, grad_x, m_small, v_small):
    parts =[_cols_to_shards(dw[n]) if n in col_sharded else _rows_to_shards(dw[n]) for n in big_names]
    ex = _exchange_core_halves(parts)
    mine, theirs = ex[:len(parts)], ex[len(parts):]
    sums = [_add_pairs(a, b, f"core_sum_{n}") for n, a, b in zip(big_names, mine, theirs)]
    sc = _scatter_to_chips([s[0] for s in sums], [s[1] for s in sums])
    own, got = sc[:len(parts)], sc[len(parts):]
    finals = [_add_received(o, g, f"chip_sum_{n}") for n, o, g in zip(big_names, own, got)]
    shared = _share_with_sibling(finals)
    grads, deltas, new_m, new_v = {}, {}, {}, {}
    for n, s in zip(big_names, shared):
        w, m, v = big[n]
        g = s.reshape(w.shape[1:])
        d, nm, nv = _adamw(w[0], g, m[0], v[0], f"adamw_{n}")
        grads[n], deltas[n], new_m[n], new_v[n] = g[None], d[None], nm[None], nv[None]

    small_names = list(small_w)
    pad_rows = SMALL_ROWS - len(small_names) - 1
    stack = lambda d: jnp.concatenate([_pad_row(d[n]) for n in small_names] + [jnp.zeros((pad_rows + 1, D_MODEL), F32)], axis=0)
    g_pack = jnp.concatenate([_pad_row(small_g[n]) for n in small_names] + [_pad_row(loss_part[0:1, 0:1])]
                             + [jnp.zeros((pad_rows, D_MODEL), F32)], axis=0)
    g_tot = _all_reduce_small(g_pack)
    d_s, m_s, v_s = _adamw(stack(small_w), g_tot, stack(m_small), stack(v_small), "adamw_small")
    for i, n in enumerate(small_names):
        shape = small_w[n].shape
        size = int(np.prod(shape))
        grads[n] = g_tot[i, :size].reshape(shape)
        deltas[n], new_m[n], new_v[n] = d_s[i, :size].reshape(shape), m_s[i, :size].reshape(shape), v_s[i, :size].reshape(shape)
    loss = g_tot[len(small_names), 0]

    order = ["g_mix", "w_in", "b_forget", "g_ret_out", "g_fox_q", "g_fox_k", "w_out", "g_xattn", "w_xq", "w_xkv", "g_mem", "g_xq",
             "g_xk", "w_xo", "g_ffn", "w_gate", "w_up", "w_down"]
    return (loss, grad_x[None], *[grads[n] for n in order], *[deltas[n] for n in order], *[new_m[n] for n in order],
            *[new_v[n] for n in order])
```

```python
import functools

import numpy as np
import jax
import jax.numpy as jnp
from jax import lax
from jax.experimental import pallas as pl
from jax.experimental.pallas import tpu as pltpu

F32 = jnp.float32
BF = jnp.bfloat16

D_MODEL = 1024
HEAD_DIM = 64
N_HEADS = 8
GROUP_W = 512
N_XH = 4
XHD = 256
D_FF = 2816
MAIN_W = 3584
IN_W = 3592
ROPE_BASE = 10000.0
EPS = 1e-6
NEG = -1e30
LANES = 128
RET_BLOCK = 256
REF_CHUNK = 64
ROW_TILE = 256
ATT_BLOCK = 256
SMALL_ROWS = 16
VMEM_LIMIT = 56 * 1024 * 1024

ADAM_LR = 0.001
ADAM_B1 = 0.9
ADAM_B2 = 0.999
ADAM_EPS = 1e-08
ADAM_WD = 0.01
ADAM_STEP = 10

MESH = pl.DeviceIdType.MESH
ANY = pl.BlockSpec(memory_space=pl.ANY)
VMEM_SPEC = pl.BlockSpec(memory_space=pltpu.VMEM)


def _cparams(sem=None, vmem=VMEM_LIMIT):
    return pltpu.CompilerParams(dimension_semantics=sem, vmem_limit_bytes=vmem)


def _dot(a, b):
    return jnp.dot(a.astype(BF), b.astype(BF), preferred_element_type=F32)


def _dot_nt(a, b):
    return lax.dot_general(a.astype(BF), b.astype(BF), (((1,), (1,)), ((), ())), preferred_element_type=F32)


def _dot_tn(a, b):
    return lax.dot_general(a.astype(BF), b.astype(BF), (((0,), (0,)), ((), ())), preferred_element_type=F32)


def _split3(x):
    hi = x.astype(BF)
    r = x - hi.astype(F32)
    mid = r.astype(BF)
    lo = (r - mid.astype(F32)).astype(BF)
    return hi, mid, lo


def _dot_exact(ind, x):
    hi, mid, lo = _split3(x)
    return (jnp.dot(ind, lo, preferred_element_type=F32) + jnp.dot(ind, mid, preferred_element_type=F32)
            + jnp.dot(ind, hi, preferred_element_type=F32))


def _dot_nt_exact(ind, x):
    hi, mid, lo = _split3(x)
    dn = (((1,), (1,)), ((), ()))
    return (lax.dot_general(ind, lo, dn, preferred_element_type=F32) + lax.dot_general(ind, mid, dn, preferred_element_type=F32)
            + lax.dot_general(ind, hi, dn, preferred_element_type=F32))


def _sigmoid(x):
    return 1.0 / (1.0 + jnp.exp(-x))


def _rms_fwd(x, g):
    r = lax.rsqrt(jnp.mean(x * x, axis=-1, keepdims=True) + EPS)
    return x * r * g


def _rms_bwd(x, g, dy):
    r = lax.rsqrt(jnp.mean(x * x, axis=-1, keepdims=True) + EPS)
    xh = x * r
    dg = jnp.sum(dy * xh, axis=0, keepdims=True)
    dxh = dy * g
    dx = r * (dxh - xh * jnp.mean(dxh * xh, axis=-1, keepdims=True))
    return dx, dg


def _group_mean64(x):
    lane = lax.broadcasted_iota(jnp.int32, x.shape, 1)
    lo = lane < HEAD_DIM
    s_lo = jnp.sum(jnp.where(lo, x, 0.0), axis=-1, keepdims=True)
    s_hi = jnp.sum(jnp.where(lo, 0.0, x), axis=-1, keepdims=True)
    return jnp.where(lo, s_lo, s_hi) * (1.0 / HEAD_DIM)


def _swap32(x):
    lane = lax.broadcasted_iota(jnp.int32, x.shape, 1)
    first = (lane % HEAD_DIM) < (HEAD_DIM // 2)
    return jnp.where(first, pltpu.roll(x, LANES - HEAD_DIM // 2, axis=1), pltpu.roll(x, HEAD_DIM // 2, axis=1))


def _chunks(w):
    return [slice(j * LANES, (j + 1) * LANES) for j in range(w // LANES)]


def _mem_kv_fwd(mem, g_mem, w_xkv, g_xk):
    m_tok = mem.shape[0]

    def body(mem_ref, gm_ref, w_ref, gk_ref, memn_ref, kraw_ref, kn_ref, v_ref):
        mn = _rms_fwd(mem_ref[...], gm_ref[...]).astype(BF)
        memn_ref[...] = mn
        kv = jnp.dot(mn, w_ref[...], preferred_element_type=F32)
        k = kv[:, :D_MODEL]
        kraw_ref[...] = k
        v_ref[...] = kv[:, D_MODEL:].astype(BF)
        for h in range(N_XH):
            sl = slice(h * XHD, (h + 1) * XHD)
            kn_ref[:, sl] = _rms_fwd(k[:, sl], gk_ref[...]).astype(BF)

    return pl.pallas_call(
        body, name="mem_kv_fwd",
        out_shape=(jax.ShapeDtypeStruct((m_tok, D_MODEL), BF), jax.ShapeDtypeStruct((m_tok, D_MODEL), F32),
                   jax.ShapeDtypeStruct((m_tok, D_MODEL), BF), jax.ShapeDtypeStruct((m_tok, D_MODEL), BF)),
        in_specs=[VMEM_SPEC] * 4, out_specs=(VMEM_SPEC,) * 4, compiler_params=_cparams(),
    )(mem, g_mem, w_xkv, g_xk)


def _in_proj_fwd(x, g_mix, w_main, w_ff, b_pad, cos_t, sin_t, gq_t, gk_t):
    t_len = x.shape[0]
    tm = min(ROW_TILE, t_len)
    n_t = t_len // tm

    def body(x_ref, g_ref, wm_ref, wf_ref, b_ref, cos_ref, sin_ref, gq_ref, gk_ref,
             n1_ref, proj_ref, rq_ref, rk_ref, fq_ref, fk_ref, z_ref, fc_ref, carry):
        i = pl.program_id(0)

        @pl.when(i == 0)
        def _():
            carry[...] = jnp.zeros_like(carry)

        n1 = _rms_fwd(x_ref[...], g_ref[...]).astype(BF)
        n1_ref[...] = n1
        proj = jnp.dot(n1, wm_ref[...], preferred_element_type=F32)
        proj_ref[...] = proj.astype(BF)
        c, s = cos_ref[...], sin_ref[...]
        for j, sl in enumerate(_chunks(GROUP_W)):
            q = proj[:, sl]
            rq_ref[:, sl] = ((q * c + _swap32(q) * s) * 0.125).astype(BF)
            k = proj[:, GROUP_W + j * LANES:GROUP_W + (j + 1) * LANES]
            rk_ref[:, sl] = (k * c + _swap32(k) * s).astype(BF)
            fq = proj[:, 4 * GROUP_W + j * LANES:4 * GROUP_W + (j + 1) * LANES]
            fq_ref[:, sl] = (fq * lax.rsqrt(_group_mean64(fq * fq) + EPS) * gq_ref[...] * 0.125).astype(BF)
            fk = proj[:, 5 * GROUP_W + j * LANES:5 * GROUP_W + (j + 1) * LANES]
            fk_ref[:, sl] = (fk * lax.rsqrt(_group_mean64(fk * fk) + EPS) * gk_ref[...]).astype(BF)
        z = jnp.dot(n1, wf_ref[...], preferred_element_type=F32) + b_ref[...]
        z_ref[...] = z
        lane = lax.broadcasted_iota(jnp.int32, z.shape, 1)
        lf = jnp.where(lane < N_HEADS, jnp.minimum(z, 0.0) - jnp.log(1.0 + jnp.exp(-jnp.abs(z))), 0.0)
        row = lax.broadcasted_iota(jnp.int32, (tm, tm), 0)
        col = lax.broadcasted_iota(jnp.int32, (tm, tm), 1)
        tri = (row >= col).astype(BF)
        fc = _dot_exact(tri, lf) + carry[0:1, :]
        fc_ref[...] = fc
        carry[...] = jnp.broadcast_to(fc[tm - 1:tm, :], carry.shape)

    row_spec = lambda w: pl.BlockSpec((tm, w), lambda i: (i, 0))
    full = lambda a: pl.BlockSpec(a.shape, lambda i: (0,) * a.ndim)
    return pl.pallas_call(
        body, name="in_proj_fwd", grid=(n_t,),
        out_shape=(jax.ShapeDtypeStruct((t_len, D_MODEL), BF), jax.ShapeDtypeStruct((t_len, MAIN_W), BF),
                   jax.ShapeDtypeStruct((t_len, GROUP_W), BF), jax.ShapeDtypeStruct((t_len, GROUP_W), BF),
                   jax.ShapeDtypeStruct((t_len, GROUP_W), BF), jax.ShapeDtypeStruct((t_len, GROUP_W), BF),
                   jax.ShapeDtypeStruct((t_len, LANES), F32), jax.ShapeDtypeStruct((t_len, LANES), F32)),
        in_specs=[row_spec(D_MODEL), full(g_mix), full(w_main), full(w_ff), full(b_pad), row_spec(LANES), row_spec(LANES),
                  full(gq_t), full(gk_t)],
        out_specs=(row_spec(D_MODEL), row_spec(MAIN_W), row_spec(GROUP_W), row_spec(GROUP_W), row_spec(GROUP_W),
                   row_spec(GROUP_W), row_spec(LANES), row_spec(LANES)),
        scratch_shapes=[pltpu.VMEM((8, LANES), F32)],
        compiler_params=_cparams(("arbitrary",)),
    )(x, g_mix, w_main, w_ff, b_pad, cos_t, sin_t, gq_t, gk_t)


def _decay_tables(c):
    h = np.arange(N_HEADS, dtype=np.float64)
    lg = np.log(1.0 - 2.0 ** (-5.0 - h)).astype(np.float32).astype(np.float64)
    t = np.arange(c)
    same_or_earlier = (t[None, :] // REF_CHUNK) <= (t[:, None] // REF_CHUNK)
    w = np.where(same_or_earlier[None], np.exp(lg[:, None, None] * np.abs(t[:, None] - t[None, :])[None]), 0.0)
    qd = np.exp(lg[:, None] * (t[None, :] + 1.0))
    kd = np.exp(lg[:, None] * (c - 1.0 - t[None, :]))
    cd = np.exp(lg * c)
    ones = np.ones((1, 1, HEAD_DIM))
    return (jnp.asarray(w, F32), jnp.asarray(qd[:, :, None] * ones, F32), jnp.asarray(kd[:, :, None] * ones, F32),
            jnp.asarray(cd[:, None, None] * np.ones((1, HEAD_DIM, HEAD_DIM)), F32))


def _retention_fwd(rq, rk, proj, g_ret, tables):
    t_len = rq.shape[0]
    c = min(RET_BLOCK, t_len)
    n_b = t_len // c
    wdec, qdec, kdec, cdec = tables
    v_col, g_col = 2 * GROUP_W // LANES, 3 * GROUP_W // LANES

    def body(q_ref, k_ref, v_ref, rg_ref, g_ref, w_ref, qd_ref, kd_ref, cd_ref, raw_ref, mix_ref, st_ref, state):
        i = pl.program_id(1)

        @pl.when(i == 0)
        def _():
            state[...] = jnp.zeros_like(state)

        q2, k2, v2 = q_ref[...], k_ref[...], v_ref[...]
        outs = []
        for hh in range(2):
            sl = slice(hh * HEAD_DIM, (hh + 1) * HEAD_DIM)
            q, k, v = q2[:, sl], k2[:, sl], v2[:, sl]
            sp = state[hh]
            st_ref[0, 0, hh] = sp
            a = _dot_nt(q, k) * w_ref[hh]
            o = _dot(a, v) + _dot(q.astype(F32) * qd_ref[hh], sp)
            state[hh] = sp * cd_ref[hh] + _dot_tn(k.astype(F32) * kd_ref[hh], v)
            outs.append(o)
        o2 = jnp.concatenate(outs, axis=-1)
        raw_ref[...] = o2
        xc = o2 - _group_mean64(o2)
        xh = xc * lax.rsqrt(_group_mean64(xc * xc) + EPS)
        gate = rg_ref[...].astype(F32)
        mix_ref[...] = (gate * _sigmoid(gate) * (xh * g_ref[0])).astype(BF)

    blk = lambda col0: pl.BlockSpec((c, LANES), lambda hp, i: (i, col0 + hp))
    tab = lambda a: pl.BlockSpec((2,) + a.shape[1:], lambda hp, i: (hp, 0, 0))
    return pl.pallas_call(
        body, name="retention_fwd", grid=(N_HEADS // 2, n_b),
        out_shape=(jax.ShapeDtypeStruct((t_len, GROUP_W), F32), jax.ShapeDtypeStruct((t_len, GROUP_W), BF),
                   jax.ShapeDtypeStruct((N_HEADS // 2, n_b, 2, HEAD_DIM, HEAD_DIM), F32)),
        in_specs=[blk(0), blk(0), blk(v_col), blk(g_col), pl.BlockSpec((1, 1, LANES), lambda hp, i: (hp, 0, 0)),
                  tab(wdec), tab(qdec), tab(kdec), tab(cdec)],
        out_specs=(blk(0), blk(0), pl.BlockSpec((1, 1, 2, HEAD_DIM, HEAD_DIM), lambda hp, i: (hp, i, 0, 0, 0))),
        scratch_shapes=[pltpu.VMEM((2, HEAD_DIM, HEAD_DIM), F32)],
        compiler_params=_cparams(("arbitrary", "arbitrary")),
    )(rq, rk, proj, proj, g_ret, wdec, qdec, kdec, cdec)


def _fox_fwd(fq, fk, proj, f_col, f_row):
    t_len = fq.shape[0]
    tq = min(ATT_BLOCK, t_len)
    n_q = t_len // tq
    v_col = 6 * GROUP_W // LANES

    def body(q_ref, k_ref, v_ref, fq_ref, fk_ref, o_ref, o32_ref, lse_ref):
        i = pl.program_id(1)
        q2 = q_ref[...]
        fq2 = fq_ref[0]
        row = i * tq + lax.broadcasted_iota(jnp.int32, (tq, tq), 0)
        col0 = lax.broadcasted_iota(jnp.int32, (tq, tq), 1)
        outs, lses = [], []
        for hh in range(2):
            sl = slice(hh * HEAD_DIM, (hh + 1) * HEAD_DIM)
            q = q2[:, sl]
            f_q = fq2[:, hh:hh + 1]

            def step(j, carry, sl=sl, q=q, f_q=f_q, hh=hh):
                m, l, acc = carry
                rows = pl.ds(pl.multiple_of(j * tq, tq), tq)
                k = k_ref[rows, :][:, sl]
                v = v_ref[rows, :][:, sl]
                s = _dot_nt(q, k) + f_q - fk_ref[0, hh:hh + 1, rows]
                s = jnp.where(row >= j * tq + col0, s, NEG)
                m_new = jnp.maximum(m, jnp.max(s, axis=-1, keepdims=True))
                alpha = jnp.exp(m - m_new)
                p = jnp.exp(s - m_new)
                l = l * alpha + jnp.sum(p, axis=-1, keepdims=True)
                acc = acc * alpha + _dot(p, v)
                return m_new, l, acc

            m, l, acc = lax.fori_loop(0, i + 1, step, (jnp.full((tq, 1), NEG, F32), jnp.zeros((tq, 1), F32),
                                                       jnp.zeros((tq, HEAD_DIM), F32)))
            outs.append(acc / l)
            lses.append(m + jnp.log(l))
        o2 = jnp.concatenate(outs, axis=-1)
        o32_ref[...] = o2
        o_ref[...] = o2.astype(BF)
        lse_ref[0] = jnp.concatenate(lses, axis=-1)

    return pl.pallas_call(
        body, name="fox_fwd", grid=(N_HEADS // 2, n_q),
        out_shape=(jax.ShapeDtypeStruct((t_len, GROUP_W), BF), jax.ShapeDtypeStruct((t_len, GROUP_W), F32),
                   jax.ShapeDtypeStruct((N_HEADS // 2, t_len, 2), F32)),
        in_specs=[pl.BlockSpec((tq, LANES), lambda hp, i: (i, hp)),
                  pl.BlockSpec((t_len, LANES), lambda hp, i: (0, hp)),
                  pl.BlockSpec((t_len, LANES), lambda hp, i: (0, v_col + hp)),
                  pl.BlockSpec((1, tq, 2), lambda hp, i: (hp, i, 0)),
                  pl.BlockSpec((1, 2, t_len), lambda hp, i: (hp, 0, 0))],
        out_specs=(pl.BlockSpec((tq, LANES), lambda hp, i: (i, hp)), pl.BlockSpec((tq, LANES), lambda hp, i: (i, hp)),
                   pl.BlockSpec((1, tq, 2), lambda hp, i: (hp, i, 0))),
        compiler_params=_cparams(("arbitrary", "arbitrary")),
    )(fq, fk, proj, f_col, f_row)


def _softmax_rows(s):
    p = jnp.exp(s - jnp.max(s, axis=-1, keepdims=True))
    return p / jnp.sum(p, axis=-1, keepdims=True)


def _attn_out_xattn_fwd(x, mix_r, mix_f, w_out, g_xattn, w_xq, g_xq, kn, v, w_xo):
    t_len = x.shape[0]
    tm = min(ROW_TILE, t_len)

    def body(x_ref, mr_ref, mf_ref, wo_ref, g_ref, wq_ref, gq_ref, kn_ref, v_ref, wxo_ref,
             h1_ref, hn_ref, qx_ref, o_ref, h2_ref):
        h1 = x_ref[...] + jnp.dot(mr_ref[...], wo_ref[:GROUP_W, :], preferred_element_type=F32) \
            + jnp.dot(mf_ref[...], wo_ref[GROUP_W:, :], preferred_element_type=F32)
        h1_ref[...] = h1
        hn = _rms_fwd(h1, g_ref[...]).astype(BF)
        hn_ref[...] = hn
        qx = jnp.dot(hn, wq_ref[...], preferred_element_type=F32).astype(BF)
        qx_ref[...] = qx
        for h in range(N_XH):
            sl = slice(h * XHD, (h + 1) * XHD)
            qn = _rms_fwd(qx[:, sl].astype(F32), gq_ref[...])
            p = _softmax_rows(_dot_nt(qn, kn_ref[:, sl]) * (XHD ** -0.5))
            o_ref[:, sl] = _dot(p, v_ref[:, sl]).astype(BF)
        h2_ref[...] = h1 + jnp.dot(o_ref[...], wxo_ref[...], preferred_element_type=F32)

    row_spec = lambda w: pl.BlockSpec((tm, w), lambda i: (i, 0))
    full = lambda a: pl.BlockSpec(a.shape, lambda i: (0,) * a.ndim)
    return pl.pallas_call(
        body, name="attn_out_xattn_fwd", grid=(t_len // tm,),
        out_shape=(jax.ShapeDtypeStruct((t_len, D_MODEL), F32), jax.ShapeDtypeStruct((t_len, D_MODEL), BF),
                   jax.ShapeDtypeStruct((t_len, D_MODEL), BF), jax.ShapeDtypeStruct((t_len, D_MODEL), BF),
                   jax.ShapeDtypeStruct((t_len, D_MODEL), F32)),
        in_specs=[row_spec(D_MODEL), row_spec(GROUP_W), row_spec(GROUP_W), full(w_out), full(g_xattn), full(w_xq), full(g_xq),
                  full(kn), full(v), full(w_xo)],
        out_specs=(row_spec(D_MODEL),) * 5,
        compiler_params=_cparams(("arbitrary",)),
    )(x, mix_r, mix_f, w_out, g_xattn, w_xq, g_xq, kn, v, w_xo)


def _ffn_loss_fwd(h2, g_ffn, w_gate, w_up, w_down, target):
    t_len = h2.shape[0]
    tm = min(ROW_TILE, t_len)

    def body(h2_ref, g_ref, wg_ref, wu_ref, wd_ref, tgt_ref, hn_ref, gate_ref, up_ref, act_ref, dh3_ref, loss_ref):
        @pl.when(pl.program_id(0) == 0)
        def _():
            loss_ref[...] = jnp.zeros_like(loss_ref)

        h2v = h2_ref[...]
        hn = _rms_fwd(h2v, g_ref[...]).astype(BF)
        hn_ref[...] = hn
        gate = jnp.dot(hn, wg_ref[...], preferred_element_type=F32)
        up = jnp.dot(hn, wu_ref[...], preferred_element_type=F32)
        gate_ref[...] = gate.astype(BF)
        up_ref[...] = up.astype(BF)
        act = (gate * _sigmoid(gate) * up).astype(BF)
        act_ref[...] = act
        diff = h2v + jnp.dot(act, wd_ref[...], preferred_element_type=F32) - tgt_ref[...]
        dh3_ref[...] = diff * (1.0 / D_MODEL)
        per_row = jnp.sum(diff * diff, axis=-1, keepdims=True) * (1.0 / D_MODEL)
        loss_ref[...] += 0.5 * jnp.sum(per_row, axis=0, keepdims=True)

    row_spec = lambda w: pl.BlockSpec((tm, w), lambda i: (i, 0))
    full = lambda a: pl.BlockSpec(a.shape, lambda i: (0,) * a.ndim, pipeline_mode=pl.Buffered(1))
    return pl.pallas_call(
        body, name="ffn_loss_fwd", grid=(t_len // tm,),
        out_shape=(jax.ShapeDtypeStruct((t_len, D_MODEL), BF), jax.ShapeDtypeStruct((t_len, D_FF), BF),
                   jax.ShapeDtypeStruct((t_len, D_FF), BF), jax.ShapeDtypeStruct((t_len, D_FF), BF),
                   jax.ShapeDtypeStruct((t_len, D_MODEL), F32), jax.ShapeDtypeStruct((8, LANES), F32)),
        in_specs=[row_spec(D_MODEL), full(g_ffn), full(w_gate), full(w_up), full(w_down), row_spec(D_MODEL)],
        out_specs=(row_spec(D_MODEL), row_spec(D_FF), row_spec(D_FF), row_spec(D_FF), row_spec(D_MODEL),
                   pl.BlockSpec((8, LANES), lambda i: (0, 0))),
        compiler_params=_cparams(("arbitrary",)),
    )(h2, g_ffn, w_gate, w_up, w_down, target)


def _ffn_bwd(dh3, gate, up, h2, g_ffn, w_gate, w_up, w_down):
    t_len = h2.shape[0]
    tm = min(ROW_TILE, t_len)

    def body(dh3_ref, gate_ref, up_ref, h2_ref, g_ref, wg_ref, wu_ref, wd_ref, dgate_ref, dup_ref, dh2_ref, dg_ref):
        @pl.when(pl.program_id(0) == 0)
        def _():
            dg_ref[...] = jnp.zeros_like(dg_ref)

        dh3v = dh3_ref[...]
        dact = _dot_nt(dh3v, wd_ref[...])
        g = gate_ref[...].astype(F32)
        sg = _sigmoid(g)
        dup = (dact * (g * sg)).astype(BF)
        dgate = (dact * up_ref[...].astype(F32) * (sg * (1.0 + g * (1.0 - sg)))).astype(BF)
        dup_ref[...] = dup
        dgate_ref[...] = dgate
        dhn = _dot_nt(dgate, wg_ref[...]) + _dot_nt(dup, wu_ref[...])
        dx, dg = _rms_bwd(h2_ref[...], g_ref[...], dhn)
        dh2_ref[...] = dh3v + dx
        dg_ref[...] += dg

    row_spec = lambda w: pl.BlockSpec((tm, w), lambda i: (i, 0))
    full = lambda a: pl.BlockSpec(a.shape, lambda i: (0,) * a.ndim, pipeline_mode=pl.Buffered(1))
    return pl.pallas_call(
        body, name="ffn_bwd", grid=(t_len // tm,),
        out_shape=(jax.ShapeDtypeStruct((t_len, D_FF), BF), jax.ShapeDtypeStruct((t_len, D_FF), BF),
                   jax.ShapeDtypeStruct((t_len, D_MODEL), F32), jax.ShapeDtypeStruct((1, D_MODEL), F32)),
        in_specs=[row_spec(D_MODEL), row_spec(D_FF), row_spec(D_FF), row_spec(D_MODEL), full(g_ffn), full(w_gate), full(w_up),
                  full(w_down)],
        out_specs=(row_spec(D_FF), row_spec(D_FF), row_spec(D_MODEL), pl.BlockSpec((1, D_MODEL), lambda i: (0, 0))),
        compiler_params=_cparams(("arbitrary",)),
    )(dh3, gate, up, h2, g_ffn, w_gate, w_up, w_down)


def _attn_out_xattn_bwd(dh2, h1, qx, kn, v, w_xo, w_xq, w_out, g_xattn, g_xq):
    t_len = h1.shape[0]
    tm = min(ROW_TILE, t_len)
    m_tok = kn.shape[0]

    def body(dh2_ref, h1_ref, qx_ref, kn_ref, v_ref, wxo_ref, wq_ref, wo_ref, g_ref, gq_ref,
             dqx_ref, dh1_ref, dmr_ref, dmf_ref, dkn_ref, dv_ref, dg_ref, dgq_ref, dqx_scr):
        @pl.when(pl.program_id(0) == 0)
        def _():
            dkn_ref[...] = jnp.zeros_like(dkn_ref)
            dv_ref[...] = jnp.zeros_like(dv_ref)
            dg_ref[...] = jnp.zeros_like(dg_ref)
            dgq_ref[...] = jnp.zeros_like(dgq_ref)

        dh2v = dh2_ref[...]
        do = _dot_nt(dh2v, wxo_ref[...])
        gq = gq_ref[...]
        dgq = jnp.zeros((1, XHD), F32)
        for h in range(N_XH):
            sl = slice(h * XHD, (h + 1) * XHD)
            qraw = qx_ref[:, sl].astype(F32)
            qn = _rms_fwd(qraw, gq)
            p = _softmax_rows(_dot_nt(qn, kn_ref[:, sl]) * (XHD ** -0.5))
            doh = do[:, sl]
            dv_ref[:, sl] += _dot_tn(p, doh)
            dp = _dot_nt(doh, v_ref[:, sl])
            ds = p * (dp - jnp.sum(dp * p, axis=-1, keepdims=True)) * (XHD ** -0.5)
            dqn = _dot(ds, kn_ref[:, sl])
            dkn_ref[:, sl] += _dot_tn(ds, qn)
            dx, dg_h = _rms_bwd(qraw, gq, dqn)
            dgq = dgq + dg_h
            dqx_scr[:, sl] = dx.astype(BF)
        dgq_ref[...] += dgq
        dqx = dqx_scr[...]
        dqx_ref[...] = dqx
        dhn = _dot_nt(dqx, wq_ref[...])
        dx, dg = _rms_bwd(h1_ref[...], g_ref[...], dhn)
        dg_ref[...] += dg
        dh1 = dh2v + dx
        dh1_ref[...] = dh1
        dmix = _dot_nt(dh1, wo_ref[...])
        dmr_ref[...] = dmix[:, :GROUP_W]
        dmf_ref[...] = dmix[:, GROUP_W:].astype(BF)

    row_spec = lambda w: pl.BlockSpec((tm, w), lambda i: (i, 0))
    full = lambda a: pl.BlockSpec(a.shape, lambda i: (0,) * a.ndim)
    acc = lambda r, c: pl.BlockSpec((r, c), lambda i: (0, 0))
    return pl.pallas_call(
        body, name="attn_out_xattn_bwd", grid=(t_len // tm,),
        out_shape=(jax.ShapeDtypeStruct((t_len, D_MODEL), BF), jax.ShapeDtypeStruct((t_len, D_MODEL), F32),
                   jax.ShapeDtypeStruct((t_len, GROUP_W), F32), jax.ShapeDtypeStruct((t_len, GROUP_W), BF),
                   jax.ShapeDtypeStruct((m_tok, D_MODEL), F32), jax.ShapeDtypeStruct((m_tok, D_MODEL), F32),
                   jax.ShapeDtypeStruct((1, D_MODEL), F32), jax.ShapeDtypeStruct((1, XHD), F32)),
        in_specs=[row_spec(D_MODEL), row_spec(D_MODEL), row_spec(D_MODEL), full(kn), full(v), full(w_xo), full(w_xq), full(w_out),
                  full(g_xattn), full(g_xq)],
        out_specs=(row_spec(D_MODEL), row_spec(D_MODEL), row_spec(GROUP_W), row_spec(GROUP_W), acc(m_tok, D_MODEL),
                   acc(m_tok, D_MODEL), acc(1, D_MODEL), acc(1, XHD)),
        scratch_shapes=[pltpu.VMEM((tm, D_MODEL), BF)],
        compiler_params=_cparams(("arbitrary",)),
    )(dh2, h1, qx, kn, v, w_xo, w_xq, w_out, g_xattn, g_xq)


def _mem_kv_bwd(dkn, dv, kraw, mem, memn, g_mem, g_xk, w_xkv):
    m_tok = mem.shape[0]

    def body(dkn_ref, dv_ref, kraw_ref, mem_ref, memn_ref, gm_ref, gk_ref, w_ref, dw_ref, dgm_ref, dgk_ref, dkv_scr):
        gk = gk_ref[...]
        dgk = jnp.zeros((1, XHD), F32)
        for h in range(N_XH):
            sl = slice(h * XHD, (h + 1) * XHD)
            dx, dg_h = _rms_bwd(kraw_ref[:, sl], gk, dkn_ref[:, sl])
            dgk = dgk + dg_h
            dkv_scr[:, sl] = dx.astype(BF)
        dgk_ref[...] = dgk
        dkv_scr[:, D_MODEL:] = dv_ref[...].astype(BF)
        dkv = dkv_scr[...]
        dw_ref[...] = _dot_tn(memn_ref[...], dkv)
        dmemn = _dot_nt(dkv, w_ref[...])
        mem_v = mem_ref[...]
        r = lax.rsqrt(jnp.mean(mem_v * mem_v, axis=-1, keepdims=True) + EPS)
        dgm_ref[...] = jnp.sum(dmemn * mem_v * r, axis=0, keepdims=True)

    return pl.pallas_call(
        body, name="mem_kv_bwd",
        out_shape=(jax.ShapeDtypeStruct((D_MODEL, 2 * D_MODEL), F32), jax.ShapeDtypeStruct((1, D_MODEL), F32),
                   jax.ShapeDtypeStruct((1, XHD), F32)),
        in_specs=[VMEM_SPEC] * 8, out_specs=(VMEM_SPEC,) * 3,
        scratch_shapes=[pltpu.VMEM((m_tok, 2 * D_MODEL), BF)],
        compiler_params=_cparams(),
    )(dkn, dv, kraw, mem, memn, g_mem, g_xk, w_xkv)


def _fox_bwd(fq, fk, proj, dmf, o32, lse_row, f_row, f_col):
    t_len = fq.shape[0]
    tb = min(ATT_BLOCK, t_len)
    n_b = t_len // tb
    v_col = 6 * GROUP_W // LANES

    def body(k_ref, v_ref, q_ref, do_ref, o_ref, lse_ref, fr_ref, fc_ref, dq_ref, dk_ref, dv_ref, df_ref, delta):
        j = pl.program_id(1)

        @pl.when(j == 0)
        def _():
            dq_ref[...] = jnp.zeros_like(dq_ref)
            dd = do_ref[...].astype(F32) * o_ref[...]
            hrow = lax.broadcasted_iota(jnp.int32, (8, LANES), 0)
            lane = lax.broadcasted_iota(jnp.int32, (8, LANES), 1)
            ind = ((lane // HEAD_DIM) == hrow).astype(BF)
            delta[...] = _dot_nt_exact(ind, dd)

        k2, v2 = k_ref[...], v_ref[...]
        krow = j * tb + lax.broadcasted_iota(jnp.int32, (tb, tb), 0)
        qcol0 = lax.broadcasted_iota(jnp.int32, (tb, tb), 1)
        dks, dvs, dfs = [], [], []
        for hh in range(2):
            sl = slice(hh * HEAD_DIM, (hh + 1) * HEAD_DIM)
            k, v = k2[:, sl], v2[:, sl]
            f_k = fc_ref[0][:, hh:hh + 1]

            def step(i, carry, sl=sl, k=k, v=v, f_k=f_k, hh=hh):
                dk, dv, df = carry
                rows = pl.ds(pl.multiple_of(i * tb, tb), tb)
                q = q_ref[rows, :][:, sl]
                do = do_ref[rows, :][:, sl]
                s_t = _dot_nt(k, q) + fr_ref[0, hh:hh + 1, rows] - f_k
                s_t = jnp.where(i * tb + qcol0 >= krow, s_t, NEG)
                p_t = jnp.exp(s_t - lse_ref[0, hh:hh + 1, rows])
                dv = dv + _dot(p_t, do)
                ds_t = p_t * (_dot_nt(v, do) - delta[hh:hh + 1, rows])
                dk = dk + _dot(ds_t, q)
                df = df - jnp.sum(ds_t, axis=-1, keepdims=True)
                dq_ref[rows, sl] += _dot_tn(ds_t, k)
                return dk, dv, df

            dk, dv, df = lax.fori_loop(j, n_b, step, (jnp.zeros((tb, HEAD_DIM), F32), jnp.zeros((tb, HEAD_DIM), F32),
                                                      jnp.zeros((tb, 1), F32)))
            dks.append(dk)
            dvs.append(dv)
            dfs.append(df)
        dk_ref[...] = jnp.concatenate(dks, axis=-1)
        dv_ref[...] = jnp.concatenate(dvs, axis=-1)
        df_ref[0] = jnp.concatenate(dfs, axis=-1)

    blk = lambda col0: pl.BlockSpec((tb, LANES), lambda hp, j: (j, col0 + hp))
    whole = pl.BlockSpec((t_len, LANES), lambda hp, j: (0, hp))
    rows2 = pl.BlockSpec((1, 2, t_len), lambda hp, j: (hp, 0, 0))
    cols2 = pl.BlockSpec((1, tb, 2), lambda hp, j: (hp, j, 0))
    return pl.pallas_call(
        body, name="fox_bwd", grid=(N_HEADS // 2, n_b),
        out_shape=(jax.ShapeDtypeStruct((t_len, GROUP_W), F32), jax.ShapeDtypeStruct((t_len, GROUP_W), F32),
                   jax.ShapeDtypeStruct((t_len, GROUP_W), F32), jax.ShapeDtypeStruct((N_HEADS // 2, t_len, 2), F32)),
        in_specs=[blk(0), blk(v_col), whole, whole, whole, rows2, rows2, cols2],
        out_specs=(whole, blk(0), blk(0), cols2),
        scratch_shapes=[pltpu.VMEM((8, t_len), F32)],
        compiler_params=_cparams(("arbitrary", "arbitrary")),
    )(fk, proj, fq, dmf, o32, lse_row, f_row, f_col)


def _retention_bwd(dmr, raw, proj, g_ret, rq, rk, states, tables):
    t_len = rq.shape[0]
    c = min(RET_BLOCK, t_len)
    n_b = t_len // c
    wdec, qdec, kdec, cdec = tables
    v_col, g_col = 2 * GROUP_W // LANES, 3 * GROUP_W // LANES

    def body(d_ref, raw_ref, rg_ref, g_ref, q_ref, k_ref, v_ref, st_ref, w_ref, qd_ref, kd_ref, cd_ref,
             dq_ref, dk_ref, dv_ref, drg_ref, dg_ref, gstate):
        @pl.when(pl.program_id(1) == 0)
        def _():
            gstate[...] = jnp.zeros_like(gstate)
            dg_ref[...] = jnp.zeros_like(dg_ref)

        d, raw_v, g = d_ref[...], raw_ref[...], g_ref[0]
        gate = rg_ref[...].astype(F32)
        xc = raw_v - _group_mean64(raw_v)
        r = lax.rsqrt(_group_mean64(xc * xc) + EPS)
        xh = xc * r
        sg = _sigmoid(gate)
        drg_ref[...] = d * (xh * g) * (sg * (1.0 + gate * (1.0 - sg)))
        dy = d * (gate * sg)
        dg_ref[0] += jnp.sum(dy * xh, axis=0, keepdims=True)
        dxh = dy * g
        do2 = r * (dxh - _group_mean64(dxh) - xh * _group_mean64(dxh * xh))
        q2, k2, v2 = q_ref[...], k_ref[...], v_ref[...]
        dqs, dks, dvs = [], [], []
        for hh in range(2):
            sl = slice(hh * HEAD_DIM, (hh + 1) * HEAD_DIM)
            q, k, v, do = q2[:, sl], k2[:, sl], v2[:, sl], do2[:, sl].astype(BF)
            w = w_ref[hh]
            a = _dot_nt(q, k) * w
            dm = _dot_nt(do, v) * w
            sp, gs = st_ref[0, 0, hh], gstate[hh]
            qd = q.astype(F32) * qd_ref[hh]
            kd = k.astype(F32) * kd_ref[hh]
            dqs.append(_dot(dm, k) + _dot_nt(do, sp) * qd_ref[hh])
            dks.append(_dot_tn(dm, q) + _dot_nt(v, gs) * kd_ref[hh])
            dvs.append(_dot_tn(a, do) + _dot(kd, gs))
            gstate[hh] = gs * cd_ref[hh] + _dot_tn(qd, do)
        dq_ref[...] = jnp.concatenate(dqs, axis=-1)
        dk_ref[...] = jnp.concatenate(dks, axis=-1)
        dv_ref[...] = jnp.concatenate(dvs, axis=-1)

    blk = lambda col0: pl.BlockSpec((c, LANES), lambda hp, i: (n_b - 1 - i, col0 + hp))
    tab = lambda a: pl.BlockSpec((2,) + a.shape[1:], lambda hp, i: (hp, 0, 0))
    gspec = pl.BlockSpec((1, 1, LANES), lambda hp, i: (hp, 0, 0))
    return pl.pallas_call(
        body, name="retention_bwd", grid=(N_HEADS // 2, n_b),
        out_shape=(jax.ShapeDtypeStruct((t_len, GROUP_W), F32),) * 4 + (jax.ShapeDtypeStruct((N_HEADS // 2, 1, LANES), F32),),
        in_specs=[blk(0), blk(0), blk(g_col), gspec, blk(0), blk(0), blk(v_col),
                  pl.BlockSpec((1, 1, 2, HEAD_DIM, HEAD_DIM), lambda hp, i: (hp, n_b - 1 - i, 0, 0, 0)),
                  tab(wdec), tab(qdec), tab(kdec), tab(cdec)],
        out_specs=(blk(0), blk(0), blk(0), blk(0), gspec),
        scratch_shapes=[pltpu.VMEM((2, HEAD_DIM, HEAD_DIM), F32)],
        compiler_params=_cparams(("arbitrary", "arbitrary")),
    )(dmr, raw, proj, g_ret, rq, rk, proj, states, wdec, qdec, kdec, cdec)


def _in_proj_bwd(x, g_mix, dh1, dq_r, dk_r, dv_r, drg, dq_f, dk_f, dv_f, df_col, proj, z, cos_t, sin_t, gq_t, gk_t, w_main, w_ff):
    t_len = x.shape[0]
    tm = min(ROW_TILE, t_len)
    n_t = t_len // tm

    def body(x_ref, g_ref, dh1_ref, dqr_ref, dkr_ref, dvr_ref, drg_ref, dqf_ref, dkf_ref, dvf_ref, df_ref, fq_ref, fk_ref, z_ref,
             cos_ref, sin_ref, gq_ref, gk_ref, wm_ref, wf_ref,
             dproj_ref, dz_ref, dx_ref, dg_ref, dgq_ref, dgk_ref, db_ref, carry, gq_acc, gk_acc):
        i = pl.program_id(0)

        @pl.when(i == 0)
        def _():
            carry[...] = jnp.zeros_like(carry)
            gq_acc[...] = jnp.zeros_like(gq_acc)
            gk_acc[...] = jnp.zeros_like(gk_acc)
            dg_ref[...] = jnp.zeros_like(dg_ref)
            db_ref[...] = jnp.zeros_like(db_ref)

        c, s = cos_ref[...], sin_ref[...]
        gq, gk = gq_ref[...], gk_ref[...]
        dgq = jnp.zeros((1, LANES), F32)
        dgk = jnp.zeros((1, LANES), F32)
        for sl in _chunks(GROUP_W):
            dy = dqr_ref[:, sl] * 0.125
            dproj_ref[:, sl] = (dy * c + _swap32(dy * s)).astype(BF)
            dy = dkr_ref[:, sl]
            dproj_ref[:, GROUP_W + sl.start:GROUP_W + sl.stop] = (dy * c + _swap32(dy * s)).astype(BF)
            dproj_ref[:, 2 * GROUP_W + sl.start:2 * GROUP_W + sl.stop] = dvr_ref[:, sl].astype(BF)
            dproj_ref[:, 3 * GROUP_W + sl.start:3 * GROUP_W + sl.stop] = drg_ref[:, sl].astype(BF)
            for src, dsrc, gain, off in ((fq_ref, dqf_ref, gq, 4), (fk_ref, dkf_ref, gk, 5)):
                xr = src[:, sl].astype(F32)
                r = lax.rsqrt(_group_mean64(xr * xr) + EPS)
                xh = xr * r
                dy = dsrc[:, sl] * (0.125 if off == 4 else 1.0)
                dgs = jnp.sum(dy * xh, axis=0, keepdims=True)
                if off == 4:
                    dgq = dgq + dgs
                else:
                    dgk = dgk + dgs
                dxh = dy * gain
                dproj_ref[:, off * GROUP_W + sl.start:off * GROUP_W + sl.stop] = \
                    (r * (dxh - xh * _group_mean64(dxh * xh))).astype(BF)
            dproj_ref[:, 6 * GROUP_W + sl.start:6 * GROUP_W + sl.stop] = dvf_ref[:, sl].astype(BF)
        gq_acc[...] += dgq
        gk_acc[...] += dgk
        row = lax.broadcasted_iota(jnp.int32, (tm, tm), 0)
        col = lax.broadcasted_iota(jnp.int32, (tm, tm), 1)
        dlf = _dot_exact((col >= row).astype(BF), df_ref[...]) + carry[0:1, :]
        carry[...] = jnp.broadcast_to(dlf[0:1, :], carry.shape)
        lane = lax.broadcasted_iota(jnp.int32, (tm, LANES), 1)
        dz = jnp.where(lane < N_HEADS, dlf / (1.0 + jnp.exp(z_ref[...])), 0.0)
        db_ref[...] += jnp.sum(dz, axis=0, keepdims=True)
        dz_bf = dz.astype(BF)
        dz_ref[...] = dz_bf
        dn1 = _dot_nt(dz_bf, wf_ref[...])
        for sec in range(MAIN_W // GROUP_W):
            sl = slice(sec * GROUP_W, (sec + 1) * GROUP_W)
            dn1 = dn1 + _dot_nt(dproj_ref[:, sl], wm_ref[:, sl])
        dx, dg = _rms_bwd(x_ref[...], g_ref[...], dn1)
        dx_ref[...] = dh1_ref[...] + dx
        dg_ref[...] += dg

        @pl.when(i == n_t - 1)
        def _():
            dgq_ref[...] = gq_acc[:, :HEAD_DIM] + gq_acc[:, HEAD_DIM:]
            dgk_ref[...] = gk_acc[:, :HEAD_DIM] + gk_acc[:, HEAD_DIM:]

    row_spec = lambda w, col=0: pl.BlockSpec((tm, w), lambda i: (n_t - 1 - i, col))
    full = lambda a: pl.BlockSpec(a.shape, lambda i: (0,) * a.ndim)
    acc = lambda r, c: pl.BlockSpec((r, c), lambda i: (0, 0))
    return pl.pallas_call(
        body, name="in_proj_bwd", grid=(n_t,),
        out_shape=(jax.ShapeDtypeStruct((t_len, MAIN_W), BF), jax.ShapeDtypeStruct((t_len, LANES), BF),
                   jax.ShapeDtypeStruct((t_len, D_MODEL), F32), jax.ShapeDtypeStruct((1, D_MODEL), F32),
                   jax.ShapeDtypeStruct((1, HEAD_DIM), F32), jax.ShapeDtypeStruct((1, HEAD_DIM), F32),
                   jax.ShapeDtypeStruct((1, LANES), F32)),
        in_specs=[row_spec(D_MODEL), full(g_mix), row_spec(D_MODEL)] + [row_spec(GROUP_W)] * 7
        + [row_spec(LANES), row_spec(GROUP_W, 4), row_spec(GROUP_W, 5), row_spec(LANES), row_spec(LANES), row_spec(LANES),
           full(gq_t), full(gk_t), full(w_main), full(w_ff)],
        out_specs=(row_spec(MAIN_W), row_spec(LANES), row_spec(D_MODEL), acc(1, D_MODEL), acc(1, HEAD_DIM), acc(1, HEAD_DIM),
                   acc(1, LANES)),
        scratch_shapes=[pltpu.VMEM((8, LANES), F32), pltpu.VMEM((1, LANES), F32), pltpu.VMEM((1, LANES), F32)],
        compiler_params=_cparams(("arbitrary",)),
    )(x, g_mix, dh1, dq_r, dk_r, dv_r, drg, dq_f, dk_f, dv_f, df_col, proj, proj, z, cos_t, sin_t, gq_t, gk_t, w_main, w_ff)


def _matmul_tn(a, b, name, bm=256, bk=512):
    t_len, m = a.shape
    n = b.shape[1]
    bm, bk = min(bm, m), min(bk, t_len)

    def body(a_ref, b_ref, o_ref):
        @pl.when(pl.program_id(1) == 0)
        def _():
            o_ref[...] = jnp.zeros_like(o_ref)

        o_ref[...] += _dot_tn(a_ref[...], b_ref[...])

    return pl.pallas_call(
        body, name=name, grid=(m // bm, t_len // bk),
        out_shape=jax.ShapeDtypeStruct((m, n), F32),
        in_specs=[pl.BlockSpec((bk, bm), lambda i, k: (k, i)), pl.BlockSpec((bk, n), lambda i, k: (k, 0))],
        out_specs=pl.BlockSpec((bm, n), lambda i, k: (i, 0)),
        compiler_params=_cparams(("arbitrary", "arbitrary")),
    )(a, b)


def _place():
    x, y, c = lax.axis_index("x"), lax.axis_index("y"), lax.axis_index("c")
    chips = [(1 - x, y), (x, 1 - y), (1 - x, 1 - y)]
    return x, y, c, chips


def _row_chunks(rows, limit):
    step = max(d for d in range(16, min(rows, limit) + 1, 16) if rows % d == 0)
    return [slice(i, i + step) for i in range(0, rows, step)]


ICI_CHUNK_ROWS = 128
D2D_CHUNK_ROWS = 64


def _all_gather_weights(shards):
    n_w = len(shards)

    def body(*refs):
        ins, outs = refs[:n_w], refs[n_w:2 * n_w]
        send_sems, recv_sems, local_sems = refs[2 * n_w:]
        x, y, c, chips = _place()
        me_chip = 2 * x + y
        sibling = (x, y, 1 - c)

        def copy(w, k, slot, half, to, rows=slice(None), src=None):
            dst = outs[w].at[slot, half, rows]
            return pltpu.make_async_remote_copy(src_ref=dst if src is None else src, dst_ref=dst,
                                                send_sem=send_sems.at[w, k], recv_sem=recv_sems.at[w, k],
                                                device_id=to, device_id_type=MESH)

        local = [pltpu.make_async_copy(ins[w], outs[w].at[me_chip], local_sems.at[w]) for w in range(n_w)]
        for cp in local:
            cp.start()
        for w in range(n_w):
            for j, chip in enumerate(chips):
                for rows in _row_chunks(ins[w].shape[1], ICI_CHUNK_ROWS):
                    copy(w, j, me_chip, c, (*chip, c), rows, src=ins[w].at[c, rows]).start()
        for w in range(n_w):
            for j, (px, py) in enumerate(chips):
                copy(w, j, 2 * px + py, c, (x, y, c)).wait_recv()
                for rows in _row_chunks(ins[w].shape[1], D2D_CHUNK_ROWS):
                    copy(w, 3 + j, 2 * px + py, c, sibling, rows).start()
        for w in range(n_w):
            for j, (px, py) in enumerate(chips):
                copy(w, 3 + j, 2 * px + py, 1 - c, (x, y, c)).wait_recv()
        for w in range(n_w):
            for j, (px, py) in enumerate(chips):
                copy(w, j, me_chip, c, (px, py, c), src=ins[w].at[c]).wait_send()
                copy(w, 3 + j, 2 * px + py, c, sibling).wait_send()
        for cp in local:
            cp.wait()

    return pl.pallas_call(
        body, name="all_gather_weights",
        out_shape=tuple(jax.ShapeDtypeStruct((4,) + s.shape, s.dtype) for s in shards),
        in_specs=[ANY] * n_w, out_specs=(ANY,) * n_w,
        scratch_shapes=[pltpu.SemaphoreType.DMA((n_w, 6)), pltpu.SemaphoreType.DMA((n_w, 6)), pltpu.SemaphoreType.DMA((n_w,))],
    )(*shards)


def _exchange_core_halves(grads):
    n_w = len(grads)

    def body(*refs):
        ins, mine, theirs = refs[:n_w], refs[n_w:2 * n_w], refs[2 * n_w:3 * n_w]
        send_sems, recv_sems, local_sems = refs[3 * n_w:]
        x, y, c, _ = _place()

        def remote(w, k=slice(None), rows=slice(None)):
            return pltpu.make_async_remote_copy(src_ref=ins[w].at[k, 1 - c, rows], dst_ref=theirs[w].at[k, rows],
                                                send_sem=send_sems.at[w], recv_sem=recv_sems.at[w], device_id=(x, y, 1 - c),
                                                device_id_type=MESH)

        def local(w, k=slice(None)):
            return pltpu.make_async_copy(ins[w].at[k, c], mine[w].at[k], local_sems.at[w])

        for w in range(n_w):
            for k in range(4):
                for rows in _row_chunks(ins[w].shape[2], D2D_CHUNK_ROWS):
                    remote(w, k, rows).start()
        for w in range(n_w):
            for k in range(4):
                local(w, k).start()
        for w in range(n_w):
            remote(w).wait()
        for w in range(n_w):
            local(w).wait()

    half =tuple(jax.ShapeDtypeStruct((4,) + g.shape[2:], g.dtype) for g in grads)
    return pl.pallas_call(
        body, name="exchange_core_halves", out_shape=half + half,
        in_specs=[ANY] * n_w, out_specs=(ANY,) * (2 * n_w),
        scratch_shapes=[pltpu.SemaphoreType.DMA((n_w,)), pltpu.SemaphoreType.DMA((n_w,)), pltpu.SemaphoreType.DMA((n_w,))],
    )(*grads)


def _add_pairs(a, b, name):
    _, r, c = a.shape
    rb = 32 if r % 32 == 0 else r

    def body(a_ref, b_ref, o_ref, ob_ref):
        s = a_ref[...] + b_ref[...]
        o_ref[...] = s
        ob_ref[...] = s.astype(BF)

    spec = pl.BlockSpec((4, rb, c), lambda i: (0, i, 0))
    return pl.pallas_call(
        body, name=name, grid=(r // rb,),
        out_shape=(jax.ShapeDtypeStruct(a.shape, F32), jax.ShapeDtypeStruct(a.shape, BF)),
        in_specs=[spec, spec], out_specs=(spec, spec), compiler_params=_cparams(("arbitrary",)),
    )(a, b)


def _scatter_to_chips(sums_f32, sums_bf16):
    n_w = len(sums_f32)

    def body(*refs):
        f32s, bfs, own, got = refs[:n_w], refs[n_w:2 * n_w], refs[2 * n_w:3 * n_w], refs[3 * n_w:4 * n_w]
        send_sems, recv_sems, local_sems = refs[4 * n_w:]
        x, y, c, chips = _place()

        def remote(w, j, px, py, rows=slice(None)):
            return pltpu.make_async_remote_copy(src_ref=bfs[w].at[2 * px + py, rows], dst_ref=got[w].at[j, rows],
                                                send_sem=send_sems.at[w, j], recv_sem=recv_sems.at[w, j], device_id=(px, py, c),
                                                device_id_type=MESH)

        local = [pltpu.make_async_copy(f32s[w].at[2 * x + y], own[w], local_sems.at[w]) for w in range(n_w)]
        for cp in local:
            cp.start()
        for w in range(n_w):
            for j, (px, py) in enumerate(chips):
                for rows in _row_chunks(bfs[w].shape[1], ICI_CHUNK_ROWS):
                    remote(w, j, px, py, rows).start()
        for w in range(n_w):
            for j, (px, py) in enumerate(chips):
                remote(w, j, px, py).wait()
        for cp in local:
            cp.wait()

    return pl.pallas_call(
        body, name="scatter_to_chips",
        out_shape=tuple(jax.ShapeDtypeStruct(s.shape[1:], F32) for s in sums_f32)
        + tuple(jax.ShapeDtypeStruct((3,) + s.shape[1:], BF) for s in sums_bf16),
        in_specs=[ANY] * (2 * n_w), out_specs=(ANY,) * (2 * n_w),
        scratch_shapes=[pltpu.SemaphoreType.DMA((n_w, 3)), pltpu.SemaphoreType.DMA((n_w, 3)), pltpu.SemaphoreType.DMA((n_w,))],
    )(*sums_f32, *sums_bf16)


def _add_received(own, got, name):
    r, c = own.shape
    rb = 32 if r % 32 == 0 else r

    def body(o_ref, g_ref, out_ref):
        out_ref[...] = ((o_ref[...] + g_ref[0].astype(F32)) + g_ref[1].astype(F32)) + g_ref[2].astype(F32)

    return pl.pallas_call(
        body, name=name, grid=(r // rb,), out_shape=jax.ShapeDtypeStruct((r, c), F32),
        in_specs=[pl.BlockSpec((rb, c), lambda i: (i, 0)), pl.BlockSpec((3, rb, c), lambda i: (0, i, 0))],
        out_specs=pl.BlockSpec((rb, c), lambda i: (i, 0)), compiler_params=_cparams(("arbitrary",)),
    )(own, got)


def _share_with_sibling(halves):
    n_w = len(halves)

    def body(*refs):
        ins, outs = refs[:n_w], refs[n_w:2 * n_w]
        send_sems, recv_sems, local_sems = refs[2 * n_w:]
        x, y, c, _ = _place()

        def remote(w, rows=slice(None)):
            return pltpu.make_async_remote_copy(src_ref=ins[w].at[rows], dst_ref=outs[w].at[c, rows], send_sem=send_sems.at[w],
                                                recv_sem=recv_sems.at[w], device_id=(x, y, 1 - c), device_id_type=MESH)

        local = [pltpu.make_async_copy(ins[w], outs[w].at[c], local_sems.at[w]) for w in range(n_w)]
        for w in range(n_w):
            for rows in _row_chunks(ins[w].shape[0], D2D_CHUNK_ROWS):
                remote(w, rows).start()
        for cp in local:
            cp.start()
        for w in range(n_w):
            remote(w).wait()
        for cp in local:
            cp.wait()

    return pl.pallas_call(
        body, name="share_with_sibling",
        out_shape=tuple(jax.ShapeDtypeStruct((2,) + h.shape, h.dtype) for h in halves),
        in_specs=[ANY] * n_w, out_specs=(ANY,) * n_w,
        scratch_shapes=[pltpu.SemaphoreType.DMA((n_w,)), pltpu.SemaphoreType.DMA((n_w,)), pltpu.SemaphoreType.DMA((n_w,))],
    )(*halves)


def _all_reduce_small(pack):
    r, c = pack.shape

    def body(p_ref, out_ref, slots, send_sems, recv_sems):
        x, y, cc, _ = _place()
        me = 4 * x + 2 * y + cc
        slots[me] = p_ref[...]
        copies = []
        for k in range(1, 8):
            dx, dy, dc = (k >> 2) & 1, (k >> 1) & 1, k & 1
            to = (1 - x if dx else x, 1 - y if dy else y, 1 - cc if dc else cc)
            cp = pltpu.make_async_remote_copy(src_ref=p_ref, dst_ref=slots.at[me], send_sem=send_sems.at[k - 1],
                                              recv_sem=recv_sems.at[k - 1], device_id=to, device_id_type=MESH)
            cp.start()
            copies.append(cp)
        for cp in copies:
            cp.wait()
        total = slots[0]
        for d in range(1, 8):
            total = total + slots[d]
        out_ref[...] = total

    return pl.pallas_call(
        body, name="all_reduce_small", out_shape=jax.ShapeDtypeStruct((r, c), F32),
        in_specs=[VMEM_SPEC], out_specs=VMEM_SPEC,
        scratch_shapes=[pltpu.VMEM((8, r, c), F32), pltpu.SemaphoreType.DMA((7,)), pltpu.SemaphoreType.DMA((7,))],
    )(pack)


def _adamw(w, g, m, v, name):
    r, c = w.shape
    rb = 64 if r % 64 == 0 else r
    c1 = 1.0 - ADAM_B1 ** ADAM_STEP
    c2 = 1.0 - ADAM_B2 ** ADAM_STEP

    def body(w_ref, g_ref, m_ref, v_ref, d_ref, nm_ref, nv_ref):
        gv = g_ref[...]
        nm = ADAM_B1 * m_ref[...] + (1.0 - ADAM_B1) * gv
        nv = ADAM_B2 * v_ref[...] + (1.0 - ADAM_B2) * (gv * gv)
        nm_ref[...] = nm
        nv_ref[...] = nv
        d_ref[...] = -ADAM_LR * ((nm / c1) / (jnp.sqrt(nv / c2) + ADAM_EPS) + ADAM_WD * w_ref[...])

    spec = pl.BlockSpec((rb, c), lambda i: (i, 0))
    return pl.pallas_call(
        body, name=name, grid=(r // rb,), out_shape=(jax.ShapeDtypeStruct((r, c), F32),) * 3,
        in_specs=[spec] * 4, out_specs=(spec,) * 3, compiler_params=_cparams(("arbitrary",)),
    )(w, g, m, v)


def _rope_tables(t_len):
    inv_freq = ROPE_BASE ** (-jnp.arange(0, HEAD_DIM, 2, dtype=F32) / HEAD_DIM)
    ang = jnp.arange(t_len, dtype=F32)[:, None] * inv_freq[None, :]
    cos, sin = jnp.cos(ang), jnp.sin(ang)
    cos_t = jnp.concatenate([cos, cos, cos, cos], axis=-1)
    sin_t = jnp.concatenate([-sin, sin, -sin, sin], axis=-1)
    return cos_t, sin_t


def _cols_to_shards(dw):
    r, n = dw.shape
    return jnp.transpose(dw.reshape(2, r // 2, 4, n // 4), (2, 0, 1, 3))


def _rows_to_shards(dw):
    r, n = dw.shape
    return dw.reshape(4, 2, r // 8, n)


def _pad_lanes(a):
    extra = -a.shape[-1] % LANES
    return a if extra == 0 else jnp.pad(a, [(0, 0)] * (a.ndim - 1) + [(0, extra)])


def _pad_row(a, width=D_MODEL):
    a = a.reshape(1, -1)
    return jnp.pad(a, ((0, 0), (0, width - a.shape[1])))


def kernel(x, mem, g_mix, w_in, b_forget, g_ret_out, g_fox_q, g_fox_k, w_out, g_xattn, w_xq, w_xkv, g_mem, g_xq, g_xk, w_xo, g_ffn, w_gate, w_up, w_down, loss_target, m_g_mix, m_w_in, m_b_forget, m_g_ret_out, m_g_fox_q, m_g_fox_k, m_w_out, m_g_xattn, m_w_xq, m_w_xkv, m_g_mem, m_g_xq, m_g_xk, m_w_xo, m_g_ffn, m_w_gate, m_w_up, m_w_down, v_g_mix, v_w_in, v_b_forget, v_g_ret_out, v_g_fox_q, v_g_fox_k, v_w_out, v_g_xattn, v_w_xq, v_w_xkv, v_g_mem, v_g_xq, v_g_xk, v_w_xo, v_g_ffn, v_w_gate, v_w_up, v_w_down):
    big = {"w_in": (w_in, m_w_in, v_w_in), "w_out": (w_out, m_w_out, v_w_out), "w_xq": (w_xq, m_w_xq, v_w_xq),
           "w_xkv": (w_xkv, m_w_xkv, v_w_xkv), "w_xo": (w_xo, m_w_xo, v_w_xo), "w_gate": (w_gate, m_w_gate, v_w_gate),
           "w_up": (w_up, m_w_up, v_w_up), "w_down": (w_down, m_w_down, v_w_down)}
    big_names = list(big)
    col_sharded = ("w_in", "w_xkv", "w_gate", "w_up")

    shards = []
    for n in big_names:
        w = _pad_lanes(big[n][0][0].astype(BF))
        shards.append(w.reshape(2, w.shape[0] // 2, w.shape[1]))
    gathered = _all_gather_weights(shards)
    full = {}
    for n, g in zip(big_names, gathered):
        _, _, rh, _ = g.shape
        cs = big[n][0].shape[2]
        g = g.reshape(4, 2 * rh, g.shape[3])[:, :, :cs]
        full[n] = jnp.transpose(g, (1, 0, 2)).reshape(2 * rh, 4 * cs) if n in col_sharded else g.reshape(8 * rh, cs)
    small_w = {"g_mix": g_mix, "b_forget": b_forget, "g_ret_out": g_ret_out, "g_fox_q": g_fox_q, "g_fox_k": g_fox_k,
               "g_xattn": g_xattn, "g_mem": g_mem, "g_xq": g_xq, "g_xk": g_xk, "g_ffn": g_ffn}
    m_small = {"g_mix": m_g_mix, "b_forget": m_b_forget, "g_ret_out": m_g_ret_out, "g_fox_q": m_g_fox_q, "g_fox_k": m_g_fox_k,
               "g_xattn": m_g_xattn, "g_mem": m_g_mem, "g_xq": m_g_xq, "g_xk": m_g_xk, "g_ffn": m_g_ffn}
    v_small = {"g_mix": v_g_mix, "b_forget": v_b_forget, "g_ret_out": v_g_ret_out, "g_fox_q": v_g_fox_q, "g_fox_k": v_g_fox_k,
               "g_xattn": v_g_xattn, "g_mem": v_g_mem, "g_xq": v_g_xq, "g_xk": v_g_xk, "g_ffn": v_g_ffn}
    loss_part, grad_x, dw, small_g = _local_step(x[0], mem[0], loss_target[0], full, small_w)
    return _reduce_and_update(big, big_names, col_sharded, dw, small_w, small_g, loss_part, grad_x, m_small, v_small)


def _local_step(xs, mems, tgt, full, small_w):
    g_mix, b_forget, g_ret_out, g_fox_q, g_fox_k = (small_w[n] for n in ("g_mix", "b_forget", "g_ret_out", "g_fox_q", "g_fox_k"))
    g_xattn, g_mem, g_xq, g_xk, g_ffn = (small_w[n] for n in ("g_xattn", "g_mem", "g_xq", "g_xk", "g_ffn"))
    w_main = full["w_in"][:, :MAIN_W]
    w_ff = jnp.pad(full["w_in"][:, MAIN_W:], ((0, 0), (0, LANES - (IN_W - MAIN_W))))
    t_len = xs.shape[0]
    cos_t, sin_t = _rope_tables(t_len)
    tables = _decay_tables(min(RET_BLOCK, t_len))
    gq_t = jnp.concatenate([g_fox_q, g_fox_q], axis=-1)
    gk_t = jnp.concatenate([g_fox_k, g_fox_k], axis=-1)
    b_pad = _pad_row(b_forget, LANES)
    g_ret = g_ret_out.reshape(N_HEADS // 2, 1, LANES)

    memn, kraw, kn, vmem = _mem_kv_fwd(mems, g_mem, full["w_xkv"], g_xk)
    n1, proj, rq, rk, fq, fk, z, fcum = _in_proj_fwd(xs, g_mix, w_main, w_ff, b_pad, cos_t, sin_t, gq_t, gk_t)
    f8 = fcum[:, :N_HEADS]
    f_col = jnp.transpose(f8.reshape(t_len, N_HEADS // 2, 2), (1, 0, 2))
    f_row = jnp.transpose(f8).reshape(N_HEADS // 2, 2, t_len)
    raw, mix_r, states = _retention_fwd(rq, rk, proj, g_ret, tables)
    mix_f, o32, lse = _fox_fwd(fq, fk, proj, f_col, f_row)
    h1, hn2, qx, o_x, h2 = _attn_out_xattn_fwd(xs, mix_r, mix_f, full["w_out"], g_xattn, full["w_xq"], g_xq, kn, vmem, full["w_xo"])
    hn3, gate, up, act, dh3, loss_part = _ffn_loss_fwd(h2, g_ffn, full["w_gate"], full["w_up"], full["w_down"], tgt)

    dgate, dup, dh2, dg_ffn = _ffn_bwd(dh3, gate, up, h2, g_ffn, full["w_gate"], full["w_up"], full["w_down"])
    dqx, dh1, dmr, dmf, dkn, dvm, dg_xattn, dg_xq = _attn_out_xattn_bwd(dh2, h1, qx, kn, vmem, full["w_xo"], full["w_xq"],
                                                                      full["w_out"], g_xattn, g_xq)
    dw_xkv, dg_mem, dg_xk = _mem_kv_bwd(dkn, dvm, kraw, mems, memn, g_mem, g_xk, full["w_xkv"])
    lse_row = jnp.transpose(lse, (0, 2, 1))
    dq_f, dk_f, dv_f, df = _fox_bwd(fq, fk, proj, dmf, o32, lse_row, f_row, f_col)
    dq_r, dk_r, dv_r, drg, dg_ret = _retention_bwd(dmr, raw, proj, g_ret, rq, rk, states, tables)
    df_col = jnp.pad(jnp.transpose(df, (1, 0, 2)).reshape(t_len, N_HEADS), ((0, 0), (0, LANES - N_HEADS)))
    dproj, dz, grad_x, dg_mix, dg_fq, dg_fk, db = _in_proj_bwd(xs, g_mix, dh1, dq_r, dk_r, dv_r, drg, dq_f, dk_f, dv_f, df_col,
                                                              proj, z, cos_t, sin_t, gq_t, gk_t, w_main, w_ff)

    dw = {
        "w_in": jnp.concatenate([_matmul_tn(n1, dproj, "dw_in_main"), _matmul_tn(n1, dz, "dw_in_ff")[:, :IN_W - MAIN_W]], axis=1),
        "w_out": jnp.concatenate([_matmul_tn(mix_r, dh1, "dw_out_ret"), _matmul_tn(mix_f, dh1, "dw_out_fox")], axis=0),
        "w_xq": _matmul_tn(hn2, dqx, "dw_xq"),
        "w_xkv": dw_xkv,
        "w_xo": _matmul_tn(o_x, dh2, "dw_xo"),
        "w_gate": _matmul_tn(hn3, dgate, "dw_gate"),
        "w_up": _matmul_tn(hn3, dup, "dw_up"),
        "w_down": _matmul_tn(act, dh3, "dw_down"),
    }
    small_g = {"g_mix": dg_mix, "b_forget": db[:, :N_HEADS], "g_ret_out": dg_ret, "g_fox_q": dg_fq, "g_fox_k": dg_fk,
               "g_xattn": dg_xattn, "g_mem": dg_mem, "g_xq": dg_xq, "g_xk": dg_xk, "g_ffn": dg_ffn}
    return loss_part, grad_x, dw, small_g


def _reduce_and_update(big, big_names, col_sharded, dw, small_w, small_g, loss_part, grad_x, m_small, v_small):
    parts = [_pad_lanes(_cols_to_shards(dw[n]) if n in col_sharded else _rows_to_shards(dw[n])) for n in big_names]
    ex = _exchange_core_halves(parts)
    mine, theirs = ex[:len(parts)], ex[len(parts):]
    sums = [_add_pairs(a, b, f"core_sum_{n}") for n, a, b in zip(big_names, mine, theirs)]
    sc = _scatter_to_chips([s[0] for s in sums], [s[1] for s in sums])
    own, got = sc[:len(parts)], sc[len(parts):]
    finals = [_add_received(o, g, f"chip_sum_{n}") for n, o, g in zip(big_names, own, got)]
    shared = _share_with_sibling(finals)
    grads, deltas, new_m, new_v = {}, {}, {}, {}
    for n, s in zip(big_names, shared):
        w, m, v = big[n]
        g = s.reshape(w.shape[1], s.shape[2])[:, :w.shape[2]]
        d, nm, nv = _adamw(w[0], g, m[0], v[0], f"adamw_{n}")
        grads[n], deltas[n], new_m[n], new_v[n] = g[None], d[None], nm[None], nv[None]

    small_names = list(small_w)
    pad_rows = SMALL_ROWS - len(small_names) - 1
    stack = lambda d: jnp.concatenate([_pad_row(d[n]) for n in small_names] + [jnp.zeros((pad_rows + 1, D_MODEL), F32)], axis=0)
    g_pack = jnp.concatenate([_pad_row(small_g[n]) for n in small_names] + [_pad_row(loss_part[0:1, 0:1])]
                             + [jnp.zeros((pad_rows, D_MODEL), F32)], axis=0)
    g_tot = _all_reduce_small(g_pack)
    d_s, m_s, v_s = _adamw(stack(small_w), g_tot, stack(m_small), stack(v_small), "adamw_small")
    for i, n in enumerate(small_names):
        shape = small_w[n].shape
        size = int(np.prod(shape))
        grads[n] = g_tot[i, :size].reshape(shape)
        deltas[n], new_m[n], new_v[n] = d_s[i, :size].reshape(shape), m_s[i, :size].reshape(shape), v_s[i, :size].reshape(shape)
    loss = g_tot[len(small_names), 0]

    order = ["g_mix", "w_in", "b_forget", "g_ret_out", "g_fox_q", "g_fox_k", "w_out", "g_xattn", "w_xq", "w_xkv", "g_mem", "g_xq",
             "g_xk", "w_xo", "g_ffn", "w_gate", "w_up", "w_down"]
    return (loss, grad_x[None], *[grads[n] for n in order], *[deltas[n] for n in order], *[new_m[n] for n in order],
            *[new_v[n] for n in order])
```

```python
import functools

import numpy as np
import jax
import jax.numpy as jnp
from jax import lax
from jax.experimental import pallas as pl
from jax.experimental.pallas import tpu as pltpu

F32 = jnp.float32
BF = jnp.bfloat16

D_MODEL = 1024
HEAD_DIM = 64
N_HEADS = 8
GROUP_W = 512
N_XH = 4
XHD = 256
D_FF = 2816
MAIN_W = 3584
IN_W = 3592
ROPE_BASE = 10000.0
EPS = 1e-6
NEG = -1e30
LANES = 128
RET_BLOCK = 256
REF_CHUNK = 64
ROW_TILE = 256
ATT_BLOCK = 256
SMALL_ROWS = 16
VMEM_LIMIT = 56 * 1024 * 1024

ADAM_LR = 0.001
ADAM_B1 = 0.9
ADAM_B2 = 0.999
ADAM_EPS = 1e-08
ADAM_WD = 0.01
ADAM_STEP = 10

MESH = pl.DeviceIdType.MESH
ANY = pl.BlockSpec(memory_space=pl.ANY)
VMEM_SPEC = pl.BlockSpec(memory_space=pltpu.VMEM)


def _cparams(sem=None, vmem=VMEM_LIMIT):
    return pltpu.CompilerParams(dimension_semantics=sem, vmem_limit_bytes=vmem)


def _dot(a, b):
    return jnp.dot(a.astype(BF), b.astype(BF), preferred_element_type=F32)


def _dot_nt(a, b):
    return lax.dot_general(a.astype(BF), b.astype(BF), (((1,), (1,)), ((), ())), preferred_element_type=F32)


def _dot_tn(a, b):
    return lax.dot_general(a.astype(BF), b.astype(BF), (((0,), (0,)), ((), ())), preferred_element_type=F32)


def _split3(x):
    hi = x.astype(BF)
    r = x - hi.astype(F32)
    mid = r.astype(BF)
    lo = (r - mid.astype(F32)).astype(BF)
    return hi, mid, lo


def _dot_exact(ind, x):
    hi, mid, lo = _split3(x)
    return (jnp.dot(ind, lo, preferred_element_type=F32) + jnp.dot(ind, mid, preferred_element_type=F32)
            + jnp.dot(ind, hi, preferred_element_type=F32))


def _dot_nt_exact(ind, x):
    hi, mid, lo = _split3(x)
    dn = (((1,), (1,)), ((), ()))
    return (lax.dot_general(ind, lo, dn, preferred_element_type=F32) + lax.dot_general(ind, mid, dn, preferred_element_type=F32)
            + lax.dot_general(ind, hi, dn, preferred_element_type=F32))


def _sigmoid(x):
    return 1.0 / (1.0 + jnp.exp(-x))


def _rms_fwd(x, g):
    r = lax.rsqrt(jnp.mean(x * x, axis=-1, keepdims=True) + EPS)
    return x * r * g


def _rms_bwd(x, g, dy):
    r = lax.rsqrt(jnp.mean(x * x, axis=-1, keepdims=True) + EPS)
    xh = x * r
    dg = jnp.sum(dy * xh, axis=0, keepdims=True)
    dxh = dy * g
    dx = r * (dxh - xh * jnp.mean(dxh * xh, axis=-1, keepdims=True))
    return dx, dg


def _group_mean64(x):
    lane = lax.broadcasted_iota(jnp.int32, x.shape, 1)
    lo = lane < HEAD_DIM
    s_lo = jnp.sum(jnp.where(lo, x, 0.0), axis=-1, keepdims=True)
    s_hi = jnp.sum(jnp.where(lo, 0.0, x), axis=-1, keepdims=True)
    return jnp.where(lo, s_lo, s_hi) * (1.0 / HEAD_DIM)


def _swap32(x):
    lane = lax.broadcasted_iota(jnp.int32, x.shape, 1)
    first = (lane % HEAD_DIM) < (HEAD_DIM // 2)
    return jnp.where(first, pltpu.roll(x, LANES - HEAD_DIM // 2, axis=1), pltpu.roll(x, HEAD_DIM // 2, axis=1))


def _chunks(w):
    return [slice(j * LANES, (j + 1) * LANES) for j in range(w // LANES)]


def _mem_kv_fwd(mem, g_mem, w_xkv, g_xk):
    m_tok = mem.shape[0]

    def body(mem_ref, gm_ref, w_ref, gk_ref, memn_ref, kraw_ref, kn_ref, v_ref):
        mn = _rms_fwd(mem_ref[...], gm_ref[...]).astype(BF)
        memn_ref[...] = mn
        kv = jnp.dot(mn, w_ref[...], preferred_element_type=F32)
        k = kv[:, :D_MODEL]
        kraw_ref[...] = k
        v_ref[...] = kv[:, D_MODEL:].astype(BF)
        for h in range(N_XH):
            sl = slice(h * XHD, (h + 1) * XHD)
            kn_ref[:, sl] = _rms_fwd(k[:, sl], gk_ref[...]).astype(BF)

    return pl.pallas_call(
        body, name="mem_kv_fwd",
        out_shape=(jax.ShapeDtypeStruct((m_tok, D_MODEL), BF), jax.ShapeDtypeStruct((m_tok, D_MODEL), F32),
                   jax.ShapeDtypeStruct((m_tok, D_MODEL), BF), jax.ShapeDtypeStruct((m_tok, D_MODEL), BF)),
        in_specs=[VMEM_SPEC] * 4, out_specs=(VMEM_SPEC,) * 4, compiler_params=_cparams(),
    )(mem, g_mem, w_xkv, g_xk)


def _in_proj_fwd(x, g_mix, w_main, w_ff, b_pad, cos_t, sin_t, gq_t, gk_t):
    t_len = x.shape[0]
    tm = min(ROW_TILE, t_len)
    n_t = t_len // tm

    def body(x_ref, g_ref, wm_ref, wf_ref, b_ref, cos_ref, sin_ref, gq_ref, gk_ref,
             n1_ref, proj_ref, rq_ref, rk_ref, fq_ref, fk_ref, z_ref, fc_ref, carry):
        i = pl.program_id(0)

        @pl.when(i == 0)
        def _():
            carry[...] = jnp.zeros_like(carry)

        n1 = _rms_fwd(x_ref[...], g_ref[...]).astype(BF)
        n1_ref[...] = n1
        proj = jnp.dot(n1, wm_ref[...], preferred_element_type=F32)
        proj_ref[...] = proj.astype(BF)
        c, s = cos_ref[...], sin_ref[...]
        for j, sl in enumerate(_chunks(GROUP_W)):
            q = proj[:, sl]
            rq_ref[:, sl] = ((q * c + _swap32(q) * s) * 0.125).astype(BF)
            k = proj[:, GROUP_W + j * LANES:GROUP_W + (j + 1) * LANES]
            rk_ref[:, sl] = (k * c + _swap32(k) * s).astype(BF)
            fq = proj[:, 4 * GROUP_W + j * LANES:4 * GROUP_W + (j + 1) * LANES]
            fq_ref[:, sl] = (fq * lax.rsqrt(_group_mean64(fq * fq) + EPS) * gq_ref[...] * 0.125).astype(BF)
            fk = proj[:, 5 * GROUP_W + j * LANES:5 * GROUP_W + (j + 1) * LANES]
            fk_ref[:, sl] = (fk * lax.rsqrt(_group_mean64(fk * fk) + EPS) * gk_ref[...]).astype(BF)
        z = jnp.dot(n1, wf_ref[...], preferred_element_type=F32) + b_ref[...]
        z_ref[...] = z
        lane = lax.broadcasted_iota(jnp.int32, z.shape, 1)
        lf = jnp.where(lane < N_HEADS, jnp.minimum(z, 0.0) - jnp.log(1.0 + jnp.exp(-jnp.abs(z))), 0.0)
        row = lax.broadcasted_iota(jnp.int32, (tm, tm), 0)
        col = lax.broadcasted_iota(jnp.int32, (tm, tm), 1)
        tri = (row >= col).astype(BF)
        fc = _dot_exact(tri, lf) + carry[0:1, :]
        fc_ref[...] = fc
        carry[...] = jnp.broadcast_to(fc[tm - 1:tm, :], carry.shape)

    row_spec = lambda w: pl.BlockSpec((tm, w), lambda i: (i, 0))
    full = lambda a: pl.BlockSpec(a.shape, lambda i: (0,) * a.ndim)
    return pl.pallas_call(
        body, name="in_proj_fwd", grid=(n_t,),
        out_shape=(jax.ShapeDtypeStruct((t_len, D_MODEL), BF), jax.ShapeDtypeStruct((t_len, MAIN_W), BF),
                   jax.ShapeDtypeStruct((t_len, GROUP_W), BF), jax.ShapeDtypeStruct((t_len, GROUP_W), BF),
                   jax.ShapeDtypeStruct((t_len, GROUP_W), BF), jax.ShapeDtypeStruct((t_len, GROUP_W), BF),
                   jax.ShapeDtypeStruct((t_len, LANES), F32), jax.ShapeDtypeStruct((t_len, LANES), F32)),
        in_specs=[row_spec(D_MODEL), full(g_mix), full(w_main), full(w_ff), full(b_pad), row_spec(LANES), row_spec(LANES),
                  full(gq_t), full(gk_t)],
        out_specs=(row_spec(D_MODEL), row_spec(MAIN_W), row_spec(GROUP_W), row_spec(GROUP_W), row_spec(GROUP_W),
                   row_spec(GROUP_W), row_spec(LANES), row_spec(LANES)),
        scratch_shapes=[pltpu.VMEM((8, LANES), F32)],
        compiler_params=_cparams(("arbitrary",)),
    )(x, g_mix, w_main, w_ff, b_pad, cos_t, sin_t, gq_t, gk_t)


def _decay_tables(c):
    h = np.arange(N_HEADS, dtype=np.float64)
    lg = np.log(1.0 - 2.0 ** (-5.0 - h)).astype(np.float32).astype(np.float64)
    t = np.arange(c)
    same_or_earlier = (t[None, :] // REF_CHUNK) <= (t[:, None] // REF_CHUNK)
    w = np.where(same_or_earlier[None], np.exp(lg[:, None, None] * np.abs(t[:, None] - t[None, :])[None]), 0.0)
    qd = np.exp(lg[:, None] * (t[None, :] + 1.0))
    kd = np.exp(lg[:, None] * (c - 1.0 - t[None, :]))
    cd = np.exp(lg * c)
    ones = np.ones((1, 1, HEAD_DIM))
    return (jnp.asarray(w, F32), jnp.asarray(qd[:, :, None] * ones, F32), jnp.asarray(kd[:, :, None] * ones, F32),
            jnp.asarray(cd[:, None, None] * np.ones((1, HEAD_DIM, HEAD_DIM)), F32))


def _retention_fwd(rq, rk, proj, g_ret, tables):
    t_len = rq.shape[0]
    c = min(RET_BLOCK, t_len)
    n_b = t_len // c
    wdec, qdec, kdec, cdec = tables
    v_col, g_col = 2 * GROUP_W // LANES, 3 * GROUP_W // LANES

    def body(q_ref, k_ref, v_ref, rg_ref, g_ref, w_ref, qd_ref, kd_ref, cd_ref, raw_ref, mix_ref, st_ref, state):
        i = pl.program_id(1)

        @pl.when(i == 0)
        def _():
            state[...] = jnp.zeros_like(state)

        q2, k2, v2 = q_ref[...], k_ref[...], v_ref[...]
        outs = []
        for hh in range(2):
            sl = slice(hh * HEAD_DIM, (hh + 1) * HEAD_DIM)
            q, k, v = q2[:, sl], k2[:, sl], v2[:, sl]
            sp = state[hh]
            st_ref[0, 0, hh] = sp
            a = _dot_nt(q, k) * w_ref[hh]
            o = _dot(a, v) + _dot(q.astype(F32) * qd_ref[hh], sp)
            state[hh] = sp * cd_ref[hh] + _dot_tn(k.astype(F32) * kd_ref[hh], v)
            outs.append(o)
        o2 = jnp.concatenate(outs, axis=-1)
        raw_ref[...] = o2
        xc = o2 - _group_mean64(o2)
        xh = xc * lax.rsqrt(_group_mean64(xc * xc) + EPS)
        gate = rg_ref[...].astype(F32)
        mix_ref[...] = (gate * _sigmoid(gate) * (xh * g_ref[0])).astype(BF)

    blk = lambda col0: pl.BlockSpec((c, LANES), lambda hp, i: (i, col0 + hp))
    tab = lambda a: pl.BlockSpec((2,) + a.shape[1:], lambda hp, i: (hp, 0, 0))
    return pl.pallas_call(
        body, name="retention_fwd", grid=(N_HEADS // 2, n_b),
        out_shape=(jax.ShapeDtypeStruct((t_len, GROUP_W), F32), jax.ShapeDtypeStruct((t_len, GROUP_W), BF),
                   jax.ShapeDtypeStruct((N_HEADS // 2, n_b, 2, HEAD_DIM, HEAD_DIM), F32)),
        in_specs=[blk(0), blk(0), blk(v_col), blk(g_col), pl.BlockSpec((1, 1, LANES), lambda hp, i: (hp, 0, 0)),
                  tab(wdec), tab(qdec), tab(kdec), tab(cdec)],
        out_specs=(blk(0), blk(0), pl.BlockSpec((1, 1, 2, HEAD_DIM, HEAD_DIM), lambda hp, i: (hp, i, 0, 0, 0))),
        scratch_shapes=[pltpu.VMEM((2, HEAD_DIM, HEAD_DIM), F32)],
        compiler_params=_cparams(("arbitrary", "arbitrary")),
    )(rq, rk, proj, proj, g_ret, wdec, qdec, kdec, cdec)


def _fox_fwd(fq, fk, proj, f_col, f_row):
    t_len = fq.shape[0]
    tq = min(ATT_BLOCK, t_len)
    n_q = t_len // tq
    v_col = 6 * GROUP_W // LANES

    def body(q_ref, k_ref, v_ref, fq_ref, fk_ref, o_ref, o32_ref, lse_ref):
        i = pl.program_id(1)
        q2 = q_ref[...]
        fq2 = fq_ref[0]
        row = i * tq + lax.broadcasted_iota(jnp.int32, (tq, tq), 0)
        col0 = lax.broadcasted_iota(jnp.int32, (tq, tq), 1)
        outs, lses = [], []
        for hh in range(2):
            sl = slice(hh * HEAD_DIM, (hh + 1) * HEAD_DIM)
            q = q2[:, sl]
            f_q = fq2[:, hh:hh + 1]

            def step(j, carry, sl=sl, q=q, f_q=f_q, hh=hh):
                m, l, acc = carry
                rows = pl.ds(pl.multiple_of(j * tq, tq), tq)
                k = k_ref[rows, :][:, sl]
                v = v_ref[rows, :][:, sl]
                s = _dot_nt(q, k) + f_q - fk_ref[0, hh:hh + 1, rows]
                s = jnp.where(row >= j * tq + col0, s, NEG)
                m_new = jnp.maximum(m, jnp.max(s, axis=-1, keepdims=True))
                alpha = jnp.exp(m - m_new)
                p = jnp.exp(s - m_new)
                l = l * alpha + jnp.sum(p, axis=-1, keepdims=True)
                acc = acc * alpha + _dot(p, v)
                return m_new, l, acc

            m, l, acc = lax.fori_loop(0, i + 1, step, (jnp.full((tq, 1), NEG, F32), jnp.zeros((tq, 1), F32),
                                                       jnp.zeros((tq, HEAD_DIM), F32)))
            outs.append(acc / l)
            lses.append(m + jnp.log(l))
        o2 = jnp.concatenate(outs, axis=-1)
        o32_ref[...] = o2
        o_ref[...] = o2.astype(BF)
        lse_ref[0] = jnp.concatenate(lses, axis=-1)

    return pl.pallas_call(
        body, name="fox_fwd", grid=(N_HEADS // 2, n_q),
        out_shape=(jax.ShapeDtypeStruct((t_len, GROUP_W), BF), jax.ShapeDtypeStruct((t_len, GROUP_W), F32),
                   jax.ShapeDtypeStruct((N_HEADS // 2, t_len, 2), F32)),
        in_specs=[pl.BlockSpec((tq, LANES), lambda hp, i: (i, hp)),
                  pl.BlockSpec((t_len, LANES), lambda hp, i: (0, hp)),
                  pl.BlockSpec((t_len, LANES), lambda hp, i: (0, v_col + hp)),
                  pl.BlockSpec((1, tq, 2), lambda hp, i: (hp, i, 0)),
                  pl.BlockSpec((1, 2, t_len), lambda hp, i: (hp, 0, 0))],
        out_specs=(pl.BlockSpec((tq, LANES), lambda hp, i: (i, hp)), pl.BlockSpec((tq, LANES), lambda hp, i: (i, hp)),
                   pl.BlockSpec((1, tq, 2), lambda hp, i: (hp, i, 0))),
        compiler_params=_cparams(("arbitrary", "arbitrary")),
    )(fq, fk, proj, f_col, f_row)


def _softmax_rows(s):
    p = jnp.exp(s - jnp.max(s, axis=-1, keepdims=True))
    return p / jnp.sum(p, axis=-1, keepdims=True)


def _attn_out_xattn_fwd(x, mix_r, mix_f, w_out, g_xattn, w_xq, g_xq, kn, v, w_xo):
    t_len = x.shape[0]
    tm = min(ROW_TILE, t_len)

    def body(x_ref, mr_ref, mf_ref, wo_ref, g_ref, wq_ref, gq_ref, kn_ref, v_ref, wxo_ref,
             h1_ref, hn_ref, qx_ref, o_ref, h2_ref):
        h1 = x_ref[...] + jnp.dot(mr_ref[...], wo_ref[:GROUP_W, :], preferred_element_type=F32) \
            + jnp.dot(mf_ref[...], wo_ref[GROUP_W:, :], preferred_element_type=F32)
        h1_ref[...] = h1
        hn = _rms_fwd(h1, g_ref[...]).astype(BF)
        hn_ref[...] = hn
        qx = jnp.dot(hn, wq_ref[...], preferred_element_type=F32).astype(BF)
        qx_ref[...] = qx
        for h in range(N_XH):
            sl = slice(h * XHD, (h + 1) * XHD)
            qn = _rms_fwd(qx[:, sl].astype(F32), gq_ref[...])
            p = _softmax_rows(_dot_nt(qn, kn_ref[:, sl]) * (XHD ** -0.5))
            o_ref[:, sl] = _dot(p, v_ref[:, sl]).astype(BF)
        h2_ref[...] = h1 + jnp.dot(o_ref[...], wxo_ref[...], preferred_element_type=F32)

    row_spec = lambda w: pl.BlockSpec((tm, w), lambda i: (i, 0))
    full = lambda a: pl.BlockSpec(a.shape, lambda i: (0,) * a.ndim)
    return pl.pallas_call(
        body, name="attn_out_xattn_fwd", grid=(t_len // tm,),
        out_shape=(jax.ShapeDtypeStruct((t_len, D_MODEL), F32), jax.ShapeDtypeStruct((t_len, D_MODEL), BF),
                   jax.ShapeDtypeStruct((t_len, D_MODEL), BF), jax.ShapeDtypeStruct((t_len, D_MODEL), BF),
                   jax.ShapeDtypeStruct((t_len, D_MODEL), F32)),
        in_specs=[row_spec(D_MODEL), row_spec(GROUP_W), row_spec(GROUP_W), full(w_out), full(g_xattn), full(w_xq), full(g_xq),
                  full(kn), full(v), full(w_xo)],
        out_specs=(row_spec(D_MODEL),) * 5,
        compiler_params=_cparams(("arbitrary",)),
    )(x, mix_r, mix_f, w_out, g_xattn, w_xq, g_xq, kn, v, w_xo)


def _ffn_loss_fwd(h2, g_ffn, w_gate, w_up, w_down, target):
    t_len = h2.shape[0]
    tm = min(ROW_TILE, t_len)

    def body(h2_ref, g_ref, wg_ref, wu_ref, wd_ref, tgt_ref, hn_ref, gate_ref, up_ref, act_ref, dh3_ref, loss_ref):
        @pl.when(pl.program_id(0) == 0)
        def _():
            loss_ref[...] = jnp.zeros_like(loss_ref)

        h2v = h2_ref[...]
        hn = _rms_fwd(h2v, g_ref[...]).astype(BF)
        hn_ref[...] = hn
        gate = jnp.dot(hn, wg_ref[...], preferred_element_type=F32)
        up = jnp.dot(hn, wu_ref[...], preferred_element_type=F32)
        gate_ref[...] = gate.astype(BF)
        up_ref[...] = up.astype(BF)
        act = (gate * _sigmoid(gate) * up).astype(BF)
        act_ref[...] = act
        diff = h2v + jnp.dot(act, wd_ref[...], preferred_element_type=F32) - tgt_ref[...]
        dh3_ref[...] = diff * (1.0 / D_MODEL)
        per_row = jnp.sum(diff * diff, axis=-1, keepdims=True) * (1.0 / D_MODEL)
        loss_ref[...] += 0.5 * jnp.sum(per_row, axis=0, keepdims=True)

    row_spec = lambda w: pl.BlockSpec((tm, w), lambda i: (i, 0))
    full = lambda a: pl.BlockSpec(a.shape, lambda i: (0,) * a.ndim, pipeline_mode=pl.Buffered(1))
    return pl.pallas_call(
        body, name="ffn_loss_fwd", grid=(t_len // tm,),
        out_shape=(jax.ShapeDtypeStruct((t_len, D_MODEL), BF), jax.ShapeDtypeStruct((t_len, D_FF), BF),
                   jax.ShapeDtypeStruct((t_len, D_FF), BF), jax.ShapeDtypeStruct((t_len, D_FF), BF),
                   jax.ShapeDtypeStruct((t_len, D_MODEL), F32), jax.ShapeDtypeStruct((8, LANES), F32)),
        in_specs=[row_spec(D_MODEL), full(g_ffn), full(w_gate), full(w_up), full(w_down), row_spec(D_MODEL)],
        out_specs=(row_spec(D_MODEL), row_spec(D_FF), row_spec(D_FF), row_spec(D_FF), row_spec(D_MODEL),
                   pl.BlockSpec((8, LANES), lambda i: (0, 0))),
        compiler_params=_cparams(("arbitrary",)),
    )(h2, g_ffn, w_gate, w_up, w_down, target)


def _ffn_bwd(dh3, gate, up, h2, g_ffn, w_gate, w_up, w_down):
    t_len = h2.shape[0]
    tm = min(ROW_TILE, t_len)

    def body(dh3_ref, gate_ref, up_ref, h2_ref, g_ref, wg_ref, wu_ref, wd_ref, dgate_ref, dup_ref, dh2_ref, dg_ref):
        @pl.when(pl.program_id(0) == 0)
        def _():
            dg_ref[...] = jnp.zeros_like(dg_ref)

        dh3v = dh3_ref[...]
        dact = _dot_nt(dh3v, wd_ref[...])
        g = gate_ref[...].astype(F32)
        sg = _sigmoid(g)
        dup = (dact * (g * sg)).astype(BF)
        dgate = (dact * up_ref[...].astype(F32) * (sg * (1.0 + g * (1.0 - sg)))).astype(BF)
        dup_ref[...] = dup
        dgate_ref[...] = dgate
        dhn = _dot_nt(dgate, wg_ref[...]) + _dot_nt(dup, wu_ref[...])
        dx, dg = _rms_bwd(h2_ref[...], g_ref[...], dhn)
        dh2_ref[...] = dh3v + dx
        dg_ref[...] += dg

    row_spec = lambda w: pl.BlockSpec((tm, w), lambda i: (i, 0))
    full = lambda a: pl.BlockSpec(a.shape, lambda i: (0,) * a.ndim, pipeline_mode=pl.Buffered(1))
    return pl.pallas_call(
        body, name="ffn_bwd", grid=(t_len // tm,),
        out_shape=(jax.ShapeDtypeStruct((t_len, D_FF), BF), jax.ShapeDtypeStruct((t_len, D_FF), BF),
                   jax.ShapeDtypeStruct((t_len, D_MODEL), F32), jax.ShapeDtypeStruct((1, D_MODEL), F32)),
        in_specs=[row_spec(D_MODEL), row_spec(D_FF), row_spec(D_FF), row_spec(D_MODEL), full(g_ffn), full(w_gate), full(w_up),
                  full(w_down)],
        out_specs=(row_spec(D_FF), row_spec(D_FF), row_spec(D_MODEL), pl.BlockSpec((1, D_MODEL), lambda i: (0, 0))),
        compiler_params=_cparams(("arbitrary",)),
    )(dh3, gate, up, h2, g_ffn, w_gate, w_up, w_down)


def _attn_out_xattn_bwd(dh2, h1, qx, kn, v, w_xo, w_xq, w_out, g_xattn, g_xq):
    t_len = h1.shape[0]
    tm = min(ROW_TILE, t_len)
    m_tok = kn.shape[0]

    def body(dh2_ref, h1_ref, qx_ref, kn_ref, v_ref, wxo_ref, wq_ref, wo_ref, g_ref, gq_ref,
             dqx_ref, dh1_ref, dmr_ref, dmf_ref, dkn_ref, dv_ref, dg_ref, dgq_ref, dqx_scr):
        @pl.when(pl.program_id(0) == 0)
        def _():
            dkn_ref[...] = jnp.zeros_like(dkn_ref)
            dv_ref[...] = jnp.zeros_like(dv_ref)
            dg_ref[...] = jnp.zeros_like(dg_ref)
            dgq_ref[...] = jnp.zeros_like(dgq_ref)

        dh2v = dh2_ref[...]
        do = _dot_nt(dh2v, wxo_ref[...])
        gq = gq_ref[...]
        dgq = jnp.zeros((1, XHD), F32)
        for h in range(N_XH):
            sl = slice(h * XHD, (h + 1) * XHD)
            qraw = qx_ref[:, sl].astype(F32)
            qn = _rms_fwd(qraw, gq)
            p = _softmax_rows(_dot_nt(qn, kn_ref[:, sl]) * (XHD ** -0.5))
            doh = do[:, sl]
            dv_ref[:, sl] += _dot_tn(p, doh)
            dp = _dot_nt(doh, v_ref[:, sl])
            ds = p * (dp - jnp.sum(dp * p, axis=-1, keepdims=True)) * (XHD ** -0.5)
            dqn = _dot(ds, kn_ref[:, sl])
            dkn_ref[:, sl] += _dot_tn(ds, qn)
            dx, dg_h = _rms_bwd(qraw, gq, dqn)
            dgq = dgq + dg_h
            dqx_scr[:, sl] = dx.astype(BF)
        dgq_ref[...] += dgq
        dqx = dqx_scr[...]
        dqx_ref[...] = dqx
        dhn = _dot_nt(dqx, wq_ref[...])
        dx, dg = _rms_bwd(h1_ref[...], g_ref[...], dhn)
        dg_ref[...] += dg
        dh1 = dh2v + dx
        dh1_ref[...] = dh1
        dmix = _dot_nt(dh1, wo_ref[...])
        dmr_ref[...] = dmix[:, :GROUP_W]
        dmf_ref[...] = dmix[:, GROUP_W:].astype(BF)

    row_spec = lambda w: pl.BlockSpec((tm, w), lambda i: (i, 0))
    full = lambda a: pl.BlockSpec(a.shape, lambda i: (0,) * a.ndim)
    acc = lambda r, c: pl.BlockSpec((r, c), lambda i: (0, 0))
    return pl.pallas_call(
        body, name="attn_out_xattn_bwd", grid=(t_len // tm,),
        out_shape=(jax.ShapeDtypeStruct((t_len, D_MODEL), BF), jax.ShapeDtypeStruct((t_len, D_MODEL), F32),
                   jax.ShapeDtypeStruct((t_len, GROUP_W), F32), jax.ShapeDtypeStruct((t_len, GROUP_W), BF),
                   jax.ShapeDtypeStruct((m_tok, D_MODEL), F32), jax.ShapeDtypeStruct((m_tok, D_MODEL), F32),
                   jax.ShapeDtypeStruct((1, D_MODEL), F32), jax.ShapeDtypeStruct((1, XHD), F32)),
        in_specs=[row_spec(D_MODEL), row_spec(D_MODEL), row_spec(D_MODEL), full(kn), full(v), full(w_xo), full(w_xq), full(w_out),
                  full(g_xattn), full(g_xq)],
        out_specs=(row_spec(D_MODEL), row_spec(D_MODEL), row_spec(GROUP_W), row_spec(GROUP_W), acc(m_tok, D_MODEL),
                   acc(m_tok, D_MODEL), acc(1, D_MODEL), acc(1, XHD)),
        scratch_shapes=[pltpu.VMEM((tm, D_MODEL), BF)],
        compiler_params=_cparams(("arbitrary",)),
    )(dh2, h1, qx, kn, v, w_xo, w_xq, w_out, g_xattn, g_xq)


def _mem_kv_bwd(dkn, dv, kraw, mem, memn, g_mem, g_xk, w_xkv):
    m_tok = mem.shape[0]

    def body(dkn_ref, dv_ref, kraw_ref, mem_ref, memn_ref, gm_ref, gk_ref, w_ref, dw_ref, dgm_ref, dgk_ref, dkv_scr):
        gk = gk_ref[...]
        dgk = jnp.zeros((1, XHD), F32)
        for h in range(N_XH):
            sl = slice(h * XHD, (h + 1) * XHD)
            dx, dg_h = _rms_bwd(kraw_ref[:, sl], gk, dkn_ref[:, sl])
            dgk = dgk + dg_h
            dkv_scr[:, sl] = dx.astype(BF)
        dgk_ref[...] = dgk
        dkv_scr[:, D_MODEL:] = dv_ref[...].astype(BF)
        dkv = dkv_scr[...]
        dw_ref[...] = _dot_tn(memn_ref[...], dkv)
        dmemn = _dot_nt(dkv, w_ref[...])
        mem_v = mem_ref[...]
        r = lax.rsqrt(jnp.mean(mem_v * mem_v, axis=-1, keepdims=True) + EPS)
        dgm_ref[...] = jnp.sum(dmemn * mem_v * r, axis=0, keepdims=True)

    return pl.pallas_call(
        body, name="mem_kv_bwd",
        out_shape=(jax.ShapeDtypeStruct((D_MODEL, 2 * D_MODEL), F32), jax.ShapeDtypeStruct((1, D_MODEL), F32),
                   jax.ShapeDtypeStruct((1, XHD), F32)),
        in_specs=[VMEM_SPEC] * 8, out_specs=(VMEM_SPEC,) * 3,
        scratch_shapes=[pltpu.VMEM((m_tok, 2 * D_MODEL), BF)],
        compiler_params=_cparams(),
    )(dkn, dv, kraw, mem, memn, g_mem, g_xk, w_xkv)


def _fox_bwd(fq, fk, proj, dmf, o32, lse_row, f_row, f_col):
    t_len = fq.shape[0]
    tb = min(ATT_BLOCK, t_len)
    n_b = t_len // tb
    v_col = 6 * GROUP_W // LANES

    def body(k_ref, v_ref, q_ref, do_ref, o_ref, lse_ref, fr_ref, fc_ref, dq_ref, dk_ref, dv_ref, df_ref, delta):
        j = pl.program_id(1)

        @pl.when(j == 0)
        def _():
            dq_ref[...] = jnp.zeros_like(dq_ref)
            dd = do_ref[...].astype(F32) * o_ref[...]
            hrow = lax.broadcasted_iota(jnp.int32, (8, LANES), 0)
            lane = lax.broadcasted_iota(jnp.int32, (8, LANES), 1)
            ind = ((lane // HEAD_DIM) == hrow).astype(BF)
            delta[...] = _dot_nt_exact(ind, dd)

        k2, v2 = k_ref[...], v_ref[...]
        krow = j * tb + lax.broadcasted_iota(jnp.int32, (tb, tb), 0)
        qcol0 = lax.broadcasted_iota(jnp.int32, (tb, tb), 1)
        dks, dvs, dfs = [], [], []
        for hh in range(2):
            sl = slice(hh * HEAD_DIM, (hh + 1) * HEAD_DIM)
            k, v = k2[:, sl], v2[:, sl]
            f_k = fc_ref[0][:, hh:hh + 1]

            def step(i, carry, sl=sl, k=k, v=v, f_k=f_k, hh=hh):
                dk, dv, df = carry
                rows = pl.ds(pl.multiple_of(i * tb, tb), tb)
                q = q_ref[rows, :][:, sl]
                do = do_ref[rows, :][:, sl]
                s_t = _dot_nt(k, q) + fr_ref[0, hh:hh + 1, rows] - f_k
                s_t = jnp.where(i * tb + qcol0 >= krow, s_t, NEG)
                p_t = jnp.exp(s_t - lse_ref[0, hh:hh + 1, rows])
                dv = dv + _dot(p_t, do)
                ds_t = p_t * (_dot_nt(v, do) - delta[hh:hh + 1, rows])
                dk = dk + _dot(ds_t, q)
                df = df - jnp.sum(ds_t, axis=-1, keepdims=True)
                dq_ref[rows, sl] += _dot_tn(ds_t, k)
                return dk, dv, df

            dk, dv, df = lax.fori_loop(j, n_b, step, (jnp.zeros((tb, HEAD_DIM), F32), jnp.zeros((tb, HEAD_DIM), F32),
                                                      jnp.zeros((tb, 1), F32)))
            dks.append(dk)
            dvs.append(dv)
            dfs.append(df)
        dk_ref[...] = jnp.concatenate(dks, axis=-1)
        dv_ref[...] = jnp.concatenate(dvs, axis=-1)
        df_ref[0] = jnp.concatenate(dfs, axis=-1)

    blk = lambda col0: pl.BlockSpec((tb, LANES), lambda hp, j: (j, col0 + hp))
    whole = pl.BlockSpec((t_len, LANES), lambda hp, j: (0, hp))
    rows2 = pl.BlockSpec((1, 2, t_len), lambda hp, j: (hp, 0, 0))
    cols2 = pl.BlockSpec((1, tb, 2), lambda hp, j: (hp, j, 0))
    return pl.pallas_call(
        body, name="fox_bwd", grid=(N_HEADS // 2, n_b),
        out_shape=(jax.ShapeDtypeStruct((t_len, GROUP_W), F32), jax.ShapeDtypeStruct((t_len, GROUP_W), F32),
                   jax.ShapeDtypeStruct((t_len, GROUP_W), F32), jax.ShapeDtypeStruct((N_HEADS // 2, t_len, 2), F32)),
        in_specs=[blk(0), blk(v_col), whole, whole, whole, rows2, rows2, cols2],
        out_specs=(whole, blk(0), blk(0), cols2),
        scratch_shapes=[pltpu.VMEM((8, t_len), F32)],
        compiler_params=_cparams(("arbitrary", "arbitrary")),
    )(fk, proj, fq, dmf, o32, lse_row, f_row, f_col)


def _retention_bwd(dmr, raw, proj, g_ret, rq, rk, states, tables):
    t_len = rq.shape[0]
    c = min(RET_BLOCK, t_len)
    n_b = t_len // c
    wdec, qdec, kdec, cdec = tables
    v_col, g_col = 2 * GROUP_W // LANES, 3 * GROUP_W // LANES

    def body(d_ref, raw_ref, rg_ref, g_ref, q_ref, k_ref, v_ref, st_ref, w_ref, qd_ref, kd_ref, cd_ref,
             dq_ref, dk_ref, dv_ref, drg_ref, dg_ref, gstate):
        @pl.when(pl.program_id(1) == 0)
        def _():
            gstate[...] = jnp.zeros_like(gstate)
            dg_ref[...] = jnp.zeros_like(dg_ref)

        d, raw_v, g = d_ref[...], raw_ref[...], g_ref[0]
        gate = rg_ref[...].astype(F32)
        xc = raw_v - _group_mean64(raw_v)
        r = lax.rsqrt(_group_mean64(xc * xc) + EPS)
        xh = xc * r
        sg = _sigmoid(gate)
        drg_ref[...] = d * (xh * g) * (sg * (1.0 + gate * (1.0 - sg)))
        dy = d * (gate * sg)
        dg_ref[0] += jnp.sum(dy * xh, axis=0, keepdims=True)
        dxh = dy * g
        do2 = r * (dxh - _group_mean64(dxh) - xh * _group_mean64(dxh * xh))
        q2, k2, v2 = q_ref[...], k_ref[...], v_ref[...]
        dqs, dks, dvs = [], [], []
        for hh in range(2):
            sl = slice(hh * HEAD_DIM, (hh + 1) * HEAD_DIM)
            q, k, v, do = q2[:, sl], k2[:, sl], v2[:, sl], do2[:, sl].astype(BF)
            w = w_ref[hh]
            a = _dot_nt(q, k) * w
            dm = _dot_nt(do, v) * w
            sp, gs = st_ref[0, 0, hh], gstate[hh]
            qd = q.astype(F32) * qd_ref[hh]
            kd = k.astype(F32) * kd_ref[hh]
            dqs.append(_dot(dm, k) + _dot_nt(do, sp) * qd_ref[hh])
            dks.append(_dot_tn(dm, q) + _dot_nt(v, gs) * kd_ref[hh])
            dvs.append(_dot_tn(a, do) + _dot(kd, gs))
            gstate[hh] = gs * cd_ref[hh] + _dot_tn(qd, do)
        dq_ref[...] = jnp.concatenate(dqs, axis=-1)
        dk_ref[...] = jnp.concatenate(dks, axis=-1)
        dv_ref[...] = jnp.concatenate(dvs, axis=-1)

    blk = lambda col0: pl.BlockSpec((c, LANES), lambda hp, i: (n_b - 1 - i, col0 + hp))
    tab = lambda a: pl.BlockSpec((2,) + a.shape[1:], lambda hp, i: (hp, 0, 0))
    gspec = pl.BlockSpec((1, 1, LANES), lambda hp, i: (hp, 0, 0))
    return pl.pallas_call(
        body, name="retention_bwd", grid=(N_HEADS // 2, n_b),
        out_shape=(jax.ShapeDtypeStruct((t_len, GROUP_W), F32),) * 4 + (jax.ShapeDtypeStruct((N_HEADS // 2, 1, LANES), F32),),
        in_specs=[blk(0), blk(0), blk(g_col), gspec, blk(0), blk(0), blk(v_col),
                  pl.BlockSpec((1, 1, 2, HEAD_DIM, HEAD_DIM), lambda hp, i: (hp, n_b - 1 - i, 0, 0, 0)),
                  tab(wdec), tab(qdec), tab(kdec), tab(cdec)],
        out_specs=(blk(0), blk(0), blk(0), blk(0), gspec),
        scratch_shapes=[pltpu.VMEM((2, HEAD_DIM, HEAD_DIM), F32)],
        compiler_params=_cparams(("arbitrary", "arbitrary")),
    )(dmr, raw, proj, g_ret, rq, rk, proj, states, wdec, qdec, kdec, cdec)


def _in_proj_bwd(x, g_mix, dh1, dq_r, dk_r, dv_r, drg, dq_f, dk_f, dv_f, df_col, proj, z, cos_t, sin_t, gq_t, gk_t, w_main, w_ff):
    t_len = x.shape[0]
    tm = min(ROW_TILE, t_len)
    n_t = t_len // tm

    def body(x_ref, g_ref, dh1_ref, dqr_ref, dkr_ref, dvr_ref, drg_ref, dqf_ref, dkf_ref, dvf_ref, df_ref, fq_ref, fk_ref, z_ref,
             cos_ref, sin_ref, gq_ref, gk_ref, wm_ref, wf_ref,
             dproj_ref, dz_ref, dx_ref, dg_ref, dgq_ref, dgk_ref, db_ref, carry, gq_acc, gk_acc):
        i = pl.program_id(0)

        @pl.when(i == 0)
        def _():
            carry[...] = jnp.zeros_like(carry)
            gq_acc[...] = jnp.zeros_like(gq_acc)
            gk_acc[...] = jnp.zeros_like(gk_acc)
            dg_ref[...] = jnp.zeros_like(dg_ref)
            db_ref[...] = jnp.zeros_like(db_ref)

        c, s = cos_ref[...], sin_ref[...]
        gq, gk = gq_ref[...], gk_ref[...]
        dgq = jnp.zeros((1, LANES), F32)
        dgk = jnp.zeros((1, LANES), F32)
        for sl in _chunks(GROUP_W):
            dy = dqr_ref[:, sl] * 0.125
            dproj_ref[:, sl] = (dy * c + _swap32(dy * s)).astype(BF)
            dy = dkr_ref[:, sl]
            dproj_ref[:, GROUP_W + sl.start:GROUP_W + sl.stop] = (dy * c + _swap32(dy * s)).astype(BF)
            dproj_ref[:, 2 * GROUP_W + sl.start:2 * GROUP_W + sl.stop] = dvr_ref[:, sl].astype(BF)
            dproj_ref[:, 3 * GROUP_W + sl.start:3 * GROUP_W + sl.stop] = drg_ref[:, sl].astype(BF)
            for src, dsrc, gain, off in ((fq_ref, dqf_ref, gq, 4), (fk_ref, dkf_ref, gk, 5)):
                xr = src[:, sl].astype(F32)
                r = lax.rsqrt(_group_mean64(xr * xr) + EPS)
                xh = xr * r
                dy = dsrc[:, sl] * (0.125 if off == 4 else 1.0)
                dgs = jnp.sum(dy * xh, axis=0, keepdims=True)
                if off == 4:
                    dgq = dgq + dgs
                else:
                    dgk = dgk + dgs
                dxh = dy * gain
                dproj_ref[:, off * GROUP_W + sl.start:off * GROUP_W + sl.stop] = \
                    (r * (dxh - xh * _group_mean64(dxh * xh))).astype(BF)
            dproj_ref[:, 6 * GROUP_W + sl.start:6 * GROUP_W + sl.stop] = dvf_ref[:, sl].astype(BF)
        gq_acc[...] += dgq
        gk_acc[...] += dgk
        row = lax.broadcasted_iota(jnp.int32, (tm, tm), 0)
        col = lax.broadcasted_iota(jnp.int32, (tm, tm), 1)
        dlf = _dot_exact((col >= row).astype(BF), df_ref[...]) + carry[0:1, :]
        carry[...] = jnp.broadcast_to(dlf[0:1, :], carry.shape)
        lane = lax.broadcasted_iota(jnp.int32, (tm, LANES), 1)
        dz = jnp.where(lane < N_HEADS, dlf / (1.0 + jnp.exp(z_ref[...])), 0.0)
        db_ref[...] += jnp.sum(dz, axis=0, keepdims=True)
        dz_bf = dz.astype(BF)
        dz_ref[...] = dz_bf
        dn1 = _dot_nt(dz_bf, wf_ref[...])
        for sec in range(MAIN_W // GROUP_W):
            sl = slice(sec * GROUP_W, (sec + 1) * GROUP_W)
            dn1 = dn1 + _dot_nt(dproj_ref[:, sl], wm_ref[:, sl])
        dx, dg = _rms_bwd(x_ref[...], g_ref[...], dn1)
        dx_ref[...] = dh1_ref[...] + dx
        dg_ref[...] += dg

        @pl.when(i == n_t - 1)
        def _():
            dgq_ref[...] = gq_acc[:, :HEAD_DIM] + gq_acc[:, HEAD_DIM:]
            dgk_ref[...] = gk_acc[:, :HEAD_DIM] + gk_acc[:, HEAD_DIM:]

    row_spec = lambda w, col=0: pl.BlockSpec((tm, w), lambda i: (n_t - 1 - i, col))
    full = lambda a: pl.BlockSpec(a.shape, lambda i: (0,) * a.ndim)
    acc = lambda r, c: pl.BlockSpec((r, c), lambda i: (0, 0))
    return pl.pallas_call(
        body, name="in_proj_bwd", grid=(n_t,),
        out_shape=(jax.ShapeDtypeStruct((t_len, MAIN_W), BF), jax.ShapeDtypeStruct((t_len, LANES), BF),
                   jax.ShapeDtypeStruct((t_len, D_MODEL), F32), jax.ShapeDtypeStruct((1, D_MODEL), F32),
                   jax.ShapeDtypeStruct((1, HEAD_DIM), F32), jax.ShapeDtypeStruct((1, HEAD_DIM), F32),
                   jax.ShapeDtypeStruct((1, LANES), F32)),
        in_specs=[row_spec(D_MODEL), full(g_mix), row_spec(D_MODEL)] + [row_spec(GROUP_W)] * 7
        + [row_spec(LANES), row_spec(GROUP_W, 4), row_spec(GROUP_W, 5), row_spec(LANES), row_spec(LANES), row_spec(LANES),
           full(gq_t), full(gk_t), full(w_main), full(w_ff)],
        out_specs=(row_spec(MAIN_W), row_spec(LANES), row_spec(D_MODEL), acc(1, D_MODEL), acc(1, HEAD_DIM), acc(1, HEAD_DIM),
                   acc(1, LANES)),
        scratch_shapes=[pltpu.VMEM((8, LANES), F32), pltpu.VMEM((1, LANES), F32), pltpu.VMEM((1, LANES), F32)],
        compiler_params=_cparams(("arbitrary",)),
    )(x, g_mix, dh1, dq_r, dk_r, dv_r, drg, dq_f, dk_f, dv_f, df_col, proj, proj, z, cos_t, sin_t, gq_t, gk_t, w_main, w_ff)


def _matmul_tn(a, b, name, bm=256, bk=512):
    t_len, m = a.shape
    n = b.shape[1]
    bm, bk = min(bm, m), min(bk, t_len)

    def body(a_ref, b_ref, o_ref):
        @pl.when(pl.program_id(1) == 0)
        def _():
            o_ref[...] = jnp.zeros_like(o_ref)

        o_ref[...] += _dot_tn(a_ref[...], b_ref[...])

    return pl.pallas_call(
        body, name=name, grid=(m // bm, t_len // bk),
        out_shape=jax.ShapeDtypeStruct((m, n), F32),
        in_specs=[pl.BlockSpec((bk, bm), lambda i, k: (k, i)), pl.BlockSpec((bk, n), lambda i, k: (k, 0))],
        out_specs=pl.BlockSpec((bm, n), lambda i, k: (i, 0)),
        compiler_params=_cparams(("arbitrary", "arbitrary")),
    )(a, b)


def _place():
    x, y, c = lax.axis_index("x"), lax.axis_index("y"), lax.axis_index("c")
    chips = [(1 - x, y), (x, 1 - y), (1 - x, 1 - y)]
    return x, y, c, chips


def _row_chunks(rows, limit):
    step = max(d for d in range(16, min(rows, limit) + 1, 16) if rows % d == 0)
    return [slice(i, i + step) for i in range(0, rows, step)]


ICI_CHUNK_ROWS = 128
D2D_CHUNK_ROWS = 64


def _all_gather_weights(shards):
    n_w = len(shards)

    def body(*refs):
        ins, outs = refs[:n_w], refs[n_w:2 * n_w]
        send_sems, recv_sems = refs[2 * n_w:]
        x, y, c, chips = _place()
        me_chip = 2 * x + y
        sibling = (x, y, 1 - c)

        def copy(w, k, slot, half, to, rows=slice(None), src=None):
            dst = outs[w].at[slot, half, rows]
            return pltpu.make_async_remote_copy(src_ref=dst if src is None else src, dst_ref=dst,
                                                send_sem=send_sems.at[w, k], recv_sem=recv_sems.at[w, k],
                                                device_id=to, device_id_type=MESH)

        for w in range(n_w):
            for j, chip in enumerate(chips):
                for rows in _row_chunks(ins[w].shape[1], ICI_CHUNK_ROWS):
                    copy(w, j, me_chip, c, (*chip, c), rows, src=ins[w].at[c, rows]).start()
        for w in range(n_w):
            for j, (px, py) in enumerate(chips):
                copy(w, j, 2 * px + py, c, (x, y, c)).wait_recv()
                for rows in _row_chunks(ins[w].shape[1], D2D_CHUNK_ROWS):
                    copy(w, 3 + j, 2 * px + py, c, sibling, rows).start()
        for w in range(n_w):
            for j, (px, py) in enumerate(chips):
                copy(w, 3 + j, 2 * px + py, 1 - c, (x, y, c)).wait_recv()
        for w in range(n_w):
            for j, (px, py) in enumerate(chips):
                copy(w, j, me_chip, c, (px, py, c), src=ins[w].at[c]).wait_send()
                copy(w, 3 + j, 2 * px + py, c, sibling).wait_send()

    return pl.pallas_call(
        body, name="all_gather_weights",
        out_shape=tuple(jax.ShapeDtypeStruct((4,) + s.shape, s.dtype) for s in shards),
        in_specs=[ANY] * n_w, out_specs=(ANY,) * n_w,
        scratch_shapes=[pltpu.SemaphoreType.DMA((n_w, 6)), pltpu.SemaphoreType.DMA((n_w, 6))],
    )(*shards)


def _exchange_core_halves(grads):
    n_w = len(grads)

    def body(*refs):
        ins, theirs = refs[:n_w], refs[n_w:2 * n_w]
        send_sems, recv_sems = refs[2 * n_w:]
        x, y, c, _ = _place()

        def remote(w, k=slice(None), rows=slice(None)):
            return pltpu.make_async_remote_copy(src_ref=ins[w].at[k, 1 - c, rows], dst_ref=theirs[w].at[k, rows],
                                                send_sem=send_sems.at[w], recv_sem=recv_sems.at[w], device_id=(x, y, 1 - c),
                                                device_id_type=MESH)

        for w in range(n_w):
            for k in range(4):
                for rows in _row_chunks(ins[w].shape[2], D2D_CHUNK_ROWS):
                    remote(w, k, rows).start()
        for w in range(n_w):
            remote(w).wait()

    half = tuple(jax.ShapeDtypeStruct((4,) + g.shape[2:], g.dtype) for g in grads)
    return pl.pallas_call(
        body, name="exchange_core_halves", out_shape=half,
        in_specs=[ANY] * n_w, out_specs=(ANY,) * n_w,
        scratch_shapes=[pltpu.SemaphoreType.DMA((n_w,)), pltpu.SemaphoreType.DMA((n_w,))],
    )(*grads)


def _add_pairs(a, b, name):
    _, r, c = a.shape
    rb = 32 if r % 32 == 0 else r

    def body(a_ref, b_ref, o_ref, ob_ref):
        s = a_ref[...] + b_ref[...]
        o_ref[...] = s
        ob_ref[...] = s.astype(BF)

    spec = pl.BlockSpec((4, rb, c), lambda i: (0, i, 0))
    return pl.pallas_call(
        body, name=name, grid=(r // rb,),
        out_shape=(jax.ShapeDtypeStruct(a.shape, F32), jax.ShapeDtypeStruct(a.shape, BF)),
        in_specs=[spec, spec], out_specs=(spec, spec), compiler_params=_cparams(("arbitrary",)),
    )(a, b)


def _scatter_to_chips(sums_bf16):
    n_w = len(sums_bf16)

    def body(*refs):
        bfs, got = refs[:n_w], refs[n_w:2 * n_w]
        send_sems, recv_sems = refs[2 * n_w:]
        x, y, c, chips = _place()

        def remote(w, j, px, py, rows=slice(None)):
            return pltpu.make_async_remote_copy(src_ref=bfs[w].at[2 * px + py, rows], dst_ref=got[w].at[j, rows],
                                                send_sem=send_sems.at[w, j], recv_sem=recv_sems.at[w, j], device_id=(px, py, c),
                                                device_id_type=MESH)

        for w in range(n_w):
            for j, (px, py) in enumerate(chips):
                for rows in _row_chunks(bfs[w].shape[1], ICI_CHUNK_ROWS):
                    remote(w, j, px, py, rows).start()
        for w in range(n_w):
            for j, (px, py) in enumerate(chips):
                remote(w, j, px, py).wait()

    return pl.pallas_call(
        body, name="scatter_to_chips",
        out_shape=tuple(jax.ShapeDtypeStruct((3,) + s.shape[1:], BF) for s in sums_bf16),
        in_specs=[ANY] * n_w, out_specs=(ANY,) * n_w,
        scratch_shapes=[pltpu.SemaphoreType.DMA((n_w, 3)), pltpu.SemaphoreType.DMA((n_w, 3))],
    )(*sums_bf16)


def _add_received(own, got, name):
    r, c = own.shape
    rb = 32 if r % 32 == 0 else r

    def body(o_ref, g_ref, out_ref):
        out_ref[...] = ((o_ref[...] + g_ref[0].astype(F32)) + g_ref[1].astype(F32)) + g_ref[2].astype(F32)

    return pl.pallas_call(
        body, name=name, grid=(r // rb,), out_shape=jax.ShapeDtypeStruct((r, c), F32),
        in_specs=[pl.BlockSpec((rb, c), lambda i: (i, 0)), pl.BlockSpec((3, rb, c), lambda i: (0, i, 0))],
        out_specs=pl.BlockSpec((rb, c), lambda i: (i, 0)), compiler_params=_cparams(("arbitrary",)),
    )(own, got)


def _share_with_sibling(halves):
    n_w = len(halves)

    def body(*refs):
        ins, outs = refs[:n_w], refs[n_w:2 * n_w]
        send_sems, recv_sems = refs[2 * n_w:]
        x, y, c, _ = _place()

        def remote(w, rows=slice(None)):
            return pltpu.make_async_remote_copy(src_ref=ins[w].at[rows], dst_ref=outs[w].at[c, rows], send_sem=send_sems.at[w],
                                                recv_sem=recv_sems.at[w], device_id=(x, y, 1 - c), device_id_type=MESH)

        for w in range(n_w):
            for rows in _row_chunks(ins[w].shape[0], D2D_CHUNK_ROWS):
                remote(w, rows).start()
        for w in range(n_w):
            remote(w).wait()

    return pl.pallas_call(
        body, name="share_with_sibling",
        out_shape=tuple(jax.ShapeDtypeStruct((2,) + h.shape, h.dtype) for h in halves),
        in_specs=[ANY] * n_w, out_specs=(ANY,) * n_w,
        scratch_shapes=[pltpu.SemaphoreType.DMA((n_w,)), pltpu.SemaphoreType.DMA((n_w,))],
    )(*halves)


def _all_reduce_small(pack):
    r, c = pack.shape

    def body(p_ref, out_ref, slots, send_sems, recv_sems):
        x, y, cc, _ = _place()
        me = 4 * x + 2 * y + cc
        slots[me] = p_ref[...]
        copies = []
        for k in range(1, 8):
            dx, dy, dc = (k >> 2) & 1, (k >> 1) & 1, k & 1
            to = (1 - x if dx else x, 1 - y if dy else y, 1 - cc if dc else cc)
            cp = pltpu.make_async_remote_copy(src_ref=p_ref, dst_ref=slots.at[me], send_sem=send_sems.at[k - 1],
                                              recv_sem=recv_sems.at[k - 1], device_id=to, device_id_type=MESH)
            cp.start()
            copies.append(cp)
        for cp in copies:
            cp.wait()
        total = slots[0]
        for d in range(1, 8):
            total = total + slots[d]
        out_ref[...] = total

    return pl.pallas_call(
        body, name="all_reduce_small", out_shape=jax.ShapeDtypeStruct((r, c), F32),
        in_specs=[VMEM_SPEC], out_specs=VMEM_SPEC,
        scratch_shapes=[pltpu.VMEM((8, r, c), F32), pltpu.SemaphoreType.DMA((7,)), pltpu.SemaphoreType.DMA((7,))],
    )(pack)


def _adamw(w, g, m, v, name):
    r, c = w.shape
    rb = 64 if r % 64 == 0 else r
    c1 = 1.0 - ADAM_B1 ** ADAM_STEP
    c2 = 1.0 - ADAM_B2 ** ADAM_STEP

    def body(w_ref, g_ref, m_ref, v_ref, d_ref, nm_ref, nv_ref):
        gv = g_ref[...]
        nm = ADAM_B1 * m_ref[...] + (1.0 - ADAM_B1) * gv
        nv = ADAM_B2 * v_ref[...] + (1.0 - ADAM_B2) * (gv * gv)
        nm_ref[...] = nm
        nv_ref[...] = nv
        d_ref[...] = -ADAM_LR * ((nm / c1) / (jnp.sqrt(nv / c2) + ADAM_EPS) + ADAM_WD * w_ref[...])

    spec = pl.BlockSpec((rb, c), lambda i: (i, 0))
    return pl.pallas_call(
        body, name=name, grid=(r // rb,), out_shape=(jax.ShapeDtypeStruct((r, c), F32),) * 3,
        in_specs=[spec] * 4, out_specs=(spec,) * 3, compiler_params=_cparams(("arbitrary",)),
    )(w, g, m, v)


def _rope_tables(t_len):
    inv_freq = ROPE_BASE ** (-jnp.arange(0, HEAD_DIM, 2, dtype=F32) / HEAD_DIM)
    ang = jnp.arange(t_len, dtype=F32)[:, None] * inv_freq[None, :]
    cos, sin = jnp.cos(ang), jnp.sin(ang)
    cos_t = jnp.concatenate([cos, cos, cos, cos], axis=-1)
    sin_t = jnp.concatenate([-sin, sin, -sin, sin], axis=-1)
    return cos_t, sin_t


def _cols_to_shards(dw):
    r, n = dw.shape
    return jnp.transpose(dw.reshape(2, r // 2, 4, n // 4), (2, 0, 1, 3))


def _rows_to_shards(dw):
    r, n = dw.shape
    return dw.reshape(4, 2, r // 8, n)


def _pad_lanes(a):
    extra = -a.shape[-1] % LANES
    return a if extra == 0 else jnp.pad(a, [(0, 0)] * (a.ndim - 1) + [(0, extra)])


def _pad_row(a, width=D_MODEL):
    a = a.reshape(1, -1)
    return jnp.pad(a, ((0, 0), (0, width - a.shape[1])))


def kernel(x, mem, g_mix, w_in, b_forget, g_ret_out, g_fox_q, g_fox_k, w_out, g_xattn, w_xq, w_xkv, g_mem, g_xq, g_xk, w_xo, g_ffn, w_gate, w_up, w_down, loss_target, m_g_mix, m_w_in, m_b_forget, m_g_ret_out, m_g_fox_q, m_g_fox_k, m_w_out, m_g_xattn, m_w_xq, m_w_xkv, m_g_mem, m_g_xq, m_g_xk, m_w_xo, m_g_ffn, m_w_gate, m_w_up, m_w_down, v_g_mix, v_w_in, v_b_forget, v_g_ret_out, v_g_fox_q, v_g_fox_k, v_w_out, v_g_xattn, v_w_xq, v_w_xkv, v_g_mem, v_g_xq, v_g_xk, v_w_xo, v_g_ffn, v_w_gate, v_w_up, v_w_down):
    big = {"w_in": (w_in, m_w_in, v_w_in), "w_out": (w_out, m_w_out, v_w_out), "w_xq": (w_xq, m_w_xq, v_w_xq),
           "w_xkv": (w_xkv, m_w_xkv, v_w_xkv), "w_xo": (w_xo, m_w_xo, v_w_xo), "w_gate": (w_gate, m_w_gate, v_w_gate),
           "w_up": (w_up, m_w_up, v_w_up), "w_down": (w_down, m_w_down, v_w_down)}
    big_names = list(big)
    col_sharded = ("w_in", "w_xkv", "w_gate", "w_up")

    shards = []
    for n in big_names:
        w = _pad_lanes(big[n][0][0].astype(BF))
        shards.append(w.reshape(2, w.shape[0] // 2, w.shape[1]))
    gathered = _all_gather_weights(shards)
    my_chip = 2 * lax.axis_index("x") + lax.axis_index("y")
    full = {}
    for n, g, own in zip(big_names, gathered, shards):
        g = lax.dynamic_update_slice(g, own[None], (my_chip, 0, 0, 0))
        _, _, rh, _ = g.shape
        cs = big[n][0].shape[2]
        g = g.reshape(4, 2 * rh, g.shape[3])[:, :, :cs]
        full[n] = jnp.transpose(g, (1, 0, 2)).reshape(2 * rh, 4 * cs) if n in col_sharded else g.reshape(8 * rh, cs)
    small_w = {"g_mix": g_mix, "b_forget": b_forget, "g_ret_out": g_ret_out, "g_fox_q": g_fox_q, "g_fox_k": g_fox_k,
               "g_xattn": g_xattn, "g_mem": g_mem, "g_xq": g_xq, "g_xk": g_xk, "g_ffn": g_ffn}
    m_small = {"g_mix": m_g_mix, "b_forget": m_b_forget, "g_ret_out": m_g_ret_out, "g_fox_q": m_g_fox_q, "g_fox_k": m_g_fox_k,
               "g_xattn": m_g_xattn, "g_mem": m_g_mem, "g_xq": m_g_xq, "g_xk": m_g_xk, "g_ffn": m_g_ffn}
    v_small = {"g_mix": v_g_mix, "b_forget": v_b_forget, "g_ret_out": v_g_ret_out, "g_fox_q": v_g_fox_q, "g_fox_k": v_g_fox_k,
               "g_xattn": v_g_xattn, "g_mem": v_g_mem, "g_xq": v_g_xq, "g_xk": v_g_xk, "g_ffn": v_g_ffn}
    loss_part, grad_x, dw, small_g = _local_step(x[0], mem[0], loss_target[0], full, small_w)
    return _reduce_and_update(big, big_names, col_sharded, dw, small_w, small_g, loss_part, grad_x, m_small, v_small)


def _local_step(xs, mems, tgt, full, small_w):
    g_mix, b_forget, g_ret_out, g_fox_q, g_fox_k = (small_w[n] for n in ("g_mix", "b_forget", "g_ret_out", "g_fox_q", "g_fox_k"))
    g_xattn, g_mem, g_xq, g_xk, g_ffn = (small_w[n] for n in ("g_xattn", "g_mem", "g_xq", "g_xk", "g_ffn"))
    w_main = full["w_in"][:, :MAIN_W]
    w_ff = jnp.pad(full["w_in"][:, MAIN_W:], ((0, 0), (0, LANES - (IN_W - MAIN_W))))
    t_len = xs.shape[0]
    cos_t, sin_t = _rope_tables(t_len)
    tables = _decay_tables(min(RET_BLOCK, t_len))
    gq_t = jnp.concatenate([g_fox_q, g_fox_q], axis=-1)
    gk_t = jnp.concatenate([g_fox_k, g_fox_k], axis=-1)
    b_pad = _pad_row(b_forget, LANES)
    g_ret = g_ret_out.reshape(N_HEADS // 2, 1, LANES)

    memn, kraw, kn, vmem = _mem_kv_fwd(mems, g_mem, full["w_xkv"], g_xk)
    n1, proj, rq, rk, fq, fk, z, fcum = _in_proj_fwd(xs, g_mix, w_main, w_ff, b_pad, cos_t, sin_t, gq_t, gk_t)
    f8 = fcum[:, :N_HEADS]
    f_col = jnp.transpose(f8.reshape(t_len, N_HEADS // 2, 2), (1, 0, 2))
    f_row = jnp.transpose(f8).reshape(N_HEADS // 2, 2, t_len)
    raw, mix_r, states = _retention_fwd(rq, rk, proj, g_ret, tables)
    mix_f, o32, lse = _fox_fwd(fq, fk, proj, f_col, f_row)
    h1, hn2, qx, o_x, h2 = _attn_out_xattn_fwd(xs, mix_r, mix_f, full["w_out"], g_xattn, full["w_xq"], g_xq, kn, vmem, full["w_xo"])
    hn3, gate, up, act, dh3, loss_part = _ffn_loss_fwd(h2, g_ffn, full["w_gate"], full["w_up"], full["w_down"], tgt)

    dgate, dup, dh2, dg_ffn = _ffn_bwd(dh3, gate, up, h2, g_ffn, full["w_gate"], full["w_up"], full["w_down"])
    dqx, dh1, dmr, dmf, dkn, dvm, dg_xattn, dg_xq = _attn_out_xattn_bwd(dh2, h1, qx, kn, vmem, full["w_xo"], full["w_xq"],
                                                                      full["w_out"], g_xattn, g_xq)
    dw_xkv, dg_mem, dg_xk = _mem_kv_bwd(dkn, dvm, kraw, mems, memn, g_mem, g_xk, full["w_xkv"])
    lse_row = jnp.transpose(lse, (0, 2, 1))
    dq_f, dk_f, dv_f, df = _fox_bwd(fq, fk, proj, dmf, o32, lse_row, f_row, f_col)
    dq_r, dk_r, dv_r, drg, dg_ret = _retention_bwd(dmr, raw, proj, g_ret, rq, rk, states, tables)
    df_col = jnp.pad(jnp.transpose(df, (1, 0, 2)).reshape(t_len, N_HEADS), ((0, 0), (0, LANES - N_HEADS)))
    dproj, dz, grad_x, dg_mix, dg_fq, dg_fk, db = _in_proj_bwd(xs, g_mix, dh1, dq_r, dk_r, dv_r, drg, dq_f, dk_f, dv_f, df_col,
                                                              proj, z, cos_t, sin_t, gq_t, gk_t, w_main, w_ff)

    dw = {
        "w_in": jnp.concatenate([_matmul_tn(n1, dproj, "dw_in_main"), _matmul_tn(n1, dz, "dw_in_ff")[:, :IN_W - MAIN_W]], axis=1),
        "w_out": jnp.concatenate([_matmul_tn(mix_r, dh1, "dw_out_ret"), _matmul_tn(mix_f, dh1, "dw_out_fox")], axis=0),
        "w_xq": _matmul_tn(hn2, dqx, "dw_xq"),
        "w_xkv": dw_xkv,
        "w_xo": _matmul_tn(o_x, dh2, "dw_xo"),
        "w_gate": _matmul_tn(hn3, dgate, "dw_gate"),
        "w_up": _matmul_tn(hn3, dup, "dw_up"),
        "w_down": _matmul_tn(act, dh3, "dw_down"),
    }
    small_g = {"g_mix": dg_mix, "b_forget": db[:, :N_HEADS], "g_ret_out": dg_ret, "g_fox_q": dg_fq, "g_fox_k": dg_fk,
               "g_xattn": dg_xattn, "g_mem": dg_mem, "g_xq": dg_xq, "g_xk": dg_xk, "g_ffn": dg_ffn}
    return loss_part, grad_x, dw, small_g


def _reduce_and_update(big, big_names, col_sharded, dw, small_w, small_g, loss_part, grad_x, m_small, v_small):
    parts = [_pad_lanes(_cols_to_shards(dw[n]) if n in col_sharded else _rows_to_shards(dw[n])) for n in big_names]
    my_core = lax.axis_index("c")
    my_chip = 2 * lax.axis_index("x") + lax.axis_index("y")
    theirs = _exchange_core_halves(parts)
    mine = [lax.dynamic_index_in_dim(p, my_core, axis=1, keepdims=False) for p in parts]
    sums = [_add_pairs(a, b, f"core_sum_{n}") for n, a, b in zip(big_names, mine, theirs)]
    got = _scatter_to_chips([s[1] for s in sums])
    own = [lax.dynamic_index_in_dim(s[0], my_chip, axis=0, keepdims=False) for s in sums]
    finals = [_add_received(o, g, f"chip_sum_{n}") for n, o, g in zip(big_names, own, got)]
    shared = _share_with_sibling(finals)
    grads, deltas, new_m, new_v = {}, {}, {}, {}
    for n, s, fin in zip(big_names, shared, finals):
        w, m, v = big[n]
        s = lax.dynamic_update_slice(s, fin[None], (my_core, 0, 0))
        g = s.reshape(w.shape[1], s.shape[2])[:, :w.shape[2]]
        d, nm, nv = _adamw(w[0], g, m[0], v[0], f"adamw_{n}")
        grads[n], deltas[n], new_m[n], new_v[n] = g[None], d[None], nm[None], nv[None]

    small_names = list(small_w)
    pad_rows = SMALL_ROWS - len(small_names) - 1
    stack = lambda d: jnp.concatenate([_pad_row(d[n]) for n in small_names] + [jnp.zeros((pad_rows + 1, D_MODEL), F32)], axis=0)
    g_pack = jnp.concatenate([_pad_row(small_g[n]) for n in small_names] + [_pad_row(loss_part[0:1, 0:1])]
                             + [jnp.zeros((pad_rows, D_MODEL), F32)], axis=0)
    g_tot = _all_reduce_small(g_pack)
    d_s, m_s, v_s = _adamw(stack(small_w), g_tot, stack(m_small), stack(v_small), "adamw_small")
    for i, n in enumerate(small_names):
        shape = small_w[n].shape
        size = int(np.prod(shape))
        grads[n] = g_tot[i, :size].reshape(shape)
        deltas[n], new_m[n], new_v[n] = d_s[i, :size].reshape(shape), m_s[i, :size].reshape(shape), v_s[i, :size].reshape(shape)
    loss = g_tot[len(small_names), 0]

    order = ["g_mix", "w_in", "b_forget", "g_ret_out", "g_fox_q", "g_fox_k", "w_out", "g_xattn", "w_xq", "w_xkv", "g_mem", "g_xq",
             "g_xk", "w_xo", "g_ffn", "w_gate", "w_up", "w_down"]
    return (loss, grad_x[None], *[grads[n] for n in order], *[deltas[n] for n in order], *[new_m[n] for n in order],
            *[new_v[n] for n in order])
```

```python
import functools

import numpy as np
import jax
import jax.numpy as jnp
from jax import lax
from jax.experimental import pallas as pl
from jax.experimental.pallas import tpu as pltpu

F32 = jnp.float32
BF = jnp.bfloat16

D_MODEL = 1024
HEAD_DIM = 64
N_HEADS = 8
GROUP_W = 512
N_XH = 4
XHD = 256
D_FF = 2816
MAIN_W = 3584
IN_W = 3592
ROPE_BASE = 10000.0
LOG2E = 1.4426950408889634
LN2 = 0.6931471805599453
EPS = 1e-6
NEG = -1e30
LANES = 128
RET_BLOCK = 256
REF_CHUNK = 64
ROW_TILE = 256
ATT_BLOCK = 256
SMALL_ROWS = 16
VMEM_LIMIT = 56 * 1024 * 1024

ADAM_LR = 0.001
ADAM_B1 = 0.9
ADAM_B2 = 0.999
ADAM_EPS = 1e-08
ADAM_WD = 0.01
ADAM_STEP = 10

MESH = pl.DeviceIdType.MESH
ANY = pl.BlockSpec(memory_space=pl.ANY)
VMEM_SPEC = pl.BlockSpec(memory_space=pltpu.VMEM)


def _cparams(sem=None, vmem=VMEM_LIMIT):
    return pltpu.CompilerParams(dimension_semantics=sem, vmem_limit_bytes=vmem)


def _dot(a, b):
    return jnp.dot(a.astype(BF), b.astype(BF), preferred_element_type=F32)


def _dot_nt(a, b):
    return lax.dot_general(a.astype(BF), b.astype(BF), (((1,), (1,)), ((), ())), preferred_element_type=F32)


def _dot_tn(a, b):
    return lax.dot_general(a.astype(BF), b.astype(BF), (((0,), (0,)), ((), ())), preferred_element_type=F32)


def _split3(x):
    hi = x.astype(BF)
    r = x - hi.astype(F32)
    mid = r.astype(BF)
    lo = (r - mid.astype(F32)).astype(BF)
    return hi, mid, lo


def _dot_exact(ind, x):
    hi, mid, lo = _split3(x)
    return (jnp.dot(ind, lo, preferred_element_type=F32) + jnp.dot(ind, mid, preferred_element_type=F32)
            + jnp.dot(ind, hi, preferred_element_type=F32))


def _dot_nt_exact(ind, x):
    hi, mid, lo = _split3(x)
    dn = (((1,), (1,)), ((), ()))
    return (lax.dot_general(ind, lo, dn, preferred_element_type=F32) + lax.dot_general(ind, mid, dn, preferred_element_type=F32)
            + lax.dot_general(ind, hi, dn, preferred_element_type=F32))


def _sigmoid(x):
    return 1.0 / (1.0 + jnp.exp(-x))


def _rms_fwd(x, g):
    r = lax.rsqrt(jnp.mean(x * x, axis=-1, keepdims=True) + EPS)
    return x * r * g


def _rms_bwd(x, g, dy):
    r = lax.rsqrt(jnp.mean(x * x, axis=-1, keepdims=True) + EPS)
    xh = x * r
    dg = jnp.sum(dy * xh, axis=0, keepdims=True)
    dxh = dy * g
    dx = r * (dxh - xh * jnp.mean(dxh * xh, axis=-1, keepdims=True))
    return dx, dg


def _group_mean64(x):
    lane = lax.broadcasted_iota(jnp.int32, x.shape, 1)
    lo = lane < HEAD_DIM
    s_lo = jnp.sum(jnp.where(lo, x, 0.0), axis=-1, keepdims=True)
    s_hi = jnp.sum(jnp.where(lo, 0.0, x), axis=-1, keepdims=True)
    return jnp.where(lo, s_lo, s_hi) * (1.0 / HEAD_DIM)


def _swap32(x):
    lane = lax.broadcasted_iota(jnp.int32, x.shape, 1)
    first = (lane % HEAD_DIM) < (HEAD_DIM // 2)
    return jnp.where(first, pltpu.roll(x, LANES - HEAD_DIM // 2, axis=1), pltpu.roll(x, HEAD_DIM // 2, axis=1))


def _chunks(w):
    return [slice(j * LANES, (j + 1) * LANES) for j in range(w // LANES)]


def _aug_pair(qk, f_cols, is_query):
    lane = lax.broadcasted_iota(jnp.int32, (qk.shape[0], HEAD_DIM), 1)
    out = []
    for hh in range(2):
        hi, mid, lo = (p.astype(F32) for p in _split3(f_cols[hh] * LOG2E))
        if is_query:
            aux = jnp.where(lane == 0, hi, jnp.where(lane == 1, mid, jnp.where(lane == 2, lo, jnp.where(lane < 6, 1.0, 0.0))))
        else:
            aux = jnp.where(lane < 3, 1.0, jnp.where(lane == 3, -hi, jnp.where(lane == 4, -mid, jnp.where(lane == 5, -lo, 0.0))))
        out += [qk[:, hh * HEAD_DIM:(hh + 1) * HEAD_DIM], aux]
    return jnp.concatenate(out, axis=-1).astype(BF)


def _mem_kv_fwd(mem, g_mem, w_xkv, g_xk):
    m_tok = mem.shape[0]

    def body(mem_ref, gm_ref, w_ref, gk_ref, memn_ref, kraw_ref, kn_ref, v_ref):
        mn = _rms_fwd(mem_ref[...], gm_ref[...]).astype(BF)
        memn_ref[...] = mn
        kv = jnp.dot(mn, w_ref[...], preferred_element_type=F32)
        k = kv[:, :D_MODEL]
        kraw_ref[...] = k
        v_ref[...] = kv[:, D_MODEL:].astype(BF)
        for h in range(N_XH):
            sl = slice(h * XHD, (h + 1) * XHD)
            kn_ref[:, sl] = _rms_fwd(k[:, sl], gk_ref[...]).astype(BF)

    return pl.pallas_call(
        body, name="mem_kv_fwd",
        out_shape=(jax.ShapeDtypeStruct((m_tok, D_MODEL), BF), jax.ShapeDtypeStruct((m_tok, D_MODEL), F32),
                   jax.ShapeDtypeStruct((m_tok, D_MODEL), BF), jax.ShapeDtypeStruct((m_tok, D_MODEL), BF)),
        in_specs=[VMEM_SPEC] * 4, out_specs=(VMEM_SPEC,) * 4, compiler_params=_cparams(),
    )(mem, g_mem, w_xkv, g_xk)


def _in_proj_fwd(x, g_mix, w_main, w_ff, b_pad, cos_t, sin_t, gq_t, gk_t):
    t_len = x.shape[0]
    tm = min(ROW_TILE, t_len)
    n_t = t_len // tm

    def body(x_ref, g_ref, wm_ref, wf_ref, b_ref, cos_ref, sin_ref, gq_ref, gk_ref,
             n1_ref, proj_ref, rq_ref, rk_ref, qa_ref, ka_ref, z_ref, carry):
        i = pl.program_id(0)

        @pl.when(i == 0)
        def _():
            carry[...] = jnp.zeros_like(carry)

        n1 = _rms_fwd(x_ref[...], g_ref[...]).astype(BF)
        n1_ref[...] = n1
        proj = jnp.dot(n1, wm_ref[...], preferred_element_type=F32)
        proj_ref[...] = proj.astype(BF)
        z = jnp.dot(n1, wf_ref[...], preferred_element_type=F32) + b_ref[...]
        z_ref[...] = z
        lane = lax.broadcasted_iota(jnp.int32, z.shape, 1)
        lf = jnp.where(lane < N_HEADS, jnp.minimum(z, 0.0) - jnp.log(1.0 + jnp.exp(-jnp.abs(z))), 0.0)
        row = lax.broadcasted_iota(jnp.int32, (tm, tm), 0)
        col = lax.broadcasted_iota(jnp.int32, (tm, tm), 1)
        tri = (row >= col).astype(BF)
        fc = _dot_exact(tri, lf) + carry[0:1, :]
        carry[...] = jnp.broadcast_to(fc[tm - 1:tm, :], carry.shape)
        c, s = cos_ref[...], sin_ref[...]
        for j, sl in enumerate(_chunks(GROUP_W)):
            q = proj[:, sl]
            rq_ref[:, sl] = ((q * c + _swap32(q) * s) * 0.125).astype(BF)
            k = proj[:, GROUP_W + j * LANES:GROUP_W + (j + 1) * LANES]
            rk_ref[:, sl] = (k * c + _swap32(k) * s).astype(BF)
            f_cols = [fc[:, 2 * j:2 * j + 1], fc[:, 2 * j + 1:2 * j + 2]]
            fq = proj[:, 4 * GROUP_W + j * LANES:4 * GROUP_W + (j + 1) * LANES]
            fq = fq * lax.rsqrt(_group_mean64(fq * fq) + EPS) * gq_ref[...] * (0.125 * LOG2E)
            qa_ref[:, 2 * j * LANES:2 * (j + 1) * LANES] = _aug_pair(fq, f_cols, True)
            fk = proj[:, 5 * GROUP_W + j * LANES:5 * GROUP_W + (j + 1) * LANES]
            fk = fk * lax.rsqrt(_group_mean64(fk * fk) + EPS) * gk_ref[...]
            ka_ref[:, 2 * j * LANES:2 * (j + 1) * LANES] = _aug_pair(fk, f_cols, False)

    row_spec = lambda w: pl.BlockSpec((tm, w), lambda i: (i, 0))
    full = lambda a: pl.BlockSpec(a.shape, lambda i: (0,) * a.ndim)
    return pl.pallas_call(
        body, name="in_proj_fwd", grid=(n_t,),
        out_shape=(jax.ShapeDtypeStruct((t_len, D_MODEL), BF), jax.ShapeDtypeStruct((t_len, MAIN_W), BF),
                   jax.ShapeDtypeStruct((t_len, GROUP_W), BF), jax.ShapeDtypeStruct((t_len, GROUP_W), BF),
                   jax.ShapeDtypeStruct((t_len, 2 * GROUP_W), BF), jax.ShapeDtypeStruct((t_len, 2 * GROUP_W), BF),
                   jax.ShapeDtypeStruct((t_len, LANES), F32)),
        in_specs=[row_spec(D_MODEL), full(g_mix), full(w_main), full(w_ff), full(b_pad), row_spec(LANES), row_spec(LANES),
                  full(gq_t), full(gk_t)],
        out_specs=(row_spec(D_MODEL), row_spec(MAIN_W), row_spec(GROUP_W), row_spec(GROUP_W), row_spec(2 * GROUP_W),
                   row_spec(2 * GROUP_W), row_spec(LANES)),
        scratch_shapes=[pltpu.VMEM((8, LANES), F32)],
        compiler_params=_cparams(("arbitrary",)),
    )(x, g_mix, w_main, w_ff, b_pad, cos_t, sin_t, gq_t, gk_t)


def _decay_tables(c):
    h = np.arange(N_HEADS, dtype=np.float64)
    lg = np.log(1.0 - 2.0 ** (-5.0 - h)).astype(np.float32).astype(np.float64)
    t = np.arange(c)
    same_or_earlier = (t[None, :] // REF_CHUNK) <= (t[:, None] // REF_CHUNK)
    w = np.where(same_or_earlier[None], np.exp(lg[:, None, None] * np.abs(t[:, None] - t[None, :])[None]), 0.0)
    qd = np.exp(lg[:, None] * (t[None, :] + 1.0))
    kd = np.exp(lg[:, None] * (c - 1.0 - t[None, :]))
    cd = np.exp(lg * c)
    ones = np.ones((1, 1, HEAD_DIM))
    return (jnp.asarray(w, F32), jnp.asarray(qd[:, :, None] * ones, F32), jnp.asarray(kd[:, :, None] * ones, F32),
            jnp.asarray(cd[:, None, None] * np.ones((1, HEAD_DIM, HEAD_DIM)), F32))


def _retention_fwd(rq, rk, proj, g_ret, tables):
    t_len = rq.shape[0]
    c = min(RET_BLOCK, t_len)
    n_b = t_len // c
    wdec, qdec, kdec, cdec = tables
    v_col, g_col = 2 * GROUP_W // LANES, 3 * GROUP_W // LANES

    def body(q_ref, k_ref, v_ref, rg_ref, g_ref, w_ref, qd_ref, kd_ref, cd_ref, raw_ref, mix_ref, st_ref, state):
        i = pl.program_id(1)

        @pl.when(i == 0)
        def _():
            state[...] = jnp.zeros_like(state)

        q2, k2, v2 = q_ref[...], k_ref[...], v_ref[...]
        outs = []
        for hh in range(2):
            sl = slice(hh * HEAD_DIM, (hh + 1) * HEAD_DIM)
            q, k, v = q2[:, sl], k2[:, sl], v2[:, sl]
            sp = state[hh]
            st_ref[0, 0, hh] = sp
            a = _dot_nt(q, k) * w_ref[hh]
            o = _dot(a, v) + _dot(q.astype(F32) * qd_ref[hh], sp)
            state[hh] = sp * cd_ref[hh] + _dot_tn(k.astype(F32) * kd_ref[hh], v)
            outs.append(o)
        o2 = jnp.concatenate(outs, axis=-1)
        raw_ref[...] = o2
        xc = o2 - _group_mean64(o2)
        xh = xc * lax.rsqrt(_group_mean64(xc * xc) + EPS)
        gate = rg_ref[...].astype(F32)
        mix_ref[...] = (gate * _sigmoid(gate) * (xh * g_ref[0])).astype(BF)

    blk = lambda col0: pl.BlockSpec((c, LANES), lambda hp, i: (i, col0 + hp))
    tab = lambda a: pl.BlockSpec((2,) + a.shape[1:], lambda hp, i: (hp, 0, 0))
    return pl.pallas_call(
        body, name="retention_fwd", grid=(N_HEADS // 2, n_b),
        out_shape=(jax.ShapeDtypeStruct((t_len, GROUP_W), F32), jax.ShapeDtypeStruct((t_len, GROUP_W), BF),
                   jax.ShapeDtypeStruct((N_HEADS // 2, n_b, 2, HEAD_DIM, HEAD_DIM), F32)),
        in_specs=[blk(0), blk(0), blk(v_col), blk(g_col), pl.BlockSpec((1, 1, LANES), lambda hp, i: (hp, 0, 0)),
                  tab(wdec), tab(qdec), tab(kdec), tab(cdec)],
        out_specs=(blk(0), blk(0), pl.BlockSpec((1, 1, 2, HEAD_DIM, HEAD_DIM), lambda hp, i: (hp, i, 0, 0, 0))),
        scratch_shapes=[pltpu.VMEM((2, HEAD_DIM, HEAD_DIM), F32)],
        compiler_params=_cparams(("arbitrary", "arbitrary")),
    )(rq, rk, proj, proj, g_ret, wdec, qdec, kdec, cdec)


def _fox_fwd(q_aug, k_aug, proj):
    t_len = q_aug.shape[0]
    tq = min(ATT_BLOCK, t_len)
    n_q = t_len // tq
    v_col = 6 * GROUP_W // LANES
    tc = min(512, t_len)

    def body(q_ref, k_ref, v_ref, o_ref, o32_ref, lse_ref, vt):
        i = pl.program_id(1)

        @pl.when(i == 0)
        def _():
            for c0 in range(0, t_len, tc):
                vt[:, c0:c0 + tc] = v_ref[c0:c0 + tc, :].T

        qs = [q_ref[:, hh * LANES:(hh + 1) * LANES] for hh in range(2)]
        ones = jnp.ones((HEAD_DIM, tq), BF)

        def block(j, carry, masked):
            rows = pl.ds(pl.multiple_of(j * tq, tq), tq)
            k2 = k_ref[rows, :]
            v2 = vt[:, rows]
            ss = [_dot_nt(k2[:, hh * LANES:(hh + 1) * LANES], qs[hh]) for hh in range(2)]
            ps, stats = [], []
            for hh in range(2):
                m = carry[hh][0]
                s_t = ss[hh]
                if masked:
                    krow = lax.broadcasted_iota(jnp.int32, (tq, tq), 0)
                    qcol = lax.broadcasted_iota(jnp.int32, (tq, tq), 1)
                    s_t = jnp.where(qcol >= krow, s_t, NEG)
                m_new = jnp.maximum(m, jnp.max(s_t, axis=0, keepdims=True))
                ps.append(jnp.exp2(s_t - m_new).astype(BF))
                stats.append((m_new, jnp.exp2(m - m_new)))
            out = []
            for hh in range(2):
                m_new, alpha = stats[hh]
                v_aug = jnp.concatenate([v2[hh * HEAD_DIM:(hh + 1) * HEAD_DIM, :], ones], axis=0)
                out.append((m_new, carry[hh][1] * alpha + jnp.dot(v_aug, ps[hh], preferred_element_type=F32)))
            return tuple(out)

        init = tuple((jnp.full((1, tq), NEG, F32), jnp.zeros((LANES, tq), F32)) for _ in range(2))
        carry = lax.fori_loop(0, i, lambda j, c: block(j, c, False), init)
        carry = block(i, carry, True)
        outs, lses = [], []
        for hh in range(2):
            m, acc = carry[hh]
            l = acc[HEAD_DIM:HEAD_DIM + 1, :]
            outs.append(acc[:HEAD_DIM, :] / l)
            lses.append(m + jnp.log2(l))
        o2 = jnp.concatenate(outs, axis=0).T
        o32_ref[...] = o2
        o_ref[...] = o2.astype(BF)
        lse_ref[0] = jnp.concatenate(lses, axis=0)

    return pl.pallas_call(
        body, name="fox_fwd", grid=(N_HEADS // 2, n_q),
        out_shape=(jax.ShapeDtypeStruct((t_len, GROUP_W), BF), jax.ShapeDtypeStruct((t_len, GROUP_W), F32),
                   jax.ShapeDtypeStruct((N_HEADS // 2, 2, t_len), F32)),
        in_specs=[pl.BlockSpec((tq, 2 * LANES), lambda hp, i: (i, hp)),
                  pl.BlockSpec((t_len, 2 * LANES), lambda hp, i: (0, hp)),
                  pl.BlockSpec((t_len, LANES), lambda hp, i: (0, v_col + hp))],
        out_specs=(pl.BlockSpec((tq, LANES), lambda hp, i: (i, hp)), pl.BlockSpec((tq, LANES), lambda hp, i: (i, hp)),
                   pl.BlockSpec((1, 2, tq), lambda hp, i: (hp, 0, i))),
        scratch_shapes=[pltpu.VMEM((LANES, t_len), BF)],
        compiler_params=_cparams(("arbitrary", "arbitrary")),
    )(q_aug, k_aug, proj)


def _softmax_rows(s):
    p = jnp.exp(s - jnp.max(s, axis=-1, keepdims=True))
    return p / jnp.sum(p, axis=-1, keepdims=True)


def _attn_out_xattn_fwd(x, mix_r, mix_f, w_out, g_xattn, w_xq, g_xq, kn, v, w_xo):
    t_len = x.shape[0]
    tm = min(ROW_TILE, t_len)

    def body(x_ref, mr_ref, mf_ref, wo_ref, g_ref, wq_ref, gq_ref, kn_ref, v_ref, wxo_ref,
             h1_ref, hn_ref, qx_ref, o_ref, h2_ref):
        h1 = x_ref[...] + jnp.dot(mr_ref[...], wo_ref[:GROUP_W, :], preferred_element_type=F32) \
            + jnp.dot(mf_ref[...], wo_ref[GROUP_W:, :], preferred_element_type=F32)
        h1_ref[...] = h1
        hn = _rms_fwd(h1, g_ref[...]).astype(BF)
        hn_ref[...] = hn
        qx = jnp.dot(hn, wq_ref[...], preferred_element_type=F32).astype(BF)
        qx_ref[...] = qx
        for h in range(N_XH):
            sl = slice(h * XHD, (h + 1) * XHD)
            qn = _rms_fwd(qx[:, sl].astype(F32), gq_ref[...])
            p = _softmax_rows(_dot_nt(qn, kn_ref[:, sl]) * (XHD ** -0.5))
            o_ref[:, sl] = _dot(p, v_ref[:, sl]).astype(BF)
        h2_ref[...] = h1 + jnp.dot(o_ref[...], wxo_ref[...], preferred_element_type=F32)

    row_spec = lambda w: pl.BlockSpec((tm, w), lambda i: (i, 0))
    full = lambda a: pl.BlockSpec(a.shape, lambda i: (0,) * a.ndim)
    return pl.pallas_call(
        body, name="attn_out_xattn_fwd", grid=(t_len // tm,),
        out_shape=(jax.ShapeDtypeStruct((t_len, D_MODEL), F32), jax.ShapeDtypeStruct((t_len, D_MODEL), BF),
                   jax.ShapeDtypeStruct((t_len, D_MODEL), BF), jax.ShapeDtypeStruct((t_len, D_MODEL), BF),
                   jax.ShapeDtypeStruct((t_len, D_MODEL), F32)),
        in_specs=[row_spec(D_MODEL), row_spec(GROUP_W), row_spec(GROUP_W), full(w_out), full(g_xattn), full(w_xq), full(g_xq),
                  full(kn), full(v), full(w_xo)],
        out_specs=(row_spec(D_MODEL),) * 5,
        compiler_params=_cparams(("arbitrary",)),
    )(x, mix_r, mix_f, w_out, g_xattn, w_xq, g_xq, kn, v, w_xo)


def _ffn_loss_fwd(h2, g_ffn, w_gate, w_up, w_down, target):
    t_len = h2.shape[0]
    tm = min(ROW_TILE, t_len)

    def body(h2_ref, g_ref, wg_ref, wu_ref, wd_ref, tgt_ref, hn_ref, gate_ref, up_ref, act_ref, dh3_ref, loss_ref):
        @pl.when(pl.program_id(0) == 0)
        def _():
            loss_ref[...] = jnp.zeros_like(loss_ref)

        h2v = h2_ref[...]
        hn = _rms_fwd(h2v, g_ref[...]).astype(BF)
        hn_ref[...] = hn
        gate = jnp.dot(hn, wg_ref[...], preferred_element_type=F32)
        up = jnp.dot(hn, wu_ref[...], preferred_element_type=F32)
        gate_ref[...] = gate.astype(BF)
        up_ref[...] = up.astype(BF)
        act = (gate * _sigmoid(gate) * up).astype(BF)
        act_ref[...] = act
        diff = h2v + jnp.dot(act, wd_ref[...], preferred_element_type=F32) - tgt_ref[...]
        dh3_ref[...] = diff * (1.0 / D_MODEL)
        per_row = jnp.sum(diff * diff, axis=-1, keepdims=True) * (1.0 / D_MODEL)
        loss_ref[...] += 0.5 * jnp.sum(per_row, axis=0, keepdims=True)

    row_spec = lambda w: pl.BlockSpec((tm, w), lambda i: (i, 0))
    full = lambda a: pl.BlockSpec(a.shape, lambda i: (0,) * a.ndim, pipeline_mode=pl.Buffered(1))
    return pl.pallas_call(
        body, name="ffn_loss_fwd", grid=(t_len // tm,),
        out_shape=(jax.ShapeDtypeStruct((t_len, D_MODEL), BF), jax.ShapeDtypeStruct((t_len, D_FF), BF),
                   jax.ShapeDtypeStruct((t_len, D_FF), BF), jax.ShapeDtypeStruct((t_len, D_FF), BF),
                   jax.ShapeDtypeStruct((t_len, D_MODEL), F32), jax.ShapeDtypeStruct((8, LANES), F32)),
        in_specs=[row_spec(D_MODEL), full(g_ffn), full(w_gate), full(w_up), full(w_down), row_spec(D_MODEL)],
        out_specs=(row_spec(D_MODEL), row_spec(D_FF), row_spec(D_FF), row_spec(D_FF), row_spec(D_MODEL),
                   pl.BlockSpec((8, LANES), lambda i: (0, 0))),
        compiler_params=_cparams(("arbitrary",)),
    )(h2, g_ffn, w_gate, w_up, w_down, target)


def _ffn_bwd(dh3, gate, up, h2, g_ffn, w_gate, w_up, w_down):
    t_len = h2.shape[0]
    tm = min(ROW_TILE, t_len)

    def body(dh3_ref, gate_ref, up_ref, h2_ref, g_ref, wg_ref, wu_ref, wd_ref, dgate_ref, dup_ref, dh2_ref, dg_ref):
        @pl.when(pl.program_id(0) == 0)
        def _():
            dg_ref[...] = jnp.zeros_like(dg_ref)

        dh3v = dh3_ref[...]
        dact = _dot_nt(dh3v, wd_ref[...])
        g = gate_ref[...].astype(F32)
        sg = _sigmoid(g)
        dup = (dact * (g * sg)).astype(BF)
        dgate = (dact * up_ref[...].astype(F32) * (sg * (1.0 + g * (1.0 - sg)))).astype(BF)
        dup_ref[...] = dup
        dgate_ref[...] = dgate
        dhn = _dot_nt(dgate, wg_ref[...]) + _dot_nt(dup, wu_ref[...])
        dx, dg = _rms_bwd(h2_ref[...], g_ref[...], dhn)
        dh2_ref[...] = dh3v + dx
        dg_ref[...] += dg

    row_spec = lambda w: pl.BlockSpec((tm, w), lambda i: (i, 0))
    full = lambda a: pl.BlockSpec(a.shape, lambda i: (0,) * a.ndim, pipeline_mode=pl.Buffered(1))
    return pl.pallas_call(
        body, name="ffn_bwd", grid=(t_len // tm,),
        out_shape=(jax.ShapeDtypeStruct((t_len, D_FF), BF), jax.ShapeDtypeStruct((t_len, D_FF), BF),
                   jax.ShapeDtypeStruct((t_len, D_MODEL), F32), jax.ShapeDtypeStruct((1, D_MODEL), F32)),
        in_specs=[row_spec(D_MODEL), row_spec(D_FF), row_spec(D_FF), row_spec(D_MODEL), full(g_ffn), full(w_gate), full(w_up),
                  full(w_down)],
        out_specs=(row_spec(D_FF), row_spec(D_FF), row_spec(D_MODEL), pl.BlockSpec((1, D_MODEL), lambda i: (0, 0))),
        compiler_params=_cparams(("arbitrary",)),
    )(dh3, gate, up, h2, g_ffn, w_gate, w_up, w_down)


def _attn_out_xattn_bwd(dh2, h1, qx, kn, v, w_xo, w_xq, w_out, g_xattn, g_xq):
    t_len = h1.shape[0]
    tm = min(ROW_TILE, t_len)
    m_tok = kn.shape[0]

    def body(dh2_ref, h1_ref, qx_ref, kn_ref, v_ref, wxo_ref, wq_ref, wo_ref, g_ref, gq_ref,
             dqx_ref, dh1_ref, dmr_ref, dmf_ref, dkn_ref, dv_ref, dg_ref, dgq_ref, dqx_scr):
        @pl.when(pl.program_id(0) == 0)
        def _():
            dkn_ref[...] = jnp.zeros_like(dkn_ref)
            dv_ref[...] = jnp.zeros_like(dv_ref)
            dg_ref[...] = jnp.zeros_like(dg_ref)
            dgq_ref[...] = jnp.zeros_like(dgq_ref)

        dh2v = dh2_ref[...]
        do = _dot_nt(dh2v, wxo_ref[...])
        gq = gq_ref[...]
        dgq = jnp.zeros((1, XHD), F32)
        for h in range(N_XH):
            sl = slice(h * XHD, (h + 1) * XHD)
            qraw = qx_ref[:, sl].astype(F32)
            qn = _rms_fwd(qraw, gq)
            p = _softmax_rows(_dot_nt(qn, kn_ref[:, sl]) * (XHD ** -0.5))
            doh = do[:, sl]
            dv_ref[:, sl] += _dot_tn(p, doh)
            dp = _dot_nt(doh, v_ref[:, sl])
            ds = p * (dp - jnp.sum(dp * p, axis=-1, keepdims=True)) * (XHD ** -0.5)
            dqn = _dot(ds, kn_ref[:, sl])
            dkn_ref[:, sl] += _dot_tn(ds, qn)
            dx, dg_h = _rms_bwd(qraw, gq, dqn)
            dgq = dgq + dg_h
            dqx_scr[:, sl] = dx.astype(BF)
        dgq_ref[...] += dgq
        dqx = dqx_scr[...]
        dqx_ref[...] = dqx
        dhn = _dot_nt(dqx, wq_ref[...])
        dx, dg = _rms_bwd(h1_ref[...], g_ref[...], dhn)
        dg_ref[...] += dg
        dh1 = dh2v + dx
        dh1_ref[...] = dh1
        dmix = _dot_nt(dh1, wo_ref[...])
        dmr_ref[...] = dmix[:, :GROUP_W]
        dmf_ref[...] = dmix[:, GROUP_W:].astype(BF)

    row_spec = lambda w: pl.BlockSpec((tm, w), lambda i: (i, 0))
    full = lambda a: pl.BlockSpec(a.shape, lambda i: (0,) * a.ndim)
    acc = lambda r, c: pl.BlockSpec((r, c), lambda i: (0, 0))
    return pl.pallas_call(
        body, name="attn_out_xattn_bwd", grid=(t_len // tm,),
        out_shape=(jax.ShapeDtypeStruct((t_len, D_MODEL), BF), jax.ShapeDtypeStruct((t_len, D_MODEL), F32),
                   jax.ShapeDtypeStruct((t_len, GROUP_W), F32), jax.ShapeDtypeStruct((t_len, GROUP_W), BF),
                   jax.ShapeDtypeStruct((m_tok, D_MODEL), F32), jax.ShapeDtypeStruct((m_tok, D_MODEL), F32),
                   jax.ShapeDtypeStruct((1, D_MODEL), F32), jax.ShapeDtypeStruct((1, XHD), F32)),
        in_specs=[row_spec(D_MODEL), row_spec(D_MODEL), row_spec(D_MODEL), full(kn), full(v), full(w_xo), full(w_xq), full(w_out),
                  full(g_xattn), full(g_xq)],
        out_specs=(row_spec(D_MODEL), row_spec(D_MODEL), row_spec(GROUP_W), row_spec(GROUP_W), acc(m_tok, D_MODEL),
                   acc(m_tok, D_MODEL), acc(1, D_MODEL), acc(1, XHD)),
        scratch_shapes=[pltpu.VMEM((tm, D_MODEL), BF)],
        compiler_params=_cparams(("arbitrary",)),
    )(dh2, h1, qx, kn, v, w_xo, w_xq, w_out, g_xattn, g_xq)


def _mem_kv_bwd(dkn, dv, kraw, mem, memn, g_mem, g_xk, w_xkv):
    m_tok = mem.shape[0]

    def body(dkn_ref, dv_ref, kraw_ref, mem_ref, memn_ref, gm_ref, gk_ref, w_ref, dw_ref, dgm_ref, dgk_ref, dkv_scr):
        gk = gk_ref[...]
        dgk = jnp.zeros((1, XHD), F32)
        for h in range(N_XH):
            sl = slice(h * XHD, (h + 1) * XHD)
            dx, dg_h = _rms_bwd(kraw_ref[:, sl], gk, dkn_ref[:, sl])
            dgk = dgk + dg_h
            dkv_scr[:, sl] = dx.astype(BF)
        dgk_ref[...] = dgk
        dkv_scr[:, D_MODEL:] = dv_ref[...].astype(BF)
        dkv = dkv_scr[...]
        dw_ref[...] = _dot_tn(memn_ref[...], dkv)
        dmemn = _dot_nt(dkv, w_ref[...])
        mem_v = mem_ref[...]
        r = lax.rsqrt(jnp.mean(mem_v * mem_v, axis=-1, keepdims=True) + EPS)
        dgm_ref[...] = jnp.sum(dmemn * mem_v * r, axis=0, keepdims=True)

    return pl.pallas_call(
        body, name="mem_kv_bwd",
        out_shape=(jax.ShapeDtypeStruct((D_MODEL, 2 * D_MODEL), F32), jax.ShapeDtypeStruct((1, D_MODEL), F32),
                   jax.ShapeDtypeStruct((1, XHD), F32)),
        in_specs=[VMEM_SPEC] * 8, out_specs=(VMEM_SPEC,) * 3,
        scratch_shapes=[pltpu.VMEM((m_tok, 2 * D_MODEL), BF)],
        compiler_params=_cparams(),
    )(dkn, dv, kraw, mem, memn, g_mem, g_xk, w_xkv)


def _fox_bwd(q_aug, k_aug, proj, dmf, o32, lse):
    t_len = q_aug.shape[0]
    tb = min(ATT_BLOCK, t_len)
    n_b = t_len // tb
    v_col = 6 * GROUP_W // LANES

    def body(k_ref, v_ref, q_ref, do_ref, o_ref, lse_ref, dq_ref, dk_ref, dv_ref, df_ref, delta):
        j = pl.program_id(1)

        @pl.when(j == 0)
        def _():
            dq_ref[...] = jnp.zeros_like(dq_ref)
            dd = do_ref[...].astype(F32) * o_ref[...]
            hrow = lax.broadcasted_iota(jnp.int32, (8, LANES), 0)
            lane = lax.broadcasted_iota(jnp.int32, (8, LANES), 1)
            ind = ((lane // HEAD_DIM) == hrow).astype(BF)
            delta[...] = _dot_nt_exact(ind, dd)

        k2, v2 = k_ref[...], v_ref[...]
        ks = [k2[:, hh * LANES:(hh + 1) * LANES] for hh in range(2)]
        vs = [v2[:, hh * HEAD_DIM:(hh + 1) * HEAD_DIM] for hh in range(2)]

        def block(i, carry, masked):
            rows = pl.ds(pl.multiple_of(i * tb, tb), tb)
            q2 = q_ref[rows, :]
            do2 = do_ref[rows, :]
            qs = [q2[:, hh * LANES:(hh + 1) * LANES] for hh in range(2)]
            dos = [do2[:, hh * HEAD_DIM:(hh + 1) * HEAD_DIM] for hh in range(2)]
            ss = [_dot_nt(ks[hh], qs[hh]) for hh in range(2)]
            dps = [_dot_nt(vs[hh], dos[hh]) for hh in range(2)]
            pts, dsts, dfs = [], [], []
            for hh in range(2):
                s_t = ss[hh]
                if masked:
                    krow = lax.broadcasted_iota(jnp.int32, (tb, tb), 0)
                    qcol = lax.broadcasted_iota(jnp.int32, (tb, tb), 1)
                    s_t = jnp.where(qcol >= krow, s_t, NEG)
                p_t = jnp.exp2(s_t - lse_ref[0, hh:hh + 1, rows])
                pts.append(p_t.astype(BF))
                ds_t = p_t * (dps[hh] - delta[hh:hh + 1, rows])
                dsts.append(ds_t.astype(BF))
                dfs.append(jnp.sum(ds_t, axis=-1, keepdims=True))
            out = []
            for hh in range(2):
                dk, dv, df = carry[hh]
                dv = dv + jnp.dot(pts[hh], dos[hh], preferred_element_type=F32)
                dk = dk + jnp.dot(dsts[hh], qs[hh], preferred_element_type=F32)
                dq_ref[rows, hh * HEAD_DIM:(hh + 1) * HEAD_DIM] += _dot_tn(dsts[hh], ks[hh])[:, :HEAD_DIM]
                out.append((dk, dv, df - dfs[hh]))
            return tuple(out)

        init = tuple((jnp.zeros((tb, LANES), F32), jnp.zeros((tb, HEAD_DIM), F32), jnp.zeros((tb, 1), F32)) for _ in range(2))
        carry = block(j, init, True)
        carry = lax.fori_loop(j + 1, n_b, lambda i, c: block(i, c, False), carry)
        dk_ref[...] = jnp.concatenate([carry[hh][0][:, :HEAD_DIM] for hh in range(2)], axis=-1) * LN2
        dv_ref[...] = jnp.concatenate([carry[hh][1] for hh in range(2)], axis=-1)
        df_ref[0] = jnp.concatenate([carry[hh][2] for hh in range(2)], axis=-1)

    blk = lambda w, col0: pl.BlockSpec((tb, w), lambda hp, j: (j, col0 + hp))
    whole = lambda w: pl.BlockSpec((t_len, w), lambda hp, j: (0, hp))
    rows2 = pl.BlockSpec((1, 2, t_len), lambda hp, j: (hp, 0, 0))
    cols2 = pl.BlockSpec((1, tb, 2), lambda hp, j: (hp, j, 0))
    return pl.pallas_call(
        body, name="fox_bwd", grid=(N_HEADS // 2, n_b),
        out_shape=(jax.ShapeDtypeStruct((t_len, GROUP_W), F32), jax.ShapeDtypeStruct((t_len, GROUP_W), F32),
                   jax.ShapeDtypeStruct((t_len, GROUP_W), F32), jax.ShapeDtypeStruct((N_HEADS // 2, t_len, 2), F32)),
        in_specs=[blk(2 * LANES, 0), blk(LANES, v_col), whole(2 * LANES), whole(LANES), whole(LANES), rows2],
        out_specs=(whole(LANES), blk(LANES, 0), blk(LANES, 0), cols2),
        scratch_shapes=[pltpu.VMEM((8, t_len), F32)],
        compiler_params=_cparams(("arbitrary", "arbitrary")),
    )(k_aug, proj, q_aug, dmf, o32, lse)


def _retention_bwd(dmr, raw, proj, g_ret, rq, rk, states, tables):
    t_len = rq.shape[0]
    c = min(RET_BLOCK, t_len)
    n_b = t_len // c
    wdec, qdec, kdec, cdec = tables
    v_col, g_col = 2 * GROUP_W // LANES, 3 * GROUP_W // LANES

    def body(d_ref, raw_ref, rg_ref, g_ref, q_ref, k_ref, v_ref, st_ref, w_ref, qd_ref, kd_ref, cd_ref,
             dq_ref, dk_ref, dv_ref, drg_ref, dg_ref, gstate):
        @pl.when(pl.program_id(1) == 0)
        def _():
            gstate[...] = jnp.zeros_like(gstate)
            dg_ref[...] = jnp.zeros_like(dg_ref)

        d, raw_v, g = d_ref[...], raw_ref[...], g_ref[0]
        gate = rg_ref[...].astype(F32)
        xc = raw_v - _group_mean64(raw_v)
        r = lax.rsqrt(_group_mean64(xc * xc) + EPS)
        xh = xc * r
        sg = _sigmoid(gate)
        drg_ref[...] = d * (xh * g) * (sg * (1.0 + gate * (1.0 - sg)))
        dy = d * (gate * sg)
        dg_ref[0] += jnp.sum(dy * xh, axis=0, keepdims=True)
        dxh = dy * g
        do2 = r * (dxh - _group_mean64(dxh) - xh * _group_mean64(dxh * xh))
        q2, k2, v2 = q_ref[...], k_ref[...], v_ref[...]
        dqs, dks, dvs = [], [], []
        for hh in range(2):
            sl = slice(hh * HEAD_DIM, (hh + 1) * HEAD_DIM)
            q, k, v, do = q2[:, sl], k2[:, sl], v2[:, sl], do2[:, sl].astype(BF)
            w = w_ref[hh]
            a = _dot_nt(q, k) * w
            dm = _dot_nt(do, v) * w
            sp, gs = st_ref[0, 0, hh], gstate[hh]
            qd = q.astype(F32) * qd_ref[hh]
            kd = k.astype(F32) * kd_ref[hh]
            dqs.append(_dot(dm, k) + _dot_nt(do, sp) * qd_ref[hh])
            dks.append(_dot_tn(dm, q) + _dot_nt(v, gs) * kd_ref[hh])
            dvs.append(_dot_tn(a, do) + _dot(kd, gs))
            gstate[hh] = gs * cd_ref[hh] + _dot_tn(qd, do)
        dq_ref[...] = jnp.concatenate(dqs, axis=-1)
        dk_ref[...] = jnp.concatenate(dks, axis=-1)
        dv_ref[...] = jnp.concatenate(dvs, axis=-1)

    blk = lambda col0: pl.BlockSpec((c, LANES), lambda hp, i: (n_b - 1 - i, col0 + hp))
    tab = lambda a: pl.BlockSpec((2,) + a.shape[1:], lambda hp, i: (hp, 0, 0))
    gspec = pl.BlockSpec((1, 1, LANES), lambda hp, i: (hp, 0, 0))
    return pl.pallas_call(
        body, name="retention_bwd", grid=(N_HEADS // 2, n_b),
        out_shape=(jax.ShapeDtypeStruct((t_len, GROUP_W), F32),) * 4 + (jax.ShapeDtypeStruct((N_HEADS // 2, 1, LANES), F32),),
        in_specs=[blk(0), blk(0), blk(g_col), gspec, blk(0), blk(0), blk(v_col),
                  pl.BlockSpec((1, 1, 2, HEAD_DIM, HEAD_DIM), lambda hp, i: (hp, n_b - 1 - i, 0, 0, 0)),
                  tab(wdec), tab(qdec), tab(kdec), tab(cdec)],
        out_specs=(blk(0), blk(0), blk(0), blk(0), gspec),
        scratch_shapes=[pltpu.VMEM((2, HEAD_DIM, HEAD_DIM), F32)],
        compiler_params=_cparams(("arbitrary", "arbitrary")),
    )(dmr, raw, proj, g_ret, rq, rk, proj, states, wdec, qdec, kdec, cdec)


def _in_proj_bwd(x, g_mix, dh1, dq_r, dk_r, dv_r, drg, dq_f, dk_f, dv_f, df_col, proj, z, cos_t, sin_t, gq_t, gk_t, w_main, w_ff):
    t_len = x.shape[0]
    tm = min(ROW_TILE, t_len)
    n_t = t_len // tm

    def body(x_ref, g_ref, dh1_ref, dqr_ref, dkr_ref, dvr_ref, drg_ref, dqf_ref, dkf_ref, dvf_ref, df_ref, fq_ref, fk_ref, z_ref,
             cos_ref, sin_ref, gq_ref, gk_ref, wm_ref, wf_ref,
             dproj_ref, dz_ref, dx_ref, dg_ref, dgq_ref, dgk_ref, db_ref, carry, gq_acc, gk_acc):
        i = pl.program_id(0)

        @pl.when(i == 0)
        def _():
            carry[...] = jnp.zeros_like(carry)
            gq_acc[...] = jnp.zeros_like(gq_acc)
            gk_acc[...] = jnp.zeros_like(gk_acc)
            dg_ref[...] = jnp.zeros_like(dg_ref)
            db_ref[...] = jnp.zeros_like(db_ref)

        c, s = cos_ref[...], sin_ref[...]
        gq, gk = gq_ref[...], gk_ref[...]
        dgq = jnp.zeros((1, LANES), F32)
        dgk = jnp.zeros((1, LANES), F32)
        for sl in _chunks(GROUP_W):
            dy = dqr_ref[:, sl] * 0.125
            dproj_ref[:, sl] = (dy * c + _swap32(dy * s)).astype(BF)
            dy = dkr_ref[:, sl]
            dproj_ref[:, GROUP_W + sl.start:GROUP_W + sl.stop] = (dy * c + _swap32(dy * s)).astype(BF)
            dproj_ref[:, 2 * GROUP_W + sl.start:2 * GROUP_W + sl.stop] = dvr_ref[:, sl].astype(BF)
            dproj_ref[:, 3 * GROUP_W + sl.start:3 * GROUP_W + sl.stop] = drg_ref[:, sl].astype(BF)
            for src, dsrc, gain, off in ((fq_ref, dqf_ref, gq, 4), (fk_ref, dkf_ref, gk, 5)):
                xr = src[:, sl].astype(F32)
                r = lax.rsqrt(_group_mean64(xr * xr) + EPS)
                xh = xr * r
                dy = dsrc[:, sl] * (0.125 if off == 4 else 1.0)
                dgs = jnp.sum(dy * xh, axis=0, keepdims=True)
                if off == 4:
                    dgq = dgq + dgs
                else:
                    dgk = dgk + dgs
                dxh = dy * gain
                dproj_ref[:, off * GROUP_W + sl.start:off * GROUP_W + sl.stop] = \
                    (r * (dxh - xh * _group_mean64(dxh * xh))).astype(BF)
            dproj_ref[:, 6 * GROUP_W + sl.start:6 * GROUP_W + sl.stop] = dvf_ref[:, sl].astype(BF)
        gq_acc[...] += dgq
        gk_acc[...] += dgk
        row = lax.broadcasted_iota(jnp.int32, (tm, tm), 0)
        col = lax.broadcasted_iota(jnp.int32, (tm, tm), 1)
        dlf = _dot_exact((col >= row).astype(BF), df_ref[...]) + carry[0:1, :]
        carry[...] = jnp.broadcast_to(dlf[0:1, :], carry.shape)
        lane = lax.broadcasted_iota(jnp.int32, (tm, LANES), 1)
        dz = jnp.where(lane < N_HEADS, dlf / (1.0 + jnp.exp(z_ref[...])), 0.0)
        db_ref[...] += jnp.sum(dz, axis=0, keepdims=True)
        dz_bf = dz.astype(BF)
        dz_ref[...] = dz_bf
        dn1 = _dot_nt(dz_bf, wf_ref[...])
        for sec in range(MAIN_W // GROUP_W):
            sl = slice(sec * GROUP_W, (sec + 1) * GROUP_W)
            dn1 = dn1 + _dot_nt(dproj_ref[:, sl], wm_ref[:, sl])
        dx, dg = _rms_bwd(x_ref[...], g_ref[...], dn1)
        dx_ref[...] = dh1_ref[...] + dx
        dg_ref[...] += dg

        @pl.when(i == n_t - 1)
        def _():
            dgq_ref[...] = gq_acc[:, :HEAD_DIM] + gq_acc[:, HEAD_DIM:]
            dgk_ref[...] = gk_acc[:, :HEAD_DIM] + gk_acc[:, HEAD_DIM:]

    row_spec = lambda w, col=0: pl.BlockSpec((tm, w), lambda i: (n_t - 1 - i, col))
    full = lambda a: pl.BlockSpec(a.shape, lambda i: (0,) * a.ndim)
    acc = lambda r, c: pl.BlockSpec((r, c), lambda i: (0, 0))
    return pl.pallas_call(
        body, name="in_proj_bwd", grid=(n_t,),
        out_shape=(jax.ShapeDtypeStruct((t_len, MAIN_W), BF), jax.ShapeDtypeStruct((t_len, LANES), BF),
                   jax.ShapeDtypeStruct((t_len, D_MODEL), F32), jax.ShapeDtypeStruct((1, D_MODEL), F32),
                   jax.ShapeDtypeStruct((1, HEAD_DIM), F32), jax.ShapeDtypeStruct((1, HEAD_DIM), F32),
                   jax.ShapeDtypeStruct((1, LANES), F32)),
        in_specs=[row_spec(D_MODEL), full(g_mix), row_spec(D_MODEL)] + [row_spec(GROUP_W)] * 7
        + [row_spec(LANES), row_spec(GROUP_W, 4), row_spec(GROUP_W, 5), row_spec(LANES), row_spec(LANES), row_spec(LANES),
           full(gq_t), full(gk_t), full(w_main), full(w_ff)],
        out_specs=(row_spec(MAIN_W), row_spec(LANES), row_spec(D_MODEL), acc(1, D_MODEL), acc(1, HEAD_DIM), acc(1, HEAD_DIM),
                   acc(1, LANES)),
        scratch_shapes=[pltpu.VMEM((8, LANES), F32), pltpu.VMEM((1, LANES), F32), pltpu.VMEM((1, LANES), F32)],
        compiler_params=_cparams(("arbitrary",)),
    )(x, g_mix, dh1, dq_r, dk_r, dv_r, drg, dq_f, dk_f, dv_f, df_col, proj, proj, z, cos_t, sin_t, gq_t, gk_t, w_main, w_ff)


def _matmul_tn(a, b, name, bm=256, bk=512):
    t_len, m = a.shape
    n = b.shape[1]
    bm, bk = min(bm, m), min(bk, t_len)

    def body(a_ref, b_ref, o_ref):
        @pl.when(pl.program_id(1) == 0)
        def _():
            o_ref[...] = jnp.zeros_like(o_ref)

        o_ref[...] += _dot_tn(a_ref[...], b_ref[...])

    return pl.pallas_call(
        body, name=name, grid=(m // bm, t_len // bk),
        out_shape=jax.ShapeDtypeStruct((m, n), F32),
        in_specs=[pl.BlockSpec((bk, bm), lambda i, k: (k, i)), pl.BlockSpec((bk, n), lambda i, k: (k, 0))],
        out_specs=pl.BlockSpec((bm, n), lambda i, k: (i, 0)),
        compiler_params=_cparams(("arbitrary", "arbitrary")),
    )(a, b)


def _place():
    x, y, c = lax.axis_index("x"), lax.axis_index("y"), lax.axis_index("c")
    chips = [(1 - x, y), (x, 1 - y), (1 - x, 1 - y)]
    return x, y, c, chips


def _row_chunks(rows, limit):
    step = max(d for d in range(16, min(rows, limit) + 1, 16) if rows % d == 0)
    return [slice(i, i + step) for i in range(0, rows, step)]


ICI_CHUNK_ROWS = 128
D2D_CHUNK_ROWS = 64


def _all_gather_weights(shards):
    n_w = len(shards)

    def body(*refs):
        ins, outs = refs[:n_w], refs[n_w:2 * n_w]
        send_sems, recv_sems = refs[2 * n_w:]
        x, y, c, chips = _place()
        me_chip = 2 * x + y
        sibling = (x, y, 1 - c)

        def copy(w, k, slot, half, to, rows=slice(None), src=None):
            dst = outs[w].at[slot, half, rows]
            return pltpu.make_async_remote_copy(src_ref=dst if src is None else src, dst_ref=dst,
                                                send_sem=send_sems.at[w, k], recv_sem=recv_sems.at[w, k],
                                                device_id=to, device_id_type=MESH)

        for w in range(n_w):
            for j, chip in enumerate(chips):
                for rows in _row_chunks(ins[w].shape[1], ICI_CHUNK_ROWS):
                    copy(w, j, me_chip, c, (*chip, c), rows, src=ins[w].at[c, rows]).start()
        for w in range(n_w):
            for j, (px, py) in enumerate(chips):
                copy(w, j, 2 * px + py, c, (x, y, c)).wait_recv()
                for rows in _row_chunks(ins[w].shape[1], D2D_CHUNK_ROWS):
                    copy(w, 3 + j, 2 * px + py, c, sibling, rows).start()
        for w in range(n_w):
            for j, (px, py) in enumerate(chips):
                copy(w, 3 + j, 2 * px + py, 1 - c, (x, y, c)).wait_recv()
        for w in range(n_w):
            for j, (px, py) in enumerate(chips):
                copy(w, j, me_chip, c, (px, py, c), src=ins[w].at[c]).wait_send()
                copy(w, 3 + j, 2 * px + py, c, sibling).wait_send()

    return pl.pallas_call(
        body, name="all_gather_weights",
        out_shape=tuple(jax.ShapeDtypeStruct((4,) + s.shape, s.dtype) for s in shards),
        in_specs=[ANY] * n_w, out_specs=(ANY,) * n_w,
        scratch_shapes=[pltpu.SemaphoreType.DMA((n_w, 6)), pltpu.SemaphoreType.DMA((n_w, 6))],
    )(*shards)


def _exchange_core_halves(grads):
    n_w = len(grads)

    def body(*refs):
        ins, theirs = refs[:n_w], refs[n_w:2 * n_w]
        send_sems, recv_sems = refs[2 * n_w:]
        x, y, c, _ = _place()

        def remote(w, k=slice(None), rows=slice(None)):
            return pltpu.make_async_remote_copy(src_ref=ins[w].at[k, 1 - c, rows], dst_ref=theirs[w].at[k, rows],
                                                send_sem=send_sems.at[w], recv_sem=recv_sems.at[w], device_id=(x, y, 1 - c),
                                                device_id_type=MESH)

        for w in range(n_w):
            for k in range(4):
                for rows in _row_chunks(ins[w].shape[2], D2D_CHUNK_ROWS):
                    remote(w, k, rows).start()
        for w in range(n_w):
            remote(w).wait()

    half = tuple(jax.ShapeDtypeStruct((4,) + g.shape[2:], g.dtype) for g in grads)
    return pl.pallas_call(
        body, name="exchange_core_halves", out_shape=half,
        in_specs=[ANY] * n_w, out_specs=(ANY,) * n_w,
        scratch_shapes=[pltpu.SemaphoreType.DMA((n_w,)), pltpu.SemaphoreType.DMA((n_w,))],
    )(*grads)


def _add_pairs(a, b, name):
    _, r, c = a.shape
    rb = 32 if r % 32 == 0 else r

    def body(a_ref, b_ref, o_ref, ob_ref):
        s = a_ref[...] + b_ref[...]
        o_ref[...] = s
        ob_ref[...] = s.astype(BF)

    spec = pl.BlockSpec((4, rb, c), lambda i: (0, i, 0))
    return pl.pallas_call(
        body, name=name, grid=(r // rb,),
        out_shape=(jax.ShapeDtypeStruct(a.shape, F32), jax.ShapeDtypeStruct(a.shape, BF)),
        in_specs=[spec, spec], out_specs=(spec, spec), compiler_params=_cparams(("arbitrary",)),
    )(a, b)


def _scatter_to_chips(sums_bf16):
    n_w = len(sums_bf16)

    def body(*refs):
        bfs, got = refs[:n_w], refs[n_w:2 * n_w]
        send_sems, recv_sems = refs[2 * n_w:]
        x, y, c, chips = _place()

        def remote(w, j, px, py, rows=slice(None)):
            return pltpu.make_async_remote_copy(src_ref=bfs[w].at[2 * px + py, rows], dst_ref=got[w].at[j, rows],
                                                send_sem=send_sems.at[w, j], recv_sem=recv_sems.at[w, j], device_id=(px, py, c),
                                                device_id_type=MESH)

        for w in range(n_w):
            for j, (px, py) in enumerate(chips):
                for rows in _row_chunks(bfs[w].shape[1], ICI_CHUNK_ROWS):
                    remote(w, j, px, py, rows).start()
        for w in range(n_w):
            for j, (px, py) in enumerate(chips):
                remote(w, j, px, py).wait()

    return pl.pallas_call(
        body, name="scatter_to_chips",
        out_shape=tuple(jax.ShapeDtypeStruct((3,) + s.shape[1:], BF) for s in sums_bf16),
        in_specs=[ANY] * n_w, out_specs=(ANY,) * n_w,
        scratch_shapes=[pltpu.SemaphoreType.DMA((n_w, 3)), pltpu.SemaphoreType.DMA((n_w, 3))],
    )(*sums_bf16)


def _add_received(own, got, name):
    r, c = own.shape
    rb = 32 if r % 32 == 0 else r

    def body(o_ref, g_ref, out_ref):
        out_ref[...] = ((o_ref[...] + g_ref[0].astype(F32)) + g_ref[1].astype(F32)) + g_ref[2].astype(F32)

    return pl.pallas_call(
        body, name=name, grid=(r // rb,), out_shape=jax.ShapeDtypeStruct((r, c), F32),
        in_specs=[pl.BlockSpec((rb, c), lambda i: (i, 0)), pl.BlockSpec((3, rb, c), lambda i: (0, i, 0))],
        out_specs=pl.BlockSpec((rb, c), lambda i: (i, 0)), compiler_params=_cparams(("arbitrary",)),
    )(own, got)


def _share_with_sibling(halves):
    n_w = len(halves)

    def body(*refs):
        ins, outs = refs[:n_w], refs[n_w:2 * n_w]
        send_sems, recv_sems = refs[2 * n_w:]
        x, y, c, _ = _place()

        def remote(w, rows=slice(None)):
            return pltpu.make_async_remote_copy(src_ref=ins[w].at[rows], dst_ref=outs[w].at[c, rows], send_sem=send_sems.at[w],
                                                recv_sem=recv_sems.at[w], device_id=(x, y, 1 - c), device_id_type=MESH)

        for w in range(n_w):
            for rows in _row_chunks(ins[w].shape[0], D2D_CHUNK_ROWS):
                remote(w, rows).start()
        for w in range(n_w):
            remote(w).wait()

    return pl.pallas_call(
        body, name="share_with_sibling",
        out_shape=tuple(jax.ShapeDtypeStruct((2,) + h.shape, h.dtype) for h in halves),
        in_specs=[ANY] * n_w, out_specs=(ANY,) * n_w,
        scratch_shapes=[pltpu.SemaphoreType.DMA((n_w,)), pltpu.SemaphoreType.DMA((n_w,))],
    )(*halves)


def _all_reduce_small(pack):
    r, c = pack.shape

    def body(p_ref, out_ref, slots, send_sems, recv_sems):
        x, y, cc, _ = _place()
        me = 4 * x + 2 * y + cc
        slots[me] = p_ref[...]
        copies = []
        for k in range(1, 8):
            dx, dy, dc = (k >> 2) & 1, (k >> 1) & 1, k & 1
            to = (1 - x if dx else x, 1 - y if dy else y, 1 - cc if dc else cc)
            cp = pltpu.make_async_remote_copy(src_ref=p_ref, dst_ref=slots.at[me], send_sem=send_sems.at[k - 1],
                                              recv_sem=recv_sems.at[k - 1], device_id=to, device_id_type=MESH)
            cp.start()
            copies.append(cp)
        for cp in copies:
            cp.wait()
        total = slots[0]
        for d in range(1, 8):
            total = total + slots[d]
        out_ref[...] = total

    return pl.pallas_call(
        body, name="all_reduce_small", out_shape=jax.ShapeDtypeStruct((r, c), F32),
        in_specs=[VMEM_SPEC], out_specs=VMEM_SPEC,
        scratch_shapes=[pltpu.VMEM((8, r, c), F32), pltpu.SemaphoreType.DMA((7,)), pltpu.SemaphoreType.DMA((7,))],
    )(pack)


def _adamw(w, g, m, v, name):
    r, c = w.shape
    rb = 64 if r % 64 == 0 else r
    c1 = 1.0 - ADAM_B1 ** ADAM_STEP
    c2 = 1.0 - ADAM_B2 ** ADAM_STEP

    def body(w_ref, g_ref, m_ref, v_ref, d_ref, nm_ref, nv_ref):
        gv = g_ref[...]
        nm = ADAM_B1 * m_ref[...] + (1.0 - ADAM_B1) * gv
        nv = ADAM_B2 * v_ref[...] + (1.0 - ADAM_B2) * (gv * gv)
        nm_ref[...] = nm
        nv_ref[...] = nv
        d_ref[...] = -ADAM_LR * ((nm / c1) / (jnp.sqrt(nv / c2) + ADAM_EPS) + ADAM_WD * w_ref[...])

    spec = pl.BlockSpec((rb, c), lambda i: (i, 0))
    return pl.pallas_call(
        body, name=name, grid=(r // rb,), out_shape=(jax.ShapeDtypeStruct((r, c), F32),) * 3,
        in_specs=[spec] * 4, out_specs=(spec,) * 3, compiler_params=_cparams(("arbitrary",)),
    )(w, g, m, v)


def _rope_tables(t_len):
    inv_freq = ROPE_BASE ** (-jnp.arange(0, HEAD_DIM, 2, dtype=F32) / HEAD_DIM)
    ang = jnp.arange(t_len, dtype=F32)[:, None] * inv_freq[None, :]
    cos, sin = jnp.cos(ang), jnp.sin(ang)
    cos_t = jnp.concatenate([cos, cos, cos, cos], axis=-1)
    sin_t = jnp.concatenate([-sin, sin, -sin, sin], axis=-1)
    return cos_t, sin_t


def _cols_to_shards(dw):
    r, n = dw.shape
    return jnp.transpose(dw.reshape(2, r // 2, 4, n // 4), (2, 0, 1, 3))


def _rows_to_shards(dw):
    r, n = dw.shape
    return dw.reshape(4, 2, r // 8, n)


def _pad_lanes(a):
    extra = -a.shape[-1] % LANES
    return a if extra == 0 else jnp.pad(a, [(0, 0)] * (a.ndim - 1) + [(0, extra)])


def _pad_row(a, width=D_MODEL):
    a = a.reshape(1, -1)
    return jnp.pad(a, ((0, 0), (0, width - a.shape[1])))


def kernel(x, mem, g_mix, w_in, b_forget, g_ret_out, g_fox_q, g_fox_k, w_out, g_xattn, w_xq, w_xkv, g_mem, g_xq, g_xk, w_xo, g_ffn, w_gate, w_up, w_down, loss_target, m_g_mix, m_w_in, m_b_forget, m_g_ret_out, m_g_fox_q, m_g_fox_k, m_w_out, m_g_xattn, m_w_xq, m_w_xkv, m_g_mem, m_g_xq, m_g_xk, m_w_xo, m_g_ffn, m_w_gate, m_w_up, m_w_down, v_g_mix, v_w_in, v_b_forget, v_g_ret_out, v_g_fox_q, v_g_fox_k, v_w_out, v_g_xattn, v_w_xq, v_w_xkv, v_g_mem, v_g_xq, v_g_xk, v_w_xo, v_g_ffn, v_w_gate, v_w_up, v_w_down):
    big = {"w_in": (w_in, m_w_in, v_w_in), "w_out": (w_out, m_w_out, v_w_out), "w_xq": (w_xq, m_w_xq, v_w_xq),
           "w_xkv": (w_xkv, m_w_xkv, v_w_xkv), "w_xo": (w_xo, m_w_xo, v_w_xo), "w_gate": (w_gate, m_w_gate, v_w_gate),
           "w_up": (w_up, m_w_up, v_w_up), "w_down": (w_down, m_w_down, v_w_down)}
    big_names = list(big)
    col_sharded = ("w_in", "w_xkv", "w_gate", "w_up")

    shards = []
    for n in big_names:
        w = _pad_lanes(big[n][0][0].astype(BF))
        shards.append(w.reshape(2, w.shape[0] // 2, w.shape[1]))
    gathered = _all_gather_weights(shards)
    my_chip = 2 * lax.axis_index("x") + lax.axis_index("y")
    full = {}
    for n, g, own in zip(big_names, gathered, shards):
        g = lax.dynamic_update_slice(g, own[None], (my_chip, 0, 0, 0))
        _, _, rh, _ = g.shape
        cs = big[n][0].shape[2]
        g = g.reshape(4, 2 * rh, g.shape[3])[:, :, :cs]
        full[n] = jnp.transpose(g, (1, 0, 2)).reshape(2 * rh, 4 * cs) if n in col_sharded else g.reshape(8 * rh, cs)
    small_w = {"g_mix": g_mix, "b_forget": b_forget, "g_ret_out": g_ret_out, "g_fox_q": g_fox_q, "g_fox_k": g_fox_k,
               "g_xattn": g_xattn, "g_mem": g_mem, "g_xq": g_xq, "g_xk": g_xk, "g_ffn": g_ffn}
    m_small = {"g_mix": m_g_mix, "b_forget": m_b_forget, "g_ret_out": m_g_ret_out, "g_fox_q": m_g_fox_q, "g_fox_k": m_g_fox_k,
               "g_xattn": m_g_xattn, "g_mem": m_g_mem, "g_xq": m_g_xq, "g_xk": m_g_xk, "g_ffn": m_g_ffn}
    v_small = {"g_mix": v_g_mix, "b_forget": v_b_forget, "g_ret_out": v_g_ret_out, "g_fox_q": v_g_fox_q, "g_fox_k": v_g_fox_k,
               "g_xattn": v_g_xattn, "g_mem": v_g_mem, "g_xq": v_g_xq, "g_xk": v_g_xk, "g_ffn": v_g_ffn}
    loss_part, grad_x, dw, small_g = _local_step(x[0], mem[0], loss_target[0], full, small_w)
    return _reduce_and_update(big, big_names, col_sharded, dw, small_w, small_g, loss_part, grad_x, m_small, v_small)


def _local_step(xs, mems, tgt, full, small_w):
    g_mix, b_forget, g_ret_out, g_fox_q, g_fox_k = (small_w[n] for n in ("g_mix", "b_forget", "g_ret_out", "g_fox_q", "g_fox_k"))
    g_xattn, g_mem, g_xq, g_xk, g_ffn = (small_w[n] for n in ("g_xattn", "g_mem", "g_xq", "g_xk", "g_ffn"))
    w_main = full["w_in"][:, :MAIN_W]
    w_ff = jnp.pad(full["w_in"][:, MAIN_W:], ((0, 0), (0, LANES - (IN_W - MAIN_W))))
    t_len = xs.shape[0]
    cos_t, sin_t = _rope_tables(t_len)
    tables = _decay_tables(min(RET_BLOCK, t_len))
    gq_t = jnp.concatenate([g_fox_q, g_fox_q], axis=-1)
    gk_t = jnp.concatenate([g_fox_k, g_fox_k], axis=-1)
    b_pad = _pad_row(b_forget, LANES)
    g_ret = g_ret_out.reshape(N_HEADS // 2, 1, LANES)

    memn, kraw, kn, vmem = _mem_kv_fwd(mems, g_mem, full["w_xkv"], g_xk)
    n1, proj, rq, rk, q_aug, k_aug, z = _in_proj_fwd(xs, g_mix, w_main, w_ff, b_pad, cos_t, sin_t, gq_t, gk_t)
    raw, mix_r, states = _retention_fwd(rq, rk, proj, g_ret, tables)
    mix_f, o32, lse = _fox_fwd(q_aug, k_aug, proj)
    h1, hn2, qx, o_x, h2 = _attn_out_xattn_fwd(xs, mix_r, mix_f, full["w_out"], g_xattn, full["w_xq"], g_xq, kn, vmem, full["w_xo"])
    hn3, gate, up, act, dh3, loss_part = _ffn_loss_fwd(h2, g_ffn, full["w_gate"], full["w_up"], full["w_down"], tgt)

    dgate, dup, dh2, dg_ffn = _ffn_bwd(dh3, gate, up, h2, g_ffn, full["w_gate"], full["w_up"], full["w_down"])
    dqx, dh1, dmr, dmf, dkn, dvm, dg_xattn, dg_xq = _attn_out_xattn_bwd(dh2, h1, qx, kn, vmem, full["w_xo"], full["w_xq"],
                                                                      full["w_out"], g_xattn, g_xq)
    dw_xkv, dg_mem, dg_xk = _mem_kv_bwd(dkn, dvm, kraw, mems, memn, g_mem, g_xk, full["w_xkv"])
    dq_f, dk_f, dv_f, df = _fox_bwd(q_aug, k_aug, proj, dmf, o32, lse)
    dq_r, dk_r, dv_r, drg, dg_ret = _retention_bwd(dmr, raw, proj, g_ret, rq, rk, states, tables)
    df_col = jnp.pad(jnp.transpose(df, (1, 0, 2)).reshape(t_len, N_HEADS), ((0, 0), (0, LANES - N_HEADS)))
    dproj, dz, grad_x, dg_mix, dg_fq, dg_fk, db = _in_proj_bwd(xs, g_mix, dh1, dq_r, dk_r, dv_r, drg, dq_f, dk_f, dv_f, df_col,
                                                              proj, z, cos_t, sin_t, gq_t, gk_t, w_main, w_ff)

    dw = {
        "w_in": jnp.concatenate([_matmul_tn(n1, dproj, "dw_in_main"), _matmul_tn(n1, dz, "dw_in_ff")[:, :IN_W - MAIN_W]], axis=1),
        "w_out": jnp.concatenate([_matmul_tn(mix_r, dh1, "dw_out_ret"), _matmul_tn(mix_f, dh1, "dw_out_fox")], axis=0),
        "w_xq": _matmul_tn(hn2, dqx, "dw_xq"),
        "w_xkv": dw_xkv,
        "w_xo": _matmul_tn(o_x, dh2, "dw_xo"),
        "w_gate": _matmul_tn(hn3, dgate, "dw_gate"),
        "w_up": _matmul_tn(hn3, dup, "dw_up"),
        "w_down": _matmul_tn(act, dh3, "dw_down"),
    }
    small_g = {"g_mix": dg_mix, "b_forget": db[:, :N_HEADS], "g_ret_out": dg_ret, "g_fox_q": dg_fq, "g_fox_k": dg_fk,
               "g_xattn": dg_xattn, "g_mem": dg_mem, "g_xq": dg_xq, "g_xk": dg_xk, "g_ffn": dg_ffn}
    return loss_part, grad_x, dw, small_g


def _reduce_and_update(big, big_names, col_sharded, dw, small_w, small_g, loss_part, grad_x, m_small, v_small):
    parts = [_pad_lanes(_cols_to_shards(dw[n]) if n in col_sharded else _rows_to_shards(dw[n])) for n in big_names]
    my_core = lax.axis_index("c")
    my_chip = 2 * lax.axis_index("x") + lax.axis_index("y")
    theirs = _exchange_core_halves(parts)
    mine = [lax.dynamic_index_in_dim(p, my_core, axis=1, keepdims=False) for p in parts]
    sums = [_add_pairs(a, b, f"core_sum_{n}") for n, a, b in zip(big_names, mine, theirs)]
    got = _scatter_to_chips([s[1] for s in sums])
    own = [lax.dynamic_index_in_dim(s[0], my_chip, axis=0, keepdims=False) for s in sums]
    finals = [_add_received(o, g, f"chip_sum_{n}") for n, o, g in zip(big_names, own, got)]
    shared = _share_with_sibling(finals)
    grads, deltas, new_m, new_v = {}, {}, {}, {}
    for n, s, fin in zip(big_names, shared, finals):
        w, m, v = big[n]
        s = lax.dynamic_update_slice(s, fin[None], (my_core, 0, 0))
        g = s.reshape(w.shape[1], s.shape[2])[:, :w.shape[2]]
        d, nm, nv = _adamw(w[0], g, m[0], v[0], f"adamw_{n}")
        grads[n], deltas[n], new_m[n], new_v[n] = g[None], d[None], nm[None], nv[None]

    small_names = list(small_w)
    pad_rows = SMALL_ROWS - len(small_names) - 1
    stack = lambda d: jnp.concatenate([_pad_row(d[n]) for n in small_names] + [jnp.zeros((pad_rows + 1, D_MODEL), F32)], axis=0)
    g_pack = jnp.concatenate([_pad_row(small_g[n]) for n in small_names] + [_pad_row(loss_part[0:1, 0:1])]
                             + [jnp.zeros((pad_rows, D_MODEL), F32)], axis=0)
    g_tot = _all_reduce_small(g_pack)
    d_s, m_s, v_s = _adamw(stack(small_w), g_tot, stack(m_small), stack(v_small), "adamw_small")
    for i, n in enumerate(small_names):
        shape = small_w[n].shape
        size = int(np.prod(shape))
        grads[n] = g_tot[i, :size].reshape(shape)
        deltas[n], new_m[n], new_v[n] = d_s[i, :size].reshape(shape), m_s[i, :size].reshape(shape), v_s[i, :size].reshape(shape)
    loss = g_tot[len(small_names), 0]

    order = ["g_mix", "w_in", "b_forget", "g_ret_out", "g_fox_q", "g_fox_k", "w_out", "g_xattn", "w_xq", "w_xkv", "g_mem", "g_xq",
             "g_xk", "w_xo", "g_ffn", "w_gate", "w_up", "w_down"]
    return (loss, grad_x[None], *[grads[n] for n in order], *[deltas[n] for n in order], *[new_m[n] for n in order],
            *[new_v[n] for n in order])
```

```python
import functools

import numpy as np
import jax
import jax.numpy as jnp
from jax import lax
from jax.experimental import pallas as pl
from jax.experimental.pallas import tpu as pltpu

F32 = jnp.float32
BF = jnp.bfloat16

D_MODEL = 1024
HEAD_DIM = 64
N_HEADS = 8
GROUP_W = 512
N_XH = 4
XHD = 256
D_FF = 2816
MAIN_W = 3584
IN_W = 3592
ROPE_BASE = 10000.0
LOG2E = 1.4426950408889634
LN2 = 0.6931471805599453
EPS = 1e-6
NEG = -1e30
LANES = 128
RET_BLOCK = 256
REF_CHUNK = 64
ROW_TILE = 256
ATT_BLOCK = 256
SMALL_ROWS = 16
COL_SHARDED = ("w_in", "w_xkv", "w_gate", "w_up")
LATE = ("w_out", "w_xq", "w_xkv", "w_xo", "w_gate", "w_up", "w_down")
VMEM_LIMIT = 56 * 1024 * 1024

ADAM_LR = 0.001
ADAM_B1 = 0.9
ADAM_B2 = 0.999
ADAM_EPS = 1e-08
ADAM_WD = 0.01
ADAM_STEP = 10

MESH = pl.DeviceIdType.MESH
ANY = pl.BlockSpec(memory_space=pl.ANY)
VMEM_SPEC = pl.BlockSpec(memory_space=pltpu.VMEM)


def _cparams(sem=None, vmem=VMEM_LIMIT):
    return pltpu.CompilerParams(dimension_semantics=sem, vmem_limit_bytes=vmem)


def _dot(a, b):
    return jnp.dot(a.astype(BF), b.astype(BF), preferred_element_type=F32)


def _dot_nt(a, b):
    return lax.dot_general(a.astype(BF), b.astype(BF), (((1,), (1,)), ((), ())), preferred_element_type=F32)


def _dot_tn(a, b):
    return lax.dot_general(a.astype(BF), b.astype(BF), (((0,), (0,)), ((), ())), preferred_element_type=F32)


def _split3(x):
    hi = x.astype(BF)
    r = x - hi.astype(F32)
    mid = r.astype(BF)
    lo = (r - mid.astype(F32)).astype(BF)
    return hi, mid, lo


def _dot_exact(ind, x):
    hi, mid, lo = _split3(x)
    return (jnp.dot(ind, lo, preferred_element_type=F32) + jnp.dot(ind, mid, preferred_element_type=F32)
            + jnp.dot(ind, hi, preferred_element_type=F32))


def _dot_nt_exact(ind, x):
    hi, mid, lo = _split3(x)
    dn = (((1,), (1,)), ((), ()))
    return (lax.dot_general(ind, lo, dn, preferred_element_type=F32) + lax.dot_general(ind, mid, dn, preferred_element_type=F32)
            + lax.dot_general(ind, hi, dn, preferred_element_type=F32))


def _sigmoid(x):
    return 1.0 / (1.0 + jnp.exp(-x))


def _rms_fwd(x, g):
    r = lax.rsqrt(jnp.mean(x * x, axis=-1, keepdims=True) + EPS)
    return x * r * g


def _rms_bwd(x, g, dy):
    r = lax.rsqrt(jnp.mean(x * x, axis=-1, keepdims=True) + EPS)
    xh = x * r
    dg = jnp.sum(dy * xh, axis=0, keepdims=True)
    dxh = dy * g
    dx = r * (dxh - xh * jnp.mean(dxh * xh, axis=-1, keepdims=True))
    return dx, dg


def _group_mean64(x):
    lane = lax.broadcasted_iota(jnp.int32, x.shape, 1)
    lo = lane < HEAD_DIM
    s_lo = jnp.sum(jnp.where(lo, x, 0.0), axis=-1, keepdims=True)
    s_hi = jnp.sum(jnp.where(lo, 0.0, x), axis=-1, keepdims=True)
    return jnp.where(lo, s_lo, s_hi) * (1.0 / HEAD_DIM)


def _swap32(x):
    lane = lax.broadcasted_iota(jnp.int32, x.shape, 1)
    first = (lane % HEAD_DIM) < (HEAD_DIM // 2)
    return jnp.where(first, pltpu.roll(x, LANES - HEAD_DIM // 2, axis=1), pltpu.roll(x, HEAD_DIM // 2, axis=1))


def _chunks(w):
    return [slice(j * LANES, (j + 1) * LANES) for j in range(w // LANES)]


def _aug_pair(qk, f_cols, is_query):
    lane = lax.broadcasted_iota(jnp.int32, (qk.shape[0], HEAD_DIM), 1)
    out = []
    for hh in range(2):
        hi, mid, lo = (p.astype(F32) for p in _split3(f_cols[hh] * LOG2E))
        if is_query:
            aux = jnp.where(lane == 0, hi, jnp.where(lane == 1, mid, jnp.where(lane == 2, lo, jnp.where(lane < 6, 1.0, 0.0))))
        else:
            aux = jnp.where(lane < 3, 1.0, jnp.where(lane == 3, -hi, jnp.where(lane == 4, -mid, jnp.where(lane == 5, -lo, 0.0))))
        out += [qk[:, hh * HEAD_DIM:(hh + 1) * HEAD_DIM], aux]
    return jnp.concatenate(out, axis=-1).astype(BF)


def _mem_kv_fwd(mem, g_mem, w_xkv, g_xk):
    m_tok = mem.shape[0]

    def body(mem_ref, gm_ref, w_ref, gk_ref, memn_ref, kraw_ref, kn_ref, v_ref):
        mn = _rms_fwd(mem_ref[...], gm_ref[...]).astype(BF)
        memn_ref[...] = mn
        kv = jnp.dot(mn, w_ref[...], preferred_element_type=F32)
        k = kv[:, :D_MODEL]
        kraw_ref[...] = k
        v_ref[...] = kv[:, D_MODEL:].astype(BF)
        for h in range(N_XH):
            sl = slice(h * XHD, (h + 1) * XHD)
            kn_ref[:, sl] = _rms_fwd(k[:, sl], gk_ref[...]).astype(BF)

    return pl.pallas_call(
        body, name="mem_kv_fwd",
        out_shape=(jax.ShapeDtypeStruct((m_tok, D_MODEL), BF), jax.ShapeDtypeStruct((m_tok, D_MODEL), F32),
                   jax.ShapeDtypeStruct((m_tok, D_MODEL), BF), jax.ShapeDtypeStruct((m_tok, D_MODEL), BF)),
        in_specs=[VMEM_SPEC] * 4, out_specs=(VMEM_SPEC,) * 4, compiler_params=_cparams(),
    )(mem, g_mem, w_xkv, g_xk)


def _in_proj_fwd(x, g_mix, w_main, w_ff, b_pad, cos_t, sin_t, gq_t, gk_t):
    t_len = x.shape[0]
    tm = min(ROW_TILE, t_len)
    n_t = t_len // tm

    def body(x_ref, g_ref, wm_ref, wf_ref, b_ref, cos_ref, sin_ref, gq_ref, gk_ref,
             n1_ref, proj_ref, rq_ref, rk_ref, qa_ref, ka_ref, z_ref, carry):
        i = pl.program_id(0)

        @pl.when(i == 0)
        def _():
            carry[...] = jnp.zeros_like(carry)

        n1 = _rms_fwd(x_ref[...], g_ref[...]).astype(BF)
        n1_ref[...] = n1
        proj = jnp.dot(n1, wm_ref[...], preferred_element_type=F32)
        proj_ref[...] = proj.astype(BF)
        z = jnp.dot(n1, wf_ref[...], preferred_element_type=F32) + b_ref[...]
        z_ref[...] = z
        lane = lax.broadcasted_iota(jnp.int32, z.shape, 1)
        lf = jnp.where(lane < N_HEADS, jnp.minimum(z, 0.0) - jnp.log(1.0 + jnp.exp(-jnp.abs(z))), 0.0)
        row = lax.broadcasted_iota(jnp.int32, (tm, tm), 0)
        col = lax.broadcasted_iota(jnp.int32, (tm, tm), 1)
        tri = (row >= col).astype(BF)
        fc = _dot_exact(tri, lf) + carry[0:1, :]
        carry[...] = jnp.broadcast_to(fc[tm - 1:tm, :], carry.shape)
        c, s = cos_ref[...], sin_ref[...]
        for j, sl in enumerate(_chunks(GROUP_W)):
            q = proj[:, sl]
            rq_ref[:, sl] = ((q * c + _swap32(q) * s) * 0.125).astype(BF)
            k = proj[:, GROUP_W + j * LANES:GROUP_W + (j + 1) * LANES]
            rk_ref[:, sl] = (k * c + _swap32(k) * s).astype(BF)
            f_cols = [fc[:, 2 * j:2 * j + 1], fc[:, 2 * j + 1:2 * j + 2]]
            fq = proj[:, 4 * GROUP_W + j * LANES:4 * GROUP_W + (j + 1) * LANES]
            fq = fq * lax.rsqrt(_group_mean64(fq * fq) + EPS) * gq_ref[...] * (0.125 * LOG2E)
            qa_ref[:, 2 * j * LANES:2 * (j + 1) * LANES] = _aug_pair(fq, f_cols, True)
            fk = proj[:, 5 * GROUP_W + j * LANES:5 * GROUP_W + (j + 1) * LANES]
            fk = fk * lax.rsqrt(_group_mean64(fk * fk) + EPS) * gk_ref[...]
            ka_ref[:, 2 * j * LANES:2 * (j + 1) * LANES] = _aug_pair(fk, f_cols, False)

    row_spec = lambda w: pl.BlockSpec((tm, w), lambda i: (i, 0))
    full = lambda a: pl.BlockSpec(a.shape, lambda i: (0,) * a.ndim)
    return pl.pallas_call(
        body, name="in_proj_fwd", grid=(n_t,),
        out_shape=(jax.ShapeDtypeStruct((t_len, D_MODEL), BF), jax.ShapeDtypeStruct((t_len, MAIN_W), BF),
                   jax.ShapeDtypeStruct((t_len, GROUP_W), BF), jax.ShapeDtypeStruct((t_len, GROUP_W), BF),
                   jax.ShapeDtypeStruct((t_len, 2 * GROUP_W), BF), jax.ShapeDtypeStruct((t_len, 2 * GROUP_W), BF),
                   jax.ShapeDtypeStruct((t_len, LANES), F32)),
        in_specs=[row_spec(D_MODEL), full(g_mix), full(w_main), full(w_ff), full(b_pad), row_spec(LANES), row_spec(LANES),
                  full(gq_t), full(gk_t)],
        out_specs=(row_spec(D_MODEL), row_spec(MAIN_W), row_spec(GROUP_W), row_spec(GROUP_W), row_spec(2 * GROUP_W),
                   row_spec(2 * GROUP_W), row_spec(LANES)),
        scratch_shapes=[pltpu.VMEM((8, LANES), F32)],
        compiler_params=_cparams(("arbitrary",)),
    )(x, g_mix, w_main, w_ff, b_pad, cos_t, sin_t, gq_t, gk_t)


def _decay_tables(c):
    h = np.arange(N_HEADS, dtype=np.float64)
    lg = np.log(1.0 - 2.0 ** (-5.0 - h)).astype(np.float32).astype(np.float64)
    t = np.arange(c)
    same_or_earlier = (t[None, :] // REF_CHUNK) <= (t[:, None] // REF_CHUNK)
    w = np.where(same_or_earlier[None], np.exp(lg[:, None, None] * np.abs(t[:, None] - t[None, :])[None]), 0.0)
    qd = np.exp(lg[:, None] * (t[None, :] + 1.0))
    kd = np.exp(lg[:, None] * (c - 1.0 - t[None, :]))
    cd = np.exp(lg * c)
    ones = np.ones((1, 1, HEAD_DIM))
    return (jnp.asarray(w, F32), jnp.asarray(qd[:, :, None] * ones, F32), jnp.asarray(kd[:, :, None] * ones, F32),
            jnp.asarray(cd[:, None, None] * np.ones((1, HEAD_DIM, HEAD_DIM)), F32))


def _retention_fwd(rq, rk, proj, g_ret, tables):
    t_len = rq.shape[0]
    c = min(RET_BLOCK, t_len)
    n_b = t_len // c
    wdec, qdec, kdec, cdec = tables
    v_col, g_col = 2 * GROUP_W // LANES, 3 * GROUP_W // LANES

    def body(q_ref, k_ref, v_ref, rg_ref, g_ref, w_ref, qd_ref, kd_ref, cd_ref, raw_ref, mix_ref, st_ref, state):
        i = pl.program_id(1)

        @pl.when(i == 0)
        def _():
            state[...] = jnp.zeros_like(state)

        q2, k2, v2 = q_ref[...], k_ref[...], v_ref[...]
        outs = []
        for hh in range(2):
            sl = slice(hh * HEAD_DIM, (hh + 1) * HEAD_DIM)
            q, k, v = q2[:, sl], k2[:, sl], v2[:, sl]
            sp = state[hh]
            st_ref[0, 0, hh] = sp
            a = _dot_nt(q, k) * w_ref[hh]
            o = _dot(a, v) + _dot(q.astype(F32) * qd_ref[hh], sp)
            state[hh] = sp * cd_ref[hh] + _dot_tn(k.astype(F32) * kd_ref[hh], v)
            outs.append(o)
        o2 = jnp.concatenate(outs, axis=-1)
        raw_ref[...] = o2
        xc = o2 - _group_mean64(o2)
        xh = xc * lax.rsqrt(_group_mean64(xc * xc) + EPS)
        gate = rg_ref[...].astype(F32)
        mix_ref[...] = (gate * _sigmoid(gate) * (xh * g_ref[0])).astype(BF)

    blk = lambda col0: pl.BlockSpec((c, LANES), lambda hp, i: (i, col0 + hp))
    tab = lambda a: pl.BlockSpec((2,) + a.shape[1:], lambda hp, i: (hp, 0, 0))
    return pl.pallas_call(
        body, name="retention_fwd", grid=(N_HEADS // 2, n_b),
        out_shape=(jax.ShapeDtypeStruct((t_len, GROUP_W), F32), jax.ShapeDtypeStruct((t_len, GROUP_W), BF),
                   jax.ShapeDtypeStruct((N_HEADS // 2, n_b, 2, HEAD_DIM, HEAD_DIM), F32)),
        in_specs=[blk(0), blk(0), blk(v_col), blk(g_col), pl.BlockSpec((1, 1, LANES), lambda hp, i: (hp, 0, 0)),
                  tab(wdec), tab(qdec), tab(kdec), tab(cdec)],
        out_specs=(blk(0), blk(0), pl.BlockSpec((1, 1, 2, HEAD_DIM, HEAD_DIM), lambda hp, i: (hp, i, 0, 0, 0))),
        scratch_shapes=[pltpu.VMEM((2, HEAD_DIM, HEAD_DIM), F32)],
        compiler_params=_cparams(("arbitrary", "arbitrary")),
    )(rq, rk, proj, proj, g_ret, wdec, qdec, kdec, cdec)


def _fox_fwd(q_aug, k_aug, proj, shards):
    t_len = q_aug.shape[0]
    tq = min(ATT_BLOCK, t_len)
    n_q = t_len // tq
    v_col = 6 * GROUP_W // LANES
    tc = min(512, t_len)
    n_w = len(shards)
    n_steps = (N_HEADS // 2) * n_q

    def body(*refs):
        q_ref, k_ref, v_ref = refs[:3]
        o_ref, o32_ref, lse_ref = refs[3 + n_w:6 + n_w]
        vt = refs[6 + 2 * n_w]
        comm = (refs[3:3 + n_w], refs[6 + n_w:6 + 2 * n_w]) + tuple(refs[7 + 2 * n_w:])
        i = pl.program_id(1)
        step = pl.program_id(0) * n_q + i

        @pl.when(step == 0)
        def _():
            _gather_phase(0, *comm)

        @pl.when(step == (3 * n_steps) // 4)
        def _():
            _gather_phase(1, *comm)

        @pl.when(i == 0)
        def _():
            for c0 in range(0, t_len, tc):
                vt[:, c0:c0 + tc] = v_ref[c0:c0 + tc, :].T

        qs = [q_ref[:, hh * LANES:(hh + 1) * LANES] for hh in range(2)]
        ones = jnp.ones((HEAD_DIM, tq), BF)

        def block(j, carry, masked):
            rows = pl.ds(pl.multiple_of(j * tq, tq), tq)
            k2 = k_ref[rows, :]
            v2 = vt[:, rows]
            ss = [_dot_nt(k2[:, hh * LANES:(hh + 1) * LANES], qs[hh]) for hh in range(2)]
            ps, stats = [], []
            for hh in range(2):
                m = carry[hh][0]
                s_t = ss[hh]
                if masked:
                    krow = lax.broadcasted_iota(jnp.int32, (tq, tq), 0)
                    qcol = lax.broadcasted_iota(jnp.int32, (tq, tq), 1)
                    s_t = jnp.where(qcol >= krow, s_t, NEG)
                m_new = jnp.maximum(m, jnp.max(s_t, axis=0, keepdims=True))
                ps.append(jnp.exp2(s_t - m_new).astype(BF))
                stats.append((m_new, jnp.exp2(m - m_new)))
            out = []
            for hh in range(2):
                m_new, alpha = stats[hh]
                v_aug = jnp.concatenate([v2[hh * HEAD_DIM:(hh + 1) * HEAD_DIM, :], ones], axis=0)
                out.append((m_new, carry[hh][1] * alpha + jnp.dot(v_aug, ps[hh], preferred_element_type=F32)))
            return tuple(out)

        init = tuple((jnp.full((1, tq), NEG, F32), jnp.zeros((LANES, tq), F32)) for _ in range(2))
        carry = lax.fori_loop(0, i, lambda j, c: block(j, c, False), init)
        carry = block(i, carry, True)
        outs, lses = [], []
        for hh in range(2):
            m, acc = carry[hh]
            l = acc[HEAD_DIM:HEAD_DIM + 1, :]
            outs.append(acc[:HEAD_DIM, :] / l)
            lses.append(m + jnp.log2(l))
        o2 = jnp.concatenate(outs, axis=0).T
        o32_ref[...] = o2
        o_ref[...] = o2.astype(BF)
        lse_ref[0] = jnp.concatenate(lses, axis=0)

        @pl.when(step == n_steps - 1)
        def _():
            _gather_phase(2, *comm)

    return pl.pallas_call(
        body, name="fox_fwd", grid=(N_HEADS // 2, n_q),
        out_shape=(jax.ShapeDtypeStruct((t_len, GROUP_W), BF), jax.ShapeDtypeStruct((t_len, GROUP_W), F32),
                   jax.ShapeDtypeStruct((N_HEADS // 2, 2, t_len), F32))
        + tuple(jax.ShapeDtypeStruct((4,) + s.shape, s.dtype) for s in shards),
        in_specs=[pl.BlockSpec((tq, 2 * LANES), lambda hp, i: (i, hp)),
                  pl.BlockSpec((t_len, 2 * LANES), lambda hp, i: (0, hp)),
                  pl.BlockSpec((t_len, LANES), lambda hp, i: (0, v_col + hp))] + [ANY] * n_w,
        out_specs=(pl.BlockSpec((tq, LANES), lambda hp, i: (i, hp)), pl.BlockSpec((tq, LANES), lambda hp, i: (i, hp)),
                   pl.BlockSpec((1, 2, tq), lambda hp, i: (hp, 0, i))) + (ANY,) * n_w,
        scratch_shapes=[pltpu.VMEM((LANES, t_len), BF)] + _gather_scratch(n_w),
        compiler_params=_cparams(("arbitrary", "arbitrary")),
    )(q_aug, k_aug, proj, *shards)


def _softmax_rows(s):
    p = jnp.exp(s - jnp.max(s, axis=-1, keepdims=True))
    return p / jnp.sum(p, axis=-1, keepdims=True)


def _attn_out_xattn_fwd(x, mix_r, mix_f, w_out, g_xattn, w_xq, g_xq, kn, v, w_xo):
    t_len = x.shape[0]
    tm = min(ROW_TILE, t_len)

    def body(x_ref, mr_ref, mf_ref, wo_ref, g_ref, wq_ref, gq_ref, kn_ref, v_ref, wxo_ref,
             h1_ref, hn_ref, qx_ref, o_ref, h2_ref):
        h1 = x_ref[...] + jnp.dot(mr_ref[...], wo_ref[:GROUP_W, :], preferred_element_type=F32) \
            + jnp.dot(mf_ref[...], wo_ref[GROUP_W:, :], preferred_element_type=F32)
        h1_ref[...] = h1
        hn = _rms_fwd(h1, g_ref[...]).astype(BF)
        hn_ref[...] = hn
        qx = jnp.dot(hn, wq_ref[...], preferred_element_type=F32).astype(BF)
        qx_ref[...] = qx
        for h in range(N_XH):
            sl = slice(h * XHD, (h + 1) * XHD)
            qn = _rms_fwd(qx[:, sl].astype(F32), gq_ref[...])
            p = _softmax_rows(_dot_nt(qn, kn_ref[:, sl]) * (XHD ** -0.5))
            o_ref[:, sl] = _dot(p, v_ref[:, sl]).astype(BF)
        h2_ref[...] = h1 + jnp.dot(o_ref[...], wxo_ref[...], preferred_element_type=F32)

    row_spec = lambda w: pl.BlockSpec((tm, w), lambda i: (i, 0))
    full = lambda a: pl.BlockSpec(a.shape, lambda i: (0,) * a.ndim)
    return pl.pallas_call(
        body, name="attn_out_xattn_fwd", grid=(t_len // tm,),
        out_shape=(jax.ShapeDtypeStruct((t_len, D_MODEL), F32), jax.ShapeDtypeStruct((t_len, D_MODEL), BF),
                   jax.ShapeDtypeStruct((t_len, D_MODEL), BF), jax.ShapeDtypeStruct((t_len, D_MODEL), BF),
                   jax.ShapeDtypeStruct((t_len, D_MODEL), F32)),
        in_specs=[row_spec(D_MODEL), row_spec(GROUP_W), row_spec(GROUP_W), full(w_out), full(g_xattn), full(w_xq), full(g_xq),
                  full(kn), full(v), full(w_xo)],
        out_specs=(row_spec(D_MODEL),) * 5,
        compiler_params=_cparams(("arbitrary",)),
    )(x, mix_r, mix_f, w_out, g_xattn, w_xq, g_xq, kn, v, w_xo)


def _ffn_loss_fwd(h2, g_ffn, w_gate, w_up, w_down, target):
    t_len = h2.shape[0]
    tm = min(ROW_TILE, t_len)

    def body(h2_ref, g_ref, wg_ref, wu_ref, wd_ref, tgt_ref, hn_ref, gate_ref, up_ref, act_ref, dh3_ref, loss_ref):
        @pl.when(pl.program_id(0) == 0)
        def _():
            loss_ref[...] = jnp.zeros_like(loss_ref)

        h2v = h2_ref[...]
        hn = _rms_fwd(h2v, g_ref[...]).astype(BF)
        hn_ref[...] = hn
        gate = jnp.dot(hn, wg_ref[...], preferred_element_type=F32)
        up = jnp.dot(hn, wu_ref[...], preferred_element_type=F32)
        gate_ref[...] = gate.astype(BF)
        up_ref[...] = up.astype(BF)
        act = (gate * _sigmoid(gate) * up).astype(BF)
        act_ref[...] = act
        diff = h2v + jnp.dot(act, wd_ref[...], preferred_element_type=F32) - tgt_ref[...]
        dh3_ref[...] = diff * (1.0 / D_MODEL)
        per_row = jnp.sum(diff * diff, axis=-1, keepdims=True) * (1.0 / D_MODEL)
        loss_ref[...] += 0.5 * jnp.sum(per_row, axis=0, keepdims=True)

    row_spec = lambda w: pl.BlockSpec((tm, w), lambda i: (i, 0))
    full = lambda a: pl.BlockSpec(a.shape, lambda i: (0,) * a.ndim, pipeline_mode=pl.Buffered(1))
    return pl.pallas_call(
        body, name="ffn_loss_fwd", grid=(t_len // tm,),
        out_shape=(jax.ShapeDtypeStruct((t_len, D_MODEL), BF), jax.ShapeDtypeStruct((t_len, D_FF), BF),
                   jax.ShapeDtypeStruct((t_len, D_FF), BF), jax.ShapeDtypeStruct((t_len, D_FF), BF),
                   jax.ShapeDtypeStruct((t_len, D_MODEL), F32), jax.ShapeDtypeStruct((8, LANES), F32)),
        in_specs=[row_spec(D_MODEL), full(g_ffn), full(w_gate), full(w_up), full(w_down), row_spec(D_MODEL)],
        out_specs=(row_spec(D_MODEL), row_spec(D_FF), row_spec(D_FF), row_spec(D_FF), row_spec(D_MODEL),
                   pl.BlockSpec((8, LANES), lambda i: (0, 0))),
        compiler_params=_cparams(("arbitrary",)),
    )(h2, g_ffn, w_gate, w_up, w_down, target)


def _ffn_bwd(dh3, gate, up, h2, g_ffn, w_gate, w_up, w_down):
    t_len = h2.shape[0]
    tm = min(ROW_TILE, t_len)

    def body(dh3_ref, gate_ref, up_ref, h2_ref, g_ref, wg_ref, wu_ref, wd_ref, dgate_ref, dup_ref, dh2_ref, dg_ref):
        @pl.when(pl.program_id(0) == 0)
        def _():
            dg_ref[...] = jnp.zeros_like(dg_ref)

        dh3v = dh3_ref[...]
        dact = _dot_nt(dh3v, wd_ref[...])
        g = gate_ref[...].astype(F32)
        sg = _sigmoid(g)
        dup = (dact * (g * sg)).astype(BF)
        dgate = (dact * up_ref[...].astype(F32) * (sg * (1.0 + g * (1.0 - sg)))).astype(BF)
        dup_ref[...] = dup
        dgate_ref[...] = dgate
        dhn = _dot_nt(dgate, wg_ref[...]) + _dot_nt(dup, wu_ref[...])
        dx, dg = _rms_bwd(h2_ref[...], g_ref[...], dhn)
        dh2_ref[...] = dh3v + dx
        dg_ref[...] += dg

    row_spec = lambda w: pl.BlockSpec((tm, w), lambda i: (i, 0))
    full = lambda a: pl.BlockSpec(a.shape, lambda i: (0,) * a.ndim, pipeline_mode=pl.Buffered(1))
    return pl.pallas_call(
        body, name="ffn_bwd", grid=(t_len // tm,),
        out_shape=(jax.ShapeDtypeStruct((t_len, D_FF), BF), jax.ShapeDtypeStruct((t_len, D_FF), BF),
                   jax.ShapeDtypeStruct((t_len, D_MODEL), F32), jax.ShapeDtypeStruct((1, D_MODEL), F32)),
        in_specs=[row_spec(D_MODEL), row_spec(D_FF), row_spec(D_FF), row_spec(D_MODEL), full(g_ffn), full(w_gate), full(w_up),
                  full(w_down)],
        out_specs=(row_spec(D_FF), row_spec(D_FF), row_spec(D_MODEL), pl.BlockSpec((1, D_MODEL), lambda i: (0, 0))),
        compiler_params=_cparams(("arbitrary",)),
    )(dh3, gate, up, h2, g_ffn, w_gate, w_up, w_down)


def _attn_out_xattn_bwd(dh2, h1, qx, kn, v, w_xo, w_xq, w_out, g_xattn, g_xq):
    t_len = h1.shape[0]
    tm = min(ROW_TILE, t_len)
    m_tok = kn.shape[0]

    def body(dh2_ref, h1_ref, qx_ref, kn_ref, v_ref, wxo_ref, wq_ref, wo_ref, g_ref, gq_ref,
             dqx_ref, dh1_ref, dmr_ref, dmf_ref, dkn_ref, dv_ref, dg_ref, dgq_ref, dqx_scr):
        @pl.when(pl.program_id(0) == 0)
        def _():
            dkn_ref[...] = jnp.zeros_like(dkn_ref)
            dv_ref[...] = jnp.zeros_like(dv_ref)
            dg_ref[...] = jnp.zeros_like(dg_ref)
            dgq_ref[...] = jnp.zeros_like(dgq_ref)

        dh2v = dh2_ref[...]
        do = _dot_nt(dh2v, wxo_ref[...])
        gq = gq_ref[...]
        dgq = jnp.zeros((1, XHD), F32)
        for h in range(N_XH):
            sl = slice(h * XHD, (h + 1) * XHD)
            qraw = qx_ref[:, sl].astype(F32)
            qn = _rms_fwd(qraw, gq)
            p = _softmax_rows(_dot_nt(qn, kn_ref[:, sl]) * (XHD ** -0.5))
            doh = do[:, sl]
            dv_ref[:, sl] += _dot_tn(p, doh)
            dp = _dot_nt(doh, v_ref[:, sl])
            ds = p * (dp - jnp.sum(dp * p, axis=-1, keepdims=True)) * (XHD ** -0.5)
            dqn = _dot(ds, kn_ref[:, sl])
            dkn_ref[:, sl] += _dot_tn(ds, qn)
            dx, dg_h = _rms_bwd(qraw, gq, dqn)
            dgq = dgq + dg_h
            dqx_scr[:, sl] = dx.astype(BF)
        dgq_ref[...] += dgq
        dqx = dqx_scr[...]
        dqx_ref[...] = dqx
        dhn = _dot_nt(dqx, wq_ref[...])
        dx, dg = _rms_bwd(h1_ref[...], g_ref[...], dhn)
        dg_ref[...] += dg
        dh1 = dh2v + dx
        dh1_ref[...] = dh1
        dmix = _dot_nt(dh1, wo_ref[...])
        dmr_ref[...] = dmix[:, :GROUP_W]
        dmf_ref[...] = dmix[:, GROUP_W:].astype(BF)

    row_spec = lambda w: pl.BlockSpec((tm, w), lambda i: (i, 0))
    full = lambda a: pl.BlockSpec(a.shape, lambda i: (0,) * a.ndim)
    acc = lambda r, c: pl.BlockSpec((r, c), lambda i: (0, 0))
    return pl.pallas_call(
        body, name="attn_out_xattn_bwd", grid=(t_len // tm,),
        out_shape=(jax.ShapeDtypeStruct((t_len, D_MODEL), BF), jax.ShapeDtypeStruct((t_len, D_MODEL), F32),
                   jax.ShapeDtypeStruct((t_len, GROUP_W), F32), jax.ShapeDtypeStruct((t_len, GROUP_W), BF),
                   jax.ShapeDtypeStruct((m_tok, D_MODEL), F32), jax.ShapeDtypeStruct((m_tok, D_MODEL), F32),
                   jax.ShapeDtypeStruct((1, D_MODEL), F32), jax.ShapeDtypeStruct((1, XHD), F32)),
        in_specs=[row_spec(D_MODEL), row_spec(D_MODEL), row_spec(D_MODEL), full(kn), full(v), full(w_xo), full(w_xq), full(w_out),
                  full(g_xattn), full(g_xq)],
        out_specs=(row_spec(D_MODEL), row_spec(D_MODEL), row_spec(GROUP_W), row_spec(GROUP_W), acc(m_tok, D_MODEL),
                   acc(m_tok, D_MODEL), acc(1, D_MODEL), acc(1, XHD)),
        scratch_shapes=[pltpu.VMEM((tm, D_MODEL), BF)],
        compiler_params=_cparams(("arbitrary",)),
    )(dh2, h1, qx, kn, v, w_xo, w_xq, w_out, g_xattn, g_xq)


def _mem_kv_bwd(dkn, dv, kraw, mem, memn, g_mem, g_xk, w_xkv):
    m_tok = mem.shape[0]

    def body(dkn_ref, dv_ref, kraw_ref, mem_ref, memn_ref, gm_ref, gk_ref, w_ref, dw_ref, dgm_ref, dgk_ref, dkv_scr):
        gk = gk_ref[...]
        dgk = jnp.zeros((1, XHD), F32)
        for h in range(N_XH):
            sl = slice(h * XHD, (h + 1) * XHD)
            dx, dg_h = _rms_bwd(kraw_ref[:, sl], gk, dkn_ref[:, sl])
            dgk = dgk + dg_h
            dkv_scr[:, sl] = dx.astype(BF)
        dgk_ref[...] = dgk
        dkv_scr[:, D_MODEL:] = dv_ref[...].astype(BF)
        dkv = dkv_scr[...]
        dw_ref[...] = _dot_tn(memn_ref[...], dkv)
        dmemn = _dot_nt(dkv, w_ref[...])
        mem_v = mem_ref[...]
        r = lax.rsqrt(jnp.mean(mem_v * mem_v, axis=-1, keepdims=True) + EPS)
        dgm_ref[...] = jnp.sum(dmemn * mem_v * r, axis=0, keepdims=True)

    return pl.pallas_call(
        body, name="mem_kv_bwd",
        out_shape=(jax.ShapeDtypeStruct((D_MODEL, 2 * D_MODEL), F32), jax.ShapeDtypeStruct((1, D_MODEL), F32),
                   jax.ShapeDtypeStruct((1, XHD), F32)),
        in_specs=[VMEM_SPEC] * 8, out_specs=(VMEM_SPEC,) * 3,
        scratch_shapes=[pltpu.VMEM((m_tok, 2 * D_MODEL), BF)],
        compiler_params=_cparams(),
    )(dkn, dv, kraw, mem, memn, g_mem, g_xk, w_xkv)


def _fox_bwd(q_aug, k_aug, proj, dmf, o32, lse, sums):
    t_len = q_aug.shape[0]
    tb = min(ATT_BLOCK, t_len)
    n_b = t_len // tb
    v_col = 6 * GROUP_W // LANES
    n_w = len(sums)
    n_steps = (N_HEADS // 2) * n_b

    def body(*refs):
        k_ref, v_ref, q_ref, do_ref, o_ref, lse_ref = refs[:6]
        dq_ref, dk_ref, dv_ref, df_ref = refs[6 + n_w:10 + n_w]
        delta = refs[10 + 2 * n_w]
        comm = (refs[6:6 + n_w], refs[10 + n_w:10 + 2 * n_w]) + tuple(refs[11 + 2 * n_w:])
        j = pl.program_id(1)
        step = pl.program_id(0) * n_b + j

        @pl.when(step == 0)
        def _():
            _scatter_phase(0, *comm)

        @pl.when(j == 0)
        def _():
            dq_ref[...] = jnp.zeros_like(dq_ref)
            dd = do_ref[...].astype(F32) * o_ref[...]
            hrow = lax.broadcasted_iota(jnp.int32, (8, LANES), 0)
            lane = lax.broadcasted_iota(jnp.int32, (8, LANES), 1)
            ind = ((lane // HEAD_DIM) == hrow).astype(BF)
            delta[...] = _dot_nt_exact(ind, dd)

        k2, v2 = k_ref[...], v_ref[...]
        ks = [k2[:, hh * LANES:(hh + 1) * LANES] for hh in range(2)]
        vs = [v2[:, hh * HEAD_DIM:(hh + 1) * HEAD_DIM] for hh in range(2)]

        def block(i, carry, masked):
            rows = pl.ds(pl.multiple_of(i * tb, tb), tb)
            q2 = q_ref[rows, :]
            do2 = do_ref[rows, :]
            qs = [q2[:, hh * LANES:(hh + 1) * LANES] for hh in range(2)]
            dos = [do2[:, hh * HEAD_DIM:(hh + 1) * HEAD_DIM] for hh in range(2)]
            ss = [_dot_nt(ks[hh], qs[hh]) for hh in range(2)]
            dps = [_dot_nt(vs[hh], dos[hh]) for hh in range(2)]
            pts, dsts, dfs = [], [], []
            for hh in range(2):
                s_t = ss[hh]
                if masked:
                    krow = lax.broadcasted_iota(jnp.int32, (tb, tb), 0)
                    qcol = lax.broadcasted_iota(jnp.int32, (tb, tb), 1)
                    s_t = jnp.where(qcol >= krow, s_t, NEG)
                p_t = jnp.exp2(s_t - lse_ref[0, hh:hh + 1, rows])
                pts.append(p_t.astype(BF))
                ds_t = p_t * (dps[hh] - delta[hh:hh + 1, rows])
                dsts.append(ds_t.astype(BF))
                dfs.append(jnp.sum(ds_t, axis=-1, keepdims=True))
            out = []
            for hh in range(2):
                dk, dv, df = carry[hh]
                dv = dv + jnp.dot(pts[hh], dos[hh], preferred_element_type=F32)
                dk = dk + jnp.dot(dsts[hh], qs[hh], preferred_element_type=F32)
                dq_ref[rows, hh * HEAD_DIM:(hh + 1) * HEAD_DIM] += _dot_tn(dsts[hh], ks[hh])[:, :HEAD_DIM]
                out.append((dk, dv, df - dfs[hh]))
            return tuple(out)

        init = tuple((jnp.zeros((tb, LANES), F32), jnp.zeros((tb, HEAD_DIM), F32), jnp.zeros((tb, 1), F32)) for _ in range(2))
        carry = block(j, init, True)
        carry = lax.fori_loop(j + 1, n_b, lambda i, c: block(i, c, False), carry)
        dk_ref[...] = jnp.concatenate([carry[hh][0][:, :HEAD_DIM] for hh in range(2)], axis=-1) * LN2
        dv_ref[...] = jnp.concatenate([carry[hh][1] for hh in range(2)], axis=-1)
        df_ref[0] = jnp.concatenate([carry[hh][2] for hh in range(2)], axis=-1)

        @pl.when(step == n_steps - 1)
        def _():
            _scatter_phase(1, *comm)

    blk = lambda w, col0: pl.BlockSpec((tb, w), lambda hp, j: (j, col0 + hp))
    whole = lambda w: pl.BlockSpec((t_len, w), lambda hp, j: (0, hp))
    rows2 = pl.BlockSpec((1, 2, t_len), lambda hp, j: (hp, 0, 0))
    cols2 = pl.BlockSpec((1, tb, 2), lambda hp, j: (hp, j, 0))
    return pl.pallas_call(
        body, name="fox_bwd", grid=(N_HEADS // 2, n_b),
        out_shape=(jax.ShapeDtypeStruct((t_len, GROUP_W), F32), jax.ShapeDtypeStruct((t_len, GROUP_W), F32),
                   jax.ShapeDtypeStruct((t_len, GROUP_W), F32), jax.ShapeDtypeStruct((N_HEADS // 2, t_len, 2), F32))
        + _scatter_out_shapes(sums),
        in_specs=[blk(2 * LANES, 0), blk(LANES, v_col), whole(2 * LANES), whole(LANES), whole(LANES), rows2] + [ANY] * n_w,
        out_specs=(whole(LANES), blk(LANES, 0), blk(LANES, 0), cols2) + (ANY,) * n_w,
        scratch_shapes=[pltpu.VMEM((8, t_len), F32)] + _scatter_scratch(n_w),
        compiler_params=_cparams(("arbitrary", "arbitrary")),
    )(k_aug, proj, q_aug, dmf, o32, lse, *sums)


def _retention_bwd(dmr, raw, proj, g_ret, rq, rk, states, tables):
    t_len = rq.shape[0]
    c = min(RET_BLOCK, t_len)
    n_b = t_len // c
    wdec, qdec, kdec, cdec = tables
    v_col, g_col = 2 * GROUP_W // LANES, 3 * GROUP_W // LANES

    def body(d_ref, raw_ref, rg_ref, g_ref, q_ref, k_ref, v_ref, st_ref, w_ref, qd_ref, kd_ref, cd_ref,
             dq_ref, dk_ref, dv_ref, drg_ref, dg_ref, gstate):
        @pl.when(pl.program_id(1) == 0)
        def _():
            gstate[...] = jnp.zeros_like(gstate)
            dg_ref[...] = jnp.zeros_like(dg_ref)

        d, raw_v, g = d_ref[...], raw_ref[...], g_ref[0]
        gate = rg_ref[...].astype(F32)
        xc = raw_v - _group_mean64(raw_v)
        r = lax.rsqrt(_group_mean64(xc * xc) + EPS)
        xh = xc * r
        sg = _sigmoid(gate)
        drg_ref[...] = d * (xh * g) * (sg * (1.0 + gate * (1.0 - sg)))
        dy = d * (gate * sg)
        dg_ref[0] += jnp.sum(dy * xh, axis=0, keepdims=True)
        dxh = dy * g
        do2 = r * (dxh - _group_mean64(dxh) - xh * _group_mean64(dxh * xh))
        q2, k2, v2 = q_ref[...], k_ref[...], v_ref[...]
        dqs, dks, dvs = [], [], []
        for hh in range(2):
            sl = slice(hh * HEAD_DIM, (hh + 1) * HEAD_DIM)
            q, k, v, do = q2[:, sl], k2[:, sl], v2[:, sl], do2[:, sl].astype(BF)
            w = w_ref[hh]
            a = _dot_nt(q, k) * w
            dm = _dot_nt(do, v) * w
            sp, gs = st_ref[0, 0, hh], gstate[hh]
            qd = q.astype(F32) * qd_ref[hh]
            kd = k.astype(F32) * kd_ref[hh]
            dqs.append(_dot(dm, k) + _dot_nt(do, sp) * qd_ref[hh])
            dks.append(_dot_tn(dm, q) + _dot_nt(v, gs) * kd_ref[hh])
            dvs.append(_dot_tn(a, do) + _dot(kd, gs))
            gstate[hh] = gs * cd_ref[hh] + _dot_tn(qd, do)
        dq_ref[...] = jnp.concatenate(dqs, axis=-1)
        dk_ref[...] = jnp.concatenate(dks, axis=-1)
        dv_ref[...] = jnp.concatenate(dvs, axis=-1)

    blk = lambda col0: pl.BlockSpec((c, LANES), lambda hp, i: (n_b - 1 - i, col0 + hp))
    tab = lambda a: pl.BlockSpec((2,) + a.shape[1:], lambda hp, i: (hp, 0, 0))
    gspec = pl.BlockSpec((1, 1, LANES), lambda hp, i: (hp, 0, 0))
    return pl.pallas_call(
        body, name="retention_bwd", grid=(N_HEADS // 2, n_b),
        out_shape=(jax.ShapeDtypeStruct((t_len, GROUP_W), F32),) * 4 + (jax.ShapeDtypeStruct((N_HEADS // 2, 1, LANES), F32),),
        in_specs=[blk(0), blk(0), blk(g_col), gspec, blk(0), blk(0), blk(v_col),
                  pl.BlockSpec((1, 1, 2, HEAD_DIM, HEAD_DIM), lambda hp, i: (hp, n_b - 1 - i, 0, 0, 0)),
                  tab(wdec), tab(qdec), tab(kdec), tab(cdec)],
        out_specs=(blk(0), blk(0), blk(0), blk(0), gspec),
        scratch_shapes=[pltpu.VMEM((2, HEAD_DIM, HEAD_DIM), F32)],
        compiler_params=_cparams(("arbitrary", "arbitrary")),
    )(dmr, raw, proj, g_ret, rq, rk, proj, states, wdec, qdec, kdec, cdec)


def _in_proj_bwd(x, g_mix, dh1, dq_r, dk_r, dv_r, drg, dq_f, dk_f, dv_f, df_col, proj, z, cos_t, sin_t, gq_t, gk_t, w_main, w_ff):
    t_len = x.shape[0]
    tm = min(ROW_TILE, t_len)
    n_t = t_len // tm

    def body(x_ref, g_ref, dh1_ref, dqr_ref, dkr_ref, dvr_ref, drg_ref, dqf_ref, dkf_ref, dvf_ref, df_ref, fq_ref, fk_ref, z_ref,
             cos_ref, sin_ref, gq_ref, gk_ref, wm_ref, wf_ref,
             dproj_ref, dz_ref, dx_ref, dg_ref, dgq_ref, dgk_ref, db_ref, carry, gq_acc, gk_acc):
        i = pl.program_id(0)

        @pl.when(i == 0)
        def _():
            carry[...] = jnp.zeros_like(carry)
            gq_acc[...] = jnp.zeros_like(gq_acc)
            gk_acc[...] = jnp.zeros_like(gk_acc)
            dg_ref[...] = jnp.zeros_like(dg_ref)
            db_ref[...] = jnp.zeros_like(db_ref)

        c, s = cos_ref[...], sin_ref[...]
        gq, gk = gq_ref[...], gk_ref[...]
        dgq = jnp.zeros((1, LANES), F32)
        dgk = jnp.zeros((1, LANES), F32)
        for sl in _chunks(GROUP_W):
            dy = dqr_ref[:, sl] * 0.125
            dproj_ref[:, sl] = (dy * c + _swap32(dy * s)).astype(BF)
            dy = dkr_ref[:, sl]
            dproj_ref[:, GROUP_W + sl.start:GROUP_W + sl.stop] = (dy * c + _swap32(dy * s)).astype(BF)
            dproj_ref[:, 2 * GROUP_W + sl.start:2 * GROUP_W + sl.stop] = dvr_ref[:, sl].astype(BF)
            dproj_ref[:, 3 * GROUP_W + sl.start:3 * GROUP_W + sl.stop] = drg_ref[:, sl].astype(BF)
            for src, dsrc, gain, off in ((fq_ref, dqf_ref, gq, 4), (fk_ref, dkf_ref, gk, 5)):
                xr = src[:, sl].astype(F32)
                r = lax.rsqrt(_group_mean64(xr * xr) + EPS)
                xh = xr * r
                dy = dsrc[:, sl] * (0.125 if off == 4 else 1.0)
                dgs = jnp.sum(dy * xh, axis=0, keepdims=True)
                if off == 4:
                    dgq = dgq + dgs
                else:
                    dgk = dgk + dgs
                dxh = dy * gain
                dproj_ref[:, off * GROUP_W + sl.start:off * GROUP_W + sl.stop] = \
                    (r * (dxh - xh * _group_mean64(dxh * xh))).astype(BF)
            dproj_ref[:, 6 * GROUP_W + sl.start:6 * GROUP_W + sl.stop] = dvf_ref[:, sl].astype(BF)
        gq_acc[...] += dgq
        gk_acc[...] += dgk
        row = lax.broadcasted_iota(jnp.int32, (tm, tm), 0)
        col = lax.broadcasted_iota(jnp.int32, (tm, tm), 1)
        dlf = _dot_exact((col >= row).astype(BF), df_ref[...]) + carry[0:1, :]
        carry[...] = jnp.broadcast_to(dlf[0:1, :], carry.shape)
        lane = lax.broadcasted_iota(jnp.int32, (tm, LANES), 1)
        dz = jnp.where(lane < N_HEADS, dlf / (1.0 + jnp.exp(z_ref[...])), 0.0)
        db_ref[...] += jnp.sum(dz, axis=0, keepdims=True)
        dz_bf = dz.astype(BF)
        dz_ref[...] = dz_bf
        dn1 = _dot_nt(dz_bf, wf_ref[...])
        for sec in range(MAIN_W // GROUP_W):
            sl = slice(sec * GROUP_W, (sec + 1) * GROUP_W)
            dn1 = dn1 + _dot_nt(dproj_ref[:, sl], wm_ref[:, sl])
        dx, dg = _rms_bwd(x_ref[...], g_ref[...], dn1)
        dx_ref[...] = dh1_ref[...] + dx
        dg_ref[...] += dg

        @pl.when(i == n_t - 1)
        def _():
            dgq_ref[...] = gq_acc[:, :HEAD_DIM] + gq_acc[:, HEAD_DIM:]
            dgk_ref[...] = gk_acc[:, :HEAD_DIM] + gk_acc[:, HEAD_DIM:]

    row_spec = lambda w, col=0: pl.BlockSpec((tm, w), lambda i: (n_t - 1 - i, col))
    full = lambda a: pl.BlockSpec(a.shape, lambda i: (0,) * a.ndim)
    acc = lambda r, c: pl.BlockSpec((r, c), lambda i: (0, 0))
    return pl.pallas_call(
        body, name="in_proj_bwd", grid=(n_t,),
        out_shape=(jax.ShapeDtypeStruct((t_len, MAIN_W), BF), jax.ShapeDtypeStruct((t_len, LANES), BF),
                   jax.ShapeDtypeStruct((t_len, D_MODEL), F32), jax.ShapeDtypeStruct((1, D_MODEL), F32),
                   jax.ShapeDtypeStruct((1, HEAD_DIM), F32), jax.ShapeDtypeStruct((1, HEAD_DIM), F32),
                   jax.ShapeDtypeStruct((1, LANES), F32)),
        in_specs=[row_spec(D_MODEL), full(g_mix), row_spec(D_MODEL)] + [row_spec(GROUP_W)] * 7
        + [row_spec(LANES), row_spec(GROUP_W, 4), row_spec(GROUP_W, 5), row_spec(LANES), row_spec(LANES), row_spec(LANES),
           full(gq_t), full(gk_t), full(w_main), full(w_ff)],
        out_specs=(row_spec(MAIN_W), row_spec(LANES), row_spec(D_MODEL), acc(1, D_MODEL), acc(1, HEAD_DIM), acc(1, HEAD_DIM),
                   acc(1, LANES)),
        scratch_shapes=[pltpu.VMEM((8, LANES), F32), pltpu.VMEM((1, LANES), F32), pltpu.VMEM((1, LANES), F32)],
        compiler_params=_cparams(("arbitrary",)),
    )(x, g_mix, dh1, dq_r, dk_r, dv_r, drg, dq_f, dk_f, dv_f, df_col, proj, proj, z, cos_t, sin_t, gq_t, gk_t, w_main, w_ff)


def _matmul_tn(a, b, name, bm=256, bk=512):
    t_len, m = a.shape
    n = b.shape[1]
    bm, bk = min(bm, m), min(bk, t_len)

    def body(a_ref, b_ref, o_ref):
        @pl.when(pl.program_id(1) == 0)
        def _():
            o_ref[...] = jnp.zeros_like(o_ref)

        o_ref[...] += _dot_tn(a_ref[...], b_ref[...])

    return pl.pallas_call(
        body, name=name, grid=(m // bm, t_len // bk),
        out_shape=jax.ShapeDtypeStruct((m, n), F32),
        in_specs=[pl.BlockSpec((bk, bm), lambda i, k: (k, i)), pl.BlockSpec((bk, n), lambda i, k: (k, 0))],
        out_specs=pl.BlockSpec((bm, n), lambda i, k: (i, 0)),
        compiler_params=_cparams(("arbitrary", "arbitrary")),
    )(a, b)


def _place():
    x, y, c = lax.axis_index("x"), lax.axis_index("y"), lax.axis_index("c")
    chips = [(1 - x, y), (x, 1 - y), (1 - x, 1 - y)]
    return x, y, c, chips


def _row_chunks(rows, limit):
    step = max(d for d in range(16, min(rows, limit) + 1, 16) if rows % d == 0)
    return [slice(i, i + step) for i in range(0, rows, step)]


ICI_CHUNK_ROWS = 128
D2D_CHUNK_ROWS = 64


def _gather_phase(phase, ins, outs, send_sems, recv_sems):
    x, y, c, chips = _place()
    me_chip = 2 * x + y
    sibling = (x, y, 1 - c)

    def copy(w, k, slot, half, to, rows=slice(None), src=None):
        dst = outs[w].at[slot, half, rows]
        return pltpu.make_async_remote_copy(src_ref=dst if src is None else src, dst_ref=dst,
                                            send_sem=send_sems.at[w, k], recv_sem=recv_sems.at[w, k],
                                            device_id=to, device_id_type=MESH)

    for w in range(len(ins)):
        for j, (px, py) in enumerate(chips):
            if phase == 0:
                for rows in _row_chunks(ins[w].shape[1], ICI_CHUNK_ROWS):
                    copy(w, j, me_chip, c, (px, py, c), rows, src=ins[w].at[c, rows]).start()
            elif phase == 1:
                copy(w, j, 2 * px + py, c, (x, y, c)).wait_recv()
                for rows in _row_chunks(ins[w].shape[1], D2D_CHUNK_ROWS):
                    copy(w, 3 + j, 2 * px + py, c, sibling, rows).start()
            else:
                copy(w, 3 + j, 2 * px + py, 1 - c, (x, y, c)).wait_recv()
                copy(w, j, me_chip, c, (px, py, c), src=ins[w].at[c]).wait_send()
                copy(w, 3 + j, 2 * px + py, c, sibling).wait_send()


def _gather_scratch(n_w):
    return [pltpu.SemaphoreType.DMA((n_w, 6)), pltpu.SemaphoreType.DMA((n_w, 6))]


def _all_gather_weights(shards):
    n_w = len(shards)

    def body(*refs):
        for phase in range(3):
            _gather_phase(phase, refs[:n_w], refs[n_w:2 * n_w], *refs[2 * n_w:])

    return pl.pallas_call(
        body, name="all_gather_weights",
        out_shape=tuple(jax.ShapeDtypeStruct((4,) + s.shape, s.dtype) for s in shards),
        in_specs=[ANY] * n_w, out_specs=(ANY,) * n_w, scratch_shapes=_gather_scratch(n_w),
    )(*shards)


def _exchange_core_halves(grads):
    n_w = len(grads)

    def body(*refs):
        ins, theirs = refs[:n_w], refs[n_w:2 * n_w]
        send_sems, recv_sems = refs[2 * n_w:]
        x, y, c, _ = _place()

        def remote(w, k=slice(None), rows=slice(None)):
            return pltpu.make_async_remote_copy(src_ref=ins[w].at[k, 1 - c, rows], dst_ref=theirs[w].at[k, rows],
                                                send_sem=send_sems.at[w], recv_sem=recv_sems.at[w], device_id=(x, y, 1 - c),
                                                device_id_type=MESH)

        for w in range(n_w):
            for k in range(4):
                for rows in _row_chunks(ins[w].shape[2], D2D_CHUNK_ROWS):
                    remote(w, k, rows).start()
        for w in range(n_w):
            remote(w).wait()

    half = tuple(jax.ShapeDtypeStruct((4,) + g.shape[2:], g.dtype) for g in grads)
    return pl.pallas_call(
        body, name="exchange_core_halves", out_shape=half,
        in_specs=[ANY] * n_w, out_specs=(ANY,) * n_w,
        scratch_shapes=[pltpu.SemaphoreType.DMA((n_w,)), pltpu.SemaphoreType.DMA((n_w,))],
    )(*grads)


def _add_pairs(a, b, name):
    _, r, c = a.shape
    rb = 32 if r % 32 == 0 else r

    def body(a_ref, b_ref, o_ref, ob_ref):
        s = a_ref[...] + b_ref[...]
        o_ref[...] = s
        ob_ref[...] = s.astype(BF)

    spec = pl.BlockSpec((4, rb, c), lambda i: (0, i, 0))
    return pl.pallas_call(
        body, name=name, grid=(r // rb,),
        out_shape=(jax.ShapeDtypeStruct(a.shape, F32), jax.ShapeDtypeStruct(a.shape, BF)),
        in_specs=[spec, spec], out_specs=(spec, spec), compiler_params=_cparams(("arbitrary",)),
    )(a, b)


def _scatter_phase(phase, bfs, got, send_sems, recv_sems):
    x, y, c, chips = _place()

    def remote(w, j, px, py, rows=slice(None)):
        return pltpu.make_async_remote_copy(src_ref=bfs[w].at[2 * px + py, rows], dst_ref=got[w].at[j, rows],
                                            send_sem=send_sems.at[w, j], recv_sem=recv_sems.at[w, j], device_id=(px, py, c),
                                            device_id_type=MESH)

    for w in range(len(bfs)):
        for j, (px, py) in enumerate(chips):
            if phase == 0:
                for rows in _row_chunks(bfs[w].shape[1], ICI_CHUNK_ROWS):
                    remote(w, j, px, py, rows).start()
            else:
                remote(w, j, px, py).wait()


def _scatter_scratch(n_w):
    return [pltpu.SemaphoreType.DMA((n_w, 3)), pltpu.SemaphoreType.DMA((n_w, 3))]


def _scatter_out_shapes(sums_bf16):
    return tuple(jax.ShapeDtypeStruct((3,) + s.shape[1:], BF) for s in sums_bf16)


def _scatter_to_chips(sums_bf16):
    n_w = len(sums_bf16)

    def body(*refs):
        for phase in range(2):
            _scatter_phase(phase, refs[:n_w], refs[n_w:2 * n_w], *refs[2 * n_w:])

    return pl.pallas_call(
        body, name="scatter_to_chips", out_shape=_scatter_out_shapes(sums_bf16),
        in_specs=[ANY] * n_w, out_specs=(ANY,) * n_w, scratch_shapes=_scatter_scratch(n_w),
    )(*sums_bf16)


def _add_received(own, got, name):
    r, c = own.shape
    rb = 32 if r % 32 == 0 else r

    def body(o_ref, g_ref, out_ref):
        out_ref[...] = ((o_ref[...] + g_ref[0].astype(F32)) + g_ref[1].astype(F32)) + g_ref[2].astype(F32)

    return pl.pallas_call(
        body, name=name, grid=(r // rb,), out_shape=jax.ShapeDtypeStruct((r, c), F32),
        in_specs=[pl.BlockSpec((rb, c), lambda i: (i, 0)), pl.BlockSpec((3, rb, c), lambda i: (0, i, 0))],
        out_specs=pl.BlockSpec((rb, c), lambda i: (i, 0)), compiler_params=_cparams(("arbitrary",)),
    )(own, got)


def _share_with_sibling(halves):
    n_w = len(halves)

    def body(*refs):
        ins, outs = refs[:n_w], refs[n_w:2 * n_w]
        send_sems, recv_sems = refs[2 * n_w:]
        x, y, c, _ = _place()

        def remote(w, rows=slice(None)):
            return pltpu.make_async_remote_copy(src_ref=ins[w].at[rows], dst_ref=outs[w].at[c, rows], send_sem=send_sems.at[w],
                                                recv_sem=recv_sems.at[w], device_id=(x, y, 1 - c), device_id_type=MESH)

        for w in range(n_w):
            for rows in _row_chunks(ins[w].shape[0], D2D_CHUNK_ROWS):
                remote(w, rows).start()
        for w in range(n_w):
            remote(w).wait()

    return pl.pallas_call(
        body, name="share_with_sibling",
        out_shape=tuple(jax.ShapeDtypeStruct((2,) + h.shape, h.dtype) for h in halves),
        in_specs=[ANY] * n_w, out_specs=(ANY,) * n_w,
        scratch_shapes=[pltpu.SemaphoreType.DMA((n_w,)), pltpu.SemaphoreType.DMA((n_w,))],
    )(*halves)


def _all_reduce_small(pack):
    r, c = pack.shape

    def body(p_ref, out_ref, slots, send_sems, recv_sems):
        x, y, cc, _ = _place()
        me = 4 * x + 2 * y + cc
        slots[me] = p_ref[...]
        copies = []
        for k in range(1, 8):
            dx, dy, dc = (k >> 2) & 1, (k >> 1) & 1, k & 1
            to = (1 - x if dx else x, 1 - y if dy else y, 1 - cc if dc else cc)
            cp = pltpu.make_async_remote_copy(src_ref=p_ref, dst_ref=slots.at[me], send_sem=send_sems.at[k - 1],
                                              recv_sem=recv_sems.at[k - 1], device_id=to, device_id_type=MESH)
            cp.start()
            copies.append(cp)
        for cp in copies:
            cp.wait()
        total = slots[0]
        for d in range(1, 8):
            total = total + slots[d]
        out_ref[...] = total

    return pl.pallas_call(
        body, name="all_reduce_small", out_shape=jax.ShapeDtypeStruct((r, c), F32),
        in_specs=[VMEM_SPEC], out_specs=VMEM_SPEC,
        scratch_shapes=[pltpu.VMEM((8, r, c), F32), pltpu.SemaphoreType.DMA((7,)), pltpu.SemaphoreType.DMA((7,))],
    )(pack)


def _adamw(w, g, m, v, name):
    r, c = w.shape
    rb = 64 if r % 64 == 0 else r
    c1 = 1.0 - ADAM_B1 ** ADAM_STEP
    c2 = 1.0 - ADAM_B2 ** ADAM_STEP

    def body(w_ref, g_ref, m_ref, v_ref, d_ref, nm_ref, nv_ref):
        gv = g_ref[...]
        nm = ADAM_B1 * m_ref[...] + (1.0 - ADAM_B1) * gv
        nv = ADAM_B2 * v_ref[...] + (1.0 - ADAM_B2) * (gv * gv)
        nm_ref[...] = nm
        nv_ref[...] = nv
        d_ref[...] = -ADAM_LR * ((nm / c1) / (jnp.sqrt(nv / c2) + ADAM_EPS) + ADAM_WD * w_ref[...])

    spec = pl.BlockSpec((rb, c), lambda i: (i, 0))
    return pl.pallas_call(
        body, name=name, grid=(r // rb,), out_shape=(jax.ShapeDtypeStruct((r, c), F32),) * 3,
        in_specs=[spec] * 4, out_specs=(spec,) * 3, compiler_params=_cparams(("arbitrary",)),
    )(w, g, m, v)


def _rope_tables(t_len):
    inv_freq = ROPE_BASE ** (-jnp.arange(0, HEAD_DIM, 2, dtype=F32) / HEAD_DIM)
    ang = jnp.arange(t_len, dtype=F32)[:, None] * inv_freq[None, :]
    cos, sin = jnp.cos(ang), jnp.sin(ang)
    cos_t = jnp.concatenate([cos, cos, cos, cos], axis=-1)
    sin_t = jnp.concatenate([-sin, sin, -sin, sin], axis=-1)
    return cos_t, sin_t


def _cols_to_shards(dw):
    r, n = dw.shape
    return jnp.transpose(dw.reshape(2, r // 2, 4, n // 4), (2, 0, 1, 3))


def _rows_to_shards(dw):
    r, n = dw.shape
    return dw.reshape(4, 2, r // 8, n)


def _pad_lanes(a):
    extra = -a.shape[-1] % LANES
    return a if extra == 0 else jnp.pad(a, [(0, 0)] * (a.ndim - 1) + [(0, extra)])


def _pad_row(a, width=D_MODEL):
    a = a.reshape(1, -1)
    return jnp.pad(a, ((0, 0), (0, width - a.shape[1])))


def kernel(x, mem, g_mix, w_in, b_forget, g_ret_out, g_fox_q, g_fox_k, w_out, g_xattn, w_xq, w_xkv, g_mem, g_xq, g_xk, w_xo, g_ffn, w_gate, w_up, w_down, loss_target, m_g_mix, m_w_in, m_b_forget, m_g_ret_out, m_g_fox_q, m_g_fox_k, m_w_out, m_g_xattn, m_w_xq, m_w_xkv, m_g_mem, m_g_xq, m_g_xk, m_w_xo, m_g_ffn, m_w_gate, m_w_up, m_w_down, v_g_mix, v_w_in, v_b_forget, v_g_ret_out, v_g_fox_q, v_g_fox_k, v_w_out, v_g_xattn, v_w_xq, v_w_xkv, v_g_mem, v_g_xq, v_g_xk, v_w_xo, v_g_ffn, v_w_gate, v_w_up, v_w_down):
    big = {"w_in": (w_in, m_w_in, v_w_in), "w_out": (w_out, m_w_out, v_w_out), "w_xq": (w_xq, m_w_xq, v_w_xq),
           "w_xkv": (w_xkv, m_w_xkv, v_w_xkv), "w_xo": (w_xo, m_w_xo, v_w_xo), "w_gate": (w_gate, m_w_gate, v_w_gate),
           "w_up": (w_up, m_w_up, v_w_up), "w_down": (w_down, m_w_down, v_w_down)}
    shards = {}
    for n in big:
        w = _pad_lanes(big[n][0][0].astype(BF))
        shards[n] = w.reshape(2, w.shape[0] // 2, w.shape[1])
    widths = {n: big[n][0].shape[2] for n in big}
    w_in_full = _assemble_weight("w_in", _all_gather_weights([shards["w_in"]])[0], shards["w_in"], widths["w_in"])
    small_w ={"g_mix": g_mix, "b_forget": b_forget, "g_ret_out": g_ret_out, "g_fox_q": g_fox_q, "g_fox_k": g_fox_k,
               "g_xattn": g_xattn, "g_mem": g_mem, "g_xq": g_xq, "g_xk": g_xk, "g_ffn": g_ffn}
    m_small = {"g_mix": m_g_mix, "b_forget": m_b_forget, "g_ret_out": m_g_ret_out, "g_fox_q": m_g_fox_q, "g_fox_k": m_g_fox_k,
               "g_xattn": m_g_xattn, "g_mem": m_g_mem, "g_xq": m_g_xq, "g_xk": m_g_xk, "g_ffn": m_g_ffn}
    v_small = {"g_mix": v_g_mix, "b_forget": v_b_forget, "g_ret_out": v_g_ret_out, "g_fox_q": v_g_fox_q, "g_fox_k": v_g_fox_k,
               "g_xattn": v_g_xattn, "g_mem": v_g_mem, "g_xq": v_g_xq, "g_xk": v_g_xk, "g_ffn": v_g_ffn}
    loss_part, grad_x, sums, got, small_g = _local_step(x[0], mem[0], loss_target[0], w_in_full, shards, widths, small_w)
    return _reduce_and_update(big, sums, got, small_w, small_g, loss_part, grad_x, m_small, v_small)


def _assemble_weight(name, gathered, own, width):
    my_chip = 2 * lax.axis_index("x") + lax.axis_index("y")
    g = lax.dynamic_update_slice(gathered, own[None], (my_chip, 0, 0, 0))
    rows = 2 * g.shape[2]
    g = g.reshape(4, rows, g.shape[3])[:, :, :width]
    return jnp.transpose(g, (1, 0, 2)).reshape(rows, 4 * width) if name in COL_SHARDED else g.reshape(4 * rows, width)


def _core_sums(names, dw):
    parts = [_pad_lanes(_cols_to_shards(dw[n]) if n in COL_SHARDED else _rows_to_shards(dw[n])) for n in names]
    my_core = lax.axis_index("c")
    theirs = _exchange_core_halves(parts)
    mine = [lax.dynamic_index_in_dim(p, my_core, axis=1, keepdims=False) for p in parts]
    return [_add_pairs(a, b, f"core_sum_{n}") for n, a, b in zip(names, mine, theirs)]


def _local_step(xs, mems, tgt, w_in_full, shards, widths, small_w):
    g_mix, b_forget, g_ret_out, g_fox_q, g_fox_k = (small_w[n] for n in ("g_mix", "b_forget", "g_ret_out", "g_fox_q", "g_fox_k"))
    g_xattn, g_mem, g_xq, g_xk, g_ffn = (small_w[n] for n in ("g_xattn", "g_mem", "g_xq", "g_xk", "g_ffn"))
    w_main = w_in_full[:, :MAIN_W]
    w_ff = jnp.pad(w_in_full[:, MAIN_W:], ((0, 0), (0, LANES - (IN_W - MAIN_W))))
    t_len = xs.shape[0]
    cos_t, sin_t = _rope_tables(t_len)
    tables = _decay_tables(min(RET_BLOCK, t_len))
    gq_t = jnp.concatenate([g_fox_q, g_fox_q], axis=-1)
    gk_t = jnp.concatenate([g_fox_k, g_fox_k], axis=-1)
    b_pad = _pad_row(b_forget, LANES)
    g_ret = g_ret_out.reshape(N_HEADS // 2, 1, LANES)

    n1, proj, rq, rk, q_aug, k_aug, z = _in_proj_fwd(xs, g_mix, w_main, w_ff, b_pad, cos_t, sin_t, gq_t, gk_t)
    raw, mix_r, states = _retention_fwd(rq, rk, proj, g_ret, tables)
    mix_f, o32, lse, *gathered = _fox_fwd(q_aug, k_aug, proj, [shards[n] for n in LATE])
    full = {n: _assemble_weight(n, g, shards[n], widths[n]) for n, g in zip(LATE, gathered)}
    memn, kraw, kn, vmem = _mem_kv_fwd(mems, g_mem, full["w_xkv"], g_xk)
    h1, hn2, qx, o_x, h2 = _attn_out_xattn_fwd(xs, mix_r, mix_f, full["w_out"], g_xattn, full["w_xq"], g_xq, kn, vmem, full["w_xo"])
    hn3, gate, up, act, dh3, loss_part = _ffn_loss_fwd(h2, g_ffn, full["w_gate"], full["w_up"], full["w_down"], tgt)

    dgate, dup, dh2, dg_ffn = _ffn_bwd(dh3, gate, up, h2, g_ffn, full["w_gate"], full["w_up"], full["w_down"])
    dqx, dh1, dmr, dmf, dkn, dvm, dg_xattn, dg_xq = _attn_out_xattn_bwd(dh2, h1, qx, kn, vmem, full["w_xo"], full["w_xq"],
                                                                      full["w_out"], g_xattn, g_xq)
    dw_xkv, dg_mem, dg_xk = _mem_kv_bwd(dkn, dvm, kraw, mems, memn, g_mem, g_xk, full["w_xkv"])
    dw = {
        "w_out": jnp.concatenate([_matmul_tn(mix_r, dh1, "dw_out_ret"), _matmul_tn(mix_f, dh1, "dw_out_fox")], axis=0),
        "w_xq": _matmul_tn(hn2, dqx, "dw_xq"),
        "w_xkv": dw_xkv,
        "w_xo": _matmul_tn(o_x, dh2, "dw_xo"),
        "w_gate": _matmul_tn(hn3, dgate, "dw_gate"),
        "w_up": _matmul_tn(hn3, dup, "dw_up"),
        "w_down": _matmul_tn(act, dh3, "dw_down"),
    }
    late_sums = _core_sums(LATE, dw)
    dq_f, dk_f, dv_f, df, *late_got = _fox_bwd(q_aug, k_aug, proj, dmf, o32, lse, [s[1] for s in late_sums])
    dq_r, dk_r, dv_r, drg, dg_ret = _retention_bwd(dmr, raw, proj, g_ret, rq, rk, states, tables)
    df_col = jnp.pad(jnp.transpose(df, (1, 0, 2)).reshape(t_len, N_HEADS), ((0, 0), (0, LANES - N_HEADS)))
    dproj, dz, grad_x, dg_mix, dg_fq, dg_fk, db = _in_proj_bwd(xs, g_mix, dh1, dq_r, dk_r, dv_r, drg, dq_f, dk_f, dv_f, df_col,
                                                              proj, z, cos_t, sin_t, gq_t, gk_t, w_main, w_ff)

    dw_in = jnp.concatenate([_matmul_tn(n1, dproj, "dw_in_main"), _matmul_tn(n1, dz, "dw_in_ff")[:, :IN_W - MAIN_W]], axis=1)
    in_sums = _core_sums(("w_in",), {"w_in": dw_in})
    in_got = _scatter_to_chips([in_sums[0][1]])
    sums = {n: s[0] for n, s in zip(("w_in",) + LATE, in_sums + late_sums)}
    got = dict(zip(("w_in",) + LATE, list(in_got) + late_got))
    small_g = {"g_mix": dg_mix, "b_forget": db[:, :N_HEADS], "g_ret_out": dg_ret, "g_fox_q": dg_fq, "g_fox_k": dg_fk,
               "g_xattn": dg_xattn, "g_mem": dg_mem, "g_xq": dg_xq, "g_xk": dg_xk, "g_ffn": dg_ffn}
    return loss_part, grad_x, sums, got, small_g


def _reduce_and_update(big, sums, got, small_w, small_g, loss_part, grad_x, m_small, v_small):
    big_names = list(big)
    my_core = lax.axis_index("c")
    my_chip = 2 * lax.axis_index("x") + lax.axis_index("y")
    own = [lax.dynamic_index_in_dim(sums[n], my_chip, axis=0, keepdims=False) for n in big_names]
    finals = [_add_received(o, got[n], f"chip_sum_{n}") for n, o in zip(big_names, own)]
    shared = _share_with_sibling(finals)
    grads, deltas, new_m, new_v = {}, {}, {}, {}
    for n, s, fin in zip(big_names, shared, finals):
        w, m, v = big[n]
        s = lax.dynamic_update_slice(s, fin[None], (my_core, 0, 0))
        g = s.reshape(w.shape[1], s.shape[2])[:, :w.shape[2]]
        d, nm, nv = _adamw(w[0], g, m[0], v[0], f"adamw_{n}")
        grads[n], deltas[n], new_m[n], new_v[n] = g[None], d[None], nm[None], nv[None]

    small_names = list(small_w)
    pad_rows = SMALL_ROWS - len(small_names) - 1
    stack = lambda d: jnp.concatenate([_pad_row(d[n]) for n in small_names] + [jnp.zeros((pad_rows + 1, D_MODEL), F32)], axis=0)
    g_pack = jnp.concatenate([_pad_row(small_g[n]) for n in small_names] + [_pad_row(loss_part[0:1, 0:1])]
                             + [jnp.zeros((pad_rows, D_MODEL), F32)], axis=0)
    g_tot = _all_reduce_small(g_pack)
    d_s, m_s, v_s = _adamw(stack(small_w), g_tot, stack(m_small), stack(v_small), "adamw_small")
    for i, n in enumerate(small_names):
        shape = small_w[n].shape
        size = int(np.prod(shape))
        grads[n] = g_tot[i, :size].reshape(shape)
        deltas[n], new_m[n], new_v[n] = d_s[i, :size].reshape(shape), m_s[i, :size].reshape(shape), v_s[i, :size].reshape(shape)
    loss = g_tot[len(small_names), 0]

    order = ["g_mix", "w_in", "b_forget", "g_ret_out", "g_fox_q", "g_fox_k", "w_out", "g_xattn", "w_xq", "w_xkv", "g_mem", "g_xq",
             "g_xk", "w_xo", "g_ffn", "w_gate", "w_up", "w_down"]
    return (loss, grad_x[None], *[grads[n] for n in order], *[deltas[n] for n in order], *[new_m[n] for n in order],
            *[new_v[n] for n in order])
```

```python
import functools

import numpy as np
import jax
import jax.numpy as jnp
from jax import lax
from jax.experimental import pallas as pl
from jax.experimental.pallas import tpu as pltpu

F32 = jnp.float32
BF = jnp.bfloat16

D_MODEL = 1024
HEAD_DIM = 64
N_HEADS = 8
GROUP_W = 512
N_XH = 4
XHD = 256
D_FF = 2816
MAIN_W = 3584
IN_W = 3592
ROPE_BASE = 10000.0
LOG2E = 1.4426950408889634
LN2 = 0.6931471805599453
EPS = 1e-6
NEG = -1e30
LANES = 128
RET_BLOCK = 256
REF_CHUNK = 64
ROW_TILE = 256
ATT_BLOCK = 256
TN_MAX_ROWS = 1408
SMALL_ROWS = 16
COL_SHARDED = ("w_in", "w_xkv")
TRANSPOSED = ("w_gate", "w_up")
LATE = ("w_out", "w_xq", "w_xkv", "w_xo", "w_gate", "w_up", "w_down")
VMEM_LIMIT = 56 * 1024 * 1024

ADAM_LR = 0.001
ADAM_B1 = 0.9
ADAM_B2 = 0.999
ADAM_EPS = 1e-08
ADAM_WD = 0.01
ADAM_STEP = 10

MESH = pl.DeviceIdType.MESH
ANY = pl.BlockSpec(memory_space=pl.ANY)
VMEM_SPEC = pl.BlockSpec(memory_space=pltpu.VMEM)


def _cparams(sem=None, vmem=VMEM_LIMIT):
    return pltpu.CompilerParams(dimension_semantics=sem, vmem_limit_bytes=vmem)


def _dot(a, b):
    return jnp.dot(a.astype(BF), b.astype(BF), preferred_element_type=F32)


def _dot_nt(a, b):
    return lax.dot_general(a.astype(BF), b.astype(BF), (((1,), (1,)), ((), ())), preferred_element_type=F32)


def _dot_tn(a, b):
    return lax.dot_general(a.astype(BF), b.astype(BF), (((0,), (0,)), ((), ())), preferred_element_type=F32)


def _split3(x):
    hi = x.astype(BF)
    r = x - hi.astype(F32)
    mid = r.astype(BF)
    lo = (r - mid.astype(F32)).astype(BF)
    return hi, mid, lo


def _dot_exact(ind, x):
    hi, mid, lo = _split3(x)
    return (jnp.dot(ind, lo, preferred_element_type=F32) + jnp.dot(ind, mid, preferred_element_type=F32)
            + jnp.dot(ind, hi, preferred_element_type=F32))


def _dot_nt_exact(ind, x):
    hi, mid, lo = _split3(x)
    dn = (((1,), (1,)), ((), ()))
    return (lax.dot_general(ind, lo, dn, preferred_element_type=F32) + lax.dot_general(ind, mid, dn, preferred_element_type=F32)
            + lax.dot_general(ind, hi, dn, preferred_element_type=F32))


def _sigmoid(x):
    return 1.0 / (1.0 + jnp.exp(-x))


def _rms_fwd(x, g):
    r = lax.rsqrt(jnp.mean(x * x, axis=-1, keepdims=True) + EPS)
    return x * r * g


def _rms_bwd(x, g, dy):
    r = lax.rsqrt(jnp.mean(x * x, axis=-1, keepdims=True) + EPS)
    xh = x * r
    dg = jnp.sum(dy * xh, axis=0, keepdims=True)
    dxh = dy * g
    dx = r * (dxh - xh * jnp.mean(dxh * xh, axis=-1, keepdims=True))
    return dx, dg


def _group_mean64(x):
    lane = lax.broadcasted_iota(jnp.int32, x.shape, 1)
    lo = lane < HEAD_DIM
    s_lo = jnp.sum(jnp.where(lo, x, 0.0), axis=-1, keepdims=True)
    s_hi = jnp.sum(jnp.where(lo, 0.0, x), axis=-1, keepdims=True)
    return jnp.where(lo, s_lo, s_hi) * (1.0 / HEAD_DIM)


def _swap32(x):
    lane = lax.broadcasted_iota(jnp.int32, x.shape, 1)
    first = (lane % HEAD_DIM) < (HEAD_DIM // 2)
    return jnp.where(first, pltpu.roll(x, LANES - HEAD_DIM // 2, axis=1), pltpu.roll(x, HEAD_DIM // 2, axis=1))


def _chunks(w):
    return [slice(j * LANES, (j + 1) * LANES) for j in range(w // LANES)]


def _aug_pair(qk, f_cols, is_query):
    lane = lax.broadcasted_iota(jnp.int32, (qk.shape[0], HEAD_DIM), 1)
    out = []
    for hh in range(2):
        hi, mid, lo = (p.astype(F32) for p in _split3(f_cols[hh] * LOG2E))
        if is_query:
            aux = jnp.where(lane == 0, hi, jnp.where(lane == 1, mid, jnp.where(lane == 2, lo, jnp.where(lane < 6, 1.0, 0.0))))
        else:
            aux = jnp.where(lane < 3, 1.0, jnp.where(lane == 3, -hi, jnp.where(lane == 4, -mid, jnp.where(lane == 5, -lo, 0.0))))
        out += [qk[:, hh * HEAD_DIM:(hh + 1) * HEAD_DIM], aux]
    return jnp.concatenate(out, axis=-1).astype(BF)


def _mem_kv_fwd(mem, g_mem, w_xkv, g_xk):
    m_tok = mem.shape[0]

    def body(mem_ref, gm_ref, w_ref, gk_ref, memn_ref, kraw_ref, kn_ref, v_ref):
        mn = _rms_fwd(mem_ref[...], gm_ref[...]).astype(BF)
        memn_ref[...] = mn
        kv = jnp.dot(mn, w_ref[...], preferred_element_type=F32)
        k = kv[:, :D_MODEL]
        kraw_ref[...] = k
        v_ref[...] = kv[:, D_MODEL:].astype(BF)
        for h in range(N_XH):
            sl = slice(h * XHD, (h + 1) * XHD)
            kn_ref[:, sl] = _rms_fwd(k[:, sl], gk_ref[...]).astype(BF)

    return pl.pallas_call(
        body, name="mem_kv_fwd",
        out_shape=(jax.ShapeDtypeStruct((m_tok, D_MODEL), BF), jax.ShapeDtypeStruct((m_tok, D_MODEL), F32),
                   jax.ShapeDtypeStruct((m_tok, D_MODEL), BF), jax.ShapeDtypeStruct((m_tok, D_MODEL), BF)),
        in_specs=[VMEM_SPEC] * 4, out_specs=(VMEM_SPEC,) * 4, compiler_params=_cparams(),
    )(mem, g_mem, w_xkv, g_xk)


def _in_proj_fwd(x, g_mix, w_main, w_ff, b_pad, cos_t, sin_t, gq_t, gk_t):
    t_len = x.shape[0]
    tm = min(ROW_TILE, t_len)
    n_t = t_len // tm

    def body(x_ref, g_ref, wm_ref, wf_ref, b_ref, cos_ref, sin_ref, gq_ref, gk_ref,
             n1_ref, proj_ref, rq_ref, rk_ref, qa_ref, ka_ref, z_ref, carry):
        i = pl.program_id(0)

        @pl.when(i == 0)
        def _():
            carry[...] = jnp.zeros_like(carry)

        n1 = _rms_fwd(x_ref[...], g_ref[...]).astype(BF)
        n1_ref[...] = n1
        proj = jnp.dot(n1, wm_ref[...], preferred_element_type=F32)
        proj_ref[...] = proj.astype(BF)
        z = jnp.dot(n1, wf_ref[...], preferred_element_type=F32) + b_ref[...]
        z_ref[...] = z
        lane = lax.broadcasted_iota(jnp.int32, z.shape, 1)
        lf = jnp.where(lane < N_HEADS, jnp.minimum(z, 0.0) - jnp.log(1.0 + jnp.exp(-jnp.abs(z))), 0.0)
        row = lax.broadcasted_iota(jnp.int32, (tm, tm), 0)
        col = lax.broadcasted_iota(jnp.int32, (tm, tm), 1)
        tri = (row >= col).astype(BF)
        fc = _dot_exact(tri, lf) + carry[0:1, :]
        carry[...] = jnp.broadcast_to(fc[tm - 1:tm, :], carry.shape)
        c, s = cos_ref[...], sin_ref[...]
        for j, sl in enumerate(_chunks(GROUP_W)):
            q = proj[:, sl]
            rq_ref[:, sl] = ((q * c + _swap32(q) * s) * 0.125).astype(BF)
            k = proj[:, GROUP_W + j * LANES:GROUP_W + (j + 1) * LANES]
            rk_ref[:, sl] = (k * c + _swap32(k) * s).astype(BF)
            f_cols = [fc[:, 2 * j:2 * j + 1], fc[:, 2 * j + 1:2 * j + 2]]
            fq = proj[:, 4 * GROUP_W + j * LANES:4 * GROUP_W + (j + 1) * LANES]
            fq = fq * lax.rsqrt(_group_mean64(fq * fq) + EPS) * gq_ref[...] * (0.125 * LOG2E)
            qa_ref[:, 2 * j * LANES:2 * (j + 1) * LANES] = _aug_pair(fq, f_cols, True)
            fk = proj[:, 5 * GROUP_W + j * LANES:5 * GROUP_W + (j + 1) * LANES]
            fk = fk * lax.rsqrt(_group_mean64(fk * fk) + EPS) * gk_ref[...]
            ka_ref[:, 2 * j * LANES:2 * (j + 1) * LANES] = _aug_pair(fk, f_cols, False)

    row_spec = lambda w: pl.BlockSpec((tm, w), lambda i: (i, 0))
    full = lambda a: pl.BlockSpec(a.shape, lambda i: (0,) * a.ndim)
    return pl.pallas_call(
        body, name="in_proj_fwd", grid=(n_t,),
        out_shape=(jax.ShapeDtypeStruct((t_len, D_MODEL), BF), jax.ShapeDtypeStruct((t_len, MAIN_W), BF),
                   jax.ShapeDtypeStruct((t_len, GROUP_W), BF), jax.ShapeDtypeStruct((t_len, GROUP_W), BF),
                   jax.ShapeDtypeStruct((t_len, 2 * GROUP_W), BF), jax.ShapeDtypeStruct((t_len, 2 * GROUP_W), BF),
                   jax.ShapeDtypeStruct((t_len, LANES), F32)),
        in_specs=[row_spec(D_MODEL), full(g_mix), full(w_main), full(w_ff), full(b_pad), row_spec(LANES), row_spec(LANES),
                  full(gq_t), full(gk_t)],
        out_specs=(row_spec(D_MODEL), row_spec(MAIN_W), row_spec(GROUP_W), row_spec(GROUP_W), row_spec(2 * GROUP_W),
                   row_spec(2 * GROUP_W), row_spec(LANES)),
        scratch_shapes=[pltpu.VMEM((8, LANES), F32)],
        compiler_params=_cparams(("arbitrary",)),
    )(x, g_mix, w_main, w_ff, b_pad, cos_t, sin_t, gq_t, gk_t)


def _decay_tables(c):
    h = np.arange(N_HEADS, dtype=np.float64)
    lg = np.log(1.0 - 2.0 ** (-5.0 - h)).astype(np.float32).astype(np.float64)
    t = np.arange(c)
    same_or_earlier = (t[None, :] // REF_CHUNK) <= (t[:, None] // REF_CHUNK)
    w = np.where(same_or_earlier[None], np.exp(lg[:, None, None] * np.abs(t[:, None] - t[None, :])[None]), 0.0)
    qd = np.exp(lg[:, None] * (t[None, :] + 1.0))
    kd = np.exp(lg[:, None] * (c - 1.0 - t[None, :]))
    cd = np.exp(lg * c)
    ones = np.ones((1, 1, HEAD_DIM))
    return (jnp.asarray(w, F32), jnp.asarray(qd[:, :, None] * ones, F32), jnp.asarray(kd[:, :, None] * ones, F32),
            jnp.asarray(cd[:, None, None] * np.ones((1, HEAD_DIM, HEAD_DIM)), F32))


def _retention_fwd(rq, rk, proj, g_ret, tables):
    t_len = rq.shape[0]
    c = min(RET_BLOCK, t_len)
    n_b = t_len // c
    wdec, qdec, kdec, cdec = tables
    v_col, g_col = 2 * GROUP_W // LANES, 3 * GROUP_W // LANES

    def body(q_ref, k_ref, v_ref, rg_ref, g_ref, w_ref, qd_ref, kd_ref, cd_ref, raw_ref, mix_ref, st_ref, state):
        i = pl.program_id(1)

        @pl.when(i == 0)
        def _():
            state[...] = jnp.zeros_like(state)

        q2, k2, v2 = q_ref[...], k_ref[...], v_ref[...]
        outs = []
        for hh in range(2):
            sl = slice(hh * HEAD_DIM, (hh + 1) * HEAD_DIM)
            q, k, v = q2[:, sl], k2[:, sl], v2[:, sl]
            sp = state[hh]
            st_ref[0, 0, hh] = sp
            a = _dot_nt(q, k) * w_ref[hh]
            o = _dot(a, v) + _dot(q.astype(F32) * qd_ref[hh], sp)
            state[hh] = sp * cd_ref[hh] + _dot_tn(k.astype(F32) * kd_ref[hh], v)
            outs.append(o)
        o2 = jnp.concatenate(outs, axis=-1)
        raw_ref[...] = o2
        xc = o2 - _group_mean64(o2)
        xh = xc * lax.rsqrt(_group_mean64(xc * xc) + EPS)
        gate = rg_ref[...].astype(F32)
        mix_ref[...] = (gate * _sigmoid(gate) * (xh * g_ref[0])).astype(BF)

    blk = lambda col0: pl.BlockSpec((c, LANES), lambda hp, i: (i, col0 + hp))
    tab = lambda a: pl.BlockSpec((2,) + a.shape[1:], lambda hp, i: (hp, 0, 0))
    return pl.pallas_call(
        body, name="retention_fwd", grid=(N_HEADS // 2, n_b),
        out_shape=(jax.ShapeDtypeStruct((t_len, GROUP_W), F32), jax.ShapeDtypeStruct((t_len, GROUP_W), BF),
                   jax.ShapeDtypeStruct((N_HEADS // 2, n_b, 2, HEAD_DIM, HEAD_DIM), F32)),
        in_specs=[blk(0), blk(0), blk(v_col), blk(g_col), pl.BlockSpec((1, 1, LANES), lambda hp, i: (hp, 0, 0)),
                  tab(wdec), tab(qdec), tab(kdec), tab(cdec)],
        out_specs=(blk(0), blk(0), pl.BlockSpec((1, 1, 2, HEAD_DIM, HEAD_DIM), lambda hp, i: (hp, i, 0, 0, 0))),
        scratch_shapes=[pltpu.VMEM((2, HEAD_DIM, HEAD_DIM), F32)],
        compiler_params=_cparams(("arbitrary", "arbitrary")),
    )(rq, rk, proj, proj, g_ret, wdec, qdec, kdec, cdec)


def _fox_fwd(q_aug, k_aug, proj, shards):
    t_len = q_aug.shape[0]
    tq = min(ATT_BLOCK, t_len)
    n_q = t_len // tq
    v_col = 6 * GROUP_W // LANES
    tc = min(512, t_len)
    n_w = len(shards)
    n_steps = (N_HEADS // 2) * n_q

    def body(*refs):
        q_ref, k_ref, v_ref = refs[:3]
        o_ref, o32_ref, lse_ref = refs[3 + n_w:6 + n_w]
        vt = refs[6 + 2 * n_w]
        comm = (refs[3:3 + n_w], refs[6 + n_w:6 + 2 * n_w]) + tuple(refs[7 + 2 * n_w:])
        i = pl.program_id(1)
        step = pl.program_id(0) * n_q + i

        @pl.when(step == 0)
        def _():
            _gather_phase(0, *comm)

        @pl.when(step == (3 * n_steps) // 4)
        def _():
            _gather_phase(1, *comm)

        @pl.when(i == 0)
        def _():
            for c0 in range(0, t_len, tc):
                vt[:, c0:c0 + tc] = v_ref[c0:c0 + tc, :].T

        qs = [q_ref[:, hh * LANES:(hh + 1) * LANES] for hh in range(2)]
        ones = jnp.ones((HEAD_DIM, tq), BF)

        def block(j, carry, masked):
            rows = pl.ds(pl.multiple_of(j * tq, tq), tq)
            k2 = k_ref[rows, :]
            v2 = vt[:, rows]
            ss = [_dot_nt(k2[:, hh * LANES:(hh + 1) * LANES], qs[hh]) for hh in range(2)]
            ps, stats = [], []
            for hh in range(2):
                m = carry[hh][0]
                s_t = ss[hh]
                if masked:
                    krow = lax.broadcasted_iota(jnp.int32, (tq, tq), 0)
                    qcol = lax.broadcasted_iota(jnp.int32, (tq, tq), 1)
                    s_t = jnp.where(qcol >= krow, s_t, NEG)
                m_new = jnp.maximum(m, jnp.max(s_t, axis=0, keepdims=True))
                ps.append(jnp.exp2(s_t - m_new).astype(BF))
                stats.append((m_new, jnp.exp2(m - m_new)))
            out = []
            for hh in range(2):
                m_new, alpha = stats[hh]
                v_aug = jnp.concatenate([v2[hh * HEAD_DIM:(hh + 1) * HEAD_DIM, :], ones], axis=0)
                out.append((m_new, carry[hh][1] * alpha + jnp.dot(v_aug, ps[hh], preferred_element_type=F32)))
            return tuple(out)

        init = tuple((jnp.full((1, tq), NEG, F32), jnp.zeros((LANES, tq), F32)) for _ in range(2))
        carry = lax.fori_loop(0, i, lambda j, c: block(j, c, False), init)
        carry = block(i, carry, True)
        outs, lses = [], []
        for hh in range(2):
            m, acc = carry[hh]
            l = acc[HEAD_DIM:HEAD_DIM + 1, :]
            outs.append(acc[:HEAD_DIM, :] / l)
            lses.append(m + jnp.log2(l))
        o2 = jnp.concatenate(outs, axis=0).T
        o32_ref[...] = o2
        o_ref[...] = o2.astype(BF)
        lse_ref[0] = jnp.concatenate(lses, axis=0)

        @pl.when(step == n_steps - 1)
        def _():
            _gather_phase(2, *comm)

    return pl.pallas_call(
        body, name="fox_fwd", grid=(N_HEADS // 2, n_q),
        out_shape=(jax.ShapeDtypeStruct((t_len, GROUP_W), BF), jax.ShapeDtypeStruct((t_len, GROUP_W), F32),
                   jax.ShapeDtypeStruct((N_HEADS // 2, 2, t_len), F32))
        + tuple(jax.ShapeDtypeStruct((4,) + s.shape, s.dtype) for s in shards),
        in_specs=[pl.BlockSpec((tq, 2 * LANES), lambda hp, i: (i, hp)),
                  pl.BlockSpec((t_len, 2 * LANES), lambda hp, i: (0, hp)),
                  pl.BlockSpec((t_len, LANES), lambda hp, i: (0, v_col + hp))] + [ANY] * n_w,
        out_specs=(pl.BlockSpec((tq, LANES), lambda hp, i: (i, hp)), pl.BlockSpec((tq, LANES), lambda hp, i: (i, hp)),
                   pl.BlockSpec((1, 2, tq), lambda hp, i: (hp, 0, i))) + (ANY,) * n_w,
        scratch_shapes=[pltpu.VMEM((LANES, t_len), BF)] + _gather_scratch(n_w),
        compiler_params=_cparams(("arbitrary", "arbitrary")),
    )(q_aug, k_aug, proj, *shards)


def _softmax_rows(s):
    p = jnp.exp(s - jnp.max(s, axis=-1, keepdims=True))
    return p / jnp.sum(p, axis=-1, keepdims=True)


def _attn_out_xattn_fwd(x, mix_r, mix_f, w_out, g_xattn, w_xq, g_xq, kn, v, w_xo):
    t_len = x.shape[0]
    tm = min(ROW_TILE, t_len)

    def body(x_ref, mr_ref, mf_ref, wo_ref, g_ref, wq_ref, gq_ref, kn_ref, v_ref, wxo_ref,
             h1_ref, hn_ref, qx_ref, o_ref, h2_ref):
        h1 = x_ref[...] + jnp.dot(mr_ref[...], wo_ref[:GROUP_W, :], preferred_element_type=F32) \
            + jnp.dot(mf_ref[...], wo_ref[GROUP_W:, :], preferred_element_type=F32)
        h1_ref[...] = h1
        hn = _rms_fwd(h1, g_ref[...]).astype(BF)
        hn_ref[...] = hn
        qx = jnp.dot(hn, wq_ref[...], preferred_element_type=F32).astype(BF)
        qx_ref[...] = qx
        for h in range(N_XH):
            sl = slice(h * XHD, (h + 1) * XHD)
            qn = _rms_fwd(qx[:, sl].astype(F32), gq_ref[...])
            p = _softmax_rows(_dot_nt(qn, kn_ref[:, sl]) * (XHD ** -0.5))
            o_ref[:, sl] = _dot(p, v_ref[:, sl]).astype(BF)
        h2_ref[...] = h1 + jnp.dot(o_ref[...], wxo_ref[...], preferred_element_type=F32)

    row_spec = lambda w: pl.BlockSpec((tm, w), lambda i: (i, 0))
    full = lambda a: pl.BlockSpec(a.shape, lambda i: (0,) * a.ndim)
    return pl.pallas_call(
        body, name="attn_out_xattn_fwd", grid=(t_len // tm,),
        out_shape=(jax.ShapeDtypeStruct((t_len, D_MODEL), F32), jax.ShapeDtypeStruct((t_len, D_MODEL), BF),
                   jax.ShapeDtypeStruct((t_len, D_MODEL), BF), jax.ShapeDtypeStruct((t_len, D_MODEL), BF),
                   jax.ShapeDtypeStruct((t_len, D_MODEL), F32)),
        in_specs=[row_spec(D_MODEL), row_spec(GROUP_W), row_spec(GROUP_W), full(w_out), full(g_xattn), full(w_xq), full(g_xq),
                  full(kn), full(v), full(w_xo)],
        out_specs=(row_spec(D_MODEL),) * 5,
        compiler_params=_cparams(("arbitrary",)),
    )(x, mix_r, mix_f, w_out, g_xattn, w_xq, g_xq, kn, v, w_xo)


def _ffn_loss_fwd(h2, g_ffn, w_gate, w_up, w_down, target):
    t_len = h2.shape[0]
    tm = min(ROW_TILE, t_len)

    def body(h2_ref, g_ref, wg_ref, wu_ref, wd_ref, tgt_ref, hn_ref, gate_ref, up_ref, act_ref, dh3_ref, loss_ref):
        @pl.when(pl.program_id(0) == 0)
        def _():
            loss_ref[...] = jnp.zeros_like(loss_ref)

        h2v = h2_ref[...]
        hn = _rms_fwd(h2v, g_ref[...]).astype(BF)
        hn_ref[...] = hn
        gate = _dot_nt(hn, wg_ref[...])
        up = _dot_nt(hn, wu_ref[...])
        gate_ref[...] = gate.astype(BF)
        up_ref[...] = up.astype(BF)
        act = (gate * _sigmoid(gate) * up).astype(BF)
        act_ref[...] = act
        diff = h2v + jnp.dot(act, wd_ref[...], preferred_element_type=F32) - tgt_ref[...]
        dh3_ref[...] = diff * (1.0 / D_MODEL)
        per_row = jnp.sum(diff * diff, axis=-1, keepdims=True) * (1.0 / D_MODEL)
        loss_ref[...] += 0.5 * jnp.sum(per_row, axis=0, keepdims=True)

    row_spec = lambda w: pl.BlockSpec((tm, w), lambda i: (i, 0))
    full = lambda a: pl.BlockSpec(a.shape, lambda i: (0,) * a.ndim, pipeline_mode=pl.Buffered(1))
    return pl.pallas_call(
        body, name="ffn_loss_fwd", grid=(t_len // tm,),
        out_shape=(jax.ShapeDtypeStruct((t_len, D_MODEL), BF), jax.ShapeDtypeStruct((t_len, D_FF), BF),
                   jax.ShapeDtypeStruct((t_len, D_FF), BF), jax.ShapeDtypeStruct((t_len, D_FF), BF),
                   jax.ShapeDtypeStruct((t_len, D_MODEL), F32), jax.ShapeDtypeStruct((8, LANES), F32)),
        in_specs=[row_spec(D_MODEL), full(g_ffn), full(w_gate), full(w_up), full(w_down), row_spec(D_MODEL)],
        out_specs=(row_spec(D_MODEL), row_spec(D_FF), row_spec(D_FF), row_spec(D_FF), row_spec(D_MODEL),
                   pl.BlockSpec((8, LANES), lambda i: (0, 0))),
        compiler_params=_cparams(("arbitrary",)),
    )(h2, g_ffn, w_gate, w_up, w_down, target)


def _ffn_bwd(dh3, gate, up, h2, g_ffn, w_gate, w_up, w_down):
    t_len = h2.shape[0]
    tm = min(ROW_TILE, t_len)

    def body(dh3_ref, gate_ref, up_ref, h2_ref, g_ref, wg_ref, wu_ref, wd_ref, dgate_ref, dup_ref, dh2_ref, dg_ref):
        @pl.when(pl.program_id(0) == 0)
        def _():
            dg_ref[...] = jnp.zeros_like(dg_ref)

        dh3v = dh3_ref[...]
        dact = _dot_nt(dh3v, wd_ref[...])
        g = gate_ref[...].astype(F32)
        sg = _sigmoid(g)
        dup = (dact * (g * sg)).astype(BF)
        dgate = (dact * up_ref[...].astype(F32) * (sg * (1.0 + g * (1.0 - sg)))).astype(BF)
        dup_ref[...] = dup
        dgate_ref[...] = dgate
        dhn = jnp.dot(dgate, wg_ref[...], preferred_element_type=F32) + jnp.dot(dup, wu_ref[...], preferred_element_type=F32)
        dx, dg = _rms_bwd(h2_ref[...], g_ref[...], dhn)
        dh2_ref[...] = dh3v + dx
        dg_ref[...] += dg

    row_spec = lambda w: pl.BlockSpec((tm, w), lambda i: (i, 0))
    full = lambda a: pl.BlockSpec(a.shape, lambda i: (0,) * a.ndim, pipeline_mode=pl.Buffered(1))
    return pl.pallas_call(
        body, name="ffn_bwd", grid=(t_len // tm,),
        out_shape=(jax.ShapeDtypeStruct((t_len, D_FF), BF), jax.ShapeDtypeStruct((t_len, D_FF), BF),
                   jax.ShapeDtypeStruct((t_len, D_MODEL), F32), jax.ShapeDtypeStruct((1, D_MODEL), F32)),
        in_specs=[row_spec(D_MODEL), row_spec(D_FF), row_spec(D_FF), row_spec(D_MODEL), full(g_ffn), full(w_gate), full(w_up),
                  full(w_down)],
        out_specs=(row_spec(D_FF), row_spec(D_FF), row_spec(D_MODEL), pl.BlockSpec((1, D_MODEL), lambda i: (0, 0))),
        compiler_params=_cparams(("arbitrary",)),
    )(dh3, gate, up, h2, g_ffn, w_gate, w_up, w_down)


def _attn_out_xattn_bwd(dh2, h1, qx, kn, v, w_xo, w_xq, w_out, g_xattn, g_xq):
    t_len = h1.shape[0]
    tm = min(ROW_TILE, t_len)
    m_tok = kn.shape[0]

    def body(dh2_ref, h1_ref, qx_ref, kn_ref, v_ref, wxo_ref, wq_ref, wo_ref, g_ref, gq_ref,
             dqx_ref, dh1_ref, dmr_ref, dmf_ref, dkn_ref, dv_ref, dg_ref, dgq_ref, dqx_scr):
        @pl.when(pl.program_id(0) == 0)
        def _():
            dkn_ref[...] = jnp.zeros_like(dkn_ref)
            dv_ref[...] = jnp.zeros_like(dv_ref)
            dg_ref[...] = jnp.zeros_like(dg_ref)
            dgq_ref[...] = jnp.zeros_like(dgq_ref)

        dh2v = dh2_ref[...]
        do = _dot_nt(dh2v, wxo_ref[...])
        gq = gq_ref[...]
        dgq = jnp.zeros((1, XHD), F32)
        for h in range(N_XH):
            sl = slice(h * XHD, (h + 1) * XHD)
            qraw = qx_ref[:, sl].astype(F32)
            qn = _rms_fwd(qraw, gq)
            p = _softmax_rows(_dot_nt(qn, kn_ref[:, sl]) * (XHD ** -0.5))
            doh = do[:, sl]
            dv_ref[:, sl] += _dot_tn(p, doh)
            dp = _dot_nt(doh, v_ref[:, sl])
            ds = p * (dp - jnp.sum(dp * p, axis=-1, keepdims=True)) * (XHD ** -0.5)
            dqn = _dot(ds, kn_ref[:, sl])
            dkn_ref[:, sl] += _dot_tn(ds, qn)
            dx, dg_h = _rms_bwd(qraw, gq, dqn)
            dgq = dgq + dg_h
            dqx_scr[:, sl] = dx.astype(BF)
        dgq_ref[...] += dgq
        dqx = dqx_scr[...]
        dqx_ref[...] = dqx
        dhn = _dot_nt(dqx, wq_ref[...])
        dx, dg = _rms_bwd(h1_ref[...], g_ref[...], dhn)
        dg_ref[...] += dg
        dh1 = dh2v + dx
        dh1_ref[...] = dh1
        dmix = _dot_nt(dh1, wo_ref[...])
        dmr_ref[...] = dmix[:, :GROUP_W]
        dmf_ref[...] = dmix[:, GROUP_W:].astype(BF)

    row_spec = lambda w: pl.BlockSpec((tm, w), lambda i: (i, 0))
    full = lambda a: pl.BlockSpec(a.shape, lambda i: (0,) * a.ndim)
    acc = lambda r, c: pl.BlockSpec((r, c), lambda i: (0, 0))
    return pl.pallas_call(
        body, name="attn_out_xattn_bwd", grid=(t_len // tm,),
        out_shape=(jax.ShapeDtypeStruct((t_len, D_MODEL), BF), jax.ShapeDtypeStruct((t_len, D_MODEL), F32),
                   jax.ShapeDtypeStruct((t_len, GROUP_W), F32), jax.ShapeDtypeStruct((t_len, GROUP_W), BF),
                   jax.ShapeDtypeStruct((m_tok, D_MODEL), F32), jax.ShapeDtypeStruct((m_tok, D_MODEL), F32),
                   jax.ShapeDtypeStruct((1, D_MODEL), F32), jax.ShapeDtypeStruct((1, XHD), F32)),
        in_specs=[row_spec(D_MODEL), row_spec(D_MODEL), row_spec(D_MODEL), full(kn), full(v), full(w_xo), full(w_xq), full(w_out),
                  full(g_xattn), full(g_xq)],
        out_specs=(row_spec(D_MODEL), row_spec(D_MODEL), row_spec(GROUP_W), row_spec(GROUP_W), acc(m_tok, D_MODEL),
                   acc(m_tok, D_MODEL), acc(1, D_MODEL), acc(1, XHD)),
        scratch_shapes=[pltpu.VMEM((tm, D_MODEL), BF)],
        compiler_params=_cparams(("arbitrary",)),
    )(dh2, h1, qx, kn, v, w_xo, w_xq, w_out, g_xattn, g_xq)


def _mem_kv_bwd(dkn, dv, kraw, mem, memn, g_mem, g_xk, w_xkv):
    m_tok = mem.shape[0]

    def body(dkn_ref, dv_ref, kraw_ref, mem_ref, memn_ref, gm_ref, gk_ref, w_ref, dw_ref, dgm_ref, dgk_ref, dkv_scr):
        gk = gk_ref[...]
        dgk = jnp.zeros((1, XHD), F32)
        for h in range(N_XH):
            sl = slice(h * XHD, (h + 1) * XHD)
            dx, dg_h = _rms_bwd(kraw_ref[:, sl], gk, dkn_ref[:, sl])
            dgk = dgk + dg_h
            dkv_scr[:, sl] = dx.astype(BF)
        dgk_ref[...] = dgk
        dkv_scr[:, D_MODEL:] = dv_ref[...].astype(BF)
        dkv = dkv_scr[...]
        dw_ref[...] = _dot_tn(memn_ref[...], dkv)
        dmemn = _dot_nt(dkv, w_ref[...])
        mem_v = mem_ref[...]
        r = lax.rsqrt(jnp.mean(mem_v * mem_v, axis=-1, keepdims=True) + EPS)
        dgm_ref[...] = jnp.sum(dmemn * mem_v * r, axis=0, keepdims=True)

    return pl.pallas_call(
        body, name="mem_kv_bwd",
        out_shape=(jax.ShapeDtypeStruct((D_MODEL, 2 * D_MODEL), F32), jax.ShapeDtypeStruct((1, D_MODEL), F32),
                   jax.ShapeDtypeStruct((1, XHD), F32)),
        in_specs=[VMEM_SPEC] * 8, out_specs=(VMEM_SPEC,) * 3,
        scratch_shapes=[pltpu.VMEM((m_tok, 2 * D_MODEL), BF)],
        compiler_params=_cparams(),
    )(dkn, dv, kraw, mem, memn, g_mem, g_xk, w_xkv)


def _fox_bwd(q_aug, k_aug, proj, dmf, o32, lse, sums):
    t_len = q_aug.shape[0]
    tb = min(ATT_BLOCK, t_len)
    n_b = t_len // tb
    v_col = 6 * GROUP_W // LANES
    n_w = len(sums)
    n_steps = (N_HEADS // 2) * n_b

    def body(*refs):
        k_ref, v_ref, q_ref, do_ref, o_ref, lse_ref = refs[:6]
        dq_ref, dk_ref, dv_ref, df_ref = refs[6 + n_w:10 + n_w]
        delta = refs[10 + 2 * n_w]
        comm = (refs[6:6 + n_w], refs[10 + n_w:10 + 2 * n_w]) + tuple(refs[11 + 2 * n_w:])
        j = pl.program_id(1)
        step = pl.program_id(0) * n_b + j

        @pl.when(step == 0)
        def _():
            _scatter_phase(0, *comm)

        @pl.when(j == 0)
        def _():
            dq_ref[...] = jnp.zeros_like(dq_ref)
            dd = do_ref[...].astype(F32) * o_ref[...]
            hrow = lax.broadcasted_iota(jnp.int32, (8, LANES), 0)
            lane = lax.broadcasted_iota(jnp.int32, (8, LANES), 1)
            ind = ((lane // HEAD_DIM) == hrow).astype(BF)
            delta[...] = _dot_nt_exact(ind, dd)

        k2, v2 = k_ref[...], v_ref[...]
        ks = [k2[:, hh * LANES:(hh + 1) * LANES] for hh in range(2)]
        vs = [v2[:, hh * HEAD_DIM:(hh + 1) * HEAD_DIM] for hh in range(2)]

        def block(i, carry, masked):
            rows = pl.ds(pl.multiple_of(i * tb, tb), tb)
            q2 = q_ref[rows, :]
            do2 = do_ref[rows, :]
            qs = [q2[:, hh * LANES:(hh + 1) * LANES] for hh in range(2)]
            dos = [do2[:, hh * HEAD_DIM:(hh + 1) * HEAD_DIM] for hh in range(2)]
            ss = [_dot_nt(ks[hh], qs[hh]) for hh in range(2)]
            dps = [_dot_nt(vs[hh], dos[hh]) for hh in range(2)]
            pts, dsts, dfs = [], [], []
            for hh in range(2):
                s_t = ss[hh]
                if masked:
                    krow = lax.broadcasted_iota(jnp.int32, (tb, tb), 0)
                    qcol = lax.broadcasted_iota(jnp.int32, (tb, tb), 1)
                    s_t = jnp.where(qcol >= krow, s_t, NEG)
                p_t = jnp.exp2(s_t - lse_ref[0, hh:hh + 1, rows])
                pts.append(p_t.astype(BF))
                ds_t = p_t * (dps[hh] - delta[hh:hh + 1, rows])
                dsts.append(ds_t.astype(BF))
                dfs.append(jnp.sum(ds_t, axis=-1, keepdims=True))
            out = []
            for hh in range(2):
                dk, dv, df = carry[hh]
                dv = dv + jnp.dot(pts[hh], dos[hh], preferred_element_type=F32)
                dk = dk + jnp.dot(dsts[hh], qs[hh], preferred_element_type=F32)
                dq_ref[rows, hh * HEAD_DIM:(hh + 1) * HEAD_DIM] += _dot_tn(dsts[hh], ks[hh])[:, :HEAD_DIM]
                out.append((dk, dv, df - dfs[hh]))
            return tuple(out)

        init = tuple((jnp.zeros((tb, LANES), F32), jnp.zeros((tb, HEAD_DIM), F32), jnp.zeros((tb, 1), F32)) for _ in range(2))
        carry = block(j, init, True)
        carry = lax.fori_loop(j + 1, n_b, lambda i, c: block(i, c, False), carry)
        dk_ref[...] = jnp.concatenate([carry[hh][0][:, :HEAD_DIM] for hh in range(2)], axis=-1) * LN2
        dv_ref[...] = jnp.concatenate([carry[hh][1] for hh in range(2)], axis=-1)
        df_ref[0] = jnp.concatenate([carry[hh][2] for hh in range(2)], axis=-1)

        @pl.when(step == n_steps - 1)
        def _():
            _scatter_phase(1, *comm)

    blk = lambda w, col0: pl.BlockSpec((tb, w), lambda hp, j: (j, col0 + hp))
    whole = lambda w: pl.BlockSpec((t_len, w), lambda hp, j: (0, hp))
    rows2 = pl.BlockSpec((1, 2, t_len), lambda hp, j: (hp, 0, 0))
    cols2 = pl.BlockSpec((1, tb, 2), lambda hp, j: (hp, j, 0))
    return pl.pallas_call(
        body, name="fox_bwd", grid=(N_HEADS // 2, n_b),
        out_shape=(jax.ShapeDtypeStruct((t_len, GROUP_W), F32), jax.ShapeDtypeStruct((t_len, GROUP_W), F32),
                   jax.ShapeDtypeStruct((t_len, GROUP_W), F32), jax.ShapeDtypeStruct((N_HEADS // 2, t_len, 2), F32))
        + _scatter_out_shapes(sums),
        in_specs=[blk(2 * LANES, 0), blk(LANES, v_col), whole(2 * LANES), whole(LANES), whole(LANES), rows2] + [ANY] * n_w,
        out_specs=(whole(LANES), blk(LANES, 0), blk(LANES, 0), cols2) + (ANY,) * n_w,
        scratch_shapes=[pltpu.VMEM((8, t_len), F32)] + _scatter_scratch(n_w),
        compiler_params=_cparams(("arbitrary", "arbitrary")),
    )(k_aug, proj, q_aug, dmf, o32, lse, *sums)


def _retention_bwd(dmr, raw, proj, g_ret, rq, rk, states, tables):
    t_len = rq.shape[0]
    c = min(RET_BLOCK, t_len)
    n_b = t_len // c
    wdec, qdec, kdec, cdec = tables
    v_col, g_col = 2 * GROUP_W // LANES, 3 * GROUP_W // LANES

    def body(d_ref, raw_ref, rg_ref, g_ref, q_ref, k_ref, v_ref, st_ref, w_ref, qd_ref, kd_ref, cd_ref,
             dq_ref, dk_ref, dv_ref, drg_ref, dg_ref, gstate):
        @pl.when(pl.program_id(1) == 0)
        def _():
            gstate[...] = jnp.zeros_like(gstate)
            dg_ref[...] = jnp.zeros_like(dg_ref)

        d, raw_v, g = d_ref[...], raw_ref[...], g_ref[0]
        gate = rg_ref[...].astype(F32)
        xc = raw_v - _group_mean64(raw_v)
        r = lax.rsqrt(_group_mean64(xc * xc) + EPS)
        xh = xc * r
        sg = _sigmoid(gate)
        drg_ref[...] = d * (xh * g) * (sg * (1.0 + gate * (1.0 - sg)))
        dy = d * (gate * sg)
        dg_ref[0] += jnp.sum(dy * xh, axis=0, keepdims=True)
        dxh = dy * g
        do2 = r * (dxh - _group_mean64(dxh) - xh * _group_mean64(dxh * xh))
        q2, k2, v2 = q_ref[...], k_ref[...], v_ref[...]
        dqs, dks, dvs = [], [], []
        for hh in range(2):
            sl = slice(hh * HEAD_DIM, (hh + 1) * HEAD_DIM)
            q, k, v, do = q2[:, sl], k2[:, sl], v2[:, sl], do2[:, sl].astype(BF)
            w = w_ref[hh]
            a = _dot_nt(q, k) * w
            dm = _dot_nt(do, v) * w
            sp, gs = st_ref[0, 0, hh], gstate[hh]
            qd = q.astype(F32) * qd_ref[hh]
            kd = k.astype(F32) * kd_ref[hh]
            dqs.append(_dot(dm, k) + _dot_nt(do, sp) * qd_ref[hh])
            dks.append(_dot_tn(dm, q) + _dot_nt(v, gs) * kd_ref[hh])
            dvs.append(_dot_tn(a, do) + _dot(kd, gs))
            gstate[hh] = gs * cd_ref[hh] + _dot_tn(qd, do)
        dq_ref[...] = jnp.concatenate(dqs, axis=-1)
        dk_ref[...] = jnp.concatenate(dks, axis=-1)
        dv_ref[...] = jnp.concatenate(dvs, axis=-1)

    blk = lambda col0: pl.BlockSpec((c, LANES), lambda hp, i: (n_b - 1 - i, col0 + hp))
    tab = lambda a: pl.BlockSpec((2,) + a.shape[1:], lambda hp, i: (hp, 0, 0))
    gspec = pl.BlockSpec((1, 1, LANES), lambda hp, i: (hp, 0, 0))
    return pl.pallas_call(
        body, name="retention_bwd", grid=(N_HEADS // 2, n_b),
        out_shape=(jax.ShapeDtypeStruct((t_len, GROUP_W), F32),) * 4 + (jax.ShapeDtypeStruct((N_HEADS // 2, 1, LANES), F32),),
        in_specs=[blk(0), blk(0), blk(g_col), gspec, blk(0), blk(0), blk(v_col),
                  pl.BlockSpec((1, 1, 2, HEAD_DIM, HEAD_DIM), lambda hp, i: (hp, n_b - 1 - i, 0, 0, 0)),
                  tab(wdec), tab(qdec), tab(kdec), tab(cdec)],
        out_specs=(blk(0), blk(0), blk(0), blk(0), gspec),
        scratch_shapes=[pltpu.VMEM((2, HEAD_DIM, HEAD_DIM), F32)],
        compiler_params=_cparams(("arbitrary", "arbitrary")),
    )(dmr, raw, proj, g_ret, rq, rk, proj, states, wdec, qdec, kdec, cdec)


def _in_proj_bwd(x, g_mix, dh1, dq_r, dk_r, dv_r, drg, dq_f, dk_f, dv_f, df_col, proj, z, cos_t, sin_t, gq_t, gk_t, w_main, w_ff):
    t_len = x.shape[0]
    tm = min(ROW_TILE, t_len)
    n_t = t_len // tm

    def body(x_ref, g_ref, dh1_ref, dqr_ref, dkr_ref, dvr_ref, drg_ref, dqf_ref, dkf_ref, dvf_ref, df_ref, fq_ref, fk_ref, z_ref,
             cos_ref, sin_ref, gq_ref, gk_ref, wm_ref, wf_ref,
             dproj_ref, dz_ref, dx_ref, dg_ref, dgq_ref, dgk_ref, db_ref, carry, gq_acc, gk_acc):
        i = pl.program_id(0)

        @pl.when(i == 0)
        def _():
            carry[...] = jnp.zeros_like(carry)
            gq_acc[...] = jnp.zeros_like(gq_acc)
            gk_acc[...] = jnp.zeros_like(gk_acc)
            dg_ref[...] = jnp.zeros_like(dg_ref)
            db_ref[...] = jnp.zeros_like(db_ref)

        c, s = cos_ref[...], sin_ref[...]
        gq, gk = gq_ref[...], gk_ref[...]
        dgq = jnp.zeros((1, LANES), F32)
        dgk = jnp.zeros((1, LANES), F32)
        for sl in _chunks(GROUP_W):
            dy = dqr_ref[:, sl] * 0.125
            dproj_ref[:, sl] = (dy * c + _swap32(dy * s)).astype(BF)
            dy = dkr_ref[:, sl]
            dproj_ref[:, GROUP_W + sl.start:GROUP_W + sl.stop] = (dy * c + _swap32(dy * s)).astype(BF)
            dproj_ref[:, 2 * GROUP_W + sl.start:2 * GROUP_W + sl.stop] = dvr_ref[:, sl].astype(BF)
            dproj_ref[:, 3 * GROUP_W + sl.start:3 * GROUP_W + sl.stop] = drg_ref[:, sl].astype(BF)
            for src, dsrc, gain, off in ((fq_ref, dqf_ref, gq, 4), (fk_ref, dkf_ref, gk, 5)):
                xr = src[:, sl].astype(F32)
                r = lax.rsqrt(_group_mean64(xr * xr) + EPS)
                xh = xr * r
                dy = dsrc[:, sl] * (0.125 if off == 4 else 1.0)
                dgs = jnp.sum(dy * xh, axis=0, keepdims=True)
                if off == 4:
                    dgq = dgq + dgs
                else:
                    dgk = dgk + dgs
                dxh = dy * gain
                dproj_ref[:, off * GROUP_W + sl.start:off * GROUP_W + sl.stop] = \
                    (r * (dxh - xh * _group_mean64(dxh * xh))).astype(BF)
            dproj_ref[:, 6 * GROUP_W + sl.start:6 * GROUP_W + sl.stop] = dvf_ref[:, sl].astype(BF)
        gq_acc[...] += dgq
        gk_acc[...] += dgk
        row = lax.broadcasted_iota(jnp.int32, (tm, tm), 0)
        col = lax.broadcasted_iota(jnp.int32, (tm, tm), 1)
        dlf = _dot_exact((col >= row).astype(BF), df_ref[...]) + carry[0:1, :]
        carry[...] = jnp.broadcast_to(dlf[0:1, :], carry.shape)
        lane = lax.broadcasted_iota(jnp.int32, (tm, LANES), 1)
        dz = jnp.where(lane < N_HEADS, dlf / (1.0 + jnp.exp(z_ref[...])), 0.0)
        db_ref[...] += jnp.sum(dz, axis=0, keepdims=True)
        dz_bf = dz.astype(BF)
        dz_ref[...] = dz_bf
        dn1 = _dot_nt(dz_bf, wf_ref[...])
        for sec in range(MAIN_W // GROUP_W):
            sl = slice(sec * GROUP_W, (sec + 1) * GROUP_W)
            dn1 = dn1 + _dot_nt(dproj_ref[:, sl], wm_ref[:, sl])
        dx, dg = _rms_bwd(x_ref[...], g_ref[...], dn1)
        dx_ref[...] = dh1_ref[...] + dx
        dg_ref[...] += dg

        @pl.when(i == n_t - 1)
        def _():
            dgq_ref[...] = gq_acc[:, :HEAD_DIM] + gq_acc[:, HEAD_DIM:]
            dgk_ref[...] = gk_acc[:, :HEAD_DIM] + gk_acc[:, HEAD_DIM:]

    row_spec = lambda w, col=0: pl.BlockSpec((tm, w), lambda i: (n_t - 1 - i, col))
    full = lambda a: pl.BlockSpec(a.shape, lambda i: (0,) * a.ndim)
    acc = lambda r, c: pl.BlockSpec((r, c), lambda i: (0, 0))
    return pl.pallas_call(
        body, name="in_proj_bwd", grid=(n_t,),
        out_shape=(jax.ShapeDtypeStruct((t_len, MAIN_W), BF), jax.ShapeDtypeStruct((t_len, LANES), BF),
                   jax.ShapeDtypeStruct((t_len, D_MODEL), F32), jax.ShapeDtypeStruct((1, D_MODEL), F32),
                   jax.ShapeDtypeStruct((1, HEAD_DIM), F32), jax.ShapeDtypeStruct((1, HEAD_DIM), F32),
                   jax.ShapeDtypeStruct((1, LANES), F32)),
        in_specs=[row_spec(D_MODEL), full(g_mix), row_spec(D_MODEL)] + [row_spec(GROUP_W)] * 7
        + [row_spec(LANES), row_spec(GROUP_W, 4), row_spec(GROUP_W, 5), row_spec(LANES), row_spec(LANES), row_spec(LANES),
           full(gq_t), full(gk_t), full(w_main), full(w_ff)],
        out_specs=(row_spec(MAIN_W), row_spec(LANES), row_spec(D_MODEL), acc(1, D_MODEL), acc(1, HEAD_DIM), acc(1, HEAD_DIM),
                   acc(1, LANES)),
        scratch_shapes=[pltpu.VMEM((8, LANES), F32), pltpu.VMEM((1, LANES), F32), pltpu.VMEM((1, LANES), F32)],
        compiler_params=_cparams(("arbitrary",)),
    )(x, g_mix, dh1, dq_r, dk_r, dv_r, drg, dq_f, dk_f, dv_f, df_col, proj, proj, z, cos_t, sin_t, gq_t, gk_t, w_main, w_ff)


def _matmul_tn(a, b, name, bk=512):
    t_len, m = a.shape
    n = b.shape[1]
    bm = m if m <= TN_MAX_ROWS else m // 2
    bk = min(bk, t_len)

    def body(a_ref, b_ref, o_ref):
        @pl.when(pl.program_id(1) == 0)
        def _():
            o_ref[...] = jnp.zeros_like(o_ref)

        o_ref[...] += _dot_tn(a_ref[...], b_ref[...])

    return pl.pallas_call(
        body, name=name, grid=(m // bm, t_len // bk),
        out_shape=jax.ShapeDtypeStruct((m, n), F32),
        in_specs=[pl.BlockSpec((bk, bm), lambda i, k: (k, i)), pl.BlockSpec((bk, n), lambda i, k: (k, 0))],
        out_specs=pl.BlockSpec((bm, n), lambda i, k: (i, 0)),
        compiler_params=_cparams(("arbitrary", "arbitrary")),
    )(a, b)


def _place():
    x, y, c = lax.axis_index("x"), lax.axis_index("y"), lax.axis_index("c")
    chips = [(1 - x, y), (x, 1 - y), (1 - x, 1 - y)]
    return x, y, c, chips


def _row_chunks(rows, limit):
    step = max(d for d in range(16, min(rows, limit) + 1, 16) if rows % d == 0)
    return [slice(i, i + step) for i in range(0, rows, step)]


ICI_CHUNK_ROWS = 128
D2D_CHUNK_ROWS = 64


def _gather_phase(phase, ins, outs, send_sems, recv_sems):
    x, y, c, chips = _place()
    me_chip = 2 * x + y
    sibling = (x, y, 1 - c)

    def copy(w, k, slot, half, to, rows=slice(None), src=None):
        dst = outs[w].at[slot, half, rows]
        return pltpu.make_async_remote_copy(src_ref=dst if src is None else src, dst_ref=dst,
                                            send_sem=send_sems.at[w, k], recv_sem=recv_sems.at[w, k],
                                            device_id=to, device_id_type=MESH)

    for w in range(len(ins)):
        for j, (px, py) in enumerate(chips):
            if phase == 0:
                for rows in _row_chunks(ins[w].shape[1], ICI_CHUNK_ROWS):
                    copy(w, j, me_chip, c, (px, py, c), rows, src=ins[w].at[c, rows]).start()
            elif phase == 1:
                copy(w, j, 2 * px + py, c, (x, y, c)).wait_recv()
                for rows in _row_chunks(ins[w].shape[1], D2D_CHUNK_ROWS):
                    copy(w, 3 + j, 2 * px + py, c, sibling, rows).start()
            else:
                copy(w, 3 + j, 2 * px + py, 1 - c, (x, y, c)).wait_recv()
                copy(w, j, me_chip, c, (px, py, c), src=ins[w].at[c]).wait_send()
                copy(w, 3 + j, 2 * px + py, c, sibling).wait_send()


def _gather_scratch(n_w):
    return [pltpu.SemaphoreType.DMA((n_w, 6)), pltpu.SemaphoreType.DMA((n_w, 6))]


def _all_gather_weights(shards):
    n_w = len(shards)

    def body(*refs):
        for phase in range(3):
            _gather_phase(phase, refs[:n_w], refs[n_w:2 * n_w], *refs[2 * n_w:])

    return pl.pallas_call(
        body, name="all_gather_weights",
        out_shape=tuple(jax.ShapeDtypeStruct((4,) + s.shape, s.dtype) for s in shards),
        in_specs=[ANY] * n_w, out_specs=(ANY,) * n_w, scratch_shapes=_gather_scratch(n_w),
    )(*shards)


def _exchange_core_halves(grads):
    n_w = len(grads)

    def body(*refs):
        ins, theirs = refs[:n_w], refs[n_w:2 * n_w]
        send_sems, recv_sems = refs[2 * n_w:]
        x, y, c, _ = _place()

        def remote(w, k=slice(None), rows=slice(None)):
            return pltpu.make_async_remote_copy(src_ref=ins[w].at[k, 1 - c, rows], dst_ref=theirs[w].at[k, rows],
                                                send_sem=send_sems.at[w], recv_sem=recv_sems.at[w], device_id=(x, y, 1 - c),
                                                device_id_type=MESH)

        for w in range(n_w):
            for k in range(4):
                for rows in _row_chunks(ins[w].shape[2], D2D_CHUNK_ROWS):
                    remote(w, k, rows).start()
        for w in range(n_w):
            remote(w).wait()

    half = tuple(jax.ShapeDtypeStruct((4,) + g.shape[2:], g.dtype) for g in grads)
    return pl.pallas_call(
        body, name="exchange_core_halves", out_shape=half,
        in_specs=[ANY] * n_w, out_specs=(ANY,) * n_w,
        scratch_shapes=[pltpu.SemaphoreType.DMA((n_w,)), pltpu.SemaphoreType.DMA((n_w,))],
    )(*grads)


def _add_pairs(a, b, name):
    _, r, c = a.shape
    rb = 32 if r % 32 == 0 else r

    def body(a_ref, b_ref, o_ref, ob_ref):
        s = a_ref[...] + b_ref[...]
        o_ref[...] = s
        ob_ref[...] = s.astype(BF)

    spec = pl.BlockSpec((4, rb, c), lambda i: (0, i, 0))
    return pl.pallas_call(
        body, name=name, grid=(r // rb,),
        out_shape=(jax.ShapeDtypeStruct(a.shape, F32), jax.ShapeDtypeStruct(a.shape, BF)),
        in_specs=[spec, spec], out_specs=(spec, spec), compiler_params=_cparams(("arbitrary",)),
    )(a, b)


def _scatter_phase(phase, bfs, got, send_sems, recv_sems):
    x, y, c, chips = _place()

    def remote(w, j, px, py, rows=slice(None)):
        return pltpu.make_async_remote_copy(src_ref=bfs[w].at[2 * px + py, rows], dst_ref=got[w].at[j, rows],
                                            send_sem=send_sems.at[w, j], recv_sem=recv_sems.at[w, j], device_id=(px, py, c),
                                            device_id_type=MESH)

    for w in range(len(bfs)):
        for j, (px, py) in enumerate(chips):
            if phase == 0:
                for rows in _row_chunks(bfs[w].shape[1], ICI_CHUNK_ROWS):
                    remote(w, j, px, py, rows).start()
            else:
                remote(w, j, px, py).wait()


def _scatter_scratch(n_w):
    return [pltpu.SemaphoreType.DMA((n_w, 3)), pltpu.SemaphoreType.DMA((n_w, 3))]


def _scatter_out_shapes(sums_bf16):
    return tuple(jax.ShapeDtypeStruct((3,) + s.shape[1:], BF) for s in sums_bf16)


def _scatter_to_chips(sums_bf16):
    n_w = len(sums_bf16)

    def body(*refs):
        for phase in range(2):
            _scatter_phase(phase, refs[:n_w], refs[n_w:2 * n_w], *refs[2 * n_w:])

    return pl.pallas_call(
        body, name="scatter_to_chips", out_shape=_scatter_out_shapes(sums_bf16),
        in_specs=[ANY] * n_w, out_specs=(ANY,) * n_w, scratch_shapes=_scatter_scratch(n_w),
    )(*sums_bf16)


def _add_received(own, got, name):
    r, c = own.shape
    rb = 32 if r % 32 == 0 else r

    def body(o_ref, g_ref, out_ref):
        out_ref[...] = ((o_ref[...] + g_ref[0].astype(F32)) + g_ref[1].astype(F32)) + g_ref[2].astype(F32)

    return pl.pallas_call(
        body, name=name, grid=(r // rb,), out_shape=jax.ShapeDtypeStruct((r, c), F32),
        in_specs=[pl.BlockSpec((rb, c), lambda i: (i, 0)), pl.BlockSpec((3, rb, c), lambda i: (0, i, 0))],
        out_specs=pl.BlockSpec((rb, c), lambda i: (i, 0)), compiler_params=_cparams(("arbitrary",)),
    )(own, got)


def _share_with_sibling(halves):
    n_w = len(halves)

    def body(*refs):
        ins, outs = refs[:n_w], refs[n_w:2 * n_w]
        send_sems, recv_sems = refs[2 * n_w:]
        x, y, c, _ = _place()

        def remote(w, rows=slice(None)):
            return pltpu.make_async_remote_copy(src_ref=ins[w].at[rows], dst_ref=outs[w].at[c, rows], send_sem=send_sems.at[w],
                                                recv_sem=recv_sems.at[w], device_id=(x, y, 1 - c), device_id_type=MESH)

        for w in range(n_w):
            for rows in _row_chunks(ins[w].shape[0], D2D_CHUNK_ROWS):
                remote(w, rows).start()
        for w in range(n_w):
            remote(w).wait()

    return pl.pallas_call(
        body, name="share_with_sibling",
        out_shape=tuple(jax.ShapeDtypeStruct((2,) + h.shape, h.dtype) for h in halves),
        in_specs=[ANY] * n_w, out_specs=(ANY,) * n_w,
        scratch_shapes=[pltpu.SemaphoreType.DMA((n_w,)), pltpu.SemaphoreType.DMA((n_w,))],
    )(*halves)


def _all_reduce_small(pack):
    r, c = pack.shape

    def body(p_ref, out_ref, slots, send_sems, recv_sems):
        x, y, cc, _ = _place()
        me = 4 * x + 2 * y + cc
        slots[me] = p_ref[...]
        copies = []
        for k in range(1, 8):
            dx, dy, dc = (k >> 2) & 1, (k >> 1) & 1, k & 1
            to = (1 - x if dx else x, 1 - y if dy else y, 1 - cc if dc else cc)
            cp = pltpu.make_async_remote_copy(src_ref=p_ref, dst_ref=slots.at[me], send_sem=send_sems.at[k - 1],
                                              recv_sem=recv_sems.at[k - 1], device_id=to, device_id_type=MESH)
            cp.start()
            copies.append(cp)
        for cp in copies:
            cp.wait()
        total = slots[0]
        for d in range(1, 8):
            total = total + slots[d]
        out_ref[...] = total

    return pl.pallas_call(
        body, name="all_reduce_small", out_shape=jax.ShapeDtypeStruct((r, c), F32),
        in_specs=[VMEM_SPEC], out_specs=VMEM_SPEC,
        scratch_shapes=[pltpu.VMEM((8, r, c), F32), pltpu.SemaphoreType.DMA((7,)), pltpu.SemaphoreType.DMA((7,))],
    )(pack)


def _adamw(w, g, m, v, name):
    r, c = w.shape
    rb = 64 if r % 64 == 0 else r
    c1 = 1.0 - ADAM_B1 ** ADAM_STEP
    c2 = 1.0 - ADAM_B2 ** ADAM_STEP

    def body(w_ref, g_ref, m_ref, v_ref, d_ref, nm_ref, nv_ref):
        gv = g_ref[...]
        nm = ADAM_B1 * m_ref[...] + (1.0 - ADAM_B1) * gv
        nv = ADAM_B2 * v_ref[...] + (1.0 - ADAM_B2) * (gv * gv)
        nm_ref[...] = nm
        nv_ref[...] = nv
        d_ref[...] = -ADAM_LR * ((nm / c1) / (jnp.sqrt(nv / c2) + ADAM_EPS) + ADAM_WD * w_ref[...])

    spec = pl.BlockSpec((rb, c), lambda i: (i, 0))
    return pl.pallas_call(
        body, name=name, grid=(r // rb,), out_shape=(jax.ShapeDtypeStruct((r, c), F32),) * 3,
        in_specs=[spec] * 4, out_specs=(spec,) * 3, compiler_params=_cparams(("arbitrary",)),
    )(w, g, m, v)


def _rope_tables(t_len):
    inv_freq = ROPE_BASE ** (-jnp.arange(0, HEAD_DIM, 2, dtype=F32) / HEAD_DIM)
    ang = jnp.arange(t_len, dtype=F32)[:, None] * inv_freq[None, :]
    cos, sin = jnp.cos(ang), jnp.sin(ang)
    cos_t = jnp.concatenate([cos, cos, cos, cos], axis=-1)
    sin_t = jnp.concatenate([-sin, sin, -sin, sin], axis=-1)
    return cos_t, sin_t


def _cols_to_shards(dw):
    r, n = dw.shape
    return jnp.transpose(dw.reshape(2, r // 2, 4, n // 4), (2, 0, 1, 3))


def _rows_to_shards(dw):
    r, n = dw.shape
    return dw.reshape(4, 2, r // 8, n)


def _pad_lanes(a):
    extra = -a.shape[-1] % LANES
    return a if extra == 0 else jnp.pad(a, [(0, 0)] * (a.ndim - 1) + [(0, extra)])


def _pad_row(a, width=D_MODEL):
    a = a.reshape(1, -1)
    return jnp.pad(a, ((0, 0), (0, width - a.shape[1])))


def kernel(x, mem, g_mix, w_in, b_forget, g_ret_out, g_fox_q, g_fox_k, w_out, g_xattn, w_xq, w_xkv, g_mem, g_xq, g_xk, w_xo, g_ffn, w_gate, w_up, w_down, loss_target, m_g_mix, m_w_in, m_b_forget, m_g_ret_out, m_g_fox_q, m_g_fox_k, m_w_out, m_g_xattn, m_w_xq, m_w_xkv, m_g_mem, m_g_xq, m_g_xk, m_w_xo, m_g_ffn, m_w_gate, m_w_up, m_w_down, v_g_mix, v_w_in, v_b_forget, v_g_ret_out, v_g_fox_q, v_g_fox_k, v_w_out, v_g_xattn, v_w_xq, v_w_xkv, v_g_mem, v_g_xq, v_g_xk, v_w_xo, v_g_ffn, v_w_gate, v_w_up, v_w_down):
    big = {"w_in": (w_in, m_w_in, v_w_in), "w_out": (w_out, m_w_out, v_w_out), "w_xq": (w_xq, m_w_xq, v_w_xq),
           "w_xkv": (w_xkv, m_w_xkv, v_w_xkv), "w_xo": (w_xo, m_w_xo, v_w_xo), "w_gate": (w_gate, m_w_gate, v_w_gate),
           "w_up": (w_up, m_w_up, v_w_up), "w_down": (w_down, m_w_down, v_w_down)}
    for n in TRANSPOSED:
        big[n] = tuple(jnp.swapaxes(a, 1, 2) for a in big[n])
    shards = {}
    for n in big:
        w = _pad_lanes(big[n][0][0].astype(BF))
        shards[n] = w.reshape(2, w.shape[0] // 2, w.shape[1])
    widths = {n: big[n][0].shape[2] for n in big}
    w_in_full = _assemble_weight("w_in", _all_gather_weights([shards["w_in"]])[0], shards["w_in"], widths["w_in"])
    small_w ={"g_mix": g_mix, "b_forget": b_forget, "g_ret_out": g_ret_out, "g_fox_q": g_fox_q, "g_fox_k": g_fox_k,
               "g_xattn": g_xattn, "g_mem": g_mem, "g_xq": g_xq, "g_xk": g_xk, "g_ffn": g_ffn}
    m_small = {"g_mix": m_g_mix, "b_forget": m_b_forget, "g_ret_out": m_g_ret_out, "g_fox_q": m_g_fox_q, "g_fox_k": m_g_fox_k,
               "g_xattn": m_g_xattn, "g_mem": m_g_mem, "g_xq": m_g_xq, "g_xk": m_g_xk, "g_ffn": m_g_ffn}
    v_small = {"g_mix": v_g_mix, "b_forget": v_b_forget, "g_ret_out": v_g_ret_out, "g_fox_q": v_g_fox_q, "g_fox_k": v_g_fox_k,
               "g_xattn": v_g_xattn, "g_mem": v_g_mem, "g_xq": v_g_xq, "g_xk": v_g_xk, "g_ffn": v_g_ffn}
    loss_part, grad_x, sums, got, small_g = _local_step(x[0], mem[0], loss_target[0], w_in_full, shards, widths, small_w)
    return _reduce_and_update(big, sums, got, small_w, small_g, loss_part, grad_x, m_small, v_small)


def _assemble_weight(name, gathered, own, width):
    my_chip = 2 * lax.axis_index("x") + lax.axis_index("y")
    g = lax.dynamic_update_slice(gathered, own[None], (my_chip, 0, 0, 0))
    rows = 2 * g.shape[2]
    g = g.reshape(4, rows, g.shape[3])[:, :, :width]
    return jnp.transpose(g, (1, 0, 2)).reshape(rows, 4 * width) if name in COL_SHARDED else g.reshape(4 * rows, width)


def _core_sums(names, dw):
    parts = [_pad_lanes(_cols_to_shards(dw[n]) if n in COL_SHARDED else _rows_to_shards(dw[n])) for n in names]
    my_core = lax.axis_index("c")
    theirs = _exchange_core_halves(parts)
    mine = [lax.dynamic_index_in_dim(p, my_core, axis=1, keepdims=False) for p in parts]
    return [_add_pairs(a, b, f"core_sum_{n}") for n, a, b in zip(names, mine, theirs)]


def _local_step(xs, mems, tgt, w_in_full, shards, widths, small_w):
    g_mix, b_forget, g_ret_out, g_fox_q, g_fox_k = (small_w[n] for n in ("g_mix", "b_forget", "g_ret_out", "g_fox_q", "g_fox_k"))
    g_xattn, g_mem, g_xq, g_xk, g_ffn = (small_w[n] for n in ("g_xattn", "g_mem", "g_xq", "g_xk", "g_ffn"))
    w_main = w_in_full[:, :MAIN_W]
    w_ff = jnp.pad(w_in_full[:, MAIN_W:], ((0, 0), (0, LANES - (IN_W - MAIN_W))))
    t_len = xs.shape[0]
    cos_t, sin_t = _rope_tables(t_len)
    tables = _decay_tables(min(RET_BLOCK, t_len))
    gq_t = jnp.concatenate([g_fox_q, g_fox_q], axis=-1)
    gk_t = jnp.concatenate([g_fox_k, g_fox_k], axis=-1)
    b_pad = _pad_row(b_forget, LANES)
    g_ret = g_ret_out.reshape(N_HEADS // 2, 1, LANES)

    n1, proj, rq, rk, q_aug, k_aug, z = _in_proj_fwd(xs, g_mix, w_main, w_ff, b_pad, cos_t, sin_t, gq_t, gk_t)
    raw, mix_r, states = _retention_fwd(rq, rk, proj, g_ret, tables)
    mix_f, o32, lse, *gathered = _fox_fwd(q_aug, k_aug, proj, [shards[n] for n in LATE])
    full = {n: _assemble_weight(n, g, shards[n], widths[n]) for n, g in zip(LATE, gathered)}
    memn, kraw, kn, vmem = _mem_kv_fwd(mems, g_mem, full["w_xkv"], g_xk)
    h1, hn2, qx, o_x, h2 = _attn_out_xattn_fwd(xs, mix_r, mix_f, full["w_out"], g_xattn, full["w_xq"], g_xq, kn, vmem, full["w_xo"])
    hn3, gate, up, act, dh3, loss_part = _ffn_loss_fwd(h2, g_ffn, full["w_gate"], full["w_up"], full["w_down"], tgt)

    dgate, dup, dh2, dg_ffn = _ffn_bwd(dh3, gate, up, h2, g_ffn, full["w_gate"], full["w_up"], full["w_down"])
    dqx, dh1, dmr, dmf, dkn, dvm, dg_xattn, dg_xq = _attn_out_xattn_bwd(dh2, h1, qx, kn, vmem, full["w_xo"], full["w_xq"],
                                                                      full["w_out"], g_xattn, g_xq)
    dw_xkv, dg_mem, dg_xk = _mem_kv_bwd(dkn, dvm, kraw, mems, memn, g_mem, g_xk, full["w_xkv"])
    dw = {
        "w_out": jnp.concatenate([_matmul_tn(mix_r, dh1, "dw_out_ret"), _matmul_tn(mix_f, dh1, "dw_out_fox")], axis=0),
        "w_xq": _matmul_tn(hn2, dqx, "dw_xq"),
        "w_xkv": dw_xkv,
        "w_xo": _matmul_tn(o_x, dh2, "dw_xo"),
        "w_gate": _matmul_tn(dgate, hn3, "dw_gate"),
        "w_up": _matmul_tn(dup, hn3, "dw_up"),
        "w_down": _matmul_tn(act, dh3, "dw_down"),
    }
    late_sums = _core_sums(LATE, dw)
    dq_f, dk_f, dv_f, df, *late_got = _fox_bwd(q_aug, k_aug, proj, dmf, o32, lse, [s[1] for s in late_sums])
    dq_r, dk_r, dv_r, drg, dg_ret = _retention_bwd(dmr, raw, proj, g_ret, rq, rk, states, tables)
    df_col = jnp.pad(jnp.transpose(df, (1, 0, 2)).reshape(t_len, N_HEADS), ((0, 0), (0, LANES - N_HEADS)))
    dproj, dz, grad_x, dg_mix, dg_fq, dg_fk, db = _in_proj_bwd(xs, g_mix, dh1, dq_r, dk_r, dv_r, drg, dq_f, dk_f, dv_f, df_col,
                                                              proj, z, cos_t, sin_t, gq_t, gk_t, w_main, w_ff)

    dw_in = jnp.concatenate([_matmul_tn(n1, dproj, "dw_in_main"), _matmul_tn(n1, dz, "dw_in_ff")[:, :IN_W - MAIN_W]], axis=1)
    in_sums = _core_sums(("w_in",), {"w_in": dw_in})
    in_got = _scatter_to_chips([in_sums[0][1]])
    sums = {n: s[0] for n, s in zip(("w_in",) + LATE, in_sums + late_sums)}
    got = dict(zip(("w_in",) + LATE, list(in_got) + late_got))
    small_g = {"g_mix": dg_mix, "b_forget": db[:, :N_HEADS], "g_ret_out": dg_ret, "g_fox_q": dg_fq, "g_fox_k": dg_fk,
               "g_xattn": dg_xattn, "g_mem": dg_mem, "g_xq": dg_xq, "g_xk": dg_xk, "g_ffn": dg_ffn}
    return loss_part, grad_x, sums, got, small_g


def _reduce_and_update(big, sums, got, small_w, small_g, loss_part, grad_x, m_small, v_small):
    big_names = list(big)
    my_core = lax.axis_index("c")
    my_chip = 2 * lax.axis_index("x") + lax.axis_index("y")
    own = [lax.dynamic_index_in_dim(sums[n], my_chip, axis=0, keepdims=False) for n in big_names]
    finals = [_add_received(o, got[n], f"chip_sum_{n}") for n, o in zip(big_names, own)]
    shared = _share_with_sibling(finals)
    grads, deltas, new_m, new_v = {}, {}, {}, {}
    for n, s, fin in zip(big_names, shared, finals):
        w, m, v = big[n]
        s = lax.dynamic_update_slice(s, fin[None], (my_core, 0, 0))
        g = s.reshape(w.shape[1], s.shape[2])[:, :w.shape[2]]
        d, nm, nv = _adamw(w[0], g, m[0], v[0], f"adamw_{n}")
        grads[n], deltas[n], new_m[n], new_v[n] = ((jnp.swapaxes(a[None], 1, 2) if n in TRANSPOSED else a[None]) for a in (g, d, nm, nv))

    small_names = list(small_w)
    pad_rows = SMALL_ROWS - len(small_names) - 1
    stack = lambda d: jnp.concatenate([_pad_row(d[n]) for n in small_names] + [jnp.zeros((pad_rows + 1, D_MODEL), F32)], axis=0)
    g_pack = jnp.concatenate([_pad_row(small_g[n]) for n in small_names] + [_pad_row(loss_part[0:1, 0:1])]
                             + [jnp.zeros((pad_rows, D_MODEL), F32)], axis=0)
    g_tot = _all_reduce_small(g_pack)
    d_s, m_s, v_s = _adamw(stack(small_w), g_tot, stack(m_small), stack(v_small), "adamw_small")
    for i, n in enumerate(small_names):
        shape = small_w[n].shape
        size = int(np.prod(shape))
        grads[n] = g_tot[i, :size].reshape(shape)
        deltas[n], new_m[n], new_v[n] = d_s[i, :size].reshape(shape), m_s[i, :size].reshape(shape), v_s[i, :size].reshape(shape)
    loss = g_tot[len(small_names), 0]

    order = ["g_mix", "w_in", "b_forget", "g_ret_out", "g_fox_q", "g_fox_k", "w_out", "g_xattn", "w_xq", "w_xkv", "g_mem", "g_xq",
             "g_xk", "w_xo", "g_ffn", "w_gate", "w_up", "w_down"]
    return (loss, grad_x[None], *[grads[n] for n in order], *[deltas[n] for n in order], *[new_m[n] for n in order],
            *[new_v[n] for n in order])
```

```python
import functools

import numpy as np
import jax
import jax.numpy as jnp
from jax import lax
from jax.experimental import pallas as pl
from jax.experimental.pallas import tpu as pltpu

F32 = jnp.float32
BF = jnp.bfloat16

D_MODEL = 1024
HEAD_DIM = 64
N_HEADS = 8
GROUP_W = 512
N_XH = 4
XHD = 256
D_FF = 2816
MAIN_W = 3584
IN_W = 3592
ROPE_BASE = 10000.0
LOG2E = 1.4426950408889634
LN2 = 0.6931471805599453
EPS = 1e-6
NEG = -1e30
LANES = 128
RET_BLOCK = 256
REF_CHUNK = 64
ROW_TILE = 256
ATT_BLOCK = 256
TN_MAX_ROWS = 1408
SMALL_ROWS = 16
COL_SHARDED = ("w_xkv",)
TRANSPOSED = ("w_in", "w_gate", "w_up")
SHARD_ROW_ALIGN = 32
SHARD_ROW_PAD = 256
LATE = ("w_out", "w_xq", "w_xkv", "w_xo", "w_gate", "w_up", "w_down")
VMEM_LIMIT = 56 * 1024 * 1024

ADAM_LR = 0.001
ADAM_B1 = 0.9
ADAM_B2 = 0.999
ADAM_EPS = 1e-08
ADAM_WD = 0.01
ADAM_STEP = 10

MESH = pl.DeviceIdType.MESH
ANY = pl.BlockSpec(memory_space=pl.ANY)
VMEM_SPEC = pl.BlockSpec(memory_space=pltpu.VMEM)


def _cparams(sem=None, vmem=VMEM_LIMIT):
    return pltpu.CompilerParams(dimension_semantics=sem, vmem_limit_bytes=vmem)


def _dot(a, b):
    return jnp.dot(a.astype(BF), b.astype(BF), preferred_element_type=F32)


def _dot_nt(a, b):
    return lax.dot_general(a.astype(BF), b.astype(BF), (((1,), (1,)), ((), ())), preferred_element_type=F32)


def _dot_tn(a, b):
    return lax.dot_general(a.astype(BF), b.astype(BF), (((0,), (0,)), ((), ())), preferred_element_type=F32)


def _split3(x):
    hi = x.astype(BF)
    r = x - hi.astype(F32)
    mid = r.astype(BF)
    lo = (r - mid.astype(F32)).astype(BF)
    return hi, mid, lo


def _dot_exact(ind, x):
    hi, mid, lo = _split3(x)
    return (jnp.dot(ind, lo, preferred_element_type=F32) + jnp.dot(ind, mid, preferred_element_type=F32)
            + jnp.dot(ind, hi, preferred_element_type=F32))


def _dot_nt_exact(ind, x):
    hi, mid, lo = _split3(x)
    dn = (((1,), (1,)), ((), ()))
    return (lax.dot_general(ind, lo, dn, preferred_element_type=F32) + lax.dot_general(ind, mid, dn, preferred_element_type=F32)
            + lax.dot_general(ind, hi, dn, preferred_element_type=F32))


def _sigmoid(x):
    return 1.0 / (1.0 + jnp.exp(-x))


def _rms_fwd(x, g):
    r = lax.rsqrt(jnp.mean(x * x, axis=-1, keepdims=True) + EPS)
    return x * r * g


def _rms_bwd(x, g, dy):
    r = lax.rsqrt(jnp.mean(x * x, axis=-1, keepdims=True) + EPS)
    xh = x * r
    dg = jnp.sum(dy * xh, axis=0, keepdims=True)
    dxh = dy * g
    dx = r * (dxh - xh * jnp.mean(dxh * xh, axis=-1, keepdims=True))
    return dx, dg


def _group_mean64(x):
    lane = lax.broadcasted_iota(jnp.int32, x.shape, 1)
    lo = lane < HEAD_DIM
    s_lo = jnp.sum(jnp.where(lo, x, 0.0), axis=-1, keepdims=True)
    s_hi = jnp.sum(jnp.where(lo, 0.0, x), axis=-1, keepdims=True)
    return jnp.where(lo, s_lo, s_hi) * (1.0 / HEAD_DIM)


def _swap32(x):
    lane = lax.broadcasted_iota(jnp.int32, x.shape, 1)
    first = (lane % HEAD_DIM) < (HEAD_DIM // 2)
    return jnp.where(first, pltpu.roll(x, LANES - HEAD_DIM // 2, axis=1), pltpu.roll(x, HEAD_DIM // 2, axis=1))


def _chunks(w):
    return [slice(j * LANES, (j + 1) * LANES) for j in range(w // LANES)]


def _aug_pair(qk, f_cols, is_query):
    lane = lax.broadcasted_iota(jnp.int32, (qk.shape[0], HEAD_DIM), 1)
    out = []
    for hh in range(2):
        hi, mid, lo = (p.astype(F32) for p in _split3(f_cols[hh] * LOG2E))
        if is_query:
            aux = jnp.where(lane == 0, hi, jnp.where(lane == 1, mid, jnp.where(lane == 2, lo, jnp.where(lane < 6, 1.0, 0.0))))
        else:
            aux = jnp.where(lane < 3, 1.0, jnp.where(lane == 3, -hi, jnp.where(lane == 4, -mid, jnp.where(lane == 5, -lo, 0.0))))
        out += [qk[:, hh * HEAD_DIM:(hh + 1) * HEAD_DIM], aux]
    return jnp.concatenate(out, axis=-1).astype(BF)


def _mem_kv_fwd(mem, g_mem, w_xkv, g_xk):
    m_tok = mem.shape[0]

    def body(mem_ref, gm_ref, w_ref, gk_ref, memn_ref, kraw_ref, kn_ref, v_ref):
        mn = _rms_fwd(mem_ref[...], gm_ref[...]).astype(BF)
        memn_ref[...] = mn
        kv = jnp.dot(mn, w_ref[...], preferred_element_type=F32)
        k = kv[:, :D_MODEL]
        kraw_ref[...] = k
        v_ref[...] = kv[:, D_MODEL:].astype(BF)
        for h in range(N_XH):
            sl = slice(h * XHD, (h + 1) * XHD)
            kn_ref[:, sl] = _rms_fwd(k[:, sl], gk_ref[...]).astype(BF)

    return pl.pallas_call(
        body, name="mem_kv_fwd",
        out_shape=(jax.ShapeDtypeStruct((m_tok, D_MODEL), BF), jax.ShapeDtypeStruct((m_tok, D_MODEL), F32),
                   jax.ShapeDtypeStruct((m_tok, D_MODEL), BF), jax.ShapeDtypeStruct((m_tok, D_MODEL), BF)),
        in_specs=[VMEM_SPEC] * 4, out_specs=(VMEM_SPEC,) * 4, compiler_params=_cparams(),
    )(mem, g_mem, w_xkv, g_xk)


def _in_proj_fwd(x, g_mix, w_main, w_ff, b_pad, cos_t, sin_t, gq_t, gk_t):
    t_len = x.shape[0]
    tm = min(ROW_TILE, t_len)
    n_t = t_len // tm

    def body(x_ref, g_ref, wm_ref, wf_ref, b_ref, cos_ref, sin_ref, gq_ref, gk_ref,
             n1_ref, proj_ref, rq_ref, rk_ref, qa_ref, ka_ref, z_ref, carry):
        i = pl.program_id(0)

        @pl.when(i == 0)
        def _():
            carry[...] = jnp.zeros_like(carry)

        n1 = _rms_fwd(x_ref[...], g_ref[...]).astype(BF)
        n1_ref[...] = n1
        proj = _dot_nt(n1, wm_ref[...])
        proj_ref[...] = proj.astype(BF)
        z = _dot_nt(n1, wf_ref[...]) + b_ref[...]
        z_ref[...] = z
        lane = lax.broadcasted_iota(jnp.int32, z.shape, 1)
        lf = jnp.where(lane < N_HEADS, jnp.minimum(z, 0.0) - jnp.log(1.0 + jnp.exp(-jnp.abs(z))), 0.0)
        row = lax.broadcasted_iota(jnp.int32, (tm, tm), 0)
        col = lax.broadcasted_iota(jnp.int32, (tm, tm), 1)
        tri = (row >= col).astype(BF)
        fc = _dot_exact(tri, lf) + carry[0:1, :]
        carry[...] = jnp.broadcast_to(fc[tm - 1:tm, :], carry.shape)
        c, s = cos_ref[...], sin_ref[...]
        for j, sl in enumerate(_chunks(GROUP_W)):
            q = proj[:, sl]
            rq_ref[:, sl] = ((q * c + _swap32(q) * s) * 0.125).astype(BF)
            k = proj[:, GROUP_W + j * LANES:GROUP_W + (j + 1) * LANES]
            rk_ref[:, sl] = (k * c + _swap32(k) * s).astype(BF)
            f_cols = [fc[:, 2 * j:2 * j + 1], fc[:, 2 * j + 1:2 * j + 2]]
            fq = proj[:, 4 * GROUP_W + j * LANES:4 * GROUP_W + (j + 1) * LANES]
            fq = fq * lax.rsqrt(_group_mean64(fq * fq) + EPS) * gq_ref[...] * (0.125 * LOG2E)
            qa_ref[:, 2 * j * LANES:2 * (j + 1) * LANES] = _aug_pair(fq, f_cols, True)
            fk = proj[:, 5 * GROUP_W + j * LANES:5 * GROUP_W + (j + 1) * LANES]
            fk = fk * lax.rsqrt(_group_mean64(fk * fk) + EPS) * gk_ref[...]
            ka_ref[:, 2 * j * LANES:2 * (j + 1) * LANES] = _aug_pair(fk, f_cols, False)

    row_spec = lambda w: pl.BlockSpec((tm, w), lambda i: (i, 0))
    full = lambda a: pl.BlockSpec(a.shape, lambda i: (0,) * a.ndim)
    return pl.pallas_call(
        body, name="in_proj_fwd", grid=(n_t,),
        out_shape=(jax.ShapeDtypeStruct((t_len, D_MODEL), BF), jax.ShapeDtypeStruct((t_len, MAIN_W), BF),
                   jax.ShapeDtypeStruct((t_len, GROUP_W), BF), jax.ShapeDtypeStruct((t_len, GROUP_W), BF),
                   jax.ShapeDtypeStruct((t_len, 2 * GROUP_W), BF), jax.ShapeDtypeStruct((t_len, 2 * GROUP_W), BF),
                   jax.ShapeDtypeStruct((t_len, LANES), F32)),
        in_specs=[row_spec(D_MODEL), full(g_mix), full(w_main), full(w_ff), full(b_pad), row_spec(LANES), row_spec(LANES),
                  full(gq_t), full(gk_t)],
        out_specs=(row_spec(D_MODEL), row_spec(MAIN_W), row_spec(GROUP_W), row_spec(GROUP_W), row_spec(2 * GROUP_W),
                   row_spec(2 * GROUP_W), row_spec(LANES)),
        scratch_shapes=[pltpu.VMEM((8, LANES), F32)],
        compiler_params=_cparams(("arbitrary",)),
    )(x, g_mix, w_main, w_ff, b_pad, cos_t, sin_t, gq_t, gk_t)


def _decay_tables(c):
    h = np.arange(N_HEADS, dtype=np.float64)
    lg = np.log(1.0 - 2.0 ** (-5.0 - h)).astype(np.float32).astype(np.float64)
    t = np.arange(c)
    same_or_earlier = (t[None, :] // REF_CHUNK) <= (t[:, None] // REF_CHUNK)
    w = np.where(same_or_earlier[None], np.exp(lg[:, None, None] * np.abs(t[:, None] - t[None, :])[None]), 0.0)
    qd = np.exp(lg[:, None] * (t[None, :] + 1.0))
    kd = np.exp(lg[:, None] * (c - 1.0 - t[None, :]))
    cd = np.exp(lg * c)
    ones = np.ones((1, 1, HEAD_DIM))
    return (jnp.asarray(w, F32), jnp.asarray(qd[:, :, None] * ones, F32), jnp.asarray(kd[:, :, None] * ones, F32),
            jnp.asarray(cd[:, None, None] * np.ones((1, HEAD_DIM, HEAD_DIM)), F32))


def _retention_fwd(rq, rk, proj, g_ret, tables):
    t_len = rq.shape[0]
    c = min(RET_BLOCK, t_len)
    n_b = t_len // c
    wdec, qdec, kdec, cdec = tables
    v_col, g_col = 2 * GROUP_W // LANES, 3 * GROUP_W // LANES

    def body(q_ref, k_ref, v_ref, rg_ref, g_ref, w_ref, qd_ref, kd_ref, cd_ref, raw_ref, mix_ref, st_ref, state):
        i = pl.program_id(1)

        @pl.when(i == 0)
        def _():
            state[...] = jnp.zeros_like(state)

        q2, k2, v2 = q_ref[...], k_ref[...], v_ref[...]
        outs = []
        for hh in range(2):
            sl = slice(hh * HEAD_DIM, (hh + 1) * HEAD_DIM)
            q, k, v = q2[:, sl], k2[:, sl], v2[:, sl]
            sp = state[hh]
            st_ref[0, 0, hh] = sp
            a = _dot_nt(q, k) * w_ref[hh]
            o = _dot(a, v) + _dot(q.astype(F32) * qd_ref[hh], sp)
            state[hh] = sp * cd_ref[hh] + _dot_tn(k.astype(F32) * kd_ref[hh], v)
            outs.append(o)
        o2 = jnp.concatenate(outs, axis=-1)
        raw_ref[...] = o2
        xc = o2 - _group_mean64(o2)
        xh = xc * lax.rsqrt(_group_mean64(xc * xc) + EPS)
        gate = rg_ref[...].astype(F32)
        mix_ref[...] = (gate * _sigmoid(gate) * (xh * g_ref[0])).astype(BF)

    blk = lambda col0: pl.BlockSpec((c, LANES), lambda hp, i: (i, col0 + hp))
    tab = lambda a: pl.BlockSpec((2,) + a.shape[1:], lambda hp, i: (hp, 0, 0))
    return pl.pallas_call(
        body, name="retention_fwd", grid=(N_HEADS // 2, n_b),
        out_shape=(jax.ShapeDtypeStruct((t_len, GROUP_W), F32), jax.ShapeDtypeStruct((t_len, GROUP_W), BF),
                   jax.ShapeDtypeStruct((N_HEADS // 2, n_b, 2, HEAD_DIM, HEAD_DIM), F32)),
        in_specs=[blk(0), blk(0), blk(v_col), blk(g_col), pl.BlockSpec((1, 1, LANES), lambda hp, i: (hp, 0, 0)),
                  tab(wdec), tab(qdec), tab(kdec), tab(cdec)],
        out_specs=(blk(0), blk(0), pl.BlockSpec((1, 1, 2, HEAD_DIM, HEAD_DIM), lambda hp, i: (hp, i, 0, 0, 0))),
        scratch_shapes=[pltpu.VMEM((2, HEAD_DIM, HEAD_DIM), F32)],
        compiler_params=_cparams(("arbitrary", "arbitrary")),
    )(rq, rk, proj, proj, g_ret, wdec, qdec, kdec, cdec)


def _fox_fwd(q_aug, k_aug, proj, shards):
    t_len = q_aug.shape[0]
    tq = min(ATT_BLOCK, t_len)
    n_q = t_len // tq
    v_col = 6 * GROUP_W // LANES
    tc = min(512, t_len)
    n_w = len(shards)
    n_steps = (N_HEADS // 2) * n_q

    def body(*refs):
        q_ref, k_ref, v_ref = refs[:3]
        o_ref, o32_ref, lse_ref = refs[3 + n_w:6 + n_w]
        vt = refs[6 + 2 * n_w]
        comm = (refs[3:3 + n_w], refs[6 + n_w:6 + 2 * n_w]) + tuple(refs[7 + 2 * n_w:])
        i = pl.program_id(1)
        step = pl.program_id(0) * n_q + i

        @pl.when(step == 0)
        def _():
            _gather_phase(0, *comm)

        @pl.when(step == (3 * n_steps) // 4)
        def _():
            _gather_phase(1, *comm)

        @pl.when(i == 0)
        def _():
            for c0 in range(0, t_len, tc):
                vt[:, c0:c0 + tc] = v_ref[c0:c0 + tc, :].T

        qs = [q_ref[:, hh * LANES:(hh + 1) * LANES] for hh in range(2)]
        ones = jnp.ones((HEAD_DIM, tq), BF)

        def scores(j):
            k2 = k_ref[pl.ds(pl.multiple_of(j * tq, tq), tq), :]
            return tuple(_dot_nt(k2[:, hh * LANES:(hh + 1) * LANES], qs[hh]) for hh in range(2))

        def update(j, ss, carry, masked):
            v2 = vt[:, pl.ds(pl.multiple_of(j * tq, tq), tq)]
            ps, stats = [], []
            for hh in range(2):
                m = carry[hh][0]
                s_t = ss[hh]
                if masked:
                    krow = lax.broadcasted_iota(jnp.int32, (tq, tq), 0)
                    qcol = lax.broadcasted_iota(jnp.int32, (tq, tq), 1)
                    s_t = jnp.where(qcol >= krow, s_t, NEG)
                m_new = jnp.maximum(m, jnp.max(s_t, axis=0, keepdims=True))
                ps.append(jnp.exp2(s_t - m_new).astype(BF))
                stats.append((m_new, jnp.exp2(m - m_new)))
            out = []
            for hh in range(2):
                m_new, alpha = stats[hh]
                v_aug = jnp.concatenate([v2[hh * HEAD_DIM:(hh + 1) * HEAD_DIM, :], ones], axis=0)
                out.append((m_new, carry[hh][1] * alpha + jnp.dot(v_aug, ps[hh], preferred_element_type=F32)))
            return tuple(out)

        def advance(j, state):
            ss, carry = state
            return scores(j + 1), update(j, ss, carry, False)

        init = tuple((jnp.full((1, tq), NEG, F32), jnp.zeros((LANES, tq), F32)) for _ in range(2))
        ss, carry = lax.fori_loop(0, i, advance, (scores(0), init))
        carry = update(i, ss, carry, True)
        outs, lses = [], []
        for hh in range(2):
            m, acc = carry[hh]
            l = acc[HEAD_DIM:HEAD_DIM + 1, :]
            outs.append(acc[:HEAD_DIM, :] / l)
            lses.append(m + jnp.log2(l))
        o2 = jnp.concatenate(outs, axis=0).T
        o32_ref[...] = o2
        o_ref[...] = o2.astype(BF)
        lse_ref[0] = jnp.concatenate(lses, axis=0)

        @pl.when(step == n_steps - 1)
        def _():
            _gather_phase(2, *comm)

    return pl.pallas_call(
        body, name="fox_fwd", grid=(N_HEADS // 2, n_q),
        out_shape=(jax.ShapeDtypeStruct((t_len, GROUP_W), BF), jax.ShapeDtypeStruct((t_len, GROUP_W), F32),
                   jax.ShapeDtypeStruct((N_HEADS // 2, 2, t_len), F32))
        + tuple(jax.ShapeDtypeStruct((4,) + s.shape, s.dtype) for s in shards),
        in_specs=[pl.BlockSpec((tq, 2 * LANES), lambda hp, i: (i, hp)),
                  pl.BlockSpec((t_len, 2 * LANES), lambda hp, i: (0, hp)),
                  pl.BlockSpec((t_len, LANES), lambda hp, i: (0, v_col + hp))] + [ANY] * n_w,
        out_specs=(pl.BlockSpec((tq, LANES), lambda hp, i: (i, hp)), pl.BlockSpec((tq, LANES), lambda hp, i: (i, hp)),
                   pl.BlockSpec((1, 2, tq), lambda hp, i: (hp, 0, i))) + (ANY,) * n_w,
        scratch_shapes=[pltpu.VMEM((LANES, t_len), BF)] + _gather_scratch(n_w),
        compiler_params=_cparams(("arbitrary", "arbitrary")),
    )(q_aug, k_aug, proj, *shards)


def _softmax_rows(s):
    p = jnp.exp(s - jnp.max(s, axis=-1, keepdims=True))
    return p / jnp.sum(p, axis=-1, keepdims=True)


def _attn_out_xattn_fwd(x, mix_r, mix_f, w_out, g_xattn, w_xq, g_xq, kn, v, w_xo):
    t_len = x.shape[0]
    tm = min(ROW_TILE, t_len)

    def body(x_ref, mr_ref, mf_ref, wo_ref, g_ref, wq_ref, gq_ref, kn_ref, v_ref, wxo_ref,
             h1_ref, hn_ref, qx_ref, o_ref, h2_ref):
        h1 = x_ref[...] + jnp.dot(mr_ref[...], wo_ref[:GROUP_W, :], preferred_element_type=F32) \
            + jnp.dot(mf_ref[...], wo_ref[GROUP_W:, :], preferred_element_type=F32)
        h1_ref[...] = h1
        hn = _rms_fwd(h1, g_ref[...]).astype(BF)
        hn_ref[...] = hn
        qx = jnp.dot(hn, wq_ref[...], preferred_element_type=F32).astype(BF)
        qx_ref[...] = qx
        for h in range(N_XH):
            sl = slice(h * XHD, (h + 1) * XHD)
            qn = _rms_fwd(qx[:, sl].astype(F32), gq_ref[...])
            p = _softmax_rows(_dot_nt(qn, kn_ref[:, sl]) * (XHD ** -0.5))
            o_ref[:, sl] = _dot(p, v_ref[:, sl]).astype(BF)
        h2_ref[...] = h1 + jnp.dot(o_ref[...], wxo_ref[...], preferred_element_type=F32)

    row_spec = lambda w: pl.BlockSpec((tm, w), lambda i: (i, 0))
    full = lambda a: pl.BlockSpec(a.shape, lambda i: (0,) * a.ndim)
    return pl.pallas_call(
        body, name="attn_out_xattn_fwd", grid=(t_len // tm,),
        out_shape=(jax.ShapeDtypeStruct((t_len, D_MODEL), F32), jax.ShapeDtypeStruct((t_len, D_MODEL), BF),
                   jax.ShapeDtypeStruct((t_len, D_MODEL), BF), jax.ShapeDtypeStruct((t_len, D_MODEL), BF),
                   jax.ShapeDtypeStruct((t_len, D_MODEL), F32)),
        in_specs=[row_spec(D_MODEL), row_spec(GROUP_W), row_spec(GROUP_W), full(w_out), full(g_xattn), full(w_xq), full(g_xq),
                  full(kn), full(v), full(w_xo)],
        out_specs=(row_spec(D_MODEL),) * 5,
        compiler_params=_cparams(("arbitrary",)),
    )(x, mix_r, mix_f, w_out, g_xattn, w_xq, g_xq, kn, v, w_xo)


def _ffn_loss_fwd(h2, g_ffn, w_gate, w_up, w_down, target):
    t_len = h2.shape[0]
    tm = min(ROW_TILE, t_len)

    def body(h2_ref, g_ref, wg_ref, wu_ref, wd_ref, tgt_ref, hn_ref, gate_ref, up_ref, act_ref, dh3_ref, loss_ref):
        @pl.when(pl.program_id(0) == 0)
        def _():
            loss_ref[...] = jnp.zeros_like(loss_ref)

        h2v = h2_ref[...]
        hn = _rms_fwd(h2v, g_ref[...]).astype(BF)
        hn_ref[...] = hn
        gate = _dot_nt(hn, wg_ref[...])
        up = _dot_nt(hn, wu_ref[...])
        gate_ref[...] = gate.astype(BF)
        up_ref[...] = up.astype(BF)
        act = (gate * _sigmoid(gate) * up).astype(BF)
        act_ref[...] = act
        diff = h2v + jnp.dot(act, wd_ref[...], preferred_element_type=F32) - tgt_ref[...]
        dh3_ref[...] = diff * (1.0 / D_MODEL)
        per_row = jnp.sum(diff * diff, axis=-1, keepdims=True) * (1.0 / D_MODEL)
        loss_ref[...] += 0.5 * jnp.sum(per_row, axis=0, keepdims=True)

    row_spec = lambda w: pl.BlockSpec((tm, w), lambda i: (i, 0))
    full = lambda a: pl.BlockSpec(a.shape, lambda i: (0,) * a.ndim, pipeline_mode=pl.Buffered(1))
    return pl.pallas_call(
        body, name="ffn_loss_fwd", grid=(t_len // tm,),
        out_shape=(jax.ShapeDtypeStruct((t_len, D_MODEL), BF), jax.ShapeDtypeStruct((t_len, D_FF), BF),
                   jax.ShapeDtypeStruct((t_len, D_FF), BF), jax.ShapeDtypeStruct((t_len, D_FF), BF),
                   jax.ShapeDtypeStruct((t_len, D_MODEL), F32), jax.ShapeDtypeStruct((8, LANES), F32)),
        in_specs=[row_spec(D_MODEL), full(g_ffn), full(w_gate), full(w_up), full(w_down), row_spec(D_MODEL)],
        out_specs=(row_spec(D_MODEL), row_spec(D_FF), row_spec(D_FF), row_spec(D_FF), row_spec(D_MODEL),
                   pl.BlockSpec((8, LANES), lambda i: (0, 0))),
        compiler_params=_cparams(("arbitrary",)),
    )(h2, g_ffn, w_gate, w_up, w_down, target)


def _ffn_bwd(dh3, gate, up, h2, g_ffn, w_gate, w_up, w_down):
    t_len = h2.shape[0]
    tm = min(ROW_TILE, t_len)

    def body(dh3_ref, gate_ref, up_ref, h2_ref, g_ref, wg_ref, wu_ref, wd_ref, dgate_ref, dup_ref, dh2_ref, dg_ref):
        @pl.when(pl.program_id(0) == 0)
        def _():
            dg_ref[...] = jnp.zeros_like(dg_ref)

        dh3v = dh3_ref[...]
        dact = _dot_nt(dh3v, wd_ref[...])
        g = gate_ref[...].astype(F32)
        sg = _sigmoid(g)
        dup = (dact * (g * sg)).astype(BF)
        dgate = (dact * up_ref[...].astype(F32) * (sg * (1.0 + g * (1.0 - sg)))).astype(BF)
        dup_ref[...] = dup
        dgate_ref[...] = dgate
        dhn = jnp.dot(dgate, wg_ref[...], preferred_element_type=F32) + jnp.dot(dup, wu_ref[...], preferred_element_type=F32)
        dx, dg = _rms_bwd(h2_ref[...], g_ref[...], dhn)
        dh2_ref[...] = dh3v + dx
        dg_ref[...] += dg

    row_spec = lambda w: pl.BlockSpec((tm, w), lambda i: (i, 0))
    full = lambda a: pl.BlockSpec(a.shape, lambda i: (0,) * a.ndim, pipeline_mode=pl.Buffered(1))
    return pl.pallas_call(
        body, name="ffn_bwd", grid=(t_len // tm,),
        out_shape=(jax.ShapeDtypeStruct((t_len, D_FF), BF), jax.ShapeDtypeStruct((t_len, D_FF), BF),
                   jax.ShapeDtypeStruct((t_len, D_MODEL), F32), jax.ShapeDtypeStruct((1, D_MODEL), F32)),
        in_specs=[row_spec(D_MODEL), row_spec(D_FF), row_spec(D_FF), row_spec(D_MODEL), full(g_ffn), full(w_gate), full(w_up),
                  full(w_down)],
        out_specs=(row_spec(D_FF), row_spec(D_FF), row_spec(D_MODEL), pl.BlockSpec((1, D_MODEL), lambda i: (0, 0))),
        compiler_params=_cparams(("arbitrary",)),
    )(dh3, gate, up, h2, g_ffn, w_gate, w_up, w_down)


def _attn_out_xattn_bwd(dh2, h1, qx, kn, v, w_xo, w_xq, w_out, g_xattn, g_xq):
    t_len = h1.shape[0]
    tm = min(ROW_TILE, t_len)
    m_tok = kn.shape[0]

    def body(dh2_ref, h1_ref, qx_ref, kn_ref, v_ref, wxo_ref, wq_ref, wo_ref, g_ref, gq_ref,
             dqx_ref, dh1_ref, dmr_ref, dmf_ref, dkn_ref, dv_ref, dg_ref, dgq_ref, dqx_scr):
        @pl.when(pl.program_id(0) == 0)
        def _():
            dkn_ref[...] = jnp.zeros_like(dkn_ref)
            dv_ref[...] = jnp.zeros_like(dv_ref)
            dg_ref[...] = jnp.zeros_like(dg_ref)
            dgq_ref[...] = jnp.zeros_like(dgq_ref)

        dh2v = dh2_ref[...]
        do = _dot_nt(dh2v, wxo_ref[...])
        gq = gq_ref[...]
        dgq = jnp.zeros((1, XHD), F32)
        for h in range(N_XH):
            sl = slice(h * XHD, (h + 1) * XHD)
            qraw = qx_ref[:, sl].astype(F32)
            qn = _rms_fwd(qraw, gq)
            p = _softmax_rows(_dot_nt(qn, kn_ref[:, sl]) * (XHD ** -0.5))
            doh = do[:, sl]
            dv_ref[:, sl] += _dot_tn(p, doh)
            dp = _dot_nt(doh, v_ref[:, sl])
            ds = p * (dp - jnp.sum(dp * p, axis=-1, keepdims=True)) * (XHD ** -0.5)
            dqn = _dot(ds, kn_ref[:, sl])
            dkn_ref[:, sl] += _dot_tn(ds, qn)
            dx, dg_h = _rms_bwd(qraw, gq, dqn)
            dgq = dgq + dg_h
            dqx_scr[:, sl] = dx.astype(BF)
        dgq_ref[...] += dgq
        dqx = dqx_scr[...]
        dqx_ref[...] = dqx
        dhn = _dot_nt(dqx, wq_ref[...])
        dx, dg = _rms_bwd(h1_ref[...], g_ref[...], dhn)
        dg_ref[...] += dg
        dh1 = dh2v + dx
        dh1_ref[...] = dh1
        dmix = _dot_nt(dh1, wo_ref[...])
        dmr_ref[...] = dmix[:, :GROUP_W]
        dmf_ref[...] = dmix[:, GROUP_W:].astype(BF)

    row_spec = lambda w: pl.BlockSpec((tm, w), lambda i: (i, 0))
    full = lambda a: pl.BlockSpec(a.shape, lambda i: (0,) * a.ndim)
    acc = lambda r, c: pl.BlockSpec((r, c), lambda i: (0, 0))
    return pl.pallas_call(
        body, name="attn_out_xattn_bwd", grid=(t_len // tm,),
        out_shape=(jax.ShapeDtypeStruct((t_len, D_MODEL), BF), jax.ShapeDtypeStruct((t_len, D_MODEL), F32),
                   jax.ShapeDtypeStruct((t_len, GROUP_W), F32), jax.ShapeDtypeStruct((t_len, GROUP_W), BF),
                   jax.ShapeDtypeStruct((m_tok, D_MODEL), F32), jax.ShapeDtypeStruct((m_tok, D_MODEL), F32),
                   jax.ShapeDtypeStruct((1, D_MODEL), F32), jax.ShapeDtypeStruct((1, XHD), F32)),
        in_specs=[row_spec(D_MODEL), row_spec(D_MODEL), row_spec(D_MODEL), full(kn), full(v), full(w_xo), full(w_xq), full(w_out),
                  full(g_xattn), full(g_xq)],
        out_specs=(row_spec(D_MODEL), row_spec(D_MODEL), row_spec(GROUP_W), row_spec(GROUP_W), acc(m_tok, D_MODEL),
                   acc(m_tok, D_MODEL), acc(1, D_MODEL), acc(1, XHD)),
        scratch_shapes=[pltpu.VMEM((tm, D_MODEL), BF)],
        compiler_params=_cparams(("arbitrary",)),
    )(dh2, h1, qx, kn, v, w_xo, w_xq, w_out, g_xattn, g_xq)


def _mem_kv_bwd(dkn, dv, kraw, mem, memn, g_mem, g_xk, w_xkv):
    m_tok = mem.shape[0]

    def body(dkn_ref, dv_ref, kraw_ref, mem_ref, memn_ref, gm_ref, gk_ref, w_ref, dw_ref, dgm_ref, dgk_ref, dkv_scr):
        gk = gk_ref[...]
        dgk = jnp.zeros((1, XHD), F32)
        for h in range(N_XH):
            sl = slice(h * XHD, (h + 1) * XHD)
            dx, dg_h = _rms_bwd(kraw_ref[:, sl], gk, dkn_ref[:, sl])
            dgk = dgk + dg_h
            dkv_scr[:, sl] = dx.astype(BF)
        dgk_ref[...] = dgk
        dkv_scr[:, D_MODEL:] = dv_ref[...].astype(BF)
        dkv = dkv_scr[...]
        dw_ref[...] = _dot_tn(memn_ref[...], dkv)
        dmemn = _dot_nt(dkv, w_ref[...])
        mem_v = mem_ref[...]
        r = lax.rsqrt(jnp.mean(mem_v * mem_v, axis=-1, keepdims=True) + EPS)
        dgm_ref[...] = jnp.sum(dmemn * mem_v * r, axis=0, keepdims=True)

    return pl.pallas_call(
        body, name="mem_kv_bwd",
        out_shape=(jax.ShapeDtypeStruct((D_MODEL, 2 * D_MODEL), F32), jax.ShapeDtypeStruct((1, D_MODEL), F32),
                   jax.ShapeDtypeStruct((1, XHD), F32)),
        in_specs=[VMEM_SPEC] * 8, out_specs=(VMEM_SPEC,) * 3,
        scratch_shapes=[pltpu.VMEM((m_tok, 2 * D_MODEL), BF)],
        compiler_params=_cparams(),
    )(dkn, dv, kraw, mem, memn, g_mem, g_xk, w_xkv)


def _fox_bwd(q_aug, k_aug, proj, dmf, o32, lse, sums):
    t_len = q_aug.shape[0]
    tb = min(ATT_BLOCK, t_len)
    n_b = t_len // tb
    v_col = 6 * GROUP_W // LANES
    n_w = len(sums)
    n_steps = (N_HEADS // 2) * n_b

    def body(*refs):
        k_ref, v_ref, q_ref, do_ref, o_ref, lse_ref = refs[:6]
        dq_ref, dk_ref, dv_ref, df_ref = refs[6 + n_w:10 + n_w]
        delta = refs[10 + 2 * n_w]
        comm = (refs[6:6 + n_w], refs[10 + n_w:10 + 2 * n_w]) + tuple(refs[11 + 2 * n_w:])
        j = pl.program_id(1)
        step = pl.program_id(0) * n_b + j

        @pl.when(step == 0)
        def _():
            _scatter_phase(0, *comm)

        @pl.when(j == 0)
        def _():
            dq_ref[...] = jnp.zeros_like(dq_ref)
            dd = do_ref[...].astype(F32) * o_ref[...]
            hrow = lax.broadcasted_iota(jnp.int32, (8, LANES), 0)
            lane = lax.broadcasted_iota(jnp.int32, (8, LANES), 1)
            ind = ((lane // HEAD_DIM) == hrow).astype(BF)
            delta[...] = _dot_nt_exact(ind, dd)

        k2, v2 = k_ref[...], v_ref[...]
        ks = [k2[:, hh * LANES:(hh + 1) * LANES] for hh in range(2)]
        vs = [v2[:, hh * HEAD_DIM:(hh + 1) * HEAD_DIM] for hh in range(2)]

        def block(i, carry, masked):
            rows = pl.ds(pl.multiple_of(i * tb, tb), tb)
            q2 = q_ref[rows, :]
            do2 = do_ref[rows, :]
            qs = [q2[:, hh * LANES:(hh + 1) * LANES] for hh in range(2)]
            dos = [do2[:, hh * HEAD_DIM:(hh + 1) * HEAD_DIM] for hh in range(2)]
            ss = [_dot_nt(ks[hh], qs[hh]) for hh in range(2)]
            dps = [_dot_nt(vs[hh], dos[hh]) for hh in range(2)]
            pts, dsts, dfs = [], [], []
            for hh in range(2):
                s_t = ss[hh]
                if masked:
                    krow = lax.broadcasted_iota(jnp.int32, (tb, tb), 0)
                    qcol = lax.broadcasted_iota(jnp.int32, (tb, tb), 1)
                    s_t = jnp.where(qcol >= krow, s_t, NEG)
                p_t = jnp.exp2(s_t - lse_ref[0, hh:hh + 1, rows])
                pts.append(p_t.astype(BF))
                ds_t = p_t * (dps[hh] - delta[hh:hh + 1, rows])
                dsts.append(ds_t.astype(BF))
                dfs.append(jnp.sum(ds_t, axis=-1, keepdims=True))
            out = []
            for hh in range(2):
                dk, dv, df = carry[hh]
                dv = dv + jnp.dot(pts[hh], dos[hh], preferred_element_type=F32)
                dk = dk + jnp.dot(dsts[hh], qs[hh], preferred_element_type=F32)
                dq_ref[rows, hh * HEAD_DIM:(hh + 1) * HEAD_DIM] += _dot_tn(dsts[hh], ks[hh])[:, :HEAD_DIM]
                out.append((dk, dv, df - dfs[hh]))
            return tuple(out)

        init = tuple((jnp.zeros((tb, LANES), F32), jnp.zeros((tb, HEAD_DIM), F32), jnp.zeros((tb, 1), F32)) for _ in range(2))
        carry = block(j, init, True)
        carry = lax.fori_loop(j + 1, n_b, lambda i, c: block(i, c, False), carry)
        dk_ref[...] = jnp.concatenate([carry[hh][0][:, :HEAD_DIM] for hh in range(2)], axis=-1) * LN2
        dv_ref[...] = jnp.concatenate([carry[hh][1] for hh in range(2)], axis=-1)
        df_ref[0] = jnp.concatenate([carry[hh][2] for hh in range(2)], axis=-1)

        @pl.when(step == n_steps - 1)
        def _():
            _scatter_phase(1, *comm)

    blk = lambda w, col0: pl.BlockSpec((tb, w), lambda hp, j: (j, col0 + hp))
    whole = lambda w: pl.BlockSpec((t_len, w), lambda hp, j: (0, hp))
    rows2 = pl.BlockSpec((1, 2, t_len), lambda hp, j: (hp, 0, 0))
    cols2 = pl.BlockSpec((1, tb, 2), lambda hp, j: (hp, j, 0))
    return pl.pallas_call(
        body, name="fox_bwd", grid=(N_HEADS // 2, n_b),
        out_shape=(jax.ShapeDtypeStruct((t_len, GROUP_W), F32), jax.ShapeDtypeStruct((t_len, GROUP_W), F32),
                   jax.ShapeDtypeStruct((t_len, GROUP_W), F32), jax.ShapeDtypeStruct((N_HEADS // 2, t_len, 2), F32))
        + _scatter_out_shapes(sums),
        in_specs=[blk(2 * LANES, 0), blk(LANES, v_col), whole(2 * LANES), whole(LANES), whole(LANES), rows2] + [ANY] * n_w,
        out_specs=(whole(LANES), blk(LANES, 0), blk(LANES, 0), cols2) + (ANY,) * n_w,
        scratch_shapes=[pltpu.VMEM((8, t_len), F32)] + _scatter_scratch(n_w),
        compiler_params=_cparams(("arbitrary", "arbitrary")),
    )(k_aug, proj, q_aug, dmf, o32, lse, *sums)


def _retention_bwd(dmr, raw, proj, g_ret, rq, rk, states, tables):
    t_len = rq.shape[0]
    c = min(RET_BLOCK, t_len)
    n_b = t_len // c
    wdec, qdec, kdec, cdec = tables
    v_col, g_col = 2 * GROUP_W // LANES, 3 * GROUP_W // LANES

    def body(d_ref, raw_ref, rg_ref, g_ref, q_ref, k_ref, v_ref, st_ref, w_ref, qd_ref, kd_ref, cd_ref,
             dq_ref, dk_ref, dv_ref, drg_ref, dg_ref, gstate):
        @pl.when(pl.program_id(1) == 0)
        def _():
            gstate[...] = jnp.zeros_like(gstate)
            dg_ref[...] = jnp.zeros_like(dg_ref)

        d, raw_v, g = d_ref[...], raw_ref[...], g_ref[0]
        gate = rg_ref[...].astype(F32)
        xc = raw_v - _group_mean64(raw_v)
        r = lax.rsqrt(_group_mean64(xc * xc) + EPS)
        xh = xc * r
        sg = _sigmoid(gate)
        drg_ref[...] = d * (xh * g) * (sg * (1.0 + gate * (1.0 - sg)))
        dy = d * (gate * sg)
        dg_ref[0] += jnp.sum(dy * xh, axis=0, keepdims=True)
        dxh = dy * g
        do2 = r * (dxh - _group_mean64(dxh) - xh * _group_mean64(dxh * xh))
        q2, k2, v2 = q_ref[...], k_ref[...], v_ref[...]
        dqs, dks, dvs = [], [], []
        for hh in range(2):
            sl = slice(hh * HEAD_DIM, (hh + 1) * HEAD_DIM)
            q, k, v, do = q2[:, sl], k2[:, sl], v2[:, sl], do2[:, sl].astype(BF)
            w = w_ref[hh]
            a = _dot_nt(q, k) * w
            dm = _dot_nt(do, v) * w
            sp, gs = st_ref[0, 0, hh], gstate[hh]
            qd = q.astype(F32) * qd_ref[hh]
            kd = k.astype(F32) * kd_ref[hh]
            dqs.append(_dot(dm, k) + _dot_nt(do, sp) * qd_ref[hh])
            dks.append(_dot_tn(dm, q) + _dot_nt(v, gs) * kd_ref[hh])
            dvs.append(_dot_tn(a, do) + _dot(kd, gs))
            gstate[hh] = gs * cd_ref[hh] + _dot_tn(qd, do)
        dq_ref[...] = jnp.concatenate(dqs, axis=-1)
        dk_ref[...] = jnp.concatenate(dks, axis=-1)
        dv_ref[...] = jnp.concatenate(dvs, axis=-1)

    blk = lambda col0: pl.BlockSpec((c, LANES), lambda hp, i: (n_b - 1 - i, col0 + hp))
    tab = lambda a: pl.BlockSpec((2,) + a.shape[1:], lambda hp, i: (hp, 0, 0))
    gspec = pl.BlockSpec((1, 1, LANES), lambda hp, i: (hp, 0, 0))
    return pl.pallas_call(
        body, name="retention_bwd", grid=(N_HEADS // 2, n_b),
        out_shape=(jax.ShapeDtypeStruct((t_len, GROUP_W), F32),) * 4 + (jax.ShapeDtypeStruct((N_HEADS // 2, 1, LANES), F32),),
        in_specs=[blk(0), blk(0), blk(g_col), gspec, blk(0), blk(0), blk(v_col),
                  pl.BlockSpec((1, 1, 2, HEAD_DIM, HEAD_DIM), lambda hp, i: (hp, n_b - 1 - i, 0, 0, 0)),
                  tab(wdec), tab(qdec), tab(kdec), tab(cdec)],
        out_specs=(blk(0), blk(0), blk(0), blk(0), gspec),
        scratch_shapes=[pltpu.VMEM((2, HEAD_DIM, HEAD_DIM), F32)],
        compiler_params=_cparams(("arbitrary", "arbitrary")),
    )(dmr, raw, proj, g_ret, rq, rk, proj, states, wdec, qdec, kdec, cdec)


def _in_proj_bwd(x, g_mix, dh1, dq_r, dk_r, dv_r, drg, dq_f, dk_f, dv_f, df_col, proj, z, cos_t, sin_t, gq_t, gk_t, w_main, w_ff):
    t_len = x.shape[0]
    tm = min(ROW_TILE, t_len)
    n_t = t_len // tm

    def body(x_ref, g_ref, dh1_ref, dqr_ref, dkr_ref, dvr_ref, drg_ref, dqf_ref, dkf_ref, dvf_ref, df_ref, fq_ref, fk_ref, z_ref,
             cos_ref, sin_ref, gq_ref, gk_ref, wm_ref, wf_ref,
             dproj_ref, dz_ref, dx_ref, dg_ref, dgq_ref, dgk_ref, db_ref, carry, gq_acc, gk_acc):
        i = pl.program_id(0)

        @pl.when(i == 0)
        def _():
            carry[...] = jnp.zeros_like(carry)
            gq_acc[...] = jnp.zeros_like(gq_acc)
            gk_acc[...] = jnp.zeros_like(gk_acc)
            dg_ref[...] = jnp.zeros_like(dg_ref)
            db_ref[...] = jnp.zeros_like(db_ref)

        c, s = cos_ref[...], sin_ref[...]
        gq, gk = gq_ref[...], gk_ref[...]
        dgq = jnp.zeros((1, LANES), F32)
        dgk = jnp.zeros((1, LANES), F32)
        for sl in _chunks(GROUP_W):
            dy = dqr_ref[:, sl] * 0.125
            dproj_ref[:, sl] = (dy * c + _swap32(dy * s)).astype(BF)
            dy = dkr_ref[:, sl]
            dproj_ref[:, GROUP_W + sl.start:GROUP_W + sl.stop] = (dy * c + _swap32(dy * s)).astype(BF)
            dproj_ref[:, 2 * GROUP_W + sl.start:2 * GROUP_W + sl.stop] = dvr_ref[:, sl].astype(BF)
            dproj_ref[:, 3 * GROUP_W + sl.start:3 * GROUP_W + sl.stop] = drg_ref[:, sl].astype(BF)
            for src, dsrc, gain, off in ((fq_ref, dqf_ref, gq, 4), (fk_ref, dkf_ref, gk, 5)):
                xr = src[:, sl].astype(F32)
                r = lax.rsqrt(_group_mean64(xr * xr) + EPS)
                xh = xr * r
                dy = dsrc[:, sl] * (0.125 if off == 4 else 1.0)
                dgs = jnp.sum(dy * xh, axis=0, keepdims=True)
                if off == 4:
                    dgq = dgq + dgs
                else:
                    dgk = dgk + dgs
                dxh = dy * gain
                dproj_ref[:, off * GROUP_W + sl.start:off * GROUP_W + sl.stop] = \
                    (r * (dxh - xh * _group_mean64(dxh * xh))).astype(BF)
            dproj_ref[:, 6 * GROUP_W + sl.start:6 * GROUP_W + sl.stop] = dvf_ref[:, sl].astype(BF)
        gq_acc[...] += dgq
        gk_acc[...] += dgk
        row = lax.broadcasted_iota(jnp.int32, (tm, tm), 0)
        col = lax.broadcasted_iota(jnp.int32, (tm, tm), 1)
        dlf = _dot_exact((col >= row).astype(BF), df_ref[...]) + carry[0:1, :]
        carry[...] = jnp.broadcast_to(dlf[0:1, :], carry.shape)
        lane = lax.broadcasted_iota(jnp.int32, (tm, LANES), 1)
        dz = jnp.where(lane < N_HEADS, dlf / (1.0 + jnp.exp(z_ref[...])), 0.0)
        db_ref[...] += jnp.sum(dz, axis=0, keepdims=True)
        dz_bf = dz.astype(BF)
        dz_ref[...] = dz_bf
        dn1 = jnp.dot(dz_bf, wf_ref[...], preferred_element_type=F32)
        for sec in range(MAIN_W // GROUP_W):
            sl = slice(sec * GROUP_W, (sec + 1) * GROUP_W)
            dn1 = dn1 + jnp.dot(dproj_ref[:, sl], wm_ref[sl, :], preferred_element_type=F32)
        dx, dg = _rms_bwd(x_ref[...], g_ref[...], dn1)
        dx_ref[...] = dh1_ref[...] + dx
        dg_ref[...] += dg

        @pl.when(i == n_t - 1)
        def _():
            dgq_ref[...] = gq_acc[:, :HEAD_DIM] + gq_acc[:, HEAD_DIM:]
            dgk_ref[...] = gk_acc[:, :HEAD_DIM] + gk_acc[:, HEAD_DIM:]

    row_spec = lambda w, col=0: pl.BlockSpec((tm, w), lambda i: (n_t - 1 - i, col))
    full = lambda a: pl.BlockSpec(a.shape, lambda i: (0,) * a.ndim)
    acc = lambda r, c: pl.BlockSpec((r, c), lambda i: (0, 0))
    return pl.pallas_call(
        body, name="in_proj_bwd", grid=(n_t,),
        out_shape=(jax.ShapeDtypeStruct((t_len, MAIN_W), BF), jax.ShapeDtypeStruct((t_len, LANES), BF),
                   jax.ShapeDtypeStruct((t_len, D_MODEL), F32), jax.ShapeDtypeStruct((1, D_MODEL), F32),
                   jax.ShapeDtypeStruct((1, HEAD_DIM), F32), jax.ShapeDtypeStruct((1, HEAD_DIM), F32),
                   jax.ShapeDtypeStruct((1, LANES), F32)),
        in_specs=[row_spec(D_MODEL), full(g_mix), row_spec(D_MODEL)] + [row_spec(GROUP_W)] * 7
        + [row_spec(LANES), row_spec(GROUP_W, 4), row_spec(GROUP_W, 5), row_spec(LANES), row_spec(LANES), row_spec(LANES),
           full(gq_t), full(gk_t), full(w_main), full(w_ff)],
        out_specs=(row_spec(MAIN_W), row_spec(LANES), row_spec(D_MODEL), acc(1, D_MODEL), acc(1, HEAD_DIM), acc(1, HEAD_DIM),
                   acc(1, LANES)),
        scratch_shapes=[pltpu.VMEM((8, LANES), F32), pltpu.VMEM((1, LANES), F32), pltpu.VMEM((1, LANES), F32)],
        compiler_params=_cparams(("arbitrary",)),
    )(x, g_mix, dh1, dq_r, dk_r, dv_r, drg, dq_f, dk_f, dv_f, df_col, proj, proj, z, cos_t, sin_t, gq_t, gk_t, w_main, w_ff)


def _matmul_tn(a, b, name, bk=512):
    t_len, m = a.shape
    n = b.shape[1]
    bm = m if m <= TN_MAX_ROWS else m // 2
    bk = min(bk, t_len)

    def body(a_ref, b_ref, o_ref):
        @pl.when(pl.program_id(1) == 0)
        def _():
            o_ref[...] = jnp.zeros_like(o_ref)

        o_ref[...] += _dot_tn(a_ref[...], b_ref[...])

    return pl.pallas_call(
        body, name=name, grid=(m // bm, t_len // bk),
        out_shape=jax.ShapeDtypeStruct((m, n), F32),
        in_specs=[pl.BlockSpec((bk, bm), lambda i, k: (k, i)), pl.BlockSpec((bk, n), lambda i, k: (k, 0))],
        out_specs=pl.BlockSpec((bm, n), lambda i, k: (i, 0)),
        compiler_params=_cparams(("arbitrary", "arbitrary")),
    )(a, b)


def _place():
    x, y, c = lax.axis_index("x"), lax.axis_index("y"), lax.axis_index("c")
    chips = [(1 - x, y), (x, 1 - y), (1 - x, 1 - y)]
    return x, y, c, chips


def _row_chunks(rows, limit):
    step = max(d for d in range(16, min(rows, limit) + 1, 16) if rows % d == 0)
    return [slice(i, i + step) for i in range(0, rows, step)]


ICI_CHUNK_ROWS = 128
D2D_CHUNK_ROWS = 64


def _gather_phase(phase, ins, outs, send_sems, recv_sems):
    x, y, c, chips = _place()
    me_chip = 2 * x + y
    sibling = (x, y, 1 - c)

    def copy(w, k, slot, half, to, rows=slice(None), src=None):
        dst = outs[w].at[slot, half, rows]
        return pltpu.make_async_remote_copy(src_ref=dst if src is None else src, dst_ref=dst,
                                            send_sem=send_sems.at[w, k], recv_sem=recv_sems.at[w, k],
                                            device_id=to, device_id_type=MESH)

    for w in range(len(ins)):
        for j, (px, py) in enumerate(chips):
            if phase == 0:
                for rows in _row_chunks(ins[w].shape[1], ICI_CHUNK_ROWS):
                    copy(w, j, me_chip, c, (px, py, c), rows, src=ins[w].at[c, rows]).start()
            elif phase == 1:
                copy(w, j, 2 * px + py, c, (x, y, c)).wait_recv()
                for rows in _row_chunks(ins[w].shape[1], D2D_CHUNK_ROWS):
                    copy(w, 3 + j, 2 * px + py, c, sibling, rows).start()
            else:
                copy(w, 3 + j, 2 * px + py, 1 - c, (x, y, c)).wait_recv()
                copy(w, j, me_chip, c, (px, py, c), src=ins[w].at[c]).wait_send()
                copy(w, 3 + j, 2 * px + py, c, sibling).wait_send()


def _gather_scratch(n_w):
    return [pltpu.SemaphoreType.DMA((n_w, 6)), pltpu.SemaphoreType.DMA((n_w, 6))]


def _all_gather_weights(shards):
    n_w = len(shards)

    def body(*refs):
        for phase in range(3):
            _gather_phase(phase, refs[:n_w], refs[n_w:2 * n_w], *refs[2 * n_w:])

    return pl.pallas_call(
        body, name="all_gather_weights",
        out_shape=tuple(jax.ShapeDtypeStruct((4,) + s.shape, s.dtype) for s in shards),
        in_specs=[ANY] * n_w, out_specs=(ANY,) * n_w, scratch_shapes=_gather_scratch(n_w),
    )(*shards)


def _exchange_core_halves(grads):
    n_w = len(grads)

    def body(*refs):
        ins, theirs = refs[:n_w], refs[n_w:2 * n_w]
        send_sems, recv_sems = refs[2 * n_w:]
        x, y, c, _ = _place()

        def remote(w, k=slice(None), rows=slice(None)):
            return pltpu.make_async_remote_copy(src_ref=ins[w].at[k, 1 - c, rows], dst_ref=theirs[w].at[k, rows],
                                                send_sem=send_sems.at[w], recv_sem=recv_sems.at[w], device_id=(x, y, 1 - c),
                                                device_id_type=MESH)

        for w in range(n_w):
            for k in range(4):
                for rows in _row_chunks(ins[w].shape[2], D2D_CHUNK_ROWS):
                    remote(w, k, rows).start()
        for w in range(n_w):
            remote(w).wait()

    half = tuple(jax.ShapeDtypeStruct((4,) + g.shape[2:], g.dtype) for g in grads)
    return pl.pallas_call(
        body, name="exchange_core_halves", out_shape=half,
        in_specs=[ANY] * n_w, out_specs=(ANY,) * n_w,
        scratch_shapes=[pltpu.SemaphoreType.DMA((n_w,)), pltpu.SemaphoreType.DMA((n_w,))],
    )(*grads)


def _add_pairs(a, b, name):
    _, r, c = a.shape
    rb = 32 if r % 32 == 0 else r

    def body(a_ref, b_ref, o_ref, ob_ref):
        s = a_ref[...] + b_ref[...]
        o_ref[...] = s
        ob_ref[...] = s.astype(BF)

    spec = pl.BlockSpec((4, rb, c), lambda i: (0, i, 0))
    return pl.pallas_call(
        body, name=name, grid=(r // rb,),
        out_shape=(jax.ShapeDtypeStruct(a.shape, F32), jax.ShapeDtypeStruct(a.shape, BF)),
        in_specs=[spec, spec], out_specs=(spec, spec), compiler_params=_cparams(("arbitrary",)),
    )(a, b)


def _scatter_phase(phase, bfs, got, send_sems, recv_sems):
    x, y, c, chips = _place()

    def remote(w, j, px, py, rows=slice(None)):
        return pltpu.make_async_remote_copy(src_ref=bfs[w].at[2 * px + py, rows], dst_ref=got[w].at[j, rows],
                                            send_sem=send_sems.at[w, j], recv_sem=recv_sems.at[w, j], device_id=(px, py, c),
                                            device_id_type=MESH)

    for w in range(len(bfs)):
        for j, (px, py) in enumerate(chips):
            if phase == 0:
                for rows in _row_chunks(bfs[w].shape[1], ICI_CHUNK_ROWS):
                    remote(w, j, px, py, rows).start()
            else:
                remote(w, j, px, py).wait()


def _scatter_scratch(n_w):
    return [pltpu.SemaphoreType.DMA((n_w, 3)), pltpu.SemaphoreType.DMA((n_w, 3))]


def _scatter_out_shapes(sums_bf16):
    return tuple(jax.ShapeDtypeStruct((3,) + s.shape[1:], BF) for s in sums_bf16)


def _scatter_to_chips(sums_bf16):
    n_w = len(sums_bf16)

    def body(*refs):
        for phase in range(2):
            _scatter_phase(phase, refs[:n_w], refs[n_w:2 * n_w], *refs[2 * n_w:])

    return pl.pallas_call(
        body, name="scatter_to_chips", out_shape=_scatter_out_shapes(sums_bf16),
        in_specs=[ANY] * n_w, out_specs=(ANY,) * n_w, scratch_shapes=_scatter_scratch(n_w),
    )(*sums_bf16)


def _add_received(own, got, name):
    r, c = own.shape
    rb = 32 if r % 32 == 0 else r

    def body(o_ref, g_ref, out_ref):
        out_ref[...] = ((o_ref[...] + g_ref[0].astype(F32)) + g_ref[1].astype(F32)) + g_ref[2].astype(F32)

    return pl.pallas_call(
        body, name=name, grid=(r // rb,), out_shape=jax.ShapeDtypeStruct((r, c), F32),
        in_specs=[pl.BlockSpec((rb, c), lambda i: (i, 0)), pl.BlockSpec((3, rb, c), lambda i: (0, i, 0))],
        out_specs=pl.BlockSpec((rb, c), lambda i: (i, 0)), compiler_params=_cparams(("arbitrary",)),
    )(own, got)


def _share_with_sibling(halves):
    n_w = len(halves)

    def body(*refs):
        ins, outs = refs[:n_w], refs[n_w:2 * n_w]
        send_sems, recv_sems = refs[2 * n_w:]
        x, y, c, _ = _place()

        def remote(w, rows=slice(None)):
            return pltpu.make_async_remote_copy(src_ref=ins[w].at[rows], dst_ref=outs[w].at[c, rows], send_sem=send_sems.at[w],
                                                recv_sem=recv_sems.at[w], device_id=(x, y, 1 - c), device_id_type=MESH)

        for w in range(n_w):
            for rows in _row_chunks(ins[w].shape[0], D2D_CHUNK_ROWS):
                remote(w, rows).start()
        for w in range(n_w):
            remote(w).wait()

    return pl.pallas_call(
        body, name="share_with_sibling",
        out_shape=tuple(jax.ShapeDtypeStruct((2,) + h.shape, h.dtype) for h in halves),
        in_specs=[ANY] * n_w, out_specs=(ANY,) * n_w,
        scratch_shapes=[pltpu.SemaphoreType.DMA((n_w,)), pltpu.SemaphoreType.DMA((n_w,))],
    )(*halves)


def _all_reduce_small(pack):
    r, c = pack.shape

    def body(p_ref, out_ref, slots, send_sems, recv_sems):
        x, y, cc, _ = _place()
        me = 4 * x + 2 * y + cc
        slots[me] = p_ref[...]
        copies = []
        for k in range(1, 8):
            dx, dy, dc = (k >> 2) & 1, (k >> 1) & 1, k & 1
            to = (1 - x if dx else x, 1 - y if dy else y, 1 - cc if dc else cc)
            cp = pltpu.make_async_remote_copy(src_ref=p_ref, dst_ref=slots.at[me], send_sem=send_sems.at[k - 1],
                                              recv_sem=recv_sems.at[k - 1], device_id=to, device_id_type=MESH)
            cp.start()
            copies.append(cp)
        for cp in copies:
            cp.wait()
        total = slots[0]
        for d in range(1, 8):
            total = total + slots[d]
        out_ref[...] = total

    return pl.pallas_call(
        body, name="all_reduce_small", out_shape=jax.ShapeDtypeStruct((r, c), F32),
        in_specs=[VMEM_SPEC], out_specs=VMEM_SPEC,
        scratch_shapes=[pltpu.VMEM((8, r, c), F32), pltpu.SemaphoreType.DMA((7,)), pltpu.SemaphoreType.DMA((7,))],
    )(pack)


def _adamw(w, g, m, v, name):
    r, c = w.shape
    rb, cb = (64, c) if r % 64 == 0 else (r, LANES if (r % 8 and c % LANES == 0) else c)
    c1 = 1.0 - ADAM_B1 ** ADAM_STEP
    c2 = 1.0 - ADAM_B2 ** ADAM_STEP

    def body(w_ref, g_ref, m_ref, v_ref, d_ref, nm_ref, nv_ref):
        gv = g_ref[...]
        nm = ADAM_B1 * m_ref[...] + (1.0 - ADAM_B1) * gv
        nv = ADAM_B2 * v_ref[...] + (1.0 - ADAM_B2) * (gv * gv)
        nm_ref[...] = nm
        nv_ref[...] = nv
        d_ref[...] = -ADAM_LR * ((nm / c1) / (jnp.sqrt(nv / c2) + ADAM_EPS) + ADAM_WD * w_ref[...])

    spec = pl.BlockSpec((rb, cb), lambda i, j: (i, j))
    return pl.pallas_call(
        body, name=name, grid=(r // rb, c // cb), out_shape=(jax.ShapeDtypeStruct((r, c), F32),) * 3,
        in_specs=[spec] * 4, out_specs=(spec,) * 3, compiler_params=_cparams(("arbitrary", "arbitrary")),
    )(w, g, m, v)


def _rope_tables(t_len):
    inv_freq = ROPE_BASE ** (-jnp.arange(0, HEAD_DIM, 2, dtype=F32) / HEAD_DIM)
    ang = jnp.arange(t_len, dtype=F32)[:, None] * inv_freq[None, :]
    cos, sin = jnp.cos(ang), jnp.sin(ang)
    cos_t = jnp.concatenate([cos, cos, cos, cos], axis=-1)
    sin_t = jnp.concatenate([-sin, sin, -sin, sin], axis=-1)
    return cos_t, sin_t


def _cols_to_shards(dw):
    r, n = dw.shape
    return jnp.transpose(dw.reshape(2, r // 2, 4, n // 4), (2, 0, 1, 3))


def _rows_to_shards(dw):
    r, n = dw.shape
    padded = _pad_rows(dw.reshape(4, r // 4, n))
    return padded.reshape(4, 2, padded.shape[1] // 2, n)


def _pad_lanes(a):
    extra = -a.shape[-1] % LANES
    return a if extra == 0 else jnp.pad(a, [(0, 0)] * (a.ndim - 1) + [(0, extra)])


def _pad_rows(a):
    rows = a.shape[-2]
    extra = 0 if rows % SHARD_ROW_ALIGN == 0 else -rows % SHARD_ROW_PAD
    return a if extra == 0 else jnp.pad(a, [(0, 0)] * (a.ndim - 2) + [(0, extra), (0, 0)])


def _pad_row(a, width=D_MODEL):
    a = a.reshape(1, -1)
    return jnp.pad(a, ((0, 0), (0, width - a.shape[1])))


def kernel(x, mem, g_mix, w_in, b_forget, g_ret_out, g_fox_q, g_fox_k, w_out, g_xattn, w_xq, w_xkv, g_mem, g_xq, g_xk, w_xo, g_ffn, w_gate, w_up, w_down, loss_target, m_g_mix, m_w_in, m_b_forget, m_g_ret_out, m_g_fox_q, m_g_fox_k, m_w_out, m_g_xattn, m_w_xq, m_w_xkv, m_g_mem, m_g_xq, m_g_xk, m_w_xo, m_g_ffn, m_w_gate, m_w_up, m_w_down, v_g_mix, v_w_in, v_b_forget, v_g_ret_out, v_g_fox_q, v_g_fox_k, v_w_out, v_g_xattn, v_w_xq, v_w_xkv, v_g_mem, v_g_xq, v_g_xk, v_w_xo, v_g_ffn, v_w_gate, v_w_up, v_w_down):
    big = {"w_in": (w_in, m_w_in, v_w_in), "w_out": (w_out, m_w_out, v_w_out), "w_xq": (w_xq, m_w_xq, v_w_xq),
           "w_xkv": (w_xkv, m_w_xkv, v_w_xkv), "w_xo": (w_xo, m_w_xo, v_w_xo), "w_gate": (w_gate, m_w_gate, v_w_gate),
           "w_up": (w_up, m_w_up, v_w_up), "w_down": (w_down, m_w_down, v_w_down)}
    for n in TRANSPOSED:
        big[n] = tuple(jnp.swapaxes(a, 1, 2) for a in big[n])
    shards = {}
    for n in big:
        w = _pad_rows(_pad_lanes(big[n][0][0].astype(BF)))
        shards[n] = w.reshape(2, w.shape[0] // 2, w.shape[1])
    sizes = {n: big[n][0].shape[1:] for n in big}
    w_in_full = _assemble_weight("w_in", _all_gather_weights([shards["w_in"]])[0], shards["w_in"], sizes["w_in"])
    small_w ={"g_mix": g_mix, "b_forget": b_forget, "g_ret_out": g_ret_out, "g_fox_q": g_fox_q, "g_fox_k": g_fox_k,
               "g_xattn": g_xattn, "g_mem": g_mem, "g_xq": g_xq, "g_xk": g_xk, "g_ffn": g_ffn}
    m_small = {"g_mix": m_g_mix, "b_forget": m_b_forget, "g_ret_out": m_g_ret_out, "g_fox_q": m_g_fox_q, "g_fox_k": m_g_fox_k,
               "g_xattn": m_g_xattn, "g_mem": m_g_mem, "g_xq": m_g_xq, "g_xk": m_g_xk, "g_ffn": m_g_ffn}
    v_small = {"g_mix": v_g_mix, "b_forget": v_b_forget, "g_ret_out": v_g_ret_out, "g_fox_q": v_g_fox_q, "g_fox_k": v_g_fox_k,
               "g_xattn": v_g_xattn, "g_mem": v_g_mem, "g_xq": v_g_xq, "g_xk": v_g_xk, "g_ffn": v_g_ffn}
    loss_part, grad_x, sums, got, small_g = _local_step(x[0], mem[0], loss_target[0], w_in_full, shards, sizes, small_w)
    return _reduce_and_update(big, sums, got, small_w, small_g, loss_part, grad_x, m_small, v_small)


def _assemble_weight(name, gathered, own, size):
    rows, width = size
    my_chip = 2 * lax.axis_index("x") + lax.axis_index("y")
    g = lax.dynamic_update_slice(gathered, own[None], (my_chip, 0, 0, 0))
    g = g.reshape(4, 2 * g.shape[2], g.shape[3])[:, :rows, :width]
    return jnp.transpose(g, (1, 0, 2)).reshape(rows, 4 * width) if name in COL_SHARDED else g.reshape(4 * rows, width)


def _core_sums(names, dw):
    parts = [_pad_lanes(_cols_to_shards(dw[n]) if n in COL_SHARDED else _rows_to_shards(dw[n])) for n in names]
    my_core = lax.axis_index("c")
    theirs = _exchange_core_halves(parts)
    mine = [lax.dynamic_index_in_dim(p, my_core, axis=1, keepdims=False) for p in parts]
    return [_add_pairs(a, b, f"core_sum_{n}") for n, a, b in zip(names, mine, theirs)]


def _local_step(xs, mems, tgt, w_in_full, shards, sizes, small_w):
    g_mix, b_forget, g_ret_out, g_fox_q, g_fox_k = (small_w[n] for n in ("g_mix", "b_forget", "g_ret_out", "g_fox_q", "g_fox_k"))
    g_xattn, g_mem, g_xq, g_xk, g_ffn = (small_w[n] for n in ("g_xattn", "g_mem", "g_xq", "g_xk", "g_ffn"))
    w_main = w_in_full[:MAIN_W]
    w_ff = jnp.pad(w_in_full[MAIN_W:], ((0, LANES - (IN_W - MAIN_W)), (0, 0)))
    t_len = xs.shape[0]
    cos_t, sin_t = _rope_tables(t_len)
    tables = _decay_tables(min(RET_BLOCK, t_len))
    gq_t = jnp.concatenate([g_fox_q, g_fox_q], axis=-1)
    gk_t = jnp.concatenate([g_fox_k, g_fox_k], axis=-1)
    b_pad = _pad_row(b_forget, LANES)
    g_ret = g_ret_out.reshape(N_HEADS // 2, 1, LANES)

    n1, proj, rq, rk, q_aug, k_aug, z = _in_proj_fwd(xs, g_mix, w_main, w_ff, b_pad, cos_t, sin_t, gq_t, gk_t)
    raw, mix_r, states = _retention_fwd(rq, rk, proj, g_ret, tables)
    mix_f, o32, lse, *gathered = _fox_fwd(q_aug, k_aug, proj, [shards[n] for n in LATE])
    full = {n: _assemble_weight(n, g, shards[n], sizes[n]) for n, g in zip(LATE, gathered)}
    memn, kraw, kn, vmem = _mem_kv_fwd(mems, g_mem, full["w_xkv"], g_xk)
    h1, hn2, qx, o_x, h2 = _attn_out_xattn_fwd(xs, mix_r, mix_f, full["w_out"], g_xattn, full["w_xq"], g_xq, kn, vmem, full["w_xo"])
    hn3, gate, up, act, dh3, loss_part = _ffn_loss_fwd(h2, g_ffn, full["w_gate"], full["w_up"], full["w_down"], tgt)

    dgate, dup, dh2, dg_ffn = _ffn_bwd(dh3, gate, up, h2, g_ffn, full["w_gate"], full["w_up"], full["w_down"])
    dqx, dh1, dmr, dmf, dkn, dvm, dg_xattn, dg_xq = _attn_out_xattn_bwd(dh2, h1, qx, kn, vmem, full["w_xo"], full["w_xq"],
                                                                      full["w_out"], g_xattn, g_xq)
    dw_xkv, dg_mem, dg_xk = _mem_kv_bwd(dkn, dvm, kraw, mems, memn, g_mem, g_xk, full["w_xkv"])
    dw = {
        "w_out": jnp.concatenate([_matmul_tn(mix_r, dh1, "dw_out_ret"), _matmul_tn(mix_f, dh1, "dw_out_fox")], axis=0),
        "w_xq": _matmul_tn(hn2, dqx, "dw_xq"),
        "w_xkv": dw_xkv,
        "w_xo": _matmul_tn(o_x, dh2, "dw_xo"),
        "w_gate": _matmul_tn(dgate, hn3, "dw_gate"),
        "w_up": _matmul_tn(dup, hn3, "dw_up"),
        "w_down": _matmul_tn(act, dh3, "dw_down"),
    }
    late_sums = _core_sums(LATE, dw)
    dq_f, dk_f, dv_f, df, *late_got = _fox_bwd(q_aug, k_aug, proj, dmf, o32, lse, [s[1] for s in late_sums])
    dq_r, dk_r, dv_r, drg, dg_ret = _retention_bwd(dmr, raw, proj, g_ret, rq, rk, states, tables)
    df_col = jnp.pad(jnp.transpose(df, (1, 0, 2)).reshape(t_len, N_HEADS), ((0, 0), (0, LANES - N_HEADS)))
    dproj, dz, grad_x, dg_mix, dg_fq, dg_fk, db = _in_proj_bwd(xs, g_mix, dh1, dq_r, dk_r, dv_r, drg, dq_f, dk_f, dv_f, df_col,
                                                              proj, z, cos_t, sin_t, gq_t, gk_t, w_main, w_ff)

    dw_in = jnp.concatenate([_matmul_tn(dproj, n1, "dw_in_main"), _matmul_tn(dz, n1, "dw_in_ff")[:IN_W - MAIN_W]], axis=0)
    in_sums = _core_sums(("w_in",), {"w_in": dw_in})
    in_got = _scatter_to_chips([in_sums[0][1]])
    sums = {n: s[0] for n, s in zip(("w_in",) + LATE, in_sums + late_sums)}
    got = dict(zip(("w_in",) + LATE, list(in_got) + late_got))
    small_g = {"g_mix": dg_mix, "b_forget": db[:, :N_HEADS], "g_ret_out": dg_ret, "g_fox_q": dg_fq, "g_fox_k": dg_fk,
               "g_xattn": dg_xattn, "g_mem": dg_mem, "g_xq": dg_xq, "g_xk": dg_xk, "g_ffn": dg_ffn}
    return loss_part, grad_x, sums, got, small_g


def _reduce_and_update(big, sums, got, small_w, small_g, loss_part, grad_x, m_small, v_small):
    big_names = list(big)
    my_core = lax.axis_index("c")
    my_chip = 2 * lax.axis_index("x") + lax.axis_index("y")
    own = [lax.dynamic_index_in_dim(sums[n], my_chip, axis=0, keepdims=False) for n in big_names]
    finals = [_add_received(o, got[n], f"chip_sum_{n}") for n, o in zip(big_names, own)]
    shared = _share_with_sibling(finals)
    grads, deltas, new_m, new_v = {}, {}, {}, {}
    for n, s, fin in zip(big_names, shared, finals):
        w, m, v = big[n]
        s = lax.dynamic_update_slice(s, fin[None], (my_core, 0, 0))
        g = s.reshape(2 * s.shape[1], s.shape[2])[:w.shape[1], :w.shape[2]]
        d, nm, nv = _adamw(w[0], g, m[0], v[0], f"adamw_{n}")
        grads[n], deltas[n], new_m[n], new_v[n] = ((jnp.swapaxes(a[None], 1, 2) if n in TRANSPOSED else a[None]) for a in (g, d, nm, nv))

    small_names = list(small_w)
    pad_rows = SMALL_ROWS - len(small_names) - 1
    stack = lambda d: jnp.concatenate([_pad_row(d[n]) for n in small_names] + [jnp.zeros((pad_rows + 1, D_MODEL), F32)], axis=0)
    g_pack = jnp.concatenate([_pad_row(small_g[n]) for n in small_names] + [_pad_row(loss_part[0:1, 0:1])]
                             + [jnp.zeros((pad_rows, D_MODEL), F32)], axis=0)
    g_tot = _all_reduce_small(g_pack)
    d_s, m_s, v_s = _adamw(stack(small_w), g_tot, stack(m_small), stack(v_small), "adamw_small")
    for i, n in enumerate(small_names):
        shape = small_w[n].shape
        size = int(np.prod(shape))
        grads[n] = g_tot[i, :size].reshape(shape)
        deltas[n], new_m[n], new_v[n] = d_s[i, :size].reshape(shape), m_s[i, :size].reshape(shape), v_s[i, :size].reshape(shape)
    loss = g_tot[len(small_names), 0]

    order = ["g_mix", "w_in", "b_forget", "g_ret_out", "g_fox_q", "g_fox_k", "w_out", "g_xattn", "w_xq", "w_xkv", "g_mem", "g_xq",
             "g_xk", "w_xo", "g_ffn", "w_gate", "w_up", "w_down"]
    return (loss, grad_x[None], *[grads[n] for n in order], *[deltas[n] for n in order], *[new_m[n] for n in order],
            *[new_v[n] for n in order])
```

```python
import functools

import numpy as np
import jax
import jax.numpy as jnp
from jax import lax
from jax.experimental import pallas as pl
from jax.experimental.pallas import tpu as pltpu

F32 = jnp.float32
BF = jnp.bfloat16

D_MODEL = 1024
HEAD_DIM = 64
N_HEADS = 8
GROUP_W = 512
N_XH = 4
XHD = 256
D_FF = 2816
MAIN_W = 3584
IN_W = 3592
ROPE_BASE = 10000.0
LOG2E = 1.4426950408889634
LN2 = 0.6931471805599453
EPS = 1e-6
NEG = -1e30
LANES = 128
RET_BLOCK = 256
REF_CHUNK = 64
ROW_TILE = 256
ATT_BLOCK = 256
TN_MAX_ROWS = 1408
SMALL_ROWS = 16
COL_SHARDED = ("w_xkv",)
TRANSPOSED = ("w_in", "w_gate", "w_up")
SHARD_ROW_ALIGN = 32
SHARD_ROW_PAD = 256
LATE = ("w_out", "w_xq", "w_xkv", "w_xo", "w_gate", "w_up", "w_down")
VMEM_LIMIT = 56 * 1024 * 1024

ADAM_LR = 0.001
ADAM_B1 = 0.9
ADAM_B2 = 0.999
ADAM_EPS = 1e-08
ADAM_WD = 0.01
ADAM_STEP = 10

MESH = pl.DeviceIdType.MESH
ANY = pl.BlockSpec(memory_space=pl.ANY)
VMEM_SPEC = pl.BlockSpec(memory_space=pltpu.VMEM)


def _cparams(sem=None, vmem=VMEM_LIMIT):
    return pltpu.CompilerParams(dimension_semantics=sem, vmem_limit_bytes=vmem)


def _dot(a, b):
    return jnp.dot(a.astype(BF), b.astype(BF), preferred_element_type=F32)


def _dot_nt(a, b):
    return lax.dot_general(a.astype(BF), b.astype(BF), (((1,), (1,)), ((), ())), preferred_element_type=F32)


def _dot_tn(a, b):
    return lax.dot_general(a.astype(BF), b.astype(BF), (((0,), (0,)), ((), ())), preferred_element_type=F32)


def _split3(x):
    hi = x.astype(BF)
    r = x - hi.astype(F32)
    mid = r.astype(BF)
    lo = (r - mid.astype(F32)).astype(BF)
    return hi, mid, lo


def _dot_exact(ind, x):
    hi, mid, lo = _split3(x)
    return (jnp.dot(ind, lo, preferred_element_type=F32) + jnp.dot(ind, mid, preferred_element_type=F32)
            + jnp.dot(ind, hi, preferred_element_type=F32))


def _dot_nt_exact(ind, x):
    hi, mid, lo = _split3(x)
    dn = (((1,), (1,)), ((), ()))
    return (lax.dot_general(ind, lo, dn, preferred_element_type=F32) + lax.dot_general(ind, mid, dn, preferred_element_type=F32)
            + lax.dot_general(ind, hi, dn, preferred_element_type=F32))


def _sigmoid(x):
    return 1.0 / (1.0 + jnp.exp(-x))


def _rms_fwd(x, g):
    r = lax.rsqrt(jnp.mean(x * x, axis=-1, keepdims=True) + EPS)
    return x * r * g


def _rms_bwd(x, g, dy):
    r = lax.rsqrt(jnp.mean(x * x, axis=-1, keepdims=True) + EPS)
    xh = x * r
    dg = jnp.sum(dy * xh, axis=0, keepdims=True)
    dxh = dy * g
    dx = r * (dxh - xh * jnp.mean(dxh * xh, axis=-1, keepdims=True))
    return dx, dg


def _group_mean64(x):
    lane = lax.broadcasted_iota(jnp.int32, x.shape, 1)
    lo = lane < HEAD_DIM
    s_lo = jnp.sum(jnp.where(lo, x, 0.0), axis=-1, keepdims=True)
    s_hi = jnp.sum(jnp.where(lo, 0.0, x), axis=-1, keepdims=True)
    return jnp.where(lo, s_lo, s_hi) * (1.0 / HEAD_DIM)


def _swap32(x):
    lane = lax.broadcasted_iota(jnp.int32, x.shape, 1)
    first = (lane % HEAD_DIM) < (HEAD_DIM // 2)
    return jnp.where(first, pltpu.roll(x, LANES - HEAD_DIM // 2, axis=1), pltpu.roll(x, HEAD_DIM // 2, axis=1))


def _chunks(w):
    return [slice(j * LANES, (j + 1) * LANES) for j in range(w // LANES)]


def _aug_pair(qk, f_cols, is_query):
    lane = lax.broadcasted_iota(jnp.int32, qk.shape, 1)
    a = lane - HEAD_DIM
    values = (qk, pltpu.roll(qk, HEAD_DIM, axis=1))
    out = []
    for hh in range(2):
        hi, mid, lo = (p.astype(F32) for p in _split3(f_cols[hh] * LOG2E))
        if is_query:
            aux = jnp.where(a == 0, hi, jnp.where(a == 1, mid, jnp.where(a == 2, lo, jnp.where(a < 6, 1.0, 0.0))))
        else:
            aux = jnp.where(a < 3, 1.0, jnp.where(a == 3, -hi, jnp.where(a == 4, -mid, jnp.where(a == 5, -lo, 0.0))))
        out.append(jnp.where(a < 0, values[hh], aux))
    return jnp.concatenate(out, axis=-1).astype(BF)


def _mem_kv_fwd(mem, g_mem, w_xkv, g_xk):
    m_tok = mem.shape[0]

    def body(mem_ref, gm_ref, w_ref, gk_ref, memn_ref, kraw_ref, kn_ref, v_ref):
        mn = _rms_fwd(mem_ref[...], gm_ref[...]).astype(BF)
        memn_ref[...] = mn
        kv = jnp.dot(mn, w_ref[...], preferred_element_type=F32)
        k = kv[:, :D_MODEL]
        kraw_ref[...] = k
        v_ref[...] = kv[:, D_MODEL:].astype(BF)
        for h in range(N_XH):
            sl = slice(h * XHD, (h + 1) * XHD)
            kn_ref[:, sl] = _rms_fwd(k[:, sl], gk_ref[...]).astype(BF)

    return pl.pallas_call(
        body, name="mem_kv_fwd",
        out_shape=(jax.ShapeDtypeStruct((m_tok, D_MODEL), BF), jax.ShapeDtypeStruct((m_tok, D_MODEL), F32),
                   jax.ShapeDtypeStruct((m_tok, D_MODEL), BF), jax.ShapeDtypeStruct((m_tok, D_MODEL), BF)),
        in_specs=[VMEM_SPEC] * 4, out_specs=(VMEM_SPEC,) * 4, compiler_params=_cparams(),
    )(mem, g_mem, w_xkv, g_xk)


def _in_proj_fwd(x, g_mix, w_in_t, b_pad, cos_t, sin_t, gq_t, gk_t):
    t_len = x.shape[0]
    tm = min(ROW_TILE, t_len)
    n_t = t_len // tm

    def body(x_ref, g_ref, wm_ref, wf_ref, b_ref, cos_ref, sin_ref, gq_ref, gk_ref,
             n1_ref, proj_ref, rq_ref, rk_ref, qa_ref, ka_ref, z_ref, carry):
        i = pl.program_id(0)

        @pl.when(i == 0)
        def _():
            carry[...] = jnp.zeros_like(carry)

        n1 = _rms_fwd(x_ref[...], g_ref[...]).astype(BF)
        n1_ref[...] = n1
        proj = _dot_nt(n1, wm_ref[...])
        proj_ref[...] = proj.astype(BF)
        z = _dot_nt(n1, wf_ref[...]) + b_ref[...]
        z_ref[...] = z
        lane = lax.broadcasted_iota(jnp.int32, z.shape, 1)
        lf = jnp.where(lane < N_HEADS, jnp.minimum(z, 0.0) - jnp.log(1.0 + jnp.exp(-jnp.abs(z))), 0.0)
        row = lax.broadcasted_iota(jnp.int32, (tm, tm), 0)
        col = lax.broadcasted_iota(jnp.int32, (tm, tm), 1)
        tri = (row >= col).astype(BF)
        fc = _dot_exact(tri, lf) + carry[0:1, :]
        carry[...] = jnp.broadcast_to(fc[tm - 1:tm, :], carry.shape)
        c, s = cos_ref[...], sin_ref[...]
        for j, sl in enumerate(_chunks(GROUP_W)):
            q = proj[:, sl]
            rq_ref[:, sl] = ((q * c + _swap32(q) * s) * 0.125).astype(BF)
            k = proj[:, GROUP_W + j * LANES:GROUP_W + (j + 1) * LANES]
            rk_ref[:, sl] = (k * c + _swap32(k) * s).astype(BF)
            f_cols = [fc[:, 2 * j:2 * j + 1], fc[:, 2 * j + 1:2 * j + 2]]
            fq = proj[:, 4 * GROUP_W + j * LANES:4 * GROUP_W + (j + 1) * LANES]
            fq = fq * lax.rsqrt(_group_mean64(fq * fq) + EPS) * gq_ref[...] * (0.125 * LOG2E)
            qa_ref[:, 2 * j * LANES:2 * (j + 1) * LANES] = _aug_pair(fq, f_cols, True)
            fk = proj[:, 5 * GROUP_W + j * LANES:5 * GROUP_W + (j + 1) * LANES]
            fk = fk * lax.rsqrt(_group_mean64(fk * fk) + EPS) * gk_ref[...]
            ka_ref[:, 2 * j * LANES:2 * (j + 1) * LANES] = _aug_pair(fk, f_cols, False)

    row_spec = lambda w: pl.BlockSpec((tm, w), lambda i: (i, 0))
    full = lambda a: pl.BlockSpec(a.shape, lambda i: (0,) * a.ndim)
    return pl.pallas_call(
        body, name="in_proj_fwd", grid=(n_t,),
        out_shape=(jax.ShapeDtypeStruct((t_len, D_MODEL), BF), jax.ShapeDtypeStruct((t_len, MAIN_W), BF),
                   jax.ShapeDtypeStruct((t_len, GROUP_W), BF), jax.ShapeDtypeStruct((t_len, GROUP_W), BF),
                   jax.ShapeDtypeStruct((t_len, 2 * GROUP_W), BF), jax.ShapeDtypeStruct((t_len, 2 * GROUP_W), BF),
                   jax.ShapeDtypeStruct((t_len, LANES), F32)),
        in_specs=[row_spec(D_MODEL), full(g_mix), *_w_in_specs(), full(b_pad), row_spec(LANES), row_spec(LANES),
                  full(gq_t), full(gk_t)],
        out_specs=(row_spec(D_MODEL), row_spec(MAIN_W), row_spec(GROUP_W), row_spec(GROUP_W), row_spec(2 * GROUP_W),
                   row_spec(2 * GROUP_W), row_spec(LANES)),
        scratch_shapes=[pltpu.VMEM((8, LANES), F32)],
        compiler_params=_cparams(("arbitrary",)),
    )(x, g_mix, w_in_t, w_in_t, b_pad, cos_t, sin_t, gq_t, gk_t)


def _w_in_specs():
    return (pl.BlockSpec((MAIN_W, D_MODEL), lambda i: (0, 0)), pl.BlockSpec((LANES, D_MODEL), lambda i: (MAIN_W // LANES, 0)))


def _decay_tables(c):
    h = np.arange(N_HEADS, dtype=np.float64)
    lg = np.log(1.0 - 2.0 ** (-5.0 - h)).astype(np.float32).astype(np.float64)
    t = np.arange(c)
    same_or_earlier = (t[None, :] // REF_CHUNK) <= (t[:, None] // REF_CHUNK)
    w = np.where(same_or_earlier[None], np.exp(lg[:, None, None] * np.abs(t[:, None] - t[None, :])[None]), 0.0)
    qd = np.exp(lg[:, None] * (t[None, :] + 1.0))
    kd = np.exp(lg[:, None] * (c - 1.0 - t[None, :]))
    cd = np.exp(lg * c)
    ones = np.ones((1, 1, HEAD_DIM))
    return (jnp.asarray(w, F32), jnp.asarray(qd[:, :, None] * ones, F32), jnp.asarray(kd[:, :, None] * ones, F32),
            jnp.asarray(cd[:, None, None] * np.ones((1, HEAD_DIM, HEAD_DIM)), F32))


def _retention_fwd(rq, rk, proj, g_ret, tables):
    t_len = rq.shape[0]
    c = min(RET_BLOCK, t_len)
    n_b = t_len // c
    wdec, qdec, kdec, cdec = tables
    v_col, g_col = 2 * GROUP_W // LANES, 3 * GROUP_W // LANES

    def body(q_ref, k_ref, v_ref, rg_ref, g_ref, w_ref, qd_ref, kd_ref, cd_ref, raw_ref, mix_ref, st_ref, state):
        i = pl.program_id(1)

        @pl.when(i == 0)
        def _():
            state[...] = jnp.zeros_like(state)

        q2, k2, v2 = q_ref[...], k_ref[...], v_ref[...]
        outs = []
        for hh in range(2):
            sl = slice(hh * HEAD_DIM, (hh + 1) * HEAD_DIM)
            q, k, v = q2[:, sl], k2[:, sl], v2[:, sl]
            sp = state[hh]
            st_ref[0, 0, hh] = sp
            a = _dot_nt(q, k) * w_ref[hh]
            o = _dot(a, v) + _dot(q.astype(F32) * qd_ref[hh], sp)
            state[hh] = sp * cd_ref[hh] + _dot_tn(k.astype(F32) * kd_ref[hh], v)
            outs.append(o)
        o2 = jnp.concatenate(outs, axis=-1)
        raw_ref[...] = o2
        xc = o2 - _group_mean64(o2)
        xh = xc * lax.rsqrt(_group_mean64(xc * xc) + EPS)
        gate = rg_ref[...].astype(F32)
        mix_ref[...] = (gate * _sigmoid(gate) * (xh * g_ref[0])).astype(BF)

    blk = lambda col0: pl.BlockSpec((c, LANES), lambda hp, i: (i, col0 + hp))
    tab = lambda a: pl.BlockSpec((2,) + a.shape[1:], lambda hp, i: (hp, 0, 0))
    return pl.pallas_call(
        body, name="retention_fwd", grid=(N_HEADS // 2, n_b),
        out_shape=(jax.ShapeDtypeStruct((t_len, GROUP_W), F32), jax.ShapeDtypeStruct((t_len, GROUP_W), BF),
                   jax.ShapeDtypeStruct((N_HEADS // 2, n_b, 2, HEAD_DIM, HEAD_DIM), F32)),
        in_specs=[blk(0), blk(0), blk(v_col), blk(g_col), pl.BlockSpec((1, 1, LANES), lambda hp, i: (hp, 0, 0)),
                  tab(wdec), tab(qdec), tab(kdec), tab(cdec)],
        out_specs=(blk(0), blk(0), pl.BlockSpec((1, 1, 2, HEAD_DIM, HEAD_DIM), lambda hp, i: (hp, i, 0, 0, 0))),
        scratch_shapes=[pltpu.VMEM((2, HEAD_DIM, HEAD_DIM), F32)],
        compiler_params=_cparams(("arbitrary", "arbitrary")),
    )(rq, rk, proj, proj, g_ret, wdec, qdec, kdec, cdec)


def _fox_fwd(q_aug, k_aug, proj, shards):
    t_len = q_aug.shape[0]
    tq = min(ATT_BLOCK, t_len)
    n_q = t_len // tq
    v_col = 6 * GROUP_W // LANES
    tc = min(512, t_len)
    n_w = len(shards)
    n_steps = (N_HEADS // 2) * n_q

    def body(*refs):
        q_ref, k_ref, v_ref = refs[:3]
        o_ref, o32_ref, lse_ref = refs[3 + n_w:6 + n_w]
        vt = refs[6 + 2 * n_w]
        comm = (refs[3:3 + n_w], refs[6 + n_w:6 + 2 * n_w]) + tuple(refs[7 + 2 * n_w:])
        i = pl.program_id(1)
        step = pl.program_id(0) * n_q + i

        @pl.when(step == 0)
        def _():
            _gather_phase(0, *comm)

        @pl.when(step == (3 * n_steps) // 4)
        def _():
            _gather_phase(1, *comm)

        @pl.when(i == 0)
        def _():
            for c0 in range(0, t_len, tc):
                vt[:, c0:c0 + tc] = v_ref[c0:c0 + tc, :].T

        qs = [q_ref[:, hh * LANES:(hh + 1) * LANES] for hh in range(2)]
        ones = jnp.ones((HEAD_DIM, tq), BF)

        def scores(j):
            k2 = k_ref[pl.ds(pl.multiple_of(j * tq, tq), tq), :]
            return tuple(_dot_nt(k2[:, hh * LANES:(hh + 1) * LANES], qs[hh]) for hh in range(2))

        def update(j, ss, carry, masked):
            v2 = vt[:, pl.ds(pl.multiple_of(j * tq, tq), tq)]
            ps, stats = [], []
            for hh in range(2):
                m = carry[hh][0]
                s_t = ss[hh]
                if masked:
                    krow = lax.broadcasted_iota(jnp.int32, (tq, tq), 0)
                    qcol = lax.broadcasted_iota(jnp.int32, (tq, tq), 1)
                    s_t = jnp.where(qcol >= krow, s_t, NEG)
                m_new = jnp.maximum(m, jnp.max(s_t, axis=0, keepdims=True))
                ps.append(jnp.exp2(s_t - m_new).astype(BF))
                stats.append((m_new, jnp.exp2(m - m_new)))
            out = []
            for hh in range(2):
                m_new, alpha = stats[hh]
                v_aug = jnp.concatenate([v2[hh * HEAD_DIM:(hh + 1) * HEAD_DIM, :], ones], axis=0)
                out.append((m_new, carry[hh][1] * alpha + jnp.dot(v_aug, ps[hh], preferred_element_type=F32)))
            return tuple(out)

        def advance(j, state):
            ss, carry = state
            return scores(j + 1), update(j, ss, carry, False)

        init = tuple((jnp.full((1, tq), NEG, F32), jnp.zeros((LANES, tq), F32)) for _ in range(2))
        ss, carry = lax.fori_loop(0, i, advance, (scores(0), init))
        carry = update(i, ss, carry, True)
        outs, lses = [], []
        for hh in range(2):
            m, acc = carry[hh]
            l = acc[HEAD_DIM:HEAD_DIM + 1, :]
            outs.append(acc[:HEAD_DIM, :] / l)
            lses.append(m + jnp.log2(l))
        o2 = jnp.concatenate(outs, axis=0).T
        o32_ref[...] = o2
        o_ref[...] = o2.astype(BF)
        lse_ref[0] = jnp.concatenate(lses, axis=0)

        @pl.when(step == n_steps - 1)
        def _():
            _gather_phase(2, *comm)

    return pl.pallas_call(
        body, name="fox_fwd", grid=(N_HEADS // 2, n_q),
        out_shape=(jax.ShapeDtypeStruct((t_len, GROUP_W), BF), jax.ShapeDtypeStruct((t_len, GROUP_W), F32),
                   jax.ShapeDtypeStruct((N_HEADS // 2, 2, t_len), F32))
        + tuple(jax.ShapeDtypeStruct((4,) + s.shape, s.dtype) for s in shards),
        in_specs=[pl.BlockSpec((tq, 2 * LANES), lambda hp, i: (i, hp)),
                  pl.BlockSpec((t_len, 2 * LANES), lambda hp, i: (0, hp)),
                  pl.BlockSpec((t_len, LANES), lambda hp, i: (0, v_col + hp))] + [ANY] * n_w,
        out_specs=(pl.BlockSpec((tq, LANES), lambda hp, i: (i, hp)), pl.BlockSpec((tq, LANES), lambda hp, i: (i, hp)),
                   pl.BlockSpec((1, 2, tq), lambda hp, i: (hp, 0, i))) + (ANY,) * n_w,
        scratch_shapes=[pltpu.VMEM((LANES, t_len), BF)] + _gather_scratch(n_w),
        compiler_params=_cparams(("arbitrary", "arbitrary")),
    )(q_aug, k_aug, proj, *shards)


def _softmax_rows(s):
    p = jnp.exp(s - jnp.max(s, axis=-1, keepdims=True))
    return p / jnp.sum(p, axis=-1, keepdims=True)


def _attn_out_xattn_fwd(x, mix_r, mix_f, w_out, g_xattn, w_xq, g_xq, kn, v, w_xo):
    t_len = x.shape[0]
    tm = min(ROW_TILE, t_len)

    def body(x_ref, mr_ref, mf_ref, wo_ref, g_ref, wq_ref, gq_ref, kn_ref, v_ref, wxo_ref,
             h1_ref, hn_ref, qx_ref, o_ref, h2_ref):
        h1 = x_ref[...] + jnp.dot(mr_ref[...], wo_ref[:GROUP_W, :], preferred_element_type=F32) \
            + jnp.dot(mf_ref[...], wo_ref[GROUP_W:, :], preferred_element_type=F32)
        h1_ref[...] = h1
        hn = _rms_fwd(h1, g_ref[...]).astype(BF)
        hn_ref[...] = hn
        qx = jnp.dot(hn, wq_ref[...], preferred_element_type=F32).astype(BF)
        qx_ref[...] = qx
        for h in range(N_XH):
            sl = slice(h * XHD, (h + 1) * XHD)
            qn = _rms_fwd(qx[:, sl].astype(F32), gq_ref[...])
            p = _softmax_rows(_dot_nt(qn, kn_ref[:, sl]) * (XHD ** -0.5))
            o_ref[:, sl] = _dot(p, v_ref[:, sl]).astype(BF)
        h2_ref[...] = h1 + jnp.dot(o_ref[...], wxo_ref[...], preferred_element_type=F32)

    row_spec = lambda w: pl.BlockSpec((tm, w), lambda i: (i, 0))
    full = lambda a: pl.BlockSpec(a.shape, lambda i: (0,) * a.ndim)
    return pl.pallas_call(
        body, name="attn_out_xattn_fwd", grid=(t_len // tm,),
        out_shape=(jax.ShapeDtypeStruct((t_len, D_MODEL), F32), jax.ShapeDtypeStruct((t_len, D_MODEL), BF),
                   jax.ShapeDtypeStruct((t_len, D_MODEL), BF), jax.ShapeDtypeStruct((t_len, D_MODEL), BF),
                   jax.ShapeDtypeStruct((t_len, D_MODEL), F32)),
        in_specs=[row_spec(D_MODEL), row_spec(GROUP_W), row_spec(GROUP_W), full(w_out), full(g_xattn), full(w_xq), full(g_xq),
                  full(kn), full(v), full(w_xo)],
        out_specs=(row_spec(D_MODEL),) * 5,
        compiler_params=_cparams(("arbitrary",)),
    )(x, mix_r, mix_f, w_out, g_xattn, w_xq, g_xq, kn, v, w_xo)


def _ffn_loss_fwd(h2, g_ffn, w_gate, w_up, w_down, target):
    t_len = h2.shape[0]
    tm = min(ROW_TILE, t_len)

    def body(h2_ref, g_ref, wg_ref, wu_ref, wd_ref, tgt_ref, hn_ref, gate_ref, up_ref, act_ref, dh3_ref, loss_ref):
        @pl.when(pl.program_id(0) == 0)
        def _():
            loss_ref[...] = jnp.zeros_like(loss_ref)

        h2v = h2_ref[...]
        hn = _rms_fwd(h2v, g_ref[...]).astype(BF)
        hn_ref[...] = hn
        gate = _dot_nt(hn, wg_ref[...])
        up = _dot_nt(hn, wu_ref[...])
        gate_ref[...] = gate.astype(BF)
        up_ref[...] = up.astype(BF)
        act = (gate * _sigmoid(gate) * up).astype(BF)
        act_ref[...] = act
        diff = h2v + jnp.dot(act, wd_ref[...], preferred_element_type=F32) - tgt_ref[...]
        dh3_ref[...] = diff * (1.0 / D_MODEL)
        per_row = jnp.sum(diff * diff, axis=-1, keepdims=True) * (1.0 / D_MODEL)
        loss_ref[...] += 0.5 * jnp.sum(per_row, axis=0, keepdims=True)

    row_spec = lambda w: pl.BlockSpec((tm, w), lambda i: (i, 0))
    full = lambda a: pl.BlockSpec(a.shape, lambda i: (0,) * a.ndim, pipeline_mode=pl.Buffered(1))
    return pl.pallas_call(
        body, name="ffn_loss_fwd", grid=(t_len // tm,),
        out_shape=(jax.ShapeDtypeStruct((t_len, D_MODEL), BF), jax.ShapeDtypeStruct((t_len, D_FF), BF),
                   jax.ShapeDtypeStruct((t_len, D_FF), BF), jax.ShapeDtypeStruct((t_len, D_FF), BF),
                   jax.ShapeDtypeStruct((t_len, D_MODEL), F32), jax.ShapeDtypeStruct((8, LANES), F32)),
        in_specs=[row_spec(D_MODEL), full(g_ffn), full(w_gate), full(w_up), full(w_down), row_spec(D_MODEL)],
        out_specs=(row_spec(D_MODEL), row_spec(D_FF), row_spec(D_FF), row_spec(D_FF), row_spec(D_MODEL),
                   pl.BlockSpec((8, LANES), lambda i: (0, 0))),
        compiler_params=_cparams(("arbitrary",)),
    )(h2, g_ffn, w_gate, w_up, w_down, target)


def _ffn_bwd(dh3, gate, up, h2, g_ffn, w_gate, w_up, w_down):
    t_len = h2.shape[0]
    tm = min(ROW_TILE, t_len)

    def body(dh3_ref, gate_ref, up_ref, h2_ref, g_ref, wg_ref, wu_ref, wd_ref, dgate_ref, dup_ref, dh2_ref, dg_ref):
        @pl.when(pl.program_id(0) == 0)
        def _():
            dg_ref[...] = jnp.zeros_like(dg_ref)

        dh3v = dh3_ref[...]
        dact = _dot_nt(dh3v, wd_ref[...])
        g = gate_ref[...].astype(F32)
        sg = _sigmoid(g)
        dup = (dact * (g * sg)).astype(BF)
        dgate = (dact * up_ref[...].astype(F32) * (sg * (1.0 + g * (1.0 - sg)))).astype(BF)
        dup_ref[...] = dup
        dgate_ref[...] = dgate
        dhn = jnp.dot(dgate, wg_ref[...], preferred_element_type=F32) + jnp.dot(dup, wu_ref[...], preferred_element_type=F32)
        dx, dg = _rms_bwd(h2_ref[...], g_ref[...], dhn)
        dh2_ref[...] = dh3v + dx
        dg_ref[...] += dg

    row_spec = lambda w: pl.BlockSpec((tm, w), lambda i: (i, 0))
    full = lambda a: pl.BlockSpec(a.shape, lambda i: (0,) * a.ndim, pipeline_mode=pl.Buffered(1))
    return pl.pallas_call(
        body, name="ffn_bwd", grid=(t_len // tm,),
        out_shape=(jax.ShapeDtypeStruct((t_len, D_FF), BF), jax.ShapeDtypeStruct((t_len, D_FF), BF),
                   jax.ShapeDtypeStruct((t_len, D_MODEL), F32), jax.ShapeDtypeStruct((1, D_MODEL), F32)),
        in_specs=[row_spec(D_MODEL), row_spec(D_FF), row_spec(D_FF), row_spec(D_MODEL), full(g_ffn), full(w_gate), full(w_up),
                  full(w_down)],
        out_specs=(row_spec(D_FF), row_spec(D_FF), row_spec(D_MODEL), pl.BlockSpec((1, D_MODEL), lambda i: (0, 0))),
        compiler_params=_cparams(("arbitrary",)),
    )(dh3, gate, up, h2, g_ffn, w_gate, w_up, w_down)


def _attn_out_xattn_bwd(dh2, h1, qx, kn, v, w_xo, w_xq, w_out, g_xattn, g_xq):
    t_len = h1.shape[0]
    tm = min(ROW_TILE, t_len)
    m_tok = kn.shape[0]

    def body(dh2_ref, h1_ref, qx_ref, kn_ref, v_ref, wxo_ref, wq_ref, wo_ref, g_ref, gq_ref,
             dqx_ref, dh1_ref, dmr_ref, dmf_ref, dkn_ref, dv_ref, dg_ref, dgq_ref, dqx_scr):
        @pl.when(pl.program_id(0) == 0)
        def _():
            dkn_ref[...] = jnp.zeros_like(dkn_ref)
            dv_ref[...] = jnp.zeros_like(dv_ref)
            dg_ref[...] = jnp.zeros_like(dg_ref)
            dgq_ref[...] = jnp.zeros_like(dgq_ref)

        dh2v = dh2_ref[...]
        do = _dot_nt(dh2v, wxo_ref[...])
        gq = gq_ref[...]
        dgq = jnp.zeros((1, XHD), F32)
        for h in range(N_XH):
            sl = slice(h * XHD, (h + 1) * XHD)
            qraw = qx_ref[:, sl].astype(F32)
            qn = _rms_fwd(qraw, gq)
            p = _softmax_rows(_dot_nt(qn, kn_ref[:, sl]) * (XHD ** -0.5))
            doh = do[:, sl]
            dv_ref[:, sl] += _dot_tn(p, doh)
            dp = _dot_nt(doh, v_ref[:, sl])
            ds = p * (dp - jnp.sum(dp * p, axis=-1, keepdims=True)) * (XHD ** -0.5)
            dqn = _dot(ds, kn_ref[:, sl])
            dkn_ref[:, sl] += _dot_tn(ds, qn)
            dx, dg_h = _rms_bwd(qraw, gq, dqn)
            dgq = dgq + dg_h
            dqx_scr[:, sl] = dx.astype(BF)
        dgq_ref[...] += dgq
        dqx = dqx_scr[...]
        dqx_ref[...] = dqx
        dhn = _dot_nt(dqx, wq_ref[...])
        dx, dg = _rms_bwd(h1_ref[...], g_ref[...], dhn)
        dg_ref[...] += dg
        dh1 = dh2v + dx
        dh1_ref[...] = dh1
        dmix = _dot_nt(dh1, wo_ref[...])
        dmr_ref[...] = dmix[:, :GROUP_W]
        dmf_ref[...] = dmix[:, GROUP_W:].astype(BF)

    row_spec = lambda w: pl.BlockSpec((tm, w), lambda i: (i, 0))
    full = lambda a: pl.BlockSpec(a.shape, lambda i: (0,) * a.ndim)
    acc = lambda r, c: pl.BlockSpec((r, c), lambda i: (0, 0))
    return pl.pallas_call(
        body, name="attn_out_xattn_bwd", grid=(t_len // tm,),
        out_shape=(jax.ShapeDtypeStruct((t_len, D_MODEL), BF), jax.ShapeDtypeStruct((t_len, D_MODEL), F32),
                   jax.ShapeDtypeStruct((t_len, GROUP_W), F32), jax.ShapeDtypeStruct((t_len, GROUP_W), BF),
                   jax.ShapeDtypeStruct((m_tok, D_MODEL), F32), jax.ShapeDtypeStruct((m_tok, D_MODEL), F32),
                   jax.ShapeDtypeStruct((1, D_MODEL), F32), jax.ShapeDtypeStruct((1, XHD), F32)),
        in_specs=[row_spec(D_MODEL), row_spec(D_MODEL), row_spec(D_MODEL), full(kn), full(v), full(w_xo), full(w_xq), full(w_out),
                  full(g_xattn), full(g_xq)],
        out_specs=(row_spec(D_MODEL), row_spec(D_MODEL), row_spec(GROUP_W), row_spec(GROUP_W), acc(m_tok, D_MODEL),
                   acc(m_tok, D_MODEL), acc(1, D_MODEL), acc(1, XHD)),
        scratch_shapes=[pltpu.VMEM((tm, D_MODEL), BF)],
        compiler_params=_cparams(("arbitrary",)),
    )(dh2, h1, qx, kn, v, w_xo, w_xq, w_out, g_xattn, g_xq)


def _mem_kv_bwd(dkn, dv, kraw, mem, memn, g_mem, g_xk, w_xkv):
    m_tok = mem.shape[0]

    def body(dkn_ref, dv_ref, kraw_ref, mem_ref, memn_ref, gm_ref, gk_ref, w_ref, dw_ref, dgm_ref, dgk_ref, dkv_scr):
        gk = gk_ref[...]
        dgk = jnp.zeros((1, XHD), F32)
        for h in range(N_XH):
            sl = slice(h * XHD, (h + 1) * XHD)
            dx, dg_h = _rms_bwd(kraw_ref[:, sl], gk, dkn_ref[:, sl])
            dgk = dgk + dg_h
            dkv_scr[:, sl] = dx.astype(BF)
        dgk_ref[...] = dgk
        dkv_scr[:, D_MODEL:] = dv_ref[...].astype(BF)
        dkv = dkv_scr[...]
        dw_ref[...] = _dot_tn(memn_ref[...], dkv)
        dmemn = _dot_nt(dkv, w_ref[...])
        mem_v = mem_ref[...]
        r = lax.rsqrt(jnp.mean(mem_v * mem_v, axis=-1, keepdims=True) + EPS)
        dgm_ref[...] = jnp.sum(dmemn * mem_v * r, axis=0, keepdims=True)

    return pl.pallas_call(
        body, name="mem_kv_bwd",
        out_shape=(jax.ShapeDtypeStruct((D_MODEL, 2 * D_MODEL), F32), jax.ShapeDtypeStruct((1, D_MODEL), F32),
                   jax.ShapeDtypeStruct((1, XHD), F32)),
        in_specs=[VMEM_SPEC] * 8, out_specs=(VMEM_SPEC,) * 3,
        scratch_shapes=[pltpu.VMEM((m_tok, 2 * D_MODEL), BF)],
        compiler_params=_cparams(),
    )(dkn, dv, kraw, mem, memn, g_mem, g_xk, w_xkv)


def _fox_bwd(q_aug, k_aug, proj, dmf, o32, lse, sums):
    t_len = q_aug.shape[0]
    tb = min(ATT_BLOCK, t_len)
    n_b = t_len // tb
    v_col = 6 * GROUP_W // LANES
    n_w = len(sums)
    n_steps = (N_HEADS // 2) * n_b

    def body(*refs):
        k_ref, v_ref, q_ref, do_ref, o_ref, lse_ref = refs[:6]
        dq_ref, dk_ref, dv_ref, df_ref = refs[6 + n_w:10 + n_w]
        delta = refs[10 + 2 * n_w]
        comm = (refs[6:6 + n_w], refs[10 + n_w:10 + 2 * n_w]) + tuple(refs[11 + 2 * n_w:])
        j = pl.program_id(1)
        step = pl.program_id(0) * n_b + j

        @pl.when(step == 0)
        def _():
            _scatter_phase(0, *comm)

        @pl.when(j == 0)
        def _():
            dq_ref[...] = jnp.zeros_like(dq_ref)
            dd = do_ref[...].astype(F32) * o_ref[...]
            hrow = lax.broadcasted_iota(jnp.int32, (8, LANES), 0)
            lane = lax.broadcasted_iota(jnp.int32, (8, LANES), 1)
            ind = ((lane // HEAD_DIM) == hrow).astype(BF)
            delta[...] = _dot_nt_exact(ind, dd)

        k2, v2 = k_ref[...], v_ref[...]
        ks = [k2[:, hh * LANES:(hh + 1) * LANES] for hh in range(2)]
        vs = [v2[:, hh * HEAD_DIM:(hh + 1) * HEAD_DIM] for hh in range(2)]

        def block(i, carry, masked):
            rows = pl.ds(pl.multiple_of(i * tb, tb), tb)
            q2 = q_ref[rows, :]
            do2 = do_ref[rows, :]
            qs = [q2[:, hh * LANES:(hh + 1) * LANES] for hh in range(2)]
            dos = [do2[:, hh * HEAD_DIM:(hh + 1) * HEAD_DIM] for hh in range(2)]
            ss = [_dot_nt(ks[hh], qs[hh]) for hh in range(2)]
            dps = [_dot_nt(vs[hh], dos[hh]) for hh in range(2)]
            pts, dsts, dfs = [], [], []
            for hh in range(2):
                s_t = ss[hh]
                if masked:
                    krow = lax.broadcasted_iota(jnp.int32, (tb, tb), 0)
                    qcol = lax.broadcasted_iota(jnp.int32, (tb, tb), 1)
                    s_t = jnp.where(qcol >= krow, s_t, NEG)
                p_t = jnp.exp2(s_t - lse_ref[0, hh:hh + 1, rows])
                pts.append(p_t.astype(BF))
                ds_t = p_t * (dps[hh] - delta[hh:hh + 1, rows])
                dsts.append(ds_t.astype(BF))
                dfs.append(jnp.sum(ds_t, axis=-1, keepdims=True))
            out = []
            for hh in range(2):
                dk, dv, df = carry[hh]
                dv = dv + jnp.dot(pts[hh], dos[hh], preferred_element_type=F32)
                dk = dk + jnp.dot(dsts[hh], qs[hh], preferred_element_type=F32)
                dq_ref[rows, hh * HEAD_DIM:(hh + 1) * HEAD_DIM] += _dot_tn(dsts[hh], ks[hh])[:, :HEAD_DIM]
                out.append((dk, dv, df - dfs[hh]))
            return tuple(out)

        init = tuple((jnp.zeros((tb, LANES), F32), jnp.zeros((tb, HEAD_DIM), F32), jnp.zeros((tb, 1), F32)) for _ in range(2))
        carry = block(j, init, True)
        carry = lax.fori_loop(j + 1, n_b, lambda i, c: block(i, c, False), carry)
        dk_ref[...] = jnp.concatenate([carry[hh][0][:, :HEAD_DIM] for hh in range(2)], axis=-1) * LN2
        dv_ref[...] = jnp.concatenate([carry[hh][1] for hh in range(2)], axis=-1)
        df_ref[0] = jnp.concatenate([carry[hh][2] for hh in range(2)], axis=-1)

        @pl.when(step == n_steps - 1)
        def _():
            _scatter_phase(1, *comm)

    blk = lambda w, col0: pl.BlockSpec((tb, w), lambda hp, j: (j, col0 + hp))
    whole = lambda w: pl.BlockSpec((t_len, w), lambda hp, j: (0, hp))
    rows2 = pl.BlockSpec((1, 2, t_len), lambda hp, j: (hp, 0, 0))
    cols2 = pl.BlockSpec((1, tb, 2), lambda hp, j: (hp, j, 0))
    return pl.pallas_call(
        body, name="fox_bwd", grid=(N_HEADS // 2, n_b),
        out_shape=(jax.ShapeDtypeStruct((t_len, GROUP_W), F32), jax.ShapeDtypeStruct((t_len, GROUP_W), F32),
                   jax.ShapeDtypeStruct((t_len, GROUP_W), F32), jax.ShapeDtypeStruct((N_HEADS // 2, t_len, 2), F32))
        + _scatter_out_shapes(sums),
        in_specs=[blk(2 * LANES, 0), blk(LANES, v_col), whole(2 * LANES), whole(LANES), whole(LANES), rows2] + [ANY] * n_w,
        out_specs=(whole(LANES), blk(LANES, 0), blk(LANES, 0), cols2) + (ANY,) * n_w,
        scratch_shapes=[pltpu.VMEM((8, t_len), F32)] + _scatter_scratch(n_w),
        compiler_params=_cparams(("arbitrary", "arbitrary")),
    )(k_aug, proj, q_aug, dmf, o32, lse, *sums)


def _retention_bwd(dmr, raw, proj, g_ret, rq, rk, states, tables):
    t_len = rq.shape[0]
    c = min(RET_BLOCK, t_len)
    n_b = t_len // c
    wdec, qdec, kdec, cdec = tables
    v_col, g_col = 2 * GROUP_W // LANES, 3 * GROUP_W // LANES

    def body(d_ref, raw_ref, rg_ref, g_ref, q_ref, k_ref, v_ref, st_ref, w_ref, qd_ref, kd_ref, cd_ref,
             dq_ref, dk_ref, dv_ref, drg_ref, dg_ref, gstate):
        @pl.when(pl.program_id(1) == 0)
        def _():
            gstate[...] = jnp.zeros_like(gstate)
            dg_ref[...] = jnp.zeros_like(dg_ref)

        d, raw_v, g = d_ref[...], raw_ref[...], g_ref[0]
        gate = rg_ref[...].astype(F32)
        xc = raw_v - _group_mean64(raw_v)
        r = lax.rsqrt(_group_mean64(xc * xc) + EPS)
        xh = xc * r
        sg = _sigmoid(gate)
        drg_ref[...] = d * (xh * g) * (sg * (1.0 + gate * (1.0 - sg)))
        dy = d * (gate * sg)
        dg_ref[0] += jnp.sum(dy * xh, axis=0, keepdims=True)
        dxh = dy * g
        do2 = r * (dxh - _group_mean64(dxh) - xh * _group_mean64(dxh * xh))
        q2, k2, v2 = q_ref[...], k_ref[...], v_ref[...]
        dqs, dks, dvs = [], [], []
        for hh in range(2):
            sl = slice(hh * HEAD_DIM, (hh + 1) * HEAD_DIM)
            q, k, v, do = q2[:, sl], k2[:, sl], v2[:, sl], do2[:, sl].astype(BF)
            w = w_ref[hh]
            a = _dot_nt(q, k) * w
            dm = _dot_nt(do, v) * w
            sp, gs = st_ref[0, 0, hh], gstate[hh]
            qd = q.astype(F32) * qd_ref[hh]
            kd = k.astype(F32) * kd_ref[hh]
            dqs.append(_dot(dm, k) + _dot_nt(do, sp) * qd_ref[hh])
            dks.append(_dot_tn(dm, q) + _dot_nt(v, gs) * kd_ref[hh])
            dvs.append(_dot_tn(a, do) + _dot(kd, gs))
            gstate[hh] = gs * cd_ref[hh] + _dot_tn(qd, do)
        dq_ref[...] = jnp.concatenate(dqs, axis=-1)
        dk_ref[...] = jnp.concatenate(dks, axis=-1)
        dv_ref[...] = jnp.concatenate(dvs, axis=-1)

    blk = lambda col0: pl.BlockSpec((c, LANES), lambda hp, i: (n_b - 1 - i, col0 + hp))
    tab = lambda a: pl.BlockSpec((2,) + a.shape[1:], lambda hp, i: (hp, 0, 0))
    gspec = pl.BlockSpec((1, 1, LANES), lambda hp, i: (hp, 0, 0))
    return pl.pallas_call(
        body, name="retention_bwd", grid=(N_HEADS // 2, n_b),
        out_shape=(jax.ShapeDtypeStruct((t_len, GROUP_W), F32),) * 4 + (jax.ShapeDtypeStruct((N_HEADS // 2, 1, LANES), F32),),
        in_specs=[blk(0), blk(0), blk(g_col), gspec, blk(0), blk(0), blk(v_col),
                  pl.BlockSpec((1, 1, 2, HEAD_DIM, HEAD_DIM), lambda hp, i: (hp, n_b - 1 - i, 0, 0, 0)),
                  tab(wdec), tab(qdec), tab(kdec), tab(cdec)],
        out_specs=(blk(0), blk(0), blk(0), blk(0), gspec),
        scratch_shapes=[pltpu.VMEM((2, HEAD_DIM, HEAD_DIM), F32)],
        compiler_params=_cparams(("arbitrary", "arbitrary")),
    )(dmr, raw, proj, g_ret, rq, rk, proj, states, wdec, qdec, kdec, cdec)


def _in_proj_bwd(x, g_mix, dh1, dq_r, dk_r, dv_r, drg, dq_f, dk_f, dv_f, df_col, proj, z, cos_t, sin_t, gq_t, gk_t, w_in_t):
    t_len = x.shape[0]
    tm = min(ROW_TILE, t_len)
    n_t = t_len // tm

    def body(x_ref, g_ref, dh1_ref, dqr_ref, dkr_ref, dvr_ref, drg_ref, dqf_ref, dkf_ref, dvf_ref, df_ref, fq_ref, fk_ref, z_ref,
             cos_ref, sin_ref, gq_ref, gk_ref, wm_ref, wf_ref,
             dproj_ref, dz_ref, dx_ref, dg_ref, dgq_ref, dgk_ref, db_ref, carry, gq_acc, gk_acc):
        i = pl.program_id(0)

        @pl.when(i == 0)
        def _():
            carry[...] = jnp.zeros_like(carry)
            gq_acc[...] = jnp.zeros_like(gq_acc)
            gk_acc[...] = jnp.zeros_like(gk_acc)
            dg_ref[...] = jnp.zeros_like(dg_ref)
            db_ref[...] = jnp.zeros_like(db_ref)

        c, s = cos_ref[...], sin_ref[...]
        gq, gk = gq_ref[...], gk_ref[...]
        dgq = jnp.zeros((1, LANES), F32)
        dgk = jnp.zeros((1, LANES), F32)
        for sl in _chunks(GROUP_W):
            dy = dqr_ref[:, sl] * 0.125
            dproj_ref[:, sl] = (dy * c + _swap32(dy * s)).astype(BF)
            dy = dkr_ref[:, sl]
            dproj_ref[:, GROUP_W + sl.start:GROUP_W + sl.stop] = (dy * c + _swap32(dy * s)).astype(BF)
            dproj_ref[:, 2 * GROUP_W + sl.start:2 * GROUP_W + sl.stop] = dvr_ref[:, sl].astype(BF)
            dproj_ref[:, 3 * GROUP_W + sl.start:3 * GROUP_W + sl.stop] = drg_ref[:, sl].astype(BF)
            for src, dsrc, gain, off in ((fq_ref, dqf_ref, gq, 4), (fk_ref, dkf_ref, gk, 5)):
                xr = src[:, sl].astype(F32)
                r = lax.rsqrt(_group_mean64(xr * xr) + EPS)
                xh = xr * r
                dy = dsrc[:, sl] * (0.125 if off == 4 else 1.0)
                dgs = jnp.sum(dy * xh, axis=0, keepdims=True)
                if off == 4:
                    dgq = dgq + dgs
                else:
                    dgk = dgk + dgs
                dxh = dy * gain
                dproj_ref[:, off * GROUP_W + sl.start:off * GROUP_W + sl.stop] = \
                    (r * (dxh - xh * _group_mean64(dxh * xh))).astype(BF)
            dproj_ref[:, 6 * GROUP_W + sl.start:6 * GROUP_W + sl.stop] = dvf_ref[:, sl].astype(BF)
        gq_acc[...] += dgq
        gk_acc[...] += dgk
        row = lax.broadcasted_iota(jnp.int32, (tm, tm), 0)
        col = lax.broadcasted_iota(jnp.int32, (tm, tm), 1)
        dlf = _dot_exact((col >= row).astype(BF), df_ref[...]) + carry[0:1, :]
        carry[...] = jnp.broadcast_to(dlf[0:1, :], carry.shape)
        lane = lax.broadcasted_iota(jnp.int32, (tm, LANES), 1)
        dz = jnp.where(lane < N_HEADS, dlf / (1.0 + jnp.exp(z_ref[...])), 0.0)
        db_ref[...] += jnp.sum(dz, axis=0, keepdims=True)
        dz_bf = dz.astype(BF)
        dz_ref[...] = dz_bf
        dn1 = jnp.dot(dz_bf, wf_ref[...], preferred_element_type=F32)
        for sec in range(MAIN_W // GROUP_W):
            sl = slice(sec * GROUP_W, (sec + 1) * GROUP_W)
            dn1 = dn1 + jnp.dot(dproj_ref[:, sl], wm_ref[sl, :], preferred_element_type=F32)
        dx, dg = _rms_bwd(x_ref[...], g_ref[...], dn1)
        dx_ref[...] = dh1_ref[...] + dx
        dg_ref[...] += dg

        @pl.when(i == n_t - 1)
        def _():
            dgq_ref[...] = gq_acc[:, :HEAD_DIM] + gq_acc[:, HEAD_DIM:]
            dgk_ref[...] = gk_acc[:, :HEAD_DIM] + gk_acc[:, HEAD_DIM:]

    row_spec = lambda w, col=0: pl.BlockSpec((tm, w), lambda i: (n_t - 1 - i, col))
    full = lambda a: pl.BlockSpec(a.shape, lambda i: (0,) * a.ndim)
    acc = lambda r, c: pl.BlockSpec((r, c), lambda i: (0, 0))
    return pl.pallas_call(
        body, name="in_proj_bwd", grid=(n_t,),
        out_shape=(jax.ShapeDtypeStruct((t_len, MAIN_W), BF), jax.ShapeDtypeStruct((t_len, LANES), BF),
                   jax.ShapeDtypeStruct((t_len, D_MODEL), F32), jax.ShapeDtypeStruct((1, D_MODEL), F32),
                   jax.ShapeDtypeStruct((1, HEAD_DIM), F32), jax.ShapeDtypeStruct((1, HEAD_DIM), F32),
                   jax.ShapeDtypeStruct((1, LANES), F32)),
        in_specs=[row_spec(D_MODEL), full(g_mix), row_spec(D_MODEL)] + [row_spec(GROUP_W)] * 7
        + [row_spec(LANES), row_spec(GROUP_W, 4), row_spec(GROUP_W, 5), row_spec(LANES), row_spec(LANES), row_spec(LANES),
           full(gq_t), full(gk_t), *_w_in_specs()],
        out_specs=(row_spec(MAIN_W), row_spec(LANES), row_spec(D_MODEL), acc(1, D_MODEL), acc(1, HEAD_DIM), acc(1, HEAD_DIM),
                   acc(1, LANES)),
        scratch_shapes=[pltpu.VMEM((8, LANES), F32), pltpu.VMEM((1, LANES), F32), pltpu.VMEM((1, LANES), F32)],
        compiler_params=_cparams(("arbitrary",)),
    )(x, g_mix, dh1, dq_r, dk_r, dv_r, drg, dq_f, dk_f, dv_f, df_col, proj, proj, z, cos_t, sin_t, gq_t, gk_t, w_in_t, w_in_t)


def _matmul_tn(a, b, name, bk=512):
    t_len, m = a.shape
    n = b.shape[1]
    bm = m if m <= TN_MAX_ROWS else m // 2
    bk = min(bk, t_len)

    def body(a_ref, b_ref, o_ref):
        @pl.when(pl.program_id(1) == 0)
        def _():
            o_ref[...] = jnp.zeros_like(o_ref)

        o_ref[...] += _dot_tn(a_ref[...], b_ref[...])

    return pl.pallas_call(
        body, name=name, grid=(m // bm, t_len // bk),
        out_shape=jax.ShapeDtypeStruct((m, n), F32),
        in_specs=[pl.BlockSpec((bk, bm), lambda i, k: (k, i)), pl.BlockSpec((bk, n), lambda i, k: (k, 0))],
        out_specs=pl.BlockSpec((bm, n), lambda i, k: (i, 0)),
        compiler_params=_cparams(("arbitrary", "arbitrary")),
    )(a, b)


def _place():
    x, y, c = lax.axis_index("x"), lax.axis_index("y"), lax.axis_index("c")
    chips = [(1 - x, y), (x, 1 - y), (1 - x, 1 - y)]
    return x, y, c, chips


def _row_chunks(rows, limit):
    step = max(d for d in range(16, min(rows, limit) + 1, 16) if rows % d == 0)
    return [slice(i, i + step) for i in range(0, rows, step)]


ICI_CHUNK_ROWS = 128
D2D_CHUNK_ROWS = 64


def _gather_phase(phase, ins, outs, send_sems, recv_sems):
    x, y, c, chips = _place()
    me_chip = 2 * x + y
    sibling = (x, y, 1 - c)

    def copy(w, k, slot, half, to, rows=slice(None), src=None):
        dst = outs[w].at[slot, half, rows]
        return pltpu.make_async_remote_copy(src_ref=dst if src is None else src, dst_ref=dst,
                                            send_sem=send_sems.at[w, k], recv_sem=recv_sems.at[w, k],
                                            device_id=to, device_id_type=MESH)

    for w in range(len(ins)):
        for j, (px, py) in enumerate(chips):
            if phase == 0:
                for rows in _row_chunks(ins[w].shape[1], ICI_CHUNK_ROWS):
                    copy(w, j, me_chip, c, (px, py, c), rows, src=ins[w].at[c, rows]).start()
            elif phase == 1:
                copy(w, j, 2 * px + py, c, (x, y, c)).wait_recv()
                for rows in _row_chunks(ins[w].shape[1], D2D_CHUNK_ROWS):
                    copy(w, 3 + j, 2 * px + py, c, sibling, rows).start()
            else:
                copy(w, 3 + j, 2 * px + py, 1 - c, (x, y, c)).wait_recv()
                copy(w, j, me_chip, c, (px, py, c), src=ins[w].at[c]).wait_send()
                copy(w, 3 + j, 2 * px + py, c, sibling).wait_send()


def _gather_scratch(n_w):
    return [pltpu.SemaphoreType.DMA((n_w, 6)), pltpu.SemaphoreType.DMA((n_w, 6))]


def _all_gather_weights(shards):
    n_w = len(shards)

    def body(*refs):
        for phase in range(3):
            _gather_phase(phase, refs[:n_w], refs[n_w:2 * n_w], *refs[2 * n_w:])

    return pl.pallas_call(
        body, name="all_gather_weights",
        out_shape=tuple(jax.ShapeDtypeStruct((4,) + s.shape, s.dtype) for s in shards),
        in_specs=[ANY] * n_w, out_specs=(ANY,) * n_w, scratch_shapes=_gather_scratch(n_w),
    )(*shards)


def _exchange_core_halves(grads):
    n_w = len(grads)

    def body(*refs):
        ins, theirs = refs[:n_w], refs[n_w:2 * n_w]
        send_sems, recv_sems = refs[2 * n_w:]
        x, y, c, _ = _place()

        def remote(w, k=slice(None), rows=slice(None)):
            return pltpu.make_async_remote_copy(src_ref=ins[w].at[k, 1 - c, rows], dst_ref=theirs[w].at[k, rows],
                                                send_sem=send_sems.at[w], recv_sem=recv_sems.at[w], device_id=(x, y, 1 - c),
                                                device_id_type=MESH)

        for w in range(n_w):
            for k in range(4):
                for rows in _row_chunks(ins[w].shape[2], D2D_CHUNK_ROWS):
                    remote(w, k, rows).start()
        for w in range(n_w):
            remote(w).wait()

    half = tuple(jax.ShapeDtypeStruct((4,) + g.shape[2:], g.dtype) for g in grads)
    return pl.pallas_call(
        body, name="exchange_core_halves", out_shape=half,
        in_specs=[ANY] * n_w, out_specs=(ANY,) * n_w,
        scratch_shapes=[pltpu.SemaphoreType.DMA((n_w,)), pltpu.SemaphoreType.DMA((n_w,))],
    )(*grads)


def _add_pairs(part, theirs, name):
    _, _, r, c = part.shape
    rb = 32 if r % 32 == 0 else r

    def body(a_ref, b_ref, own_ref, ob_ref):
        my_chip = 2 * lax.axis_index("x") + lax.axis_index("y")
        ob_ref[...] = (a_ref[...] + b_ref[...]).astype(BF)
        own_ref[...] = a_ref[my_chip] + b_ref[my_chip]

    spec = pl.BlockSpec((4, rb, c), lambda i: (0, i, 0))
    return pl.pallas_call(
        body, name=name, grid=(r // rb,),
        out_shape=(jax.ShapeDtypeStruct((r, c), F32), jax.ShapeDtypeStruct((4, r, c), BF)),
        in_specs=[pl.BlockSpec((4, None, rb, c), lambda i: (0, lax.axis_index("c"), i, 0)), spec],
        out_specs=(pl.BlockSpec((rb, c), lambda i: (i, 0)), spec), compiler_params=_cparams(("arbitrary",)),
    )(part, theirs)


def _scatter_phase(phase, bfs, got, send_sems, recv_sems):
    x, y, c, chips = _place()

    def remote(w, j, px, py, rows=slice(None)):
        return pltpu.make_async_remote_copy(src_ref=bfs[w].at[2 * px + py, rows], dst_ref=got[w].at[j, rows],
                                            send_sem=send_sems.at[w, j], recv_sem=recv_sems.at[w, j], device_id=(px, py, c),
                                            device_id_type=MESH)

    for w in range(len(bfs)):
        for j, (px, py) in enumerate(chips):
            if phase == 0:
                for rows in _row_chunks(bfs[w].shape[1], ICI_CHUNK_ROWS):
                    remote(w, j, px, py, rows).start()
            else:
                remote(w, j, px, py).wait()


def _scatter_scratch(n_w):
    return [pltpu.SemaphoreType.DMA((n_w, 3)), pltpu.SemaphoreType.DMA((n_w, 3))]


def _scatter_out_shapes(sums_bf16):
    return tuple(jax.ShapeDtypeStruct((3,) + s.shape[1:], BF) for s in sums_bf16)


def _scatter_to_chips(sums_bf16):
    n_w = len(sums_bf16)

    def body(*refs):
        for phase in range(2):
            _scatter_phase(phase, refs[:n_w], refs[n_w:2 * n_w], *refs[2 * n_w:])

    return pl.pallas_call(
        body, name="scatter_to_chips", out_shape=_scatter_out_shapes(sums_bf16),
        in_specs=[ANY] * n_w, out_specs=(ANY,) * n_w, scratch_shapes=_scatter_scratch(n_w),
    )(*sums_bf16)


def _add_received(own, got, name):
    r, c = own.shape
    rb = 32 if r % 32 == 0 else r

    def body(o_ref, g_ref, out_ref):
        out_ref[...] = ((o_ref[...] + g_ref[0].astype(F32)) + g_ref[1].astype(F32)) + g_ref[2].astype(F32)

    return pl.pallas_call(
        body, name=name, grid=(r // rb,), out_shape=jax.ShapeDtypeStruct((r, c), F32),
        in_specs=[pl.BlockSpec((rb, c), lambda i: (i, 0)), pl.BlockSpec((3, rb, c), lambda i: (0, i, 0))],
        out_specs=pl.BlockSpec((rb, c), lambda i: (i, 0)), compiler_params=_cparams(("arbitrary",)),
    )(own, got)


def _share_with_sibling(halves):
    n_w = len(halves)

    def body(*refs):
        ins, outs = refs[:n_w], refs[n_w:2 * n_w]
        send_sems, recv_sems = refs[2 * n_w:]
        x, y, c, _ = _place()

        def remote(w, rows=slice(None)):
            return pltpu.make_async_remote_copy(src_ref=ins[w].at[rows], dst_ref=outs[w].at[c, rows], send_sem=send_sems.at[w],
                                                recv_sem=recv_sems.at[w], device_id=(x, y, 1 - c), device_id_type=MESH)

        for w in range(n_w):
            for rows in _row_chunks(ins[w].shape[0], D2D_CHUNK_ROWS):
                remote(w, rows).start()
        for w in range(n_w):
            remote(w).wait()

    return pl.pallas_call(
        body, name="share_with_sibling",
        out_shape=tuple(jax.ShapeDtypeStruct((2,) + h.shape, h.dtype) for h in halves),
        in_specs=[ANY] * n_w, out_specs=(ANY,) * n_w,
        scratch_shapes=[pltpu.SemaphoreType.DMA((n_w,)), pltpu.SemaphoreType.DMA((n_w,))],
    )(*halves)


def _all_reduce_small(pack):
    r, c = pack.shape

    def body(p_ref, out_ref, slots, send_sems, recv_sems):
        x, y, cc, _ = _place()
        me = 4 * x + 2 * y + cc
        slots[me] = p_ref[...]
        copies = []
        for k in range(1, 8):
            dx, dy, dc = (k >> 2) & 1, (k >> 1) & 1, k & 1
            to = (1 - x if dx else x, 1 - y if dy else y, 1 - cc if dc else cc)
            cp = pltpu.make_async_remote_copy(src_ref=p_ref, dst_ref=slots.at[me], send_sem=send_sems.at[k - 1],
                                              recv_sem=recv_sems.at[k - 1], device_id=to, device_id_type=MESH)
            cp.start()
            copies.append(cp)
        for cp in copies:
            cp.wait()
        total = slots[0]
        for d in range(1, 8):
            total = total + slots[d]
        out_ref[...] = total

    return pl.pallas_call(
        body, name="all_reduce_small", out_shape=jax.ShapeDtypeStruct((r, c), F32),
        in_specs=[VMEM_SPEC], out_specs=VMEM_SPEC,
        scratch_shapes=[pltpu.VMEM((8, r, c), F32), pltpu.SemaphoreType.DMA((7,)), pltpu.SemaphoreType.DMA((7,))],
    )(pack)


def _adamw(w, g, m, v, name):
    r, c = w.shape
    rb, cb = (64, c) if r % 64 == 0 else (r, LANES if (r % 8 and c % LANES == 0) else c)
    c1 = 1.0 - ADAM_B1 ** ADAM_STEP
    c2 = 1.0 - ADAM_B2 ** ADAM_STEP

    def body(w_ref, g_ref, m_ref, v_ref, d_ref, nm_ref, nv_ref):
        gv = g_ref[...]
        nm = ADAM_B1 * m_ref[...] + (1.0 - ADAM_B1) * gv
        nv = ADAM_B2 * v_ref[...] + (1.0 - ADAM_B2) * (gv * gv)
        nm_ref[...] = nm
        nv_ref[...] = nv
        d_ref[...] = -ADAM_LR * ((nm / c1) / (jnp.sqrt(nv / c2) + ADAM_EPS) + ADAM_WD * w_ref[...])

    spec = pl.BlockSpec((rb, cb), lambda i, j: (i, j))
    return pl.pallas_call(
        body, name=name, grid=(r // rb, c // cb), out_shape=(jax.ShapeDtypeStruct((r, c), F32),) * 3,
        in_specs=[spec] * 4, out_specs=(spec,) * 3, compiler_params=_cparams(("arbitrary", "arbitrary")),
    )(w, g, m, v)


def _rope_tables(t_len):
    inv_freq = ROPE_BASE ** (-jnp.arange(0, HEAD_DIM, 2, dtype=F32) / HEAD_DIM)
    ang = jnp.arange(t_len, dtype=F32)[:, None] * inv_freq[None, :]
    cos, sin = jnp.cos(ang), jnp.sin(ang)
    cos_t = jnp.concatenate([cos, cos, cos, cos], axis=-1)
    sin_t = jnp.concatenate([-sin, sin, -sin, sin], axis=-1)
    return cos_t, sin_t


def _cols_to_shards(dw):
    r, n = dw.shape
    return jnp.transpose(dw.reshape(2, r // 2, 4, n // 4), (2, 0, 1, 3))


def _rows_to_shards(dw):
    r, n = dw.shape
    padded = _pad_rows(dw.reshape(4, r // 4, n))
    return padded.reshape(4, 2, padded.shape[1] // 2, n)


def _pad_lanes(a):
    extra = -a.shape[-1] % LANES
    return a if extra == 0 else jnp.pad(a, [(0, 0)] * (a.ndim - 1) + [(0, extra)])


def _pad_rows(a):
    rows = a.shape[-2]
    extra = 0 if rows % SHARD_ROW_ALIGN == 0 else -rows % SHARD_ROW_PAD
    return a if extra == 0 else jnp.pad(a, [(0, 0)] * (a.ndim - 2) + [(0, extra), (0, 0)])


def _pad_row(a, width=D_MODEL):
    a = a.reshape(1, -1)
    return jnp.pad(a, ((0, 0), (0, width - a.shape[1])))


def kernel(x, mem, g_mix, w_in, b_forget, g_ret_out, g_fox_q, g_fox_k, w_out, g_xattn, w_xq, w_xkv, g_mem, g_xq, g_xk, w_xo, g_ffn, w_gate, w_up, w_down, loss_target, m_g_mix, m_w_in, m_b_forget, m_g_ret_out, m_g_fox_q, m_g_fox_k, m_w_out, m_g_xattn, m_w_xq, m_w_xkv, m_g_mem, m_g_xq, m_g_xk, m_w_xo, m_g_ffn, m_w_gate, m_w_up, m_w_down, v_g_mix, v_w_in, v_b_forget, v_g_ret_out, v_g_fox_q, v_g_fox_k, v_w_out, v_g_xattn, v_w_xq, v_w_xkv, v_g_mem, v_g_xq, v_g_xk, v_w_xo, v_g_ffn, v_w_gate, v_w_up, v_w_down):
    big = {"w_in": (w_in, m_w_in, v_w_in), "w_out": (w_out, m_w_out, v_w_out), "w_xq": (w_xq, m_w_xq, v_w_xq),
           "w_xkv": (w_xkv, m_w_xkv, v_w_xkv), "w_xo": (w_xo, m_w_xo, v_w_xo), "w_gate": (w_gate, m_w_gate, v_w_gate),
           "w_up": (w_up, m_w_up, v_w_up), "w_down": (w_down, m_w_down, v_w_down)}
    for n in TRANSPOSED:
        big[n] = tuple(jnp.swapaxes(a, 1, 2) for a in big[n])
    shards = {}
    for n in big:
        w = _pad_rows(_pad_lanes(big[n][0][0].astype(BF)))
        shards[n] = w.reshape(2, w.shape[0] // 2, w.shape[1])
    sizes = {n: big[n][0].shape[1:] for n in big}
    w_in_full = _assemble_weight("w_in", _all_gather_weights([shards["w_in"]])[0], shards["w_in"], sizes["w_in"])
    small_w ={"g_mix": g_mix, "b_forget": b_forget, "g_ret_out": g_ret_out, "g_fox_q": g_fox_q, "g_fox_k": g_fox_k,
               "g_xattn": g_xattn, "g_mem": g_mem, "g_xq": g_xq, "g_xk": g_xk, "g_ffn": g_ffn}
    m_small = {"g_mix": m_g_mix, "b_forget": m_b_forget, "g_ret_out": m_g_ret_out, "g_fox_q": m_g_fox_q, "g_fox_k": m_g_fox_k,
               "g_xattn": m_g_xattn, "g_mem": m_g_mem, "g_xq": m_g_xq, "g_xk": m_g_xk, "g_ffn": m_g_ffn}
    v_small = {"g_mix": v_g_mix, "b_forget": v_b_forget, "g_ret_out": v_g_ret_out, "g_fox_q": v_g_fox_q, "g_fox_k": v_g_fox_k,
               "g_xattn": v_g_xattn, "g_mem": v_g_mem, "g_xq": v_g_xq, "g_xk": v_g_xk, "g_ffn": v_g_ffn}
    loss_part, grad_x, sums, got, small_g = _local_step(x[0], mem[0], loss_target[0], w_in_full, shards, sizes, small_w)
    return _reduce_and_update(big, sums, got, small_w, small_g, loss_part, grad_x, m_small, v_small)


def _assemble_weight(name, gathered, own, size):
    rows, width = size
    my_chip = 2 * lax.axis_index("x") + lax.axis_index("y")
    g = lax.dynamic_update_slice(gathered, own[None], (my_chip, 0, 0, 0))
    g = g.reshape(4, 2 * g.shape[2], g.shape[3])[:, :rows, :width]
    return jnp.transpose(g, (1, 0, 2)).reshape(rows, 4 * width) if name in COL_SHARDED else g.reshape(4 * rows, width)


def _core_sums(names, dw):
    parts = [_pad_lanes(_cols_to_shards(dw[n]) if n in COL_SHARDED else _rows_to_shards(dw[n])) for n in names]
    theirs = _exchange_core_halves(parts)
    return [_add_pairs(p, t, f"core_sum_{n}") for n, p, t in zip(names, parts, theirs)]


def _local_step(xs, mems, tgt, w_in_full, shards, sizes, small_w):
    g_mix, b_forget, g_ret_out, g_fox_q, g_fox_k = (small_w[n] for n in ("g_mix", "b_forget", "g_ret_out", "g_fox_q", "g_fox_k"))
    g_xattn, g_mem, g_xq, g_xk, g_ffn = (small_w[n] for n in ("g_xattn", "g_mem", "g_xq", "g_xk", "g_ffn"))
    w_in_t = jnp.pad(w_in_full, ((0, MAIN_W + LANES - IN_W), (0, 0)))
    t_len = xs.shape[0]
    cos_t, sin_t = _rope_tables(t_len)
    tables = _decay_tables(min(RET_BLOCK, t_len))
    gq_t = jnp.concatenate([g_fox_q, g_fox_q], axis=-1)
    gk_t = jnp.concatenate([g_fox_k, g_fox_k], axis=-1)
    b_pad = _pad_row(b_forget, LANES)
    g_ret = g_ret_out.reshape(N_HEADS // 2, 1, LANES)

    n1, proj, rq, rk, q_aug, k_aug, z = _in_proj_fwd(xs, g_mix, w_in_t, b_pad, cos_t, sin_t, gq_t, gk_t)
    raw, mix_r, states = _retention_fwd(rq, rk, proj, g_ret, tables)
    mix_f, o32, lse, *gathered = _fox_fwd(q_aug, k_aug, proj, [shards[n] for n in LATE])
    full = {n: _assemble_weight(n, g, shards[n], sizes[n]) for n, g in zip(LATE, gathered)}
    memn, kraw, kn, vmem = _mem_kv_fwd(mems, g_mem, full["w_xkv"], g_xk)
    h1, hn2, qx, o_x, h2 = _attn_out_xattn_fwd(xs, mix_r, mix_f, full["w_out"], g_xattn, full["w_xq"], g_xq, kn, vmem, full["w_xo"])
    hn3, gate, up, act, dh3, loss_part = _ffn_loss_fwd(h2, g_ffn, full["w_gate"], full["w_up"], full["w_down"], tgt)

    dgate, dup, dh2, dg_ffn = _ffn_bwd(dh3, gate, up, h2, g_ffn, full["w_gate"], full["w_up"], full["w_down"])
    dqx, dh1, dmr, dmf, dkn, dvm, dg_xattn, dg_xq = _attn_out_xattn_bwd(dh2, h1, qx, kn, vmem, full["w_xo"], full["w_xq"],
                                                                      full["w_out"], g_xattn, g_xq)
    dw_xkv, dg_mem, dg_xk = _mem_kv_bwd(dkn, dvm, kraw, mems, memn, g_mem, g_xk, full["w_xkv"])
    dw = {
        "w_out": jnp.concatenate([_matmul_tn(mix_r, dh1, "dw_out_ret"), _matmul_tn(mix_f, dh1, "dw_out_fox")], axis=0),
        "w_xq": _matmul_tn(hn2, dqx, "dw_xq"),
        "w_xkv": dw_xkv,
        "w_xo": _matmul_tn(o_x, dh2, "dw_xo"),
        "w_gate": _matmul_tn(dgate, hn3, "dw_gate"),
        "w_up": _matmul_tn(dup, hn3, "dw_up"),
        "w_down": _matmul_tn(act, dh3, "dw_down"),
    }
    late_sums = _core_sums(LATE, dw)
    dq_f, dk_f, dv_f, df, *late_got = _fox_bwd(q_aug, k_aug, proj, dmf, o32, lse, [s[1] for s in late_sums])
    dq_r, dk_r, dv_r, drg, dg_ret = _retention_bwd(dmr, raw, proj, g_ret, rq, rk, states, tables)
    df_col = jnp.pad(jnp.transpose(df, (1, 0, 2)).reshape(t_len, N_HEADS), ((0, 0), (0, LANES - N_HEADS)))
    dproj, dz, grad_x, dg_mix, dg_fq, dg_fk, db = _in_proj_bwd(xs, g_mix, dh1, dq_r, dk_r, dv_r, drg, dq_f, dk_f, dv_f, df_col,
                                                              proj, z, cos_t, sin_t, gq_t, gk_t, w_in_t)

    dw_in = jnp.concatenate([_matmul_tn(dproj, n1, "dw_in_main"), _matmul_tn(dz, n1, "dw_in_ff")[:IN_W - MAIN_W]], axis=0)
    in_sums = _core_sums(("w_in",), {"w_in": dw_in})
    in_got = _scatter_to_chips([in_sums[0][1]])
    sums = {n: s[0] for n, s in zip(("w_in",) + LATE, in_sums + late_sums)}
    got = dict(zip(("w_in",) + LATE, list(in_got) + late_got))
    small_g = {"g_mix": dg_mix, "b_forget": db[:, :N_HEADS], "g_ret_out": dg_ret, "g_fox_q": dg_fq, "g_fox_k": dg_fk,
               "g_xattn": dg_xattn, "g_mem": dg_mem, "g_xq": dg_xq, "g_xk": dg_xk, "g_ffn": dg_ffn}
    return loss_part, grad_x, sums, got, small_g


def _reduce_and_update(big, sums, got, small_w, small_g, loss_part, grad_x, m_small, v_small):
    big_names = list(big)
    my_core = lax.axis_index("c")
    finals = [_add_received(sums[n], got[n], f"chip_sum_{n}") for n in big_names]
    shared = _share_with_sibling(finals)
    grads, deltas, new_m, new_v = {}, {}, {}, {}
    for n, s, fin in zip(big_names, shared, finals):
        w, m, v = big[n]
        s = lax.dynamic_update_slice(s, fin[None], (my_core, 0, 0))
        g = s.reshape(2 * s.shape[1], s.shape[2])[:w.shape[1], :w.shape[2]]
        d, nm, nv = _adamw(w[0], g, m[0], v[0], f"adamw_{n}")
        grads[n], deltas[n], new_m[n], new_v[n] = ((jnp.swapaxes(a[None], 1, 2) if n in TRANSPOSED else a[None]) for a in (g, d, nm, nv))

    small_names = list(small_w)
    pad_rows = SMALL_ROWS - len(small_names) - 1
    stack = lambda d: jnp.concatenate([_pad_row(d[n]) for n in small_names] + [jnp.zeros((pad_rows + 1, D_MODEL), F32)], axis=0)
    g_pack = jnp.concatenate([_pad_row(small_g[n]) for n in small_names] + [_pad_row(loss_part[0:1, 0:1])]
                             + [jnp.zeros((pad_rows, D_MODEL), F32)], axis=0)
    g_tot = _all_reduce_small(g_pack)
    d_s, m_s, v_s = _adamw(stack(small_w), g_tot, stack(m_small), stack(v_small), "adamw_small")
    for i, n in enumerate(small_names):
        shape = small_w[n].shape
        size = int(np.prod(shape))
        grads[n] = g_tot[i, :size].reshape(shape)
        deltas[n], new_m[n], new_v[n] = d_s[i, :size].reshape(shape), m_s[i, :size].reshape(shape), v_s[i, :size].reshape(shape)
    loss = g_tot[len(small_names), 0]

    order = ["g_mix", "w_in", "b_forget", "g_ret_out", "g_fox_q", "g_fox_k", "w_out", "g_xattn", "w_xq", "w_xkv", "g_mem", "g_xq",
             "g_xk", "w_xo", "g_ffn", "w_gate", "w_up", "w_down"]
    return (loss, grad_x[None], *[grads[n] for n in order], *[deltas[n] for n in order], *[new_m[n] for n in order],
            *[new_v[n] for n in order])
```

```python
import functools

import numpy as np
import jax
import jax.numpy as jnp
from jax import lax
from jax.experimental import pallas as pl
from jax.experimental.pallas import tpu as pltpu

F32 = jnp.float32
BF = jnp.bfloat16

D_MODEL = 1024
HEAD_DIM = 64
N_HEADS = 8
GROUP_W = 512
N_XH = 4
XHD = 256
D_FF = 2816
MAIN_W = 3584
IN_W = 3592
ROPE_BASE = 10000.0
LOG2E = 1.4426950408889634
LN2 = 0.6931471805599453
EPS = 1e-6
NEG = -1e30
LANES = 128
RET_BLOCK = 256
REF_CHUNK = 64
ROW_TILE = 256
ATT_BLOCK = 256
TN_MAX_ROWS = 1408
SMALL_ROWS = 16
COL_SHARDED = ("w_xkv",)
TRANSPOSED = ("w_in", "w_gate", "w_up")
SHARD_ROW_ALIGN = 32
SHARD_ROW_PAD = 256
LATE = ("w_out", "w_xq", "w_xkv", "w_xo", "w_gate", "w_up", "w_down")
VMEM_LIMIT = 56 * 1024 * 1024

ADAM_LR = 0.001
ADAM_B1 = 0.9
ADAM_B2 = 0.999
ADAM_EPS = 1e-08
ADAM_WD = 0.01
ADAM_STEP = 10

MESH = pl.DeviceIdType.MESH
ANY = pl.BlockSpec(memory_space=pl.ANY)
VMEM_SPEC = pl.BlockSpec(memory_space=pltpu.VMEM)


def _cparams(sem=None, vmem=VMEM_LIMIT):
    return pltpu.CompilerParams(dimension_semantics=sem, vmem_limit_bytes=vmem)


def _dot(a, b):
    return jnp.dot(a.astype(BF), b.astype(BF), preferred_element_type=F32)


def _dot_nt(a, b):
    return lax.dot_general(a.astype(BF), b.astype(BF), (((1,), (1,)), ((), ())), preferred_element_type=F32)


def _dot_tn(a, b):
    return lax.dot_general(a.astype(BF), b.astype(BF), (((0,), (0,)), ((), ())), preferred_element_type=F32)


def _split3(x):
    hi = x.astype(BF)
    r = x - hi.astype(F32)
    mid = r.astype(BF)
    lo = (r - mid.astype(F32)).astype(BF)
    return hi, mid, lo


def _dot_exact(ind, x):
    hi, mid, lo = _split3(x)
    return (jnp.dot(ind, lo, preferred_element_type=F32) + jnp.dot(ind, mid, preferred_element_type=F32)
            + jnp.dot(ind, hi, preferred_element_type=F32))


def _dot_nt_exact(ind, x):
    hi, mid, lo = _split3(x)
    dn = (((1,), (1,)), ((), ()))
    return (lax.dot_general(ind, lo, dn, preferred_element_type=F32) + lax.dot_general(ind, mid, dn, preferred_element_type=F32)
            + lax.dot_general(ind, hi, dn, preferred_element_type=F32))


def _sigmoid(x):
    return 1.0 / (1.0 + jnp.exp(-x))


def _rms_fwd(x, g):
    r = lax.rsqrt(jnp.mean(x * x, axis=-1, keepdims=True) + EPS)
    return x * r * g


def _rms_bwd(x, g, dy):
    r = lax.rsqrt(jnp.mean(x * x, axis=-1, keepdims=True) + EPS)
    xh = x * r
    dg = jnp.sum(dy * xh, axis=0, keepdims=True)
    dxh = dy * g
    dx = r * (dxh - xh * jnp.mean(dxh * xh, axis=-1, keepdims=True))
    return dx, dg


def _group_mean64(x):
    lane = lax.broadcasted_iota(jnp.int32, x.shape, 1)
    lo = lane < HEAD_DIM
    s_lo = jnp.sum(jnp.where(lo, x, 0.0), axis=-1, keepdims=True)
    s_hi = jnp.sum(jnp.where(lo, 0.0, x), axis=-1, keepdims=True)
    return jnp.where(lo, s_lo, s_hi) * (1.0 / HEAD_DIM)


def _swap32(x):
    lane = lax.broadcasted_iota(jnp.int32, x.shape, 1)
    first = (lane % HEAD_DIM) < (HEAD_DIM // 2)
    return jnp.where(first, pltpu.roll(x, LANES - HEAD_DIM // 2, axis=1), pltpu.roll(x, HEAD_DIM // 2, axis=1))


def _chunks(w):
    return [slice(j * LANES, (j + 1) * LANES) for j in range(w // LANES)]


def _aug_pair(qk, f_cols, is_query):
    lane = lax.broadcasted_iota(jnp.int32, qk.shape, 1)
    a = lane - HEAD_DIM
    values = (qk, pltpu.roll(qk, HEAD_DIM, axis=1))
    out = []
    for hh in range(2):
        hi, mid, lo = (p.astype(F32) for p in _split3(f_cols[hh] * LOG2E))
        if is_query:
            aux = jnp.where(a == 0, hi, jnp.where(a == 1, mid, jnp.where(a == 2, lo, jnp.where(a < 6, 1.0, 0.0))))
        else:
            aux = jnp.where(a < 3, 1.0, jnp.where(a == 3, -hi, jnp.where(a == 4, -mid, jnp.where(a == 5, -lo, 0.0))))
        out.append(jnp.where(a < 0, values[hh], aux))
    return jnp.concatenate(out, axis=-1).astype(BF)


def _mem_kv_fwd(mem, g_mem, w_xkv, g_xk):
    m_tok = mem.shape[0]

    def body(mem_ref, gm_ref, w_ref, gk_ref, memn_ref, kraw_ref, kn_ref, v_ref):
        mn = _rms_fwd(mem_ref[...], gm_ref[...]).astype(BF)
        memn_ref[...] = mn
        kv = jnp.dot(mn, w_ref[...], preferred_element_type=F32)
        k = kv[:, :D_MODEL]
        kraw_ref[...] = k
        v_ref[...] = kv[:, D_MODEL:].astype(BF)
        for h in range(N_XH):
            sl = slice(h * XHD, (h + 1) * XHD)
            kn_ref[:, sl] = _rms_fwd(k[:, sl], gk_ref[...]).astype(BF)

    return pl.pallas_call(
        body, name="mem_kv_fwd",
        out_shape=(jax.ShapeDtypeStruct((m_tok, D_MODEL), BF), jax.ShapeDtypeStruct((m_tok, D_MODEL), F32),
                   jax.ShapeDtypeStruct((m_tok, D_MODEL), BF), jax.ShapeDtypeStruct((m_tok, D_MODEL), BF)),
        in_specs=[VMEM_SPEC] * 4, out_specs=(VMEM_SPEC,) * 4, compiler_params=_cparams(),
    )(mem, g_mem, w_xkv, g_xk)


def _in_proj_fwd(x, g_mix, w_in_t, b_pad, cos_t, sin_t, gq_t, gk_t):
    t_len = x.shape[0]
    tm = min(ROW_TILE, t_len)
    n_t = t_len // tm

    def body(x_ref, g_ref, wm_ref, wf_ref, b_ref, cos_ref, sin_ref, gq_ref, gk_ref,
             n1_ref, proj_ref, rq_ref, rk_ref, qa_ref, ka_ref, z_ref, carry):
        i = pl.program_id(0)

        @pl.when(i == 0)
        def _():
            carry[...] = jnp.zeros_like(carry)

        n1 = _rms_fwd(x_ref[...], g_ref[...]).astype(BF)
        n1_ref[...] = n1
        proj = _dot_nt(n1, wm_ref[...])
        proj_ref[...] = proj.astype(BF)
        z = _dot_nt(n1, wf_ref[...]) + b_ref[...]
        z_ref[...] = z
        lane = lax.broadcasted_iota(jnp.int32, z.shape, 1)
        lf = jnp.where(lane < N_HEADS, jnp.minimum(z, 0.0) - jnp.log(1.0 + jnp.exp(-jnp.abs(z))), 0.0)
        row = lax.broadcasted_iota(jnp.int32, (tm, tm), 0)
        col = lax.broadcasted_iota(jnp.int32, (tm, tm), 1)
        tri = (row >= col).astype(BF)
        fc = _dot_exact(tri, lf) + carry[0:1, :]
        carry[...] = jnp.broadcast_to(fc[tm - 1:tm, :], carry.shape)
        c, s = cos_ref[...], sin_ref[...]
        for j, sl in enumerate(_chunks(GROUP_W)):
            q = proj[:, sl]
            rq_ref[:, sl] = ((q * c + _swap32(q) * s) * 0.125).astype(BF)
            k = proj[:, GROUP_W + j * LANES:GROUP_W + (j + 1) * LANES]
            rk_ref[:, sl] = (k * c + _swap32(k) * s).astype(BF)
            f_cols = [fc[:, 2 * j:2 * j + 1], fc[:, 2 * j + 1:2 * j + 2]]
            fq = proj[:, 4 * GROUP_W + j * LANES:4 * GROUP_W + (j + 1) * LANES]
            fq = fq * lax.rsqrt(_group_mean64(fq * fq) + EPS) * gq_ref[...] * (0.125 * LOG2E)
            qa_ref[:, 2 * j * LANES:2 * (j + 1) * LANES] = _aug_pair(fq, f_cols, True)
            fk = proj[:, 5 * GROUP_W + j * LANES:5 * GROUP_W + (j + 1) * LANES]
            fk = fk * lax.rsqrt(_group_mean64(fk * fk) + EPS) * gk_ref[...]
            ka_ref[:, 2 * j * LANES:2 * (j + 1) * LANES] = _aug_pair(fk, f_cols, False)

    row_spec = lambda w: pl.BlockSpec((tm, w), lambda i: (i, 0))
    full = lambda a: pl.BlockSpec(a.shape, lambda i: (0,) * a.ndim)
    return pl.pallas_call(
        body, name="in_proj_fwd", grid=(n_t,),
        out_shape=(jax.ShapeDtypeStruct((t_len, D_MODEL), BF), jax.ShapeDtypeStruct((t_len, MAIN_W), BF),
                   jax.ShapeDtypeStruct((t_len, GROUP_W), BF), jax.ShapeDtypeStruct((t_len, GROUP_W), BF),
                   jax.ShapeDtypeStruct((t_len, 2 * GROUP_W), BF), jax.ShapeDtypeStruct((t_len, 2 * GROUP_W), BF),
                   jax.ShapeDtypeStruct((t_len, LANES), F32)),
        in_specs=[row_spec(D_MODEL), full(g_mix), *_w_in_specs(), full(b_pad), row_spec(LANES), row_spec(LANES),
                  full(gq_t), full(gk_t)],
        out_specs=(row_spec(D_MODEL), row_spec(MAIN_W), row_spec(GROUP_W), row_spec(GROUP_W), row_spec(2 * GROUP_W),
                   row_spec(2 * GROUP_W), row_spec(LANES)),
        scratch_shapes=[pltpu.VMEM((8, LANES), F32)],
        compiler_params=_cparams(("arbitrary",)),
    )(x, g_mix, w_in_t, w_in_t, b_pad, cos_t, sin_t, gq_t, gk_t)


def _w_in_specs():
    return (pl.BlockSpec((MAIN_W, D_MODEL), lambda i: (0, 0)), pl.BlockSpec((LANES, D_MODEL), lambda i: (MAIN_W // LANES, 0)))


def _decay_tables(c):
    h = np.arange(N_HEADS, dtype=np.float64)
    lg = np.log(1.0 - 2.0 ** (-5.0 - h)).astype(np.float32).astype(np.float64)
    t = np.arange(c)
    same_or_earlier = (t[None, :] // REF_CHUNK) <= (t[:, None] // REF_CHUNK)
    w = np.where(same_or_earlier[None], np.exp(lg[:, None, None] * np.abs(t[:, None] - t[None, :])[None]), 0.0)
    qd = np.exp(lg[:, None] * (t[None, :] + 1.0))
    kd = np.exp(lg[:, None] * (c - 1.0 - t[None, :]))
    cd = np.exp(lg * c)
    ones = np.ones((1, 1, HEAD_DIM))
    return (jnp.asarray(w, F32), jnp.asarray(qd[:, :, None] * ones, F32), jnp.asarray(kd[:, :, None] * ones, F32),
            jnp.asarray(cd[:, None, None] * np.ones((1, HEAD_DIM, HEAD_DIM)), F32))


def _retention_fwd(rq, rk, proj, g_ret, tables):
    t_len = rq.shape[0]
    c = min(RET_BLOCK, t_len)
    n_b = t_len // c
    wdec, qdec, kdec, cdec = tables
    v_col, g_col = 2 * GROUP_W // LANES, 3 * GROUP_W // LANES

    def body(q_ref, k_ref, v_ref, rg_ref, g_ref, w_ref, qd_ref, kd_ref, cd_ref, raw_ref, mix_ref, st_ref, state):
        i = pl.program_id(1)

        @pl.when(i == 0)
        def _():
            state[...] = jnp.zeros_like(state)

        q2, k2, v2 = q_ref[...], k_ref[...], v_ref[...]
        outs = []
        for hh in range(2):
            sl = slice(hh * HEAD_DIM, (hh + 1) * HEAD_DIM)
            q, k, v = q2[:, sl], k2[:, sl], v2[:, sl]
            sp = state[hh]
            st_ref[0, 0, hh] = sp
            a = _dot_nt(q, k) * w_ref[hh]
            o = _dot(a, v) + _dot(q.astype(F32) * qd_ref[hh], sp)
            state[hh] = sp * cd_ref[hh] + _dot_tn(k.astype(F32) * kd_ref[hh], v)
            outs.append(o)
        o2 = jnp.concatenate(outs, axis=-1)
        raw_ref[...] = o2
        xc = o2 - _group_mean64(o2)
        xh = xc * lax.rsqrt(_group_mean64(xc * xc) + EPS)
        gate = rg_ref[...].astype(F32)
        mix_ref[...] = (gate * _sigmoid(gate) * (xh * g_ref[0])).astype(BF)

    blk = lambda col0: pl.BlockSpec((c, LANES), lambda hp, i: (i, col0 + hp))
    tab = lambda a: pl.BlockSpec((2,) + a.shape[1:], lambda hp, i: (hp, 0, 0))
    return pl.pallas_call(
        body, name="retention_fwd", grid=(N_HEADS // 2, n_b),
        out_shape=(jax.ShapeDtypeStruct((t_len, GROUP_W), F32), jax.ShapeDtypeStruct((t_len, GROUP_W), BF),
                   jax.ShapeDtypeStruct((N_HEADS // 2, n_b, 2, HEAD_DIM, HEAD_DIM), F32)),
        in_specs=[blk(0), blk(0), blk(v_col), blk(g_col), pl.BlockSpec((1, 1, LANES), lambda hp, i: (hp, 0, 0)),
                  tab(wdec), tab(qdec), tab(kdec), tab(cdec)],
        out_specs=(blk(0), blk(0), pl.BlockSpec((1, 1, 2, HEAD_DIM, HEAD_DIM), lambda hp, i: (hp, i, 0, 0, 0))),
        scratch_shapes=[pltpu.VMEM((2, HEAD_DIM, HEAD_DIM), F32)],
        compiler_params=_cparams(("arbitrary", "arbitrary")),
    )(rq, rk, proj, proj, g_ret, wdec, qdec, kdec, cdec)


def _fox_fwd(q_aug, k_aug, proj, shards):
    t_len = q_aug.shape[0]
    tq = min(ATT_BLOCK, t_len)
    n_q = t_len // tq
    v_col = 6 * GROUP_W // LANES
    tc = min(512, t_len)
    n_w = len(shards)
    n_steps = (N_HEADS // 2) * n_q

    def body(*refs):
        q_ref, k_ref, v_ref = refs[:3]
        o_ref, o32_ref, lse_ref = refs[3 + n_w:6 + n_w]
        vt = refs[6 + 2 * n_w]
        comm = (refs[3:3 + n_w], refs[6 + n_w:6 + 2 * n_w]) + tuple(refs[7 + 2 * n_w:])
        i = pl.program_id(1)
        step = pl.program_id(0) * n_q + i

        @pl.when(step == 0)
        def _():
            _gather_phase(0, *comm)

        @pl.when(step == (3 * n_steps) // 4)
        def _():
            _gather_phase(1, *comm)

        @pl.when(i == 0)
        def _():
            for c0 in range(0, t_len, tc):
                vt[:, c0:c0 + tc] = v_ref[c0:c0 + tc, :].T

        qs = [q_ref[:, hh * LANES:(hh + 1) * LANES] for hh in range(2)]
        ones = jnp.ones((HEAD_DIM, tq), BF)

        def scores(j):
            k2 = k_ref[pl.ds(pl.multiple_of(j * tq, tq), tq), :]
            return tuple(_dot_nt(k2[:, hh * LANES:(hh + 1) * LANES], qs[hh]) for hh in range(2))

        def update(j, ss, carry, masked):
            v2 = vt[:, pl.ds(pl.multiple_of(j * tq, tq), tq)]
            ps, stats = [], []
            for hh in range(2):
                m = carry[hh][0]
                s_t = ss[hh]
                if masked:
                    krow = lax.broadcasted_iota(jnp.int32, (tq, tq), 0)
                    qcol = lax.broadcasted_iota(jnp.int32, (tq, tq), 1)
                    s_t = jnp.where(qcol >= krow, s_t, NEG)
                m_new = jnp.maximum(m, jnp.max(s_t, axis=0, keepdims=True))
                ps.append(jnp.exp2(s_t - m_new).astype(BF))
                stats.append((m_new, jnp.exp2(m - m_new)))
            out = []
            for hh in range(2):
                m_new, alpha = stats[hh]
                v_aug = jnp.concatenate([v2[hh * HEAD_DIM:(hh + 1) * HEAD_DIM, :], ones], axis=0)
                out.append((m_new, carry[hh][1] * alpha + jnp.dot(v_aug, ps[hh], preferred_element_type=F32)))
            return tuple(out)

        def advance(j, state):
            ss, carry = state
            return scores(j + 1), update(j, ss, carry, False)

        init = tuple((jnp.full((1, tq), NEG, F32), jnp.zeros((LANES, tq), F32)) for _ in range(2))
        ss, carry = lax.fori_loop(0, i, advance, (scores(0), init))
        carry = update(i, ss, carry, True)
        outs, lses = [], []
        for hh in range(2):
            m, acc = carry[hh]
            l = acc[HEAD_DIM:HEAD_DIM + 1, :]
            outs.append(acc[:HEAD_DIM, :] / l)
            lses.append(m + jnp.log2(l))
        o2 = jnp.concatenate(outs, axis=0).T
        o32_ref[...] = o2
        o_ref[...] = o2.astype(BF)
        lse_ref[0] = jnp.concatenate(lses, axis=0)

        @pl.when(step == n_steps - 1)
        def _():
            _gather_phase(2, *comm)

    return pl.pallas_call(
        body, name="fox_fwd", grid=(N_HEADS // 2, n_q),
        out_shape=(jax.ShapeDtypeStruct((t_len, GROUP_W), BF), jax.ShapeDtypeStruct((t_len, GROUP_W), F32),
                   jax.ShapeDtypeStruct((N_HEADS // 2, 2, t_len), F32))
        + tuple(jax.ShapeDtypeStruct((4,) + s.shape, s.dtype) for s in shards),
        in_specs=[pl.BlockSpec((tq, 2 * LANES), lambda hp, i: (i, hp)),
                  pl.BlockSpec((t_len, 2 * LANES), lambda hp, i: (0, hp)),
                  pl.BlockSpec((t_len, LANES), lambda hp, i: (0, v_col + hp))] + [ANY] * n_w,
        out_specs=(pl.BlockSpec((tq, LANES), lambda hp, i: (i, hp)), pl.BlockSpec((tq, LANES), lambda hp, i: (i, hp)),
                   pl.BlockSpec((1, 2, tq), lambda hp, i: (hp, 0, i))) + (ANY,) * n_w,
        scratch_shapes=[pltpu.VMEM((LANES, t_len), BF)] + _gather_scratch(n_w),
        compiler_params=_cparams(("arbitrary", "arbitrary")),
    )(q_aug, k_aug, proj, *shards)


def _softmax_rows(s):
    p = jnp.exp(s - jnp.max(s, axis=-1, keepdims=True))
    return p / jnp.sum(p, axis=-1, keepdims=True)


def _attn_out_xattn_fwd(x, mix_r, mix_f, w_out, g_xattn, w_xq, g_xq, kn, v, w_xo):
    t_len = x.shape[0]
    tm = min(ROW_TILE, t_len)

    def body(x_ref, mr_ref, mf_ref, wo_ref, g_ref, wq_ref, gq_ref, kn_ref, v_ref, wxo_ref,
             h1_ref, hn_ref, qx_ref, o_ref, h2_ref):
        h1 = x_ref[...] + jnp.dot(mr_ref[...], wo_ref[:GROUP_W, :], preferred_element_type=F32) \
            + jnp.dot(mf_ref[...], wo_ref[GROUP_W:, :], preferred_element_type=F32)
        h1_ref[...] = h1
        hn = _rms_fwd(h1, g_ref[...]).astype(BF)
        hn_ref[...] = hn
        qx = jnp.dot(hn, wq_ref[...], preferred_element_type=F32).astype(BF)
        qx_ref[...] = qx
        for h in range(N_XH):
            sl = slice(h * XHD, (h + 1) * XHD)
            qn = _rms_fwd(qx[:, sl].astype(F32), gq_ref[...])
            p = _softmax_rows(_dot_nt(qn, kn_ref[:, sl]) * (XHD ** -0.5))
            o_ref[:, sl] = _dot(p, v_ref[:, sl]).astype(BF)
        h2_ref[...] = h1 + jnp.dot(o_ref[...], wxo_ref[...], preferred_element_type=F32)

    row_spec = lambda w: pl.BlockSpec((tm, w), lambda i: (i, 0))
    full = lambda a: pl.BlockSpec(a.shape, lambda i: (0,) * a.ndim)
    return pl.pallas_call(
        body, name="attn_out_xattn_fwd", grid=(t_len // tm,),
        out_shape=(jax.ShapeDtypeStruct((t_len, D_MODEL), F32), jax.ShapeDtypeStruct((t_len, D_MODEL), BF),
                   jax.ShapeDtypeStruct((t_len, D_MODEL), BF), jax.ShapeDtypeStruct((t_len, D_MODEL), BF),
                   jax.ShapeDtypeStruct((t_len, D_MODEL), F32)),
        in_specs=[row_spec(D_MODEL), row_spec(GROUP_W), row_spec(GROUP_W), full(w_out), full(g_xattn), full(w_xq), full(g_xq),
                  full(kn), full(v), full(w_xo)],
        out_specs=(row_spec(D_MODEL),) * 5,
        compiler_params=_cparams(("arbitrary",)),
    )(x, mix_r, mix_f, w_out, g_xattn, w_xq, g_xq, kn, v, w_xo)


def _ffn_loss_fwd(h2, g_ffn, w_gate, w_up, w_down, target):
    t_len = h2.shape[0]
    tm = min(ROW_TILE, t_len)

    def body(h2_ref, g_ref, wg_ref, wu_ref, wd_ref, tgt_ref, hn_ref, gate_ref, up_ref, act_ref, dh3_ref, loss_ref):
        @pl.when(pl.program_id(0) == 0)
        def _():
            loss_ref[...] = jnp.zeros_like(loss_ref)

        h2v = h2_ref[...]
        hn = _rms_fwd(h2v, g_ref[...]).astype(BF)
        hn_ref[...] = hn
        gate = _dot_nt(hn, wg_ref[...])
        up = _dot_nt(hn, wu_ref[...])
        gate_ref[...] = gate.astype(BF)
        up_ref[...] = up.astype(BF)
        act = (gate * _sigmoid(gate) * up).astype(BF)
        act_ref[...] = act
        diff = h2v + jnp.dot(act, wd_ref[...], preferred_element_type=F32) - tgt_ref[...]
        dh3_ref[...] = diff * (1.0 / D_MODEL)
        per_row = jnp.sum(diff * diff, axis=-1, keepdims=True) * (1.0 / D_MODEL)
        loss_ref[...] += 0.5 * jnp.sum(per_row, axis=0, keepdims=True)

    row_spec = lambda w: pl.BlockSpec((tm, w), lambda i: (i, 0))
    full = lambda a: pl.BlockSpec(a.shape, lambda i: (0,) * a.ndim, pipeline_mode=pl.Buffered(1))
    return pl.pallas_call(
        body, name="ffn_loss_fwd", grid=(t_len // tm,),
        out_shape=(jax.ShapeDtypeStruct((t_len, D_MODEL), BF), jax.ShapeDtypeStruct((t_len, D_FF), BF),
                   jax.ShapeDtypeStruct((t_len, D_FF), BF), jax.ShapeDtypeStruct((t_len, D_FF), BF),
                   jax.ShapeDtypeStruct((t_len, D_MODEL), F32), jax.ShapeDtypeStruct((8, LANES), F32)),
        in_specs=[row_spec(D_MODEL), full(g_ffn), full(w_gate), full(w_up), full(w_down), row_spec(D_MODEL)],
        out_specs=(row_spec(D_MODEL), row_spec(D_FF), row_spec(D_FF), row_spec(D_FF), row_spec(D_MODEL),
                   pl.BlockSpec((8, LANES), lambda i: (0, 0))),
        compiler_params=_cparams(("arbitrary",)),
    )(h2, g_ffn, w_gate, w_up, w_down, target)


def _ffn_bwd(dh3, gate, up, h2, g_ffn, w_gate, w_up, w_down):
    t_len = h2.shape[0]
    tm = min(ROW_TILE, t_len)

    def body(dh3_ref, gate_ref, up_ref, h2_ref, g_ref, wg_ref, wu_ref, wd_ref, dgate_ref, dup_ref, dh2_ref, dg_ref):
        @pl.when(pl.program_id(0) == 0)
        def _():
            dg_ref[...] = jnp.zeros_like(dg_ref)

        dh3v = dh3_ref[...]
        dact = _dot_nt(dh3v, wd_ref[...])
        g = gate_ref[...].astype(F32)
        sg = _sigmoid(g)
        dup = (dact * (g * sg)).astype(BF)
        dgate = (dact * up_ref[...].astype(F32) * (sg * (1.0 + g * (1.0 - sg)))).astype(BF)
        dup_ref[...] = dup
        dgate_ref[...] = dgate
        dhn = jnp.dot(dgate, wg_ref[...], preferred_element_type=F32) + jnp.dot(dup, wu_ref[...], preferred_element_type=F32)
        dx, dg = _rms_bwd(h2_ref[...], g_ref[...], dhn)
        dh2_ref[...] = dh3v + dx
        dg_ref[...] += dg

    row_spec = lambda w: pl.BlockSpec((tm, w), lambda i: (i, 0))
    full = lambda a: pl.BlockSpec(a.shape, lambda i: (0,) * a.ndim, pipeline_mode=pl.Buffered(1))
    return pl.pallas_call(
        body, name="ffn_bwd", grid=(t_len // tm,),
        out_shape=(jax.ShapeDtypeStruct((t_len, D_FF), BF), jax.ShapeDtypeStruct((t_len, D_FF), BF),
                   jax.ShapeDtypeStruct((t_len, D_MODEL), F32), jax.ShapeDtypeStruct((1, D_MODEL), F32)),
        in_specs=[row_spec(D_MODEL), row_spec(D_FF), row_spec(D_FF), row_spec(D_MODEL), full(g_ffn), full(w_gate), full(w_up),
                  full(w_down)],
        out_specs=(row_spec(D_FF), row_spec(D_FF), row_spec(D_MODEL), pl.BlockSpec((1, D_MODEL), lambda i: (0, 0))),
        compiler_params=_cparams(("arbitrary",)),
    )(dh3, gate, up, h2, g_ffn, w_gate, w_up, w_down)


def _attn_out_xattn_bwd(dh2, h1, qx, kn, v, w_xo, w_xq, w_out, g_xattn, g_xq):
    t_len = h1.shape[0]
    tm = min(ROW_TILE, t_len)
    m_tok = kn.shape[0]

    def body(dh2_ref, h1_ref, qx_ref, kn_ref, v_ref, wxo_ref, wq_ref, wo_ref, g_ref, gq_ref,
             dqx_ref, dh1_ref, dmr_ref, dmf_ref, dkn_ref, dv_ref, dg_ref, dgq_ref, dqx_scr):
        @pl.when(pl.program_id(0) == 0)
        def _():
            dkn_ref[...] = jnp.zeros_like(dkn_ref)
            dv_ref[...] = jnp.zeros_like(dv_ref)
            dg_ref[...] = jnp.zeros_like(dg_ref)
            dgq_ref[...] = jnp.zeros_like(dgq_ref)

        dh2v = dh2_ref[...]
        do = _dot_nt(dh2v, wxo_ref[...])
        gq = gq_ref[...]
        dgq = jnp.zeros((1, XHD), F32)
        for h in range(N_XH):
            sl = slice(h * XHD, (h + 1) * XHD)
            qraw = qx_ref[:, sl].astype(F32)
            qn = _rms_fwd(qraw, gq)
            p = _softmax_rows(_dot_nt(qn, kn_ref[:, sl]) * (XHD ** -0.5))
            doh = do[:, sl]
            dv_ref[:, sl] += _dot_tn(p, doh)
            dp = _dot_nt(doh, v_ref[:, sl])
            ds = p * (dp - jnp.sum(dp * p, axis=-1, keepdims=True)) * (XHD ** -0.5)
            dqn = _dot(ds, kn_ref[:, sl])
            dkn_ref[:, sl] += _dot_tn(ds, qn)
            dx, dg_h = _rms_bwd(qraw, gq, dqn)
            dgq = dgq + dg_h
            dqx_scr[:, sl] = dx.astype(BF)
        dgq_ref[...] += dgq
        dqx = dqx_scr[...]
        dqx_ref[...] = dqx
        dhn = _dot_nt(dqx, wq_ref[...])
        dx, dg = _rms_bwd(h1_ref[...], g_ref[...], dhn)
        dg_ref[...] += dg
        dh1 = dh2v + dx
        dh1_ref[...] = dh1
        dmix = _dot_nt(dh1, wo_ref[...])
        dmr_ref[...] = dmix[:, :GROUP_W]
        dmf_ref[...] = dmix[:, GROUP_W:].astype(BF)

    row_spec = lambda w: pl.BlockSpec((tm, w), lambda i: (i, 0))
    full = lambda a: pl.BlockSpec(a.shape, lambda i: (0,) * a.ndim)
    acc = lambda r, c: pl.BlockSpec((r, c), lambda i: (0, 0))
    return pl.pallas_call(
        body, name="attn_out_xattn_bwd", grid=(t_len // tm,),
        out_shape=(jax.ShapeDtypeStruct((t_len, D_MODEL), BF), jax.ShapeDtypeStruct((t_len, D_MODEL), F32),
                   jax.ShapeDtypeStruct((t_len, GROUP_W), F32), jax.ShapeDtypeStruct((t_len, GROUP_W), BF),
                   jax.ShapeDtypeStruct((m_tok, D_MODEL), F32), jax.ShapeDtypeStruct((m_tok, D_MODEL), F32),
                   jax.ShapeDtypeStruct((1, D_MODEL), F32), jax.ShapeDtypeStruct((1, XHD), F32)),
        in_specs=[row_spec(D_MODEL), row_spec(D_MODEL), row_spec(D_MODEL), full(kn), full(v), full(w_xo), full(w_xq), full(w_out),
                  full(g_xattn), full(g_xq)],
        out_specs=(row_spec(D_MODEL), row_spec(D_MODEL), row_spec(GROUP_W), row_spec(GROUP_W), acc(m_tok, D_MODEL),
                   acc(m_tok, D_MODEL), acc(1, D_MODEL), acc(1, XHD)),
        scratch_shapes=[pltpu.VMEM((tm, D_MODEL), BF)],
        compiler_params=_cparams(("arbitrary",)),
    )(dh2, h1, qx, kn, v, w_xo, w_xq, w_out, g_xattn, g_xq)


def _mem_kv_bwd(dkn, dv, kraw, mem, memn, g_mem, g_xk, w_xkv):
    m_tok = mem.shape[0]

    def body(dkn_ref, dv_ref, kraw_ref, mem_ref, memn_ref, gm_ref, gk_ref, w_ref, dw_ref, dgm_ref, dgk_ref, dkv_scr):
        gk = gk_ref[...]
        dgk = jnp.zeros((1, XHD), F32)
        for h in range(N_XH):
            sl = slice(h * XHD, (h + 1) * XHD)
            dx, dg_h = _rms_bwd(kraw_ref[:, sl], gk, dkn_ref[:, sl])
            dgk = dgk + dg_h
            dkv_scr[:, sl] = dx.astype(BF)
        dgk_ref[...] = dgk
        dkv_scr[:, D_MODEL:] = dv_ref[...].astype(BF)
        dkv = dkv_scr[...]
        dw_ref[...] = _dot_tn(memn_ref[...], dkv)
        dmemn = _dot_nt(dkv, w_ref[...])
        mem_v = mem_ref[...]
        r = lax.rsqrt(jnp.mean(mem_v * mem_v, axis=-1, keepdims=True) + EPS)
        dgm_ref[...] = jnp.sum(dmemn * mem_v * r, axis=0, keepdims=True)

    return pl.pallas_call(
        body, name="mem_kv_bwd",
        out_shape=(jax.ShapeDtypeStruct((D_MODEL, 2 * D_MODEL), F32), jax.ShapeDtypeStruct((1, D_MODEL), F32),
                   jax.ShapeDtypeStruct((1, XHD), F32)),
        in_specs=[VMEM_SPEC] * 8, out_specs=(VMEM_SPEC,) * 3,
        scratch_shapes=[pltpu.VMEM((m_tok, 2 * D_MODEL), BF)],
        compiler_params=_cparams(),
    )(dkn, dv, kraw, mem, memn, g_mem, g_xk, w_xkv)


def _fox_bwd(q_aug, k_aug, proj, dmf, o32, lse, sums):
    t_len = q_aug.shape[0]
    tb = min(ATT_BLOCK, t_len)
    n_b = t_len // tb
    v_col = 6 * GROUP_W // LANES
    n_w = len(sums)
    n_steps = (N_HEADS // 2) * n_b

    def body(*refs):
        k_ref, v_ref, q_ref, do_ref, o_ref, lse_ref = refs[:6]
        dq_ref, dk_ref, dv_ref, df_ref = refs[6 + n_w:10 + n_w]
        delta = refs[10 + 2 * n_w]
        comm = (refs[6:6 + n_w], refs[10 + n_w:10 + 2 * n_w]) + tuple(refs[11 + 2 * n_w:])
        j = pl.program_id(1)
        step = pl.program_id(0) * n_b + j

        @pl.when(step == 0)
        def _():
            _scatter_phase(0, *comm)

        @pl.when(j == 0)
        def _():
            dq_ref[...] = jnp.zeros_like(dq_ref)
            dd = do_ref[...].astype(F32) * o_ref[...]
            hrow = lax.broadcasted_iota(jnp.int32, (8, LANES), 0)
            lane = lax.broadcasted_iota(jnp.int32, (8, LANES), 1)
            ind = ((lane // HEAD_DIM) == hrow).astype(BF)
            delta[...] = _dot_nt_exact(ind, dd)

        k2, v2 = k_ref[...], v_ref[...]
        ks = [k2[:, hh * LANES:(hh + 1) * LANES] for hh in range(2)]
        vs = [v2[:, hh * HEAD_DIM:(hh + 1) * HEAD_DIM] for hh in range(2)]

        def blocks(idx, carry, masked, valid):
            loaded = []
            for i in idx:
                rows = pl.ds(pl.multiple_of(i * tb, tb), tb)
                q2 = q_ref[rows, :]
                do2 = do_ref[rows, :]
                loaded.append((rows, [q2[:, hh * LANES:(hh + 1) * LANES] for hh in range(2)],
                               [do2[:, hh * HEAD_DIM:(hh + 1) * HEAD_DIM] for hh in range(2)]))
            ss = [[_dot_nt(ks[hh], qs[hh]) for hh in range(2)] for _, qs, _ in loaded]
            dps = [[_dot_nt(vs[hh], dos[hh]) for hh in range(2)] for _, _, dos in loaded]
            pts, dsts, dfs = [], [], []
            for b, (rows, _, _) in enumerate(loaded):
                for hh in range(2):
                    s_t = ss[b][hh]
                    if masked[b]:
                        krow = lax.broadcasted_iota(jnp.int32, (tb, tb), 0)
                        qcol = lax.broadcasted_iota(jnp.int32, (tb, tb), 1)
                        s_t = jnp.where(qcol >= krow, s_t, NEG)
                    lse_row = lse_ref[0, hh:hh + 1, rows]
                    if valid[b] is not None:
                        lse_row = jnp.where(valid[b], lse_row, -NEG)
                    p_t = jnp.exp2(s_t - lse_row)
                    pts.append(p_t.astype(BF))
                    ds_t = p_t * (dps[b][hh] - delta[hh:hh + 1, rows])
                    dsts.append(ds_t.astype(BF))
                    dfs.append(jnp.sum(ds_t, axis=-1, keepdims=True))
            out = list(carry)
            for b, (rows, qs, dos) in enumerate(loaded):
                for hh in range(2):
                    dk, dv, df = out[hh]
                    dv = dv + jnp.dot(pts[2 * b + hh], dos[hh], preferred_element_type=F32)
                    dk = dk + jnp.dot(dsts[2 * b + hh], qs[hh], preferred_element_type=F32)
                    dq_ref[rows, hh * HEAD_DIM:(hh + 1) * HEAD_DIM] += _dot_tn(dsts[2 * b + hh], ks[hh])[:, :HEAD_DIM]
                    out[hh] = (dk, dv, df - dfs[2 * b + hh])
            return tuple(out)

        def pair(t, carry):
            i0 = j + 1 + 2 * t
            return blocks([i0, jnp.minimum(i0 + 1, n_b - 1)], carry, [False, False], [None, i0 + 1 < n_b])

        init = tuple((jnp.zeros((tb, LANES), F32), jnp.zeros((tb, HEAD_DIM), F32), jnp.zeros((tb, 1), F32)) for _ in range(2))
        carry = blocks([j], init, [True], [None])
        carry = lax.fori_loop(0, (n_b - j) // 2, pair, carry)
        dk_ref[...] = jnp.concatenate([carry[hh][0][:, :HEAD_DIM] for hh in range(2)], axis=-1) * LN2
        dv_ref[...] = jnp.concatenate([carry[hh][1] for hh in range(2)], axis=-1)
        df_ref[0] = jnp.concatenate([carry[hh][2] for hh in range(2)], axis=-1)

        @pl.when(step == n_steps - 1)
        def _():
            _scatter_phase(1, *comm)

    blk = lambda w, col0: pl.BlockSpec((tb, w), lambda hp, j: (j, col0 + hp))
    whole = lambda w: pl.BlockSpec((t_len, w), lambda hp, j: (0, hp))
    rows2 = pl.BlockSpec((1, 2, t_len), lambda hp, j: (hp, 0, 0))
    cols2 = pl.BlockSpec((1, tb, 2), lambda hp, j: (hp, j, 0))
    return pl.pallas_call(
        body, name="fox_bwd", grid=(N_HEADS // 2, n_b),
        out_shape=(jax.ShapeDtypeStruct((t_len, GROUP_W), F32), jax.ShapeDtypeStruct((t_len, GROUP_W), F32),
                   jax.ShapeDtypeStruct((t_len, GROUP_W), F32), jax.ShapeDtypeStruct((N_HEADS // 2, t_len, 2), F32))
        + _scatter_out_shapes(sums),
        in_specs=[blk(2 * LANES, 0), blk(LANES, v_col), whole(2 * LANES), whole(LANES), whole(LANES), rows2] + [ANY] * n_w,
        out_specs=(whole(LANES), blk(LANES, 0), blk(LANES, 0), cols2) + (ANY,) * n_w,
        scratch_shapes=[pltpu.VMEM((8, t_len), F32)] + _scatter_scratch(n_w),
        compiler_params=_cparams(("arbitrary", "arbitrary")),
    )(k_aug, proj, q_aug, dmf, o32, lse, *sums)


def _retention_bwd(dmr, raw, proj, g_ret, rq, rk, states, tables):
    t_len = rq.shape[0]
    c = min(RET_BLOCK, t_len)
    n_b = t_len // c
    wdec, qdec, kdec, cdec = tables
    v_col, g_col = 2 * GROUP_W // LANES, 3 * GROUP_W // LANES

    def body(d_ref, raw_ref, rg_ref, g_ref, q_ref, k_ref, v_ref, st_ref, w_ref, qd_ref, kd_ref, cd_ref,
             dq_ref, dk_ref, dv_ref, drg_ref, dg_ref, gstate):
        @pl.when(pl.program_id(1) == 0)
        def _():
            gstate[...] = jnp.zeros_like(gstate)
            dg_ref[...] = jnp.zeros_like(dg_ref)

        d, raw_v, g = d_ref[...], raw_ref[...], g_ref[0]
        gate = rg_ref[...].astype(F32)
        xc = raw_v - _group_mean64(raw_v)
        r = lax.rsqrt(_group_mean64(xc * xc) + EPS)
        xh = xc * r
        sg = _sigmoid(gate)
        drg_ref[...] = d * (xh * g) * (sg * (1.0 + gate * (1.0 - sg)))
        dy = d * (gate * sg)
        dg_ref[0] += jnp.sum(dy * xh, axis=0, keepdims=True)
        dxh = dy * g
        do2 = r * (dxh - _group_mean64(dxh) - xh * _group_mean64(dxh * xh))
        q2, k2, v2 = q_ref[...], k_ref[...], v_ref[...]
        dqs, dks, dvs = [], [], []
        for hh in range(2):
            sl = slice(hh * HEAD_DIM, (hh + 1) * HEAD_DIM)
            q, k, v, do = q2[:, sl], k2[:, sl], v2[:, sl], do2[:, sl].astype(BF)
            w = w_ref[hh]
            a = _dot_nt(q, k) * w
            dm = _dot_nt(do, v) * w
            sp, gs = st_ref[0, 0, hh], gstate[hh]
            qd = q.astype(F32) * qd_ref[hh]
            kd = k.astype(F32) * kd_ref[hh]
            dqs.append(_dot(dm, k) + _dot_nt(do, sp) * qd_ref[hh])
            dks.append(_dot_tn(dm, q) + _dot_nt(v, gs) * kd_ref[hh])
            dvs.append(_dot_tn(a, do) + _dot(kd, gs))
            gstate[hh] = gs * cd_ref[hh] + _dot_tn(qd, do)
        dq_ref[...] = jnp.concatenate(dqs, axis=-1)
        dk_ref[...] = jnp.concatenate(dks, axis=-1)
        dv_ref[...] = jnp.concatenate(dvs, axis=-1)

    blk = lambda col0: pl.BlockSpec((c, LANES), lambda hp, i: (n_b - 1 - i, col0 + hp))
    tab = lambda a: pl.BlockSpec((2,) + a.shape[1:], lambda hp, i: (hp, 0, 0))
    gspec = pl.BlockSpec((1, 1, LANES), lambda hp, i: (hp, 0, 0))
    return pl.pallas_call(
        body, name="retention_bwd", grid=(N_HEADS // 2, n_b),
        out_shape=(jax.ShapeDtypeStruct((t_len, GROUP_W), F32),) * 4 + (jax.ShapeDtypeStruct((N_HEADS // 2, 1, LANES), F32),),
        in_specs=[blk(0), blk(0), blk(g_col), gspec, blk(0), blk(0), blk(v_col),
                  pl.BlockSpec((1, 1, 2, HEAD_DIM, HEAD_DIM), lambda hp, i: (hp, n_b - 1 - i, 0, 0, 0)),
                  tab(wdec), tab(qdec), tab(kdec), tab(cdec)],
        out_specs=(blk(0), blk(0), blk(0), blk(0), gspec),
        scratch_shapes=[pltpu.VMEM((2, HEAD_DIM, HEAD_DIM), F32)],
        compiler_params=_cparams(("arbitrary", "arbitrary")),
    )(dmr, raw, proj, g_ret, rq, rk, proj, states, wdec, qdec, kdec, cdec)


def _in_proj_bwd(x, g_mix, dh1, dq_r, dk_r, dv_r, drg, dq_f, dk_f, dv_f, df_col, proj, z, cos_t, sin_t, gq_t, gk_t, w_in_t):
    t_len = x.shape[0]
    tm = min(ROW_TILE, t_len)
    n_t = t_len // tm

    def body(x_ref, g_ref, dh1_ref, dqr_ref, dkr_ref, dvr_ref, drg_ref, dqf_ref, dkf_ref, dvf_ref, df_ref, fq_ref, fk_ref, z_ref,
             cos_ref, sin_ref, gq_ref, gk_ref, wm_ref, wf_ref,
             dproj_ref, dz_ref, dx_ref, dg_ref, dgq_ref, dgk_ref, db_ref, carry, gq_acc, gk_acc):
        i = pl.program_id(0)

        @pl.when(i == 0)
        def _():
            carry[...] = jnp.zeros_like(carry)
            gq_acc[...] = jnp.zeros_like(gq_acc)
            gk_acc[...] = jnp.zeros_like(gk_acc)
            dg_ref[...] = jnp.zeros_like(dg_ref)
            db_ref[...] = jnp.zeros_like(db_ref)

        c, s = cos_ref[...], sin_ref[...]
        gq, gk = gq_ref[...], gk_ref[...]
        dgq = jnp.zeros((1, LANES), F32)
        dgk = jnp.zeros((1, LANES), F32)
        for sl in _chunks(GROUP_W):
            dy = dqr_ref[:, sl] * 0.125
            dproj_ref[:, sl] = (dy * c + _swap32(dy * s)).astype(BF)
            dy = dkr_ref[:, sl]
            dproj_ref[:, GROUP_W + sl.start:GROUP_W + sl.stop] = (dy * c + _swap32(dy * s)).astype(BF)
            dproj_ref[:, 2 * GROUP_W + sl.start:2 * GROUP_W + sl.stop] = dvr_ref[:, sl].astype(BF)
            dproj_ref[:, 3 * GROUP_W + sl.start:3 * GROUP_W + sl.stop] = drg_ref[:, sl].astype(BF)
            for src, dsrc, gain, off in ((fq_ref, dqf_ref, gq, 4), (fk_ref, dkf_ref, gk, 5)):
                xr = src[:, sl].astype(F32)
                r = lax.rsqrt(_group_mean64(xr * xr) + EPS)
                xh = xr * r
                dy = dsrc[:, sl] * (0.125 if off == 4 else 1.0)
                dgs = jnp.sum(dy * xh, axis=0, keepdims=True)
                if off == 4:
                    dgq = dgq + dgs
                else:
                    dgk = dgk + dgs
                dxh = dy * gain
                dproj_ref[:, off * GROUP_W + sl.start:off * GROUP_W + sl.stop] = \
                    (r * (dxh - xh * _group_mean64(dxh * xh))).astype(BF)
            dproj_ref[:, 6 * GROUP_W + sl.start:6 * GROUP_W + sl.stop] = dvf_ref[:, sl].astype(BF)
        gq_acc[...] += dgq
        gk_acc[...] += dgk
        row = lax.broadcasted_iota(jnp.int32, (tm, tm), 0)
        col = lax.broadcasted_iota(jnp.int32, (tm, tm), 1)
        dlf = _dot_exact((col >= row).astype(BF), df_ref[...]) + carry[0:1, :]
        carry[...] = jnp.broadcast_to(dlf[0:1, :], carry.shape)
        lane = lax.broadcasted_iota(jnp.int32, (tm, LANES), 1)
        dz = jnp.where(lane < N_HEADS, dlf / (1.0 + jnp.exp(z_ref[...])), 0.0)
        db_ref[...] += jnp.sum(dz, axis=0, keepdims=True)
        dz_bf = dz.astype(BF)
        dz_ref[...] = dz_bf
        dn1 = jnp.dot(dz_bf, wf_ref[...], preferred_element_type=F32)
        for sec in range(MAIN_W // GROUP_W):
            sl = slice(sec * GROUP_W, (sec + 1) * GROUP_W)
            dn1 = dn1 + jnp.dot(dproj_ref[:, sl], wm_ref[sl, :], preferred_element_type=F32)
        dx, dg = _rms_bwd(x_ref[...], g_ref[...], dn1)
        dx_ref[...] = dh1_ref[...] + dx
        dg_ref[...] += dg

        @pl.when(i == n_t - 1)
        def _():
            dgq_ref[...] = gq_acc[:, :HEAD_DIM] + gq_acc[:, HEAD_DIM:]
            dgk_ref[...] = gk_acc[:, :HEAD_DIM] + gk_acc[:, HEAD_DIM:]

    row_spec = lambda w, col=0: pl.BlockSpec((tm, w), lambda i: (n_t - 1 - i, col))
    full = lambda a: pl.BlockSpec(a.shape, lambda i: (0,) * a.ndim)
    acc = lambda r, c: pl.BlockSpec((r, c), lambda i: (0, 0))
    return pl.pallas_call(
        body, name="in_proj_bwd", grid=(n_t,),
        out_shape=(jax.ShapeDtypeStruct((t_len, MAIN_W), BF), jax.ShapeDtypeStruct((t_len, LANES), BF),
                   jax.ShapeDtypeStruct((t_len, D_MODEL), F32), jax.ShapeDtypeStruct((1, D_MODEL), F32),
                   jax.ShapeDtypeStruct((1, HEAD_DIM), F32), jax.ShapeDtypeStruct((1, HEAD_DIM), F32),
                   jax.ShapeDtypeStruct((1, LANES), F32)),
        in_specs=[row_spec(D_MODEL), full(g_mix), row_spec(D_MODEL)] + [row_spec(GROUP_W)] * 7
        + [row_spec(LANES), row_spec(GROUP_W, 4), row_spec(GROUP_W, 5), row_spec(LANES), row_spec(LANES), row_spec(LANES),
           full(gq_t), full(gk_t), *_w_in_specs()],
        out_specs=(row_spec(MAIN_W), row_spec(LANES), row_spec(D_MODEL), acc(1, D_MODEL), acc(1, HEAD_DIM), acc(1, HEAD_DIM),
                   acc(1, LANES)),
        scratch_shapes=[pltpu.VMEM((8, LANES), F32), pltpu.VMEM((1, LANES), F32), pltpu.VMEM((1, LANES), F32)],
        compiler_params=_cparams(("arbitrary",)),
    )(x, g_mix, dh1, dq_r, dk_r, dv_r, drg, dq_f, dk_f, dv_f, df_col, proj, proj, z, cos_t, sin_t, gq_t, gk_t, w_in_t, w_in_t)


def _matmul_tn(a, b, name, bk=512):
    t_len, m = a.shape
    n = b.shape[1]
    bm = m if m <= TN_MAX_ROWS else m // 2
    bk = min(bk, t_len)

    def body(a_ref, b_ref, o_ref):
        @pl.when(pl.program_id(1) == 0)
        def _():
            o_ref[...] = jnp.zeros_like(o_ref)

        o_ref[...] += _dot_tn(a_ref[...], b_ref[...])

    return pl.pallas_call(
        body, name=name, grid=(m // bm, t_len // bk),
        out_shape=jax.ShapeDtypeStruct((m, n), F32),
        in_specs=[pl.BlockSpec((bk, bm), lambda i, k: (k, i)), pl.BlockSpec((bk, n), lambda i, k: (k, 0))],
        out_specs=pl.BlockSpec((bm, n), lambda i, k: (i, 0)),
        compiler_params=_cparams(("arbitrary", "arbitrary")),
    )(a, b)


def _place():
    x, y, c = lax.axis_index("x"), lax.axis_index("y"), lax.axis_index("c")
    chips = [(1 - x, y), (x, 1 - y), (1 - x, 1 - y)]
    return x, y, c, chips


def _row_chunks(rows, limit):
    step = max(d for d in range(16, min(rows, limit) + 1, 16) if rows % d == 0)
    return [slice(i, i + step) for i in range(0, rows, step)]


ICI_CHUNK_ROWS = 128
D2D_CHUNK_ROWS = 64


def _gather_phase(phase, ins, outs, send_sems, recv_sems):
    x, y, c, chips = _place()
    me_chip = 2 * x + y
    sibling = (x, y, 1 - c)

    def copy(w, k, slot, half, to, rows=slice(None), src=None):
        dst = outs[w].at[slot, half, rows]
        return pltpu.make_async_remote_copy(src_ref=dst if src is None else src, dst_ref=dst,
                                            send_sem=send_sems.at[w, k], recv_sem=recv_sems.at[w, k],
                                            device_id=to, device_id_type=MESH)

    for w in range(len(ins)):
        for j, (px, py) in enumerate(chips):
            if phase == 0:
                for rows in _row_chunks(ins[w].shape[1], ICI_CHUNK_ROWS):
                    copy(w, j, me_chip, c, (px, py, c), rows, src=ins[w].at[c, rows]).start()
            elif phase == 1:
                copy(w, j, 2 * px + py, c, (x, y, c)).wait_recv()
                for rows in _row_chunks(ins[w].shape[1], D2D_CHUNK_ROWS):
                    copy(w, 3 + j, 2 * px + py, c, sibling, rows).start()
            else:
                copy(w, 3 + j, 2 * px + py, 1 - c, (x, y, c)).wait_recv()
                copy(w, j, me_chip, c, (px, py, c), src=ins[w].at[c]).wait_send()
                copy(w, 3 + j, 2 * px + py, c, sibling).wait_send()


def _gather_scratch(n_w):
    return [pltpu.SemaphoreType.DMA((n_w, 6)), pltpu.SemaphoreType.DMA((n_w, 6))]


def _all_gather_weights(shards):
    n_w = len(shards)

    def body(*refs):
        for phase in range(3):
            _gather_phase(phase, refs[:n_w], refs[n_w:2 * n_w], *refs[2 * n_w:])

    return pl.pallas_call(
        body, name="all_gather_weights",
        out_shape=tuple(jax.ShapeDtypeStruct((4,) + s.shape, s.dtype) for s in shards),
        in_specs=[ANY] * n_w, out_specs=(ANY,) * n_w, scratch_shapes=_gather_scratch(n_w),
    )(*shards)


def _exchange_core_halves(grads):
    n_w = len(grads)

    def body(*refs):
        ins, theirs = refs[:n_w], refs[n_w:2 * n_w]
        send_sems, recv_sems = refs[2 * n_w:]
        x, y, c, _ = _place()

        def remote(w, k=slice(None), rows=slice(None)):
            return pltpu.make_async_remote_copy(src_ref=ins[w].at[k, 1 - c, rows], dst_ref=theirs[w].at[k, rows],
                                                send_sem=send_sems.at[w], recv_sem=recv_sems.at[w], device_id=(x, y, 1 - c),
                                                device_id_type=MESH)

        for w in range(n_w):
            for k in range(4):
                for rows in _row_chunks(ins[w].shape[2], D2D_CHUNK_ROWS):
                    remote(w, k, rows).start()
        for w in range(n_w):
            remote(w).wait()

    half = tuple(jax.ShapeDtypeStruct((4,) + g.shape[2:], g.dtype) for g in grads)
    return pl.pallas_call(
        body, name="exchange_core_halves", out_shape=half,
        in_specs=[ANY] * n_w, out_specs=(ANY,) * n_w,
        scratch_shapes=[pltpu.SemaphoreType.DMA((n_w,)), pltpu.SemaphoreType.DMA((n_w,))],
    )(*grads)


def _add_pairs(part, theirs, name):
    _, _, r, c = part.shape
    rb = 32 if r % 32 == 0 else r

    def body(a_ref, b_ref, own_ref, ob_ref):
        my_chip = 2 * lax.axis_index("x") + lax.axis_index("y")
        ob_ref[...] = (a_ref[...] + b_ref[...]).astype(BF)
        own_ref[...] = a_ref[my_chip] + b_ref[my_chip]

    spec = pl.BlockSpec((4, rb, c), lambda i: (0, i, 0))
    return pl.pallas_call(
        body, name=name, grid=(r // rb,),
        out_shape=(jax.ShapeDtypeStruct((r, c), F32), jax.ShapeDtypeStruct((4, r, c), BF)),
        in_specs=[pl.BlockSpec((4, None, rb, c), lambda i: (0, lax.axis_index("c"), i, 0)), spec],
        out_specs=(pl.BlockSpec((rb, c), lambda i: (i, 0)), spec), compiler_params=_cparams(("arbitrary",)),
    )(part, theirs)


def _scatter_phase(phase, bfs, got, send_sems, recv_sems):
    x, y, c, chips = _place()

    def remote(w, j, px, py, rows=slice(None)):
        return pltpu.make_async_remote_copy(src_ref=bfs[w].at[2 * px + py, rows], dst_ref=got[w].at[j, rows],
                                            send_sem=send_sems.at[w, j], recv_sem=recv_sems.at[w, j], device_id=(px, py, c),
                                            device_id_type=MESH)

    for w in range(len(bfs)):
        for j, (px, py) in enumerate(chips):
            if phase == 0:
                for rows in _row_chunks(bfs[w].shape[1], ICI_CHUNK_ROWS):
                    remote(w, j, px, py, rows).start()
            else:
                remote(w, j, px, py).wait()


def _scatter_scratch(n_w):
    return [pltpu.SemaphoreType.DMA((n_w, 3)), pltpu.SemaphoreType.DMA((n_w, 3))]


def _scatter_out_shapes(sums_bf16):
    return tuple(jax.ShapeDtypeStruct((3,) + s.shape[1:], BF) for s in sums_bf16)


def _scatter_to_chips(sums_bf16):
    n_w = len(sums_bf16)

    def body(*refs):
        for phase in range(2):
            _scatter_phase(phase, refs[:n_w], refs[n_w:2 * n_w], *refs[2 * n_w:])

    return pl.pallas_call(
        body, name="scatter_to_chips", out_shape=_scatter_out_shapes(sums_bf16),
        in_specs=[ANY] * n_w, out_specs=(ANY,) * n_w, scratch_shapes=_scatter_scratch(n_w),
    )(*sums_bf16)


def _add_received(own, got, name):
    r, c = own.shape
    rb = 32 if r % 32 == 0 else r

    def body(o_ref, g_ref, out_ref):
        out_ref[...] = ((o_ref[...] + g_ref[0].astype(F32)) + g_ref[1].astype(F32)) + g_ref[2].astype(F32)

    return pl.pallas_call(
        body, name=name, grid=(r // rb,), out_shape=jax.ShapeDtypeStruct((r, c), F32),
        in_specs=[pl.BlockSpec((rb, c), lambda i: (i, 0)), pl.BlockSpec((3, rb, c), lambda i: (0, i, 0))],
        out_specs=pl.BlockSpec((rb, c), lambda i: (i, 0)), compiler_params=_cparams(("arbitrary",)),
    )(own, got)


def _share_with_sibling(halves):
    n_w = len(halves)

    def body(*refs):
        ins, outs = refs[:n_w], refs[n_w:2 * n_w]
        send_sems, recv_sems = refs[2 * n_w:]
        x, y, c, _ = _place()

        def remote(w, rows=slice(None)):
            return pltpu.make_async_remote_copy(src_ref=ins[w].at[rows], dst_ref=outs[w].at[c, rows], send_sem=send_sems.at[w],
                                                recv_sem=recv_sems.at[w], device_id=(x, y, 1 - c), device_id_type=MESH)

        for w in range(n_w):
            for rows in _row_chunks(ins[w].shape[0], D2D_CHUNK_ROWS):
                remote(w, rows).start()
        for w in range(n_w):
            remote(w).wait()

    return pl.pallas_call(
        body, name="share_with_sibling",
        out_shape=tuple(jax.ShapeDtypeStruct((2,) + h.shape, h.dtype) for h in halves),
        in_specs=[ANY] * n_w, out_specs=(ANY,) * n_w,
        scratch_shapes=[pltpu.SemaphoreType.DMA((n_w,)), pltpu.SemaphoreType.DMA((n_w,))],
    )(*halves)


def _all_reduce_small(pack):
    r, c = pack.shape

    def body(p_ref, out_ref, slots, send_sems, recv_sems):
        x, y, cc, _ = _place()
        me = 4 * x + 2 * y + cc
        slots[me] = p_ref[...]
        copies = []
        for k in range(1, 8):
            dx, dy, dc = (k >> 2) & 1, (k >> 1) & 1, k & 1
            to = (1 - x if dx else x, 1 - y if dy else y, 1 - cc if dc else cc)
            cp = pltpu.make_async_remote_copy(src_ref=p_ref, dst_ref=slots.at[me], send_sem=send_sems.at[k - 1],
                                              recv_sem=recv_sems.at[k - 1], device_id=to, device_id_type=MESH)
            cp.start()
            copies.append(cp)
        for cp in copies:
            cp.wait()
        total = slots[0]
        for d in range(1, 8):
            total = total + slots[d]
        out_ref[...] = total

    return pl.pallas_call(
        body, name="all_reduce_small", out_shape=jax.ShapeDtypeStruct((r, c), F32),
        in_specs=[VMEM_SPEC], out_specs=VMEM_SPEC,
        scratch_shapes=[pltpu.VMEM((8, r, c), F32), pltpu.SemaphoreType.DMA((7,)), pltpu.SemaphoreType.DMA((7,))],
    )(pack)


def _adamw(w, g, m, v, name):
    r, c = w.shape
    rb, cb = (64, c) if r % 64 == 0 else (r, LANES if (r % 8 and c % LANES == 0) else c)
    c1 = 1.0 - ADAM_B1 ** ADAM_STEP
    c2 = 1.0 - ADAM_B2 ** ADAM_STEP

    def body(w_ref, g_ref, m_ref, v_ref, d_ref, nm_ref, nv_ref):
        gv = g_ref[...]
        nm = ADAM_B1 * m_ref[...] + (1.0 - ADAM_B1) * gv
        nv = ADAM_B2 * v_ref[...] + (1.0 - ADAM_B2) * (gv * gv)
        nm_ref[...] = nm
        nv_ref[...] = nv
        d_ref[...] = -ADAM_LR * ((nm / c1) / (jnp.sqrt(nv / c2) + ADAM_EPS) + ADAM_WD * w_ref[...])

    spec = pl.BlockSpec((rb, cb), lambda i, j: (i, j))
    return pl.pallas_call(
        body, name=name, grid=(r // rb, c // cb), out_shape=(jax.ShapeDtypeStruct((r, c), F32),) * 3,
        in_specs=[spec] * 4, out_specs=(spec,) * 3, compiler_params=_cparams(("arbitrary", "arbitrary")),
    )(w, g, m, v)


def _rope_tables(t_len):
    inv_freq = ROPE_BASE ** (-jnp.arange(0, HEAD_DIM, 2, dtype=F32) / HEAD_DIM)
    ang = jnp.arange(t_len, dtype=F32)[:, None] * inv_freq[None, :]
    cos, sin = jnp.cos(ang), jnp.sin(ang)
    cos_t = jnp.concatenate([cos, cos, cos, cos], axis=-1)
    sin_t = jnp.concatenate([-sin, sin, -sin, sin], axis=-1)
    return cos_t, sin_t


def _cols_to_shards(dw):
    r, n = dw.shape
    return jnp.transpose(dw.reshape(2, r // 2, 4, n // 4), (2, 0, 1, 3))


def _rows_to_shards(dw):
    r, n = dw.shape
    padded = _pad_rows(dw.reshape(4, r // 4, n))
    return padded.reshape(4, 2, padded.shape[1] // 2, n)


def _pad_lanes(a):
    extra = -a.shape[-1] % LANES
    return a if extra == 0 else jnp.pad(a, [(0, 0)] * (a.ndim - 1) + [(0, extra)])


def _pad_rows(a):
    rows = a.shape[-2]
    extra = 0 if rows % SHARD_ROW_ALIGN == 0 else -rows % SHARD_ROW_PAD
    return a if extra == 0 else jnp.pad(a, [(0, 0)] * (a.ndim - 2) + [(0, extra), (0, 0)])


def _pad_row(a, width=D_MODEL):
    a = a.reshape(1, -1)
    return jnp.pad(a, ((0, 0), (0, width - a.shape[1])))


def kernel(x, mem, g_mix, w_in, b_forget, g_ret_out, g_fox_q, g_fox_k, w_out, g_xattn, w_xq, w_xkv, g_mem, g_xq, g_xk, w_xo, g_ffn, w_gate, w_up, w_down, loss_target, m_g_mix, m_w_in, m_b_forget, m_g_ret_out, m_g_fox_q, m_g_fox_k, m_w_out, m_g_xattn, m_w_xq, m_w_xkv, m_g_mem, m_g_xq, m_g_xk, m_w_xo, m_g_ffn, m_w_gate, m_w_up, m_w_down, v_g_mix, v_w_in, v_b_forget, v_g_ret_out, v_g_fox_q, v_g_fox_k, v_w_out, v_g_xattn, v_w_xq, v_w_xkv, v_g_mem, v_g_xq, v_g_xk, v_w_xo, v_g_ffn, v_w_gate, v_w_up, v_w_down):
    big = {"w_in": (w_in, m_w_in, v_w_in), "w_out": (w_out, m_w_out, v_w_out), "w_xq": (w_xq, m_w_xq, v_w_xq),
           "w_xkv": (w_xkv, m_w_xkv, v_w_xkv), "w_xo": (w_xo, m_w_xo, v_w_xo), "w_gate": (w_gate, m_w_gate, v_w_gate),
           "w_up": (w_up, m_w_up, v_w_up), "w_down": (w_down, m_w_down, v_w_down)}
    for n in TRANSPOSED:
        big[n] = tuple(jnp.swapaxes(a, 1, 2) for a in big[n])
    shards = {}
    for n in big:
        w = _pad_rows(_pad_lanes(big[n][0][0].astype(BF)))
        shards[n] = w.reshape(2, w.shape[0] // 2, w.shape[1])
    sizes = {n: big[n][0].shape[1:] for n in big}
    w_in_full = _assemble_weight("w_in", _all_gather_weights([shards["w_in"]])[0], shards["w_in"], sizes["w_in"])
    small_w ={"g_mix": g_mix, "b_forget": b_forget, "g_ret_out": g_ret_out, "g_fox_q": g_fox_q, "g_fox_k": g_fox_k,
               "g_xattn": g_xattn, "g_mem": g_mem, "g_xq": g_xq, "g_xk": g_xk, "g_ffn": g_ffn}
    m_small = {"g_mix": m_g_mix, "b_forget": m_b_forget, "g_ret_out": m_g_ret_out, "g_fox_q": m_g_fox_q, "g_fox_k": m_g_fox_k,
               "g_xattn": m_g_xattn, "g_mem": m_g_mem, "g_xq": m_g_xq, "g_xk": m_g_xk, "g_ffn": m_g_ffn}
    v_small = {"g_mix": v_g_mix, "b_forget": v_b_forget, "g_ret_out": v_g_ret_out, "g_fox_q": v_g_fox_q, "g_fox_k": v_g_fox_k,
               "g_xattn": v_g_xattn, "g_mem": v_g_mem, "g_xq": v_g_xq, "g_xk": v_g_xk, "g_ffn": v_g_ffn}
    loss_part, grad_x, sums, got, small_g = _local_step(x[0], mem[0], loss_target[0], w_in_full, shards, sizes, small_w)
    return _reduce_and_update(big, sums, got, small_w, small_g, loss_part, grad_x, m_small, v_small)


def _assemble_weight(name, gathered, own, size):
    rows, width = size
    my_chip = 2 * lax.axis_index("x") + lax.axis_index("y")
    g = lax.dynamic_update_slice(gathered, own[None], (my_chip, 0, 0, 0))
    g = g.reshape(4, 2 * g.shape[2], g.shape[3])[:, :rows, :width]
    return jnp.transpose(g, (1, 0, 2)).reshape(rows, 4 * width) if name in COL_SHARDED else g.reshape(4 * rows, width)


def _core_sums(names, dw):
    parts = [_pad_lanes(_cols_to_shards(dw[n]) if n in COL_SHARDED else _rows_to_shards(dw[n])) for n in names]
    theirs = _exchange_core_halves(parts)
    return [_add_pairs(p, t, f"core_sum_{n}") for n, p, t in zip(names, parts, theirs)]


def _local_step(xs, mems, tgt, w_in_full, shards, sizes, small_w):
    g_mix, b_forget, g_ret_out, g_fox_q, g_fox_k = (small_w[n] for n in ("g_mix", "b_forget", "g_ret_out", "g_fox_q", "g_fox_k"))
    g_xattn, g_mem, g_xq, g_xk, g_ffn = (small_w[n] for n in ("g_xattn", "g_mem", "g_xq", "g_xk", "g_ffn"))
    w_in_t = jnp.pad(w_in_full, ((0, MAIN_W + LANES - IN_W), (0, 0)))
    t_len = xs.shape[0]
    cos_t, sin_t = _rope_tables(t_len)
    tables = _decay_tables(min(RET_BLOCK, t_len))
    gq_t = jnp.concatenate([g_fox_q, g_fox_q], axis=-1)
    gk_t = jnp.concatenate([g_fox_k, g_fox_k], axis=-1)
    b_pad = _pad_row(b_forget, LANES)
    g_ret = g_ret_out.reshape(N_HEADS // 2, 1, LANES)

    n1, proj, rq, rk, q_aug, k_aug, z = _in_proj_fwd(xs, g_mix, w_in_t, b_pad, cos_t, sin_t, gq_t, gk_t)
    raw, mix_r, states = _retention_fwd(rq, rk, proj, g_ret, tables)
    mix_f, o32, lse, *gathered = _fox_fwd(q_aug, k_aug, proj, [shards[n] for n in LATE])
    full = {n: _assemble_weight(n, g, shards[n], sizes[n]) for n, g in zip(LATE, gathered)}
    memn, kraw, kn, vmem = _mem_kv_fwd(mems, g_mem, full["w_xkv"], g_xk)
    h1, hn2, qx, o_x, h2 = _attn_out_xattn_fwd(xs, mix_r, mix_f, full["w_out"], g_xattn, full["w_xq"], g_xq, kn, vmem, full["w_xo"])
    hn3, gate, up, act, dh3, loss_part = _ffn_loss_fwd(h2, g_ffn, full["w_gate"], full["w_up"], full["w_down"], tgt)

    dgate, dup, dh2, dg_ffn = _ffn_bwd(dh3, gate, up, h2, g_ffn, full["w_gate"], full["w_up"], full["w_down"])
    dqx, dh1, dmr, dmf, dkn, dvm, dg_xattn, dg_xq = _attn_out_xattn_bwd(dh2, h1, qx, kn, vmem, full["w_xo"], full["w_xq"],
                                                                      full["w_out"], g_xattn, g_xq)
    dw_xkv, dg_mem, dg_xk = _mem_kv_bwd(dkn, dvm, kraw, mems, memn, g_mem, g_xk, full["w_xkv"])
    dw = {
        "w_out": jnp.concatenate([_matmul_tn(mix_r, dh1, "dw_out_ret"), _matmul_tn(mix_f, dh1, "dw_out_fox")], axis=0),
        "w_xq": _matmul_tn(hn2, dqx, "dw_xq"),
        "w_xkv": dw_xkv,
        "w_xo": _matmul_tn(o_x, dh2, "dw_xo"),
        "w_gate": _matmul_tn(dgate, hn3, "dw_gate"),
        "w_up": _matmul_tn(dup, hn3, "dw_up"),
        "w_down": _matmul_tn(act, dh3, "dw_down"),
    }
    late_sums = _core_sums(LATE, dw)
    dq_f, dk_f, dv_f, df, *late_got = _fox_bwd(q_aug, k_aug, proj, dmf, o32, lse, [s[1] for s in late_sums])
    dq_r, dk_r, dv_r, drg, dg_ret = _retention_bwd(dmr, raw, proj, g_ret, rq, rk, states, tables)
    df_col = jnp.pad(jnp.transpose(df, (1, 0, 2)).reshape(t_len, N_HEADS), ((0, 0), (0, LANES - N_HEADS)))
    dproj, dz, grad_x, dg_mix, dg_fq, dg_fk, db = _in_proj_bwd(xs, g_mix, dh1, dq_r, dk_r, dv_r, drg, dq_f, dk_f, dv_f, df_col,
                                                              proj, z, cos_t, sin_t, gq_t, gk_t, w_in_t)

    dw_in = jnp.concatenate([_matmul_tn(dproj, n1, "dw_in_main"), _matmul_tn(dz, n1, "dw_in_ff")[:IN_W - MAIN_W]], axis=0)
    in_sums = _core_sums(("w_in",), {"w_in": dw_in})
    in_got = _scatter_to_chips([in_sums[0][1]])
    sums = {n: s[0] for n, s in zip(("w_in",) + LATE, in_sums + late_sums)}
    got = dict(zip(("w_in",) + LATE, list(in_got) + late_got))
    small_g = {"g_mix": dg_mix, "b_forget": db[:, :N_HEADS], "g_ret_out": dg_ret, "g_fox_q": dg_fq, "g_fox_k": dg_fk,
               "g_xattn": dg_xattn, "g_mem": dg_mem, "g_xq": dg_xq, "g_xk": dg_xk, "g_ffn": dg_ffn}
    return loss_part, grad_x, sums, got, small_g


def _reduce_and_update(big, sums, got, small_w, small_g, loss_part, grad_x, m_small, v_small):
    big_names = list(big)
    my_core = lax.axis_index("c")
    finals = [_add_received(sums[n], got[n], f"chip_sum_{n}") for n in big_names]
    shared = _share_with_sibling(finals)
    grads, deltas, new_m, new_v = {}, {}, {}, {}
    for n, s, fin in zip(big_names, shared, finals):
        w, m, v = big[n]
        s = lax.dynamic_update_slice(s, fin[None], (my_core, 0, 0))
        g = s.reshape(2 * s.shape[1], s.shape[2])[:w.shape[1], :w.shape[2]]
        d, nm, nv = _adamw(w[0], g, m[0], v[0], f"adamw_{n}")
        grads[n], deltas[n], new_m[n], new_v[n] = ((jnp.swapaxes(a[None], 1, 2) if n in TRANSPOSED else a[None]) for a in (g, d, nm, nv))

    small_names = list(small_w)
    pad_rows = SMALL_ROWS - len(small_names) - 1
    stack = lambda d: jnp.concatenate([_pad_row(d[n]) for n in small_names] + [jnp.zeros((pad_rows + 1, D_MODEL), F32)], axis=0)
    g_pack = jnp.concatenate([_pad_row(small_g[n]) for n in small_names] + [_pad_row(loss_part[0:1, 0:1])]
                             + [jnp.zeros((pad_rows, D_MODEL), F32)], axis=0)
    g_tot = _all_reduce_small(g_pack)
    d_s, m_s, v_s = _adamw(stack(small_w), g_tot, stack(m_small), stack(v_small), "adamw_small")
    for i, n in enumerate(small_names):
        shape = small_w[n].shape
        size = int(np.prod(shape))
        grads[n] = g_tot[i, :size].reshape(shape)
        deltas[n], new_m[n], new_v[n] = d_s[i, :size].reshape(shape), m_s[i, :size].reshape(shape), v_s[i, :size].reshape(shape)
    loss = g_tot[len(small_names), 0]

    order = ["g_mix", "w_in", "b_forget", "g_ret_out", "g_fox_q", "g_fox_k", "w_out", "g_xattn", "w_xq", "w_xkv", "g_mem", "g_xq",
             "g_xk", "w_xo", "g_ffn", "w_gate", "w_up", "w_down"]
    return (loss, grad_x[None], *[grads[n] for n in order], *[deltas[n] for n in order], *[new_m[n] for n in order],
            *[new_v[n] for n in order])
```

```python
import functools

import numpy as np
import jax
import jax.numpy as jnp
from jax import lax
from jax.experimental import pallas as pl
from jax.experimental.pallas import tpu as pltpu

F32 = jnp.float32
BF = jnp.bfloat16

D_MODEL = 1024
HEAD_DIM = 64
N_HEADS = 8
GROUP_W = 512
N_XH = 4
XHD = 256
D_FF = 2816
MAIN_W = 3584
IN_W = 3592
ROPE_BASE = 10000.0
LOG2E = 1.4426950408889634
LN2 = 0.6931471805599453
EPS = 1e-6
NEG = -1e30
LANES = 128
RET_BLOCK = 256
REF_CHUNK = 64
ROW_TILE = 256
ATT_BLOCK = 256
TN_MAX_ROWS = 1408
SMALL_ROWS = 16
COL_SHARDED = ("w_xkv",)
TRANSPOSED = ("w_in", "w_gate", "w_up")
SHARD_ROW_ALIGN = 32
SHARD_ROW_PAD = 256
LATE = ("w_out", "w_xq", "w_xkv", "w_xo", "w_gate", "w_up", "w_down")
VMEM_LIMIT = 56 * 1024 * 1024

ADAM_LR = 0.001
ADAM_B1 = 0.9
ADAM_B2 = 0.999
ADAM_EPS = 1e-08
ADAM_WD = 0.01
ADAM_STEP = 10

MESH = pl.DeviceIdType.MESH
ANY = pl.BlockSpec(memory_space=pl.ANY)
VMEM_SPEC = pl.BlockSpec(memory_space=pltpu.VMEM)


def _cparams(sem=None, vmem=VMEM_LIMIT):
    return pltpu.CompilerParams(dimension_semantics=sem, vmem_limit_bytes=vmem)


def _dot(a, b):
    return jnp.dot(a.astype(BF), b.astype(BF), preferred_element_type=F32)


def _dot_nt(a, b):
    return lax.dot_general(a.astype(BF), b.astype(BF), (((1,), (1,)), ((), ())), preferred_element_type=F32)


def _dot_tn(a, b):
    return lax.dot_general(a.astype(BF), b.astype(BF), (((0,), (0,)), ((), ())), preferred_element_type=F32)


def _split3(x):
    hi = x.astype(BF)
    r = x - hi.astype(F32)
    mid = r.astype(BF)
    lo = (r - mid.astype(F32)).astype(BF)
    return hi, mid, lo


def _dot_exact(ind, x):
    hi, mid, lo = _split3(x)
    return (jnp.dot(ind, lo, preferred_element_type=F32) + jnp.dot(ind, mid, preferred_element_type=F32)
            + jnp.dot(ind, hi, preferred_element_type=F32))


def _dot_nt_exact(ind, x):
    hi, mid, lo = _split3(x)
    dn = (((1,), (1,)), ((), ()))
    return (lax.dot_general(ind, lo, dn, preferred_element_type=F32) + lax.dot_general(ind, mid, dn, preferred_element_type=F32)
            + lax.dot_general(ind, hi, dn, preferred_element_type=F32))


def _sigmoid(x):
    return 1.0 / (1.0 + jnp.exp(-x))


def _rms_fwd(x, g):
    r = lax.rsqrt(jnp.mean(x * x, axis=-1, keepdims=True) + EPS)
    return x * r * g


def _rms_bwd(x, g, dy):
    r = lax.rsqrt(jnp.mean(x * x, axis=-1, keepdims=True) + EPS)
    xh = x * r
    dg = jnp.sum(dy * xh, axis=0, keepdims=True)
    dxh = dy * g
    dx = r * (dxh - xh * jnp.mean(dxh * xh, axis=-1, keepdims=True))
    return dx, dg


def _group_mean64(x):
    lane = lax.broadcasted_iota(jnp.int32, x.shape, 1)
    lo = lane < HEAD_DIM
    s_lo = jnp.sum(jnp.where(lo, x, 0.0), axis=-1, keepdims=True)
    s_hi = jnp.sum(jnp.where(lo, 0.0, x), axis=-1, keepdims=True)
    return jnp.where(lo, s_lo, s_hi) * (1.0 / HEAD_DIM)


def _swap32(x):
    lane = lax.broadcasted_iota(jnp.int32, x.shape, 1)
    first = (lane % HEAD_DIM) < (HEAD_DIM // 2)
    return jnp.where(first, pltpu.roll(x, LANES - HEAD_DIM // 2, axis=1), pltpu.roll(x, HEAD_DIM // 2, axis=1))


def _chunks(w):
    return [slice(j * LANES, (j + 1) * LANES) for j in range(w // LANES)]


def _aug_pair(qk, f_cols, is_query):
    lane = lax.broadcasted_iota(jnp.int32, qk.shape, 1)
    a = lane - HEAD_DIM
    values = (qk, pltpu.roll(qk, HEAD_DIM, axis=1))
    out = []
    for hh in range(2):
        hi, mid, lo = (p.astype(F32) for p in _split3(f_cols[hh] * LOG2E))
        if is_query:
            aux = jnp.where(a == 0, hi, jnp.where(a == 1, mid, jnp.where(a == 2, lo, jnp.where(a < 6, 1.0, 0.0))))
        else:
            aux = jnp.where(a < 3, 1.0, jnp.where(a == 3, -hi, jnp.where(a == 4, -mid, jnp.where(a == 5, -lo, 0.0))))
        out.append(jnp.where(a < 0, values[hh], aux))
    return jnp.concatenate(out, axis=-1).astype(BF)


def _mem_kv_fwd(mem, g_mem, w_xkv, g_xk):
    m_tok = mem.shape[0]

    def body(mem_ref, gm_ref, w_ref, gk_ref, memn_ref, kraw_ref, kn_ref, v_ref):
        mn = _rms_fwd(mem_ref[...], gm_ref[...]).astype(BF)
        memn_ref[...] = mn
        kv = jnp.dot(mn, w_ref[...], preferred_element_type=F32)
        k = kv[:, :D_MODEL]
        kraw_ref[...] = k
        v_ref[...] = kv[:, D_MODEL:].astype(BF)
        for h in range(N_XH):
            sl = slice(h * XHD, (h + 1) * XHD)
            kn_ref[:, sl] = _rms_fwd(k[:, sl], gk_ref[...]).astype(BF)

    return pl.pallas_call(
        body, name="mem_kv_fwd",
        out_shape=(jax.ShapeDtypeStruct((m_tok, D_MODEL), BF), jax.ShapeDtypeStruct((m_tok, D_MODEL), F32),
                   jax.ShapeDtypeStruct((m_tok, D_MODEL), BF), jax.ShapeDtypeStruct((m_tok, D_MODEL), BF)),
        in_specs=[VMEM_SPEC] * 4, out_specs=(VMEM_SPEC,) * 4, compiler_params=_cparams(),
    )(mem, g_mem, w_xkv, g_xk)


def _in_proj_fwd(x, g_mix, w_in_t, b_pad, cos_t, sin_t, gq_t, gk_t):
    t_len = x.shape[0]
    tm = min(ROW_TILE, t_len)
    n_t = t_len // tm

    def body(x_ref, g_ref, wm_ref, wf_ref, b_ref, cos_ref, sin_ref, gq_ref, gk_ref,
             n1_ref, proj_ref, rq_ref, rk_ref, qa_ref, ka_ref, z_ref, carry):
        i = pl.program_id(0)

        @pl.when(i == 0)
        def _():
            carry[...] = jnp.zeros_like(carry)

        n1 = _rms_fwd(x_ref[...], g_ref[...]).astype(BF)
        n1_ref[...] = n1
        proj = _dot_nt(n1, wm_ref[...])
        proj_ref[...] = proj.astype(BF)
        z = _dot_nt(n1, wf_ref[...]) + b_ref[...]
        z_ref[...] = z
        lane = lax.broadcasted_iota(jnp.int32, z.shape, 1)
        lf = jnp.where(lane < N_HEADS, jnp.minimum(z, 0.0) - jnp.log(1.0 + jnp.exp(-jnp.abs(z))), 0.0)
        row = lax.broadcasted_iota(jnp.int32, (tm, tm), 0)
        col = lax.broadcasted_iota(jnp.int32, (tm, tm), 1)
        tri = (row >= col).astype(BF)
        fc = _dot_exact(tri, lf) + carry[0:1, :]
        carry[...] = jnp.broadcast_to(fc[tm - 1:tm, :], carry.shape)
        c, s = cos_ref[...], sin_ref[...]
        for j, sl in enumerate(_chunks(GROUP_W)):
            q = proj[:, sl]
            rq_ref[:, sl] = ((q * c + _swap32(q) * s) * 0.125).astype(BF)
            k = proj[:, GROUP_W + j * LANES:GROUP_W + (j + 1) * LANES]
            rk_ref[:, sl] = (k * c + _swap32(k) * s).astype(BF)
            f_cols = [fc[:, 2 * j:2 * j + 1], fc[:, 2 * j + 1:2 * j + 2]]
            fq = proj[:, 4 * GROUP_W + j * LANES:4 * GROUP_W + (j + 1) * LANES]
            fq = fq * lax.rsqrt(_group_mean64(fq * fq) + EPS) * gq_ref[...] * (0.125 * LOG2E)
            qa_ref[:, 2 * j * LANES:2 * (j + 1) * LANES] = _aug_pair(fq, f_cols, True)
            fk = proj[:, 5 * GROUP_W + j * LANES:5 * GROUP_W + (j + 1) * LANES]
            fk = fk * lax.rsqrt(_group_mean64(fk * fk) + EPS) * gk_ref[...]
            ka_ref[:, 2 * j * LANES:2 * (j + 1) * LANES] = _aug_pair(fk, f_cols, False)

    row_spec = lambda w: pl.BlockSpec((tm, w), lambda i: (i, 0))
    full = lambda a: pl.BlockSpec(a.shape, lambda i: (0,) * a.ndim)
    return pl.pallas_call(
        body, name="in_proj_fwd", grid=(n_t,),
        out_shape=(jax.ShapeDtypeStruct((t_len, D_MODEL), BF), jax.ShapeDtypeStruct((t_len, MAIN_W), BF),
                   jax.ShapeDtypeStruct((t_len, GROUP_W), BF), jax.ShapeDtypeStruct((t_len, GROUP_W), BF),
                   jax.ShapeDtypeStruct((t_len, 2 * GROUP_W), BF), jax.ShapeDtypeStruct((t_len, 2 * GROUP_W), BF),
                   jax.ShapeDtypeStruct((t_len, LANES), F32)),
        in_specs=[row_spec(D_MODEL), full(g_mix), *_w_in_specs(), full(b_pad), row_spec(LANES), row_spec(LANES),
                  full(gq_t), full(gk_t)],
        out_specs=(row_spec(D_MODEL), row_spec(MAIN_W), row_spec(GROUP_W), row_spec(GROUP_W), row_spec(2 * GROUP_W),
                   row_spec(2 * GROUP_W), row_spec(LANES)),
        scratch_shapes=[pltpu.VMEM((8, LANES), F32)],
        compiler_params=_cparams(("arbitrary",)),
    )(x, g_mix, w_in_t, w_in_t, b_pad, cos_t, sin_t, gq_t, gk_t)


def _w_in_specs():
    return (pl.BlockSpec((MAIN_W, D_MODEL), lambda i: (0, 0)), pl.BlockSpec((LANES, D_MODEL), lambda i: (MAIN_W // LANES, 0)))


def _decay_tables(c):
    h = np.arange(N_HEADS, dtype=np.float64)
    lg = np.log(1.0 - 2.0 ** (-5.0 - h)).astype(np.float32).astype(np.float64)
    t = np.arange(c)
    same_or_earlier = (t[None, :] // REF_CHUNK) <= (t[:, None] // REF_CHUNK)
    w = np.where(same_or_earlier[None], np.exp(lg[:, None, None] * np.abs(t[:, None] - t[None, :])[None]), 0.0)
    qd = np.exp(lg[:, None] * (t[None, :] + 1.0))
    kd = np.exp(lg[:, None] * (c - 1.0 - t[None, :]))
    cd = np.exp(lg * c)
    ones = np.ones((1, 1, HEAD_DIM))
    return (jnp.asarray(w, F32), jnp.asarray(qd[:, :, None] * ones, F32), jnp.asarray(kd[:, :, None] * ones, F32),
            jnp.asarray(cd[:, None, None] * np.ones((1, HEAD_DIM, HEAD_DIM)), F32))


def _retention_fwd(rq, rk, proj, g_ret, tables):
    t_len = rq.shape[0]
    c = min(RET_BLOCK, t_len)
    n_b = t_len // c
    wdec, qdec, kdec, cdec = tables
    v_col, g_col = 2 * GROUP_W // LANES, 3 * GROUP_W // LANES

    def body(q_ref, k_ref, v_ref, rg_ref, g_ref, w_ref, qd_ref, kd_ref, cd_ref, raw_ref, mix_ref, st_ref, state):
        i = pl.program_id(1)

        @pl.when(i == 0)
        def _():
            state[...] = jnp.zeros_like(state)

        q2, k2, v2 = q_ref[...], k_ref[...], v_ref[...]
        outs = []
        for hh in range(2):
            sl = slice(hh * HEAD_DIM, (hh + 1) * HEAD_DIM)
            q, k, v = q2[:, sl], k2[:, sl], v2[:, sl]
            sp = state[hh]
            st_ref[0, 0, hh] = sp
            a = _dot_nt(q, k) * w_ref[hh]
            o = _dot(a, v) + _dot(q.astype(F32) * qd_ref[hh], sp)
            state[hh] = sp * cd_ref[hh] + _dot_tn(k.astype(F32) * kd_ref[hh], v)
            outs.append(o)
        o2 = jnp.concatenate(outs, axis=-1)
        raw_ref[...] = o2
        xc = o2 - _group_mean64(o2)
        xh = xc * lax.rsqrt(_group_mean64(xc * xc) + EPS)
        gate = rg_ref[...].astype(F32)
        mix_ref[...] = (gate * _sigmoid(gate) * (xh * g_ref[0])).astype(BF)

    blk = lambda col0: pl.BlockSpec((c, LANES), lambda hp, i: (i, col0 + hp))
    tab = lambda a: pl.BlockSpec((2,) + a.shape[1:], lambda hp, i: (hp, 0, 0))
    return pl.pallas_call(
        body, name="retention_fwd", grid=(N_HEADS // 2, n_b),
        out_shape=(jax.ShapeDtypeStruct((t_len, GROUP_W), F32), jax.ShapeDtypeStruct((t_len, GROUP_W), BF),
                   jax.ShapeDtypeStruct((N_HEADS // 2, n_b, 2, HEAD_DIM, HEAD_DIM), F32)),
        in_specs=[blk(0), blk(0), blk(v_col), blk(g_col), pl.BlockSpec((1, 1, LANES), lambda hp, i: (hp, 0, 0)),
                  tab(wdec), tab(qdec), tab(kdec), tab(cdec)],
        out_specs=(blk(0), blk(0), pl.BlockSpec((1, 1, 2, HEAD_DIM, HEAD_DIM), lambda hp, i: (hp, i, 0, 0, 0))),
        scratch_shapes=[pltpu.VMEM((2, HEAD_DIM, HEAD_DIM), F32)],
        compiler_params=_cparams(("arbitrary", "arbitrary")),
    )(rq, rk, proj, proj, g_ret, wdec, qdec, kdec, cdec)


def _fox_fwd(q_aug, k_aug, proj, shards):
    t_len = q_aug.shape[0]
    tq = min(ATT_BLOCK, t_len)
    n_q = t_len // tq
    v_col = 6 * GROUP_W // LANES
    tc = min(512, t_len)
    n_w = len(shards)
    n_steps = (N_HEADS // 2) * n_q

    def body(*refs):
        q_ref, k_ref, v_ref = refs[:3]
        o_ref, o32_ref, lse_ref = refs[3 + n_w:6 + n_w]
        vt = refs[6 + 2 * n_w]
        comm = (refs[3:3 + n_w], refs[6 + n_w:6 + 2 * n_w]) + tuple(refs[7 + 2 * n_w:])
        i = pl.program_id(1)
        step = pl.program_id(0) * n_q + i

        @pl.when(step == 0)
        def _():
            _gather_phase(0, *comm)

        @pl.when(step == (3 * n_steps) // 4)
        def _():
            _gather_phase(1, *comm)

        @pl.when(i == 0)
        def _():
            for c0 in range(0, t_len, tc):
                vt[:, c0:c0 + tc] = v_ref[c0:c0 + tc, :].T

        qs = [q_ref[:, hh * LANES:(hh + 1) * LANES] for hh in range(2)]
        ones = jnp.ones((HEAD_DIM, tq), BF)

        def scores(j):
            k2 = k_ref[pl.ds(pl.multiple_of(j * tq, tq), tq), :]
            return tuple(_dot_nt(k2[:, hh * LANES:(hh + 1) * LANES], qs[hh]) for hh in range(2))

        def update(j, ss, carry, masked):
            v2 = vt[:, pl.ds(pl.multiple_of(j * tq, tq), tq)]
            ps, stats = [], []
            for hh in range(2):
                m = carry[hh][0]
                s_t = ss[hh]
                if masked:
                    krow = lax.broadcasted_iota(jnp.int32, (tq, tq), 0)
                    qcol = lax.broadcasted_iota(jnp.int32, (tq, tq), 1)
                    s_t = jnp.where(qcol >= krow, s_t, NEG)
                m_new = jnp.maximum(m, jnp.max(s_t, axis=0, keepdims=True))
                ps.append(jnp.exp2(s_t - m_new).astype(BF))
                stats.append((m_new, jnp.exp2(m - m_new)))
            out = []
            for hh in range(2):
                m_new, alpha = stats[hh]
                v_aug = jnp.concatenate([v2[hh * HEAD_DIM:(hh + 1) * HEAD_DIM, :], ones], axis=0)
                out.append((m_new, carry[hh][1] * alpha + jnp.dot(v_aug, ps[hh], preferred_element_type=F32)))
            return tuple(out)

        def advance(j, state):
            ss, carry = state
            return scores(j + 1), update(j, ss, carry, False)

        init = tuple((jnp.full((1, tq), NEG, F32), jnp.zeros((LANES, tq), F32)) for _ in range(2))
        ss, carry = lax.fori_loop(0, i, advance, (scores(0), init))
        carry = update(i, ss, carry, True)
        outs, lses = [], []
        for hh in range(2):
            m, acc = carry[hh]
            l = acc[HEAD_DIM:HEAD_DIM + 1, :]
            outs.append(acc[:HEAD_DIM, :] / l)
            lses.append(m + jnp.log2(l))
        o2 = jnp.concatenate(outs, axis=0).T
        o32_ref[...] = o2
        o_ref[...] = o2.astype(BF)
        lse_ref[0] = jnp.concatenate(lses, axis=0)

        @pl.when(step == n_steps - 1)
        def _():
            _gather_phase(2, *comm)

    return pl.pallas_call(
        body, name="fox_fwd", grid=(N_HEADS // 2, n_q),
        out_shape=(jax.ShapeDtypeStruct((t_len, GROUP_W), BF), jax.ShapeDtypeStruct((t_len, GROUP_W), F32),
                   jax.ShapeDtypeStruct((N_HEADS // 2, 2, t_len), F32))
        + tuple(jax.ShapeDtypeStruct((4,) + s.shape, s.dtype) for s in shards),
        in_specs=[pl.BlockSpec((tq, 2 * LANES), lambda hp, i: (i, hp)),
                  pl.BlockSpec((t_len, 2 * LANES), lambda hp, i: (0, hp)),
                  pl.BlockSpec((t_len, LANES), lambda hp, i: (0, v_col + hp))] + [ANY] * n_w,
        out_specs=(pl.BlockSpec((tq, LANES), lambda hp, i: (i, hp)), pl.BlockSpec((tq, LANES), lambda hp, i: (i, hp)),
                   pl.BlockSpec((1, 2, tq), lambda hp, i: (hp, 0, i))) + (ANY,) * n_w,
        scratch_shapes=[pltpu.VMEM((LANES, t_len), BF)] + _gather_scratch(n_w),
        compiler_params=_cparams(("arbitrary", "arbitrary")),
    )(q_aug, k_aug, proj, *shards)


def _softmax_rows(s):
    p = jnp.exp(s - jnp.max(s, axis=-1, keepdims=True))
    return p / jnp.sum(p, axis=-1, keepdims=True)


def _attn_out_xattn_fwd(x, mix_r, mix_f, w_out, g_xattn, w_xq, g_xq, kn, v, w_xo):
    t_len = x.shape[0]
    tm = min(ROW_TILE, t_len)

    def body(x_ref, mr_ref, mf_ref, wo_ref, g_ref, wq_ref, gq_ref, kn_ref, v_ref, wxo_ref,
             h1_ref, hn_ref, qx_ref, o_ref, h2_ref):
        h1 = x_ref[...] + jnp.dot(mr_ref[...], wo_ref[:GROUP_W, :], preferred_element_type=F32) \
            + jnp.dot(mf_ref[...], wo_ref[GROUP_W:, :], preferred_element_type=F32)
        h1_ref[...] = h1
        hn = _rms_fwd(h1, g_ref[...]).astype(BF)
        hn_ref[...] = hn
        qx = jnp.dot(hn, wq_ref[...], preferred_element_type=F32).astype(BF)
        qx_ref[...] = qx
        for h in range(N_XH):
            sl = slice(h * XHD, (h + 1) * XHD)
            qn = _rms_fwd(qx[:, sl].astype(F32), gq_ref[...])
            p = _softmax_rows(_dot_nt(qn, kn_ref[:, sl]) * (XHD ** -0.5))
            o_ref[:, sl] = _dot(p, v_ref[:, sl]).astype(BF)
        h2_ref[...] = h1 + jnp.dot(o_ref[...], wxo_ref[...], preferred_element_type=F32)

    row_spec = lambda w: pl.BlockSpec((tm, w), lambda i: (i, 0))
    full = lambda a: pl.BlockSpec(a.shape, lambda i: (0,) * a.ndim)
    return pl.pallas_call(
        body, name="attn_out_xattn_fwd", grid=(t_len // tm,),
        out_shape=(jax.ShapeDtypeStruct((t_len, D_MODEL), F32), jax.ShapeDtypeStruct((t_len, D_MODEL), BF),
                   jax.ShapeDtypeStruct((t_len, D_MODEL), BF), jax.ShapeDtypeStruct((t_len, D_MODEL), BF),
                   jax.ShapeDtypeStruct((t_len, D_MODEL), F32)),
        in_specs=[row_spec(D_MODEL), row_spec(GROUP_W), row_spec(GROUP_W), full(w_out), full(g_xattn), full(w_xq), full(g_xq),
                  full(kn), full(v), full(w_xo)],
        out_specs=(row_spec(D_MODEL),) * 5,
        compiler_params=_cparams(("arbitrary",)),
    )(x, mix_r, mix_f, w_out, g_xattn, w_xq, g_xq, kn, v, w_xo)


def _ffn_loss_fwd(h2, g_ffn, w_gate, w_up, w_down, target):
    t_len = h2.shape[0]
    tm = min(ROW_TILE, t_len)

    def body(h2_ref, g_ref, wg_ref, wu_ref, wd_ref, tgt_ref, hn_ref, gate_ref, up_ref, act_ref, dh3_ref, loss_ref):
        @pl.when(pl.program_id(0) == 0)
        def _():
            loss_ref[...] = jnp.zeros_like(loss_ref)

        h2v = h2_ref[...]
        hn = _rms_fwd(h2v, g_ref[...]).astype(BF)
        hn_ref[...] = hn
        gate = _dot_nt(hn, wg_ref[...])
        up = _dot_nt(hn, wu_ref[...])
        gate_ref[...] = gate.astype(BF)
        up_ref[...] = up.astype(BF)
        act = (gate * _sigmoid(gate) * up).astype(BF)
        act_ref[...] = act
        diff = h2v + jnp.dot(act, wd_ref[...], preferred_element_type=F32) - tgt_ref[...]
        dh3_ref[...] = diff * (1.0 / D_MODEL)
        per_row = jnp.sum(diff * diff, axis=-1, keepdims=True) * (1.0 / D_MODEL)
        loss_ref[...] += 0.5 * jnp.sum(per_row, axis=0, keepdims=True)

    row_spec = lambda w: pl.BlockSpec((tm, w), lambda i: (i, 0))
    full = lambda a: pl.BlockSpec(a.shape, lambda i: (0,) * a.ndim, pipeline_mode=pl.Buffered(1))
    return pl.pallas_call(
        body, name="ffn_loss_fwd", grid=(t_len // tm,),
        out_shape=(jax.ShapeDtypeStruct((t_len, D_MODEL), BF), jax.ShapeDtypeStruct((t_len, D_FF), BF),
                   jax.ShapeDtypeStruct((t_len, D_FF), BF), jax.ShapeDtypeStruct((t_len, D_FF), BF),
                   jax.ShapeDtypeStruct((t_len, D_MODEL), F32), jax.ShapeDtypeStruct((8, LANES), F32)),
        in_specs=[row_spec(D_MODEL), full(g_ffn), full(w_gate), full(w_up), full(w_down), row_spec(D_MODEL)],
        out_specs=(row_spec(D_MODEL), row_spec(D_FF), row_spec(D_FF), row_spec(D_FF), row_spec(D_MODEL),
                   pl.BlockSpec((8, LANES), lambda i: (0, 0))),
        compiler_params=_cparams(("arbitrary",)),
    )(h2, g_ffn, w_gate, w_up, w_down, target)


def _ffn_bwd(dh3, gate, up, h2, g_ffn, w_gate, w_up, w_down):
    t_len = h2.shape[0]
    tm = min(ROW_TILE, t_len)

    def body(dh3_ref, gate_ref, up_ref, h2_ref, g_ref, wg_ref, wu_ref, wd_ref, dgate_ref, dup_ref, dh2_ref, dg_ref):
        @pl.when(pl.program_id(0) == 0)
        def _():
            dg_ref[...] = jnp.zeros_like(dg_ref)

        dh3v = dh3_ref[...]
        dact = _dot_nt(dh3v, wd_ref[...])
        g = gate_ref[...].astype(F32)
        sg = _sigmoid(g)
        dup = (dact * (g * sg)).astype(BF)
        dgate = (dact * up_ref[...].astype(F32) * (sg * (1.0 + g * (1.0 - sg)))).astype(BF)
        dup_ref[...] = dup
        dgate_ref[...] = dgate
        dhn = jnp.dot(dgate, wg_ref[...], preferred_element_type=F32) + jnp.dot(dup, wu_ref[...], preferred_element_type=F32)
        dx, dg = _rms_bwd(h2_ref[...], g_ref[...], dhn)
        dh2_ref[...] = dh3v + dx
        dg_ref[...] += dg

    row_spec = lambda w: pl.BlockSpec((tm, w), lambda i: (i, 0))
    full = lambda a: pl.BlockSpec(a.shape, lambda i: (0,) * a.ndim, pipeline_mode=pl.Buffered(1))
    return pl.pallas_call(
        body, name="ffn_bwd", grid=(t_len // tm,),
        out_shape=(jax.ShapeDtypeStruct((t_len, D_FF), BF), jax.ShapeDtypeStruct((t_len, D_FF), BF),
                   jax.ShapeDtypeStruct((t_len, D_MODEL), F32), jax.ShapeDtypeStruct((1, D_MODEL), F32)),
        in_specs=[row_spec(D_MODEL), row_spec(D_FF), row_spec(D_FF), row_spec(D_MODEL), full(g_ffn), full(w_gate), full(w_up),
                  full(w_down)],
        out_specs=(row_spec(D_FF), row_spec(D_FF), row_spec(D_MODEL), pl.BlockSpec((1, D_MODEL), lambda i: (0, 0))),
        compiler_params=_cparams(("arbitrary",)),
    )(dh3, gate, up, h2, g_ffn, w_gate, w_up, w_down)


def _attn_out_xattn_bwd(dh2, h1, qx, kn, v, w_xo, w_xq, w_out, g_xattn, g_xq):
    t_len = h1.shape[0]
    tm = min(ROW_TILE, t_len)
    m_tok = kn.shape[0]

    def body(dh2_ref, h1_ref, qx_ref, kn_ref, v_ref, wxo_ref, wq_ref, wo_ref, g_ref, gq_ref,
             dqx_ref, dh1_ref, dmr_ref, dmf_ref, dkn_ref, dv_ref, dg_ref, dgq_ref, dqx_scr):
        @pl.when(pl.program_id(0) == 0)
        def _():
            dkn_ref[...] = jnp.zeros_like(dkn_ref)
            dv_ref[...] = jnp.zeros_like(dv_ref)
            dg_ref[...] = jnp.zeros_like(dg_ref)
            dgq_ref[...] = jnp.zeros_like(dgq_ref)

        dh2v = dh2_ref[...]
        do = _dot_nt(dh2v, wxo_ref[...])
        gq = gq_ref[...]
        dgq = jnp.zeros((1, XHD), F32)
        for h in range(N_XH):
            sl = slice(h * XHD, (h + 1) * XHD)
            qraw = qx_ref[:, sl].astype(F32)
            qn = _rms_fwd(qraw, gq)
            p = _softmax_rows(_dot_nt(qn, kn_ref[:, sl]) * (XHD ** -0.5))
            doh = do[:, sl]
            dv_ref[:, sl] += _dot_tn(p, doh)
            dp = _dot_nt(doh, v_ref[:, sl])
            ds = p * (dp - jnp.sum(dp * p, axis=-1, keepdims=True)) * (XHD ** -0.5)
            dqn = _dot(ds, kn_ref[:, sl])
            dkn_ref[:, sl] += _dot_tn(ds, qn)
            dx, dg_h = _rms_bwd(qraw, gq, dqn)
            dgq = dgq + dg_h
            dqx_scr[:, sl] = dx.astype(BF)
        dgq_ref[...] += dgq
        dqx = dqx_scr[...]
        dqx_ref[...] = dqx
        dhn = _dot_nt(dqx, wq_ref[...])
        dx, dg = _rms_bwd(h1_ref[...], g_ref[...], dhn)
        dg_ref[...] += dg
        dh1 = dh2v + dx
        dh1_ref[...] = dh1
        dmix = _dot_nt(dh1, wo_ref[...])
        dmr_ref[...] = dmix[:, :GROUP_W]
        dmf_ref[...] = dmix[:, GROUP_W:].astype(BF)

    row_spec = lambda w: pl.BlockSpec((tm, w), lambda i: (i, 0))
    full = lambda a: pl.BlockSpec(a.shape, lambda i: (0,) * a.ndim)
    acc = lambda r, c: pl.BlockSpec((r, c), lambda i: (0, 0))
    return pl.pallas_call(
        body, name="attn_out_xattn_bwd", grid=(t_len // tm,),
        out_shape=(jax.ShapeDtypeStruct((t_len, D_MODEL), BF), jax.ShapeDtypeStruct((t_len, D_MODEL), F32),
                   jax.ShapeDtypeStruct((t_len, GROUP_W), F32), jax.ShapeDtypeStruct((t_len, GROUP_W), BF),
                   jax.ShapeDtypeStruct((m_tok, D_MODEL), F32), jax.ShapeDtypeStruct((m_tok, D_MODEL), F32),
                   jax.ShapeDtypeStruct((1, D_MODEL), F32), jax.ShapeDtypeStruct((1, XHD), F32)),
        in_specs=[row_spec(D_MODEL), row_spec(D_MODEL), row_spec(D_MODEL), full(kn), full(v), full(w_xo), full(w_xq), full(w_out),
                  full(g_xattn), full(g_xq)],
        out_specs=(row_spec(D_MODEL), row_spec(D_MODEL), row_spec(GROUP_W), row_spec(GROUP_W), acc(m_tok, D_MODEL),
                   acc(m_tok, D_MODEL), acc(1, D_MODEL), acc(1, XHD)),
        scratch_shapes=[pltpu.VMEM((tm, D_MODEL), BF)],
        compiler_params=_cparams(("arbitrary",)),
    )(dh2, h1, qx, kn, v, w_xo, w_xq, w_out, g_xattn, g_xq)


def _mem_kv_bwd(dkn, dv, kraw, mem, memn, g_mem, g_xk, w_xkv):
    m_tok = mem.shape[0]

    def body(dkn_ref, dv_ref, kraw_ref, mem_ref, memn_ref, gm_ref, gk_ref, w_ref, dw_ref, dgm_ref, dgk_ref, dkv_scr):
        gk = gk_ref[...]
        dgk = jnp.zeros((1, XHD), F32)
        for h in range(N_XH):
            sl = slice(h * XHD, (h + 1) * XHD)
            dx, dg_h = _rms_bwd(kraw_ref[:, sl], gk, dkn_ref[:, sl])
            dgk = dgk + dg_h
            dkv_scr[:, sl] = dx.astype(BF)
        dgk_ref[...] = dgk
        dkv_scr[:, D_MODEL:] = dv_ref[...].astype(BF)
        dkv = dkv_scr[...]
        dw_ref[...] = _dot_tn(memn_ref[...], dkv)
        dmemn = _dot_nt(dkv, w_ref[...])
        mem_v = mem_ref[...]
        r = lax.rsqrt(jnp.mean(mem_v * mem_v, axis=-1, keepdims=True) + EPS)
        dgm_ref[...] = jnp.sum(dmemn * mem_v * r, axis=0, keepdims=True)

    return pl.pallas_call(
        body, name="mem_kv_bwd",
        out_shape=(jax.ShapeDtypeStruct((D_MODEL, 2 * D_MODEL), F32), jax.ShapeDtypeStruct((1, D_MODEL), F32),
                   jax.ShapeDtypeStruct((1, XHD), F32)),
        in_specs=[VMEM_SPEC] * 8, out_specs=(VMEM_SPEC,) * 3,
        scratch_shapes=[pltpu.VMEM((m_tok, 2 * D_MODEL), BF)],
        compiler_params=_cparams(),
    )(dkn, dv, kraw, mem, memn, g_mem, g_xk, w_xkv)


def _fox_bwd(q_aug, k_aug, proj, dmf, o32, lse, sums):
    t_len = q_aug.shape[0]
    tb = min(ATT_BLOCK, t_len)
    n_b = t_len // tb
    v_col = 6 * GROUP_W // LANES
    n_w = len(sums)
    n_steps = (N_HEADS // 2) * n_b

    def body(*refs):
        k_ref, v_ref, q_ref, do_ref, o_ref, lse_ref = refs[:6]
        dq_ref, dk_ref, dv_ref, df_ref = refs[6 + n_w:10 + n_w]
        delta = refs[10 + 2 * n_w]
        comm = (refs[6:6 + n_w], refs[10 + n_w:10 + 2 * n_w]) + tuple(refs[11 + 2 * n_w:])
        j = pl.program_id(1)
        step = pl.program_id(0) * n_b + j

        @pl.when(step == 0)
        def _():
            _scatter_phase(0, *comm)

        @pl.when(j == 0)
        def _():
            dq_ref[...] = jnp.zeros_like(dq_ref)
            dd = do_ref[...].astype(F32) * o_ref[...]
            hrow = lax.broadcasted_iota(jnp.int32, (8, LANES), 0)
            lane = lax.broadcasted_iota(jnp.int32, (8, LANES), 1)
            ind = ((lane // HEAD_DIM) == hrow).astype(BF)
            delta[...] = _dot_nt_exact(ind, dd)

        k2, v2 = k_ref[...], v_ref[...]
        ks = [k2[:, hh * LANES:(hh + 1) * LANES] for hh in range(2)]
        vs = [v2[:, hh * HEAD_DIM:(hh + 1) * HEAD_DIM] for hh in range(2)]

        def blocks(idx, carry, masked, valid):
            loaded = []
            for i in idx:
                rows = pl.ds(pl.multiple_of(i * tb, tb), tb)
                q2 = q_ref[rows, :]
                do2 = do_ref[rows, :]
                loaded.append((rows, [q2[:, hh * LANES:(hh + 1) * LANES] for hh in range(2)],
                               [do2[:, hh * HEAD_DIM:(hh + 1) * HEAD_DIM] for hh in range(2)]))
            ss = [[_dot_nt(ks[hh], qs[hh]) for hh in range(2)] for _, qs, _ in loaded]
            dps = [[_dot_nt(vs[hh], dos[hh]) for hh in range(2)] for _, _, dos in loaded]
            pts, dsts, dfs = [], [], []
            for b, (rows, _, _) in enumerate(loaded):
                for hh in range(2):
                    s_t = ss[b][hh]
                    if masked[b]:
                        krow = lax.broadcasted_iota(jnp.int32, (tb, tb), 0)
                        qcol = lax.broadcasted_iota(jnp.int32, (tb, tb), 1)
                        s_t = jnp.where(qcol >= krow, s_t, NEG)
                    lse_row = lse_ref[0, hh:hh + 1, rows]
                    if valid[b] is not None:
                        lse_row = jnp.where(valid[b], lse_row, -NEG)
                    p_t = jnp.exp2(s_t - lse_row)
                    pts.append(p_t.astype(BF))
                    ds_t = p_t * (dps[b][hh] - delta[hh:hh + 1, rows])
                    dsts.append(ds_t.astype(BF))
                    dfs.append(jnp.sum(ds_t, axis=-1, keepdims=True))
            out = list(carry)
            for b, (rows, qs, dos) in enumerate(loaded):
                for hh in range(2):
                    dk, dv, df = out[hh]
                    dv = dv + jnp.dot(pts[2 * b + hh], dos[hh], preferred_element_type=F32)
                    dk = dk + jnp.dot(dsts[2 * b + hh], qs[hh], preferred_element_type=F32)
                    dq_ref[rows, hh * HEAD_DIM:(hh + 1) * HEAD_DIM] += _dot_tn(dsts[2 * b + hh], ks[hh])[:, :HEAD_DIM]
                    out[hh] = (dk, dv, df - dfs[2 * b + hh])
            return tuple(out)

        def pair(t, carry):
            i0 = j + 1 + 2 * t
            return blocks([i0, jnp.minimum(i0 + 1, n_b - 1)], carry, [False, False], [None, i0 + 1 < n_b])

        init = tuple((jnp.zeros((tb, LANES), F32), jnp.zeros((tb, HEAD_DIM), F32), jnp.zeros((tb, 1), F32)) for _ in range(2))
        carry = blocks([j], init, [True], [None])
        carry = lax.fori_loop(0, (n_b - j) // 2, pair, carry)
        dk_ref[...] = jnp.concatenate([carry[hh][0][:, :HEAD_DIM] for hh in range(2)], axis=-1) * LN2
        dv_ref[...] = jnp.concatenate([carry[hh][1] for hh in range(2)], axis=-1)
        df_ref[0] = jnp.concatenate([carry[hh][2] for hh in range(2)], axis=-1)

        @pl.when(step == n_steps - 1)
        def _():
            _scatter_phase(1, *comm)

    blk = lambda w, col0: pl.BlockSpec((tb, w), lambda hp, j: (j, col0 + hp))
    whole = lambda w: pl.BlockSpec((t_len, w), lambda hp, j: (0, hp))
    rows2 = pl.BlockSpec((1, 2, t_len), lambda hp, j: (hp, 0, 0))
    cols2 = pl.BlockSpec((1, tb, 2), lambda hp, j: (hp, j, 0))
    return pl.pallas_call(
        body, name="fox_bwd", grid=(N_HEADS // 2, n_b),
        out_shape=(jax.ShapeDtypeStruct((t_len, GROUP_W), F32), jax.ShapeDtypeStruct((t_len, GROUP_W), F32),
                   jax.ShapeDtypeStruct((t_len, GROUP_W), F32), jax.ShapeDtypeStruct((N_HEADS // 2, t_len, 2), F32))
        + _scatter_out_shapes(sums),
        in_specs=[blk(2 * LANES, 0), blk(LANES, v_col), whole(2 * LANES), whole(LANES), whole(LANES), rows2] + [ANY] * n_w,
        out_specs=(whole(LANES), blk(LANES, 0), blk(LANES, 0), cols2) + (ANY,) * n_w,
        scratch_shapes=[pltpu.VMEM((8, t_len), F32)] + _scatter_scratch(n_w),
        compiler_params=_cparams(("arbitrary", "arbitrary")),
    )(k_aug, proj, q_aug, dmf, o32, lse, *sums)


def _retention_bwd(dmr, raw, proj, g_ret, rq, rk, states, tables, parts):
    t_len = rq.shape[0]
    c = min(RET_BLOCK, t_len)
    n_b = t_len // c
    wdec, qdec, kdec, cdec = tables
    v_col, g_col = 2 * GROUP_W // LANES, 3 * GROUP_W // LANES
    n_w = len(parts)
    n_steps = (N_HEADS // 2) * n_b

    def body(*refs):
        d_ref, raw_ref, rg_ref, g_ref, q_ref, k_ref, v_ref, st_ref, w_ref, qd_ref, kd_ref, cd_ref = refs[:12]
        dq_ref, dk_ref, dv_ref, drg_ref, dg_ref = refs[12 + n_w:17 + n_w]
        gstate = refs[17 + 2 * n_w]
        comm = (refs[12:12 + n_w], refs[17 + n_w:17 + 2 * n_w]) + tuple(refs[18 + 2 * n_w:])
        step = pl.program_id(0) * n_b + pl.program_id(1)

        @pl.when(step == 0)
        def _():
            _exchange_phase(0, *comm)

        @pl.when(pl.program_id(1) == 0)
        def _():
            gstate[...] = jnp.zeros_like(gstate)
            dg_ref[...] = jnp.zeros_like(dg_ref)

        d, raw_v, g = d_ref[...], raw_ref[...], g_ref[0]
        gate = rg_ref[...].astype(F32)
        xc = raw_v - _group_mean64(raw_v)
        r = lax.rsqrt(_group_mean64(xc * xc) + EPS)
        xh = xc * r
        sg = _sigmoid(gate)
        drg_ref[...] = d * (xh * g) * (sg * (1.0 + gate * (1.0 - sg)))
        dy = d * (gate * sg)
        dg_ref[0] += jnp.sum(dy * xh, axis=0, keepdims=True)
        dxh = dy * g
        do2 = r * (dxh - _group_mean64(dxh) - xh * _group_mean64(dxh * xh))
        q2, k2, v2 = q_ref[...], k_ref[...], v_ref[...]
        dqs, dks, dvs = [], [], []
        for hh in range(2):
            sl = slice(hh * HEAD_DIM, (hh + 1) * HEAD_DIM)
            q, k, v, do = q2[:, sl], k2[:, sl], v2[:, sl], do2[:, sl].astype(BF)
            w = w_ref[hh]
            a = _dot_nt(q, k) * w
            dm = _dot_nt(do, v) * w
            sp, gs = st_ref[0, 0, hh], gstate[hh]
            qd = q.astype(F32) * qd_ref[hh]
            kd = k.astype(F32) * kd_ref[hh]
            dqs.append(_dot(dm, k) + _dot_nt(do, sp) * qd_ref[hh])
            dks.append(_dot_tn(dm, q) + _dot_nt(v, gs) * kd_ref[hh])
            dvs.append(_dot_tn(a, do) + _dot(kd, gs))
            gstate[hh] = gs * cd_ref[hh] + _dot_tn(qd, do)
        dq_ref[...] = jnp.concatenate(dqs, axis=-1)
        dk_ref[...] = jnp.concatenate(dks, axis=-1)
        dv_ref[...] = jnp.concatenate(dvs, axis=-1)

        @pl.when(step == n_steps - 1)
        def _():
            _exchange_phase(1, *comm)

    blk = lambda col0: pl.BlockSpec((c, LANES), lambda hp, i: (n_b - 1 - i, col0 + hp))
    tab = lambda a: pl.BlockSpec((2,) + a.shape[1:], lambda hp, i: (hp, 0, 0))
    gspec = pl.BlockSpec((1, 1, LANES), lambda hp, i: (hp, 0, 0))
    return pl.pallas_call(
        body, name="retention_bwd", grid=(N_HEADS // 2, n_b),
        out_shape=(jax.ShapeDtypeStruct((t_len, GROUP_W), F32),) * 4 + (jax.ShapeDtypeStruct((N_HEADS // 2, 1, LANES), F32),)
        + _exchange_out_shapes(parts),
        in_specs=[blk(0), blk(0), blk(g_col), gspec, blk(0), blk(0), blk(v_col),
                  pl.BlockSpec((1, 1, 2, HEAD_DIM, HEAD_DIM), lambda hp, i: (hp, n_b - 1 - i, 0, 0, 0)),
                  tab(wdec), tab(qdec), tab(kdec), tab(cdec)] + [ANY] * n_w,
        out_specs=(blk(0), blk(0), blk(0), blk(0), gspec) + (ANY,) * n_w,
        scratch_shapes=[pltpu.VMEM((2, HEAD_DIM, HEAD_DIM), F32)] + _exchange_scratch(n_w),
        compiler_params=_cparams(("arbitrary", "arbitrary")),
    )(dmr, raw, proj, g_ret, rq, rk, proj, states, wdec, qdec, kdec, cdec, *parts)


def _in_proj_bwd(x, g_mix, dh1, dq_r, dk_r, dv_r, drg, dq_f, dk_f, dv_f, df_col, proj, z, cos_t, sin_t, gq_t, gk_t, w_in_t):
    t_len = x.shape[0]
    tm = min(ROW_TILE, t_len)
    n_t = t_len // tm

    def body(x_ref, g_ref, dh1_ref, dqr_ref, dkr_ref, dvr_ref, drg_ref, dqf_ref, dkf_ref, dvf_ref, df_ref, fq_ref, fk_ref, z_ref,
             cos_ref, sin_ref, gq_ref, gk_ref, wm_ref, wf_ref,
             dproj_ref, dz_ref, dx_ref, dg_ref, dgq_ref, dgk_ref, db_ref, carry, gq_acc, gk_acc):
        i = pl.program_id(0)

        @pl.when(i == 0)
        def _():
            carry[...] = jnp.zeros_like(carry)
            gq_acc[...] = jnp.zeros_like(gq_acc)
            gk_acc[...] = jnp.zeros_like(gk_acc)
            dg_ref[...] = jnp.zeros_like(dg_ref)
            db_ref[...] = jnp.zeros_like(db_ref)

        c, s = cos_ref[...], sin_ref[...]
        gq, gk = gq_ref[...], gk_ref[...]
        dgq = jnp.zeros((1, LANES), F32)
        dgk = jnp.zeros((1, LANES), F32)
        for sl in _chunks(GROUP_W):
            dy = dqr_ref[:, sl] * 0.125
            dproj_ref[:, sl] = (dy * c + _swap32(dy * s)).astype(BF)
            dy = dkr_ref[:, sl]
            dproj_ref[:, GROUP_W + sl.start:GROUP_W + sl.stop] = (dy * c + _swap32(dy * s)).astype(BF)
            dproj_ref[:, 2 * GROUP_W + sl.start:2 * GROUP_W + sl.stop] = dvr_ref[:, sl].astype(BF)
            dproj_ref[:, 3 * GROUP_W + sl.start:3 * GROUP_W + sl.stop] = drg_ref[:, sl].astype(BF)
            for src, dsrc, gain, off in ((fq_ref, dqf_ref, gq, 4), (fk_ref, dkf_ref, gk, 5)):
                xr = src[:, sl].astype(F32)
                r = lax.rsqrt(_group_mean64(xr * xr) + EPS)
                xh = xr * r
                dy = dsrc[:, sl] * (0.125 if off == 4 else 1.0)
                dgs = jnp.sum(dy * xh, axis=0, keepdims=True)
                if off == 4:
                    dgq = dgq + dgs
                else:
                    dgk = dgk + dgs
                dxh = dy * gain
                dproj_ref[:, off * GROUP_W + sl.start:off * GROUP_W + sl.stop] = \
                    (r * (dxh - xh * _group_mean64(dxh * xh))).astype(BF)
            dproj_ref[:, 6 * GROUP_W + sl.start:6 * GROUP_W + sl.stop] = dvf_ref[:, sl].astype(BF)
        gq_acc[...] += dgq
        gk_acc[...] += dgk
        row = lax.broadcasted_iota(jnp.int32, (tm, tm), 0)
        col = lax.broadcasted_iota(jnp.int32, (tm, tm), 1)
        dlf = _dot_exact((col >= row).astype(BF), df_ref[...]) + carry[0:1, :]
        carry[...] = jnp.broadcast_to(dlf[0:1, :], carry.shape)
        lane = lax.broadcasted_iota(jnp.int32, (tm, LANES), 1)
        dz = jnp.where(lane < N_HEADS, dlf / (1.0 + jnp.exp(z_ref[...])), 0.0)
        db_ref[...] += jnp.sum(dz, axis=0, keepdims=True)
        dz_bf = dz.astype(BF)
        dz_ref[...] = dz_bf
        dn1 = jnp.dot(dz_bf, wf_ref[...], preferred_element_type=F32)
        for sec in range(MAIN_W // GROUP_W):
            sl = slice(sec * GROUP_W, (sec + 1) * GROUP_W)
            dn1 = dn1 + jnp.dot(dproj_ref[:, sl], wm_ref[sl, :], preferred_element_type=F32)
        dx, dg = _rms_bwd(x_ref[...], g_ref[...], dn1)
        dx_ref[...] = dh1_ref[...] + dx
        dg_ref[...] += dg

        @pl.when(i == n_t - 1)
        def _():
            dgq_ref[...] = gq_acc[:, :HEAD_DIM] + gq_acc[:, HEAD_DIM:]
            dgk_ref[...] = gk_acc[:, :HEAD_DIM] + gk_acc[:, HEAD_DIM:]

    row_spec = lambda w, col=0: pl.BlockSpec((tm, w), lambda i: (n_t - 1 - i, col))
    full = lambda a: pl.BlockSpec(a.shape, lambda i: (0,) * a.ndim)
    acc = lambda r, c: pl.BlockSpec((r, c), lambda i: (0, 0))
    return pl.pallas_call(
        body, name="in_proj_bwd", grid=(n_t,),
        out_shape=(jax.ShapeDtypeStruct((t_len, MAIN_W), BF), jax.ShapeDtypeStruct((t_len, LANES), BF),
                   jax.ShapeDtypeStruct((t_len, D_MODEL), F32), jax.ShapeDtypeStruct((1, D_MODEL), F32),
                   jax.ShapeDtypeStruct((1, HEAD_DIM), F32), jax.ShapeDtypeStruct((1, HEAD_DIM), F32),
                   jax.ShapeDtypeStruct((1, LANES), F32)),
        in_specs=[row_spec(D_MODEL), full(g_mix), row_spec(D_MODEL)] + [row_spec(GROUP_W)] * 7
        + [row_spec(LANES), row_spec(GROUP_W, 4), row_spec(GROUP_W, 5), row_spec(LANES), row_spec(LANES), row_spec(LANES),
           full(gq_t), full(gk_t), *_w_in_specs()],
        out_specs=(row_spec(MAIN_W), row_spec(LANES), row_spec(D_MODEL), acc(1, D_MODEL), acc(1, HEAD_DIM), acc(1, HEAD_DIM),
                   acc(1, LANES)),
        scratch_shapes=[pltpu.VMEM((8, LANES), F32), pltpu.VMEM((1, LANES), F32), pltpu.VMEM((1, LANES), F32)],
        compiler_params=_cparams(("arbitrary",)),
    )(x, g_mix, dh1, dq_r, dk_r, dv_r, drg, dq_f, dk_f, dv_f, df_col, proj, proj, z, cos_t, sin_t, gq_t, gk_t, w_in_t, w_in_t)


def _matmul_tn(a, b, name, bk=512):
    t_len, m = a.shape
    n = b.shape[1]
    bm = m if m <= TN_MAX_ROWS else m // 2
    bk = min(bk, t_len)

    def body(a_ref, b_ref, o_ref):
        @pl.when(pl.program_id(1) == 0)
        def _():
            o_ref[...] = jnp.zeros_like(o_ref)

        o_ref[...] += _dot_tn(a_ref[...], b_ref[...])

    return pl.pallas_call(
        body, name=name, grid=(m // bm, t_len // bk),
        out_shape=jax.ShapeDtypeStruct((m, n), F32),
        in_specs=[pl.BlockSpec((bk, bm), lambda i, k: (k, i)), pl.BlockSpec((bk, n), lambda i, k: (k, 0))],
        out_specs=pl.BlockSpec((bm, n), lambda i, k: (i, 0)),
        compiler_params=_cparams(("arbitrary", "arbitrary")),
    )(a, b)


def _place():
    x, y, c = lax.axis_index("x"), lax.axis_index("y"), lax.axis_index("c")
    chips = [(1 - x, y), (x, 1 - y), (1 - x, 1 - y)]
    return x, y, c, chips


def _row_chunks(rows, limit):
    step = max(d for d in range(16, min(rows, limit) + 1, 16) if rows % d == 0)
    return [slice(i, i + step) for i in range(0, rows, step)]


ICI_CHUNK_ROWS = 128
D2D_CHUNK_ROWS = 64


def _gather_phase(phase, ins, outs, send_sems, recv_sems):
    x, y, c, chips = _place()
    me_chip = 2 * x + y
    sibling = (x, y, 1 - c)

    def copy(w, k, slot, half, to, rows=slice(None), src=None):
        dst = outs[w].at[slot, half, rows]
        return pltpu.make_async_remote_copy(src_ref=dst if src is None else src, dst_ref=dst,
                                            send_sem=send_sems.at[w, k], recv_sem=recv_sems.at[w, k],
                                            device_id=to, device_id_type=MESH)

    for w in range(len(ins)):
        for j, (px, py) in enumerate(chips):
            if phase == 0:
                for rows in _row_chunks(ins[w].shape[1], ICI_CHUNK_ROWS):
                    copy(w, j, me_chip, c, (px, py, c), rows, src=ins[w].at[c, rows]).start()
            elif phase == 1:
                copy(w, j, 2 * px + py, c, (x, y, c)).wait_recv()
                for rows in _row_chunks(ins[w].shape[1], D2D_CHUNK_ROWS):
                    copy(w, 3 + j, 2 * px + py, c, sibling, rows).start()
            else:
                copy(w, 3 + j, 2 * px + py, 1 - c, (x, y, c)).wait_recv()
                copy(w, j, me_chip, c, (px, py, c), src=ins[w].at[c]).wait_send()
                copy(w, 3 + j, 2 * px + py, c, sibling).wait_send()


def _gather_scratch(n_w):
    return [pltpu.SemaphoreType.DMA((n_w, 6)), pltpu.SemaphoreType.DMA((n_w, 6))]


def _all_gather_weights(shards):
    n_w = len(shards)

    def body(*refs):
        for phase in range(3):
            _gather_phase(phase, refs[:n_w], refs[n_w:2 * n_w], *refs[2 * n_w:])

    return pl.pallas_call(
        body, name="all_gather_weights",
        out_shape=tuple(jax.ShapeDtypeStruct((4,) + s.shape, s.dtype) for s in shards),
        in_specs=[ANY] * n_w, out_specs=(ANY,) * n_w, scratch_shapes=_gather_scratch(n_w),
    )(*shards)


def _exchange_phase(phase, ins, theirs, send_sems, recv_sems):
    x, y, c, _ = _place()

    def remote(w, k=slice(None), rows=slice(None)):
        return pltpu.make_async_remote_copy(src_ref=ins[w].at[k, 1 - c, rows], dst_ref=theirs[w].at[k, rows],
                                            send_sem=send_sems.at[w], recv_sem=recv_sems.at[w], device_id=(x, y, 1 - c),
                                            device_id_type=MESH)

    for w in range(len(ins)):
        if phase == 0:
            for k in range(4):
                for rows in _row_chunks(ins[w].shape[2], D2D_CHUNK_ROWS):
                    remote(w, k, rows).start()
        else:
            remote(w).wait()


def _exchange_scratch(n_w):
    return [pltpu.SemaphoreType.DMA((n_w,)), pltpu.SemaphoreType.DMA((n_w,))]


def _exchange_out_shapes(grads):
    return tuple(jax.ShapeDtypeStruct((4,) + g.shape[2:], g.dtype) for g in grads)


def _exchange_core_halves(grads):
    n_w = len(grads)

    def body(*refs):
        for phase in range(2):
            _exchange_phase(phase, refs[:n_w], refs[n_w:2 * n_w], *refs[2 * n_w:])

    return pl.pallas_call(
        body, name="exchange_core_halves", out_shape=_exchange_out_shapes(grads),
        in_specs=[ANY] * n_w, out_specs=(ANY,) * n_w, scratch_shapes=_exchange_scratch(n_w),
    )(*grads)


def _add_pairs(part, theirs, name):
    _, _, r, c = part.shape
    rb = 32 if r % 32 == 0 else r

    def body(a_ref, b_ref, own_ref, ob_ref):
        my_chip = 2 * lax.axis_index("x") + lax.axis_index("y")
        ob_ref[...] = (a_ref[...] + b_ref[...]).astype(BF)
        own_ref[...] = a_ref[my_chip] + b_ref[my_chip]

    spec = pl.BlockSpec((4, rb, c), lambda i: (0, i, 0))
    return pl.pallas_call(
        body, name=name, grid=(r // rb,),
        out_shape=(jax.ShapeDtypeStruct((r, c), F32), jax.ShapeDtypeStruct((4, r, c), BF)),
        in_specs=[pl.BlockSpec((4, None, rb, c), lambda i: (0, lax.axis_index("c"), i, 0)), spec],
        out_specs=(pl.BlockSpec((rb, c), lambda i: (i, 0)), spec), compiler_params=_cparams(("arbitrary",)),
    )(part, theirs)


def _scatter_phase(phase, bfs, got, send_sems, recv_sems):
    x, y, c, chips = _place()

    def remote(w, j, px, py, rows=slice(None)):
        return pltpu.make_async_remote_copy(src_ref=bfs[w].at[2 * px + py, rows], dst_ref=got[w].at[j, rows],
                                            send_sem=send_sems.at[w, j], recv_sem=recv_sems.at[w, j], device_id=(px, py, c),
                                            device_id_type=MESH)

    for w in range(len(bfs)):
        for j, (px, py) in enumerate(chips):
            if phase == 0:
                for rows in _row_chunks(bfs[w].shape[1], ICI_CHUNK_ROWS):
                    remote(w, j, px, py, rows).start()
            else:
                remote(w, j, px, py).wait()


def _scatter_scratch(n_w):
    return [pltpu.SemaphoreType.DMA((n_w, 3)), pltpu.SemaphoreType.DMA((n_w, 3))]


def _scatter_out_shapes(sums_bf16):
    return tuple(jax.ShapeDtypeStruct((3,) + s.shape[1:], BF) for s in sums_bf16)


def _scatter_to_chips(sums_bf16):
    n_w = len(sums_bf16)

    def body(*refs):
        for phase in range(2):
            _scatter_phase(phase, refs[:n_w], refs[n_w:2 * n_w], *refs[2 * n_w:])

    return pl.pallas_call(
        body, name="scatter_to_chips", out_shape=_scatter_out_shapes(sums_bf16),
        in_specs=[ANY] * n_w, out_specs=(ANY,) * n_w, scratch_shapes=_scatter_scratch(n_w),
    )(*sums_bf16)


def _add_received(own, got, name):
    r, c = own.shape
    rb = 32 if r % 32 == 0 else r

    def body(o_ref, g_ref, out_ref):
        out_ref[...] = ((o_ref[...] + g_ref[0].astype(F32)) + g_ref[1].astype(F32)) + g_ref[2].astype(F32)

    return pl.pallas_call(
        body, name=name, grid=(r // rb,), out_shape=jax.ShapeDtypeStruct((r, c), F32),
        in_specs=[pl.BlockSpec((rb, c), lambda i: (i, 0)), pl.BlockSpec((3, rb, c), lambda i: (0, i, 0))],
        out_specs=pl.BlockSpec((rb, c), lambda i: (i, 0)), compiler_params=_cparams(("arbitrary",)),
    )(own, got)


def _share_with_sibling(halves):
    n_w = len(halves)

    def body(*refs):
        ins, outs = refs[:n_w], refs[n_w:2 * n_w]
        send_sems, recv_sems = refs[2 * n_w:]
        x, y, c, _ = _place()

        def remote(w, rows=slice(None)):
            return pltpu.make_async_remote_copy(src_ref=ins[w].at[rows], dst_ref=outs[w].at[c, rows], send_sem=send_sems.at[w],
                                                recv_sem=recv_sems.at[w], device_id=(x, y, 1 - c), device_id_type=MESH)

        for w in range(n_w):
            for rows in _row_chunks(ins[w].shape[0], D2D_CHUNK_ROWS):
                remote(w, rows).start()
        for w in range(n_w):
            remote(w).wait()

    return pl.pallas_call(
        body, name="share_with_sibling",
        out_shape=tuple(jax.ShapeDtypeStruct((2,) + h.shape, h.dtype) for h in halves),
        in_specs=[ANY] * n_w, out_specs=(ANY,) * n_w,
        scratch_shapes=[pltpu.SemaphoreType.DMA((n_w,)), pltpu.SemaphoreType.DMA((n_w,))],
    )(*halves)


def _all_reduce_small(pack):
    r, c = pack.shape

    def body(p_ref, out_ref, slots, send_sems, recv_sems):
        x, y, cc, _ = _place()
        me = 4 * x + 2 * y + cc
        slots[me] = p_ref[...]
        copies = []
        for k in range(1, 8):
            dx, dy, dc = (k >> 2) & 1, (k >> 1) & 1, k & 1
            to = (1 - x if dx else x, 1 - y if dy else y, 1 - cc if dc else cc)
            cp = pltpu.make_async_remote_copy(src_ref=p_ref, dst_ref=slots.at[me], send_sem=send_sems.at[k - 1],
                                              recv_sem=recv_sems.at[k - 1], device_id=to, device_id_type=MESH)
            cp.start()
            copies.append(cp)
        for cp in copies:
            cp.wait()
        total = slots[0]
        for d in range(1, 8):
            total = total + slots[d]
        out_ref[...] = total

    return pl.pallas_call(
        body, name="all_reduce_small", out_shape=jax.ShapeDtypeStruct((r, c), F32),
        in_specs=[VMEM_SPEC], out_specs=VMEM_SPEC,
        scratch_shapes=[pltpu.VMEM((8, r, c), F32), pltpu.SemaphoreType.DMA((7,)), pltpu.SemaphoreType.DMA((7,))],
    )(pack)


def _adamw(w, g, m, v, name):
    r, c = w.shape
    rb, cb = (64, c) if r % 64 == 0 else (r, LANES if (r % 8 and c % LANES == 0) else c)
    c1 = 1.0 - ADAM_B1 ** ADAM_STEP
    c2 = 1.0 - ADAM_B2 ** ADAM_STEP

    def body(w_ref, g_ref, m_ref, v_ref, d_ref, nm_ref, nv_ref):
        gv = g_ref[...]
        nm = ADAM_B1 * m_ref[...] + (1.0 - ADAM_B1) * gv
        nv = ADAM_B2 * v_ref[...] + (1.0 - ADAM_B2) * (gv * gv)
        nm_ref[...] = nm
        nv_ref[...] = nv
        d_ref[...] = -ADAM_LR * ((nm / c1) / (jnp.sqrt(nv / c2) + ADAM_EPS) + ADAM_WD * w_ref[...])

    spec = pl.BlockSpec((rb, cb), lambda i, j: (i, j))
    return pl.pallas_call(
        body, name=name, grid=(r // rb, c // cb), out_shape=(jax.ShapeDtypeStruct((r, c), F32),) * 3,
        in_specs=[spec] * 4, out_specs=(spec,) * 3, compiler_params=_cparams(("arbitrary", "arbitrary")),
    )(w, g, m, v)


def _rope_tables(t_len):
    inv_freq = ROPE_BASE ** (-jnp.arange(0, HEAD_DIM, 2, dtype=F32) / HEAD_DIM)
    ang = jnp.arange(t_len, dtype=F32)[:, None] * inv_freq[None, :]
    cos, sin = jnp.cos(ang), jnp.sin(ang)
    cos_t = jnp.concatenate([cos, cos, cos, cos], axis=-1)
    sin_t = jnp.concatenate([-sin, sin, -sin, sin], axis=-1)
    return cos_t, sin_t


def _cols_to_shards(dw):
    r, n = dw.shape
    return jnp.transpose(dw.reshape(2, r // 2, 4, n // 4), (2, 0, 1, 3))


def _rows_to_shards(dw):
    r, n = dw.shape
    padded = _pad_rows(dw.reshape(4, r // 4, n))
    return padded.reshape(4, 2, padded.shape[1] // 2, n)


def _pad_lanes(a):
    extra = -a.shape[-1] % LANES
    return a if extra == 0 else jnp.pad(a, [(0, 0)] * (a.ndim - 1) + [(0, extra)])


def _pad_rows(a):
    rows = a.shape[-2]
    extra = 0 if rows % SHARD_ROW_ALIGN == 0 else -rows % SHARD_ROW_PAD
    return a if extra == 0 else jnp.pad(a, [(0, 0)] * (a.ndim - 2) + [(0, extra), (0, 0)])


def _pad_row(a, width=D_MODEL):
    a = a.reshape(1, -1)
    return jnp.pad(a, ((0, 0), (0, width - a.shape[1])))


def kernel(x, mem, g_mix, w_in, b_forget, g_ret_out, g_fox_q, g_fox_k, w_out, g_xattn, w_xq, w_xkv, g_mem, g_xq, g_xk, w_xo, g_ffn, w_gate, w_up, w_down, loss_target, m_g_mix, m_w_in, m_b_forget, m_g_ret_out, m_g_fox_q, m_g_fox_k, m_w_out, m_g_xattn, m_w_xq, m_w_xkv, m_g_mem, m_g_xq, m_g_xk, m_w_xo, m_g_ffn, m_w_gate, m_w_up, m_w_down, v_g_mix, v_w_in, v_b_forget, v_g_ret_out, v_g_fox_q, v_g_fox_k, v_w_out, v_g_xattn, v_w_xq, v_w_xkv, v_g_mem, v_g_xq, v_g_xk, v_w_xo, v_g_ffn, v_w_gate, v_w_up, v_w_down):
    big = {"w_in": (w_in, m_w_in, v_w_in), "w_out": (w_out, m_w_out, v_w_out), "w_xq": (w_xq, m_w_xq, v_w_xq),
           "w_xkv": (w_xkv, m_w_xkv, v_w_xkv), "w_xo": (w_xo, m_w_xo, v_w_xo), "w_gate": (w_gate, m_w_gate, v_w_gate),
           "w_up": (w_up, m_w_up, v_w_up), "w_down": (w_down, m_w_down, v_w_down)}
    for n in TRANSPOSED:
        big[n] = tuple(jnp.swapaxes(a, 1, 2) for a in big[n])
    shards = {}
    for n in big:
        w = _pad_rows(_pad_lanes(big[n][0][0].astype(BF)))
        shards[n] = w.reshape(2, w.shape[0] // 2, w.shape[1])
    sizes = {n: big[n][0].shape[1:] for n in big}
    w_in_full = _assemble_weight("w_in", _all_gather_weights([shards["w_in"]])[0], shards["w_in"], sizes["w_in"])
    small_w ={"g_mix": g_mix, "b_forget": b_forget, "g_ret_out": g_ret_out, "g_fox_q": g_fox_q, "g_fox_k": g_fox_k,
               "g_xattn": g_xattn, "g_mem": g_mem, "g_xq": g_xq, "g_xk": g_xk, "g_ffn": g_ffn}
    m_small = {"g_mix": m_g_mix, "b_forget": m_b_forget, "g_ret_out": m_g_ret_out, "g_fox_q": m_g_fox_q, "g_fox_k": m_g_fox_k,
               "g_xattn": m_g_xattn, "g_mem": m_g_mem, "g_xq": m_g_xq, "g_xk": m_g_xk, "g_ffn": m_g_ffn}
    v_small = {"g_mix": v_g_mix, "b_forget": v_b_forget, "g_ret_out": v_g_ret_out, "g_fox_q": v_g_fox_q, "g_fox_k": v_g_fox_k,
               "g_xattn": v_g_xattn, "g_mem": v_g_mem, "g_xq": v_g_xq, "g_xk": v_g_xk, "g_ffn": v_g_ffn}
    loss_part, grad_x, sums, got, small_g = _local_step(x[0], mem[0], loss_target[0], w_in_full, shards, sizes, small_w)
    return _reduce_and_update(big, sums, got, small_w, small_g, loss_part, grad_x, m_small, v_small)


def _assemble_weight(name, gathered, own, size):
    rows, width = size
    my_chip = 2 * lax.axis_index("x") + lax.axis_index("y")
    g = lax.dynamic_update_slice(gathered, own[None], (my_chip, 0, 0, 0))
    g = g.reshape(4, 2 * g.shape[2], g.shape[3])[:, :rows, :width]
    return jnp.transpose(g, (1, 0, 2)).reshape(rows, 4 * width) if name in COL_SHARDED else g.reshape(4 * rows, width)


def _shard_parts(names, dw):
    return [_pad_lanes(_cols_to_shards(dw[n]) if n in COL_SHARDED else _rows_to_shards(dw[n])) for n in names]


def _core_sums(names, parts, theirs):
    return [_add_pairs(p, t, f"core_sum_{n}") for n, p, t in zip(names, parts, theirs)]


def _local_step(xs, mems, tgt, w_in_full, shards, sizes, small_w):
    g_mix, b_forget, g_ret_out, g_fox_q, g_fox_k = (small_w[n] for n in ("g_mix", "b_forget", "g_ret_out", "g_fox_q", "g_fox_k"))
    g_xattn, g_mem, g_xq, g_xk, g_ffn = (small_w[n] for n in ("g_xattn", "g_mem", "g_xq", "g_xk", "g_ffn"))
    w_in_t = jnp.pad(w_in_full, ((0, MAIN_W + LANES - IN_W), (0, 0)))
    t_len = xs.shape[0]
    cos_t, sin_t = _rope_tables(t_len)
    tables = _decay_tables(min(RET_BLOCK, t_len))
    gq_t = jnp.concatenate([g_fox_q, g_fox_q], axis=-1)
    gk_t = jnp.concatenate([g_fox_k, g_fox_k], axis=-1)
    b_pad = _pad_row(b_forget, LANES)
    g_ret = g_ret_out.reshape(N_HEADS // 2, 1, LANES)

    n1, proj, rq, rk, q_aug, k_aug, z = _in_proj_fwd(xs, g_mix, w_in_t, b_pad, cos_t, sin_t, gq_t, gk_t)
    raw, mix_r, states = _retention_fwd(rq, rk, proj, g_ret, tables)
    mix_f, o32, lse, *gathered = _fox_fwd(q_aug, k_aug, proj, [shards[n] for n in LATE])
    full = {n: _assemble_weight(n, g, shards[n], sizes[n]) for n, g in zip(LATE, gathered)}
    memn, kraw, kn, vmem = _mem_kv_fwd(mems, g_mem, full["w_xkv"], g_xk)
    h1, hn2, qx, o_x, h2 = _attn_out_xattn_fwd(xs, mix_r, mix_f, full["w_out"], g_xattn, full["w_xq"], g_xq, kn, vmem, full["w_xo"])
    hn3, gate, up, act, dh3, loss_part = _ffn_loss_fwd(h2, g_ffn, full["w_gate"], full["w_up"], full["w_down"], tgt)

    dgate, dup, dh2, dg_ffn = _ffn_bwd(dh3, gate, up, h2, g_ffn, full["w_gate"], full["w_up"], full["w_down"])
    dqx, dh1, dmr, dmf, dkn, dvm, dg_xattn, dg_xq = _attn_out_xattn_bwd(dh2, h1, qx, kn, vmem, full["w_xo"], full["w_xq"],
                                                                      full["w_out"], g_xattn, g_xq)
    dw_xkv, dg_mem, dg_xk = _mem_kv_bwd(dkn, dvm, kraw, mems, memn, g_mem, g_xk, full["w_xkv"])
    dw = {
        "w_out": jnp.concatenate([_matmul_tn(mix_r, dh1, "dw_out_ret"), _matmul_tn(mix_f, dh1, "dw_out_fox")], axis=0),
        "w_xq": _matmul_tn(hn2, dqx, "dw_xq"),
        "w_xkv": dw_xkv,
        "w_xo": _matmul_tn(o_x, dh2, "dw_xo"),
        "w_gate": _matmul_tn(dgate, hn3, "dw_gate"),
        "w_up": _matmul_tn(dup, hn3, "dw_up"),
        "w_down": _matmul_tn(act, dh3, "dw_down"),
    }
    late_parts = _shard_parts(LATE, dw)
    dq_r, dk_r, dv_r, drg, dg_ret, *late_theirs = _retention_bwd(dmr, raw, proj, g_ret, rq, rk, states, tables, late_parts)
    late_sums = _core_sums(LATE, late_parts, late_theirs)
    dq_f, dk_f, dv_f, df, *late_got = _fox_bwd(q_aug, k_aug, proj, dmf, o32, lse, [s[1] for s in late_sums])
    df_col = jnp.pad(jnp.transpose(df, (1, 0, 2)).reshape(t_len, N_HEADS), ((0, 0), (0, LANES - N_HEADS)))
    dproj, dz, grad_x, dg_mix, dg_fq, dg_fk, db = _in_proj_bwd(xs, g_mix, dh1, dq_r, dk_r, dv_r, drg, dq_f, dk_f, dv_f, df_col,
                                                              proj, z, cos_t, sin_t, gq_t, gk_t, w_in_t)

    dw_in = jnp.concatenate([_matmul_tn(dproj, n1, "dw_in_main"), _matmul_tn(dz, n1, "dw_in_ff")[:IN_W - MAIN_W]], axis=0)
    in_parts = _shard_parts(("w_in",), {"w_in": dw_in})
    in_sums = _core_sums(("w_in",), in_parts, _exchange_core_halves(in_parts))
    in_got = _scatter_to_chips([in_sums[0][1]])
    sums = {n: s[0] for n, s in zip(("w_in",) + LATE, in_sums + late_sums)}
    got = dict(zip(("w_in",) + LATE, list(in_got) + late_got))
    small_g = {"g_mix": dg_mix, "b_forget": db[:, :N_HEADS], "g_ret_out": dg_ret, "g_fox_q": dg_fq, "g_fox_k": dg_fk,
               "g_xattn": dg_xattn, "g_mem": dg_mem, "g_xq": dg_xq, "g_xk": dg_xk, "g_ffn": dg_ffn}
    return loss_part, grad_x, sums, got, small_g


def _reduce_and_update(big, sums, got, small_w, small_g, loss_part, grad_x, m_small, v_small):
    big_names = list(big)
    my_core = lax.axis_index("c")
    finals = [_add_received(sums[n], got[n], f"chip_sum_{n}") for n in big_names]
    shared = _share_with_sibling(finals)
    grads, deltas, new_m, new_v = {}, {}, {}, {}
    for n, s, fin in zip(big_names, shared, finals):
        w, m, v = big[n]
        s = lax.dynamic_update_slice(s, fin[None], (my_core, 0, 0))
        g = s.reshape(2 * s.shape[1], s.shape[2])[:w.shape[1], :w.shape[2]]
        d, nm, nv = _adamw(w[0], g, m[0], v[0], f"adamw_{n}")
        grads[n], deltas[n], new_m[n], new_v[n] = ((jnp.swapaxes(a[None], 1, 2) if n in TRANSPOSED else a[None]) for a in (g, d, nm, nv))

    small_names = list(small_w)
    pad_rows = SMALL_ROWS - len(small_names) - 1
    stack = lambda d: jnp.concatenate([_pad_row(d[n]) for n in small_names] + [jnp.zeros((pad_rows + 1, D_MODEL), F32)], axis=0)
    g_pack = jnp.concatenate([_pad_row(small_g[n]) for n in small_names] + [_pad_row(loss_part[0:1, 0:1])]
                             + [jnp.zeros((pad_rows, D_MODEL), F32)], axis=0)
    g_tot = _all_reduce_small(g_pack)
    d_s, m_s, v_s = _adamw(stack(small_w), g_tot, stack(m_small), stack(v_small), "adamw_small")
    for i, n in enumerate(small_names):
        shape = small_w[n].shape
        size = int(np.prod(shape))
        grads[n] = g_tot[i, :size].reshape(shape)
        deltas[n], new_m[n], new_v[n] = d_s[i, :size].reshape(shape), m_s[i, :size].reshape(shape), v_s[i, :size].reshape(shape)
    loss = g_tot[len(small_names), 0]

    order = ["g_mix", "w_in", "b_forget", "g_ret_out", "g_fox_q", "g_fox_k", "w_out", "g_xattn", "w_xq", "w_xkv", "g_mem", "g_xq",
             "g_xk", "w_xo", "g_ffn", "w_gate", "w_up", "w_down"]
    return (loss, grad_x[None], *[grads[n] for n in order], *[deltas[n] for n in order], *[new_m[n] for n in order],
            *[new_v[n] for n in order])
```

```python
import functools

import numpy as np
import jax
import jax.numpy as jnp
from jax import lax
from jax.experimental import pallas as pl
from jax.experimental.pallas import tpu as pltpu

F32 = jnp.float32
BF = jnp.bfloat16

D_MODEL = 1024
HEAD_DIM = 64
N_HEADS = 8
GROUP_W = 512
N_XH = 4
XHD = 256
D_FF = 2816
MAIN_W = 3584
IN_W = 3592
ROPE_BASE = 10000.0
LOG2E = 1.4426950408889634
LN2 = 0.6931471805599453
EPS = 1e-6
NEG = -1e30
LANES = 128
RET_BLOCK = 256
REF_CHUNK = 64
ROW_TILE = 256
ATT_BLOCK = 256
TN_MAX_ROWS = 1408
SMALL_ROWS = 16
COL_SHARDED = ("w_xkv",)
TRANSPOSED = ("w_in", "w_gate", "w_up")
SHARD_ROW_ALIGN = 32
SHARD_ROW_PAD = 256
LATE = ("w_out", "w_xq", "w_xkv", "w_xo", "w_gate", "w_up", "w_down")
VMEM_LIMIT = 56 * 1024 * 1024

ADAM_LR = 0.001
ADAM_B1 = 0.9
ADAM_B2 = 0.999
ADAM_EPS = 1e-08
ADAM_WD = 0.01
ADAM_STEP = 10
ADAM_STEPS = 8

MESH = pl.DeviceIdType.MESH
ANY = pl.BlockSpec(memory_space=pl.ANY)
VMEM_SPEC = pl.BlockSpec(memory_space=pltpu.VMEM)


def _cparams(sem=None, vmem=VMEM_LIMIT):
    return pltpu.CompilerParams(dimension_semantics=sem, vmem_limit_bytes=vmem)


def _dot(a, b):
    return jnp.dot(a.astype(BF), b.astype(BF), preferred_element_type=F32)


def _dot_nt(a, b):
    return lax.dot_general(a.astype(BF), b.astype(BF), (((1,), (1,)), ((), ())), preferred_element_type=F32)


def _dot_tn(a, b):
    return lax.dot_general(a.astype(BF), b.astype(BF), (((0,), (0,)), ((), ())), preferred_element_type=F32)


def _split3(x):
    hi = x.astype(BF)
    r = x - hi.astype(F32)
    mid = r.astype(BF)
    lo = (r - mid.astype(F32)).astype(BF)
    return hi, mid, lo


def _dot_exact(ind, x):
    hi, mid, lo = _split3(x)
    return (jnp.dot(ind, lo, preferred_element_type=F32) + jnp.dot(ind, mid, preferred_element_type=F32)
            + jnp.dot(ind, hi, preferred_element_type=F32))


def _dot_nt_exact(ind, x):
    hi, mid, lo = _split3(x)
    dn = (((1,), (1,)), ((), ()))
    return (lax.dot_general(ind, lo, dn, preferred_element_type=F32) + lax.dot_general(ind, mid, dn, preferred_element_type=F32)
            + lax.dot_general(ind, hi, dn, preferred_element_type=F32))


def _sigmoid(x):
    return 1.0 / (1.0 + jnp.exp(-x))


def _rms_fwd(x, g):
    r = lax.rsqrt(jnp.mean(x * x, axis=-1, keepdims=True) + EPS)
    return x * r * g


def _rms_bwd(x, g, dy):
    r = lax.rsqrt(jnp.mean(x * x, axis=-1, keepdims=True) + EPS)
    xh = x * r
    dg = jnp.sum(dy * xh, axis=0, keepdims=True)
    dxh = dy * g
    dx = r * (dxh - xh * jnp.mean(dxh * xh, axis=-1, keepdims=True))
    return dx, dg


def _group_mean64(x):
    lane = lax.broadcasted_iota(jnp.int32, x.shape, 1)
    lo = lane < HEAD_DIM
    s_lo = jnp.sum(jnp.where(lo, x, 0.0), axis=-1, keepdims=True)
    s_hi = jnp.sum(jnp.where(lo, 0.0, x), axis=-1, keepdims=True)
    return jnp.where(lo, s_lo, s_hi) * (1.0 / HEAD_DIM)


def _swap32(x):
    lane = lax.broadcasted_iota(jnp.int32, x.shape, 1)
    first = (lane % HEAD_DIM) < (HEAD_DIM // 2)
    return jnp.where(first, pltpu.roll(x, LANES - HEAD_DIM // 2, axis=1), pltpu.roll(x, HEAD_DIM // 2, axis=1))


def _chunks(w):
    return [slice(j * LANES, (j + 1) * LANES) for j in range(w // LANES)]


def _aug_pair(qk, f_cols, is_query):
    lane = lax.broadcasted_iota(jnp.int32, qk.shape, 1)
    a = lane - HEAD_DIM
    values = (qk, pltpu.roll(qk, HEAD_DIM, axis=1))
    out = []
    for hh in range(2):
        hi, mid, lo = (p.astype(F32) for p in _split3(f_cols[hh] * LOG2E))
        if is_query:
            aux = jnp.where(a == 0, hi, jnp.where(a == 1, mid, jnp.where(a == 2, lo, jnp.where(a < 6, 1.0, 0.0))))
        else:
            aux = jnp.where(a < 3, 1.0, jnp.where(a == 3, -hi, jnp.where(a == 4, -mid, jnp.where(a == 5, -lo, 0.0))))
        out.append(jnp.where(a < 0, values[hh], aux))
    return jnp.concatenate(out, axis=-1).astype(BF)


def _mem_kv_fwd(mem, g_mem, w_xkv, g_xk):
    m_tok = mem.shape[0]

    def body(mem_ref, gm_ref, w_ref, gk_ref, memn_ref, kraw_ref, kn_ref, v_ref):
        mn = _rms_fwd(mem_ref[...], gm_ref[...]).astype(BF)
        memn_ref[...] = mn
        kv = jnp.dot(mn, w_ref[...], preferred_element_type=F32)
        k = kv[:, :D_MODEL]
        kraw_ref[...] = k
        v_ref[...] = kv[:, D_MODEL:].astype(BF)
        for h in range(N_XH):
            sl = slice(h * XHD, (h + 1) * XHD)
            kn_ref[:, sl] = _rms_fwd(k[:, sl], gk_ref[...]).astype(BF)

    return pl.pallas_call(
        body, name="mem_kv_fwd",
        out_shape=(jax.ShapeDtypeStruct((m_tok, D_MODEL), BF), jax.ShapeDtypeStruct((m_tok, D_MODEL), F32),
                   jax.ShapeDtypeStruct((m_tok, D_MODEL), BF), jax.ShapeDtypeStruct((m_tok, D_MODEL), BF)),
        in_specs=[VMEM_SPEC] * 4, out_specs=(VMEM_SPEC,) * 4, compiler_params=_cparams(),
    )(mem, g_mem, w_xkv, g_xk)


def _in_proj_fwd(x, g_mix, w_in_t, b_pad, cos_t, sin_t, gq_t, gk_t):
    t_len = x.shape[0]
    tm = min(ROW_TILE, t_len)
    n_t = t_len // tm

    def body(x_ref, g_ref, wm_ref, wf_ref, b_ref, cos_ref, sin_ref, gq_ref, gk_ref,
             n1_ref, proj_ref, rq_ref, rk_ref, qa_ref, ka_ref, z_ref, carry):
        i = pl.program_id(0)

        @pl.when(i == 0)
        def _():
            carry[...] = jnp.zeros_like(carry)

        n1 = _rms_fwd(x_ref[...], g_ref[...]).astype(BF)
        n1_ref[...] = n1
        proj = _dot_nt(n1, wm_ref[...])
        proj_ref[...] = proj.astype(BF)
        z = _dot_nt(n1, wf_ref[...]) + b_ref[...]
        z_ref[...] = z
        lane = lax.broadcasted_iota(jnp.int32, z.shape, 1)
        lf = jnp.where(lane < N_HEADS, jnp.minimum(z, 0.0) - jnp.log(1.0 + jnp.exp(-jnp.abs(z))), 0.0)
        row = lax.broadcasted_iota(jnp.int32, (tm, tm), 0)
        col = lax.broadcasted_iota(jnp.int32, (tm, tm), 1)
        tri = (row >= col).astype(BF)
        fc = _dot_exact(tri, lf) + carry[0:1, :]
        carry[...] = jnp.broadcast_to(fc[tm - 1:tm, :], carry.shape)
        c, s = cos_ref[...], sin_ref[...]
        for j, sl in enumerate(_chunks(GROUP_W)):
            q = proj[:, sl]
            rq_ref[:, sl] = ((q * c + _swap32(q) * s) * 0.125).astype(BF)
            k = proj[:, GROUP_W + j * LANES:GROUP_W + (j + 1) * LANES]
            rk_ref[:, sl] = (k * c + _swap32(k) * s).astype(BF)
            f_cols = [fc[:, 2 * j:2 * j + 1], fc[:, 2 * j + 1:2 * j + 2]]
            fq = proj[:, 4 * GROUP_W + j * LANES:4 * GROUP_W + (j + 1) * LANES]
            fq = fq * lax.rsqrt(_group_mean64(fq * fq) + EPS) * gq_ref[...] * (0.125 * LOG2E)
            qa_ref[:, 2 * j * LANES:2 * (j + 1) * LANES] = _aug_pair(fq, f_cols, True)
            fk = proj[:, 5 * GROUP_W + j * LANES:5 * GROUP_W + (j + 1) * LANES]
            fk = fk * lax.rsqrt(_group_mean64(fk * fk) + EPS) * gk_ref[...]
            ka_ref[:, 2 * j * LANES:2 * (j + 1) * LANES] = _aug_pair(fk, f_cols, False)

    row_spec = lambda w: pl.BlockSpec((tm, w), lambda i: (i, 0))
    full = lambda a: pl.BlockSpec(a.shape, lambda i: (0,) * a.ndim)
    return pl.pallas_call(
        body, name="in_proj_fwd", grid=(n_t,),
        out_shape=(jax.ShapeDtypeStruct((t_len, D_MODEL), BF), jax.ShapeDtypeStruct((t_len, MAIN_W), BF),
                   jax.ShapeDtypeStruct((t_len, GROUP_W), BF), jax.ShapeDtypeStruct((t_len, GROUP_W), BF),
                   jax.ShapeDtypeStruct((t_len, 2 * GROUP_W), BF), jax.ShapeDtypeStruct((t_len, 2 * GROUP_W), BF),
                   jax.ShapeDtypeStruct((t_len, LANES), F32)),
        in_specs=[row_spec(D_MODEL), full(g_mix), *_w_in_specs(), full(b_pad), row_spec(LANES), row_spec(LANES),
                  full(gq_t), full(gk_t)],
        out_specs=(row_spec(D_MODEL), row_spec(MAIN_W), row_spec(GROUP_W), row_spec(GROUP_W), row_spec(2 * GROUP_W),
                   row_spec(2 * GROUP_W), row_spec(LANES)),
        scratch_shapes=[pltpu.VMEM((8, LANES), F32)],
        compiler_params=_cparams(("arbitrary",)),
    )(x, g_mix, w_in_t, w_in_t, b_pad, cos_t, sin_t, gq_t, gk_t)


def _w_in_specs():
    return (pl.BlockSpec((MAIN_W, D_MODEL), lambda i: (0, 0)), pl.BlockSpec((LANES, D_MODEL), lambda i: (MAIN_W // LANES, 0)))


def _decay_tables(c):
    h = np.arange(N_HEADS, dtype=np.float64)
    lg = np.log(1.0 - 2.0 ** (-5.0 - h)).astype(np.float32).astype(np.float64)
    t = np.arange(c)
    same_or_earlier = (t[None, :] // REF_CHUNK) <= (t[:, None] // REF_CHUNK)
    w = np.where(same_or_earlier[None], np.exp(lg[:, None, None] * np.abs(t[:, None] - t[None, :])[None]), 0.0)
    qd = np.exp(lg[:, None] * (t[None, :] + 1.0))
    kd = np.exp(lg[:, None] * (c - 1.0 - t[None, :]))
    cd = np.exp(lg * c)
    ones = np.ones((1, 1, HEAD_DIM))
    return (jnp.asarray(w, F32), jnp.asarray(qd[:, :, None] * ones, F32), jnp.asarray(kd[:, :, None] * ones, F32),
            jnp.asarray(cd[:, None, None] * np.ones((1, HEAD_DIM, HEAD_DIM)), F32))


def _retention_fwd(rq, rk, proj, g_ret, tables):
    t_len = rq.shape[0]
    c = min(RET_BLOCK, t_len)
    n_b = t_len // c
    wdec, qdec, kdec, cdec = tables
    v_col, g_col = 2 * GROUP_W // LANES, 3 * GROUP_W // LANES

    def body(q_ref, k_ref, v_ref, rg_ref, g_ref, w_ref, qd_ref, kd_ref, cd_ref, raw_ref, mix_ref, st_ref, state):
        i = pl.program_id(1)

        @pl.when(i == 0)
        def _():
            state[...] = jnp.zeros_like(state)

        q2, k2, v2 = q_ref[...], k_ref[...], v_ref[...]
        outs = []
        for hh in range(2):
            sl = slice(hh * HEAD_DIM, (hh + 1) * HEAD_DIM)
            q, k, v = q2[:, sl], k2[:, sl], v2[:, sl]
            sp = state[hh]
            st_ref[0, 0, hh] = sp
            a = _dot_nt(q, k) * w_ref[hh]
            o = _dot(a, v) + _dot(q.astype(F32) * qd_ref[hh], sp)
            state[hh] = sp * cd_ref[hh] + _dot_tn(k.astype(F32) * kd_ref[hh], v)
            outs.append(o)
        o2 = jnp.concatenate(outs, axis=-1)
        raw_ref[...] = o2
        xc = o2 - _group_mean64(o2)
        xh = xc * lax.rsqrt(_group_mean64(xc * xc) + EPS)
        gate = rg_ref[...].astype(F32)
        mix_ref[...] = (gate * _sigmoid(gate) * (xh * g_ref[0])).astype(BF)

    blk = lambda col0: pl.BlockSpec((c, LANES), lambda hp, i: (i, col0 + hp))
    tab = lambda a: pl.BlockSpec((2,) + a.shape[1:], lambda hp, i: (hp, 0, 0))
    return pl.pallas_call(
        body, name="retention_fwd", grid=(N_HEADS // 2, n_b),
        out_shape=(jax.ShapeDtypeStruct((t_len, GROUP_W), F32), jax.ShapeDtypeStruct((t_len, GROUP_W), BF),
                   jax.ShapeDtypeStruct((N_HEADS // 2, n_b, 2, HEAD_DIM, HEAD_DIM), F32)),
        in_specs=[blk(0), blk(0), blk(v_col), blk(g_col), pl.BlockSpec((1, 1, LANES), lambda hp, i: (hp, 0, 0)),
                  tab(wdec), tab(qdec), tab(kdec), tab(cdec)],
        out_specs=(blk(0), blk(0), pl.BlockSpec((1, 1, 2, HEAD_DIM, HEAD_DIM), lambda hp, i: (hp, i, 0, 0, 0))),
        scratch_shapes=[pltpu.VMEM((2, HEAD_DIM, HEAD_DIM), F32)],
        compiler_params=_cparams(("arbitrary", "arbitrary")),
    )(rq, rk, proj, proj, g_ret, wdec, qdec, kdec, cdec)


def _fox_fwd(q_aug, k_aug, proj, shards):
    t_len = q_aug.shape[0]
    tq = min(ATT_BLOCK, t_len)
    n_q = t_len // tq
    v_col = 6 * GROUP_W // LANES
    tc = min(512, t_len)
    n_w = len(shards)
    n_steps = (N_HEADS // 2) * n_q

    def body(*refs):
        q_ref, k_ref, v_ref = refs[:3]
        o_ref, o32_ref, lse_ref = refs[3 + n_w:6 + n_w]
        vt = refs[6 + 2 * n_w]
        comm = (refs[3:3 + n_w], refs[6 + n_w:6 + 2 * n_w]) + tuple(refs[7 + 2 * n_w:])
        i = pl.program_id(1)
        step = pl.program_id(0) * n_q + i

        @pl.when(step == 0)
        def _():
            _gather_phase(0, *comm)

        @pl.when(step == (3 * n_steps) // 4)
        def _():
            _gather_phase(1, *comm)

        @pl.when(i == 0)
        def _():
            for c0 in range(0, t_len, tc):
                vt[:, c0:c0 + tc] = v_ref[c0:c0 + tc, :].T

        qs = [q_ref[:, hh * LANES:(hh + 1) * LANES] for hh in range(2)]
        ones = jnp.ones((HEAD_DIM, tq), BF)

        def scores(j):
            k2 = k_ref[pl.ds(pl.multiple_of(j * tq, tq), tq), :]
            return tuple(_dot_nt(k2[:, hh * LANES:(hh + 1) * LANES], qs[hh]) for hh in range(2))

        def update(j, ss, carry, masked):
            v2 = vt[:, pl.ds(pl.multiple_of(j * tq, tq), tq)]
            ps, stats = [], []
            for hh in range(2):
                m = carry[hh][0]
                s_t = ss[hh]
                if masked:
                    krow = lax.broadcasted_iota(jnp.int32, (tq, tq), 0)
                    qcol = lax.broadcasted_iota(jnp.int32, (tq, tq), 1)
                    s_t = jnp.where(qcol >= krow, s_t, NEG)
                m_new = jnp.maximum(m, jnp.max(s_t, axis=0, keepdims=True))
                ps.append(jnp.exp2(s_t - m_new).astype(BF))
                stats.append((m_new, jnp.exp2(m - m_new)))
            out = []
            for hh in range(2):
                m_new, alpha = stats[hh]
                v_aug = jnp.concatenate([v2[hh * HEAD_DIM:(hh + 1) * HEAD_DIM, :], ones], axis=0)
                out.append((m_new, carry[hh][1] * alpha + jnp.dot(v_aug, ps[hh], preferred_element_type=F32)))
            return tuple(out)

        def advance(j, state):
            ss, carry = state
            return scores(j + 1), update(j, ss, carry, False)

        init = tuple((jnp.full((1, tq), NEG, F32), jnp.zeros((LANES, tq), F32)) for _ in range(2))
        ss, carry = lax.fori_loop(0, i, advance, (scores(0), init))
        carry = update(i, ss, carry, True)
        outs, lses = [], []
        for hh in range(2):
            m, acc = carry[hh]
            l = acc[HEAD_DIM:HEAD_DIM + 1, :]
            outs.append(acc[:HEAD_DIM, :] / l)
            lses.append(m + jnp.log2(l))
        o2 = jnp.concatenate(outs, axis=0).T
        o32_ref[...] = o2
        o_ref[...] = o2.astype(BF)
        lse_ref[0] = jnp.concatenate(lses, axis=0)

        @pl.when(step == n_steps - 1)
        def _():
            _gather_phase(2, *comm)

    return pl.pallas_call(
        body, name="fox_fwd", grid=(N_HEADS // 2, n_q),
        out_shape=(jax.ShapeDtypeStruct((t_len, GROUP_W), BF), jax.ShapeDtypeStruct((t_len, GROUP_W), F32),
                   jax.ShapeDtypeStruct((N_HEADS // 2, 2, t_len), F32))
        + tuple(jax.ShapeDtypeStruct((4,) + s.shape, s.dtype) for s in shards),
        in_specs=[pl.BlockSpec((tq, 2 * LANES), lambda hp, i: (i, hp)),
                  pl.BlockSpec((t_len, 2 * LANES), lambda hp, i: (0, hp)),
                  pl.BlockSpec((t_len, LANES), lambda hp, i: (0, v_col + hp))] + [ANY] * n_w,
        out_specs=(pl.BlockSpec((tq, LANES), lambda hp, i: (i, hp)), pl.BlockSpec((tq, LANES), lambda hp, i: (i, hp)),
                   pl.BlockSpec((1, 2, tq), lambda hp, i: (hp, 0, i))) + (ANY,) * n_w,
        scratch_shapes=[pltpu.VMEM((LANES, t_len), BF)] + _gather_scratch(n_w),
        compiler_params=_cparams(("arbitrary", "arbitrary")),
    )(q_aug, k_aug, proj, *shards)


def _softmax_rows(s):
    p = jnp.exp(s - jnp.max(s, axis=-1, keepdims=True))
    return p / jnp.sum(p, axis=-1, keepdims=True)


def _attn_out_xattn_fwd(x, mix_r, mix_f, w_out, g_xattn, w_xq, g_xq, kn, v, w_xo):
    t_len = x.shape[0]
    tm = min(ROW_TILE, t_len)

    def body(x_ref, mr_ref, mf_ref, wo_ref, g_ref, wq_ref, gq_ref, kn_ref, v_ref, wxo_ref,
             h1_ref, hn_ref, qx_ref, o_ref, h2_ref):
        h1 = x_ref[...] + jnp.dot(mr_ref[...], wo_ref[:GROUP_W, :], preferred_element_type=F32) \
            + jnp.dot(mf_ref[...], wo_ref[GROUP_W:, :], preferred_element_type=F32)
        h1_ref[...] = h1
        hn = _rms_fwd(h1, g_ref[...]).astype(BF)
        hn_ref[...] = hn
        qx = jnp.dot(hn, wq_ref[...], preferred_element_type=F32).astype(BF)
        qx_ref[...] = qx
        for h in range(N_XH):
            sl = slice(h * XHD, (h + 1) * XHD)
            qn = _rms_fwd(qx[:, sl].astype(F32), gq_ref[...])
            p = _softmax_rows(_dot_nt(qn, kn_ref[:, sl]) * (XHD ** -0.5))
            o_ref[:, sl] = _dot(p, v_ref[:, sl]).astype(BF)
        h2_ref[...] = h1 + jnp.dot(o_ref[...], wxo_ref[...], preferred_element_type=F32)

    row_spec = lambda w: pl.BlockSpec((tm, w), lambda i: (i, 0))
    full = lambda a: pl.BlockSpec(a.shape, lambda i: (0,) * a.ndim)
    return pl.pallas_call(
        body, name="attn_out_xattn_fwd", grid=(t_len // tm,),
        out_shape=(jax.ShapeDtypeStruct((t_len, D_MODEL), F32), jax.ShapeDtypeStruct((t_len, D_MODEL), BF),
                   jax.ShapeDtypeStruct((t_len, D_MODEL), BF), jax.ShapeDtypeStruct((t_len, D_MODEL), BF),
                   jax.ShapeDtypeStruct((t_len, D_MODEL), F32)),
        in_specs=[row_spec(D_MODEL), row_spec(GROUP_W), row_spec(GROUP_W), full(w_out), full(g_xattn), full(w_xq), full(g_xq),
                  full(kn), full(v), full(w_xo)],
        out_specs=(row_spec(D_MODEL),) * 5,
        compiler_params=_cparams(("arbitrary",)),
    )(x, mix_r, mix_f, w_out, g_xattn, w_xq, g_xq, kn, v, w_xo)


def _ffn_loss_fwd(h2, g_ffn, w_gate, w_up, w_down, target):
    t_len = h2.shape[0]
    tm = min(ROW_TILE, t_len)

    def body(h2_ref, g_ref, wg_ref, wu_ref, wd_ref, tgt_ref, hn_ref, gate_ref, up_ref, act_ref, dh3_ref, loss_ref):
        @pl.when(pl.program_id(0) == 0)
        def _():
            loss_ref[...] = jnp.zeros_like(loss_ref)

        h2v = h2_ref[...]
        hn = _rms_fwd(h2v, g_ref[...]).astype(BF)
        hn_ref[...] = hn
        gate = _dot_nt(hn, wg_ref[...])
        up = _dot_nt(hn, wu_ref[...])
        gate_ref[...] = gate.astype(BF)
        up_ref[...] = up.astype(BF)
        act = (gate * _sigmoid(gate) * up).astype(BF)
        act_ref[...] = act
        diff = h2v + jnp.dot(act, wd_ref[...], preferred_element_type=F32) - tgt_ref[...]
        dh3_ref[...] = diff * (1.0 / D_MODEL)
        per_row = jnp.sum(diff * diff, axis=-1, keepdims=True) * (1.0 / D_MODEL)
        loss_ref[...] += 0.5 * jnp.sum(per_row, axis=0, keepdims=True)

    row_spec = lambda w: pl.BlockSpec((tm, w), lambda i: (i, 0))
    full = lambda a: pl.BlockSpec(a.shape, lambda i: (0,) * a.ndim, pipeline_mode=pl.Buffered(1))
    return pl.pallas_call(
        body, name="ffn_loss_fwd", grid=(t_len // tm,),
        out_shape=(jax.ShapeDtypeStruct((t_len, D_MODEL), BF), jax.ShapeDtypeStruct((t_len, D_FF), BF),
                   jax.ShapeDtypeStruct((t_len, D_FF), BF), jax.ShapeDtypeStruct((t_len, D_FF), BF),
                   jax.ShapeDtypeStruct((t_len, D_MODEL), F32), jax.ShapeDtypeStruct((8, LANES), F32)),
        in_specs=[row_spec(D_MODEL), full(g_ffn), full(w_gate), full(w_up), full(w_down), row_spec(D_MODEL)],
        out_specs=(row_spec(D_MODEL), row_spec(D_FF), row_spec(D_FF), row_spec(D_FF), row_spec(D_MODEL),
                   pl.BlockSpec((8, LANES), lambda i: (0, 0))),
        compiler_params=_cparams(("arbitrary",)),
    )(h2, g_ffn, w_gate, w_up, w_down, target)


def _ffn_bwd(dh3, gate, up, h2, g_ffn, w_gate, w_up, w_down):
    t_len = h2.shape[0]
    tm = min(ROW_TILE, t_len)

    def body(dh3_ref, gate_ref, up_ref, h2_ref, g_ref, wg_ref, wu_ref, wd_ref, dgate_ref, dup_ref, dh2_ref, dg_ref):
        @pl.when(pl.program_id(0) == 0)
        def _():
            dg_ref[...] = jnp.zeros_like(dg_ref)

        dh3v = dh3_ref[...]
        dact = _dot_nt(dh3v, wd_ref[...])
        g = gate_ref[...].astype(F32)
        sg = _sigmoid(g)
        dup = (dact * (g * sg)).astype(BF)
        dgate = (dact * up_ref[...].astype(F32) * (sg * (1.0 + g * (1.0 - sg)))).astype(BF)
        dup_ref[...] = dup
        dgate_ref[...] = dgate
        dhn = jnp.dot(dgate, wg_ref[...], preferred_element_type=F32) + jnp.dot(dup, wu_ref[...], preferred_element_type=F32)
        dx, dg = _rms_bwd(h2_ref[...], g_ref[...], dhn)
        dh2_ref[...] = dh3v + dx
        dg_ref[...] += dg

    row_spec = lambda w: pl.BlockSpec((tm, w), lambda i: (i, 0))
    full = lambda a: pl.BlockSpec(a.shape, lambda i: (0,) * a.ndim, pipeline_mode=pl.Buffered(1))
    return pl.pallas_call(
        body, name="ffn_bwd", grid=(t_len // tm,),
        out_shape=(jax.ShapeDtypeStruct((t_len, D_FF), BF), jax.ShapeDtypeStruct((t_len, D_FF), BF),
                   jax.ShapeDtypeStruct((t_len, D_MODEL), F32), jax.ShapeDtypeStruct((1, D_MODEL), F32)),
        in_specs=[row_spec(D_MODEL), row_spec(D_FF), row_spec(D_FF), row_spec(D_MODEL), full(g_ffn), full(w_gate), full(w_up),
                  full(w_down)],
        out_specs=(row_spec(D_FF), row_spec(D_FF), row_spec(D_MODEL), pl.BlockSpec((1, D_MODEL), lambda i: (0, 0))),
        compiler_params=_cparams(("arbitrary",)),
    )(dh3, gate, up, h2, g_ffn, w_gate, w_up, w_down)


def _attn_out_xattn_bwd(dh2, h1, qx, kn, v, w_xo, w_xq, w_out, g_xattn, g_xq):
    t_len = h1.shape[0]
    tm = min(ROW_TILE, t_len)
    m_tok = kn.shape[0]

    def body(dh2_ref, h1_ref, qx_ref, kn_ref, v_ref, wxo_ref, wq_ref, wo_ref, g_ref, gq_ref,
             dqx_ref, dh1_ref, dmr_ref, dmf_ref, dkn_ref, dv_ref, dg_ref, dgq_ref, dqx_scr):
        @pl.when(pl.program_id(0) == 0)
        def _():
            dkn_ref[...] = jnp.zeros_like(dkn_ref)
            dv_ref[...] = jnp.zeros_like(dv_ref)
            dg_ref[...] = jnp.zeros_like(dg_ref)
            dgq_ref[...] = jnp.zeros_like(dgq_ref)

        dh2v = dh2_ref[...]
        do = _dot_nt(dh2v, wxo_ref[...])
        gq = gq_ref[...]
        dgq = jnp.zeros((1, XHD), F32)
        for h in range(N_XH):
            sl = slice(h * XHD, (h + 1) * XHD)
            qraw = qx_ref[:, sl].astype(F32)
            qn = _rms_fwd(qraw, gq)
            p = _softmax_rows(_dot_nt(qn, kn_ref[:, sl]) * (XHD ** -0.5))
            doh = do[:, sl]
            dv_ref[:, sl] += _dot_tn(p, doh)
            dp = _dot_nt(doh, v_ref[:, sl])
            ds = p * (dp - jnp.sum(dp * p, axis=-1, keepdims=True)) * (XHD ** -0.5)
            dqn = _dot(ds, kn_ref[:, sl])
            dkn_ref[:, sl] += _dot_tn(ds, qn)
            dx, dg_h = _rms_bwd(qraw, gq, dqn)
            dgq = dgq + dg_h
            dqx_scr[:, sl] = dx.astype(BF)
        dgq_ref[...] += dgq
        dqx = dqx_scr[...]
        dqx_ref[...] = dqx
        dhn = _dot_nt(dqx, wq_ref[...])
        dx, dg = _rms_bwd(h1_ref[...], g_ref[...], dhn)
        dg_ref[...] += dg
        dh1 = dh2v + dx
        dh1_ref[...] = dh1
        dmix = _dot_nt(dh1, wo_ref[...])
        dmr_ref[...] = dmix[:, :GROUP_W]
        dmf_ref[...] = dmix[:, GROUP_W:].astype(BF)

    row_spec = lambda w: pl.BlockSpec((tm, w), lambda i: (i, 0))
    full = lambda a: pl.BlockSpec(a.shape, lambda i: (0,) * a.ndim)
    acc = lambda r, c: pl.BlockSpec((r, c), lambda i: (0, 0))
    return pl.pallas_call(
        body, name="attn_out_xattn_bwd", grid=(t_len // tm,),
        out_shape=(jax.ShapeDtypeStruct((t_len, D_MODEL), BF), jax.ShapeDtypeStruct((t_len, D_MODEL), F32),
                   jax.ShapeDtypeStruct((t_len, GROUP_W), F32), jax.ShapeDtypeStruct((t_len, GROUP_W), BF),
                   jax.ShapeDtypeStruct((m_tok, D_MODEL), F32), jax.ShapeDtypeStruct((m_tok, D_MODEL), F32),
                   jax.ShapeDtypeStruct((1, D_MODEL), F32), jax.ShapeDtypeStruct((1, XHD), F32)),
        in_specs=[row_spec(D_MODEL), row_spec(D_MODEL), row_spec(D_MODEL), full(kn), full(v), full(w_xo), full(w_xq), full(w_out),
                  full(g_xattn), full(g_xq)],
        out_specs=(row_spec(D_MODEL), row_spec(D_MODEL), row_spec(GROUP_W), row_spec(GROUP_W), acc(m_tok, D_MODEL),
                   acc(m_tok, D_MODEL), acc(1, D_MODEL), acc(1, XHD)),
        scratch_shapes=[pltpu.VMEM((tm, D_MODEL), BF)],
        compiler_params=_cparams(("arbitrary",)),
    )(dh2, h1, qx, kn, v, w_xo, w_xq, w_out, g_xattn, g_xq)


def _mem_kv_bwd(dkn, dv, kraw, mem, memn, g_mem, g_xk, w_xkv):
    m_tok = mem.shape[0]

    def body(dkn_ref, dv_ref, kraw_ref, mem_ref, memn_ref, gm_ref, gk_ref, w_ref, dw_ref, dgm_ref, dgk_ref, dkv_scr):
        gk = gk_ref[...]
        dgk = jnp.zeros((1, XHD), F32)
        for h in range(N_XH):
            sl = slice(h * XHD, (h + 1) * XHD)
            dx, dg_h = _rms_bwd(kraw_ref[:, sl], gk, dkn_ref[:, sl])
            dgk = dgk + dg_h
            dkv_scr[:, sl] = dx.astype(BF)
        dgk_ref[...] = dgk
        dkv_scr[:, D_MODEL:] = dv_ref[...].astype(BF)
        dkv = dkv_scr[...]
        dw_ref[...] = _dot_tn(memn_ref[...], dkv)
        dmemn = _dot_nt(dkv, w_ref[...])
        mem_v = mem_ref[...]
        r = lax.rsqrt(jnp.mean(mem_v * mem_v, axis=-1, keepdims=True) + EPS)
        dgm_ref[...] = jnp.sum(dmemn * mem_v * r, axis=0, keepdims=True)

    return pl.pallas_call(
        body, name="mem_kv_bwd",
        out_shape=(jax.ShapeDtypeStruct((D_MODEL, 2 * D_MODEL), F32), jax.ShapeDtypeStruct((1, D_MODEL), F32),
                   jax.ShapeDtypeStruct((1, XHD), F32)),
        in_specs=[VMEM_SPEC] * 8, out_specs=(VMEM_SPEC,) * 3,
        scratch_shapes=[pltpu.VMEM((m_tok, 2 * D_MODEL), BF)],
        compiler_params=_cparams(),
    )(dkn, dv, kraw, mem, memn, g_mem, g_xk, w_xkv)


def _fox_bwd(q_aug, k_aug, proj, dmf, o32, lse, sums):
    t_len = q_aug.shape[0]
    tb = min(ATT_BLOCK, t_len)
    n_b = t_len // tb
    v_col = 6 * GROUP_W // LANES
    n_w = len(sums)
    n_steps = (N_HEADS // 2) * n_b

    def body(*refs):
        k_ref, v_ref, q_ref, do_ref, o_ref, lse_ref = refs[:6]
        dq_ref, dk_ref, dv_ref, df_ref = refs[6 + n_w:10 + n_w]
        delta = refs[10 + 2 * n_w]
        comm = (refs[6:6 + n_w], refs[10 + n_w:10 + 2 * n_w]) + tuple(refs[11 + 2 * n_w:])
        j = pl.program_id(1)
        step = pl.program_id(0) * n_b + j

        @pl.when(step == 0)
        def _():
            _scatter_phase(0, *comm)

        @pl.when(j == 0)
        def _():
            dq_ref[...] = jnp.zeros_like(dq_ref)
            dd = do_ref[...].astype(F32) * o_ref[...]
            hrow = lax.broadcasted_iota(jnp.int32, (8, LANES), 0)
            lane = lax.broadcasted_iota(jnp.int32, (8, LANES), 1)
            ind = ((lane // HEAD_DIM) == hrow).astype(BF)
            delta[...] = _dot_nt_exact(ind, dd)

        k2, v2 = k_ref[...], v_ref[...]
        ks = [k2[:, hh * LANES:(hh + 1) * LANES] for hh in range(2)]
        vs = [v2[:, hh * HEAD_DIM:(hh + 1) * HEAD_DIM] for hh in range(2)]

        def blocks(idx, carry, masked, valid):
            loaded = []
            for i in idx:
                rows = pl.ds(pl.multiple_of(i * tb, tb), tb)
                q2 = q_ref[rows, :]
                do2 = do_ref[rows, :]
                loaded.append((rows, [q2[:, hh * LANES:(hh + 1) * LANES] for hh in range(2)],
                               [do2[:, hh * HEAD_DIM:(hh + 1) * HEAD_DIM] for hh in range(2)]))
            ss = [[_dot_nt(ks[hh], qs[hh]) for hh in range(2)] for _, qs, _ in loaded]
            dps = [[_dot_nt(vs[hh], dos[hh]) for hh in range(2)] for _, _, dos in loaded]
            pts, dsts, dfs = [], [], []
            for b, (rows, _, _) in enumerate(loaded):
                for hh in range(2):
                    s_t = ss[b][hh]
                    if masked[b]:
                        krow = lax.broadcasted_iota(jnp.int32, (tb, tb), 0)
                        qcol = lax.broadcasted_iota(jnp.int32, (tb, tb), 1)
                        s_t = jnp.where(qcol >= krow, s_t, NEG)
                    lse_row = lse_ref[0, hh:hh + 1, rows]
                    if valid[b] is not None:
                        lse_row = jnp.where(valid[b], lse_row, -NEG)
                    p_t = jnp.exp2(s_t - lse_row)
                    pts.append(p_t.astype(BF))
                    ds_t = p_t * (dps[b][hh] - delta[hh:hh + 1, rows])
                    dsts.append(ds_t.astype(BF))
                    dfs.append(jnp.sum(ds_t, axis=-1, keepdims=True))
            out = list(carry)
            for b, (rows, qs, dos) in enumerate(loaded):
                for hh in range(2):
                    dk, dv, df = out[hh]
                    dv = dv + jnp.dot(pts[2 * b + hh], dos[hh], preferred_element_type=F32)
                    dk = dk + jnp.dot(dsts[2 * b + hh], qs[hh], preferred_element_type=F32)
                    dq_ref[rows, hh * HEAD_DIM:(hh + 1) * HEAD_DIM] += _dot_tn(dsts[2 * b + hh], ks[hh])[:, :HEAD_DIM]
                    out[hh] = (dk, dv, df - dfs[2 * b + hh])
            return tuple(out)

        def pair(t, carry):
            i0 = j + 1 + 2 * t
            return blocks([i0, jnp.minimum(i0 + 1, n_b - 1)], carry, [False, False], [None, i0 + 1 < n_b])

        init = tuple((jnp.zeros((tb, LANES), F32), jnp.zeros((tb, HEAD_DIM), F32), jnp.zeros((tb, 1), F32)) for _ in range(2))
        carry = blocks([j], init, [True], [None])
        carry = lax.fori_loop(0, (n_b - j) // 2, pair, carry)
        dk_ref[...] = jnp.concatenate([carry[hh][0][:, :HEAD_DIM] for hh in range(2)], axis=-1) * LN2
        dv_ref[...] = jnp.concatenate([carry[hh][1] for hh in range(2)], axis=-1)
        df_ref[0] = jnp.concatenate([carry[hh][2] for hh in range(2)], axis=-1)

        @pl.when(step == n_steps - 1)
        def _():
            _scatter_phase(1, *comm)

    blk = lambda w, col0: pl.BlockSpec((tb, w), lambda hp, j: (j, col0 + hp))
    whole = lambda w: pl.BlockSpec((t_len, w), lambda hp, j: (0, hp))
    rows2 = pl.BlockSpec((1, 2, t_len), lambda hp, j: (hp, 0, 0))
    cols2 = pl.BlockSpec((1, tb, 2), lambda hp, j: (hp, j, 0))
    return pl.pallas_call(
        body, name="fox_bwd", grid=(N_HEADS // 2, n_b),
        out_shape=(jax.ShapeDtypeStruct((t_len, GROUP_W), F32), jax.ShapeDtypeStruct((t_len, GROUP_W), F32),
                   jax.ShapeDtypeStruct((t_len, GROUP_W), F32), jax.ShapeDtypeStruct((N_HEADS // 2, t_len, 2), F32))
        + _scatter_out_shapes(sums),
        in_specs=[blk(2 * LANES, 0), blk(LANES, v_col), whole(2 * LANES), whole(LANES), whole(LANES), rows2] + [ANY] * n_w,
        out_specs=(whole(LANES), blk(LANES, 0), blk(LANES, 0), cols2) + (ANY,) * n_w,
        scratch_shapes=[pltpu.VMEM((8, t_len), F32)] + _scatter_scratch(n_w),
        compiler_params=_cparams(("arbitrary", "arbitrary")),
    )(k_aug, proj, q_aug, dmf, o32, lse, *sums)


def _retention_bwd(dmr, raw, proj, g_ret, rq, rk, states, tables, parts):
    t_len = rq.shape[0]
    c = min(RET_BLOCK, t_len)
    n_b = t_len // c
    wdec, qdec, kdec, cdec = tables
    v_col, g_col = 2 * GROUP_W // LANES, 3 * GROUP_W // LANES
    n_w = len(parts)
    n_steps = (N_HEADS // 2) * n_b

    def body(*refs):
        d_ref, raw_ref, rg_ref, g_ref, q_ref, k_ref, v_ref, st_ref, w_ref, qd_ref, kd_ref, cd_ref = refs[:12]
        dq_ref, dk_ref, dv_ref, drg_ref, dg_ref = refs[12 + n_w:17 + n_w]
        gstate = refs[17 + 2 * n_w]
        comm = (refs[12:12 + n_w], refs[17 + n_w:17 + 2 * n_w]) + tuple(refs[18 + 2 * n_w:])
        step = pl.program_id(0) * n_b + pl.program_id(1)

        @pl.when(step == 0)
        def _():
            _exchange_phase(0, *comm)

        @pl.when(pl.program_id(1) == 0)
        def _():
            gstate[...] = jnp.zeros_like(gstate)
            dg_ref[...] = jnp.zeros_like(dg_ref)

        d, raw_v, g = d_ref[...], raw_ref[...], g_ref[0]
        gate = rg_ref[...].astype(F32)
        xc = raw_v - _group_mean64(raw_v)
        r = lax.rsqrt(_group_mean64(xc * xc) + EPS)
        xh = xc * r
        sg = _sigmoid(gate)
        drg_ref[...] = d * (xh * g) * (sg * (1.0 + gate * (1.0 - sg)))
        dy = d * (gate * sg)
        dg_ref[0] += jnp.sum(dy * xh, axis=0, keepdims=True)
        dxh = dy * g
        do2 = r * (dxh - _group_mean64(dxh) - xh * _group_mean64(dxh * xh))
        q2, k2, v2 = q_ref[...], k_ref[...], v_ref[...]
        dqs, dks, dvs = [], [], []
        for hh in range(2):
            sl = slice(hh * HEAD_DIM, (hh + 1) * HEAD_DIM)
            q, k, v, do = q2[:, sl], k2[:, sl], v2[:, sl], do2[:, sl].astype(BF)
            w = w_ref[hh]
            a = _dot_nt(q, k) * w
            dm = _dot_nt(do, v) * w
            sp, gs = st_ref[0, 0, hh], gstate[hh]
            qd = q.astype(F32) * qd_ref[hh]
            kd = k.astype(F32) * kd_ref[hh]
            dqs.append(_dot(dm, k) + _dot_nt(do, sp) * qd_ref[hh])
            dks.append(_dot_tn(dm, q) + _dot_nt(v, gs) * kd_ref[hh])
            dvs.append(_dot_tn(a, do) + _dot(kd, gs))
            gstate[hh] = gs * cd_ref[hh] + _dot_tn(qd, do)
        dq_ref[...] = jnp.concatenate(dqs, axis=-1)
        dk_ref[...] = jnp.concatenate(dks, axis=-1)
        dv_ref[...] = jnp.concatenate(dvs, axis=-1)

        @pl.when(step == n_steps - 1)
        def _():
            _exchange_phase(1, *comm)

    blk = lambda col0: pl.BlockSpec((c, LANES), lambda hp, i: (n_b - 1 - i, col0 + hp))
    tab = lambda a: pl.BlockSpec((2,) + a.shape[1:], lambda hp, i: (hp, 0, 0))
    gspec = pl.BlockSpec((1, 1, LANES), lambda hp, i: (hp, 0, 0))
    return pl.pallas_call(
        body, name="retention_bwd", grid=(N_HEADS // 2, n_b),
        out_shape=(jax.ShapeDtypeStruct((t_len, GROUP_W), F32),) * 4 + (jax.ShapeDtypeStruct((N_HEADS // 2, 1, LANES), F32),)
        + _exchange_out_shapes(parts),
        in_specs=[blk(0), blk(0), blk(g_col), gspec, blk(0), blk(0), blk(v_col),
                  pl.BlockSpec((1, 1, 2, HEAD_DIM, HEAD_DIM), lambda hp, i: (hp, n_b - 1 - i, 0, 0, 0)),
                  tab(wdec), tab(qdec), tab(kdec), tab(cdec)] + [ANY] * n_w,
        out_specs=(blk(0), blk(0), blk(0), blk(0), gspec) + (ANY,) * n_w,
        scratch_shapes=[pltpu.VMEM((2, HEAD_DIM, HEAD_DIM), F32)] + _exchange_scratch(n_w),
        compiler_params=_cparams(("arbitrary", "arbitrary")),
    )(dmr, raw, proj, g_ret, rq, rk, proj, states, wdec, qdec, kdec, cdec, *parts)


def _in_proj_bwd(x, g_mix, dh1, dq_r, dk_r, dv_r, drg, dq_f, dk_f, dv_f, df_col, proj, z, cos_t, sin_t, gq_t, gk_t, w_in_t):
    t_len = x.shape[0]
    tm = min(ROW_TILE, t_len)
    n_t = t_len // tm

    def body(x_ref, g_ref, dh1_ref, dqr_ref, dkr_ref, dvr_ref, drg_ref, dqf_ref, dkf_ref, dvf_ref, df_ref, fq_ref, fk_ref, z_ref,
             cos_ref, sin_ref, gq_ref, gk_ref, wm_ref, wf_ref,
             dproj_ref, dz_ref, dx_ref, dg_ref, dgq_ref, dgk_ref, db_ref, carry, gq_acc, gk_acc):
        i = pl.program_id(0)

        @pl.when(i == 0)
        def _():
            carry[...] = jnp.zeros_like(carry)
            gq_acc[...] = jnp.zeros_like(gq_acc)
            gk_acc[...] = jnp.zeros_like(gk_acc)
            dg_ref[...] = jnp.zeros_like(dg_ref)
            db_ref[...] = jnp.zeros_like(db_ref)

        c, s = cos_ref[...], sin_ref[...]
        gq, gk = gq_ref[...], gk_ref[...]
        dgq = jnp.zeros((1, LANES), F32)
        dgk = jnp.zeros((1, LANES), F32)
        for sl in _chunks(GROUP_W):
            dy = dqr_ref[:, sl] * 0.125
            dproj_ref[:, sl] = (dy * c + _swap32(dy * s)).astype(BF)
            dy = dkr_ref[:, sl]
            dproj_ref[:, GROUP_W + sl.start:GROUP_W + sl.stop] = (dy * c + _swap32(dy * s)).astype(BF)
            dproj_ref[:, 2 * GROUP_W + sl.start:2 * GROUP_W + sl.stop] = dvr_ref[:, sl].astype(BF)
            dproj_ref[:, 3 * GROUP_W + sl.start:3 * GROUP_W + sl.stop] = drg_ref[:, sl].astype(BF)
            for src, dsrc, gain, off in ((fq_ref, dqf_ref, gq, 4), (fk_ref, dkf_ref, gk, 5)):
                xr = src[:, sl].astype(F32)
                r = lax.rsqrt(_group_mean64(xr * xr) + EPS)
                xh = xr * r
                dy = dsrc[:, sl] * (0.125 if off == 4 else 1.0)
                dgs = jnp.sum(dy * xh, axis=0, keepdims=True)
                if off == 4:
                    dgq = dgq + dgs
                else:
                    dgk = dgk + dgs
                dxh = dy * gain
                dproj_ref[:, off * GROUP_W + sl.start:off * GROUP_W + sl.stop] = \
                    (r * (dxh - xh * _group_mean64(dxh * xh))).astype(BF)
            dproj_ref[:, 6 * GROUP_W + sl.start:6 * GROUP_W + sl.stop] = dvf_ref[:, sl].astype(BF)
        gq_acc[...] += dgq
        gk_acc[...] += dgk
        row = lax.broadcasted_iota(jnp.int32, (tm, tm), 0)
        col = lax.broadcasted_iota(jnp.int32, (tm, tm), 1)
        dlf = _dot_exact((col >= row).astype(BF), df_ref[...]) + carry[0:1, :]
        carry[...] = jnp.broadcast_to(dlf[0:1, :], carry.shape)
        lane = lax.broadcasted_iota(jnp.int32, (tm, LANES), 1)
        dz = jnp.where(lane < N_HEADS, dlf / (1.0 + jnp.exp(z_ref[...])), 0.0)
        db_ref[...] += jnp.sum(dz, axis=0, keepdims=True)
        dz_bf = dz.astype(BF)
        dz_ref[...] = dz_bf
        dn1 = jnp.dot(dz_bf, wf_ref[...], preferred_element_type=F32)
        for sec in range(MAIN_W // GROUP_W):
            sl = slice(sec * GROUP_W, (sec + 1) * GROUP_W)
            dn1 = dn1 + jnp.dot(dproj_ref[:, sl], wm_ref[sl, :], preferred_element_type=F32)
        dx, dg = _rms_bwd(x_ref[...], g_ref[...], dn1)
        dx_ref[...] = dh1_ref[...] + dx
        dg_ref[...] += dg

        @pl.when(i == n_t - 1)
        def _():
            dgq_ref[...] = gq_acc[:, :HEAD_DIM] + gq_acc[:, HEAD_DIM:]
            dgk_ref[...] = gk_acc[:, :HEAD_DIM] + gk_acc[:, HEAD_DIM:]

    row_spec = lambda w, col=0: pl.BlockSpec((tm, w), lambda i: (n_t - 1 - i, col))
    full = lambda a: pl.BlockSpec(a.shape, lambda i: (0,) * a.ndim)
    acc = lambda r, c: pl.BlockSpec((r, c), lambda i: (0, 0))
    return pl.pallas_call(
        body, name="in_proj_bwd", grid=(n_t,),
        out_shape=(jax.ShapeDtypeStruct((t_len, MAIN_W), BF), jax.ShapeDtypeStruct((t_len, LANES), BF),
                   jax.ShapeDtypeStruct((t_len, D_MODEL), F32), jax.ShapeDtypeStruct((1, D_MODEL), F32),
                   jax.ShapeDtypeStruct((1, HEAD_DIM), F32), jax.ShapeDtypeStruct((1, HEAD_DIM), F32),
                   jax.ShapeDtypeStruct((1, LANES), F32)),
        in_specs=[row_spec(D_MODEL), full(g_mix), row_spec(D_MODEL)] + [row_spec(GROUP_W)] * 7
        + [row_spec(LANES), row_spec(GROUP_W, 4), row_spec(GROUP_W, 5), row_spec(LANES), row_spec(LANES), row_spec(LANES),
           full(gq_t), full(gk_t), *_w_in_specs()],
        out_specs=(row_spec(MAIN_W), row_spec(LANES), row_spec(D_MODEL), acc(1, D_MODEL), acc(1, HEAD_DIM), acc(1, HEAD_DIM),
                   acc(1, LANES)),
        scratch_shapes=[pltpu.VMEM((8, LANES), F32), pltpu.VMEM((1, LANES), F32), pltpu.VMEM((1, LANES), F32)],
        compiler_params=_cparams(("arbitrary",)),
    )(x, g_mix, dh1, dq_r, dk_r, dv_r, drg, dq_f, dk_f, dv_f, df_col, proj, proj, z, cos_t, sin_t, gq_t, gk_t, w_in_t, w_in_t)


def _matmul_tn(a, b, name, bk=512):
    t_len, m = a.shape
    n = b.shape[1]
    bm = m if m <= TN_MAX_ROWS else m // 2
    bk = min(bk, t_len)

    def body(a_ref, b_ref, o_ref):
        @pl.when(pl.program_id(1) == 0)
        def _():
            o_ref[...] = jnp.zeros_like(o_ref)

        o_ref[...] += _dot_tn(a_ref[...], b_ref[...])

    return pl.pallas_call(
        body, name=name, grid=(m // bm, t_len // bk),
        out_shape=jax.ShapeDtypeStruct((m, n), F32),
        in_specs=[pl.BlockSpec((bk, bm), lambda i, k: (k, i)), pl.BlockSpec((bk, n), lambda i, k: (k, 0))],
        out_specs=pl.BlockSpec((bm, n), lambda i, k: (i, 0)),
        compiler_params=_cparams(("arbitrary", "arbitrary")),
    )(a, b)


def _place():
    x, y, c = lax.axis_index("x"), lax.axis_index("y"), lax.axis_index("c")
    chips = [(1 - x, y), (x, 1 - y), (1 - x, 1 - y)]
    return x, y, c, chips


def _row_chunks(rows, limit):
    step = max(d for d in range(16, min(rows, limit) + 1, 16) if rows % d == 0)
    return [slice(i, i + step) for i in range(0, rows, step)]


ICI_CHUNK_ROWS = 128
D2D_CHUNK_ROWS = 64


def _gather_phase(phase, ins, outs, send_sems, recv_sems):
    x, y, c, chips = _place()
    me_chip = 2 * x + y
    sibling = (x, y, 1 - c)

    def copy(w, k, slot, half, to, rows=slice(None), src=None):
        dst = outs[w].at[slot, half, rows]
        return pltpu.make_async_remote_copy(src_ref=dst if src is None else src, dst_ref=dst,
                                            send_sem=send_sems.at[w, k], recv_sem=recv_sems.at[w, k],
                                            device_id=to, device_id_type=MESH)

    for w in range(len(ins)):
        for j, (px, py) in enumerate(chips):
            if phase == 0:
                for rows in _row_chunks(ins[w].shape[1], ICI_CHUNK_ROWS):
                    copy(w, j, me_chip, c, (px, py, c), rows, src=ins[w].at[c, rows]).start()
            elif phase == 1:
                copy(w, j, 2 * px + py, c, (x, y, c)).wait_recv()
                for rows in _row_chunks(ins[w].shape[1], D2D_CHUNK_ROWS):
                    copy(w, 3 + j, 2 * px + py, c, sibling, rows).start()
            else:
                copy(w, 3 + j, 2 * px + py, 1 - c, (x, y, c)).wait_recv()
                copy(w, j, me_chip, c, (px, py, c), src=ins[w].at[c]).wait_send()
                copy(w, 3 + j, 2 * px + py, c, sibling).wait_send()


def _gather_scratch(n_w):
    return [pltpu.SemaphoreType.DMA((n_w, 6)), pltpu.SemaphoreType.DMA((n_w, 6))]


def _all_gather_weights(shards):
    n_w = len(shards)

    def body(*refs):
        for phase in range(3):
            _gather_phase(phase, refs[:n_w], refs[n_w:2 * n_w], *refs[2 * n_w:])

    return pl.pallas_call(
        body, name="all_gather_weights",
        out_shape=tuple(jax.ShapeDtypeStruct((4,) + s.shape, s.dtype) for s in shards),
        in_specs=[ANY] * n_w, out_specs=(ANY,) * n_w, scratch_shapes=_gather_scratch(n_w),
    )(*shards)


def _exchange_phase(phase, ins, theirs, send_sems, recv_sems):
    x, y, c, _ = _place()

    def remote(w, k=slice(None), rows=slice(None)):
        return pltpu.make_async_remote_copy(src_ref=ins[w].at[k, 1 - c, rows], dst_ref=theirs[w].at[k, rows],
                                            send_sem=send_sems.at[w], recv_sem=recv_sems.at[w], device_id=(x, y, 1 - c),
                                            device_id_type=MESH)

    for w in range(len(ins)):
        if phase == 0:
            for k in range(4):
                for rows in _row_chunks(ins[w].shape[2], D2D_CHUNK_ROWS):
                    remote(w, k, rows).start()
        else:
            remote(w).wait()


def _exchange_scratch(n_w):
    return [pltpu.SemaphoreType.DMA((n_w,)), pltpu.SemaphoreType.DMA((n_w,))]


def _exchange_out_shapes(grads):
    return tuple(jax.ShapeDtypeStruct((4,) + g.shape[2:], g.dtype) for g in grads)


def _exchange_core_halves(grads):
    n_w = len(grads)

    def body(*refs):
        for phase in range(2):
            _exchange_phase(phase, refs[:n_w], refs[n_w:2 * n_w], *refs[2 * n_w:])

    return pl.pallas_call(
        body, name="exchange_core_halves", out_shape=_exchange_out_shapes(grads),
        in_specs=[ANY] * n_w, out_specs=(ANY,) * n_w, scratch_shapes=_exchange_scratch(n_w),
    )(*grads)


def _add_pairs(part, theirs, name):
    _, _, r, c = part.shape
    rb = 32 if r % 32 == 0 else r

    def body(a_ref, b_ref, own_ref, ob_ref):
        my_chip = 2 * lax.axis_index("x") + lax.axis_index("y")
        ob_ref[...] = (a_ref[...] + b_ref[...]).astype(BF)
        own_ref[...] = a_ref[my_chip] + b_ref[my_chip]

    spec = pl.BlockSpec((4, rb, c), lambda i: (0, i, 0))
    return pl.pallas_call(
        body, name=name, grid=(r // rb,),
        out_shape=(jax.ShapeDtypeStruct((r, c), F32), jax.ShapeDtypeStruct((4, r, c), BF)),
        in_specs=[pl.BlockSpec((4, None, rb, c), lambda i: (0, lax.axis_index("c"), i, 0)), spec],
        out_specs=(pl.BlockSpec((rb, c), lambda i: (i, 0)), spec), compiler_params=_cparams(("arbitrary",)),
    )(part, theirs)


def _scatter_phase(phase, bfs, got, send_sems, recv_sems):
    x, y, c, chips = _place()

    def remote(w, j, px, py, rows=slice(None)):
        return pltpu.make_async_remote_copy(src_ref=bfs[w].at[2 * px + py, rows], dst_ref=got[w].at[j, rows],
                                            send_sem=send_sems.at[w, j], recv_sem=recv_sems.at[w, j], device_id=(px, py, c),
                                            device_id_type=MESH)

    for w in range(len(bfs)):
        for j, (px, py) in enumerate(chips):
            if phase == 0:
                for rows in _row_chunks(bfs[w].shape[1], ICI_CHUNK_ROWS):
                    remote(w, j, px, py, rows).start()
            else:
                remote(w, j, px, py).wait()


def _scatter_scratch(n_w):
    return [pltpu.SemaphoreType.DMA((n_w, 3)), pltpu.SemaphoreType.DMA((n_w, 3))]


def _scatter_out_shapes(sums_bf16):
    return tuple(jax.ShapeDtypeStruct((3,) + s.shape[1:], BF) for s in sums_bf16)


def _add_received(own, got, name):
    r, c = own.shape
    rb = 32 if r % 32 == 0 else r

    def body(o_ref, g_ref, out_ref):
        out_ref[...] = ((o_ref[...] + g_ref[0].astype(F32)) + g_ref[1].astype(F32)) + g_ref[2].astype(F32)

    return pl.pallas_call(
        body, name=name, grid=(r // rb,), out_shape=jax.ShapeDtypeStruct((r, c), F32),
        in_specs=[pl.BlockSpec((rb, c), lambda i: (i, 0)), pl.BlockSpec((3, rb, c), lambda i: (0, i, 0))],
        out_specs=pl.BlockSpec((rb, c), lambda i: (i, 0)), compiler_params=_cparams(("arbitrary",)),
    )(own, got)


def _share_with_sibling(halves):
    n_w = len(halves)

    def body(*refs):
        ins, outs = refs[:n_w], refs[n_w:2 * n_w]
        send_sems, recv_sems = refs[2 * n_w:]
        x, y, c, _ = _place()

        def remote(w, rows=slice(None)):
            return pltpu.make_async_remote_copy(src_ref=ins[w].at[rows], dst_ref=outs[w].at[c, rows], send_sem=send_sems.at[w],
                                                recv_sem=recv_sems.at[w], device_id=(x, y, 1 - c), device_id_type=MESH)

        for w in range(n_w):
            for rows in _row_chunks(ins[w].shape[0], D2D_CHUNK_ROWS):
                remote(w, rows).start()
        for w in range(n_w):
            remote(w).wait()

    return pl.pallas_call(
        body, name="share_with_sibling",
        out_shape=tuple(jax.ShapeDtypeStruct((2,) + h.shape, h.dtype) for h in halves),
        in_specs=[ANY] * n_w, out_specs=(ANY,) * n_w,
        scratch_shapes=[pltpu.SemaphoreType.DMA((n_w,)), pltpu.SemaphoreType.DMA((n_w,))],
    )(*halves)


def _all_reduce_small(pack):
    r, c = pack.shape

    def body(p_ref, out_ref, slots, send_sems, recv_sems):
        x, y, cc, _ = _place()
        me = 4 * x + 2 * y + cc
        slots[me] = p_ref[...]
        copies = []
        for k in range(1, 8):
            dx, dy, dc = (k >> 2) & 1, (k >> 1) & 1, k & 1
            to = (1 - x if dx else x, 1 - y if dy else y, 1 - cc if dc else cc)
            cp = pltpu.make_async_remote_copy(src_ref=p_ref, dst_ref=slots.at[me], send_sem=send_sems.at[k - 1],
                                              recv_sem=recv_sems.at[k - 1], device_id=to, device_id_type=MESH)
            cp.start()
            copies.append(cp)
        for cp in copies:
            cp.wait()
        total = slots[0]
        for d in range(1, 8):
            total = total + slots[d]
        out_ref[...] = total

    return pl.pallas_call(
        body, name="all_reduce_small", out_shape=jax.ShapeDtypeStruct((r, c), F32),
        in_specs=[VMEM_SPEC], out_specs=VMEM_SPEC,
        scratch_shapes=[pltpu.VMEM((8, r, c), F32), pltpu.SemaphoreType.DMA((7,)), pltpu.SemaphoreType.DMA((7,))],
    )(pack)


def _adamw_update(w_ref, g_ref, m_ref, v_ref, d_ref, nm_ref, nv_ref):
    gv = g_ref[...]
    nm = ADAM_B1 * m_ref[...] + (1.0 - ADAM_B1) * gv
    nv = ADAM_B2 * v_ref[...] + (1.0 - ADAM_B2) * (gv * gv)
    nm_ref[...] = nm
    nv_ref[...] = nv
    m_hat = nm / (1.0 - ADAM_B1 ** ADAM_STEP)
    v_hat = nv / (1.0 - ADAM_B2 ** ADAM_STEP)
    d_ref[...] = -ADAM_LR * (m_hat / (jnp.sqrt(v_hat) + ADAM_EPS) + ADAM_WD * w_ref[...])


def _adamw_many(ws, gs, ms, vs, sums):
    n_a, n_w = len(ws), len(sums)
    n_steps = ADAM_STEPS
    specs = [pl.BlockSpec((w.shape[0] // n_steps, w.shape[1]), lambda i: (i, 0)) for w in ws]

    def body(*refs):
        ins = refs[:4 * n_a]
        outs = refs[4 * n_a + n_w:7 * n_a + n_w]
        comm = (refs[4 * n_a:4 * n_a + n_w], refs[7 * n_a + n_w:7 * n_a + 2 * n_w]) + tuple(refs[7 * n_a + 2 * n_w:])
        step = pl.program_id(0)

        @pl.when(step == 0)
        def _():
            _scatter_phase(0, *comm)

        for a in range(n_a):
            _adamw_update(*(ins[k * n_a + a] for k in range(4)), *(outs[3 * a + k] for k in range(3)))

        @pl.when(step == n_steps - 1)
        def _():
            _scatter_phase(1, *comm)

    flat = pl.pallas_call(
        body, name="adamw_late", grid=(n_steps,),
        out_shape=tuple(jax.ShapeDtypeStruct(w.shape, F32) for w in ws for _ in range(3)) + _scatter_out_shapes(sums),
        in_specs=specs * 4 + [ANY] * n_w, out_specs=tuple(s for s in specs for _ in range(3)) + (ANY,) * n_w,
        scratch_shapes=_scatter_scratch(n_w), compiler_params=_cparams(("arbitrary",)),
    )(*ws, *gs, *ms, *vs, *sums)
    return [tuple(flat[3 * a:3 * a + 3]) for a in range(n_a)] + list(flat[3 * n_a:])


def _adamw(w, g, m, v, name):
    r, c = w.shape
    rb, cb = (64, c) if r % 64 == 0 else (r, LANES if (r % 8 and c % LANES == 0) else c)

    def body(*refs):
        _adamw_update(*refs)

    spec = pl.BlockSpec((rb, cb), lambda i, j: (i, j))
    return pl.pallas_call(
        body, name=name, grid=(r // rb, c // cb), out_shape=(jax.ShapeDtypeStruct((r, c), F32),) * 3,
        in_specs=[spec] * 4, out_specs=(spec,) * 3, compiler_params=_cparams(("arbitrary", "arbitrary")),
    )(w, g, m, v)


def _rope_tables(t_len):
    inv_freq = ROPE_BASE ** (-jnp.arange(0, HEAD_DIM, 2, dtype=F32) / HEAD_DIM)
    ang = jnp.arange(t_len, dtype=F32)[:, None] * inv_freq[None, :]
    cos, sin = jnp.cos(ang), jnp.sin(ang)
    cos_t = jnp.concatenate([cos, cos, cos, cos], axis=-1)
    sin_t = jnp.concatenate([-sin, sin, -sin, sin], axis=-1)
    return cos_t, sin_t


def _cols_to_shards(dw):
    r, n = dw.shape
    return jnp.transpose(dw.reshape(2, r // 2, 4, n // 4), (2, 0, 1, 3))


def _rows_to_shards(dw):
    r, n = dw.shape
    padded = _pad_rows(dw.reshape(4, r // 4, n))
    return padded.reshape(4, 2, padded.shape[1] // 2, n)


def _pad_lanes(a):
    extra = -a.shape[-1] % LANES
    return a if extra == 0 else jnp.pad(a, [(0, 0)] * (a.ndim - 1) + [(0, extra)])


def _pad_rows(a):
    rows = a.shape[-2]
    extra = 0 if rows % SHARD_ROW_ALIGN == 0 else -rows % SHARD_ROW_PAD
    return a if extra == 0 else jnp.pad(a, [(0, 0)] * (a.ndim - 2) + [(0, extra), (0, 0)])


def _pad_row(a, width=D_MODEL):
    a = a.reshape(1, -1)
    return jnp.pad(a, ((0, 0), (0, width - a.shape[1])))


def kernel(x, mem, g_mix, w_in, b_forget, g_ret_out, g_fox_q, g_fox_k, w_out, g_xattn, w_xq, w_xkv, g_mem, g_xq, g_xk, w_xo, g_ffn, w_gate, w_up, w_down, loss_target, m_g_mix, m_w_in, m_b_forget, m_g_ret_out, m_g_fox_q, m_g_fox_k, m_w_out, m_g_xattn, m_w_xq, m_w_xkv, m_g_mem, m_g_xq, m_g_xk, m_w_xo, m_g_ffn, m_w_gate, m_w_up, m_w_down, v_g_mix, v_w_in, v_b_forget, v_g_ret_out, v_g_fox_q, v_g_fox_k, v_w_out, v_g_xattn, v_w_xq, v_w_xkv, v_g_mem, v_g_xq, v_g_xk, v_w_xo, v_g_ffn, v_w_gate, v_w_up, v_w_down):
    big = {"w_in": (w_in, m_w_in, v_w_in), "w_out": (w_out, m_w_out, v_w_out), "w_xq": (w_xq, m_w_xq, v_w_xq),
           "w_xkv": (w_xkv, m_w_xkv, v_w_xkv), "w_xo": (w_xo, m_w_xo, v_w_xo), "w_gate": (w_gate, m_w_gate, v_w_gate),
           "w_up": (w_up, m_w_up, v_w_up), "w_down": (w_down, m_w_down, v_w_down)}
    for n in TRANSPOSED:
        big[n] = tuple(jnp.swapaxes(a, 1, 2) for a in big[n])
    shards = {}
    for n in big:
        w = _pad_rows(_pad_lanes(big[n][0][0].astype(BF)))
        shards[n] = w.reshape(2, w.shape[0] // 2, w.shape[1])
    sizes = {n: big[n][0].shape[1:] for n in big}
    w_in_full = _assemble_weight("w_in", _all_gather_weights([shards["w_in"]])[0], shards["w_in"], sizes["w_in"])
    small_w ={"g_mix": g_mix, "b_forget": b_forget, "g_ret_out": g_ret_out, "g_fox_q": g_fox_q, "g_fox_k": g_fox_k,
               "g_xattn": g_xattn, "g_mem": g_mem, "g_xq": g_xq, "g_xk": g_xk, "g_ffn": g_ffn}
    m_small = {"g_mix": m_g_mix, "b_forget": m_b_forget, "g_ret_out": m_g_ret_out, "g_fox_q": m_g_fox_q, "g_fox_k": m_g_fox_k,
               "g_xattn": m_g_xattn, "g_mem": m_g_mem, "g_xq": m_g_xq, "g_xk": m_g_xk, "g_ffn": m_g_ffn}
    v_small = {"g_mix": v_g_mix, "b_forget": v_b_forget, "g_ret_out": v_g_ret_out, "g_fox_q": v_g_fox_q, "g_fox_k": v_g_fox_k,
               "g_xattn": v_g_xattn, "g_mem": v_g_mem, "g_xq": v_g_xq, "g_xk": v_g_xk, "g_ffn": v_g_ffn}
    loss_part, grad_x, sums, got, in_bf, small_g = _local_step(x[0], mem[0], loss_target[0], w_in_full, shards, sizes, small_w)
    return _reduce_and_update(big, sums, got, in_bf, small_w, small_g, loss_part, grad_x, m_small, v_small)


def _assemble_weight(name, gathered, own, size):
    rows, width = size
    my_chip = 2 * lax.axis_index("x") + lax.axis_index("y")
    g = lax.dynamic_update_slice(gathered, own[None], (my_chip, 0, 0, 0))
    g = g.reshape(4, 2 * g.shape[2], g.shape[3])[:, :rows, :width]
    return jnp.transpose(g, (1, 0, 2)).reshape(rows, 4 * width) if name in COL_SHARDED else g.reshape(4 * rows, width)


def _shard_parts(names, dw):
    return [_pad_lanes(_cols_to_shards(dw[n]) if n in COL_SHARDED else _rows_to_shards(dw[n])) for n in names]


def _core_sums(names, parts, theirs):
    return [_add_pairs(p, t, f"core_sum_{n}") for n, p, t in zip(names, parts, theirs)]


def _local_step(xs, mems, tgt, w_in_full, shards, sizes, small_w):
    g_mix, b_forget, g_ret_out, g_fox_q, g_fox_k = (small_w[n] for n in ("g_mix", "b_forget", "g_ret_out", "g_fox_q", "g_fox_k"))
    g_xattn, g_mem, g_xq, g_xk, g_ffn = (small_w[n] for n in ("g_xattn", "g_mem", "g_xq", "g_xk", "g_ffn"))
    w_in_t = jnp.pad(w_in_full, ((0, MAIN_W + LANES - IN_W), (0, 0)))
    t_len = xs.shape[0]
    cos_t, sin_t = _rope_tables(t_len)
    tables = _decay_tables(min(RET_BLOCK, t_len))
    gq_t = jnp.concatenate([g_fox_q, g_fox_q], axis=-1)
    gk_t = jnp.concatenate([g_fox_k, g_fox_k], axis=-1)
    b_pad = _pad_row(b_forget, LANES)
    g_ret = g_ret_out.reshape(N_HEADS // 2, 1, LANES)

    n1, proj, rq, rk, q_aug, k_aug, z = _in_proj_fwd(xs, g_mix, w_in_t, b_pad, cos_t, sin_t, gq_t, gk_t)
    raw, mix_r, states = _retention_fwd(rq, rk, proj, g_ret, tables)
    mix_f, o32, lse, *gathered = _fox_fwd(q_aug, k_aug, proj, [shards[n] for n in LATE])
    full = {n: _assemble_weight(n, g, shards[n], sizes[n]) for n, g in zip(LATE, gathered)}
    memn, kraw, kn, vmem = _mem_kv_fwd(mems, g_mem, full["w_xkv"], g_xk)
    h1, hn2, qx, o_x, h2 = _attn_out_xattn_fwd(xs, mix_r, mix_f, full["w_out"], g_xattn, full["w_xq"], g_xq, kn, vmem, full["w_xo"])
    hn3, gate, up, act, dh3, loss_part = _ffn_loss_fwd(h2, g_ffn, full["w_gate"], full["w_up"], full["w_down"], tgt)

    dgate, dup, dh2, dg_ffn = _ffn_bwd(dh3, gate, up, h2, g_ffn, full["w_gate"], full["w_up"], full["w_down"])
    dqx, dh1, dmr, dmf, dkn, dvm, dg_xattn, dg_xq = _attn_out_xattn_bwd(dh2, h1, qx, kn, vmem, full["w_xo"], full["w_xq"],
                                                                      full["w_out"], g_xattn, g_xq)
    dw_xkv, dg_mem, dg_xk = _mem_kv_bwd(dkn, dvm, kraw, mems, memn, g_mem, g_xk, full["w_xkv"])
    dw = {
        "w_out": jnp.concatenate([_matmul_tn(mix_r, dh1, "dw_out_ret"), _matmul_tn(mix_f, dh1, "dw_out_fox")], axis=0),
        "w_xq": _matmul_tn(hn2, dqx, "dw_xq"),
        "w_xkv": dw_xkv,
        "w_xo": _matmul_tn(o_x, dh2, "dw_xo"),
        "w_gate": _matmul_tn(dgate, hn3, "dw_gate"),
        "w_up": _matmul_tn(dup, hn3, "dw_up"),
        "w_down": _matmul_tn(act, dh3, "dw_down"),
    }
    late_parts = _shard_parts(LATE, dw)
    dq_r, dk_r, dv_r, drg, dg_ret, *late_theirs = _retention_bwd(dmr, raw, proj, g_ret, rq, rk, states, tables, late_parts)
    late_sums = _core_sums(LATE, late_parts, late_theirs)
    dq_f, dk_f, dv_f, df, *late_got = _fox_bwd(q_aug, k_aug, proj, dmf, o32, lse, [s[1] for s in late_sums])
    df_col = jnp.pad(jnp.transpose(df, (1, 0, 2)).reshape(t_len, N_HEADS), ((0, 0), (0, LANES - N_HEADS)))
    dproj, dz, grad_x, dg_mix, dg_fq, dg_fk, db = _in_proj_bwd(xs, g_mix, dh1, dq_r, dk_r, dv_r, drg, dq_f, dk_f, dv_f, df_col,
                                                              proj, z, cos_t, sin_t, gq_t, gk_t, w_in_t)

    dw_in = jnp.concatenate([_matmul_tn(dproj, n1, "dw_in_main"), _matmul_tn(dz, n1, "dw_in_ff")[:IN_W - MAIN_W]], axis=0)
    in_parts = _shard_parts(("w_in",), {"w_in": dw_in})
    in_sums = _core_sums(("w_in",), in_parts, _exchange_core_halves(in_parts))
    sums = {n: s[0] for n, s in zip(("w_in",) + LATE, in_sums + late_sums)}
    got = dict(zip(LATE, late_got))
    in_bf = in_sums[0][1]
    small_g = {"g_mix": dg_mix, "b_forget": db[:, :N_HEADS], "g_ret_out": dg_ret, "g_fox_q": dg_fq, "g_fox_k": dg_fk,
               "g_xattn": dg_xattn, "g_mem": dg_mem, "g_xq": dg_xq, "g_xk": dg_xk, "g_ffn": dg_ffn}
    return loss_part, grad_x, sums, got, in_bf, small_g


def _final_grads(names, big, sums, got):
    my_core = lax.axis_index("c")
    finals = [_add_received(sums[n], got[n], f"chip_sum_{n}") for n in names]
    shared = _share_with_sibling(finals)
    out = {}
    for n, s, fin in zip(names, shared, finals):
        s = lax.dynamic_update_slice(s, fin[None], (my_core, 0, 0))
        out[n] = s.reshape(2 * s.shape[1], s.shape[2])[:big[n][0].shape[1], :big[n][0].shape[2]]
    return out


def _reduce_and_update(big, sums, got, in_bf, small_w, small_g, loss_part, grad_x, m_small, v_small):
    grads = _final_grads(LATE, big, sums, got)
    *late_updates, in_got = _adamw_many([big[n][0][0] for n in LATE], [grads[n] for n in LATE], [big[n][1][0] for n in LATE],
                                        [big[n][2][0] for n in LATE], [in_bf])
    updates = dict(zip(LATE, late_updates))
    grads.update(_final_grads(("w_in",), big, sums, {"w_in": in_got}))
    updates["w_in"] = _adamw(big["w_in"][0][0], grads["w_in"], big["w_in"][1][0], big["w_in"][2][0], "adamw_w_in")
    deltas, new_m, new_v = {}, {}, {}
    for n in big:
        restore = (lambda a: jnp.swapaxes(a[None], 1, 2)) if n in TRANSPOSED else (lambda a: a[None])
        grads[n] = restore(grads[n])
        deltas[n], new_m[n], new_v[n] = (restore(a) for a in updates[n])

    small_names = list(small_w)
    pad_rows = SMALL_ROWS - len(small_names) - 1
    stack = lambda d: jnp.concatenate([_pad_row(d[n]) for n in small_names] + [jnp.zeros((pad_rows + 1, D_MODEL), F32)], axis=0)
    g_pack = jnp.concatenate([_pad_row(small_g[n]) for n in small_names] + [_pad_row(loss_part[0:1, 0:1])]
                             + [jnp.zeros((pad_rows, D_MODEL), F32)], axis=0)
    g_tot = _all_reduce_small(g_pack)
    d_s, m_s, v_s = _adamw(stack(small_w), g_tot, stack(m_small), stack(v_small), "adamw_small")
    for i, n in enumerate(small_names):
        shape = small_w[n].shape
        size = int(np.prod(shape))
        grads[n] = g_tot[i, :size].reshape(shape)
        deltas[n], new_m[n], new_v[n] = d_s[i, :size].reshape(shape), m_s[i, :size].reshape(shape), v_s[i, :size].reshape(shape)
    loss = g_tot[len(small_names), 0]

    order = ["g_mix", "w_in", "b_forget", "g_ret_out", "g_fox_q", "g_fox_k", "w_out", "g_xattn", "w_xq", "w_xkv", "g_mem", "g_xq",
             "g_xk", "w_xo", "g_ffn", "w_gate", "w_up", "w_down"]
    return (loss, grad_x[None], *[grads[n] for n in order], *[deltas[n] for n in order], *[new_m[n] for n in order],
            *[new_v[n] for n in order])
```

```python
import functools

import numpy as np
import jax
import jax.numpy as jnp
from jax import lax
from jax.experimental import pallas as pl
from jax.experimental.pallas import tpu as pltpu

F32 = jnp.float32
BF = jnp.bfloat16

D_MODEL = 1024
HEAD_DIM = 64
N_HEADS = 8
GROUP_W = 512
N_XH = 4
XHD = 256
D_FF = 2816
MAIN_W = 3584
IN_W = 3592
ROPE_BASE = 10000.0
LOG2E = 1.4426950408889634
LN2 = 0.6931471805599453
EPS = 1e-6
NEG = -1e30
LANES = 128
RET_BLOCK = 256
REF_CHUNK = 64
ROW_TILE = 256
ATT_BLOCK = 256
TN_MAX_ROWS = 1408
SMALL_ROWS = 16
COL_SHARDED = ("w_xkv",)
TRANSPOSED = ("w_in", "w_gate", "w_up")
SHARD_ROW_ALIGN = 32
SHARD_ROW_PAD = 256
LATE = ("w_out", "w_xq", "w_xkv", "w_xo", "w_gate", "w_up", "w_down")
VMEM_LIMIT = 56 * 1024 * 1024

ADAM_LR = 0.001
ADAM_B1 = 0.9
ADAM_B2 = 0.999
ADAM_EPS = 1e-08
ADAM_WD = 0.01
ADAM_STEP = 10
ADAM_STEPS = 8

MESH = pl.DeviceIdType.MESH
ANY = pl.BlockSpec(memory_space=pl.ANY)
VMEM_SPEC = pl.BlockSpec(memory_space=pltpu.VMEM)


def _cparams(sem=None, vmem=VMEM_LIMIT):
    return pltpu.CompilerParams(dimension_semantics=sem, vmem_limit_bytes=vmem)


def _dot(a, b):
    return jnp.dot(a.astype(BF), b.astype(BF), preferred_element_type=F32)


def _dot_nt(a, b):
    return lax.dot_general(a.astype(BF), b.astype(BF), (((1,), (1,)), ((), ())), preferred_element_type=F32)


def _dot_tn(a, b):
    return lax.dot_general(a.astype(BF), b.astype(BF), (((0,), (0,)), ((), ())), preferred_element_type=F32)


def _split3(x):
    hi = x.astype(BF)
    r = x - hi.astype(F32)
    mid = r.astype(BF)
    lo = (r - mid.astype(F32)).astype(BF)
    return hi, mid, lo


def _dot_exact(ind, x):
    hi, mid, lo = _split3(x)
    return (jnp.dot(ind, lo, preferred_element_type=F32) + jnp.dot(ind, mid, preferred_element_type=F32)
            + jnp.dot(ind, hi, preferred_element_type=F32))


def _dot_nt_exact(ind, x):
    hi, mid, lo = _split3(x)
    dn = (((1,), (1,)), ((), ()))
    return (lax.dot_general(ind, lo, dn, preferred_element_type=F32) + lax.dot_general(ind, mid, dn, preferred_element_type=F32)
            + lax.dot_general(ind, hi, dn, preferred_element_type=F32))


def _sigmoid(x):
    return 1.0 / (1.0 + jnp.exp(-x))


def _rms_fwd(x, g):
    r = lax.rsqrt(jnp.mean(x * x, axis=-1, keepdims=True) + EPS)
    return x * r * g


def _rms_bwd(x, g, dy):
    r = lax.rsqrt(jnp.mean(x * x, axis=-1, keepdims=True) + EPS)
    xh = x * r
    dg = jnp.sum(dy * xh, axis=0, keepdims=True)
    dxh = dy * g
    dx = r * (dxh - xh * jnp.mean(dxh * xh, axis=-1, keepdims=True))
    return dx, dg


def _group_mean64(x):
    lane = lax.broadcasted_iota(jnp.int32, x.shape, 1)
    lo = lane < HEAD_DIM
    s_lo = jnp.sum(jnp.where(lo, x, 0.0), axis=-1, keepdims=True)
    s_hi = jnp.sum(jnp.where(lo, 0.0, x), axis=-1, keepdims=True)
    return jnp.where(lo, s_lo, s_hi) * (1.0 / HEAD_DIM)


def _swap32(x):
    lane = lax.broadcasted_iota(jnp.int32, x.shape, 1)
    first = (lane % HEAD_DIM) < (HEAD_DIM // 2)
    return jnp.where(first, pltpu.roll(x, LANES - HEAD_DIM // 2, axis=1), pltpu.roll(x, HEAD_DIM // 2, axis=1))


def _chunks(w):
    return [slice(j * LANES, (j + 1) * LANES) for j in range(w // LANES)]


def _aug_pair(qk, f_cols, is_query):
    lane = lax.broadcasted_iota(jnp.int32, qk.shape, 1)
    a = lane - HEAD_DIM
    values = (qk, pltpu.roll(qk, HEAD_DIM, axis=1))
    out = []
    for hh in range(2):
        hi, mid, lo = (p.astype(F32) for p in _split3(f_cols[hh] * LOG2E))
        if is_query:
            aux = jnp.where(a == 0, hi, jnp.where(a == 1, mid, jnp.where(a == 2, lo, jnp.where(a < 6, 1.0, 0.0))))
        else:
            aux = jnp.where(a < 3, 1.0, jnp.where(a == 3, -hi, jnp.where(a == 4, -mid, jnp.where(a == 5, -lo, 0.0))))
        out.append(jnp.where(a < 0, values[hh], aux))
    return jnp.concatenate(out, axis=-1).astype(BF)


def _mem_kv_fwd(mem, g_mem, w_xkv, g_xk):
    m_tok = mem.shape[0]

    def body(mem_ref, gm_ref, w_ref, gk_ref, memn_ref, kraw_ref, kn_ref, v_ref):
        mn = _rms_fwd(mem_ref[...], gm_ref[...]).astype(BF)
        memn_ref[...] = mn
        kv = jnp.dot(mn, w_ref[...], preferred_element_type=F32)
        k = kv[:, :D_MODEL]
        kraw_ref[...] = k
        v_ref[...] = kv[:, D_MODEL:].astype(BF)
        for h in range(N_XH):
            sl = slice(h * XHD, (h + 1) * XHD)
            kn_ref[:, sl] = _rms_fwd(k[:, sl], gk_ref[...]).astype(BF)

    return pl.pallas_call(
        body, name="mem_kv_fwd",
        out_shape=(jax.ShapeDtypeStruct((m_tok, D_MODEL), BF), jax.ShapeDtypeStruct((m_tok, D_MODEL), F32),
                   jax.ShapeDtypeStruct((m_tok, D_MODEL), BF), jax.ShapeDtypeStruct((m_tok, D_MODEL), BF)),
        in_specs=[VMEM_SPEC] * 4, out_specs=(VMEM_SPEC,) * 4, compiler_params=_cparams(),
    )(mem, g_mem, w_xkv, g_xk)


def _in_proj_fwd(x, g_mix, w_in_t, b_pad, cos_t, sin_t, gq_t, gk_t):
    t_len = x.shape[0]
    tm = min(ROW_TILE, t_len)
    n_t = t_len // tm

    def body(x_ref, g_ref, wm_ref, wf_ref, b_ref, cos_ref, sin_ref, gq_ref, gk_ref,
             n1_ref, proj_ref, rq_ref, rk_ref, qa_ref, ka_ref, z_ref, carry):
        i = pl.program_id(0)

        @pl.when(i == 0)
        def _():
            carry[...] = jnp.zeros_like(carry)

        n1 = _rms_fwd(x_ref[...], g_ref[...]).astype(BF)
        n1_ref[...] = n1
        proj = _dot_nt(n1, wm_ref[...])
        proj_ref[...] = proj.astype(BF)
        z = _dot_nt(n1, wf_ref[...]) + b_ref[...]
        z_ref[...] = z
        lane = lax.broadcasted_iota(jnp.int32, z.shape, 1)
        lf = jnp.where(lane < N_HEADS, jnp.minimum(z, 0.0) - jnp.log(1.0 + jnp.exp(-jnp.abs(z))), 0.0)
        row = lax.broadcasted_iota(jnp.int32, (tm, tm), 0)
        col = lax.broadcasted_iota(jnp.int32, (tm, tm), 1)
        tri = (row >= col).astype(BF)
        fc = _dot_exact(tri, lf) + carry[0:1, :]
        carry[...] = jnp.broadcast_to(fc[tm - 1:tm, :], carry.shape)
        c, s = cos_ref[...], sin_ref[...]
        for j, sl in enumerate(_chunks(GROUP_W)):
            q = proj[:, sl]
            rq_ref[:, sl] = ((q * c + _swap32(q) * s) * 0.125).astype(BF)
            k = proj[:, GROUP_W + j * LANES:GROUP_W + (j + 1) * LANES]
            rk_ref[:, sl] = (k * c + _swap32(k) * s).astype(BF)
            f_cols = [fc[:, 2 * j:2 * j + 1], fc[:, 2 * j + 1:2 * j + 2]]
            fq = proj[:, 4 * GROUP_W + j * LANES:4 * GROUP_W + (j + 1) * LANES]
            fq = fq * lax.rsqrt(_group_mean64(fq * fq) + EPS) * gq_ref[...] * (0.125 * LOG2E)
            qa_ref[:, 2 * j * LANES:2 * (j + 1) * LANES] = _aug_pair(fq, f_cols, True)
            fk = proj[:, 5 * GROUP_W + j * LANES:5 * GROUP_W + (j + 1) * LANES]
            fk = fk * lax.rsqrt(_group_mean64(fk * fk) + EPS) * gk_ref[...]
            ka_ref[:, 2 * j * LANES:2 * (j + 1) * LANES] = _aug_pair(fk, f_cols, False)

    row_spec = lambda w: pl.BlockSpec((tm, w), lambda i: (i, 0))
    full = lambda a: pl.BlockSpec(a.shape, lambda i: (0,) * a.ndim)
    return pl.pallas_call(
        body, name="in_proj_fwd", grid=(n_t,),
        out_shape=(jax.ShapeDtypeStruct((t_len, D_MODEL), BF), jax.ShapeDtypeStruct((t_len, MAIN_W), BF),
                   jax.ShapeDtypeStruct((t_len, GROUP_W), BF), jax.ShapeDtypeStruct((t_len, GROUP_W), BF),
                   jax.ShapeDtypeStruct((t_len, 2 * GROUP_W), BF), jax.ShapeDtypeStruct((t_len, 2 * GROUP_W), BF),
                   jax.ShapeDtypeStruct((t_len, LANES), F32)),
        in_specs=[row_spec(D_MODEL), full(g_mix), *_w_in_specs(), full(b_pad), row_spec(LANES), row_spec(LANES),
                  full(gq_t), full(gk_t)],
        out_specs=(row_spec(D_MODEL), row_spec(MAIN_W), row_spec(GROUP_W), row_spec(GROUP_W), row_spec(2 * GROUP_W),
                   row_spec(2 * GROUP_W), row_spec(LANES)),
        scratch_shapes=[pltpu.VMEM((8, LANES), F32)],
        compiler_params=_cparams(("arbitrary",)),
    )(x, g_mix, w_in_t, w_in_t, b_pad, cos_t, sin_t, gq_t, gk_t)


def _w_in_specs():
    return (pl.BlockSpec((MAIN_W, D_MODEL), lambda i: (0, 0)), pl.BlockSpec((LANES, D_MODEL), lambda i: (MAIN_W // LANES, 0)))


def _decay_tables(c):
    h = np.arange(N_HEADS, dtype=np.float64)
    lg = np.log(1.0 - 2.0 ** (-5.0 - h)).astype(np.float32).astype(np.float64)
    t = np.arange(c)
    same_or_earlier = (t[None, :] // REF_CHUNK) <= (t[:, None] // REF_CHUNK)
    w = np.where(same_or_earlier[None], np.exp(lg[:, None, None] * np.abs(t[:, None] - t[None, :])[None]), 0.0)
    qd = np.exp(lg[:, None] * (t[None, :] + 1.0))
    kd = np.exp(lg[:, None] * (c - 1.0 - t[None, :]))
    cd = np.exp(lg * c)
    ones = np.ones((1, 1, HEAD_DIM))
    return (jnp.asarray(w, F32), jnp.asarray(qd[:, :, None] * ones, F32), jnp.asarray(kd[:, :, None] * ones, F32),
            jnp.asarray(cd[:, None, None] * np.ones((1, HEAD_DIM, HEAD_DIM)), F32))


def _retention_fwd(rq, rk, proj, g_ret, tables):
    t_len = rq.shape[0]
    c = min(RET_BLOCK, t_len)
    n_b = t_len // c
    wdec, qdec, kdec, cdec = tables
    v_col, g_col = 2 * GROUP_W // LANES, 3 * GROUP_W // LANES

    def body(q_ref, k_ref, v_ref, rg_ref, g_ref, w_ref, qd_ref, kd_ref, cd_ref, raw_ref, mix_ref, st_ref, state):
        i = pl.program_id(1)

        @pl.when(i == 0)
        def _():
            state[...] = jnp.zeros_like(state)

        q2, k2, v2 = q_ref[...], k_ref[...], v_ref[...]
        outs = []
        for hh in range(2):
            sl = slice(hh * HEAD_DIM, (hh + 1) * HEAD_DIM)
            q, k, v = q2[:, sl], k2[:, sl], v2[:, sl]
            sp = state[hh]
            st_ref[0, 0, hh] = sp
            a = _dot_nt(q, k) * w_ref[hh]
            o = _dot(a, v) + _dot(q.astype(F32) * qd_ref[hh], sp)
            state[hh] = sp * cd_ref[hh] + _dot_tn(k.astype(F32) * kd_ref[hh], v)
            outs.append(o)
        o2 = jnp.concatenate(outs, axis=-1)
        raw_ref[...] = o2
        xc = o2 - _group_mean64(o2)
        xh = xc * lax.rsqrt(_group_mean64(xc * xc) + EPS)
        gate = rg_ref[...].astype(F32)
        mix_ref[...] = (gate * _sigmoid(gate) * (xh * g_ref[0])).astype(BF)

    blk = lambda col0: pl.BlockSpec((c, LANES), lambda hp, i: (i, col0 + hp))
    tab = lambda a: pl.BlockSpec((2,) + a.shape[1:], lambda hp, i: (hp, 0, 0))
    return pl.pallas_call(
        body, name="retention_fwd", grid=(N_HEADS // 2, n_b),
        out_shape=(jax.ShapeDtypeStruct((t_len, GROUP_W), F32), jax.ShapeDtypeStruct((t_len, GROUP_W), BF),
                   jax.ShapeDtypeStruct((N_HEADS // 2, n_b, 2, HEAD_DIM, HEAD_DIM), F32)),
        in_specs=[blk(0), blk(0), blk(v_col), blk(g_col), pl.BlockSpec((1, 1, LANES), lambda hp, i: (hp, 0, 0)),
                  tab(wdec), tab(qdec), tab(kdec), tab(cdec)],
        out_specs=(blk(0), blk(0), pl.BlockSpec((1, 1, 2, HEAD_DIM, HEAD_DIM), lambda hp, i: (hp, i, 0, 0, 0))),
        scratch_shapes=[pltpu.VMEM((2, HEAD_DIM, HEAD_DIM), F32)],
        compiler_params=_cparams(("arbitrary", "arbitrary")),
    )(rq, rk, proj, proj, g_ret, wdec, qdec, kdec, cdec)


def _fox_fwd(q_aug, k_aug, proj, shards):
    t_len = q_aug.shape[0]
    tq = min(ATT_BLOCK, t_len)
    n_q = t_len // tq
    v_col = 6 * GROUP_W // LANES
    tc = min(512, t_len)
    n_w = len(shards)
    n_steps = (N_HEADS // 2) * n_q

    def body(*refs):
        q_ref, k_ref, v_ref = refs[:3]
        o_ref, o32_ref, lse_ref = refs[3 + n_w:6 + n_w]
        vt = refs[6 + 2 * n_w]
        comm = (refs[3:3 + n_w], refs[6 + n_w:6 + 2 * n_w]) + tuple(refs[7 + 2 * n_w:])
        i = pl.program_id(1)
        step = pl.program_id(0) * n_q + i

        @pl.when(step == 0)
        def _():
            _gather_phase(0, *comm)

        @pl.when(step == (3 * n_steps) // 4)
        def _():
            _gather_phase(1, *comm)

        @pl.when(i == 0)
        def _():
            for c0 in range(0, t_len, tc):
                vt[:, c0:c0 + tc] = v_ref[c0:c0 + tc, :].T

        qs = [q_ref[:, hh * LANES:(hh + 1) * LANES] for hh in range(2)]
        ones = jnp.ones((HEAD_DIM, tq), BF)

        def scores(j):
            k2 = k_ref[pl.ds(pl.multiple_of(j * tq, tq), tq), :]
            return tuple(_dot_nt(k2[:, hh * LANES:(hh + 1) * LANES], qs[hh]) for hh in range(2))

        def update(j, ss, carry, masked):
            v2 = vt[:, pl.ds(pl.multiple_of(j * tq, tq), tq)]
            ps, stats = [], []
            for hh in range(2):
                m = carry[hh][0]
                s_t = ss[hh]
                if masked:
                    krow = lax.broadcasted_iota(jnp.int32, (tq, tq), 0)
                    qcol = lax.broadcasted_iota(jnp.int32, (tq, tq), 1)
                    s_t = jnp.where(qcol >= krow, s_t, NEG)
                m_new = jnp.maximum(m, jnp.max(s_t, axis=0, keepdims=True))
                ps.append(jnp.exp2(s_t - m_new).astype(BF))
                stats.append((m_new, jnp.exp2(m - m_new)))
            out = []
            for hh in range(2):
                m_new, alpha = stats[hh]
                v_aug = jnp.concatenate([v2[hh * HEAD_DIM:(hh + 1) * HEAD_DIM, :], ones], axis=0)
                out.append((m_new, carry[hh][1] * alpha + jnp.dot(v_aug, ps[hh], preferred_element_type=F32)))
            return tuple(out)

        def advance(j, state):
            ss, carry = state
            return scores(j + 1), update(j, ss, carry, False)

        init = tuple((jnp.full((1, tq), NEG, F32), jnp.zeros((LANES, tq), F32)) for _ in range(2))
        ss, carry = lax.fori_loop(0, i, advance, (scores(0), init))
        carry = update(i, ss, carry, True)
        outs, lses = [], []
        for hh in range(2):
            m, acc = carry[hh]
            l = acc[HEAD_DIM:HEAD_DIM + 1, :]
            outs.append(acc[:HEAD_DIM, :] / l)
            lses.append(m + jnp.log2(l))
        o2 = jnp.concatenate(outs, axis=0).T
        o32_ref[...] = o2
        o_ref[...] = o2.astype(BF)
        lse_ref[0] = jnp.concatenate(lses, axis=0)

        @pl.when(step == n_steps - 1)
        def _():
            _gather_phase(2, *comm)

    return pl.pallas_call(
        body, name="fox_fwd", grid=(N_HEADS // 2, n_q),
        out_shape=(jax.ShapeDtypeStruct((t_len, GROUP_W), BF), jax.ShapeDtypeStruct((t_len, GROUP_W), F32),
                   jax.ShapeDtypeStruct((N_HEADS // 2, 2, t_len), F32))
        + tuple(jax.ShapeDtypeStruct((4,) + s.shape, s.dtype) for s in shards),
        in_specs=[pl.BlockSpec((tq, 2 * LANES), lambda hp, i: (i, hp)),
                  pl.BlockSpec((t_len, 2 * LANES), lambda hp, i: (0, hp)),
                  pl.BlockSpec((t_len, LANES), lambda hp, i: (0, v_col + hp))] + [ANY] * n_w,
        out_specs=(pl.BlockSpec((tq, LANES), lambda hp, i: (i, hp)), pl.BlockSpec((tq, LANES), lambda hp, i: (i, hp)),
                   pl.BlockSpec((1, 2, tq), lambda hp, i: (hp, 0, i))) + (ANY,) * n_w,
        scratch_shapes=[pltpu.VMEM((LANES, t_len), BF)] + _gather_scratch(n_w),
        compiler_params=_cparams(("arbitrary", "arbitrary")),
    )(q_aug, k_aug, proj, *shards)


def _softmax_rows(s):
    p = jnp.exp(s - jnp.max(s, axis=-1, keepdims=True))
    return p / jnp.sum(p, axis=-1, keepdims=True)


def _attn_out_xattn_fwd(x, mix_r, mix_f, w_out, g_xattn, w_xq, g_xq, kn, v, w_xo):
    t_len = x.shape[0]
    tm = min(ROW_TILE, t_len)

    def body(x_ref, mr_ref, mf_ref, wo_ref, g_ref, wq_ref, gq_ref, kn_ref, v_ref, wxo_ref,
             h1_ref, hn_ref, qx_ref, o_ref, h2_ref):
        h1 = x_ref[...] + jnp.dot(mr_ref[...], wo_ref[:GROUP_W, :], preferred_element_type=F32) \
            + jnp.dot(mf_ref[...], wo_ref[GROUP_W:, :], preferred_element_type=F32)
        h1_ref[...] = h1
        hn = _rms_fwd(h1, g_ref[...]).astype(BF)
        hn_ref[...] = hn
        qx = jnp.dot(hn, wq_ref[...], preferred_element_type=F32).astype(BF)
        qx_ref[...] = qx
        sls = [slice(h * XHD, (h + 1) * XHD) for h in range(N_XH)]
        qns = [_rms_fwd(qx[:, sl].astype(F32), gq_ref[...]).astype(BF) for sl in sls]
        logits = [_dot_nt(qn, kn_ref[:, sl]) * (XHD ** -0.5) for qn, sl in zip(qns, sls)]
        ps = [_softmax_rows(s).astype(BF) for s in logits]
        for p, sl in zip(ps, sls):
            o_ref[:, sl] = jnp.dot(p, v_ref[:, sl], preferred_element_type=F32).astype(BF)
        h2_ref[...] = h1 + jnp.dot(o_ref[...], wxo_ref[...], preferred_element_type=F32)

    row_spec = lambda w: pl.BlockSpec((tm, w), lambda i: (i, 0))
    full = lambda a: pl.BlockSpec(a.shape, lambda i: (0,) * a.ndim)
    return pl.pallas_call(
        body, name="attn_out_xattn_fwd", grid=(t_len // tm,),
        out_shape=(jax.ShapeDtypeStruct((t_len, D_MODEL), F32), jax.ShapeDtypeStruct((t_len, D_MODEL), BF),
                   jax.ShapeDtypeStruct((t_len, D_MODEL), BF), jax.ShapeDtypeStruct((t_len, D_MODEL), BF),
                   jax.ShapeDtypeStruct((t_len, D_MODEL), F32)),
        in_specs=[row_spec(D_MODEL), row_spec(GROUP_W), row_spec(GROUP_W), full(w_out), full(g_xattn), full(w_xq), full(g_xq),
                  full(kn), full(v), full(w_xo)],
        out_specs=(row_spec(D_MODEL),) * 5,
        compiler_params=_cparams(("arbitrary",)),
    )(x, mix_r, mix_f, w_out, g_xattn, w_xq, g_xq, kn, v, w_xo)


def _ffn_loss_fwd(h2, g_ffn, w_gate, w_up, w_down, target):
    t_len = h2.shape[0]
    tm = min(ROW_TILE, t_len)

    def body(h2_ref, g_ref, wg_ref, wu_ref, wd_ref, tgt_ref, hn_ref, gate_ref, up_ref, act_ref, dh3_ref, loss_ref):
        @pl.when(pl.program_id(0) == 0)
        def _():
            loss_ref[...] = jnp.zeros_like(loss_ref)

        h2v = h2_ref[...]
        hn = _rms_fwd(h2v, g_ref[...]).astype(BF)
        hn_ref[...] = hn
        gate = _dot_nt(hn, wg_ref[...])
        up = _dot_nt(hn, wu_ref[...])
        gate_ref[...] = gate.astype(BF)
        up_ref[...] = up.astype(BF)
        act = (gate * _sigmoid(gate) * up).astype(BF)
        act_ref[...] = act
        diff = h2v + jnp.dot(act, wd_ref[...], preferred_element_type=F32) - tgt_ref[...]
        dh3_ref[...] = diff * (1.0 / D_MODEL)
        per_row = jnp.sum(diff * diff, axis=-1, keepdims=True) * (1.0 / D_MODEL)
        loss_ref[...] += 0.5 * jnp.sum(per_row, axis=0, keepdims=True)

    row_spec = lambda w: pl.BlockSpec((tm, w), lambda i: (i, 0))
    full = lambda a: pl.BlockSpec(a.shape, lambda i: (0,) * a.ndim, pipeline_mode=pl.Buffered(1))
    return pl.pallas_call(
        body, name="ffn_loss_fwd", grid=(t_len // tm,),
        out_shape=(jax.ShapeDtypeStruct((t_len, D_MODEL), BF), jax.ShapeDtypeStruct((t_len, D_FF), BF),
                   jax.ShapeDtypeStruct((t_len, D_FF), BF), jax.ShapeDtypeStruct((t_len, D_FF), BF),
                   jax.ShapeDtypeStruct((t_len, D_MODEL), F32), jax.ShapeDtypeStruct((8, LANES), F32)),
        in_specs=[row_spec(D_MODEL), full(g_ffn), full(w_gate), full(w_up), full(w_down), row_spec(D_MODEL)],
        out_specs=(row_spec(D_MODEL), row_spec(D_FF), row_spec(D_FF), row_spec(D_FF), row_spec(D_MODEL),
                   pl.BlockSpec((8, LANES), lambda i: (0, 0))),
        compiler_params=_cparams(("arbitrary",)),
    )(h2, g_ffn, w_gate, w_up, w_down, target)


def _ffn_bwd(dh3, gate, up, h2, g_ffn, w_gate, w_up, w_down):
    t_len = h2.shape[0]
    tm = min(ROW_TILE, t_len)

    def body(dh3_ref, gate_ref, up_ref, h2_ref, g_ref, wg_ref, wu_ref, wd_ref, dgate_ref, dup_ref, dh2_ref, dg_ref):
        @pl.when(pl.program_id(0) == 0)
        def _():
            dg_ref[...] = jnp.zeros_like(dg_ref)

        dh3v = dh3_ref[...]
        dact = _dot_nt(dh3v, wd_ref[...])
        g = gate_ref[...].astype(F32)
        sg = _sigmoid(g)
        dup = (dact * (g * sg)).astype(BF)
        dgate = (dact * up_ref[...].astype(F32) * (sg * (1.0 + g * (1.0 - sg)))).astype(BF)
        dup_ref[...] = dup
        dgate_ref[...] = dgate
        dhn = jnp.dot(dgate, wg_ref[...], preferred_element_type=F32) + jnp.dot(dup, wu_ref[...], preferred_element_type=F32)
        dx, dg = _rms_bwd(h2_ref[...], g_ref[...], dhn)
        dh2_ref[...] = dh3v + dx
        dg_ref[...] += dg

    row_spec = lambda w: pl.BlockSpec((tm, w), lambda i: (i, 0))
    full = lambda a: pl.BlockSpec(a.shape, lambda i: (0,) * a.ndim, pipeline_mode=pl.Buffered(1))
    return pl.pallas_call(
        body, name="ffn_bwd", grid=(t_len // tm,),
        out_shape=(jax.ShapeDtypeStruct((t_len, D_FF), BF), jax.ShapeDtypeStruct((t_len, D_FF), BF),
                   jax.ShapeDtypeStruct((t_len, D_MODEL), F32), jax.ShapeDtypeStruct((1, D_MODEL), F32)),
        in_specs=[row_spec(D_MODEL), row_spec(D_FF), row_spec(D_FF), row_spec(D_MODEL), full(g_ffn), full(w_gate), full(w_up),
                  full(w_down)],
        out_specs=(row_spec(D_FF), row_spec(D_FF), row_spec(D_MODEL), pl.BlockSpec((1, D_MODEL), lambda i: (0, 0))),
        compiler_params=_cparams(("arbitrary",)),
    )(dh3, gate, up, h2, g_ffn, w_gate, w_up, w_down)


def _attn_out_xattn_bwd(dh2, h1, qx, kn, v, w_xo, w_xq, w_out, g_xattn, g_xq):
    t_len = h1.shape[0]
    tm = min(ROW_TILE, t_len)
    m_tok = kn.shape[0]

    def body(dh2_ref, h1_ref, qx_ref, kn_ref, v_ref, wxo_ref, wq_ref, wo_ref, g_ref, gq_ref,
             dqx_ref, dh1_ref, dmr_ref, dmf_ref, dkn_ref, dv_ref, dg_ref, dgq_ref, dqx_scr):
        @pl.when(pl.program_id(0) == 0)
        def _():
            dkn_ref[...] = jnp.zeros_like(dkn_ref)
            dv_ref[...] = jnp.zeros_like(dv_ref)
            dg_ref[...] = jnp.zeros_like(dg_ref)
            dgq_ref[...] = jnp.zeros_like(dgq_ref)

        dh2v = dh2_ref[...]
        do = _dot_nt(dh2v, wxo_ref[...])
        gq = gq_ref[...]
        sls = [slice(h * XHD, (h + 1) * XHD) for h in range(N_XH)]
        qraws = [qx_ref[:, sl].astype(F32) for sl in sls]
        qns = [_rms_fwd(qraw, gq).astype(BF) for qraw in qraws]
        dohs = [do[:, sl].astype(BF) for sl in sls]
        logits = [_dot_nt(qn, kn_ref[:, sl]) * (XHD ** -0.5) for qn, sl in zip(qns, sls)]
        dps = [_dot_nt(doh, v_ref[:, sl]) for doh, sl in zip(dohs, sls)]
        ps = [_softmax_rows(s) for s in logits]
        dss = [(p * (dp - jnp.sum(dp * p, axis=-1, keepdims=True)) * (XHD ** -0.5)).astype(BF) for p, dp in zip(ps, dps)]
        dqns = []
        for h, sl in enumerate(sls):
            dv_ref[:, sl] += _dot_tn(ps[h], dohs[h])
            dqns.append(jnp.dot(dss[h], kn_ref[:, sl], preferred_element_type=F32))
            dkn_ref[:, sl] += _dot_tn(dss[h], qns[h])
        dgq = jnp.zeros((1, XHD), F32)
        for h, sl in enumerate(sls):
            dx, dg_h = _rms_bwd(qraws[h], gq, dqns[h])
            dgq = dgq + dg_h
            dqx_scr[:, sl] = dx.astype(BF)
        dgq_ref[...] += dgq
        dqx = dqx_scr[...]
        dqx_ref[...] = dqx
        dhn = _dot_nt(dqx, wq_ref[...])
        dx, dg = _rms_bwd(h1_ref[...], g_ref[...], dhn)
        dg_ref[...] += dg
        dh1 = dh2v + dx
        dh1_ref[...] = dh1
        dmix = _dot_nt(dh1, wo_ref[...])
        dmr_ref[...] = dmix[:, :GROUP_W]
        dmf_ref[...] = dmix[:, GROUP_W:].astype(BF)

    row_spec = lambda w: pl.BlockSpec((tm, w), lambda i: (i, 0))
    full = lambda a: pl.BlockSpec(a.shape, lambda i: (0,) * a.ndim)
    acc = lambda r, c: pl.BlockSpec((r, c), lambda i: (0, 0))
    return pl.pallas_call(
        body, name="attn_out_xattn_bwd", grid=(t_len // tm,),
        out_shape=(jax.ShapeDtypeStruct((t_len, D_MODEL), BF), jax.ShapeDtypeStruct((t_len, D_MODEL), F32),
                   jax.ShapeDtypeStruct((t_len, GROUP_W), F32), jax.ShapeDtypeStruct((t_len, GROUP_W), BF),
                   jax.ShapeDtypeStruct((m_tok, D_MODEL), F32), jax.ShapeDtypeStruct((m_tok, D_MODEL), F32),
                   jax.ShapeDtypeStruct((1, D_MODEL), F32), jax.ShapeDtypeStruct((1, XHD), F32)),
        in_specs=[row_spec(D_MODEL), row_spec(D_MODEL), row_spec(D_MODEL), full(kn), full(v), full(w_xo), full(w_xq), full(w_out),
                  full(g_xattn), full(g_xq)],
        out_specs=(row_spec(D_MODEL), row_spec(D_MODEL), row_spec(GROUP_W), row_spec(GROUP_W), acc(m_tok, D_MODEL),
                   acc(m_tok, D_MODEL), acc(1, D_MODEL), acc(1, XHD)),
        scratch_shapes=[pltpu.VMEM((tm, D_MODEL), BF)],
        compiler_params=_cparams(("arbitrary",)),
    )(dh2, h1, qx, kn, v, w_xo, w_xq, w_out, g_xattn, g_xq)


def _mem_kv_bwd(dkn, dv, kraw, mem, memn, g_mem, g_xk, w_xkv):
    m_tok = mem.shape[0]

    def body(dkn_ref, dv_ref, kraw_ref, mem_ref, memn_ref, gm_ref, gk_ref, w_ref, dw_ref, dgm_ref, dgk_ref, dkv_scr):
        gk = gk_ref[...]
        dgk = jnp.zeros((1, XHD), F32)
        for h in range(N_XH):
            sl = slice(h * XHD, (h + 1) * XHD)
            dx, dg_h = _rms_bwd(kraw_ref[:, sl], gk, dkn_ref[:, sl])
            dgk = dgk + dg_h
            dkv_scr[:, sl] = dx.astype(BF)
        dgk_ref[...] = dgk
        dkv_scr[:, D_MODEL:] = dv_ref[...].astype(BF)
        dkv = dkv_scr[...]
        dw_ref[...] = _dot_tn(memn_ref[...], dkv)
        dmemn = _dot_nt(dkv, w_ref[...])
        mem_v = mem_ref[...]
        r = lax.rsqrt(jnp.mean(mem_v * mem_v, axis=-1, keepdims=True) + EPS)
        dgm_ref[...] = jnp.sum(dmemn * mem_v * r, axis=0, keepdims=True)

    return pl.pallas_call(
        body, name="mem_kv_bwd",
        out_shape=(jax.ShapeDtypeStruct((D_MODEL, 2 * D_MODEL), F32), jax.ShapeDtypeStruct((1, D_MODEL), F32),
                   jax.ShapeDtypeStruct((1, XHD), F32)),
        in_specs=[VMEM_SPEC] * 8, out_specs=(VMEM_SPEC,) * 3,
        scratch_shapes=[pltpu.VMEM((m_tok, 2 * D_MODEL), BF)],
        compiler_params=_cparams(),
    )(dkn, dv, kraw, mem, memn, g_mem, g_xk, w_xkv)


def _fox_bwd(q_aug, k_aug, proj, dmf, o32, lse, sums):
    t_len = q_aug.shape[0]
    tb = min(ATT_BLOCK, t_len)
    n_b = t_len // tb
    v_col = 6 * GROUP_W // LANES
    n_w = len(sums)
    n_steps = (N_HEADS // 2) * n_b

    def body(*refs):
        k_ref, v_ref, q_ref, do_ref, o_ref, lse_ref = refs[:6]
        dq_ref, dk_ref, dv_ref, df_ref = refs[6 + n_w:10 + n_w]
        delta = refs[10 + 2 * n_w]
        comm = (refs[6:6 + n_w], refs[10 + n_w:10 + 2 * n_w]) + tuple(refs[11 + 2 * n_w:])
        j = pl.program_id(1)
        step = pl.program_id(0) * n_b + j

        @pl.when(step == 0)
        def _():
            _scatter_phase(0, *comm)

        @pl.when(j == 0)
        def _():
            dq_ref[...] = jnp.zeros_like(dq_ref)
            dd = do_ref[...].astype(F32) * o_ref[...]
            hrow = lax.broadcasted_iota(jnp.int32, (8, LANES), 0)
            lane = lax.broadcasted_iota(jnp.int32, (8, LANES), 1)
            ind = ((lane // HEAD_DIM) == hrow).astype(BF)
            delta[...] = _dot_nt_exact(ind, dd)

        k2, v2 = k_ref[...], v_ref[...]
        ks = [k2[:, hh * LANES:(hh + 1) * LANES] for hh in range(2)]
        vs = [v2[:, hh * HEAD_DIM:(hh + 1) * HEAD_DIM] for hh in range(2)]

        def blocks(idx, carry, masked, valid):
            loaded = []
            for i in idx:
                rows = pl.ds(pl.multiple_of(i * tb, tb), tb)
                q2 = q_ref[rows, :]
                do2 = do_ref[rows, :]
                loaded.append((rows, [q2[:, hh * LANES:(hh + 1) * LANES] for hh in range(2)],
                               [do2[:, hh * HEAD_DIM:(hh + 1) * HEAD_DIM] for hh in range(2)]))
            ss = [[_dot_nt(ks[hh], qs[hh]) for hh in range(2)] for _, qs, _ in loaded]
            dps = [[_dot_nt(vs[hh], dos[hh]) for hh in range(2)] for _, _, dos in loaded]
            pts, dsts, dfs = [], [], []
            for b, (rows, _, _) in enumerate(loaded):
                for hh in range(2):
                    s_t = ss[b][hh]
                    if masked[b]:
                        krow = lax.broadcasted_iota(jnp.int32, (tb, tb), 0)
                        qcol = lax.broadcasted_iota(jnp.int32, (tb, tb), 1)
                        s_t = jnp.where(qcol >= krow, s_t, NEG)
                    lse_row = lse_ref[0, hh:hh + 1, rows]
                    if valid[b] is not None:
                        lse_row = jnp.where(valid[b], lse_row, -NEG)
                    p_t = jnp.exp2(s_t - lse_row)
                    pts.append(p_t.astype(BF))
                    ds_t = p_t * (dps[b][hh] - delta[hh:hh + 1, rows])
                    dsts.append(ds_t.astype(BF))
                    dfs.append(jnp.sum(ds_t, axis=-1, keepdims=True))
            out = list(carry)
            for b, (rows, qs, dos) in enumerate(loaded):
                for hh in range(2):
                    dk, dv, df = out[hh]
                    dv = dv + jnp.dot(pts[2 * b + hh], dos[hh], preferred_element_type=F32)
                    dk = dk + jnp.dot(dsts[2 * b + hh], qs[hh], preferred_element_type=F32)
                    dq_ref[rows, hh * HEAD_DIM:(hh + 1) * HEAD_DIM] += _dot_tn(dsts[2 * b + hh], ks[hh])[:, :HEAD_DIM]
                    out[hh] = (dk, dv, df - dfs[2 * b + hh])
            return tuple(out)

        def pair(t, carry):
            i0 = j + 1 + 2 * t
            return blocks([i0, jnp.minimum(i0 + 1, n_b - 1)], carry, [False, False], [None, i0 + 1 < n_b])

        init = tuple((jnp.zeros((tb, LANES), F32), jnp.zeros((tb, HEAD_DIM), F32), jnp.zeros((tb, 1), F32)) for _ in range(2))
        carry = blocks([j], init, [True], [None])
        carry = lax.fori_loop(0, (n_b - j) // 2, pair, carry)
        dk_ref[...] = jnp.concatenate([carry[hh][0][:, :HEAD_DIM] for hh in range(2)], axis=-1) * LN2
        dv_ref[...] = jnp.concatenate([carry[hh][1] for hh in range(2)], axis=-1)
        df_ref[0] = jnp.concatenate([carry[hh][2] for hh in range(2)], axis=-1)

        @pl.when(step == n_steps - 1)
        def _():
            _scatter_phase(1, *comm)

    blk = lambda w, col0: pl.BlockSpec((tb, w), lambda hp, j: (j, col0 + hp))
    whole = lambda w: pl.BlockSpec((t_len, w), lambda hp, j: (0, hp))
    rows2 = pl.BlockSpec((1, 2, t_len), lambda hp, j: (hp, 0, 0))
    cols2 = pl.BlockSpec((1, tb, 2), lambda hp, j: (hp, j, 0))
    return pl.pallas_call(
        body, name="fox_bwd", grid=(N_HEADS // 2, n_b),
        out_shape=(jax.ShapeDtypeStruct((t_len, GROUP_W), F32), jax.ShapeDtypeStruct((t_len, GROUP_W), F32),
                   jax.ShapeDtypeStruct((t_len, GROUP_W), F32), jax.ShapeDtypeStruct((N_HEADS // 2, t_len, 2), F32))
        + _scatter_out_shapes(sums),
        in_specs=[blk(2 * LANES, 0), blk(LANES, v_col), whole(2 * LANES), whole(LANES), whole(LANES), rows2] + [ANY] * n_w,
        out_specs=(whole(LANES), blk(LANES, 0), blk(LANES, 0), cols2) + (ANY,) * n_w,
        scratch_shapes=[pltpu.VMEM((8, t_len), F32)] + _scatter_scratch(n_w),
        compiler_params=_cparams(("arbitrary", "arbitrary")),
    )(k_aug, proj, q_aug, dmf, o32, lse, *sums)


def _retention_bwd(dmr, raw, proj, g_ret, rq, rk, states, tables, parts):
    t_len = rq.shape[0]
    c = min(RET_BLOCK, t_len)
    n_b = t_len // c
    wdec, qdec, kdec, cdec = tables
    v_col, g_col = 2 * GROUP_W // LANES, 3 * GROUP_W // LANES
    n_w = len(parts)
    n_steps = (N_HEADS // 2) * n_b

    def body(*refs):
        d_ref, raw_ref, rg_ref, g_ref, q_ref, k_ref, v_ref, st_ref, w_ref, wt_ref, qd_ref, kd_ref, cd_ref = refs[:13]
        dq_ref, dk_ref, dv_ref, drg_ref, dg_ref = refs[13 + n_w:18 + n_w]
        gstate = refs[18 + 2 * n_w]
        comm = (refs[13:13 + n_w], refs[18 + n_w:18 + 2 * n_w]) + tuple(refs[19 + 2 * n_w:])
        step = pl.program_id(0) * n_b + pl.program_id(1)

        @pl.when(step == 0)
        def _():
            _exchange_phase(0, *comm)

        @pl.when(pl.program_id(1) == 0)
        def _():
            gstate[...] = jnp.zeros_like(gstate)
            dg_ref[...] = jnp.zeros_like(dg_ref)

        d, raw_v, g = d_ref[...], raw_ref[...], g_ref[0]
        gate = rg_ref[...].astype(F32)
        xc = raw_v - _group_mean64(raw_v)
        r = lax.rsqrt(_group_mean64(xc * xc) + EPS)
        xh = xc * r
        sg = _sigmoid(gate)
        drg_ref[...] = d * (xh * g) * (sg * (1.0 + gate * (1.0 - sg)))
        dy = d * (gate * sg)
        dg_ref[0] += jnp.sum(dy * xh, axis=0, keepdims=True)
        dxh = dy * g
        do2 = r * (dxh - _group_mean64(dxh) - xh * _group_mean64(dxh * xh))
        q2, k2, v2 = q_ref[...], k_ref[...], v_ref[...]
        dqs, dks, dvs = [], [], []
        heads = [tuple(t[:, hh * HEAD_DIM:(hh + 1) * HEAD_DIM] for t in (q2, k2, v2, do2.astype(BF))) for hh in range(2)]
        firsts = [(_dot_nt(k, q) * wt_ref[hh], _dot_nt(do, v) * w_ref[hh], _dot_nt(v, do) * wt_ref[hh])
                  for hh, (q, k, v, do) in enumerate(heads)]
        for hh, (q, k, v, do) in enumerate(heads):
            a_t, dm, dm_t = firsts[hh]
            sp, gs = st_ref[0, 0, hh], gstate[hh]
            qd = q.astype(F32) * qd_ref[hh]
            kd = k.astype(F32) * kd_ref[hh]
            dqs.append(_dot(dm, k) + _dot_nt(do, sp) * qd_ref[hh])
            dks.append(_dot(dm_t, q) + _dot_nt(v, gs) * kd_ref[hh])
            dvs.append(_dot(a_t, do) + _dot(kd, gs))
            gstate[hh] = gs * cd_ref[hh] + _dot_tn(qd, do)
        dq_ref[...] = jnp.concatenate(dqs, axis=-1)
        dk_ref[...] = jnp.concatenate(dks, axis=-1)
        dv_ref[...] = jnp.concatenate(dvs, axis=-1)

        @pl.when(step == n_steps - 1)
        def _():
            _exchange_phase(1, *comm)

    blk = lambda col0: pl.BlockSpec((c, LANES), lambda hp, i: (n_b - 1 - i, col0 + hp))
    tab = lambda a: pl.BlockSpec((2,) + a.shape[1:], lambda hp, i: (hp, 0, 0))
    gspec = pl.BlockSpec((1, 1, LANES), lambda hp, i: (hp, 0, 0))
    return pl.pallas_call(
        body, name="retention_bwd", grid=(N_HEADS // 2, n_b),
        out_shape=(jax.ShapeDtypeStruct((t_len, GROUP_W), F32),) * 4 + (jax.ShapeDtypeStruct((N_HEADS // 2, 1, LANES), F32),)
        + _exchange_out_shapes(parts),
        in_specs=[blk(0), blk(0), blk(g_col), gspec, blk(0), blk(0), blk(v_col),
                  pl.BlockSpec((1, 1, 2, HEAD_DIM, HEAD_DIM), lambda hp, i: (hp, n_b - 1 - i, 0, 0, 0)),
                  tab(wdec), tab(wdec), tab(qdec), tab(kdec), tab(cdec)] + [ANY] * n_w,
        out_specs=(blk(0), blk(0), blk(0), blk(0), gspec) + (ANY,) * n_w,
        scratch_shapes=[pltpu.VMEM((2, HEAD_DIM, HEAD_DIM), F32)] + _exchange_scratch(n_w),
        compiler_params=_cparams(("arbitrary", "arbitrary")),
    )(dmr, raw, proj, g_ret, rq, rk, proj, states, wdec, jnp.transpose(wdec, (0, 2, 1)), qdec, kdec, cdec, *parts)


def _in_proj_bwd(x, g_mix, dh1, dq_r, dk_r, dv_r, drg, dq_f, dk_f, dv_f, df_col, proj, z, cos_t, sin_t, gq_t, gk_t, w_in_t):
    t_len = x.shape[0]
    tm = min(ROW_TILE, t_len)
    n_t = t_len // tm

    def body(x_ref, g_ref, dh1_ref, dqr_ref, dkr_ref, dvr_ref, drg_ref, dqf_ref, dkf_ref, dvf_ref, df_ref, fq_ref, fk_ref, z_ref,
             cos_ref, sin_ref, gq_ref, gk_ref, wm_ref, wf_ref,
             dproj_ref, dz_ref, dx_ref, dg_ref, dgq_ref, dgk_ref, db_ref, carry, gq_acc, gk_acc):
        i = pl.program_id(0)

        @pl.when(i == 0)
        def _():
            carry[...] = jnp.zeros_like(carry)
            gq_acc[...] = jnp.zeros_like(gq_acc)
            gk_acc[...] = jnp.zeros_like(gk_acc)
            dg_ref[...] = jnp.zeros_like(dg_ref)
            db_ref[...] = jnp.zeros_like(db_ref)

        c, s = cos_ref[...], sin_ref[...]
        gq, gk = gq_ref[...], gk_ref[...]
        dgq = jnp.zeros((1, LANES), F32)
        dgk = jnp.zeros((1, LANES), F32)
        for sl in _chunks(GROUP_W):
            dy = dqr_ref[:, sl] * 0.125
            dproj_ref[:, sl] = (dy * c + _swap32(dy * s)).astype(BF)
            dy = dkr_ref[:, sl]
            dproj_ref[:, GROUP_W + sl.start:GROUP_W + sl.stop] = (dy * c + _swap32(dy * s)).astype(BF)
            dproj_ref[:, 2 * GROUP_W + sl.start:2 * GROUP_W + sl.stop] = dvr_ref[:, sl].astype(BF)
            dproj_ref[:, 3 * GROUP_W + sl.start:3 * GROUP_W + sl.stop] = drg_ref[:, sl].astype(BF)
            for src, dsrc, gain, off in ((fq_ref, dqf_ref, gq, 4), (fk_ref, dkf_ref, gk, 5)):
                xr = src[:, sl].astype(F32)
                r = lax.rsqrt(_group_mean64(xr * xr) + EPS)
                xh = xr * r
                dy = dsrc[:, sl] * (0.125 if off == 4 else 1.0)
                dgs = jnp.sum(dy * xh, axis=0, keepdims=True)
                if off == 4:
                    dgq = dgq + dgs
                else:
                    dgk = dgk + dgs
                dxh = dy * gain
                dproj_ref[:, off * GROUP_W + sl.start:off * GROUP_W + sl.stop] = \
                    (r * (dxh - xh * _group_mean64(dxh * xh))).astype(BF)
            dproj_ref[:, 6 * GROUP_W + sl.start:6 * GROUP_W + sl.stop] = dvf_ref[:, sl].astype(BF)
        gq_acc[...] += dgq
        gk_acc[...] += dgk
        row = lax.broadcasted_iota(jnp.int32, (tm, tm), 0)
        col = lax.broadcasted_iota(jnp.int32, (tm, tm), 1)
        dlf = _dot_exact((col >= row).astype(BF), df_ref[...]) + carry[0:1, :]
        carry[...] = jnp.broadcast_to(dlf[0:1, :], carry.shape)
        lane = lax.broadcasted_iota(jnp.int32, (tm, LANES), 1)
        dz = jnp.where(lane < N_HEADS, dlf / (1.0 + jnp.exp(z_ref[...])), 0.0)
        db_ref[...] += jnp.sum(dz, axis=0, keepdims=True)
        dz_bf = dz.astype(BF)
        dz_ref[...] = dz_bf
        dn1 = jnp.dot(dz_bf, wf_ref[...], preferred_element_type=F32)
        for sec in range(MAIN_W // GROUP_W):
            sl = slice(sec * GROUP_W, (sec + 1) * GROUP_W)
            dn1 = dn1 + jnp.dot(dproj_ref[:, sl], wm_ref[sl, :], preferred_element_type=F32)
        dx, dg = _rms_bwd(x_ref[...], g_ref[...], dn1)
        dx_ref[...] = dh1_ref[...] + dx
        dg_ref[...] += dg

        @pl.when(i == n_t - 1)
        def _():
            dgq_ref[...] = gq_acc[:, :HEAD_DIM] + gq_acc[:, HEAD_DIM:]
            dgk_ref[...] = gk_acc[:, :HEAD_DIM] + gk_acc[:, HEAD_DIM:]

    row_spec = lambda w, col=0: pl.BlockSpec((tm, w), lambda i: (n_t - 1 - i, col))
    full = lambda a: pl.BlockSpec(a.shape, lambda i: (0,) * a.ndim)
    acc = lambda r, c: pl.BlockSpec((r, c), lambda i: (0, 0))
    return pl.pallas_call(
        body, name="in_proj_bwd", grid=(n_t,),
        out_shape=(jax.ShapeDtypeStruct((t_len, MAIN_W), BF), jax.ShapeDtypeStruct((t_len, LANES), BF),
                   jax.ShapeDtypeStruct((t_len, D_MODEL), F32), jax.ShapeDtypeStruct((1, D_MODEL), F32),
                   jax.ShapeDtypeStruct((1, HEAD_DIM), F32), jax.ShapeDtypeStruct((1, HEAD_DIM), F32),
                   jax.ShapeDtypeStruct((1, LANES), F32)),
        in_specs=[row_spec(D_MODEL), full(g_mix), row_spec(D_MODEL)] + [row_spec(GROUP_W)] * 7
        + [row_spec(LANES), row_spec(GROUP_W, 4), row_spec(GROUP_W, 5), row_spec(LANES), row_spec(LANES), row_spec(LANES),
           full(gq_t), full(gk_t), *_w_in_specs()],
        out_specs=(row_spec(MAIN_W), row_spec(LANES), row_spec(D_MODEL), acc(1, D_MODEL), acc(1, HEAD_DIM), acc(1, HEAD_DIM),
                   acc(1, LANES)),
        scratch_shapes=[pltpu.VMEM((8, LANES), F32), pltpu.VMEM((1, LANES), F32), pltpu.VMEM((1, LANES), F32)],
        compiler_params=_cparams(("arbitrary",)),
    )(x, g_mix, dh1, dq_r, dk_r, dv_r, drg, dq_f, dk_f, dv_f, df_col, proj, proj, z, cos_t, sin_t, gq_t, gk_t, w_in_t, w_in_t)


def _matmul_tn(a, b, name, bk=512):
    t_len, m = a.shape
    n = b.shape[1]
    bm = m if m <= TN_MAX_ROWS else m // 2
    bk = min(bk, t_len)

    def body(a_ref, b_ref, o_ref):
        @pl.when(pl.program_id(1) == 0)
        def _():
            o_ref[...] = jnp.zeros_like(o_ref)

        o_ref[...] += _dot_tn(a_ref[...], b_ref[...])

    return pl.pallas_call(
        body, name=name, grid=(m // bm, t_len // bk),
        out_shape=jax.ShapeDtypeStruct((m, n), F32),
        in_specs=[pl.BlockSpec((bk, bm), lambda i, k: (k, i)), pl.BlockSpec((bk, n), lambda i, k: (k, 0))],
        out_specs=pl.BlockSpec((bm, n), lambda i, k: (i, 0)),
        compiler_params=_cparams(("arbitrary", "arbitrary")),
    )(a, b)


def _place():
    x, y, c = lax.axis_index("x"), lax.axis_index("y"), lax.axis_index("c")
    chips = [(1 - x, y), (x, 1 - y), (1 - x, 1 - y)]
    return x, y, c, chips


def _row_chunks(rows, limit):
    step = max(d for d in range(16, min(rows, limit) + 1, 16) if rows % d == 0)
    return [slice(i, i + step) for i in range(0, rows, step)]


ICI_CHUNK_ROWS = 128
D2D_CHUNK_ROWS = 64


def _gather_phase(phase, ins, outs, send_sems, recv_sems):
    x, y, c, chips = _place()
    me_chip = 2 * x + y
    sibling = (x, y, 1 - c)

    def copy(w, k, slot, half, to, rows=slice(None), src=None):
        dst = outs[w].at[slot, half, rows]
        return pltpu.make_async_remote_copy(src_ref=dst if src is None else src, dst_ref=dst,
                                            send_sem=send_sems.at[w, k], recv_sem=recv_sems.at[w, k],
                                            device_id=to, device_id_type=MESH)

    for w in range(len(ins)):
        for j, (px, py) in enumerate(chips):
            if phase == 0:
                for rows in _row_chunks(ins[w].shape[1], ICI_CHUNK_ROWS):
                    copy(w, j, me_chip, c, (px, py, c), rows, src=ins[w].at[c, rows]).start()
            elif phase == 1:
                copy(w, j, 2 * px + py, c, (x, y, c)).wait_recv()
                for rows in _row_chunks(ins[w].shape[1], D2D_CHUNK_ROWS):
                    copy(w, 3 + j, 2 * px + py, c, sibling, rows).start()
            else:
                copy(w, 3 + j, 2 * px + py, 1 - c, (x, y, c)).wait_recv()
                copy(w, j, me_chip, c, (px, py, c), src=ins[w].at[c]).wait_send()
                copy(w, 3 + j, 2 * px + py, c, sibling).wait_send()


def _gather_scratch(n_w):
    return [pltpu.SemaphoreType.DMA((n_w, 6)), pltpu.SemaphoreType.DMA((n_w, 6))]


def _all_gather_weights(shards):
    n_w = len(shards)

    def body(*refs):
        for phase in range(3):
            _gather_phase(phase, refs[:n_w], refs[n_w:2 * n_w], *refs[2 * n_w:])

    return pl.pallas_call(
        body, name="all_gather_weights",
        out_shape=tuple(jax.ShapeDtypeStruct((4,) + s.shape, s.dtype) for s in shards),
        in_specs=[ANY] * n_w, out_specs=(ANY,) * n_w, scratch_shapes=_gather_scratch(n_w),
    )(*shards)


def _exchange_phase(phase, ins, theirs, send_sems, recv_sems):
    x, y, c, _ = _place()

    def remote(w, k=slice(None), rows=slice(None)):
        return pltpu.make_async_remote_copy(src_ref=ins[w].at[k, 1 - c, rows], dst_ref=theirs[w].at[k, rows],
                                            send_sem=send_sems.at[w], recv_sem=recv_sems.at[w], device_id=(x, y, 1 - c),
                                            device_id_type=MESH)

    for w in range(len(ins)):
        if phase == 0:
            for k in range(4):
                for rows in _row_chunks(ins[w].shape[2], D2D_CHUNK_ROWS):
                    remote(w, k, rows).start()
        else:
            remote(w).wait()


def _exchange_scratch(n_w):
    return [pltpu.SemaphoreType.DMA((n_w,)), pltpu.SemaphoreType.DMA((n_w,))]


def _exchange_out_shapes(grads):
    return tuple(jax.ShapeDtypeStruct((4,) + g.shape[2:], g.dtype) for g in grads)


def _exchange_core_halves(grads):
    n_w = len(grads)

    def body(*refs):
        for phase in range(2):
            _exchange_phase(phase, refs[:n_w], refs[n_w:2 * n_w], *refs[2 * n_w:])

    return pl.pallas_call(
        body, name="exchange_core_halves", out_shape=_exchange_out_shapes(grads),
        in_specs=[ANY] * n_w, out_specs=(ANY,) * n_w, scratch_shapes=_exchange_scratch(n_w),
    )(*grads)


def _add_pairs(part, theirs, name):
    _, _, r, c = part.shape
    rb = 32 if r % 32 == 0 else r

    def body(a_ref, b_ref, own_ref, ob_ref):
        my_chip = 2 * lax.axis_index("x") + lax.axis_index("y")
        ob_ref[...] = (a_ref[...] + b_ref[...]).astype(BF)
        own_ref[...] = a_ref[my_chip] + b_ref[my_chip]

    spec = pl.BlockSpec((4, rb, c), lambda i: (0, i, 0))
    return pl.pallas_call(
        body, name=name, grid=(r // rb,),
        out_shape=(jax.ShapeDtypeStruct((r, c), F32), jax.ShapeDtypeStruct((4, r, c), BF)),
        in_specs=[pl.BlockSpec((4, None, rb, c), lambda i: (0, lax.axis_index("c"), i, 0)), spec],
        out_specs=(pl.BlockSpec((rb, c), lambda i: (i, 0)), spec), compiler_params=_cparams(("arbitrary",)),
    )(part, theirs)


def _scatter_phase(phase, bfs, got, send_sems, recv_sems):
    x, y, c, chips = _place()

    def remote(w, j, px, py, rows=slice(None)):
        return pltpu.make_async_remote_copy(src_ref=bfs[w].at[2 * px + py, rows], dst_ref=got[w].at[j, rows],
                                            send_sem=send_sems.at[w, j], recv_sem=recv_sems.at[w, j], device_id=(px, py, c),
                                            device_id_type=MESH)

    for w in range(len(bfs)):
        for j, (px, py) in enumerate(chips):
            if phase == 0:
                for rows in _row_chunks(bfs[w].shape[1], ICI_CHUNK_ROWS):
                    remote(w, j, px, py, rows).start()
            else:
                remote(w, j, px, py).wait()


def _scatter_scratch(n_w):
    return [pltpu.SemaphoreType.DMA((n_w, 3)), pltpu.SemaphoreType.DMA((n_w, 3))]


def _scatter_out_shapes(sums_bf16):
    return tuple(jax.ShapeDtypeStruct((3,) + s.shape[1:], BF) for s in sums_bf16)


def _add_received(own, got, name):
    r, c = own.shape
    rb = 32 if r % 32 == 0 else r

    def body(o_ref, g_ref, out_ref):
        out_ref[...] = ((o_ref[...] + g_ref[0].astype(F32)) + g_ref[1].astype(F32)) + g_ref[2].astype(F32)

    return pl.pallas_call(
        body, name=name, grid=(r // rb,), out_shape=jax.ShapeDtypeStruct((r, c), F32),
        in_specs=[pl.BlockSpec((rb, c), lambda i: (i, 0)), pl.BlockSpec((3, rb, c), lambda i: (0, i, 0))],
        out_specs=pl.BlockSpec((rb, c), lambda i: (i, 0)), compiler_params=_cparams(("arbitrary",)),
    )(own, got)


def _share_with_sibling(halves):
    n_w = len(halves)

    def body(*refs):
        ins, outs = refs[:n_w], refs[n_w:2 * n_w]
        send_sems, recv_sems = refs[2 * n_w:]
        x, y, c, _ = _place()

        def remote(w, rows=slice(None)):
            return pltpu.make_async_remote_copy(src_ref=ins[w].at[rows], dst_ref=outs[w].at[c, rows], send_sem=send_sems.at[w],
                                                recv_sem=recv_sems.at[w], device_id=(x, y, 1 - c), device_id_type=MESH)

        for w in range(n_w):
            for rows in _row_chunks(ins[w].shape[0], D2D_CHUNK_ROWS):
                remote(w, rows).start()
        for w in range(n_w):
            remote(w).wait()

    return pl.pallas_call(
        body, name="share_with_sibling",
        out_shape=tuple(jax.ShapeDtypeStruct((2,) + h.shape, h.dtype) for h in halves),
        in_specs=[ANY] * n_w, out_specs=(ANY,) * n_w,
        scratch_shapes=[pltpu.SemaphoreType.DMA((n_w,)), pltpu.SemaphoreType.DMA((n_w,))],
    )(*halves)


def _all_reduce_small(pack):
    r, c = pack.shape

    def body(p_ref, out_ref, slots, send_sems, recv_sems):
        x, y, cc, _ = _place()
        me = 4 * x + 2 * y + cc
        slots[me] = p_ref[...]
        copies = []
        for k in range(1, 8):
            dx, dy, dc = (k >> 2) & 1, (k >> 1) & 1, k & 1
            to = (1 - x if dx else x, 1 - y if dy else y, 1 - cc if dc else cc)
            cp = pltpu.make_async_remote_copy(src_ref=p_ref, dst_ref=slots.at[me], send_sem=send_sems.at[k - 1],
                                              recv_sem=recv_sems.at[k - 1], device_id=to, device_id_type=MESH)
            cp.start()
            copies.append(cp)
        for cp in copies:
            cp.wait()
        total = slots[0]
        for d in range(1, 8):
            total = total + slots[d]
        out_ref[...] = total

    return pl.pallas_call(
        body, name="all_reduce_small", out_shape=jax.ShapeDtypeStruct((r, c), F32),
        in_specs=[VMEM_SPEC], out_specs=VMEM_SPEC,
        scratch_shapes=[pltpu.VMEM((8, r, c), F32), pltpu.SemaphoreType.DMA((7,)), pltpu.SemaphoreType.DMA((7,))],
    )(pack)


def _adamw_update(w_ref, g_ref, m_ref, v_ref, d_ref, nm_ref, nv_ref):
    gv = g_ref[...]
    nm = ADAM_B1 * m_ref[...] + (1.0 - ADAM_B1) * gv
    nv = ADAM_B2 * v_ref[...] + (1.0 - ADAM_B2) * (gv * gv)
    nm_ref[...] = nm
    nv_ref[...] = nv
    m_hat = nm / (1.0 - ADAM_B1 ** ADAM_STEP)
    v_hat = nv / (1.0 - ADAM_B2 ** ADAM_STEP)
    d_ref[...] = -ADAM_LR * (m_hat / (jnp.sqrt(v_hat) + ADAM_EPS) + ADAM_WD * w_ref[...])


def _adamw_many(ws, gs, ms, vs, sums):
    n_a, n_w = len(ws), len(sums)
    n_steps = ADAM_STEPS
    specs = [pl.BlockSpec((w.shape[0] // n_steps, w.shape[1]), lambda i: (i, 0)) for w in ws]

    def body(*refs):
        ins = refs[:4 * n_a]
        outs = refs[4 * n_a + n_w:7 * n_a + n_w]
        comm = (refs[4 * n_a:4 * n_a + n_w], refs[7 * n_a + n_w:7 * n_a + 2 * n_w]) + tuple(refs[7 * n_a + 2 * n_w:])
        step = pl.program_id(0)

        @pl.when(step == 0)
        def _():
            _scatter_phase(0, *comm)

        for a in range(n_a):
            _adamw_update(*(ins[k * n_a + a] for k in range(4)), *(outs[3 * a + k] for k in range(3)))

        @pl.when(step == n_steps - 1)
        def _():
            _scatter_phase(1, *comm)

    flat = pl.pallas_call(
        body, name="adamw_late", grid=(n_steps,),
        out_shape=tuple(jax.ShapeDtypeStruct(w.shape, F32) for w in ws for _ in range(3)) + _scatter_out_shapes(sums),
        in_specs=specs * 4 + [ANY] * n_w, out_specs=tuple(s for s in specs for _ in range(3)) + (ANY,) * n_w,
        scratch_shapes=_scatter_scratch(n_w), compiler_params=_cparams(("arbitrary",)),
    )(*ws, *gs, *ms, *vs, *sums)
    return [tuple(flat[3 * a:3 * a + 3]) for a in range(n_a)] + list(flat[3 * n_a:])


def _adamw(w, g, m, v, name):
    r, c = w.shape
    rb, cb = (64, c) if r % 64 == 0 else (r, LANES if (r % 8 and c % LANES == 0) else c)

    def body(*refs):
        _adamw_update(*refs)

    spec = pl.BlockSpec((rb, cb), lambda i, j: (i, j))
    return pl.pallas_call(
        body, name=name, grid=(r // rb, c // cb), out_shape=(jax.ShapeDtypeStruct((r, c), F32),) * 3,
        in_specs=[spec] * 4, out_specs=(spec,) * 3, compiler_params=_cparams(("arbitrary", "arbitrary")),
    )(w, g, m, v)


def _rope_tables(t_len):
    inv_freq = ROPE_BASE ** (-jnp.arange(0, HEAD_DIM, 2, dtype=F32) / HEAD_DIM)
    ang = jnp.arange(t_len, dtype=F32)[:, None] * inv_freq[None, :]
    cos, sin = jnp.cos(ang), jnp.sin(ang)
    cos_t = jnp.concatenate([cos, cos, cos, cos], axis=-1)
    sin_t = jnp.concatenate([-sin, sin, -sin, sin], axis=-1)
    return cos_t, sin_t


def _cols_to_shards(dw):
    r, n = dw.shape
    return jnp.transpose(dw.reshape(2, r // 2, 4, n // 4), (2, 0, 1, 3))


def _rows_to_shards(dw):
    r, n = dw.shape
    padded = _pad_rows(dw.reshape(4, r // 4, n))
    return padded.reshape(4, 2, padded.shape[1] // 2, n)


def _pad_lanes(a):
    extra = -a.shape[-1] % LANES
    return a if extra == 0 else jnp.pad(a, [(0, 0)] * (a.ndim - 1) + [(0, extra)])


def _pad_rows(a):
    rows = a.shape[-2]
    extra = 0 if rows % SHARD_ROW_ALIGN == 0 else -rows % SHARD_ROW_PAD
    return a if extra == 0 else jnp.pad(a, [(0, 0)] * (a.ndim - 2) + [(0, extra), (0, 0)])


def _pad_row(a, width=D_MODEL):
    a = a.reshape(1, -1)
    return jnp.pad(a, ((0, 0), (0, width - a.shape[1])))


def kernel(x, mem, g_mix, w_in, b_forget, g_ret_out, g_fox_q, g_fox_k, w_out, g_xattn, w_xq, w_xkv, g_mem, g_xq, g_xk, w_xo, g_ffn, w_gate, w_up, w_down, loss_target, m_g_mix, m_w_in, m_b_forget, m_g_ret_out, m_g_fox_q, m_g_fox_k, m_w_out, m_g_xattn, m_w_xq, m_w_xkv, m_g_mem, m_g_xq, m_g_xk, m_w_xo, m_g_ffn, m_w_gate, m_w_up, m_w_down, v_g_mix, v_w_in, v_b_forget, v_g_ret_out, v_g_fox_q, v_g_fox_k, v_w_out, v_g_xattn, v_w_xq, v_w_xkv, v_g_mem, v_g_xq, v_g_xk, v_w_xo, v_g_ffn, v_w_gate, v_w_up, v_w_down):
    big = {"w_in": (w_in, m_w_in, v_w_in), "w_out": (w_out, m_w_out, v_w_out), "w_xq": (w_xq, m_w_xq, v_w_xq),
           "w_xkv": (w_xkv, m_w_xkv, v_w_xkv), "w_xo": (w_xo, m_w_xo, v_w_xo), "w_gate": (w_gate, m_w_gate, v_w_gate),
           "w_up": (w_up, m_w_up, v_w_up), "w_down": (w_down, m_w_down, v_w_down)}
    for n in TRANSPOSED:
        big[n] = tuple(jnp.swapaxes(a, 1, 2) for a in big[n])
    shards = {}
    for n in big:
        w = _pad_rows(_pad_lanes(big[n][0][0].astype(BF)))
        shards[n] = w.reshape(2, w.shape[0] // 2, w.shape[1])
    sizes = {n: big[n][0].shape[1:] for n in big}
    w_in_full = _assemble_weight("w_in", _all_gather_weights([shards["w_in"]])[0], shards["w_in"], sizes["w_in"])
    small_w ={"g_mix": g_mix, "b_forget": b_forget, "g_ret_out": g_ret_out, "g_fox_q": g_fox_q, "g_fox_k": g_fox_k,
               "g_xattn": g_xattn, "g_mem": g_mem, "g_xq": g_xq, "g_xk": g_xk, "g_ffn": g_ffn}
    m_small = {"g_mix": m_g_mix, "b_forget": m_b_forget, "g_ret_out": m_g_ret_out, "g_fox_q": m_g_fox_q, "g_fox_k": m_g_fox_k,
               "g_xattn": m_g_xattn, "g_mem": m_g_mem, "g_xq": m_g_xq, "g_xk": m_g_xk, "g_ffn": m_g_ffn}
    v_small = {"g_mix": v_g_mix, "b_forget": v_b_forget, "g_ret_out": v_g_ret_out, "g_fox_q": v_g_fox_q, "g_fox_k": v_g_fox_k,
               "g_xattn": v_g_xattn, "g_mem": v_g_mem, "g_xq": v_g_xq, "g_xk": v_g_xk, "g_ffn": v_g_ffn}
    loss_part, grad_x, sums, got, in_bf, small_g = _local_step(x[0], mem[0], loss_target[0], w_in_full, shards, sizes, small_w)
    return _reduce_and_update(big, sums, got, in_bf, small_w, small_g, loss_part, grad_x, m_small, v_small)


def _assemble_weight(name, gathered, own, size):
    rows, width = size
    my_chip = 2 * lax.axis_index("x") + lax.axis_index("y")
    g = lax.dynamic_update_slice(gathered, own[None], (my_chip, 0, 0, 0))
    g = g.reshape(4, 2 * g.shape[2], g.shape[3])[:, :rows, :width]
    return jnp.transpose(g, (1, 0, 2)).reshape(rows, 4 * width) if name in COL_SHARDED else g.reshape(4 * rows, width)


def _shard_parts(names, dw):
    return [_pad_lanes(_cols_to_shards(dw[n]) if n in COL_SHARDED else _rows_to_shards(dw[n])) for n in names]


def _core_sums(names, parts, theirs):
    return [_add_pairs(p, t, f"core_sum_{n}") for n, p, t in zip(names, parts, theirs)]


def _local_step(xs, mems, tgt, w_in_full, shards, sizes, small_w):
    g_mix, b_forget, g_ret_out, g_fox_q, g_fox_k = (small_w[n] for n in ("g_mix", "b_forget", "g_ret_out", "g_fox_q", "g_fox_k"))
    g_xattn, g_mem, g_xq, g_xk, g_ffn = (small_w[n] for n in ("g_xattn", "g_mem", "g_xq", "g_xk", "g_ffn"))
    w_in_t = jnp.pad(w_in_full, ((0, MAIN_W + LANES - IN_W), (0, 0)))
    t_len = xs.shape[0]
    cos_t, sin_t = _rope_tables(t_len)
    tables = _decay_tables(min(RET_BLOCK, t_len))
    gq_t = jnp.concatenate([g_fox_q, g_fox_q], axis=-1)
    gk_t = jnp.concatenate([g_fox_k, g_fox_k], axis=-1)
    b_pad = _pad_row(b_forget, LANES)
    g_ret = g_ret_out.reshape(N_HEADS // 2, 1, LANES)

    n1, proj, rq, rk, q_aug, k_aug, z = _in_proj_fwd(xs, g_mix, w_in_t, b_pad, cos_t, sin_t, gq_t, gk_t)
    raw, mix_r, states = _retention_fwd(rq, rk, proj, g_ret, tables)
    mix_f, o32, lse, *gathered = _fox_fwd(q_aug, k_aug, proj, [shards[n] for n in LATE])
    full = {n: _assemble_weight(n, g, shards[n], sizes[n]) for n, g in zip(LATE, gathered)}
    memn, kraw, kn, vmem = _mem_kv_fwd(mems, g_mem, full["w_xkv"], g_xk)
    h1, hn2, qx, o_x, h2 = _attn_out_xattn_fwd(xs, mix_r, mix_f, full["w_out"], g_xattn, full["w_xq"], g_xq, kn, vmem, full["w_xo"])
    hn3, gate, up, act, dh3, loss_part = _ffn_loss_fwd(h2, g_ffn, full["w_gate"], full["w_up"], full["w_down"], tgt)

    dgate, dup, dh2, dg_ffn = _ffn_bwd(dh3, gate, up, h2, g_ffn, full["w_gate"], full["w_up"], full["w_down"])
    dqx, dh1, dmr, dmf, dkn, dvm, dg_xattn, dg_xq = _attn_out_xattn_bwd(dh2, h1, qx, kn, vmem, full["w_xo"], full["w_xq"],
                                                                      full["w_out"], g_xattn, g_xq)
    dw_xkv, dg_mem, dg_xk = _mem_kv_bwd(dkn, dvm, kraw, mems, memn, g_mem, g_xk, full["w_xkv"])
    dw = {
        "w_out": jnp.concatenate([_matmul_tn(mix_r, dh1, "dw_out_ret"), _matmul_tn(mix_f, dh1, "dw_out_fox")], axis=0),
        "w_xq": _matmul_tn(hn2, dqx, "dw_xq"),
        "w_xkv": dw_xkv,
        "w_xo": _matmul_tn(o_x, dh2, "dw_xo"),
        "w_gate": _matmul_tn(dgate, hn3, "dw_gate"),
        "w_up": _matmul_tn(dup, hn3, "dw_up"),
        "w_down": _matmul_tn(act, dh3, "dw_down"),
    }
    late_parts = _shard_parts(LATE, dw)
    dq_r, dk_r, dv_r, drg, dg_ret, *late_theirs = _retention_bwd(dmr, raw, proj, g_ret, rq, rk, states, tables, late_parts)
    late_sums = _core_sums(LATE, late_parts, late_theirs)
    dq_f, dk_f, dv_f, df, *late_got = _fox_bwd(q_aug, k_aug, proj, dmf, o32, lse, [s[1] for s in late_sums])
    df_col = jnp.pad(jnp.transpose(df, (1, 0, 2)).reshape(t_len, N_HEADS), ((0, 0), (0, LANES - N_HEADS)))
    dproj, dz, grad_x, dg_mix, dg_fq, dg_fk, db = _in_proj_bwd(xs, g_mix, dh1, dq_r, dk_r, dv_r, drg, dq_f, dk_f, dv_f, df_col,
                                                              proj, z, cos_t, sin_t, gq_t, gk_t, w_in_t)

    dw_in = jnp.concatenate([_matmul_tn(dproj, n1, "dw_in_main"), _matmul_tn(dz, n1, "dw_in_ff")[:IN_W - MAIN_W]], axis=0)
    in_parts = _shard_parts(("w_in",), {"w_in": dw_in})
    in_sums = _core_sums(("w_in",), in_parts, _exchange_core_halves(in_parts))
    sums = {n: s[0] for n, s in zip(("w_in",) + LATE, in_sums + late_sums)}
    got = dict(zip(LATE, late_got))
    in_bf = in_sums[0][1]
    small_g = {"g_mix": dg_mix, "b_forget": db[:, :N_HEADS], "g_ret_out": dg_ret, "g_fox_q": dg_fq, "g_fox_k": dg_fk,
               "g_xattn": dg_xattn, "g_mem": dg_mem, "g_xq": dg_xq, "g_xk": dg_xk, "g_ffn": dg_ffn}
    return loss_part, grad_x, sums, got, in_bf, small_g


def _final_grads(names, big, sums, got):
    my_core = lax.axis_index("c")
    finals = [_add_received(sums[n], got[n], f"chip_sum_{n}") for n in names]
    shared = _share_with_sibling(finals)
    out = {}
    for n, s, fin in zip(names, shared, finals):
        s = lax.dynamic_update_slice(s, fin[None], (my_core, 0, 0))
        out[n] = s.reshape(2 * s.shape[1], s.shape[2])[:big[n][0].shape[1], :big[n][0].shape[2]]
    return out


def _reduce_and_update(big, sums, got, in_bf, small_w, small_g, loss_part, grad_x, m_small, v_small):
    grads = _final_grads(LATE, big, sums, got)
    *late_updates, in_got = _adamw_many([big[n][0][0] for n in LATE], [grads[n] for n in LATE], [big[n][1][0] for n in LATE],
                                        [big[n][2][0] for n in LATE], [in_bf])
    updates = dict(zip(LATE, late_updates))
    grads.update(_final_grads(("w_in",), big, sums, {"w_in": in_got}))
    updates["w_in"] = _adamw(big["w_in"][0][0], grads["w_in"], big["w_in"][1][0], big["w_in"][2][0], "adamw_w_in")
    deltas, new_m, new_v = {}, {}, {}
    for n in big:
        restore = (lambda a: jnp.swapaxes(a[None], 1, 2)) if n in TRANSPOSED else (lambda a: a[None])
        grads[n] = restore(grads[n])
        deltas[n], new_m[n], new_v[n] = (restore(a) for a in updates[n])

    small_names = list(small_w)
    pad_rows = SMALL_ROWS - len(small_names) - 1
    stack = lambda d: jnp.concatenate([_pad_row(d[n]) for n in small_names] + [jnp.zeros((pad_rows + 1, D_MODEL), F32)], axis=0)
    g_pack = jnp.concatenate([_pad_row(small_g[n]) for n in small_names] + [_pad_row(loss_part[0:1, 0:1])]
                             + [jnp.zeros((pad_rows, D_MODEL), F32)], axis=0)
    g_tot = _all_reduce_small(g_pack)
    d_s, m_s, v_s = _adamw(stack(small_w), g_tot, stack(m_small), stack(v_small), "adamw_small")
    for i, n in enumerate(small_names):
        shape = small_w[n].shape
        size = int(np.prod(shape))
        grads[n] = g_tot[i, :size].reshape(shape)
        deltas[n], new_m[n], new_v[n] = d_s[i, :size].reshape(shape), m_s[i, :size].reshape(shape), v_s[i, :size].reshape(shape)
    loss = g_tot[len(small_names), 0]

    order = ["g_mix", "w_in", "b_forget", "g_ret_out", "g_fox_q", "g_fox_k", "w_out", "g_xattn", "w_xq", "w_xkv", "g_mem", "g_xq",
             "g_xk", "w_xo", "g_ffn", "w_gate", "w_up", "w_down"]
    return (loss, grad_x[None], *[grads[n] for n in order], *[deltas[n] for n in order], *[new_m[n] for n in order],
            *[new_v[n] for n in order])
```

```python
import functools

import numpy as np
import jax
import jax.numpy as jnp
from jax import lax
from jax.experimental import pallas as pl
from jax.experimental.pallas import tpu as pltpu

F32 = jnp.float32
BF = jnp.bfloat16

D_MODEL = 1024
HEAD_DIM = 64
N_HEADS = 8
GROUP_W = 512
N_XH = 4
XHD = 256
D_FF = 2816
MAIN_W = 3584
IN_W = 3592
ROPE_BASE = 10000.0
LOG2E = 1.4426950408889634
LN2 = 0.6931471805599453
EPS = 1e-6
NEG = -1e30
LANES = 128
RET_BLOCK = 256
REF_CHUNK = 64
ROW_TILE = 256
ATT_BLOCK = 256
TN_MAX_ROWS = 1408
SMALL_ROWS = 16
COL_SHARDED = ("w_xkv",)
TRANSPOSED = ("w_in", "w_gate", "w_up")
SHARD_ROW_ALIGN = 32
SHARD_ROW_PAD = 256
LATE = ("w_out", "w_xq", "w_xkv", "w_xo", "w_gate", "w_up", "w_down")
VMEM_LIMIT = 56 * 1024 * 1024

ADAM_LR = 0.001
ADAM_B1 = 0.9
ADAM_B2 = 0.999
ADAM_EPS = 1e-08
ADAM_WD = 0.01
ADAM_STEP = 10
ADAM_STEPS = 8

MESH = pl.DeviceIdType.MESH
ANY = pl.BlockSpec(memory_space=pl.ANY)
VMEM_SPEC = pl.BlockSpec(memory_space=pltpu.VMEM)


def _cparams(sem=None, vmem=VMEM_LIMIT):
    return pltpu.CompilerParams(dimension_semantics=sem, vmem_limit_bytes=vmem)


def _dot(a, b):
    return jnp.dot(a.astype(BF), b.astype(BF), preferred_element_type=F32)


def _dot_nt(a, b):
    return lax.dot_general(a.astype(BF), b.astype(BF), (((1,), (1,)), ((), ())), preferred_element_type=F32)


def _dot_tn(a, b):
    return lax.dot_general(a.astype(BF), b.astype(BF), (((0,), (0,)), ((), ())), preferred_element_type=F32)


def _split3(x):
    hi = x.astype(BF)
    r = x - hi.astype(F32)
    mid = r.astype(BF)
    lo = (r - mid.astype(F32)).astype(BF)
    return hi, mid, lo


def _dot_exact(ind, x):
    hi, mid, lo = _split3(x)
    return (jnp.dot(ind, lo, preferred_element_type=F32) + jnp.dot(ind, mid, preferred_element_type=F32)
            + jnp.dot(ind, hi, preferred_element_type=F32))


def _dot_nt_exact(ind, x):
    hi, mid, lo = _split3(x)
    dn = (((1,), (1,)), ((), ()))
    return (lax.dot_general(ind, lo, dn, preferred_element_type=F32) + lax.dot_general(ind, mid, dn, preferred_element_type=F32)
            + lax.dot_general(ind, hi, dn, preferred_element_type=F32))


def _sigmoid(x):
    return 1.0 / (1.0 + jnp.exp(-x))


def _rms_fwd(x, g):
    r = lax.rsqrt(jnp.mean(x * x, axis=-1, keepdims=True) + EPS)
    return x * r * g


def _rms_bwd(x, g, dy):
    r = lax.rsqrt(jnp.mean(x * x, axis=-1, keepdims=True) + EPS)
    xh = x * r
    dg = jnp.sum(dy * xh, axis=0, keepdims=True)
    dxh = dy * g
    dx = r * (dxh - xh * jnp.mean(dxh * xh, axis=-1, keepdims=True))
    return dx, dg


def _group_mean64(x):
    lane = lax.broadcasted_iota(jnp.int32, x.shape, 1)
    lo = lane < HEAD_DIM
    s_lo = jnp.sum(jnp.where(lo, x, 0.0), axis=-1, keepdims=True)
    s_hi = jnp.sum(jnp.where(lo, 0.0, x), axis=-1, keepdims=True)
    return jnp.where(lo, s_lo, s_hi) * (1.0 / HEAD_DIM)


def _swap32(x):
    lane = lax.broadcasted_iota(jnp.int32, x.shape, 1)
    first = (lane % HEAD_DIM) < (HEAD_DIM // 2)
    return jnp.where(first, pltpu.roll(x, LANES - HEAD_DIM // 2, axis=1), pltpu.roll(x, HEAD_DIM // 2, axis=1))


def _chunks(w):
    return [slice(j * LANES, (j + 1) * LANES) for j in range(w // LANES)]


def _aug_pair(qk, f_cols, is_query):
    lane = lax.broadcasted_iota(jnp.int32, qk.shape, 1)
    a = lane - HEAD_DIM
    values = (qk, pltpu.roll(qk, HEAD_DIM, axis=1))
    out = []
    for hh in range(2):
        hi, mid, lo = (p.astype(F32) for p in _split3(f_cols[hh] * LOG2E))
        if is_query:
            aux = jnp.where(a == 0, hi, jnp.where(a == 1, mid, jnp.where(a == 2, lo, jnp.where(a < 6, 1.0, 0.0))))
        else:
            aux = jnp.where(a < 3, 1.0, jnp.where(a == 3, -hi, jnp.where(a == 4, -mid, jnp.where(a == 5, -lo, 0.0))))
        out.append(jnp.where(a < 0, values[hh], aux))
    return jnp.concatenate(out, axis=-1).astype(BF)


def _mem_kv_fwd(mem, g_mem, w_xkv, g_xk):
    m_tok = mem.shape[0]

    def body(mem_ref, gm_ref, w_ref, gk_ref, memn_ref, kraw_ref, kn_ref, v_ref):
        mn = _rms_fwd(mem_ref[...], gm_ref[...]).astype(BF)
        memn_ref[...] = mn
        kv = jnp.dot(mn, w_ref[...], preferred_element_type=F32)
        k = kv[:, :D_MODEL]
        kraw_ref[...] = k
        v_ref[...] = kv[:, D_MODEL:].astype(BF)
        for h in range(N_XH):
            sl = slice(h * XHD, (h + 1) * XHD)
            kn_ref[:, sl] = _rms_fwd(k[:, sl], gk_ref[...]).astype(BF)

    return pl.pallas_call(
        body, name="mem_kv_fwd",
        out_shape=(jax.ShapeDtypeStruct((m_tok, D_MODEL), BF), jax.ShapeDtypeStruct((m_tok, D_MODEL), F32),
                   jax.ShapeDtypeStruct((m_tok, D_MODEL), BF), jax.ShapeDtypeStruct((m_tok, D_MODEL), BF)),
        in_specs=[VMEM_SPEC] * 4, out_specs=(VMEM_SPEC,) * 4, compiler_params=_cparams(),
    )(mem, g_mem, w_xkv, g_xk)


def _in_proj_fwd(x, g_mix, w_in_t, b_pad, cos_t, sin_t, gq_t, gk_t):
    t_len = x.shape[0]
    tm = min(ROW_TILE, t_len)
    n_t = t_len // tm

    def body(x_ref, g_ref, wm_ref, wf_ref, b_ref, cos_ref, sin_ref, gq_ref, gk_ref,
             n1_ref, proj_ref, rq_ref, rk_ref, qa_ref, ka_ref, z_ref, carry):
        i = pl.program_id(0)

        @pl.when(i == 0)
        def _():
            carry[...] = jnp.zeros_like(carry)

        n1 = _rms_fwd(x_ref[...], g_ref[...]).astype(BF)
        n1_ref[...] = n1
        z = _dot_nt(n1, wf_ref[...]) + b_ref[...]
        z_ref[...] = z
        lane = lax.broadcasted_iota(jnp.int32, z.shape, 1)
        lf = jnp.where(lane < N_HEADS, jnp.minimum(z, 0.0) - jnp.log(1.0 + jnp.exp(-jnp.abs(z))), 0.0)
        row = lax.broadcasted_iota(jnp.int32, (tm, tm), 0)
        col = lax.broadcasted_iota(jnp.int32, (tm, tm), 1)
        tri = (row >= col).astype(BF)
        fc = _dot_exact(tri, lf) + carry[0:1, :]
        carry[...] = jnp.broadcast_to(fc[tm - 1:tm, :], carry.shape)
        c, s = cos_ref[...], sin_ref[...]

        def section(n):
            p = _dot_nt(n1, wm_ref[n * GROUP_W:(n + 1) * GROUP_W, :])
            proj_ref[:, n * GROUP_W:(n + 1) * GROUP_W] = p.astype(BF)
            return p

        def rotate(p, out_ref, scale):
            for sl in _chunks(GROUP_W):
                out_ref[:, sl] = ((p[:, sl] * c + _swap32(p[:, sl]) * s) * scale).astype(BF)

        def norm_aug(p, gain, out_ref, scale, is_query):
            for j, sl in enumerate(_chunks(GROUP_W)):
                f = p[:, sl]
                f = f * lax.rsqrt(_group_mean64(f * f) + EPS) * gain * scale
                out_ref[:, 2 * j * LANES:2 * (j + 1) * LANES] = _aug_pair(f, [fc[:, 2 * j:2 * j + 1], fc[:, 2 * j + 1:2 * j + 2]], is_query)

        p_rq, p_rk = section(0), section(1)
        rotate(p_rq, rq_ref, 0.125)
        section(2)
        rotate(p_rk, rk_ref, 1.0)
        section(3)
        p_fq = section(4)
        p_fk = section(5)
        norm_aug(p_fq, gq_ref[...], qa_ref, 0.125 * LOG2E, True)
        section(6)
        norm_aug(p_fk, gk_ref[...], ka_ref, 1.0, False)

    row_spec = lambda w: pl.BlockSpec((tm, w), lambda i: (i, 0))
    full = lambda a: pl.BlockSpec(a.shape, lambda i: (0,) * a.ndim)
    return pl.pallas_call(
        body, name="in_proj_fwd", grid=(n_t,),
        out_shape=(jax.ShapeDtypeStruct((t_len, D_MODEL), BF), jax.ShapeDtypeStruct((t_len, MAIN_W), BF),
                   jax.ShapeDtypeStruct((t_len, GROUP_W), BF), jax.ShapeDtypeStruct((t_len, GROUP_W), BF),
                   jax.ShapeDtypeStruct((t_len, 2 * GROUP_W), BF), jax.ShapeDtypeStruct((t_len, 2 * GROUP_W), BF),
                   jax.ShapeDtypeStruct((t_len, LANES), F32)),
        in_specs=[row_spec(D_MODEL), full(g_mix), *_w_in_specs(), full(b_pad), row_spec(LANES), row_spec(LANES),
                  full(gq_t), full(gk_t)],
        out_specs=(row_spec(D_MODEL), row_spec(MAIN_W), row_spec(GROUP_W), row_spec(GROUP_W), row_spec(2 * GROUP_W),
                   row_spec(2 * GROUP_W), row_spec(LANES)),
        scratch_shapes=[pltpu.VMEM((8, LANES), F32)],
        compiler_params=_cparams(("arbitrary",)),
    )(x, g_mix, w_in_t, w_in_t, b_pad, cos_t, sin_t, gq_t, gk_t)


def _w_in_specs():
    return (pl.BlockSpec((MAIN_W, D_MODEL), lambda i: (0, 0)), pl.BlockSpec((LANES, D_MODEL), lambda i: (MAIN_W // LANES, 0)))


def _decay_tables(c):
    h = np.arange(N_HEADS, dtype=np.float64)
    lg = np.log(1.0 - 2.0 ** (-5.0 - h)).astype(np.float32).astype(np.float64)
    t = np.arange(c)
    same_or_earlier = (t[None, :] // REF_CHUNK) <= (t[:, None] // REF_CHUNK)
    w = np.where(same_or_earlier[None], np.exp(lg[:, None, None] * np.abs(t[:, None] - t[None, :])[None]), 0.0)
    qd = np.exp(lg[:, None] * (t[None, :] + 1.0))
    kd = np.exp(lg[:, None] * (c - 1.0 - t[None, :]))
    cd = np.exp(lg * c)
    ones = np.ones((1, 1, HEAD_DIM))
    return (jnp.asarray(w, F32), jnp.asarray(qd[:, :, None] * ones, F32), jnp.asarray(kd[:, :, None] * ones, F32),
            jnp.asarray(cd[:, None, None] * np.ones((1, HEAD_DIM, HEAD_DIM)), F32))


def _retention_fwd(rq, rk, proj, g_ret, tables):
    t_len = rq.shape[0]
    c = min(RET_BLOCK, t_len)
    n_b = t_len // c
    wdec, qdec, kdec, cdec = tables
    v_col, g_col = 2 * GROUP_W // LANES, 3 * GROUP_W // LANES

    def body(q_ref, k_ref, v_ref, rg_ref, g_ref, w_ref, qd_ref, kd_ref, cd_ref, raw_ref, mix_ref, st_ref, state):
        i = pl.program_id(1)

        @pl.when(i == 0)
        def _():
            state[...] = jnp.zeros_like(state)

        q2, k2, v2 = q_ref[...], k_ref[...], v_ref[...]
        outs = []
        for hh in range(2):
            sl = slice(hh * HEAD_DIM, (hh + 1) * HEAD_DIM)
            q, k, v = q2[:, sl], k2[:, sl], v2[:, sl]
            sp = state[hh]
            st_ref[0, 0, hh] = sp
            a = _dot_nt(q, k) * w_ref[hh]
            o = _dot(a, v) + _dot(q.astype(F32) * qd_ref[hh], sp)
            state[hh] = sp * cd_ref[hh] + _dot_tn(k.astype(F32) * kd_ref[hh], v)
            outs.append(o)
        o2 = jnp.concatenate(outs, axis=-1)
        raw_ref[...] = o2
        xc = o2 - _group_mean64(o2)
        xh = xc * lax.rsqrt(_group_mean64(xc * xc) + EPS)
        gate = rg_ref[...].astype(F32)
        mix_ref[...] = (gate * _sigmoid(gate) * (xh * g_ref[0])).astype(BF)

    blk = lambda col0: pl.BlockSpec((c, LANES), lambda hp, i: (i, col0 + hp))
    tab = lambda a: pl.BlockSpec((2,) + a.shape[1:], lambda hp, i: (hp, 0, 0))
    return pl.pallas_call(
        body, name="retention_fwd", grid=(N_HEADS // 2, n_b),
        out_shape=(jax.ShapeDtypeStruct((t_len, GROUP_W), F32), jax.ShapeDtypeStruct((t_len, GROUP_W), BF),
                   jax.ShapeDtypeStruct((N_HEADS // 2, n_b, 2, HEAD_DIM, HEAD_DIM), F32)),
        in_specs=[blk(0), blk(0), blk(v_col), blk(g_col), pl.BlockSpec((1, 1, LANES), lambda hp, i: (hp, 0, 0)),
                  tab(wdec), tab(qdec), tab(kdec), tab(cdec)],
        out_specs=(blk(0), blk(0), pl.BlockSpec((1, 1, 2, HEAD_DIM, HEAD_DIM), lambda hp, i: (hp, i, 0, 0, 0))),
        scratch_shapes=[pltpu.VMEM((2, HEAD_DIM, HEAD_DIM), F32)],
        compiler_params=_cparams(("arbitrary", "arbitrary")),
    )(rq, rk, proj, proj, g_ret, wdec, qdec, kdec, cdec)


def _fox_fwd(q_aug, k_aug, proj, shards):
    t_len = q_aug.shape[0]
    tq = min(ATT_BLOCK, t_len)
    n_q = t_len // tq
    v_col = 6 * GROUP_W // LANES
    tc = min(512, t_len)
    n_w = len(shards)
    n_steps = (N_HEADS // 2) * n_q

    def body(*refs):
        q_ref, k_ref, v_ref = refs[:3]
        o_ref, o32_ref, lse_ref = refs[3 + n_w:6 + n_w]
        vt = refs[6 + 2 * n_w]
        comm = (refs[3:3 + n_w], refs[6 + n_w:6 + 2 * n_w]) + tuple(refs[7 + 2 * n_w:])
        i = pl.program_id(1)
        step = pl.program_id(0) * n_q + i

        @pl.when(step == 0)
        def _():
            _gather_phase(0, *comm)

        @pl.when(step == (3 * n_steps) // 4)
        def _():
            _gather_phase(1, *comm)

        @pl.when(i == 0)
        def _():
            for c0 in range(0, t_len, tc):
                vt[:, c0:c0 + tc] = v_ref[c0:c0 + tc, :].T

        qs = [q_ref[:, hh * LANES:(hh + 1) * LANES] for hh in range(2)]
        ones = jnp.ones((HEAD_DIM, tq), BF)

        def scores(j):
            k2 = k_ref[pl.ds(pl.multiple_of(j * tq, tq), tq), :]
            return tuple(_dot_nt(k2[:, hh * LANES:(hh + 1) * LANES], qs[hh]) for hh in range(2))

        def update(j, ss, carry, masked):
            v2 = vt[:, pl.ds(pl.multiple_of(j * tq, tq), tq)]
            ps, stats = [], []
            for hh in range(2):
                m = carry[hh][0]
                s_t = ss[hh]
                if masked:
                    krow = lax.broadcasted_iota(jnp.int32, (tq, tq), 0)
                    qcol = lax.broadcasted_iota(jnp.int32, (tq, tq), 1)
                    s_t = jnp.where(qcol >= krow, s_t, NEG)
                m_new = jnp.maximum(m, jnp.max(s_t, axis=0, keepdims=True))
                ps.append(jnp.exp2(s_t - m_new).astype(BF))
                stats.append((m_new, jnp.exp2(m - m_new)))
            out = []
            for hh in range(2):
                m_new, alpha = stats[hh]
                v_aug = jnp.concatenate([v2[hh * HEAD_DIM:(hh + 1) * HEAD_DIM, :], ones], axis=0)
                out.append((m_new, carry[hh][1] * alpha + jnp.dot(v_aug, ps[hh], preferred_element_type=F32)))
            return tuple(out)

        def advance(j, state):
            ss, carry = state
            return scores(j + 1), update(j, ss, carry, False)

        init = tuple((jnp.full((1, tq), NEG, F32), jnp.zeros((LANES, tq), F32)) for _ in range(2))
        ss, carry = lax.fori_loop(0, i, advance, (scores(0), init))
        carry = update(i, ss, carry, True)
        outs, lses = [], []
        for hh in range(2):
            m, acc = carry[hh]
            l = acc[HEAD_DIM:HEAD_DIM + 1, :]
            outs.append(acc[:HEAD_DIM, :] / l)
            lses.append(m + jnp.log2(l))
        o2 = jnp.concatenate(outs, axis=0).T
        o32_ref[...] = o2
        o_ref[...] = o2.astype(BF)
        lse_ref[0] = jnp.concatenate(lses, axis=0)

        @pl.when(step == n_steps - 1)
        def _():
            _gather_phase(2, *comm)

    return pl.pallas_call(
        body, name="fox_fwd", grid=(N_HEADS // 2, n_q),
        out_shape=(jax.ShapeDtypeStruct((t_len, GROUP_W), BF), jax.ShapeDtypeStruct((t_len, GROUP_W), F32),
                   jax.ShapeDtypeStruct((N_HEADS // 2, 2, t_len), F32))
        + tuple(jax.ShapeDtypeStruct((4,) + s.shape, s.dtype) for s in shards),
        in_specs=[pl.BlockSpec((tq, 2 * LANES), lambda hp, i: (i, hp)),
                  pl.BlockSpec((t_len, 2 * LANES), lambda hp, i: (0, hp)),
                  pl.BlockSpec((t_len, LANES), lambda hp, i: (0, v_col + hp))] + [ANY] * n_w,
        out_specs=(pl.BlockSpec((tq, LANES), lambda hp, i: (i, hp)), pl.BlockSpec((tq, LANES), lambda hp, i: (i, hp)),
                   pl.BlockSpec((1, 2, tq), lambda hp, i: (hp, 0, i))) + (ANY,) * n_w,
        scratch_shapes=[pltpu.VMEM((LANES, t_len), BF)] + _gather_scratch(n_w),
        compiler_params=_cparams(("arbitrary", "arbitrary")),
    )(q_aug, k_aug, proj, *shards)


def _softmax_rows(s):
    p = jnp.exp(s - jnp.max(s, axis=-1, keepdims=True))
    return p / jnp.sum(p, axis=-1, keepdims=True)


def _attn_out_xattn_fwd(x, mix_r, mix_f, w_out, g_xattn, w_xq, g_xq, kn, v, w_xo):
    t_len = x.shape[0]
    tm = min(ROW_TILE, t_len)

    def body(x_ref, mr_ref, mf_ref, wo_ref, g_ref, wq_ref, gq_ref, kn_ref, v_ref, wxo_ref,
             h1_ref, hn_ref, qx_ref, o_ref, h2_ref):
        h1 = x_ref[...] + jnp.dot(mr_ref[...], wo_ref[:GROUP_W, :], preferred_element_type=F32) \
            + jnp.dot(mf_ref[...], wo_ref[GROUP_W:, :], preferred_element_type=F32)
        h1_ref[...] = h1
        hn = _rms_fwd(h1, g_ref[...]).astype(BF)
        hn_ref[...] = hn
        qx = jnp.dot(hn, wq_ref[...], preferred_element_type=F32).astype(BF)
        qx_ref[...] = qx
        sls = [slice(h * XHD, (h + 1) * XHD) for h in range(N_XH)]
        qns = [_rms_fwd(qx[:, sl].astype(F32), gq_ref[...]).astype(BF) for sl in sls]
        logits = [_dot_nt(qn, kn_ref[:, sl]) * (XHD ** -0.5) for qn, sl in zip(qns, sls)]
        ps = [_softmax_rows(s).astype(BF) for s in logits]
        for p, sl in zip(ps, sls):
            o_ref[:, sl] = jnp.dot(p, v_ref[:, sl], preferred_element_type=F32).astype(BF)
        h2_ref[...] = h1 + jnp.dot(o_ref[...], wxo_ref[...], preferred_element_type=F32)

    row_spec = lambda w: pl.BlockSpec((tm, w), lambda i: (i, 0))
    full = lambda a: pl.BlockSpec(a.shape, lambda i: (0,) * a.ndim)
    return pl.pallas_call(
        body, name="attn_out_xattn_fwd", grid=(t_len // tm,),
        out_shape=(jax.ShapeDtypeStruct((t_len, D_MODEL), F32), jax.ShapeDtypeStruct((t_len, D_MODEL), BF),
                   jax.ShapeDtypeStruct((t_len, D_MODEL), BF), jax.ShapeDtypeStruct((t_len, D_MODEL), BF),
                   jax.ShapeDtypeStruct((t_len, D_MODEL), F32)),
        in_specs=[row_spec(D_MODEL), row_spec(GROUP_W), row_spec(GROUP_W), full(w_out), full(g_xattn), full(w_xq), full(g_xq),
                  full(kn), full(v), full(w_xo)],
        out_specs=(row_spec(D_MODEL),) * 5,
        compiler_params=_cparams(("arbitrary",)),
    )(x, mix_r, mix_f, w_out, g_xattn, w_xq, g_xq, kn, v, w_xo)


def _ffn_loss_fwd(h2, g_ffn, w_gate, w_up, w_down, target):
    t_len = h2.shape[0]
    tm = min(ROW_TILE, t_len)

    def body(h2_ref, g_ref, wg_ref, wu_ref, wd_ref, tgt_ref, hn_ref, gate_ref, up_ref, act_ref, dh3_ref, loss_ref):
        @pl.when(pl.program_id(0) == 0)
        def _():
            loss_ref[...] = jnp.zeros_like(loss_ref)

        h2v = h2_ref[...]
        hn = _rms_fwd(h2v, g_ref[...]).astype(BF)
        hn_ref[...] = hn
        gate = _dot_nt(hn, wg_ref[...])
        up = _dot_nt(hn, wu_ref[...])
        gate_ref[...] = gate.astype(BF)
        up_ref[...] = up.astype(BF)
        act = (gate * _sigmoid(gate) * up).astype(BF)
        act_ref[...] = act
        diff = h2v + jnp.dot(act, wd_ref[...], preferred_element_type=F32) - tgt_ref[...]
        dh3_ref[...] = diff * (1.0 / D_MODEL)
        per_row = jnp.sum(diff * diff, axis=-1, keepdims=True) * (1.0 / D_MODEL)
        loss_ref[...] += 0.5 * jnp.sum(per_row, axis=0, keepdims=True)

    row_spec = lambda w: pl.BlockSpec((tm, w), lambda i: (i, 0))
    full = lambda a: pl.BlockSpec(a.shape, lambda i: (0,) * a.ndim, pipeline_mode=pl.Buffered(1))
    return pl.pallas_call(
        body, name="ffn_loss_fwd", grid=(t_len // tm,),
        out_shape=(jax.ShapeDtypeStruct((t_len, D_MODEL), BF), jax.ShapeDtypeStruct((t_len, D_FF), BF),
                   jax.ShapeDtypeStruct((t_len, D_FF), BF), jax.ShapeDtypeStruct((t_len, D_FF), BF),
                   jax.ShapeDtypeStruct((t_len, D_MODEL), F32), jax.ShapeDtypeStruct((8, LANES), F32)),
        in_specs=[row_spec(D_MODEL), full(g_ffn), full(w_gate), full(w_up), full(w_down), row_spec(D_MODEL)],
        out_specs=(row_spec(D_MODEL), row_spec(D_FF), row_spec(D_FF), row_spec(D_FF), row_spec(D_MODEL),
                   pl.BlockSpec((8, LANES), lambda i: (0, 0))),
        compiler_params=_cparams(("arbitrary",)),
    )(h2, g_ffn, w_gate, w_up, w_down, target)


def _ffn_bwd(dh3, gate, up, h2, g_ffn, w_gate, w_up, w_down):
    t_len = h2.shape[0]
    tm = min(ROW_TILE, t_len)

    def body(dh3_ref, gate_ref, up_ref, h2_ref, g_ref, wg_ref, wu_ref, wd_ref, dgate_ref, dup_ref, dh2_ref, dg_ref):
        @pl.when(pl.program_id(0) == 0)
        def _():
            dg_ref[...] = jnp.zeros_like(dg_ref)

        dh3v = dh3_ref[...]
        dact = _dot_nt(dh3v, wd_ref[...])
        g = gate_ref[...].astype(F32)
        sg = _sigmoid(g)
        dup = (dact * (g * sg)).astype(BF)
        dgate = (dact * up_ref[...].astype(F32) * (sg * (1.0 + g * (1.0 - sg)))).astype(BF)
        dup_ref[...] = dup
        dgate_ref[...] = dgate
        dhn = jnp.dot(dgate, wg_ref[...], preferred_element_type=F32) + jnp.dot(dup, wu_ref[...], preferred_element_type=F32)
        dx, dg = _rms_bwd(h2_ref[...], g_ref[...], dhn)
        dh2_ref[...] = dh3v + dx
        dg_ref[...] += dg

    row_spec = lambda w: pl.BlockSpec((tm, w), lambda i: (i, 0))
    full = lambda a: pl.BlockSpec(a.shape, lambda i: (0,) * a.ndim, pipeline_mode=pl.Buffered(1))
    return pl.pallas_call(
        body, name="ffn_bwd", grid=(t_len // tm,),
        out_shape=(jax.ShapeDtypeStruct((t_len, D_FF), BF), jax.ShapeDtypeStruct((t_len, D_FF), BF),
                   jax.ShapeDtypeStruct((t_len, D_MODEL), F32), jax.ShapeDtypeStruct((1, D_MODEL), F32)),
        in_specs=[row_spec(D_MODEL), row_spec(D_FF), row_spec(D_FF), row_spec(D_MODEL), full(g_ffn), full(w_gate), full(w_up),
                  full(w_down)],
        out_specs=(row_spec(D_FF), row_spec(D_FF), row_spec(D_MODEL), pl.BlockSpec((1, D_MODEL), lambda i: (0, 0))),
        compiler_params=_cparams(("arbitrary",)),
    )(dh3, gate, up, h2, g_ffn, w_gate, w_up, w_down)


def _attn_out_xattn_bwd(dh2, h1, qx, kn, v, w_xo, w_xq, w_out, g_xattn, g_xq):
    t_len = h1.shape[0]
    tm = min(ROW_TILE, t_len)
    m_tok = kn.shape[0]

    def body(dh2_ref, h1_ref, qx_ref, kn_ref, v_ref, wxo_ref, wq_ref, wo_ref, g_ref, gq_ref,
             dqx_ref, dh1_ref, dmr_ref, dmf_ref, dkn_ref, dv_ref, dg_ref, dgq_ref, dqx_scr):
        @pl.when(pl.program_id(0) == 0)
        def _():
            dkn_ref[...] = jnp.zeros_like(dkn_ref)
            dv_ref[...] = jnp.zeros_like(dv_ref)
            dg_ref[...] = jnp.zeros_like(dg_ref)
            dgq_ref[...] = jnp.zeros_like(dgq_ref)

        dh2v = dh2_ref[...]
        do = _dot_nt(dh2v, wxo_ref[...])
        gq = gq_ref[...]
        sls = [slice(h * XHD, (h + 1) * XHD) for h in range(N_XH)]
        qraws = [qx_ref[:, sl].astype(F32) for sl in sls]
        qns = [_rms_fwd(qraw, gq).astype(BF) for qraw in qraws]
        dohs = [do[:, sl].astype(BF) for sl in sls]
        logits = [_dot_nt(qn, kn_ref[:, sl]) * (XHD ** -0.5) for qn, sl in zip(qns, sls)]
        dps = [_dot_nt(doh, v_ref[:, sl]) for doh, sl in zip(dohs, sls)]
        ps = [_softmax_rows(s) for s in logits]
        dss = [(p * (dp - jnp.sum(dp * p, axis=-1, keepdims=True)) * (XHD ** -0.5)).astype(BF) for p, dp in zip(ps, dps)]
        dqns = []
        for h, sl in enumerate(sls):
            dv_ref[:, sl] += _dot_tn(ps[h], dohs[h])
            dqns.append(jnp.dot(dss[h], kn_ref[:, sl], preferred_element_type=F32))
            dkn_ref[:, sl] += _dot_tn(dss[h], qns[h])
        dgq = jnp.zeros((1, XHD), F32)
        for h, sl in enumerate(sls):
            dx, dg_h = _rms_bwd(qraws[h], gq, dqns[h])
            dgq = dgq + dg_h
            dqx_scr[:, sl] = dx.astype(BF)
        dgq_ref[...] += dgq
        dqx = dqx_scr[...]
        dqx_ref[...] = dqx
        dhn = _dot_nt(dqx, wq_ref[...])
        dx, dg = _rms_bwd(h1_ref[...], g_ref[...], dhn)
        dg_ref[...] += dg
        dh1 = dh2v + dx
        dh1_ref[...] = dh1
        dmix = _dot_nt(dh1, wo_ref[...])
        dmr_ref[...] = dmix[:, :GROUP_W]
        dmf_ref[...] = dmix[:, GROUP_W:].astype(BF)

    row_spec = lambda w: pl.BlockSpec((tm, w), lambda i: (i, 0))
    full = lambda a: pl.BlockSpec(a.shape, lambda i: (0,) * a.ndim)
    acc = lambda r, c: pl.BlockSpec((r, c), lambda i: (0, 0))
    return pl.pallas_call(
        body, name="attn_out_xattn_bwd", grid=(t_len // tm,),
        out_shape=(jax.ShapeDtypeStruct((t_len, D_MODEL), BF), jax.ShapeDtypeStruct((t_len, D_MODEL), F32),
                   jax.ShapeDtypeStruct((t_len, GROUP_W), F32), jax.ShapeDtypeStruct((t_len, GROUP_W), BF),
                   jax.ShapeDtypeStruct((m_tok, D_MODEL), F32), jax.ShapeDtypeStruct((m_tok, D_MODEL), F32),
                   jax.ShapeDtypeStruct((1, D_MODEL), F32), jax.ShapeDtypeStruct((1, XHD), F32)),
        in_specs=[row_spec(D_MODEL), row_spec(D_MODEL), row_spec(D_MODEL), full(kn), full(v), full(w_xo), full(w_xq), full(w_out),
                  full(g_xattn), full(g_xq)],
        out_specs=(row_spec(D_MODEL), row_spec(D_MODEL), row_spec(GROUP_W), row_spec(GROUP_W), acc(m_tok, D_MODEL),
                   acc(m_tok, D_MODEL), acc(1, D_MODEL), acc(1, XHD)),
        scratch_shapes=[pltpu.VMEM((tm, D_MODEL), BF)],
        compiler_params=_cparams(("arbitrary",)),
    )(dh2, h1, qx, kn, v, w_xo, w_xq, w_out, g_xattn, g_xq)


def _mem_kv_bwd(dkn, dv, kraw, mem, memn, g_mem, g_xk, w_xkv):
    m_tok = mem.shape[0]

    def body(dkn_ref, dv_ref, kraw_ref, mem_ref, memn_ref, gm_ref, gk_ref, w_ref, dw_ref, dgm_ref, dgk_ref, dkv_scr):
        gk = gk_ref[...]
        dgk = jnp.zeros((1, XHD), F32)
        for h in range(N_XH):
            sl = slice(h * XHD, (h + 1) * XHD)
            dx, dg_h = _rms_bwd(kraw_ref[:, sl], gk, dkn_ref[:, sl])
            dgk = dgk + dg_h
            dkv_scr[:, sl] = dx.astype(BF)
        dgk_ref[...] = dgk
        dkv_scr[:, D_MODEL:] = dv_ref[...].astype(BF)
        dkv = dkv_scr[...]
        dw_ref[...] = _dot_tn(memn_ref[...], dkv)
        dmemn = _dot_nt(dkv, w_ref[...])
        mem_v = mem_ref[...]
        r = lax.rsqrt(jnp.mean(mem_v * mem_v, axis=-1, keepdims=True) + EPS)
        dgm_ref[...] = jnp.sum(dmemn * mem_v * r, axis=0, keepdims=True)

    return pl.pallas_call(
        body, name="mem_kv_bwd",
        out_shape=(jax.ShapeDtypeStruct((D_MODEL, 2 * D_MODEL), F32), jax.ShapeDtypeStruct((1, D_MODEL), F32),
                   jax.ShapeDtypeStruct((1, XHD), F32)),
        in_specs=[VMEM_SPEC] * 8, out_specs=(VMEM_SPEC,) * 3,
        scratch_shapes=[pltpu.VMEM((m_tok, 2 * D_MODEL), BF)],
        compiler_params=_cparams(),
    )(dkn, dv, kraw, mem, memn, g_mem, g_xk, w_xkv)


def _fox_bwd(q_aug, k_aug, proj, dmf, o32, lse, sums):
    t_len = q_aug.shape[0]
    tb = min(ATT_BLOCK, t_len)
    n_b = t_len // tb
    v_col = 6 * GROUP_W // LANES
    n_w = len(sums)
    n_steps = (N_HEADS // 2) * n_b

    def body(*refs):
        k_ref, v_ref, q_ref, do_ref, o_ref, lse_ref = refs[:6]
        dq_ref, dk_ref, dv_ref, df_ref = refs[6 + n_w:10 + n_w]
        delta = refs[10 + 2 * n_w]
        comm = (refs[6:6 + n_w], refs[10 + n_w:10 + 2 * n_w]) + tuple(refs[11 + 2 * n_w:])
        j = pl.program_id(1)
        step = pl.program_id(0) * n_b + j

        @pl.when(step == 0)
        def _():
            _scatter_phase(0, *comm)

        @pl.when(j == 0)
        def _():
            dq_ref[...] = jnp.zeros_like(dq_ref)
            dd = do_ref[...].astype(F32) * o_ref[...]
            hrow = lax.broadcasted_iota(jnp.int32, (8, LANES), 0)
            lane = lax.broadcasted_iota(jnp.int32, (8, LANES), 1)
            ind = ((lane // HEAD_DIM) == hrow).astype(BF)
            delta[...] = _dot_nt_exact(ind, dd)

        k2, v2 = k_ref[...], v_ref[...]
        ks = [k2[:, hh * LANES:(hh + 1) * LANES] for hh in range(2)]
        vs = [v2[:, hh * HEAD_DIM:(hh + 1) * HEAD_DIM] for hh in range(2)]

        def blocks(idx, carry, masked, valid):
            loaded = []
            for i in idx:
                rows = pl.ds(pl.multiple_of(i * tb, tb), tb)
                q2 = q_ref[rows, :]
                do2 = do_ref[rows, :]
                loaded.append((rows, [q2[:, hh * LANES:(hh + 1) * LANES] for hh in range(2)],
                               [do2[:, hh * HEAD_DIM:(hh + 1) * HEAD_DIM] for hh in range(2)]))
            ss = [[_dot_nt(ks[hh], qs[hh]) for hh in range(2)] for _, qs, _ in loaded]
            dps = [[_dot_nt(vs[hh], dos[hh]) for hh in range(2)] for _, _, dos in loaded]
            pts, dsts, dfs = [], [], []
            for b, (rows, _, _) in enumerate(loaded):
                for hh in range(2):
                    s_t = ss[b][hh]
                    if masked[b]:
                        krow = lax.broadcasted_iota(jnp.int32, (tb, tb), 0)
                        qcol = lax.broadcasted_iota(jnp.int32, (tb, tb), 1)
                        s_t = jnp.where(qcol >= krow, s_t, NEG)
                    lse_row = lse_ref[0, hh:hh + 1, rows]
                    if valid[b] is not None:
                        lse_row = jnp.where(valid[b], lse_row, -NEG)
                    p_t = jnp.exp2(s_t - lse_row)
                    pts.append(p_t.astype(BF))
                    ds_t = p_t * (dps[b][hh] - delta[hh:hh + 1, rows])
                    dsts.append(ds_t.astype(BF))
                    dfs.append(jnp.sum(ds_t, axis=-1, keepdims=True))
            out = list(carry)
            for b, (rows, qs, dos) in enumerate(loaded):
                for hh in range(2):
                    dk, dv, df = out[hh]
                    dv = dv + jnp.dot(pts[2 * b + hh], dos[hh], preferred_element_type=F32)
                    dk = dk + jnp.dot(dsts[2 * b + hh], qs[hh], preferred_element_type=F32)
                    dq_ref[rows, hh * HEAD_DIM:(hh + 1) * HEAD_DIM] += _dot_tn(dsts[2 * b + hh], ks[hh])[:, :HEAD_DIM]
                    out[hh] = (dk, dv, df - dfs[2 * b + hh])
            return tuple(out)

        def pair(t, carry):
            i0 = j + 1 + 2 * t
            return blocks([i0, jnp.minimum(i0 + 1, n_b - 1)], carry, [False, False], [None, i0 + 1 < n_b])

        init = tuple((jnp.zeros((tb, LANES), F32), jnp.zeros((tb, HEAD_DIM), F32), jnp.zeros((tb, 1), F32)) for _ in range(2))
        carry = blocks([j], init, [True], [None])
        carry = lax.fori_loop(0, (n_b - j) // 2, pair, carry)
        dk_ref[...] = jnp.concatenate([carry[hh][0][:, :HEAD_DIM] for hh in range(2)], axis=-1) * LN2
        dv_ref[...] = jnp.concatenate([carry[hh][1] for hh in range(2)], axis=-1)
        df_ref[0] = jnp.concatenate([carry[hh][2] for hh in range(2)], axis=-1)

        @pl.when(step == n_steps - 1)
        def _():
            _scatter_phase(1, *comm)

    blk = lambda w, col0: pl.BlockSpec((tb, w), lambda hp, j: (j, col0 + hp))
    whole = lambda w: pl.BlockSpec((t_len, w), lambda hp, j: (0, hp))
    rows2 = pl.BlockSpec((1, 2, t_len), lambda hp, j: (hp, 0, 0))
    cols2 = pl.BlockSpec((1, tb, 2), lambda hp, j: (hp, j, 0))
    return pl.pallas_call(
        body, name="fox_bwd", grid=(N_HEADS // 2, n_b),
        out_shape=(jax.ShapeDtypeStruct((t_len, GROUP_W), F32), jax.ShapeDtypeStruct((t_len, GROUP_W), F32),
                   jax.ShapeDtypeStruct((t_len, GROUP_W), F32), jax.ShapeDtypeStruct((N_HEADS // 2, t_len, 2), F32))
        + _scatter_out_shapes(sums),
        in_specs=[blk(2 * LANES, 0), blk(LANES, v_col), whole(2 * LANES), whole(LANES), whole(LANES), rows2] + [ANY] * n_w,
        out_specs=(whole(LANES), blk(LANES, 0), blk(LANES, 0), cols2) + (ANY,) * n_w,
        scratch_shapes=[pltpu.VMEM((8, t_len), F32)] + _scatter_scratch(n_w),
        compiler_params=_cparams(("arbitrary", "arbitrary")),
    )(k_aug, proj, q_aug, dmf, o32, lse, *sums)


def _retention_bwd(dmr, raw, proj, g_ret, rq, rk, states, tables, parts):
    t_len = rq.shape[0]
    c = min(RET_BLOCK, t_len)
    n_b = t_len // c
    wdec, qdec, kdec, cdec = tables
    v_col, g_col = 2 * GROUP_W // LANES, 3 * GROUP_W // LANES
    n_w = len(parts)
    n_steps = (N_HEADS // 2) * n_b

    def body(*refs):
        d_ref, raw_ref, rg_ref, g_ref, q_ref, k_ref, v_ref, st_ref, w_ref, wt_ref, qd_ref, kd_ref, cd_ref = refs[:13]
        dq_ref, dk_ref, dv_ref, drg_ref, dg_ref = refs[13 + n_w:18 + n_w]
        gstate = refs[18 + 2 * n_w]
        comm = (refs[13:13 + n_w], refs[18 + n_w:18 + 2 * n_w]) + tuple(refs[19 + 2 * n_w:])
        step = pl.program_id(0) * n_b + pl.program_id(1)

        @pl.when(step == 0)
        def _():
            _exchange_phase(0, *comm)

        @pl.when(pl.program_id(1) == 0)
        def _():
            gstate[...] = jnp.zeros_like(gstate)
            dg_ref[...] = jnp.zeros_like(dg_ref)

        d, raw_v, g = d_ref[...], raw_ref[...], g_ref[0]
        gate = rg_ref[...].astype(F32)
        xc = raw_v - _group_mean64(raw_v)
        r = lax.rsqrt(_group_mean64(xc * xc) + EPS)
        xh = xc * r
        sg = _sigmoid(gate)
        drg_ref[...] = d * (xh * g) * (sg * (1.0 + gate * (1.0 - sg)))
        dy = d * (gate * sg)
        dg_ref[0] += jnp.sum(dy * xh, axis=0, keepdims=True)
        dxh = dy * g
        do2 = r * (dxh - _group_mean64(dxh) - xh * _group_mean64(dxh * xh))
        q2, k2, v2 = q_ref[...], k_ref[...], v_ref[...]
        dqs, dks, dvs = [], [], []
        heads = [tuple(t[:, hh * HEAD_DIM:(hh + 1) * HEAD_DIM] for t in (q2, k2, v2, do2.astype(BF))) for hh in range(2)]
        firsts = [(_dot_nt(k, q) * wt_ref[hh], _dot_nt(do, v) * w_ref[hh], _dot_nt(v, do) * wt_ref[hh])
                  for hh, (q, k, v, do) in enumerate(heads)]
        for hh, (q, k, v, do) in enumerate(heads):
            a_t, dm, dm_t = firsts[hh]
            sp, gs = st_ref[0, 0, hh], gstate[hh]
            qd = q.astype(F32) * qd_ref[hh]
            kd = k.astype(F32) * kd_ref[hh]
            dqs.append(_dot(dm, k) + _dot_nt(do, sp) * qd_ref[hh])
            dks.append(_dot(dm_t, q) + _dot_nt(v, gs) * kd_ref[hh])
            dvs.append(_dot(a_t, do) + _dot(kd, gs))
            gstate[hh] = gs * cd_ref[hh] + _dot_tn(qd, do)
        dq_ref[...] = jnp.concatenate(dqs, axis=-1)
        dk_ref[...] = jnp.concatenate(dks, axis=-1)
        dv_ref[...] = jnp.concatenate(dvs, axis=-1)

        @pl.when(step == n_steps - 1)
        def _():
            _exchange_phase(1, *comm)

    blk = lambda col0: pl.BlockSpec((c, LANES), lambda hp, i: (n_b - 1 - i, col0 + hp))
    tab = lambda a: pl.BlockSpec((2,) + a.shape[1:], lambda hp, i: (hp, 0, 0))
    gspec = pl.BlockSpec((1, 1, LANES), lambda hp, i: (hp, 0, 0))
    return pl.pallas_call(
        body, name="retention_bwd", grid=(N_HEADS // 2, n_b),
        out_shape=(jax.ShapeDtypeStruct((t_len, GROUP_W), F32),) * 4 + (jax.ShapeDtypeStruct((N_HEADS // 2, 1, LANES), F32),)
        + _exchange_out_shapes(parts),
        in_specs=[blk(0), blk(0), blk(g_col), gspec, blk(0), blk(0), blk(v_col),
                  pl.BlockSpec((1, 1, 2, HEAD_DIM, HEAD_DIM), lambda hp, i: (hp, n_b - 1 - i, 0, 0, 0)),
                  tab(wdec), tab(wdec), tab(qdec), tab(kdec), tab(cdec)] + [ANY] * n_w,
        out_specs=(blk(0), blk(0), blk(0), blk(0), gspec) + (ANY,) * n_w,
        scratch_shapes=[pltpu.VMEM((2, HEAD_DIM, HEAD_DIM), F32)] + _exchange_scratch(n_w),
        compiler_params=_cparams(("arbitrary", "arbitrary")),
    )(dmr, raw, proj, g_ret, rq, rk, proj, states, wdec, jnp.transpose(wdec, (0, 2, 1)), qdec, kdec, cdec, *parts)


def _in_proj_bwd(x, g_mix, dh1, dq_r, dk_r, dv_r, drg, dq_f, dk_f, dv_f, df_col, proj, z, cos_t, sin_t, gq_t, gk_t, w_in_t):
    t_len = x.shape[0]
    tm = min(ROW_TILE, t_len)
    n_t = t_len // tm

    def body(x_ref, g_ref, dh1_ref, dqr_ref, dkr_ref, dvr_ref, drg_ref, dqf_ref, dkf_ref, dvf_ref, df_ref, fq_ref, fk_ref, z_ref,
             cos_ref, sin_ref, gq_ref, gk_ref, wm_ref, wf_ref,
             dproj_ref, dz_ref, dx_ref, dg_ref, dgq_ref, dgk_ref, db_ref, carry, gq_acc, gk_acc):
        i = pl.program_id(0)

        @pl.when(i == 0)
        def _():
            carry[...] = jnp.zeros_like(carry)
            gq_acc[...] = jnp.zeros_like(gq_acc)
            gk_acc[...] = jnp.zeros_like(gk_acc)
            dg_ref[...] = jnp.zeros_like(dg_ref)
            db_ref[...] = jnp.zeros_like(db_ref)

        c, s = cos_ref[...], sin_ref[...]
        gq, gk = gq_ref[...], gk_ref[...]
        dgq = jnp.zeros((1, LANES), F32)
        dgk = jnp.zeros((1, LANES), F32)
        for sl in _chunks(GROUP_W):
            dy = dqr_ref[:, sl] * 0.125
            dproj_ref[:, sl] = (dy * c + _swap32(dy * s)).astype(BF)
            dy = dkr_ref[:, sl]
            dproj_ref[:, GROUP_W + sl.start:GROUP_W + sl.stop] = (dy * c + _swap32(dy * s)).astype(BF)
            dproj_ref[:, 2 * GROUP_W + sl.start:2 * GROUP_W + sl.stop] = dvr_ref[:, sl].astype(BF)
            dproj_ref[:, 3 * GROUP_W + sl.start:3 * GROUP_W + sl.stop] = drg_ref[:, sl].astype(BF)
            for src, dsrc, gain, off in ((fq_ref, dqf_ref, gq, 4), (fk_ref, dkf_ref, gk, 5)):
                xr = src[:, sl].astype(F32)
                r = lax.rsqrt(_group_mean64(xr * xr) + EPS)
                xh = xr * r
                dy = dsrc[:, sl] * (0.125 if off == 4 else 1.0)
                dgs = jnp.sum(dy * xh, axis=0, keepdims=True)
                if off == 4:
                    dgq = dgq + dgs
                else:
                    dgk = dgk + dgs
                dxh = dy * gain
                dproj_ref[:, off * GROUP_W + sl.start:off * GROUP_W + sl.stop] = \
                    (r * (dxh - xh * _group_mean64(dxh * xh))).astype(BF)
            dproj_ref[:, 6 * GROUP_W + sl.start:6 * GROUP_W + sl.stop] = dvf_ref[:, sl].astype(BF)
        gq_acc[...] += dgq
        gk_acc[...] += dgk
        row = lax.broadcasted_iota(jnp.int32, (tm, tm), 0)
        col = lax.broadcasted_iota(jnp.int32, (tm, tm), 1)
        dlf = _dot_exact((col >= row).astype(BF), df_ref[...]) + carry[0:1, :]
        carry[...] = jnp.broadcast_to(dlf[0:1, :], carry.shape)
        lane = lax.broadcasted_iota(jnp.int32, (tm, LANES), 1)
        dz = jnp.where(lane < N_HEADS, dlf / (1.0 + jnp.exp(z_ref[...])), 0.0)
        db_ref[...] += jnp.sum(dz, axis=0, keepdims=True)
        dz_bf = dz.astype(BF)
        dz_ref[...] = dz_bf
        dn1 = jnp.dot(dz_bf, wf_ref[...], preferred_element_type=F32)
        for sec in range(MAIN_W // GROUP_W):
            sl = slice(sec * GROUP_W, (sec + 1) * GROUP_W)
            dn1 = dn1 + jnp.dot(dproj_ref[:, sl], wm_ref[sl, :], preferred_element_type=F32)
        dx, dg = _rms_bwd(x_ref[...], g_ref[...], dn1)
        dx_ref[...] = dh1_ref[...] + dx
        dg_ref[...] += dg

        @pl.when(i == n_t - 1)
        def _():
            dgq_ref[...] = gq_acc[:, :HEAD_DIM] + gq_acc[:, HEAD_DIM:]
            dgk_ref[...] = gk_acc[:, :HEAD_DIM] + gk_acc[:, HEAD_DIM:]

    row_spec = lambda w, col=0: pl.BlockSpec((tm, w), lambda i: (n_t - 1 - i, col))
    full = lambda a: pl.BlockSpec(a.shape, lambda i: (0,) * a.ndim)
    acc = lambda r, c: pl.BlockSpec((r, c), lambda i: (0, 0))
    return pl.pallas_call(
        body, name="in_proj_bwd", grid=(n_t,),
        out_shape=(jax.ShapeDtypeStruct((t_len, MAIN_W), BF), jax.ShapeDtypeStruct((t_len, LANES), BF),
                   jax.ShapeDtypeStruct((t_len, D_MODEL), F32), jax.ShapeDtypeStruct((1, D_MODEL), F32),
                   jax.ShapeDtypeStruct((1, HEAD_DIM), F32), jax.ShapeDtypeStruct((1, HEAD_DIM), F32),
                   jax.ShapeDtypeStruct((1, LANES), F32)),
        in_specs=[row_spec(D_MODEL), full(g_mix), row_spec(D_MODEL)] + [row_spec(GROUP_W)] * 7
        + [row_spec(LANES), row_spec(GROUP_W, 4), row_spec(GROUP_W, 5), row_spec(LANES), row_spec(LANES), row_spec(LANES),
           full(gq_t), full(gk_t), *_w_in_specs()],
        out_specs=(row_spec(MAIN_W), row_spec(LANES), row_spec(D_MODEL), acc(1, D_MODEL), acc(1, HEAD_DIM), acc(1, HEAD_DIM),
                   acc(1, LANES)),
        scratch_shapes=[pltpu.VMEM((8, LANES), F32), pltpu.VMEM((1, LANES), F32), pltpu.VMEM((1, LANES), F32)],
        compiler_params=_cparams(("arbitrary",)),
    )(x, g_mix, dh1, dq_r, dk_r, dv_r, drg, dq_f, dk_f, dv_f, df_col, proj, proj, z, cos_t, sin_t, gq_t, gk_t, w_in_t, w_in_t)


def _matmul_tn(a, b, name, bk=512):
    t_len, m = a.shape
    n = b.shape[1]
    bm = m if m <= TN_MAX_ROWS else m // 2
    bk = min(bk, t_len)

    def body(a_ref, b_ref, o_ref):
        @pl.when(pl.program_id(1) == 0)
        def _():
            o_ref[...] = jnp.zeros_like(o_ref)

        o_ref[...] += _dot_tn(a_ref[...], b_ref[...])

    return pl.pallas_call(
        body, name=name, grid=(m // bm, t_len // bk),
        out_shape=jax.ShapeDtypeStruct((m, n), F32),
        in_specs=[pl.BlockSpec((bk, bm), lambda i, k: (k, i)), pl.BlockSpec((bk, n), lambda i, k: (k, 0))],
        out_specs=pl.BlockSpec((bm, n), lambda i, k: (i, 0)),
        compiler_params=_cparams(("arbitrary", "arbitrary")),
    )(a, b)


def _place():
    x, y, c = lax.axis_index("x"), lax.axis_index("y"), lax.axis_index("c")
    chips = [(1 - x, y), (x, 1 - y), (1 - x, 1 - y)]
    return x, y, c, chips


def _row_chunks(rows, limit):
    step = max(d for d in range(16, min(rows, limit) + 1, 16) if rows % d == 0)
    return [slice(i, i + step) for i in range(0, rows, step)]


ICI_CHUNK_ROWS = 128
D2D_CHUNK_ROWS = 64


def _gather_phase(phase, ins, outs, send_sems, recv_sems):
    x, y, c, chips = _place()
    me_chip = 2 * x + y
    sibling = (x, y, 1 - c)

    def copy(w, k, slot, half, to, rows=slice(None), src=None):
        dst = outs[w].at[slot, half, rows]
        return pltpu.make_async_remote_copy(src_ref=dst if src is None else src, dst_ref=dst,
                                            send_sem=send_sems.at[w, k], recv_sem=recv_sems.at[w, k],
                                            device_id=to, device_id_type=MESH)

    for w in range(len(ins)):
        for j, (px, py) in enumerate(chips):
            if phase == 0:
                for rows in _row_chunks(ins[w].shape[1], ICI_CHUNK_ROWS):
                    copy(w, j, me_chip, c, (px, py, c), rows, src=ins[w].at[c, rows]).start()
            elif phase == 1:
                copy(w, j, 2 * px + py, c, (x, y, c)).wait_recv()
                for rows in _row_chunks(ins[w].shape[1], D2D_CHUNK_ROWS):
                    copy(w, 3 + j, 2 * px + py, c, sibling, rows).start()
            else:
                copy(w, 3 + j, 2 * px + py, 1 - c, (x, y, c)).wait_recv()
                copy(w, j, me_chip, c, (px, py, c), src=ins[w].at[c]).wait_send()
                copy(w, 3 + j, 2 * px + py, c, sibling).wait_send()


def _gather_scratch(n_w):
    return [pltpu.SemaphoreType.DMA((n_w, 6)), pltpu.SemaphoreType.DMA((n_w, 6))]


def _all_gather_weights(shards):
    n_w = len(shards)

    def body(*refs):
        for phase in range(3):
            _gather_phase(phase, refs[:n_w], refs[n_w:2 * n_w], *refs[2 * n_w:])

    return pl.pallas_call(
        body, name="all_gather_weights",
        out_shape=tuple(jax.ShapeDtypeStruct((4,) + s.shape, s.dtype) for s in shards),
        in_specs=[ANY] * n_w, out_specs=(ANY,) * n_w, scratch_shapes=_gather_scratch(n_w),
    )(*shards)


def _exchange_phase(phase, ins, theirs, send_sems, recv_sems):
    x, y, c, _ = _place()

    def remote(w, k=slice(None), rows=slice(None)):
        return pltpu.make_async_remote_copy(src_ref=ins[w].at[k, 1 - c, rows], dst_ref=theirs[w].at[k, rows],
                                            send_sem=send_sems.at[w], recv_sem=recv_sems.at[w], device_id=(x, y, 1 - c),
                                            device_id_type=MESH)

    for w in range(len(ins)):
        if phase == 0:
            for k in range(4):
                for rows in _row_chunks(ins[w].shape[2], D2D_CHUNK_ROWS):
                    remote(w, k, rows).start()
        else:
            remote(w).wait()


def _exchange_scratch(n_w):
    return [pltpu.SemaphoreType.DMA((n_w,)), pltpu.SemaphoreType.DMA((n_w,))]


def _exchange_out_shapes(grads):
    return tuple(jax.ShapeDtypeStruct((4,) + g.shape[2:], g.dtype) for g in grads)


def _exchange_core_halves(grads):
    n_w = len(grads)

    def body(*refs):
        for phase in range(2):
            _exchange_phase(phase, refs[:n_w], refs[n_w:2 * n_w], *refs[2 * n_w:])

    return pl.pallas_call(
        body, name="exchange_core_halves", out_shape=_exchange_out_shapes(grads),
        in_specs=[ANY] * n_w, out_specs=(ANY,) * n_w, scratch_shapes=_exchange_scratch(n_w),
    )(*grads)


def _add_pairs(part, theirs, name):
    _, _, r, c = part.shape
    rb = 32 if r % 32 == 0 else r

    def body(a_ref, b_ref, own_ref, ob_ref):
        my_chip = 2 * lax.axis_index("x") + lax.axis_index("y")
        ob_ref[...] = (a_ref[...] + b_ref[...]).astype(BF)
        own_ref[...] = a_ref[my_chip] + b_ref[my_chip]

    spec = pl.BlockSpec((4, rb, c), lambda i: (0, i, 0))
    return pl.pallas_call(
        body, name=name, grid=(r // rb,),
        out_shape=(jax.ShapeDtypeStruct((r, c), F32), jax.ShapeDtypeStruct((4, r, c), BF)),
        in_specs=[pl.BlockSpec((4, None, rb, c), lambda i: (0, lax.axis_index("c"), i, 0)), spec],
        out_specs=(pl.BlockSpec((rb, c), lambda i: (i, 0)), spec), compiler_params=_cparams(("arbitrary",)),
    )(part, theirs)


def _scatter_phase(phase, bfs, got, send_sems, recv_sems):
    x, y, c, chips = _place()

    def remote(w, j, px, py, rows=slice(None)):
        return pltpu.make_async_remote_copy(src_ref=bfs[w].at[2 * px + py, rows], dst_ref=got[w].at[j, rows],
                                            send_sem=send_sems.at[w, j], recv_sem=recv_sems.at[w, j], device_id=(px, py, c),
                                            device_id_type=MESH)

    for w in range(len(bfs)):
        for j, (px, py) in enumerate(chips):
            if phase == 0:
                for rows in _row_chunks(bfs[w].shape[1], ICI_CHUNK_ROWS):
                    remote(w, j, px, py, rows).start()
            else:
                remote(w, j, px, py).wait()


def _scatter_scratch(n_w):
    return [pltpu.SemaphoreType.DMA((n_w, 3)), pltpu.SemaphoreType.DMA((n_w, 3))]


def _scatter_out_shapes(sums_bf16):
    return tuple(jax.ShapeDtypeStruct((3,) + s.shape[1:], BF) for s in sums_bf16)


def _add_received(own, got, name):
    r, c = own.shape
    rb = 32 if r % 32 == 0 else r

    def body(o_ref, g_ref, out_ref):
        out_ref[...] = ((o_ref[...] + g_ref[0].astype(F32)) + g_ref[1].astype(F32)) + g_ref[2].astype(F32)

    return pl.pallas_call(
        body, name=name, grid=(r // rb,), out_shape=jax.ShapeDtypeStruct((r, c), F32),
        in_specs=[pl.BlockSpec((rb, c), lambda i: (i, 0)), pl.BlockSpec((3, rb, c), lambda i: (0, i, 0))],
        out_specs=pl.BlockSpec((rb, c), lambda i: (i, 0)), compiler_params=_cparams(("arbitrary",)),
    )(own, got)


def _share_with_sibling(halves):
    n_w = len(halves)

    def body(*refs):
        ins, outs = refs[:n_w], refs[n_w:2 * n_w]
        send_sems, recv_sems = refs[2 * n_w:]
        x, y, c, _ = _place()

        def remote(w, rows=slice(None)):
            return pltpu.make_async_remote_copy(src_ref=ins[w].at[rows], dst_ref=outs[w].at[c, rows], send_sem=send_sems.at[w],
                                                recv_sem=recv_sems.at[w], device_id=(x, y, 1 - c), device_id_type=MESH)

        for w in range(n_w):
            for rows in _row_chunks(ins[w].shape[0], D2D_CHUNK_ROWS):
                remote(w, rows).start()
        for w in range(n_w):
            remote(w).wait()

    return pl.pallas_call(
        body, name="share_with_sibling",
        out_shape=tuple(jax.ShapeDtypeStruct((2,) + h.shape, h.dtype) for h in halves),
        in_specs=[ANY] * n_w, out_specs=(ANY,) * n_w,
        scratch_shapes=[pltpu.SemaphoreType.DMA((n_w,)), pltpu.SemaphoreType.DMA((n_w,))],
    )(*halves)


def _all_reduce_small(pack):
    r, c = pack.shape

    def body(p_ref, out_ref, slots, send_sems, recv_sems):
        x, y, cc, _ = _place()
        me = 4 * x + 2 * y + cc
        slots[me] = p_ref[...]
        copies = []
        for k in range(1, 8):
            dx, dy, dc = (k >> 2) & 1, (k >> 1) & 1, k & 1
            to = (1 - x if dx else x, 1 - y if dy else y, 1 - cc if dc else cc)
            cp = pltpu.make_async_remote_copy(src_ref=p_ref, dst_ref=slots.at[me], send_sem=send_sems.at[k - 1],
                                              recv_sem=recv_sems.at[k - 1], device_id=to, device_id_type=MESH)
            cp.start()
            copies.append(cp)
        for cp in copies:
            cp.wait()
        total = slots[0]
        for d in range(1, 8):
            total = total + slots[d]
        out_ref[...] = total

    return pl.pallas_call(
        body, name="all_reduce_small", out_shape=jax.ShapeDtypeStruct((r, c), F32),
        in_specs=[VMEM_SPEC], out_specs=VMEM_SPEC,
        scratch_shapes=[pltpu.VMEM((8, r, c), F32), pltpu.SemaphoreType.DMA((7,)), pltpu.SemaphoreType.DMA((7,))],
    )(pack)


def _adamw_update(w_ref, g_ref, m_ref, v_ref, d_ref, nm_ref, nv_ref):
    gv = g_ref[...]
    nm = ADAM_B1 * m_ref[...] + (1.0 - ADAM_B1) * gv
    nv = ADAM_B2 * v_ref[...] + (1.0 - ADAM_B2) * (gv * gv)
    nm_ref[...] = nm
    nv_ref[...] = nv
    m_hat = nm / (1.0 - ADAM_B1 ** ADAM_STEP)
    v_hat = nv / (1.0 - ADAM_B2 ** ADAM_STEP)
    d_ref[...] = -ADAM_LR * (m_hat / (jnp.sqrt(v_hat) + ADAM_EPS) + ADAM_WD * w_ref[...])


def _adamw_many(ws, gs, ms, vs, sums):
    n_a, n_w = len(ws), len(sums)
    n_steps = ADAM_STEPS
    specs = [pl.BlockSpec((w.shape[0] // n_steps, w.shape[1]), lambda i: (i, 0)) for w in ws]

    def body(*refs):
        ins = refs[:4 * n_a]
        outs = refs[4 * n_a + n_w:7 * n_a + n_w]
        comm = (refs[4 * n_a:4 * n_a + n_w], refs[7 * n_a + n_w:7 * n_a + 2 * n_w]) + tuple(refs[7 * n_a + 2 * n_w:])
        step = pl.program_id(0)

        @pl.when(step == 0)
        def _():
            _scatter_phase(0, *comm)

        for a in range(n_a):
            _adamw_update(*(ins[k * n_a + a] for k in range(4)), *(outs[3 * a + k] for k in range(3)))

        @pl.when(step == n_steps - 1)
        def _():
            _scatter_phase(1, *comm)

    flat = pl.pallas_call(
        body, name="adamw_late", grid=(n_steps,),
        out_shape=tuple(jax.ShapeDtypeStruct(w.shape, F32) for w in ws for _ in range(3)) + _scatter_out_shapes(sums),
        in_specs=specs * 4 + [ANY] * n_w, out_specs=tuple(s for s in specs for _ in range(3)) + (ANY,) * n_w,
        scratch_shapes=_scatter_scratch(n_w), compiler_params=_cparams(("arbitrary",)),
    )(*ws, *gs, *ms, *vs, *sums)
    return [tuple(flat[3 * a:3 * a + 3]) for a in range(n_a)] + list(flat[3 * n_a:])


def _adamw(w, g, m, v, name):
    r, c = w.shape
    rb, cb = (64, c) if r % 64 == 0 else (r, LANES if (r % 8 and c % LANES == 0) else c)

    def body(*refs):
        _adamw_update(*refs)

    spec = pl.BlockSpec((rb, cb), lambda i, j: (i, j))
    return pl.pallas_call(
        body, name=name, grid=(r // rb, c // cb), out_shape=(jax.ShapeDtypeStruct((r, c), F32),) * 3,
        in_specs=[spec] * 4, out_specs=(spec,) * 3, compiler_params=_cparams(("arbitrary", "arbitrary")),
    )(w, g, m, v)


def _rope_tables(t_len):
    inv_freq = ROPE_BASE ** (-jnp.arange(0, HEAD_DIM, 2, dtype=F32) / HEAD_DIM)
    ang = jnp.arange(t_len, dtype=F32)[:, None] * inv_freq[None, :]
    cos, sin = jnp.cos(ang), jnp.sin(ang)
    cos_t = jnp.concatenate([cos, cos, cos, cos], axis=-1)
    sin_t = jnp.concatenate([-sin, sin, -sin, sin], axis=-1)
    return cos_t, sin_t


def _cols_to_shards(dw):
    r, n = dw.shape
    return jnp.transpose(dw.reshape(2, r // 2, 4, n // 4), (2, 0, 1, 3))


def _rows_to_shards(dw):
    r, n = dw.shape
    padded = _pad_rows(dw.reshape(4, r // 4, n))
    return padded.reshape(4, 2, padded.shape[1] // 2, n)


def _pad_lanes(a):
    extra = -a.shape[-1] % LANES
    return a if extra == 0 else jnp.pad(a, [(0, 0)] * (a.ndim - 1) + [(0, extra)])


def _pad_rows(a):
    rows = a.shape[-2]
    extra = 0 if rows % SHARD_ROW_ALIGN == 0 else -rows % SHARD_ROW_PAD
    return a if extra == 0 else jnp.pad(a, [(0, 0)] * (a.ndim - 2) + [(0, extra), (0, 0)])


def _pad_row(a, width=D_MODEL):
    a = a.reshape(1, -1)
    return jnp.pad(a, ((0, 0), (0, width - a.shape[1])))


def kernel(x, mem, g_mix, w_in, b_forget, g_ret_out, g_fox_q, g_fox_k, w_out, g_xattn, w_xq, w_xkv, g_mem, g_xq, g_xk, w_xo, g_ffn, w_gate, w_up, w_down, loss_target, m_g_mix, m_w_in, m_b_forget, m_g_ret_out, m_g_fox_q, m_g_fox_k, m_w_out, m_g_xattn, m_w_xq, m_w_xkv, m_g_mem, m_g_xq, m_g_xk, m_w_xo, m_g_ffn, m_w_gate, m_w_up, m_w_down, v_g_mix, v_w_in, v_b_forget, v_g_ret_out, v_g_fox_q, v_g_fox_k, v_w_out, v_g_xattn, v_w_xq, v_w_xkv, v_g_mem, v_g_xq, v_g_xk, v_w_xo, v_g_ffn, v_w_gate, v_w_up, v_w_down):
    big = {"w_in": (w_in, m_w_in, v_w_in), "w_out": (w_out, m_w_out, v_w_out), "w_xq": (w_xq, m_w_xq, v_w_xq),
           "w_xkv": (w_xkv, m_w_xkv, v_w_xkv), "w_xo": (w_xo, m_w_xo, v_w_xo), "w_gate": (w_gate, m_w_gate, v_w_gate),
           "w_up": (w_up, m_w_up, v_w_up), "w_down": (w_down, m_w_down, v_w_down)}
    for n in TRANSPOSED:
        big[n] = tuple(jnp.swapaxes(a, 1, 2) for a in big[n])
    shards = {}
    for n in big:
        w = _pad_rows(_pad_lanes(big[n][0][0].astype(BF)))
        shards[n] = w.reshape(2, w.shape[0] // 2, w.shape[1])
    sizes = {n: big[n][0].shape[1:] for n in big}
    w_in_full = _assemble_weight("w_in", _all_gather_weights([shards["w_in"]])[0], shards["w_in"], sizes["w_in"])
    small_w ={"g_mix": g_mix, "b_forget": b_forget, "g_ret_out": g_ret_out, "g_fox_q": g_fox_q, "g_fox_k": g_fox_k,
               "g_xattn": g_xattn, "g_mem": g_mem, "g_xq": g_xq, "g_xk": g_xk, "g_ffn": g_ffn}
    m_small = {"g_mix": m_g_mix, "b_forget": m_b_forget, "g_ret_out": m_g_ret_out, "g_fox_q": m_g_fox_q, "g_fox_k": m_g_fox_k,
               "g_xattn": m_g_xattn, "g_mem": m_g_mem, "g_xq": m_g_xq, "g_xk": m_g_xk, "g_ffn": m_g_ffn}
    v_small = {"g_mix": v_g_mix, "b_forget": v_b_forget, "g_ret_out": v_g_ret_out, "g_fox_q": v_g_fox_q, "g_fox_k": v_g_fox_k,
               "g_xattn": v_g_xattn, "g_mem": v_g_mem, "g_xq": v_g_xq, "g_xk": v_g_xk, "g_ffn": v_g_ffn}
    loss_part, grad_x, sums, got, in_bf, small_g = _local_step(x[0], mem[0], loss_target[0], w_in_full, shards, sizes, small_w)
    return _reduce_and_update(big, sums, got, in_bf, small_w, small_g, loss_part, grad_x, m_small, v_small)


def _assemble_weight(name, gathered, own, size):
    rows, width = size
    my_chip = 2 * lax.axis_index("x") + lax.axis_index("y")
    g = lax.dynamic_update_slice(gathered, own[None], (my_chip, 0, 0, 0))
    g = g.reshape(4, 2 * g.shape[2], g.shape[3])[:, :rows, :width]
    return jnp.transpose(g, (1, 0, 2)).reshape(rows, 4 * width) if name in COL_SHARDED else g.reshape(4 * rows, width)


def _shard_parts(names, dw):
    return [_pad_lanes(_cols_to_shards(dw[n]) if n in COL_SHARDED else _rows_to_shards(dw[n])) for n in names]


def _core_sums(names, parts, theirs):
    return [_add_pairs(p, t, f"core_sum_{n}") for n, p, t in zip(names, parts, theirs)]


def _local_step(xs, mems, tgt, w_in_full, shards, sizes, small_w):
    g_mix, b_forget, g_ret_out, g_fox_q, g_fox_k = (small_w[n] for n in ("g_mix", "b_forget", "g_ret_out", "g_fox_q", "g_fox_k"))
    g_xattn, g_mem, g_xq, g_xk, g_ffn = (small_w[n] for n in ("g_xattn", "g_mem", "g_xq", "g_xk", "g_ffn"))
    w_in_t = jnp.pad(w_in_full, ((0, MAIN_W + LANES - IN_W), (0, 0)))
    t_len = xs.shape[0]
    cos_t, sin_t = _rope_tables(t_len)
    tables = _decay_tables(min(RET_BLOCK, t_len))
    gq_t = jnp.concatenate([g_fox_q, g_fox_q], axis=-1)
    gk_t = jnp.concatenate([g_fox_k, g_fox_k], axis=-1)
    b_pad = _pad_row(b_forget, LANES)
    g_ret = g_ret_out.reshape(N_HEADS // 2, 1, LANES)

    n1, proj, rq, rk, q_aug, k_aug, z = _in_proj_fwd(xs, g_mix, w_in_t, b_pad, cos_t, sin_t, gq_t, gk_t)
    raw, mix_r, states = _retention_fwd(rq, rk, proj, g_ret, tables)
    mix_f, o32, lse, *gathered = _fox_fwd(q_aug, k_aug, proj, [shards[n] for n in LATE])
    full = {n: _assemble_weight(n, g, shards[n], sizes[n]) for n, g in zip(LATE, gathered)}
    memn, kraw, kn, vmem = _mem_kv_fwd(mems, g_mem, full["w_xkv"], g_xk)
    h1, hn2, qx, o_x, h2 = _attn_out_xattn_fwd(xs, mix_r, mix_f, full["w_out"], g_xattn, full["w_xq"], g_xq, kn, vmem, full["w_xo"])
    hn3, gate, up, act, dh3, loss_part = _ffn_loss_fwd(h2, g_ffn, full["w_gate"], full["w_up"], full["w_down"], tgt)

    dgate, dup, dh2, dg_ffn = _ffn_bwd(dh3, gate, up, h2, g_ffn, full["w_gate"], full["w_up"], full["w_down"])
    dqx, dh1, dmr, dmf, dkn, dvm, dg_xattn, dg_xq = _attn_out_xattn_bwd(dh2, h1, qx, kn, vmem, full["w_xo"], full["w_xq"],
                                                                      full["w_out"], g_xattn, g_xq)
    dw_xkv, dg_mem, dg_xk = _mem_kv_bwd(dkn, dvm, kraw, mems, memn, g_mem, g_xk, full["w_xkv"])
    dw = {
        "w_out": jnp.concatenate([_matmul_tn(mix_r, dh1, "dw_out_ret"), _matmul_tn(mix_f, dh1, "dw_out_fox")], axis=0),
        "w_xq": _matmul_tn(hn2, dqx, "dw_xq"),
        "w_xkv": dw_xkv,
        "w_xo": _matmul_tn(o_x, dh2, "dw_xo"),
        "w_gate": _matmul_tn(dgate, hn3, "dw_gate"),
        "w_up": _matmul_tn(dup, hn3, "dw_up"),
        "w_down": _matmul_tn(act, dh3, "dw_down"),
    }
    late_parts = _shard_parts(LATE, dw)
    dq_r, dk_r, dv_r, drg, dg_ret, *late_theirs = _retention_bwd(dmr, raw, proj, g_ret, rq, rk, states, tables, late_parts)
    late_sums = _core_sums(LATE, late_parts, late_theirs)
    dq_f, dk_f, dv_f, df, *late_got = _fox_bwd(q_aug, k_aug, proj, dmf, o32, lse, [s[1] for s in late_sums])
    df_col = jnp.pad(jnp.transpose(df, (1, 0, 2)).reshape(t_len, N_HEADS), ((0, 0), (0, LANES - N_HEADS)))
    dproj, dz, grad_x, dg_mix, dg_fq, dg_fk, db = _in_proj_bwd(xs, g_mix, dh1, dq_r, dk_r, dv_r, drg, dq_f, dk_f, dv_f, df_col,
                                                              proj, z, cos_t, sin_t, gq_t, gk_t, w_in_t)

    dw_in = jnp.concatenate([_matmul_tn(dproj, n1, "dw_in_main"), _matmul_tn(dz, n1, "dw_in_ff")[:IN_W - MAIN_W]], axis=0)
    in_parts = _shard_parts(("w_in",), {"w_in": dw_in})
    in_sums = _core_sums(("w_in",), in_parts, _exchange_core_halves(in_parts))
    sums = {n: s[0] for n, s in zip(("w_in",) + LATE, in_sums + late_sums)}
    got = dict(zip(LATE, late_got))
    in_bf = in_sums[0][1]
    small_g = {"g_mix": dg_mix, "b_forget": db[:, :N_HEADS], "g_ret_out": dg_ret, "g_fox_q": dg_fq, "g_fox_k": dg_fk,
               "g_xattn": dg_xattn, "g_mem": dg_mem, "g_xq": dg_xq, "g_xk": dg_xk, "g_ffn": dg_ffn}
    return loss_part, grad_x, sums, got, in_bf, small_g


def _final_grads(names, big, sums, got):
    my_core = lax.axis_index("c")
    finals = [_add_received(sums[n], got[n], f"chip_sum_{n}") for n in names]
    shared = _share_with_sibling(finals)
    out = {}
    for n, s, fin in zip(names, shared, finals):
        s = lax.dynamic_update_slice(s, fin[None], (my_core, 0, 0))
        out[n] = s.reshape(2 * s.shape[1], s.shape[2])[:big[n][0].shape[1], :big[n][0].shape[2]]
    return out


def _reduce_and_update(big, sums, got, in_bf, small_w, small_g, loss_part, grad_x, m_small, v_small):
    grads = _final_grads(LATE, big, sums, got)
    *late_updates, in_got = _adamw_many([big[n][0][0] for n in LATE], [grads[n] for n in LATE], [big[n][1][0] for n in LATE],
                                        [big[n][2][0] for n in LATE], [in_bf])
    updates = dict(zip(LATE, late_updates))
    grads.update(_final_grads(("w_in",), big, sums, {"w_in": in_got}))
    updates["w_in"] = _adamw(big["w_in"][0][0], grads["w_in"], big["w_in"][1][0], big["w_in"][2][0], "adamw_w_in")
    deltas, new_m, new_v = {}, {}, {}
    for n in big:
        restore = (lambda a: jnp.swapaxes(a[None], 1, 2)) if n in TRANSPOSED else (lambda a: a[None])
        grads[n] = restore(grads[n])
        deltas[n], new_m[n], new_v[n] = (restore(a) for a in updates[n])

    small_names = list(small_w)
    pad_rows = SMALL_ROWS - len(small_names) - 1
    stack = lambda d: jnp.concatenate([_pad_row(d[n]) for n in small_names] + [jnp.zeros((pad_rows + 1, D_MODEL), F32)], axis=0)
    g_pack = jnp.concatenate([_pad_row(small_g[n]) for n in small_names] + [_pad_row(loss_part[0:1, 0:1])]
                             + [jnp.zeros((pad_rows, D_MODEL), F32)], axis=0)
    g_tot = _all_reduce_small(g_pack)
    d_s, m_s, v_s = _adamw(stack(small_w), g_tot, stack(m_small), stack(v_small), "adamw_small")
    for i, n in enumerate(small_names):
        shape = small_w[n].shape
        size = int(np.prod(shape))
        grads[n] = g_tot[i, :size].reshape(shape)
        deltas[n], new_m[n], new_v[n] = d_s[i, :size].reshape(shape), m_s[i, :size].reshape(shape), v_s[i, :size].reshape(shape)
    loss = g_tot[len(small_names), 0]

    order = ["g_mix", "w_in", "b_forget", "g_ret_out", "g_fox_q", "g_fox_k", "w_out", "g_xattn", "w_xq", "w_xkv", "g_mem", "g_xq",
             "g_xk", "w_xo", "g_ffn", "w_gate", "w_up", "w_down"]
    return (loss, grad_x[None], *[grads[n] for n in order], *[deltas[n] for n in order], *[new_m[n] for n in order],
            *[new_v[n] for n in order])
```

```python
import functools

import numpy as np
import jax
import jax.numpy as jnp
from jax import lax
from jax.experimental import pallas as pl
from jax.experimental.pallas import tpu as pltpu

F32 = jnp.float32
BF = jnp.bfloat16

D_MODEL = 1024
HEAD_DIM = 64
N_HEADS = 8
GROUP_W = 512
N_XH = 4
XHD = 256
D_FF = 2816
MAIN_W = 3584
IN_W = 3592
ROPE_BASE = 10000.0
LOG2E = 1.4426950408889634
LN2 = 0.6931471805599453
EPS = 1e-6
NEG = -1e30
LANES = 128
RET_BLOCK = 256
REF_CHUNK = 64
ROW_TILE = 256
ATT_BLOCK = 256
TN_MAX_ROWS = 1408
SMALL_ROWS = 16
COL_SHARDED = ("w_xkv",)
TRANSPOSED = ("w_in", "w_gate", "w_up")
SHARD_ROW_ALIGN = 32
SHARD_ROW_PAD = 256
LATE = ("w_out", "w_xq", "w_xkv", "w_xo", "w_gate", "w_up", "w_down")
VMEM_LIMIT = 56 * 1024 * 1024

ADAM_LR = 0.001
ADAM_B1 = 0.9
ADAM_B2 = 0.999
ADAM_EPS = 1e-08
ADAM_WD = 0.01
ADAM_STEP = 10
ADAM_STEPS = 8

MESH = pl.DeviceIdType.MESH
ANY = pl.BlockSpec(memory_space=pl.ANY)
VMEM_SPEC = pl.BlockSpec(memory_space=pltpu.VMEM)


def _cparams(sem=None, vmem=VMEM_LIMIT):
    return pltpu.CompilerParams(dimension_semantics=sem, vmem_limit_bytes=vmem)


def _dot(a, b):
    return jnp.dot(a.astype(BF), b.astype(BF), preferred_element_type=F32)


def _dot_nt(a, b):
    return lax.dot_general(a.astype(BF), b.astype(BF), (((1,), (1,)), ((), ())), preferred_element_type=F32)


def _dot_tn(a, b):
    return lax.dot_general(a.astype(BF), b.astype(BF), (((0,), (0,)), ((), ())), preferred_element_type=F32)


def _split3(x):
    hi = x.astype(BF)
    r = x - hi.astype(F32)
    mid = r.astype(BF)
    lo = (r - mid.astype(F32)).astype(BF)
    return hi, mid, lo


def _dot_exact(ind, x):
    hi, mid, lo = _split3(x)
    return (jnp.dot(ind, lo, preferred_element_type=F32) + jnp.dot(ind, mid, preferred_element_type=F32)
            + jnp.dot(ind, hi, preferred_element_type=F32))


def _dot_nt_exact(ind, x):
    hi, mid, lo = _split3(x)
    dn = (((1,), (1,)), ((), ()))
    return (lax.dot_general(ind, lo, dn, preferred_element_type=F32) + lax.dot_general(ind, mid, dn, preferred_element_type=F32)
            + lax.dot_general(ind, hi, dn, preferred_element_type=F32))


def _sigmoid(x):
    return 1.0 / (1.0 + jnp.exp(-x))


def _rms_fwd(x, g):
    r = lax.rsqrt(jnp.mean(x * x, axis=-1, keepdims=True) + EPS)
    return x * r * g


def _rms_bwd(x, g, dy):
    r = lax.rsqrt(jnp.mean(x * x, axis=-1, keepdims=True) + EPS)
    xh = x * r
    dg = jnp.sum(dy * xh, axis=0, keepdims=True)
    dxh = dy * g
    dx = r * (dxh - xh * jnp.mean(dxh * xh, axis=-1, keepdims=True))
    return dx, dg


def _group_mean64(x):
    lane = lax.broadcasted_iota(jnp.int32, x.shape, 1)
    lo = lane < HEAD_DIM
    s_lo = jnp.sum(jnp.where(lo, x, 0.0), axis=-1, keepdims=True)
    s_hi = jnp.sum(jnp.where(lo, 0.0, x), axis=-1, keepdims=True)
    return jnp.where(lo, s_lo, s_hi) * (1.0 / HEAD_DIM)


def _swap32(x):
    lane = lax.broadcasted_iota(jnp.int32, x.shape, 1)
    first = (lane % HEAD_DIM) < (HEAD_DIM // 2)
    return jnp.where(first, pltpu.roll(x, LANES - HEAD_DIM // 2, axis=1), pltpu.roll(x, HEAD_DIM // 2, axis=1))


def _chunks(w):
    return [slice(j * LANES, (j + 1) * LANES) for j in range(w // LANES)]


def _aug_pair(qk, f_cols, is_query):
    lane = lax.broadcasted_iota(jnp.int32, qk.shape, 1)
    a = lane - HEAD_DIM
    values = (qk, pltpu.roll(qk, HEAD_DIM, axis=1))
    out = []
    for hh in range(2):
        hi, mid, lo = (p.astype(F32) for p in _split3(f_cols[hh] * LOG2E))
        if is_query:
            aux = jnp.where(a == 0, hi, jnp.where(a == 1, mid, jnp.where(a == 2, lo, jnp.where(a < 6, 1.0, 0.0))))
        else:
            aux = jnp.where(a < 3, 1.0, jnp.where(a == 3, -hi, jnp.where(a == 4, -mid, jnp.where(a == 5, -lo, 0.0))))
        out.append(jnp.where(a < 0, values[hh], aux))
    return jnp.concatenate(out, axis=-1).astype(BF)


def _mem_kv_fwd(mem, g_mem, w_xkv, g_xk):
    m_tok = mem.shape[0]

    def body(mem_ref, gm_ref, w_ref, gk_ref, memn_ref, kraw_ref, kn_ref, v_ref):
        mn = _rms_fwd(mem_ref[...], gm_ref[...]).astype(BF)
        memn_ref[...] = mn
        kv = jnp.dot(mn, w_ref[...], preferred_element_type=F32)
        k = kv[:, :D_MODEL]
        kraw_ref[...] = k
        v_ref[...] = kv[:, D_MODEL:].astype(BF)
        for h in range(N_XH):
            sl = slice(h * XHD, (h + 1) * XHD)
            kn_ref[:, sl] = _rms_fwd(k[:, sl], gk_ref[...]).astype(BF)

    return pl.pallas_call(
        body, name="mem_kv_fwd",
        out_shape=(jax.ShapeDtypeStruct((m_tok, D_MODEL), BF), jax.ShapeDtypeStruct((m_tok, D_MODEL), F32),
                   jax.ShapeDtypeStruct((m_tok, D_MODEL), BF), jax.ShapeDtypeStruct((m_tok, D_MODEL), BF)),
        in_specs=[VMEM_SPEC] * 4, out_specs=(VMEM_SPEC,) * 4, compiler_params=_cparams(),
    )(mem, g_mem, w_xkv, g_xk)


def _in_proj_fwd(x, g_mix, w_in_t, b_pad, cos_t, sin_t, gq_t, gk_t):
    t_len = x.shape[0]
    tm = min(ROW_TILE, t_len)
    n_t = t_len // tm

    def body(x_ref, g_ref, wm_ref, wf_ref, b_ref, cos_ref, sin_ref, gq_ref, gk_ref,
             n1_ref, proj_ref, rq_ref, rk_ref, qa_ref, ka_ref, z_ref, carry):
        i = pl.program_id(0)

        @pl.when(i == 0)
        def _():
            carry[...] = jnp.zeros_like(carry)

        n1 = _rms_fwd(x_ref[...], g_ref[...]).astype(BF)
        n1_ref[...] = n1
        z = _dot_nt(n1, wf_ref[...]) + b_ref[...]
        z_ref[...] = z
        lane = lax.broadcasted_iota(jnp.int32, z.shape, 1)
        lf = jnp.where(lane < N_HEADS, jnp.minimum(z, 0.0) - jnp.log(1.0 + jnp.exp(-jnp.abs(z))), 0.0)
        row = lax.broadcasted_iota(jnp.int32, (tm, tm), 0)
        col = lax.broadcasted_iota(jnp.int32, (tm, tm), 1)
        tri = (row >= col).astype(BF)
        fc = _dot_exact(tri, lf) + carry[0:1, :]
        carry[...] = jnp.broadcast_to(fc[tm - 1:tm, :], carry.shape)
        c, s = cos_ref[...], sin_ref[...]

        def section(n):
            p = _dot_nt(n1, wm_ref[n * GROUP_W:(n + 1) * GROUP_W, :])
            proj_ref[:, n * GROUP_W:(n + 1) * GROUP_W] = p.astype(BF)
            return p

        def rotate(p, out_ref, scale):
            for sl in _chunks(GROUP_W):
                out_ref[:, sl] = ((p[:, sl] * c + _swap32(p[:, sl]) * s) * scale).astype(BF)

        def norm_aug(p, gain, out_ref, scale, is_query):
            for j, sl in enumerate(_chunks(GROUP_W)):
                f = p[:, sl]
                f = f * lax.rsqrt(_group_mean64(f * f) + EPS) * gain * scale
                out_ref[:, 2 * j * LANES:2 * (j + 1) * LANES] = _aug_pair(f, [fc[:, 2 * j:2 * j + 1], fc[:, 2 * j + 1:2 * j + 2]], is_query)

        p_rq, p_rk = section(0), section(1)
        rotate(p_rq, rq_ref, 0.125)
        section(2)
        rotate(p_rk, rk_ref, 1.0)
        section(3)
        p_fq = section(4)
        p_fk = section(5)
        norm_aug(p_fq, gq_ref[...], qa_ref, 0.125 * LOG2E, True)
        section(6)
        norm_aug(p_fk, gk_ref[...], ka_ref, 1.0, False)

    row_spec = lambda w: pl.BlockSpec((tm, w), lambda i: (i, 0))
    full = lambda a: pl.BlockSpec(a.shape, lambda i: (0,) * a.ndim)
    return pl.pallas_call(
        body, name="in_proj_fwd", grid=(n_t,),
        out_shape=(jax.ShapeDtypeStruct((t_len, D_MODEL), BF), jax.ShapeDtypeStruct((t_len, MAIN_W), BF),
                   jax.ShapeDtypeStruct((t_len, GROUP_W), BF), jax.ShapeDtypeStruct((t_len, GROUP_W), BF),
                   jax.ShapeDtypeStruct((t_len, 2 * GROUP_W), BF), jax.ShapeDtypeStruct((t_len, 2 * GROUP_W), BF),
                   jax.ShapeDtypeStruct((t_len, LANES), F32)),
        in_specs=[row_spec(D_MODEL), full(g_mix), *_w_in_specs(), full(b_pad), row_spec(LANES), row_spec(LANES),
                  full(gq_t), full(gk_t)],
        out_specs=(row_spec(D_MODEL), row_spec(MAIN_W), row_spec(GROUP_W), row_spec(GROUP_W), row_spec(2 * GROUP_W),
                   row_spec(2 * GROUP_W), row_spec(LANES)),
        scratch_shapes=[pltpu.VMEM((8, LANES), F32)],
        compiler_params=_cparams(("arbitrary",)),
    )(x, g_mix, w_in_t, w_in_t, b_pad, cos_t, sin_t, gq_t, gk_t)


def _w_in_specs():
    return (pl.BlockSpec((MAIN_W, D_MODEL), lambda i: (0, 0)), pl.BlockSpec((LANES, D_MODEL), lambda i: (MAIN_W // LANES, 0)))


def _decay_tables(c):
    h = np.arange(N_HEADS, dtype=np.float64)
    lg = np.log(1.0 - 2.0 ** (-5.0 - h)).astype(np.float32).astype(np.float64)
    t = np.arange(c)
    same_or_earlier = (t[None, :] // REF_CHUNK) <= (t[:, None] // REF_CHUNK)
    w = np.where(same_or_earlier[None], np.exp(lg[:, None, None] * np.abs(t[:, None] - t[None, :])[None]), 0.0)
    qd = np.exp(lg[:, None] * (t[None, :] + 1.0))
    kd = np.exp(lg[:, None] * (c - 1.0 - t[None, :]))
    cd = np.exp(lg * c)
    ones = np.ones((1, 1, HEAD_DIM))
    return (jnp.asarray(w, F32), jnp.asarray(qd[:, :, None] * ones, F32), jnp.asarray(kd[:, :, None] * ones, F32),
            jnp.asarray(cd[:, None, None] * np.ones((1, HEAD_DIM, HEAD_DIM)), F32))


def _retention_fwd(rq, rk, proj, g_ret, tables):
    t_len = rq.shape[0]
    c = min(RET_BLOCK, t_len)
    n_b = t_len // c
    wdec, qdec, kdec, cdec = tables
    v_col, g_col = 2 * GROUP_W // LANES, 3 * GROUP_W // LANES

    def body(q_ref, k_ref, v_ref, rg_ref, g_ref, w_ref, qd_ref, kd_ref, cd_ref, raw_ref, mix_ref, st_ref, state):
        i = pl.program_id(1)

        @pl.when(i == 0)
        def _():
            state[...] = jnp.zeros_like(state)

        q2, k2, v2 = q_ref[...], k_ref[...], v_ref[...]
        outs = []
        for hh in range(2):
            sl = slice(hh * HEAD_DIM, (hh + 1) * HEAD_DIM)
            q, k, v = q2[:, sl], k2[:, sl], v2[:, sl]
            sp = state[hh]
            st_ref[0, 0, hh] = sp
            a = _dot_nt(q, k) * w_ref[hh]
            o = _dot(a, v) + _dot(q.astype(F32) * qd_ref[hh], sp)
            state[hh] = sp * cd_ref[hh] + _dot_tn(k.astype(F32) * kd_ref[hh], v)
            outs.append(o)
        o2 = jnp.concatenate(outs, axis=-1)
        raw_ref[...] = o2
        xc = o2 - _group_mean64(o2)
        xh = xc * lax.rsqrt(_group_mean64(xc * xc) + EPS)
        gate = rg_ref[...].astype(F32)
        mix_ref[...] = (gate * _sigmoid(gate) * (xh * g_ref[0])).astype(BF)

    blk = lambda col0: pl.BlockSpec((c, LANES), lambda hp, i: (i, col0 + hp))
    tab = lambda a: pl.BlockSpec((2,) + a.shape[1:], lambda hp, i: (hp, 0, 0))
    return pl.pallas_call(
        body, name="retention_fwd", grid=(N_HEADS // 2, n_b),
        out_shape=(jax.ShapeDtypeStruct((t_len, GROUP_W), F32), jax.ShapeDtypeStruct((t_len, GROUP_W), BF),
                   jax.ShapeDtypeStruct((N_HEADS // 2, n_b, 2, HEAD_DIM, HEAD_DIM), F32)),
        in_specs=[blk(0), blk(0), blk(v_col), blk(g_col), pl.BlockSpec((1, 1, LANES), lambda hp, i: (hp, 0, 0)),
                  tab(wdec), tab(qdec), tab(kdec), tab(cdec)],
        out_specs=(blk(0), blk(0), pl.BlockSpec((1, 1, 2, HEAD_DIM, HEAD_DIM), lambda hp, i: (hp, i, 0, 0, 0))),
        scratch_shapes=[pltpu.VMEM((2, HEAD_DIM, HEAD_DIM), F32)],
        compiler_params=_cparams(("arbitrary", "arbitrary")),
    )(rq, rk, proj, proj, g_ret, wdec, qdec, kdec, cdec)


def _fox_fwd(q_aug, k_aug, proj, shards):
    t_len = q_aug.shape[0]
    tq = min(ATT_BLOCK, t_len)
    nsub = 2 if t_len >= 2 * tq else 1
    tg = nsub * tq
    n_q = t_len // tg
    v_col = 6 * GROUP_W // LANES
    tc = min(512, t_len)
    n_w = len(shards)
    n_steps = (N_HEADS // 2) * n_q

    def body(*refs):
        q_ref, k_ref, v_ref = refs[:3]
        o_ref, o32_ref, lse_ref = refs[3 + n_w:6 + n_w]
        vt = refs[6 + 2 * n_w]
        comm = (refs[3:3 + n_w], refs[6 + n_w:6 + 2 * n_w]) + tuple(refs[7 + 2 * n_w:])
        i = pl.program_id(1)
        step = pl.program_id(0) * n_q + i

        @pl.when(step == 0)
        def _():
            _gather_phase(0, *comm)

        @pl.when(step == (3 * n_steps) // 4)
        def _():
            _gather_phase(1, *comm)

        @pl.when(i == 0)
        def _():
            for c0 in range(0, t_len, tc):
                vt[:, c0:c0 + tc] = v_ref[c0:c0 + tc, :].T

        chains = [(u, hh) for u in range(nsub) for hh in range(2)]
        qs = {(u, hh): q_ref[u * tq:(u + 1) * tq, hh * LANES:(hh + 1) * LANES] for u, hh in chains}
        ones = jnp.ones((HEAD_DIM, tq), BF)

        def scores(j, which):
            k2 = k_ref[pl.ds(pl.multiple_of(j * tq, tq), tq), :]
            return {ch: _dot_nt(k2[:, ch[1] * LANES:(ch[1] + 1) * LANES], qs[ch]) for ch in which}

        def update(j, ss, carry, masked):
            v2 = vt[:, pl.ds(pl.multiple_of(j * tq, tq), tq)]
            ps, stats = {}, {}
            for ch in ss:
                m = carry[ch][0]
                s_t = ss[ch]
                if ch in masked:
                    krow = lax.broadcasted_iota(jnp.int32, (tq, tq), 0)
                    qcol = lax.broadcasted_iota(jnp.int32, (tq, tq), 1)
                    s_t = jnp.where(qcol >= krow, s_t, NEG)
                m_new = jnp.maximum(m, jnp.max(s_t, axis=0, keepdims=True))
                ps[ch] = jnp.exp2(s_t - m_new).astype(BF)
                stats[ch] = (m_new, jnp.exp2(m - m_new))
            out = dict(carry)
            for ch in ss:
                m_new, alpha = stats[ch]
                v_aug = jnp.concatenate([v2[ch[1] * HEAD_DIM:(ch[1] + 1) * HEAD_DIM, :], ones], axis=0)
                out[ch] = (m_new, carry[ch][1] * alpha + jnp.dot(v_aug, ps[ch], preferred_element_type=F32))
            return out

        def advance(j, state):
            ss, carry = state
            return scores(j + 1, chains), update(j, ss, carry, ())

        init = {ch: (jnp.full((1, tq), NEG, F32), jnp.zeros((LANES, tq), F32)) for ch in chains}
        first = nsub * i
        ss, carry = lax.fori_loop(0, first, advance, (scores(0, chains), init))
        carry = update(first, ss, carry, [(0, 0), (0, 1)])
        if nsub == 2:
            last = [(1, 0), (1, 1)]
            carry = update(first + 1, scores(first + 1, last), carry, last)
        for u in range(nsub):
            outs, lses = [], []
            for hh in range(2):
                m, acc = carry[u, hh]
                l = acc[HEAD_DIM:HEAD_DIM + 1, :]
                outs.append(acc[:HEAD_DIM, :] / l)
                lses.append(m + jnp.log2(l))
            o2 = jnp.concatenate(outs, axis=0).T
            o32_ref[u * tq:(u + 1) * tq, :] = o2
            o_ref[u * tq:(u + 1) * tq, :] = o2.astype(BF)
            lse_ref[0, :, u * tq:(u + 1) * tq] = jnp.concatenate(lses, axis=0)

        @pl.when(step == n_steps - 1)
        def _():
            _gather_phase(2, *comm)

    return pl.pallas_call(
        body, name="fox_fwd", grid=(N_HEADS // 2, n_q),
        out_shape=(jax.ShapeDtypeStruct((t_len, GROUP_W), BF), jax.ShapeDtypeStruct((t_len, GROUP_W), F32),
                   jax.ShapeDtypeStruct((N_HEADS // 2, 2, t_len), F32))
        + tuple(jax.ShapeDtypeStruct((4,) + s.shape, s.dtype) for s in shards),
        in_specs=[pl.BlockSpec((tg, 2 * LANES), lambda hp, i: (i, hp)),
                  pl.BlockSpec((t_len, 2 * LANES), lambda hp, i: (0, hp)),
                  pl.BlockSpec((t_len, LANES), lambda hp, i: (0, v_col + hp))] + [ANY] * n_w,
        out_specs=(pl.BlockSpec((tg, LANES), lambda hp, i: (i, hp)), pl.BlockSpec((tg, LANES), lambda hp, i: (i, hp)),
                   pl.BlockSpec((1, 2, tg), lambda hp, i: (hp, 0, i))) + (ANY,) * n_w,
        scratch_shapes=[pltpu.VMEM((LANES, t_len), BF)] + _gather_scratch(n_w),
        compiler_params=_cparams(("arbitrary", "arbitrary")),
    )(q_aug, k_aug, proj, *shards)


def _softmax_rows(s):
    p = jnp.exp(s - jnp.max(s, axis=-1, keepdims=True))
    return p / jnp.sum(p, axis=-1, keepdims=True)


def _attn_out_xattn_fwd(x, mix_r, mix_f, w_out, g_xattn, w_xq, g_xq, kn, v, w_xo):
    t_len = x.shape[0]
    tm = min(ROW_TILE, t_len)

    def body(x_ref, mr_ref, mf_ref, wo_ref, g_ref, wq_ref, gq_ref, kn_ref, v_ref, wxo_ref,
             h1_ref, hn_ref, qx_ref, o_ref, h2_ref):
        h1 = x_ref[...] + jnp.dot(mr_ref[...], wo_ref[:GROUP_W, :], preferred_element_type=F32) \
            + jnp.dot(mf_ref[...], wo_ref[GROUP_W:, :], preferred_element_type=F32)
        h1_ref[...] = h1
        hn = _rms_fwd(h1, g_ref[...]).astype(BF)
        hn_ref[...] = hn
        qx = jnp.dot(hn, wq_ref[...], preferred_element_type=F32).astype(BF)
        qx_ref[...] = qx
        sls = [slice(h * XHD, (h + 1) * XHD) for h in range(N_XH)]
        qns = [_rms_fwd(qx[:, sl].astype(F32), gq_ref[...]).astype(BF) for sl in sls]
        logits = [_dot_nt(qn, kn_ref[:, sl]) * (XHD ** -0.5) for qn, sl in zip(qns, sls)]
        ps = [_softmax_rows(s).astype(BF) for s in logits]
        for p, sl in zip(ps, sls):
            o_ref[:, sl] = jnp.dot(p, v_ref[:, sl], preferred_element_type=F32).astype(BF)
        h2_ref[...] = h1 + jnp.dot(o_ref[...], wxo_ref[...], preferred_element_type=F32)

    row_spec = lambda w: pl.BlockSpec((tm, w), lambda i: (i, 0))
    full = lambda a: pl.BlockSpec(a.shape, lambda i: (0,) * a.ndim)
    return pl.pallas_call(
        body, name="attn_out_xattn_fwd", grid=(t_len // tm,),
        out_shape=(jax.ShapeDtypeStruct((t_len, D_MODEL), F32), jax.ShapeDtypeStruct((t_len, D_MODEL), BF),
                   jax.ShapeDtypeStruct((t_len, D_MODEL), BF), jax.ShapeDtypeStruct((t_len, D_MODEL), BF),
                   jax.ShapeDtypeStruct((t_len, D_MODEL), F32)),
        in_specs=[row_spec(D_MODEL), row_spec(GROUP_W), row_spec(GROUP_W), full(w_out), full(g_xattn), full(w_xq), full(g_xq),
                  full(kn), full(v), full(w_xo)],
        out_specs=(row_spec(D_MODEL),) * 5,
        compiler_params=_cparams(("arbitrary",)),
    )(x, mix_r, mix_f, w_out, g_xattn, w_xq, g_xq, kn, v, w_xo)


def _ffn_loss_fwd(h2, g_ffn, w_gate, w_up, w_down, target):
    t_len = h2.shape[0]
    tm = min(ROW_TILE, t_len)

    def body(h2_ref, g_ref, wg_ref, wu_ref, wd_ref, tgt_ref, hn_ref, gate_ref, up_ref, act_ref, dh3_ref, loss_ref):
        @pl.when(pl.program_id(0) == 0)
        def _():
            loss_ref[...] = jnp.zeros_like(loss_ref)

        h2v = h2_ref[...]
        hn = _rms_fwd(h2v, g_ref[...]).astype(BF)
        hn_ref[...] = hn
        gate = _dot_nt(hn, wg_ref[...])
        up = _dot_nt(hn, wu_ref[...])
        gate_ref[...] = gate.astype(BF)
        up_ref[...] = up.astype(BF)
        act = (gate * _sigmoid(gate) * up).astype(BF)
        act_ref[...] = act
        diff = h2v + jnp.dot(act, wd_ref[...], preferred_element_type=F32) - tgt_ref[...]
        dh3_ref[...] = diff * (1.0 / D_MODEL)
        per_row = jnp.sum(diff * diff, axis=-1, keepdims=True) * (1.0 / D_MODEL)
        loss_ref[...] += 0.5 * jnp.sum(per_row, axis=0, keepdims=True)

    row_spec = lambda w: pl.BlockSpec((tm, w), lambda i: (i, 0))
    full = lambda a: pl.BlockSpec(a.shape, lambda i: (0,) * a.ndim, pipeline_mode=pl.Buffered(1))
    return pl.pallas_call(
        body, name="ffn_loss_fwd", grid=(t_len // tm,),
        out_shape=(jax.ShapeDtypeStruct((t_len, D_MODEL), BF), jax.ShapeDtypeStruct((t_len, D_FF), BF),
                   jax.ShapeDtypeStruct((t_len, D_FF), BF), jax.ShapeDtypeStruct((t_len, D_FF), BF),
                   jax.ShapeDtypeStruct((t_len, D_MODEL), F32), jax.ShapeDtypeStruct((8, LANES), F32)),
        in_specs=[row_spec(D_MODEL), full(g_ffn), full(w_gate), full(w_up), full(w_down), row_spec(D_MODEL)],
        out_specs=(row_spec(D_MODEL), row_spec(D_FF), row_spec(D_FF), row_spec(D_FF), row_spec(D_MODEL),
                   pl.BlockSpec((8, LANES), lambda i: (0, 0))),
        compiler_params=_cparams(("arbitrary",)),
    )(h2, g_ffn, w_gate, w_up, w_down, target)


def _ffn_bwd(dh3, gate, up, h2, g_ffn, w_gate, w_up, w_down):
    t_len = h2.shape[0]
    tm = min(ROW_TILE, t_len)

    def body(dh3_ref, gate_ref, up_ref, h2_ref, g_ref, wg_ref, wu_ref, wd_ref, dgate_ref, dup_ref, dh2_ref, dg_ref):
        @pl.when(pl.program_id(0) == 0)
        def _():
            dg_ref[...] = jnp.zeros_like(dg_ref)

        dh3v = dh3_ref[...]
        dact = _dot_nt(dh3v, wd_ref[...])
        g = gate_ref[...].astype(F32)
        sg = _sigmoid(g)
        dup = (dact * (g * sg)).astype(BF)
        dgate = (dact * up_ref[...].astype(F32) * (sg * (1.0 + g * (1.0 - sg)))).astype(BF)
        dup_ref[...] = dup
        dgate_ref[...] = dgate
        dhn = jnp.dot(dgate, wg_ref[...], preferred_element_type=F32) + jnp.dot(dup, wu_ref[...], preferred_element_type=F32)
        dx, dg = _rms_bwd(h2_ref[...], g_ref[...], dhn)
        dh2_ref[...] = dh3v + dx
        dg_ref[...] += dg

    row_spec = lambda w: pl.BlockSpec((tm, w), lambda i: (i, 0))
    full = lambda a: pl.BlockSpec(a.shape, lambda i: (0,) * a.ndim, pipeline_mode=pl.Buffered(1))
    return pl.pallas_call(
        body, name="ffn_bwd", grid=(t_len // tm,),
        out_shape=(jax.ShapeDtypeStruct((t_len, D_FF), BF), jax.ShapeDtypeStruct((t_len, D_FF), BF),
                   jax.ShapeDtypeStruct((t_len, D_MODEL), F32), jax.ShapeDtypeStruct((1, D_MODEL), F32)),
        in_specs=[row_spec(D_MODEL), row_spec(D_FF), row_spec(D_FF), row_spec(D_MODEL), full(g_ffn), full(w_gate), full(w_up),
                  full(w_down)],
        out_specs=(row_spec(D_FF), row_spec(D_FF), row_spec(D_MODEL), pl.BlockSpec((1, D_MODEL), lambda i: (0, 0))),
        compiler_params=_cparams(("arbitrary",)),
    )(dh3, gate, up, h2, g_ffn, w_gate, w_up, w_down)


def _attn_out_xattn_bwd(dh2, h1, qx, kn, v, w_xo, w_xq, w_out, g_xattn, g_xq):
    t_len = h1.shape[0]
    tm = min(ROW_TILE, t_len)
    m_tok = kn.shape[0]

    def body(dh2_ref, h1_ref, qx_ref, kn_ref, v_ref, wxo_ref, wq_ref, wo_ref, g_ref, gq_ref,
             dqx_ref, dh1_ref, dmr_ref, dmf_ref, dkn_ref, dv_ref, dg_ref, dgq_ref, dqx_scr):
        @pl.when(pl.program_id(0) == 0)
        def _():
            dkn_ref[...] = jnp.zeros_like(dkn_ref)
            dv_ref[...] = jnp.zeros_like(dv_ref)
            dg_ref[...] = jnp.zeros_like(dg_ref)
            dgq_ref[...] = jnp.zeros_like(dgq_ref)

        dh2v = dh2_ref[...]
        do = _dot_nt(dh2v, wxo_ref[...])
        gq = gq_ref[...]
        sls = [slice(h * XHD, (h + 1) * XHD) for h in range(N_XH)]
        qraws = [qx_ref[:, sl].astype(F32) for sl in sls]
        qns = [_rms_fwd(qraw, gq).astype(BF) for qraw in qraws]
        dohs = [do[:, sl].astype(BF) for sl in sls]
        logits = [_dot_nt(qn, kn_ref[:, sl]) * (XHD ** -0.5) for qn, sl in zip(qns, sls)]
        dps = [_dot_nt(doh, v_ref[:, sl]) for doh, sl in zip(dohs, sls)]
        ps = [_softmax_rows(s) for s in logits]
        dss = [(p * (dp - jnp.sum(dp * p, axis=-1, keepdims=True)) * (XHD ** -0.5)).astype(BF) for p, dp in zip(ps, dps)]
        dqns = []
        for h, sl in enumerate(sls):
            dv_ref[:, sl] += _dot_tn(ps[h], dohs[h])
            dqns.append(jnp.dot(dss[h], kn_ref[:, sl], preferred_element_type=F32))
            dkn_ref[:, sl] += _dot_tn(dss[h], qns[h])
        dgq = jnp.zeros((1, XHD), F32)
        for h, sl in enumerate(sls):
            dx, dg_h = _rms_bwd(qraws[h], gq, dqns[h])
            dgq = dgq + dg_h
            dqx_scr[:, sl] = dx.astype(BF)
        dgq_ref[...] += dgq
        dqx = dqx_scr[...]
        dqx_ref[...] = dqx
        dhn = _dot_nt(dqx, wq_ref[...])
        dx, dg = _rms_bwd(h1_ref[...], g_ref[...], dhn)
        dg_ref[...] += dg
        dh1 = dh2v + dx
        dh1_ref[...] = dh1
        dmix = _dot_nt(dh1, wo_ref[...])
        dmr_ref[...] = dmix[:, :GROUP_W]
        dmf_ref[...] = dmix[:, GROUP_W:].astype(BF)

    row_spec = lambda w: pl.BlockSpec((tm, w), lambda i: (i, 0))
    full = lambda a: pl.BlockSpec(a.shape, lambda i: (0,) * a.ndim)
    acc = lambda r, c: pl.BlockSpec((r, c), lambda i: (0, 0))
    return pl.pallas_call(
        body, name="attn_out_xattn_bwd", grid=(t_len // tm,),
        out_shape=(jax.ShapeDtypeStruct((t_len, D_MODEL), BF), jax.ShapeDtypeStruct((t_len, D_MODEL), F32),
                   jax.ShapeDtypeStruct((t_len, GROUP_W), F32), jax.ShapeDtypeStruct((t_len, GROUP_W), BF),
                   jax.ShapeDtypeStruct((m_tok, D_MODEL), F32), jax.ShapeDtypeStruct((m_tok, D_MODEL), F32),
                   jax.ShapeDtypeStruct((1, D_MODEL), F32), jax.ShapeDtypeStruct((1, XHD), F32)),
        in_specs=[row_spec(D_MODEL), row_spec(D_MODEL), row_spec(D_MODEL), full(kn), full(v), full(w_xo), full(w_xq), full(w_out),
                  full(g_xattn), full(g_xq)],
        out_specs=(row_spec(D_MODEL), row_spec(D_MODEL), row_spec(GROUP_W), row_spec(GROUP_W), acc(m_tok, D_MODEL),
                   acc(m_tok, D_MODEL), acc(1, D_MODEL), acc(1, XHD)),
        scratch_shapes=[pltpu.VMEM((tm, D_MODEL), BF)],
        compiler_params=_cparams(("arbitrary",)),
    )(dh2, h1, qx, kn, v, w_xo, w_xq, w_out, g_xattn, g_xq)


def _mem_kv_bwd(dkn, dv, kraw, mem, memn, g_mem, g_xk, w_xkv):
    m_tok = mem.shape[0]

    def body(dkn_ref, dv_ref, kraw_ref, mem_ref, memn_ref, gm_ref, gk_ref, w_ref, dw_ref, dgm_ref, dgk_ref, dkv_scr):
        gk = gk_ref[...]
        dgk = jnp.zeros((1, XHD), F32)
        for h in range(N_XH):
            sl = slice(h * XHD, (h + 1) * XHD)
            dx, dg_h = _rms_bwd(kraw_ref[:, sl], gk, dkn_ref[:, sl])
            dgk = dgk + dg_h
            dkv_scr[:, sl] = dx.astype(BF)
        dgk_ref[...] = dgk
        dkv_scr[:, D_MODEL:] = dv_ref[...].astype(BF)
        dkv = dkv_scr[...]
        dw_ref[...] = _dot_tn(memn_ref[...], dkv)
        dmemn = _dot_nt(dkv, w_ref[...])
        mem_v = mem_ref[...]
        r = lax.rsqrt(jnp.mean(mem_v * mem_v, axis=-1, keepdims=True) + EPS)
        dgm_ref[...] = jnp.sum(dmemn * mem_v * r, axis=0, keepdims=True)

    return pl.pallas_call(
        body, name="mem_kv_bwd",
        out_shape=(jax.ShapeDtypeStruct((D_MODEL, 2 * D_MODEL), F32), jax.ShapeDtypeStruct((1, D_MODEL), F32),
                   jax.ShapeDtypeStruct((1, XHD), F32)),
        in_specs=[VMEM_SPEC] * 8, out_specs=(VMEM_SPEC,) * 3,
        scratch_shapes=[pltpu.VMEM((m_tok, 2 * D_MODEL), BF)],
        compiler_params=_cparams(),
    )(dkn, dv, kraw, mem, memn, g_mem, g_xk, w_xkv)


def _fox_bwd(q_aug, k_aug, proj, dmf, o32, lse, sums):
    t_len = q_aug.shape[0]
    tb = min(ATT_BLOCK, t_len)
    n_b = t_len // tb
    v_col = 6 * GROUP_W // LANES
    n_w = len(sums)
    n_steps = (N_HEADS // 2) * n_b

    def body(*refs):
        k_ref, v_ref, q_ref, do_ref, o_ref, lse_ref = refs[:6]
        dq_ref, dk_ref, dv_ref, df_ref = refs[6 + n_w:10 + n_w]
        delta = refs[10 + 2 * n_w]
        comm = (refs[6:6 + n_w], refs[10 + n_w:10 + 2 * n_w]) + tuple(refs[11 + 2 * n_w:])
        j = pl.program_id(1)
        step = pl.program_id(0) * n_b + j

        @pl.when(step == 0)
        def _():
            _scatter_phase(0, *comm)

        @pl.when(j == 0)
        def _():
            dq_ref[...] = jnp.zeros_like(dq_ref)
            dd = do_ref[...].astype(F32) * o_ref[...]
            hrow = lax.broadcasted_iota(jnp.int32, (8, LANES), 0)
            lane = lax.broadcasted_iota(jnp.int32, (8, LANES), 1)
            ind = ((lane // HEAD_DIM) == hrow).astype(BF)
            delta[...] = _dot_nt_exact(ind, dd)

        k2, v2 = k_ref[...], v_ref[...]
        ks = [k2[:, hh * LANES:(hh + 1) * LANES] for hh in range(2)]
        vs = [v2[:, hh * HEAD_DIM:(hh + 1) * HEAD_DIM] for hh in range(2)]

        def blocks(idx, carry, masked, valid):
            loaded = []
            for i in idx:
                rows = pl.ds(pl.multiple_of(i * tb, tb), tb)
                q2 = q_ref[rows, :]
                do2 = do_ref[rows, :]
                loaded.append((rows, [q2[:, hh * LANES:(hh + 1) * LANES] for hh in range(2)],
                               [do2[:, hh * HEAD_DIM:(hh + 1) * HEAD_DIM] for hh in range(2)]))
            ss = [[_dot_nt(ks[hh], qs[hh]) for hh in range(2)] for _, qs, _ in loaded]
            dps = [[_dot_nt(vs[hh], dos[hh]) for hh in range(2)] for _, _, dos in loaded]
            pts, dsts, dfs = [], [], []
            for b, (rows, _, _) in enumerate(loaded):
                for hh in range(2):
                    s_t = ss[b][hh]
                    if masked[b]:
                        krow = lax.broadcasted_iota(jnp.int32, (tb, tb), 0)
                        qcol = lax.broadcasted_iota(jnp.int32, (tb, tb), 1)
                        s_t = jnp.where(qcol >= krow, s_t, NEG)
                    lse_row = lse_ref[0, hh:hh + 1, rows]
                    if valid[b] is not None:
                        lse_row = jnp.where(valid[b], lse_row, -NEG)
                    p_t = jnp.exp2(s_t - lse_row)
                    pts.append(p_t.astype(BF))
                    ds_t = p_t * (dps[b][hh] - delta[hh:hh + 1, rows])
                    dsts.append(ds_t.astype(BF))
                    dfs.append(jnp.sum(ds_t, axis=-1, keepdims=True))
            out = list(carry)
            for b, (rows, qs, dos) in enumerate(loaded):
                for hh in range(2):
                    dk, dv, df = out[hh]
                    dv = dv + jnp.dot(pts[2 * b + hh], dos[hh], preferred_element_type=F32)
                    dk = dk + jnp.dot(dsts[2 * b + hh], qs[hh], preferred_element_type=F32)
                    dq_ref[rows, hh * HEAD_DIM:(hh + 1) * HEAD_DIM] += _dot_tn(dsts[2 * b + hh], ks[hh])[:, :HEAD_DIM]
                    out[hh] = (dk, dv, df - dfs[2 * b + hh])
            return tuple(out)

        def pair(t, carry):
            i0 = j + 1 + 2 * t
            return blocks([i0, jnp.minimum(i0 + 1, n_b - 1)], carry, [False, False], [None, i0 + 1 < n_b])

        init = tuple((jnp.zeros((tb, LANES), F32), jnp.zeros((tb, HEAD_DIM), F32), jnp.zeros((tb, 1), F32)) for _ in range(2))
        carry = blocks([j], init, [True], [None])
        carry = lax.fori_loop(0, (n_b - j) // 2, pair, carry)
        dk_ref[...] = jnp.concatenate([carry[hh][0][:, :HEAD_DIM] for hh in range(2)], axis=-1) * LN2
        dv_ref[...] = jnp.concatenate([carry[hh][1] for hh in range(2)], axis=-1)
        df_ref[0] = jnp.concatenate([carry[hh][2] for hh in range(2)], axis=-1)

        @pl.when(step == n_steps - 1)
        def _():
            _scatter_phase(1, *comm)

    blk = lambda w, col0: pl.BlockSpec((tb, w), lambda hp, j: (j, col0 + hp))
    whole = lambda w: pl.BlockSpec((t_len, w), lambda hp, j: (0, hp))
    rows2 = pl.BlockSpec((1, 2, t_len), lambda hp, j: (hp, 0, 0))
    cols2 = pl.BlockSpec((1, tb, 2), lambda hp, j: (hp, j, 0))
    return pl.pallas_call(
        body, name="fox_bwd", grid=(N_HEADS // 2, n_b),
        out_shape=(jax.ShapeDtypeStruct((t_len, GROUP_W), F32), jax.ShapeDtypeStruct((t_len, GROUP_W), F32),
                   jax.ShapeDtypeStruct((t_len, GROUP_W), F32), jax.ShapeDtypeStruct((N_HEADS // 2, t_len, 2), F32))
        + _scatter_out_shapes(sums),
        in_specs=[blk(2 * LANES, 0), blk(LANES, v_col), whole(2 * LANES), whole(LANES), whole(LANES), rows2] + [ANY] * n_w,
        out_specs=(whole(LANES), blk(LANES, 0), blk(LANES, 0), cols2) + (ANY,) * n_w,
        scratch_shapes=[pltpu.VMEM((8, t_len), F32)] + _scatter_scratch(n_w),
        compiler_params=_cparams(("arbitrary", "arbitrary")),
    )(k_aug, proj, q_aug, dmf, o32, lse, *sums)


def _retention_bwd(dmr, raw, proj, g_ret, rq, rk, states, tables, parts):
    t_len = rq.shape[0]
    c = min(RET_BLOCK, t_len)
    n_b = t_len // c
    wdec, qdec, kdec, cdec = tables
    v_col, g_col = 2 * GROUP_W // LANES, 3 * GROUP_W // LANES
    n_w = len(parts)
    n_steps = (N_HEADS // 2) * n_b

    def body(*refs):
        d_ref, raw_ref, rg_ref, g_ref, q_ref, k_ref, v_ref, st_ref, w_ref, wt_ref, qd_ref, kd_ref, cd_ref = refs[:13]
        dq_ref, dk_ref, dv_ref, drg_ref, dg_ref = refs[13 + n_w:18 + n_w]
        gstate = refs[18 + 2 * n_w]
        comm = (refs[13:13 + n_w], refs[18 + n_w:18 + 2 * n_w]) + tuple(refs[19 + 2 * n_w:])
        step = pl.program_id(0) * n_b + pl.program_id(1)

        @pl.when(step == 0)
        def _():
            _exchange_phase(0, *comm)

        @pl.when(pl.program_id(1) == 0)
        def _():
            gstate[...] = jnp.zeros_like(gstate)
            dg_ref[...] = jnp.zeros_like(dg_ref)

        d, raw_v, g = d_ref[...], raw_ref[...], g_ref[0]
        gate = rg_ref[...].astype(F32)
        xc = raw_v - _group_mean64(raw_v)
        r = lax.rsqrt(_group_mean64(xc * xc) + EPS)
        xh = xc * r
        sg = _sigmoid(gate)
        drg_ref[...] = d * (xh * g) * (sg * (1.0 + gate * (1.0 - sg)))
        dy = d * (gate * sg)
        dg_ref[0] += jnp.sum(dy * xh, axis=0, keepdims=True)
        dxh = dy * g
        do2 = r * (dxh - _group_mean64(dxh) - xh * _group_mean64(dxh * xh))
        q2, k2, v2 = q_ref[...], k_ref[...], v_ref[...]
        dqs, dks, dvs = [], [], []
        heads = [tuple(t[:, hh * HEAD_DIM:(hh + 1) * HEAD_DIM] for t in (q2, k2, v2, do2.astype(BF))) for hh in range(2)]
        firsts = [(_dot_nt(k, q) * wt_ref[hh], _dot_nt(do, v) * w_ref[hh], _dot_nt(v, do) * wt_ref[hh])
                  for hh, (q, k, v, do) in enumerate(heads)]
        for hh, (q, k, v, do) in enumerate(heads):
            a_t, dm, dm_t = firsts[hh]
            sp, gs = st_ref[0, 0, hh], gstate[hh]
            qd = q.astype(F32) * qd_ref[hh]
            kd = k.astype(F32) * kd_ref[hh]
            dqs.append(_dot(dm, k) + _dot_nt(do, sp) * qd_ref[hh])
            dks.append(_dot(dm_t, q) + _dot_nt(v, gs) * kd_ref[hh])
            dvs.append(_dot(a_t, do) + _dot(kd, gs))
            gstate[hh] = gs * cd_ref[hh] + _dot_tn(qd, do)
        dq_ref[...] = jnp.concatenate(dqs, axis=-1)
        dk_ref[...] = jnp.concatenate(dks, axis=-1)
        dv_ref[...] = jnp.concatenate(dvs, axis=-1)

        @pl.when(step == n_steps - 1)
        def _():
            _exchange_phase(1, *comm)

    blk = lambda col0: pl.BlockSpec((c, LANES), lambda hp, i: (n_b - 1 - i, col0 + hp))
    tab = lambda a: pl.BlockSpec((2,) + a.shape[1:], lambda hp, i: (hp, 0, 0))
    gspec = pl.BlockSpec((1, 1, LANES), lambda hp, i: (hp, 0, 0))
    return pl.pallas_call(
        body, name="retention_bwd", grid=(N_HEADS // 2, n_b),
        out_shape=(jax.ShapeDtypeStruct((t_len, GROUP_W), F32),) * 4 + (jax.ShapeDtypeStruct((N_HEADS // 2, 1, LANES), F32),)
        + _exchange_out_shapes(parts),
        in_specs=[blk(0), blk(0), blk(g_col), gspec, blk(0), blk(0), blk(v_col),
                  pl.BlockSpec((1, 1, 2, HEAD_DIM, HEAD_DIM), lambda hp, i: (hp, n_b - 1 - i, 0, 0, 0)),
                  tab(wdec), tab(wdec), tab(qdec), tab(kdec), tab(cdec)] + [ANY] * n_w,
        out_specs=(blk(0), blk(0), blk(0), blk(0), gspec) + (ANY,) * n_w,
        scratch_shapes=[pltpu.VMEM((2, HEAD_DIM, HEAD_DIM), F32)] + _exchange_scratch(n_w),
        compiler_params=_cparams(("arbitrary", "arbitrary")),
    )(dmr, raw, proj, g_ret, rq, rk, proj, states, wdec, jnp.transpose(wdec, (0, 2, 1)), qdec, kdec, cdec, *parts)


def _in_proj_bwd(x, g_mix, dh1, dq_r, dk_r, dv_r, drg, dq_f, dk_f, dv_f, df_col, proj, z, cos_t, sin_t, gq_t, gk_t, w_in_t):
    t_len = x.shape[0]
    tm = min(ROW_TILE, t_len)
    n_t = t_len // tm

    def body(x_ref, g_ref, dh1_ref, dqr_ref, dkr_ref, dvr_ref, drg_ref, dqf_ref, dkf_ref, dvf_ref, df_ref, fq_ref, fk_ref, z_ref,
             cos_ref, sin_ref, gq_ref, gk_ref, wm_ref, wf_ref,
             dproj_ref, dz_ref, dx_ref, dg_ref, dgq_ref, dgk_ref, db_ref, carry, gq_acc, gk_acc):
        i = pl.program_id(0)

        @pl.when(i == 0)
        def _():
            carry[...] = jnp.zeros_like(carry)
            gq_acc[...] = jnp.zeros_like(gq_acc)
            gk_acc[...] = jnp.zeros_like(gk_acc)
            dg_ref[...] = jnp.zeros_like(dg_ref)
            db_ref[...] = jnp.zeros_like(db_ref)

        c, s = cos_ref[...], sin_ref[...]
        gq, gk = gq_ref[...], gk_ref[...]
        dgq = jnp.zeros((1, LANES), F32)
        dgk = jnp.zeros((1, LANES), F32)
        for sl in _chunks(GROUP_W):
            dy = dqr_ref[:, sl] * 0.125
            dproj_ref[:, sl] = (dy * c + _swap32(dy * s)).astype(BF)
            dy = dkr_ref[:, sl]
            dproj_ref[:, GROUP_W + sl.start:GROUP_W + sl.stop] = (dy * c + _swap32(dy * s)).astype(BF)
            dproj_ref[:, 2 * GROUP_W + sl.start:2 * GROUP_W + sl.stop] = dvr_ref[:, sl].astype(BF)
            dproj_ref[:, 3 * GROUP_W + sl.start:3 * GROUP_W + sl.stop] = drg_ref[:, sl].astype(BF)
            for src, dsrc, gain, off in ((fq_ref, dqf_ref, gq, 4), (fk_ref, dkf_ref, gk, 5)):
                xr = src[:, sl].astype(F32)
                r = lax.rsqrt(_group_mean64(xr * xr) + EPS)
                xh = xr * r
                dy = dsrc[:, sl] * (0.125 if off == 4 else 1.0)
                dgs = jnp.sum(dy * xh, axis=0, keepdims=True)
                if off == 4:
                    dgq = dgq + dgs
                else:
                    dgk = dgk + dgs
                dxh = dy * gain
                dproj_ref[:, off * GROUP_W + sl.start:off * GROUP_W + sl.stop] = \
                    (r * (dxh - xh * _group_mean64(dxh * xh))).astype(BF)
            dproj_ref[:, 6 * GROUP_W + sl.start:6 * GROUP_W + sl.stop] = dvf_ref[:, sl].astype(BF)
        gq_acc[...] += dgq
        gk_acc[...] += dgk
        row = lax.broadcasted_iota(jnp.int32, (tm, tm), 0)
        col = lax.broadcasted_iota(jnp.int32, (tm, tm), 1)
        dlf = _dot_exact((col >= row).astype(BF), df_ref[...]) + carry[0:1, :]
        carry[...] = jnp.broadcast_to(dlf[0:1, :], carry.shape)
        lane = lax.broadcasted_iota(jnp.int32, (tm, LANES), 1)
        dz = jnp.where(lane < N_HEADS, dlf / (1.0 + jnp.exp(z_ref[...])), 0.0)
        db_ref[...] += jnp.sum(dz, axis=0, keepdims=True)
        dz_bf = dz.astype(BF)
        dz_ref[...] = dz_bf
        dn1 = jnp.dot(dz_bf, wf_ref[...], preferred_element_type=F32)
        for sec in range(MAIN_W // GROUP_W):
            sl = slice(sec * GROUP_W, (sec + 1) * GROUP_W)
            dn1 = dn1 + jnp.dot(dproj_ref[:, sl], wm_ref[sl, :], preferred_element_type=F32)
        dx, dg = _rms_bwd(x_ref[...], g_ref[...], dn1)
        dx_ref[...] = dh1_ref[...] + dx
        dg_ref[...] += dg

        @pl.when(i == n_t - 1)
        def _():
            dgq_ref[...] = gq_acc[:, :HEAD_DIM] + gq_acc[:, HEAD_DIM:]
            dgk_ref[...] = gk_acc[:, :HEAD_DIM] + gk_acc[:, HEAD_DIM:]

    row_spec = lambda w, col=0: pl.BlockSpec((tm, w), lambda i: (n_t - 1 - i, col))
    full = lambda a: pl.BlockSpec(a.shape, lambda i: (0,) * a.ndim)
    acc = lambda r, c: pl.BlockSpec((r, c), lambda i: (0, 0))
    return pl.pallas_call(
        body, name="in_proj_bwd", grid=(n_t,),
        out_shape=(jax.ShapeDtypeStruct((t_len, MAIN_W), BF), jax.ShapeDtypeStruct((t_len, LANES), BF),
                   jax.ShapeDtypeStruct((t_len, D_MODEL), F32), jax.ShapeDtypeStruct((1, D_MODEL), F32),
                   jax.ShapeDtypeStruct((1, HEAD_DIM), F32), jax.ShapeDtypeStruct((1, HEAD_DIM), F32),
                   jax.ShapeDtypeStruct((1, LANES), F32)),
        in_specs=[row_spec(D_MODEL), full(g_mix), row_spec(D_MODEL)] + [row_spec(GROUP_W)] * 7
        + [row_spec(LANES), row_spec(GROUP_W, 4), row_spec(GROUP_W, 5), row_spec(LANES), row_spec(LANES), row_spec(LANES),
           full(gq_t), full(gk_t), *_w_in_specs()],
        out_specs=(row_spec(MAIN_W), row_spec(LANES), row_spec(D_MODEL), acc(1, D_MODEL), acc(1, HEAD_DIM), acc(1, HEAD_DIM),
                   acc(1, LANES)),
        scratch_shapes=[pltpu.VMEM((8, LANES), F32), pltpu.VMEM((1, LANES), F32), pltpu.VMEM((1, LANES), F32)],
        compiler_params=_cparams(("arbitrary",)),
    )(x, g_mix, dh1, dq_r, dk_r, dv_r, drg, dq_f, dk_f, dv_f, df_col, proj, proj, z, cos_t, sin_t, gq_t, gk_t, w_in_t, w_in_t)


def _matmul_tn(a, b, name, bk=512):
    t_len, m = a.shape
    n = b.shape[1]
    bm = m if m <= TN_MAX_ROWS else m // 2
    bk = min(bk, t_len)

    def body(a_ref, b_ref, o_ref):
        @pl.when(pl.program_id(1) == 0)
        def _():
            o_ref[...] = jnp.zeros_like(o_ref)

        o_ref[...] += _dot_tn(a_ref[...], b_ref[...])

    return pl.pallas_call(
        body, name=name, grid=(m // bm, t_len // bk),
        out_shape=jax.ShapeDtypeStruct((m, n), F32),
        in_specs=[pl.BlockSpec((bk, bm), lambda i, k: (k, i)), pl.BlockSpec((bk, n), lambda i, k: (k, 0))],
        out_specs=pl.BlockSpec((bm, n), lambda i, k: (i, 0)),
        compiler_params=_cparams(("arbitrary", "arbitrary")),
    )(a, b)


def _place():
    x, y, c = lax.axis_index("x"), lax.axis_index("y"), lax.axis_index("c")
    chips = [(1 - x, y), (x, 1 - y), (1 - x, 1 - y)]
    return x, y, c, chips


def _row_chunks(rows, limit):
    step = max(d for d in range(16, min(rows, limit) + 1, 16) if rows % d == 0)
    return [slice(i, i + step) for i in range(0, rows, step)]


ICI_CHUNK_ROWS = 128
D2D_CHUNK_ROWS = 64


def _gather_phase(phase, ins, outs, send_sems, recv_sems):
    x, y, c, chips = _place()
    me_chip = 2 * x + y
    sibling = (x, y, 1 - c)

    def copy(w, k, slot, half, to, rows=slice(None), src=None):
        dst = outs[w].at[slot, half, rows]
        return pltpu.make_async_remote_copy(src_ref=dst if src is None else src, dst_ref=dst,
                                            send_sem=send_sems.at[w, k], recv_sem=recv_sems.at[w, k],
                                            device_id=to, device_id_type=MESH)

    for w in range(len(ins)):
        for j, (px, py) in enumerate(chips):
            if phase == 0:
                for rows in _row_chunks(ins[w].shape[1], ICI_CHUNK_ROWS):
                    copy(w, j, me_chip, c, (px, py, c), rows, src=ins[w].at[c, rows]).start()
            elif phase == 1:
                copy(w, j, 2 * px + py, c, (x, y, c)).wait_recv()
                for rows in _row_chunks(ins[w].shape[1], D2D_CHUNK_ROWS):
                    copy(w, 3 + j, 2 * px + py, c, sibling, rows).start()
            else:
                copy(w, 3 + j, 2 * px + py, 1 - c, (x, y, c)).wait_recv()
                copy(w, j, me_chip, c, (px, py, c), src=ins[w].at[c]).wait_send()
                copy(w, 3 + j, 2 * px + py, c, sibling).wait_send()


def _gather_scratch(n_w):
    return [pltpu.SemaphoreType.DMA((n_w, 6)), pltpu.SemaphoreType.DMA((n_w, 6))]


def _all_gather_weights(shards):
    n_w = len(shards)

    def body(*refs):
        for phase in range(3):
            _gather_phase(phase, refs[:n_w], refs[n_w:2 * n_w], *refs[2 * n_w:])

    return pl.pallas_call(
        body, name="all_gather_weights",
        out_shape=tuple(jax.ShapeDtypeStruct((4,) + s.shape, s.dtype) for s in shards),
        in_specs=[ANY] * n_w, out_specs=(ANY,) * n_w, scratch_shapes=_gather_scratch(n_w),
    )(*shards)


def _exchange_phase(phase, ins, theirs, send_sems, recv_sems):
    x, y, c, _ = _place()

    def remote(w, k=slice(None), rows=slice(None)):
        return pltpu.make_async_remote_copy(src_ref=ins[w].at[k, 1 - c, rows], dst_ref=theirs[w].at[k, rows],
                                            send_sem=send_sems.at[w], recv_sem=recv_sems.at[w], device_id=(x, y, 1 - c),
                                            device_id_type=MESH)

    for w in range(len(ins)):
        if phase == 0:
            for k in range(4):
                for rows in _row_chunks(ins[w].shape[2], D2D_CHUNK_ROWS):
                    remote(w, k, rows).start()
        else:
            remote(w).wait()


def _exchange_scratch(n_w):
    return [pltpu.SemaphoreType.DMA((n_w,)), pltpu.SemaphoreType.DMA((n_w,))]


def _exchange_out_shapes(grads):
    return tuple(jax.ShapeDtypeStruct((4,) + g.shape[2:], g.dtype) for g in grads)


def _exchange_core_halves(grads):
    n_w = len(grads)

    def body(*refs):
        for phase in range(2):
            _exchange_phase(phase, refs[:n_w], refs[n_w:2 * n_w], *refs[2 * n_w:])

    return pl.pallas_call(
        body, name="exchange_core_halves", out_shape=_exchange_out_shapes(grads),
        in_specs=[ANY] * n_w, out_specs=(ANY,) * n_w, scratch_shapes=_exchange_scratch(n_w),
    )(*grads)


def _add_pairs(part, theirs, name):
    _, _, r, c = part.shape
    rb = 32 if r % 32 == 0 else r

    def body(a_ref, b_ref, own_ref, ob_ref):
        my_chip = 2 * lax.axis_index("x") + lax.axis_index("y")
        ob_ref[...] = (a_ref[...] + b_ref[...]).astype(BF)
        own_ref[...] = a_ref[my_chip] + b_ref[my_chip]

    spec = pl.BlockSpec((4, rb, c), lambda i: (0, i, 0))
    return pl.pallas_call(
        body, name=name, grid=(r // rb,),
        out_shape=(jax.ShapeDtypeStruct((r, c), F32), jax.ShapeDtypeStruct((4, r, c), BF)),
        in_specs=[pl.BlockSpec((4, None, rb, c), lambda i: (0, lax.axis_index("c"), i, 0)), spec],
        out_specs=(pl.BlockSpec((rb, c), lambda i: (i, 0)), spec), compiler_params=_cparams(("arbitrary",)),
    )(part, theirs)


def _scatter_phase(phase, bfs, got, send_sems, recv_sems):
    x, y, c, chips = _place()

    def remote(w, j, px, py, rows=slice(None)):
        return pltpu.make_async_remote_copy(src_ref=bfs[w].at[2 * px + py, rows], dst_ref=got[w].at[j, rows],
                                            send_sem=send_sems.at[w, j], recv_sem=recv_sems.at[w, j], device_id=(px, py, c),
                                            device_id_type=MESH)

    for w in range(len(bfs)):
        for j, (px, py) in enumerate(chips):
            if phase == 0:
                for rows in _row_chunks(bfs[w].shape[1], ICI_CHUNK_ROWS):
                    remote(w, j, px, py, rows).start()
            else:
                remote(w, j, px, py).wait()


def _scatter_scratch(n_w):
    return [pltpu.SemaphoreType.DMA((n_w, 3)), pltpu.SemaphoreType.DMA((n_w, 3))]


def _scatter_out_shapes(sums_bf16):
    return tuple(jax.ShapeDtypeStruct((3,) + s.shape[1:], BF) for s in sums_bf16)


def _add_received(own, got, name):
    r, c = own.shape
    rb = 32 if r % 32 == 0 else r

    def body(o_ref, g_ref, out_ref):
        out_ref[...] = ((o_ref[...] + g_ref[0].astype(F32)) + g_ref[1].astype(F32)) + g_ref[2].astype(F32)

    return pl.pallas_call(
        body, name=name, grid=(r // rb,), out_shape=jax.ShapeDtypeStruct((r, c), F32),
        in_specs=[pl.BlockSpec((rb, c), lambda i: (i, 0)), pl.BlockSpec((3, rb, c), lambda i: (0, i, 0))],
        out_specs=pl.BlockSpec((rb, c), lambda i: (i, 0)), compiler_params=_cparams(("arbitrary",)),
    )(own, got)


def _share_with_sibling(halves):
    n_w = len(halves)

    def body(*refs):
        ins, outs = refs[:n_w], refs[n_w:2 * n_w]
        send_sems, recv_sems = refs[2 * n_w:]
        x, y, c, _ = _place()

        def remote(w, rows=slice(None)):
            return pltpu.make_async_remote_copy(src_ref=ins[w].at[rows], dst_ref=outs[w].at[c, rows], send_sem=send_sems.at[w],
                                                recv_sem=recv_sems.at[w], device_id=(x, y, 1 - c), device_id_type=MESH)

        for w in range(n_w):
            for rows in _row_chunks(ins[w].shape[0], D2D_CHUNK_ROWS):
                remote(w, rows).start()
        for w in range(n_w):
            remote(w).wait()

    return pl.pallas_call(
        body, name="share_with_sibling",
        out_shape=tuple(jax.ShapeDtypeStruct((2,) + h.shape, h.dtype) for h in halves),
        in_specs=[ANY] * n_w, out_specs=(ANY,) * n_w,
        scratch_shapes=[pltpu.SemaphoreType.DMA((n_w,)), pltpu.SemaphoreType.DMA((n_w,))],
    )(*halves)


def _all_reduce_small(pack):
    r, c = pack.shape

    def body(p_ref, out_ref, slots, send_sems, recv_sems):
        x, y, cc, _ = _place()
        me = 4 * x + 2 * y + cc
        slots[me] = p_ref[...]
        copies = []
        for k in range(1, 8):
            dx, dy, dc = (k >> 2) & 1, (k >> 1) & 1, k & 1
            to = (1 - x if dx else x, 1 - y if dy else y, 1 - cc if dc else cc)
            cp = pltpu.make_async_remote_copy(src_ref=p_ref, dst_ref=slots.at[me], send_sem=send_sems.at[k - 1],
                                              recv_sem=recv_sems.at[k - 1], device_id=to, device_id_type=MESH)
            cp.start()
            copies.append(cp)
        for cp in copies:
            cp.wait()
        total = slots[0]
        for d in range(1, 8):
            total = total + slots[d]
        out_ref[...] = total

    return pl.pallas_call(
        body, name="all_reduce_small", out_shape=jax.ShapeDtypeStruct((r, c), F32),
        in_specs=[VMEM_SPEC], out_specs=VMEM_SPEC,
        scratch_shapes=[pltpu.VMEM((8, r, c), F32), pltpu.SemaphoreType.DMA((7,)), pltpu.SemaphoreType.DMA((7,))],
    )(pack)


def _adamw_update(w_ref, g_ref, m_ref, v_ref, d_ref, nm_ref, nv_ref):
    gv = g_ref[...]
    nm = ADAM_B1 * m_ref[...] + (1.0 - ADAM_B1) * gv
    nv = ADAM_B2 * v_ref[...] + (1.0 - ADAM_B2) * (gv * gv)
    nm_ref[...] = nm
    nv_ref[...] = nv
    m_hat = nm / (1.0 - ADAM_B1 ** ADAM_STEP)
    v_hat = nv / (1.0 - ADAM_B2 ** ADAM_STEP)
    d_ref[...] = -ADAM_LR * (m_hat / (jnp.sqrt(v_hat) + ADAM_EPS) + ADAM_WD * w_ref[...])


def _adamw_many(ws, gs, ms, vs, sums):
    n_a, n_w = len(ws), len(sums)
    n_steps = ADAM_STEPS
    specs = [pl.BlockSpec((w.shape[0] // n_steps, w.shape[1]), lambda i: (i, 0)) for w in ws]

    def body(*refs):
        ins = refs[:4 * n_a]
        outs = refs[4 * n_a + n_w:7 * n_a + n_w]
        comm = (refs[4 * n_a:4 * n_a + n_w], refs[7 * n_a + n_w:7 * n_a + 2 * n_w]) + tuple(refs[7 * n_a + 2 * n_w:])
        step = pl.program_id(0)

        @pl.when(step == 0)
        def _():
            _scatter_phase(0, *comm)

        for a in range(n_a):
            _adamw_update(*(ins[k * n_a + a] for k in range(4)), *(outs[3 * a + k] for k in range(3)))

        @pl.when(step == n_steps - 1)
        def _():
            _scatter_phase(1, *comm)

    flat = pl.pallas_call(
        body, name="adamw_late", grid=(n_steps,),
        out_shape=tuple(jax.ShapeDtypeStruct(w.shape, F32) for w in ws for _ in range(3)) + _scatter_out_shapes(sums),
        in_specs=specs * 4 + [ANY] * n_w, out_specs=tuple(s for s in specs for _ in range(3)) + (ANY,) * n_w,
        scratch_shapes=_scatter_scratch(n_w), compiler_params=_cparams(("arbitrary",)),
    )(*ws, *gs, *ms, *vs, *sums)
    return [tuple(flat[3 * a:3 * a + 3]) for a in range(n_a)] + list(flat[3 * n_a:])


def _adamw(w, g, m, v, name):
    r, c = w.shape
    rb, cb = (64, c) if r % 64 == 0 else (r, LANES if (r % 8 and c % LANES == 0) else c)

    def body(*refs):
        _adamw_update(*refs)

    spec = pl.BlockSpec((rb, cb), lambda i, j: (i, j))
    return pl.pallas_call(
        body, name=name, grid=(r // rb, c // cb), out_shape=(jax.ShapeDtypeStruct((r, c), F32),) * 3,
        in_specs=[spec] * 4, out_specs=(spec,) * 3, compiler_params=_cparams(("arbitrary", "arbitrary")),
    )(w, g, m, v)


def _rope_tables(t_len):
    inv_freq = ROPE_BASE ** (-jnp.arange(0, HEAD_DIM, 2, dtype=F32) / HEAD_DIM)
    ang = jnp.arange(t_len, dtype=F32)[:, None] * inv_freq[None, :]
    cos, sin = jnp.cos(ang), jnp.sin(ang)
    cos_t = jnp.concatenate([cos, cos, cos, cos], axis=-1)
    sin_t = jnp.concatenate([-sin, sin, -sin, sin], axis=-1)
    return cos_t, sin_t


def _cols_to_shards(dw):
    r, n = dw.shape
    return jnp.transpose(dw.reshape(2, r // 2, 4, n // 4), (2, 0, 1, 3))


def _rows_to_shards(dw):
    r, n = dw.shape
    padded = _pad_rows(dw.reshape(4, r // 4, n))
    return padded.reshape(4, 2, padded.shape[1] // 2, n)


def _pad_lanes(a):
    extra = -a.shape[-1] % LANES
    return a if extra == 0 else jnp.pad(a, [(0, 0)] * (a.ndim - 1) + [(0, extra)])


def _pad_rows(a):
    rows = a.shape[-2]
    extra = 0 if rows % SHARD_ROW_ALIGN == 0 else -rows % SHARD_ROW_PAD
    return a if extra == 0 else jnp.pad(a, [(0, 0)] * (a.ndim - 2) + [(0, extra), (0, 0)])


def _pad_row(a, width=D_MODEL):
    a = a.reshape(1, -1)
    return jnp.pad(a, ((0, 0), (0, width - a.shape[1])))


def kernel(x, mem, g_mix, w_in, b_forget, g_ret_out, g_fox_q, g_fox_k, w_out, g_xattn, w_xq, w_xkv, g_mem, g_xq, g_xk, w_xo, g_ffn, w_gate, w_up, w_down, loss_target, m_g_mix, m_w_in, m_b_forget, m_g_ret_out, m_g_fox_q, m_g_fox_k, m_w_out, m_g_xattn, m_w_xq, m_w_xkv, m_g_mem, m_g_xq, m_g_xk, m_w_xo, m_g_ffn, m_w_gate, m_w_up, m_w_down, v_g_mix, v_w_in, v_b_forget, v_g_ret_out, v_g_fox_q, v_g_fox_k, v_w_out, v_g_xattn, v_w_xq, v_w_xkv, v_g_mem, v_g_xq, v_g_xk, v_w_xo, v_g_ffn, v_w_gate, v_w_up, v_w_down):
    big = {"w_in": (w_in, m_w_in, v_w_in), "w_out": (w_out, m_w_out, v_w_out), "w_xq": (w_xq, m_w_xq, v_w_xq),
           "w_xkv": (w_xkv, m_w_xkv, v_w_xkv), "w_xo": (w_xo, m_w_xo, v_w_xo), "w_gate": (w_gate, m_w_gate, v_w_gate),
           "w_up": (w_up, m_w_up, v_w_up), "w_down": (w_down, m_w_down, v_w_down)}
    for n in TRANSPOSED:
        big[n] = tuple(jnp.swapaxes(a, 1, 2) for a in big[n])
    shards = {}
    for n in big:
        w = _pad_rows(_pad_lanes(big[n][0][0].astype(BF)))
        shards[n] = w.reshape(2, w.shape[0] // 2, w.shape[1])
    sizes = {n: big[n][0].shape[1:] for n in big}
    w_in_full = _assemble_weight("w_in", _all_gather_weights([shards["w_in"]])[0], shards["w_in"], sizes["w_in"])
    small_w ={"g_mix": g_mix, "b_forget": b_forget, "g_ret_out": g_ret_out, "g_fox_q": g_fox_q, "g_fox_k": g_fox_k,
               "g_xattn": g_xattn, "g_mem": g_mem, "g_xq": g_xq, "g_xk": g_xk, "g_ffn": g_ffn}
    m_small = {"g_mix": m_g_mix, "b_forget": m_b_forget, "g_ret_out": m_g_ret_out, "g_fox_q": m_g_fox_q, "g_fox_k": m_g_fox_k,
               "g_xattn": m_g_xattn, "g_mem": m_g_mem, "g_xq": m_g_xq, "g_xk": m_g_xk, "g_ffn": m_g_ffn}
    v_small = {"g_mix": v_g_mix, "b_forget": v_b_forget, "g_ret_out": v_g_ret_out, "g_fox_q": v_g_fox_q, "g_fox_k": v_g_fox_k,
               "g_xattn": v_g_xattn, "g_mem": v_g_mem, "g_xq": v_g_xq, "g_xk": v_g_xk, "g_ffn": v_g_ffn}
    loss_part, grad_x, sums, got, in_bf, small_g = _local_step(x[0], mem[0], loss_target[0], w_in_full, shards, sizes, small_w)
    return _reduce_and_update(big, sums, got, in_bf, small_w, small_g, loss_part, grad_x, m_small, v_small)


def _assemble_weight(name, gathered, own, size):
    rows, width = size
    my_chip = 2 * lax.axis_index("x") + lax.axis_index("y")
    g = lax.dynamic_update_slice(gathered, own[None], (my_chip, 0, 0, 0))
    g = g.reshape(4, 2 * g.shape[2], g.shape[3])[:, :rows, :width]
    return jnp.transpose(g, (1, 0, 2)).reshape(rows, 4 * width) if name in COL_SHARDED else g.reshape(4 * rows, width)


def _shard_parts(names, dw):
    return [_pad_lanes(_cols_to_shards(dw[n]) if n in COL_SHARDED else _rows_to_shards(dw[n])) for n in names]


def _core_sums(names, parts, theirs):
    return [_add_pairs(p, t, f"core_sum_{n}") for n, p, t in zip(names, parts, theirs)]


def _local_step(xs, mems, tgt, w_in_full, shards, sizes, small_w):
    g_mix, b_forget, g_ret_out, g_fox_q, g_fox_k = (small_w[n] for n in ("g_mix", "b_forget", "g_ret_out", "g_fox_q", "g_fox_k"))
    g_xattn, g_mem, g_xq, g_xk, g_ffn = (small_w[n] for n in ("g_xattn", "g_mem", "g_xq", "g_xk", "g_ffn"))
    w_in_t = jnp.pad(w_in_full, ((0, MAIN_W + LANES - IN_W), (0, 0)))
    t_len = xs.shape[0]
    cos_t, sin_t = _rope_tables(t_len)
    tables = _decay_tables(min(RET_BLOCK, t_len))
    gq_t = jnp.concatenate([g_fox_q, g_fox_q], axis=-1)
    gk_t = jnp.concatenate([g_fox_k, g_fox_k], axis=-1)
    b_pad = _pad_row(b_forget, LANES)
    g_ret = g_ret_out.reshape(N_HEADS // 2, 1, LANES)

    n1, proj, rq, rk, q_aug, k_aug, z = _in_proj_fwd(xs, g_mix, w_in_t, b_pad, cos_t, sin_t, gq_t, gk_t)
    raw, mix_r, states = _retention_fwd(rq, rk, proj, g_ret, tables)
    mix_f, o32, lse, *gathered = _fox_fwd(q_aug, k_aug, proj, [shards[n] for n in LATE])
    full = {n: _assemble_weight(n, g, shards[n], sizes[n]) for n, g in zip(LATE, gathered)}
    memn, kraw, kn, vmem = _mem_kv_fwd(mems, g_mem, full["w_xkv"], g_xk)
    h1, hn2, qx, o_x, h2 = _attn_out_xattn_fwd(xs, mix_r, mix_f, full["w_out"], g_xattn, full["w_xq"], g_xq, kn, vmem, full["w_xo"])
    hn3, gate, up, act, dh3, loss_part = _ffn_loss_fwd(h2, g_ffn, full["w_gate"], full["w_up"], full["w_down"], tgt)

    dgate, dup, dh2, dg_ffn = _ffn_bwd(dh3, gate, up, h2, g_ffn, full["w_gate"], full["w_up"], full["w_down"])
    dqx, dh1, dmr, dmf, dkn, dvm, dg_xattn, dg_xq = _attn_out_xattn_bwd(dh2, h1, qx, kn, vmem, full["w_xo"], full["w_xq"],
                                                                      full["w_out"], g_xattn, g_xq)
    dw_xkv, dg_mem, dg_xk = _mem_kv_bwd(dkn, dvm, kraw, mems, memn, g_mem, g_xk, full["w_xkv"])
    dw = {
        "w_out": jnp.concatenate([_matmul_tn(mix_r, dh1, "dw_out_ret"), _matmul_tn(mix_f, dh1, "dw_out_fox")], axis=0),
        "w_xq": _matmul_tn(hn2, dqx, "dw_xq"),
        "w_xkv": dw_xkv,
        "w_xo": _matmul_tn(o_x, dh2, "dw_xo"),
        "w_gate": _matmul_tn(dgate, hn3, "dw_gate"),
        "w_up": _matmul_tn(dup, hn3, "dw_up"),
        "w_down": _matmul_tn(act, dh3, "dw_down"),
    }
    late_parts = _shard_parts(LATE, dw)
    dq_r, dk_r, dv_r, drg, dg_ret, *late_theirs = _retention_bwd(dmr, raw, proj, g_ret, rq, rk, states, tables, late_parts)
    late_sums = _core_sums(LATE, late_parts, late_theirs)
    dq_f, dk_f, dv_f, df, *late_got = _fox_bwd(q_aug, k_aug, proj, dmf, o32, lse, [s[1] for s in late_sums])
    df_col = jnp.pad(jnp.transpose(df, (1, 0, 2)).reshape(t_len, N_HEADS), ((0, 0), (0, LANES - N_HEADS)))
    dproj, dz, grad_x, dg_mix, dg_fq, dg_fk, db = _in_proj_bwd(xs, g_mix, dh1, dq_r, dk_r, dv_r, drg, dq_f, dk_f, dv_f, df_col,
                                                              proj, z, cos_t, sin_t, gq_t, gk_t, w_in_t)

    dw_in = jnp.concatenate([_matmul_tn(dproj, n1, "dw_in_main"), _matmul_tn(dz, n1, "dw_in_ff")[:IN_W - MAIN_W]], axis=0)
    in_parts = _shard_parts(("w_in",), {"w_in": dw_in})
    in_sums = _core_sums(("w_in",), in_parts, _exchange_core_halves(in_parts))
    sums = {n: s[0] for n, s in zip(("w_in",) + LATE, in_sums + late_sums)}
    got = dict(zip(LATE, late_got))
    in_bf = in_sums[0][1]
    small_g = {"g_mix": dg_mix, "b_forget": db[:, :N_HEADS], "g_ret_out": dg_ret, "g_fox_q": dg_fq, "g_fox_k": dg_fk,
               "g_xattn": dg_xattn, "g_mem": dg_mem, "g_xq": dg_xq, "g_xk": dg_xk, "g_ffn": dg_ffn}
    return loss_part, grad_x, sums, got, in_bf, small_g


def _final_grads(names, big, sums, got):
    my_core = lax.axis_index("c")
    finals = [_add_received(sums[n], got[n], f"chip_sum_{n}") for n in names]
    shared = _share_with_sibling(finals)
    out = {}
    for n, s, fin in zip(names, shared, finals):
        s = lax.dynamic_update_slice(s, fin[None], (my_core, 0, 0))
        out[n] = s.reshape(2 * s.shape[1], s.shape[2])[:big[n][0].shape[1], :big[n][0].shape[2]]
    return out


def _reduce_and_update(big, sums, got, in_bf, small_w, small_g, loss_part, grad_x, m_small, v_small):
    grads = _final_grads(LATE, big, sums, got)
    *late_updates, in_got = _adamw_many([big[n][0][0] for n in LATE], [grads[n] for n in LATE], [big[n][1][0] for n in LATE],
                                        [big[n][2][0] for n in LATE], [in_bf])
    updates = dict(zip(LATE, late_updates))
    grads.update(_final_grads(("w_in",), big, sums, {"w_in": in_got}))
    updates["w_in"] = _adamw(big["w_in"][0][0], grads["w_in"], big["w_in"][1][0], big["w_in"][2][0], "adamw_w_in")
    deltas, new_m, new_v = {}, {}, {}
    for n in big:
        restore = (lambda a: jnp.swapaxes(a[None], 1, 2)) if n in TRANSPOSED else (lambda a: a[None])
        grads[n] = restore(grads[n])
        deltas[n], new_m[n], new_v[n] = (restore(a) for a in updates[n])

    small_names = list(small_w)
    pad_rows = SMALL_ROWS - len(small_names) - 1
    stack = lambda d: jnp.concatenate([_pad_row(d[n]) for n in small_names] + [jnp.zeros((pad_rows + 1, D_MODEL), F32)], axis=0)
    g_pack = jnp.concatenate([_pad_row(small_g[n]) for n in small_names] + [_pad_row(loss_part[0:1, 0:1])]
                             + [jnp.zeros((pad_rows, D_MODEL), F32)], axis=0)
    g_tot = _all_reduce_small(g_pack)
    d_s, m_s, v_s = _adamw(stack(small_w), g_tot, stack(m_small), stack(v_small), "adamw_small")
    for i, n in enumerate(small_names):
        shape = small_w[n].shape
        size = int(np.prod(shape))
        grads[n] = g_tot[i, :size].reshape(shape)
        deltas[n], new_m[n], new_v[n] = d_s[i, :size].reshape(shape), m_s[i, :size].reshape(shape), v_s[i, :size].reshape(shape)
    loss = g_tot[len(small_names), 0]

    order = ["g_mix", "w_in", "b_forget", "g_ret_out", "g_fox_q", "g_fox_k", "w_out", "g_xattn", "w_xq", "w_xkv", "g_mem", "g_xq",
             "g_xk", "w_xo", "g_ffn", "w_gate", "w_up", "w_down"]
    return (loss, grad_x[None], *[grads[n] for n in order], *[deltas[n] for n in order], *[new_m[n] for n in order],
            *[new_v[n] for n in order])
```

```python
import functools

import numpy as np
import jax
import jax.numpy as jnp
from jax import lax
from jax.experimental import pallas as pl
from jax.experimental.pallas import tpu as pltpu

F32 = jnp.float32
BF = jnp.bfloat16

D_MODEL = 1024
HEAD_DIM = 64
N_HEADS = 8
GROUP_W = 512
N_XH = 4
XHD = 256
D_FF = 2816
MAIN_W = 3584
IN_W = 3592
ROPE_BASE = 10000.0
LOG2E = 1.4426950408889634
LN2 = 0.6931471805599453
EPS = 1e-6
NEG = -1e30
LANES = 128
RET_BLOCK = 256
REF_CHUNK = 64
ROW_TILE = 256
ATT_BLOCK = 256
TN_MAX_ROWS = 1408
SMALL_ROWS = 16
COL_SHARDED = ("w_xkv",)
TRANSPOSED = ("w_in", "w_gate", "w_up")
SHARD_ROW_ALIGN = 32
SHARD_ROW_PAD = 256
LATE = ("w_out", "w_xq", "w_xkv", "w_xo", "w_gate", "w_up", "w_down")
VMEM_LIMIT = 56 * 1024 * 1024

ADAM_LR = 0.001
ADAM_B1 = 0.9
ADAM_B2 = 0.999
ADAM_EPS = 1e-08
ADAM_WD = 0.01
ADAM_STEP = 10
ADAM_STEPS = 8

MESH = pl.DeviceIdType.MESH
ANY = pl.BlockSpec(memory_space=pl.ANY)
VMEM_SPEC = pl.BlockSpec(memory_space=pltpu.VMEM)


def _cparams(sem=None, vmem=VMEM_LIMIT):
    return pltpu.CompilerParams(dimension_semantics=sem, vmem_limit_bytes=vmem)


def _dot(a, b):
    return jnp.dot(a.astype(BF), b.astype(BF), preferred_element_type=F32)


def _dot_nt(a, b):
    return lax.dot_general(a.astype(BF), b.astype(BF), (((1,), (1,)), ((), ())), preferred_element_type=F32)


def _dot_tn(a, b):
    return lax.dot_general(a.astype(BF), b.astype(BF), (((0,), (0,)), ((), ())), preferred_element_type=F32)


def _split3(x):
    hi = x.astype(BF)
    r = x - hi.astype(F32)
    mid = r.astype(BF)
    lo = (r - mid.astype(F32)).astype(BF)
    return hi, mid, lo


def _dot_exact(ind, x):
    hi, mid, lo = _split3(x)
    return (jnp.dot(ind, lo, preferred_element_type=F32) + jnp.dot(ind, mid, preferred_element_type=F32)
            + jnp.dot(ind, hi, preferred_element_type=F32))


def _dot_nt_exact(ind, x):
    hi, mid, lo = _split3(x)
    dn = (((1,), (1,)), ((), ()))
    return (lax.dot_general(ind, lo, dn, preferred_element_type=F32) + lax.dot_general(ind, mid, dn, preferred_element_type=F32)
            + lax.dot_general(ind, hi, dn, preferred_element_type=F32))


def _sigmoid(x):
    return 1.0 / (1.0 + jnp.exp(-x))


def _rms_fwd(x, g):
    r = lax.rsqrt(jnp.mean(x * x, axis=-1, keepdims=True) + EPS)
    return x * r * g


def _rms_bwd(x, g, dy):
    r = lax.rsqrt(jnp.mean(x * x, axis=-1, keepdims=True) + EPS)
    xh = x * r
    dg = jnp.sum(dy * xh, axis=0, keepdims=True)
    dxh = dy * g
    dx = r * (dxh - xh * jnp.mean(dxh * xh, axis=-1, keepdims=True))
    return dx, dg


def _group_mean64(x):
    lane = lax.broadcasted_iota(jnp.int32, x.shape, 1)
    lo = lane < HEAD_DIM
    s_lo = jnp.sum(jnp.where(lo, x, 0.0), axis=-1, keepdims=True)
    s_hi = jnp.sum(jnp.where(lo, 0.0, x), axis=-1, keepdims=True)
    return jnp.where(lo, s_lo, s_hi) * (1.0 / HEAD_DIM)


def _swap32(x):
    lane = lax.broadcasted_iota(jnp.int32, x.shape, 1)
    first = (lane % HEAD_DIM) < (HEAD_DIM // 2)
    return jnp.where(first, pltpu.roll(x, LANES - HEAD_DIM // 2, axis=1), pltpu.roll(x, HEAD_DIM // 2, axis=1))


def _chunks(w):
    return [slice(j * LANES, (j + 1) * LANES) for j in range(w // LANES)]


def _aug_pair(qk, f_cols, is_query):
    lane = lax.broadcasted_iota(jnp.int32, qk.shape, 1)
    a = lane - HEAD_DIM
    values = (qk, pltpu.roll(qk, HEAD_DIM, axis=1))
    out = []
    for hh in range(2):
        hi, mid, lo = (p.astype(F32) for p in _split3(f_cols[hh] * LOG2E))
        if is_query:
            aux = jnp.where(a == 0, hi, jnp.where(a == 1, mid, jnp.where(a == 2, lo, jnp.where(a < 6, 1.0, 0.0))))
        else:
            aux = jnp.where(a < 3, 1.0, jnp.where(a == 3, -hi, jnp.where(a == 4, -mid, jnp.where(a == 5, -lo, 0.0))))
        out.append(jnp.where(a < 0, values[hh], aux))
    return jnp.concatenate(out, axis=-1).astype(BF)


def _mem_kv_fwd(mem, g_mem, w_xkv, g_xk):
    m_tok = mem.shape[0]

    def body(mem_ref, gm_ref, w_ref, gk_ref, memn_ref, kraw_ref, kn_ref, v_ref):
        mn = _rms_fwd(mem_ref[...], gm_ref[...]).astype(BF)
        memn_ref[...] = mn
        kv = jnp.dot(mn, w_ref[...], preferred_element_type=F32)
        k = kv[:, :D_MODEL]
        kraw_ref[...] = k
        v_ref[...] = kv[:, D_MODEL:].astype(BF)
        for h in range(N_XH):
            sl = slice(h * XHD, (h + 1) * XHD)
            kn_ref[:, sl] = _rms_fwd(k[:, sl], gk_ref[...]).astype(BF)

    return pl.pallas_call(
        body, name="mem_kv_fwd",
        out_shape=(jax.ShapeDtypeStruct((m_tok, D_MODEL), BF), jax.ShapeDtypeStruct((m_tok, D_MODEL), F32),
                   jax.ShapeDtypeStruct((m_tok, D_MODEL), BF), jax.ShapeDtypeStruct((m_tok, D_MODEL), BF)),
        in_specs=[VMEM_SPEC] * 4, out_specs=(VMEM_SPEC,) * 4, compiler_params=_cparams(),
    )(mem, g_mem, w_xkv, g_xk)


def _in_proj_fwd(x, g_mix, w_in_t, b_pad, cos_t, sin_t, gq_t, gk_t):
    t_len = x.shape[0]
    tm = min(ROW_TILE, t_len)
    n_t = t_len // tm

    def body(x_ref, g_ref, wm_ref, wf_ref, b_ref, cos_ref, sin_ref, gq_ref, gk_ref,
             n1_ref, proj_ref, rq_ref, rk_ref, qa_ref, ka_ref, z_ref, carry):
        i = pl.program_id(0)

        @pl.when(i == 0)
        def _():
            carry[...] = jnp.zeros_like(carry)

        n1 = _rms_fwd(x_ref[...], g_ref[...]).astype(BF)
        n1_ref[...] = n1
        z = _dot_nt(n1, wf_ref[...]) + b_ref[...]
        z_ref[...] = z
        lane = lax.broadcasted_iota(jnp.int32, z.shape, 1)
        lf = jnp.where(lane < N_HEADS, jnp.minimum(z, 0.0) - jnp.log(1.0 + jnp.exp(-jnp.abs(z))), 0.0)
        row = lax.broadcasted_iota(jnp.int32, (tm, tm), 0)
        col = lax.broadcasted_iota(jnp.int32, (tm, tm), 1)
        tri = (row >= col).astype(BF)
        fc = _dot_exact(tri, lf) + carry[0:1, :]
        carry[...] = jnp.broadcast_to(fc[tm - 1:tm, :], carry.shape)
        c, s = cos_ref[...], sin_ref[...]

        def section(n):
            p = _dot_nt(n1, wm_ref[n * GROUP_W:(n + 1) * GROUP_W, :])
            proj_ref[:, n * GROUP_W:(n + 1) * GROUP_W] = p.astype(BF)
            return p

        def rotate(p, out_ref, scale):
            for sl in _chunks(GROUP_W):
                out_ref[:, sl] = ((p[:, sl] * c + _swap32(p[:, sl]) * s) * scale).astype(BF)

        def norm_aug(p, gain, out_ref, scale, is_query):
            for j, sl in enumerate(_chunks(GROUP_W)):
                f = p[:, sl]
                f = f * lax.rsqrt(_group_mean64(f * f) + EPS) * gain * scale
                out_ref[:, 2 * j * LANES:2 * (j + 1) * LANES] = _aug_pair(f, [fc[:, 2 * j:2 * j + 1], fc[:, 2 * j + 1:2 * j + 2]], is_query)

        p_rq, p_rk = section(0), section(1)
        rotate(p_rq, rq_ref, 0.125)
        section(2)
        rotate(p_rk, rk_ref, 1.0)
        section(3)
        p_fq = section(4)
        p_fk = section(5)
        norm_aug(p_fq, gq_ref[...], qa_ref, 0.125 * LOG2E, True)
        section(6)
        norm_aug(p_fk, gk_ref[...], ka_ref, 1.0, False)

    row_spec = lambda w: pl.BlockSpec((tm, w), lambda i: (i, 0))
    full = lambda a: pl.BlockSpec(a.shape, lambda i: (0,) * a.ndim)
    return pl.pallas_call(
        body, name="in_proj_fwd", grid=(n_t,),
        out_shape=(jax.ShapeDtypeStruct((t_len, D_MODEL), BF), jax.ShapeDtypeStruct((t_len, MAIN_W), BF),
                   jax.ShapeDtypeStruct((t_len, GROUP_W), BF), jax.ShapeDtypeStruct((t_len, GROUP_W), BF),
                   jax.ShapeDtypeStruct((t_len, 2 * GROUP_W), BF), jax.ShapeDtypeStruct((t_len, 2 * GROUP_W), BF),
                   jax.ShapeDtypeStruct((t_len, LANES), F32)),
        in_specs=[row_spec(D_MODEL), full(g_mix), *_w_in_specs(), full(b_pad), row_spec(LANES), row_spec(LANES),
                  full(gq_t), full(gk_t)],
        out_specs=(row_spec(D_MODEL), row_spec(MAIN_W), row_spec(GROUP_W), row_spec(GROUP_W), row_spec(2 * GROUP_W),
                   row_spec(2 * GROUP_W), row_spec(LANES)),
        scratch_shapes=[pltpu.VMEM((8, LANES), F32)],
        compiler_params=_cparams(("arbitrary",)),
    )(x, g_mix, w_in_t, w_in_t, b_pad, cos_t, sin_t, gq_t, gk_t)


def _w_in_specs():
    return (pl.BlockSpec((MAIN_W, D_MODEL), lambda i: (0, 0)), pl.BlockSpec((LANES, D_MODEL), lambda i: (MAIN_W // LANES, 0)))


def _decay_tables(c):
    h = np.arange(N_HEADS, dtype=np.float64)
    lg = np.log(1.0 - 2.0 ** (-5.0 - h)).astype(np.float32).astype(np.float64)
    t = np.arange(c)
    same_or_earlier = (t[None, :] // REF_CHUNK) <= (t[:, None] // REF_CHUNK)
    w = np.where(same_or_earlier[None], np.exp(lg[:, None, None] * np.abs(t[:, None] - t[None, :])[None]), 0.0)
    qd = np.exp(lg[:, None] * (t[None, :] + 1.0))
    kd = np.exp(lg[:, None] * (c - 1.0 - t[None, :]))
    cd = np.exp(lg * c)
    ones = np.ones((1, 1, HEAD_DIM))
    return (jnp.asarray(w, F32), jnp.asarray(qd[:, :, None] * ones, F32), jnp.asarray(kd[:, :, None] * ones, F32),
            jnp.asarray(cd[:, None, None] * np.ones((1, HEAD_DIM, HEAD_DIM)), F32))


def _retention_fwd(rq, rk, proj, g_ret, tables):
    t_len = rq.shape[0]
    c = min(RET_BLOCK, t_len)
    n_b = t_len // c
    wdec, qdec, kdec, cdec = tables
    v_col, g_col = 2 * GROUP_W // LANES, 3 * GROUP_W // LANES

    def body(q_ref, k_ref, v_ref, rg_ref, g_ref, w_ref, qd_ref, kd_ref, cd_ref, raw_ref, mix_ref, st_ref, state):
        i = pl.program_id(1)

        @pl.when(i == 0)
        def _():
            state[...] = jnp.zeros_like(state)

        q2, k2, v2 = q_ref[...], k_ref[...], v_ref[...]
        outs = []
        for hh in range(2):
            sl = slice(hh * HEAD_DIM, (hh + 1) * HEAD_DIM)
            q, k, v = q2[:, sl], k2[:, sl], v2[:, sl]
            sp = state[hh]
            st_ref[0, 0, hh] = sp
            a = _dot_nt(q, k) * w_ref[hh]
            o = _dot(a, v) + _dot(q.astype(F32) * qd_ref[hh], sp)
            state[hh] = sp * cd_ref[hh] + _dot_tn(k.astype(F32) * kd_ref[hh], v)
            outs.append(o)
        o2 = jnp.concatenate(outs, axis=-1)
        raw_ref[...] = o2
        xc = o2 - _group_mean64(o2)
        xh = xc * lax.rsqrt(_group_mean64(xc * xc) + EPS)
        gate = rg_ref[...].astype(F32)
        mix_ref[...] = (gate * _sigmoid(gate) * (xh * g_ref[0])).astype(BF)

    blk = lambda col0: pl.BlockSpec((c, LANES), lambda hp, i: (i, col0 + hp))
    tab = lambda a: pl.BlockSpec((2,) + a.shape[1:], lambda hp, i: (hp, 0, 0))
    return pl.pallas_call(
        body, name="retention_fwd", grid=(N_HEADS // 2, n_b),
        out_shape=(jax.ShapeDtypeStruct((t_len, GROUP_W), F32), jax.ShapeDtypeStruct((t_len, GROUP_W), BF),
                   jax.ShapeDtypeStruct((N_HEADS // 2, n_b, 2, HEAD_DIM, HEAD_DIM), F32)),
        in_specs=[blk(0), blk(0), blk(v_col), blk(g_col), pl.BlockSpec((1, 1, LANES), lambda hp, i: (hp, 0, 0)),
                  tab(wdec), tab(qdec), tab(kdec), tab(cdec)],
        out_specs=(blk(0), blk(0), pl.BlockSpec((1, 1, 2, HEAD_DIM, HEAD_DIM), lambda hp, i: (hp, i, 0, 0, 0))),
        scratch_shapes=[pltpu.VMEM((2, HEAD_DIM, HEAD_DIM), F32)],
        compiler_params=_cparams(("arbitrary", "arbitrary")),
    )(rq, rk, proj, proj, g_ret, wdec, qdec, kdec, cdec)


def _fox_fwd(q_aug, k_aug, proj, shards):
    t_len = q_aug.shape[0]
    tq = min(ATT_BLOCK, t_len)
    nsub = 2 if t_len >= 2 * tq else 1
    tg = nsub * tq
    n_q = t_len // tg
    v_col = 6 * GROUP_W // LANES
    tc = min(512, t_len)
    n_w = len(shards)
    n_steps = (N_HEADS // 2) * n_q

    def body(*refs):
        q_ref, k_ref, v_ref = refs[:3]
        o_ref, o32_ref, lse_ref = refs[3 + n_w:6 + n_w]
        vt = refs[6 + 2 * n_w]
        comm = (refs[3:3 + n_w], refs[6 + n_w:6 + 2 * n_w]) + tuple(refs[7 + 2 * n_w:])
        i = pl.program_id(1)
        step = pl.program_id(0) * n_q + i

        @pl.when(step == 0)
        def _():
            _gather_phase(0, *comm)

        @pl.when(step == (3 * n_steps) // 4)
        def _():
            _gather_phase(1, *comm)

        @pl.when(i == 0)
        def _():
            for c0 in range(0, t_len, tc):
                vt[:, c0:c0 + tc] = v_ref[c0:c0 + tc, :].T

        chains = [(u, hh) for u in range(nsub) for hh in range(2)]
        qs = {(u, hh): q_ref[u * tq:(u + 1) * tq, hh * LANES:(hh + 1) * LANES] for u, hh in chains}
        ones = jnp.ones((HEAD_DIM, tq), BF)

        def scores(j, which):
            k2 = k_ref[pl.ds(pl.multiple_of(j * tq, tq), tq), :]
            return {ch: _dot_nt(k2[:, ch[1] * LANES:(ch[1] + 1) * LANES], qs[ch]) for ch in which}

        def update(j, ss, carry, masked):
            v2 = vt[:, pl.ds(pl.multiple_of(j * tq, tq), tq)]
            ps, stats = {}, {}
            for ch in ss:
                m = carry[ch][0]
                s_t = ss[ch]
                if ch in masked:
                    krow = lax.broadcasted_iota(jnp.int32, (tq, tq), 0)
                    qcol = lax.broadcasted_iota(jnp.int32, (tq, tq), 1)
                    s_t = jnp.where(qcol >= krow, s_t, NEG)
                m_new = jnp.maximum(m, jnp.max(s_t, axis=0, keepdims=True))
                ps[ch] = jnp.exp2(s_t - m_new).astype(BF)
                stats[ch] = (m_new, jnp.exp2(m - m_new))
            out = dict(carry)
            for ch in ss:
                m_new, alpha = stats[ch]
                v_aug = jnp.concatenate([v2[ch[1] * HEAD_DIM:(ch[1] + 1) * HEAD_DIM, :], ones], axis=0)
                out[ch] = (m_new, carry[ch][1] * alpha + jnp.dot(v_aug, ps[ch], preferred_element_type=F32))
            return out

        def advance(j, state):
            ss, carry = state
            return scores(j + 1, chains), update(j, ss, carry, ())

        init = {ch: (jnp.full((1, tq), NEG, F32), jnp.zeros((LANES, tq), F32)) for ch in chains}
        first = nsub * i
        ss, carry = lax.fori_loop(0, first, advance, (scores(0, chains), init))
        carry = update(first, ss, carry, [(0, 0), (0, 1)])
        if nsub == 2:
            last = [(1, 0), (1, 1)]
            carry = update(first + 1, scores(first + 1, last), carry, last)
        for u in range(nsub):
            outs, lses = [], []
            for hh in range(2):
                m, acc = carry[u, hh]
                l = acc[HEAD_DIM:HEAD_DIM + 1, :]
                outs.append(acc[:HEAD_DIM, :] / l)
                lses.append(m + jnp.log2(l))
            o2 = jnp.concatenate(outs, axis=0).T
            o32_ref[u * tq:(u + 1) * tq, :] = o2
            o_ref[u * tq:(u + 1) * tq, :] = o2.astype(BF)
            lse_ref[0, :, u * tq:(u + 1) * tq] = jnp.concatenate(lses, axis=0)

        @pl.when(step == n_steps - 1)
        def _():
            _gather_phase(2, *comm)

    return pl.pallas_call(
        body, name="fox_fwd", grid=(N_HEADS // 2, n_q),
        out_shape=(jax.ShapeDtypeStruct((t_len, GROUP_W), BF), jax.ShapeDtypeStruct((t_len, GROUP_W), F32),
                   jax.ShapeDtypeStruct((N_HEADS // 2, 2, t_len), F32))
        + tuple(jax.ShapeDtypeStruct((4,) + s.shape, s.dtype) for s in shards),
        in_specs=[pl.BlockSpec((tg, 2 * LANES), lambda hp, i: (i, hp)),
                  pl.BlockSpec((t_len, 2 * LANES), lambda hp, i: (0, hp)),
                  pl.BlockSpec((t_len, LANES), lambda hp, i: (0, v_col + hp))] + [ANY] * n_w,
        out_specs=(pl.BlockSpec((tg, LANES), lambda hp, i: (i, hp)), pl.BlockSpec((tg, LANES), lambda hp, i: (i, hp)),
                   pl.BlockSpec((1, 2, tg), lambda hp, i: (hp, 0, i))) + (ANY,) * n_w,
        scratch_shapes=[pltpu.VMEM((LANES, t_len), BF)] + _gather_scratch(n_w),
        compiler_params=_cparams(("arbitrary", "arbitrary")),
    )(q_aug, k_aug, proj, *shards)


def _softmax_rows(s):
    p = jnp.exp(s - jnp.max(s, axis=-1, keepdims=True))
    return p / jnp.sum(p, axis=-1, keepdims=True)


def _attn_out_xattn_fwd(x, mix_r, mix_f, w_out, g_xattn, w_xq, g_xq, kn, v, w_xo):
    t_len = x.shape[0]
    tm = min(ROW_TILE, t_len)

    def body(x_ref, mr_ref, mf_ref, wo_ref, g_ref, wq_ref, gq_ref, kn_ref, v_ref, wxo_ref,
             h1_ref, hn_ref, qx_ref, o_ref, h2_ref):
        h1 = x_ref[...] + jnp.dot(mr_ref[...], wo_ref[:GROUP_W, :], preferred_element_type=F32) \
            + jnp.dot(mf_ref[...], wo_ref[GROUP_W:, :], preferred_element_type=F32)
        h1_ref[...] = h1
        hn = _rms_fwd(h1, g_ref[...]).astype(BF)
        hn_ref[...] = hn
        qx = jnp.dot(hn, wq_ref[...], preferred_element_type=F32).astype(BF)
        qx_ref[...] = qx
        sls = [slice(h * XHD, (h + 1) * XHD) for h in range(N_XH)]
        qns = [_rms_fwd(qx[:, sl].astype(F32), gq_ref[...]).astype(BF) for sl in sls]
        logits = [_dot_nt(qn, kn_ref[:, sl]) * (XHD ** -0.5) for qn, sl in zip(qns, sls)]
        ps = [_softmax_rows(s).astype(BF) for s in logits]
        for p, sl in zip(ps, sls):
            o_ref[:, sl] = jnp.dot(p, v_ref[:, sl], preferred_element_type=F32).astype(BF)
        h2_ref[...] = h1 + jnp.dot(o_ref[...], wxo_ref[...], preferred_element_type=F32)

    row_spec = lambda w: pl.BlockSpec((tm, w), lambda i: (i, 0))
    full = lambda a: pl.BlockSpec(a.shape, lambda i: (0,) * a.ndim)
    return pl.pallas_call(
        body, name="attn_out_xattn_fwd", grid=(t_len // tm,),
        out_shape=(jax.ShapeDtypeStruct((t_len, D_MODEL), F32), jax.ShapeDtypeStruct((t_len, D_MODEL), BF),
                   jax.ShapeDtypeStruct((t_len, D_MODEL), BF), jax.ShapeDtypeStruct((t_len, D_MODEL), BF),
                   jax.ShapeDtypeStruct((t_len, D_MODEL), F32)),
        in_specs=[row_spec(D_MODEL), row_spec(GROUP_W), row_spec(GROUP_W), full(w_out), full(g_xattn), full(w_xq), full(g_xq),
                  full(kn), full(v), full(w_xo)],
        out_specs=(row_spec(D_MODEL),) * 5,
        compiler_params=_cparams(("arbitrary",)),
    )(x, mix_r, mix_f, w_out, g_xattn, w_xq, g_xq, kn, v, w_xo)


def _ffn_loss_fwd(h2, g_ffn, w_gate, w_up, w_down, target):
    t_len = h2.shape[0]
    tm = min(ROW_TILE, t_len)

    def body(h2_ref, g_ref, wg_ref, wu_ref, wd_ref, tgt_ref, hn_ref, gate_ref, up_ref, act_ref, dh3_ref, loss_ref):
        @pl.when(pl.program_id(0) == 0)
        def _():
            loss_ref[...] = jnp.zeros_like(loss_ref)

        h2v = h2_ref[...]
        hn = _rms_fwd(h2v, g_ref[...]).astype(BF)
        hn_ref[...] = hn
        gate = _dot_nt(hn, wg_ref[...])
        up = _dot_nt(hn, wu_ref[...])
        gate_ref[...] = gate.astype(BF)
        up_ref[...] = up.astype(BF)
        act = (gate * _sigmoid(gate) * up).astype(BF)
        act_ref[...] = act
        diff = h2v + jnp.dot(act, wd_ref[...], preferred_element_type=F32) - tgt_ref[...]
        dh3_ref[...] = diff * (1.0 / D_MODEL)
        per_row = jnp.sum(diff * diff, axis=-1, keepdims=True) * (1.0 / D_MODEL)
        loss_ref[...] += 0.5 * jnp.sum(per_row, axis=0, keepdims=True)

    row_spec = lambda w: pl.BlockSpec((tm, w), lambda i: (i, 0))
    full = lambda a: pl.BlockSpec(a.shape, lambda i: (0,) * a.ndim, pipeline_mode=pl.Buffered(1))
    return pl.pallas_call(
        body, name="ffn_loss_fwd", grid=(t_len // tm,),
        out_shape=(jax.ShapeDtypeStruct((t_len, D_MODEL), BF), jax.ShapeDtypeStruct((t_len, D_FF), BF),
                   jax.ShapeDtypeStruct((t_len, D_FF), BF), jax.ShapeDtypeStruct((t_len, D_FF), BF),
                   jax.ShapeDtypeStruct((t_len, D_MODEL), F32), jax.ShapeDtypeStruct((8, LANES), F32)),
        in_specs=[row_spec(D_MODEL), full(g_ffn), full(w_gate), full(w_up), full(w_down), row_spec(D_MODEL)],
        out_specs=(row_spec(D_MODEL), row_spec(D_FF), row_spec(D_FF), row_spec(D_FF), row_spec(D_MODEL),
                   pl.BlockSpec((8, LANES), lambda i: (0, 0))),
        compiler_params=_cparams(("arbitrary",)),
    )(h2, g_ffn, w_gate, w_up, w_down, target)


def _ffn_bwd(dh3, gate, up, h2, g_ffn, w_gate, w_up, w_down):
    t_len = h2.shape[0]
    tm = min(ROW_TILE, t_len)

    def body(dh3_ref, gate_ref, up_ref, h2_ref, g_ref, wg_ref, wu_ref, wd_ref, dgate_ref, dup_ref, dh2_ref, dg_ref):
        @pl.when(pl.program_id(0) == 0)
        def _():
            dg_ref[...] = jnp.zeros_like(dg_ref)

        dh3v = dh3_ref[...]
        dact = _dot_nt(dh3v, wd_ref[...])
        g = gate_ref[...].astype(F32)
        sg = _sigmoid(g)
        dup = (dact * (g * sg)).astype(BF)
        dgate = (dact * up_ref[...].astype(F32) * (sg * (1.0 + g * (1.0 - sg)))).astype(BF)
        dup_ref[...] = dup
        dgate_ref[...] = dgate
        dhn = jnp.dot(dgate, wg_ref[...], preferred_element_type=F32) + jnp.dot(dup, wu_ref[...], preferred_element_type=F32)
        dx, dg = _rms_bwd(h2_ref[...], g_ref[...], dhn)
        dh2_ref[...] = dh3v + dx
        dg_ref[...] += dg

    row_spec = lambda w: pl.BlockSpec((tm, w), lambda i: (i, 0))
    full = lambda a: pl.BlockSpec(a.shape, lambda i: (0,) * a.ndim, pipeline_mode=pl.Buffered(1))
    return pl.pallas_call(
        body, name="ffn_bwd", grid=(t_len // tm,),
        out_shape=(jax.ShapeDtypeStruct((t_len, D_FF), BF), jax.ShapeDtypeStruct((t_len, D_FF), BF),
                   jax.ShapeDtypeStruct((t_len, D_MODEL), F32), jax.ShapeDtypeStruct((1, D_MODEL), F32)),
        in_specs=[row_spec(D_MODEL), row_spec(D_FF), row_spec(D_FF), row_spec(D_MODEL), full(g_ffn), full(w_gate), full(w_up),
                  full(w_down)],
        out_specs=(row_spec(D_FF), row_spec(D_FF), row_spec(D_MODEL), pl.BlockSpec((1, D_MODEL), lambda i: (0, 0))),
        compiler_params=_cparams(("arbitrary",)),
    )(dh3, gate, up, h2, g_ffn, w_gate, w_up, w_down)


def _attn_out_xattn_bwd(dh2, h1, qx, kn, v, w_xo, w_xq, w_out, g_xattn, g_xq):
    t_len = h1.shape[0]
    tm = min(ROW_TILE, t_len)
    m_tok = kn.shape[0]

    def body(dh2_ref, h1_ref, qx_ref, kn_ref, v_ref, wxo_ref, wq_ref, wo_ref, g_ref, gq_ref,
             dqx_ref, dh1_ref, dmr_ref, dmf_ref, dkn_ref, dv_ref, dg_ref, dgq_ref, dqx_scr):
        @pl.when(pl.program_id(0) == 0)
        def _():
            dkn_ref[...] = jnp.zeros_like(dkn_ref)
            dv_ref[...] = jnp.zeros_like(dv_ref)
            dg_ref[...] = jnp.zeros_like(dg_ref)
            dgq_ref[...] = jnp.zeros_like(dgq_ref)

        dh2v = dh2_ref[...]
        do = _dot_nt(dh2v, wxo_ref[...])
        gq = gq_ref[...]
        sls = [slice(h * XHD, (h + 1) * XHD) for h in range(N_XH)]
        qraws = [qx_ref[:, sl].astype(F32) for sl in sls]
        qns = [_rms_fwd(qraw, gq).astype(BF) for qraw in qraws]
        dohs = [do[:, sl].astype(BF) for sl in sls]
        logits = [_dot_nt(qn, kn_ref[:, sl]) * (XHD ** -0.5) for qn, sl in zip(qns, sls)]
        dps = [_dot_nt(doh, v_ref[:, sl]) for doh, sl in zip(dohs, sls)]
        ps = [_softmax_rows(s) for s in logits]
        dss = [(p * (dp - jnp.sum(dp * p, axis=-1, keepdims=True)) * (XHD ** -0.5)).astype(BF) for p, dp in zip(ps, dps)]
        dqns = []
        for h, sl in enumerate(sls):
            dv_ref[:, sl] += _dot_tn(ps[h], dohs[h])
            dqns.append(jnp.dot(dss[h], kn_ref[:, sl], preferred_element_type=F32))
            dkn_ref[:, sl] += _dot_tn(dss[h], qns[h])
        dgq = jnp.zeros((1, XHD), F32)
        for h, sl in enumerate(sls):
            dx, dg_h = _rms_bwd(qraws[h], gq, dqns[h])
            dgq = dgq + dg_h
            dqx_scr[:, sl] = dx.astype(BF)
        dgq_ref[...] += dgq
        dqx = dqx_scr[...]
        dqx_ref[...] = dqx
        dhn = _dot_nt(dqx, wq_ref[...])
        dx, dg = _rms_bwd(h1_ref[...], g_ref[...], dhn)
        dg_ref[...] += dg
        dh1 = dh2v + dx
        dh1_ref[...] = dh1
        dmix = _dot_nt(dh1, wo_ref[...])
        dmr_ref[...] = dmix[:, :GROUP_W]
        dmf_ref[...] = dmix[:, GROUP_W:].astype(BF)

    row_spec = lambda w: pl.BlockSpec((tm, w), lambda i: (i, 0))
    full = lambda a: pl.BlockSpec(a.shape, lambda i: (0,) * a.ndim)
    acc = lambda r, c: pl.BlockSpec((r, c), lambda i: (0, 0))
    return pl.pallas_call(
        body, name="attn_out_xattn_bwd", grid=(t_len // tm,),
        out_shape=(jax.ShapeDtypeStruct((t_len, D_MODEL), BF), jax.ShapeDtypeStruct((t_len, D_MODEL), F32),
                   jax.ShapeDtypeStruct((t_len, GROUP_W), F32), jax.ShapeDtypeStruct((t_len, GROUP_W), BF),
                   jax.ShapeDtypeStruct((m_tok, D_MODEL), F32), jax.ShapeDtypeStruct((m_tok, D_MODEL), F32),
                   jax.ShapeDtypeStruct((1, D_MODEL), F32), jax.ShapeDtypeStruct((1, XHD), F32)),
        in_specs=[row_spec(D_MODEL), row_spec(D_MODEL), row_spec(D_MODEL), full(kn), full(v), full(w_xo), full(w_xq), full(w_out),
                  full(g_xattn), full(g_xq)],
        out_specs=(row_spec(D_MODEL), row_spec(D_MODEL), row_spec(GROUP_W), row_spec(GROUP_W), acc(m_tok, D_MODEL),
                   acc(m_tok, D_MODEL), acc(1, D_MODEL), acc(1, XHD)),
        scratch_shapes=[pltpu.VMEM((tm, D_MODEL), BF)],
        compiler_params=_cparams(("arbitrary",)),
    )(dh2, h1, qx, kn, v, w_xo, w_xq, w_out, g_xattn, g_xq)


def _mem_kv_bwd(dkn, dv, kraw, mem, memn, g_mem, g_xk, w_xkv):
    m_tok = mem.shape[0]

    def body(dkn_ref, dv_ref, kraw_ref, mem_ref, memn_ref, gm_ref, gk_ref, w_ref, dw_ref, dgm_ref, dgk_ref, dkv_scr):
        gk = gk_ref[...]
        dgk = jnp.zeros((1, XHD), F32)
        for h in range(N_XH):
            sl = slice(h * XHD, (h + 1) * XHD)
            dx, dg_h = _rms_bwd(kraw_ref[:, sl], gk, dkn_ref[:, sl])
            dgk = dgk + dg_h
            dkv_scr[:, sl] = dx.astype(BF)
        dgk_ref[...] = dgk
        dkv_scr[:, D_MODEL:] = dv_ref[...].astype(BF)
        dkv = dkv_scr[...]
        dw_ref[...] = _dot_tn(memn_ref[...], dkv)
        dmemn = _dot_nt(dkv, w_ref[...])
        mem_v = mem_ref[...]
        r = lax.rsqrt(jnp.mean(mem_v * mem_v, axis=-1, keepdims=True) + EPS)
        dgm_ref[...] = jnp.sum(dmemn * mem_v * r, axis=0, keepdims=True)

    return pl.pallas_call(
        body, name="mem_kv_bwd",
        out_shape=(jax.ShapeDtypeStruct((D_MODEL, 2 * D_MODEL), F32), jax.ShapeDtypeStruct((1, D_MODEL), F32),
                   jax.ShapeDtypeStruct((1, XHD), F32)),
        in_specs=[VMEM_SPEC] * 8, out_specs=(VMEM_SPEC,) * 3,
        scratch_shapes=[pltpu.VMEM((m_tok, 2 * D_MODEL), BF)],
        compiler_params=_cparams(),
    )(dkn, dv, kraw, mem, memn, g_mem, g_xk, w_xkv)


def _fox_bwd(q_aug, k_aug, proj, dmf, o32, lse, sums):
    t_len = q_aug.shape[0]
    tb = min(ATT_BLOCK, t_len)
    n_b = t_len // tb
    nsub = 2 if n_b >= 2 else 1
    tg = nsub * tb
    n_g = t_len // tg
    v_col = 6 * GROUP_W // LANES
    n_w = len(sums)
    n_steps = (N_HEADS // 2) * n_g

    def body(*refs):
        k_ref, v_ref, q_ref, do_ref, o_ref, lse_ref = refs[:6]
        dq_ref, dk_ref, dv_ref, df_ref = refs[6 + n_w:10 + n_w]
        delta = refs[10 + 2 * n_w]
        comm = (refs[6:6 + n_w], refs[10 + n_w:10 + 2 * n_w]) + tuple(refs[11 + 2 * n_w:])
        j = pl.program_id(1)
        step = pl.program_id(0) * n_g + j

        @pl.when(step == 0)
        def _():
            _scatter_phase(0, *comm)

        @pl.when(j == 0)
        def _():
            dq_ref[...] = jnp.zeros_like(dq_ref)
            dd = do_ref[...].astype(F32) * o_ref[...]
            hrow = lax.broadcasted_iota(jnp.int32, (8, LANES), 0)
            lane = lax.broadcasted_iota(jnp.int32, (8, LANES), 1)
            ind = ((lane // HEAD_DIM) == hrow).astype(BF)
            delta[...] = _dot_nt_exact(ind, dd)

        k2, v2 = k_ref[...], v_ref[...]
        chains = [(u, hh) for u in range(nsub) for hh in range(2)]
        ks = {(u, hh): k2[u * tb:(u + 1) * tb, hh * LANES:(hh + 1) * LANES] for u, hh in chains}
        vs = {(u, hh): v2[u * tb:(u + 1) * tb, hh * HEAD_DIM:(hh + 1) * HEAD_DIM] for u, hh in chains}

        def block(i, carry, which, masked):
            rows = pl.ds(pl.multiple_of(i * tb, tb), tb)
            q2 = q_ref[rows, :]
            do2 = do_ref[rows, :]
            qs = [q2[:, hh * LANES:(hh + 1) * LANES] for hh in range(2)]
            dos = [do2[:, hh * HEAD_DIM:(hh + 1) * HEAD_DIM] for hh in range(2)]
            ss = {ch: _dot_nt(ks[ch], qs[ch[1]]) for ch in which}
            dps = {ch: _dot_nt(vs[ch], dos[ch[1]]) for ch in which}
            pts, dsts, dfs = {}, {}, {}
            for ch in which:
                hh = ch[1]
                s_t = ss[ch]
                if ch in masked:
                    krow = lax.broadcasted_iota(jnp.int32, (tb, tb), 0)
                    qcol = lax.broadcasted_iota(jnp.int32, (tb, tb), 1)
                    s_t = jnp.where(qcol >= krow, s_t, NEG)
                p_t = jnp.exp2(s_t - lse_ref[0, hh:hh + 1, rows])
                pts[ch] = p_t.astype(BF)
                ds_t = p_t * (dps[ch] - delta[hh:hh + 1, rows])
                dsts[ch] = ds_t.astype(BF)
                dfs[ch] = jnp.sum(ds_t, axis=-1, keepdims=True)
            out = dict(carry)
            for ch in which:
                dk, dv, df = carry[ch]
                dv = dv + jnp.dot(pts[ch], dos[ch[1]], preferred_element_type=F32)
                dk = dk + jnp.dot(dsts[ch], qs[ch[1]], preferred_element_type=F32)
                out[ch] = (dk, dv, df - dfs[ch])
            for hh in range(2):
                parts_dq = [_dot_tn(dsts[ch], ks[ch])[:, :HEAD_DIM] for ch in which if ch[1] == hh]
                dq_ref[rows, hh * HEAD_DIM:(hh + 1) * HEAD_DIM] += sum(parts_dq[1:], parts_dq[0])
            return out

        init = {ch: (jnp.zeros((tb, LANES), F32), jnp.zeros((tb, HEAD_DIM), F32), jnp.zeros((tb, 1), F32)) for ch in chains}
        first = nsub * j
        carry = block(first, init, [(0, 0), (0, 1)], [(0, 0), (0, 1)])
        if nsub == 2:
            carry = block(first + 1, carry, chains, [(1, 0), (1, 1)])
        carry = lax.fori_loop(first + nsub, n_b, lambda i, c: block(i, c, chains, ()), carry)
        for u in range(nsub):
            rs = slice(u * tb, (u + 1) * tb)
            dk_ref[rs, :] = jnp.concatenate([carry[u, hh][0][:, :HEAD_DIM] for hh in range(2)], axis=-1) * LN2
            dv_ref[rs, :] = jnp.concatenate([carry[u, hh][1] for hh in range(2)], axis=-1)
            df_ref[0, rs, :] = jnp.concatenate([carry[u, hh][2] for hh in range(2)], axis=-1)

        @pl.when(step == n_steps - 1)
        def _():
            _scatter_phase(1, *comm)

    blk = lambda w, col0: pl.BlockSpec((tg, w), lambda hp, j: (j, col0 + hp))
    whole = lambda w: pl.BlockSpec((t_len, w), lambda hp, j: (0, hp))
    rows2 = pl.BlockSpec((1, 2, t_len), lambda hp, j: (hp, 0, 0))
    cols2 = pl.BlockSpec((1, tg, 2), lambda hp, j: (hp, j, 0))
    return pl.pallas_call(
        body, name="fox_bwd", grid=(N_HEADS // 2, n_g),
        out_shape=(jax.ShapeDtypeStruct((t_len, GROUP_W), F32), jax.ShapeDtypeStruct((t_len, GROUP_W), F32),
                   jax.ShapeDtypeStruct((t_len, GROUP_W), F32), jax.ShapeDtypeStruct((N_HEADS // 2, t_len, 2), F32))
        + _scatter_out_shapes(sums),
        in_specs=[blk(2 * LANES, 0), blk(LANES, v_col), whole(2 * LANES), whole(LANES), whole(LANES), rows2] + [ANY] * n_w,
        out_specs=(whole(LANES), blk(LANES, 0), blk(LANES, 0), cols2) + (ANY,) * n_w,
        scratch_shapes=[pltpu.VMEM((8, t_len), F32)] + _scatter_scratch(n_w),
        compiler_params=_cparams(("arbitrary", "arbitrary")),
    )(k_aug, proj, q_aug, dmf, o32, lse, *sums)


def _retention_bwd(dmr, raw, proj, g_ret, rq, rk, states, tables, parts):
    t_len = rq.shape[0]
    c = min(RET_BLOCK, t_len)
    n_b = t_len // c
    wdec, qdec, kdec, cdec = tables
    v_col, g_col = 2 * GROUP_W // LANES, 3 * GROUP_W // LANES
    n_w = len(parts)
    n_steps = (N_HEADS // 2) * n_b

    def body(*refs):
        d_ref, raw_ref, rg_ref, g_ref, q_ref, k_ref, v_ref, st_ref, w_ref, wt_ref, qd_ref, kd_ref, cd_ref = refs[:13]
        dq_ref, dk_ref, dv_ref, drg_ref, dg_ref = refs[13 + n_w:18 + n_w]
        gstate = refs[18 + 2 * n_w]
        comm = (refs[13:13 + n_w], refs[18 + n_w:18 + 2 * n_w]) + tuple(refs[19 + 2 * n_w:])
        step = pl.program_id(0) * n_b + pl.program_id(1)

        @pl.when(step == 0)
        def _():
            _exchange_phase(0, *comm)

        @pl.when(pl.program_id(1) == 0)
        def _():
            gstate[...] = jnp.zeros_like(gstate)
            dg_ref[...] = jnp.zeros_like(dg_ref)

        d, raw_v, g = d_ref[...], raw_ref[...], g_ref[0]
        gate = rg_ref[...].astype(F32)
        xc = raw_v - _group_mean64(raw_v)
        r = lax.rsqrt(_group_mean64(xc * xc) + EPS)
        xh = xc * r
        sg = _sigmoid(gate)
        drg_ref[...] = d * (xh * g) * (sg * (1.0 + gate * (1.0 - sg)))
        dy = d * (gate * sg)
        dg_ref[0] += jnp.sum(dy * xh, axis=0, keepdims=True)
        dxh = dy * g
        do2 = r * (dxh - _group_mean64(dxh) - xh * _group_mean64(dxh * xh))
        q2, k2, v2 = q_ref[...], k_ref[...], v_ref[...]
        dqs, dks, dvs = [], [], []
        heads = [tuple(t[:, hh * HEAD_DIM:(hh + 1) * HEAD_DIM] for t in (q2, k2, v2, do2.astype(BF))) for hh in range(2)]
        firsts = [(_dot_nt(k, q) * wt_ref[hh], _dot_nt(do, v) * w_ref[hh], _dot_nt(v, do) * wt_ref[hh])
                  for hh, (q, k, v, do) in enumerate(heads)]
        for hh, (q, k, v, do) in enumerate(heads):
            a_t, dm, dm_t = firsts[hh]
            sp, gs = st_ref[0, 0, hh], gstate[hh]
            qd = q.astype(F32) * qd_ref[hh]
            kd = k.astype(F32) * kd_ref[hh]
            dqs.append(_dot(dm, k) + _dot_nt(do, sp) * qd_ref[hh])
            dks.append(_dot(dm_t, q) + _dot_nt(v, gs) * kd_ref[hh])
            dvs.append(_dot(a_t, do) + _dot(kd, gs))
            gstate[hh] = gs * cd_ref[hh] + _dot_tn(qd, do)
        dq_ref[...] = jnp.concatenate(dqs, axis=-1)
        dk_ref[...] = jnp.concatenate(dks, axis=-1)
        dv_ref[...] = jnp.concatenate(dvs, axis=-1)

        @pl.when(step == n_steps - 1)
        def _():
            _exchange_phase(1, *comm)

    blk = lambda col0: pl.BlockSpec((c, LANES), lambda hp, i: (n_b - 1 - i, col0 + hp))
    tab = lambda a: pl.BlockSpec((2,) + a.shape[1:], lambda hp, i: (hp, 0, 0))
    gspec = pl.BlockSpec((1, 1, LANES), lambda hp, i: (hp, 0, 0))
    return pl.pallas_call(
        body, name="retention_bwd", grid=(N_HEADS // 2, n_b),
        out_shape=(jax.ShapeDtypeStruct((t_len, GROUP_W), F32),) * 4 + (jax.ShapeDtypeStruct((N_HEADS // 2, 1, LANES), F32),)
        + _exchange_out_shapes(parts),
        in_specs=[blk(0), blk(0), blk(g_col), gspec, blk(0), blk(0), blk(v_col),
                  pl.BlockSpec((1, 1, 2, HEAD_DIM, HEAD_DIM), lambda hp, i: (hp, n_b - 1 - i, 0, 0, 0)),
                  tab(wdec), tab(wdec), tab(qdec), tab(kdec), tab(cdec)] + [ANY] * n_w,
        out_specs=(blk(0), blk(0), blk(0), blk(0), gspec) + (ANY,) * n_w,
        scratch_shapes=[pltpu.VMEM((2, HEAD_DIM, HEAD_DIM), F32)] + _exchange_scratch(n_w),
        compiler_params=_cparams(("arbitrary", "arbitrary")),
    )(dmr, raw, proj, g_ret, rq, rk, proj, states, wdec, jnp.transpose(wdec, (0, 2, 1)), qdec, kdec, cdec, *parts)


def _in_proj_bwd(x, g_mix, dh1, dq_r, dk_r, dv_r, drg, dq_f, dk_f, dv_f, df_col, proj, z, cos_t, sin_t, gq_t, gk_t, w_in_t):
    t_len = x.shape[0]
    tm = min(ROW_TILE, t_len)
    n_t = t_len // tm

    def body(x_ref, g_ref, dh1_ref, dqr_ref, dkr_ref, dvr_ref, drg_ref, dqf_ref, dkf_ref, dvf_ref, df_ref, fq_ref, fk_ref, z_ref,
             cos_ref, sin_ref, gq_ref, gk_ref, wm_ref, wf_ref,
             dproj_ref, dz_ref, dx_ref, dg_ref, dgq_ref, dgk_ref, db_ref, carry, gq_acc, gk_acc):
        i = pl.program_id(0)

        @pl.when(i == 0)
        def _():
            carry[...] = jnp.zeros_like(carry)
            gq_acc[...] = jnp.zeros_like(gq_acc)
            gk_acc[...] = jnp.zeros_like(gk_acc)
            dg_ref[...] = jnp.zeros_like(dg_ref)
            db_ref[...] = jnp.zeros_like(db_ref)

        c, s = cos_ref[...], sin_ref[...]
        gq, gk = gq_ref[...], gk_ref[...]
        dgq = jnp.zeros((1, LANES), F32)
        dgk = jnp.zeros((1, LANES), F32)
        for sl in _chunks(GROUP_W):
            dy = dqr_ref[:, sl] * 0.125
            dproj_ref[:, sl] = (dy * c + _swap32(dy * s)).astype(BF)
            dy = dkr_ref[:, sl]
            dproj_ref[:, GROUP_W + sl.start:GROUP_W + sl.stop] = (dy * c + _swap32(dy * s)).astype(BF)
            dproj_ref[:, 2 * GROUP_W + sl.start:2 * GROUP_W + sl.stop] = dvr_ref[:, sl].astype(BF)
            dproj_ref[:, 3 * GROUP_W + sl.start:3 * GROUP_W + sl.stop] = drg_ref[:, sl].astype(BF)
            for src, dsrc, gain, off in ((fq_ref, dqf_ref, gq, 4), (fk_ref, dkf_ref, gk, 5)):
                xr = src[:, sl].astype(F32)
                r = lax.rsqrt(_group_mean64(xr * xr) + EPS)
                xh = xr * r
                dy = dsrc[:, sl] * (0.125 if off == 4 else 1.0)
                dgs = jnp.sum(dy * xh, axis=0, keepdims=True)
                if off == 4:
                    dgq = dgq + dgs
                else:
                    dgk = dgk + dgs
                dxh = dy * gain
                dproj_ref[:, off * GROUP_W + sl.start:off * GROUP_W + sl.stop] = \
                    (r * (dxh - xh * _group_mean64(dxh * xh))).astype(BF)
            dproj_ref[:, 6 * GROUP_W + sl.start:6 * GROUP_W + sl.stop] = dvf_ref[:, sl].astype(BF)
        gq_acc[...] += dgq
        gk_acc[...] += dgk
        row = lax.broadcasted_iota(jnp.int32, (tm, tm), 0)
        col = lax.broadcasted_iota(jnp.int32, (tm, tm), 1)
        dlf = _dot_exact((col >= row).astype(BF), df_ref[...]) + carry[0:1, :]
        carry[...] = jnp.broadcast_to(dlf[0:1, :], carry.shape)
        lane = lax.broadcasted_iota(jnp.int32, (tm, LANES), 1)
        dz = jnp.where(lane < N_HEADS, dlf / (1.0 + jnp.exp(z_ref[...])), 0.0)
        db_ref[...] += jnp.sum(dz, axis=0, keepdims=True)
        dz_bf = dz.astype(BF)
        dz_ref[...] = dz_bf
        dn1 = jnp.dot(dz_bf, wf_ref[...], preferred_element_type=F32)
        for sec in range(MAIN_W // GROUP_W):
            sl = slice(sec * GROUP_W, (sec + 1) * GROUP_W)
            dn1 = dn1 + jnp.dot(dproj_ref[:, sl], wm_ref[sl, :], preferred_element_type=F32)
        dx, dg = _rms_bwd(x_ref[...], g_ref[...], dn1)
        dx_ref[...] = dh1_ref[...] + dx
        dg_ref[...] += dg

        @pl.when(i == n_t - 1)
        def _():
            dgq_ref[...] = gq_acc[:, :HEAD_DIM] + gq_acc[:, HEAD_DIM:]
            dgk_ref[...] = gk_acc[:, :HEAD_DIM] + gk_acc[:, HEAD_DIM:]

    row_spec = lambda w, col=0: pl.BlockSpec((tm, w), lambda i: (n_t - 1 - i, col))
    full = lambda a: pl.BlockSpec(a.shape, lambda i: (0,) * a.ndim)
    acc = lambda r, c: pl.BlockSpec((r, c), lambda i: (0, 0))
    return pl.pallas_call(
        body, name="in_proj_bwd", grid=(n_t,),
        out_shape=(jax.ShapeDtypeStruct((t_len, MAIN_W), BF), jax.ShapeDtypeStruct((t_len, LANES), BF),
                   jax.ShapeDtypeStruct((t_len, D_MODEL), F32), jax.ShapeDtypeStruct((1, D_MODEL), F32),
                   jax.ShapeDtypeStruct((1, HEAD_DIM), F32), jax.ShapeDtypeStruct((1, HEAD_DIM), F32),
                   jax.ShapeDtypeStruct((1, LANES), F32)),
        in_specs=[row_spec(D_MODEL), full(g_mix), row_spec(D_MODEL)] + [row_spec(GROUP_W)] * 7
        + [row_spec(LANES), row_spec(GROUP_W, 4), row_spec(GROUP_W, 5), row_spec(LANES), row_spec(LANES), row_spec(LANES),
           full(gq_t), full(gk_t), *_w_in_specs()],
        out_specs=(row_spec(MAIN_W), row_spec(LANES), row_spec(D_MODEL), acc(1, D_MODEL), acc(1, HEAD_DIM), acc(1, HEAD_DIM),
                   acc(1, LANES)),
        scratch_shapes=[pltpu.VMEM((8, LANES), F32), pltpu.VMEM((1, LANES), F32), pltpu.VMEM((1, LANES), F32)],
        compiler_params=_cparams(("arbitrary",)),
    )(x, g_mix, dh1, dq_r, dk_r, dv_r, drg, dq_f, dk_f, dv_f, df_col, proj, proj, z, cos_t, sin_t, gq_t, gk_t, w_in_t, w_in_t)


def _matmul_tn(a, b, name, bk=512):
    t_len, m = a.shape
    n = b.shape[1]
    bm = m if m <= TN_MAX_ROWS else m // 2
    bk = min(bk, t_len)

    def body(a_ref, b_ref, o_ref):
        @pl.when(pl.program_id(1) == 0)
        def _():
            o_ref[...] = jnp.zeros_like(o_ref)

        o_ref[...] += _dot_tn(a_ref[...], b_ref[...])

    return pl.pallas_call(
        body, name=name, grid=(m // bm, t_len // bk),
        out_shape=jax.ShapeDtypeStruct((m, n), F32),
        in_specs=[pl.BlockSpec((bk, bm), lambda i, k: (k, i)), pl.BlockSpec((bk, n), lambda i, k: (k, 0))],
        out_specs=pl.BlockSpec((bm, n), lambda i, k: (i, 0)),
        compiler_params=_cparams(("arbitrary", "arbitrary")),
    )(a, b)


def _place():
    x, y, c = lax.axis_index("x"), lax.axis_index("y"), lax.axis_index("c")
    chips = [(1 - x, y), (x, 1 - y), (1 - x, 1 - y)]
    return x, y, c, chips


def _row_chunks(rows, limit):
    step = max(d for d in range(16, min(rows, limit) + 1, 16) if rows % d == 0)
    return [slice(i, i + step) for i in range(0, rows, step)]


ICI_CHUNK_ROWS = 128
D2D_CHUNK_ROWS = 64


def _gather_phase(phase, ins, outs, send_sems, recv_sems):
    x, y, c, chips = _place()
    me_chip = 2 * x + y
    sibling = (x, y, 1 - c)

    def copy(w, k, slot, half, to, rows=slice(None), src=None):
        dst = outs[w].at[slot, half, rows]
        return pltpu.make_async_remote_copy(src_ref=dst if src is None else src, dst_ref=dst,
                                            send_sem=send_sems.at[w, k], recv_sem=recv_sems.at[w, k],
                                            device_id=to, device_id_type=MESH)

    for w in range(len(ins)):
        for j, (px, py) in enumerate(chips):
            if phase == 0:
                for rows in _row_chunks(ins[w].shape[1], ICI_CHUNK_ROWS):
                    copy(w, j, me_chip, c, (px, py, c), rows, src=ins[w].at[c, rows]).start()
            elif phase == 1:
                copy(w, j, 2 * px + py, c, (x, y, c)).wait_recv()
                for rows in _row_chunks(ins[w].shape[1], D2D_CHUNK_ROWS):
                    copy(w, 3 + j, 2 * px + py, c, sibling, rows).start()
            else:
                copy(w, 3 + j, 2 * px + py, 1 - c, (x, y, c)).wait_recv()
                copy(w, j, me_chip, c, (px, py, c), src=ins[w].at[c]).wait_send()
                copy(w, 3 + j, 2 * px + py, c, sibling).wait_send()


def _gather_scratch(n_w):
    return [pltpu.SemaphoreType.DMA((n_w, 6)), pltpu.SemaphoreType.DMA((n_w, 6))]


def _all_gather_weights(shards):
    n_w = len(shards)

    def body(*refs):
        for phase in range(3):
            _gather_phase(phase, refs[:n_w], refs[n_w:2 * n_w], *refs[2 * n_w:])

    return pl.pallas_call(
        body, name="all_gather_weights",
        out_shape=tuple(jax.ShapeDtypeStruct((4,) + s.shape, s.dtype) for s in shards),
        in_specs=[ANY] * n_w, out_specs=(ANY,) * n_w, scratch_shapes=_gather_scratch(n_w),
    )(*shards)


def _exchange_phase(phase, ins, theirs, send_sems, recv_sems):
    x, y, c, _ = _place()

    def remote(w, k=slice(None), rows=slice(None)):
        return pltpu.make_async_remote_copy(src_ref=ins[w].at[k, 1 - c, rows], dst_ref=theirs[w].at[k, rows],
                                            send_sem=send_sems.at[w], recv_sem=recv_sems.at[w], device_id=(x, y, 1 - c),
                                            device_id_type=MESH)

    for w in range(len(ins)):
        if phase == 0:
            for k in range(4):
                for rows in _row_chunks(ins[w].shape[2], D2D_CHUNK_ROWS):
                    remote(w, k, rows).start()
        else:
            remote(w).wait()


def _exchange_scratch(n_w):
    return [pltpu.SemaphoreType.DMA((n_w,)), pltpu.SemaphoreType.DMA((n_w,))]


def _exchange_out_shapes(grads):
    return tuple(jax.ShapeDtypeStruct((4,) + g.shape[2:], g.dtype) for g in grads)


def _exchange_core_halves(grads):
    n_w = len(grads)

    def body(*refs):
        for phase in range(2):
            _exchange_phase(phase, refs[:n_w], refs[n_w:2 * n_w], *refs[2 * n_w:])

    return pl.pallas_call(
        body, name="exchange_core_halves", out_shape=_exchange_out_shapes(grads),
        in_specs=[ANY] * n_w, out_specs=(ANY,) * n_w, scratch_shapes=_exchange_scratch(n_w),
    )(*grads)


def _add_pairs(part, theirs, name):
    _, _, r, c = part.shape
    rb = 32 if r % 32 == 0 else r

    def body(a_ref, b_ref, own_ref, ob_ref):
        my_chip = 2 * lax.axis_index("x") + lax.axis_index("y")
        ob_ref[...] = (a_ref[...] + b_ref[...]).astype(BF)
        own_ref[...] = a_ref[my_chip] + b_ref[my_chip]

    spec = pl.BlockSpec((4, rb, c), lambda i: (0, i, 0))
    return pl.pallas_call(
        body, name=name, grid=(r // rb,),
        out_shape=(jax.ShapeDtypeStruct((r, c), F32), jax.ShapeDtypeStruct((4, r, c), BF)),
        in_specs=[pl.BlockSpec((4, None, rb, c), lambda i: (0, lax.axis_index("c"), i, 0)), spec],
        out_specs=(pl.BlockSpec((rb, c), lambda i: (i, 0)), spec), compiler_params=_cparams(("arbitrary",)),
    )(part, theirs)


def _scatter_phase(phase, bfs, got, send_sems, recv_sems):
    x, y, c, chips = _place()

    def remote(w, j, px, py, rows=slice(None)):
        return pltpu.make_async_remote_copy(src_ref=bfs[w].at[2 * px + py, rows], dst_ref=got[w].at[j, rows],
                                            send_sem=send_sems.at[w, j], recv_sem=recv_sems.at[w, j], device_id=(px, py, c),
                                            device_id_type=MESH)

    for w in range(len(bfs)):
        for j, (px, py) in enumerate(chips):
            if phase == 0:
                for rows in _row_chunks(bfs[w].shape[1], ICI_CHUNK_ROWS):
                    remote(w, j, px, py, rows).start()
            else:
                remote(w, j, px, py).wait()


def _scatter_scratch(n_w):
    return [pltpu.SemaphoreType.DMA((n_w, 3)), pltpu.SemaphoreType.DMA((n_w, 3))]


def _scatter_out_shapes(sums_bf16):
    return tuple(jax.ShapeDtypeStruct((3,) + s.shape[1:], BF) for s in sums_bf16)


def _add_received(own, got, name):
    r, c = own.shape
    rb = 32 if r % 32 == 0 else r

    def body(o_ref, g_ref, out_ref):
        out_ref[...] = ((o_ref[...] + g_ref[0].astype(F32)) + g_ref[1].astype(F32)) + g_ref[2].astype(F32)

    return pl.pallas_call(
        body, name=name, grid=(r // rb,), out_shape=jax.ShapeDtypeStruct((r, c), F32),
        in_specs=[pl.BlockSpec((rb, c), lambda i: (i, 0)), pl.BlockSpec((3, rb, c), lambda i: (0, i, 0))],
        out_specs=pl.BlockSpec((rb, c), lambda i: (i, 0)), compiler_params=_cparams(("arbitrary",)),
    )(own, got)


def _share_with_sibling(halves):
    n_w = len(halves)

    def body(*refs):
        ins, outs = refs[:n_w], refs[n_w:2 * n_w]
        send_sems, recv_sems = refs[2 * n_w:]
        x, y, c, _ = _place()

        def remote(w, rows=slice(None)):
            return pltpu.make_async_remote_copy(src_ref=ins[w].at[rows], dst_ref=outs[w].at[c, rows], send_sem=send_sems.at[w],
                                                recv_sem=recv_sems.at[w], device_id=(x, y, 1 - c), device_id_type=MESH)

        for w in range(n_w):
            for rows in _row_chunks(ins[w].shape[0], D2D_CHUNK_ROWS):
                remote(w, rows).start()
        for w in range(n_w):
            remote(w).wait()

    return pl.pallas_call(
        body, name="share_with_sibling",
        out_shape=tuple(jax.ShapeDtypeStruct((2,) + h.shape, h.dtype) for h in halves),
        in_specs=[ANY] * n_w, out_specs=(ANY,) * n_w,
        scratch_shapes=[pltpu.SemaphoreType.DMA((n_w,)), pltpu.SemaphoreType.DMA((n_w,))],
    )(*halves)


def _all_reduce_small(pack):
    r, c = pack.shape

    def body(p_ref, out_ref, slots, send_sems, recv_sems):
        x, y, cc, _ = _place()
        me = 4 * x + 2 * y + cc
        slots[me] = p_ref[...]
        copies = []
        for k in range(1, 8):
            dx, dy, dc = (k >> 2) & 1, (k >> 1) & 1, k & 1
            to = (1 - x if dx else x, 1 - y if dy else y, 1 - cc if dc else cc)
            cp = pltpu.make_async_remote_copy(src_ref=p_ref, dst_ref=slots.at[me], send_sem=send_sems.at[k - 1],
                                              recv_sem=recv_sems.at[k - 1], device_id=to, device_id_type=MESH)
            cp.start()
            copies.append(cp)
        for cp in copies:
            cp.wait()
        total = slots[0]
        for d in range(1, 8):
            total = total + slots[d]
        out_ref[...] = total

    return pl.pallas_call(
        body, name="all_reduce_small", out_shape=jax.ShapeDtypeStruct((r, c), F32),
        in_specs=[VMEM_SPEC], out_specs=VMEM_SPEC,
        scratch_shapes=[pltpu.VMEM((8, r, c), F32), pltpu.SemaphoreType.DMA((7,)), pltpu.SemaphoreType.DMA((7,))],
    )(pack)


def _adamw_update(w_ref, g_ref, m_ref, v_ref, d_ref, nm_ref, nv_ref):
    gv = g_ref[...]
    nm = ADAM_B1 * m_ref[...] + (1.0 - ADAM_B1) * gv
    nv = ADAM_B2 * v_ref[...] + (1.0 - ADAM_B2) * (gv * gv)
    nm_ref[...] = nm
    nv_ref[...] = nv
    m_hat = nm / (1.0 - ADAM_B1 ** ADAM_STEP)
    v_hat = nv / (1.0 - ADAM_B2 ** ADAM_STEP)
    d_ref[...] = -ADAM_LR * (m_hat / (jnp.sqrt(v_hat) + ADAM_EPS) + ADAM_WD * w_ref[...])


def _adamw_many(ws, gs, ms, vs, sums):
    n_a, n_w = len(ws), len(sums)
    n_steps = ADAM_STEPS
    specs = [pl.BlockSpec((w.shape[0] // n_steps, w.shape[1]), lambda i: (i, 0)) for w in ws]

    def body(*refs):
        ins = refs[:4 * n_a]
        outs = refs[4 * n_a + n_w:7 * n_a + n_w]
        comm = (refs[4 * n_a:4 * n_a + n_w], refs[7 * n_a + n_w:7 * n_a + 2 * n_w]) + tuple(refs[7 * n_a + 2 * n_w:])
        step = pl.program_id(0)

        @pl.when(step == 0)
        def _():
            _scatter_phase(0, *comm)

        for a in range(n_a):
            _adamw_update(*(ins[k * n_a + a] for k in range(4)), *(outs[3 * a + k] for k in range(3)))

        @pl.when(step == n_steps - 1)
        def _():
            _scatter_phase(1, *comm)

    flat = pl.pallas_call(
        body, name="adamw_late", grid=(n_steps,),
        out_shape=tuple(jax.ShapeDtypeStruct(w.shape, F32) for w in ws for _ in range(3)) + _scatter_out_shapes(sums),
        in_specs=specs * 4 + [ANY] * n_w, out_specs=tuple(s for s in specs for _ in range(3)) + (ANY,) * n_w,
        scratch_shapes=_scatter_scratch(n_w), compiler_params=_cparams(("arbitrary",)),
    )(*ws, *gs, *ms, *vs, *sums)
    return [tuple(flat[3 * a:3 * a + 3]) for a in range(n_a)] + list(flat[3 * n_a:])


def _adamw(w, g, m, v, name):
    r, c = w.shape
    rb, cb = (64, c) if r % 64 == 0 else (r, LANES if (r % 8 and c % LANES == 0) else c)

    def body(*refs):
        _adamw_update(*refs)

    spec = pl.BlockSpec((rb, cb), lambda i, j: (i, j))
    return pl.pallas_call(
        body, name=name, grid=(r // rb, c // cb), out_shape=(jax.ShapeDtypeStruct((r, c), F32),) * 3,
        in_specs=[spec] * 4, out_specs=(spec,) * 3, compiler_params=_cparams(("arbitrary", "arbitrary")),
    )(w, g, m, v)


def _rope_tables(t_len):
    inv_freq = ROPE_BASE ** (-jnp.arange(0, HEAD_DIM, 2, dtype=F32) / HEAD_DIM)
    ang = jnp.arange(t_len, dtype=F32)[:, None] * inv_freq[None, :]
    cos, sin = jnp.cos(ang), jnp.sin(ang)
    cos_t = jnp.concatenate([cos, cos, cos, cos], axis=-1)
    sin_t = jnp.concatenate([-sin, sin, -sin, sin], axis=-1)
    return cos_t, sin_t


def _cols_to_shards(dw):
    r, n = dw.shape
    return jnp.transpose(dw.reshape(2, r // 2, 4, n // 4), (2, 0, 1, 3))


def _rows_to_shards(dw):
    r, n = dw.shape
    padded = _pad_rows(dw.reshape(4, r // 4, n))
    return padded.reshape(4, 2, padded.shape[1] // 2, n)


def _pad_lanes(a):
    extra = -a.shape[-1] % LANES
    return a if extra == 0 else jnp.pad(a, [(0, 0)] * (a.ndim - 1) + [(0, extra)])


def _pad_rows(a):
    rows = a.shape[-2]
    extra = 0 if rows % SHARD_ROW_ALIGN == 0 else -rows % SHARD_ROW_PAD
    return a if extra == 0 else jnp.pad(a, [(0, 0)] * (a.ndim - 2) + [(0, extra), (0, 0)])


def _pad_row(a, width=D_MODEL):
    a = a.reshape(1, -1)
    return jnp.pad(a, ((0, 0), (0, width - a.shape[1])))


def kernel(x, mem, g_mix, w_in, b_forget, g_ret_out, g_fox_q, g_fox_k, w_out, g_xattn, w_xq, w_xkv, g_mem, g_xq, g_xk, w_xo, g_ffn, w_gate, w_up, w_down, loss_target, m_g_mix, m_w_in, m_b_forget, m_g_ret_out, m_g_fox_q, m_g_fox_k, m_w_out, m_g_xattn, m_w_xq, m_w_xkv, m_g_mem, m_g_xq, m_g_xk, m_w_xo, m_g_ffn, m_w_gate, m_w_up, m_w_down, v_g_mix, v_w_in, v_b_forget, v_g_ret_out, v_g_fox_q, v_g_fox_k, v_w_out, v_g_xattn, v_w_xq, v_w_xkv, v_g_mem, v_g_xq, v_g_xk, v_w_xo, v_g_ffn, v_w_gate, v_w_up, v_w_down):
    big = {"w_in": (w_in, m_w_in, v_w_in), "w_out": (w_out, m_w_out, v_w_out), "w_xq": (w_xq, m_w_xq, v_w_xq),
           "w_xkv": (w_xkv, m_w_xkv, v_w_xkv), "w_xo": (w_xo, m_w_xo, v_w_xo), "w_gate": (w_gate, m_w_gate, v_w_gate),
           "w_up": (w_up, m_w_up, v_w_up), "w_down": (w_down, m_w_down, v_w_down)}
    for n in TRANSPOSED:
        big[n] = tuple(jnp.swapaxes(a, 1, 2) for a in big[n])
    shards = {}
    for n in big:
        w = _pad_rows(_pad_lanes(big[n][0][0].astype(BF)))
        shards[n] = w.reshape(2, w.shape[0] // 2, w.shape[1])
    sizes = {n: big[n][0].shape[1:] for n in big}
    w_in_full = _assemble_weight("w_in", _all_gather_weights([shards["w_in"]])[0], shards["w_in"], sizes["w_in"])
    small_w ={"g_mix": g_mix, "b_forget": b_forget, "g_ret_out": g_ret_out, "g_fox_q": g_fox_q, "g_fox_k": g_fox_k,
               "g_xattn": g_xattn, "g_mem": g_mem, "g_xq": g_xq, "g_xk": g_xk, "g_ffn": g_ffn}
    m_small = {"g_mix": m_g_mix, "b_forget": m_b_forget, "g_ret_out": m_g_ret_out, "g_fox_q": m_g_fox_q, "g_fox_k": m_g_fox_k,
               "g_xattn": m_g_xattn, "g_mem": m_g_mem, "g_xq": m_g_xq, "g_xk": m_g_xk, "g_ffn": m_g_ffn}
    v_small = {"g_mix": v_g_mix, "b_forget": v_b_forget, "g_ret_out": v_g_ret_out, "g_fox_q": v_g_fox_q, "g_fox_k": v_g_fox_k,
               "g_xattn": v_g_xattn, "g_mem": v_g_mem, "g_xq": v_g_xq, "g_xk": v_g_xk, "g_ffn": v_g_ffn}
    loss_part, grad_x, sums, got, in_bf, small_g = _local_step(x[0], mem[0], loss_target[0], w_in_full, shards, sizes, small_w)
    return _reduce_and_update(big, sums, got, in_bf, small_w, small_g, loss_part, grad_x, m_small, v_small)


def _assemble_weight(name, gathered, own, size):
    rows, width = size
    my_chip = 2 * lax.axis_index("x") + lax.axis_index("y")
    g = lax.dynamic_update_slice(gathered, own[None], (my_chip, 0, 0, 0))
    g = g.reshape(4, 2 * g.shape[2], g.shape[3])[:, :rows, :width]
    return jnp.transpose(g, (1, 0, 2)).reshape(rows, 4 * width) if name in COL_SHARDED else g.reshape(4 * rows, width)


def _shard_parts(names, dw):
    return [_pad_lanes(_cols_to_shards(dw[n]) if n in COL_SHARDED else _rows_to_shards(dw[n])) for n in names]


def _core_sums(names, parts, theirs):
    return [_add_pairs(p, t, f"core_sum_{n}") for n, p, t in zip(names, parts, theirs)]


def _local_step(xs, mems, tgt, w_in_full, shards, sizes, small_w):
    g_mix, b_forget, g_ret_out, g_fox_q, g_fox_k = (small_w[n] for n in ("g_mix", "b_forget", "g_ret_out", "g_fox_q", "g_fox_k"))
    g_xattn, g_mem, g_xq, g_xk, g_ffn = (small_w[n] for n in ("g_xattn", "g_mem", "g_xq", "g_xk", "g_ffn"))
    w_in_t = jnp.pad(w_in_full, ((0, MAIN_W + LANES - IN_W), (0, 0)))
    t_len = xs.shape[0]
    cos_t, sin_t = _rope_tables(t_len)
    tables = _decay_tables(min(RET_BLOCK, t_len))
    gq_t = jnp.concatenate([g_fox_q, g_fox_q], axis=-1)
    gk_t = jnp.concatenate([g_fox_k, g_fox_k], axis=-1)
    b_pad = _pad_row(b_forget, LANES)
    g_ret = g_ret_out.reshape(N_HEADS // 2, 1, LANES)

    n1, proj, rq, rk, q_aug, k_aug, z = _in_proj_fwd(xs, g_mix, w_in_t, b_pad, cos_t, sin_t, gq_t, gk_t)
    raw, mix_r, states = _retention_fwd(rq, rk, proj, g_ret, tables)
    mix_f, o32, lse, *gathered = _fox_fwd(q_aug, k_aug, proj, [shards[n] for n in LATE])
    full = {n: _assemble_weight(n, g, shards[n], sizes[n]) for n, g in zip(LATE, gathered)}
    memn, kraw, kn, vmem = _mem_kv_fwd(mems, g_mem, full["w_xkv"], g_xk)
    h1, hn2, qx, o_x, h2 = _attn_out_xattn_fwd(xs, mix_r, mix_f, full["w_out"], g_xattn, full["w_xq"], g_xq, kn, vmem, full["w_xo"])
    hn3, gate, up, act, dh3, loss_part = _ffn_loss_fwd(h2, g_ffn, full["w_gate"], full["w_up"], full["w_down"], tgt)

    dgate, dup, dh2, dg_ffn = _ffn_bwd(dh3, gate, up, h2, g_ffn, full["w_gate"], full["w_up"], full["w_down"])
    dqx, dh1, dmr, dmf, dkn, dvm, dg_xattn, dg_xq = _attn_out_xattn_bwd(dh2, h1, qx, kn, vmem, full["w_xo"], full["w_xq"],
                                                                      full["w_out"], g_xattn, g_xq)
    dw_xkv, dg_mem, dg_xk = _mem_kv_bwd(dkn, dvm, kraw, mems, memn, g_mem, g_xk, full["w_xkv"])
    dw = {
        "w_out": jnp.concatenate([_matmul_tn(mix_r, dh1, "dw_out_ret"), _matmul_tn(mix_f, dh1, "dw_out_fox")], axis=0),
        "w_xq": _matmul_tn(hn2, dqx, "dw_xq"),
        "w_xkv": dw_xkv,
        "w_xo": _matmul_tn(o_x, dh2, "dw_xo"),
        "w_gate": _matmul_tn(dgate, hn3, "dw_gate"),
        "w_up": _matmul_tn(dup, hn3, "dw_up"),
        "w_down": _matmul_tn(act, dh3, "dw_down"),
    }
    late_parts = _shard_parts(LATE, dw)
    dq_r, dk_r, dv_r, drg, dg_ret, *late_theirs = _retention_bwd(dmr, raw, proj, g_ret, rq, rk, states, tables, late_parts)
    late_sums = _core_sums(LATE, late_parts, late_theirs)
    dq_f, dk_f, dv_f, df, *late_got = _fox_bwd(q_aug, k_aug, proj, dmf, o32, lse, [s[1] for s in late_sums])
    df_col = jnp.pad(jnp.transpose(df, (1, 0, 2)).reshape(t_len, N_HEADS), ((0, 0), (0, LANES - N_HEADS)))
    dproj, dz, grad_x, dg_mix, dg_fq, dg_fk, db = _in_proj_bwd(xs, g_mix, dh1, dq_r, dk_r, dv_r, drg, dq_f, dk_f, dv_f, df_col,
                                                              proj, z, cos_t, sin_t, gq_t, gk_t, w_in_t)

    dw_in = jnp.concatenate([_matmul_tn(dproj, n1, "dw_in_main"), _matmul_tn(dz, n1, "dw_in_ff")[:IN_W - MAIN_W]], axis=0)
    in_parts = _shard_parts(("w_in",), {"w_in": dw_in})
    in_sums = _core_sums(("w_in",), in_parts, _exchange_core_halves(in_parts))
    sums = {n: s[0] for n, s in zip(("w_in",) + LATE, in_sums + late_sums)}
    got = dict(zip(LATE, late_got))
    in_bf = in_sums[0][1]
    small_g = {"g_mix": dg_mix, "b_forget": db[:, :N_HEADS], "g_ret_out": dg_ret, "g_fox_q": dg_fq, "g_fox_k": dg_fk,
               "g_xattn": dg_xattn, "g_mem": dg_mem, "g_xq": dg_xq, "g_xk": dg_xk, "g_ffn": dg_ffn}
    return loss_part, grad_x, sums, got, in_bf, small_g


def _final_grads(names, big, sums, got):
    my_core = lax.axis_index("c")
    finals = [_add_received(sums[n], got[n], f"chip_sum_{n}") for n in names]
    shared = _share_with_sibling(finals)
    out = {}
    for n, s, fin in zip(names, shared, finals):
        s = lax.dynamic_update_slice(s, fin[None], (my_core, 0, 0))
        out[n] = s.reshape(2 * s.shape[1], s.shape[2])[:big[n][0].shape[1], :big[n][0].shape[2]]
    return out


def _reduce_and_update(big, sums, got, in_bf, small_w, small_g, loss_part, grad_x, m_small, v_small):
    grads = _final_grads(LATE, big, sums, got)
    *late_updates, in_got = _adamw_many([big[n][0][0] for n in LATE], [grads[n] for n in LATE], [big[n][1][0] for n in LATE],
                                        [big[n][2][0] for n in LATE], [in_bf])
    updates = dict(zip(LATE, late_updates))
    grads.update(_final_grads(("w_in",), big, sums, {"w_in": in_got}))
    updates["w_in"] = _adamw(big["w_in"][0][0], grads["w_in"], big["w_in"][1][0], big["w_in"][2][0], "adamw_w_in")
    deltas, new_m, new_v = {}, {}, {}
    for n in big:
        restore = (lambda a: jnp.swapaxes(a[None], 1, 2)) if n in TRANSPOSED else (lambda a: a[None])
        grads[n] = restore(grads[n])
        deltas[n], new_m[n], new_v[n] = (restore(a) for a in updates[n])

    small_names = list(small_w)
    pad_rows = SMALL_ROWS - len(small_names) - 1
    stack = lambda d: jnp.concatenate([_pad_row(d[n]) for n in small_names] + [jnp.zeros((pad_rows + 1, D_MODEL), F32)], axis=0)
    g_pack = jnp.concatenate([_pad_row(small_g[n]) for n in small_names] + [_pad_row(loss_part[0:1, 0:1])]
                             + [jnp.zeros((pad_rows, D_MODEL), F32)], axis=0)
    g_tot = _all_reduce_small(g_pack)
    d_s, m_s, v_s = _adamw(stack(small_w), g_tot, stack(m_small), stack(v_small), "adamw_small")
    for i, n in enumerate(small_names):
        shape = small_w[n].shape
        size = int(np.prod(shape))
        grads[n] = g_tot[i, :size].reshape(shape)
        deltas[n], new_m[n], new_v[n] = d_s[i, :size].reshape(shape), m_s[i, :size].reshape(shape), v_s[i, :size].reshape(shape)
    loss = g_tot[len(small_names), 0]

    order = ["g_mix", "w_in", "b_forget", "g_ret_out", "g_fox_q", "g_fox_k", "w_out", "g_xattn", "w_xq", "w_xkv", "g_mem", "g_xq",
             "g_xk", "w_xo", "g_ffn", "w_gate", "w_up", "w_down"]
    return (loss, grad_x[None], *[grads[n] for n in order], *[deltas[n] for n in order], *[new_m[n] for n in order],
            *[new_v[n] for n in order])
```

```python
import functools

import numpy as np
import jax
import jax.numpy as jnp
from jax import lax
from jax.experimental import pallas as pl
from jax.experimental.pallas import tpu as pltpu

F32 = jnp.float32
BF = jnp.bfloat16

D_MODEL = 1024
HEAD_DIM = 64
N_HEADS = 8
GROUP_W = 512
N_XH = 4
XHD = 256
D_FF = 2816
MAIN_W = 3584
IN_W = 3592
ROPE_BASE = 10000.0
LOG2E = 1.4426950408889634
LN2 = 0.6931471805599453
EPS = 1e-6
NEG = -1e30
LANES = 128
RET_BLOCK = 256
REF_CHUNK = 64
ROW_TILE = 256
ATT_BLOCK = 256
FWD_GROUP = 4
TN_MAX_ROWS = 1408
SMALL_ROWS = 16
COL_SHARDED = ("w_xkv",)
TRANSPOSED = ("w_in", "w_gate", "w_up")
SHARD_ROW_ALIGN = 32
SHARD_ROW_PAD = 256
LATE = ("w_out", "w_xq", "w_xkv", "w_xo", "w_gate", "w_up", "w_down")
VMEM_LIMIT = 56 * 1024 * 1024

ADAM_LR = 0.001
ADAM_B1 = 0.9
ADAM_B2 = 0.999
ADAM_EPS = 1e-08
ADAM_WD = 0.01
ADAM_STEP = 10
ADAM_STEPS = 8

MESH = pl.DeviceIdType.MESH
ANY = pl.BlockSpec(memory_space=pl.ANY)
VMEM_SPEC = pl.BlockSpec(memory_space=pltpu.VMEM)


def _cparams(sem=None, vmem=VMEM_LIMIT):
    return pltpu.CompilerParams(dimension_semantics=sem, vmem_limit_bytes=vmem)


def _dot(a, b):
    return jnp.dot(a.astype(BF), b.astype(BF), preferred_element_type=F32)


def _dot_nt(a, b):
    return lax.dot_general(a.astype(BF), b.astype(BF), (((1,), (1,)), ((), ())), preferred_element_type=F32)


def _dot_tn(a, b):
    return lax.dot_general(a.astype(BF), b.astype(BF), (((0,), (0,)), ((), ())), preferred_element_type=F32)


def _split3(x):
    hi = x.astype(BF)
    r = x - hi.astype(F32)
    mid = r.astype(BF)
    lo = (r - mid.astype(F32)).astype(BF)
    return hi, mid, lo


def _dot_exact(ind, x):
    hi, mid, lo = _split3(x)
    return (jnp.dot(ind, lo, preferred_element_type=F32) + jnp.dot(ind, mid, preferred_element_type=F32)
            + jnp.dot(ind, hi, preferred_element_type=F32))


def _dot_nt_exact(ind, x):
    hi, mid, lo = _split3(x)
    dn = (((1,), (1,)), ((), ()))
    return (lax.dot_general(ind, lo, dn, preferred_element_type=F32) + lax.dot_general(ind, mid, dn, preferred_element_type=F32)
            + lax.dot_general(ind, hi, dn, preferred_element_type=F32))


def _sigmoid(x):
    return 1.0 / (1.0 + jnp.exp(-x))


def _rms_fwd(x, g):
    r = lax.rsqrt(jnp.mean(x * x, axis=-1, keepdims=True) + EPS)
    return x * r * g


def _rms_bwd(x, g, dy):
    r = lax.rsqrt(jnp.mean(x * x, axis=-1, keepdims=True) + EPS)
    xh = x * r
    dg = jnp.sum(dy * xh, axis=0, keepdims=True)
    dxh = dy * g
    dx = r * (dxh - xh * jnp.mean(dxh * xh, axis=-1, keepdims=True))
    return dx, dg


def _group_mean64(x):
    lane = lax.broadcasted_iota(jnp.int32, x.shape, 1)
    lo = lane < HEAD_DIM
    s_lo = jnp.sum(jnp.where(lo, x, 0.0), axis=-1, keepdims=True)
    s_hi = jnp.sum(jnp.where(lo, 0.0, x), axis=-1, keepdims=True)
    return jnp.where(lo, s_lo, s_hi) * (1.0 / HEAD_DIM)


def _swap32(x):
    lane = lax.broadcasted_iota(jnp.int32, x.shape, 1)
    first = (lane % HEAD_DIM) < (HEAD_DIM // 2)
    return jnp.where(first, pltpu.roll(x, LANES - HEAD_DIM // 2, axis=1), pltpu.roll(x, HEAD_DIM // 2, axis=1))


def _chunks(w):
    return [slice(j * LANES, (j + 1) * LANES) for j in range(w // LANES)]


def _aug_pair(qk, f_cols, is_query):
    lane = lax.broadcasted_iota(jnp.int32, qk.shape, 1)
    a = lane - HEAD_DIM
    values = (qk, pltpu.roll(qk, HEAD_DIM, axis=1))
    out = []
    for hh in range(2):
        hi, mid, lo = (p.astype(F32) for p in _split3(f_cols[hh] * LOG2E))
        if is_query:
            aux = jnp.where(a == 0, hi, jnp.where(a == 1, mid, jnp.where(a == 2, lo, jnp.where(a < 6, 1.0, 0.0))))
        else:
            aux = jnp.where(a < 3, 1.0, jnp.where(a == 3, -hi, jnp.where(a == 4, -mid, jnp.where(a == 5, -lo, 0.0))))
        out.append(jnp.where(a < 0, values[hh], aux))
    return jnp.concatenate(out, axis=-1).astype(BF)


def _mem_kv_fwd(mem, g_mem, w_xkv, g_xk):
    m_tok = mem.shape[0]

    def body(mem_ref, gm_ref, w_ref, gk_ref, memn_ref, kraw_ref, kn_ref, v_ref):
        mn = _rms_fwd(mem_ref[...], gm_ref[...]).astype(BF)
        memn_ref[...] = mn
        kv = jnp.dot(mn, w_ref[...], preferred_element_type=F32)
        k = kv[:, :D_MODEL]
        kraw_ref[...] = k
        v_ref[...] = kv[:, D_MODEL:].astype(BF)
        for h in range(N_XH):
            sl = slice(h * XHD, (h + 1) * XHD)
            kn_ref[:, sl] = _rms_fwd(k[:, sl], gk_ref[...]).astype(BF)

    return pl.pallas_call(
        body, name="mem_kv_fwd",
        out_shape=(jax.ShapeDtypeStruct((m_tok, D_MODEL), BF), jax.ShapeDtypeStruct((m_tok, D_MODEL), F32),
                   jax.ShapeDtypeStruct((m_tok, D_MODEL), BF), jax.ShapeDtypeStruct((m_tok, D_MODEL), BF)),
        in_specs=[VMEM_SPEC] * 4, out_specs=(VMEM_SPEC,) * 4, compiler_params=_cparams(),
    )(mem, g_mem, w_xkv, g_xk)


def _in_proj_fwd(x, g_mix, w_in_t, b_pad, cos_t, sin_t, gq_t, gk_t):
    t_len = x.shape[0]
    tm = min(ROW_TILE, t_len)
    n_t = t_len // tm

    def body(x_ref, g_ref, wm_ref, wf_ref, b_ref, cos_ref, sin_ref, gq_ref, gk_ref,
             n1_ref, proj_ref, rq_ref, rk_ref, qa_ref, ka_ref, z_ref, carry):
        i = pl.program_id(0)

        @pl.when(i == 0)
        def _():
            carry[...] = jnp.zeros_like(carry)

        n1 = _rms_fwd(x_ref[...], g_ref[...]).astype(BF)
        n1_ref[...] = n1
        z = _dot_nt(n1, wf_ref[...]) + b_ref[...]
        z_ref[...] = z
        lane = lax.broadcasted_iota(jnp.int32, z.shape, 1)
        lf = jnp.where(lane < N_HEADS, jnp.minimum(z, 0.0) - jnp.log(1.0 + jnp.exp(-jnp.abs(z))), 0.0)
        row = lax.broadcasted_iota(jnp.int32, (tm, tm), 0)
        col = lax.broadcasted_iota(jnp.int32, (tm, tm), 1)
        tri = (row >= col).astype(BF)
        fc = _dot_exact(tri, lf) + carry[0:1, :]
        carry[...] = jnp.broadcast_to(fc[tm - 1:tm, :], carry.shape)
        c, s = cos_ref[...], sin_ref[...]

        def section(n):
            p = _dot_nt(n1, wm_ref[n * GROUP_W:(n + 1) * GROUP_W, :])
            proj_ref[:, n * GROUP_W:(n + 1) * GROUP_W] = p.astype(BF)
            return p

        def rotate(p, out_ref, scale):
            for sl in _chunks(GROUP_W):
                out_ref[:, sl] = ((p[:, sl] * c + _swap32(p[:, sl]) * s) * scale).astype(BF)

        def norm_aug(p, gain, out_ref, scale, is_query):
            for j, sl in enumerate(_chunks(GROUP_W)):
                f = p[:, sl]
                f = f * lax.rsqrt(_group_mean64(f * f) + EPS) * gain * scale
                out_ref[:, 2 * j * LANES:2 * (j + 1) * LANES] = _aug_pair(f, [fc[:, 2 * j:2 * j + 1], fc[:, 2 * j + 1:2 * j + 2]], is_query)

        p_rq, p_rk = section(0), section(1)
        rotate(p_rq, rq_ref, 0.125)
        section(2)
        rotate(p_rk, rk_ref, 1.0)
        section(3)
        p_fq = section(4)
        p_fk = section(5)
        norm_aug(p_fq, gq_ref[...], qa_ref, 0.125 * LOG2E, True)
        section(6)
        norm_aug(p_fk, gk_ref[...], ka_ref, 1.0, False)

    row_spec = lambda w: pl.BlockSpec((tm, w), lambda i: (i, 0))
    full = lambda a: pl.BlockSpec(a.shape, lambda i: (0,) * a.ndim)
    return pl.pallas_call(
        body, name="in_proj_fwd", grid=(n_t,),
        out_shape=(jax.ShapeDtypeStruct((t_len, D_MODEL), BF), jax.ShapeDtypeStruct((t_len, MAIN_W), BF),
                   jax.ShapeDtypeStruct((t_len, GROUP_W), BF), jax.ShapeDtypeStruct((t_len, GROUP_W), BF),
                   jax.ShapeDtypeStruct((t_len, 2 * GROUP_W), BF), jax.ShapeDtypeStruct((t_len, 2 * GROUP_W), BF),
                   jax.ShapeDtypeStruct((t_len, LANES), F32)),
        in_specs=[row_spec(D_MODEL), full(g_mix), *_w_in_specs(), full(b_pad), row_spec(LANES), row_spec(LANES),
                  full(gq_t), full(gk_t)],
        out_specs=(row_spec(D_MODEL), row_spec(MAIN_W), row_spec(GROUP_W), row_spec(GROUP_W), row_spec(2 * GROUP_W),
                   row_spec(2 * GROUP_W), row_spec(LANES)),
        scratch_shapes=[pltpu.VMEM((8, LANES), F32)],
        compiler_params=_cparams(("arbitrary",)),
    )(x, g_mix, w_in_t, w_in_t, b_pad, cos_t, sin_t, gq_t, gk_t)


def _w_in_specs():
    return (pl.BlockSpec((MAIN_W, D_MODEL), lambda i: (0, 0)), pl.BlockSpec((LANES, D_MODEL), lambda i: (MAIN_W // LANES, 0)))


def _decay_tables(c):
    h = np.arange(N_HEADS, dtype=np.float64)
    lg = np.log(1.0 - 2.0 ** (-5.0 - h)).astype(np.float32).astype(np.float64)
    t = np.arange(c)
    same_or_earlier = (t[None, :] // REF_CHUNK) <= (t[:, None] // REF_CHUNK)
    w = np.where(same_or_earlier[None], np.exp(lg[:, None, None] * np.abs(t[:, None] - t[None, :])[None]), 0.0)
    qd = np.exp(lg[:, None] * (t[None, :] + 1.0))
    kd = np.exp(lg[:, None] * (c - 1.0 - t[None, :]))
    cd = np.exp(lg * c)
    ones = np.ones((1, 1, HEAD_DIM))
    return (jnp.asarray(w, F32), jnp.asarray(qd[:, :, None] * ones, F32), jnp.asarray(kd[:, :, None] * ones, F32),
            jnp.asarray(cd[:, None, None] * np.ones((1, HEAD_DIM, HEAD_DIM)), F32))


def _retention_fwd(rq, rk, proj, g_ret, tables):
    t_len = rq.shape[0]
    c = min(RET_BLOCK, t_len)
    n_b = t_len // c
    wdec, qdec, kdec, cdec = tables
    v_col, g_col = 2 * GROUP_W // LANES, 3 * GROUP_W // LANES

    def body(q_ref, k_ref, v_ref, rg_ref, g_ref, w_ref, qd_ref, kd_ref, cd_ref, raw_ref, mix_ref, st_ref, state):
        i = pl.program_id(1)

        @pl.when(i == 0)
        def _():
            state[...] = jnp.zeros_like(state)

        q2, k2, v2 = q_ref[...], k_ref[...], v_ref[...]
        outs = []
        for hh in range(2):
            sl = slice(hh * HEAD_DIM, (hh + 1) * HEAD_DIM)
            q, k, v = q2[:, sl], k2[:, sl], v2[:, sl]
            sp = state[hh]
            st_ref[0, 0, hh] = sp
            a = _dot_nt(q, k) * w_ref[hh]
            o = _dot(a, v) + _dot(q.astype(F32) * qd_ref[hh], sp)
            state[hh] = sp * cd_ref[hh] + _dot_tn(k.astype(F32) * kd_ref[hh], v)
            outs.append(o)
        o2 = jnp.concatenate(outs, axis=-1)
        raw_ref[...] = o2
        xc = o2 - _group_mean64(o2)
        xh = xc * lax.rsqrt(_group_mean64(xc * xc) + EPS)
        gate = rg_ref[...].astype(F32)
        mix_ref[...] = (gate * _sigmoid(gate) * (xh * g_ref[0])).astype(BF)

    blk = lambda col0: pl.BlockSpec((c, LANES), lambda hp, i: (i, col0 + hp))
    tab = lambda a: pl.BlockSpec((2,) + a.shape[1:], lambda hp, i: (hp, 0, 0))
    return pl.pallas_call(
        body, name="retention_fwd", grid=(N_HEADS // 2, n_b),
        out_shape=(jax.ShapeDtypeStruct((t_len, GROUP_W), F32), jax.ShapeDtypeStruct((t_len, GROUP_W), BF),
                   jax.ShapeDtypeStruct((N_HEADS // 2, n_b, 2, HEAD_DIM, HEAD_DIM), F32)),
        in_specs=[blk(0), blk(0), blk(v_col), blk(g_col), pl.BlockSpec((1, 1, LANES), lambda hp, i: (hp, 0, 0)),
                  tab(wdec), tab(qdec), tab(kdec), tab(cdec)],
        out_specs=(blk(0), blk(0), pl.BlockSpec((1, 1, 2, HEAD_DIM, HEAD_DIM), lambda hp, i: (hp, i, 0, 0, 0))),
        scratch_shapes=[pltpu.VMEM((2, HEAD_DIM, HEAD_DIM), F32)],
        compiler_params=_cparams(("arbitrary", "arbitrary")),
    )(rq, rk, proj, proj, g_ret, wdec, qdec, kdec, cdec)


def _fox_fwd(q_aug, k_aug, proj, shards):
    t_len = q_aug.shape[0]
    tq = min(ATT_BLOCK, t_len)
    nsub = min(FWD_GROUP, t_len // tq)
    tg = nsub * tq
    n_q = t_len // tg
    v_col = 6 * GROUP_W // LANES
    tc = min(512, t_len)
    n_w = len(shards)
    n_steps = (N_HEADS // 2) * n_q

    def body(*refs):
        q_ref, k_ref, v_ref = refs[:3]
        o_ref, o32_ref, lse_ref = refs[3 + n_w:6 + n_w]
        vt = refs[6 + 2 * n_w]
        comm = (refs[3:3 + n_w], refs[6 + n_w:6 + 2 * n_w]) + tuple(refs[7 + 2 * n_w:])
        i = pl.program_id(1)
        step = pl.program_id(0) * n_q + i

        @pl.when(step == 0)
        def _():
            _gather_phase(0, *comm)

        @pl.when(step == (3 * n_steps) // 4)
        def _():
            _gather_phase(1, *comm)

        @pl.when(i == 0)
        def _():
            for c0 in range(0, t_len, tc):
                vt[:, c0:c0 + tc] = v_ref[c0:c0 + tc, :].T

        chains = [(u, hh) for u in range(nsub) for hh in range(2)]
        qs = {(u, hh): q_ref[u * tq:(u + 1) * tq, hh * LANES:(hh + 1) * LANES] for u, hh in chains}
        ones = jnp.ones((HEAD_DIM, tq), BF)

        def scores(j, which):
            k2 = k_ref[pl.ds(pl.multiple_of(j * tq, tq), tq), :]
            return {ch: _dot_nt(k2[:, ch[1] * LANES:(ch[1] + 1) * LANES], qs[ch]) for ch in which}

        def update(j, ss, carry, masked):
            v2 = vt[:, pl.ds(pl.multiple_of(j * tq, tq), tq)]
            ps, stats = {}, {}
            for ch in ss:
                m = carry[ch][0]
                s_t = ss[ch]
                if ch in masked:
                    krow = lax.broadcasted_iota(jnp.int32, (tq, tq), 0)
                    qcol = lax.broadcasted_iota(jnp.int32, (tq, tq), 1)
                    s_t = jnp.where(qcol >= krow, s_t, NEG)
                m_new = jnp.maximum(m, jnp.max(s_t, axis=0, keepdims=True))
                ps[ch] = jnp.exp2(s_t - m_new).astype(BF)
                stats[ch] = (m_new, jnp.exp2(m - m_new))
            out = dict(carry)
            for ch in ss:
                m_new, alpha = stats[ch]
                v_aug = jnp.concatenate([v2[ch[1] * HEAD_DIM:(ch[1] + 1) * HEAD_DIM, :], ones], axis=0)
                out[ch] = (m_new, carry[ch][1] * alpha + jnp.dot(v_aug, ps[ch], preferred_element_type=F32))
            return out

        def advance(j, state):
            ss, carry = state
            return scores(j + 1, chains), update(j, ss, carry, ())

        init = {ch: (jnp.full((1, tq), NEG, F32), jnp.zeros((LANES, tq), F32)) for ch in chains}
        first = nsub * i
        ss, carry = lax.fori_loop(0, first, advance, (scores(0, chains), init))
        carry = update(first, ss, carry, [(0, 0), (0, 1)])
        for u in range(1, nsub):
            rest = [(uu, hh) for uu in range(u, nsub) for hh in range(2)]
            carry = update(first + u, scores(first + u, rest), carry, [(u, 0), (u, 1)])
        for u in range(nsub):
            outs, lses = [], []
            for hh in range(2):
                m, acc = carry[u, hh]
                l = acc[HEAD_DIM:HEAD_DIM + 1, :]
                outs.append(acc[:HEAD_DIM, :] / l)
                lses.append(m + jnp.log2(l))
            o2 = jnp.concatenate(outs, axis=0).T
            o32_ref[u * tq:(u + 1) * tq, :] = o2
            o_ref[u * tq:(u + 1) * tq, :] = o2.astype(BF)
            lse_ref[0, :, u * tq:(u + 1) * tq] = jnp.concatenate(lses, axis=0)

        @pl.when(step == n_steps - 1)
        def _():
            _gather_phase(2, *comm)

    return pl.pallas_call(
        body, name="fox_fwd", grid=(N_HEADS // 2, n_q),
        out_shape=(jax.ShapeDtypeStruct((t_len, GROUP_W), BF), jax.ShapeDtypeStruct((t_len, GROUP_W), F32),
                   jax.ShapeDtypeStruct((N_HEADS // 2, 2, t_len), F32))
        + tuple(jax.ShapeDtypeStruct((4,) + s.shape, s.dtype) for s in shards),
        in_specs=[pl.BlockSpec((tg, 2 * LANES), lambda hp, i: (i, hp)),
                  pl.BlockSpec((t_len, 2 * LANES), lambda hp, i: (0, hp)),
                  pl.BlockSpec((t_len, LANES), lambda hp, i: (0, v_col + hp))] + [ANY] * n_w,
        out_specs=(pl.BlockSpec((tg, LANES), lambda hp, i: (i, hp)), pl.BlockSpec((tg, LANES), lambda hp, i: (i, hp)),
                   pl.BlockSpec((1, 2, tg), lambda hp, i: (hp, 0, i))) + (ANY,) * n_w,
        scratch_shapes=[pltpu.VMEM((LANES, t_len), BF)] + _gather_scratch(n_w),
        compiler_params=_cparams(("arbitrary", "arbitrary")),
    )(q_aug, k_aug, proj, *shards)


def _softmax_rows(s):
    p = jnp.exp(s - jnp.max(s, axis=-1, keepdims=True))
    return p / jnp.sum(p, axis=-1, keepdims=True)


def _attn_out_xattn_fwd(x, mix_r, mix_f, w_out, g_xattn, w_xq, g_xq, kn, v, w_xo):
    t_len = x.shape[0]
    tm = min(ROW_TILE, t_len)

    def body(x_ref, mr_ref, mf_ref, wo_ref, g_ref, wq_ref, gq_ref, kn_ref, v_ref, wxo_ref,
             h1_ref, hn_ref, qx_ref, o_ref, h2_ref):
        h1 = x_ref[...] + jnp.dot(mr_ref[...], wo_ref[:GROUP_W, :], preferred_element_type=F32) \
            + jnp.dot(mf_ref[...], wo_ref[GROUP_W:, :], preferred_element_type=F32)
        h1_ref[...] = h1
        hn = _rms_fwd(h1, g_ref[...]).astype(BF)
        hn_ref[...] = hn
        qx = jnp.dot(hn, wq_ref[...], preferred_element_type=F32).astype(BF)
        qx_ref[...] = qx
        sls = [slice(h * XHD, (h + 1) * XHD) for h in range(N_XH)]
        qns = [_rms_fwd(qx[:, sl].astype(F32), gq_ref[...]).astype(BF) for sl in sls]
        logits = [_dot_nt(qn, kn_ref[:, sl]) * (XHD ** -0.5) for qn, sl in zip(qns, sls)]
        ps = [_softmax_rows(s).astype(BF) for s in logits]
        for p, sl in zip(ps, sls):
            o_ref[:, sl] = jnp.dot(p, v_ref[:, sl], preferred_element_type=F32).astype(BF)
        h2_ref[...] = h1 + jnp.dot(o_ref[...], wxo_ref[...], preferred_element_type=F32)

    row_spec = lambda w: pl.BlockSpec((tm, w), lambda i: (i, 0))
    full = lambda a: pl.BlockSpec(a.shape, lambda i: (0,) * a.ndim)
    return pl.pallas_call(
        body, name="attn_out_xattn_fwd", grid=(t_len // tm,),
        out_shape=(jax.ShapeDtypeStruct((t_len, D_MODEL), F32), jax.ShapeDtypeStruct((t_len, D_MODEL), BF),
                   jax.ShapeDtypeStruct((t_len, D_MODEL), BF), jax.ShapeDtypeStruct((t_len, D_MODEL), BF),
                   jax.ShapeDtypeStruct((t_len, D_MODEL), F32)),
        in_specs=[row_spec(D_MODEL), row_spec(GROUP_W), row_spec(GROUP_W), full(w_out), full(g_xattn), full(w_xq), full(g_xq),
                  full(kn), full(v), full(w_xo)],
        out_specs=(row_spec(D_MODEL),) * 5,
        compiler_params=_cparams(("arbitrary",)),
    )(x, mix_r, mix_f, w_out, g_xattn, w_xq, g_xq, kn, v, w_xo)


def _ffn_loss_fwd(h2, g_ffn, w_gate, w_up, w_down, target):
    t_len = h2.shape[0]
    tm = min(ROW_TILE, t_len)

    def body(h2_ref, g_ref, wg_ref, wu_ref, wd_ref, tgt_ref, hn_ref, gate_ref, up_ref, act_ref, dh3_ref, loss_ref):
        @pl.when(pl.program_id(0) == 0)
        def _():
            loss_ref[...] = jnp.zeros_like(loss_ref)

        h2v = h2_ref[...]
        hn = _rms_fwd(h2v, g_ref[...]).astype(BF)
        hn_ref[...] = hn
        gate = _dot_nt(hn, wg_ref[...])
        up = _dot_nt(hn, wu_ref[...])
        gate_ref[...] = gate.astype(BF)
        up_ref[...] = up.astype(BF)
        act = (gate * _sigmoid(gate) * up).astype(BF)
        act_ref[...] = act
        diff = h2v + jnp.dot(act, wd_ref[...], preferred_element_type=F32) - tgt_ref[...]
        dh3_ref[...] = diff * (1.0 / D_MODEL)
        per_row = jnp.sum(diff * diff, axis=-1, keepdims=True) * (1.0 / D_MODEL)
        loss_ref[...] += 0.5 * jnp.sum(per_row, axis=0, keepdims=True)

    row_spec = lambda w: pl.BlockSpec((tm, w), lambda i: (i, 0))
    full = lambda a: pl.BlockSpec(a.shape, lambda i: (0,) * a.ndim, pipeline_mode=pl.Buffered(1))
    return pl.pallas_call(
        body, name="ffn_loss_fwd", grid=(t_len // tm,),
        out_shape=(jax.ShapeDtypeStruct((t_len, D_MODEL), BF), jax.ShapeDtypeStruct((t_len, D_FF), BF),
                   jax.ShapeDtypeStruct((t_len, D_FF), BF), jax.ShapeDtypeStruct((t_len, D_FF), BF),
                   jax.ShapeDtypeStruct((t_len, D_MODEL), F32), jax.ShapeDtypeStruct((8, LANES), F32)),
        in_specs=[row_spec(D_MODEL), full(g_ffn), full(w_gate), full(w_up), full(w_down), row_spec(D_MODEL)],
        out_specs=(row_spec(D_MODEL), row_spec(D_FF), row_spec(D_FF), row_spec(D_FF), row_spec(D_MODEL),
                   pl.BlockSpec((8, LANES), lambda i: (0, 0))),
        compiler_params=_cparams(("arbitrary",)),
    )(h2, g_ffn, w_gate, w_up, w_down, target)


def _ffn_bwd(dh3, gate, up, h2, g_ffn, w_gate, w_up, w_down):
    t_len = h2.shape[0]
    tm = min(ROW_TILE, t_len)

    def body(dh3_ref, gate_ref, up_ref, h2_ref, g_ref, wg_ref, wu_ref, wd_ref, dgate_ref, dup_ref, dh2_ref, dg_ref):
        @pl.when(pl.program_id(0) == 0)
        def _():
            dg_ref[...] = jnp.zeros_like(dg_ref)

        dh3v = dh3_ref[...]
        dact = _dot_nt(dh3v, wd_ref[...])
        g = gate_ref[...].astype(F32)
        sg = _sigmoid(g)
        dup = (dact * (g * sg)).astype(BF)
        dgate = (dact * up_ref[...].astype(F32) * (sg * (1.0 + g * (1.0 - sg)))).astype(BF)
        dup_ref[...] = dup
        dgate_ref[...] = dgate
        dhn = jnp.dot(dgate, wg_ref[...], preferred_element_type=F32) + jnp.dot(dup, wu_ref[...], preferred_element_type=F32)
        dx, dg = _rms_bwd(h2_ref[...], g_ref[...], dhn)
        dh2_ref[...] = dh3v + dx
        dg_ref[...] += dg

    row_spec = lambda w: pl.BlockSpec((tm, w), lambda i: (i, 0))
    full = lambda a: pl.BlockSpec(a.shape, lambda i: (0,) * a.ndim, pipeline_mode=pl.Buffered(1))
    return pl.pallas_call(
        body, name="ffn_bwd", grid=(t_len // tm,),
        out_shape=(jax.ShapeDtypeStruct((t_len, D_FF), BF), jax.ShapeDtypeStruct((t_len, D_FF), BF),
                   jax.ShapeDtypeStruct((t_len, D_MODEL), F32), jax.ShapeDtypeStruct((1, D_MODEL), F32)),
        in_specs=[row_spec(D_MODEL), row_spec(D_FF), row_spec(D_FF), row_spec(D_MODEL), full(g_ffn), full(w_gate), full(w_up),
                  full(w_down)],
        out_specs=(row_spec(D_FF), row_spec(D_FF), row_spec(D_MODEL), pl.BlockSpec((1, D_MODEL), lambda i: (0, 0))),
        compiler_params=_cparams(("arbitrary",)),
    )(dh3, gate, up, h2, g_ffn, w_gate, w_up, w_down)


def _attn_out_xattn_bwd(dh2, h1, qx, kn, v, w_xo, w_xq, w_out, g_xattn, g_xq):
    t_len = h1.shape[0]
    tm = min(ROW_TILE, t_len)
    m_tok = kn.shape[0]

    def body(dh2_ref, h1_ref, qx_ref, kn_ref, v_ref, wxo_ref, wq_ref, wo_ref, g_ref, gq_ref,
             dqx_ref, dh1_ref, dmr_ref, dmf_ref, dkn_ref, dv_ref, dg_ref, dgq_ref, dqx_scr):
        @pl.when(pl.program_id(0) == 0)
        def _():
            dkn_ref[...] = jnp.zeros_like(dkn_ref)
            dv_ref[...] = jnp.zeros_like(dv_ref)
            dg_ref[...] = jnp.zeros_like(dg_ref)
            dgq_ref[...] = jnp.zeros_like(dgq_ref)

        dh2v = dh2_ref[...]
        do = _dot_nt(dh2v, wxo_ref[...])
        gq = gq_ref[...]
        sls = [slice(h * XHD, (h + 1) * XHD) for h in range(N_XH)]
        qraws = [qx_ref[:, sl].astype(F32) for sl in sls]
        qns = [_rms_fwd(qraw, gq).astype(BF) for qraw in qraws]
        dohs = [do[:, sl].astype(BF) for sl in sls]
        logits = [_dot_nt(qn, kn_ref[:, sl]) * (XHD ** -0.5) for qn, sl in zip(qns, sls)]
        dps = [_dot_nt(doh, v_ref[:, sl]) for doh, sl in zip(dohs, sls)]
        ps = [_softmax_rows(s) for s in logits]
        dss = [(p * (dp - jnp.sum(dp * p, axis=-1, keepdims=True)) * (XHD ** -0.5)).astype(BF) for p, dp in zip(ps, dps)]
        dqns = []
        for h, sl in enumerate(sls):
            dv_ref[:, sl] += _dot_tn(ps[h], dohs[h])
            dqns.append(jnp.dot(dss[h], kn_ref[:, sl], preferred_element_type=F32))
            dkn_ref[:, sl] += _dot_tn(dss[h], qns[h])
        dgq = jnp.zeros((1, XHD), F32)
        for h, sl in enumerate(sls):
            dx, dg_h = _rms_bwd(qraws[h], gq, dqns[h])
            dgq = dgq + dg_h
            dqx_scr[:, sl] = dx.astype(BF)
        dgq_ref[...] += dgq
        dqx = dqx_scr[...]
        dqx_ref[...] = dqx
        dhn = _dot_nt(dqx, wq_ref[...])
        dx, dg = _rms_bwd(h1_ref[...], g_ref[...], dhn)
        dg_ref[...] += dg
        dh1 = dh2v + dx
        dh1_ref[...] = dh1
        dmix = _dot_nt(dh1, wo_ref[...])
        dmr_ref[...] = dmix[:, :GROUP_W]
        dmf_ref[...] = dmix[:, GROUP_W:].astype(BF)

    row_spec = lambda w: pl.BlockSpec((tm, w), lambda i: (i, 0))
    full = lambda a: pl.BlockSpec(a.shape, lambda i: (0,) * a.ndim)
    acc = lambda r, c: pl.BlockSpec((r, c), lambda i: (0, 0))
    return pl.pallas_call(
        body, name="attn_out_xattn_bwd", grid=(t_len // tm,),
        out_shape=(jax.ShapeDtypeStruct((t_len, D_MODEL), BF), jax.ShapeDtypeStruct((t_len, D_MODEL), F32),
                   jax.ShapeDtypeStruct((t_len, GROUP_W), F32), jax.ShapeDtypeStruct((t_len, GROUP_W), BF),
                   jax.ShapeDtypeStruct((m_tok, D_MODEL), F32), jax.ShapeDtypeStruct((m_tok, D_MODEL), F32),
                   jax.ShapeDtypeStruct((1, D_MODEL), F32), jax.ShapeDtypeStruct((1, XHD), F32)),
        in_specs=[row_spec(D_MODEL), row_spec(D_MODEL), row_spec(D_MODEL), full(kn), full(v), full(w_xo), full(w_xq), full(w_out),
                  full(g_xattn), full(g_xq)],
        out_specs=(row_spec(D_MODEL), row_spec(D_MODEL), row_spec(GROUP_W), row_spec(GROUP_W), acc(m_tok, D_MODEL),
                   acc(m_tok, D_MODEL), acc(1, D_MODEL), acc(1, XHD)),
        scratch_shapes=[pltpu.VMEM((tm, D_MODEL), BF)],
        compiler_params=_cparams(("arbitrary",)),
    )(dh2, h1, qx, kn, v, w_xo, w_xq, w_out, g_xattn, g_xq)


def _mem_kv_bwd(dkn, dv, kraw, mem, memn, g_mem, g_xk, w_xkv):
    m_tok = mem.shape[0]

    def body(dkn_ref, dv_ref, kraw_ref, mem_ref, memn_ref, gm_ref, gk_ref, w_ref, dw_ref, dgm_ref, dgk_ref, dkv_scr):
        gk = gk_ref[...]
        dgk = jnp.zeros((1, XHD), F32)
        for h in range(N_XH):
            sl = slice(h * XHD, (h + 1) * XHD)
            dx, dg_h = _rms_bwd(kraw_ref[:, sl], gk, dkn_ref[:, sl])
            dgk = dgk + dg_h
            dkv_scr[:, sl] = dx.astype(BF)
        dgk_ref[...] = dgk
        dkv_scr[:, D_MODEL:] = dv_ref[...].astype(BF)
        dkv = dkv_scr[...]
        dw_ref[...] = _dot_tn(memn_ref[...], dkv)
        dmemn = _dot_nt(dkv, w_ref[...])
        mem_v = mem_ref[...]
        r = lax.rsqrt(jnp.mean(mem_v * mem_v, axis=-1, keepdims=True) + EPS)
        dgm_ref[...] = jnp.sum(dmemn * mem_v * r, axis=0, keepdims=True)

    return pl.pallas_call(
        body, name="mem_kv_bwd",
        out_shape=(jax.ShapeDtypeStruct((D_MODEL, 2 * D_MODEL), F32), jax.ShapeDtypeStruct((1, D_MODEL), F32),
                   jax.ShapeDtypeStruct((1, XHD), F32)),
        in_specs=[VMEM_SPEC] * 8, out_specs=(VMEM_SPEC,) * 3,
        scratch_shapes=[pltpu.VMEM((m_tok, 2 * D_MODEL), BF)],
        compiler_params=_cparams(),
    )(dkn, dv, kraw, mem, memn, g_mem, g_xk, w_xkv)


def _fox_bwd(q_aug, k_aug, proj, dmf, o32, lse, sums):
    t_len = q_aug.shape[0]
    tb = min(ATT_BLOCK, t_len)
    n_b = t_len // tb
    nsub = 2 if n_b >= 2 else 1
    tg = nsub * tb
    n_g = t_len // tg
    v_col = 6 * GROUP_W // LANES
    n_w = len(sums)
    n_steps = (N_HEADS // 2) * n_g

    def body(*refs):
        k_ref, v_ref, q_ref, do_ref, o_ref, lse_ref = refs[:6]
        dq_ref, dk_ref, dv_ref, df_ref = refs[6 + n_w:10 + n_w]
        delta = refs[10 + 2 * n_w]
        comm = (refs[6:6 + n_w], refs[10 + n_w:10 + 2 * n_w]) + tuple(refs[11 + 2 * n_w:])
        j = pl.program_id(1)
        step = pl.program_id(0) * n_g + j

        @pl.when(step == 0)
        def _():
            _scatter_phase(0, *comm)

        @pl.when(j == 0)
        def _():
            dq_ref[...] = jnp.zeros_like(dq_ref)
            dd = do_ref[...].astype(F32) * o_ref[...]
            hrow = lax.broadcasted_iota(jnp.int32, (8, LANES), 0)
            lane = lax.broadcasted_iota(jnp.int32, (8, LANES), 1)
            ind = ((lane // HEAD_DIM) == hrow).astype(BF)
            delta[...] = _dot_nt_exact(ind, dd)

        k2, v2 = k_ref[...], v_ref[...]
        chains = [(u, hh) for u in range(nsub) for hh in range(2)]
        ks = {(u, hh): k2[u * tb:(u + 1) * tb, hh * LANES:(hh + 1) * LANES] for u, hh in chains}
        vs = {(u, hh): v2[u * tb:(u + 1) * tb, hh * HEAD_DIM:(hh + 1) * HEAD_DIM] for u, hh in chains}

        def block(i, carry, which, masked):
            rows = pl.ds(pl.multiple_of(i * tb, tb), tb)
            q2 = q_ref[rows, :]
            do2 = do_ref[rows, :]
            qs = [q2[:, hh * LANES:(hh + 1) * LANES] for hh in range(2)]
            dos = [do2[:, hh * HEAD_DIM:(hh + 1) * HEAD_DIM] for hh in range(2)]
            ss = {ch: _dot_nt(ks[ch], qs[ch[1]]) for ch in which}
            dps = {ch: _dot_nt(vs[ch], dos[ch[1]]) for ch in which}
            pts, dsts, dfs = {}, {}, {}
            for ch in which:
                hh = ch[1]
                s_t = ss[ch]
                if ch in masked:
                    krow = lax.broadcasted_iota(jnp.int32, (tb, tb), 0)
                    qcol = lax.broadcasted_iota(jnp.int32, (tb, tb), 1)
                    s_t = jnp.where(qcol >= krow, s_t, NEG)
                p_t = jnp.exp2(s_t - lse_ref[0, hh:hh + 1, rows])
                pts[ch] = p_t.astype(BF)
                ds_t = p_t * (dps[ch] - delta[hh:hh + 1, rows])
                dsts[ch] = ds_t.astype(BF)
                dfs[ch] = jnp.sum(ds_t, axis=-1, keepdims=True)
            out = dict(carry)
            for ch in which:
                dk, dv, df = carry[ch]
                dv = dv + jnp.dot(pts[ch], dos[ch[1]], preferred_element_type=F32)
                dk = dk + jnp.dot(dsts[ch], qs[ch[1]], preferred_element_type=F32)
                out[ch] = (dk, dv, df - dfs[ch])
            for hh in range(2):
                parts_dq = [_dot_tn(dsts[ch], ks[ch])[:, :HEAD_DIM] for ch in which if ch[1] == hh]
                dq_ref[rows, hh * HEAD_DIM:(hh + 1) * HEAD_DIM] += sum(parts_dq[1:], parts_dq[0])
            return out

        init = {ch: (jnp.zeros((tb, LANES), F32), jnp.zeros((tb, HEAD_DIM), F32), jnp.zeros((tb, 1), F32)) for ch in chains}
        first = nsub * j
        carry = block(first, init, [(0, 0), (0, 1)], [(0, 0), (0, 1)])
        if nsub == 2:
            carry = block(first + 1, carry, chains, [(1, 0), (1, 1)])
        carry = lax.fori_loop(first + nsub, n_b, lambda i, c: block(i, c, chains, ()), carry)
        for u in range(nsub):
            rs = slice(u * tb, (u + 1) * tb)
            dk_ref[rs, :] = jnp.concatenate([carry[u, hh][0][:, :HEAD_DIM] for hh in range(2)], axis=-1) * LN2
            dv_ref[rs, :] = jnp.concatenate([carry[u, hh][1] for hh in range(2)], axis=-1)
            df_ref[0, rs, :] = jnp.concatenate([carry[u, hh][2] for hh in range(2)], axis=-1)

        @pl.when(step == n_steps - 1)
        def _():
            _scatter_phase(1, *comm)

    blk = lambda w, col0: pl.BlockSpec((tg, w), lambda hp, j: (j, col0 + hp))
    whole = lambda w: pl.BlockSpec((t_len, w), lambda hp, j: (0, hp))
    rows2 = pl.BlockSpec((1, 2, t_len), lambda hp, j: (hp, 0, 0))
    cols2 = pl.BlockSpec((1, tg, 2), lambda hp, j: (hp, j, 0))
    return pl.pallas_call(
        body, name="fox_bwd", grid=(N_HEADS // 2, n_g),
        out_shape=(jax.ShapeDtypeStruct((t_len, GROUP_W), F32), jax.ShapeDtypeStruct((t_len, GROUP_W), F32),
                   jax.ShapeDtypeStruct((t_len, GROUP_W), F32), jax.ShapeDtypeStruct((N_HEADS // 2, t_len, 2), F32))
        + _scatter_out_shapes(sums),
        in_specs=[blk(2 * LANES, 0), blk(LANES, v_col), whole(2 * LANES), whole(LANES), whole(LANES), rows2] + [ANY] * n_w,
        out_specs=(whole(LANES), blk(LANES, 0), blk(LANES, 0), cols2) + (ANY,) * n_w,
        scratch_shapes=[pltpu.VMEM((8, t_len), F32)] + _scatter_scratch(n_w),
        compiler_params=_cparams(("arbitrary", "arbitrary")),
    )(k_aug, proj, q_aug, dmf, o32, lse, *sums)


def _retention_bwd(dmr, raw, proj, g_ret, rq, rk, states, tables, parts):
    t_len = rq.shape[0]
    c = min(RET_BLOCK, t_len)
    n_b = t_len // c
    wdec, qdec, kdec, cdec = tables
    v_col, g_col = 2 * GROUP_W // LANES, 3 * GROUP_W // LANES
    n_w = len(parts)
    n_steps = (N_HEADS // 2) * n_b

    def body(*refs):
        d_ref, raw_ref, rg_ref, g_ref, q_ref, k_ref, v_ref, st_ref, w_ref, wt_ref, qd_ref, kd_ref, cd_ref = refs[:13]
        dq_ref, dk_ref, dv_ref, drg_ref, dg_ref = refs[13 + n_w:18 + n_w]
        gstate = refs[18 + 2 * n_w]
        comm = (refs[13:13 + n_w], refs[18 + n_w:18 + 2 * n_w]) + tuple(refs[19 + 2 * n_w:])
        step = pl.program_id(0) * n_b + pl.program_id(1)

        @pl.when(step == 0)
        def _():
            _exchange_phase(0, *comm)

        @pl.when(pl.program_id(1) == 0)
        def _():
            gstate[...] = jnp.zeros_like(gstate)
            dg_ref[...] = jnp.zeros_like(dg_ref)

        d, raw_v, g = d_ref[...], raw_ref[...], g_ref[0]
        gate = rg_ref[...].astype(F32)
        xc = raw_v - _group_mean64(raw_v)
        r = lax.rsqrt(_group_mean64(xc * xc) + EPS)
        xh = xc * r
        sg = _sigmoid(gate)
        drg_ref[...] = d * (xh * g) * (sg * (1.0 + gate * (1.0 - sg)))
        dy = d * (gate * sg)
        dg_ref[0] += jnp.sum(dy * xh, axis=0, keepdims=True)
        dxh = dy * g
        do2 = r * (dxh - _group_mean64(dxh) - xh * _group_mean64(dxh * xh))
        q2, k2, v2 = q_ref[...], k_ref[...], v_ref[...]
        dqs, dks, dvs = [], [], []
        heads = [tuple(t[:, hh * HEAD_DIM:(hh + 1) * HEAD_DIM] for t in (q2, k2, v2, do2.astype(BF))) for hh in range(2)]
        firsts = [(_dot_nt(k, q) * wt_ref[hh], _dot_nt(do, v) * w_ref[hh], _dot_nt(v, do) * wt_ref[hh])
                  for hh, (q, k, v, do) in enumerate(heads)]
        for hh, (q, k, v, do) in enumerate(heads):
            a_t, dm, dm_t = firsts[hh]
            sp, gs = st_ref[0, 0, hh], gstate[hh]
            qd = q.astype(F32) * qd_ref[hh]
            kd = k.astype(F32) * kd_ref[hh]
            dqs.append(_dot(dm, k) + _dot_nt(do, sp) * qd_ref[hh])
            dks.append(_dot(dm_t, q) + _dot_nt(v, gs) * kd_ref[hh])
            dvs.append(_dot(a_t, do) + _dot(kd, gs))
            gstate[hh] = gs * cd_ref[hh] + _dot_tn(qd, do)
        dq_ref[...] = jnp.concatenate(dqs, axis=-1)
        dk_ref[...] = jnp.concatenate(dks, axis=-1)
        dv_ref[...] = jnp.concatenate(dvs, axis=-1)

        @pl.when(step == n_steps - 1)
        def _():
            _exchange_phase(1, *comm)

    blk = lambda col0: pl.BlockSpec((c, LANES), lambda hp, i: (n_b - 1 - i, col0 + hp))
    tab = lambda a: pl.BlockSpec((2,) + a.shape[1:], lambda hp, i: (hp, 0, 0))
    gspec = pl.BlockSpec((1, 1, LANES), lambda hp, i: (hp, 0, 0))
    return pl.pallas_call(
        body, name="retention_bwd", grid=(N_HEADS // 2, n_b),
        out_shape=(jax.ShapeDtypeStruct((t_len, GROUP_W), F32),) * 4 + (jax.ShapeDtypeStruct((N_HEADS // 2, 1, LANES), F32),)
        + _exchange_out_shapes(parts),
        in_specs=[blk(0), blk(0), blk(g_col), gspec, blk(0), blk(0), blk(v_col),
                  pl.BlockSpec((1, 1, 2, HEAD_DIM, HEAD_DIM), lambda hp, i: (hp, n_b - 1 - i, 0, 0, 0)),
                  tab(wdec), tab(wdec), tab(qdec), tab(kdec), tab(cdec)] + [ANY] * n_w,
        out_specs=(blk(0), blk(0), blk(0), blk(0), gspec) + (ANY,) * n_w,
        scratch_shapes=[pltpu.VMEM((2, HEAD_DIM, HEAD_DIM), F32)] + _exchange_scratch(n_w),
        compiler_params=_cparams(("arbitrary", "arbitrary")),
    )(dmr, raw, proj, g_ret, rq, rk, proj, states, wdec, jnp.transpose(wdec, (0, 2, 1)), qdec, kdec, cdec, *parts)


def _in_proj_bwd(x, g_mix, dh1, dq_r, dk_r, dv_r, drg, dq_f, dk_f, dv_f, df_col, proj, z, cos_t, sin_t, gq_t, gk_t, w_in_t):
    t_len = x.shape[0]
    tm = min(ROW_TILE, t_len)
    n_t = t_len // tm

    def body(x_ref, g_ref, dh1_ref, dqr_ref, dkr_ref, dvr_ref, drg_ref, dqf_ref, dkf_ref, dvf_ref, df_ref, fq_ref, fk_ref, z_ref,
             cos_ref, sin_ref, gq_ref, gk_ref, wm_ref, wf_ref,
             dproj_ref, dz_ref, dx_ref, dg_ref, dgq_ref, dgk_ref, db_ref, carry, gq_acc, gk_acc):
        i = pl.program_id(0)

        @pl.when(i == 0)
        def _():
            carry[...] = jnp.zeros_like(carry)
            gq_acc[...] = jnp.zeros_like(gq_acc)
            gk_acc[...] = jnp.zeros_like(gk_acc)
            dg_ref[...] = jnp.zeros_like(dg_ref)
            db_ref[...] = jnp.zeros_like(db_ref)

        c, s = cos_ref[...], sin_ref[...]
        gq, gk = gq_ref[...], gk_ref[...]
        dgq = jnp.zeros((1, LANES), F32)
        dgk = jnp.zeros((1, LANES), F32)
        for sl in _chunks(GROUP_W):
            dy = dqr_ref[:, sl] * 0.125
            dproj_ref[:, sl] = (dy * c + _swap32(dy * s)).astype(BF)
            dy = dkr_ref[:, sl]
            dproj_ref[:, GROUP_W + sl.start:GROUP_W + sl.stop] = (dy * c + _swap32(dy * s)).astype(BF)
            dproj_ref[:, 2 * GROUP_W + sl.start:2 * GROUP_W + sl.stop] = dvr_ref[:, sl].astype(BF)
            dproj_ref[:, 3 * GROUP_W + sl.start:3 * GROUP_W + sl.stop] = drg_ref[:, sl].astype(BF)
            for src, dsrc, gain, off in ((fq_ref, dqf_ref, gq, 4), (fk_ref, dkf_ref, gk, 5)):
                xr = src[:, sl].astype(F32)
                r = lax.rsqrt(_group_mean64(xr * xr) + EPS)
                xh = xr * r
                dy = dsrc[:, sl] * (0.125 if off == 4 else 1.0)
                dgs = jnp.sum(dy * xh, axis=0, keepdims=True)
                if off == 4:
                    dgq = dgq + dgs
                else:
                    dgk = dgk + dgs
                dxh = dy * gain
                dproj_ref[:, off * GROUP_W + sl.start:off * GROUP_W + sl.stop] = \
                    (r * (dxh - xh * _group_mean64(dxh * xh))).astype(BF)
            dproj_ref[:, 6 * GROUP_W + sl.start:6 * GROUP_W + sl.stop] = dvf_ref[:, sl].astype(BF)
        gq_acc[...] += dgq
        gk_acc[...] += dgk
        row = lax.broadcasted_iota(jnp.int32, (tm, tm), 0)
        col = lax.broadcasted_iota(jnp.int32, (tm, tm), 1)
        dlf = _dot_exact((col >= row).astype(BF), df_ref[...]) + carry[0:1, :]
        carry[...] = jnp.broadcast_to(dlf[0:1, :], carry.shape)
        lane = lax.broadcasted_iota(jnp.int32, (tm, LANES), 1)
        dz = jnp.where(lane < N_HEADS, dlf / (1.0 + jnp.exp(z_ref[...])), 0.0)
        db_ref[...] += jnp.sum(dz, axis=0, keepdims=True)
        dz_bf = dz.astype(BF)
        dz_ref[...] = dz_bf
        dn1 = jnp.dot(dz_bf, wf_ref[...], preferred_element_type=F32)
        for sec in range(MAIN_W // GROUP_W):
            sl = slice(sec * GROUP_W, (sec + 1) * GROUP_W)
            dn1 = dn1 + jnp.dot(dproj_ref[:, sl], wm_ref[sl, :], preferred_element_type=F32)
        dx, dg = _rms_bwd(x_ref[...], g_ref[...], dn1)
        dx_ref[...] = dh1_ref[...] + dx
        dg_ref[...] += dg

        @pl.when(i == n_t - 1)
        def _():
            dgq_ref[...] = gq_acc[:, :HEAD_DIM] + gq_acc[:, HEAD_DIM:]
            dgk_ref[...] = gk_acc[:, :HEAD_DIM] + gk_acc[:, HEAD_DIM:]

    row_spec = lambda w, col=0: pl.BlockSpec((tm, w), lambda i: (n_t - 1 - i, col))
    full = lambda a: pl.BlockSpec(a.shape, lambda i: (0,) * a.ndim)
    acc = lambda r, c: pl.BlockSpec((r, c), lambda i: (0, 0))
    return pl.pallas_call(
        body, name="in_proj_bwd", grid=(n_t,),
        out_shape=(jax.ShapeDtypeStruct((t_len, MAIN_W), BF), jax.ShapeDtypeStruct((t_len, LANES), BF),
                   jax.ShapeDtypeStruct((t_len, D_MODEL), F32), jax.ShapeDtypeStruct((1, D_MODEL), F32),
                   jax.ShapeDtypeStruct((1, HEAD_DIM), F32), jax.ShapeDtypeStruct((1, HEAD_DIM), F32),
                   jax.ShapeDtypeStruct((1, LANES), F32)),
        in_specs=[row_spec(D_MODEL), full(g_mix), row_spec(D_MODEL)] + [row_spec(GROUP_W)] * 7
        + [row_spec(LANES), row_spec(GROUP_W, 4), row_spec(GROUP_W, 5), row_spec(LANES), row_spec(LANES), row_spec(LANES),
           full(gq_t), full(gk_t), *_w_in_specs()],
        out_specs=(row_spec(MAIN_W), row_spec(LANES), row_spec(D_MODEL), acc(1, D_MODEL), acc(1, HEAD_DIM), acc(1, HEAD_DIM),
                   acc(1, LANES)),
        scratch_shapes=[pltpu.VMEM((8, LANES), F32), pltpu.VMEM((1, LANES), F32), pltpu.VMEM((1, LANES), F32)],
        compiler_params=_cparams(("arbitrary",)),
    )(x, g_mix, dh1, dq_r, dk_r, dv_r, drg, dq_f, dk_f, dv_f, df_col, proj, proj, z, cos_t, sin_t, gq_t, gk_t, w_in_t, w_in_t)


def _matmul_tn(a, b, name, bk=512):
    t_len, m = a.shape
    n = b.shape[1]
    bm = m if m <= TN_MAX_ROWS else m // 2
    bk = min(bk, t_len)

    def body(a_ref, b_ref, o_ref):
        @pl.when(pl.program_id(1) == 0)
        def _():
            o_ref[...] = jnp.zeros_like(o_ref)

        o_ref[...] += _dot_tn(a_ref[...], b_ref[...])

    return pl.pallas_call(
        body, name=name, grid=(m // bm, t_len // bk),
        out_shape=jax.ShapeDtypeStruct((m, n), F32),
        in_specs=[pl.BlockSpec((bk, bm), lambda i, k: (k, i)), pl.BlockSpec((bk, n), lambda i, k: (k, 0))],
        out_specs=pl.BlockSpec((bm, n), lambda i, k: (i, 0)),
        compiler_params=_cparams(("arbitrary", "arbitrary")),
    )(a, b)


def _place():
    x, y, c = lax.axis_index("x"), lax.axis_index("y"), lax.axis_index("c")
    chips = [(1 - x, y), (x, 1 - y), (1 - x, 1 - y)]
    return x, y, c, chips


def _row_chunks(rows, limit):
    step = max(d for d in range(16, min(rows, limit) + 1, 16) if rows % d == 0)
    return [slice(i, i + step) for i in range(0, rows, step)]


ICI_CHUNK_ROWS = 128
D2D_CHUNK_ROWS = 64


def _gather_phase(phase, ins, outs, send_sems, recv_sems):
    x, y, c, chips = _place()
    me_chip = 2 * x + y
    sibling = (x, y, 1 - c)

    def copy(w, k, slot, half, to, rows=slice(None), src=None):
        dst = outs[w].at[slot, half, rows]
        return pltpu.make_async_remote_copy(src_ref=dst if src is None else src, dst_ref=dst,
                                            send_sem=send_sems.at[w, k], recv_sem=recv_sems.at[w, k],
                                            device_id=to, device_id_type=MESH)

    for w in range(len(ins)):
        for j, (px, py) in enumerate(chips):
            if phase == 0:
                for rows in _row_chunks(ins[w].shape[1], ICI_CHUNK_ROWS):
                    copy(w, j, me_chip, c, (px, py, c), rows, src=ins[w].at[c, rows]).start()
            elif phase == 1:
                copy(w, j, 2 * px + py, c, (x, y, c)).wait_recv()
                for rows in _row_chunks(ins[w].shape[1], D2D_CHUNK_ROWS):
                    copy(w, 3 + j, 2 * px + py, c, sibling, rows).start()
            else:
                copy(w, 3 + j, 2 * px + py, 1 - c, (x, y, c)).wait_recv()
                copy(w, j, me_chip, c, (px, py, c), src=ins[w].at[c]).wait_send()
                copy(w, 3 + j, 2 * px + py, c, sibling).wait_send()


def _gather_scratch(n_w):
    return [pltpu.SemaphoreType.DMA((n_w, 6)), pltpu.SemaphoreType.DMA((n_w, 6))]


def _all_gather_weights(shards):
    n_w = len(shards)

    def body(*refs):
        for phase in range(3):
            _gather_phase(phase, refs[:n_w], refs[n_w:2 * n_w], *refs[2 * n_w:])

    return pl.pallas_call(
        body, name="all_gather_weights",
        out_shape=tuple(jax.ShapeDtypeStruct((4,) + s.shape, s.dtype) for s in shards),
        in_specs=[ANY] * n_w, out_specs=(ANY,) * n_w, scratch_shapes=_gather_scratch(n_w),
    )(*shards)


def _exchange_phase(phase, ins, theirs, send_sems, recv_sems):
    x, y, c, _ = _place()

    def remote(w, k=slice(None), rows=slice(None)):
        return pltpu.make_async_remote_copy(src_ref=ins[w].at[k, 1 - c, rows], dst_ref=theirs[w].at[k, rows],
                                            send_sem=send_sems.at[w], recv_sem=recv_sems.at[w], device_id=(x, y, 1 - c),
                                            device_id_type=MESH)

    for w in range(len(ins)):
        if phase == 0:
            for k in range(4):
                for rows in _row_chunks(ins[w].shape[2], D2D_CHUNK_ROWS):
                    remote(w, k, rows).start()
        else:
            remote(w).wait()


def _exchange_scratch(n_w):
    return [pltpu.SemaphoreType.DMA((n_w,)), pltpu.SemaphoreType.DMA((n_w,))]


def _exchange_out_shapes(grads):
    return tuple(jax.ShapeDtypeStruct((4,) + g.shape[2:], g.dtype) for g in grads)


def _exchange_core_halves(grads):
    n_w = len(grads)

    def body(*refs):
        for phase in range(2):
            _exchange_phase(phase, refs[:n_w], refs[n_w:2 * n_w], *refs[2 * n_w:])

    return pl.pallas_call(
        body, name="exchange_core_halves", out_shape=_exchange_out_shapes(grads),
        in_specs=[ANY] * n_w, out_specs=(ANY,) * n_w, scratch_shapes=_exchange_scratch(n_w),
    )(*grads)


def _add_pairs(part, theirs, name):
    _, _, r, c = part.shape
    rb = 32 if r % 32 == 0 else r

    def body(a_ref, b_ref, own_ref, ob_ref):
        my_chip = 2 * lax.axis_index("x") + lax.axis_index("y")
        ob_ref[...] = (a_ref[...] + b_ref[...]).astype(BF)
        own_ref[...] = a_ref[my_chip] + b_ref[my_chip]

    spec = pl.BlockSpec((4, rb, c), lambda i: (0, i, 0))
    return pl.pallas_call(
        body, name=name, grid=(r // rb,),
        out_shape=(jax.ShapeDtypeStruct((r, c), F32), jax.ShapeDtypeStruct((4, r, c), BF)),
        in_specs=[pl.BlockSpec((4, None, rb, c), lambda i: (0, lax.axis_index("c"), i, 0)), spec],
        out_specs=(pl.BlockSpec((rb, c), lambda i: (i, 0)), spec), compiler_params=_cparams(("arbitrary",)),
    )(part, theirs)


def _scatter_phase(phase, bfs, got, send_sems, recv_sems):
    x, y, c, chips = _place()

    def remote(w, j, px, py, rows=slice(None)):
        return pltpu.make_async_remote_copy(src_ref=bfs[w].at[2 * px + py, rows], dst_ref=got[w].at[j, rows],
                                            send_sem=send_sems.at[w, j], recv_sem=recv_sems.at[w, j], device_id=(px, py, c),
                                            device_id_type=MESH)

    for w in range(len(bfs)):
        for j, (px, py) in enumerate(chips):
            if phase == 0:
                for rows in _row_chunks(bfs[w].shape[1], ICI_CHUNK_ROWS):
                    remote(w, j, px, py, rows).start()
            else:
                remote(w, j, px, py).wait()


def _scatter_scratch(n_w):
    return [pltpu.SemaphoreType.DMA((n_w, 3)), pltpu.SemaphoreType.DMA((n_w, 3))]


def _scatter_out_shapes(sums_bf16):
    return tuple(jax.ShapeDtypeStruct((3,) + s.shape[1:], BF) for s in sums_bf16)


def _add_received(own, got, name):
    r, c = own.shape
    rb = 32 if r % 32 == 0 else r

    def body(o_ref, g_ref, out_ref):
        out_ref[...] = ((o_ref[...] + g_ref[0].astype(F32)) + g_ref[1].astype(F32)) + g_ref[2].astype(F32)

    return pl.pallas_call(
        body, name=name, grid=(r // rb,), out_shape=jax.ShapeDtypeStruct((r, c), F32),
        in_specs=[pl.BlockSpec((rb, c), lambda i: (i, 0)), pl.BlockSpec((3, rb, c), lambda i: (0, i, 0))],
        out_specs=pl.BlockSpec((rb, c), lambda i: (i, 0)), compiler_params=_cparams(("arbitrary",)),
    )(own, got)


def _share_with_sibling(halves):
    n_w = len(halves)

    def body(*refs):
        ins, outs = refs[:n_w], refs[n_w:2 * n_w]
        send_sems, recv_sems = refs[2 * n_w:]
        x, y, c, _ = _place()

        def remote(w, rows=slice(None)):
            return pltpu.make_async_remote_copy(src_ref=ins[w].at[rows], dst_ref=outs[w].at[c, rows], send_sem=send_sems.at[w],
                                                recv_sem=recv_sems.at[w], device_id=(x, y, 1 - c), device_id_type=MESH)

        for w in range(n_w):
            for rows in _row_chunks(ins[w].shape[0], D2D_CHUNK_ROWS):
                remote(w, rows).start()
        for w in range(n_w):
            remote(w).wait()

    return pl.pallas_call(
        body, name="share_with_sibling",
        out_shape=tuple(jax.ShapeDtypeStruct((2,) + h.shape, h.dtype) for h in halves),
        in_specs=[ANY] * n_w, out_specs=(ANY,) * n_w,
        scratch_shapes=[pltpu.SemaphoreType.DMA((n_w,)), pltpu.SemaphoreType.DMA((n_w,))],
    )(*halves)


def _small_phase(phase, p_ref, out_ref, slots, send_sems, recv_sems):
    x, y, cc, _ = _place()
    me = 4 * x + 2 * y + cc
    copies = []
    for k in range(1, 8):
        dx, dy, dc = (k >> 2) & 1, (k >> 1) & 1, k & 1
        to = (1 - x if dx else x, 1 - y if dy else y, 1 - cc if dc else cc)
        copies.append(pltpu.make_async_remote_copy(src_ref=p_ref, dst_ref=slots.at[me], send_sem=send_sems.at[k - 1],
                                                   recv_sem=recv_sems.at[k - 1], device_id=to, device_id_type=MESH))
    if phase == 0:
        slots[me] = p_ref[...]
        for cp in copies:
            cp.start()
    else:
        for cp in copies:
            cp.wait()
        total = slots[0]
        for d in range(1, 8):
            total = total + slots[d]
        out_ref[...] = total


def _adamw_update(w_ref, g_ref, m_ref, v_ref, d_ref, nm_ref, nv_ref):
    gv = g_ref[...]
    nm = ADAM_B1 * m_ref[...] + (1.0 - ADAM_B1) * gv
    nv = ADAM_B2 * v_ref[...] + (1.0 - ADAM_B2) * (gv * gv)
    nm_ref[...] = nm
    nv_ref[...] = nv
    m_hat = nm / (1.0 - ADAM_B1 ** ADAM_STEP)
    v_hat = nv / (1.0 - ADAM_B2 ** ADAM_STEP)
    d_ref[...] = -ADAM_LR * (m_hat / (jnp.sqrt(v_hat) + ADAM_EPS) + ADAM_WD * w_ref[...])


def _adamw_many(ws, gs, ms, vs, sums, pack):
    n_a, n_w = len(ws), len(sums)
    n_steps = ADAM_STEPS
    specs = [pl.BlockSpec((w.shape[0] // n_steps, w.shape[1]), lambda i: (i, 0)) for w in ws]
    pack_spec = pl.BlockSpec(pack.shape, lambda i: (0, 0))

    def body(*refs):
        ins = refs[:4 * n_a]
        p_ref = refs[4 * n_a + n_w]
        first_out = 4 * n_a + n_w + 1
        outs = refs[first_out:first_out + 3 * n_a]
        total_ref = refs[first_out + 3 * n_a + n_w]
        scratch = refs[first_out + 3 * n_a + n_w + 1:]
        scatter = (refs[4 * n_a:4 * n_a + n_w], refs[first_out + 3 * n_a:first_out + 3 * n_a + n_w]) + tuple(scratch[:2])
        small = (p_ref, total_ref) + tuple(scratch[2:])
        step = pl.program_id(0)

        @pl.when(step == 0)
        def _():
            _scatter_phase(0, *scatter)
            _small_phase(0, *small)

        for a in range(n_a):
            _adamw_update(*(ins[k * n_a + a] for k in range(4)), *(outs[3 * a + k] for k in range(3)))

        @pl.when(step == n_steps - 1)
        def _():
            _scatter_phase(1, *scatter)
            _small_phase(1, *small)

    flat = pl.pallas_call(
        body, name="adamw_late", grid=(n_steps,),
        out_shape=tuple(jax.ShapeDtypeStruct(w.shape, F32) for w in ws for _ in range(3)) + _scatter_out_shapes(sums)
        + (jax.ShapeDtypeStruct(pack.shape, F32),),
        in_specs=specs * 4 + [ANY] * n_w + [pack_spec],
        out_specs=tuple(s for s in specs for _ in range(3)) + (ANY,) * n_w + (pack_spec,),
        scratch_shapes=_scatter_scratch(n_w) + [pltpu.VMEM((8,) + pack.shape, F32), pltpu.SemaphoreType.DMA((7,)),
                                                pltpu.SemaphoreType.DMA((7,))],
        compiler_params=_cparams(("arbitrary",)),
    )(*ws, *gs, *ms, *vs, *sums, pack)
    return [tuple(flat[3 * a:3 * a + 3]) for a in range(n_a)] + list(flat[3 * n_a:])


def _adamw(w, g, m, v, name):
    r, c = w.shape
    rb, cb = (64, c) if r % 64 == 0 else (r, LANES if (r % 8 and c % LANES == 0) else c)

    def body(*refs):
        _adamw_update(*refs)

    spec = pl.BlockSpec((rb, cb), lambda i, j: (i, j))
    return pl.pallas_call(
        body, name=name, grid=(r // rb, c // cb), out_shape=(jax.ShapeDtypeStruct((r, c), F32),) * 3,
        in_specs=[spec] * 4, out_specs=(spec,) * 3, compiler_params=_cparams(("arbitrary", "arbitrary")),
    )(w, g, m, v)


def _rope_tables(t_len):
    inv_freq = ROPE_BASE ** (-jnp.arange(0, HEAD_DIM, 2, dtype=F32) / HEAD_DIM)
    ang = jnp.arange(t_len, dtype=F32)[:, None] * inv_freq[None, :]
    cos, sin = jnp.cos(ang), jnp.sin(ang)
    cos_t = jnp.concatenate([cos, cos, cos, cos], axis=-1)
    sin_t = jnp.concatenate([-sin, sin, -sin, sin], axis=-1)
    return cos_t, sin_t


def _cols_to_shards(dw):
    r, n = dw.shape
    return jnp.transpose(dw.reshape(2, r // 2, 4, n // 4), (2, 0, 1, 3))


def _rows_to_shards(dw):
    r, n = dw.shape
    padded = _pad_rows(dw.reshape(4, r // 4, n))
    return padded.reshape(4, 2, padded.shape[1] // 2, n)


def _pad_lanes(a):
    extra = -a.shape[-1] % LANES
    return a if extra == 0 else jnp.pad(a, [(0, 0)] * (a.ndim - 1) + [(0, extra)])


def _pad_rows(a):
    rows = a.shape[-2]
    extra = 0 if rows % SHARD_ROW_ALIGN == 0 else -rows % SHARD_ROW_PAD
    return a if extra == 0 else jnp.pad(a, [(0, 0)] * (a.ndim - 2) + [(0, extra), (0, 0)])


def _pad_row(a, width=D_MODEL):
    a = a.reshape(1, -1)
    return jnp.pad(a, ((0, 0), (0, width - a.shape[1])))


def kernel(x, mem, g_mix, w_in, b_forget, g_ret_out, g_fox_q, g_fox_k, w_out, g_xattn, w_xq, w_xkv, g_mem, g_xq, g_xk, w_xo, g_ffn, w_gate, w_up, w_down, loss_target, m_g_mix, m_w_in, m_b_forget, m_g_ret_out, m_g_fox_q, m_g_fox_k, m_w_out, m_g_xattn, m_w_xq, m_w_xkv, m_g_mem, m_g_xq, m_g_xk, m_w_xo, m_g_ffn, m_w_gate, m_w_up, m_w_down, v_g_mix, v_w_in, v_b_forget, v_g_ret_out, v_g_fox_q, v_g_fox_k, v_w_out, v_g_xattn, v_w_xq, v_w_xkv, v_g_mem, v_g_xq, v_g_xk, v_w_xo, v_g_ffn, v_w_gate, v_w_up, v_w_down):
    big = {"w_in": (w_in, m_w_in, v_w_in), "w_out": (w_out, m_w_out, v_w_out), "w_xq": (w_xq, m_w_xq, v_w_xq),
           "w_xkv": (w_xkv, m_w_xkv, v_w_xkv), "w_xo": (w_xo, m_w_xo, v_w_xo), "w_gate": (w_gate, m_w_gate, v_w_gate),
           "w_up": (w_up, m_w_up, v_w_up), "w_down": (w_down, m_w_down, v_w_down)}
    for n in TRANSPOSED:
        big[n] = tuple(jnp.swapaxes(a, 1, 2) for a in big[n])
    shards = {}
    for n in big:
        w = _pad_rows(_pad_lanes(big[n][0][0].astype(BF)))
        shards[n] = w.reshape(2, w.shape[0] // 2, w.shape[1])
    sizes = {n: big[n][0].shape[1:] for n in big}
    w_in_full = _assemble_weight("w_in", _all_gather_weights([shards["w_in"]])[0], shards["w_in"], sizes["w_in"])
    small_w ={"g_mix": g_mix, "b_forget": b_forget, "g_ret_out": g_ret_out, "g_fox_q": g_fox_q, "g_fox_k": g_fox_k,
               "g_xattn": g_xattn, "g_mem": g_mem, "g_xq": g_xq, "g_xk": g_xk, "g_ffn": g_ffn}
    m_small = {"g_mix": m_g_mix, "b_forget": m_b_forget, "g_ret_out": m_g_ret_out, "g_fox_q": m_g_fox_q, "g_fox_k": m_g_fox_k,
               "g_xattn": m_g_xattn, "g_mem": m_g_mem, "g_xq": m_g_xq, "g_xk": m_g_xk, "g_ffn": m_g_ffn}
    v_small = {"g_mix": v_g_mix, "b_forget": v_b_forget, "g_ret_out": v_g_ret_out, "g_fox_q": v_g_fox_q, "g_fox_k": v_g_fox_k,
               "g_xattn": v_g_xattn, "g_mem": v_g_mem, "g_xq": v_g_xq, "g_xk": v_g_xk, "g_ffn": v_g_ffn}
    loss_part, grad_x, sums, got, in_bf, small_g = _local_step(x[0], mem[0], loss_target[0], w_in_full, shards, sizes, small_w)
    return _reduce_and_update(big, sums, got, in_bf, small_w, small_g, loss_part, grad_x, m_small, v_small)


def _assemble_weight(name, gathered, own, size):
    rows, width = size
    my_chip = 2 * lax.axis_index("x") + lax.axis_index("y")
    g = lax.dynamic_update_slice(gathered, own[None], (my_chip, 0, 0, 0))
    g = g.reshape(4, 2 * g.shape[2], g.shape[3])[:, :rows, :width]
    return jnp.transpose(g, (1, 0, 2)).reshape(rows, 4 * width) if name in COL_SHARDED else g.reshape(4 * rows, width)


def _shard_parts(names, dw):
    return [_pad_lanes(_cols_to_shards(dw[n]) if n in COL_SHARDED else _rows_to_shards(dw[n])) for n in names]


def _core_sums(names, parts, theirs):
    return [_add_pairs(p, t, f"core_sum_{n}") for n, p, t in zip(names, parts, theirs)]


def _local_step(xs, mems, tgt, w_in_full, shards, sizes, small_w):
    g_mix, b_forget, g_ret_out, g_fox_q, g_fox_k = (small_w[n] for n in ("g_mix", "b_forget", "g_ret_out", "g_fox_q", "g_fox_k"))
    g_xattn, g_mem, g_xq, g_xk, g_ffn = (small_w[n] for n in ("g_xattn", "g_mem", "g_xq", "g_xk", "g_ffn"))
    w_in_t = jnp.pad(w_in_full, ((0, MAIN_W + LANES - IN_W), (0, 0)))
    t_len = xs.shape[0]
    cos_t, sin_t = _rope_tables(t_len)
    tables = _decay_tables(min(RET_BLOCK, t_len))
    gq_t = jnp.concatenate([g_fox_q, g_fox_q], axis=-1)
    gk_t = jnp.concatenate([g_fox_k, g_fox_k], axis=-1)
    b_pad = _pad_row(b_forget, LANES)
    g_ret = g_ret_out.reshape(N_HEADS // 2, 1, LANES)

    n1, proj, rq, rk, q_aug, k_aug, z = _in_proj_fwd(xs, g_mix, w_in_t, b_pad, cos_t, sin_t, gq_t, gk_t)
    raw, mix_r, states = _retention_fwd(rq, rk, proj, g_ret, tables)
    mix_f, o32, lse, *gathered = _fox_fwd(q_aug, k_aug, proj, [shards[n] for n in LATE])
    full = {n: _assemble_weight(n, g, shards[n], sizes[n]) for n, g in zip(LATE, gathered)}
    memn, kraw, kn, vmem = _mem_kv_fwd(mems, g_mem, full["w_xkv"], g_xk)
    h1, hn2, qx, o_x, h2 = _attn_out_xattn_fwd(xs, mix_r, mix_f, full["w_out"], g_xattn, full["w_xq"], g_xq, kn, vmem, full["w_xo"])
    hn3, gate, up, act, dh3, loss_part = _ffn_loss_fwd(h2, g_ffn, full["w_gate"], full["w_up"], full["w_down"], tgt)

    dgate, dup, dh2, dg_ffn = _ffn_bwd(dh3, gate, up, h2, g_ffn, full["w_gate"], full["w_up"], full["w_down"])
    dqx, dh1, dmr, dmf, dkn, dvm, dg_xattn, dg_xq = _attn_out_xattn_bwd(dh2, h1, qx, kn, vmem, full["w_xo"], full["w_xq"],
                                                                      full["w_out"], g_xattn, g_xq)
    dw_xkv, dg_mem, dg_xk = _mem_kv_bwd(dkn, dvm, kraw, mems, memn, g_mem, g_xk, full["w_xkv"])
    dw = {
        "w_out": jnp.concatenate([_matmul_tn(mix_r, dh1, "dw_out_ret"), _matmul_tn(mix_f, dh1, "dw_out_fox")], axis=0),
        "w_xq": _matmul_tn(hn2, dqx, "dw_xq"),
        "w_xkv": dw_xkv,
        "w_xo": _matmul_tn(o_x, dh2, "dw_xo"),
        "w_gate": _matmul_tn(dgate, hn3, "dw_gate"),
        "w_up": _matmul_tn(dup, hn3, "dw_up"),
        "w_down": _matmul_tn(act, dh3, "dw_down"),
    }
    late_parts = _shard_parts(LATE, dw)
    dq_r, dk_r, dv_r, drg, dg_ret, *late_theirs = _retention_bwd(dmr, raw, proj, g_ret, rq, rk, states, tables, late_parts)
    late_sums = _core_sums(LATE, late_parts, late_theirs)
    dq_f, dk_f, dv_f, df, *late_got = _fox_bwd(q_aug, k_aug, proj, dmf, o32, lse, [s[1] for s in late_sums])
    df_col = jnp.pad(jnp.transpose(df, (1, 0, 2)).reshape(t_len, N_HEADS), ((0, 0), (0, LANES - N_HEADS)))
    dproj, dz, grad_x, dg_mix, dg_fq, dg_fk, db = _in_proj_bwd(xs, g_mix, dh1, dq_r, dk_r, dv_r, drg, dq_f, dk_f, dv_f, df_col,
                                                              proj, z, cos_t, sin_t, gq_t, gk_t, w_in_t)

    dw_in = jnp.concatenate([_matmul_tn(dproj, n1, "dw_in_main"), _matmul_tn(dz, n1, "dw_in_ff")[:IN_W - MAIN_W]], axis=0)
    in_parts = _shard_parts(("w_in",), {"w_in": dw_in})
    in_sums = _core_sums(("w_in",), in_parts, _exchange_core_halves(in_parts))
    sums = {n: s[0] for n, s in zip(("w_in",) + LATE, in_sums + late_sums)}
    got = dict(zip(LATE, late_got))
    in_bf = in_sums[0][1]
    small_g = {"g_mix": dg_mix, "b_forget": db[:, :N_HEADS], "g_ret_out": dg_ret, "g_fox_q": dg_fq, "g_fox_k": dg_fk,
               "g_xattn": dg_xattn, "g_mem": dg_mem, "g_xq": dg_xq, "g_xk": dg_xk, "g_ffn": dg_ffn}
    return loss_part, grad_x, sums, got, in_bf, small_g


def _final_grads(names, big, sums, got):
    my_core = lax.axis_index("c")
    finals = [_add_received(sums[n], got[n], f"chip_sum_{n}") for n in names]
    shared = _share_with_sibling(finals)
    out = {}
    for n, s, fin in zip(names, shared, finals):
        s = lax.dynamic_update_slice(s, fin[None], (my_core, 0, 0))
        out[n] = s.reshape(2 * s.shape[1], s.shape[2])[:big[n][0].shape[1], :big[n][0].shape[2]]
    return out


def _reduce_and_update(big, sums, got, in_bf, small_w, small_g, loss_part, grad_x, m_small, v_small):
    small_names = list(small_w)
    pad_rows = SMALL_ROWS - len(small_names) - 1
    stack = lambda d: jnp.concatenate([_pad_row(d[n]) for n in small_names] + [jnp.zeros((pad_rows + 1, D_MODEL), F32)], axis=0)
    g_pack = jnp.concatenate([_pad_row(small_g[n]) for n in small_names] + [_pad_row(loss_part[0:1, 0:1])]
                             + [jnp.zeros((pad_rows, D_MODEL), F32)], axis=0)
    grads = _final_grads(LATE, big, sums, got)
    *late_updates, in_got, g_tot = _adamw_many([big[n][0][0] for n in LATE], [grads[n] for n in LATE], [big[n][1][0] for n in LATE],
                                               [big[n][2][0] for n in LATE], [in_bf], g_pack)
    updates = dict(zip(LATE, late_updates))
    grads.update(_final_grads(("w_in",), big, sums, {"w_in": in_got}))
    updates["w_in"] = _adamw(big["w_in"][0][0], grads["w_in"], big["w_in"][1][0], big["w_in"][2][0], "adamw_w_in")
    deltas, new_m, new_v = {}, {}, {}
    for n in big:
        restore = (lambda a: jnp.swapaxes(a[None], 1, 2)) if n in TRANSPOSED else (lambda a: a[None])
        grads[n] = restore(grads[n])
        deltas[n], new_m[n], new_v[n] = (restore(a) for a in updates[n])

    d_s, m_s, v_s = _adamw(stack(small_w), g_tot, stack(m_small), stack(v_small), "adamw_small")
    for i, n in enumerate(small_names):
        shape = small_w[n].shape
        size = int(np.prod(shape))
        grads[n] = g_tot[i, :size].reshape(shape)
        deltas[n], new_m[n], new_v[n] = d_s[i, :size].reshape(shape), m_s[i, :size].reshape(shape), v_s[i, :size].reshape(shape)
    loss = g_tot[len(small_names), 0]

    order = ["g_mix", "w_in", "b_forget", "g_ret_out", "g_fox_q", "g_fox_k", "w_out", "g_xattn", "w_xq", "w_xkv", "g_mem", "g_xq",
             "g_xk", "w_xo", "g_ffn", "w_gate", "w_up", "w_down"]
    return (loss, grad_x[None], *[grads[n] for n in order], *[deltas[n] for n in order], *[new_m[n] for n in order],
            *[new_v[n] for n in order])
```

```python
import functools

import numpy as np
import jax
import jax.numpy as jnp
from jax import lax
from jax.experimental import pallas as pl
from jax.experimental.pallas import tpu as pltpu

F32 = jnp.float32
BF = jnp.bfloat16

D_MODEL = 1024
HEAD_DIM = 64
N_HEADS = 8
GROUP_W = 512
N_XH = 4
XHD = 256
D_FF = 2816
MAIN_W = 3584
IN_W = 3592
ROPE_BASE = 10000.0
LOG2E = 1.4426950408889634
LN2 = 0.6931471805599453
EPS = 1e-6
NEG = -1e30
LANES = 128
RET_BLOCK = 256
REF_CHUNK = 64
ROW_TILE = 512
FFN_BWD_TILE = 256
ATT_BLOCK = 256
FWD_GROUP = 4
TN_MAX_ROWS = 1408
SMALL_ROWS = 16
COL_SHARDED = ("w_xkv",)
TRANSPOSED = ("w_in", "w_gate", "w_up")
SHARD_ROW_ALIGN = 32
SHARD_ROW_PAD = 256
LATE = ("w_out", "w_xq", "w_xkv", "w_xo", "w_gate", "w_up", "w_down")
VMEM_LIMIT = 56 * 1024 * 1024

ADAM_LR = 0.001
ADAM_B1 = 0.9
ADAM_B2 = 0.999
ADAM_EPS = 1e-08
ADAM_WD = 0.01
ADAM_STEP = 10
ADAM_STEPS = 8

MESH = pl.DeviceIdType.MESH
ANY = pl.BlockSpec(memory_space=pl.ANY)
VMEM_SPEC = pl.BlockSpec(memory_space=pltpu.VMEM)


def _cparams(sem=None, vmem=VMEM_LIMIT):
    return pltpu.CompilerParams(dimension_semantics=sem, vmem_limit_bytes=vmem)


def _dot(a, b):
    return jnp.dot(a.astype(BF), b.astype(BF), preferred_element_type=F32)


def _dot_nt(a, b):
    return lax.dot_general(a.astype(BF), b.astype(BF), (((1,), (1,)), ((), ())), preferred_element_type=F32)


def _dot_tn(a, b):
    return lax.dot_general(a.astype(BF), b.astype(BF), (((0,), (0,)), ((), ())), preferred_element_type=F32)


def _split3(x):
    hi = x.astype(BF)
    r = x - hi.astype(F32)
    mid = r.astype(BF)
    lo = (r - mid.astype(F32)).astype(BF)
    return hi, mid, lo


def _dot_exact(ind, x):
    hi, mid, lo = _split3(x)
    return (jnp.dot(ind, lo, preferred_element_type=F32) + jnp.dot(ind, mid, preferred_element_type=F32)
            + jnp.dot(ind, hi, preferred_element_type=F32))


def _dot_nt_exact(ind, x):
    hi, mid, lo = _split3(x)
    dn = (((1,), (1,)), ((), ()))
    return (lax.dot_general(ind, lo, dn, preferred_element_type=F32) + lax.dot_general(ind, mid, dn, preferred_element_type=F32)
            + lax.dot_general(ind, hi, dn, preferred_element_type=F32))


def _sigmoid(x):
    return 1.0 / (1.0 + jnp.exp(-x))


def _rms_fwd(x, g):
    r = lax.rsqrt(jnp.mean(x * x, axis=-1, keepdims=True) + EPS)
    return x * r * g


def _rms_bwd(x, g, dy):
    r = lax.rsqrt(jnp.mean(x * x, axis=-1, keepdims=True) + EPS)
    xh = x * r
    dg = jnp.sum(dy * xh, axis=0, keepdims=True)
    dxh = dy * g
    dx = r * (dxh - xh * jnp.mean(dxh * xh, axis=-1, keepdims=True))
    return dx, dg


def _group_mean64(x):
    lane = lax.broadcasted_iota(jnp.int32, x.shape, 1)
    lo = lane < HEAD_DIM
    s_lo = jnp.sum(jnp.where(lo, x, 0.0), axis=-1, keepdims=True)
    s_hi = jnp.sum(jnp.where(lo, 0.0, x), axis=-1, keepdims=True)
    return jnp.where(lo, s_lo, s_hi) * (1.0 / HEAD_DIM)


def _swap32(x):
    lane = lax.broadcasted_iota(jnp.int32, x.shape, 1)
    first = (lane % HEAD_DIM) < (HEAD_DIM // 2)
    return jnp.where(first, pltpu.roll(x, LANES - HEAD_DIM // 2, axis=1), pltpu.roll(x, HEAD_DIM // 2, axis=1))


def _chunks(w):
    return [slice(j * LANES, (j + 1) * LANES) for j in range(w // LANES)]


def _aug_pair(qk, f_cols, is_query):
    lane = lax.broadcasted_iota(jnp.int32, qk.shape, 1)
    a = lane - HEAD_DIM
    values = (qk, pltpu.roll(qk, HEAD_DIM, axis=1))
    out = []
    for hh in range(2):
        hi, mid, lo = (p.astype(F32) for p in _split3(f_cols[hh] * LOG2E))
        if is_query:
            aux = jnp.where(a == 0, hi, jnp.where(a == 1, mid, jnp.where(a == 2, lo, jnp.where(a < 6, 1.0, 0.0))))
        else:
            aux = jnp.where(a < 3, 1.0, jnp.where(a == 3, -hi, jnp.where(a == 4, -mid, jnp.where(a == 5, -lo, 0.0))))
        out.append(jnp.where(a < 0, values[hh], aux))
    return jnp.concatenate(out, axis=-1).astype(BF)


def _mem_kv_fwd(mem, g_mem, w_xkv, g_xk):
    m_tok = mem.shape[0]

    def body(mem_ref, gm_ref, w_ref, gk_ref, memn_ref, kraw_ref, kn_ref, v_ref):
        mn = _rms_fwd(mem_ref[...], gm_ref[...]).astype(BF)
        memn_ref[...] = mn
        kv = jnp.dot(mn, w_ref[...], preferred_element_type=F32)
        k = kv[:, :D_MODEL]
        kraw_ref[...] = k
        v_ref[...] = kv[:, D_MODEL:].astype(BF)
        for h in range(N_XH):
            sl = slice(h * XHD, (h + 1) * XHD)
            kn_ref[:, sl] = _rms_fwd(k[:, sl], gk_ref[...]).astype(BF)

    return pl.pallas_call(
        body, name="mem_kv_fwd",
        out_shape=(jax.ShapeDtypeStruct((m_tok, D_MODEL), BF), jax.ShapeDtypeStruct((m_tok, D_MODEL), F32),
                   jax.ShapeDtypeStruct((m_tok, D_MODEL), BF), jax.ShapeDtypeStruct((m_tok, D_MODEL), BF)),
        in_specs=[VMEM_SPEC] * 4, out_specs=(VMEM_SPEC,) * 4, compiler_params=_cparams(),
    )(mem, g_mem, w_xkv, g_xk)


def _in_proj_fwd(x, g_mix, w_in_t, b_pad, cos_t, sin_t, gq_t, gk_t):
    t_len = x.shape[0]
    tm = min(ROW_TILE, t_len)
    n_t = t_len // tm

    def body(x_ref, g_ref, wm_ref, wf_ref, b_ref, cos_ref, sin_ref, gq_ref, gk_ref,
             n1_ref, proj_ref, rq_ref, rk_ref, qa_ref, ka_ref, z_ref, carry):
        i = pl.program_id(0)

        @pl.when(i == 0)
        def _():
            carry[...] = jnp.zeros_like(carry)

        n1 = _rms_fwd(x_ref[...], g_ref[...]).astype(BF)
        n1_ref[...] = n1
        z = _dot_nt(n1, wf_ref[...]) + b_ref[...]
        z_ref[...] = z
        lane = lax.broadcasted_iota(jnp.int32, z.shape, 1)
        lf = jnp.where(lane < N_HEADS, jnp.minimum(z, 0.0) - jnp.log(1.0 + jnp.exp(-jnp.abs(z))), 0.0)
        row = lax.broadcasted_iota(jnp.int32, (tm, tm), 0)
        col = lax.broadcasted_iota(jnp.int32, (tm, tm), 1)
        tri = (row >= col).astype(BF)
        fc = _dot_exact(tri, lf) + carry[0:1, :]
        carry[...] = jnp.broadcast_to(fc[tm - 1:tm, :], carry.shape)
        c, s = cos_ref[...], sin_ref[...]

        def section(n):
            p = _dot_nt(n1, wm_ref[n * GROUP_W:(n + 1) * GROUP_W, :])
            proj_ref[:, n * GROUP_W:(n + 1) * GROUP_W] = p.astype(BF)
            return p

        def rotate(p, out_ref, scale):
            for sl in _chunks(GROUP_W):
                out_ref[:, sl] = ((p[:, sl] * c + _swap32(p[:, sl]) * s) * scale).astype(BF)

        def norm_aug(p, gain, out_ref, scale, is_query):
            for j, sl in enumerate(_chunks(GROUP_W)):
                f = p[:, sl]
                f = f * lax.rsqrt(_group_mean64(f * f) + EPS) * gain * scale
                out_ref[:, 2 * j * LANES:2 * (j + 1) * LANES] = _aug_pair(f, [fc[:, 2 * j:2 * j + 1], fc[:, 2 * j + 1:2 * j + 2]], is_query)

        p_rq, p_rk = section(0), section(1)
        rotate(p_rq, rq_ref, 0.125)
        section(2)
        rotate(p_rk, rk_ref, 1.0)
        section(3)
        p_fq = section(4)
        p_fk = section(5)
        norm_aug(p_fq, gq_ref[...], qa_ref, 0.125 * LOG2E, True)
        section(6)
        norm_aug(p_fk, gk_ref[...], ka_ref, 1.0, False)

    row_spec = lambda w: pl.BlockSpec((tm, w), lambda i: (i, 0))
    full = lambda a: pl.BlockSpec(a.shape, lambda i: (0,) * a.ndim)
    return pl.pallas_call(
        body, name="in_proj_fwd", grid=(n_t,),
        out_shape=(jax.ShapeDtypeStruct((t_len, D_MODEL), BF), jax.ShapeDtypeStruct((t_len, MAIN_W), BF),
                   jax.ShapeDtypeStruct((t_len, GROUP_W), BF), jax.ShapeDtypeStruct((t_len, GROUP_W), BF),
                   jax.ShapeDtypeStruct((t_len, 2 * GROUP_W), BF), jax.ShapeDtypeStruct((t_len, 2 * GROUP_W), BF),
                   jax.ShapeDtypeStruct((t_len, LANES), F32)),
        in_specs=[row_spec(D_MODEL), full(g_mix), *_w_in_specs(), full(b_pad), row_spec(LANES), row_spec(LANES),
                  full(gq_t), full(gk_t)],
        out_specs=(row_spec(D_MODEL), row_spec(MAIN_W), row_spec(GROUP_W), row_spec(GROUP_W), row_spec(2 * GROUP_W),
                   row_spec(2 * GROUP_W), row_spec(LANES)),
        scratch_shapes=[pltpu.VMEM((8, LANES), F32)],
        compiler_params=_cparams(("arbitrary",)),
    )(x, g_mix, w_in_t, w_in_t, b_pad, cos_t, sin_t, gq_t, gk_t)


def _w_in_specs():
    return (pl.BlockSpec((MAIN_W, D_MODEL), lambda i: (0, 0)), pl.BlockSpec((LANES, D_MODEL), lambda i: (MAIN_W // LANES, 0)))


def _decay_tables(c):
    h = np.arange(N_HEADS, dtype=np.float64)
    lg = np.log(1.0 - 2.0 ** (-5.0 - h)).astype(np.float32).astype(np.float64)
    t = np.arange(c)
    same_or_earlier = (t[None, :] // REF_CHUNK) <= (t[:, None] // REF_CHUNK)
    w = np.where(same_or_earlier[None], np.exp(lg[:, None, None] * np.abs(t[:, None] - t[None, :])[None]), 0.0)
    qd = np.exp(lg[:, None] * (t[None, :] + 1.0))
    kd = np.exp(lg[:, None] * (c - 1.0 - t[None, :]))
    cd = np.exp(lg * c)
    ones = np.ones((1, 1, HEAD_DIM))
    return (jnp.asarray(w, F32), jnp.asarray(qd[:, :, None] * ones, F32), jnp.asarray(kd[:, :, None] * ones, F32),
            jnp.asarray(cd[:, None, None] * np.ones((1, HEAD_DIM, HEAD_DIM)), F32))


def _retention_fwd(rq, rk, proj, g_ret, tables):
    t_len = rq.shape[0]
    c = min(RET_BLOCK, t_len)
    n_b = t_len // c
    wdec, qdec, kdec, cdec = tables
    v_col, g_col = 2 * GROUP_W // LANES, 3 * GROUP_W // LANES

    def body(q_ref, k_ref, v_ref, rg_ref, g_ref, w_ref, qd_ref, kd_ref, cd_ref, raw_ref, mix_ref, st_ref, state):
        i = pl.program_id(1)

        @pl.when(i == 0)
        def _():
            state[...] = jnp.zeros_like(state)

        q2, k2, v2 = q_ref[...], k_ref[...], v_ref[...]
        outs = []
        for hh in range(2):
            sl = slice(hh * HEAD_DIM, (hh + 1) * HEAD_DIM)
            q, k, v = q2[:, sl], k2[:, sl], v2[:, sl]
            sp = state[hh]
            st_ref[0, 0, hh] = sp
            a = _dot_nt(q, k) * w_ref[hh]
            o = _dot(a, v) + _dot(q.astype(F32) * qd_ref[hh], sp)
            state[hh] = sp * cd_ref[hh] + _dot_tn(k.astype(F32) * kd_ref[hh], v)
            outs.append(o)
        o2 = jnp.concatenate(outs, axis=-1)
        raw_ref[...] = o2
        xc = o2 - _group_mean64(o2)
        xh = xc * lax.rsqrt(_group_mean64(xc * xc) + EPS)
        gate = rg_ref[...].astype(F32)
        mix_ref[...] = (gate * _sigmoid(gate) * (xh * g_ref[0])).astype(BF)

    blk = lambda col0: pl.BlockSpec((c, LANES), lambda hp, i: (i, col0 + hp))
    tab = lambda a: pl.BlockSpec((2,) + a.shape[1:], lambda hp, i: (hp, 0, 0))
    return pl.pallas_call(
        body, name="retention_fwd", grid=(N_HEADS // 2, n_b),
        out_shape=(jax.ShapeDtypeStruct((t_len, GROUP_W), F32), jax.ShapeDtypeStruct((t_len, GROUP_W), BF),
                   jax.ShapeDtypeStruct((N_HEADS // 2, n_b, 2, HEAD_DIM, HEAD_DIM), F32)),
        in_specs=[blk(0), blk(0), blk(v_col), blk(g_col), pl.BlockSpec((1, 1, LANES), lambda hp, i: (hp, 0, 0)),
                  tab(wdec), tab(qdec), tab(kdec), tab(cdec)],
        out_specs=(blk(0), blk(0), pl.BlockSpec((1, 1, 2, HEAD_DIM, HEAD_DIM), lambda hp, i: (hp, i, 0, 0, 0))),
        scratch_shapes=[pltpu.VMEM((2, HEAD_DIM, HEAD_DIM), F32)],
        compiler_params=_cparams(("arbitrary", "arbitrary")),
    )(rq, rk, proj, proj, g_ret, wdec, qdec, kdec, cdec)


def _fox_fwd(q_aug, k_aug, proj, shards):
    t_len = q_aug.shape[0]
    tq = min(ATT_BLOCK, t_len)
    nsub = min(FWD_GROUP, t_len // tq)
    tg = nsub * tq
    n_q = t_len // tg
    v_col = 6 * GROUP_W // LANES
    tc = min(512, t_len)
    n_w = len(shards)
    n_steps = (N_HEADS // 2) * n_q

    def body(*refs):
        q_ref, k_ref, v_ref = refs[:3]
        o_ref, o32_ref, lse_ref = refs[3 + n_w:6 + n_w]
        vt = refs[6 + 2 * n_w]
        comm = (refs[3:3 + n_w], refs[6 + n_w:6 + 2 * n_w]) + tuple(refs[7 + 2 * n_w:])
        i = pl.program_id(1)
        step = pl.program_id(0) * n_q + i

        @pl.when(step == 0)
        def _():
            _gather_phase(0, *comm)

        @pl.when(step == (3 * n_steps) // 4)
        def _():
            _gather_phase(1, *comm)

        @pl.when(i == 0)
        def _():
            for c0 in range(0, t_len, tc):
                vt[:, c0:c0 + tc] = v_ref[c0:c0 + tc, :].T

        chains = [(u, hh) for u in range(nsub) for hh in range(2)]
        qs = {(u, hh): q_ref[u * tq:(u + 1) * tq, hh * LANES:(hh + 1) * LANES] for u, hh in chains}
        ones = jnp.ones((HEAD_DIM, tq), BF)

        def scores(j, which):
            k2 = k_ref[pl.ds(pl.multiple_of(j * tq, tq), tq), :]
            return {ch: _dot_nt(k2[:, ch[1] * LANES:(ch[1] + 1) * LANES], qs[ch]) for ch in which}

        def update(j, ss, carry, masked):
            v2 = vt[:, pl.ds(pl.multiple_of(j * tq, tq), tq)]
            ps, stats = {}, {}
            for ch in ss:
                m = carry[ch][0]
                s_t = ss[ch]
                if ch in masked:
                    krow = lax.broadcasted_iota(jnp.int32, (tq, tq), 0)
                    qcol = lax.broadcasted_iota(jnp.int32, (tq, tq), 1)
                    s_t = jnp.where(qcol >= krow, s_t, NEG)
                m_new = jnp.maximum(m, jnp.max(s_t, axis=0, keepdims=True))
                ps[ch] = jnp.exp2(s_t - m_new).astype(BF)
                stats[ch] = (m_new, jnp.exp2(m - m_new))
            out = dict(carry)
            for ch in ss:
                m_new, alpha = stats[ch]
                v_aug = jnp.concatenate([v2[ch[1] * HEAD_DIM:(ch[1] + 1) * HEAD_DIM, :], ones], axis=0)
                out[ch] = (m_new, carry[ch][1] * alpha + jnp.dot(v_aug, ps[ch], preferred_element_type=F32))
            return out

        def advance(j, state):
            ss, carry = state
            return scores(j + 1, chains), update(j, ss, carry, ())

        init = {ch: (jnp.full((1, tq), NEG, F32), jnp.zeros((LANES, tq), F32)) for ch in chains}
        first = nsub * i
        ss, carry = lax.fori_loop(0, first, advance, (scores(0, chains), init))
        carry = update(first, ss, carry, [(0, 0), (0, 1)])
        for u in range(1, nsub):
            rest = [(uu, hh) for uu in range(u, nsub) for hh in range(2)]
            carry = update(first + u, scores(first + u, rest), carry, [(u, 0), (u, 1)])
        for u in range(nsub):
            outs, lses = [], []
            for hh in range(2):
                m, acc = carry[u, hh]
                l = acc[HEAD_DIM:HEAD_DIM + 1, :]
                outs.append(acc[:HEAD_DIM, :] / l)
                lses.append(m + jnp.log2(l))
            o2 = jnp.concatenate(outs, axis=0).T
            o32_ref[u * tq:(u + 1) * tq, :] = o2
            o_ref[u * tq:(u + 1) * tq, :] = o2.astype(BF)
            lse_ref[0, :, u * tq:(u + 1) * tq] = jnp.concatenate(lses, axis=0)

        @pl.when(step == n_steps - 1)
        def _():
            _gather_phase(2, *comm)

    return pl.pallas_call(
        body, name="fox_fwd", grid=(N_HEADS // 2, n_q),
        out_shape=(jax.ShapeDtypeStruct((t_len, GROUP_W), BF), jax.ShapeDtypeStruct((t_len, GROUP_W), F32),
                   jax.ShapeDtypeStruct((N_HEADS // 2, 2, t_len), F32))
        + tuple(jax.ShapeDtypeStruct((4,) + s.shape, s.dtype) for s in shards),
        in_specs=[pl.BlockSpec((tg, 2 * LANES), lambda hp, i: (i, hp)),
                  pl.BlockSpec((t_len, 2 * LANES), lambda hp, i: (0, hp)),
                  pl.BlockSpec((t_len, LANES), lambda hp, i: (0, v_col + hp))] + [ANY] * n_w,
        out_specs=(pl.BlockSpec((tg, LANES), lambda hp, i: (i, hp)), pl.BlockSpec((tg, LANES), lambda hp, i: (i, hp)),
                   pl.BlockSpec((1, 2, tg), lambda hp, i: (hp, 0, i))) + (ANY,) * n_w,
        scratch_shapes=[pltpu.VMEM((LANES, t_len), BF)] + _gather_scratch(n_w),
        compiler_params=_cparams(("arbitrary", "arbitrary")),
    )(q_aug, k_aug, proj, *shards)


def _softmax_rows(s):
    p = jnp.exp(s - jnp.max(s, axis=-1, keepdims=True))
    return p / jnp.sum(p, axis=-1, keepdims=True)


def _attn_out_xattn_fwd(x, mix_r, mix_f, w_out, g_xattn, w_xq, g_xq, kn, v, w_xo):
    t_len = x.shape[0]
    tm = min(ROW_TILE, t_len)

    def body(x_ref, mr_ref, mf_ref, wo_ref, g_ref, wq_ref, gq_ref, kn_ref, v_ref, wxo_ref,
             h1_ref, hn_ref, qx_ref, o_ref, h2_ref):
        h1 = x_ref[...] + jnp.dot(mr_ref[...], wo_ref[:GROUP_W, :], preferred_element_type=F32) \
            + jnp.dot(mf_ref[...], wo_ref[GROUP_W:, :], preferred_element_type=F32)
        h1_ref[...] = h1
        hn = _rms_fwd(h1, g_ref[...]).astype(BF)
        hn_ref[...] = hn
        qx = jnp.dot(hn, wq_ref[...], preferred_element_type=F32).astype(BF)
        qx_ref[...] = qx
        sls = [slice(h * XHD, (h + 1) * XHD) for h in range(N_XH)]
        qns = [_rms_fwd(qx[:, sl].astype(F32), gq_ref[...]).astype(BF) for sl in sls]
        logits = [_dot_nt(qn, kn_ref[:, sl]) * (XHD ** -0.5) for qn, sl in zip(qns, sls)]
        ps = [_softmax_rows(s).astype(BF) for s in logits]
        for p, sl in zip(ps, sls):
            o_ref[:, sl] = jnp.dot(p, v_ref[:, sl], preferred_element_type=F32).astype(BF)
        h2_ref[...] = h1 + jnp.dot(o_ref[...], wxo_ref[...], preferred_element_type=F32)

    row_spec = lambda w: pl.BlockSpec((tm, w), lambda i: (i, 0))
    full = lambda a: pl.BlockSpec(a.shape, lambda i: (0,) * a.ndim)
    return pl.pallas_call(
        body, name="attn_out_xattn_fwd", grid=(t_len // tm,),
        out_shape=(jax.ShapeDtypeStruct((t_len, D_MODEL), F32), jax.ShapeDtypeStruct((t_len, D_MODEL), BF),
                   jax.ShapeDtypeStruct((t_len, D_MODEL), BF), jax.ShapeDtypeStruct((t_len, D_MODEL), BF),
                   jax.ShapeDtypeStruct((t_len, D_MODEL), F32)),
        in_specs=[row_spec(D_MODEL), row_spec(GROUP_W), row_spec(GROUP_W), full(w_out), full(g_xattn), full(w_xq), full(g_xq),
                  full(kn), full(v), full(w_xo)],
        out_specs=(row_spec(D_MODEL),) * 5,
        compiler_params=_cparams(("arbitrary",)),
    )(x, mix_r, mix_f, w_out, g_xattn, w_xq, g_xq, kn, v, w_xo)


def _ffn_loss_fwd(h2, g_ffn, w_gate, w_up, w_down, target):
    t_len = h2.shape[0]
    tm = min(ROW_TILE, t_len)

    def body(h2_ref, g_ref, wg_ref, wu_ref, wd_ref, tgt_ref, hn_ref, gate_ref, up_ref, act_ref, dh3_ref, loss_ref):
        @pl.when(pl.program_id(0) == 0)
        def _():
            loss_ref[...] = jnp.zeros_like(loss_ref)

        h2v = h2_ref[...]
        hn = _rms_fwd(h2v, g_ref[...]).astype(BF)
        hn_ref[...] = hn
        gate = _dot_nt(hn, wg_ref[...])
        up = _dot_nt(hn, wu_ref[...])
        gate_ref[...] = gate.astype(BF)
        up_ref[...] = up.astype(BF)
        act = (gate * _sigmoid(gate) * up).astype(BF)
        act_ref[...] = act
        diff = h2v + jnp.dot(act, wd_ref[...], preferred_element_type=F32) - tgt_ref[...]
        dh3_ref[...] = diff * (1.0 / D_MODEL)
        per_row = jnp.sum(diff * diff, axis=-1, keepdims=True) * (1.0 / D_MODEL)
        loss_ref[...] += 0.5 * jnp.sum(per_row, axis=0, keepdims=True)

    row_spec = lambda w: pl.BlockSpec((tm, w), lambda i: (i, 0))
    full = lambda a: pl.BlockSpec(a.shape, lambda i: (0,) * a.ndim, pipeline_mode=pl.Buffered(1))
    return pl.pallas_call(
        body, name="ffn_loss_fwd", grid=(t_len // tm,),
        out_shape=(jax.ShapeDtypeStruct((t_len, D_MODEL), BF), jax.ShapeDtypeStruct((t_len, D_FF), BF),
                   jax.ShapeDtypeStruct((t_len, D_FF), BF), jax.ShapeDtypeStruct((t_len, D_FF), BF),
                   jax.ShapeDtypeStruct((t_len, D_MODEL), F32), jax.ShapeDtypeStruct((8, LANES), F32)),
        in_specs=[row_spec(D_MODEL), full(g_ffn), full(w_gate), full(w_up), full(w_down), row_spec(D_MODEL)],
        out_specs=(row_spec(D_MODEL), row_spec(D_FF), row_spec(D_FF), row_spec(D_FF), row_spec(D_MODEL),
                   pl.BlockSpec((8, LANES), lambda i: (0, 0))),
        compiler_params=_cparams(("arbitrary",)),
    )(h2, g_ffn, w_gate, w_up, w_down, target)


def _ffn_bwd(dh3, gate, up, h2, g_ffn, w_gate, w_up, w_down):
    t_len = h2.shape[0]
    tm = min(FFN_BWD_TILE, t_len)

    def body(dh3_ref, gate_ref, up_ref, h2_ref, g_ref, wg_ref, wu_ref, wd_ref, dgate_ref, dup_ref, dh2_ref, dg_ref):
        @pl.when(pl.program_id(0) == 0)
        def _():
            dg_ref[...] = jnp.zeros_like(dg_ref)

        dh3v = dh3_ref[...]
        dact = _dot_nt(dh3v, wd_ref[...])
        g = gate_ref[...].astype(F32)
        sg = _sigmoid(g)
        dup = (dact * (g * sg)).astype(BF)
        dgate = (dact * up_ref[...].astype(F32) * (sg * (1.0 + g * (1.0 - sg)))).astype(BF)
        dup_ref[...] = dup
        dgate_ref[...] = dgate
        dhn = jnp.dot(dgate, wg_ref[...], preferred_element_type=F32) + jnp.dot(dup, wu_ref[...], preferred_element_type=F32)
        dx, dg = _rms_bwd(h2_ref[...], g_ref[...], dhn)
        dh2_ref[...] = dh3v + dx
        dg_ref[...] += dg

    row_spec = lambda w: pl.BlockSpec((tm, w), lambda i: (i, 0))
    full = lambda a: pl.BlockSpec(a.shape, lambda i: (0,) * a.ndim, pipeline_mode=pl.Buffered(1))
    return pl.pallas_call(
        body, name="ffn_bwd", grid=(t_len // tm,),
        out_shape=(jax.ShapeDtypeStruct((t_len, D_FF), BF), jax.ShapeDtypeStruct((t_len, D_FF), BF),
                   jax.ShapeDtypeStruct((t_len, D_MODEL), F32), jax.ShapeDtypeStruct((1, D_MODEL), F32)),
        in_specs=[row_spec(D_MODEL), row_spec(D_FF), row_spec(D_FF), row_spec(D_MODEL), full(g_ffn), full(w_gate), full(w_up),
                  full(w_down)],
        out_specs=(row_spec(D_FF), row_spec(D_FF), row_spec(D_MODEL), pl.BlockSpec((1, D_MODEL), lambda i: (0, 0))),
        compiler_params=_cparams(("arbitrary",)),
    )(dh3, gate, up, h2, g_ffn, w_gate, w_up, w_down)


def _attn_out_xattn_bwd(dh2, h1, qx, kn, v, w_xo, w_xq, w_out, g_xattn, g_xq):
    t_len = h1.shape[0]
    tm = min(ROW_TILE, t_len)
    m_tok = kn.shape[0]

    def body(dh2_ref, h1_ref, qx_ref, kn_ref, v_ref, wxo_ref, wq_ref, wo_ref, g_ref, gq_ref,
             dqx_ref, dh1_ref, dmr_ref, dmf_ref, dkn_ref, dv_ref, dg_ref, dgq_ref, dqx_scr):
        @pl.when(pl.program_id(0) == 0)
        def _():
            dkn_ref[...] = jnp.zeros_like(dkn_ref)
            dv_ref[...] = jnp.zeros_like(dv_ref)
            dg_ref[...] = jnp.zeros_like(dg_ref)
            dgq_ref[...] = jnp.zeros_like(dgq_ref)

        dh2v = dh2_ref[...]
        do = _dot_nt(dh2v, wxo_ref[...])
        gq = gq_ref[...]
        sls = [slice(h * XHD, (h + 1) * XHD) for h in range(N_XH)]
        qraws = [qx_ref[:, sl].astype(F32) for sl in sls]
        qns = [_rms_fwd(qraw, gq).astype(BF) for qraw in qraws]
        dohs = [do[:, sl].astype(BF) for sl in sls]
        logits = [_dot_nt(qn, kn_ref[:, sl]) * (XHD ** -0.5) for qn, sl in zip(qns, sls)]
        dps = [_dot_nt(doh, v_ref[:, sl]) for doh, sl in zip(dohs, sls)]
        ps = [_softmax_rows(s) for s in logits]
        dss = [(p * (dp - jnp.sum(dp * p, axis=-1, keepdims=True)) * (XHD ** -0.5)).astype(BF) for p, dp in zip(ps, dps)]
        dqns = []
        for h, sl in enumerate(sls):
            dv_ref[:, sl] += _dot_tn(ps[h], dohs[h])
            dqns.append(jnp.dot(dss[h], kn_ref[:, sl], preferred_element_type=F32))
            dkn_ref[:, sl] += _dot_tn(dss[h], qns[h])
        dgq = jnp.zeros((1, XHD), F32)
        for h, sl in enumerate(sls):
            dx, dg_h = _rms_bwd(qraws[h], gq, dqns[h])
            dgq = dgq + dg_h
            dqx_scr[:, sl] = dx.astype(BF)
        dgq_ref[...] += dgq
        dqx = dqx_scr[...]
        dqx_ref[...] = dqx
        dhn = _dot_nt(dqx, wq_ref[...])
        dx, dg = _rms_bwd(h1_ref[...], g_ref[...], dhn)
        dg_ref[...] += dg
        dh1 = dh2v + dx
        dh1_ref[...] = dh1
        dmix = _dot_nt(dh1, wo_ref[...])
        dmr_ref[...] = dmix[:, :GROUP_W]
        dmf_ref[...] = dmix[:, GROUP_W:].astype(BF)

    row_spec = lambda w: pl.BlockSpec((tm, w), lambda i: (i, 0))
    full = lambda a: pl.BlockSpec(a.shape, lambda i: (0,) * a.ndim)
    acc = lambda r, c: pl.BlockSpec((r, c), lambda i: (0, 0))
    return pl.pallas_call(
        body, name="attn_out_xattn_bwd", grid=(t_len // tm,),
        out_shape=(jax.ShapeDtypeStruct((t_len, D_MODEL), BF), jax.ShapeDtypeStruct((t_len, D_MODEL), F32),
                   jax.ShapeDtypeStruct((t_len, GROUP_W), F32), jax.ShapeDtypeStruct((t_len, GROUP_W), BF),
                   jax.ShapeDtypeStruct((m_tok, D_MODEL), F32), jax.ShapeDtypeStruct((m_tok, D_MODEL), F32),
                   jax.ShapeDtypeStruct((1, D_MODEL), F32), jax.ShapeDtypeStruct((1, XHD), F32)),
        in_specs=[row_spec(D_MODEL), row_spec(D_MODEL), row_spec(D_MODEL), full(kn), full(v), full(w_xo), full(w_xq), full(w_out),
                  full(g_xattn), full(g_xq)],
        out_specs=(row_spec(D_MODEL), row_spec(D_MODEL), row_spec(GROUP_W), row_spec(GROUP_W), acc(m_tok, D_MODEL),
                   acc(m_tok, D_MODEL), acc(1, D_MODEL), acc(1, XHD)),
        scratch_shapes=[pltpu.VMEM((tm, D_MODEL), BF)],
        compiler_params=_cparams(("arbitrary",)),
    )(dh2, h1, qx, kn, v, w_xo, w_xq, w_out, g_xattn, g_xq)


def _mem_kv_bwd(dkn, dv, kraw, mem, memn, g_mem, g_xk, w_xkv):
    m_tok = mem.shape[0]

    def body(dkn_ref, dv_ref, kraw_ref, mem_ref, memn_ref, gm_ref, gk_ref, w_ref, dw_ref, dgm_ref, dgk_ref, dkv_scr):
        gk = gk_ref[...]
        dgk = jnp.zeros((1, XHD), F32)
        for h in range(N_XH):
            sl = slice(h * XHD, (h + 1) * XHD)
            dx, dg_h = _rms_bwd(kraw_ref[:, sl], gk, dkn_ref[:, sl])
            dgk = dgk + dg_h
            dkv_scr[:, sl] = dx.astype(BF)
        dgk_ref[...] = dgk
        dkv_scr[:, D_MODEL:] = dv_ref[...].astype(BF)
        dkv = dkv_scr[...]
        dw_ref[...] = _dot_tn(memn_ref[...], dkv)
        dmemn = _dot_nt(dkv, w_ref[...])
        mem_v = mem_ref[...]
        r = lax.rsqrt(jnp.mean(mem_v * mem_v, axis=-1, keepdims=True) + EPS)
        dgm_ref[...] = jnp.sum(dmemn * mem_v * r, axis=0, keepdims=True)

    return pl.pallas_call(
        body, name="mem_kv_bwd",
        out_shape=(jax.ShapeDtypeStruct((D_MODEL, 2 * D_MODEL), F32), jax.ShapeDtypeStruct((1, D_MODEL), F32),
                   jax.ShapeDtypeStruct((1, XHD), F32)),
        in_specs=[VMEM_SPEC] * 8, out_specs=(VMEM_SPEC,) * 3,
        scratch_shapes=[pltpu.VMEM((m_tok, 2 * D_MODEL), BF)],
        compiler_params=_cparams(),
    )(dkn, dv, kraw, mem, memn, g_mem, g_xk, w_xkv)


def _fox_bwd(q_aug, k_aug, proj, dmf, o32, lse, sums):
    t_len = q_aug.shape[0]
    tb = min(ATT_BLOCK, t_len)
    n_b = t_len // tb
    nsub = 2 if n_b >= 2 else 1
    tg = nsub * tb
    n_g = t_len // tg
    v_col = 6 * GROUP_W // LANES
    n_w = len(sums)
    n_steps = (N_HEADS // 2) * n_g

    def body(*refs):
        k_ref, v_ref, q_ref, do_ref, o_ref, lse_ref = refs[:6]
        dq_ref, dk_ref, dv_ref, df_ref = refs[6 + n_w:10 + n_w]
        delta = refs[10 + 2 * n_w]
        comm = (refs[6:6 + n_w], refs[10 + n_w:10 + 2 * n_w]) + tuple(refs[11 + 2 * n_w:])
        j = pl.program_id(1)
        step = pl.program_id(0) * n_g + j

        @pl.when(step == 0)
        def _():
            _scatter_phase(0, *comm)

        @pl.when(j == 0)
        def _():
            dq_ref[...] = jnp.zeros_like(dq_ref)
            dd = do_ref[...].astype(F32) * o_ref[...]
            hrow = lax.broadcasted_iota(jnp.int32, (8, LANES), 0)
            lane = lax.broadcasted_iota(jnp.int32, (8, LANES), 1)
            ind = ((lane // HEAD_DIM) == hrow).astype(BF)
            delta[...] = _dot_nt_exact(ind, dd)

        k2, v2 = k_ref[...], v_ref[...]
        chains = [(u, hh) for u in range(nsub) for hh in range(2)]
        ks = {(u, hh): k2[u * tb:(u + 1) * tb, hh * LANES:(hh + 1) * LANES] for u, hh in chains}
        vs = {(u, hh): v2[u * tb:(u + 1) * tb, hh * HEAD_DIM:(hh + 1) * HEAD_DIM] for u, hh in chains}

        def block(i, carry, which, masked):
            rows = pl.ds(pl.multiple_of(i * tb, tb), tb)
            q2 = q_ref[rows, :]
            do2 = do_ref[rows, :]
            qs = [q2[:, hh * LANES:(hh + 1) * LANES] for hh in range(2)]
            dos = [do2[:, hh * HEAD_DIM:(hh + 1) * HEAD_DIM] for hh in range(2)]
            ss = {ch: _dot_nt(ks[ch], qs[ch[1]]) for ch in which}
            dps = {ch: _dot_nt(vs[ch], dos[ch[1]]) for ch in which}
            pts, dsts, dfs = {}, {}, {}
            for ch in which:
                hh = ch[1]
                s_t = ss[ch]
                if ch in masked:
                    krow = lax.broadcasted_iota(jnp.int32, (tb, tb), 0)
                    qcol = lax.broadcasted_iota(jnp.int32, (tb, tb), 1)
                    s_t = jnp.where(qcol >= krow, s_t, NEG)
                p_t = jnp.exp2(s_t - lse_ref[0, hh:hh + 1, rows])
                pts[ch] = p_t.astype(BF)
                ds_t = p_t * (dps[ch] - delta[hh:hh + 1, rows])
                dsts[ch] = ds_t.astype(BF)
                dfs[ch] = jnp.sum(ds_t, axis=-1, keepdims=True)
            out = dict(carry)
            for ch in which:
                dk, dv, df = carry[ch]
                dv = dv + jnp.dot(pts[ch], dos[ch[1]], preferred_element_type=F32)
                dk = dk + jnp.dot(dsts[ch], qs[ch[1]], preferred_element_type=F32)
                out[ch] = (dk, dv, df - dfs[ch])
            for hh in range(2):
                parts_dq = [_dot_tn(dsts[ch], ks[ch])[:, :HEAD_DIM] for ch in which if ch[1] == hh]
                dq_ref[rows, hh * HEAD_DIM:(hh + 1) * HEAD_DIM] += sum(parts_dq[1:], parts_dq[0])
            return out

        init = {ch: (jnp.zeros((tb, LANES), F32), jnp.zeros((tb, HEAD_DIM), F32), jnp.zeros((tb, 1), F32)) for ch in chains}
        first = nsub * j
        carry = block(first, init, [(0, 0), (0, 1)], [(0, 0), (0, 1)])
        if nsub == 2:
            carry = block(first + 1, carry, chains, [(1, 0), (1, 1)])
        carry = lax.fori_loop(first + nsub, n_b, lambda i, c: block(i, c, chains, ()), carry)
        for u in range(nsub):
            rs = slice(u * tb, (u + 1) * tb)
            dk_ref[rs, :] = jnp.concatenate([carry[u, hh][0][:, :HEAD_DIM] for hh in range(2)], axis=-1) * LN2
            dv_ref[rs, :] = jnp.concatenate([carry[u, hh][1] for hh in range(2)], axis=-1)
            df_ref[0, rs, :] = jnp.concatenate([carry[u, hh][2] for hh in range(2)], axis=-1)

        @pl.when(step == n_steps - 1)
        def _():
            _scatter_phase(1, *comm)

    blk = lambda w, col0: pl.BlockSpec((tg, w), lambda hp, j: (j, col0 + hp))
    whole = lambda w: pl.BlockSpec((t_len, w), lambda hp, j: (0, hp))
    rows2 = pl.BlockSpec((1, 2, t_len), lambda hp, j: (hp, 0, 0))
    cols2 = pl.BlockSpec((1, tg, 2), lambda hp, j: (hp, j, 0))
    return pl.pallas_call(
        body, name="fox_bwd", grid=(N_HEADS // 2, n_g),
        out_shape=(jax.ShapeDtypeStruct((t_len, GROUP_W), F32), jax.ShapeDtypeStruct((t_len, GROUP_W), F32),
                   jax.ShapeDtypeStruct((t_len, GROUP_W), F32), jax.ShapeDtypeStruct((N_HEADS // 2, t_len, 2), F32))
        + _scatter_out_shapes(sums),
        in_specs=[blk(2 * LANES, 0), blk(LANES, v_col), whole(2 * LANES), whole(LANES), whole(LANES), rows2] + [ANY] * n_w,
        out_specs=(whole(LANES), blk(LANES, 0), blk(LANES, 0), cols2) + (ANY,) * n_w,
        scratch_shapes=[pltpu.VMEM((8, t_len), F32)] + _scatter_scratch(n_w),
        compiler_params=_cparams(("arbitrary", "arbitrary")),
    )(k_aug, proj, q_aug, dmf, o32, lse, *sums)


def _retention_bwd(dmr, raw, proj, g_ret, rq, rk, states, tables, parts):
    t_len = rq.shape[0]
    c = min(RET_BLOCK, t_len)
    n_b = t_len // c
    wdec, qdec, kdec, cdec = tables
    v_col, g_col = 2 * GROUP_W // LANES, 3 * GROUP_W // LANES
    n_w = len(parts)
    n_steps = (N_HEADS // 2) * n_b

    def body(*refs):
        d_ref, raw_ref, rg_ref, g_ref, q_ref, k_ref, v_ref, st_ref, w_ref, wt_ref, qd_ref, kd_ref, cd_ref = refs[:13]
        dq_ref, dk_ref, dv_ref, drg_ref, dg_ref = refs[13 + n_w:18 + n_w]
        gstate = refs[18 + 2 * n_w]
        comm = (refs[13:13 + n_w], refs[18 + n_w:18 + 2 * n_w]) + tuple(refs[19 + 2 * n_w:])
        step = pl.program_id(0) * n_b + pl.program_id(1)

        @pl.when(step == 0)
        def _():
            _exchange_phase(0, *comm)

        @pl.when(pl.program_id(1) == 0)
        def _():
            gstate[...] = jnp.zeros_like(gstate)
            dg_ref[...] = jnp.zeros_like(dg_ref)

        d, raw_v, g = d_ref[...], raw_ref[...], g_ref[0]
        gate = rg_ref[...].astype(F32)
        xc = raw_v - _group_mean64(raw_v)
        r = lax.rsqrt(_group_mean64(xc * xc) + EPS)
        xh = xc * r
        sg = _sigmoid(gate)
        drg_ref[...] = d * (xh * g) * (sg * (1.0 + gate * (1.0 - sg)))
        dy = d * (gate * sg)
        dg_ref[0] += jnp.sum(dy * xh, axis=0, keepdims=True)
        dxh = dy * g
        do2 = r * (dxh - _group_mean64(dxh) - xh * _group_mean64(dxh * xh))
        q2, k2, v2 = q_ref[...], k_ref[...], v_ref[...]
        dqs, dks, dvs = [], [], []
        heads = [tuple(t[:, hh * HEAD_DIM:(hh + 1) * HEAD_DIM] for t in (q2, k2, v2, do2.astype(BF))) for hh in range(2)]
        firsts = [(_dot_nt(k, q) * wt_ref[hh], _dot_nt(do, v) * w_ref[hh], _dot_nt(v, do) * wt_ref[hh])
                  for hh, (q, k, v, do) in enumerate(heads)]
        for hh, (q, k, v, do) in enumerate(heads):
            a_t, dm, dm_t = firsts[hh]
            sp, gs = st_ref[0, 0, hh], gstate[hh]
            qd = q.astype(F32) * qd_ref[hh]
            kd = k.astype(F32) * kd_ref[hh]
            dqs.append(_dot(dm, k) + _dot_nt(do, sp) * qd_ref[hh])
            dks.append(_dot(dm_t, q) + _dot_nt(v, gs) * kd_ref[hh])
            dvs.append(_dot(a_t, do) + _dot(kd, gs))
            gstate[hh] = gs * cd_ref[hh] + _dot_tn(qd, do)
        dq_ref[...] = jnp.concatenate(dqs, axis=-1)
        dk_ref[...] = jnp.concatenate(dks, axis=-1)
        dv_ref[...] = jnp.concatenate(dvs, axis=-1)

        @pl.when(step == n_steps - 1)
        def _():
            _exchange_phase(1, *comm)

    blk = lambda col0: pl.BlockSpec((c, LANES), lambda hp, i: (n_b - 1 - i, col0 + hp))
    tab = lambda a: pl.BlockSpec((2,) + a.shape[1:], lambda hp, i: (hp, 0, 0))
    gspec = pl.BlockSpec((1, 1, LANES), lambda hp, i: (hp, 0, 0))
    return pl.pallas_call(
        body, name="retention_bwd", grid=(N_HEADS // 2, n_b),
        out_shape=(jax.ShapeDtypeStruct((t_len, GROUP_W), F32),) * 4 + (jax.ShapeDtypeStruct((N_HEADS // 2, 1, LANES), F32),)
        + _exchange_out_shapes(parts),
        in_specs=[blk(0), blk(0), blk(g_col), gspec, blk(0), blk(0), blk(v_col),
                  pl.BlockSpec((1, 1, 2, HEAD_DIM, HEAD_DIM), lambda hp, i: (hp, n_b - 1 - i, 0, 0, 0)),
                  tab(wdec), tab(wdec), tab(qdec), tab(kdec), tab(cdec)] + [ANY] * n_w,
        out_specs=(blk(0), blk(0), blk(0), blk(0), gspec) + (ANY,) * n_w,
        scratch_shapes=[pltpu.VMEM((2, HEAD_DIM, HEAD_DIM), F32)] + _exchange_scratch(n_w),
        compiler_params=_cparams(("arbitrary", "arbitrary")),
    )(dmr, raw, proj, g_ret, rq, rk, proj, states, wdec, jnp.transpose(wdec, (0, 2, 1)), qdec, kdec, cdec, *parts)


def _in_proj_bwd(x, g_mix, dh1, dq_r, dk_r, dv_r, drg, dq_f, dk_f, dv_f, df_col, proj, z, cos_t, sin_t, gq_t, gk_t, w_in_t):
    t_len = x.shape[0]
    tm = min(ROW_TILE, t_len)
    n_t = t_len // tm

    def body(x_ref, g_ref, dh1_ref, dqr_ref, dkr_ref, dvr_ref, drg_ref, dqf_ref, dkf_ref, dvf_ref, df_ref, fq_ref, fk_ref, z_ref,
             cos_ref, sin_ref, gq_ref, gk_ref, wm_ref, wf_ref,
             dproj_ref, dz_ref, dx_ref, dg_ref, dgq_ref, dgk_ref, db_ref, carry, gq_acc, gk_acc):
        i = pl.program_id(0)

        @pl.when(i == 0)
        def _():
            carry[...] = jnp.zeros_like(carry)
            gq_acc[...] = jnp.zeros_like(gq_acc)
            gk_acc[...] = jnp.zeros_like(gk_acc)
            dg_ref[...] = jnp.zeros_like(dg_ref)
            db_ref[...] = jnp.zeros_like(db_ref)

        c, s = cos_ref[...], sin_ref[...]
        gq, gk = gq_ref[...], gk_ref[...]
        dgq = jnp.zeros((1, LANES), F32)
        dgk = jnp.zeros((1, LANES), F32)
        for sl in _chunks(GROUP_W):
            dy = dqr_ref[:, sl] * 0.125
            dproj_ref[:, sl] = (dy * c + _swap32(dy * s)).astype(BF)
            dy = dkr_ref[:, sl]
            dproj_ref[:, GROUP_W + sl.start:GROUP_W + sl.stop] = (dy * c + _swap32(dy * s)).astype(BF)
            dproj_ref[:, 2 * GROUP_W + sl.start:2 * GROUP_W + sl.stop] = dvr_ref[:, sl].astype(BF)
            dproj_ref[:, 3 * GROUP_W + sl.start:3 * GROUP_W + sl.stop] = drg_ref[:, sl].astype(BF)
            for src, dsrc, gain, off in ((fq_ref, dqf_ref, gq, 4), (fk_ref, dkf_ref, gk, 5)):
                xr = src[:, sl].astype(F32)
                r = lax.rsqrt(_group_mean64(xr * xr) + EPS)
                xh = xr * r
                dy = dsrc[:, sl] * (0.125 if off == 4 else 1.0)
                dgs = jnp.sum(dy * xh, axis=0, keepdims=True)
                if off == 4:
                    dgq = dgq + dgs
                else:
                    dgk = dgk + dgs
                dxh = dy * gain
                dproj_ref[:, off * GROUP_W + sl.start:off * GROUP_W + sl.stop] = \
                    (r * (dxh - xh * _group_mean64(dxh * xh))).astype(BF)
            dproj_ref[:, 6 * GROUP_W + sl.start:6 * GROUP_W + sl.stop] = dvf_ref[:, sl].astype(BF)
        gq_acc[...] += dgq
        gk_acc[...] += dgk
        row = lax.broadcasted_iota(jnp.int32, (tm, tm), 0)
        col = lax.broadcasted_iota(jnp.int32, (tm, tm), 1)
        dlf = _dot_exact((col >= row).astype(BF), df_ref[...]) + carry[0:1, :]
        carry[...] = jnp.broadcast_to(dlf[0:1, :], carry.shape)
        lane = lax.broadcasted_iota(jnp.int32, (tm, LANES), 1)
        dz = jnp.where(lane < N_HEADS, dlf / (1.0 + jnp.exp(z_ref[...])), 0.0)
        db_ref[...] += jnp.sum(dz, axis=0, keepdims=True)
        dz_bf = dz.astype(BF)
        dz_ref[...] = dz_bf
        dn1 = jnp.dot(dz_bf, wf_ref[...], preferred_element_type=F32)
        for sec in range(MAIN_W // GROUP_W):
            sl = slice(sec * GROUP_W, (sec + 1) * GROUP_W)
            dn1 = dn1 + jnp.dot(dproj_ref[:, sl], wm_ref[sl, :], preferred_element_type=F32)
        dx, dg = _rms_bwd(x_ref[...], g_ref[...], dn1)
        dx_ref[...] = dh1_ref[...] + dx
        dg_ref[...] += dg

        @pl.when(i == n_t - 1)
        def _():
            dgq_ref[...] = gq_acc[:, :HEAD_DIM] + gq_acc[:, HEAD_DIM:]
            dgk_ref[...] = gk_acc[:, :HEAD_DIM] + gk_acc[:, HEAD_DIM:]

    row_spec = lambda w, col=0: pl.BlockSpec((tm, w), lambda i: (n_t - 1 - i, col))
    full = lambda a: pl.BlockSpec(a.shape, lambda i: (0,) * a.ndim)
    acc = lambda r, c: pl.BlockSpec((r, c), lambda i: (0, 0))
    return pl.pallas_call(
        body, name="in_proj_bwd", grid=(n_t,),
        out_shape=(jax.ShapeDtypeStruct((t_len, MAIN_W), BF), jax.ShapeDtypeStruct((t_len, LANES), BF),
                   jax.ShapeDtypeStruct((t_len, D_MODEL), F32), jax.ShapeDtypeStruct((1, D_MODEL), F32),
                   jax.ShapeDtypeStruct((1, HEAD_DIM), F32), jax.ShapeDtypeStruct((1, HEAD_DIM), F32),
                   jax.ShapeDtypeStruct((1, LANES), F32)),
        in_specs=[row_spec(D_MODEL), full(g_mix), row_spec(D_MODEL)] + [row_spec(GROUP_W)] * 7
        + [row_spec(LANES), row_spec(GROUP_W, 4), row_spec(GROUP_W, 5), row_spec(LANES), row_spec(LANES), row_spec(LANES),
           full(gq_t), full(gk_t), *_w_in_specs()],
        out_specs=(row_spec(MAIN_W), row_spec(LANES), row_spec(D_MODEL), acc(1, D_MODEL), acc(1, HEAD_DIM), acc(1, HEAD_DIM),
                   acc(1, LANES)),
        scratch_shapes=[pltpu.VMEM((8, LANES), F32), pltpu.VMEM((1, LANES), F32), pltpu.VMEM((1, LANES), F32)],
        compiler_params=_cparams(("arbitrary",)),
    )(x, g_mix, dh1, dq_r, dk_r, dv_r, drg, dq_f, dk_f, dv_f, df_col, proj, proj, z, cos_t, sin_t, gq_t, gk_t, w_in_t, w_in_t)


def _matmul_tn(a, b, name, bk=1024):
    t_len, m = a.shape
    n = b.shape[1]
    bm = m if m <= TN_MAX_ROWS else m // 2
    bk = min(bk, t_len)

    def body(a_ref, b_ref, o_ref):
        @pl.when(pl.program_id(1) == 0)
        def _():
            o_ref[...] = jnp.zeros_like(o_ref)

        o_ref[...] += _dot_tn(a_ref[...], b_ref[...])

    return pl.pallas_call(
        body, name=name, grid=(m // bm, t_len // bk),
        out_shape=jax.ShapeDtypeStruct((m, n), F32),
        in_specs=[pl.BlockSpec((bk, bm), lambda i, k: (k, i)), pl.BlockSpec((bk, n), lambda i, k: (k, 0))],
        out_specs=pl.BlockSpec((bm, n), lambda i, k: (i, 0)),
        compiler_params=_cparams(("arbitrary", "arbitrary")),
    )(a, b)


def _place():
    x, y, c = lax.axis_index("x"), lax.axis_index("y"), lax.axis_index("c")
    chips = [(1 - x, y), (x, 1 - y), (1 - x, 1 - y)]
    return x, y, c, chips


def _row_chunks(rows, limit):
    step = max(d for d in range(16, min(rows, limit) + 1, 16) if rows % d == 0)
    return [slice(i, i + step) for i in range(0, rows, step)]


ICI_CHUNK_ROWS = 128
D2D_CHUNK_ROWS = 64


def _gather_phase(phase, ins, outs, send_sems, recv_sems):
    x, y, c, chips = _place()
    me_chip = 2 * x + y
    sibling = (x, y, 1 - c)

    def copy(w, k, slot, half, to, rows=slice(None), src=None):
        dst = outs[w].at[slot, half, rows]
        return pltpu.make_async_remote_copy(src_ref=dst if src is None else src, dst_ref=dst,
                                            send_sem=send_sems.at[w, k], recv_sem=recv_sems.at[w, k],
                                            device_id=to, device_id_type=MESH)

    for w in range(len(ins)):
        for j, (px, py) in enumerate(chips):
            if phase == 0:
                for rows in _row_chunks(ins[w].shape[1], ICI_CHUNK_ROWS):
                    copy(w, j, me_chip, c, (px, py, c), rows, src=ins[w].at[c, rows]).start()
            elif phase == 1:
                copy(w, j, 2 * px + py, c, (x, y, c)).wait_recv()
                for rows in _row_chunks(ins[w].shape[1], D2D_CHUNK_ROWS):
                    copy(w, 3 + j, 2 * px + py, c, sibling, rows).start()
            else:
                copy(w, 3 + j, 2 * px + py, 1 - c, (x, y, c)).wait_recv()
                copy(w, j, me_chip, c, (px, py, c), src=ins[w].at[c]).wait_send()
                copy(w, 3 + j, 2 * px + py, c, sibling).wait_send()


def _gather_scratch(n_w):
    return [pltpu.SemaphoreType.DMA((n_w, 6)), pltpu.SemaphoreType.DMA((n_w, 6))]


def _all_gather_weights(shards):
    n_w = len(shards)

    def body(*refs):
        for phase in range(3):
            _gather_phase(phase, refs[:n_w], refs[n_w:2 * n_w], *refs[2 * n_w:])

    return pl.pallas_call(
        body, name="all_gather_weights",
        out_shape=tuple(jax.ShapeDtypeStruct((4,) + s.shape, s.dtype) for s in shards),
        in_specs=[ANY] * n_w, out_specs=(ANY,) * n_w, scratch_shapes=_gather_scratch(n_w),
    )(*shards)


def _exchange_phase(phase, ins, theirs, send_sems, recv_sems):
    x, y, c, _ = _place()

    def remote(w, k=slice(None), rows=slice(None)):
        return pltpu.make_async_remote_copy(src_ref=ins[w].at[k, 1 - c, rows], dst_ref=theirs[w].at[k, rows],
                                            send_sem=send_sems.at[w], recv_sem=recv_sems.at[w], device_id=(x, y, 1 - c),
                                            device_id_type=MESH)

    for w in range(len(ins)):
        if phase == 0:
            for k in range(4):
                for rows in _row_chunks(ins[w].shape[2], D2D_CHUNK_ROWS):
                    remote(w, k, rows).start()
        else:
            remote(w).wait()


def _exchange_scratch(n_w):
    return [pltpu.SemaphoreType.DMA((n_w,)), pltpu.SemaphoreType.DMA((n_w,))]


def _exchange_out_shapes(grads):
    return tuple(jax.ShapeDtypeStruct((4,) + g.shape[2:], g.dtype) for g in grads)


def _exchange_core_halves(grads):
    n_w = len(grads)

    def body(*refs):
        for phase in range(2):
            _exchange_phase(phase, refs[:n_w], refs[n_w:2 * n_w], *refs[2 * n_w:])

    return pl.pallas_call(
        body, name="exchange_core_halves", out_shape=_exchange_out_shapes(grads),
        in_specs=[ANY] * n_w, out_specs=(ANY,) * n_w, scratch_shapes=_exchange_scratch(n_w),
    )(*grads)


def _add_pairs(part, theirs, name):
    _, _, r, c = part.shape
    rb = 32 if r % 32 == 0 else r

    def body(a_ref, b_ref, own_ref, ob_ref):
        my_chip = 2 * lax.axis_index("x") + lax.axis_index("y")
        ob_ref[...] = (a_ref[...] + b_ref[...]).astype(BF)
        own_ref[...] = a_ref[my_chip] + b_ref[my_chip]

    spec = pl.BlockSpec((4, rb, c), lambda i: (0, i, 0))
    return pl.pallas_call(
        body, name=name, grid=(r // rb,),
        out_shape=(jax.ShapeDtypeStruct((r, c), F32), jax.ShapeDtypeStruct((4, r, c), BF)),
        in_specs=[pl.BlockSpec((4, None, rb, c), lambda i: (0, lax.axis_index("c"), i, 0)), spec],
        out_specs=(pl.BlockSpec((rb, c), lambda i: (i, 0)), spec), compiler_params=_cparams(("arbitrary",)),
    )(part, theirs)


def _scatter_phase(phase, bfs, got, send_sems, recv_sems):
    x, y, c, chips = _place()

    def remote(w, j, px, py, rows=slice(None)):
        return pltpu.make_async_remote_copy(src_ref=bfs[w].at[2 * px + py, rows], dst_ref=got[w].at[j, rows],
                                            send_sem=send_sems.at[w, j], recv_sem=recv_sems.at[w, j], device_id=(px, py, c),
                                            device_id_type=MESH)

    for w in range(len(bfs)):
        for j, (px, py) in enumerate(chips):
            if phase == 0:
                for rows in _row_chunks(bfs[w].shape[1], ICI_CHUNK_ROWS):
                    remote(w, j, px, py, rows).start()
            else:
                remote(w, j, px, py).wait()


def _scatter_scratch(n_w):
    return [pltpu.SemaphoreType.DMA((n_w, 3)), pltpu.SemaphoreType.DMA((n_w, 3))]


def _scatter_out_shapes(sums_bf16):
    return tuple(jax.ShapeDtypeStruct((3,) + s.shape[1:], BF) for s in sums_bf16)


def _add_received(own, got, name):
    r, c = own.shape
    rb = 32 if r % 32 == 0 else r

    def body(o_ref, g_ref, out_ref):
        out_ref[...] = ((o_ref[...] + g_ref[0].astype(F32)) + g_ref[1].astype(F32)) + g_ref[2].astype(F32)

    return pl.pallas_call(
        body, name=name, grid=(r // rb,), out_shape=jax.ShapeDtypeStruct((r, c), F32),
        in_specs=[pl.BlockSpec((rb, c), lambda i: (i, 0)), pl.BlockSpec((3, rb, c), lambda i: (0, i, 0))],
        out_specs=pl.BlockSpec((rb, c), lambda i: (i, 0)), compiler_params=_cparams(("arbitrary",)),
    )(own, got)


def _share_with_sibling(halves):
    n_w = len(halves)

    def body(*refs):
        ins, outs = refs[:n_w], refs[n_w:2 * n_w]
        send_sems, recv_sems = refs[2 * n_w:]
        x, y, c, _ = _place()

        def remote(w, rows=slice(None)):
            return pltpu.make_async_remote_copy(src_ref=ins[w].at[rows], dst_ref=outs[w].at[c, rows], send_sem=send_sems.at[w],
                                                recv_sem=recv_sems.at[w], device_id=(x, y, 1 - c), device_id_type=MESH)

        for w in range(n_w):
            for rows in _row_chunks(ins[w].shape[0], D2D_CHUNK_ROWS):
                remote(w, rows).start()
        for w in range(n_w):
            remote(w).wait()

    return pl.pallas_call(
        body, name="share_with_sibling",
        out_shape=tuple(jax.ShapeDtypeStruct((2,) + h.shape, h.dtype) for h in halves),
        in_specs=[ANY] * n_w, out_specs=(ANY,) * n_w,
        scratch_shapes=[pltpu.SemaphoreType.DMA((n_w,)), pltpu.SemaphoreType.DMA((n_w,))],
    )(*halves)


def _small_phase(phase, p_ref, out_ref, slots, send_sems, recv_sems):
    x, y, cc, _ = _place()
    me = 4 * x + 2 * y + cc
    copies = []
    for k in range(1, 8):
        dx, dy, dc = (k >> 2) & 1, (k >> 1) & 1, k & 1
        to = (1 - x if dx else x, 1 - y if dy else y, 1 - cc if dc else cc)
        copies.append(pltpu.make_async_remote_copy(src_ref=p_ref, dst_ref=slots.at[me], send_sem=send_sems.at[k - 1],
                                                   recv_sem=recv_sems.at[k - 1], device_id=to, device_id_type=MESH))
    if phase == 0:
        slots[me] = p_ref[...]
        for cp in copies:
            cp.start()
    else:
        for cp in copies:
            cp.wait()
        total = slots[0]
        for d in range(1, 8):
            total = total + slots[d]
        out_ref[...] = total


def _adamw_update(w_ref, g_ref, m_ref, v_ref, d_ref, nm_ref, nv_ref):
    gv = g_ref[...]
    nm = ADAM_B1 * m_ref[...] + (1.0 - ADAM_B1) * gv
    nv = ADAM_B2 * v_ref[...] + (1.0 - ADAM_B2) * (gv * gv)
    nm_ref[...] = nm
    nv_ref[...] = nv
    m_hat = nm / (1.0 - ADAM_B1 ** ADAM_STEP)
    v_hat = nv / (1.0 - ADAM_B2 ** ADAM_STEP)
    d_ref[...] = -ADAM_LR * (m_hat / (jnp.sqrt(v_hat) + ADAM_EPS) + ADAM_WD * w_ref[...])


def _adamw_many(ws, gs, ms, vs, sums, pack):
    n_a, n_w = len(ws), len(sums)
    n_steps = ADAM_STEPS
    specs = [pl.BlockSpec((w.shape[0] // n_steps, w.shape[1]), lambda i: (i, 0)) for w in ws]
    pack_spec = pl.BlockSpec(pack.shape, lambda i: (0, 0))

    def body(*refs):
        ins = refs[:4 * n_a]
        p_ref = refs[4 * n_a + n_w]
        first_out = 4 * n_a + n_w + 1
        outs = refs[first_out:first_out + 3 * n_a]
        total_ref = refs[first_out + 3 * n_a + n_w]
        scratch = refs[first_out + 3 * n_a + n_w + 1:]
        scatter = (refs[4 * n_a:4 * n_a + n_w], refs[first_out + 3 * n_a:first_out + 3 * n_a + n_w]) + tuple(scratch[:2])
        small = (p_ref, total_ref) + tuple(scratch[2:])
        step = pl.program_id(0)

        @pl.when(step == 0)
        def _():
            _scatter_phase(0, *scatter)
            _small_phase(0, *small)

        for a in range(n_a):
            _adamw_update(*(ins[k * n_a + a] for k in range(4)), *(outs[3 * a + k] for k in range(3)))

        @pl.when(step == n_steps - 1)
        def _():
            _scatter_phase(1, *scatter)
            _small_phase(1, *small)

    flat = pl.pallas_call(
        body, name="adamw_late", grid=(n_steps,),
        out_shape=tuple(jax.ShapeDtypeStruct(w.shape, F32) for w in ws for _ in range(3)) + _scatter_out_shapes(sums)
        + (jax.ShapeDtypeStruct(pack.shape, F32),),
        in_specs=specs * 4 + [ANY] * n_w + [pack_spec],
        out_specs=tuple(s for s in specs for _ in range(3)) + (ANY,) * n_w + (pack_spec,),
        scratch_shapes=_scatter_scratch(n_w) + [pltpu.VMEM((8,) + pack.shape, F32), pltpu.SemaphoreType.DMA((7,)),
                                                pltpu.SemaphoreType.DMA((7,))],
        compiler_params=_cparams(("arbitrary",)),
    )(*ws, *gs, *ms, *vs, *sums, pack)
    return [tuple(flat[3 * a:3 * a + 3]) for a in range(n_a)] + list(flat[3 * n_a:])


def _adamw(w, g, m, v, name):
    r, c = w.shape
    rb, cb = (64, c) if r % 64 == 0 else (r, LANES if (r % 8 and c % LANES == 0) else c)

    def body(*refs):
        _adamw_update(*refs)

    spec = pl.BlockSpec((rb, cb), lambda i, j: (i, j))
    return pl.pallas_call(
        body, name=name, grid=(r // rb, c // cb), out_shape=(jax.ShapeDtypeStruct((r, c), F32),) * 3,
        in_specs=[spec] * 4, out_specs=(spec,) * 3, compiler_params=_cparams(("arbitrary", "arbitrary")),
    )(w, g, m, v)


def _rope_tables(t_len):
    inv_freq = ROPE_BASE ** (-jnp.arange(0, HEAD_DIM, 2, dtype=F32) / HEAD_DIM)
    ang = jnp.arange(t_len, dtype=F32)[:, None] * inv_freq[None, :]
    cos, sin = jnp.cos(ang), jnp.sin(ang)
    cos_t = jnp.concatenate([cos, cos, cos, cos], axis=-1)
    sin_t = jnp.concatenate([-sin, sin, -sin, sin], axis=-1)
    return cos_t, sin_t


def _cols_to_shards(dw):
    r, n = dw.shape
    return jnp.transpose(dw.reshape(2, r // 2, 4, n // 4), (2, 0, 1, 3))


def _rows_to_shards(dw):
    r, n = dw.shape
    padded = _pad_rows(dw.reshape(4, r // 4, n))
    return padded.reshape(4, 2, padded.shape[1] // 2, n)


def _pad_lanes(a):
    extra = -a.shape[-1] % LANES
    return a if extra == 0 else jnp.pad(a, [(0, 0)] * (a.ndim - 1) + [(0, extra)])


def _pad_rows(a):
    rows = a.shape[-2]
    extra = 0 if rows % SHARD_ROW_ALIGN == 0 else -rows % SHARD_ROW_PAD
    return a if extra == 0 else jnp.pad(a, [(0, 0)] * (a.ndim - 2) + [(0, extra), (0, 0)])


def _pad_row(a, width=D_MODEL):
    a = a.reshape(1, -1)
    return jnp.pad(a, ((0, 0), (0, width - a.shape[1])))


def kernel(x, mem, g_mix, w_in, b_forget, g_ret_out, g_fox_q, g_fox_k, w_out, g_xattn, w_xq, w_xkv, g_mem, g_xq, g_xk, w_xo, g_ffn, w_gate, w_up, w_down, loss_target, m_g_mix, m_w_in, m_b_forget, m_g_ret_out, m_g_fox_q, m_g_fox_k, m_w_out, m_g_xattn, m_w_xq, m_w_xkv, m_g_mem, m_g_xq, m_g_xk, m_w_xo, m_g_ffn, m_w_gate, m_w_up, m_w_down, v_g_mix, v_w_in, v_b_forget, v_g_ret_out, v_g_fox_q, v_g_fox_k, v_w_out, v_g_xattn, v_w_xq, v_w_xkv, v_g_mem, v_g_xq, v_g_xk, v_w_xo, v_g_ffn, v_w_gate, v_w_up, v_w_down):
    big = {"w_in": (w_in, m_w_in, v_w_in), "w_out": (w_out, m_w_out, v_w_out), "w_xq": (w_xq, m_w_xq, v_w_xq),
           "w_xkv": (w_xkv, m_w_xkv, v_w_xkv), "w_xo": (w_xo, m_w_xo, v_w_xo), "w_gate": (w_gate, m_w_gate, v_w_gate),
           "w_up": (w_up, m_w_up, v_w_up), "w_down": (w_down, m_w_down, v_w_down)}
    for n in TRANSPOSED:
        big[n] = tuple(jnp.swapaxes(a, 1, 2) for a in big[n])
    shards = {}
    for n in big:
        w = _pad_rows(_pad_lanes(big[n][0][0].astype(BF)))
        shards[n] = w.reshape(2, w.shape[0] // 2, w.shape[1])
    sizes = {n: big[n][0].shape[1:] for n in big}
    w_in_full = _assemble_weight("w_in", _all_gather_weights([shards["w_in"]])[0], shards["w_in"], sizes["w_in"])
    small_w ={"g_mix": g_mix, "b_forget": b_forget, "g_ret_out": g_ret_out, "g_fox_q": g_fox_q, "g_fox_k": g_fox_k,
               "g_xattn": g_xattn, "g_mem": g_mem, "g_xq": g_xq, "g_xk": g_xk, "g_ffn": g_ffn}
    m_small = {"g_mix": m_g_mix, "b_forget": m_b_forget, "g_ret_out": m_g_ret_out, "g_fox_q": m_g_fox_q, "g_fox_k": m_g_fox_k,
               "g_xattn": m_g_xattn, "g_mem": m_g_mem, "g_xq": m_g_xq, "g_xk": m_g_xk, "g_ffn": m_g_ffn}
    v_small = {"g_mix": v_g_mix, "b_forget": v_b_forget, "g_ret_out": v_g_ret_out, "g_fox_q": v_g_fox_q, "g_fox_k": v_g_fox_k,
               "g_xattn": v_g_xattn, "g_mem": v_g_mem, "g_xq": v_g_xq, "g_xk": v_g_xk, "g_ffn": v_g_ffn}
    loss_part, grad_x, sums, got, in_bf, small_g = _local_step(x[0], mem[0], loss_target[0], w_in_full, shards, sizes, small_w)
    return _reduce_and_update(big, sums, got, in_bf, small_w, small_g, loss_part, grad_x, m_small, v_small)


def _assemble_weight(name, gathered, own, size):
    rows, width = size
    my_chip = 2 * lax.axis_index("x") + lax.axis_index("y")
    g = lax.dynamic_update_slice(gathered, own[None], (my_chip, 0, 0, 0))
    g = g.reshape(4, 2 * g.shape[2], g.shape[3])[:, :rows, :width]
    return jnp.transpose(g, (1, 0, 2)).reshape(rows, 4 * width) if name in COL_SHARDED else g.reshape(4 * rows, width)


def _shard_parts(names, dw):
    return [_pad_lanes(_cols_to_shards(dw[n]) if n in COL_SHARDED else _rows_to_shards(dw[n])) for n in names]


def _core_sums(names, parts, theirs):
    return [_add_pairs(p, t, f"core_sum_{n}") for n, p, t in zip(names, parts, theirs)]


def _local_step(xs, mems, tgt, w_in_full, shards, sizes, small_w):
    g_mix, b_forget, g_ret_out, g_fox_q, g_fox_k = (small_w[n] for n in ("g_mix", "b_forget", "g_ret_out", "g_fox_q", "g_fox_k"))
    g_xattn, g_mem, g_xq, g_xk, g_ffn = (small_w[n] for n in ("g_xattn", "g_mem", "g_xq", "g_xk", "g_ffn"))
    w_in_t = jnp.pad(w_in_full, ((0, MAIN_W + LANES - IN_W), (0, 0)))
    t_len = xs.shape[0]
    cos_t, sin_t = _rope_tables(t_len)
    tables = _decay_tables(min(RET_BLOCK, t_len))
    gq_t = jnp.concatenate([g_fox_q, g_fox_q], axis=-1)
    gk_t = jnp.concatenate([g_fox_k, g_fox_k], axis=-1)
    b_pad = _pad_row(b_forget, LANES)
    g_ret = g_ret_out.reshape(N_HEADS // 2, 1, LANES)

    n1, proj, rq, rk, q_aug, k_aug, z = _in_proj_fwd(xs, g_mix, w_in_t, b_pad, cos_t, sin_t, gq_t, gk_t)
    raw, mix_r, states = _retention_fwd(rq, rk, proj, g_ret, tables)
    mix_f, o32, lse, *gathered = _fox_fwd(q_aug, k_aug, proj, [shards[n] for n in LATE])
    full = {n: _assemble_weight(n, g, shards[n], sizes[n]) for n, g in zip(LATE, gathered)}
    memn, kraw, kn, vmem = _mem_kv_fwd(mems, g_mem, full["w_xkv"], g_xk)
    h1, hn2, qx, o_x, h2 = _attn_out_xattn_fwd(xs, mix_r, mix_f, full["w_out"], g_xattn, full["w_xq"], g_xq, kn, vmem, full["w_xo"])
    hn3, gate, up, act, dh3, loss_part = _ffn_loss_fwd(h2, g_ffn, full["w_gate"], full["w_up"], full["w_down"], tgt)

    dgate, dup, dh2, dg_ffn = _ffn_bwd(dh3, gate, up, h2, g_ffn, full["w_gate"], full["w_up"], full["w_down"])
    dqx, dh1, dmr, dmf, dkn, dvm, dg_xattn, dg_xq = _attn_out_xattn_bwd(dh2, h1, qx, kn, vmem, full["w_xo"], full["w_xq"],
                                                                      full["w_out"], g_xattn, g_xq)
    dw_xkv, dg_mem, dg_xk = _mem_kv_bwd(dkn, dvm, kraw, mems, memn, g_mem, g_xk, full["w_xkv"])
    dw = {
        "w_out": jnp.concatenate([_matmul_tn(mix_r, dh1, "dw_out_ret"), _matmul_tn(mix_f, dh1, "dw_out_fox")], axis=0),
        "w_xq": _matmul_tn(hn2, dqx, "dw_xq"),
        "w_xkv": dw_xkv,
        "w_xo": _matmul_tn(o_x, dh2, "dw_xo"),
        "w_gate": _matmul_tn(dgate, hn3, "dw_gate"),
        "w_up": _matmul_tn(dup, hn3, "dw_up"),
        "w_down": _matmul_tn(act, dh3, "dw_down"),
    }
    late_parts = _shard_parts(LATE, dw)
    dq_r, dk_r, dv_r, drg, dg_ret, *late_theirs = _retention_bwd(dmr, raw, proj, g_ret, rq, rk, states, tables, late_parts)
    late_sums = _core_sums(LATE, late_parts, late_theirs)
    dq_f, dk_f, dv_f, df, *late_got = _fox_bwd(q_aug, k_aug, proj, dmf, o32, lse, [s[1] for s in late_sums])
    df_col = jnp.pad(jnp.transpose(df, (1, 0, 2)).reshape(t_len, N_HEADS), ((0, 0), (0, LANES - N_HEADS)))
    dproj, dz, grad_x, dg_mix, dg_fq, dg_fk, db = _in_proj_bwd(xs, g_mix, dh1, dq_r, dk_r, dv_r, drg, dq_f, dk_f, dv_f, df_col,
                                                              proj, z, cos_t, sin_t, gq_t, gk_t, w_in_t)

    dw_in = jnp.concatenate([_matmul_tn(dproj, n1, "dw_in_main"), _matmul_tn(dz, n1, "dw_in_ff")[:IN_W - MAIN_W]], axis=0)
    in_parts = _shard_parts(("w_in",), {"w_in": dw_in})
    in_sums = _core_sums(("w_in",), in_parts, _exchange_core_halves(in_parts))
    sums = {n: s[0] for n, s in zip(("w_in",) + LATE, in_sums + late_sums)}
    got = dict(zip(LATE, late_got))
    in_bf = in_sums[0][1]
    small_g = {"g_mix": dg_mix, "b_forget": db[:, :N_HEADS], "g_ret_out": dg_ret, "g_fox_q": dg_fq, "g_fox_k": dg_fk,
               "g_xattn": dg_xattn, "g_mem": dg_mem, "g_xq": dg_xq, "g_xk": dg_xk, "g_ffn": dg_ffn}
    return loss_part, grad_x, sums, got, in_bf, small_g


def _final_grads(names, big, sums, got):
    my_core = lax.axis_index("c")
    finals = [_add_received(sums[n], got[n], f"chip_sum_{n}") for n in names]
    shared = _share_with_sibling(finals)
    out = {}
    for n, s, fin in zip(names, shared, finals):
        s = lax.dynamic_update_slice(s, fin[None], (my_core, 0, 0))
        out[n] = s.reshape(2 * s.shape[1], s.shape[2])[:big[n][0].shape[1], :big[n][0].shape[2]]
    return out


def _reduce_and_update(big, sums, got, in_bf, small_w, small_g, loss_part, grad_x, m_small, v_small):
    small_names = list(small_w)
    pad_rows = SMALL_ROWS - len(small_names) - 1
    stack = lambda d: jnp.concatenate([_pad_row(d[n]) for n in small_names] + [jnp.zeros((pad_rows + 1, D_MODEL), F32)], axis=0)
    g_pack = jnp.concatenate([_pad_row(small_g[n]) for n in small_names] + [_pad_row(loss_part[0:1, 0:1])]
                             + [jnp.zeros((pad_rows, D_MODEL), F32)], axis=0)
    grads = _final_grads(LATE, big, sums, got)
    *late_updates, in_got, g_tot = _adamw_many([big[n][0][0] for n in LATE], [grads[n] for n in LATE], [big[n][1][0] for n in LATE],
                                               [big[n][2][0] for n in LATE], [in_bf], g_pack)
    updates = dict(zip(LATE, late_updates))
    grads.update(_final_grads(("w_in",), big, sums, {"w_in": in_got}))
    updates["w_in"] = _adamw(big["w_in"][0][0], grads["w_in"], big["w_in"][1][0], big["w_in"][2][0], "adamw_w_in")
    deltas, new_m, new_v = {}, {}, {}
    for n in big:
        restore = (lambda a: jnp.swapaxes(a[None], 1, 2)) if n in TRANSPOSED else (lambda a: a[None])
        grads[n] = restore(grads[n])
        deltas[n], new_m[n], new_v[n] = (restore(a) for a in updates[n])

    d_s, m_s, v_s = _adamw(stack(small_w), g_tot, stack(m_small), stack(v_small), "adamw_small")
    for i, n in enumerate(small_names):
        shape = small_w[n].shape
        size = int(np.prod(shape))
        grads[n] = g_tot[i, :size].reshape(shape)
        deltas[n], new_m[n], new_v[n] = d_s[i, :size].reshape(shape), m_s[i, :size].reshape(shape), v_s[i, :size].reshape(shape)
    loss = g_tot[len(small_names), 0]

    order = ["g_mix", "w_in", "b_forget", "g_ret_out", "g_fox_q", "g_fox_k", "w_out", "g_xattn", "w_xq", "w_xkv", "g_mem", "g_xq",
             "g_xk", "w_xo", "g_ffn", "w_gate", "w_up", "w_down"]
    return (loss, grad_x[None], *[grads[n] for n in order], *[deltas[n] for n in order], *[new_m[n] for n in order],
            *[new_v[n] for n in order])
```

```python
import functools

import numpy as np
import jax
import jax.numpy as jnp
from jax import lax
from jax.experimental import pallas as pl
from jax.experimental.pallas import tpu as pltpu

F32 = jnp.float32
BF = jnp.bfloat16

D_MODEL = 1024
HEAD_DIM = 64
N_HEADS = 8
GROUP_W = 512
N_XH = 4
XHD = 256
D_FF = 2816
MAIN_W = 3584
IN_W = 3592
ROPE_BASE = 10000.0
LOG2E = 1.4426950408889634
LN2 = 0.6931471805599453
EPS = 1e-6
NEG = -1e30
LANES = 128
RET_BLOCK = 256
REF_CHUNK = 64
ROW_TILE = 512
FFN_BWD_TILE = 256
ATT_BLOCK = 256
FWD_GROUP = 4
TN_MAX_ROWS = 1408
SMALL_ROWS = 16
COL_SHARDED = ("w_xkv",)
TRANSPOSED = ("w_in", "w_gate", "w_up")
SHARD_ROW_ALIGN = 32
SHARD_ROW_PAD = 256
LATE = ("w_out", "w_xq", "w_xkv", "w_xo", "w_gate", "w_up", "w_down")
VMEM_LIMIT = 56 * 1024 * 1024

ADAM_LR = 0.001
ADAM_B1 = 0.9
ADAM_B2 = 0.999
ADAM_EPS = 1e-08
ADAM_WD = 0.01
ADAM_STEP = 10
CHIP_SUM_STEPS = 2
ADAM_STEPS = 8

MESH = pl.DeviceIdType.MESH
ANY = pl.BlockSpec(memory_space=pl.ANY)
VMEM_SPEC = pl.BlockSpec(memory_space=pltpu.VMEM)


def _cparams(sem=None, vmem=VMEM_LIMIT):
    return pltpu.CompilerParams(dimension_semantics=sem, vmem_limit_bytes=vmem)


def _dot(a, b):
    return jnp.dot(a.astype(BF), b.astype(BF), preferred_element_type=F32)


def _dot_nt(a, b):
    return lax.dot_general(a.astype(BF), b.astype(BF), (((1,), (1,)), ((), ())), preferred_element_type=F32)


def _dot_tn(a, b):
    return lax.dot_general(a.astype(BF), b.astype(BF), (((0,), (0,)), ((), ())), preferred_element_type=F32)


def _split3(x):
    hi = x.astype(BF)
    r = x - hi.astype(F32)
    mid = r.astype(BF)
    lo = (r - mid.astype(F32)).astype(BF)
    return hi, mid, lo


def _dot_exact(ind, x):
    hi, mid, lo = _split3(x)
    return (jnp.dot(ind, lo, preferred_element_type=F32) + jnp.dot(ind, mid, preferred_element_type=F32)
            + jnp.dot(ind, hi, preferred_element_type=F32))


def _dot_nt_exact(ind, x):
    hi, mid, lo = _split3(x)
    dn = (((1,), (1,)), ((), ()))
    return (lax.dot_general(ind, lo, dn, preferred_element_type=F32) + lax.dot_general(ind, mid, dn, preferred_element_type=F32)
            + lax.dot_general(ind, hi, dn, preferred_element_type=F32))


def _sigmoid(x):
    return 1.0 / (1.0 + jnp.exp(-x))


def _rms_fwd(x, g):
    r = lax.rsqrt(jnp.mean(x * x, axis=-1, keepdims=True) + EPS)
    return x * r * g


def _rms_bwd(x, g, dy):
    r = lax.rsqrt(jnp.mean(x * x, axis=-1, keepdims=True) + EPS)
    xh = x * r
    dg = jnp.sum(dy * xh, axis=0, keepdims=True)
    dxh = dy * g
    dx = r * (dxh - xh * jnp.mean(dxh * xh, axis=-1, keepdims=True))
    return dx, dg


def _group_mean64(x):
    lane = lax.broadcasted_iota(jnp.int32, x.shape, 1)
    lo = lane < HEAD_DIM
    s_lo = jnp.sum(jnp.where(lo, x, 0.0), axis=-1, keepdims=True)
    s_hi = jnp.sum(jnp.where(lo, 0.0, x), axis=-1, keepdims=True)
    return jnp.where(lo, s_lo, s_hi) * (1.0 / HEAD_DIM)


def _swap32(x):
    lane = lax.broadcasted_iota(jnp.int32, x.shape, 1)
    first = (lane % HEAD_DIM) < (HEAD_DIM // 2)
    return jnp.where(first, pltpu.roll(x, LANES - HEAD_DIM // 2, axis=1), pltpu.roll(x, HEAD_DIM // 2, axis=1))


def _chunks(w):
    return [slice(j * LANES, (j + 1) * LANES) for j in range(w // LANES)]


def _aug_pair(qk, f_cols, is_query):
    lane = lax.broadcasted_iota(jnp.int32, qk.shape, 1)
    a = lane - HEAD_DIM
    values = (qk, pltpu.roll(qk, HEAD_DIM, axis=1))
    out = []
    for hh in range(2):
        hi, mid, lo = (p.astype(F32) for p in _split3(f_cols[hh] * LOG2E))
        if is_query:
            aux = jnp.where(a == 0, hi, jnp.where(a == 1, mid, jnp.where(a == 2, lo, jnp.where(a < 6, 1.0, 0.0))))
        else:
            aux = jnp.where(a < 3, 1.0, jnp.where(a == 3, -hi, jnp.where(a == 4, -mid, jnp.where(a == 5, -lo, 0.0))))
        out.append(jnp.where(a < 0, values[hh], aux))
    return jnp.concatenate(out, axis=-1).astype(BF)


def _mem_kv_fwd(mem, g_mem, w_xkv, g_xk):
    m_tok = mem.shape[0]

    def body(mem_ref, gm_ref, w_ref, gk_ref, memn_ref, kraw_ref, kn_ref, v_ref):
        mn = _rms_fwd(mem_ref[...], gm_ref[...]).astype(BF)
        memn_ref[...] = mn
        kv = jnp.dot(mn, w_ref[...], preferred_element_type=F32)
        k = kv[:, :D_MODEL]
        kraw_ref[...] = k
        v_ref[...] = kv[:, D_MODEL:].astype(BF)
        for h in range(N_XH):
            sl = slice(h * XHD, (h + 1) * XHD)
            kn_ref[:, sl] = _rms_fwd(k[:, sl], gk_ref[...]).astype(BF)

    return pl.pallas_call(
        body, name="mem_kv_fwd",
        out_shape=(jax.ShapeDtypeStruct((m_tok, D_MODEL), BF), jax.ShapeDtypeStruct((m_tok, D_MODEL), F32),
                   jax.ShapeDtypeStruct((m_tok, D_MODEL), BF), jax.ShapeDtypeStruct((m_tok, D_MODEL), BF)),
        in_specs=[VMEM_SPEC] * 4, out_specs=(VMEM_SPEC,) * 4, compiler_params=_cparams(),
    )(mem, g_mem, w_xkv, g_xk)


def _in_proj_fwd(x, g_mix, w_in_t, b_pad, cos_t, sin_t, gq_t, gk_t):
    t_len = x.shape[0]
    tm = min(ROW_TILE, t_len)
    n_t = t_len // tm

    def body(x_ref, g_ref, wm_ref, wf_ref, b_ref, cos_ref, sin_ref, gq_ref, gk_ref,
             n1_ref, proj_ref, rq_ref, rk_ref, qa_ref, ka_ref, z_ref, carry):
        i = pl.program_id(0)

        @pl.when(i == 0)
        def _():
            carry[...] = jnp.zeros_like(carry)

        n1 = _rms_fwd(x_ref[...], g_ref[...]).astype(BF)
        n1_ref[...] = n1
        z = _dot_nt(n1, wf_ref[...]) + b_ref[...]
        z_ref[...] = z
        lane = lax.broadcasted_iota(jnp.int32, z.shape, 1)
        lf = jnp.where(lane < N_HEADS, jnp.minimum(z, 0.0) - jnp.log(1.0 + jnp.exp(-jnp.abs(z))), 0.0)
        row = lax.broadcasted_iota(jnp.int32, (tm, tm), 0)
        col = lax.broadcasted_iota(jnp.int32, (tm, tm), 1)
        tri = (row >= col).astype(BF)
        fc = _dot_exact(tri, lf) + carry[0:1, :]
        carry[...] = jnp.broadcast_to(fc[tm - 1:tm, :], carry.shape)
        c, s = cos_ref[...], sin_ref[...]

        def section(n):
            p = _dot_nt(n1, wm_ref[n * GROUP_W:(n + 1) * GROUP_W, :])
            proj_ref[:, n * GROUP_W:(n + 1) * GROUP_W] = p.astype(BF)
            return p

        def rotate(p, out_ref, scale):
            for sl in _chunks(GROUP_W):
                out_ref[:, sl] = ((p[:, sl] * c + _swap32(p[:, sl]) * s) * scale).astype(BF)

        def norm_aug(p, gain, out_ref, scale, is_query):
            for j, sl in enumerate(_chunks(GROUP_W)):
                f = p[:, sl]
                f = f * lax.rsqrt(_group_mean64(f * f) + EPS) * gain * scale
                out_ref[:, 2 * j * LANES:2 * (j + 1) * LANES] = _aug_pair(f, [fc[:, 2 * j:2 * j + 1], fc[:, 2 * j + 1:2 * j + 2]], is_query)

        p_rq, p_rk = section(0), section(1)
        rotate(p_rq, rq_ref, 0.125)
        section(2)
        rotate(p_rk, rk_ref, 1.0)
        section(3)
        p_fq = section(4)
        p_fk = section(5)
        norm_aug(p_fq, gq_ref[...], qa_ref, 0.125 * LOG2E, True)
        section(6)
        norm_aug(p_fk, gk_ref[...], ka_ref, 1.0, False)

    row_spec = lambda w: pl.BlockSpec((tm, w), lambda i: (i, 0))
    full = lambda a: pl.BlockSpec(a.shape, lambda i: (0,) * a.ndim)
    return pl.pallas_call(
        body, name="in_proj_fwd", grid=(n_t,),
        out_shape=(jax.ShapeDtypeStruct((t_len, D_MODEL), BF), jax.ShapeDtypeStruct((t_len, MAIN_W), BF),
                   jax.ShapeDtypeStruct((t_len, GROUP_W), BF), jax.ShapeDtypeStruct((t_len, GROUP_W), BF),
                   jax.ShapeDtypeStruct((t_len, 2 * GROUP_W), BF), jax.ShapeDtypeStruct((t_len, 2 * GROUP_W), BF),
                   jax.ShapeDtypeStruct((t_len, LANES), F32)),
        in_specs=[row_spec(D_MODEL), full(g_mix), *_w_in_specs(), full(b_pad), row_spec(LANES), row_spec(LANES),
                  full(gq_t), full(gk_t)],
        out_specs=(row_spec(D_MODEL), row_spec(MAIN_W), row_spec(GROUP_W), row_spec(GROUP_W), row_spec(2 * GROUP_W),
                   row_spec(2 * GROUP_W), row_spec(LANES)),
        scratch_shapes=[pltpu.VMEM((8, LANES), F32)],
        compiler_params=_cparams(("arbitrary",)),
    )(x, g_mix, w_in_t, w_in_t, b_pad, cos_t, sin_t, gq_t, gk_t)


def _w_in_specs():
    return (pl.BlockSpec((MAIN_W, D_MODEL), lambda i: (0, 0)), pl.BlockSpec((LANES, D_MODEL), lambda i: (MAIN_W // LANES, 0)))


def _decay_tables(c):
    h = np.arange(N_HEADS, dtype=np.float64)
    lg = np.log(1.0 - 2.0 ** (-5.0 - h)).astype(np.float32).astype(np.float64)
    t = np.arange(c)
    same_or_earlier = (t[None, :] // REF_CHUNK) <= (t[:, None] // REF_CHUNK)
    w = np.where(same_or_earlier[None], np.exp(lg[:, None, None] * np.abs(t[:, None] - t[None, :])[None]), 0.0)
    qd = np.exp(lg[:, None] * (t[None, :] + 1.0))
    kd = np.exp(lg[:, None] * (c - 1.0 - t[None, :]))
    cd = np.exp(lg * c)
    ones = np.ones((1, 1, HEAD_DIM))
    return (jnp.asarray(w, F32), jnp.asarray(qd[:, :, None] * ones, F32), jnp.asarray(kd[:, :, None] * ones, F32),
            jnp.asarray(cd[:, None, None] * np.ones((1, HEAD_DIM, HEAD_DIM)), F32))


def _retention_fwd(rq, rk, proj, g_ret, tables):
    t_len = rq.shape[0]
    c = min(RET_BLOCK, t_len)
    n_b = t_len // c
    wdec, qdec, kdec, cdec = tables
    v_col, g_col = 2 * GROUP_W // LANES, 3 * GROUP_W // LANES

    def body(q_ref, k_ref, v_ref, rg_ref, g_ref, w_ref, qd_ref, kd_ref, cd_ref, raw_ref, mix_ref, st_ref, state):
        i = pl.program_id(1)

        @pl.when(i == 0)
        def _():
            state[...] = jnp.zeros_like(state)

        q2, k2, v2 = q_ref[...], k_ref[...], v_ref[...]
        heads = [tuple(t[:, hh * HEAD_DIM:(hh + 1) * HEAD_DIM] for t in (q2, k2, v2)) for hh in range(2)]
        scores = [(_dot_nt(q, k) * w_ref[hh]).astype(BF) for hh, (q, k, _) in enumerate(heads)]
        outs = []
        for hh, (q, k, v) in enumerate(heads):
            sp = state[hh]
            st_ref[0, 0, hh] = sp
            outs.append(jnp.dot(scores[hh], v, preferred_element_type=F32) + _dot(q.astype(F32) * qd_ref[hh], sp))
            state[hh] = sp * cd_ref[hh] + _dot_tn(k.astype(F32) * kd_ref[hh], v)
        o2 = jnp.concatenate(outs, axis=-1)
        raw_ref[...] = o2
        xc = o2 - _group_mean64(o2)
        xh = xc * lax.rsqrt(_group_mean64(xc * xc) + EPS)
        gate = rg_ref[...].astype(F32)
        mix_ref[...] = (gate * _sigmoid(gate) * (xh * g_ref[0])).astype(BF)

    blk = lambda col0: pl.BlockSpec((c, LANES), lambda hp, i: (i, col0 + hp))
    tab = lambda a: pl.BlockSpec((2,) + a.shape[1:], lambda hp, i: (hp, 0, 0))
    return pl.pallas_call(
        body, name="retention_fwd", grid=(N_HEADS // 2, n_b),
        out_shape=(jax.ShapeDtypeStruct((t_len, GROUP_W), F32), jax.ShapeDtypeStruct((t_len, GROUP_W), BF),
                   jax.ShapeDtypeStruct((N_HEADS // 2, n_b, 2, HEAD_DIM, HEAD_DIM), F32)),
        in_specs=[blk(0), blk(0), blk(v_col), blk(g_col), pl.BlockSpec((1, 1, LANES), lambda hp, i: (hp, 0, 0)),
                  tab(wdec), tab(qdec), tab(kdec), tab(cdec)],
        out_specs=(blk(0), blk(0), pl.BlockSpec((1, 1, 2, HEAD_DIM, HEAD_DIM), lambda hp, i: (hp, i, 0, 0, 0))),
        scratch_shapes=[pltpu.VMEM((2, HEAD_DIM, HEAD_DIM), F32)],
        compiler_params=_cparams(("arbitrary", "arbitrary")),
    )(rq, rk, proj, proj, g_ret, wdec, qdec, kdec, cdec)


def _fox_fwd(q_aug, k_aug, proj, shards):
    t_len = q_aug.shape[0]
    tq = min(ATT_BLOCK, t_len)
    nsub = min(FWD_GROUP, t_len // tq)
    tg = nsub * tq
    n_q = t_len // tg
    v_col = 6 * GROUP_W // LANES
    tc = min(512, t_len)
    n_w = len(shards)
    n_steps = (N_HEADS // 2) * n_q

    def body(*refs):
        q_ref, k_ref, v_ref = refs[:3]
        o_ref, o32_ref, lse_ref = refs[3 + n_w:6 + n_w]
        vt = refs[6 + 2 * n_w]
        comm = (refs[3:3 + n_w], refs[6 + n_w:6 + 2 * n_w]) + tuple(refs[7 + 2 * n_w:])
        i = pl.program_id(1)
        step = pl.program_id(0) * n_q + i

        @pl.when(step == 0)
        def _():
            _gather_phase(0, *comm)

        @pl.when(step == (3 * n_steps) // 4)
        def _():
            _gather_phase(1, *comm)

        @pl.when(i == 0)
        def _():
            for c0 in range(0, t_len, tc):
                vt[:, c0:c0 + tc] = v_ref[c0:c0 + tc, :].T

        chains = [(u, hh) for u in range(nsub) for hh in range(2)]
        qs = {(u, hh): q_ref[u * tq:(u + 1) * tq, hh * LANES:(hh + 1) * LANES] for u, hh in chains}
        ones = jnp.ones((HEAD_DIM, tq), BF)

        def scores(j, which):
            k2 = k_ref[pl.ds(pl.multiple_of(j * tq, tq), tq), :]
            return {ch: _dot_nt(k2[:, ch[1] * LANES:(ch[1] + 1) * LANES], qs[ch]) for ch in which}

        def update(j, ss, carry, masked):
            v2 = vt[:, pl.ds(pl.multiple_of(j * tq, tq), tq)]
            ps, stats = {}, {}
            for ch in ss:
                m = carry[ch][0]
                s_t = ss[ch]
                if ch in masked:
                    krow = lax.broadcasted_iota(jnp.int32, (tq, tq), 0)
                    qcol = lax.broadcasted_iota(jnp.int32, (tq, tq), 1)
                    s_t = jnp.where(qcol >= krow, s_t, NEG)
                m_new = jnp.maximum(m, jnp.max(s_t, axis=0, keepdims=True))
                ps[ch] = jnp.exp2(s_t - m_new).astype(BF)
                stats[ch] = (m_new, jnp.exp2(m - m_new))
            out = dict(carry)
            for ch in ss:
                m_new, alpha = stats[ch]
                v_aug = jnp.concatenate([v2[ch[1] * HEAD_DIM:(ch[1] + 1) * HEAD_DIM, :], ones], axis=0)
                out[ch] = (m_new, carry[ch][1] * alpha + jnp.dot(v_aug, ps[ch], preferred_element_type=F32))
            return out

        def advance(j, state):
            ss, carry = state
            return scores(j + 1, chains), update(j, ss, carry, ())

        init = {ch: (jnp.full((1, tq), NEG, F32), jnp.zeros((LANES, tq), F32)) for ch in chains}
        first = nsub * i
        ss, carry = lax.fori_loop(0, first, advance, (scores(0, chains), init))
        carry = update(first, ss, carry, [(0, 0), (0, 1)])
        for u in range(1, nsub):
            rest = [(uu, hh) for uu in range(u, nsub) for hh in range(2)]
            carry = update(first + u, scores(first + u, rest), carry, [(u, 0), (u, 1)])
        for u in range(nsub):
            outs, lses = [], []
            for hh in range(2):
                m, acc = carry[u, hh]
                l = acc[HEAD_DIM:HEAD_DIM + 1, :]
                outs.append(acc[:HEAD_DIM, :] / l)
                lses.append(m + jnp.log2(l))
            o2 = jnp.concatenate(outs, axis=0).T
            o32_ref[u * tq:(u + 1) * tq, :] = o2
            o_ref[u * tq:(u + 1) * tq, :] = o2.astype(BF)
            lse_ref[0, :, u * tq:(u + 1) * tq] = jnp.concatenate(lses, axis=0)

        @pl.when(step == n_steps - 1)
        def _():
            _gather_phase(2, *comm)

    return pl.pallas_call(
        body, name="fox_fwd", grid=(N_HEADS // 2, n_q),
        out_shape=(jax.ShapeDtypeStruct((t_len, GROUP_W), BF), jax.ShapeDtypeStruct((t_len, GROUP_W), F32),
                   jax.ShapeDtypeStruct((N_HEADS // 2, 2, t_len), F32))
        + tuple(jax.ShapeDtypeStruct((4,) + s.shape, s.dtype) for s in shards),
        in_specs=[pl.BlockSpec((tg, 2 * LANES), lambda hp, i: (i, hp)),
                  pl.BlockSpec((t_len, 2 * LANES), lambda hp, i: (0, hp)),
                  pl.BlockSpec((t_len, LANES), lambda hp, i: (0, v_col + hp))] + [ANY] * n_w,
        out_specs=(pl.BlockSpec((tg, LANES), lambda hp, i: (i, hp)), pl.BlockSpec((tg, LANES), lambda hp, i: (i, hp)),
                   pl.BlockSpec((1, 2, tg), lambda hp, i: (hp, 0, i))) + (ANY,) * n_w,
        scratch_shapes=[pltpu.VMEM((LANES, t_len), BF)] + _gather_scratch(n_w),
        compiler_params=_cparams(("arbitrary", "arbitrary")),
    )(q_aug, k_aug, proj, *shards)


def _softmax_rows(s):
    p = jnp.exp(s - jnp.max(s, axis=-1, keepdims=True))
    return p / jnp.sum(p, axis=-1, keepdims=True)


def _attn_out_xattn_fwd(x, mix_r, mix_f, w_out, g_xattn, w_xq, g_xq, kn, v, w_xo):
    t_len = x.shape[0]
    tm = min(ROW_TILE, t_len)

    def body(x_ref, mr_ref, mf_ref, wo_ref, g_ref, wq_ref, gq_ref, kn_ref, v_ref, wxo_ref,
             h1_ref, hn_ref, qx_ref, o_ref, h2_ref):
        h1 = x_ref[...] + jnp.dot(mr_ref[...], wo_ref[:GROUP_W, :], preferred_element_type=F32) \
            + jnp.dot(mf_ref[...], wo_ref[GROUP_W:, :], preferred_element_type=F32)
        h1_ref[...] = h1
        hn = _rms_fwd(h1, g_ref[...]).astype(BF)
        hn_ref[...] = hn
        qx = jnp.dot(hn, wq_ref[...], preferred_element_type=F32).astype(BF)
        qx_ref[...] = qx
        sls = [slice(h * XHD, (h + 1) * XHD) for h in range(N_XH)]
        qns = [_rms_fwd(qx[:, sl].astype(F32), gq_ref[...]).astype(BF) for sl in sls]
        logits = [_dot_nt(qn, kn_ref[:, sl]) * (XHD ** -0.5) for qn, sl in zip(qns, sls)]
        ps = [_softmax_rows(s).astype(BF) for s in logits]
        for p, sl in zip(ps, sls):
            o_ref[:, sl] = jnp.dot(p, v_ref[:, sl], preferred_element_type=F32).astype(BF)
        h2_ref[...] = h1 + jnp.dot(o_ref[...], wxo_ref[...], preferred_element_type=F32)

    row_spec = lambda w: pl.BlockSpec((tm, w), lambda i: (i, 0))
    full = lambda a: pl.BlockSpec(a.shape, lambda i: (0,) * a.ndim)
    return pl.pallas_call(
        body, name="attn_out_xattn_fwd", grid=(t_len // tm,),
        out_shape=(jax.ShapeDtypeStruct((t_len, D_MODEL), F32), jax.ShapeDtypeStruct((t_len, D_MODEL), BF),
                   jax.ShapeDtypeStruct((t_len, D_MODEL), BF), jax.ShapeDtypeStruct((t_len, D_MODEL), BF),
                   jax.ShapeDtypeStruct((t_len, D_MODEL), F32)),
        in_specs=[row_spec(D_MODEL), row_spec(GROUP_W), row_spec(GROUP_W), full(w_out), full(g_xattn), full(w_xq), full(g_xq),
                  full(kn), full(v), full(w_xo)],
        out_specs=(row_spec(D_MODEL),) * 5,
        compiler_params=_cparams(("arbitrary",)),
    )(x, mix_r, mix_f, w_out, g_xattn, w_xq, g_xq, kn, v, w_xo)


def _ffn_loss_fwd(h2, g_ffn, w_gate, w_up, w_down, target):
    t_len = h2.shape[0]
    tm = min(ROW_TILE, t_len)

    def body(h2_ref, g_ref, wg_ref, wu_ref, wd_ref, tgt_ref, hn_ref, gate_ref, up_ref, act_ref, dh3_ref, loss_ref):
        @pl.when(pl.program_id(0) == 0)
        def _():
            loss_ref[...] = jnp.zeros_like(loss_ref)

        h2v = h2_ref[...]
        hn = _rms_fwd(h2v, g_ref[...]).astype(BF)
        hn_ref[...] = hn
        gate = _dot_nt(hn, wg_ref[...])
        up = _dot_nt(hn, wu_ref[...])
        gate_ref[...] = gate.astype(BF)
        up_ref[...] = up.astype(BF)
        act = (gate * _sigmoid(gate) * up).astype(BF)
        act_ref[...] = act
        diff = h2v + jnp.dot(act, wd_ref[...], preferred_element_type=F32) - tgt_ref[...]
        dh3_ref[...] = diff * (1.0 / D_MODEL)
        per_row = jnp.sum(diff * diff, axis=-1, keepdims=True) * (1.0 / D_MODEL)
        loss_ref[...] += 0.5 * jnp.sum(per_row, axis=0, keepdims=True)

    row_spec = lambda w: pl.BlockSpec((tm, w), lambda i: (i, 0))
    full = lambda a: pl.BlockSpec(a.shape, lambda i: (0,) * a.ndim, pipeline_mode=pl.Buffered(1))
    return pl.pallas_call(
        body, name="ffn_loss_fwd", grid=(t_len // tm,),
        out_shape=(jax.ShapeDtypeStruct((t_len, D_MODEL), BF), jax.ShapeDtypeStruct((t_len, D_FF), BF),
                   jax.ShapeDtypeStruct((t_len, D_FF), BF), jax.ShapeDtypeStruct((t_len, D_FF), BF),
                   jax.ShapeDtypeStruct((t_len, D_MODEL), F32), jax.ShapeDtypeStruct((8, LANES), F32)),
        in_specs=[row_spec(D_MODEL), full(g_ffn), full(w_gate), full(w_up), full(w_down), row_spec(D_MODEL)],
        out_specs=(row_spec(D_MODEL), row_spec(D_FF), row_spec(D_FF), row_spec(D_FF), row_spec(D_MODEL),
                   pl.BlockSpec((8, LANES), lambda i: (0, 0))),
        compiler_params=_cparams(("arbitrary",)),
    )(h2, g_ffn, w_gate, w_up, w_down, target)


def _ffn_bwd(dh3, gate, up, h2, g_ffn, w_gate, w_up, w_down):
    t_len = h2.shape[0]
    tm = min(FFN_BWD_TILE, t_len)

    def body(dh3_ref, gate_ref, up_ref, h2_ref, g_ref, wg_ref, wu_ref, wd_ref, dgate_ref, dup_ref, dh2_ref, dg_ref):
        @pl.when(pl.program_id(0) == 0)
        def _():
            dg_ref[...] = jnp.zeros_like(dg_ref)

        dh3v = dh3_ref[...]
        dact = _dot_nt(dh3v, wd_ref[...])
        g = gate_ref[...].astype(F32)
        sg = _sigmoid(g)
        dup = (dact * (g * sg)).astype(BF)
        dgate = (dact * up_ref[...].astype(F32) * (sg * (1.0 + g * (1.0 - sg)))).astype(BF)
        dup_ref[...] = dup
        dgate_ref[...] = dgate
        dhn = jnp.dot(dgate, wg_ref[...], preferred_element_type=F32) + jnp.dot(dup, wu_ref[...], preferred_element_type=F32)
        dx, dg = _rms_bwd(h2_ref[...], g_ref[...], dhn)
        dh2_ref[...] = dh3v + dx
        dg_ref[...] += dg

    row_spec = lambda w: pl.BlockSpec((tm, w), lambda i: (i, 0))
    full = lambda a: pl.BlockSpec(a.shape, lambda i: (0,) * a.ndim, pipeline_mode=pl.Buffered(1))
    return pl.pallas_call(
        body, name="ffn_bwd", grid=(t_len // tm,),
        out_shape=(jax.ShapeDtypeStruct((t_len, D_FF), BF), jax.ShapeDtypeStruct((t_len, D_FF), BF),
                   jax.ShapeDtypeStruct((t_len, D_MODEL), F32), jax.ShapeDtypeStruct((1, D_MODEL), F32)),
        in_specs=[row_spec(D_MODEL), row_spec(D_FF), row_spec(D_FF), row_spec(D_MODEL), full(g_ffn), full(w_gate), full(w_up),
                  full(w_down)],
        out_specs=(row_spec(D_FF), row_spec(D_FF), row_spec(D_MODEL), pl.BlockSpec((1, D_MODEL), lambda i: (0, 0))),
        compiler_params=_cparams(("arbitrary",)),
    )(dh3, gate, up, h2, g_ffn, w_gate, w_up, w_down)


def _attn_out_xattn_bwd(dh2, h1, qx, kn, v, w_xo, w_xq, w_out, g_xattn, g_xq):
    t_len = h1.shape[0]
    tm = min(ROW_TILE, t_len)
    m_tok = kn.shape[0]

    def body(dh2_ref, h1_ref, qx_ref, kn_ref, v_ref, wxo_ref, wq_ref, wo_ref, g_ref, gq_ref,
             dqx_ref, dh1_ref, dmr_ref, dmf_ref, dkn_ref, dv_ref, dg_ref, dgq_ref, dqx_scr):
        @pl.when(pl.program_id(0) == 0)
        def _():
            dkn_ref[...] = jnp.zeros_like(dkn_ref)
            dv_ref[...] = jnp.zeros_like(dv_ref)
            dg_ref[...] = jnp.zeros_like(dg_ref)
            dgq_ref[...] = jnp.zeros_like(dgq_ref)

        dh2v = dh2_ref[...]
        do = _dot_nt(dh2v, wxo_ref[...])
        gq = gq_ref[...]
        sls = [slice(h * XHD, (h + 1) * XHD) for h in range(N_XH)]
        qraws = [qx_ref[:, sl].astype(F32) for sl in sls]
        qns = [_rms_fwd(qraw, gq).astype(BF) for qraw in qraws]
        dohs = [do[:, sl].astype(BF) for sl in sls]
        logits = [_dot_nt(qn, kn_ref[:, sl]) * (XHD ** -0.5) for qn, sl in zip(qns, sls)]
        dps = [_dot_nt(doh, v_ref[:, sl]) for doh, sl in zip(dohs, sls)]
        ps = [_softmax_rows(s) for s in logits]
        dss = [(p * (dp - jnp.sum(dp * p, axis=-1, keepdims=True)) * (XHD ** -0.5)).astype(BF) for p, dp in zip(ps, dps)]
        dqns = []
        for h, sl in enumerate(sls):
            dv_ref[:, sl] += _dot_tn(ps[h], dohs[h])
            dqns.append(jnp.dot(dss[h], kn_ref[:, sl], preferred_element_type=F32))
            dkn_ref[:, sl] += _dot_tn(dss[h], qns[h])
        dgq = jnp.zeros((1, XHD), F32)
        for h, sl in enumerate(sls):
            dx, dg_h = _rms_bwd(qraws[h], gq, dqns[h])
            dgq = dgq + dg_h
            dqx_scr[:, sl] = dx.astype(BF)
        dgq_ref[...] += dgq
        dqx = dqx_scr[...]
        dqx_ref[...] = dqx
        dhn = _dot_nt(dqx, wq_ref[...])
        dx, dg = _rms_bwd(h1_ref[...], g_ref[...], dhn)
        dg_ref[...] += dg
        dh1 = dh2v + dx
        dh1_ref[...] = dh1
        dmix = _dot_nt(dh1, wo_ref[...])
        dmr_ref[...] = dmix[:, :GROUP_W]
        dmf_ref[...] = dmix[:, GROUP_W:].astype(BF)

    row_spec = lambda w: pl.BlockSpec((tm, w), lambda i: (i, 0))
    full = lambda a: pl.BlockSpec(a.shape, lambda i: (0,) * a.ndim)
    acc = lambda r, c: pl.BlockSpec((r, c), lambda i: (0, 0))
    return pl.pallas_call(
        body, name="attn_out_xattn_bwd", grid=(t_len // tm,),
        out_shape=(jax.ShapeDtypeStruct((t_len, D_MODEL), BF), jax.ShapeDtypeStruct((t_len, D_MODEL), F32),
                   jax.ShapeDtypeStruct((t_len, GROUP_W), F32), jax.ShapeDtypeStruct((t_len, GROUP_W), BF),
                   jax.ShapeDtypeStruct((m_tok, D_MODEL), F32), jax.ShapeDtypeStruct((m_tok, D_MODEL), F32),
                   jax.ShapeDtypeStruct((1, D_MODEL), F32), jax.ShapeDtypeStruct((1, XHD), F32)),
        in_specs=[row_spec(D_MODEL), row_spec(D_MODEL), row_spec(D_MODEL), full(kn), full(v), full(w_xo), full(w_xq), full(w_out),
                  full(g_xattn), full(g_xq)],
        out_specs=(row_spec(D_MODEL), row_spec(D_MODEL), row_spec(GROUP_W), row_spec(GROUP_W), acc(m_tok, D_MODEL),
                   acc(m_tok, D_MODEL), acc(1, D_MODEL), acc(1, XHD)),
        scratch_shapes=[pltpu.VMEM((tm, D_MODEL), BF)],
        compiler_params=_cparams(("arbitrary",)),
    )(dh2, h1, qx, kn, v, w_xo, w_xq, w_out, g_xattn, g_xq)


def _mem_kv_bwd(dkn, dv, kraw, mem, memn, g_mem, g_xk, w_xkv):
    m_tok = mem.shape[0]

    def body(dkn_ref, dv_ref, kraw_ref, mem_ref, memn_ref, gm_ref, gk_ref, w_ref, dw_ref, dgm_ref, dgk_ref, dkv_scr):
        gk = gk_ref[...]
        dgk = jnp.zeros((1, XHD), F32)
        for h in range(N_XH):
            sl = slice(h * XHD, (h + 1) * XHD)
            dx, dg_h = _rms_bwd(kraw_ref[:, sl], gk, dkn_ref[:, sl])
            dgk = dgk + dg_h
            dkv_scr[:, sl] = dx.astype(BF)
        dgk_ref[...] = dgk
        dkv_scr[:, D_MODEL:] = dv_ref[...].astype(BF)
        dkv = dkv_scr[...]
        dw_ref[...] = _dot_tn(memn_ref[...], dkv)
        dmemn = _dot_nt(dkv, w_ref[...])
        mem_v = mem_ref[...]
        r = lax.rsqrt(jnp.mean(mem_v * mem_v, axis=-1, keepdims=True) + EPS)
        dgm_ref[...] = jnp.sum(dmemn * mem_v * r, axis=0, keepdims=True)

    return pl.pallas_call(
        body, name="mem_kv_bwd",
        out_shape=(jax.ShapeDtypeStruct((D_MODEL, 2 * D_MODEL), F32), jax.ShapeDtypeStruct((1, D_MODEL), F32),
                   jax.ShapeDtypeStruct((1, XHD), F32)),
        in_specs=[VMEM_SPEC] * 8, out_specs=(VMEM_SPEC,) * 3,
        scratch_shapes=[pltpu.VMEM((m_tok, 2 * D_MODEL), BF)],
        compiler_params=_cparams(),
    )(dkn, dv, kraw, mem, memn, g_mem, g_xk, w_xkv)


def _fox_bwd(q_aug, k_aug, proj, dmf, o32, lse, sums):
    t_len = q_aug.shape[0]
    tb = min(ATT_BLOCK, t_len)
    n_b = t_len // tb
    nsub = 2 if n_b >= 2 else 1
    tg = nsub * tb
    n_g = t_len // tg
    v_col = 6 * GROUP_W // LANES
    n_w = len(sums)
    n_steps = (N_HEADS // 2) * n_g

    def body(*refs):
        k_ref, v_ref, q_ref, do_ref, o_ref, lse_ref = refs[:6]
        dq_ref, dk_ref, dv_ref, df_ref = refs[6 + n_w:10 + n_w]
        delta = refs[10 + 2 * n_w]
        comm = (refs[6:6 + n_w], refs[10 + n_w:10 + 2 * n_w]) + tuple(refs[11 + 2 * n_w:])
        j = pl.program_id(1)
        step = pl.program_id(0) * n_g + j

        @pl.when(step == 0)
        def _():
            _scatter_phase(0, *comm)

        @pl.when(j == 0)
        def _():
            dq_ref[...] = jnp.zeros_like(dq_ref)
            dd = do_ref[...].astype(F32) * o_ref[...]
            hrow = lax.broadcasted_iota(jnp.int32, (8, LANES), 0)
            lane = lax.broadcasted_iota(jnp.int32, (8, LANES), 1)
            ind = ((lane // HEAD_DIM) == hrow).astype(BF)
            delta[...] = _dot_nt_exact(ind, dd)

        k2, v2 = k_ref[...], v_ref[...]
        chains = [(u, hh) for u in range(nsub) for hh in range(2)]
        ks = {(u, hh): k2[u * tb:(u + 1) * tb, hh * LANES:(hh + 1) * LANES] for u, hh in chains}
        vs = {(u, hh): v2[u * tb:(u + 1) * tb, hh * HEAD_DIM:(hh + 1) * HEAD_DIM] for u, hh in chains}

        def block(i, carry, which, masked):
            rows = pl.ds(pl.multiple_of(i * tb, tb), tb)
            q2 = q_ref[rows, :]
            do2 = do_ref[rows, :]
            qs = [q2[:, hh * LANES:(hh + 1) * LANES] for hh in range(2)]
            dos = [do2[:, hh * HEAD_DIM:(hh + 1) * HEAD_DIM] for hh in range(2)]
            ss = {ch: _dot_nt(ks[ch], qs[ch[1]]) for ch in which}
            dps = {ch: _dot_nt(vs[ch], dos[ch[1]]) for ch in which}
            pts, dsts, dfs = {}, {}, {}
            for ch in which:
                hh = ch[1]
                s_t = ss[ch]
                if ch in masked:
                    krow = lax.broadcasted_iota(jnp.int32, (tb, tb), 0)
                    qcol = lax.broadcasted_iota(jnp.int32, (tb, tb), 1)
                    s_t = jnp.where(qcol >= krow, s_t, NEG)
                p_t = jnp.exp2(s_t - lse_ref[0, hh:hh + 1, rows])
                pts[ch] = p_t.astype(BF)
                ds_t = p_t * (dps[ch] - delta[hh:hh + 1, rows])
                dsts[ch] = ds_t.astype(BF)
                dfs[ch] = jnp.sum(ds_t, axis=-1, keepdims=True)
            out = dict(carry)
            for ch in which:
                dk, dv, df = carry[ch]
                dv = dv + jnp.dot(pts[ch], dos[ch[1]], preferred_element_type=F32)
                dk = dk + jnp.dot(dsts[ch], qs[ch[1]], preferred_element_type=F32)
                out[ch] = (dk, dv, df - dfs[ch])
            for hh in range(2):
                parts_dq = [_dot_tn(dsts[ch], ks[ch])[:, :HEAD_DIM] for ch in which if ch[1] == hh]
                dq_ref[rows, hh * HEAD_DIM:(hh + 1) * HEAD_DIM] += sum(parts_dq[1:], parts_dq[0])
            return out

        init = {ch: (jnp.zeros((tb, LANES), F32), jnp.zeros((tb, HEAD_DIM), F32), jnp.zeros((tb, 1), F32)) for ch in chains}
        first = nsub * j
        carry = block(first, init, [(0, 0), (0, 1)], [(0, 0), (0, 1)])
        if nsub == 2:
            carry = block(first + 1, carry, chains, [(1, 0), (1, 1)])
        carry = lax.fori_loop(first + nsub, n_b, lambda i, c: block(i, c, chains, ()), carry)
        for u in range(nsub):
            rs = slice(u * tb, (u + 1) * tb)
            dk_ref[rs, :] = jnp.concatenate([carry[u, hh][0][:, :HEAD_DIM] for hh in range(2)], axis=-1) * LN2
            dv_ref[rs, :] = jnp.concatenate([carry[u, hh][1] for hh in range(2)], axis=-1)
            df_ref[0, rs, :] = jnp.concatenate([carry[u, hh][2] for hh in range(2)], axis=-1)

        @pl.when(step == n_steps - 1)
        def _():
            _scatter_phase(1, *comm)

    blk = lambda w, col0: pl.BlockSpec((tg, w), lambda hp, j: (j, col0 + hp))
    whole = lambda w: pl.BlockSpec((t_len, w), lambda hp, j: (0, hp))
    rows2 = pl.BlockSpec((1, 2, t_len), lambda hp, j: (hp, 0, 0))
    cols2 = pl.BlockSpec((1, tg, 2), lambda hp, j: (hp, j, 0))
    return pl.pallas_call(
        body, name="fox_bwd", grid=(N_HEADS // 2, n_g),
        out_shape=(jax.ShapeDtypeStruct((t_len, GROUP_W), F32), jax.ShapeDtypeStruct((t_len, GROUP_W), F32),
                   jax.ShapeDtypeStruct((t_len, GROUP_W), F32), jax.ShapeDtypeStruct((N_HEADS // 2, t_len, 2), F32))
        + _scatter_out_shapes(sums),
        in_specs=[blk(2 * LANES, 0), blk(LANES, v_col), whole(2 * LANES), whole(LANES), whole(LANES), rows2] + [ANY] * n_w,
        out_specs=(whole(LANES), blk(LANES, 0), blk(LANES, 0), cols2) + (ANY,) * n_w,
        scratch_shapes=[pltpu.VMEM((8, t_len), F32)] + _scatter_scratch(n_w),
        compiler_params=_cparams(("arbitrary", "arbitrary")),
    )(k_aug, proj, q_aug, dmf, o32, lse, *sums)


def _retention_bwd(dmr, raw, proj, g_ret, rq, rk, states, tables, parts):
    t_len = rq.shape[0]
    c = min(RET_BLOCK, t_len)
    n_b = t_len // c
    wdec, qdec, kdec, cdec = tables
    v_col, g_col = 2 * GROUP_W // LANES, 3 * GROUP_W // LANES
    n_w = len(parts)
    n_steps = (N_HEADS // 2) * n_b

    def body(*refs):
        d_ref, raw_ref, rg_ref, g_ref, q_ref, k_ref, v_ref, st_ref, w_ref, wt_ref, qd_ref, kd_ref, cd_ref = refs[:13]
        dq_ref, dk_ref, dv_ref, drg_ref, dg_ref = refs[13 + n_w:18 + n_w]
        gstate = refs[18 + 2 * n_w]
        comm = (refs[13:13 + n_w], refs[18 + n_w:18 + 2 * n_w]) + tuple(refs[19 + 2 * n_w:])
        step = pl.program_id(0) * n_b + pl.program_id(1)

        @pl.when(step == 0)
        def _():
            _exchange_phase(0, *comm)

        @pl.when(pl.program_id(1) == 0)
        def _():
            gstate[...] = jnp.zeros_like(gstate)
            dg_ref[...] = jnp.zeros_like(dg_ref)

        d, raw_v, g = d_ref[...], raw_ref[...], g_ref[0]
        gate = rg_ref[...].astype(F32)
        xc = raw_v - _group_mean64(raw_v)
        r = lax.rsqrt(_group_mean64(xc * xc) + EPS)
        xh = xc * r
        sg = _sigmoid(gate)
        drg_ref[...] = d * (xh * g) * (sg * (1.0 + gate * (1.0 - sg)))
        dy = d * (gate * sg)
        dg_ref[0] += jnp.sum(dy * xh, axis=0, keepdims=True)
        dxh = dy * g
        do2 = r * (dxh - _group_mean64(dxh) - xh * _group_mean64(dxh * xh))
        q2, k2, v2 = q_ref[...], k_ref[...], v_ref[...]
        dqs, dks, dvs = [], [], []
        heads = [tuple(t[:, hh * HEAD_DIM:(hh + 1) * HEAD_DIM] for t in (q2, k2, v2, do2.astype(BF))) for hh in range(2)]
        firsts = [(_dot_nt(k, q) * wt_ref[hh], _dot_nt(do, v) * w_ref[hh], _dot_nt(v, do) * wt_ref[hh])
                  for hh, (q, k, v, do) in enumerate(heads)]
        for hh, (q, k, v, do) in enumerate(heads):
            a_t, dm, dm_t = firsts[hh]
            sp, gs = st_ref[0, 0, hh], gstate[hh]
            qd = q.astype(F32) * qd_ref[hh]
            kd = k.astype(F32) * kd_ref[hh]
            dqs.append(_dot(dm, k) + _dot_nt(do, sp) * qd_ref[hh])
            dks.append(_dot(dm_t, q) + _dot_nt(v, gs) * kd_ref[hh])
            dvs.append(_dot(a_t, do) + _dot(kd, gs))
            gstate[hh] = gs * cd_ref[hh] + _dot_tn(qd, do)
        dq_ref[...] = jnp.concatenate(dqs, axis=-1)
        dk_ref[...] = jnp.concatenate(dks, axis=-1)
        dv_ref[...] = jnp.concatenate(dvs, axis=-1)

        @pl.when(step == n_steps - 1)
        def _():
            _exchange_phase(1, *comm)

    blk = lambda col0: pl.BlockSpec((c, LANES), lambda hp, i: (n_b - 1 - i, col0 + hp))
    tab = lambda a: pl.BlockSpec((2,) + a.shape[1:], lambda hp, i: (hp, 0, 0))
    gspec = pl.BlockSpec((1, 1, LANES), lambda hp, i: (hp, 0, 0))
    return pl.pallas_call(
        body, name="retention_bwd", grid=(N_HEADS // 2, n_b),
        out_shape=(jax.ShapeDtypeStruct((t_len, GROUP_W), F32),) * 4 + (jax.ShapeDtypeStruct((N_HEADS // 2, 1, LANES), F32),)
        + _exchange_out_shapes(parts),
        in_specs=[blk(0), blk(0), blk(g_col), gspec, blk(0), blk(0), blk(v_col),
                  pl.BlockSpec((1, 1, 2, HEAD_DIM, HEAD_DIM), lambda hp, i: (hp, n_b - 1 - i, 0, 0, 0)),
                  tab(wdec), tab(wdec), tab(qdec), tab(kdec), tab(cdec)] + [ANY] * n_w,
        out_specs=(blk(0), blk(0), blk(0), blk(0), gspec) + (ANY,) * n_w,
        scratch_shapes=[pltpu.VMEM((2, HEAD_DIM, HEAD_DIM), F32)] + _exchange_scratch(n_w),
        compiler_params=_cparams(("arbitrary", "arbitrary")),
    )(dmr, raw, proj, g_ret, rq, rk, proj, states, wdec, jnp.transpose(wdec, (0, 2, 1)), qdec, kdec, cdec, *parts)


def _in_proj_bwd(x, g_mix, dh1, dq_r, dk_r, dv_r, drg, dq_f, dk_f, dv_f, df_col, proj, z, cos_t, sin_t, gq_t, gk_t, w_in_t):
    t_len = x.shape[0]
    tm = min(ROW_TILE, t_len)
    n_t = t_len // tm

    def body(x_ref, g_ref, dh1_ref, dqr_ref, dkr_ref, dvr_ref, drg_ref, dqf_ref, dkf_ref, dvf_ref, df_ref, fq_ref, fk_ref, z_ref,
             cos_ref, sin_ref, gq_ref, gk_ref, wm_ref, wf_ref,
             dproj_ref, dz_ref, dx_ref, dg_ref, dgq_ref, dgk_ref, db_ref, carry, gq_acc, gk_acc):
        i = pl.program_id(0)

        @pl.when(i == 0)
        def _():
            carry[...] = jnp.zeros_like(carry)
            gq_acc[...] = jnp.zeros_like(gq_acc)
            gk_acc[...] = jnp.zeros_like(gk_acc)
            dg_ref[...] = jnp.zeros_like(dg_ref)
            db_ref[...] = jnp.zeros_like(db_ref)

        c, s = cos_ref[...], sin_ref[...]
        gq, gk = gq_ref[...], gk_ref[...]
        dgq = jnp.zeros((1, LANES), F32)
        dgk = jnp.zeros((1, LANES), F32)
        for sl in _chunks(GROUP_W):
            dy = dqr_ref[:, sl] * 0.125
            dproj_ref[:, sl] = (dy * c + _swap32(dy * s)).astype(BF)
            dy = dkr_ref[:, sl]
            dproj_ref[:, GROUP_W + sl.start:GROUP_W + sl.stop] = (dy * c + _swap32(dy * s)).astype(BF)
            dproj_ref[:, 2 * GROUP_W + sl.start:2 * GROUP_W + sl.stop] = dvr_ref[:, sl].astype(BF)
            dproj_ref[:, 3 * GROUP_W + sl.start:3 * GROUP_W + sl.stop] = drg_ref[:, sl].astype(BF)
            for src, dsrc, gain, off in ((fq_ref, dqf_ref, gq, 4), (fk_ref, dkf_ref, gk, 5)):
                xr = src[:, sl].astype(F32)
                r = lax.rsqrt(_group_mean64(xr * xr) + EPS)
                xh = xr * r
                dy = dsrc[:, sl] * (0.125 if off == 4 else 1.0)
                dgs = jnp.sum(dy * xh, axis=0, keepdims=True)
                if off == 4:
                    dgq = dgq + dgs
                else:
                    dgk = dgk + dgs
                dxh = dy * gain
                dproj_ref[:, off * GROUP_W + sl.start:off * GROUP_W + sl.stop] = \
                    (r * (dxh - xh * _group_mean64(dxh * xh))).astype(BF)
            dproj_ref[:, 6 * GROUP_W + sl.start:6 * GROUP_W + sl.stop] = dvf_ref[:, sl].astype(BF)
        gq_acc[...] += dgq
        gk_acc[...] += dgk
        row = lax.broadcasted_iota(jnp.int32, (tm, tm), 0)
        col = lax.broadcasted_iota(jnp.int32, (tm, tm), 1)
        dlf = _dot_exact((col >= row).astype(BF), df_ref[...]) + carry[0:1, :]
        carry[...] = jnp.broadcast_to(dlf[0:1, :], carry.shape)
        lane = lax.broadcasted_iota(jnp.int32, (tm, LANES), 1)
        dz = jnp.where(lane < N_HEADS, dlf / (1.0 + jnp.exp(z_ref[...])), 0.0)
        db_ref[...] += jnp.sum(dz, axis=0, keepdims=True)
        dz_bf = dz.astype(BF)
        dz_ref[...] = dz_bf
        dn1 = jnp.dot(dz_bf, wf_ref[...], preferred_element_type=F32)
        for sec in range(MAIN_W // GROUP_W):
            sl = slice(sec * GROUP_W, (sec + 1) * GROUP_W)
            dn1 = dn1 + jnp.dot(dproj_ref[:, sl], wm_ref[sl, :], preferred_element_type=F32)
        dx, dg = _rms_bwd(x_ref[...], g_ref[...], dn1)
        dx_ref[...] = dh1_ref[...] + dx
        dg_ref[...] += dg

        @pl.when(i == n_t - 1)
        def _():
            dgq_ref[...] = gq_acc[:, :HEAD_DIM] + gq_acc[:, HEAD_DIM:]
            dgk_ref[...] = gk_acc[:, :HEAD_DIM] + gk_acc[:, HEAD_DIM:]

    row_spec = lambda w, col=0: pl.BlockSpec((tm, w), lambda i: (n_t - 1 - i, col))
    full = lambda a: pl.BlockSpec(a.shape, lambda i: (0,) * a.ndim)
    acc = lambda r, c: pl.BlockSpec((r, c), lambda i: (0, 0))
    return pl.pallas_call(
        body, name="in_proj_bwd", grid=(n_t,),
        out_shape=(jax.ShapeDtypeStruct((t_len, MAIN_W), BF), jax.ShapeDtypeStruct((t_len, LANES), BF),
                   jax.ShapeDtypeStruct((t_len, D_MODEL), F32), jax.ShapeDtypeStruct((1, D_MODEL), F32),
                   jax.ShapeDtypeStruct((1, HEAD_DIM), F32), jax.ShapeDtypeStruct((1, HEAD_DIM), F32),
                   jax.ShapeDtypeStruct((1, LANES), F32)),
        in_specs=[row_spec(D_MODEL), full(g_mix), row_spec(D_MODEL)] + [row_spec(GROUP_W)] * 7
        + [row_spec(LANES), row_spec(GROUP_W, 4), row_spec(GROUP_W, 5), row_spec(LANES), row_spec(LANES), row_spec(LANES),
           full(gq_t), full(gk_t), *_w_in_specs()],
        out_specs=(row_spec(MAIN_W), row_spec(LANES), row_spec(D_MODEL), acc(1, D_MODEL), acc(1, HEAD_DIM), acc(1, HEAD_DIM),
                   acc(1, LANES)),
        scratch_shapes=[pltpu.VMEM((8, LANES), F32), pltpu.VMEM((1, LANES), F32), pltpu.VMEM((1, LANES), F32)],
        compiler_params=_cparams(("arbitrary",)),
    )(x, g_mix, dh1, dq_r, dk_r, dv_r, drg, dq_f, dk_f, dv_f, df_col, proj, proj, z, cos_t, sin_t, gq_t, gk_t, w_in_t, w_in_t)


def _matmul_tn(a, b, name, bk=1024):
    t_len, m = a.shape
    n = b.shape[1]
    bm = m if m <= TN_MAX_ROWS else m // 2
    bk = min(bk, t_len)

    def body(a_ref, b_ref, o_ref):
        @pl.when(pl.program_id(1) == 0)
        def _():
            o_ref[...] = jnp.zeros_like(o_ref)

        o_ref[...] += _dot_tn(a_ref[...], b_ref[...])

    return pl.pallas_call(
        body, name=name, grid=(m // bm, t_len // bk),
        out_shape=jax.ShapeDtypeStruct((m, n), F32),
        in_specs=[pl.BlockSpec((bk, bm), lambda i, k: (k, i)), pl.BlockSpec((bk, n), lambda i, k: (k, 0))],
        out_specs=pl.BlockSpec((bm, n), lambda i, k: (i, 0)),
        compiler_params=_cparams(("arbitrary", "arbitrary")),
    )(a, b)


def _place():
    x, y, c = lax.axis_index("x"), lax.axis_index("y"), lax.axis_index("c")
    chips = [(1 - x, y), (x, 1 - y), (1 - x, 1 - y)]
    return x, y, c, chips


def _row_chunks(rows, limit):
    step = max(d for d in range(16, min(rows, limit) + 1, 16) if rows % d == 0)
    return [slice(i, i + step) for i in range(0, rows, step)]


ICI_CHUNK_ROWS = 128
D2D_CHUNK_ROWS = 64


def _gather_phase(phase, ins, outs, send_sems, recv_sems):
    x, y, c, chips = _place()
    me_chip = 2 * x + y
    sibling = (x, y, 1 - c)

    def copy(w, k, slot, half, to, rows=slice(None), src=None):
        dst = outs[w].at[slot, half, rows]
        return pltpu.make_async_remote_copy(src_ref=dst if src is None else src, dst_ref=dst,
                                            send_sem=send_sems.at[w, k], recv_sem=recv_sems.at[w, k],
                                            device_id=to, device_id_type=MESH)

    for w in range(len(ins)):
        for j, (px, py) in enumerate(chips):
            if phase == 0:
                for rows in _row_chunks(ins[w].shape[1], ICI_CHUNK_ROWS):
                    copy(w, j, me_chip, c, (px, py, c), rows, src=ins[w].at[c, rows]).start()
            elif phase == 1:
                copy(w, j, 2 * px + py, c, (x, y, c)).wait_recv()
                for rows in _row_chunks(ins[w].shape[1], D2D_CHUNK_ROWS):
                    copy(w, 3 + j, 2 * px + py, c, sibling, rows).start()
            else:
                copy(w, 3 + j, 2 * px + py, 1 - c, (x, y, c)).wait_recv()
                copy(w, j, me_chip, c, (px, py, c), src=ins[w].at[c]).wait_send()
                copy(w, 3 + j, 2 * px + py, c, sibling).wait_send()


def _gather_scratch(n_w):
    return [pltpu.SemaphoreType.DMA((n_w, 6)), pltpu.SemaphoreType.DMA((n_w, 6))]


def _all_gather_weights(shards):
    n_w = len(shards)

    def body(*refs):
        for phase in range(3):
            _gather_phase(phase, refs[:n_w], refs[n_w:2 * n_w], *refs[2 * n_w:])

    return pl.pallas_call(
        body, name="all_gather_weights",
        out_shape=tuple(jax.ShapeDtypeStruct((4,) + s.shape, s.dtype) for s in shards),
        in_specs=[ANY] * n_w, out_specs=(ANY,) * n_w, scratch_shapes=_gather_scratch(n_w),
    )(*shards)


def _exchange_phase(phase, ins, theirs, send_sems, recv_sems):
    x, y, c, _ = _place()

    def remote(w, k=slice(None), rows=slice(None)):
        return pltpu.make_async_remote_copy(src_ref=ins[w].at[k, 1 - c, rows], dst_ref=theirs[w].at[k, rows],
                                            send_sem=send_sems.at[w], recv_sem=recv_sems.at[w], device_id=(x, y, 1 - c),
                                            device_id_type=MESH)

    for w in range(len(ins)):
        if phase == 0:
            for k in range(4):
                for rows in _row_chunks(ins[w].shape[2], D2D_CHUNK_ROWS):
                    remote(w, k, rows).start()
        else:
            remote(w).wait()


def _exchange_scratch(n_w):
    return [pltpu.SemaphoreType.DMA((n_w,)), pltpu.SemaphoreType.DMA((n_w,))]


def _exchange_out_shapes(grads):
    return tuple(jax.ShapeDtypeStruct((4,) + g.shape[2:], g.dtype) for g in grads)


def _add_pairs(part, theirs, name):
    _, _, r, c = part.shape
    rb = 32 if r % 32 == 0 else r

    def body(a_ref, b_ref, own_ref, ob_ref):
        my_chip = 2 * lax.axis_index("x") + lax.axis_index("y")
        ob_ref[...] = (a_ref[...] + b_ref[...]).astype(BF)
        own_ref[...] = a_ref[my_chip] + b_ref[my_chip]

    spec = pl.BlockSpec((4, rb, c), lambda i: (0, i, 0))
    return pl.pallas_call(
        body, name=name, grid=(r // rb,),
        out_shape=(jax.ShapeDtypeStruct((r, c), F32), jax.ShapeDtypeStruct((4, r, c), BF)),
        in_specs=[pl.BlockSpec((4, None, rb, c), lambda i: (0, lax.axis_index("c"), i, 0)), spec],
        out_specs=(pl.BlockSpec((rb, c), lambda i: (i, 0)), spec), compiler_params=_cparams(("arbitrary",)),
    )(part, theirs)


def _scatter_phase(phase, bfs, got, send_sems, recv_sems):
    x, y, c, chips = _place()

    def remote(w, j, px, py, rows=slice(None)):
        return pltpu.make_async_remote_copy(src_ref=bfs[w].at[2 * px + py, rows], dst_ref=got[w].at[j, rows],
                                            send_sem=send_sems.at[w, j], recv_sem=recv_sems.at[w, j], device_id=(px, py, c),
                                            device_id_type=MESH)

    for w in range(len(bfs)):
        for j, (px, py) in enumerate(chips):
            if phase == 0:
                for rows in _row_chunks(bfs[w].shape[1], ICI_CHUNK_ROWS):
                    remote(w, j, px, py, rows).start()
            else:
                remote(w, j, px, py).wait()


def _scatter_scratch(n_w):
    return [pltpu.SemaphoreType.DMA((n_w, 3)), pltpu.SemaphoreType.DMA((n_w, 3))]


def _scatter_out_shapes(sums_bf16):
    return tuple(jax.ShapeDtypeStruct((3,) + s.shape[1:], BF) for s in sums_bf16)


def _add_received(own, got, name):
    r, c = own.shape
    rb = 32 if r % 32 == 0 else r

    def body(o_ref, g_ref, out_ref):
        out_ref[...] = ((o_ref[...] + g_ref[0].astype(F32)) + g_ref[1].astype(F32)) + g_ref[2].astype(F32)

    return pl.pallas_call(
        body, name=name, grid=(r // rb,), out_shape=jax.ShapeDtypeStruct((r, c), F32),
        in_specs=[pl.BlockSpec((rb, c), lambda i: (i, 0)), pl.BlockSpec((3, rb, c), lambda i: (0, i, 0))],
        out_specs=pl.BlockSpec((rb, c), lambda i: (i, 0)), compiler_params=_cparams(("arbitrary",)),
    )(own, got)


def _share_with_sibling(halves):
    n_w = len(halves)

    def body(*refs):
        ins, outs = refs[:n_w], refs[n_w:2 * n_w]
        send_sems, recv_sems = refs[2 * n_w:]
        x, y, c, _ = _place()

        def remote(w, rows=slice(None)):
            return pltpu.make_async_remote_copy(src_ref=ins[w].at[rows], dst_ref=outs[w].at[c, rows], send_sem=send_sems.at[w],
                                                recv_sem=recv_sems.at[w], device_id=(x, y, 1 - c), device_id_type=MESH)

        for w in range(n_w):
            for rows in _row_chunks(ins[w].shape[0], D2D_CHUNK_ROWS):
                remote(w, rows).start()
        for w in range(n_w):
            remote(w).wait()

    return pl.pallas_call(
        body, name="share_with_sibling",
        out_shape=tuple(jax.ShapeDtypeStruct((2,) + h.shape, h.dtype) for h in halves),
        in_specs=[ANY] * n_w, out_specs=(ANY,) * n_w,
        scratch_shapes=[pltpu.SemaphoreType.DMA((n_w,)), pltpu.SemaphoreType.DMA((n_w,))],
    )(*halves)


def _small_phase(phase, p_ref, out_ref, slots, send_sems, recv_sems):
    x, y, cc, _ = _place()
    me = 4 * x + 2 * y + cc
    copies = []
    for k in range(1, 8):
        dx, dy, dc = (k >> 2) & 1, (k >> 1) & 1, k & 1
        to = (1 - x if dx else x, 1 - y if dy else y, 1 - cc if dc else cc)
        copies.append(pltpu.make_async_remote_copy(src_ref=p_ref, dst_ref=slots.at[me], send_sem=send_sems.at[k - 1],
                                                   recv_sem=recv_sems.at[k - 1], device_id=to, device_id_type=MESH))
    if phase == 0:
        slots[me] = p_ref[...]
        for cp in copies:
            cp.start()
    else:
        for cp in copies:
            cp.wait()
        total = slots[0]
        for d in range(1, 8):
            total = total + slots[d]
        out_ref[...] = total


def _adamw_update(w_ref, g_ref, m_ref, v_ref, d_ref, nm_ref, nv_ref):
    gv = g_ref[...]
    nm = ADAM_B1 * m_ref[...] + (1.0 - ADAM_B1) * gv
    nv = ADAM_B2 * v_ref[...] + (1.0 - ADAM_B2) * (gv * gv)
    nm_ref[...] = nm
    nv_ref[...] = nv
    m_hat = nm / (1.0 - ADAM_B1 ** ADAM_STEP)
    v_hat = nv / (1.0 - ADAM_B2 ** ADAM_STEP)
    d_ref[...] = -ADAM_LR * (m_hat / (jnp.sqrt(v_hat) + ADAM_EPS) + ADAM_WD * w_ref[...])


def _adamw_many(ws, gs, ms, vs, sums, pack):
    n_a, n_w = len(ws), len(sums)
    n_steps = ADAM_STEPS
    specs = [pl.BlockSpec((w.shape[0] // n_steps, w.shape[1]), lambda i: (i, 0)) for w in ws]
    pack_spec = pl.BlockSpec(pack.shape, lambda i: (0, 0))

    def body(*refs):
        ins = refs[:4 * n_a]
        p_ref = refs[4 * n_a + n_w]
        first_out = 4 * n_a + n_w + 1
        outs = refs[first_out:first_out + 3 * n_a]
        total_ref = refs[first_out + 3 * n_a + n_w]
        scratch = refs[first_out + 3 * n_a + n_w + 1:]
        scatter = (refs[4 * n_a:4 * n_a + n_w], refs[first_out + 3 * n_a:first_out + 3 * n_a + n_w]) + tuple(scratch[:2])
        small = (p_ref, total_ref) + tuple(scratch[2:])
        step = pl.program_id(0)

        @pl.when(step == 0)
        def _():
            _scatter_phase(0, *scatter)
            _small_phase(0, *small)

        for a in range(n_a):
            _adamw_update(*(ins[k * n_a + a] for k in range(4)), *(outs[3 * a + k] for k in range(3)))

        @pl.when(step == n_steps - 1)
        def _():
            _scatter_phase(1, *scatter)
            _small_phase(1, *small)

    flat = pl.pallas_call(
        body, name="adamw_late", grid=(n_steps,),
        out_shape=tuple(jax.ShapeDtypeStruct(w.shape, F32) for w in ws for _ in range(3)) + _scatter_out_shapes(sums)
        + (jax.ShapeDtypeStruct(pack.shape, F32),),
        in_specs=specs * 4 + [ANY] * n_w + [pack_spec],
        out_specs=tuple(s for s in specs for _ in range(3)) + (ANY,) * n_w + (pack_spec,),
        scratch_shapes=_scatter_scratch(n_w) + [pltpu.VMEM((8,) + pack.shape, F32), pltpu.SemaphoreType.DMA((7,)),
                                                pltpu.SemaphoreType.DMA((7,))],
        compiler_params=_cparams(("arbitrary",)),
    )(*ws, *gs, *ms, *vs, *sums, pack)
    return [tuple(flat[3 * a:3 * a + 3]) for a in range(n_a)] + list(flat[3 * n_a:])


def _adamw(w, g, m, v, name):
    r, c = w.shape
    rb, cb = (64, c) if r % 64 == 0 else (r, LANES if (r % 8 and c % LANES == 0) else c)

    def body(*refs):
        _adamw_update(*refs)

    spec = pl.BlockSpec((rb, cb), lambda i, j: (i, j))
    return pl.pallas_call(
        body, name=name, grid=(r // rb, c // cb), out_shape=(jax.ShapeDtypeStruct((r, c), F32),) * 3,
        in_specs=[spec] * 4, out_specs=(spec,) * 3, compiler_params=_cparams(("arbitrary", "arbitrary")),
    )(w, g, m, v)


def _rope_tables(t_len):
    inv_freq = ROPE_BASE ** (-jnp.arange(0, HEAD_DIM, 2, dtype=F32) / HEAD_DIM)
    ang = jnp.arange(t_len, dtype=F32)[:, None] * inv_freq[None, :]
    cos, sin = jnp.cos(ang), jnp.sin(ang)
    cos_t = jnp.concatenate([cos, cos, cos, cos], axis=-1)
    sin_t = jnp.concatenate([-sin, sin, -sin, sin], axis=-1)
    return cos_t, sin_t


def _cols_to_shards(dw):
    r, n = dw.shape
    return jnp.transpose(dw.reshape(2, r // 2, 4, n // 4), (2, 0, 1, 3))


def _rows_to_shards(dw):
    r, n = dw.shape
    padded = _pad_rows(dw.reshape(4, r // 4, n))
    return padded.reshape(4, 2, padded.shape[1] // 2, n)


def _pad_lanes(a):
    extra = -a.shape[-1] % LANES
    return a if extra == 0 else jnp.pad(a, [(0, 0)] * (a.ndim - 1) + [(0, extra)])


def _pad_rows(a):
    rows = a.shape[-2]
    extra = 0 if rows % SHARD_ROW_ALIGN == 0 else -rows % SHARD_ROW_PAD
    return a if extra == 0 else jnp.pad(a, [(0, 0)] * (a.ndim - 2) + [(0, extra), (0, 0)])


def _pad_row(a, width=D_MODEL):
    a = a.reshape(1, -1)
    return jnp.pad(a, ((0, 0), (0, width - a.shape[1])))


def kernel(x, mem, g_mix, w_in, b_forget, g_ret_out, g_fox_q, g_fox_k, w_out, g_xattn, w_xq, w_xkv, g_mem, g_xq, g_xk, w_xo, g_ffn, w_gate, w_up, w_down, loss_target, m_g_mix, m_w_in, m_b_forget, m_g_ret_out, m_g_fox_q, m_g_fox_k, m_w_out, m_g_xattn, m_w_xq, m_w_xkv, m_g_mem, m_g_xq, m_g_xk, m_w_xo, m_g_ffn, m_w_gate, m_w_up, m_w_down, v_g_mix, v_w_in, v_b_forget, v_g_ret_out, v_g_fox_q, v_g_fox_k, v_w_out, v_g_xattn, v_w_xq, v_w_xkv, v_g_mem, v_g_xq, v_g_xk, v_w_xo, v_g_ffn, v_w_gate, v_w_up, v_w_down):
    big = {"w_in": (w_in, m_w_in, v_w_in), "w_out": (w_out, m_w_out, v_w_out), "w_xq": (w_xq, m_w_xq, v_w_xq),
           "w_xkv": (w_xkv, m_w_xkv, v_w_xkv), "w_xo": (w_xo, m_w_xo, v_w_xo), "w_gate": (w_gate, m_w_gate, v_w_gate),
           "w_up": (w_up, m_w_up, v_w_up), "w_down": (w_down, m_w_down, v_w_down)}
    for n in TRANSPOSED:
        big[n] = tuple(jnp.swapaxes(a, 1, 2) for a in big[n])
    shards = {}
    for n in big:
        w = _pad_rows(_pad_lanes(big[n][0][0].astype(BF)))
        shards[n] = w.reshape(2, w.shape[0] // 2, w.shape[1])
    sizes = {n: big[n][0].shape[1:] for n in big}
    w_in_full = _assemble_weight("w_in", _all_gather_weights([shards["w_in"]])[0], shards["w_in"], sizes["w_in"])
    small_w ={"g_mix": g_mix, "b_forget": b_forget, "g_ret_out": g_ret_out, "g_fox_q": g_fox_q, "g_fox_k": g_fox_k,
               "g_xattn": g_xattn, "g_mem": g_mem, "g_xq": g_xq, "g_xk": g_xk, "g_ffn": g_ffn}
    m_small = {"g_mix": m_g_mix, "b_forget": m_b_forget, "g_ret_out": m_g_ret_out, "g_fox_q": m_g_fox_q, "g_fox_k": m_g_fox_k,
               "g_xattn": m_g_xattn, "g_mem": m_g_mem, "g_xq": m_g_xq, "g_xk": m_g_xk, "g_ffn": m_g_ffn}
    v_small = {"g_mix": v_g_mix, "b_forget": v_b_forget, "g_ret_out": v_g_ret_out, "g_fox_q": v_g_fox_q, "g_fox_k": v_g_fox_k,
               "g_xattn": v_g_xattn, "g_mem": v_g_mem, "g_xq": v_g_xq, "g_xk": v_g_xk, "g_ffn": v_g_ffn}
    loss_part, grad_x, sums, got, in_parts, small_g = _local_step(x[0], mem[0], loss_target[0], w_in_full, shards, sizes, small_w)
    return _reduce_and_update(big, sums, got, in_parts, small_w, small_g, loss_part, grad_x, m_small, v_small)


def _assemble_weight(name, gathered, own, size):
    rows, width = size
    my_chip = 2 * lax.axis_index("x") + lax.axis_index("y")
    g = lax.dynamic_update_slice(gathered, own[None], (my_chip, 0, 0, 0))
    g = g.reshape(4, 2 * g.shape[2], g.shape[3])[:, :rows, :width]
    return jnp.transpose(g, (1, 0, 2)).reshape(rows, 4 * width) if name in COL_SHARDED else g.reshape(4 * rows, width)


def _shard_parts(names, dw):
    return [_pad_lanes(_cols_to_shards(dw[n]) if n in COL_SHARDED else _rows_to_shards(dw[n])) for n in names]


def _core_sums(names, parts, theirs):
    return [_add_pairs(p, t, f"core_sum_{n}") for n, p, t in zip(names, parts, theirs)]


def _local_step(xs, mems, tgt, w_in_full, shards, sizes, small_w):
    g_mix, b_forget, g_ret_out, g_fox_q, g_fox_k = (small_w[n] for n in ("g_mix", "b_forget", "g_ret_out", "g_fox_q", "g_fox_k"))
    g_xattn, g_mem, g_xq, g_xk, g_ffn = (small_w[n] for n in ("g_xattn", "g_mem", "g_xq", "g_xk", "g_ffn"))
    w_in_t = jnp.pad(w_in_full, ((0, MAIN_W + LANES - IN_W), (0, 0)))
    t_len = xs.shape[0]
    cos_t, sin_t = _rope_tables(t_len)
    tables = _decay_tables(min(RET_BLOCK, t_len))
    gq_t = jnp.concatenate([g_fox_q, g_fox_q], axis=-1)
    gk_t = jnp.concatenate([g_fox_k, g_fox_k], axis=-1)
    b_pad = _pad_row(b_forget, LANES)
    g_ret = g_ret_out.reshape(N_HEADS // 2, 1, LANES)

    n1, proj, rq, rk, q_aug, k_aug, z = _in_proj_fwd(xs, g_mix, w_in_t, b_pad, cos_t, sin_t, gq_t, gk_t)
    raw, mix_r, states = _retention_fwd(rq, rk, proj, g_ret, tables)
    mix_f, o32, lse, *gathered = _fox_fwd(q_aug, k_aug, proj, [shards[n] for n in LATE])
    full = {n: _assemble_weight(n, g, shards[n], sizes[n]) for n, g in zip(LATE, gathered)}
    memn, kraw, kn, vmem = _mem_kv_fwd(mems, g_mem, full["w_xkv"], g_xk)
    h1, hn2, qx, o_x, h2 = _attn_out_xattn_fwd(xs, mix_r, mix_f, full["w_out"], g_xattn, full["w_xq"], g_xq, kn, vmem, full["w_xo"])
    hn3, gate, up, act, dh3, loss_part = _ffn_loss_fwd(h2, g_ffn, full["w_gate"], full["w_up"], full["w_down"], tgt)

    dgate, dup, dh2, dg_ffn = _ffn_bwd(dh3, gate, up, h2, g_ffn, full["w_gate"], full["w_up"], full["w_down"])
    dqx, dh1, dmr, dmf, dkn, dvm, dg_xattn, dg_xq = _attn_out_xattn_bwd(dh2, h1, qx, kn, vmem, full["w_xo"], full["w_xq"],
                                                                      full["w_out"], g_xattn, g_xq)
    dw_xkv, dg_mem, dg_xk = _mem_kv_bwd(dkn, dvm, kraw, mems, memn, g_mem, g_xk, full["w_xkv"])
    dw = {
        "w_out": jnp.concatenate([_matmul_tn(mix_r, dh1, "dw_out_ret"), _matmul_tn(mix_f, dh1, "dw_out_fox")], axis=0),
        "w_xq": _matmul_tn(hn2, dqx, "dw_xq"),
        "w_xkv": dw_xkv,
        "w_xo": _matmul_tn(o_x, dh2, "dw_xo"),
        "w_gate": _matmul_tn(dgate, hn3, "dw_gate"),
        "w_up": _matmul_tn(dup, hn3, "dw_up"),
        "w_down": _matmul_tn(act, dh3, "dw_down"),
    }
    late_parts = _shard_parts(LATE, dw)
    dq_r, dk_r, dv_r, drg, dg_ret, *late_theirs = _retention_bwd(dmr, raw, proj, g_ret, rq, rk, states, tables, late_parts)
    late_sums = _core_sums(LATE, late_parts, late_theirs)
    dq_f, dk_f, dv_f, df, *late_got = _fox_bwd(q_aug, k_aug, proj, dmf, o32, lse, [s[1] for s in late_sums])
    df_col = jnp.pad(jnp.transpose(df, (1, 0, 2)).reshape(t_len, N_HEADS), ((0, 0), (0, LANES - N_HEADS)))
    dproj, dz, grad_x, dg_mix, dg_fq, dg_fk, db = _in_proj_bwd(xs, g_mix, dh1, dq_r, dk_r, dv_r, drg, dq_f, dk_f, dv_f, df_col,
                                                              proj, z, cos_t, sin_t, gq_t, gk_t, w_in_t)

    dw_in = jnp.concatenate([_matmul_tn(dproj, n1, "dw_in_main"), _matmul_tn(dz, n1, "dw_in_ff")[:IN_W - MAIN_W]], axis=0)
    in_parts = _shard_parts(("w_in",), {"w_in": dw_in})
    sums = {n: s[0] for n, s in zip(LATE, late_sums)}
    got = dict(zip(LATE, late_got))
    small_g = {"g_mix": dg_mix, "b_forget": db[:, :N_HEADS], "g_ret_out": dg_ret, "g_fox_q": dg_fq, "g_fox_k": dg_fk,
               "g_xattn": dg_xattn, "g_mem": dg_mem, "g_xq": dg_xq, "g_xk": dg_xk, "g_ffn": dg_ffn}
    return loss_part, grad_x, sums, got, in_parts, small_g


def _add_received_many(owns, gots, parts):
    n_a, n_w = len(owns), len(parts)
    n_steps = CHIP_SUM_STEPS
    own_specs = [pl.BlockSpec((o.shape[0] // n_steps, o.shape[1]), lambda i: (i, 0)) for o in owns]
    got_specs = [pl.BlockSpec((3, o.shape[0] // n_steps, o.shape[1]), lambda i: (0, i, 0)) for o in owns]

    def body(*refs):
        first_out = 2 * n_a + n_w
        comm = (refs[2 * n_a:first_out], refs[first_out + n_a:first_out + n_a + n_w]) + tuple(refs[first_out + n_a + n_w:])
        step = pl.program_id(0)

        @pl.when(step == 0)
        def _():
            _exchange_phase(0, *comm)

        for a in range(n_a):
            o_ref, g_ref, out_ref = refs[a], refs[n_a + a], refs[first_out + a]
            out_ref[...] = ((o_ref[...] + g_ref[0].astype(F32)) + g_ref[1].astype(F32)) + g_ref[2].astype(F32)

        @pl.when(step == n_steps - 1)
        def _():
            _exchange_phase(1, *comm)

    flat = pl.pallas_call(
        body, name="chip_sum_late", grid=(n_steps,),
        out_shape=tuple(jax.ShapeDtypeStruct(o.shape, F32) for o in owns) + _exchange_out_shapes(parts),
        in_specs=own_specs + got_specs + [ANY] * n_w, out_specs=tuple(own_specs) + (ANY,) * n_w,
        scratch_shapes=_exchange_scratch(n_w), compiler_params=_cparams(("arbitrary",)),
    )(*owns, *gots, *parts)
    return list(flat[:n_a]), list(flat[n_a:])


def _final_grads(names, big, finals):
    my_core = lax.axis_index("c")
    shared = _share_with_sibling(finals)
    out = {}
    for n, s, fin in zip(names, shared, finals):
        s = lax.dynamic_update_slice(s, fin[None], (my_core, 0, 0))
        out[n] = s.reshape(2 * s.shape[1], s.shape[2])[:big[n][0].shape[1], :big[n][0].shape[2]]
    return out


def _reduce_and_update(big, sums, got, in_parts, small_w, small_g, loss_part, grad_x, m_small, v_small):
    small_names = list(small_w)
    pad_rows = SMALL_ROWS - len(small_names) - 1
    stack = lambda d: jnp.concatenate([_pad_row(d[n]) for n in small_names] + [jnp.zeros((pad_rows + 1, D_MODEL), F32)], axis=0)
    g_pack = jnp.concatenate([_pad_row(small_g[n]) for n in small_names] + [_pad_row(loss_part[0:1, 0:1])]
                             + [jnp.zeros((pad_rows, D_MODEL), F32)], axis=0)
    late_finals, in_theirs = _add_received_many([sums[n] for n in LATE], [got[n] for n in LATE], in_parts)
    in_own, in_bf = _core_sums(("w_in",), in_parts, in_theirs)[0]
    grads = _final_grads(LATE, big, late_finals)
    *late_updates, in_got, g_tot = _adamw_many([big[n][0][0] for n in LATE], [grads[n] for n in LATE], [big[n][1][0] for n in LATE],
                                               [big[n][2][0] for n in LATE], [in_bf], g_pack)
    updates = dict(zip(LATE, late_updates))
    grads.update(_final_grads(("w_in",), big, [_add_received(in_own, in_got, "chip_sum_w_in")]))
    updates["w_in"] = _adamw(big["w_in"][0][0], grads["w_in"], big["w_in"][1][0], big["w_in"][2][0], "adamw_w_in")
    deltas, new_m, new_v = {}, {}, {}
    for n in big:
        restore = (lambda a: jnp.swapaxes(a[None], 1, 2)) if n in TRANSPOSED else (lambda a: a[None])
        grads[n] = restore(grads[n])
        deltas[n], new_m[n], new_v[n] = (restore(a) for a in updates[n])

    d_s, m_s, v_s = _adamw(stack(small_w), g_tot, stack(m_small), stack(v_small), "adamw_small")
    for i, n in enumerate(small_names):
        shape = small_w[n].shape
        size = int(np.prod(shape))
        grads[n] = g_tot[i, :size].reshape(shape)
        deltas[n], new_m[n], new_v[n] = d_s[i, :size].reshape(shape), m_s[i, :size].reshape(shape), v_s[i, :size].reshape(shape)
    loss = g_tot[len(small_names), 0]

    order = ["g_mix", "w_in", "b_forget", "g_ret_out", "g_fox_q", "g_fox_k", "w_out", "g_xattn", "w_xq", "w_xkv", "g_mem", "g_xq",
             "g_xk", "w_xo", "g_ffn", "w_gate", "w_up", "w_down"]
    return (loss, grad_x[None], *[grads[n] for n in order], *[deltas[n] for n in order], *[new_m[n] for n in order],
            *[new_v[n] for n in order])
```

```python
import functools

import numpy as np
import jax
import jax.numpy as jnp
from jax import lax
from jax.experimental import pallas as pl
from jax.experimental.pallas import tpu as pltpu

F32 = jnp.float32
BF = jnp.bfloat16

D_MODEL = 1024
HEAD_DIM = 64
N_HEADS = 8
GROUP_W = 512
N_XH = 4
XHD = 256
D_FF = 2816
MAIN_W = 3584
IN_W = 3592
ROPE_BASE = 10000.0
LOG2E = 1.4426950408889634
LN2 = 0.6931471805599453
EPS = 1e-6
NEG = -1e30
LANES = 128
SUBLANES = 8
RET_BLOCK = 256
REF_CHUNK = 64
ROW_TILE = 512
FFN_BWD_TILE = 256
ATT_BLOCK = 256
FWD_GROUP = 4
TN_MAX_ROWS = 1408
SMALL_ROWS = 16
COL_SHARDED = ("w_xkv",)
TRANSPOSED = ("w_in", "w_gate", "w_up")
SHARD_ROW_ALIGN = 32
SHARD_ROW_PAD = 256
LATE = ("w_out", "w_xq", "w_xkv", "w_xo", "w_gate", "w_up", "w_down")
VMEM_LIMIT = 56 * 1024 * 1024

ADAM_LR = 0.001
ADAM_B1 = 0.9
ADAM_B2 = 0.999
ADAM_EPS = 1e-08
ADAM_WD = 0.01
ADAM_STEP = 10
CHIP_SUM_STEPS = 2
ADAM_STEPS = 8

MESH = pl.DeviceIdType.MESH
ANY = pl.BlockSpec(memory_space=pl.ANY)
VMEM_SPEC = pl.BlockSpec(memory_space=pltpu.VMEM)


def _cparams(sem=None, vmem=VMEM_LIMIT):
    return pltpu.CompilerParams(dimension_semantics=sem, vmem_limit_bytes=vmem)


def _dot(a, b):
    return jnp.dot(a.astype(BF), b.astype(BF), preferred_element_type=F32)


def _dot_nt(a, b):
    return lax.dot_general(a.astype(BF), b.astype(BF), (((1,), (1,)), ((), ())), preferred_element_type=F32)


def _dot_tn(a, b):
    return lax.dot_general(a.astype(BF), b.astype(BF), (((0,), (0,)), ((), ())), preferred_element_type=F32)


def _split3(x):
    hi = x.astype(BF)
    r = x - hi.astype(F32)
    mid = r.astype(BF)
    lo = (r - mid.astype(F32)).astype(BF)
    return hi, mid, lo


def _dot_exact(ind, x):
    hi, mid, lo = _split3(x)
    return (jnp.dot(ind, lo, preferred_element_type=F32) + jnp.dot(ind, mid, preferred_element_type=F32)
            + jnp.dot(ind, hi, preferred_element_type=F32))


def _dot_nt_exact(ind, x):
    hi, mid, lo = _split3(x)
    dn = (((1,), (1,)), ((), ()))
    return (lax.dot_general(ind, lo, dn, preferred_element_type=F32) + lax.dot_general(ind, mid, dn, preferred_element_type=F32)
            + lax.dot_general(ind, hi, dn, preferred_element_type=F32))


def _sigmoid(x):
    return 1.0 / (1.0 + jnp.exp(-x))


def _rms_fwd(x, g):
    r = lax.rsqrt(jnp.mean(x * x, axis=-1, keepdims=True) + EPS)
    return x * r * g


def _rms_bwd(x, g, dy):
    r = lax.rsqrt(jnp.mean(x * x, axis=-1, keepdims=True) + EPS)
    xh = x * r
    dg = jnp.sum(dy * xh, axis=0, keepdims=True)
    dxh = dy * g
    dx = r * (dxh - xh * jnp.mean(dxh * xh, axis=-1, keepdims=True))
    return dx, dg


def _group_mean64(x):
    lane = lax.broadcasted_iota(jnp.int32, x.shape, 1)
    lo = lane < HEAD_DIM
    s_lo = jnp.sum(jnp.where(lo, x, 0.0), axis=-1, keepdims=True)
    s_hi = jnp.sum(jnp.where(lo, 0.0, x), axis=-1, keepdims=True)
    return jnp.where(lo, s_lo, s_hi) * (1.0 / HEAD_DIM)


def _swap32(x):
    lane = lax.broadcasted_iota(jnp.int32, x.shape, 1)
    first = (lane % HEAD_DIM) < (HEAD_DIM // 2)
    return jnp.where(first, pltpu.roll(x, LANES - HEAD_DIM // 2, axis=1), pltpu.roll(x, HEAD_DIM // 2, axis=1))


def _chunks(w):
    return [slice(j * LANES, (j + 1) * LANES) for j in range(w // LANES)]


def _aug_pair(qk, f_cols, is_query):
    lane = lax.broadcasted_iota(jnp.int32, qk.shape, 1)
    a = lane - HEAD_DIM
    values = (qk, pltpu.roll(qk, HEAD_DIM, axis=1))
    out = []
    for hh in range(2):
        hi, mid, lo = (p.astype(F32) for p in _split3(f_cols[hh] * LOG2E))
        if is_query:
            aux = jnp.where(a == 0, hi, jnp.where(a == 1, mid, jnp.where(a == 2, lo, jnp.where(a < 6, 1.0, 0.0))))
        else:
            aux = jnp.where(a < 3, 1.0, jnp.where(a == 3, -hi, jnp.where(a == 4, -mid, jnp.where(a == 5, -lo, 0.0))))
        out.append(jnp.where(a < 0, values[hh], aux))
    return jnp.concatenate(out, axis=-1).astype(BF)


def _mem_kv_fwd(mem, g_mem, w_xkv, g_xk):
    m_tok = mem.shape[0]

    def body(mem_ref, gm_ref, w_ref, gk_ref, memn_ref, kraw_ref, kn_ref, v_ref):
        mn = _rms_fwd(mem_ref[...], gm_ref[...]).astype(BF)
        memn_ref[...] = mn
        kv = jnp.dot(mn, w_ref[...], preferred_element_type=F32)
        k = kv[:, :D_MODEL]
        kraw_ref[...] = k
        v_ref[...] = kv[:, D_MODEL:].astype(BF)
        for h in range(N_XH):
            sl = slice(h * XHD, (h + 1) * XHD)
            kn_ref[:, sl] = _rms_fwd(k[:, sl], gk_ref[...]).astype(BF)

    return pl.pallas_call(
        body, name="mem_kv_fwd",
        out_shape=(jax.ShapeDtypeStruct((m_tok, D_MODEL), BF), jax.ShapeDtypeStruct((m_tok, D_MODEL), F32),
                   jax.ShapeDtypeStruct((m_tok, D_MODEL), BF), jax.ShapeDtypeStruct((m_tok, D_MODEL), BF)),
        in_specs=[VMEM_SPEC] * 4, out_specs=(VMEM_SPEC,) * 4, compiler_params=_cparams(),
    )(mem, g_mem, w_xkv, g_xk)


def _in_proj_fwd(x, g_mix, w_in_t, b_pad, cos_t, sin_t, gq_t, gk_t):
    t_len = x.shape[0]
    tm = min(ROW_TILE, t_len)
    n_t = t_len // tm

    def body(x_ref, g_ref, wm_ref, wf_ref, b_ref, cos_ref, sin_ref, gq_ref, gk_ref,
             n1_ref, proj_ref, rq_ref, rk_ref, qa_ref, ka_ref, z_ref, carry):
        i = pl.program_id(0)

        @pl.when(i == 0)
        def _():
            carry[...] = jnp.zeros_like(carry)

        n1 = _rms_fwd(x_ref[...], g_ref[...]).astype(BF)
        n1_ref[...] = n1
        z = _dot_nt(n1, wf_ref[...]) + b_ref[...]
        z_ref[...] = z
        lane = lax.broadcasted_iota(jnp.int32, z.shape, 1)
        lf = jnp.where(lane < N_HEADS, jnp.minimum(z, 0.0) - jnp.log(1.0 + jnp.exp(-jnp.abs(z))), 0.0)
        row = lax.broadcasted_iota(jnp.int32, (tm, tm), 0)
        col = lax.broadcasted_iota(jnp.int32, (tm, tm), 1)
        tri = (row >= col).astype(BF)
        fc = _dot_exact(tri, lf) + carry[0:1, :]
        carry[...] = jnp.broadcast_to(fc[tm - 1:tm, :], carry.shape)
        c, s = cos_ref[...], sin_ref[...]

        def section(n):
            p = _dot_nt(n1, wm_ref[n * GROUP_W:(n + 1) * GROUP_W, :])
            proj_ref[:, n * GROUP_W:(n + 1) * GROUP_W] = p.astype(BF)
            return p

        def rotate(p, out_ref, scale):
            for sl in _chunks(GROUP_W):
                out_ref[:, sl] = ((p[:, sl] * c + _swap32(p[:, sl]) * s) * scale).astype(BF)

        def norm_aug(p, gain, out_ref, scale, is_query):
            for j, sl in enumerate(_chunks(GROUP_W)):
                f = p[:, sl]
                f = f * lax.rsqrt(_group_mean64(f * f) + EPS) * gain * scale
                out_ref[:, 2 * j * LANES:2 * (j + 1) * LANES] = _aug_pair(f, [fc[:, 2 * j:2 * j + 1], fc[:, 2 * j + 1:2 * j + 2]], is_query)

        p_rq, p_rk = section(0), section(1)
        rotate(p_rq, rq_ref, 0.125)
        section(2)
        rotate(p_rk, rk_ref, 1.0)
        section(3)
        p_fq = section(4)
        p_fk = section(5)
        norm_aug(p_fq, gq_ref[...], qa_ref, 0.125 * LOG2E, True)
        section(6)
        norm_aug(p_fk, gk_ref[...], ka_ref, 1.0, False)

    row_spec = lambda w: pl.BlockSpec((tm, w), lambda i: (i, 0))
    full = lambda a: pl.BlockSpec(a.shape, lambda i: (0,) * a.ndim)
    return pl.pallas_call(
        body, name="in_proj_fwd", grid=(n_t,),
        out_shape=(jax.ShapeDtypeStruct((t_len, D_MODEL), BF), jax.ShapeDtypeStruct((t_len, MAIN_W), BF),
                   jax.ShapeDtypeStruct((t_len, GROUP_W), BF), jax.ShapeDtypeStruct((t_len, GROUP_W), BF),
                   jax.ShapeDtypeStruct((t_len, 2 * GROUP_W), BF), jax.ShapeDtypeStruct((t_len, 2 * GROUP_W), BF),
                   jax.ShapeDtypeStruct((t_len, LANES), F32)),
        in_specs=[row_spec(D_MODEL), full(g_mix), *_w_in_specs(), full(b_pad), row_spec(LANES), row_spec(LANES),
                  full(gq_t), full(gk_t)],
        out_specs=(row_spec(D_MODEL), row_spec(MAIN_W), row_spec(GROUP_W), row_spec(GROUP_W), row_spec(2 * GROUP_W),
                   row_spec(2 * GROUP_W), row_spec(LANES)),
        scratch_shapes=[pltpu.VMEM((8, LANES), F32)],
        compiler_params=_cparams(("arbitrary",)),
    )(x, g_mix, w_in_t, w_in_t, b_pad, cos_t, sin_t, gq_t, gk_t)


def _w_in_specs():
    return (pl.BlockSpec((MAIN_W, D_MODEL), lambda i: (0, 0)), pl.BlockSpec((LANES, D_MODEL), lambda i: (MAIN_W // LANES, 0)))


def _decay_tables(c):
    h = np.arange(N_HEADS, dtype=np.float64)
    lg = np.log(1.0 - 2.0 ** (-5.0 - h)).astype(np.float32).astype(np.float64)
    t = np.arange(c)
    same_or_earlier = (t[None, :] // REF_CHUNK) <= (t[:, None] // REF_CHUNK)
    w = np.where(same_or_earlier[None], np.exp(lg[:, None, None] * np.abs(t[:, None] - t[None, :])[None]), 0.0)
    qd = np.exp(lg[:, None] * (t[None, :] + 1.0))
    kd = np.exp(lg[:, None] * (c - 1.0 - t[None, :]))
    cd = np.exp(lg * c)
    ones = np.ones((1, 1, HEAD_DIM))
    return (jnp.asarray(w, F32), jnp.asarray(qd[:, :, None] * ones, F32), jnp.asarray(kd[:, :, None] * ones, F32),
            jnp.asarray(cd[:, None, None] * np.ones((1, HEAD_DIM, HEAD_DIM)), F32))


def _retention_fwd(rq, rk, proj, g_ret, tables):
    t_len = rq.shape[0]
    c = min(RET_BLOCK, t_len)
    n_b = t_len // c
    wdec, qdec, kdec, cdec = tables
    v_col, g_col = 2 * GROUP_W // LANES, 3 * GROUP_W // LANES

    def body(q_ref, k_ref, v_ref, rg_ref, g_ref, w_ref, qd_ref, kd_ref, cd_ref, raw_ref, mix_ref, st_ref, state):
        i = pl.program_id(1)

        @pl.when(i == 0)
        def _():
            state[...] = jnp.zeros_like(state)

        q2, k2, v2 = q_ref[...], k_ref[...], v_ref[...]
        heads = [tuple(t[:, hh * HEAD_DIM:(hh + 1) * HEAD_DIM] for t in (q2, k2, v2)) for hh in range(2)]
        scores = [(_dot_nt(q, k) * w_ref[hh]).astype(BF) for hh, (q, k, _) in enumerate(heads)]
        outs = []
        for hh, (q, k, v) in enumerate(heads):
            sp = state[hh]
            st_ref[0, 0, hh] = sp
            outs.append(jnp.dot(scores[hh], v, preferred_element_type=F32) + _dot(q.astype(F32) * qd_ref[hh], sp))
            state[hh] = sp * cd_ref[hh] + _dot_tn(k.astype(F32) * kd_ref[hh], v)
        o2 = jnp.concatenate(outs, axis=-1)
        raw_ref[...] = o2
        xc = o2 - _group_mean64(o2)
        xh = xc * lax.rsqrt(_group_mean64(xc * xc) + EPS)
        gate = rg_ref[...].astype(F32)
        mix_ref[...] = (gate * _sigmoid(gate) * (xh * g_ref[0])).astype(BF)

    blk = lambda col0: pl.BlockSpec((c, LANES), lambda hp, i: (i, col0 + hp))
    tab = lambda a: pl.BlockSpec((2,) + a.shape[1:], lambda hp, i: (hp, 0, 0))
    return pl.pallas_call(
        body, name="retention_fwd", grid=(N_HEADS // 2, n_b),
        out_shape=(jax.ShapeDtypeStruct((t_len, GROUP_W), F32), jax.ShapeDtypeStruct((t_len, GROUP_W), BF),
                   jax.ShapeDtypeStruct((N_HEADS // 2, n_b, 2, HEAD_DIM, HEAD_DIM), F32)),
        in_specs=[blk(0), blk(0), blk(v_col), blk(g_col), pl.BlockSpec((1, 1, LANES), lambda hp, i: (hp, 0, 0)),
                  tab(wdec), tab(qdec), tab(kdec), tab(cdec)],
        out_specs=(blk(0), blk(0), pl.BlockSpec((1, 1, 2, HEAD_DIM, HEAD_DIM), lambda hp, i: (hp, i, 0, 0, 0))),
        scratch_shapes=[pltpu.VMEM((2, HEAD_DIM, HEAD_DIM), F32)],
        compiler_params=_cparams(("arbitrary", "arbitrary")),
    )(rq, rk, proj, proj, g_ret, wdec, qdec, kdec, cdec)


def _fox_fwd(q_aug, k_aug, proj, shards):
    t_len = q_aug.shape[0]
    tq = min(ATT_BLOCK, t_len)
    nsub = min(FWD_GROUP, t_len // tq)
    tg = nsub * tq
    n_q = t_len // tg
    v_col = 6 * GROUP_W // LANES
    tc = min(512, t_len)
    n_w = len(shards)
    n_steps = (N_HEADS // 2) * n_q

    def body(*refs):
        q_ref, k_ref, v_ref = refs[:3]
        o_ref, o32_ref, lse_ref = refs[3 + n_w:6 + n_w]
        vt = refs[6 + 2 * n_w]
        comm = (refs[3:3 + n_w], refs[6 + n_w:6 + 2 * n_w]) + tuple(refs[7 + 2 * n_w:])
        i = pl.program_id(1)
        step = pl.program_id(0) * n_q + i

        @pl.when(step == 0)
        def _():
            _gather_phase(0, *comm)

        @pl.when(step == (3 * n_steps) // 4)
        def _():
            _gather_phase(1, *comm)

        @pl.when(i == 0)
        def _():
            for c0 in range(0, t_len, tc):
                vt[:, c0:c0 + tc] = v_ref[c0:c0 + tc, :].T

        chains = [(u, hh) for u in range(nsub) for hh in range(2)]
        qs = {(u, hh): q_ref[u * tq:(u + 1) * tq, hh * LANES:(hh + 1) * LANES] for u, hh in chains}
        ones = jnp.ones((HEAD_DIM, tq), BF)

        def scores(j, which):
            k2 = k_ref[pl.ds(pl.multiple_of(j * tq, tq), tq), :]
            return {ch: _dot_nt(k2[:, ch[1] * LANES:(ch[1] + 1) * LANES], qs[ch]) for ch in which}

        def update(j, ss, carry, masked):
            v2 = vt[:, pl.ds(pl.multiple_of(j * tq, tq), tq)]
            ps, stats = {}, {}
            for ch in ss:
                m = carry[ch][0]
                s_t = ss[ch]
                if ch in masked:
                    krow = lax.broadcasted_iota(jnp.int32, (tq, tq), 0)
                    qcol = lax.broadcasted_iota(jnp.int32, (tq, tq), 1)
                    s_t = jnp.where(qcol >= krow, s_t, NEG)
                m_new = jnp.maximum(m, jnp.max(s_t, axis=0, keepdims=True))
                ps[ch] = jnp.exp2(s_t - m_new).astype(BF)
                stats[ch] = (m_new, jnp.exp2(m - m_new))
            out = dict(carry)
            for ch in ss:
                m_new, alpha = stats[ch]
                v_aug = jnp.concatenate([v2[ch[1] * HEAD_DIM:(ch[1] + 1) * HEAD_DIM, :], ones], axis=0)
                out[ch] = (m_new, carry[ch][1] * alpha + jnp.dot(v_aug, ps[ch], preferred_element_type=F32))
            return out

        def advance(j, state):
            ss, carry = state
            return scores(j + 1, chains), update(j, ss, carry, ())

        init = {ch: (jnp.full((1, tq), NEG, F32), jnp.zeros((LANES, tq), F32)) for ch in chains}
        first = nsub * i
        ss, carry = lax.fori_loop(0, first, advance, (scores(0, chains), init))
        carry = update(first, ss, carry, [(0, 0), (0, 1)])
        for u in range(1, nsub):
            rest = [(uu, hh) for uu in range(u, nsub) for hh in range(2)]
            carry = update(first + u, scores(first + u, rest), carry, [(u, 0), (u, 1)])
        for u in range(nsub):
            outs, lses = [], []
            for hh in range(2):
                m, acc = carry[u, hh]
                l = acc[HEAD_DIM:HEAD_DIM + 1, :]
                outs.append(acc[:HEAD_DIM, :] / l)
                lses.append(m + jnp.log2(l))
            o2 = jnp.concatenate(outs, axis=0).T
            o32_ref[u * tq:(u + 1) * tq, :] = o2
            o_ref[u * tq:(u + 1) * tq, :] = o2.astype(BF)
            lse_ref[0, :, u * tq:(u + 1) * tq] = jnp.concatenate(lses, axis=0)

        @pl.when(step == n_steps - 1)
        def _():
            _gather_phase(2, *comm)

    return pl.pallas_call(
        body, name="fox_fwd", grid=(N_HEADS // 2, n_q),
        out_shape=(jax.ShapeDtypeStruct((t_len, GROUP_W), BF), jax.ShapeDtypeStruct((t_len, GROUP_W), F32),
                   jax.ShapeDtypeStruct((N_HEADS // 2, 2, t_len), F32))
        + tuple(jax.ShapeDtypeStruct((4,) + s.shape, s.dtype) for s in shards),
        in_specs=[pl.BlockSpec((tg, 2 * LANES), lambda hp, i: (i, hp)),
                  pl.BlockSpec((t_len, 2 * LANES), lambda hp, i: (0, hp)),
                  pl.BlockSpec((t_len, LANES), lambda hp, i: (0, v_col + hp))] + [ANY] * n_w,
        out_specs=(pl.BlockSpec((tg, LANES), lambda hp, i: (i, hp)), pl.BlockSpec((tg, LANES), lambda hp, i: (i, hp)),
                   pl.BlockSpec((1, 2, tg), lambda hp, i: (hp, 0, i))) + (ANY,) * n_w,
        scratch_shapes=[pltpu.VMEM((LANES, t_len), BF)] + _gather_scratch(n_w),
        compiler_params=_cparams(("arbitrary", "arbitrary")),
    )(q_aug, k_aug, proj, *shards)


def _softmax_rows(s):
    p = jnp.exp(s - jnp.max(s, axis=-1, keepdims=True))
    return p / jnp.sum(p, axis=-1, keepdims=True)


def _attn_out_xattn_fwd(x, mix_r, mix_f, w_out, g_xattn, w_xq, g_xq, kn, v, w_xo):
    t_len = x.shape[0]
    tm = min(ROW_TILE, t_len)

    def body(x_ref, mr_ref, mf_ref, wo_ref, g_ref, wq_ref, gq_ref, kn_ref, v_ref, wxo_ref,
             h1_ref, hn_ref, qx_ref, o_ref, h2_ref):
        h1 = x_ref[...] + jnp.dot(mr_ref[...], wo_ref[:GROUP_W, :], preferred_element_type=F32) \
            + jnp.dot(mf_ref[...], wo_ref[GROUP_W:, :], preferred_element_type=F32)
        h1_ref[...] = h1
        hn = _rms_fwd(h1, g_ref[...]).astype(BF)
        hn_ref[...] = hn
        qx = jnp.dot(hn, wq_ref[...], preferred_element_type=F32).astype(BF)
        qx_ref[...] = qx
        sls = [slice(h * XHD, (h + 1) * XHD) for h in range(N_XH)]
        qns = [_rms_fwd(qx[:, sl].astype(F32), gq_ref[...]).astype(BF) for sl in sls]
        logits = [_dot_nt(qn, kn_ref[:, sl]) * (XHD ** -0.5) for qn, sl in zip(qns, sls)]
        ps = [_softmax_rows(s).astype(BF) for s in logits]
        for p, sl in zip(ps, sls):
            o_ref[:, sl] = jnp.dot(p, v_ref[:, sl], preferred_element_type=F32).astype(BF)
        h2_ref[...] = h1 + jnp.dot(o_ref[...], wxo_ref[...], preferred_element_type=F32)

    row_spec = lambda w: pl.BlockSpec((tm, w), lambda i: (i, 0))
    full = lambda a: pl.BlockSpec(a.shape, lambda i: (0,) * a.ndim)
    return pl.pallas_call(
        body, name="attn_out_xattn_fwd", grid=(t_len // tm,),
        out_shape=(jax.ShapeDtypeStruct((t_len, D_MODEL), F32), jax.ShapeDtypeStruct((t_len, D_MODEL), BF),
                   jax.ShapeDtypeStruct((t_len, D_MODEL), BF), jax.ShapeDtypeStruct((t_len, D_MODEL), BF),
                   jax.ShapeDtypeStruct((t_len, D_MODEL), F32)),
        in_specs=[row_spec(D_MODEL), row_spec(GROUP_W), row_spec(GROUP_W), full(w_out), full(g_xattn), full(w_xq), full(g_xq),
                  full(kn), full(v), full(w_xo)],
        out_specs=(row_spec(D_MODEL),) * 5,
        compiler_params=_cparams(("arbitrary",)),
    )(x, mix_r, mix_f, w_out, g_xattn, w_xq, g_xq, kn, v, w_xo)


def _ffn_loss_fwd(h2, g_ffn, w_gate, w_up, w_down, target):
    t_len = h2.shape[0]
    tm = min(ROW_TILE, t_len)

    def body(h2_ref, g_ref, wg_ref, wu_ref, wd_ref, tgt_ref, hn_ref, gate_ref, up_ref, act_ref, dh3_ref, loss_ref):
        @pl.when(pl.program_id(0) == 0)
        def _():
            loss_ref[...] = jnp.zeros_like(loss_ref)

        h2v = h2_ref[...]
        hn = _rms_fwd(h2v, g_ref[...]).astype(BF)
        hn_ref[...] = hn
        gate = _dot_nt(hn, wg_ref[...])
        up = _dot_nt(hn, wu_ref[...])
        gate_ref[...] = gate.astype(BF)
        up_ref[...] = up.astype(BF)
        act = (gate * _sigmoid(gate) * up).astype(BF)
        act_ref[...] = act
        diff = h2v + jnp.dot(act, wd_ref[...], preferred_element_type=F32) - tgt_ref[...]
        dh3_ref[...] = diff * (1.0 / D_MODEL)
        per_row = jnp.sum(diff * diff, axis=-1, keepdims=True) * (1.0 / D_MODEL)
        loss_ref[...] += 0.5 * jnp.sum(per_row, axis=0, keepdims=True)

    row_spec = lambda w: pl.BlockSpec((tm, w), lambda i: (i, 0))
    full = lambda a: pl.BlockSpec(a.shape, lambda i: (0,) * a.ndim, pipeline_mode=pl.Buffered(1))
    return pl.pallas_call(
        body, name="ffn_loss_fwd", grid=(t_len // tm,),
        out_shape=(jax.ShapeDtypeStruct((t_len, D_MODEL), BF), jax.ShapeDtypeStruct((t_len, D_FF), BF),
                   jax.ShapeDtypeStruct((t_len, D_FF), BF), jax.ShapeDtypeStruct((t_len, D_FF), BF),
                   jax.ShapeDtypeStruct((t_len, D_MODEL), F32), jax.ShapeDtypeStruct((8, LANES), F32)),
        in_specs=[row_spec(D_MODEL), full(g_ffn), full(w_gate), full(w_up), full(w_down), row_spec(D_MODEL)],
        out_specs=(row_spec(D_MODEL), row_spec(D_FF), row_spec(D_FF), row_spec(D_FF), row_spec(D_MODEL),
                   pl.BlockSpec((8, LANES), lambda i: (0, 0))),
        compiler_params=_cparams(("arbitrary",)),
    )(h2, g_ffn, w_gate, w_up, w_down, target)


def _ffn_bwd(dh3, gate, up, h2, g_ffn, w_gate, w_up, w_down):
    t_len = h2.shape[0]
    tm = min(FFN_BWD_TILE, t_len)

    def body(dh3_ref, gate_ref, up_ref, h2_ref, g_ref, wg_ref, wu_ref, wd_ref, dgate_ref, dup_ref, dh2_ref, dg_ref):
        @pl.when(pl.program_id(0) == 0)
        def _():
            dg_ref[...] = jnp.zeros_like(dg_ref)

        dh3v = dh3_ref[...]
        dact = _dot_nt(dh3v, wd_ref[...])
        g = gate_ref[...].astype(F32)
        sg = _sigmoid(g)
        dup = (dact * (g * sg)).astype(BF)
        dgate = (dact * up_ref[...].astype(F32) * (sg * (1.0 + g * (1.0 - sg)))).astype(BF)
        dup_ref[...] = dup
        dgate_ref[...] = dgate
        dhn = jnp.dot(dgate, wg_ref[...], preferred_element_type=F32) + jnp.dot(dup, wu_ref[...], preferred_element_type=F32)
        dx, dg = _rms_bwd(h2_ref[...], g_ref[...], dhn)
        dh2_ref[...] = dh3v + dx
        dg_ref[...] += dg

    row_spec = lambda w: pl.BlockSpec((tm, w), lambda i: (i, 0))
    full = lambda a: pl.BlockSpec(a.shape, lambda i: (0,) * a.ndim, pipeline_mode=pl.Buffered(1))
    return pl.pallas_call(
        body, name="ffn_bwd", grid=(t_len // tm,),
        out_shape=(jax.ShapeDtypeStruct((t_len, D_FF), BF), jax.ShapeDtypeStruct((t_len, D_FF), BF),
                   jax.ShapeDtypeStruct((t_len, D_MODEL), F32), jax.ShapeDtypeStruct((1, D_MODEL), F32)),
        in_specs=[row_spec(D_MODEL), row_spec(D_FF), row_spec(D_FF), row_spec(D_MODEL), full(g_ffn), full(w_gate), full(w_up),
                  full(w_down)],
        out_specs=(row_spec(D_FF), row_spec(D_FF), row_spec(D_MODEL), pl.BlockSpec((1, D_MODEL), lambda i: (0, 0))),
        compiler_params=_cparams(("arbitrary",)),
    )(dh3, gate, up, h2, g_ffn, w_gate, w_up, w_down)


def _attn_out_xattn_bwd(dh2, h1, qx, kn, v, w_xo, w_xq, w_out, g_xattn, g_xq):
    t_len = h1.shape[0]
    tm = min(ROW_TILE, t_len)
    m_tok = kn.shape[0]

    def body(dh2_ref, h1_ref, qx_ref, kn_ref, v_ref, wxo_ref, wq_ref, wo_ref, g_ref, gq_ref,
             dqx_ref, dh1_ref, dmr_ref, dmf_ref, dkn_ref, dv_ref, dg_ref, dgq_ref, dqx_scr):
        @pl.when(pl.program_id(0) == 0)
        def _():
            dkn_ref[...] = jnp.zeros_like(dkn_ref)
            dv_ref[...] = jnp.zeros_like(dv_ref)
            dg_ref[...] = jnp.zeros_like(dg_ref)
            dgq_ref[...] = jnp.zeros_like(dgq_ref)

        dh2v = dh2_ref[...]
        do = _dot_nt(dh2v, wxo_ref[...])
        gq = gq_ref[...]
        sls = [slice(h * XHD, (h + 1) * XHD) for h in range(N_XH)]
        qraws = [qx_ref[:, sl].astype(F32) for sl in sls]
        qns = [_rms_fwd(qraw, gq).astype(BF) for qraw in qraws]
        dohs = [do[:, sl].astype(BF) for sl in sls]
        logits = [_dot_nt(qn, kn_ref[:, sl]) * (XHD ** -0.5) for qn, sl in zip(qns, sls)]
        dps = [_dot_nt(doh, v_ref[:, sl]) for doh, sl in zip(dohs, sls)]
        ps = [_softmax_rows(s) for s in logits]
        dss = [(p * (dp - jnp.sum(dp * p, axis=-1, keepdims=True)) * (XHD ** -0.5)).astype(BF) for p, dp in zip(ps, dps)]
        dqns = []
        for h, sl in enumerate(sls):
            dv_ref[:, sl] += _dot_tn(ps[h], dohs[h])
            dqns.append(jnp.dot(dss[h], kn_ref[:, sl], preferred_element_type=F32))
            dkn_ref[:, sl] += _dot_tn(dss[h], qns[h])
        dgq = jnp.zeros((1, XHD), F32)
        for h, sl in enumerate(sls):
            dx, dg_h = _rms_bwd(qraws[h], gq, dqns[h])
            dgq = dgq + dg_h
            dqx_scr[:, sl] = dx.astype(BF)
        dgq_ref[...] += dgq
        dqx = dqx_scr[...]
        dqx_ref[...] = dqx
        dhn = _dot_nt(dqx, wq_ref[...])
        dx, dg = _rms_bwd(h1_ref[...], g_ref[...], dhn)
        dg_ref[...] += dg
        dh1 = dh2v + dx
        dh1_ref[...] = dh1
        dmix = _dot_nt(dh1, wo_ref[...])
        dmr_ref[...] = dmix[:, :GROUP_W]
        dmf_ref[...] = dmix[:, GROUP_W:].astype(BF)

    row_spec = lambda w: pl.BlockSpec((tm, w), lambda i: (i, 0))
    full = lambda a: pl.BlockSpec(a.shape, lambda i: (0,) * a.ndim)
    acc = lambda r, c: pl.BlockSpec((r, c), lambda i: (0, 0))
    return pl.pallas_call(
        body, name="attn_out_xattn_bwd", grid=(t_len // tm,),
        out_shape=(jax.ShapeDtypeStruct((t_len, D_MODEL), BF), jax.ShapeDtypeStruct((t_len, D_MODEL), F32),
                   jax.ShapeDtypeStruct((t_len, GROUP_W), F32), jax.ShapeDtypeStruct((t_len, GROUP_W), BF),
                   jax.ShapeDtypeStruct((m_tok, D_MODEL), F32), jax.ShapeDtypeStruct((m_tok, D_MODEL), F32),
                   jax.ShapeDtypeStruct((1, D_MODEL), F32), jax.ShapeDtypeStruct((1, XHD), F32)),
        in_specs=[row_spec(D_MODEL), row_spec(D_MODEL), row_spec(D_MODEL), full(kn), full(v), full(w_xo), full(w_xq), full(w_out),
                  full(g_xattn), full(g_xq)],
        out_specs=(row_spec(D_MODEL), row_spec(D_MODEL), row_spec(GROUP_W), row_spec(GROUP_W), acc(m_tok, D_MODEL),
                   acc(m_tok, D_MODEL), acc(1, D_MODEL), acc(1, XHD)),
        scratch_shapes=[pltpu.VMEM((tm, D_MODEL), BF)],
        compiler_params=_cparams(("arbitrary",)),
    )(dh2, h1, qx, kn, v, w_xo, w_xq, w_out, g_xattn, g_xq)


def _mem_kv_bwd(dkn, dv, kraw, mem, memn, g_mem, g_xk, w_xkv):
    m_tok = mem.shape[0]

    def body(dkn_ref, dv_ref, kraw_ref, mem_ref, memn_ref, gm_ref, gk_ref, w_ref, dw_ref, dgm_ref, dgk_ref, dkv_scr):
        gk = gk_ref[...]
        dgk = jnp.zeros((1, XHD), F32)
        for h in range(N_XH):
            sl = slice(h * XHD, (h + 1) * XHD)
            dx, dg_h = _rms_bwd(kraw_ref[:, sl], gk, dkn_ref[:, sl])
            dgk = dgk + dg_h
            dkv_scr[:, sl] = dx.astype(BF)
        dgk_ref[...] = dgk
        dkv_scr[:, D_MODEL:] = dv_ref[...].astype(BF)
        dkv = dkv_scr[...]
        dw_ref[...] = _dot_tn(memn_ref[...], dkv)
        dmemn = _dot_nt(dkv, w_ref[...])
        mem_v = mem_ref[...]
        r = lax.rsqrt(jnp.mean(mem_v * mem_v, axis=-1, keepdims=True) + EPS)
        dgm_ref[...] = jnp.sum(dmemn * mem_v * r, axis=0, keepdims=True)

    return pl.pallas_call(
        body, name="mem_kv_bwd",
        out_shape=(jax.ShapeDtypeStruct((D_MODEL, 2 * D_MODEL), F32), jax.ShapeDtypeStruct((1, D_MODEL), F32),
                   jax.ShapeDtypeStruct((1, XHD), F32)),
        in_specs=[VMEM_SPEC] * 8, out_specs=(VMEM_SPEC,) * 3,
        scratch_shapes=[pltpu.VMEM((m_tok, 2 * D_MODEL), BF)],
        compiler_params=_cparams(),
    )(dkn, dv, kraw, mem, memn, g_mem, g_xk, w_xkv)


def _fox_bwd(q_aug, k_aug, proj, dmf, o32, lse, sums):
    t_len = q_aug.shape[0]
    tb = min(ATT_BLOCK, t_len)
    n_b = t_len // tb
    nsub = 2 if n_b >= 2 else 1
    tg = nsub * tb
    n_g = t_len // tg
    v_col = 6 * GROUP_W // LANES
    n_w = len(sums)
    n_steps = (N_HEADS // 2) * n_g

    def body(*refs):
        k_ref, v_ref, q_ref, do_ref, o_ref, lse_ref = refs[:6]
        dq_ref, dk_ref, dv_ref, df_ref = refs[6 + n_w:10 + n_w]
        delta = refs[10 + 2 * n_w]
        comm = (refs[6:6 + n_w], refs[10 + n_w:10 + 2 * n_w]) + tuple(refs[11 + 2 * n_w:])
        j = pl.program_id(1)
        step = pl.program_id(0) * n_g + j

        @pl.when(step == 0)
        def _():
            _scatter_phase(0, *comm)

        @pl.when(j == 0)
        def _():
            dq_ref[...] = jnp.zeros_like(dq_ref)
            dd = do_ref[...].astype(F32) * o_ref[...]
            hrow = lax.broadcasted_iota(jnp.int32, (8, LANES), 0)
            lane = lax.broadcasted_iota(jnp.int32, (8, LANES), 1)
            ind = ((lane // HEAD_DIM) == hrow).astype(BF)
            delta[...] = _dot_nt_exact(ind, dd)

        k2, v2 = k_ref[...], v_ref[...]
        chains = [(u, hh) for u in range(nsub) for hh in range(2)]
        ks = {(u, hh): k2[u * tb:(u + 1) * tb, hh * LANES:(hh + 1) * LANES] for u, hh in chains}
        vs = {(u, hh): v2[u * tb:(u + 1) * tb, hh * HEAD_DIM:(hh + 1) * HEAD_DIM] for u, hh in chains}

        def block(i, carry, which, masked):
            rows = pl.ds(pl.multiple_of(i * tb, tb), tb)
            q2 = q_ref[rows, :]
            do2 = do_ref[rows, :]
            qs = [q2[:, hh * LANES:(hh + 1) * LANES] for hh in range(2)]
            dos = [do2[:, hh * HEAD_DIM:(hh + 1) * HEAD_DIM] for hh in range(2)]
            ss = {ch: _dot_nt(ks[ch], qs[ch[1]]) for ch in which}
            dps = {ch: _dot_nt(vs[ch], dos[ch[1]]) for ch in which}
            pts, dsts, dfs = {}, {}, {}
            for ch in which:
                hh = ch[1]
                s_t = ss[ch]
                if ch in masked:
                    krow = lax.broadcasted_iota(jnp.int32, (tb, tb), 0)
                    qcol = lax.broadcasted_iota(jnp.int32, (tb, tb), 1)
                    s_t = jnp.where(qcol >= krow, s_t, NEG)
                p_t = jnp.exp2(s_t - lse_ref[0, hh:hh + 1, rows])
                pts[ch] = p_t.astype(BF)
                ds_t = p_t * (dps[ch] - delta[hh:hh + 1, rows])
                dsts[ch] = ds_t.astype(BF)
                dfs[ch] = jnp.sum(ds_t, axis=-1, keepdims=True)
            out = dict(carry)
            for ch in which:
                dk, dv, df = carry[ch]
                dv = dv + jnp.dot(pts[ch], dos[ch[1]], preferred_element_type=F32)
                dk = dk + jnp.dot(dsts[ch], qs[ch[1]], preferred_element_type=F32)
                out[ch] = (dk, dv, df - dfs[ch])
            for hh in range(2):
                parts_dq = [_dot_tn(dsts[ch], ks[ch])[:, :HEAD_DIM] for ch in which if ch[1] == hh]
                dq_ref[rows, hh * HEAD_DIM:(hh + 1) * HEAD_DIM] += sum(parts_dq[1:], parts_dq[0])
            return out

        init = {ch: (jnp.zeros((tb, LANES), F32), jnp.zeros((tb, HEAD_DIM), F32), jnp.zeros((tb, 1), F32)) for ch in chains}
        first = nsub * j
        carry = block(first, init, [(0, 0), (0, 1)], [(0, 0), (0, 1)])
        if nsub == 2:
            carry = block(first + 1, carry, chains, [(1, 0), (1, 1)])
        carry = lax.fori_loop(first + nsub, n_b, lambda i, c: block(i, c, chains, ()), carry)
        for u in range(nsub):
            rs = slice(u * tb, (u + 1) * tb)
            dk_ref[rs, :] = jnp.concatenate([carry[u, hh][0][:, :HEAD_DIM] for hh in range(2)], axis=-1) * LN2
            dv_ref[rs, :] = jnp.concatenate([carry[u, hh][1] for hh in range(2)], axis=-1)
            df_ref[0, rs, :] = jnp.concatenate([carry[u, hh][2] for hh in range(2)], axis=-1)

        @pl.when(step == n_steps - 1)
        def _():
            _scatter_phase(1, *comm)

    blk = lambda w, col0: pl.BlockSpec((tg, w), lambda hp, j: (j, col0 + hp))
    whole = lambda w: pl.BlockSpec((t_len, w), lambda hp, j: (0, hp))
    rows2 = pl.BlockSpec((1, 2, t_len), lambda hp, j: (hp, 0, 0))
    cols2 = pl.BlockSpec((1, tg, 2), lambda hp, j: (hp, j, 0))
    return pl.pallas_call(
        body, name="fox_bwd", grid=(N_HEADS // 2, n_g),
        out_shape=(jax.ShapeDtypeStruct((t_len, GROUP_W), F32), jax.ShapeDtypeStruct((t_len, GROUP_W), F32),
                   jax.ShapeDtypeStruct((t_len, GROUP_W), F32), jax.ShapeDtypeStruct((N_HEADS // 2, t_len, 2), F32))
        + _scatter_out_shapes(sums),
        in_specs=[blk(2 * LANES, 0), blk(LANES, v_col), whole(2 * LANES), whole(LANES), whole(LANES), rows2] + [ANY] * n_w,
        out_specs=(whole(LANES), blk(LANES, 0), blk(LANES, 0), cols2) + (ANY,) * n_w,
        scratch_shapes=[pltpu.VMEM((8, t_len), F32)] + _scatter_scratch(n_w),
        compiler_params=_cparams(("arbitrary", "arbitrary")),
    )(k_aug, proj, q_aug, dmf, o32, lse, *sums)


def _retention_bwd(dmr, raw, proj, g_ret, rq, rk, states, tables, parts):
    t_len = rq.shape[0]
    c = min(RET_BLOCK, t_len)
    n_b = t_len // c
    wdec, qdec, kdec, cdec = tables
    v_col, g_col = 2 * GROUP_W // LANES, 3 * GROUP_W // LANES
    n_w = len(parts)
    n_steps = (N_HEADS // 2) * n_b

    def body(*refs):
        d_ref, raw_ref, rg_ref, g_ref, q_ref, k_ref, v_ref, st_ref, w_ref, wt_ref, qd_ref, kd_ref, cd_ref = refs[:13]
        dq_ref, dk_ref, dv_ref, drg_ref, dg_ref = refs[13 + n_w:18 + n_w]
        gstate = refs[18 + 2 * n_w]
        comm = (refs[13:13 + n_w], refs[18 + n_w:18 + 2 * n_w]) + tuple(refs[19 + 2 * n_w:])
        step = pl.program_id(0) * n_b + pl.program_id(1)

        @pl.when(step == 0)
        def _():
            _exchange_phase(0, *comm)

        @pl.when(pl.program_id(1) == 0)
        def _():
            gstate[...] = jnp.zeros_like(gstate)
            dg_ref[...] = jnp.zeros_like(dg_ref)

        d, raw_v, g = d_ref[...], raw_ref[...], g_ref[0]
        gate = rg_ref[...].astype(F32)
        xc = raw_v - _group_mean64(raw_v)
        r = lax.rsqrt(_group_mean64(xc * xc) + EPS)
        xh = xc * r
        sg = _sigmoid(gate)
        drg_ref[...] = d * (xh * g) * (sg * (1.0 + gate * (1.0 - sg)))
        dy = d * (gate * sg)
        dg_ref[0] += jnp.sum(dy * xh, axis=0, keepdims=True)
        dxh = dy * g
        do2 = r * (dxh - _group_mean64(dxh) - xh * _group_mean64(dxh * xh))
        q2, k2, v2 = q_ref[...], k_ref[...], v_ref[...]
        dqs, dks, dvs = [], [], []
        heads = [tuple(t[:, hh * HEAD_DIM:(hh + 1) * HEAD_DIM] for t in (q2, k2, v2, do2.astype(BF))) for hh in range(2)]
        firsts = [(_dot_nt(k, q) * wt_ref[hh], _dot_nt(do, v) * w_ref[hh], _dot_nt(v, do) * wt_ref[hh])
                  for hh, (q, k, v, do) in enumerate(heads)]
        for hh, (q, k, v, do) in enumerate(heads):
            a_t, dm, dm_t = firsts[hh]
            sp, gs = st_ref[0, 0, hh], gstate[hh]
            qd = q.astype(F32) * qd_ref[hh]
            kd = k.astype(F32) * kd_ref[hh]
            dqs.append(_dot(dm, k) + _dot_nt(do, sp) * qd_ref[hh])
            dks.append(_dot(dm_t, q) + _dot_nt(v, gs) * kd_ref[hh])
            dvs.append(_dot(a_t, do) + _dot(kd, gs))
            gstate[hh] = gs * cd_ref[hh] + _dot_tn(qd, do)
        dq_ref[...] = jnp.concatenate(dqs, axis=-1)
        dk_ref[...] = jnp.concatenate(dks, axis=-1)
        dv_ref[...] = jnp.concatenate(dvs, axis=-1)

        @pl.when(step == n_steps - 1)
        def _():
            _exchange_phase(1, *comm)

    blk = lambda col0: pl.BlockSpec((c, LANES), lambda hp, i: (n_b - 1 - i, col0 + hp))
    tab = lambda a: pl.BlockSpec((2,) + a.shape[1:], lambda hp, i: (hp, 0, 0))
    gspec = pl.BlockSpec((1, 1, LANES), lambda hp, i: (hp, 0, 0))
    return pl.pallas_call(
        body, name="retention_bwd", grid=(N_HEADS // 2, n_b),
        out_shape=(jax.ShapeDtypeStruct((t_len, GROUP_W), F32),) * 4 + (jax.ShapeDtypeStruct((N_HEADS // 2, 1, LANES), F32),)
        + _exchange_out_shapes(parts),
        in_specs=[blk(0), blk(0), blk(g_col), gspec, blk(0), blk(0), blk(v_col),
                  pl.BlockSpec((1, 1, 2, HEAD_DIM, HEAD_DIM), lambda hp, i: (hp, n_b - 1 - i, 0, 0, 0)),
                  tab(wdec), tab(wdec), tab(qdec), tab(kdec), tab(cdec)] + [ANY] * n_w,
        out_specs=(blk(0), blk(0), blk(0), blk(0), gspec) + (ANY,) * n_w,
        scratch_shapes=[pltpu.VMEM((2, HEAD_DIM, HEAD_DIM), F32)] + _exchange_scratch(n_w),
        compiler_params=_cparams(("arbitrary", "arbitrary")),
    )(dmr, raw, proj, g_ret, rq, rk, proj, states, wdec, jnp.transpose(wdec, (0, 2, 1)), qdec, kdec, cdec, *parts)


def _in_proj_bwd(x, g_mix, dh1, dq_r, dk_r, dv_r, drg, dq_f, dk_f, dv_f, df_col, proj, z, cos_t, sin_t, gq_t, gk_t, w_in_t):
    t_len = x.shape[0]
    tm = min(ROW_TILE, t_len)
    n_t = t_len // tm

    def body(x_ref, g_ref, dh1_ref, dqr_ref, dkr_ref, dvr_ref, drg_ref, dqf_ref, dkf_ref, dvf_ref, df_ref, fq_ref, fk_ref, z_ref,
             cos_ref, sin_ref, gq_ref, gk_ref, wm_ref, wf_ref,
             dproj_ref, dz_ref, dx_ref, dg_ref, dgq_ref, dgk_ref, db_ref, carry, gq_acc, gk_acc):
        i = pl.program_id(0)

        @pl.when(i == 0)
        def _():
            carry[...] = jnp.zeros_like(carry)
            gq_acc[...] = jnp.zeros_like(gq_acc)
            gk_acc[...] = jnp.zeros_like(gk_acc)
            dg_ref[...] = jnp.zeros_like(dg_ref)
            db_ref[...] = jnp.zeros_like(db_ref)

        c, s = cos_ref[...], sin_ref[...]
        gq, gk = gq_ref[...], gk_ref[...]
        dgq = jnp.zeros((1, LANES), F32)
        dgk = jnp.zeros((1, LANES), F32)
        for sl in _chunks(GROUP_W):
            dy = dqr_ref[:, sl] * 0.125
            dproj_ref[:, sl] = (dy * c + _swap32(dy * s)).astype(BF)
            dy = dkr_ref[:, sl]
            dproj_ref[:, GROUP_W + sl.start:GROUP_W + sl.stop] = (dy * c + _swap32(dy * s)).astype(BF)
            dproj_ref[:, 2 * GROUP_W + sl.start:2 * GROUP_W + sl.stop] = dvr_ref[:, sl].astype(BF)
            dproj_ref[:, 3 * GROUP_W + sl.start:3 * GROUP_W + sl.stop] = drg_ref[:, sl].astype(BF)
            for src, dsrc, gain, off in ((fq_ref, dqf_ref, gq, 4), (fk_ref, dkf_ref, gk, 5)):
                xr = src[:, sl].astype(F32)
                r = lax.rsqrt(_group_mean64(xr * xr) + EPS)
                xh = xr * r
                dy = dsrc[:, sl] * (0.125 if off == 4 else 1.0)
                dgs = jnp.sum(dy * xh, axis=0, keepdims=True)
                if off == 4:
                    dgq = dgq + dgs
                else:
                    dgk = dgk + dgs
                dxh = dy * gain
                dproj_ref[:, off * GROUP_W + sl.start:off * GROUP_W + sl.stop] = \
                    (r * (dxh - xh * _group_mean64(dxh * xh))).astype(BF)
            dproj_ref[:, 6 * GROUP_W + sl.start:6 * GROUP_W + sl.stop] = dvf_ref[:, sl].astype(BF)
        gq_acc[...] += dgq
        gk_acc[...] += dgk
        row = lax.broadcasted_iota(jnp.int32, (tm, tm), 0)
        col = lax.broadcasted_iota(jnp.int32, (tm, tm), 1)
        dlf = _dot_exact((col >= row).astype(BF), df_ref[...]) + carry[0:1, :]
        carry[...] = jnp.broadcast_to(dlf[0:1, :], carry.shape)
        lane = lax.broadcasted_iota(jnp.int32, (tm, LANES), 1)
        dz = jnp.where(lane < N_HEADS, dlf / (1.0 + jnp.exp(z_ref[...])), 0.0)
        db_ref[...] += jnp.sum(dz, axis=0, keepdims=True)
        dz_bf = dz.astype(BF)
        dz_ref[...] = dz_bf
        dn1 = jnp.dot(dz_bf, wf_ref[...], preferred_element_type=F32)
        for sec in range(MAIN_W // GROUP_W):
            sl = slice(sec * GROUP_W, (sec + 1) * GROUP_W)
            dn1 = dn1 + jnp.dot(dproj_ref[:, sl], wm_ref[sl, :], preferred_element_type=F32)
        dx, dg = _rms_bwd(x_ref[...], g_ref[...], dn1)
        dx_ref[...] = dh1_ref[...] + dx
        dg_ref[...] += dg

        @pl.when(i == n_t - 1)
        def _():
            dgq_ref[...] = gq_acc[:, :HEAD_DIM] + gq_acc[:, HEAD_DIM:]
            dgk_ref[...] = gk_acc[:, :HEAD_DIM] + gk_acc[:, HEAD_DIM:]

    row_spec = lambda w, col=0: pl.BlockSpec((tm, w), lambda i: (n_t - 1 - i, col))
    full = lambda a: pl.BlockSpec(a.shape, lambda i: (0,) * a.ndim)
    acc = lambda r, c: pl.BlockSpec((r, c), lambda i: (0, 0))
    return pl.pallas_call(
        body, name="in_proj_bwd", grid=(n_t,),
        out_shape=(jax.ShapeDtypeStruct((t_len, MAIN_W), BF), jax.ShapeDtypeStruct((t_len, LANES), BF),
                   jax.ShapeDtypeStruct((t_len, D_MODEL), F32), jax.ShapeDtypeStruct((1, D_MODEL), F32),
                   jax.ShapeDtypeStruct((1, HEAD_DIM), F32), jax.ShapeDtypeStruct((1, HEAD_DIM), F32),
                   jax.ShapeDtypeStruct((1, LANES), F32)),
        in_specs=[row_spec(D_MODEL), full(g_mix), row_spec(D_MODEL)] + [row_spec(GROUP_W)] * 7
        + [row_spec(LANES), row_spec(GROUP_W, 4), row_spec(GROUP_W, 5), row_spec(LANES), row_spec(LANES), row_spec(LANES),
           full(gq_t), full(gk_t), *_w_in_specs()],
        out_specs=(row_spec(MAIN_W), row_spec(LANES), row_spec(D_MODEL), acc(1, D_MODEL), acc(1, HEAD_DIM), acc(1, HEAD_DIM),
                   acc(1, LANES)),
        scratch_shapes=[pltpu.VMEM((8, LANES), F32), pltpu.VMEM((1, LANES), F32), pltpu.VMEM((1, LANES), F32)],
        compiler_params=_cparams(("arbitrary",)),
    )(x, g_mix, dh1, dq_r, dk_r, dv_r, drg, dq_f, dk_f, dv_f, df_col, proj, proj, z, cos_t, sin_t, gq_t, gk_t, w_in_t, w_in_t)


def _matmul_tn(a, b, name, bk=1024):
    t_len, m = a.shape
    n = b.shape[1]
    bm = m if m <= TN_MAX_ROWS else m // 2
    bk = min(bk, t_len)

    def body(a_ref, b_ref, o_ref):
        @pl.when(pl.program_id(1) == 0)
        def _():
            o_ref[...] = jnp.zeros_like(o_ref)

        o_ref[...] += _dot_tn(a_ref[...], b_ref[...])

    return pl.pallas_call(
        body, name=name, grid=(m // bm, t_len // bk),
        out_shape=jax.ShapeDtypeStruct((m, n), F32),
        in_specs=[pl.BlockSpec((bk, bm), lambda i, k: (k, i)), pl.BlockSpec((bk, n), lambda i, k: (k, 0))],
        out_specs=pl.BlockSpec((bm, n), lambda i, k: (i, 0)),
        compiler_params=_cparams(("arbitrary", "arbitrary")),
    )(a, b)


def _place():
    x, y, c = lax.axis_index("x"), lax.axis_index("y"), lax.axis_index("c")
    chips = [(1 - x, y), (x, 1 - y), (1 - x, 1 - y)]
    return x, y, c, chips


def _row_chunks(rows, limit):
    step = max(d for d in range(16, min(rows, limit) + 1, 16) if rows % d == 0)
    return [slice(i, i + step) for i in range(0, rows, step)]


ICI_CHUNK_ROWS = 128
D2D_CHUNK_ROWS = 64


def _gather_phase(phase, ins, outs, send_sems, recv_sems):
    x, y, c, chips = _place()
    me_chip = 2 * x + y
    sibling = (x, y, 1 - c)

    def copy(w, k, slot, half, to, rows=slice(None), src=None):
        dst = outs[w].at[slot, half, rows]
        return pltpu.make_async_remote_copy(src_ref=dst if src is None else src, dst_ref=dst,
                                            send_sem=send_sems.at[w, k], recv_sem=recv_sems.at[w, k],
                                            device_id=to, device_id_type=MESH)

    for w in range(len(ins)):
        for j, (px, py) in enumerate(chips):
            if phase == 0:
                for rows in _row_chunks(ins[w].shape[1], ICI_CHUNK_ROWS):
                    copy(w, j, me_chip, c, (px, py, c), rows, src=ins[w].at[c, rows]).start()
            elif phase == 1:
                copy(w, j, 2 * px + py, c, (x, y, c)).wait_recv()
                for rows in _row_chunks(ins[w].shape[1], D2D_CHUNK_ROWS):
                    copy(w, 3 + j, 2 * px + py, c, sibling, rows).start()
            else:
                copy(w, 3 + j, 2 * px + py, 1 - c, (x, y, c)).wait_recv()
                copy(w, j, me_chip, c, (px, py, c), src=ins[w].at[c]).wait_send()
                copy(w, 3 + j, 2 * px + py, c, sibling).wait_send()


def _gather_scratch(n_w):
    return [pltpu.SemaphoreType.DMA((n_w, 6)), pltpu.SemaphoreType.DMA((n_w, 6))]


def _all_gather_weights(shards):
    n_w = len(shards)

    def body(*refs):
        for phase in range(3):
            _gather_phase(phase, refs[:n_w], refs[n_w:2 * n_w], *refs[2 * n_w:])

    return pl.pallas_call(
        body, name="all_gather_weights",
        out_shape=tuple(jax.ShapeDtypeStruct((4,) + s.shape, s.dtype) for s in shards),
        in_specs=[ANY] * n_w, out_specs=(ANY,) * n_w, scratch_shapes=_gather_scratch(n_w),
    )(*shards)


def _exchange_phase(phase, ins, theirs, send_sems, recv_sems):
    x, y, c, _ = _place()

    def remote(w, k=slice(None), rows=slice(None)):
        return pltpu.make_async_remote_copy(src_ref=ins[w].at[k, 1 - c, rows], dst_ref=theirs[w].at[k, rows],
                                            send_sem=send_sems.at[w], recv_sem=recv_sems.at[w], device_id=(x, y, 1 - c),
                                            device_id_type=MESH)

    for w in range(len(ins)):
        if phase == 0:
            for k in range(4):
                for rows in _row_chunks(ins[w].shape[2], D2D_CHUNK_ROWS):
                    remote(w, k, rows).start()
        else:
            remote(w).wait()


def _exchange_scratch(n_w):
    return [pltpu.SemaphoreType.DMA((n_w,)), pltpu.SemaphoreType.DMA((n_w,))]


def _exchange_out_shapes(grads):
    return tuple(jax.ShapeDtypeStruct((4,) + g.shape[2:], g.dtype) for g in grads)


def _add_pairs(part, theirs, name):
    _, _, r, c = part.shape
    rb = 32 if r % 32 == 0 else r

    def body(a_ref, b_ref, own_ref, ob_ref):
        my_chip = 2 * lax.axis_index("x") + lax.axis_index("y")
        ob_ref[...] = (a_ref[...] + b_ref[...]).astype(BF)
        own_ref[...] = a_ref[my_chip] + b_ref[my_chip]

    spec = pl.BlockSpec((4, rb, c), lambda i: (0, i, 0))
    return pl.pallas_call(
        body, name=name, grid=(r // rb,),
        out_shape=(jax.ShapeDtypeStruct((r, c), F32), jax.ShapeDtypeStruct((4, r, c), BF)),
        in_specs=[pl.BlockSpec((4, None, rb, c), lambda i: (0, lax.axis_index("c"), i, 0)), spec],
        out_specs=(pl.BlockSpec((rb, c), lambda i: (i, 0)), spec), compiler_params=_cparams(("arbitrary",)),
    )(part, theirs)


def _scatter_phase(phase, bfs, got, send_sems, recv_sems):
    x, y, c, chips = _place()

    def remote(w, j, px, py, rows=slice(None)):
        return pltpu.make_async_remote_copy(src_ref=bfs[w].at[2 * px + py, rows], dst_ref=got[w].at[j, rows],
                                            send_sem=send_sems.at[w, j], recv_sem=recv_sems.at[w, j], device_id=(px, py, c),
                                            device_id_type=MESH)

    for w in range(len(bfs)):
        for j, (px, py) in enumerate(chips):
            if phase == 0:
                for rows in _row_chunks(bfs[w].shape[1], ICI_CHUNK_ROWS):
                    remote(w, j, px, py, rows).start()
            else:
                remote(w, j, px, py).wait()


def _scatter_scratch(n_w):
    return [pltpu.SemaphoreType.DMA((n_w, 3)), pltpu.SemaphoreType.DMA((n_w, 3))]


def _scatter_out_shapes(sums_bf16):
    return tuple(jax.ShapeDtypeStruct((3,) + s.shape[1:], BF) for s in sums_bf16)


def _add_received(own, got, name):
    r, c = own.shape
    rb = 32 if r % 32 == 0 else r

    def body(o_ref, g_ref, out_ref):
        out_ref[...] = ((o_ref[...] + g_ref[0].astype(F32)) + g_ref[1].astype(F32)) + g_ref[2].astype(F32)

    return pl.pallas_call(
        body, name=name, grid=(r // rb,), out_shape=jax.ShapeDtypeStruct((r, c), F32),
        in_specs=[pl.BlockSpec((rb, c), lambda i: (i, 0)), pl.BlockSpec((3, rb, c), lambda i: (0, i, 0))],
        out_specs=pl.BlockSpec((rb, c), lambda i: (i, 0)), compiler_params=_cparams(("arbitrary",)),
    )(own, got)


def _share_with_sibling(halves):
    n_w = len(halves)

    def body(*refs):
        ins, outs = refs[:n_w], refs[n_w:2 * n_w]
        send_sems, recv_sems = refs[2 * n_w:]
        x, y, c, _ = _place()

        def remote(w, rows=slice(None)):
            return pltpu.make_async_remote_copy(src_ref=ins[w].at[rows], dst_ref=outs[w].at[c, rows], send_sem=send_sems.at[w],
                                                recv_sem=recv_sems.at[w], device_id=(x, y, 1 - c), device_id_type=MESH)

        for w in range(n_w):
            for rows in _row_chunks(ins[w].shape[0], D2D_CHUNK_ROWS):
                remote(w, rows).start()
        for w in range(n_w):
            remote(w).wait()

    return pl.pallas_call(
        body, name="share_with_sibling",
        out_shape=tuple(jax.ShapeDtypeStruct((2,) + h.shape, h.dtype) for h in halves),
        in_specs=[ANY] * n_w, out_specs=(ANY,) * n_w,
        scratch_shapes=[pltpu.SemaphoreType.DMA((n_w,)), pltpu.SemaphoreType.DMA((n_w,))],
    )(*halves)


def _small_phase(phase, p_ref, out_ref, slots, send_sems, recv_sems):
    x, y, cc, _ = _place()
    me = 4 * x + 2 * y + cc
    copies = []
    for k in range(1, 8):
        dx, dy, dc = (k >> 2) & 1, (k >> 1) & 1, k & 1
        to = (1 - x if dx else x, 1 - y if dy else y, 1 - cc if dc else cc)
        copies.append(pltpu.make_async_remote_copy(src_ref=p_ref, dst_ref=slots.at[me], send_sem=send_sems.at[k - 1],
                                                   recv_sem=recv_sems.at[k - 1], device_id=to, device_id_type=MESH))
    if phase == 0:
        slots[me] = p_ref[...]
        for cp in copies:
            cp.start()
    else:
        for cp in copies:
            cp.wait()
        total = slots[0]
        for d in range(1, 8):
            total = total + slots[d]
        out_ref[...] = total


def _adamw_update(w_ref, g_ref, m_ref, v_ref, d_ref, nm_ref, nv_ref):
    gv = g_ref[...]
    nm = ADAM_B1 * m_ref[...] + (1.0 - ADAM_B1) * gv
    nv = ADAM_B2 * v_ref[...] + (1.0 - ADAM_B2) * (gv * gv)
    nm_ref[...] = nm
    nv_ref[...] = nv
    m_hat = nm / (1.0 - ADAM_B1 ** ADAM_STEP)
    v_hat = nv / (1.0 - ADAM_B2 ** ADAM_STEP)
    d_ref[...] = -ADAM_LR * (m_hat / (jnp.sqrt(v_hat) + ADAM_EPS) + ADAM_WD * w_ref[...])


def _adamw_many(ws, gs, ms, vs, sums, pack):
    n_a, n_w = len(ws), len(sums)
    n_steps = ADAM_STEPS
    specs = [pl.BlockSpec((w.shape[0] // n_steps, w.shape[1]), lambda i: (i, 0)) for w in ws]
    pack_spec = pl.BlockSpec(pack.shape, lambda i: (0, 0))

    def body(*refs):
        ins = refs[:4 * n_a]
        p_ref = refs[4 * n_a + n_w]
        first_out = 4 * n_a + n_w + 1
        outs = refs[first_out:first_out + 3 * n_a]
        total_ref = refs[first_out + 3 * n_a + n_w]
        scratch = refs[first_out + 3 * n_a + n_w + 1:]
        scatter = (refs[4 * n_a:4 * n_a + n_w], refs[first_out + 3 * n_a:first_out + 3 * n_a + n_w]) + tuple(scratch[:2])
        small = (p_ref, total_ref) + tuple(scratch[2:])
        step = pl.program_id(0)

        @pl.when(step == 0)
        def _():
            _scatter_phase(0, *scatter)
            _small_phase(0, *small)

        for a in range(n_a):
            _adamw_update(*(ins[k * n_a + a] for k in range(4)), *(outs[3 * a + k] for k in range(3)))

        @pl.when(step == n_steps - 1)
        def _():
            _scatter_phase(1, *scatter)
            _small_phase(1, *small)

    flat = pl.pallas_call(
        body, name="adamw_late", grid=(n_steps,),
        out_shape=tuple(jax.ShapeDtypeStruct(w.shape, F32) for w in ws for _ in range(3)) + _scatter_out_shapes(sums)
        + (jax.ShapeDtypeStruct(pack.shape, F32),),
        in_specs=specs * 4 + [ANY] * n_w + [pack_spec],
        out_specs=tuple(s for s in specs for _ in range(3)) + (ANY,) * n_w + (pack_spec,),
        scratch_shapes=_scatter_scratch(n_w) + [pltpu.VMEM((8,) + pack.shape, F32), pltpu.SemaphoreType.DMA((7,)),
                                                pltpu.SemaphoreType.DMA((7,))],
        compiler_params=_cparams(("arbitrary",)),
    )(*ws, *gs, *ms, *vs, *sums, pack)
    return [tuple(flat[3 * a:3 * a + 3]) for a in range(n_a)] + list(flat[3 * n_a:])


def _adamw(w, g, m, v, name):
    r, c = w.shape
    rb, cb = (64, c) if r % 64 == 0 else (r, LANES if (r % 8 and c % LANES == 0) else c)

    def body(*refs):
        _adamw_update(*refs)

    spec = pl.BlockSpec((rb, cb), lambda i, j: (i, j))
    return pl.pallas_call(
        body, name=name, grid=(r // rb, c // cb), out_shape=(jax.ShapeDtypeStruct((r, c), F32),) * 3,
        in_specs=[spec] * 4, out_specs=(spec,) * 3, compiler_params=_cparams(("arbitrary", "arbitrary")),
    )(w, g, m, v)


def _rope_tables(t_len):
    inv_freq = ROPE_BASE ** (-jnp.arange(0, HEAD_DIM, 2, dtype=F32) / HEAD_DIM)
    ang = jnp.arange(t_len, dtype=F32)[:, None] * inv_freq[None, :]
    cos, sin = jnp.cos(ang), jnp.sin(ang)
    cos_t = jnp.concatenate([cos, cos, cos, cos], axis=-1)
    sin_t = jnp.concatenate([-sin, sin, -sin, sin], axis=-1)
    return cos_t, sin_t


def _cols_to_shards(dw):
    r, n = dw.shape
    return jnp.transpose(dw.reshape(2, r // 2, 4, n // 4), (2, 0, 1, 3))


def _rows_to_shards(dw):
    r, n = dw.shape
    rows = r // 4
    if rows % SUBLANES == 0:
        padded = _pad_rows(dw.reshape(4, rows, n))
    else:
        window = rows + SUBLANES - rows % SUBLANES
        padded = _pad_rows(jnp.stack([dw[rows * k // SUBLANES * SUBLANES:][:window] for k in range(4)]))
    return padded.reshape(4, 2, padded.shape[1] // 2, n)


def _shard_row_offset(rows):
    return (rows * (2 * lax.axis_index("x") + lax.axis_index("y"))) % SUBLANES


def _pad_lanes(a):
    extra = -a.shape[-1] % LANES
    return a if extra == 0 else jnp.pad(a, [(0, 0)] * (a.ndim - 1) + [(0, extra)])


def _pad_rows(a):
    rows = a.shape[-2]
    extra = 0 if rows % SHARD_ROW_ALIGN == 0 else -rows % SHARD_ROW_PAD
    return a if extra == 0 else jnp.pad(a, [(0, 0)] * (a.ndim - 2) + [(0, extra), (0, 0)])


def _pad_row(a, width=D_MODEL):
    a = a.reshape(1, -1)
    return jnp.pad(a, ((0, 0), (0, width - a.shape[1])))


def kernel(x, mem, g_mix, w_in, b_forget, g_ret_out, g_fox_q, g_fox_k, w_out, g_xattn, w_xq, w_xkv, g_mem, g_xq, g_xk, w_xo, g_ffn, w_gate, w_up, w_down, loss_target, m_g_mix, m_w_in, m_b_forget, m_g_ret_out, m_g_fox_q, m_g_fox_k, m_w_out, m_g_xattn, m_w_xq, m_w_xkv, m_g_mem, m_g_xq, m_g_xk, m_w_xo, m_g_ffn, m_w_gate, m_w_up, m_w_down, v_g_mix, v_w_in, v_b_forget, v_g_ret_out, v_g_fox_q, v_g_fox_k, v_w_out, v_g_xattn, v_w_xq, v_w_xkv, v_g_mem, v_g_xq, v_g_xk, v_w_xo, v_g_ffn, v_w_gate, v_w_up, v_w_down):
    big = {"w_in": (w_in, m_w_in, v_w_in), "w_out": (w_out, m_w_out, v_w_out), "w_xq": (w_xq, m_w_xq, v_w_xq),
           "w_xkv": (w_xkv, m_w_xkv, v_w_xkv), "w_xo": (w_xo, m_w_xo, v_w_xo), "w_gate": (w_gate, m_w_gate, v_w_gate),
           "w_up": (w_up, m_w_up, v_w_up), "w_down": (w_down, m_w_down, v_w_down)}
    for n in TRANSPOSED:
        big[n] = tuple(jnp.swapaxes(a, 1, 2) for a in big[n])
    shards = {}
    for n in big:
        w = _pad_rows(_pad_lanes(big[n][0][0].astype(BF)))
        shards[n] = w.reshape(2, w.shape[0] // 2, w.shape[1])
    sizes = {n: big[n][0].shape[1:] for n in big}
    w_in_full = _assemble_weight("w_in", _all_gather_weights([shards["w_in"]])[0], shards["w_in"], sizes["w_in"])
    small_w ={"g_mix": g_mix, "b_forget": b_forget, "g_ret_out": g_ret_out, "g_fox_q": g_fox_q, "g_fox_k": g_fox_k,
               "g_xattn": g_xattn, "g_mem": g_mem, "g_xq": g_xq, "g_xk": g_xk, "g_ffn": g_ffn}
    m_small = {"g_mix": m_g_mix, "b_forget": m_b_forget, "g_ret_out": m_g_ret_out, "g_fox_q": m_g_fox_q, "g_fox_k": m_g_fox_k,
               "g_xattn": m_g_xattn, "g_mem": m_g_mem, "g_xq": m_g_xq, "g_xk": m_g_xk, "g_ffn": m_g_ffn}
    v_small = {"g_mix": v_g_mix, "b_forget": v_b_forget, "g_ret_out": v_g_ret_out, "g_fox_q": v_g_fox_q, "g_fox_k": v_g_fox_k,
               "g_xattn": v_g_xattn, "g_mem": v_g_mem, "g_xq": v_g_xq, "g_xk": v_g_xk, "g_ffn": v_g_ffn}
    loss_part, grad_x, sums, got, in_parts, small_g = _local_step(x[0], mem[0], loss_target[0], w_in_full, shards, sizes, small_w)
    return _reduce_and_update(big, sums, got, in_parts, small_w, small_g, loss_part, grad_x, m_small, v_small)


def _assemble_weight(name, gathered, own, size):
    rows, width = size
    my_chip = 2 * lax.axis_index("x") + lax.axis_index("y")
    g = lax.dynamic_update_slice(gathered, own[None], (my_chip, 0, 0, 0))
    g = g.reshape(4, 2 * g.shape[2], g.shape[3])[:, :rows, :width]
    return jnp.transpose(g, (1, 0, 2)).reshape(rows, 4 * width) if name in COL_SHARDED else g.reshape(4 * rows, width)


def _shard_parts(names, dw):
    return [_pad_lanes(_cols_to_shards(dw[n]) if n in COL_SHARDED else _rows_to_shards(dw[n])) for n in names]


def _add_pairs_many(parts, theirs, name):
    n_a = len(parts)
    n_steps = min(p.shape[2] for p in parts) // 32
    rb = [p.shape[2] // n_steps for p in parts]
    part_specs = [pl.BlockSpec((4, None, r, p.shape[3]), lambda i: (0, lax.axis_index("c"), i, 0)) for p, r in zip(parts, rb)]
    quad_specs = [pl.BlockSpec((4, r, p.shape[3]), lambda i: (0, i, 0)) for p, r in zip(parts, rb)]
    own_specs = [pl.BlockSpec((r, p.shape[3]), lambda i: (i, 0)) for p, r in zip(parts, rb)]

    def body(*refs):
        my_chip = 2 * lax.axis_index("x") + lax.axis_index("y")
        for a in range(n_a):
            a_ref, b_ref, own_ref, ob_ref = refs[a], refs[n_a + a], refs[2 * n_a + a], refs[3 * n_a + a]
            ob_ref[...] = (a_ref[...] + b_ref[...]).astype(BF)
            own_ref[...] = a_ref[my_chip] + b_ref[my_chip]

    flat = pl.pallas_call(
        body, name=name, grid=(n_steps,),
        out_shape=tuple(jax.ShapeDtypeStruct(p.shape[2:], F32) for p in parts)
        + tuple(jax.ShapeDtypeStruct((4,) + p.shape[2:], BF) for p in parts),
        in_specs=part_specs + quad_specs, out_specs=tuple(own_specs) + tuple(quad_specs),
        compiler_params=_cparams(("arbitrary",)),
    )(*parts, *theirs)
    return [(flat[a], flat[n_a + a]) for a in range(n_a)]


def _core_sums(names, parts, theirs):
    if len(names) == 1:
        return [_add_pairs(parts[0], theirs[0], f"core_sum_{names[0]}")]
    out = [None] * len(names)
    for tag, pick in (("a", lambda p: p.shape[2] % LANES == 0), ("b", lambda p: p.shape[2] % LANES != 0)):
        idx = [i for i, p in enumerate(parts) if pick(p)]
        for i, res in zip(idx, _add_pairs_many([parts[i] for i in idx], [theirs[i] for i in idx], f"core_sum_late_{tag}")):
            out[i] = res
    return out


def _local_step(xs, mems, tgt, w_in_full, shards, sizes, small_w):
    g_mix, b_forget, g_ret_out, g_fox_q, g_fox_k = (small_w[n] for n in ("g_mix", "b_forget", "g_ret_out", "g_fox_q", "g_fox_k"))
    g_xattn, g_mem, g_xq, g_xk, g_ffn = (small_w[n] for n in ("g_xattn", "g_mem", "g_xq", "g_xk", "g_ffn"))
    w_in_t = jnp.pad(w_in_full, ((0, MAIN_W + LANES - IN_W), (0, 0)))
    t_len = xs.shape[0]
    cos_t, sin_t = _rope_tables(t_len)
    tables = _decay_tables(min(RET_BLOCK, t_len))
    gq_t = jnp.concatenate([g_fox_q, g_fox_q], axis=-1)
    gk_t = jnp.concatenate([g_fox_k, g_fox_k], axis=-1)
    b_pad = _pad_row(b_forget, LANES)
    g_ret = g_ret_out.reshape(N_HEADS // 2, 1, LANES)

    n1, proj, rq, rk, q_aug, k_aug, z = _in_proj_fwd(xs, g_mix, w_in_t, b_pad, cos_t, sin_t, gq_t, gk_t)
    raw, mix_r, states = _retention_fwd(rq, rk, proj, g_ret, tables)
    mix_f, o32, lse, *gathered = _fox_fwd(q_aug, k_aug, proj, [shards[n] for n in LATE])
    full = {n: _assemble_weight(n, g, shards[n], sizes[n]) for n, g in zip(LATE, gathered)}
    memn, kraw, kn, vmem = _mem_kv_fwd(mems, g_mem, full["w_xkv"], g_xk)
    h1, hn2, qx, o_x, h2 = _attn_out_xattn_fwd(xs, mix_r, mix_f, full["w_out"], g_xattn, full["w_xq"], g_xq, kn, vmem, full["w_xo"])
    hn3, gate, up, act, dh3, loss_part = _ffn_loss_fwd(h2, g_ffn, full["w_gate"], full["w_up"], full["w_down"], tgt)

    dgate, dup, dh2, dg_ffn = _ffn_bwd(dh3, gate, up, h2, g_ffn, full["w_gate"], full["w_up"], full["w_down"])
    dqx, dh1, dmr, dmf, dkn, dvm, dg_xattn, dg_xq = _attn_out_xattn_bwd(dh2, h1, qx, kn, vmem, full["w_xo"], full["w_xq"],
                                                                      full["w_out"], g_xattn, g_xq)
    dw_xkv, dg_mem, dg_xk = _mem_kv_bwd(dkn, dvm, kraw, mems, memn, g_mem, g_xk, full["w_xkv"])
    dw = {
        "w_out": jnp.concatenate([_matmul_tn(mix_r, dh1, "dw_out_ret"), _matmul_tn(mix_f, dh1, "dw_out_fox")], axis=0),
        "w_xq": _matmul_tn(hn2, dqx, "dw_xq"),
        "w_xkv": dw_xkv,
        "w_xo": _matmul_tn(o_x, dh2, "dw_xo"),
        "w_gate": _matmul_tn(dgate, hn3, "dw_gate"),
        "w_up": _matmul_tn(dup, hn3, "dw_up"),
        "w_down": _matmul_tn(act, dh3, "dw_down"),
    }
    late_parts = _shard_parts(LATE, dw)
    dq_r, dk_r, dv_r, drg, dg_ret, *late_theirs = _retention_bwd(dmr, raw, proj, g_ret, rq, rk, states, tables, late_parts)
    late_sums = _core_sums(LATE, late_parts, late_theirs)
    dq_f, dk_f, dv_f, df, *late_got = _fox_bwd(q_aug, k_aug, proj, dmf, o32, lse, [s[1] for s in late_sums])
    df_col = jnp.pad(jnp.transpose(df, (1, 0, 2)).reshape(t_len, N_HEADS), ((0, 0), (0, LANES - N_HEADS)))
    dproj, dz, grad_x, dg_mix, dg_fq, dg_fk, db = _in_proj_bwd(xs, g_mix, dh1, dq_r, dk_r, dv_r, drg, dq_f, dk_f, dv_f, df_col,
                                                              proj, z, cos_t, sin_t, gq_t, gk_t, w_in_t)

    dw_in = jnp.concatenate([_matmul_tn(dproj, n1, "dw_in_main"), _matmul_tn(dz, n1, "dw_in_ff")[:IN_W - MAIN_W]], axis=0)
    in_parts = _shard_parts(("w_in",), {"w_in": dw_in})
    sums = {n: s[0] for n, s in zip(LATE, late_sums)}
    got = dict(zip(LATE, late_got))
    small_g = {"g_mix": dg_mix, "b_forget": db[:, :N_HEADS], "g_ret_out": dg_ret, "g_fox_q": dg_fq, "g_fox_k": dg_fk,
               "g_xattn": dg_xattn, "g_mem": dg_mem, "g_xq": dg_xq, "g_xk": dg_xk, "g_ffn": dg_ffn}
    return loss_part, grad_x, sums, got, in_parts, small_g


def _add_received_many(owns, gots, parts):
    n_a, n_w = len(owns), len(parts)
    n_steps = CHIP_SUM_STEPS
    own_specs = [pl.BlockSpec((o.shape[0] // n_steps, o.shape[1]), lambda i: (i, 0)) for o in owns]
    got_specs = [pl.BlockSpec((3, o.shape[0] // n_steps, o.shape[1]), lambda i: (0, i, 0)) for o in owns]

    def body(*refs):
        first_out = 2 * n_a + n_w
        comm = (refs[2 * n_a:first_out], refs[first_out + n_a:first_out + n_a + n_w]) + tuple(refs[first_out + n_a + n_w:])
        step = pl.program_id(0)

        @pl.when(step == 0)
        def _():
            _exchange_phase(0, *comm)

        for a in range(n_a):
            o_ref, g_ref, out_ref = refs[a], refs[n_a + a], refs[first_out + a]
            out_ref[...] = ((o_ref[...] + g_ref[0].astype(F32)) + g_ref[1].astype(F32)) + g_ref[2].astype(F32)

        @pl.when(step == n_steps - 1)
        def _():
            _exchange_phase(1, *comm)

    flat = pl.pallas_call(
        body, name="chip_sum_late", grid=(n_steps,),
        out_shape=tuple(jax.ShapeDtypeStruct(o.shape, F32) for o in owns) + _exchange_out_shapes(parts),
        in_specs=own_specs + got_specs + [ANY] * n_w, out_specs=tuple(own_specs) + (ANY,) * n_w,
        scratch_shapes=_exchange_scratch(n_w), compiler_params=_cparams(("arbitrary",)),
    )(*owns, *gots, *parts)
    return list(flat[:n_a]), list(flat[n_a:])


def _final_grads(names, big, finals):
    my_core = lax.axis_index("c")
    shared = _share_with_sibling(finals)
    out = {}
    for n, s, fin in zip(names, shared, finals):
        s = lax.dynamic_update_slice(s, fin[None], (my_core, 0, 0))
        s = s.reshape(2 * s.shape[1], s.shape[2])
        rows, width = big[n][0].shape[1:]
        out[n] = s[:rows, :width] if rows % SUBLANES == 0 else lax.dynamic_slice(s, (_shard_row_offset(rows), 0), (rows, width))
    return out


def _reduce_and_update(big, sums, got, in_parts, small_w, small_g, loss_part, grad_x, m_small, v_small):
    small_names = list(small_w)
    pad_rows = SMALL_ROWS - len(small_names) - 1
    stack = lambda d: jnp.concatenate([_pad_row(d[n]) for n in small_names] + [jnp.zeros((pad_rows + 1, D_MODEL), F32)], axis=0)
    g_pack = jnp.concatenate([_pad_row(small_g[n]) for n in small_names] + [_pad_row(loss_part[0:1, 0:1])]
                             + [jnp.zeros((pad_rows, D_MODEL), F32)], axis=0)
    late_finals, in_theirs = _add_received_many([sums[n] for n in LATE], [got[n] for n in LATE], in_parts)
    in_own, in_bf = _core_sums(("w_in",), in_parts, in_theirs)[0]
    grads = _final_grads(LATE, big, late_finals)
    *late_updates, in_got, g_tot = _adamw_many([big[n][0][0] for n in LATE], [grads[n] for n in LATE], [big[n][1][0] for n in LATE],
                                               [big[n][2][0] for n in LATE], [in_bf], g_pack)
    updates = dict(zip(LATE, late_updates))
    grads.update(_final_grads(("w_in",), big, [_add_received(in_own, in_got, "chip_sum_w_in")]))
    updates["w_in"] = _adamw(big["w_in"][0][0], grads["w_in"], big["w_in"][1][0], big["w_in"][2][0], "adamw_w_in")
    deltas, new_m, new_v = {}, {}, {}
    for n in big:
        restore = (lambda a: jnp.swapaxes(a[None], 1, 2)) if n in TRANSPOSED else (lambda a: a[None])
        grads[n] = restore(grads[n])
        deltas[n], new_m[n], new_v[n] = (restore(a) for a in updates[n])

    d_s, m_s, v_s = _adamw(stack(small_w), g_tot, stack(m_small), stack(v_small), "adamw_small")
    for i, n in enumerate(small_names):
        shape = small_w[n].shape
        size = int(np.prod(shape))
        grads[n] = g_tot[i, :size].reshape(shape)
        deltas[n], new_m[n], new_v[n] = d_s[i, :size].reshape(shape), m_s[i, :size].reshape(shape), v_s[i, :size].reshape(shape)
    loss = g_tot[len(small_names), 0]

    order = ["g_mix", "w_in", "b_forget", "g_ret_out", "g_fox_q", "g_fox_k", "w_out", "g_xattn", "w_xq", "w_xkv", "g_mem", "g_xq",
             "g_xk", "w_xo", "g_ffn", "w_gate", "w_up", "w_down"]
    return (loss, grad_x[None], *[grads[n] for n in order], *[deltas[n] for n in order], *[new_m[n] for n in order],
            *[new_v[n] for n in order])
```

```python
import functools

import numpy as np
import jax
import jax.numpy as jnp
from jax import lax
from jax.experimental import pallas as pl
from jax.experimental.pallas import tpu as pltpu

F32 = jnp.float32
BF = jnp.bfloat16

D_MODEL = 1024
HEAD_DIM = 64
N_HEADS = 8
GROUP_W = 512
N_XH = 4
XHD = 256
D_FF = 2816
MAIN_W = 3584
IN_W = 3592
ROPE_BASE = 10000.0
LOG2E = 1.4426950408889634
LN2 = 0.6931471805599453
EPS = 1e-6
NEG = -1e30
LANES = 128
SUBLANES = 8
RET_BLOCK = 256
REF_CHUNK = 64
ROW_TILE = 512
FFN_BWD_TILE = 256
ATT_BLOCK = 256
FWD_GROUP = 4
TN_MAX_ROWS = 1408
SMALL_ROWS = 16
COL_SHARDED = ("w_xkv",)
TRANSPOSED = ("w_in", "w_gate", "w_up")
SHARD_ROW_ALIGN = 32
SHARD_ROW_PAD = 256
LATE = ("w_out", "w_xq", "w_xkv", "w_xo", "w_gate", "w_up", "w_down")
VMEM_LIMIT = 56 * 1024 * 1024

ADAM_LR = 0.001
ADAM_B1 = 0.9
ADAM_B2 = 0.999
ADAM_EPS = 1e-08
ADAM_WD = 0.01
ADAM_STEP = 10
CHIP_SUM_STEPS = 2
ADAM_STEPS = 8

MESH = pl.DeviceIdType.MESH
ANY = pl.BlockSpec(memory_space=pl.ANY)
VMEM_SPEC = pl.BlockSpec(memory_space=pltpu.VMEM)


def _cparams(sem=None, vmem=VMEM_LIMIT):
    return pltpu.CompilerParams(dimension_semantics=sem, vmem_limit_bytes=vmem)


def _dot(a, b):
    return jnp.dot(a.astype(BF), b.astype(BF), preferred_element_type=F32)


def _dot_nt(a, b):
    return lax.dot_general(a.astype(BF), b.astype(BF), (((1,), (1,)), ((), ())), preferred_element_type=F32)


def _dot_tn(a, b):
    return lax.dot_general(a.astype(BF), b.astype(BF), (((0,), (0,)), ((), ())), preferred_element_type=F32)


def _split3(x):
    hi = x.astype(BF)
    r = x - hi.astype(F32)
    mid = r.astype(BF)
    lo = (r - mid.astype(F32)).astype(BF)
    return hi, mid, lo


def _dot_exact(ind, x):
    hi, mid, lo = _split3(x)
    return (jnp.dot(ind, lo, preferred_element_type=F32) + jnp.dot(ind, mid, preferred_element_type=F32)
            + jnp.dot(ind, hi, preferred_element_type=F32))


def _dot_nt_exact(ind, x):
    hi, mid, lo = _split3(x)
    dn = (((1,), (1,)), ((), ()))
    return (lax.dot_general(ind, lo, dn, preferred_element_type=F32) + lax.dot_general(ind, mid, dn, preferred_element_type=F32)
            + lax.dot_general(ind, hi, dn, preferred_element_type=F32))


def _sigmoid(x):
    return 1.0 / (1.0 + jnp.exp(-x))


def _rms_fwd(x, g):
    r = lax.rsqrt(jnp.mean(x * x, axis=-1, keepdims=True) + EPS)
    return x * r * g


def _rms_bwd(x, g, dy):
    r = lax.rsqrt(jnp.mean(x * x, axis=-1, keepdims=True) + EPS)
    xh = x * r
    dg = jnp.sum(dy * xh, axis=0, keepdims=True)
    dxh = dy * g
    dx = r * (dxh - xh * jnp.mean(dxh * xh, axis=-1, keepdims=True))
    return dx, dg


def _group_mean64(x):
    lane = lax.broadcasted_iota(jnp.int32, x.shape, 1)
    lo = lane < HEAD_DIM
    s_lo = jnp.sum(jnp.where(lo, x, 0.0), axis=-1, keepdims=True)
    s_hi = jnp.sum(jnp.where(lo, 0.0, x), axis=-1, keepdims=True)
    return jnp.where(lo, s_lo, s_hi) * (1.0 / HEAD_DIM)


def _swap32(x):
    lane = lax.broadcasted_iota(jnp.int32, x.shape, 1)
    first = (lane % HEAD_DIM) < (HEAD_DIM // 2)
    return jnp.where(first, pltpu.roll(x, LANES - HEAD_DIM // 2, axis=1), pltpu.roll(x, HEAD_DIM // 2, axis=1))


def _chunks(w):
    return [slice(j * LANES, (j + 1) * LANES) for j in range(w // LANES)]


def _aug_pair(qk, f_cols, is_query):
    lane = lax.broadcasted_iota(jnp.int32, qk.shape, 1)
    a = lane - HEAD_DIM
    values = (qk, pltpu.roll(qk, HEAD_DIM, axis=1))
    out = []
    for hh in range(2):
        hi, mid, lo = (p.astype(F32) for p in _split3(f_cols[hh] * LOG2E))
        if is_query:
            aux = jnp.where(a == 0, hi, jnp.where(a == 1, mid, jnp.where(a == 2, lo, jnp.where(a < 6, 1.0, 0.0))))
        else:
            aux = jnp.where(a < 3, 1.0, jnp.where(a == 3, -hi, jnp.where(a == 4, -mid, jnp.where(a == 5, -lo, 0.0))))
        out.append(jnp.where(a < 0, values[hh], aux))
    return jnp.concatenate(out, axis=-1).astype(BF)


def _mem_kv_fwd(mem, g_mem, w_xkv, g_xk):
    m_tok = mem.shape[0]

    def body(mem_ref, gm_ref, w_ref, gk_ref, memn_ref, kraw_ref, kn_ref, v_ref):
        mn = _rms_fwd(mem_ref[...], gm_ref[...]).astype(BF)
        memn_ref[...] = mn
        kv = jnp.dot(mn, w_ref[...], preferred_element_type=F32)
        k = kv[:, :D_MODEL]
        kraw_ref[...] = k
        v_ref[...] = kv[:, D_MODEL:].astype(BF)
        for h in range(N_XH):
            sl = slice(h * XHD, (h + 1) * XHD)
            kn_ref[:, sl] = _rms_fwd(k[:, sl], gk_ref[...]).astype(BF)

    return pl.pallas_call(
        body, name="mem_kv_fwd",
        out_shape=(jax.ShapeDtypeStruct((m_tok, D_MODEL), BF), jax.ShapeDtypeStruct((m_tok, D_MODEL), F32),
                   jax.ShapeDtypeStruct((m_tok, D_MODEL), BF), jax.ShapeDtypeStruct((m_tok, D_MODEL), BF)),
        in_specs=[VMEM_SPEC] * 4, out_specs=(VMEM_SPEC,) * 4, compiler_params=_cparams(),
    )(mem, g_mem, w_xkv, g_xk)


def _in_proj_fwd(x, g_mix, w_in_t, b_pad, cos_t, sin_t, gq_t, gk_t):
    t_len = x.shape[0]
    tm = min(ROW_TILE, t_len)
    n_t = t_len // tm

    def body(x_ref, g_ref, wm_ref, wf_ref, b_ref, cos_ref, sin_ref, gq_ref, gk_ref,
             n1_ref, proj_ref, rq_ref, rk_ref, qa_ref, ka_ref, z_ref, carry):
        i = pl.program_id(0)

        @pl.when(i == 0)
        def _():
            carry[...] = jnp.zeros_like(carry)

        n1 = _rms_fwd(x_ref[...], g_ref[...]).astype(BF)
        n1_ref[...] = n1
        z = _dot_nt(n1, wf_ref[...]) + b_ref[...]
        z_ref[...] = z
        lane = lax.broadcasted_iota(jnp.int32, z.shape, 1)
        lf = jnp.where(lane < N_HEADS, jnp.minimum(z, 0.0) - jnp.log(1.0 + jnp.exp(-jnp.abs(z))), 0.0)
        row = lax.broadcasted_iota(jnp.int32, (tm, tm), 0)
        col = lax.broadcasted_iota(jnp.int32, (tm, tm), 1)
        tri = (row >= col).astype(BF)
        fc = _dot_exact(tri, lf) + carry[0:1, :]
        carry[...] = jnp.broadcast_to(fc[tm - 1:tm, :], carry.shape)
        c, s = cos_ref[...], sin_ref[...]

        def section(n):
            p = _dot_nt(n1, wm_ref[n * GROUP_W:(n + 1) * GROUP_W, :])
            proj_ref[:, n * GROUP_W:(n + 1) * GROUP_W] = p.astype(BF)
            return p

        def rotate(p, out_ref, scale):
            for sl in _chunks(GROUP_W):
                out_ref[:, sl] = ((p[:, sl] * c + _swap32(p[:, sl]) * s) * scale).astype(BF)

        def norm_aug(p, gain, out_ref, scale, is_query):
            for j, sl in enumerate(_chunks(GROUP_W)):
                f = p[:, sl]
                f = f * lax.rsqrt(_group_mean64(f * f) + EPS) * gain * scale
                out_ref[:, 2 * j * LANES:2 * (j + 1) * LANES] = _aug_pair(f, [fc[:, 2 * j:2 * j + 1], fc[:, 2 * j + 1:2 * j + 2]], is_query)

        p_rq, p_rk = section(0), section(1)
        rotate(p_rq, rq_ref, 0.125)
        section(2)
        rotate(p_rk, rk_ref, 1.0)
        section(3)
        p_fq = section(4)
        p_fk = section(5)
        norm_aug(p_fq, gq_ref[...], qa_ref, 0.125 * LOG2E, True)
        section(6)
        norm_aug(p_fk, gk_ref[...], ka_ref, 1.0, False)

    row_spec = lambda w: pl.BlockSpec((tm, w), lambda i: (i, 0))
    full = lambda a: pl.BlockSpec(a.shape, lambda i: (0,) * a.ndim)
    return pl.pallas_call(
        body, name="in_proj_fwd", grid=(n_t,),
        out_shape=(jax.ShapeDtypeStruct((t_len, D_MODEL), BF), jax.ShapeDtypeStruct((t_len, MAIN_W), BF),
                   jax.ShapeDtypeStruct((t_len, GROUP_W), BF), jax.ShapeDtypeStruct((t_len, GROUP_W), BF),
                   jax.ShapeDtypeStruct((t_len, 2 * GROUP_W), BF), jax.ShapeDtypeStruct((t_len, 2 * GROUP_W), BF),
                   jax.ShapeDtypeStruct((t_len, LANES), F32)),
        in_specs=[row_spec(D_MODEL), full(g_mix), *_w_in_specs(), full(b_pad), row_spec(LANES), row_spec(LANES),
                  full(gq_t), full(gk_t)],
        out_specs=(row_spec(D_MODEL), row_spec(MAIN_W), row_spec(GROUP_W), row_spec(GROUP_W), row_spec(2 * GROUP_W),
                   row_spec(2 * GROUP_W), row_spec(LANES)),
        scratch_shapes=[pltpu.VMEM((8, LANES), F32)],
        compiler_params=_cparams(("arbitrary",)),
    )(x, g_mix, w_in_t, w_in_t, b_pad, cos_t, sin_t, gq_t, gk_t)


def _w_in_specs():
    return (pl.BlockSpec((MAIN_W, D_MODEL), lambda i: (0, 0)), pl.BlockSpec((LANES, D_MODEL), lambda i: (MAIN_W // LANES, 0)))


def _decay_tables(c):
    h = np.arange(N_HEADS, dtype=np.float64)
    lg = np.log(1.0 - 2.0 ** (-5.0 - h)).astype(np.float32).astype(np.float64)
    t = np.arange(c)
    same_or_earlier = (t[None, :] // REF_CHUNK) <= (t[:, None] // REF_CHUNK)
    w = np.where(same_or_earlier[None], np.exp(lg[:, None, None] * np.abs(t[:, None] - t[None, :])[None]), 0.0)
    qd = np.exp(lg[:, None] * (t[None, :] + 1.0))
    kd = np.exp(lg[:, None] * (c - 1.0 - t[None, :]))
    cd = np.exp(lg * c)
    ones = np.ones((1, 1, HEAD_DIM))
    return (jnp.asarray(w, F32), jnp.asarray(qd[:, :, None] * ones, F32), jnp.asarray(kd[:, :, None] * ones, F32),
            jnp.asarray(cd[:, None, None] * np.ones((1, HEAD_DIM, HEAD_DIM)), F32))


def _retention_fwd(rq, rk, proj, g_ret, tables):
    t_len = rq.shape[0]
    c = min(RET_BLOCK, t_len)
    n_b = t_len // c
    wdec, qdec, kdec, cdec = tables
    v_col, g_col = 2 * GROUP_W // LANES, 3 * GROUP_W // LANES

    def body(q_ref, k_ref, v_ref, rg_ref, g_ref, w_ref, qd_ref, kd_ref, cd_ref, raw_ref, mix_ref, st_ref, state):
        i = pl.program_id(1)

        @pl.when(i == 0)
        def _():
            state[...] = jnp.zeros_like(state)

        q2, k2, v2 = q_ref[...], k_ref[...], v_ref[...]
        heads = [tuple(t[:, hh * HEAD_DIM:(hh + 1) * HEAD_DIM] for t in (q2, k2, v2)) for hh in range(2)]
        scores = [(_dot_nt(q, k) * w_ref[hh]).astype(BF) for hh, (q, k, _) in enumerate(heads)]
        outs = []
        for hh, (q, k, v) in enumerate(heads):
            sp = state[hh]
            st_ref[0, 0, hh] = sp
            outs.append(jnp.dot(scores[hh], v, preferred_element_type=F32) + _dot(q.astype(F32) * qd_ref[hh], sp))
            state[hh] = sp * cd_ref[hh] + _dot_tn(k.astype(F32) * kd_ref[hh], v)
        o2 = jnp.concatenate(outs, axis=-1)
        raw_ref[...] = o2
        xc = o2 - _group_mean64(o2)
        xh = xc * lax.rsqrt(_group_mean64(xc * xc) + EPS)
        gate = rg_ref[...].astype(F32)
        mix_ref[...] = (gate * _sigmoid(gate) * (xh * g_ref[0])).astype(BF)

    blk = lambda col0: pl.BlockSpec((c, LANES), lambda hp, i: (i, col0 + hp))
    tab = lambda a: pl.BlockSpec((2,) + a.shape[1:], lambda hp, i: (hp, 0, 0))
    return pl.pallas_call(
        body, name="retention_fwd", grid=(N_HEADS // 2, n_b),
        out_shape=(jax.ShapeDtypeStruct((t_len, GROUP_W), F32), jax.ShapeDtypeStruct((t_len, GROUP_W), BF),
                   jax.ShapeDtypeStruct((N_HEADS // 2, n_b, 2, HEAD_DIM, HEAD_DIM), F32)),
        in_specs=[blk(0), blk(0), blk(v_col), blk(g_col), pl.BlockSpec((1, 1, LANES), lambda hp, i: (hp, 0, 0)),
                  tab(wdec), tab(qdec), tab(kdec), tab(cdec)],
        out_specs=(blk(0), blk(0), pl.BlockSpec((1, 1, 2, HEAD_DIM, HEAD_DIM), lambda hp, i: (hp, i, 0, 0, 0))),
        scratch_shapes=[pltpu.VMEM((2, HEAD_DIM, HEAD_DIM), F32)],
        compiler_params=_cparams(("arbitrary", "arbitrary")),
    )(rq, rk, proj, proj, g_ret, wdec, qdec, kdec, cdec)


def _fox_fwd(q_aug, k_aug, proj, shards):
    t_len = q_aug.shape[0]
    tq = min(ATT_BLOCK, t_len)
    nsub = min(FWD_GROUP, t_len // tq)
    tg = nsub * tq
    n_q = t_len // tg
    v_col = 6 * GROUP_W // LANES
    tc = min(512, t_len)
    n_w = len(shards)
    n_steps = (N_HEADS // 2) * n_q

    def body(*refs):
        q_ref, k_ref, v_ref = refs[:3]
        o_ref, o32_ref, lse_ref = refs[3 + n_w:6 + n_w]
        vt = refs[6 + 2 * n_w]
        comm = (refs[3:3 + n_w], refs[6 + n_w:6 + 2 * n_w]) + tuple(refs[7 + 2 * n_w:])
        i = pl.program_id(1)
        step = pl.program_id(0) * n_q + i

        @pl.when(step == 0)
        def _():
            _gather_phase(0, *comm)

        @pl.when(step == (3 * n_steps) // 4)
        def _():
            _gather_phase(1, *comm)

        @pl.when(i == 0)
        def _():
            for c0 in range(0, t_len, tc):
                vt[:, c0:c0 + tc] = v_ref[c0:c0 + tc, :].T

        chains = [(u, hh) for u in range(nsub) for hh in range(2)]
        qs = {(u, hh): q_ref[u * tq:(u + 1) * tq, hh * LANES:(hh + 1) * LANES] for u, hh in chains}
        ones = jnp.ones((HEAD_DIM, tq), BF)

        def scores(j, which):
            k2 = k_ref[pl.ds(pl.multiple_of(j * tq, tq), tq), :]
            return {ch: _dot_nt(k2[:, ch[1] * LANES:(ch[1] + 1) * LANES], qs[ch]) for ch in which}

        def update(j, ss, carry, masked):
            v2 = vt[:, pl.ds(pl.multiple_of(j * tq, tq), tq)]
            ps, stats = {}, {}
            for ch in ss:
                m = carry[ch][0]
                s_t = ss[ch]
                if ch in masked:
                    krow = lax.broadcasted_iota(jnp.int32, (tq, tq), 0)
                    qcol = lax.broadcasted_iota(jnp.int32, (tq, tq), 1)
                    s_t = jnp.where(qcol >= krow, s_t, NEG)
                m_new = jnp.maximum(m, jnp.max(s_t, axis=0, keepdims=True))
                ps[ch] = jnp.exp2(s_t - m_new).astype(BF)
                stats[ch] = (m_new, jnp.exp2(m - m_new))
            out = dict(carry)
            for ch in ss:
                m_new, alpha = stats[ch]
                v_aug = jnp.concatenate([v2[ch[1] * HEAD_DIM:(ch[1] + 1) * HEAD_DIM, :], ones], axis=0)
                out[ch] = (m_new, carry[ch][1] * alpha + jnp.dot(v_aug, ps[ch], preferred_element_type=F32))
            return out

        def advance(j, state):
            ss, carry = state
            return scores(j + 1, chains), update(j, ss, carry, ())

        init = {ch: (jnp.full((1, tq), NEG, F32), jnp.zeros((LANES, tq), F32)) for ch in chains}
        first = nsub * i
        ss, carry = lax.fori_loop(0, first, advance, (scores(0, chains), init))
        carry = update(first, ss, carry, [(0, 0), (0, 1)])
        for u in range(1, nsub):
            rest = [(uu, hh) for uu in range(u, nsub) for hh in range(2)]
            carry = update(first + u, scores(first + u, rest), carry, [(u, 0), (u, 1)])
        for u in range(nsub):
            outs, lses = [], []
            for hh in range(2):
                m, acc = carry[u, hh]
                l = acc[HEAD_DIM:HEAD_DIM + 1, :]
                outs.append(acc[:HEAD_DIM, :] / l)
                lses.append(m + jnp.log2(l))
            o2 = jnp.concatenate(outs, axis=0).T
            o32_ref[u * tq:(u + 1) * tq, :] = o2
            o_ref[u * tq:(u + 1) * tq, :] = o2.astype(BF)
            lse_ref[0, :, u * tq:(u + 1) * tq] = jnp.concatenate(lses, axis=0)

        @pl.when(step == n_steps - 1)
        def _():
            _gather_phase(2, *comm)

    return pl.pallas_call(
        body, name="fox_fwd", grid=(N_HEADS // 2, n_q),
        out_shape=(jax.ShapeDtypeStruct((t_len, GROUP_W), BF), jax.ShapeDtypeStruct((t_len, GROUP_W), F32),
                   jax.ShapeDtypeStruct((N_HEADS // 2, 2, t_len), F32))
        + tuple(jax.ShapeDtypeStruct((4,) + s.shape, s.dtype) for s in shards),
        in_specs=[pl.BlockSpec((tg, 2 * LANES), lambda hp, i: (i, hp)),
                  pl.BlockSpec((t_len, 2 * LANES), lambda hp, i: (0, hp)),
                  pl.BlockSpec((t_len, LANES), lambda hp, i: (0, v_col + hp))] + [ANY] * n_w,
        out_specs=(pl.BlockSpec((tg, LANES), lambda hp, i: (i, hp)), pl.BlockSpec((tg, LANES), lambda hp, i: (i, hp)),
                   pl.BlockSpec((1, 2, tg), lambda hp, i: (hp, 0, i))) + (ANY,) * n_w,
        scratch_shapes=[pltpu.VMEM((LANES, t_len), BF)] + _gather_scratch(n_w),
        compiler_params=_cparams(("arbitrary", "arbitrary")),
    )(q_aug, k_aug, proj, *shards)


def _softmax_rows(s):
    p = jnp.exp(s - jnp.max(s, axis=-1, keepdims=True))
    return p / jnp.sum(p, axis=-1, keepdims=True)


def _attn_out_xattn_fwd(x, mix_r, mix_f, w_out, g_xattn, w_xq, g_xq, kn, v, w_xo):
    t_len = x.shape[0]
    tm = min(ROW_TILE, t_len)

    def body(x_ref, mr_ref, mf_ref, wo_ref, g_ref, wq_ref, gq_ref, kn_ref, v_ref, wxo_ref,
             h1_ref, hn_ref, qx_ref, o_ref, h2_ref):
        h1 = x_ref[...] + jnp.dot(mr_ref[...], wo_ref[:GROUP_W, :], preferred_element_type=F32) \
            + jnp.dot(mf_ref[...], wo_ref[GROUP_W:, :], preferred_element_type=F32)
        h1_ref[...] = h1
        hn = _rms_fwd(h1, g_ref[...]).astype(BF)
        hn_ref[...] = hn
        qx = jnp.dot(hn, wq_ref[...], preferred_element_type=F32).astype(BF)
        qx_ref[...] = qx
        sls = [slice(h * XHD, (h + 1) * XHD) for h in range(N_XH)]
        qns = [_rms_fwd(qx[:, sl].astype(F32), gq_ref[...]).astype(BF) for sl in sls]
        logits = [_dot_nt(qn, kn_ref[:, sl]) * (XHD ** -0.5) for qn, sl in zip(qns, sls)]
        ps = [_softmax_rows(s).astype(BF) for s in logits]
        for p, sl in zip(ps, sls):
            o_ref[:, sl] = jnp.dot(p, v_ref[:, sl], preferred_element_type=F32).astype(BF)
        h2_ref[...] = h1 + jnp.dot(o_ref[...], wxo_ref[...], preferred_element_type=F32)

    row_spec = lambda w: pl.BlockSpec((tm, w), lambda i: (i, 0))
    full = lambda a: pl.BlockSpec(a.shape, lambda i: (0,) * a.ndim)
    return pl.pallas_call(
        body, name="attn_out_xattn_fwd", grid=(t_len // tm,),
        out_shape=(jax.ShapeDtypeStruct((t_len, D_MODEL), F32), jax.ShapeDtypeStruct((t_len, D_MODEL), BF),
                   jax.ShapeDtypeStruct((t_len, D_MODEL), BF), jax.ShapeDtypeStruct((t_len, D_MODEL), BF),
                   jax.ShapeDtypeStruct((t_len, D_MODEL), F32)),
        in_specs=[row_spec(D_MODEL), row_spec(GROUP_W), row_spec(GROUP_W), full(w_out), full(g_xattn), full(w_xq), full(g_xq),
                  full(kn), full(v), full(w_xo)],
        out_specs=(row_spec(D_MODEL),) * 5,
        compiler_params=_cparams(("arbitrary",)),
    )(x, mix_r, mix_f, w_out, g_xattn, w_xq, g_xq, kn, v, w_xo)


def _ffn_loss_fwd(h2, g_ffn, w_gate, w_up, w_down, target):
    t_len = h2.shape[0]
    tm = min(ROW_TILE, t_len)

    def body(h2_ref, g_ref, wg_ref, wu_ref, wd_ref, tgt_ref, hn_ref, gate_ref, up_ref, act_ref, dh3_ref, loss_ref):
        @pl.when(pl.program_id(0) == 0)
        def _():
            loss_ref[...] = jnp.zeros_like(loss_ref)

        h2v = h2_ref[...]
        hn = _rms_fwd(h2v, g_ref[...]).astype(BF)
        hn_ref[...] = hn
        gate = _dot_nt(hn, wg_ref[...])
        up = _dot_nt(hn, wu_ref[...])
        gate_ref[...] = gate.astype(BF)
        up_ref[...] = up.astype(BF)
        act = (gate * _sigmoid(gate) * up).astype(BF)
        act_ref[...] = act
        diff = h2v + jnp.dot(act, wd_ref[...], preferred_element_type=F32) - tgt_ref[...]
        dh3_ref[...] = diff * (1.0 / D_MODEL)
        per_row = jnp.sum(diff * diff, axis=-1, keepdims=True) * (1.0 / D_MODEL)
        loss_ref[...] += 0.5 * jnp.sum(per_row, axis=0, keepdims=True)

    row_spec = lambda w: pl.BlockSpec((tm, w), lambda i: (i, 0))
    full = lambda a: pl.BlockSpec(a.shape, lambda i: (0,) * a.ndim, pipeline_mode=pl.Buffered(1))
    return pl.pallas_call(
        body, name="ffn_loss_fwd", grid=(t_len // tm,),
        out_shape=(jax.ShapeDtypeStruct((t_len, D_MODEL), BF), jax.ShapeDtypeStruct((t_len, D_FF), BF),
                   jax.ShapeDtypeStruct((t_len, D_FF), BF), jax.ShapeDtypeStruct((t_len, D_FF), BF),
                   jax.ShapeDtypeStruct((t_len, D_MODEL), F32), jax.ShapeDtypeStruct((8, LANES), F32)),
        in_specs=[row_spec(D_MODEL), full(g_ffn), full(w_gate), full(w_up), full(w_down), row_spec(D_MODEL)],
        out_specs=(row_spec(D_MODEL), row_spec(D_FF), row_spec(D_FF), row_spec(D_FF), row_spec(D_MODEL),
                   pl.BlockSpec((8, LANES), lambda i: (0, 0))),
        compiler_params=_cparams(("arbitrary",)),
    )(h2, g_ffn, w_gate, w_up, w_down, target)


def _ffn_bwd(dh3, gate, up, h2, g_ffn, w_gate, w_up, w_down):
    t_len = h2.shape[0]
    tm = min(FFN_BWD_TILE, t_len)

    def body(dh3_ref, gate_ref, up_ref, h2_ref, g_ref, wg_ref, wu_ref, wd_ref, dgate_ref, dup_ref, dh2_ref, dg_ref):
        @pl.when(pl.program_id(0) == 0)
        def _():
            dg_ref[...] = jnp.zeros_like(dg_ref)

        dh3v = dh3_ref[...]
        dact = _dot_nt(dh3v, wd_ref[...])
        g = gate_ref[...].astype(F32)
        sg = _sigmoid(g)
        dup = (dact * (g * sg)).astype(BF)
        dgate = (dact * up_ref[...].astype(F32) * (sg * (1.0 + g * (1.0 - sg)))).astype(BF)
        dup_ref[...] = dup
        dgate_ref[...] = dgate
        dhn = jnp.dot(dgate, wg_ref[...], preferred_element_type=F32) + jnp.dot(dup, wu_ref[...], preferred_element_type=F32)
        dx, dg = _rms_bwd(h2_ref[...], g_ref[...], dhn)
        dh2_ref[...] = dh3v + dx
        dg_ref[...] += dg

    row_spec = lambda w: pl.BlockSpec((tm, w), lambda i: (i, 0))
    full = lambda a: pl.BlockSpec(a.shape, lambda i: (0,) * a.ndim, pipeline_mode=pl.Buffered(1))
    return pl.pallas_call(
        body, name="ffn_bwd", grid=(t_len // tm,),
        out_shape=(jax.ShapeDtypeStruct((t_len, D_FF), BF), jax.ShapeDtypeStruct((t_len, D_FF), BF),
                   jax.ShapeDtypeStruct((t_len, D_MODEL), F32), jax.ShapeDtypeStruct((1, D_MODEL), F32)),
        in_specs=[row_spec(D_MODEL), row_spec(D_FF), row_spec(D_FF), row_spec(D_MODEL), full(g_ffn), full(w_gate), full(w_up),
                  full(w_down)],
        out_specs=(row_spec(D_FF), row_spec(D_FF), row_spec(D_MODEL), pl.BlockSpec((1, D_MODEL), lambda i: (0, 0))),
        compiler_params=_cparams(("arbitrary",)),
    )(dh3, gate, up, h2, g_ffn, w_gate, w_up, w_down)


def _attn_out_xattn_bwd(dh2, h1, qx, kn, v, w_xo, w_xq, w_out, g_xattn, g_xq):
    t_len = h1.shape[0]
    tm = min(ROW_TILE, t_len)
    m_tok = kn.shape[0]

    def body(dh2_ref, h1_ref, qx_ref, kn_ref, v_ref, wxo_ref, wq_ref, wo_ref, g_ref, gq_ref,
             dqx_ref, dh1_ref, dmr_ref, dmf_ref, dkn_ref, dv_ref, dg_ref, dgq_ref, dqx_scr):
        @pl.when(pl.program_id(0) == 0)
        def _():
            dkn_ref[...] = jnp.zeros_like(dkn_ref)
            dv_ref[...] = jnp.zeros_like(dv_ref)
            dg_ref[...] = jnp.zeros_like(dg_ref)
            dgq_ref[...] = jnp.zeros_like(dgq_ref)

        dh2v = dh2_ref[...]
        do = _dot_nt(dh2v, wxo_ref[...])
        gq = gq_ref[...]
        sls = [slice(h * XHD, (h + 1) * XHD) for h in range(N_XH)]
        qraws = [qx_ref[:, sl].astype(F32) for sl in sls]
        qns = [_rms_fwd(qraw, gq).astype(BF) for qraw in qraws]
        dohs = [do[:, sl].astype(BF) for sl in sls]
        logits = [_dot_nt(qn, kn_ref[:, sl]) * (XHD ** -0.5) for qn, sl in zip(qns, sls)]
        dps = [_dot_nt(doh, v_ref[:, sl]) for doh, sl in zip(dohs, sls)]
        ps = [_softmax_rows(s) for s in logits]
        dss = [(p * (dp - jnp.sum(dp * p, axis=-1, keepdims=True)) * (XHD ** -0.5)).astype(BF) for p, dp in zip(ps, dps)]
        dqns = []
        for h, sl in enumerate(sls):
            dv_ref[:, sl] += _dot_tn(ps[h], dohs[h])
            dqns.append(jnp.dot(dss[h], kn_ref[:, sl], preferred_element_type=F32))
            dkn_ref[:, sl] += _dot_tn(dss[h], qns[h])
        dgq = jnp.zeros((1, XHD), F32)
        for h, sl in enumerate(sls):
            dx, dg_h = _rms_bwd(qraws[h], gq, dqns[h])
            dgq = dgq + dg_h
            dqx_scr[:, sl] = dx.astype(BF)
        dgq_ref[...] += dgq
        dqx = dqx_scr[...]
        dqx_ref[...] = dqx
        dhn = _dot_nt(dqx, wq_ref[...])
        dx, dg = _rms_bwd(h1_ref[...], g_ref[...], dhn)
        dg_ref[...] += dg
        dh1 = dh2v + dx
        dh1_ref[...] = dh1
        dmix = _dot_nt(dh1, wo_ref[...])
        dmr_ref[...] = dmix[:, :GROUP_W]
        dmf_ref[...] = dmix[:, GROUP_W:].astype(BF)

    row_spec = lambda w: pl.BlockSpec((tm, w), lambda i: (i, 0))
    full = lambda a: pl.BlockSpec(a.shape, lambda i: (0,) * a.ndim)
    acc = lambda r, c: pl.BlockSpec((r, c), lambda i: (0, 0))
    return pl.pallas_call(
        body, name="attn_out_xattn_bwd", grid=(t_len // tm,),
        out_shape=(jax.ShapeDtypeStruct((t_len, D_MODEL), BF), jax.ShapeDtypeStruct((t_len, D_MODEL), F32),
                   jax.ShapeDtypeStruct((t_len, GROUP_W), F32), jax.ShapeDtypeStruct((t_len, GROUP_W), BF),
                   jax.ShapeDtypeStruct((m_tok, D_MODEL), F32), jax.ShapeDtypeStruct((m_tok, D_MODEL), F32),
                   jax.ShapeDtypeStruct((1, D_MODEL), F32), jax.ShapeDtypeStruct((1, XHD), F32)),
        in_specs=[row_spec(D_MODEL), row_spec(D_MODEL), row_spec(D_MODEL), full(kn), full(v), full(w_xo), full(w_xq), full(w_out),
                  full(g_xattn), full(g_xq)],
        out_specs=(row_spec(D_MODEL), row_spec(D_MODEL), row_spec(GROUP_W), row_spec(GROUP_W), acc(m_tok, D_MODEL),
                   acc(m_tok, D_MODEL), acc(1, D_MODEL), acc(1, XHD)),
        scratch_shapes=[pltpu.VMEM((tm, D_MODEL), BF)],
        compiler_params=_cparams(("arbitrary",)),
    )(dh2, h1, qx, kn, v, w_xo, w_xq, w_out, g_xattn, g_xq)


def _mem_kv_bwd(dkn, dv, kraw, mem, memn, g_mem, g_xk, w_xkv):
    m_tok = mem.shape[0]

    def body(dkn_ref, dv_ref, kraw_ref, mem_ref, memn_ref, gm_ref, gk_ref, w_ref, dw_ref, dgm_ref, dgk_ref, dkv_scr):
        gk = gk_ref[...]
        dgk = jnp.zeros((1, XHD), F32)
        for h in range(N_XH):
            sl = slice(h * XHD, (h + 1) * XHD)
            dx, dg_h = _rms_bwd(kraw_ref[:, sl], gk, dkn_ref[:, sl])
            dgk = dgk + dg_h
            dkv_scr[:, sl] = dx.astype(BF)
        dgk_ref[...] = dgk
        dkv_scr[:, D_MODEL:] = dv_ref[...].astype(BF)
        dkv = dkv_scr[...]
        dw_ref[...] = _dot_tn(memn_ref[...], dkv)
        dmemn = _dot_nt(dkv, w_ref[...])
        mem_v = mem_ref[...]
        r = lax.rsqrt(jnp.mean(mem_v * mem_v, axis=-1, keepdims=True) + EPS)
        dgm_ref[...] = jnp.sum(dmemn * mem_v * r, axis=0, keepdims=True)

    return pl.pallas_call(
        body, name="mem_kv_bwd",
        out_shape=(jax.ShapeDtypeStruct((D_MODEL, 2 * D_MODEL), F32), jax.ShapeDtypeStruct((1, D_MODEL), F32),
                   jax.ShapeDtypeStruct((1, XHD), F32)),
        in_specs=[VMEM_SPEC] * 8, out_specs=(VMEM_SPEC,) * 3,
        scratch_shapes=[pltpu.VMEM((m_tok, 2 * D_MODEL), BF)],
        compiler_params=_cparams(),
    )(dkn, dv, kraw, mem, memn, g_mem, g_xk, w_xkv)


def _fox_bwd(q_aug, k_aug, proj, dmf, o32, lse, sums):
    t_len = q_aug.shape[0]
    tb = min(ATT_BLOCK, t_len)
    n_b = t_len // tb
    nsub = 2 if n_b >= 2 else 1
    tg = nsub * tb
    n_g = t_len // tg
    v_col = 6 * GROUP_W // LANES
    n_w = len(sums)
    n_steps = (N_HEADS // 2) * n_g

    def body(*refs):
        k_ref, v_ref, q_ref, do_ref, o_ref, lse_ref = refs[:6]
        dq_ref, dk_ref, dv_ref, df_ref = refs[6 + n_w:10 + n_w]
        delta = refs[10 + 2 * n_w]
        comm = (refs[6:6 + n_w], refs[10 + n_w:10 + 2 * n_w]) + tuple(refs[11 + 2 * n_w:])
        j = pl.program_id(1)
        step = pl.program_id(0) * n_g + j

        @pl.when(step == 0)
        def _():
            _scatter_phase(0, *comm)

        @pl.when(j == 0)
        def _():
            dq_ref[...] = jnp.zeros_like(dq_ref)
            dd = do_ref[...].astype(F32) * o_ref[...]
            hrow = lax.broadcasted_iota(jnp.int32, (8, LANES), 0)
            lane = lax.broadcasted_iota(jnp.int32, (8, LANES), 1)
            ind = ((lane // HEAD_DIM) == hrow).astype(BF)
            delta[...] = _dot_nt_exact(ind, dd)

        k2, v2 = k_ref[...], v_ref[...]
        chains = [(u, hh) for u in range(nsub) for hh in range(2)]
        ks = {(u, hh): k2[u * tb:(u + 1) * tb, hh * LANES:(hh + 1) * LANES] for u, hh in chains}
        vs = {(u, hh): v2[u * tb:(u + 1) * tb, hh * HEAD_DIM:(hh + 1) * HEAD_DIM] for u, hh in chains}

        def block(i, carry, which, masked):
            rows = pl.ds(pl.multiple_of(i * tb, tb), tb)
            q2 = q_ref[rows, :]
            do2 = do_ref[rows, :]
            qs = [q2[:, hh * LANES:(hh + 1) * LANES] for hh in range(2)]
            dos = [do2[:, hh * HEAD_DIM:(hh + 1) * HEAD_DIM] for hh in range(2)]
            ss = {ch: _dot_nt(ks[ch], qs[ch[1]]) for ch in which}
            dps = {ch: _dot_nt(vs[ch], dos[ch[1]]) for ch in which}
            pts, dsts, dfs = {}, {}, {}
            for ch in which:
                hh = ch[1]
                s_t = ss[ch]
                if ch in masked:
                    krow = lax.broadcasted_iota(jnp.int32, (tb, tb), 0)
                    qcol = lax.broadcasted_iota(jnp.int32, (tb, tb), 1)
                    s_t = jnp.where(qcol >= krow, s_t, NEG)
                p_t = jnp.exp2(s_t - lse_ref[0, hh:hh + 1, rows])
                pts[ch] = p_t.astype(BF)
                ds_t = p_t * (dps[ch] - delta[hh:hh + 1, rows])
                dsts[ch] = ds_t.astype(BF)
                dfs[ch] = jnp.sum(ds_t, axis=-1, keepdims=True)
            out = dict(carry)
            for ch in which:
                dk, dv, df = carry[ch]
                dv = dv + jnp.dot(pts[ch], dos[ch[1]], preferred_element_type=F32)
                dk = dk + jnp.dot(dsts[ch], qs[ch[1]], preferred_element_type=F32)
                out[ch] = (dk, dv, df - dfs[ch])
            for hh in range(2):
                parts_dq = [_dot_tn(dsts[ch], ks[ch])[:, :HEAD_DIM] for ch in which if ch[1] == hh]
                dq_ref[rows, hh * HEAD_DIM:(hh + 1) * HEAD_DIM] += sum(parts_dq[1:], parts_dq[0])
            return out

        init = {ch: (jnp.zeros((tb, LANES), F32), jnp.zeros((tb, HEAD_DIM), F32), jnp.zeros((tb, 1), F32)) for ch in chains}
        first = nsub * j
        carry = block(first, init, [(0, 0), (0, 1)], [(0, 0), (0, 1)])
        if nsub == 2:
            carry = block(first + 1, carry, chains, [(1, 0), (1, 1)])
        carry = lax.fori_loop(first + nsub, n_b, lambda i, c: block(i, c, chains, ()), carry)
        for u in range(nsub):
            rs = slice(u * tb, (u + 1) * tb)
            dk_ref[rs, :] = jnp.concatenate([carry[u, hh][0][:, :HEAD_DIM] for hh in range(2)], axis=-1) * LN2
            dv_ref[rs, :] = jnp.concatenate([carry[u, hh][1] for hh in range(2)], axis=-1)
            df_ref[0, rs, :] = jnp.concatenate([carry[u, hh][2] for hh in range(2)], axis=-1)

        @pl.when(step == n_steps - 1)
        def _():
            _scatter_phase(1, *comm)

    blk = lambda w, col0: pl.BlockSpec((tg, w), lambda hp, j: (j, col0 + hp))
    whole = lambda w: pl.BlockSpec((t_len, w), lambda hp, j: (0, hp))
    rows2 = pl.BlockSpec((1, 2, t_len), lambda hp, j: (hp, 0, 0))
    cols2 = pl.BlockSpec((1, tg, 2), lambda hp, j: (hp, j, 0))
    return pl.pallas_call(
        body, name="fox_bwd", grid=(N_HEADS // 2, n_g),
        out_shape=(jax.ShapeDtypeStruct((t_len, GROUP_W), F32), jax.ShapeDtypeStruct((t_len, GROUP_W), F32),
                   jax.ShapeDtypeStruct((t_len, GROUP_W), F32), jax.ShapeDtypeStruct((N_HEADS // 2, t_len, 2), F32))
        + _scatter_out_shapes(sums),
        in_specs=[blk(2 * LANES, 0), blk(LANES, v_col), whole(2 * LANES), whole(LANES), whole(LANES), rows2] + [ANY] * n_w,
        out_specs=(whole(LANES), blk(LANES, 0), blk(LANES, 0), cols2) + (ANY,) * n_w,
        scratch_shapes=[pltpu.VMEM((8, t_len), F32)] + _scatter_scratch(n_w),
        compiler_params=_cparams(("arbitrary", "arbitrary")),
    )(k_aug, proj, q_aug, dmf, o32, lse, *sums)


def _retention_bwd(dmr, raw, proj, g_ret, rq, rk, states, tables, parts):
    t_len = rq.shape[0]
    c = min(RET_BLOCK, t_len)
    n_b = t_len // c
    wdec, qdec, kdec, cdec = tables
    v_col, g_col = 2 * GROUP_W // LANES, 3 * GROUP_W // LANES
    n_w = len(parts)
    n_steps = (N_HEADS // 2) * n_b

    def body(*refs):
        d_ref, raw_ref, rg_ref, g_ref, q_ref, k_ref, v_ref, st_ref, w_ref, wt_ref, qd_ref, kd_ref, cd_ref = refs[:13]
        dq_ref, dk_ref, dv_ref, drg_ref, dg_ref = refs[13 + n_w:18 + n_w]
        gstate = refs[18 + 2 * n_w]
        comm = (refs[13:13 + n_w], refs[18 + n_w:18 + 2 * n_w]) + tuple(refs[19 + 2 * n_w:])
        step = pl.program_id(0) * n_b + pl.program_id(1)

        @pl.when(step == 0)
        def _():
            _exchange_phase(0, *comm)

        @pl.when(pl.program_id(1) == 0)
        def _():
            gstate[...] = jnp.zeros_like(gstate)
            dg_ref[...] = jnp.zeros_like(dg_ref)

        d, raw_v, g = d_ref[...], raw_ref[...], g_ref[0]
        gate = rg_ref[...].astype(F32)
        xc = raw_v - _group_mean64(raw_v)
        r = lax.rsqrt(_group_mean64(xc * xc) + EPS)
        xh = xc * r
        sg = _sigmoid(gate)
        drg_ref[...] = d * (xh * g) * (sg * (1.0 + gate * (1.0 - sg)))
        dy = d * (gate * sg)
        dg_ref[0] += jnp.sum(dy * xh, axis=0, keepdims=True)
        dxh = dy * g
        do2 = r * (dxh - _group_mean64(dxh) - xh * _group_mean64(dxh * xh))
        q2, k2, v2 = q_ref[...], k_ref[...], v_ref[...]
        dqs, dks, dvs = [], [], []
        heads = [tuple(t[:, hh * HEAD_DIM:(hh + 1) * HEAD_DIM] for t in (q2, k2, v2, do2.astype(BF))) for hh in range(2)]
        firsts = [(_dot_nt(k, q) * wt_ref[hh], _dot_nt(do, v) * w_ref[hh], _dot_nt(v, do) * wt_ref[hh])
                  for hh, (q, k, v, do) in enumerate(heads)]
        for hh, (q, k, v, do) in enumerate(heads):
            a_t, dm, dm_t = firsts[hh]
            sp, gs = st_ref[0, 0, hh], gstate[hh]
            qd = q.astype(F32) * qd_ref[hh]
            kd = k.astype(F32) * kd_ref[hh]
            dqs.append(_dot(dm, k) + _dot_nt(do, sp) * qd_ref[hh])
            dks.append(_dot(dm_t, q) + _dot_nt(v, gs) * kd_ref[hh])
            dvs.append(_dot(a_t, do) + _dot(kd, gs))
            gstate[hh] = gs * cd_ref[hh] + _dot_tn(qd, do)
        dq_ref[...] = jnp.concatenate(dqs, axis=-1)
        dk_ref[...] = jnp.concatenate(dks, axis=-1)
        dv_ref[...] = jnp.concatenate(dvs, axis=-1)

        @pl.when(step == n_steps - 1)
        def _():
            _exchange_phase(1, *comm)

    blk = lambda col0: pl.BlockSpec((c, LANES), lambda hp, i: (n_b - 1 - i, col0 + hp))
    tab = lambda a: pl.BlockSpec((2,) + a.shape[1:], lambda hp, i: (hp, 0, 0))
    gspec = pl.BlockSpec((1, 1, LANES), lambda hp, i: (hp, 0, 0))
    return pl.pallas_call(
        body, name="retention_bwd", grid=(N_HEADS // 2, n_b),
        out_shape=(jax.ShapeDtypeStruct((t_len, GROUP_W), F32),) * 4 + (jax.ShapeDtypeStruct((N_HEADS // 2, 1, LANES), F32),)
        + _exchange_out_shapes(parts),
        in_specs=[blk(0), blk(0), blk(g_col), gspec, blk(0), blk(0), blk(v_col),
                  pl.BlockSpec((1, 1, 2, HEAD_DIM, HEAD_DIM), lambda hp, i: (hp, n_b - 1 - i, 0, 0, 0)),
                  tab(wdec), tab(wdec), tab(qdec), tab(kdec), tab(cdec)] + [ANY] * n_w,
        out_specs=(blk(0), blk(0), blk(0), blk(0), gspec) + (ANY,) * n_w,
        scratch_shapes=[pltpu.VMEM((2, HEAD_DIM, HEAD_DIM), F32)] + _exchange_scratch(n_w),
        compiler_params=_cparams(("arbitrary", "arbitrary")),
    )(dmr, raw, proj, g_ret, rq, rk, proj, states, wdec, jnp.transpose(wdec, (0, 2, 1)), qdec, kdec, cdec, *parts)


def _in_proj_bwd(x, g_mix, dh1, dq_r, dk_r, dv_r, drg, dq_f, dk_f, dv_f, df_col, proj, z, cos_t, sin_t, gq_t, gk_t, w_in_t):
    t_len = x.shape[0]
    tm = min(ROW_TILE, t_len)
    n_t = t_len // tm

    def body(x_ref, g_ref, dh1_ref, dqr_ref, dkr_ref, dvr_ref, drg_ref, dqf_ref, dkf_ref, dvf_ref, df_ref, fq_ref, fk_ref, z_ref,
             cos_ref, sin_ref, gq_ref, gk_ref, wm_ref, wf_ref,
             dproj_ref, dz_ref, dx_ref, dg_ref, dgq_ref, dgk_ref, db_ref, carry, gq_acc, gk_acc):
        i = pl.program_id(0)

        @pl.when(i == 0)
        def _():
            carry[...] = jnp.zeros_like(carry)
            gq_acc[...] = jnp.zeros_like(gq_acc)
            gk_acc[...] = jnp.zeros_like(gk_acc)
            dg_ref[...] = jnp.zeros_like(dg_ref)
            db_ref[...] = jnp.zeros_like(db_ref)

        c, s = cos_ref[...], sin_ref[...]
        gq, gk = gq_ref[...], gk_ref[...]
        dgq = jnp.zeros((1, LANES), F32)
        dgk = jnp.zeros((1, LANES), F32)
        for sl in _chunks(GROUP_W):
            dy = dqr_ref[:, sl] * 0.125
            dproj_ref[:, sl] = (dy * c + _swap32(dy * s)).astype(BF)
            dy = dkr_ref[:, sl]
            dproj_ref[:, GROUP_W + sl.start:GROUP_W + sl.stop] = (dy * c + _swap32(dy * s)).astype(BF)
            dproj_ref[:, 2 * GROUP_W + sl.start:2 * GROUP_W + sl.stop] = dvr_ref[:, sl].astype(BF)
            dproj_ref[:, 3 * GROUP_W + sl.start:3 * GROUP_W + sl.stop] = drg_ref[:, sl].astype(BF)
            for src, dsrc, gain, off in ((fq_ref, dqf_ref, gq, 4), (fk_ref, dkf_ref, gk, 5)):
                xr = src[:, sl].astype(F32)
                r = lax.rsqrt(_group_mean64(xr * xr) + EPS)
                xh = xr * r
                dy = dsrc[:, sl] * (0.125 if off == 4 else 1.0)
                dgs = jnp.sum(dy * xh, axis=0, keepdims=True)
                if off == 4:
                    dgq = dgq + dgs
                else:
                    dgk = dgk + dgs
                dxh = dy * gain
                dproj_ref[:, off * GROUP_W + sl.start:off * GROUP_W + sl.stop] = \
                    (r * (dxh - xh * _group_mean64(dxh * xh))).astype(BF)
            dproj_ref[:, 6 * GROUP_W + sl.start:6 * GROUP_W + sl.stop] = dvf_ref[:, sl].astype(BF)
        gq_acc[...] += dgq
        gk_acc[...] += dgk
        row = lax.broadcasted_iota(jnp.int32, (tm, tm), 0)
        col = lax.broadcasted_iota(jnp.int32, (tm, tm), 1)
        dlf = _dot_exact((col >= row).astype(BF), df_ref[...]) + carry[0:1, :]
        carry[...] = jnp.broadcast_to(dlf[0:1, :], carry.shape)
        lane = lax.broadcasted_iota(jnp.int32, (tm, LANES), 1)
        dz = jnp.where(lane < N_HEADS, dlf / (1.0 + jnp.exp(z_ref[...])), 0.0)
        db_ref[...] += jnp.sum(dz, axis=0, keepdims=True)
        dz_bf = dz.astype(BF)
        dz_ref[...] = dz_bf
        dn1 = jnp.dot(dz_bf, wf_ref[...], preferred_element_type=F32)
        for sec in range(MAIN_W // GROUP_W):
            sl = slice(sec * GROUP_W, (sec + 1) * GROUP_W)
            dn1 = dn1 + jnp.dot(dproj_ref[:, sl], wm_ref[sl, :], preferred_element_type=F32)
        dx, dg = _rms_bwd(x_ref[...], g_ref[...], dn1)
        dx_ref[...] = dh1_ref[...] + dx
        dg_ref[...] += dg

        @pl.when(i == n_t - 1)
        def _():
            dgq_ref[...] = gq_acc[:, :HEAD_DIM] + gq_acc[:, HEAD_DIM:]
            dgk_ref[...] = gk_acc[:, :HEAD_DIM] + gk_acc[:, HEAD_DIM:]

    row_spec = lambda w, col=0: pl.BlockSpec((tm, w), lambda i: (n_t - 1 - i, col))
    full = lambda a: pl.BlockSpec(a.shape, lambda i: (0,) * a.ndim)
    acc = lambda r, c: pl.BlockSpec((r, c), lambda i: (0, 0))
    return pl.pallas_call(
        body, name="in_proj_bwd", grid=(n_t,),
        out_shape=(jax.ShapeDtypeStruct((t_len, MAIN_W), BF), jax.ShapeDtypeStruct((t_len, LANES), BF),
                   jax.ShapeDtypeStruct((t_len, D_MODEL), F32), jax.ShapeDtypeStruct((1, D_MODEL), F32),
                   jax.ShapeDtypeStruct((1, HEAD_DIM), F32), jax.ShapeDtypeStruct((1, HEAD_DIM), F32),
                   jax.ShapeDtypeStruct((1, LANES), F32)),
        in_specs=[row_spec(D_MODEL), full(g_mix), row_spec(D_MODEL)] + [row_spec(GROUP_W)] * 7
        + [row_spec(LANES), row_spec(GROUP_W, 4), row_spec(GROUP_W, 5), row_spec(LANES), row_spec(LANES), row_spec(LANES),
           full(gq_t), full(gk_t), *_w_in_specs()],
        out_specs=(row_spec(MAIN_W), row_spec(LANES), row_spec(D_MODEL), acc(1, D_MODEL), acc(1, HEAD_DIM), acc(1, HEAD_DIM),
                   acc(1, LANES)),
        scratch_shapes=[pltpu.VMEM((8, LANES), F32), pltpu.VMEM((1, LANES), F32), pltpu.VMEM((1, LANES), F32)],
        compiler_params=_cparams(("arbitrary",)),
    )(x, g_mix, dh1, dq_r, dk_r, dv_r, drg, dq_f, dk_f, dv_f, df_col, proj, proj, z, cos_t, sin_t, gq_t, gk_t, w_in_t, w_in_t)


def _matmul_tn(a, b, name, bk=1024):
    t_len, m = a.shape
    n = b.shape[1]
    bm = m if m <= TN_MAX_ROWS else m // 2
    bk = min(bk, t_len)

    def body(a_ref, b_ref, o_ref):
        @pl.when(pl.program_id(1) == 0)
        def _():
            o_ref[...] = jnp.zeros_like(o_ref)

        o_ref[...] += _dot_tn(a_ref[...], b_ref[...])

    return pl.pallas_call(
        body, name=name, grid=(m // bm, t_len // bk),
        out_shape=jax.ShapeDtypeStruct((m, n), F32),
        in_specs=[pl.BlockSpec((bk, bm), lambda i, k: (k, i)), pl.BlockSpec((bk, n), lambda i, k: (k, 0))],
        out_specs=pl.BlockSpec((bm, n), lambda i, k: (i, 0)),
        compiler_params=_cparams(("arbitrary", "arbitrary")),
    )(a, b)


def _place():
    x, y, c = lax.axis_index("x"), lax.axis_index("y"), lax.axis_index("c")
    chips = [(1 - x, y), (x, 1 - y), (1 - x, 1 - y)]
    return x, y, c, chips


def _row_chunks(rows, limit):
    step = max(d for d in range(16, min(rows, limit) + 1, 16) if rows % d == 0)
    return [slice(i, i + step) for i in range(0, rows, step)]


ICI_CHUNK_ROWS = 128
D2D_CHUNK_ROWS = 64


def _gather_phase(phase, ins, outs, send_sems, recv_sems):
    x, y, c, chips = _place()
    me_chip = 2 * x + y
    sibling = (x, y, 1 - c)

    def copy(w, k, slot, half, to, rows=slice(None), src=None):
        dst = outs[w].at[slot, half, rows]
        return pltpu.make_async_remote_copy(src_ref=dst if src is None else src, dst_ref=dst,
                                            send_sem=send_sems.at[w, k], recv_sem=recv_sems.at[w, k],
                                            device_id=to, device_id_type=MESH)

    for w in range(len(ins)):
        for j, (px, py) in enumerate(chips):
            if phase == 0:
                for rows in _row_chunks(ins[w].shape[1], ICI_CHUNK_ROWS):
                    copy(w, j, me_chip, c, (px, py, c), rows, src=ins[w].at[c, rows]).start()
            elif phase == 1:
                copy(w, j, 2 * px + py, c, (x, y, c)).wait_recv()
                for rows in _row_chunks(ins[w].shape[1], D2D_CHUNK_ROWS):
                    copy(w, 3 + j, 2 * px + py, c, sibling, rows).start()
            else:
                copy(w, 3 + j, 2 * px + py, 1 - c, (x, y, c)).wait_recv()
                copy(w, j, me_chip, c, (px, py, c), src=ins[w].at[c]).wait_send()
                copy(w, 3 + j, 2 * px + py, c, sibling).wait_send()


def _gather_scratch(n_w):
    return [pltpu.SemaphoreType.DMA((n_w, 6)), pltpu.SemaphoreType.DMA((n_w, 6))]


def _all_gather_weights(shards):
    n_w = len(shards)

    def body(*refs):
        for phase in range(3):
            _gather_phase(phase, refs[:n_w], refs[n_w:2 * n_w], *refs[2 * n_w:])

    return pl.pallas_call(
        body, name="all_gather_weights",
        out_shape=tuple(jax.ShapeDtypeStruct((4,) + s.shape, s.dtype) for s in shards),
        in_specs=[ANY] * n_w, out_specs=(ANY,) * n_w, scratch_shapes=_gather_scratch(n_w),
    )(*shards)


def _exchange_phase(phase, ins, theirs, send_sems, recv_sems):
    x, y, c, _ = _place()

    def remote(w, k=slice(None), rows=slice(None)):
        return pltpu.make_async_remote_copy(src_ref=ins[w].at[k, 1 - c, rows], dst_ref=theirs[w].at[k, rows],
                                            send_sem=send_sems.at[w], recv_sem=recv_sems.at[w], device_id=(x, y, 1 - c),
                                            device_id_type=MESH)

    for w in range(len(ins)):
        if phase == 0:
            for k in range(4):
                for rows in _row_chunks(ins[w].shape[2], D2D_CHUNK_ROWS):
                    remote(w, k, rows).start()
        else:
            remote(w).wait()


def _exchange_scratch(n_w):
    return [pltpu.SemaphoreType.DMA((n_w,)), pltpu.SemaphoreType.DMA((n_w,))]


def _exchange_out_shapes(grads):
    return tuple(jax.ShapeDtypeStruct((4,) + g.shape[2:], g.dtype) for g in grads)


def _add_pairs(part, theirs, name, halves):
    _, _, r, c = part.shape
    rb = 32 if r % 32 == 0 else r
    n_w = len(halves)
    n_steps = r // rb

    def body(*refs):
        a_ref, b_ref = refs[:2]
        own_ref, ob_ref = refs[2 + n_w:4 + n_w]
        comm = (refs[2:2 + n_w], refs[4 + n_w:4 + 2 * n_w]) + tuple(refs[4 + 2 * n_w:])
        step = pl.program_id(0)

        @pl.when(step == 0)
        def _():
            _share_phase(0, *comm)

        my_chip = 2 * lax.axis_index("x") + lax.axis_index("y")
        ob_ref[...] = (a_ref[...] + b_ref[...]).astype(BF)
        own_ref[...] = a_ref[my_chip] + b_ref[my_chip]

        @pl.when(step == n_steps - 1)
        def _():
            _share_phase(1, *comm)

    spec = pl.BlockSpec((4, rb, c), lambda i: (0, i, 0))
    flat = pl.pallas_call(
        body, name=name, grid=(n_steps,),
        out_shape=(jax.ShapeDtypeStruct((r, c), F32), jax.ShapeDtypeStruct((4, r, c), BF)) + _share_out_shapes(halves),
        in_specs=[pl.BlockSpec((4, None, rb, c), lambda i: (0, lax.axis_index("c"), i, 0)), spec] + [ANY] * n_w,
        out_specs=(pl.BlockSpec((rb, c), lambda i: (i, 0)), spec) + (ANY,) * n_w,
        scratch_shapes=_share_scratch(n_w), compiler_params=_cparams(("arbitrary",)),
    )(part, theirs, *halves)
    return (flat[0], flat[1]), list(flat[2:])


def _scatter_phase(phase, bfs, got, send_sems, recv_sems):
    x, y, c, chips = _place()

    def remote(w, j, px, py, rows=slice(None)):
        return pltpu.make_async_remote_copy(src_ref=bfs[w].at[2 * px + py, rows], dst_ref=got[w].at[j, rows],
                                            send_sem=send_sems.at[w, j], recv_sem=recv_sems.at[w, j], device_id=(px, py, c),
                                            device_id_type=MESH)

    for w in range(len(bfs)):
        for j, (px, py) in enumerate(chips):
            if phase == 0:
                for rows in _row_chunks(bfs[w].shape[1], ICI_CHUNK_ROWS):
                    remote(w, j, px, py, rows).start()
            else:
                remote(w, j, px, py).wait()


def _scatter_scratch(n_w):
    return [pltpu.SemaphoreType.DMA((n_w, 3)), pltpu.SemaphoreType.DMA((n_w, 3))]


def _scatter_out_shapes(sums_bf16):
    return tuple(jax.ShapeDtypeStruct((3,) + s.shape[1:], BF) for s in sums_bf16)


def _add_received(own, got, name):
    r, c = own.shape
    rb = 32 if r % 32 == 0 else r

    def body(o_ref, g_ref, out_ref):
        out_ref[...] = ((o_ref[...] + g_ref[0].astype(F32)) + g_ref[1].astype(F32)) + g_ref[2].astype(F32)

    return pl.pallas_call(
        body, name=name, grid=(r // rb,), out_shape=jax.ShapeDtypeStruct((r, c), F32),
        in_specs=[pl.BlockSpec((rb, c), lambda i: (i, 0)), pl.BlockSpec((3, rb, c), lambda i: (0, i, 0))],
        out_specs=pl.BlockSpec((rb, c), lambda i: (i, 0)), compiler_params=_cparams(("arbitrary",)),
    )(own, got)


def _share_phase(phase, ins, outs, send_sems, recv_sems):
    x, y, c, _ = _place()

    def remote(w, rows=slice(None)):
        return pltpu.make_async_remote_copy(src_ref=ins[w].at[rows], dst_ref=outs[w].at[c, rows], send_sem=send_sems.at[w],
                                            recv_sem=recv_sems.at[w], device_id=(x, y, 1 - c), device_id_type=MESH)

    for w in range(len(ins)):
        if phase == 0:
            for rows in _row_chunks(ins[w].shape[0], D2D_CHUNK_ROWS):
                remote(w, rows).start()
        else:
            remote(w).wait()


def _share_scratch(n_w):
    return [pltpu.SemaphoreType.DMA((n_w,)), pltpu.SemaphoreType.DMA((n_w,))]


def _share_out_shapes(halves):
    return tuple(jax.ShapeDtypeStruct((2,) + h.shape, h.dtype) for h in halves)


def _share_with_sibling(halves):
    n_w = len(halves)

    def body(*refs):
        for phase in range(2):
            _share_phase(phase, refs[:n_w], refs[n_w:2 * n_w], *refs[2 * n_w:])

    return pl.pallas_call(
        body, name="share_with_sibling", out_shape=_share_out_shapes(halves),
        in_specs=[ANY] * n_w, out_specs=(ANY,) * n_w, scratch_shapes=_share_scratch(n_w),
    )(*halves)


def _small_phase(phase, p_ref, out_ref, slots, send_sems, recv_sems):
    x, y, cc, _ = _place()
    me = 4 * x + 2 * y + cc
    copies = []
    for k in range(1, 8):
        dx, dy, dc = (k >> 2) & 1, (k >> 1) & 1, k & 1
        to = (1 - x if dx else x, 1 - y if dy else y, 1 - cc if dc else cc)
        copies.append(pltpu.make_async_remote_copy(src_ref=p_ref, dst_ref=slots.at[me], send_sem=send_sems.at[k - 1],
                                                   recv_sem=recv_sems.at[k - 1], device_id=to, device_id_type=MESH))
    if phase == 0:
        slots[me] = p_ref[...]
        for cp in copies:
            cp.start()
    else:
        for cp in copies:
            cp.wait()
        total = slots[0]
        for d in range(1, 8):
            total = total + slots[d]
        out_ref[...] = total


def _adamw_update(w_ref, g_ref, m_ref, v_ref, d_ref, nm_ref, nv_ref):
    gv = g_ref[...]
    nm = ADAM_B1 * m_ref[...] + (1.0 - ADAM_B1) * gv
    nv = ADAM_B2 * v_ref[...] + (1.0 - ADAM_B2) * (gv * gv)
    nm_ref[...] = nm
    nv_ref[...] = nv
    m_hat = nm / (1.0 - ADAM_B1 ** ADAM_STEP)
    v_hat = nv / (1.0 - ADAM_B2 ** ADAM_STEP)
    d_ref[...] = -ADAM_LR * (m_hat / (jnp.sqrt(v_hat) + ADAM_EPS) + ADAM_WD * w_ref[...])


def _adamw_many(ws, gs, ms, vs, sums, pack):
    n_a, n_w = len(ws), len(sums)
    n_steps = ADAM_STEPS
    specs = [pl.BlockSpec((w.shape[0] // n_steps, w.shape[1]), lambda i: (i, 0)) for w in ws]
    pack_spec = pl.BlockSpec(pack.shape, lambda i: (0, 0))

    def body(*refs):
        ins = refs[:4 * n_a]
        p_ref = refs[4 * n_a + n_w]
        first_out = 4 * n_a + n_w + 1
        outs = refs[first_out:first_out + 3 * n_a]
        total_ref = refs[first_out + 3 * n_a + n_w]
        scratch = refs[first_out + 3 * n_a + n_w + 1:]
        scatter = (refs[4 * n_a:4 * n_a + n_w], refs[first_out + 3 * n_a:first_out + 3 * n_a + n_w]) + tuple(scratch[:2])
        small = (p_ref, total_ref) + tuple(scratch[2:])
        step = pl.program_id(0)

        @pl.when(step == 0)
        def _():
            _scatter_phase(0, *scatter)
            _small_phase(0, *small)

        for a in range(n_a):
            _adamw_update(*(ins[k * n_a + a] for k in range(4)), *(outs[3 * a + k] for k in range(3)))

        @pl.when(step == n_steps - 1)
        def _():
            _scatter_phase(1, *scatter)
            _small_phase(1, *small)

    flat = pl.pallas_call(
        body, name="adamw_late", grid=(n_steps,),
        out_shape=tuple(jax.ShapeDtypeStruct(w.shape, F32) for w in ws for _ in range(3)) + _scatter_out_shapes(sums)
        + (jax.ShapeDtypeStruct(pack.shape, F32),),
        in_specs=specs * 4 + [ANY] * n_w + [pack_spec],
        out_specs=tuple(s for s in specs for _ in range(3)) + (ANY,) * n_w + (pack_spec,),
        scratch_shapes=_scatter_scratch(n_w) + [pltpu.VMEM((8,) + pack.shape, F32), pltpu.SemaphoreType.DMA((7,)),
                                                pltpu.SemaphoreType.DMA((7,))],
        compiler_params=_cparams(("arbitrary",)),
    )(*ws, *gs, *ms, *vs, *sums, pack)
    return [tuple(flat[3 * a:3 * a + 3]) for a in range(n_a)] + list(flat[3 * n_a:])


def _adamw(w, g, m, v, name):
    r, c = w.shape
    rb, cb = (64, c) if r % 64 == 0 else (r, LANES if (r % 8 and c % LANES == 0) else c)

    def body(*refs):
        _adamw_update(*refs)

    spec = pl.BlockSpec((rb, cb), lambda i, j: (i, j))
    return pl.pallas_call(
        body, name=name, grid=(r // rb, c // cb), out_shape=(jax.ShapeDtypeStruct((r, c), F32),) * 3,
        in_specs=[spec] * 4, out_specs=(spec,) * 3, compiler_params=_cparams(("arbitrary", "arbitrary")),
    )(w, g, m, v)


def _rope_tables(t_len):
    inv_freq = ROPE_BASE ** (-jnp.arange(0, HEAD_DIM, 2, dtype=F32) / HEAD_DIM)
    ang = jnp.arange(t_len, dtype=F32)[:, None] * inv_freq[None, :]
    cos, sin = jnp.cos(ang), jnp.sin(ang)
    cos_t = jnp.concatenate([cos, cos, cos, cos], axis=-1)
    sin_t = jnp.concatenate([-sin, sin, -sin, sin], axis=-1)
    return cos_t, sin_t


def _cols_to_shards(dw):
    r, n = dw.shape
    return jnp.transpose(dw.reshape(2, r // 2, 4, n // 4), (2, 0, 1, 3))


def _rows_to_shards(dw):
    r, n = dw.shape
    rows = r // 4
    if rows % SUBLANES == 0:
        padded = _pad_rows(dw.reshape(4, rows, n))
    else:
        window = rows + SUBLANES - rows % SUBLANES
        padded = _pad_rows(jnp.stack([dw[rows * k // SUBLANES * SUBLANES:][:window] for k in range(4)]))
    return padded.reshape(4, 2, padded.shape[1] // 2, n)


def _shard_row_offset(rows):
    return (rows * (2 * lax.axis_index("x") + lax.axis_index("y"))) % SUBLANES


def _pad_lanes(a):
    extra = -a.shape[-1] % LANES
    return a if extra == 0 else jnp.pad(a, [(0, 0)] * (a.ndim - 1) + [(0, extra)])


def _pad_rows(a):
    rows = a.shape[-2]
    extra = 0 if rows % SHARD_ROW_ALIGN == 0 else -rows % SHARD_ROW_PAD
    return a if extra == 0 else jnp.pad(a, [(0, 0)] * (a.ndim - 2) + [(0, extra), (0, 0)])


def _pad_row(a, width=D_MODEL):
    a = a.reshape(1, -1)
    return jnp.pad(a, ((0, 0), (0, width - a.shape[1])))


def kernel(x, mem, g_mix, w_in, b_forget, g_ret_out, g_fox_q, g_fox_k, w_out, g_xattn, w_xq, w_xkv, g_mem, g_xq, g_xk, w_xo, g_ffn, w_gate, w_up, w_down, loss_target, m_g_mix, m_w_in, m_b_forget, m_g_ret_out, m_g_fox_q, m_g_fox_k, m_w_out, m_g_xattn, m_w_xq, m_w_xkv, m_g_mem, m_g_xq, m_g_xk, m_w_xo, m_g_ffn, m_w_gate, m_w_up, m_w_down, v_g_mix, v_w_in, v_b_forget, v_g_ret_out, v_g_fox_q, v_g_fox_k, v_w_out, v_g_xattn, v_w_xq, v_w_xkv, v_g_mem, v_g_xq, v_g_xk, v_w_xo, v_g_ffn, v_w_gate, v_w_up, v_w_down):
    big = {"w_in": (w_in, m_w_in, v_w_in), "w_out": (w_out, m_w_out, v_w_out), "w_xq": (w_xq, m_w_xq, v_w_xq),
           "w_xkv": (w_xkv, m_w_xkv, v_w_xkv), "w_xo": (w_xo, m_w_xo, v_w_xo), "w_gate": (w_gate, m_w_gate, v_w_gate),
           "w_up": (w_up, m_w_up, v_w_up), "w_down": (w_down, m_w_down, v_w_down)}
    for n in TRANSPOSED:
        big[n] = tuple(jnp.swapaxes(a, 1, 2) for a in big[n])
    shards = {}
    for n in big:
        w = _pad_rows(_pad_lanes(big[n][0][0].astype(BF)))
        shards[n] = w.reshape(2, w.shape[0] // 2, w.shape[1])
    sizes = {n: big[n][0].shape[1:] for n in big}
    w_in_full = _assemble_weight("w_in", _all_gather_weights([shards["w_in"]])[0], shards["w_in"], sizes["w_in"])
    small_w ={"g_mix": g_mix, "b_forget": b_forget, "g_ret_out": g_ret_out, "g_fox_q": g_fox_q, "g_fox_k": g_fox_k,
               "g_xattn": g_xattn, "g_mem": g_mem, "g_xq": g_xq, "g_xk": g_xk, "g_ffn": g_ffn}
    m_small = {"g_mix": m_g_mix, "b_forget": m_b_forget, "g_ret_out": m_g_ret_out, "g_fox_q": m_g_fox_q, "g_fox_k": m_g_fox_k,
               "g_xattn": m_g_xattn, "g_mem": m_g_mem, "g_xq": m_g_xq, "g_xk": m_g_xk, "g_ffn": m_g_ffn}
    v_small = {"g_mix": v_g_mix, "b_forget": v_b_forget, "g_ret_out": v_g_ret_out, "g_fox_q": v_g_fox_q, "g_fox_k": v_g_fox_k,
               "g_xattn": v_g_xattn, "g_mem": v_g_mem, "g_xq": v_g_xq, "g_xk": v_g_xk, "g_ffn": v_g_ffn}
    loss_part, grad_x, sums, got, in_parts, small_g = _local_step(x[0], mem[0], loss_target[0], w_in_full, shards, sizes, small_w)
    return _reduce_and_update(big, sums, got, in_parts, small_w, small_g, loss_part, grad_x, m_small, v_small)


def _assemble_weight(name, gathered, own, size):
    rows, width = size
    my_chip = 2 * lax.axis_index("x") + lax.axis_index("y")
    g = lax.dynamic_update_slice(gathered, own[None], (my_chip, 0, 0, 0))
    g = g.reshape(4, 2 * g.shape[2], g.shape[3])[:, :rows, :width]
    return jnp.transpose(g, (1, 0, 2)).reshape(rows, 4 * width) if name in COL_SHARDED else g.reshape(4 * rows, width)


def _shard_parts(names, dw):
    return [_pad_lanes(_cols_to_shards(dw[n]) if n in COL_SHARDED else _rows_to_shards(dw[n])) for n in names]


def _add_pairs_many(parts, theirs, name):
    n_a = len(parts)
    n_steps = min(p.shape[2] for p in parts) // 32
    rb = [p.shape[2] // n_steps for p in parts]
    part_specs = [pl.BlockSpec((4, None, r, p.shape[3]), lambda i: (0, lax.axis_index("c"), i, 0)) for p, r in zip(parts, rb)]
    quad_specs = [pl.BlockSpec((4, r, p.shape[3]), lambda i: (0, i, 0)) for p, r in zip(parts, rb)]
    own_specs = [pl.BlockSpec((r, p.shape[3]), lambda i: (i, 0)) for p, r in zip(parts, rb)]

    def body(*refs):
        my_chip = 2 * lax.axis_index("x") + lax.axis_index("y")
        for a in range(n_a):
            a_ref, b_ref, own_ref, ob_ref = refs[a], refs[n_a + a], refs[2 * n_a + a], refs[3 * n_a + a]
            ob_ref[...] = (a_ref[...] + b_ref[...]).astype(BF)
            own_ref[...] = a_ref[my_chip] + b_ref[my_chip]

    flat = pl.pallas_call(
        body, name=name, grid=(n_steps,),
        out_shape=tuple(jax.ShapeDtypeStruct(p.shape[2:], F32) for p in parts)
        + tuple(jax.ShapeDtypeStruct((4,) + p.shape[2:], BF) for p in parts),
        in_specs=part_specs + quad_specs, out_specs=tuple(own_specs) + tuple(quad_specs),
        compiler_params=_cparams(("arbitrary",)),
    )(*parts, *theirs)
    return [(flat[a], flat[n_a + a]) for a in range(n_a)]


def _core_sums(parts, theirs):
    out = [None] * len(parts)
    for tag, pick in (("a", lambda p: p.shape[2] % LANES == 0), ("b", lambda p: p.shape[2] % LANES != 0)):
        idx = [i for i, p in enumerate(parts) if pick(p)]
        for i, res in zip(idx, _add_pairs_many([parts[i] for i in idx], [theirs[i] for i in idx], f"core_sum_late_{tag}")):
            out[i] = res
    return out


def _local_step(xs, mems, tgt, w_in_full, shards, sizes, small_w):
    g_mix, b_forget, g_ret_out, g_fox_q, g_fox_k = (small_w[n] for n in ("g_mix", "b_forget", "g_ret_out", "g_fox_q", "g_fox_k"))
    g_xattn, g_mem, g_xq, g_xk, g_ffn = (small_w[n] for n in ("g_xattn", "g_mem", "g_xq", "g_xk", "g_ffn"))
    w_in_t = jnp.pad(w_in_full, ((0, MAIN_W + LANES - IN_W), (0, 0)))
    t_len = xs.shape[0]
    cos_t, sin_t = _rope_tables(t_len)
    tables = _decay_tables(min(RET_BLOCK, t_len))
    gq_t = jnp.concatenate([g_fox_q, g_fox_q], axis=-1)
    gk_t = jnp.concatenate([g_fox_k, g_fox_k], axis=-1)
    b_pad = _pad_row(b_forget, LANES)
    g_ret = g_ret_out.reshape(N_HEADS // 2, 1, LANES)

    n1, proj, rq, rk, q_aug, k_aug, z = _in_proj_fwd(xs, g_mix, w_in_t, b_pad, cos_t, sin_t, gq_t, gk_t)
    raw, mix_r, states = _retention_fwd(rq, rk, proj, g_ret, tables)
    mix_f, o32, lse, *gathered = _fox_fwd(q_aug, k_aug, proj, [shards[n] for n in LATE])
    full = {n: _assemble_weight(n, g, shards[n], sizes[n]) for n, g in zip(LATE, gathered)}
    memn, kraw, kn, vmem = _mem_kv_fwd(mems, g_mem, full["w_xkv"], g_xk)
    h1, hn2, qx, o_x, h2 = _attn_out_xattn_fwd(xs, mix_r, mix_f, full["w_out"], g_xattn, full["w_xq"], g_xq, kn, vmem, full["w_xo"])
    hn3, gate, up, act, dh3, loss_part = _ffn_loss_fwd(h2, g_ffn, full["w_gate"], full["w_up"], full["w_down"], tgt)

    dgate, dup, dh2, dg_ffn = _ffn_bwd(dh3, gate, up, h2, g_ffn, full["w_gate"], full["w_up"], full["w_down"])
    dqx, dh1, dmr, dmf, dkn, dvm, dg_xattn, dg_xq = _attn_out_xattn_bwd(dh2, h1, qx, kn, vmem, full["w_xo"], full["w_xq"],
                                                                      full["w_out"], g_xattn, g_xq)
    dw_xkv, dg_mem, dg_xk = _mem_kv_bwd(dkn, dvm, kraw, mems, memn, g_mem, g_xk, full["w_xkv"])
    dw = {
        "w_out": jnp.concatenate([_matmul_tn(mix_r, dh1, "dw_out_ret"), _matmul_tn(mix_f, dh1, "dw_out_fox")], axis=0),
        "w_xq": _matmul_tn(hn2, dqx, "dw_xq"),
        "w_xkv": dw_xkv,
        "w_xo": _matmul_tn(o_x, dh2, "dw_xo"),
        "w_gate": _matmul_tn(dgate, hn3, "dw_gate"),
        "w_up": _matmul_tn(dup, hn3, "dw_up"),
        "w_down": _matmul_tn(act, dh3, "dw_down"),
    }
    late_parts = _shard_parts(LATE, dw)
    dq_r, dk_r, dv_r, drg, dg_ret, *late_theirs = _retention_bwd(dmr, raw, proj, g_ret, rq, rk, states, tables, late_parts)
    late_sums = _core_sums(late_parts, late_theirs)
    dq_f, dk_f, dv_f, df, *late_got = _fox_bwd(q_aug, k_aug, proj, dmf, o32, lse, [s[1] for s in late_sums])
    df_col = jnp.pad(jnp.transpose(df, (1, 0, 2)).reshape(t_len, N_HEADS), ((0, 0), (0, LANES - N_HEADS)))
    dproj, dz, grad_x, dg_mix, dg_fq, dg_fk, db = _in_proj_bwd(xs, g_mix, dh1, dq_r, dk_r, dv_r, drg, dq_f, dk_f, dv_f, df_col,
                                                              proj, z, cos_t, sin_t, gq_t, gk_t, w_in_t)

    dw_in = jnp.concatenate([_matmul_tn(dproj, n1, "dw_in_main"), _matmul_tn(dz, n1, "dw_in_ff")[:IN_W - MAIN_W]], axis=0)
    in_parts = _shard_parts(("w_in",), {"w_in": dw_in})
    sums = {n: s[0] for n, s in zip(LATE, late_sums)}
    got = dict(zip(LATE, late_got))
    small_g = {"g_mix": dg_mix, "b_forget": db[:, :N_HEADS], "g_ret_out": dg_ret, "g_fox_q": dg_fq, "g_fox_k": dg_fk,
               "g_xattn": dg_xattn, "g_mem": dg_mem, "g_xq": dg_xq, "g_xk": dg_xk, "g_ffn": dg_ffn}
    return loss_part, grad_x, sums, got, in_parts, small_g


def _add_received_many(owns, gots, parts):
    n_a, n_w = len(owns), len(parts)
    n_steps = CHIP_SUM_STEPS
    own_specs = [pl.BlockSpec((o.shape[0] // n_steps, o.shape[1]), lambda i: (i, 0)) for o in owns]
    got_specs = [pl.BlockSpec((3, o.shape[0] // n_steps, o.shape[1]), lambda i: (0, i, 0)) for o in owns]

    def body(*refs):
        first_out = 2 * n_a + n_w
        comm = (refs[2 * n_a:first_out], refs[first_out + n_a:first_out + n_a + n_w]) + tuple(refs[first_out + n_a + n_w:])
        step = pl.program_id(0)

        @pl.when(step == 0)
        def _():
            _exchange_phase(0, *comm)

        for a in range(n_a):
            o_ref, g_ref, out_ref = refs[a], refs[n_a + a], refs[first_out + a]
            out_ref[...] = ((o_ref[...] + g_ref[0].astype(F32)) + g_ref[1].astype(F32)) + g_ref[2].astype(F32)

        @pl.when(step == n_steps - 1)
        def _():
            _exchange_phase(1, *comm)

    flat = pl.pallas_call(
        body, name="chip_sum_late", grid=(n_steps,),
        out_shape=tuple(jax.ShapeDtypeStruct(o.shape, F32) for o in owns) + _exchange_out_shapes(parts),
        in_specs=own_specs + got_specs + [ANY] * n_w, out_specs=tuple(own_specs) + (ANY,) * n_w,
        scratch_shapes=_exchange_scratch(n_w), compiler_params=_cparams(("arbitrary",)),
    )(*owns, *gots, *parts)
    return list(flat[:n_a]), list(flat[n_a:])


def _final_grads(names, big, finals, shared):
    my_core = lax.axis_index("c")
    out = {}
    for n, s, fin in zip(names, shared, finals):
        s = lax.dynamic_update_slice(s, fin[None], (my_core, 0, 0))
        s = s.reshape(2 * s.shape[1], s.shape[2])
        rows, width = big[n][0].shape[1:]
        out[n] = s[:rows, :width] if rows % SUBLANES == 0 else lax.dynamic_slice(s, (_shard_row_offset(rows), 0), (rows, width))
    return out


def _reduce_and_update(big, sums, got, in_parts, small_w, small_g, loss_part, grad_x, m_small, v_small):
    small_names = list(small_w)
    pad_rows = SMALL_ROWS - len(small_names) - 1
    stack = lambda d: jnp.concatenate([_pad_row(d[n]) for n in small_names] + [jnp.zeros((pad_rows + 1, D_MODEL), F32)], axis=0)
    g_pack = jnp.concatenate([_pad_row(small_g[n]) for n in small_names] + [_pad_row(loss_part[0:1, 0:1])]
                             + [jnp.zeros((pad_rows, D_MODEL), F32)], axis=0)
    late_finals, in_theirs = _add_received_many([sums[n] for n in LATE], [got[n] for n in LATE], in_parts)
    (in_own, in_bf), late_shared = _add_pairs(in_parts[0], in_theirs[0], "core_sum_w_in", late_finals)
    grads = _final_grads(LATE, big, late_finals, late_shared)
    *late_updates, in_got, g_tot = _adamw_many([big[n][0][0] for n in LATE], [grads[n] for n in LATE], [big[n][1][0] for n in LATE],
                                               [big[n][2][0] for n in LATE], [in_bf], g_pack)
    updates = dict(zip(LATE, late_updates))
    in_final = [_add_received(in_own, in_got, "chip_sum_w_in")]
    grads.update(_final_grads(("w_in",), big, in_final, _share_with_sibling(in_final)))
    updates["w_in"] = _adamw(big["w_in"][0][0], grads["w_in"], big["w_in"][1][0], big["w_in"][2][0], "adamw_w_in")
    deltas, new_m, new_v = {}, {}, {}
    for n in big:
        restore = (lambda a: jnp.swapaxes(a[None], 1, 2)) if n in TRANSPOSED else (lambda a: a[None])
        grads[n] = restore(grads[n])
        deltas[n], new_m[n], new_v[n] = (restore(a) for a in updates[n])

    d_s, m_s, v_s = _adamw(stack(small_w), g_tot, stack(m_small), stack(v_small), "adamw_small")
    for i, n in enumerate(small_names):
        shape = small_w[n].shape
        size = int(np.prod(shape))
        grads[n] = g_tot[i, :size].reshape(shape)
        deltas[n], new_m[n], new_v[n] = d_s[i, :size].reshape(shape), m_s[i, :size].reshape(shape), v_s[i, :size].reshape(shape)
    loss = g_tot[len(small_names), 0]

    order = ["g_mix", "w_in", "b_forget", "g_ret_out", "g_fox_q", "g_fox_k", "w_out", "g_xattn", "w_xq", "w_xkv", "g_mem", "g_xq",
             "g_xk", "w_xo", "g_ffn", "w_gate", "w_up", "w_down"]
    return (loss, grad_x[None], *[grads[n] for n in order], *[deltas[n] for n in order], *[new_m[n] for n in order],
            *[new_v[n] for n in order])
```

```python
import functools

import numpy as np
import jax
import jax.numpy as jnp
from jax import lax
from jax.experimental import pallas as pl
from jax.experimental.pallas import tpu as pltpu

F32 = jnp.float32
BF = jnp.bfloat16

D_MODEL = 1024
HEAD_DIM = 64
N_HEADS = 8
GROUP_W = 512
N_XH = 4
XHD = 256
D_FF = 2816
MAIN_W = 3584
IN_W = 3592
ROPE_BASE = 10000.0
LOG2E = 1.4426950408889634
LN2 = 0.6931471805599453
EPS = 1e-6
NEG = -1e30
LANES = 128
SUBLANES = 8
RET_BLOCK = 256
REF_CHUNK = 64
ROW_TILE = 512
FFN_BWD_TILE = 256
ATT_BLOCK = 256
FWD_GROUP = 4
TN_MAX_ROWS = 1408
SMALL_ROWS = 16
COL_SHARDED = ("w_xkv",)
TRANSPOSED = ("w_in", "w_gate", "w_up")
SHARD_ROW_ALIGN = 32
SHARD_ROW_PAD = 256
LATE = ("w_out", "w_xq", "w_xkv", "w_xo", "w_gate", "w_up", "w_down")
VMEM_LIMIT = 56 * 1024 * 1024

ADAM_LR = 0.001
ADAM_B1 = 0.9
ADAM_B2 = 0.999
ADAM_EPS = 1e-08
ADAM_WD = 0.01
ADAM_STEP = 10
CHIP_SUM_STEPS = 2
ADAM_STEPS = 8

MESH = pl.DeviceIdType.MESH
ANY = pl.BlockSpec(memory_space=pl.ANY)
VMEM_SPEC = pl.BlockSpec(memory_space=pltpu.VMEM)


def _cparams(sem=None, vmem=VMEM_LIMIT):
    return pltpu.CompilerParams(dimension_semantics=sem, vmem_limit_bytes=vmem)


def _dot(a, b):
    return jnp.dot(a.astype(BF), b.astype(BF), preferred_element_type=F32)


def _dot_nt(a, b):
    return lax.dot_general(a.astype(BF), b.astype(BF), (((1,), (1,)), ((), ())), preferred_element_type=F32)


def _dot_tn(a, b):
    return lax.dot_general(a.astype(BF), b.astype(BF), (((0,), (0,)), ((), ())), preferred_element_type=F32)


def _split3(x):
    hi = x.astype(BF)
    r = x - hi.astype(F32)
    mid = r.astype(BF)
    lo = (r - mid.astype(F32)).astype(BF)
    return hi, mid, lo


def _dot_exact(ind, x):
    hi, mid, lo = _split3(x)
    return (jnp.dot(ind, lo, preferred_element_type=F32) + jnp.dot(ind, mid, preferred_element_type=F32)
            + jnp.dot(ind, hi, preferred_element_type=F32))


def _dot_nt_exact(ind, x):
    hi, mid, lo = _split3(x)
    dn = (((1,), (1,)), ((), ()))
    return (lax.dot_general(ind, lo, dn, preferred_element_type=F32) + lax.dot_general(ind, mid, dn, preferred_element_type=F32)
            + lax.dot_general(ind, hi, dn, preferred_element_type=F32))


def _sigmoid(x):
    return 1.0 / (1.0 + jnp.exp(-x))


def _rms_fwd(x, g):
    r = lax.rsqrt(jnp.mean(x * x, axis=-1, keepdims=True) + EPS)
    return x * r * g


def _rms_bwd(x, g, dy):
    r = lax.rsqrt(jnp.mean(x * x, axis=-1, keepdims=True) + EPS)
    xh = x * r
    dg = jnp.sum(dy * xh, axis=0, keepdims=True)
    dxh = dy * g
    dx = r * (dxh - xh * jnp.mean(dxh * xh, axis=-1, keepdims=True))
    return dx, dg


def _group_mean64(x):
    lane = lax.broadcasted_iota(jnp.int32, x.shape, 1)
    lo = lane < HEAD_DIM
    s_lo = jnp.sum(jnp.where(lo, x, 0.0), axis=-1, keepdims=True)
    s_hi = jnp.sum(jnp.where(lo, 0.0, x), axis=-1, keepdims=True)
    return jnp.where(lo, s_lo, s_hi) * (1.0 / HEAD_DIM)


def _swap32(x):
    lane = lax.broadcasted_iota(jnp.int32, x.shape, 1)
    first = (lane % HEAD_DIM) < (HEAD_DIM // 2)
    return jnp.where(first, pltpu.roll(x, LANES - HEAD_DIM // 2, axis=1), pltpu.roll(x, HEAD_DIM // 2, axis=1))


def _chunks(w):
    return [slice(j * LANES, (j + 1) * LANES) for j in range(w // LANES)]


def _aug_pair(qk, f_cols, is_query):
    lane = lax.broadcasted_iota(jnp.int32, qk.shape, 1)
    a = lane - HEAD_DIM
    values = (qk, pltpu.roll(qk, HEAD_DIM, axis=1))
    out = []
    for hh in range(2):
        hi, mid, lo = (p.astype(F32) for p in _split3(f_cols[hh] * LOG2E))
        if is_query:
            aux = jnp.where(a == 0, hi, jnp.where(a == 1, mid, jnp.where(a == 2, lo, jnp.where(a < 6, 1.0, 0.0))))
        else:
            aux = jnp.where(a < 3, 1.0, jnp.where(a == 3, -hi, jnp.where(a == 4, -mid, jnp.where(a == 5, -lo, 0.0))))
        out.append(jnp.where(a < 0, values[hh], aux))
    return jnp.concatenate(out, axis=-1).astype(BF)


def _mem_kv_fwd(mem, g_mem, w_xkv, g_xk):
    m_tok = mem.shape[0]

    def body(mem_ref, gm_ref, w_ref, gk_ref, memn_ref, kraw_ref, kn_ref, v_ref):
        mn = _rms_fwd(mem_ref[...], gm_ref[...]).astype(BF)
        memn_ref[...] = mn
        kv = jnp.dot(mn, w_ref[...], preferred_element_type=F32)
        k = kv[:, :D_MODEL]
        kraw_ref[...] = k
        v_ref[...] = kv[:, D_MODEL:].astype(BF)
        for h in range(N_XH):
            sl = slice(h * XHD, (h + 1) * XHD)
            kn_ref[:, sl] = _rms_fwd(k[:, sl], gk_ref[...]).astype(BF)

    return pl.pallas_call(
        body, name="mem_kv_fwd",
        out_shape=(jax.ShapeDtypeStruct((m_tok, D_MODEL), BF), jax.ShapeDtypeStruct((m_tok, D_MODEL), F32),
                   jax.ShapeDtypeStruct((m_tok, D_MODEL), BF), jax.ShapeDtypeStruct((m_tok, D_MODEL), BF)),
        in_specs=[VMEM_SPEC] * 4, out_specs=(VMEM_SPEC,) * 4, compiler_params=_cparams(),
    )(mem, g_mem, w_xkv, g_xk)


def _in_proj_fwd(x, g_mix, w_in_t, b_pad, cos_t, sin_t, gq_t, gk_t):
    t_len = x.shape[0]
    tm = min(ROW_TILE, t_len)
    n_t = t_len // tm

    def body(x_ref, g_ref, wm_ref, wf_ref, b_ref, cos_ref, sin_ref, gq_ref, gk_ref,
             n1_ref, proj_ref, rq_ref, rk_ref, qa_ref, ka_ref, z_ref, carry):
        i = pl.program_id(0)

        @pl.when(i == 0)
        def _():
            carry[...] = jnp.zeros_like(carry)

        n1 = _rms_fwd(x_ref[...], g_ref[...]).astype(BF)
        n1_ref[...] = n1
        z = _dot_nt(n1, wf_ref[...]) + b_ref[...]
        z_ref[...] = z
        lane = lax.broadcasted_iota(jnp.int32, z.shape, 1)
        lf = jnp.where(lane < N_HEADS, jnp.minimum(z, 0.0) - jnp.log(1.0 + jnp.exp(-jnp.abs(z))), 0.0)
        row = lax.broadcasted_iota(jnp.int32, (tm, tm), 0)
        col = lax.broadcasted_iota(jnp.int32, (tm, tm), 1)
        tri = (row >= col).astype(BF)
        fc = _dot_exact(tri, lf) + carry[0:1, :]
        carry[...] = jnp.broadcast_to(fc[tm - 1:tm, :], carry.shape)
        c, s = cos_ref[...], sin_ref[...]

        def section(n):
            p = _dot_nt(n1, wm_ref[n * GROUP_W:(n + 1) * GROUP_W, :])
            proj_ref[:, n * GROUP_W:(n + 1) * GROUP_W] = p.astype(BF)
            return p

        def rotate(p, out_ref, scale):
            for sl in _chunks(GROUP_W):
                out_ref[:, sl] = ((p[:, sl] * c + _swap32(p[:, sl]) * s) * scale).astype(BF)

        def norm_aug(p, gain, out_ref, scale, is_query):
            for j, sl in enumerate(_chunks(GROUP_W)):
                f = p[:, sl]
                f = f * lax.rsqrt(_group_mean64(f * f) + EPS) * gain * scale
                out_ref[:, 2 * j * LANES:2 * (j + 1) * LANES] = _aug_pair(f, [fc[:, 2 * j:2 * j + 1], fc[:, 2 * j + 1:2 * j + 2]], is_query)

        p_rq, p_rk = section(0), section(1)
        rotate(p_rq, rq_ref, 0.125)
        section(2)
        rotate(p_rk, rk_ref, 1.0)
        section(3)
        p_fq = section(4)
        p_fk = section(5)
        norm_aug(p_fq, gq_ref[...], qa_ref, 0.125 * LOG2E, True)
        section(6)
        norm_aug(p_fk, gk_ref[...], ka_ref, 1.0, False)

    row_spec = lambda w: pl.BlockSpec((tm, w), lambda i: (i, 0))
    full = lambda a: pl.BlockSpec(a.shape, lambda i: (0,) * a.ndim)
    return pl.pallas_call(
        body, name="in_proj_fwd", grid=(n_t,),
        out_shape=(jax.ShapeDtypeStruct((t_len, D_MODEL), BF), jax.ShapeDtypeStruct((t_len, MAIN_W), BF),
                   jax.ShapeDtypeStruct((t_len, GROUP_W), BF), jax.ShapeDtypeStruct((t_len, GROUP_W), BF),
                   jax.ShapeDtypeStruct((t_len, 2 * GROUP_W), BF), jax.ShapeDtypeStruct((t_len, 2 * GROUP_W), BF),
                   jax.ShapeDtypeStruct((t_len, LANES), F32)),
        in_specs=[row_spec(D_MODEL), full(g_mix), *_w_in_specs(), full(b_pad), row_spec(LANES), row_spec(LANES),
                  full(gq_t), full(gk_t)],
        out_specs=(row_spec(D_MODEL), row_spec(MAIN_W), row_spec(GROUP_W), row_spec(GROUP_W), row_spec(2 * GROUP_W),
                   row_spec(2 * GROUP_W), row_spec(LANES)),
        scratch_shapes=[pltpu.VMEM((8, LANES), F32)],
        compiler_params=_cparams(("arbitrary",)),
    )(x, g_mix, w_in_t, w_in_t, b_pad, cos_t, sin_t, gq_t, gk_t)


def _w_in_specs():
    return (pl.BlockSpec((MAIN_W, D_MODEL), lambda i: (0, 0)), pl.BlockSpec((LANES, D_MODEL), lambda i: (MAIN_W // LANES, 0)))


def _matmul_tn_pair(a1, a2, b, name, bk=1024):
    t_len, m = a1.shape
    n = b.shape[1]
    bm = m if m <= TN_MAX_ROWS else m // 2
    bk = min(bk, t_len)

    def body(a1_ref, a2_ref, b_ref, o_ref):
        @pl.when(pl.program_id(1) == 0)
        def _():
            o_ref[...] = jnp.zeros_like(o_ref)

        bv = b_ref[...]
        o_ref[0] += _dot_tn(a1_ref[...], bv)
        o_ref[1] += _dot_tn(a2_ref[...], bv)

    a_spec = pl.BlockSpec((bk, bm), lambda i, k: (k, i))
    return pl.pallas_call(
        body, name=name, grid=(m // bm, t_len // bk),
        out_shape=jax.ShapeDtypeStruct((2, m, n), F32),
        in_specs=[a_spec, a_spec, pl.BlockSpec((bk, n), lambda i, k: (k, 0))],
        out_specs=pl.BlockSpec((2, bm, n), lambda i, k: (0, i, 0)),
        compiler_params=_cparams(("arbitrary", "arbitrary")),
    )(a1, a2, b)


def _decay_tables(c):
    h = np.arange(N_HEADS, dtype=np.float64)
    lg = np.log(1.0 - 2.0 ** (-5.0 - h)).astype(np.float32).astype(np.float64)
    t = np.arange(c)
    same_or_earlier = (t[None, :] // REF_CHUNK) <= (t[:, None] // REF_CHUNK)
    w = np.where(same_or_earlier[None], np.exp(lg[:, None, None] * np.abs(t[:, None] - t[None, :])[None]), 0.0)
    qd = np.exp(lg[:, None] * (t[None, :] + 1.0))
    kd = np.exp(lg[:, None] * (c - 1.0 - t[None, :]))
    cd = np.exp(lg * c)
    ones = np.ones((1, 1, HEAD_DIM))
    return (jnp.asarray(w, F32), jnp.asarray(qd[:, :, None] * ones, F32), jnp.asarray(kd[:, :, None] * ones, F32),
            jnp.asarray(cd[:, None, None] * np.ones((1, HEAD_DIM, HEAD_DIM)), F32))


def _retention_fwd(rq, rk, proj, g_ret, tables):
    t_len = rq.shape[0]
    c = min(RET_BLOCK, t_len)
    n_b = t_len // c
    wdec, qdec, kdec, cdec = tables
    v_col, g_col = 2 * GROUP_W // LANES, 3 * GROUP_W // LANES

    def body(q_ref, k_ref, v_ref, rg_ref, g_ref, w_ref, qd_ref, kd_ref, cd_ref, raw_ref, mix_ref, st_ref, state):
        i = pl.program_id(1)

        @pl.when(i == 0)
        def _():
            state[...] = jnp.zeros_like(state)

        q2, k2, v2 = q_ref[...], k_ref[...], v_ref[...]
        heads = [tuple(t[:, hh * HEAD_DIM:(hh + 1) * HEAD_DIM] for t in (q2, k2, v2)) for hh in range(2)]
        scores = [(_dot_nt(q, k) * w_ref[hh]).astype(BF) for hh, (q, k, _) in enumerate(heads)]
        outs = []
        for hh, (q, k, v) in enumerate(heads):
            sp = state[hh]
            st_ref[0, 0, hh] = sp
            outs.append(jnp.dot(scores[hh], v, preferred_element_type=F32) + _dot(q.astype(F32) * qd_ref[hh], sp))
            state[hh] = sp * cd_ref[hh] + _dot_tn(k.astype(F32) * kd_ref[hh], v)
        o2 = jnp.concatenate(outs, axis=-1)
        raw_ref[...] = o2
        xc = o2 - _group_mean64(o2)
        xh = xc * lax.rsqrt(_group_mean64(xc * xc) + EPS)
        gate = rg_ref[...].astype(F32)
        mix_ref[...] = (gate * _sigmoid(gate) * (xh * g_ref[0])).astype(BF)

    blk = lambda col0: pl.BlockSpec((c, LANES), lambda hp, i: (i, col0 + hp))
    tab = lambda a: pl.BlockSpec((2,) + a.shape[1:], lambda hp, i: (hp, 0, 0))
    return pl.pallas_call(
        body, name="retention_fwd", grid=(N_HEADS // 2, n_b),
        out_shape=(jax.ShapeDtypeStruct((t_len, GROUP_W), F32), jax.ShapeDtypeStruct((t_len, GROUP_W), BF),
                   jax.ShapeDtypeStruct((N_HEADS // 2, n_b, 2, HEAD_DIM, HEAD_DIM), F32)),
        in_specs=[blk(0), blk(0), blk(v_col), blk(g_col), pl.BlockSpec((1, 1, LANES), lambda hp, i: (hp, 0, 0)),
                  tab(wdec), tab(qdec), tab(kdec), tab(cdec)],
        out_specs=(blk(0), blk(0), pl.BlockSpec((1, 1, 2, HEAD_DIM, HEAD_DIM), lambda hp, i: (hp, i, 0, 0, 0))),
        scratch_shapes=[pltpu.VMEM((2, HEAD_DIM, HEAD_DIM), F32)],
        compiler_params=_cparams(("arbitrary", "arbitrary")),
    )(rq, rk, proj, proj, g_ret, wdec, qdec, kdec, cdec)


def _fox_fwd(q_aug, k_aug, proj, shards):
    t_len = q_aug.shape[0]
    tq = min(ATT_BLOCK, t_len)
    nsub = min(FWD_GROUP, t_len // tq)
    tg = nsub * tq
    n_q = t_len // tg
    v_col = 6 * GROUP_W // LANES
    tc = min(512, t_len)
    n_w = len(shards)
    n_steps = (N_HEADS // 2) * n_q

    def body(*refs):
        q_ref, k_ref, v_ref = refs[:3]
        o_ref, o32_ref, lse_ref = refs[3 + n_w:6 + n_w]
        vt = refs[6 + 2 * n_w]
        comm = (refs[3:3 + n_w], refs[6 + n_w:6 + 2 * n_w]) + tuple(refs[7 + 2 * n_w:])
        i = pl.program_id(1)
        step = pl.program_id(0) * n_q + i

        @pl.when(step == 0)
        def _():
            _gather_phase(0, *comm)

        @pl.when(step == (3 * n_steps) // 4)
        def _():
            _gather_phase(1, *comm)

        @pl.when(i == 0)
        def _():
            for c0 in range(0, t_len, tc):
                vt[:, c0:c0 + tc] = v_ref[c0:c0 + tc, :].T

        chains = [(u, hh) for u in range(nsub) for hh in range(2)]
        qs = {(u, hh): q_ref[u * tq:(u + 1) * tq, hh * LANES:(hh + 1) * LANES] for u, hh in chains}
        ones = jnp.ones((HEAD_DIM, tq), BF)

        def scores(j, which):
            k2 = k_ref[pl.ds(pl.multiple_of(j * tq, tq), tq), :]
            return {ch: _dot_nt(k2[:, ch[1] * LANES:(ch[1] + 1) * LANES], qs[ch]) for ch in which}

        def update(j, ss, carry, masked):
            v2 = vt[:, pl.ds(pl.multiple_of(j * tq, tq), tq)]
            ps, stats = {}, {}
            for ch in ss:
                m = carry[ch][0]
                s_t = ss[ch]
                if ch in masked:
                    krow = lax.broadcasted_iota(jnp.int32, (tq, tq), 0)
                    qcol = lax.broadcasted_iota(jnp.int32, (tq, tq), 1)
                    s_t = jnp.where(qcol >= krow, s_t, NEG)
                m_new = jnp.maximum(m, jnp.max(s_t, axis=0, keepdims=True))
                ps[ch] = jnp.exp2(s_t - m_new).astype(BF)
                stats[ch] = (m_new, jnp.exp2(m - m_new))
            out = dict(carry)
            for ch in ss:
                m_new, alpha = stats[ch]
                v_aug = jnp.concatenate([v2[ch[1] * HEAD_DIM:(ch[1] + 1) * HEAD_DIM, :], ones], axis=0)
                out[ch] = (m_new, carry[ch][1] * alpha + jnp.dot(v_aug, ps[ch], preferred_element_type=F32))
            return out

        def advance(j, state):
            ss, carry = state
            return scores(j + 1, chains), update(j, ss, carry, ())

        init = {ch: (jnp.full((1, tq), NEG, F32), jnp.zeros((LANES, tq), F32)) for ch in chains}
        first = nsub * i
        ss, carry = lax.fori_loop(0, first, advance, (scores(0, chains), init))
        carry = update(first, ss, carry, [(0, 0), (0, 1)])
        for u in range(1, nsub):
            rest = [(uu, hh) for uu in range(u, nsub) for hh in range(2)]
            carry = update(first + u, scores(first + u, rest), carry, [(u, 0), (u, 1)])
        for u in range(nsub):
            outs, lses = [], []
            for hh in range(2):
                m, acc = carry[u, hh]
                l = acc[HEAD_DIM:HEAD_DIM + 1, :]
                outs.append(acc[:HEAD_DIM, :] / l)
                lses.append(m + jnp.log2(l))
            o2 = jnp.concatenate(outs, axis=0).T
            o32_ref[u * tq:(u + 1) * tq, :] = o2
            o_ref[u * tq:(u + 1) * tq, :] = o2.astype(BF)
            lse_ref[0, :, u * tq:(u + 1) * tq] = jnp.concatenate(lses, axis=0)

        @pl.when(step == n_steps - 1)
        def _():
            _gather_phase(2, *comm)

    return pl.pallas_call(
        body, name="fox_fwd", grid=(N_HEADS // 2, n_q),
        out_shape=(jax.ShapeDtypeStruct((t_len, GROUP_W), BF), jax.ShapeDtypeStruct((t_len, GROUP_W), F32),
                   jax.ShapeDtypeStruct((N_HEADS // 2, 2, t_len), F32))
        + tuple(jax.ShapeDtypeStruct((4,) + s.shape, s.dtype) for s in shards),
        in_specs=[pl.BlockSpec((tg, 2 * LANES), lambda hp, i: (i, hp)),
                  pl.BlockSpec((t_len, 2 * LANES), lambda hp, i: (0, hp)),
                  pl.BlockSpec((t_len, LANES), lambda hp, i: (0, v_col + hp))] + [ANY] * n_w,
        out_specs=(pl.BlockSpec((tg, LANES), lambda hp, i: (i, hp)), pl.BlockSpec((tg, LANES), lambda hp, i: (i, hp)),
                   pl.BlockSpec((1, 2, tg), lambda hp, i: (hp, 0, i))) + (ANY,) * n_w,
        scratch_shapes=[pltpu.VMEM((LANES, t_len), BF)] + _gather_scratch(n_w),
        compiler_params=_cparams(("arbitrary", "arbitrary")),
    )(q_aug, k_aug, proj, *shards)


def _softmax_rows(s):
    p = jnp.exp(s - jnp.max(s, axis=-1, keepdims=True))
    return p / jnp.sum(p, axis=-1, keepdims=True)


def _attn_out_xattn_fwd(x, mix_r, mix_f, w_out, g_xattn, w_xq, g_xq, kn, v, w_xo):
    t_len = x.shape[0]
    tm = min(ROW_TILE, t_len)

    def body(x_ref, mr_ref, mf_ref, wo_ref, g_ref, wq_ref, gq_ref, kn_ref, v_ref, wxo_ref,
             h1_ref, hn_ref, qx_ref, o_ref, h2_ref):
        h1 = x_ref[...] + jnp.dot(mr_ref[...], wo_ref[:GROUP_W, :], preferred_element_type=F32) \
            + jnp.dot(mf_ref[...], wo_ref[GROUP_W:, :], preferred_element_type=F32)
        h1_ref[...] = h1
        hn = _rms_fwd(h1, g_ref[...]).astype(BF)
        hn_ref[...] = hn
        qx = jnp.dot(hn, wq_ref[...], preferred_element_type=F32).astype(BF)
        qx_ref[...] = qx
        sls = [slice(h * XHD, (h + 1) * XHD) for h in range(N_XH)]
        qns = [_rms_fwd(qx[:, sl].astype(F32), gq_ref[...]).astype(BF) for sl in sls]
        logits = [_dot_nt(qn, kn_ref[:, sl]) * (XHD ** -0.5) for qn, sl in zip(qns, sls)]
        ps = [_softmax_rows(s).astype(BF) for s in logits]
        for p, sl in zip(ps, sls):
            o_ref[:, sl] = jnp.dot(p, v_ref[:, sl], preferred_element_type=F32).astype(BF)
        h2_ref[...] = h1 + jnp.dot(o_ref[...], wxo_ref[...], preferred_element_type=F32)

    row_spec = lambda w: pl.BlockSpec((tm, w), lambda i: (i, 0))
    full = lambda a: pl.BlockSpec(a.shape, lambda i: (0,) * a.ndim)
    return pl.pallas_call(
        body, name="attn_out_xattn_fwd", grid=(t_len // tm,),
        out_shape=(jax.ShapeDtypeStruct((t_len, D_MODEL), F32), jax.ShapeDtypeStruct((t_len, D_MODEL), BF),
                   jax.ShapeDtypeStruct((t_len, D_MODEL), BF), jax.ShapeDtypeStruct((t_len, D_MODEL), BF),
                   jax.ShapeDtypeStruct((t_len, D_MODEL), F32)),
        in_specs=[row_spec(D_MODEL), row_spec(GROUP_W), row_spec(GROUP_W), full(w_out), full(g_xattn), full(w_xq), full(g_xq),
                  full(kn), full(v), full(w_xo)],
        out_specs=(row_spec(D_MODEL),) * 5,
        compiler_params=_cparams(("arbitrary",)),
    )(x, mix_r, mix_f, w_out, g_xattn, w_xq, g_xq, kn, v, w_xo)


def _ffn_loss_fwd(h2, g_ffn, w_gate, w_up, w_down, target):
    t_len = h2.shape[0]
    tm = min(ROW_TILE, t_len)

    def body(h2_ref, g_ref, wg_ref, wu_ref, wd_ref, tgt_ref, hn_ref, gate_ref, up_ref, act_ref, dh3_ref, loss_ref):
        @pl.when(pl.program_id(0) == 0)
        def _():
            loss_ref[...] = jnp.zeros_like(loss_ref)

        h2v = h2_ref[...]
        hn = _rms_fwd(h2v, g_ref[...]).astype(BF)
        hn_ref[...] = hn
        gate = _dot_nt(hn, wg_ref[...])
        up = _dot_nt(hn, wu_ref[...])
        gate_ref[...] = gate.astype(BF)
        up_ref[...] = up.astype(BF)
        act = (gate * _sigmoid(gate) * up).astype(BF)
        act_ref[...] = act
        diff = h2v + jnp.dot(act, wd_ref[...], preferred_element_type=F32) - tgt_ref[...]
        dh3_ref[...] = diff * (1.0 / D_MODEL)
        per_row = jnp.sum(diff * diff, axis=-1, keepdims=True) * (1.0 / D_MODEL)
        loss_ref[...] += 0.5 * jnp.sum(per_row, axis=0, keepdims=True)

    row_spec = lambda w: pl.BlockSpec((tm, w), lambda i: (i, 0))
    full = lambda a: pl.BlockSpec(a.shape, lambda i: (0,) * a.ndim, pipeline_mode=pl.Buffered(1))
    return pl.pallas_call(
        body, name="ffn_loss_fwd", grid=(t_len // tm,),
        out_shape=(jax.ShapeDtypeStruct((t_len, D_MODEL), BF), jax.ShapeDtypeStruct((t_len, D_FF), BF),
                   jax.ShapeDtypeStruct((t_len, D_FF), BF), jax.ShapeDtypeStruct((t_len, D_FF), BF),
                   jax.ShapeDtypeStruct((t_len, D_MODEL), F32), jax.ShapeDtypeStruct((8, LANES), F32)),
        in_specs=[row_spec(D_MODEL), full(g_ffn), full(w_gate), full(w_up), full(w_down), row_spec(D_MODEL)],
        out_specs=(row_spec(D_MODEL), row_spec(D_FF), row_spec(D_FF), row_spec(D_FF), row_spec(D_MODEL),
                   pl.BlockSpec((8, LANES), lambda i: (0, 0))),
        compiler_params=_cparams(("arbitrary",)),
    )(h2, g_ffn, w_gate, w_up, w_down, target)


def _ffn_bwd(dh3, gate, up, h2, g_ffn, w_gate, w_up, w_down):
    t_len = h2.shape[0]
    tm = min(FFN_BWD_TILE, t_len)

    def body(dh3_ref, gate_ref, up_ref, h2_ref, g_ref, wg_ref, wu_ref, wd_ref, dgate_ref, dup_ref, dh2_ref, dg_ref):
        @pl.when(pl.program_id(0) == 0)
        def _():
            dg_ref[...] = jnp.zeros_like(dg_ref)

        dh3v = dh3_ref[...]
        dact = _dot_nt(dh3v, wd_ref[...])
        g = gate_ref[...].astype(F32)
        sg = _sigmoid(g)
        dup = (dact * (g * sg)).astype(BF)
        dgate = (dact * up_ref[...].astype(F32) * (sg * (1.0 + g * (1.0 - sg)))).astype(BF)
        dup_ref[...] = dup
        dgate_ref[...] = dgate
        dhn = jnp.dot(dgate, wg_ref[...], preferred_element_type=F32) + jnp.dot(dup, wu_ref[...], preferred_element_type=F32)
        dx, dg = _rms_bwd(h2_ref[...], g_ref[...], dhn)
        dh2_ref[...] = dh3v + dx
        dg_ref[...] += dg

    row_spec = lambda w: pl.BlockSpec((tm, w), lambda i: (i, 0))
    full = lambda a: pl.BlockSpec(a.shape, lambda i: (0,) * a.ndim, pipeline_mode=pl.Buffered(1))
    return pl.pallas_call(
        body, name="ffn_bwd", grid=(t_len // tm,),
        out_shape=(jax.ShapeDtypeStruct((t_len, D_FF), BF), jax.ShapeDtypeStruct((t_len, D_FF), BF),
                   jax.ShapeDtypeStruct((t_len, D_MODEL), F32), jax.ShapeDtypeStruct((1, D_MODEL), F32)),
        in_specs=[row_spec(D_MODEL), row_spec(D_FF), row_spec(D_FF), row_spec(D_MODEL), full(g_ffn), full(w_gate), full(w_up),
                  full(w_down)],
        out_specs=(row_spec(D_FF), row_spec(D_FF), row_spec(D_MODEL), pl.BlockSpec((1, D_MODEL), lambda i: (0, 0))),
        compiler_params=_cparams(("arbitrary",)),
    )(dh3, gate, up, h2, g_ffn, w_gate, w_up, w_down)


def _attn_out_xattn_bwd(dh2, h1, qx, kn, v, w_xo, w_xq, w_out, g_xattn, g_xq):
    t_len = h1.shape[0]
    tm = min(ROW_TILE, t_len)
    m_tok = kn.shape[0]

    def body(dh2_ref, h1_ref, qx_ref, kn_ref, v_ref, wxo_ref, wq_ref, wo_ref, g_ref, gq_ref,
             dqx_ref, dh1_ref, dmr_ref, dmf_ref, dkn_ref, dv_ref, dg_ref, dgq_ref, dqx_scr):
        @pl.when(pl.program_id(0) == 0)
        def _():
            dkn_ref[...] = jnp.zeros_like(dkn_ref)
            dv_ref[...] = jnp.zeros_like(dv_ref)
            dg_ref[...] = jnp.zeros_like(dg_ref)
            dgq_ref[...] = jnp.zeros_like(dgq_ref)

        dh2v = dh2_ref[...]
        do = _dot_nt(dh2v, wxo_ref[...])
        gq = gq_ref[...]
        sls = [slice(h * XHD, (h + 1) * XHD) for h in range(N_XH)]
        qraws = [qx_ref[:, sl].astype(F32) for sl in sls]
        qns = [_rms_fwd(qraw, gq).astype(BF) for qraw in qraws]
        dohs = [do[:, sl].astype(BF) for sl in sls]
        logits = [_dot_nt(qn, kn_ref[:, sl]) * (XHD ** -0.5) for qn, sl in zip(qns, sls)]
        dps = [_dot_nt(doh, v_ref[:, sl]) for doh, sl in zip(dohs, sls)]
        ps = [_softmax_rows(s) for s in logits]
        dss = [(p * (dp - jnp.sum(dp * p, axis=-1, keepdims=True)) * (XHD ** -0.5)).astype(BF) for p, dp in zip(ps, dps)]
        dqns = []
        for h, sl in enumerate(sls):
            dv_ref[:, sl] += _dot_tn(ps[h], dohs[h])
            dqns.append(jnp.dot(dss[h], kn_ref[:, sl], preferred_element_type=F32))
            dkn_ref[:, sl] += _dot_tn(dss[h], qns[h])
        dgq = jnp.zeros((1, XHD), F32)
        for h, sl in enumerate(sls):
            dx, dg_h = _rms_bwd(qraws[h], gq, dqns[h])
            dgq = dgq + dg_h
            dqx_scr[:, sl] = dx.astype(BF)
        dgq_ref[...] += dgq
        dqx = dqx_scr[...]
        dqx_ref[...] = dqx
        dhn = _dot_nt(dqx, wq_ref[...])
        dx, dg = _rms_bwd(h1_ref[...], g_ref[...], dhn)
        dg_ref[...] += dg
        dh1 = dh2v + dx
        dh1_ref[...] = dh1
        dmix = _dot_nt(dh1, wo_ref[...])
        dmr_ref[...] = dmix[:, :GROUP_W]
        dmf_ref[...] = dmix[:, GROUP_W:].astype(BF)

    row_spec = lambda w: pl.BlockSpec((tm, w), lambda i: (i, 0))
    full = lambda a: pl.BlockSpec(a.shape, lambda i: (0,) * a.ndim)
    acc = lambda r, c: pl.BlockSpec((r, c), lambda i: (0, 0))
    return pl.pallas_call(
        body, name="attn_out_xattn_bwd", grid=(t_len // tm,),
        out_shape=(jax.ShapeDtypeStruct((t_len, D_MODEL), BF), jax.ShapeDtypeStruct((t_len, D_MODEL), F32),
                   jax.ShapeDtypeStruct((t_len, GROUP_W), F32), jax.ShapeDtypeStruct((t_len, GROUP_W), BF),
                   jax.ShapeDtypeStruct((m_tok, D_MODEL), F32), jax.ShapeDtypeStruct((m_tok, D_MODEL), F32),
                   jax.ShapeDtypeStruct((1, D_MODEL), F32), jax.ShapeDtypeStruct((1, XHD), F32)),
        in_specs=[row_spec(D_MODEL), row_spec(D_MODEL), row_spec(D_MODEL), full(kn), full(v), full(w_xo), full(w_xq), full(w_out),
                  full(g_xattn), full(g_xq)],
        out_specs=(row_spec(D_MODEL), row_spec(D_MODEL), row_spec(GROUP_W), row_spec(GROUP_W), acc(m_tok, D_MODEL),
                   acc(m_tok, D_MODEL), acc(1, D_MODEL), acc(1, XHD)),
        scratch_shapes=[pltpu.VMEM((tm, D_MODEL), BF)],
        compiler_params=_cparams(("arbitrary",)),
    )(dh2, h1, qx, kn, v, w_xo, w_xq, w_out, g_xattn, g_xq)


def _mem_kv_bwd(dkn, dv, kraw, mem, memn, g_mem, g_xk, w_xkv):
    m_tok = mem.shape[0]

    def body(dkn_ref, dv_ref, kraw_ref, mem_ref, memn_ref, gm_ref, gk_ref, w_ref, dw_ref, dgm_ref, dgk_ref, dkv_scr):
        gk = gk_ref[...]
        dgk = jnp.zeros((1, XHD), F32)
        for h in range(N_XH):
            sl = slice(h * XHD, (h + 1) * XHD)
            dx, dg_h = _rms_bwd(kraw_ref[:, sl], gk, dkn_ref[:, sl])
            dgk = dgk + dg_h
            dkv_scr[:, sl] = dx.astype(BF)
        dgk_ref[...] = dgk
        dkv_scr[:, D_MODEL:] = dv_ref[...].astype(BF)
        dkv = dkv_scr[...]
        dw_ref[...] = _dot_tn(memn_ref[...], dkv)
        dmemn = _dot_nt(dkv, w_ref[...])
        mem_v = mem_ref[...]
        r = lax.rsqrt(jnp.mean(mem_v * mem_v, axis=-1, keepdims=True) + EPS)
        dgm_ref[...] = jnp.sum(dmemn * mem_v * r, axis=0, keepdims=True)

    return pl.pallas_call(
        body, name="mem_kv_bwd",
        out_shape=(jax.ShapeDtypeStruct((D_MODEL, 2 * D_MODEL), F32), jax.ShapeDtypeStruct((1, D_MODEL), F32),
                   jax.ShapeDtypeStruct((1, XHD), F32)),
        in_specs=[VMEM_SPEC] * 8, out_specs=(VMEM_SPEC,) * 3,
        scratch_shapes=[pltpu.VMEM((m_tok, 2 * D_MODEL), BF)],
        compiler_params=_cparams(),
    )(dkn, dv, kraw, mem, memn, g_mem, g_xk, w_xkv)


def _fox_bwd(q_aug, k_aug, proj, dmf, o32, lse, sums):
    t_len = q_aug.shape[0]
    tb = min(ATT_BLOCK, t_len)
    n_b = t_len // tb
    nsub = 2 if n_b >= 2 else 1
    tg = nsub * tb
    n_g = t_len // tg
    v_col = 6 * GROUP_W // LANES
    n_w = len(sums)
    n_steps = (N_HEADS // 2) * n_g

    def body(*refs):
        k_ref, v_ref, q_ref, do_ref, o_ref, lse_ref = refs[:6]
        dq_ref, dk_ref, dv_ref, df_ref = refs[6 + n_w:10 + n_w]
        delta = refs[10 + 2 * n_w]
        comm = (refs[6:6 + n_w], refs[10 + n_w:10 + 2 * n_w]) + tuple(refs[11 + 2 * n_w:])
        j = pl.program_id(1)
        step = pl.program_id(0) * n_g + j

        @pl.when(step == 0)
        def _():
            _scatter_phase(0, *comm)

        @pl.when(j == 0)
        def _():
            dq_ref[...] = jnp.zeros_like(dq_ref)
            dd = do_ref[...].astype(F32) * o_ref[...]
            hrow = lax.broadcasted_iota(jnp.int32, (8, LANES), 0)
            lane = lax.broadcasted_iota(jnp.int32, (8, LANES), 1)
            ind = ((lane // HEAD_DIM) == hrow).astype(BF)
            delta[...] = _dot_nt_exact(ind, dd)

        k2, v2 = k_ref[...], v_ref[...]
        chains = [(u, hh) for u in range(nsub) for hh in range(2)]
        ks = {(u, hh): k2[u * tb:(u + 1) * tb, hh * LANES:(hh + 1) * LANES] for u, hh in chains}
        vs = {(u, hh): v2[u * tb:(u + 1) * tb, hh * HEAD_DIM:(hh + 1) * HEAD_DIM] for u, hh in chains}

        def block(i, carry, which, masked):
            rows = pl.ds(pl.multiple_of(i * tb, tb), tb)
            q2 = q_ref[rows, :]
            do2 = do_ref[rows, :]
            qs = [q2[:, hh * LANES:(hh + 1) * LANES] for hh in range(2)]
            dos = [do2[:, hh * HEAD_DIM:(hh + 1) * HEAD_DIM] for hh in range(2)]
            ss = {ch: _dot_nt(ks[ch], qs[ch[1]]) for ch in which}
            dps = {ch: _dot_nt(vs[ch], dos[ch[1]]) for ch in which}
            pts, dsts, dfs = {}, {}, {}
            for ch in which:
                hh = ch[1]
                s_t = ss[ch]
                if ch in masked:
                    krow = lax.broadcasted_iota(jnp.int32, (tb, tb), 0)
                    qcol = lax.broadcasted_iota(jnp.int32, (tb, tb), 1)
                    s_t = jnp.where(qcol >= krow, s_t, NEG)
                p_t = jnp.exp2(s_t - lse_ref[0, hh:hh + 1, rows])
                pts[ch] = p_t.astype(BF)
                ds_t = p_t * (dps[ch] - delta[hh:hh + 1, rows])
                dsts[ch] = ds_t.astype(BF)
                dfs[ch] = jnp.sum(ds_t, axis=-1, keepdims=True)
            out = dict(carry)
            for ch in which:
                dk, dv, df = carry[ch]
                dv = dv + jnp.dot(pts[ch], dos[ch[1]], preferred_element_type=F32)
                dk = dk + jnp.dot(dsts[ch], qs[ch[1]], preferred_element_type=F32)
                out[ch] = (dk, dv, df - dfs[ch])
            for hh in range(2):
                parts_dq = [_dot_tn(dsts[ch], ks[ch])[:, :HEAD_DIM] for ch in which if ch[1] == hh]
                dq_ref[rows, hh * HEAD_DIM:(hh + 1) * HEAD_DIM] += sum(parts_dq[1:], parts_dq[0])
            return out

        init = {ch: (jnp.zeros((tb, LANES), F32), jnp.zeros((tb, HEAD_DIM), F32), jnp.zeros((tb, 1), F32)) for ch in chains}
        first = nsub * j
        carry = block(first, init, [(0, 0), (0, 1)], [(0, 0), (0, 1)])
        if nsub == 2:
            carry = block(first + 1, carry, chains, [(1, 0), (1, 1)])
        carry = lax.fori_loop(first + nsub, n_b, lambda i, c: block(i, c, chains, ()), carry)
        for u in range(nsub):
            rs = slice(u * tb, (u + 1) * tb)
            dk_ref[rs, :] = jnp.concatenate([carry[u, hh][0][:, :HEAD_DIM] for hh in range(2)], axis=-1) * LN2
            dv_ref[rs, :] = jnp.concatenate([carry[u, hh][1] for hh in range(2)], axis=-1)
            df_ref[0, rs, :] = jnp.concatenate([carry[u, hh][2] for hh in range(2)], axis=-1)

        @pl.when(step == n_steps - 1)
        def _():
            _scatter_phase(1, *comm)

    blk = lambda w, col0: pl.BlockSpec((tg, w), lambda hp, j: (j, col0 + hp))
    whole = lambda w: pl.BlockSpec((t_len, w), lambda hp, j: (0, hp))
    rows2 = pl.BlockSpec((1, 2, t_len), lambda hp, j: (hp, 0, 0))
    cols2 = pl.BlockSpec((1, tg, 2), lambda hp, j: (hp, j, 0))
    return pl.pallas_call(
        body, name="fox_bwd", grid=(N_HEADS // 2, n_g),
        out_shape=(jax.ShapeDtypeStruct((t_len, GROUP_W), F32), jax.ShapeDtypeStruct((t_len, GROUP_W), F32),
                   jax.ShapeDtypeStruct((t_len, GROUP_W), F32), jax.ShapeDtypeStruct((N_HEADS // 2, t_len, 2), F32))
        + _scatter_out_shapes(sums),
        in_specs=[blk(2 * LANES, 0), blk(LANES, v_col), whole(2 * LANES), whole(LANES), whole(LANES), rows2] + [ANY] * n_w,
        out_specs=(whole(LANES), blk(LANES, 0), blk(LANES, 0), cols2) + (ANY,) * n_w,
        scratch_shapes=[pltpu.VMEM((8, t_len), F32)] + _scatter_scratch(n_w),
        compiler_params=_cparams(("arbitrary", "arbitrary")),
    )(k_aug, proj, q_aug, dmf, o32, lse, *sums)


def _retention_bwd(dmr, raw, proj, g_ret, rq, rk, states, tables, parts):
    t_len = rq.shape[0]
    c = min(RET_BLOCK, t_len)
    n_b = t_len // c
    wdec, qdec, kdec, cdec = tables
    v_col, g_col = 2 * GROUP_W // LANES, 3 * GROUP_W // LANES
    n_w = len(parts)
    n_steps = (N_HEADS // 2) * n_b

    def body(*refs):
        d_ref, raw_ref, rg_ref, g_ref, q_ref, k_ref, v_ref, st_ref, w_ref, wt_ref, qd_ref, kd_ref, cd_ref = refs[:13]
        dq_ref, dk_ref, dv_ref, drg_ref, dg_ref = refs[13 + n_w:18 + n_w]
        gstate = refs[18 + 2 * n_w]
        comm = (refs[13:13 + n_w], refs[18 + n_w:18 + 2 * n_w]) + tuple(refs[19 + 2 * n_w:])
        step = pl.program_id(0) * n_b + pl.program_id(1)

        @pl.when(step == 0)
        def _():
            _exchange_phase(0, *comm)

        @pl.when(pl.program_id(1) == 0)
        def _():
            gstate[...] = jnp.zeros_like(gstate)
            dg_ref[...] = jnp.zeros_like(dg_ref)

        d, raw_v, g = d_ref[...], raw_ref[...], g_ref[0]
        gate = rg_ref[...].astype(F32)
        xc = raw_v - _group_mean64(raw_v)
        r = lax.rsqrt(_group_mean64(xc * xc) + EPS)
        xh = xc * r
        sg = _sigmoid(gate)
        drg_ref[...] = d * (xh * g) * (sg * (1.0 + gate * (1.0 - sg)))
        dy = d * (gate * sg)
        dg_ref[0] += jnp.sum(dy * xh, axis=0, keepdims=True)
        dxh = dy * g
        do2 = r * (dxh - _group_mean64(dxh) - xh * _group_mean64(dxh * xh))
        q2, k2, v2 = q_ref[...], k_ref[...], v_ref[...]
        dqs, dks, dvs = [], [], []
        heads = [tuple(t[:, hh * HEAD_DIM:(hh + 1) * HEAD_DIM] for t in (q2, k2, v2, do2.astype(BF))) for hh in range(2)]
        firsts = [(_dot_nt(k, q) * wt_ref[hh], _dot_nt(do, v) * w_ref[hh], _dot_nt(v, do) * wt_ref[hh])
                  for hh, (q, k, v, do) in enumerate(heads)]
        for hh, (q, k, v, do) in enumerate(heads):
            a_t, dm, dm_t = firsts[hh]
            sp, gs = st_ref[0, 0, hh], gstate[hh]
            qd = q.astype(F32) * qd_ref[hh]
            kd = k.astype(F32) * kd_ref[hh]
            dqs.append(_dot(dm, k) + _dot_nt(do, sp) * qd_ref[hh])
            dks.append(_dot(dm_t, q) + _dot_nt(v, gs) * kd_ref[hh])
            dvs.append(_dot(a_t, do) + _dot(kd, gs))
            gstate[hh] = gs * cd_ref[hh] + _dot_tn(qd, do)
        dq_ref[...] = jnp.concatenate(dqs, axis=-1)
        dk_ref[...] = jnp.concatenate(dks, axis=-1)
        dv_ref[...] = jnp.concatenate(dvs, axis=-1)

        @pl.when(step == n_steps - 1)
        def _():
            _exchange_phase(1, *comm)

    blk = lambda col0: pl.BlockSpec((c, LANES), lambda hp, i: (n_b - 1 - i, col0 + hp))
    tab = lambda a: pl.BlockSpec((2,) + a.shape[1:], lambda hp, i: (hp, 0, 0))
    gspec = pl.BlockSpec((1, 1, LANES), lambda hp, i: (hp, 0, 0))
    return pl.pallas_call(
        body, name="retention_bwd", grid=(N_HEADS // 2, n_b),
        out_shape=(jax.ShapeDtypeStruct((t_len, GROUP_W), F32),) * 4 + (jax.ShapeDtypeStruct((N_HEADS // 2, 1, LANES), F32),)
        + _exchange_out_shapes(parts),
        in_specs=[blk(0), blk(0), blk(g_col), gspec, blk(0), blk(0), blk(v_col),
                  pl.BlockSpec((1, 1, 2, HEAD_DIM, HEAD_DIM), lambda hp, i: (hp, n_b - 1 - i, 0, 0, 0)),
                  tab(wdec), tab(wdec), tab(qdec), tab(kdec), tab(cdec)] + [ANY] * n_w,
        out_specs=(blk(0), blk(0), blk(0), blk(0), gspec) + (ANY,) * n_w,
        scratch_shapes=[pltpu.VMEM((2, HEAD_DIM, HEAD_DIM), F32)] + _exchange_scratch(n_w),
        compiler_params=_cparams(("arbitrary", "arbitrary")),
    )(dmr, raw, proj, g_ret, rq, rk, proj, states, wdec, jnp.transpose(wdec, (0, 2, 1)), qdec, kdec, cdec, *parts)


def _in_proj_bwd(x, g_mix, dh1, dq_r, dk_r, dv_r, drg, dq_f, dk_f, dv_f, df_col, proj, z, cos_t, sin_t, gq_t, gk_t, w_in_t):
    t_len = x.shape[0]
    tm = min(ROW_TILE, t_len)
    n_t = t_len // tm

    def body(x_ref, g_ref, dh1_ref, dqr_ref, dkr_ref, dvr_ref, drg_ref, dqf_ref, dkf_ref, dvf_ref, df_ref, fq_ref, fk_ref, z_ref,
             cos_ref, sin_ref, gq_ref, gk_ref, wm_ref, wf_ref,
             dproj_ref, dz_ref, dx_ref, dg_ref, dgq_ref, dgk_ref, db_ref, carry, gq_acc, gk_acc):
        i = pl.program_id(0)

        @pl.when(i == 0)
        def _():
            carry[...] = jnp.zeros_like(carry)
            gq_acc[...] = jnp.zeros_like(gq_acc)
            gk_acc[...] = jnp.zeros_like(gk_acc)
            dg_ref[...] = jnp.zeros_like(dg_ref)
            db_ref[...] = jnp.zeros_like(db_ref)

        c, s = cos_ref[...], sin_ref[...]
        gq, gk = gq_ref[...], gk_ref[...]
        dgq = jnp.zeros((1, LANES), F32)
        dgk = jnp.zeros((1, LANES), F32)
        for sl in _chunks(GROUP_W):
            dy = dqr_ref[:, sl] * 0.125
            dproj_ref[:, sl] = (dy * c + _swap32(dy * s)).astype(BF)
            dy = dkr_ref[:, sl]
            dproj_ref[:, GROUP_W + sl.start:GROUP_W + sl.stop] = (dy * c + _swap32(dy * s)).astype(BF)
            dproj_ref[:, 2 * GROUP_W + sl.start:2 * GROUP_W + sl.stop] = dvr_ref[:, sl].astype(BF)
            dproj_ref[:, 3 * GROUP_W + sl.start:3 * GROUP_W + sl.stop] = drg_ref[:, sl].astype(BF)
            for src, dsrc, gain, off in ((fq_ref, dqf_ref, gq, 4), (fk_ref, dkf_ref, gk, 5)):
                xr = src[:, sl].astype(F32)
                r = lax.rsqrt(_group_mean64(xr * xr) + EPS)
                xh = xr * r
                dy = dsrc[:, sl] * (0.125 if off == 4 else 1.0)
                dgs = jnp.sum(dy * xh, axis=0, keepdims=True)
                if off == 4:
                    dgq = dgq + dgs
                else:
                    dgk = dgk + dgs
                dxh = dy * gain
                dproj_ref[:, off * GROUP_W + sl.start:off * GROUP_W + sl.stop] = \
                    (r * (dxh - xh * _group_mean64(dxh * xh))).astype(BF)
            dproj_ref[:, 6 * GROUP_W + sl.start:6 * GROUP_W + sl.stop] = dvf_ref[:, sl].astype(BF)
        gq_acc[...] += dgq
        gk_acc[...] += dgk
        row = lax.broadcasted_iota(jnp.int32, (tm, tm), 0)
        col = lax.broadcasted_iota(jnp.int32, (tm, tm), 1)
        dlf = _dot_exact((col >= row).astype(BF), df_ref[...]) + carry[0:1, :]
        carry[...] = jnp.broadcast_to(dlf[0:1, :], carry.shape)
        lane = lax.broadcasted_iota(jnp.int32, (tm, LANES), 1)
        dz = jnp.where(lane < N_HEADS, dlf / (1.0 + jnp.exp(z_ref[...])), 0.0)
        db_ref[...] += jnp.sum(dz, axis=0, keepdims=True)
        dz_bf = dz.astype(BF)
        dz_ref[...] = dz_bf
        dn1 = jnp.dot(dz_bf, wf_ref[...], preferred_element_type=F32)
        for sec in range(MAIN_W // GROUP_W):
            sl = slice(sec * GROUP_W, (sec + 1) * GROUP_W)
            dn1 = dn1 + jnp.dot(dproj_ref[:, sl], wm_ref[sl, :], preferred_element_type=F32)
        dx, dg = _rms_bwd(x_ref[...], g_ref[...], dn1)
        dx_ref[...] = dh1_ref[...] + dx
        dg_ref[...] += dg

        @pl.when(i == n_t - 1)
        def _():
            dgq_ref[...] = gq_acc[:, :HEAD_DIM] + gq_acc[:, HEAD_DIM:]
            dgk_ref[...] = gk_acc[:, :HEAD_DIM] + gk_acc[:, HEAD_DIM:]

    row_spec = lambda w, col=0: pl.BlockSpec((tm, w), lambda i: (n_t - 1 - i, col))
    full = lambda a: pl.BlockSpec(a.shape, lambda i: (0,) * a.ndim)
    acc = lambda r, c: pl.BlockSpec((r, c), lambda i: (0, 0))
    return pl.pallas_call(
        body, name="in_proj_bwd", grid=(n_t,),
        out_shape=(jax.ShapeDtypeStruct((t_len, MAIN_W), BF), jax.ShapeDtypeStruct((t_len, LANES), BF),
                   jax.ShapeDtypeStruct((t_len, D_MODEL), F32), jax.ShapeDtypeStruct((1, D_MODEL), F32),
                   jax.ShapeDtypeStruct((1, HEAD_DIM), F32), jax.ShapeDtypeStruct((1, HEAD_DIM), F32),
                   jax.ShapeDtypeStruct((1, LANES), F32)),
        in_specs=[row_spec(D_MODEL), full(g_mix), row_spec(D_MODEL)] + [row_spec(GROUP_W)] * 7
        + [row_spec(LANES), row_spec(GROUP_W, 4), row_spec(GROUP_W, 5), row_spec(LANES), row_spec(LANES), row_spec(LANES),
           full(gq_t), full(gk_t), *_w_in_specs()],
        out_specs=(row_spec(MAIN_W), row_spec(LANES), row_spec(D_MODEL), acc(1, D_MODEL), acc(1, HEAD_DIM), acc(1, HEAD_DIM),
                   acc(1, LANES)),
        scratch_shapes=[pltpu.VMEM((8, LANES), F32), pltpu.VMEM((1, LANES), F32), pltpu.VMEM((1, LANES), F32)],
        compiler_params=_cparams(("arbitrary",)),
    )(x, g_mix, dh1, dq_r, dk_r, dv_r, drg, dq_f, dk_f, dv_f, df_col, proj, proj, z, cos_t, sin_t, gq_t, gk_t, w_in_t, w_in_t)


def _matmul_tn(a, b, name, bk=1024):
    t_len, m = a.shape
    n = b.shape[1]
    bm = m if m <= TN_MAX_ROWS else m // 2
    bk = min(bk, t_len)

    def body(a_ref, b_ref, o_ref):
        @pl.when(pl.program_id(1) == 0)
        def _():
            o_ref[...] = jnp.zeros_like(o_ref)

        o_ref[...] += _dot_tn(a_ref[...], b_ref[...])

    return pl.pallas_call(
        body, name=name, grid=(m // bm, t_len // bk),
        out_shape=jax.ShapeDtypeStruct((m, n), F32),
        in_specs=[pl.BlockSpec((bk, bm), lambda i, k: (k, i)), pl.BlockSpec((bk, n), lambda i, k: (k, 0))],
        out_specs=pl.BlockSpec((bm, n), lambda i, k: (i, 0)),
        compiler_params=_cparams(("arbitrary", "arbitrary")),
    )(a, b)


def _place():
    x, y, c = lax.axis_index("x"), lax.axis_index("y"), lax.axis_index("c")
    chips = [(1 - x, y), (x, 1 - y), (1 - x, 1 - y)]
    return x, y, c, chips


def _row_chunks(rows, limit):
    step = max(d for d in range(16, min(rows, limit) + 1, 16) if rows % d == 0)
    return [slice(i, i + step) for i in range(0, rows, step)]


ICI_CHUNK_ROWS = 128
D2D_CHUNK_ROWS = 64


def _gather_phase(phase, ins, outs, send_sems, recv_sems):
    x, y, c, chips = _place()
    me_chip = 2 * x + y
    sibling = (x, y, 1 - c)

    def copy(w, k, slot, half, to, rows=slice(None), src=None):
        dst = outs[w].at[slot, half, rows]
        return pltpu.make_async_remote_copy(src_ref=dst if src is None else src, dst_ref=dst,
                                            send_sem=send_sems.at[w, k], recv_sem=recv_sems.at[w, k],
                                            device_id=to, device_id_type=MESH)

    for w in range(len(ins)):
        for j, (px, py) in enumerate(chips):
            if phase == 0:
                for rows in _row_chunks(ins[w].shape[1], ICI_CHUNK_ROWS):
                    copy(w, j, me_chip, c, (px, py, c), rows, src=ins[w].at[c, rows]).start()
            elif phase == 1:
                copy(w, j, 2 * px + py, c, (x, y, c)).wait_recv()
                for rows in _row_chunks(ins[w].shape[1], D2D_CHUNK_ROWS):
                    copy(w, 3 + j, 2 * px + py, c, sibling, rows).start()
            else:
                copy(w, 3 + j, 2 * px + py, 1 - c, (x, y, c)).wait_recv()
                copy(w, j, me_chip, c, (px, py, c), src=ins[w].at[c]).wait_send()
                copy(w, 3 + j, 2 * px + py, c, sibling).wait_send()


def _gather_scratch(n_w):
    return [pltpu.SemaphoreType.DMA((n_w, 6)), pltpu.SemaphoreType.DMA((n_w, 6))]


def _all_gather_weights(shards):
    n_w = len(shards)

    def body(*refs):
        for phase in range(3):
            _gather_phase(phase, refs[:n_w], refs[n_w:2 * n_w], *refs[2 * n_w:])

    return pl.pallas_call(
        body, name="all_gather_weights",
        out_shape=tuple(jax.ShapeDtypeStruct((4,) + s.shape, s.dtype) for s in shards),
        in_specs=[ANY] * n_w, out_specs=(ANY,) * n_w, scratch_shapes=_gather_scratch(n_w),
    )(*shards)


def _exchange_phase(phase, ins, theirs, send_sems, recv_sems):
    x, y, c, _ = _place()

    def remote(w, k=slice(None), rows=slice(None)):
        return pltpu.make_async_remote_copy(src_ref=ins[w].at[k, 1 - c, rows], dst_ref=theirs[w].at[k, rows],
                                            send_sem=send_sems.at[w], recv_sem=recv_sems.at[w], device_id=(x, y, 1 - c),
                                            device_id_type=MESH)

    for w in range(len(ins)):
        if phase == 0:
            for k in range(4):
                for rows in _row_chunks(ins[w].shape[2], D2D_CHUNK_ROWS):
                    remote(w, k, rows).start()
        else:
            remote(w).wait()


def _exchange_scratch(n_w):
    return [pltpu.SemaphoreType.DMA((n_w,)), pltpu.SemaphoreType.DMA((n_w,))]


def _exchange_out_shapes(grads):
    return tuple(jax.ShapeDtypeStruct((4,) + g.shape[2:], g.dtype) for g in grads)


def _add_pairs(part, theirs, name, halves):
    _, _, r, c = part.shape
    rb = 32 if r % 32 == 0 else r
    n_w = len(halves)
    n_steps = r // rb

    def body(*refs):
        a_ref, b_ref = refs[:2]
        own_ref, ob_ref = refs[2 + n_w:4 + n_w]
        comm = (refs[2:2 + n_w], refs[4 + n_w:4 + 2 * n_w]) + tuple(refs[4 + 2 * n_w:])
        step = pl.program_id(0)

        @pl.when(step == 0)
        def _():
            _share_phase(0, *comm)

        my_chip = 2 * lax.axis_index("x") + lax.axis_index("y")
        ob_ref[...] = (a_ref[...] + b_ref[...]).astype(BF)
        own_ref[...] = a_ref[my_chip] + b_ref[my_chip]

        @pl.when(step == n_steps - 1)
        def _():
            _share_phase(1, *comm)

    spec = pl.BlockSpec((4, rb, c), lambda i: (0, i, 0))
    flat = pl.pallas_call(
        body, name=name, grid=(n_steps,),
        out_shape=(jax.ShapeDtypeStruct((r, c), F32), jax.ShapeDtypeStruct((4, r, c), BF)) + _share_out_shapes(halves),
        in_specs=[pl.BlockSpec((4, None, rb, c), lambda i: (0, lax.axis_index("c"), i, 0)), spec] + [ANY] * n_w,
        out_specs=(pl.BlockSpec((rb, c), lambda i: (i, 0)), spec) + (ANY,) * n_w,
        scratch_shapes=_share_scratch(n_w), compiler_params=_cparams(("arbitrary",)),
    )(part, theirs, *halves)
    return (flat[0], flat[1]), list(flat[2:])


def _scatter_phase(phase, bfs, got, send_sems, recv_sems):
    x, y, c, chips = _place()

    def remote(w, j, px, py, rows=slice(None)):
        return pltpu.make_async_remote_copy(src_ref=bfs[w].at[2 * px + py, rows], dst_ref=got[w].at[j, rows],
                                            send_sem=send_sems.at[w, j], recv_sem=recv_sems.at[w, j], device_id=(px, py, c),
                                            device_id_type=MESH)

    for w in range(len(bfs)):
        for j, (px, py) in enumerate(chips):
            if phase == 0:
                for rows in _row_chunks(bfs[w].shape[1], ICI_CHUNK_ROWS):
                    remote(w, j, px, py, rows).start()
            else:
                remote(w, j, px, py).wait()


def _scatter_scratch(n_w):
    return [pltpu.SemaphoreType.DMA((n_w, 3)), pltpu.SemaphoreType.DMA((n_w, 3))]


def _scatter_out_shapes(sums_bf16):
    return tuple(jax.ShapeDtypeStruct((3,) + s.shape[1:], BF) for s in sums_bf16)


def _add_received(own, got, name):
    r, c = own.shape
    rb = 32 if r % 32 == 0 else r

    def body(o_ref, g_ref, out_ref):
        out_ref[...] = ((o_ref[...] + g_ref[0].astype(F32)) + g_ref[1].astype(F32)) + g_ref[2].astype(F32)

    return pl.pallas_call(
        body, name=name, grid=(r // rb,), out_shape=jax.ShapeDtypeStruct((r, c), F32),
        in_specs=[pl.BlockSpec((rb, c), lambda i: (i, 0)), pl.BlockSpec((3, rb, c), lambda i: (0, i, 0))],
        out_specs=pl.BlockSpec((rb, c), lambda i: (i, 0)), compiler_params=_cparams(("arbitrary",)),
    )(own, got)


def _share_phase(phase, ins, outs, send_sems, recv_sems):
    x, y, c, _ = _place()

    def remote(w, rows=slice(None)):
        return pltpu.make_async_remote_copy(src_ref=ins[w].at[rows], dst_ref=outs[w].at[c, rows], send_sem=send_sems.at[w],
                                            recv_sem=recv_sems.at[w], device_id=(x, y, 1 - c), device_id_type=MESH)

    for w in range(len(ins)):
        if phase == 0:
            for rows in _row_chunks(ins[w].shape[0], D2D_CHUNK_ROWS):
                remote(w, rows).start()
        else:
            remote(w).wait()


def _share_scratch(n_w):
    return [pltpu.SemaphoreType.DMA((n_w,)), pltpu.SemaphoreType.DMA((n_w,))]


def _share_out_shapes(halves):
    return tuple(jax.ShapeDtypeStruct((2,) + h.shape, h.dtype) for h in halves)


def _share_with_sibling(halves):
    n_w = len(halves)

    def body(*refs):
        for phase in range(2):
            _share_phase(phase, refs[:n_w], refs[n_w:2 * n_w], *refs[2 * n_w:])

    return pl.pallas_call(
        body, name="share_with_sibling", out_shape=_share_out_shapes(halves),
        in_specs=[ANY] * n_w, out_specs=(ANY,) * n_w, scratch_shapes=_share_scratch(n_w),
    )(*halves)


def _small_phase(phase, p_ref, out_ref, slots, send_sems, recv_sems):
    x, y, cc, _ = _place()
    me = 4 * x + 2 * y + cc
    copies = []
    for k in range(1, 8):
        dx, dy, dc = (k >> 2) & 1, (k >> 1) & 1, k & 1
        to = (1 - x if dx else x, 1 - y if dy else y, 1 - cc if dc else cc)
        copies.append(pltpu.make_async_remote_copy(src_ref=p_ref, dst_ref=slots.at[me], send_sem=send_sems.at[k - 1],
                                                   recv_sem=recv_sems.at[k - 1], device_id=to, device_id_type=MESH))
    if phase == 0:
        slots[me] = p_ref[...]
        for cp in copies:
            cp.start()
    else:
        for cp in copies:
            cp.wait()
        total = slots[0]
        for d in range(1, 8):
            total = total + slots[d]
        out_ref[...] = total


def _adamw_update(w_ref, g_ref, m_ref, v_ref, d_ref, nm_ref, nv_ref):
    gv = g_ref[...]
    nm = ADAM_B1 * m_ref[...] + (1.0 - ADAM_B1) * gv
    nv = ADAM_B2 * v_ref[...] + (1.0 - ADAM_B2) * (gv * gv)
    nm_ref[...] = nm
    nv_ref[...] = nv
    m_hat = nm / (1.0 - ADAM_B1 ** ADAM_STEP)
    v_hat = nv / (1.0 - ADAM_B2 ** ADAM_STEP)
    d_ref[...] = -ADAM_LR * (m_hat / (jnp.sqrt(v_hat) + ADAM_EPS) + ADAM_WD * w_ref[...])


def _adamw_many(ws, gs, ms, vs, sums, pack):
    n_a, n_w = len(ws), len(sums)
    n_steps = ADAM_STEPS
    specs = [pl.BlockSpec((w.shape[0] // n_steps, w.shape[1]), lambda i: (i, 0)) for w in ws]
    pack_spec = pl.BlockSpec(pack.shape, lambda i: (0, 0))

    def body(*refs):
        ins = refs[:4 * n_a]
        p_ref = refs[4 * n_a + n_w]
        first_out = 4 * n_a + n_w + 1
        outs = refs[first_out:first_out + 3 * n_a]
        total_ref = refs[first_out + 3 * n_a + n_w]
        scratch = refs[first_out + 3 * n_a + n_w + 1:]
        scatter = (refs[4 * n_a:4 * n_a + n_w], refs[first_out + 3 * n_a:first_out + 3 * n_a + n_w]) + tuple(scratch[:2])
        small = (p_ref, total_ref) + tuple(scratch[2:])
        step = pl.program_id(0)

        @pl.when(step == 0)
        def _():
            _scatter_phase(0, *scatter)
            _small_phase(0, *small)

        for a in range(n_a):
            _adamw_update(*(ins[k * n_a + a] for k in range(4)), *(outs[3 * a + k] for k in range(3)))

        @pl.when(step == n_steps - 1)
        def _():
            _scatter_phase(1, *scatter)
            _small_phase(1, *small)

    flat = pl.pallas_call(
        body, name="adamw_late", grid=(n_steps,),
        out_shape=tuple(jax.ShapeDtypeStruct(w.shape, F32) for w in ws for _ in range(3)) + _scatter_out_shapes(sums)
        + (jax.ShapeDtypeStruct(pack.shape, F32),),
        in_specs=specs * 4 + [ANY] * n_w + [pack_spec],
        out_specs=tuple(s for s in specs for _ in range(3)) + (ANY,) * n_w + (pack_spec,),
        scratch_shapes=_scatter_scratch(n_w) + [pltpu.VMEM((8,) + pack.shape, F32), pltpu.SemaphoreType.DMA((7,)),
                                                pltpu.SemaphoreType.DMA((7,))],
        compiler_params=_cparams(("arbitrary",)),
    )(*ws, *gs, *ms, *vs, *sums, pack)
    return [tuple(flat[3 * a:3 * a + 3]) for a in range(n_a)] + list(flat[3 * n_a:])


def _adamw(w, g, m, v, name):
    r, c = w.shape
    rb, cb = (64, c) if r % 64 == 0 else (r, LANES if (r % 8 and c % LANES == 0) else c)

    def body(*refs):
        _adamw_update(*refs)

    spec = pl.BlockSpec((rb, cb), lambda i, j: (i, j))
    return pl.pallas_call(
        body, name=name, grid=(r // rb, c // cb), out_shape=(jax.ShapeDtypeStruct((r, c), F32),) * 3,
        in_specs=[spec] * 4, out_specs=(spec,) * 3, compiler_params=_cparams(("arbitrary", "arbitrary")),
    )(w, g, m, v)


def _rope_tables(t_len):
    inv_freq = ROPE_BASE ** (-jnp.arange(0, HEAD_DIM, 2, dtype=F32) / HEAD_DIM)
    ang = jnp.arange(t_len, dtype=F32)[:, None] * inv_freq[None, :]
    cos, sin = jnp.cos(ang), jnp.sin(ang)
    cos_t = jnp.concatenate([cos, cos, cos, cos], axis=-1)
    sin_t = jnp.concatenate([-sin, sin, -sin, sin], axis=-1)
    return cos_t, sin_t


def _cols_to_shards(dw):
    r, n = dw.shape
    return jnp.transpose(dw.reshape(2, r // 2, 4, n // 4), (2, 0, 1, 3))


def _rows_to_shards(dw):
    r, n = dw.shape
    rows = r // 4
    if rows % SUBLANES == 0:
        padded = _pad_rows(dw.reshape(4, rows, n))
    else:
        window = rows + SUBLANES - rows % SUBLANES
        padded = _pad_rows(jnp.stack([dw[rows * k // SUBLANES * SUBLANES:][:window] for k in range(4)]))
    return padded.reshape(4, 2, padded.shape[1] // 2, n)


def _shard_row_offset(rows):
    return (rows * (2 * lax.axis_index("x") + lax.axis_index("y"))) % SUBLANES


def _pad_lanes(a):
    extra = -a.shape[-1] % LANES
    return a if extra == 0 else jnp.pad(a, [(0, 0)] * (a.ndim - 1) + [(0, extra)])


def _pad_rows(a):
    rows = a.shape[-2]
    extra = 0 if rows % SHARD_ROW_ALIGN == 0 else -rows % SHARD_ROW_PAD
    return a if extra == 0 else jnp.pad(a, [(0, 0)] * (a.ndim - 2) + [(0, extra), (0, 0)])


def _pad_row(a, width=D_MODEL):
    a = a.reshape(1, -1)
    return jnp.pad(a, ((0, 0), (0, width - a.shape[1])))


def kernel(x, mem, g_mix, w_in, b_forget, g_ret_out, g_fox_q, g_fox_k, w_out, g_xattn, w_xq, w_xkv, g_mem, g_xq, g_xk, w_xo, g_ffn, w_gate, w_up, w_down, loss_target, m_g_mix, m_w_in, m_b_forget, m_g_ret_out, m_g_fox_q, m_g_fox_k, m_w_out, m_g_xattn, m_w_xq, m_w_xkv, m_g_mem, m_g_xq, m_g_xk, m_w_xo, m_g_ffn, m_w_gate, m_w_up, m_w_down, v_g_mix, v_w_in, v_b_forget, v_g_ret_out, v_g_fox_q, v_g_fox_k, v_w_out, v_g_xattn, v_w_xq, v_w_xkv, v_g_mem, v_g_xq, v_g_xk, v_w_xo, v_g_ffn, v_w_gate, v_w_up, v_w_down):
    big = {"w_in": (w_in, m_w_in, v_w_in), "w_out": (w_out, m_w_out, v_w_out), "w_xq": (w_xq, m_w_xq, v_w_xq),
           "w_xkv": (w_xkv, m_w_xkv, v_w_xkv), "w_xo": (w_xo, m_w_xo, v_w_xo), "w_gate": (w_gate, m_w_gate, v_w_gate),
           "w_up": (w_up, m_w_up, v_w_up), "w_down": (w_down, m_w_down, v_w_down)}
    for n in TRANSPOSED:
        big[n] = tuple(jnp.swapaxes(a, 1, 2) for a in big[n])
    shards = {}
    for n in big:
        w = _pad_rows(_pad_lanes(big[n][0][0].astype(BF)))
        shards[n] = w.reshape(2, w.shape[0] // 2, w.shape[1])
    sizes = {n: big[n][0].shape[1:] for n in big}
    w_in_full = _assemble_weight("w_in", _all_gather_weights([shards["w_in"]])[0], shards["w_in"], sizes["w_in"])
    small_w ={"g_mix": g_mix, "b_forget": b_forget, "g_ret_out": g_ret_out, "g_fox_q": g_fox_q, "g_fox_k": g_fox_k,
               "g_xattn": g_xattn, "g_mem": g_mem, "g_xq": g_xq, "g_xk": g_xk, "g_ffn": g_ffn}
    m_small = {"g_mix": m_g_mix, "b_forget": m_b_forget, "g_ret_out": m_g_ret_out, "g_fox_q": m_g_fox_q, "g_fox_k": m_g_fox_k,
               "g_xattn": m_g_xattn, "g_mem": m_g_mem, "g_xq": m_g_xq, "g_xk": m_g_xk, "g_ffn": m_g_ffn}
    v_small = {"g_mix": v_g_mix, "b_forget": v_b_forget, "g_ret_out": v_g_ret_out, "g_fox_q": v_g_fox_q, "g_fox_k": v_g_fox_k,
               "g_xattn": v_g_xattn, "g_mem": v_g_mem, "g_xq": v_g_xq, "g_xk": v_g_xk, "g_ffn": v_g_ffn}
    loss_part, grad_x, sums, got, in_parts, small_g = _local_step(x[0], mem[0], loss_target[0], w_in_full, shards, sizes, small_w)
    return _reduce_and_update(big, sums, got, in_parts, small_w, small_g, loss_part, grad_x, m_small, v_small)


def _assemble_weight(name, gathered, own, size):
    rows, width = size
    my_chip = 2 * lax.axis_index("x") + lax.axis_index("y")
    g = lax.dynamic_update_slice(gathered, own[None], (my_chip, 0, 0, 0))
    g = g.reshape(4, 2 * g.shape[2], g.shape[3])[:, :rows, :width]
    return jnp.transpose(g, (1, 0, 2)).reshape(rows, 4 * width) if name in COL_SHARDED else g.reshape(4 * rows, width)


def _shard_parts(names, dw):
    return [_pad_lanes(_cols_to_shards(dw[n]) if n in COL_SHARDED else _rows_to_shards(dw[n])) for n in names]


def _add_pairs_many(parts, theirs, name):
    n_a = len(parts)
    n_steps = min(p.shape[2] for p in parts) // 32
    rb = [p.shape[2] // n_steps for p in parts]
    part_specs = [pl.BlockSpec((4, None, r, p.shape[3]), lambda i: (0, lax.axis_index("c"), i, 0)) for p, r in zip(parts, rb)]
    quad_specs = [pl.BlockSpec((4, r, p.shape[3]), lambda i: (0, i, 0)) for p, r in zip(parts, rb)]
    own_specs = [pl.BlockSpec((r, p.shape[3]), lambda i: (i, 0)) for p, r in zip(parts, rb)]

    def body(*refs):
        my_chip = 2 * lax.axis_index("x") + lax.axis_index("y")
        for a in range(n_a):
            a_ref, b_ref, own_ref, ob_ref = refs[a], refs[n_a + a], refs[2 * n_a + a], refs[3 * n_a + a]
            ob_ref[...] = (a_ref[...] + b_ref[...]).astype(BF)
            own_ref[...] = a_ref[my_chip] + b_ref[my_chip]

    flat = pl.pallas_call(
        body, name=name, grid=(n_steps,),
        out_shape=tuple(jax.ShapeDtypeStruct(p.shape[2:], F32) for p in parts)
        + tuple(jax.ShapeDtypeStruct((4,) + p.shape[2:], BF) for p in parts),
        in_specs=part_specs + quad_specs, out_specs=tuple(own_specs) + tuple(quad_specs),
        compiler_params=_cparams(("arbitrary",)),
    )(*parts, *theirs)
    return [(flat[a], flat[n_a + a]) for a in range(n_a)]


def _core_sums(parts, theirs):
    out = [None] * len(parts)
    for tag, pick in (("a", lambda p: p.shape[2] % LANES == 0), ("b", lambda p: p.shape[2] % LANES != 0)):
        idx = [i for i, p in enumerate(parts) if pick(p)]
        for i, res in zip(idx, _add_pairs_many([parts[i] for i in idx], [theirs[i] for i in idx], f"core_sum_late_{tag}")):
            out[i] = res
    return out


def _local_step(xs, mems, tgt, w_in_full, shards, sizes, small_w):
    g_mix, b_forget, g_ret_out, g_fox_q, g_fox_k = (small_w[n] for n in ("g_mix", "b_forget", "g_ret_out", "g_fox_q", "g_fox_k"))
    g_xattn, g_mem, g_xq, g_xk, g_ffn = (small_w[n] for n in ("g_xattn", "g_mem", "g_xq", "g_xk", "g_ffn"))
    w_in_t = jnp.pad(w_in_full, ((0, MAIN_W + LANES - IN_W), (0, 0)))
    t_len = xs.shape[0]
    cos_t, sin_t = _rope_tables(t_len)
    tables = _decay_tables(min(RET_BLOCK, t_len))
    gq_t = jnp.concatenate([g_fox_q, g_fox_q], axis=-1)
    gk_t = jnp.concatenate([g_fox_k, g_fox_k], axis=-1)
    b_pad = _pad_row(b_forget, LANES)
    g_ret = g_ret_out.reshape(N_HEADS // 2, 1, LANES)

    n1, proj, rq, rk, q_aug, k_aug, z = _in_proj_fwd(xs, g_mix, w_in_t, b_pad, cos_t, sin_t, gq_t, gk_t)
    raw, mix_r, states = _retention_fwd(rq, rk, proj, g_ret, tables)
    mix_f, o32, lse, *gathered = _fox_fwd(q_aug, k_aug, proj, [shards[n] for n in LATE])
    full = {n: _assemble_weight(n, g, shards[n], sizes[n]) for n, g in zip(LATE, gathered)}
    memn, kraw, kn, vmem = _mem_kv_fwd(mems, g_mem, full["w_xkv"], g_xk)
    h1, hn2, qx, o_x, h2 = _attn_out_xattn_fwd(xs, mix_r, mix_f, full["w_out"], g_xattn, full["w_xq"], g_xq, kn, vmem, full["w_xo"])
    hn3, gate, up, act, dh3, loss_part = _ffn_loss_fwd(h2, g_ffn, full["w_gate"], full["w_up"], full["w_down"], tgt)

    dgate, dup, dh2, dg_ffn = _ffn_bwd(dh3, gate, up, h2, g_ffn, full["w_gate"], full["w_up"], full["w_down"])
    dqx, dh1, dmr, dmf, dkn, dvm, dg_xattn, dg_xq = _attn_out_xattn_bwd(dh2, h1, qx, kn, vmem, full["w_xo"], full["w_xq"],
                                                                      full["w_out"], g_xattn, g_xq)
    dw_xkv, dg_mem, dg_xk = _mem_kv_bwd(dkn, dvm, kraw, mems, memn, g_mem, g_xk, full["w_xkv"])
    dw_gu = _matmul_tn_pair(dgate, dup, hn3, "dw_gate_up")
    dw = {
        "w_out": _matmul_tn_pair(mix_r, mix_f, dh1, "dw_out").reshape(D_MODEL, D_MODEL),
        "w_xq": _matmul_tn(hn2, dqx, "dw_xq"),
        "w_xkv": dw_xkv,
        "w_xo": _matmul_tn(o_x, dh2, "dw_xo"),
        "w_gate": dw_gu[0],
        "w_up": dw_gu[1],
        "w_down": _matmul_tn(act, dh3, "dw_down"),
    }
    late_parts = _shard_parts(LATE, dw)
    dq_r, dk_r, dv_r, drg, dg_ret, *late_theirs = _retention_bwd(dmr, raw, proj, g_ret, rq, rk, states, tables, late_parts)
    late_sums = _core_sums(late_parts, late_theirs)
    dq_f, dk_f, dv_f, df, *late_got = _fox_bwd(q_aug, k_aug, proj, dmf, o32, lse, [s[1] for s in late_sums])
    df_col = jnp.pad(jnp.transpose(df, (1, 0, 2)).reshape(t_len, N_HEADS), ((0, 0), (0, LANES - N_HEADS)))
    dproj, dz, grad_x, dg_mix, dg_fq, dg_fk, db = _in_proj_bwd(xs, g_mix, dh1, dq_r, dk_r, dv_r, drg, dq_f, dk_f, dv_f, df_col,
                                                              proj, z, cos_t, sin_t, gq_t, gk_t, w_in_t)

    dw_in = jnp.concatenate([_matmul_tn(dproj, n1, "dw_in_main"), _matmul_tn(dz, n1, "dw_in_ff")[:IN_W - MAIN_W]], axis=0)
    in_parts = _shard_parts(("w_in",), {"w_in": dw_in})
    sums = {n: s[0] for n, s in zip(LATE, late_sums)}
    got = dict(zip(LATE, late_got))
    small_g = {"g_mix": dg_mix, "b_forget": db[:, :N_HEADS], "g_ret_out": dg_ret, "g_fox_q": dg_fq, "g_fox_k": dg_fk,
               "g_xattn": dg_xattn, "g_mem": dg_mem, "g_xq": dg_xq, "g_xk": dg_xk, "g_ffn": dg_ffn}
    return loss_part, grad_x, sums, got, in_parts, small_g


def _add_received_many(owns, gots, parts):
    n_a, n_w = len(owns), len(parts)
    n_steps = CHIP_SUM_STEPS
    own_specs = [pl.BlockSpec((o.shape[0] // n_steps, o.shape[1]), lambda i: (i, 0)) for o in owns]
    got_specs = [pl.BlockSpec((3, o.shape[0] // n_steps, o.shape[1]), lambda i: (0, i, 0)) for o in owns]

    def body(*refs):
        first_out = 2 * n_a + n_w
        comm = (refs[2 * n_a:first_out], refs[first_out + n_a:first_out + n_a + n_w]) + tuple(refs[first_out + n_a + n_w:])
        step = pl.program_id(0)

        @pl.when(step == 0)
        def _():
            _exchange_phase(0, *comm)

        for a in range(n_a):
            o_ref, g_ref, out_ref = refs[a], refs[n_a + a], refs[first_out + a]
            out_ref[...] = ((o_ref[...] + g_ref[0].astype(F32)) + g_ref[1].astype(F32)) + g_ref[2].astype(F32)

        @pl.when(step == n_steps - 1)
        def _():
            _exchange_phase(1, *comm)

    flat = pl.pallas_call(
        body, name="chip_sum_late", grid=(n_steps,),
        out_shape=tuple(jax.ShapeDtypeStruct(o.shape, F32) for o in owns) + _exchange_out_shapes(parts),
        in_specs=own_specs + got_specs + [ANY] * n_w, out_specs=tuple(own_specs) + (ANY,) * n_w,
        scratch_shapes=_exchange_scratch(n_w), compiler_params=_cparams(("arbitrary",)),
    )(*owns, *gots, *parts)
    return list(flat[:n_a]), list(flat[n_a:])


def _final_grads(names, big, finals, shared):
    my_core = lax.axis_index("c")
    out = {}
    for n, s, fin in zip(names, shared, finals):
        s = lax.dynamic_update_slice(s, fin[None], (my_core, 0, 0))
        s = s.reshape(2 * s.shape[1], s.shape[2])
        rows, width = big[n][0].shape[1:]
        out[n] = s[:rows, :width] if rows % SUBLANES == 0 else lax.dynamic_slice(s, (_shard_row_offset(rows), 0), (rows, width))
    return out


def _reduce_and_update(big, sums, got, in_parts, small_w, small_g, loss_part, grad_x, m_small, v_small):
    small_names = list(small_w)
    pad_rows = SMALL_ROWS - len(small_names) - 1
    stack = lambda d: jnp.concatenate([_pad_row(d[n]) for n in small_names] + [jnp.zeros((pad_rows + 1, D_MODEL), F32)], axis=0)
    g_pack = jnp.concatenate([_pad_row(small_g[n]) for n in small_names] + [_pad_row(loss_part[0:1, 0:1])]
                             + [jnp.zeros((pad_rows, D_MODEL), F32)], axis=0)
    late_finals, in_theirs = _add_received_many([sums[n] for n in LATE], [got[n] for n in LATE], in_parts)
    (in_own, in_bf), late_shared = _add_pairs(in_parts[0], in_theirs[0], "core_sum_w_in", late_finals)
    grads = _final_grads(LATE, big, late_finals, late_shared)
    *late_updates, in_got, g_tot = _adamw_many([big[n][0][0] for n in LATE], [grads[n] for n in LATE], [big[n][1][0] for n in LATE],
                                               [big[n][2][0] for n in LATE], [in_bf], g_pack)
    updates = dict(zip(LATE, late_updates))
    in_final = [_add_received(in_own, in_got, "chip_sum_w_in")]
    grads.update(_final_grads(("w_in",), big, in_final, _share_with_sibling(in_final)))
    updates["w_in"] = _adamw(big["w_in"][0][0], grads["w_in"], big["w_in"][1][0], big["w_in"][2][0], "adamw_w_in")
    deltas, new_m, new_v = {}, {}, {}
    for n in big:
        restore = (lambda a: jnp.swapaxes(a[None], 1, 2)) if n in TRANSPOSED else (lambda a: a[None])
        grads[n] = restore(grads[n])
        deltas[n], new_m[n], new_v[n] = (restore(a) for a in updates[n])

    d_s, m_s, v_s = _adamw(stack(small_w), g_tot, stack(m_small), stack(v_small), "adamw_small")
    for i, n in enumerate(small_names):
        shape = small_w[n].shape
        size = int(np.prod(shape))
        grads[n] = g_tot[i, :size].reshape(shape)
        deltas[n], new_m[n], new_v[n] = d_s[i, :size].reshape(shape), m_s[i, :size].reshape(shape), v_s[i, :size].reshape(shape)
    loss = g_tot[len(small_names), 0]

    order = ["g_mix", "w_in", "b_forget", "g_ret_out", "g_fox_q", "g_fox_k", "w_out", "g_xattn", "w_xq", "w_xkv", "g_mem", "g_xq",
             "g_xk", "w_xo", "g_ffn", "w_gate", "w_up", "w_down"]
    return (loss, grad_x[None], *[grads[n] for n in order], *[deltas[n] for n in order], *[new_m[n] for n in order],
            *[new_v[n] for n in order])
```

```python
import functools

import numpy as np
import jax
import jax.numpy as jnp
from jax import lax
from jax.experimental import pallas as pl
from jax.experimental.pallas import tpu as pltpu

F32 = jnp.float32
BF = jnp.bfloat16

D_MODEL = 1024
HEAD_DIM = 64
N_HEADS = 8
GROUP_W = 512
N_XH = 4
XHD = 256
D_FF = 2816
MAIN_W = 3584
IN_W = 3592
ROPE_BASE = 10000.0
LOG2E = 1.4426950408889634
LN2 = 0.6931471805599453
EPS = 1e-6
NEG = -1e30
LANES = 128
SUBLANES = 8
RET_BLOCK = 256
REF_CHUNK = 64
ROW_TILE = 512
FFN_BWD_TILE = 256
ATT_BLOCK = 256
FWD_GROUP = 4
TN_MAX_ROWS = 1408
SMALL_ROWS = 16
COL_SHARDED = ("w_xkv",)
TRANSPOSED = ("w_in", "w_gate", "w_up")
SHARD_ROW_ALIGN = 32
SHARD_ROW_PAD = 256
LATE = ("w_out", "w_xq", "w_xkv", "w_xo", "w_gate", "w_up", "w_down")
VMEM_LIMIT = 56 * 1024 * 1024

ADAM_LR = 0.001
ADAM_B1 = 0.9
ADAM_B2 = 0.999
ADAM_EPS = 1e-08
ADAM_WD = 0.01
ADAM_STEP = 10
CHIP_SUM_STEPS = 2
ADAM_STEPS = 8

MESH = pl.DeviceIdType.MESH
ANY = pl.BlockSpec(memory_space=pl.ANY)
VMEM_SPEC = pl.BlockSpec(memory_space=pltpu.VMEM)


def _cparams(sem=None, vmem=VMEM_LIMIT):
    return pltpu.CompilerParams(dimension_semantics=sem, vmem_limit_bytes=vmem)


def _dot(a, b):
    return jnp.dot(a.astype(BF), b.astype(BF), preferred_element_type=F32)


def _dot_nt(a, b):
    return lax.dot_general(a.astype(BF), b.astype(BF), (((1,), (1,)), ((), ())), preferred_element_type=F32)


def _dot_tn(a, b):
    return lax.dot_general(a.astype(BF), b.astype(BF), (((0,), (0,)), ((), ())), preferred_element_type=F32)


def _split3(x):
    hi = x.astype(BF)
    r = x - hi.astype(F32)
    mid = r.astype(BF)
    lo = (r - mid.astype(F32)).astype(BF)
    return hi, mid, lo


def _dot_exact(ind, x):
    hi, mid, lo = _split3(x)
    return (jnp.dot(ind, lo, preferred_element_type=F32) + jnp.dot(ind, mid, preferred_element_type=F32)
            + jnp.dot(ind, hi, preferred_element_type=F32))


def _dot_nt_exact(ind, x):
    hi, mid, lo = _split3(x)
    dn = (((1,), (1,)), ((), ()))
    return (lax.dot_general(ind, lo, dn, preferred_element_type=F32) + lax.dot_general(ind, mid, dn, preferred_element_type=F32)
            + lax.dot_general(ind, hi, dn, preferred_element_type=F32))


def _sigmoid(x):
    return 1.0 / (1.0 + jnp.exp(-x))


def _rms_fwd(x, g):
    r = lax.rsqrt(jnp.mean(x * x, axis=-1, keepdims=True) + EPS)
    return x * r * g


def _rms_bwd(x, g, dy):
    r = lax.rsqrt(jnp.mean(x * x, axis=-1, keepdims=True) + EPS)
    xh = x * r
    dg = jnp.sum(dy * xh, axis=0, keepdims=True)
    dxh = dy * g
    dx = r * (dxh - xh * jnp.mean(dxh * xh, axis=-1, keepdims=True))
    return dx, dg


def _group_mean64(x):
    lane = lax.broadcasted_iota(jnp.int32, x.shape, 1)
    lo = lane < HEAD_DIM
    s_lo = jnp.sum(jnp.where(lo, x, 0.0), axis=-1, keepdims=True)
    s_hi = jnp.sum(jnp.where(lo, 0.0, x), axis=-1, keepdims=True)
    return jnp.where(lo, s_lo, s_hi) * (1.0 / HEAD_DIM)


def _swap32(x):
    lane = lax.broadcasted_iota(jnp.int32, x.shape, 1)
    first = (lane % HEAD_DIM) < (HEAD_DIM // 2)
    return jnp.where(first, pltpu.roll(x, LANES - HEAD_DIM // 2, axis=1), pltpu.roll(x, HEAD_DIM // 2, axis=1))


def _chunks(w):
    return [slice(j * LANES, (j + 1) * LANES) for j in range(w // LANES)]


def _aug_pair(qk, f_cols, is_query):
    lane = lax.broadcasted_iota(jnp.int32, qk.shape, 1)
    a = lane - HEAD_DIM
    values = (qk, pltpu.roll(qk, HEAD_DIM, axis=1))
    out = []
    for hh in range(2):
        hi, mid, lo = (p.astype(F32) for p in _split3(f_cols[hh] * LOG2E))
        if is_query:
            aux = jnp.where(a == 0, hi, jnp.where(a == 1, mid, jnp.where(a == 2, lo, jnp.where(a < 6, 1.0, 0.0))))
        else:
            aux = jnp.where(a < 3, 1.0, jnp.where(a == 3, -hi, jnp.where(a == 4, -mid, jnp.where(a == 5, -lo, 0.0))))
        out.append(jnp.where(a < 0, values[hh], aux))
    return jnp.concatenate(out, axis=-1).astype(BF)


def _mem_kv_fwd(mem, g_mem, w_xkv, g_xk):
    m_tok = mem.shape[0]

    def body(mem_ref, gm_ref, w_ref, gk_ref, memn_ref, kraw_ref, kn_ref, v_ref):
        mn = _rms_fwd(mem_ref[...], gm_ref[...]).astype(BF)
        memn_ref[...] = mn
        kv = jnp.dot(mn, w_ref[...], preferred_element_type=F32)
        k = kv[:, :D_MODEL]
        kraw_ref[...] = k
        v_ref[...] = kv[:, D_MODEL:].astype(BF)
        for h in range(N_XH):
            sl = slice(h * XHD, (h + 1) * XHD)
            kn_ref[:, sl] = _rms_fwd(k[:, sl], gk_ref[...]).astype(BF)

    return pl.pallas_call(
        body, name="mem_kv_fwd",
        out_shape=(jax.ShapeDtypeStruct((m_tok, D_MODEL), BF), jax.ShapeDtypeStruct((m_tok, D_MODEL), F32),
                   jax.ShapeDtypeStruct((m_tok, D_MODEL), BF), jax.ShapeDtypeStruct((m_tok, D_MODEL), BF)),
        in_specs=[VMEM_SPEC] * 4, out_specs=(VMEM_SPEC,) * 4, compiler_params=_cparams(),
    )(mem, g_mem, w_xkv, g_xk)


def _in_proj_fwd(x, g_mix, w_in_t, b_pad, cos_t, sin_t, gq_t, gk_t):
    t_len = x.shape[0]
    tm = min(ROW_TILE, t_len)
    n_t = t_len // tm

    def body(x_ref, g_ref, wm_ref, wf_ref, b_ref, cos_ref, sin_ref, gq_ref, gk_ref,
             n1_ref, proj_ref, rq_ref, rk_ref, qa_ref, ka_ref, z_ref, carry):
        i = pl.program_id(0)

        @pl.when(i == 0)
        def _():
            carry[...] = jnp.zeros_like(carry)

        n1 = _rms_fwd(x_ref[...], g_ref[...]).astype(BF)
        n1_ref[...] = n1
        z = _dot_nt(n1, wf_ref[...]) + b_ref[...]
        z_ref[...] = z
        lane = lax.broadcasted_iota(jnp.int32, z.shape, 1)
        lf = jnp.where(lane < N_HEADS, jnp.minimum(z, 0.0) - jnp.log(1.0 + jnp.exp(-jnp.abs(z))), 0.0)
        row = lax.broadcasted_iota(jnp.int32, (tm, tm), 0)
        col = lax.broadcasted_iota(jnp.int32, (tm, tm), 1)
        tri = (row >= col).astype(BF)
        fc = _dot_exact(tri, lf) + carry[0:1, :]
        carry[...] = jnp.broadcast_to(fc[tm - 1:tm, :], carry.shape)
        c, s = cos_ref[...], sin_ref[...]

        def section(n):
            p = _dot_nt(n1, wm_ref[n * GROUP_W:(n + 1) * GROUP_W, :])
            proj_ref[:, n * GROUP_W:(n + 1) * GROUP_W] = p.astype(BF)
            return p

        def rotate(p, out_ref, scale):
            for sl in _chunks(GROUP_W):
                out_ref[:, sl] = ((p[:, sl] * c + _swap32(p[:, sl]) * s) * scale).astype(BF)

        def norm_aug(p, gain, out_ref, scale, is_query):
            for j, sl in enumerate(_chunks(GROUP_W)):
                f = p[:, sl]
                f = f * lax.rsqrt(_group_mean64(f * f) + EPS) * gain * scale
                out_ref[:, 2 * j * LANES:2 * (j + 1) * LANES] = _aug_pair(f, [fc[:, 2 * j:2 * j + 1], fc[:, 2 * j + 1:2 * j + 2]], is_query)

        p_rq, p_rk = section(0), section(1)
        rotate(p_rq, rq_ref, 0.125)
        section(2)
        rotate(p_rk, rk_ref, 1.0)
        section(3)
        p_fq = section(4)
        p_fk = section(5)
        norm_aug(p_fq, gq_ref[...], qa_ref, 0.125 * LOG2E, True)
        section(6)
        norm_aug(p_fk, gk_ref[...], ka_ref, 1.0, False)

    row_spec = lambda w: pl.BlockSpec((tm, w), lambda i: (i, 0))
    full = lambda a: pl.BlockSpec(a.shape, lambda i: (0,) * a.ndim)
    return pl.pallas_call(
        body, name="in_proj_fwd", grid=(n_t,),
        out_shape=(jax.ShapeDtypeStruct((t_len, D_MODEL), BF), jax.ShapeDtypeStruct((t_len, MAIN_W), BF),
                   jax.ShapeDtypeStruct((t_len, GROUP_W), BF), jax.ShapeDtypeStruct((t_len, GROUP_W), BF),
                   jax.ShapeDtypeStruct((t_len, 2 * GROUP_W), BF), jax.ShapeDtypeStruct((t_len, 2 * GROUP_W), BF),
                   jax.ShapeDtypeStruct((t_len, LANES), F32)),
        in_specs=[row_spec(D_MODEL), full(g_mix), *_w_in_specs(), full(b_pad), row_spec(LANES), row_spec(LANES),
                  full(gq_t), full(gk_t)],
        out_specs=(row_spec(D_MODEL), row_spec(MAIN_W), row_spec(GROUP_W), row_spec(GROUP_W), row_spec(2 * GROUP_W),
                   row_spec(2 * GROUP_W), row_spec(LANES)),
        scratch_shapes=[pltpu.VMEM((8, LANES), F32)],
        compiler_params=_cparams(("arbitrary",)),
    )(x, g_mix, w_in_t, w_in_t, b_pad, cos_t, sin_t, gq_t, gk_t)


def _w_in_specs():
    return (pl.BlockSpec((MAIN_W, D_MODEL), lambda i: (0, 0)), pl.BlockSpec((LANES, D_MODEL), lambda i: (MAIN_W // LANES, 0)))


def _matmul_tn_pair(a1, a2, b, name, bk=1024):
    t_len, m = a1.shape
    n = b.shape[1]
    bm = m if m <= TN_MAX_ROWS else m // 2
    bk = min(bk, t_len)

    def body(a1_ref, a2_ref, b_ref, o_ref):
        @pl.when(pl.program_id(1) == 0)
        def _():
            o_ref[...] = jnp.zeros_like(o_ref)

        bv = b_ref[...]
        o_ref[0] += _dot_tn(a1_ref[...], bv)
        o_ref[1] += _dot_tn(a2_ref[...], bv)

    a_spec = pl.BlockSpec((bk, bm), lambda i, k: (k, i))
    return pl.pallas_call(
        body, name=name, grid=(m // bm, t_len // bk),
        out_shape=jax.ShapeDtypeStruct((2, m, n), F32),
        in_specs=[a_spec, a_spec, pl.BlockSpec((bk, n), lambda i, k: (k, 0))],
        out_specs=pl.BlockSpec((2, bm, n), lambda i, k: (0, i, 0)),
        compiler_params=_cparams(("arbitrary", "arbitrary")),
    )(a1, a2, b)


def _decay_tables(c):
    h = np.arange(N_HEADS, dtype=np.float64)
    lg = np.log(1.0 - 2.0 ** (-5.0 - h)).astype(np.float32).astype(np.float64)
    t = np.arange(c)
    same_or_earlier = (t[None, :] // REF_CHUNK) <= (t[:, None] // REF_CHUNK)
    w = np.where(same_or_earlier[None], np.exp(lg[:, None, None] * np.abs(t[:, None] - t[None, :])[None]), 0.0)
    qd = np.exp(lg[:, None] * (t[None, :] + 1.0))
    kd = np.exp(lg[:, None] * (c - 1.0 - t[None, :]))
    cd = np.exp(lg * c)
    ones = np.ones((1, 1, HEAD_DIM))
    return (jnp.asarray(w, F32), jnp.asarray(qd[:, :, None] * ones, F32), jnp.asarray(kd[:, :, None] * ones, F32),
            jnp.asarray(cd[:, None, None] * np.ones((1, HEAD_DIM, HEAD_DIM)), F32))


def _retention_fwd(rq, rk, proj, g_ret, tables):
    t_len = rq.shape[0]
    c = min(RET_BLOCK, t_len)
    n_b = t_len // c
    wdec, qdec, kdec, cdec = tables
    v_col, g_col = 2 * GROUP_W // LANES, 3 * GROUP_W // LANES

    def body(q_ref, k_ref, v_ref, rg_ref, g_ref, w_ref, qd_ref, kd_ref, cd_ref, raw_ref, mix_ref, st_ref, state):
        i = pl.program_id(1)

        @pl.when(i == 0)
        def _():
            state[...] = jnp.zeros_like(state)

        q2, k2, v2 = q_ref[...], k_ref[...], v_ref[...]
        heads = [tuple(t[:, hh * HEAD_DIM:(hh + 1) * HEAD_DIM] for t in (q2, k2, v2)) for hh in range(2)]
        scores = [(_dot_nt(q, k) * w_ref[hh]).astype(BF) for hh, (q, k, _) in enumerate(heads)]
        outs = []
        for hh, (q, k, v) in enumerate(heads):
            sp = state[hh]
            st_ref[0, 0, hh] = sp
            outs.append(jnp.dot(scores[hh], v, preferred_element_type=F32) + _dot(q.astype(F32) * qd_ref[hh], sp))
            state[hh] = sp * cd_ref[hh] + _dot_tn(k.astype(F32) * kd_ref[hh], v)
        o2 = jnp.concatenate(outs, axis=-1)
        raw_ref[...] = o2
        xc = o2 - _group_mean64(o2)
        xh = xc * lax.rsqrt(_group_mean64(xc * xc) + EPS)
        gate = rg_ref[...].astype(F32)
        mix_ref[...] = (gate * _sigmoid(gate) * (xh * g_ref[0])).astype(BF)

    blk = lambda col0: pl.BlockSpec((c, LANES), lambda hp, i: (i, col0 + hp))
    tab = lambda a: pl.BlockSpec((2,) + a.shape[1:], lambda hp, i: (hp, 0, 0))
    return pl.pallas_call(
        body, name="retention_fwd", grid=(N_HEADS // 2, n_b),
        out_shape=(jax.ShapeDtypeStruct((t_len, GROUP_W), F32), jax.ShapeDtypeStruct((t_len, GROUP_W), BF),
                   jax.ShapeDtypeStruct((N_HEADS // 2, n_b, 2, HEAD_DIM, HEAD_DIM), F32)),
        in_specs=[blk(0), blk(0), blk(v_col), blk(g_col), pl.BlockSpec((1, 1, LANES), lambda hp, i: (hp, 0, 0)),
                  tab(wdec), tab(qdec), tab(kdec), tab(cdec)],
        out_specs=(blk(0), blk(0), pl.BlockSpec((1, 1, 2, HEAD_DIM, HEAD_DIM), lambda hp, i: (hp, i, 0, 0, 0))),
        scratch_shapes=[pltpu.VMEM((2, HEAD_DIM, HEAD_DIM), F32)],
        compiler_params=_cparams(("arbitrary", "arbitrary")),
    )(rq, rk, proj, proj, g_ret, wdec, qdec, kdec, cdec)


def _fox_fwd(q_aug, k_aug, proj, shards):
    t_len = q_aug.shape[0]
    tq = min(ATT_BLOCK, t_len)
    nsub = min(FWD_GROUP, t_len // tq)
    tg = nsub * tq
    n_q = t_len // tg
    v_col = 6 * GROUP_W // LANES
    tc = min(512, t_len)
    n_w = len(shards)
    n_steps = (N_HEADS // 2) * n_q

    def body(*refs):
        q_ref, k_ref, v_ref = refs[:3]
        o_ref, o32_ref, lse_ref = refs[3 + n_w:6 + n_w]
        vt = refs[6 + 2 * n_w]
        comm = (refs[3:3 + n_w], refs[6 + n_w:6 + 2 * n_w]) + tuple(refs[7 + 2 * n_w:])
        i = pl.program_id(1)
        step = pl.program_id(0) * n_q + i

        @pl.when(step == 0)
        def _():
            _gather_phase(0, *comm)

        @pl.when(step == (3 * n_steps) // 4)
        def _():
            _gather_phase(1, *comm)

        @pl.when(i == 0)
        def _():
            for c0 in range(0, t_len, tc):
                vt[:, c0:c0 + tc] = v_ref[c0:c0 + tc, :].T

        chains = [(u, hh) for u in range(nsub) for hh in range(2)]
        qs = {(u, hh): q_ref[u * tq:(u + 1) * tq, hh * LANES:(hh + 1) * LANES] for u, hh in chains}
        ones = jnp.ones((HEAD_DIM, tq), BF)

        def scores(j, which):
            k2 = k_ref[pl.ds(pl.multiple_of(j * tq, tq), tq), :]
            return {ch: _dot_nt(k2[:, ch[1] * LANES:(ch[1] + 1) * LANES], qs[ch]) for ch in which}

        def update(j, ss, carry, masked):
            v2 = vt[:, pl.ds(pl.multiple_of(j * tq, tq), tq)]
            ps, stats = {}, {}
            for ch in ss:
                m = carry[ch][0]
                s_t = ss[ch]
                if ch in masked:
                    krow = lax.broadcasted_iota(jnp.int32, (tq, tq), 0)
                    qcol = lax.broadcasted_iota(jnp.int32, (tq, tq), 1)
                    s_t = jnp.where(qcol >= krow, s_t, NEG)
                m_new = jnp.maximum(m, jnp.max(s_t, axis=0, keepdims=True))
                ps[ch] = jnp.exp2(s_t - m_new).astype(BF)
                stats[ch] = (m_new, jnp.exp2(m - m_new))
            out = dict(carry)
            for ch in ss:
                m_new, alpha = stats[ch]
                v_aug = jnp.concatenate([v2[ch[1] * HEAD_DIM:(ch[1] + 1) * HEAD_DIM, :], ones], axis=0)
                out[ch] = (m_new, carry[ch][1] * alpha + jnp.dot(v_aug, ps[ch], preferred_element_type=F32))
            return out

        def advance(j, state):
            ss, carry = state
            return scores(j + 1, chains), update(j, ss, carry, ())

        init = {ch: (jnp.full((1, tq), NEG, F32), jnp.zeros((LANES, tq), F32)) for ch in chains}
        first = nsub * i
        ss, carry = lax.fori_loop(0, first, advance, (scores(0, chains), init))
        carry = update(first, ss, carry, [(0, 0), (0, 1)])
        for u in range(1, nsub):
            rest = [(uu, hh) for uu in range(u, nsub) for hh in range(2)]
            carry = update(first + u, scores(first + u, rest), carry, [(u, 0), (u, 1)])
        for u in range(nsub):
            outs, lses = [], []
            for hh in range(2):
                m, acc = carry[u, hh]
                l = acc[HEAD_DIM:HEAD_DIM + 1, :]
                outs.append(acc[:HEAD_DIM, :] / l)
                lses.append(m + jnp.log2(l))
            o2 = jnp.concatenate(outs, axis=0).T
            o32_ref[u * tq:(u + 1) * tq, :] = o2
            o_ref[u * tq:(u + 1) * tq, :] = o2.astype(BF)
            lse_ref[0, :, u * tq:(u + 1) * tq] = jnp.concatenate(lses, axis=0)

        @pl.when(step == n_steps - 1)
        def _():
            _gather_phase(2, *comm)

    return pl.pallas_call(
        body, name="fox_fwd", grid=(N_HEADS // 2, n_q),
        out_shape=(jax.ShapeDtypeStruct((t_len, GROUP_W), BF), jax.ShapeDtypeStruct((t_len, GROUP_W), F32),
                   jax.ShapeDtypeStruct((N_HEADS // 2, 2, t_len), F32))
        + tuple(jax.ShapeDtypeStruct((4,) + s.shape, s.dtype) for s in shards),
        in_specs=[pl.BlockSpec((tg, 2 * LANES), lambda hp, i: (i, hp)),
                  pl.BlockSpec((t_len, 2 * LANES), lambda hp, i: (0, hp)),
                  pl.BlockSpec((t_len, LANES), lambda hp, i: (0, v_col + hp))] + [ANY] * n_w,
        out_specs=(pl.BlockSpec((tg, LANES), lambda hp, i: (i, hp)), pl.BlockSpec((tg, LANES), lambda hp, i: (i, hp)),
                   pl.BlockSpec((1, 2, tg), lambda hp, i: (hp, 0, i))) + (ANY,) * n_w,
        scratch_shapes=[pltpu.VMEM((LANES, t_len), BF)] + _gather_scratch(n_w),
        compiler_params=_cparams(("arbitrary", "arbitrary")),
    )(q_aug, k_aug, proj, *shards)


def _softmax_rows(s):
    p = jnp.exp(s - jnp.max(s, axis=-1, keepdims=True))
    return p / jnp.sum(p, axis=-1, keepdims=True)


def _attn_out_xattn_fwd(x, mix_r, mix_f, w_out, g_xattn, w_xq, g_xq, kn, v, w_xo):
    t_len = x.shape[0]
    tm = min(ROW_TILE, t_len)

    def body(x_ref, mr_ref, mf_ref, wo_ref, g_ref, wq_ref, gq_ref, kn_ref, v_ref, wxo_ref,
             h1_ref, hn_ref, qx_ref, o_ref, h2_ref):
        h1 = x_ref[...] + jnp.dot(mr_ref[...], wo_ref[:GROUP_W, :], preferred_element_type=F32) \
            + jnp.dot(mf_ref[...], wo_ref[GROUP_W:, :], preferred_element_type=F32)
        h1_ref[...] = h1
        hn = _rms_fwd(h1, g_ref[...]).astype(BF)
        hn_ref[...] = hn
        qx = jnp.dot(hn, wq_ref[...], preferred_element_type=F32).astype(BF)
        qx_ref[...] = qx
        sls = [slice(h * XHD, (h + 1) * XHD) for h in range(N_XH)]
        qns = [_rms_fwd(qx[:, sl].astype(F32), gq_ref[...]).astype(BF) for sl in sls]
        logits = [_dot_nt(qn, kn_ref[:, sl]) * (XHD ** -0.5) for qn, sl in zip(qns, sls)]
        ps = [_softmax_rows(s).astype(BF) for s in logits]
        for p, sl in zip(ps, sls):
            o_ref[:, sl] = jnp.dot(p, v_ref[:, sl], preferred_element_type=F32).astype(BF)
        h2_ref[...] = h1 + jnp.dot(o_ref[...], wxo_ref[...], preferred_element_type=F32)

    row_spec = lambda w: pl.BlockSpec((tm, w), lambda i: (i, 0))
    full = lambda a: pl.BlockSpec(a.shape, lambda i: (0,) * a.ndim)
    return pl.pallas_call(
        body, name="attn_out_xattn_fwd", grid=(t_len // tm,),
        out_shape=(jax.ShapeDtypeStruct((t_len, D_MODEL), F32), jax.ShapeDtypeStruct((t_len, D_MODEL), BF),
                   jax.ShapeDtypeStruct((t_len, D_MODEL), BF), jax.ShapeDtypeStruct((t_len, D_MODEL), BF),
                   jax.ShapeDtypeStruct((t_len, D_MODEL), F32)),
        in_specs=[row_spec(D_MODEL), row_spec(GROUP_W), row_spec(GROUP_W), full(w_out), full(g_xattn), full(w_xq), full(g_xq),
                  full(kn), full(v), full(w_xo)],
        out_specs=(row_spec(D_MODEL),) * 5,
        compiler_params=_cparams(("arbitrary",)),
    )(x, mix_r, mix_f, w_out, g_xattn, w_xq, g_xq, kn, v, w_xo)


def _ffn_loss_fwd(h2, g_ffn, w_gate, w_up, w_down, target):
    t_len = h2.shape[0]
    tm = min(ROW_TILE, t_len)

    def body(h2_ref, g_ref, wg_ref, wu_ref, wd_ref, tgt_ref, hn_ref, gate_ref, up_ref, act_ref, dh3_ref, loss_ref):
        @pl.when(pl.program_id(0) == 0)
        def _():
            loss_ref[...] = jnp.zeros_like(loss_ref)

        h2v = h2_ref[...]
        hn = _rms_fwd(h2v, g_ref[...]).astype(BF)
        hn_ref[...] = hn
        gate = _dot_nt(hn, wg_ref[...])
        up = _dot_nt(hn, wu_ref[...])
        gate_ref[...] = gate.astype(BF)
        up_ref[...] = up.astype(BF)
        act = (gate * _sigmoid(gate) * up).astype(BF)
        act_ref[...] = act
        diff = h2v + jnp.dot(act, wd_ref[...], preferred_element_type=F32) - tgt_ref[...]
        dh3_ref[...] = diff * (1.0 / D_MODEL)
        per_row = jnp.sum(diff * diff, axis=-1, keepdims=True) * (1.0 / D_MODEL)
        loss_ref[...] += 0.5 * jnp.sum(per_row, axis=0, keepdims=True)

    row_spec = lambda w: pl.BlockSpec((tm, w), lambda i: (i, 0))
    full = lambda a: pl.BlockSpec(a.shape, lambda i: (0,) * a.ndim, pipeline_mode=pl.Buffered(1))
    return pl.pallas_call(
        body, name="ffn_loss_fwd", grid=(t_len // tm,),
        out_shape=(jax.ShapeDtypeStruct((t_len, D_MODEL), BF), jax.ShapeDtypeStruct((t_len, D_FF), BF),
                   jax.ShapeDtypeStruct((t_len, D_FF), BF), jax.ShapeDtypeStruct((t_len, D_FF), BF),
                   jax.ShapeDtypeStruct((t_len, D_MODEL), F32), jax.ShapeDtypeStruct((8, LANES), F32)),
        in_specs=[row_spec(D_MODEL), full(g_ffn), full(w_gate), full(w_up), full(w_down), row_spec(D_MODEL)],
        out_specs=(row_spec(D_MODEL), row_spec(D_FF), row_spec(D_FF), row_spec(D_FF), row_spec(D_MODEL),
                   pl.BlockSpec((8, LANES), lambda i: (0, 0))),
        compiler_params=_cparams(("arbitrary",)),
    )(h2, g_ffn, w_gate, w_up, w_down, target)


def _ffn_bwd(dh3, gate, up, h2, g_ffn, w_gate, w_up, w_down):
    t_len = h2.shape[0]
    tm = min(FFN_BWD_TILE, t_len)

    def body(dh3_ref, gate_ref, up_ref, h2_ref, g_ref, wg_ref, wu_ref, wd_ref, dgate_ref, dup_ref, dh2_ref, dg_ref):
        @pl.when(pl.program_id(0) == 0)
        def _():
            dg_ref[...] = jnp.zeros_like(dg_ref)

        dh3v = dh3_ref[...]
        dact = _dot_nt(dh3v, wd_ref[...])
        g = gate_ref[...].astype(F32)
        sg = _sigmoid(g)
        dup = (dact * (g * sg)).astype(BF)
        dgate = (dact * up_ref[...].astype(F32) * (sg * (1.0 + g * (1.0 - sg)))).astype(BF)
        dup_ref[...] = dup
        dgate_ref[...] = dgate
        dhn = jnp.dot(dgate, wg_ref[...], preferred_element_type=F32) + jnp.dot(dup, wu_ref[...], preferred_element_type=F32)
        dx, dg = _rms_bwd(h2_ref[...], g_ref[...], dhn)
        dh2_ref[...] = dh3v + dx
        dg_ref[...] += dg

    row_spec = lambda w: pl.BlockSpec((tm, w), lambda i: (i, 0))
    full = lambda a: pl.BlockSpec(a.shape, lambda i: (0,) * a.ndim, pipeline_mode=pl.Buffered(1))
    return pl.pallas_call(
        body, name="ffn_bwd", grid=(t_len // tm,),
        out_shape=(jax.ShapeDtypeStruct((t_len, D_FF), BF), jax.ShapeDtypeStruct((t_len, D_FF), BF),
                   jax.ShapeDtypeStruct((t_len, D_MODEL), F32), jax.ShapeDtypeStruct((1, D_MODEL), F32)),
        in_specs=[row_spec(D_MODEL), row_spec(D_FF), row_spec(D_FF), row_spec(D_MODEL), full(g_ffn), full(w_gate), full(w_up),
                  full(w_down)],
        out_specs=(row_spec(D_FF), row_spec(D_FF), row_spec(D_MODEL), pl.BlockSpec((1, D_MODEL), lambda i: (0, 0))),
        compiler_params=_cparams(("arbitrary",)),
    )(dh3, gate, up, h2, g_ffn, w_gate, w_up, w_down)


def _attn_out_xattn_bwd(dh2, h1, qx, kn, v, w_xo, w_xq, w_out, g_xattn, g_xq):
    t_len = h1.shape[0]
    tm = min(ROW_TILE, t_len)
    m_tok = kn.shape[0]

    def body(dh2_ref, h1_ref, qx_ref, kn_ref, v_ref, wxo_ref, wq_ref, wo_ref, g_ref, gq_ref,
             dqx_ref, dh1_ref, dmr_ref, dmf_ref, dkn_ref, dv_ref, dg_ref, dgq_ref, dqx_scr):
        @pl.when(pl.program_id(0) == 0)
        def _():
            dkn_ref[...] = jnp.zeros_like(dkn_ref)
            dv_ref[...] = jnp.zeros_like(dv_ref)
            dg_ref[...] = jnp.zeros_like(dg_ref)
            dgq_ref[...] = jnp.zeros_like(dgq_ref)

        dh2v = dh2_ref[...]
        do = _dot_nt(dh2v, wxo_ref[...])
        gq = gq_ref[...]
        sls = [slice(h * XHD, (h + 1) * XHD) for h in range(N_XH)]
        qraws = [qx_ref[:, sl].astype(F32) for sl in sls]
        qns = [_rms_fwd(qraw, gq).astype(BF) for qraw in qraws]
        dohs = [do[:, sl].astype(BF) for sl in sls]
        logits = [_dot_nt(qn, kn_ref[:, sl]) * (XHD ** -0.5) for qn, sl in zip(qns, sls)]
        dps = [_dot_nt(doh, v_ref[:, sl]) for doh, sl in zip(dohs, sls)]
        ps = [_softmax_rows(s) for s in logits]
        dss = [(p * (dp - jnp.sum(dp * p, axis=-1, keepdims=True)) * (XHD ** -0.5)).astype(BF) for p, dp in zip(ps, dps)]
        dqns = []
        for h, sl in enumerate(sls):
            dv_ref[:, sl] += _dot_tn(ps[h], dohs[h])
            dqns.append(jnp.dot(dss[h], kn_ref[:, sl], preferred_element_type=F32))
            dkn_ref[:, sl] += _dot_tn(dss[h], qns[h])
        dgq = jnp.zeros((1, XHD), F32)
        for h, sl in enumerate(sls):
            dx, dg_h = _rms_bwd(qraws[h], gq, dqns[h])
            dgq = dgq + dg_h
            dqx_scr[:, sl] = dx.astype(BF)
        dgq_ref[...] += dgq
        dqx = dqx_scr[...]
        dqx_ref[...] = dqx
        dhn = _dot_nt(dqx, wq_ref[...])
        dx, dg = _rms_bwd(h1_ref[...], g_ref[...], dhn)
        dg_ref[...] += dg
        dh1 = dh2v + dx
        dh1_ref[...] = dh1
        dmix = _dot_nt(dh1, wo_ref[...])
        dmr_ref[...] = dmix[:, :GROUP_W]
        dmf_ref[...] = dmix[:, GROUP_W:].astype(BF)

    row_spec = lambda w: pl.BlockSpec((tm, w), lambda i: (i, 0))
    full = lambda a: pl.BlockSpec(a.shape, lambda i: (0,) * a.ndim)
    acc = lambda r, c: pl.BlockSpec((r, c), lambda i: (0, 0))
    return pl.pallas_call(
        body, name="attn_out_xattn_bwd", grid=(t_len // tm,),
        out_shape=(jax.ShapeDtypeStruct((t_len, D_MODEL), BF), jax.ShapeDtypeStruct((t_len, D_MODEL), F32),
                   jax.ShapeDtypeStruct((t_len, GROUP_W), F32), jax.ShapeDtypeStruct((t_len, GROUP_W), BF),
                   jax.ShapeDtypeStruct((m_tok, D_MODEL), F32), jax.ShapeDtypeStruct((m_tok, D_MODEL), F32),
                   jax.ShapeDtypeStruct((1, D_MODEL), F32), jax.ShapeDtypeStruct((1, XHD), F32)),
        in_specs=[row_spec(D_MODEL), row_spec(D_MODEL), row_spec(D_MODEL), full(kn), full(v), full(w_xo), full(w_xq), full(w_out),
                  full(g_xattn), full(g_xq)],
        out_specs=(row_spec(D_MODEL), row_spec(D_MODEL), row_spec(GROUP_W), row_spec(GROUP_W), acc(m_tok, D_MODEL),
                   acc(m_tok, D_MODEL), acc(1, D_MODEL), acc(1, XHD)),
        scratch_shapes=[pltpu.VMEM((tm, D_MODEL), BF)],
        compiler_params=_cparams(("arbitrary",)),
    )(dh2, h1, qx, kn, v, w_xo, w_xq, w_out, g_xattn, g_xq)


def _mem_kv_bwd(dkn, dv, kraw, mem, memn, g_mem, g_xk, w_xkv):
    m_tok = mem.shape[0]

    def body(dkn_ref, dv_ref, kraw_ref, mem_ref, memn_ref, gm_ref, gk_ref, w_ref, dw_ref, dgm_ref, dgk_ref, dkv_scr):
        gk = gk_ref[...]
        dgk = jnp.zeros((1, XHD), F32)
        for h in range(N_XH):
            sl = slice(h * XHD, (h + 1) * XHD)
            dx, dg_h = _rms_bwd(kraw_ref[:, sl], gk, dkn_ref[:, sl])
            dgk = dgk + dg_h
            dkv_scr[:, sl] = dx.astype(BF)
        dgk_ref[...] = dgk
        dkv_scr[:, D_MODEL:] = dv_ref[...].astype(BF)
        dkv = dkv_scr[...]
        dw_ref[...] = _dot_tn(memn_ref[...], dkv)
        dmemn = _dot_nt(dkv, w_ref[...])
        mem_v = mem_ref[...]
        r = lax.rsqrt(jnp.mean(mem_v * mem_v, axis=-1, keepdims=True) + EPS)
        dgm_ref[...] = jnp.sum(dmemn * mem_v * r, axis=0, keepdims=True)

    return pl.pallas_call(
        body, name="mem_kv_bwd",
        out_shape=(jax.ShapeDtypeStruct((D_MODEL, 2 * D_MODEL), F32), jax.ShapeDtypeStruct((1, D_MODEL), F32),
                   jax.ShapeDtypeStruct((1, XHD), F32)),
        in_specs=[VMEM_SPEC] * 8, out_specs=(VMEM_SPEC,) * 3,
        scratch_shapes=[pltpu.VMEM((m_tok, 2 * D_MODEL), BF)],
        compiler_params=_cparams(),
    )(dkn, dv, kraw, mem, memn, g_mem, g_xk, w_xkv)


def _fox_bwd(q_aug, k_aug, proj, dmf, o32, lse, sums):
    t_len = q_aug.shape[0]
    tb = min(ATT_BLOCK, t_len)
    n_b = t_len // tb
    nsub = 2 if n_b >= 2 else 1
    tg = nsub * tb
    n_g = t_len // tg
    v_col = 6 * GROUP_W // LANES
    n_w = len(sums)
    n_steps = (N_HEADS // 2) * n_g

    def body(*refs):
        k_ref, v_ref, q_ref, do_ref, o_ref, lse_ref = refs[:6]
        dq_ref, dk_ref, dv_ref, df_ref = refs[6 + n_w:10 + n_w]
        delta = refs[10 + 2 * n_w]
        comm = (refs[6:6 + n_w], refs[10 + n_w:10 + 2 * n_w]) + tuple(refs[11 + 2 * n_w:])
        j = pl.program_id(1)
        step = pl.program_id(0) * n_g + j

        @pl.when(step == 0)
        def _():
            _scatter_phase(0, *comm)

        @pl.when(j == 0)
        def _():
            dq_ref[...] = jnp.zeros_like(dq_ref)
            dd = do_ref[...].astype(F32) * o_ref[...]
            hrow = lax.broadcasted_iota(jnp.int32, (8, LANES), 0)
            lane = lax.broadcasted_iota(jnp.int32, (8, LANES), 1)
            ind = ((lane // HEAD_DIM) == hrow).astype(BF)
            delta[...] = _dot_nt_exact(ind, dd)

        k2, v2 = k_ref[...], v_ref[...]
        chains = [(u, hh) for u in range(nsub) for hh in range(2)]
        ks = {(u, hh): k2[u * tb:(u + 1) * tb, hh * LANES:(hh + 1) * LANES] for u, hh in chains}
        vs = {(u, hh): v2[u * tb:(u + 1) * tb, hh * HEAD_DIM:(hh + 1) * HEAD_DIM] for u, hh in chains}

        def block(i, carry, which, masked):
            rows = pl.ds(pl.multiple_of(i * tb, tb), tb)
            q2 = q_ref[rows, :]
            do2 = do_ref[rows, :]
            qs = [q2[:, hh * LANES:(hh + 1) * LANES] for hh in range(2)]
            dos = [do2[:, hh * HEAD_DIM:(hh + 1) * HEAD_DIM] for hh in range(2)]
            ss = {ch: _dot_nt(ks[ch], qs[ch[1]]) for ch in which}
            dps = {ch: _dot_nt(vs[ch], dos[ch[1]]) for ch in which}
            pts, dsts, dfs = {}, {}, {}
            for ch in which:
                hh = ch[1]
                s_t = ss[ch]
                if ch in masked:
                    krow = lax.broadcasted_iota(jnp.int32, (tb, tb), 0)
                    qcol = lax.broadcasted_iota(jnp.int32, (tb, tb), 1)
                    s_t = jnp.where(qcol >= krow, s_t, NEG)
                p_t = jnp.exp2(s_t - lse_ref[0, hh:hh + 1, rows])
                pts[ch] = p_t.astype(BF)
                ds_t = p_t * (dps[ch] - delta[hh:hh + 1, rows])
                dsts[ch] = ds_t.astype(BF)
                dfs[ch] = jnp.sum(ds_t, axis=-1, keepdims=True)
            out = dict(carry)
            for ch in which:
                dk, dv, df = carry[ch]
                dv = dv + jnp.dot(pts[ch], dos[ch[1]], preferred_element_type=F32)
                dk = dk + jnp.dot(dsts[ch], qs[ch[1]], preferred_element_type=F32)
                out[ch] = (dk, dv, df - dfs[ch])
            for hh in range(2):
                parts_dq = [_dot_tn(dsts[ch], ks[ch])[:, :HEAD_DIM] for ch in which if ch[1] == hh]
                dq_ref[rows, hh * HEAD_DIM:(hh + 1) * HEAD_DIM] += sum(parts_dq[1:], parts_dq[0])
            return out

        init = {ch: (jnp.zeros((tb, LANES), F32), jnp.zeros((tb, HEAD_DIM), F32), jnp.zeros((tb, 1), F32)) for ch in chains}
        first = nsub * j
        carry = block(first, init, [(0, 0), (0, 1)], [(0, 0), (0, 1)])
        if nsub == 2:
            carry = block(first + 1, carry, chains, [(1, 0), (1, 1)])
        carry = lax.fori_loop(first + nsub, n_b, lambda i, c: block(i, c, chains, ()), carry)
        for u in range(nsub):
            rs = slice(u * tb, (u + 1) * tb)
            dk_ref[rs, :] = jnp.concatenate([carry[u, hh][0][:, :HEAD_DIM] for hh in range(2)], axis=-1) * LN2
            dv_ref[rs, :] = jnp.concatenate([carry[u, hh][1] for hh in range(2)], axis=-1)
            df_ref[0, rs, :] = jnp.concatenate([carry[u, hh][2] for hh in range(2)], axis=-1)

        @pl.when(step == n_steps - 1)
        def _():
            _scatter_phase(1, *comm)

    blk = lambda w, col0: pl.BlockSpec((tg, w), lambda hp, j: (j, col0 + hp))
    whole = lambda w: pl.BlockSpec((t_len, w), lambda hp, j: (0, hp))
    rows2 = pl.BlockSpec((1, 2, t_len), lambda hp, j: (hp, 0, 0))
    cols2 = pl.BlockSpec((1, tg, 2), lambda hp, j: (hp, j, 0))
    return pl.pallas_call(
        body, name="fox_bwd", grid=(N_HEADS // 2, n_g),
        out_shape=(jax.ShapeDtypeStruct((t_len, GROUP_W), F32), jax.ShapeDtypeStruct((t_len, GROUP_W), F32),
                   jax.ShapeDtypeStruct((t_len, GROUP_W), F32), jax.ShapeDtypeStruct((N_HEADS // 2, t_len, 2), F32))
        + _scatter_out_shapes(sums),
        in_specs=[blk(2 * LANES, 0), blk(LANES, v_col), whole(2 * LANES), whole(LANES), whole(LANES), rows2] + [ANY] * n_w,
        out_specs=(whole(LANES), blk(LANES, 0), blk(LANES, 0), cols2) + (ANY,) * n_w,
        scratch_shapes=[pltpu.VMEM((8, t_len), F32)] + _scatter_scratch(n_w),
        compiler_params=_cparams(("arbitrary", "arbitrary")),
    )(k_aug, proj, q_aug, dmf, o32, lse, *sums)


def _retention_bwd(dmr, raw, proj, g_ret, rq, rk, states, tables, parts):
    t_len = rq.shape[0]
    c = min(RET_BLOCK, t_len)
    n_b = t_len // c
    wdec, qdec, kdec, cdec = tables
    v_col, g_col = 2 * GROUP_W // LANES, 3 * GROUP_W // LANES
    n_w = len(parts)
    n_steps = (N_HEADS // 2) * n_b

    def body(*refs):
        d_ref, raw_ref, rg_ref, g_ref, q_ref, k_ref, v_ref, st_ref, w_ref, wt_ref, qd_ref, kd_ref, cd_ref = refs[:13]
        dq_ref, dk_ref, dv_ref, drg_ref, dg_ref = refs[13 + n_w:18 + n_w]
        gstate = refs[18 + 2 * n_w]
        comm = (refs[13:13 + n_w], refs[18 + n_w:18 + 2 * n_w]) + tuple(refs[19 + 2 * n_w:])
        step = pl.program_id(0) * n_b + pl.program_id(1)

        @pl.when(step == 0)
        def _():
            _exchange_phase(0, *comm)

        @pl.when(pl.program_id(1) == 0)
        def _():
            gstate[...] = jnp.zeros_like(gstate)
            dg_ref[...] = jnp.zeros_like(dg_ref)

        d, raw_v, g = d_ref[...], raw_ref[...], g_ref[0]
        gate = rg_ref[...].astype(F32)
        xc = raw_v - _group_mean64(raw_v)
        r = lax.rsqrt(_group_mean64(xc * xc) + EPS)
        xh = xc * r
        sg = _sigmoid(gate)
        drg_ref[...] = d * (xh * g) * (sg * (1.0 + gate * (1.0 - sg)))
        dy = d * (gate * sg)
        dg_ref[0] += jnp.sum(dy * xh, axis=0, keepdims=True)
        dxh = dy * g
        do2 = r * (dxh - _group_mean64(dxh) - xh * _group_mean64(dxh * xh))
        q2, k2, v2 = q_ref[...], k_ref[...], v_ref[...]
        dqs, dks, dvs = [], [], []
        heads = [tuple(t[:, hh * HEAD_DIM:(hh + 1) * HEAD_DIM] for t in (q2, k2, v2, do2.astype(BF))) for hh in range(2)]
        firsts = [(_dot_nt(k, q) * wt_ref[hh], _dot_nt(do, v) * w_ref[hh], _dot_nt(v, do) * wt_ref[hh])
                  for hh, (q, k, v, do) in enumerate(heads)]
        for hh, (q, k, v, do) in enumerate(heads):
            a_t, dm, dm_t = firsts[hh]
            sp, gs = st_ref[0, 0, hh], gstate[hh]
            qd = q.astype(F32) * qd_ref[hh]
            kd = k.astype(F32) * kd_ref[hh]
            dqs.append(_dot(dm, k) + _dot_nt(do, sp) * qd_ref[hh])
            dks.append(_dot(dm_t, q) + _dot_nt(v, gs) * kd_ref[hh])
            dvs.append(_dot(a_t, do) + _dot(kd, gs))
            gstate[hh] = gs * cd_ref[hh] + _dot_tn(qd, do)
        dq_ref[...] = jnp.concatenate(dqs, axis=-1)
        dk_ref[...] = jnp.concatenate(dks, axis=-1)
        dv_ref[...] = jnp.concatenate(dvs, axis=-1)

        @pl.when(step == n_steps - 1)
        def _():
            _exchange_phase(1, *comm)

    blk = lambda col0: pl.BlockSpec((c, LANES), lambda hp, i: (n_b - 1 - i, col0 + hp))
    tab = lambda a: pl.BlockSpec((2,) + a.shape[1:], lambda hp, i: (hp, 0, 0))
    gspec = pl.BlockSpec((1, 1, LANES), lambda hp, i: (hp, 0, 0))
    return pl.pallas_call(
        body, name="retention_bwd", grid=(N_HEADS // 2, n_b),
        out_shape=(jax.ShapeDtypeStruct((t_len, GROUP_W), F32),) * 4 + (jax.ShapeDtypeStruct((N_HEADS // 2, 1, LANES), F32),)
        + _exchange_out_shapes(parts),
        in_specs=[blk(0), blk(0), blk(g_col), gspec, blk(0), blk(0), blk(v_col),
                  pl.BlockSpec((1, 1, 2, HEAD_DIM, HEAD_DIM), lambda hp, i: (hp, n_b - 1 - i, 0, 0, 0)),
                  tab(wdec), tab(wdec), tab(qdec), tab(kdec), tab(cdec)] + [ANY] * n_w,
        out_specs=(blk(0), blk(0), blk(0), blk(0), gspec) + (ANY,) * n_w,
        scratch_shapes=[pltpu.VMEM((2, HEAD_DIM, HEAD_DIM), F32)] + _exchange_scratch(n_w),
        compiler_params=_cparams(("arbitrary", "arbitrary")),
    )(dmr, raw, proj, g_ret, rq, rk, proj, states, wdec, jnp.transpose(wdec, (0, 2, 1)), qdec, kdec, cdec, *parts)


def _in_proj_bwd(x, g_mix, dh1, dq_r, dk_r, dv_r, drg, dq_f, dk_f, dv_f, df_col, proj, z, cos_t, sin_t, gq_t, gk_t, w_in_t):
    t_len = x.shape[0]
    tm = min(ROW_TILE, t_len)
    n_t = t_len // tm

    def body(x_ref, g_ref, dh1_ref, dqr_ref, dkr_ref, dvr_ref, drg_ref, dqf_ref, dkf_ref, dvf_ref, df_ref, fq_ref, fk_ref, z_ref,
             cos_ref, sin_ref, gq_ref, gk_ref, wm_ref, wf_ref,
             dproj_ref, dz_ref, dx_ref, dg_ref, dgq_ref, dgk_ref, db_ref, carry, gq_acc, gk_acc):
        i = pl.program_id(0)

        @pl.when(i == 0)
        def _():
            carry[...] = jnp.zeros_like(carry)
            gq_acc[...] = jnp.zeros_like(gq_acc)
            gk_acc[...] = jnp.zeros_like(gk_acc)
            dg_ref[...] = jnp.zeros_like(dg_ref)
            db_ref[...] = jnp.zeros_like(db_ref)

        c, s = cos_ref[...], sin_ref[...]
        gq, gk = gq_ref[...], gk_ref[...]
        dgq = jnp.zeros((1, LANES), F32)
        dgk = jnp.zeros((1, LANES), F32)
        for sl in _chunks(GROUP_W):
            dy = dqr_ref[:, sl] * 0.125
            dproj_ref[:, sl] = (dy * c + _swap32(dy * s)).astype(BF)
            dy = dkr_ref[:, sl]
            dproj_ref[:, GROUP_W + sl.start:GROUP_W + sl.stop] = (dy * c + _swap32(dy * s)).astype(BF)
            dproj_ref[:, 2 * GROUP_W + sl.start:2 * GROUP_W + sl.stop] = dvr_ref[:, sl].astype(BF)
            dproj_ref[:, 3 * GROUP_W + sl.start:3 * GROUP_W + sl.stop] = drg_ref[:, sl].astype(BF)
            for src, dsrc, gain, off in ((fq_ref, dqf_ref, gq, 4), (fk_ref, dkf_ref, gk, 5)):
                xr = src[:, sl].astype(F32)
                r = lax.rsqrt(_group_mean64(xr * xr) + EPS)
                xh = xr * r
                dy = dsrc[:, sl] * (0.125 if off == 4 else 1.0)
                dgs = jnp.sum(dy * xh, axis=0, keepdims=True)
                if off == 4:
                    dgq = dgq + dgs
                else:
                    dgk = dgk + dgs
                dxh = dy * gain
                dproj_ref[:, off * GROUP_W + sl.start:off * GROUP_W + sl.stop] = \
                    (r * (dxh - xh * _group_mean64(dxh * xh))).astype(BF)
            dproj_ref[:, 6 * GROUP_W + sl.start:6 * GROUP_W + sl.stop] = dvf_ref[:, sl].astype(BF)
        gq_acc[...] += dgq
        gk_acc[...] += dgk
        row = lax.broadcasted_iota(jnp.int32, (tm, tm), 0)
        col = lax.broadcasted_iota(jnp.int32, (tm, tm), 1)
        dlf = _dot_exact((col >= row).astype(BF), df_ref[...]) + carry[0:1, :]
        carry[...] = jnp.broadcast_to(dlf[0:1, :], carry.shape)
        lane = lax.broadcasted_iota(jnp.int32, (tm, LANES), 1)
        dz = jnp.where(lane < N_HEADS, dlf / (1.0 + jnp.exp(z_ref[...])), 0.0)
        db_ref[...] += jnp.sum(dz, axis=0, keepdims=True)
        dz_bf = dz.astype(BF)
        dz_ref[...] = dz_bf
        dn1 = jnp.dot(dz_bf, wf_ref[...], preferred_element_type=F32)
        for sec in range(MAIN_W // GROUP_W):
            sl = slice(sec * GROUP_W, (sec + 1) * GROUP_W)
            dn1 = dn1 + jnp.dot(dproj_ref[:, sl], wm_ref[sl, :], preferred_element_type=F32)
        dx, dg = _rms_bwd(x_ref[...], g_ref[...], dn1)
        dx_ref[...] = dh1_ref[...] + dx
        dg_ref[...] += dg

        @pl.when(i == n_t - 1)
        def _():
            dgq_ref[...] = gq_acc[:, :HEAD_DIM] + gq_acc[:, HEAD_DIM:]
            dgk_ref[...] = gk_acc[:, :HEAD_DIM] + gk_acc[:, HEAD_DIM:]

    row_spec = lambda w, col=0: pl.BlockSpec((tm, w), lambda i: (n_t - 1 - i, col))
    full = lambda a: pl.BlockSpec(a.shape, lambda i: (0,) * a.ndim)
    acc = lambda r, c: pl.BlockSpec((r, c), lambda i: (0, 0))
    return pl.pallas_call(
        body, name="in_proj_bwd", grid=(n_t,),
        out_shape=(jax.ShapeDtypeStruct((t_len, MAIN_W), BF), jax.ShapeDtypeStruct((t_len, LANES), BF),
                   jax.ShapeDtypeStruct((t_len, D_MODEL), F32), jax.ShapeDtypeStruct((1, D_MODEL), F32),
                   jax.ShapeDtypeStruct((1, HEAD_DIM), F32), jax.ShapeDtypeStruct((1, HEAD_DIM), F32),
                   jax.ShapeDtypeStruct((1, LANES), F32)),
        in_specs=[row_spec(D_MODEL), full(g_mix), row_spec(D_MODEL)] + [row_spec(GROUP_W)] * 7
        + [row_spec(LANES), row_spec(GROUP_W, 4), row_spec(GROUP_W, 5), row_spec(LANES), row_spec(LANES), row_spec(LANES),
           full(gq_t), full(gk_t), *_w_in_specs()],
        out_specs=(row_spec(MAIN_W), row_spec(LANES), row_spec(D_MODEL), acc(1, D_MODEL), acc(1, HEAD_DIM), acc(1, HEAD_DIM),
                   acc(1, LANES)),
        scratch_shapes=[pltpu.VMEM((8, LANES), F32), pltpu.VMEM((1, LANES), F32), pltpu.VMEM((1, LANES), F32)],
        compiler_params=_cparams(("arbitrary",)),
    )(x, g_mix, dh1, dq_r, dk_r, dv_r, drg, dq_f, dk_f, dv_f, df_col, proj, proj, z, cos_t, sin_t, gq_t, gk_t, w_in_t, w_in_t)


def _matmul_tn(a, b, name, bk=1024):
    t_len, m = a.shape
    n = b.shape[1]
    bm = m if m <= TN_MAX_ROWS else m // 2
    bk = min(bk, t_len)

    def body(a_ref, b_ref, o_ref):
        @pl.when(pl.program_id(1) == 0)
        def _():
            o_ref[...] = jnp.zeros_like(o_ref)

        o_ref[...] += _dot_tn(a_ref[...], b_ref[...])

    return pl.pallas_call(
        body, name=name, grid=(m // bm, t_len // bk),
        out_shape=jax.ShapeDtypeStruct((m, n), F32),
        in_specs=[pl.BlockSpec((bk, bm), lambda i, k: (k, i)), pl.BlockSpec((bk, n), lambda i, k: (k, 0))],
        out_specs=pl.BlockSpec((bm, n), lambda i, k: (i, 0)),
        compiler_params=_cparams(("arbitrary", "arbitrary")),
    )(a, b)


def _place():
    x, y, c = lax.axis_index("x"), lax.axis_index("y"), lax.axis_index("c")
    chips = [(1 - x, y), (x, 1 - y), (1 - x, 1 - y)]
    return x, y, c, chips


def _row_chunks(rows, limit):
    step = max(d for d in range(16, min(rows, limit) + 1, 16) if rows % d == 0)
    return [slice(i, i + step) for i in range(0, rows, step)]


ICI_CHUNK_ROWS = 256
D2D_CHUNK_ROWS = 256


def _gather_phase(phase, ins, outs, send_sems, recv_sems):
    x, y, c, chips = _place()
    me_chip = 2 * x + y
    sibling = (x, y, 1 - c)

    def copy(w, k, slot, half, to, rows=slice(None), src=None):
        dst = outs[w].at[slot, half, rows]
        return pltpu.make_async_remote_copy(src_ref=dst if src is None else src, dst_ref=dst,
                                            send_sem=send_sems.at[w, k], recv_sem=recv_sems.at[w, k],
                                            device_id=to, device_id_type=MESH)

    for w in range(len(ins)):
        for j, (px, py) in enumerate(chips):
            if phase == 0:
                for rows in _row_chunks(ins[w].shape[1], ICI_CHUNK_ROWS):
                    copy(w, j, me_chip, c, (px, py, c), rows, src=ins[w].at[c, rows]).start()
            elif phase == 1:
                copy(w, j, 2 * px + py, c, (x, y, c)).wait_recv()
                for rows in _row_chunks(ins[w].shape[1], D2D_CHUNK_ROWS):
                    copy(w, 3 + j, 2 * px + py, c, sibling, rows).start()
            else:
                copy(w, 3 + j, 2 * px + py, 1 - c, (x, y, c)).wait_recv()
                copy(w, j, me_chip, c, (px, py, c), src=ins[w].at[c]).wait_send()
                copy(w, 3 + j, 2 * px + py, c, sibling).wait_send()


def _gather_scratch(n_w):
    return [pltpu.SemaphoreType.DMA((n_w, 6)), pltpu.SemaphoreType.DMA((n_w, 6))]


def _all_gather_weights(shards):
    n_w = len(shards)

    def body(*refs):
        for phase in range(3):
            _gather_phase(phase, refs[:n_w], refs[n_w:2 * n_w], *refs[2 * n_w:])

    return pl.pallas_call(
        body, name="all_gather_weights",
        out_shape=tuple(jax.ShapeDtypeStruct((4,) + s.shape, s.dtype) for s in shards),
        in_specs=[ANY] * n_w, out_specs=(ANY,) * n_w, scratch_shapes=_gather_scratch(n_w),
    )(*shards)


def _exchange_phase(phase, ins, theirs, send_sems, recv_sems):
    x, y, c, _ = _place()

    def remote(w, k=slice(None), rows=slice(None)):
        return pltpu.make_async_remote_copy(src_ref=ins[w].at[k, 1 - c, rows], dst_ref=theirs[w].at[k, rows],
                                            send_sem=send_sems.at[w], recv_sem=recv_sems.at[w], device_id=(x, y, 1 - c),
                                            device_id_type=MESH)

    for w in range(len(ins)):
        if phase == 0:
            for k in range(4):
                for rows in _row_chunks(ins[w].shape[2], D2D_CHUNK_ROWS):
                    remote(w, k, rows).start()
        else:
            remote(w).wait()


def _exchange_scratch(n_w):
    return [pltpu.SemaphoreType.DMA((n_w,)), pltpu.SemaphoreType.DMA((n_w,))]


def _exchange_out_shapes(grads):
    return tuple(jax.ShapeDtypeStruct((4,) + g.shape[2:], g.dtype) for g in grads)


def _add_pairs(part, theirs, name, halves):
    _, _, r, c = part.shape
    rb = 32 if r % 32 == 0 else r
    n_w = len(halves)
    n_steps = r // rb

    def body(*refs):
        a_ref, b_ref = refs[:2]
        own_ref, ob_ref = refs[2 + n_w:4 + n_w]
        comm = (refs[2:2 + n_w], refs[4 + n_w:4 + 2 * n_w]) + tuple(refs[4 + 2 * n_w:])
        step = pl.program_id(0)

        @pl.when(step == 0)
        def _():
            _share_phase(0, *comm)

        my_chip = 2 * lax.axis_index("x") + lax.axis_index("y")
        ob_ref[...] = (a_ref[...] + b_ref[...]).astype(BF)
        own_ref[...] = a_ref[my_chip] + b_ref[my_chip]

        @pl.when(step == n_steps - 1)
        def _():
            _share_phase(1, *comm)

    spec = pl.BlockSpec((4, rb, c), lambda i: (0, i, 0))
    flat = pl.pallas_call(
        body, name=name, grid=(n_steps,),
        out_shape=(jax.ShapeDtypeStruct((r, c), F32), jax.ShapeDtypeStruct((4, r, c), BF)) + _share_out_shapes(halves),
        in_specs=[pl.BlockSpec((4, None, rb, c), lambda i: (0, lax.axis_index("c"), i, 0)), spec] + [ANY] * n_w,
        out_specs=(pl.BlockSpec((rb, c), lambda i: (i, 0)), spec) + (ANY,) * n_w,
        scratch_shapes=_share_scratch(n_w), compiler_params=_cparams(("arbitrary",)),
    )(part, theirs, *halves)
    return (flat[0], flat[1]), list(flat[2:])


def _scatter_phase(phase, bfs, got, send_sems, recv_sems):
    x, y, c, chips = _place()

    def remote(w, j, px, py, rows=slice(None)):
        return pltpu.make_async_remote_copy(src_ref=bfs[w].at[2 * px + py, rows], dst_ref=got[w].at[j, rows],
                                            send_sem=send_sems.at[w, j], recv_sem=recv_sems.at[w, j], device_id=(px, py, c),
                                            device_id_type=MESH)

    for w in range(len(bfs)):
        for j, (px, py) in enumerate(chips):
            if phase == 0:
                for rows in _row_chunks(bfs[w].shape[1], ICI_CHUNK_ROWS):
                    remote(w, j, px, py, rows).start()
            else:
                remote(w, j, px, py).wait()


def _scatter_scratch(n_w):
    return [pltpu.SemaphoreType.DMA((n_w, 3)), pltpu.SemaphoreType.DMA((n_w, 3))]


def _scatter_out_shapes(sums_bf16):
    return tuple(jax.ShapeDtypeStruct((3,) + s.shape[1:], BF) for s in sums_bf16)


def _add_received(own, got, name):
    r, c = own.shape
    rb = 32 if r % 32 == 0 else r

    def body(o_ref, g_ref, out_ref):
        out_ref[...] = ((o_ref[...] + g_ref[0].astype(F32)) + g_ref[1].astype(F32)) + g_ref[2].astype(F32)

    return pl.pallas_call(
        body, name=name, grid=(r // rb,), out_shape=jax.ShapeDtypeStruct((r, c), F32),
        in_specs=[pl.BlockSpec((rb, c), lambda i: (i, 0)), pl.BlockSpec((3, rb, c), lambda i: (0, i, 0))],
        out_specs=pl.BlockSpec((rb, c), lambda i: (i, 0)), compiler_params=_cparams(("arbitrary",)),
    )(own, got)


def _share_phase(phase, ins, outs, send_sems, recv_sems):
    x, y, c, _ = _place()

    def remote(w, rows=slice(None)):
        return pltpu.make_async_remote_copy(src_ref=ins[w].at[rows], dst_ref=outs[w].at[c, rows], send_sem=send_sems.at[w],
                                            recv_sem=recv_sems.at[w], device_id=(x, y, 1 - c), device_id_type=MESH)

    for w in range(len(ins)):
        if phase == 0:
            for rows in _row_chunks(ins[w].shape[0], D2D_CHUNK_ROWS):
                remote(w, rows).start()
        else:
            remote(w).wait()


def _share_scratch(n_w):
    return [pltpu.SemaphoreType.DMA((n_w,)), pltpu.SemaphoreType.DMA((n_w,))]


def _share_out_shapes(halves):
    return tuple(jax.ShapeDtypeStruct((2,) + h.shape, h.dtype) for h in halves)


def _share_with_sibling(halves):
    n_w = len(halves)

    def body(*refs):
        for phase in range(2):
            _share_phase(phase, refs[:n_w], refs[n_w:2 * n_w], *refs[2 * n_w:])

    return pl.pallas_call(
        body, name="share_with_sibling", out_shape=_share_out_shapes(halves),
        in_specs=[ANY] * n_w, out_specs=(ANY,) * n_w, scratch_shapes=_share_scratch(n_w),
    )(*halves)


def _small_phase(phase, p_ref, out_ref, slots, send_sems, recv_sems):
    x, y, cc, _ = _place()
    me = 4 * x + 2 * y + cc
    copies = []
    for k in range(1, 8):
        dx, dy, dc = (k >> 2) & 1, (k >> 1) & 1, k & 1
        to = (1 - x if dx else x, 1 - y if dy else y, 1 - cc if dc else cc)
        copies.append(pltpu.make_async_remote_copy(src_ref=p_ref, dst_ref=slots.at[me], send_sem=send_sems.at[k - 1],
                                                   recv_sem=recv_sems.at[k - 1], device_id=to, device_id_type=MESH))
    if phase == 0:
        slots[me] = p_ref[...]
        for cp in copies:
            cp.start()
    else:
        for cp in copies:
            cp.wait()
        total = slots[0]
        for d in range(1, 8):
            total = total + slots[d]
        out_ref[...] = total


def _adamw_update(w_ref, g_ref, m_ref, v_ref, d_ref, nm_ref, nv_ref):
    gv = g_ref[...]
    nm = ADAM_B1 * m_ref[...] + (1.0 - ADAM_B1) * gv
    nv = ADAM_B2 * v_ref[...] + (1.0 - ADAM_B2) * (gv * gv)
    nm_ref[...] = nm
    nv_ref[...] = nv
    m_hat = nm / (1.0 - ADAM_B1 ** ADAM_STEP)
    v_hat = nv / (1.0 - ADAM_B2 ** ADAM_STEP)
    d_ref[...] = -ADAM_LR * (m_hat / (jnp.sqrt(v_hat) + ADAM_EPS) + ADAM_WD * w_ref[...])


def _adamw_many(ws, gs, ms, vs, sums, pack):
    n_a, n_w = len(ws), len(sums)
    n_steps = ADAM_STEPS
    specs = [pl.BlockSpec((w.shape[0] // n_steps, w.shape[1]), lambda i: (i, 0)) for w in ws]
    pack_spec = pl.BlockSpec(pack.shape, lambda i: (0, 0))

    def body(*refs):
        ins = refs[:4 * n_a]
        p_ref = refs[4 * n_a + n_w]
        first_out = 4 * n_a + n_w + 1
        outs = refs[first_out:first_out + 3 * n_a]
        total_ref = refs[first_out + 3 * n_a + n_w]
        scratch = refs[first_out + 3 * n_a + n_w + 1:]
        scatter = (refs[4 * n_a:4 * n_a + n_w], refs[first_out + 3 * n_a:first_out + 3 * n_a + n_w]) + tuple(scratch[:2])
        small = (p_ref, total_ref) + tuple(scratch[2:])
        step = pl.program_id(0)

        @pl.when(step == 0)
        def _():
            _scatter_phase(0, *scatter)
            _small_phase(0, *small)

        for a in range(n_a):
            _adamw_update(*(ins[k * n_a + a] for k in range(4)), *(outs[3 * a + k] for k in range(3)))

        @pl.when(step == n_steps - 1)
        def _():
            _scatter_phase(1, *scatter)
            _small_phase(1, *small)

    flat = pl.pallas_call(
        body, name="adamw_late", grid=(n_steps,),
        out_shape=tuple(jax.ShapeDtypeStruct(w.shape, F32) for w in ws for _ in range(3)) + _scatter_out_shapes(sums)
        + (jax.ShapeDtypeStruct(pack.shape, F32),),
        in_specs=specs * 4 + [ANY] * n_w + [pack_spec],
        out_specs=tuple(s for s in specs for _ in range(3)) + (ANY,) * n_w + (pack_spec,),
        scratch_shapes=_scatter_scratch(n_w) + [pltpu.VMEM((8,) + pack.shape, F32), pltpu.SemaphoreType.DMA((7,)),
                                                pltpu.SemaphoreType.DMA((7,))],
        compiler_params=_cparams(("arbitrary",)),
    )(*ws, *gs, *ms, *vs, *sums, pack)
    return [tuple(flat[3 * a:3 * a + 3]) for a in range(n_a)] + list(flat[3 * n_a:])


def _adamw(w, g, m, v, name):
    r, c = w.shape
    rb, cb = (64, c) if r % 64 == 0 else (r, LANES if (r % 8 and c % LANES == 0) else c)

    def body(*refs):
        _adamw_update(*refs)

    spec = pl.BlockSpec((rb, cb), lambda i, j: (i, j))
    return pl.pallas_call(
        body, name=name, grid=(r // rb, c // cb), out_shape=(jax.ShapeDtypeStruct((r, c), F32),) * 3,
        in_specs=[spec] * 4, out_specs=(spec,) * 3, compiler_params=_cparams(("arbitrary", "arbitrary")),
    )(w, g, m, v)


def _rope_tables(t_len):
    inv_freq = ROPE_BASE ** (-jnp.arange(0, HEAD_DIM, 2, dtype=F32) / HEAD_DIM)
    ang = jnp.arange(t_len, dtype=F32)[:, None] * inv_freq[None, :]
    cos, sin = jnp.cos(ang), jnp.sin(ang)
    cos_t = jnp.concatenate([cos, cos, cos, cos], axis=-1)
    sin_t = jnp.concatenate([-sin, sin, -sin, sin], axis=-1)
    return cos_t, sin_t


def _cols_to_shards(dw):
    r, n = dw.shape
    return jnp.transpose(dw.reshape(2, r // 2, 4, n // 4), (2, 0, 1, 3))


def _rows_to_shards(dw):
    r, n = dw.shape
    rows = r // 4
    if rows % SUBLANES == 0:
        padded = _pad_rows(dw.reshape(4, rows, n))
    else:
        window = rows + SUBLANES - rows % SUBLANES
        padded = _pad_rows(jnp.stack([dw[rows * k // SUBLANES * SUBLANES:][:window] for k in range(4)]))
    return padded.reshape(4, 2, padded.shape[1] // 2, n)


def _shard_row_offset(rows):
    return (rows * (2 * lax.axis_index("x") + lax.axis_index("y"))) % SUBLANES


def _pad_lanes(a):
    extra = -a.shape[-1] % LANES
    return a if extra == 0 else jnp.pad(a, [(0, 0)] * (a.ndim - 1) + [(0, extra)])


def _pad_rows(a):
    rows = a.shape[-2]
    extra = 0 if rows % SHARD_ROW_ALIGN == 0 else -rows % SHARD_ROW_PAD
    return a if extra == 0 else jnp.pad(a, [(0, 0)] * (a.ndim - 2) + [(0, extra), (0, 0)])


def _pad_row(a, width=D_MODEL):
    a = a.reshape(1, -1)
    return jnp.pad(a, ((0, 0), (0, width - a.shape[1])))


def kernel(x, mem, g_mix, w_in, b_forget, g_ret_out, g_fox_q, g_fox_k, w_out, g_xattn, w_xq, w_xkv, g_mem, g_xq, g_xk, w_xo, g_ffn, w_gate, w_up, w_down, loss_target, m_g_mix, m_w_in, m_b_forget, m_g_ret_out, m_g_fox_q, m_g_fox_k, m_w_out, m_g_xattn, m_w_xq, m_w_xkv, m_g_mem, m_g_xq, m_g_xk, m_w_xo, m_g_ffn, m_w_gate, m_w_up, m_w_down, v_g_mix, v_w_in, v_b_forget, v_g_ret_out, v_g_fox_q, v_g_fox_k, v_w_out, v_g_xattn, v_w_xq, v_w_xkv, v_g_mem, v_g_xq, v_g_xk, v_w_xo, v_g_ffn, v_w_gate, v_w_up, v_w_down):
    big = {"w_in": (w_in, m_w_in, v_w_in), "w_out": (w_out, m_w_out, v_w_out), "w_xq": (w_xq, m_w_xq, v_w_xq),
           "w_xkv": (w_xkv, m_w_xkv, v_w_xkv), "w_xo": (w_xo, m_w_xo, v_w_xo), "w_gate": (w_gate, m_w_gate, v_w_gate),
           "w_up": (w_up, m_w_up, v_w_up), "w_down": (w_down, m_w_down, v_w_down)}
    for n in TRANSPOSED:
        big[n] = tuple(jnp.swapaxes(a, 1, 2) for a in big[n])
    shards = {}
    for n in big:
        w = _pad_rows(_pad_lanes(big[n][0][0].astype(BF)))
        shards[n] = w.reshape(2, w.shape[0] // 2, w.shape[1])
    sizes = {n: big[n][0].shape[1:] for n in big}
    w_in_full = _assemble_weight("w_in", _all_gather_weights([shards["w_in"]])[0], shards["w_in"], sizes["w_in"])
    small_w ={"g_mix": g_mix, "b_forget": b_forget, "g_ret_out": g_ret_out, "g_fox_q": g_fox_q, "g_fox_k": g_fox_k,
               "g_xattn": g_xattn, "g_mem": g_mem, "g_xq": g_xq, "g_xk": g_xk, "g_ffn": g_ffn}
    m_small = {"g_mix": m_g_mix, "b_forget": m_b_forget, "g_ret_out": m_g_ret_out, "g_fox_q": m_g_fox_q, "g_fox_k": m_g_fox_k,
               "g_xattn": m_g_xattn, "g_mem": m_g_mem, "g_xq": m_g_xq, "g_xk": m_g_xk, "g_ffn": m_g_ffn}
    v_small = {"g_mix": v_g_mix, "b_forget": v_b_forget, "g_ret_out": v_g_ret_out, "g_fox_q": v_g_fox_q, "g_fox_k": v_g_fox_k,
               "g_xattn": v_g_xattn, "g_mem": v_g_mem, "g_xq": v_g_xq, "g_xk": v_g_xk, "g_ffn": v_g_ffn}
    loss_part, grad_x, sums, got, in_parts, small_g = _local_step(x[0], mem[0], loss_target[0], w_in_full, shards, sizes, small_w)
    return _reduce_and_update(big, sums, got, in_parts, small_w, small_g, loss_part, grad_x, m_small, v_small)


def _assemble_weight(name, gathered, own, size):
    rows, width = size
    my_chip = 2 * lax.axis_index("x") + lax.axis_index("y")
    g = lax.dynamic_update_slice(gathered, own[None], (my_chip, 0, 0, 0))
    g = g.reshape(4, 2 * g.shape[2], g.shape[3])[:, :rows, :width]
    return jnp.transpose(g, (1, 0, 2)).reshape(rows, 4 * width) if name in COL_SHARDED else g.reshape(4 * rows, width)


def _shard_parts(names, dw):
    return [_pad_lanes(_cols_to_shards(dw[n]) if n in COL_SHARDED else _rows_to_shards(dw[n])) for n in names]


def _add_pairs_many(parts, theirs, name):
    n_a = len(parts)
    n_steps = min(p.shape[2] for p in parts) // 32
    rb = [p.shape[2] // n_steps for p in parts]
    part_specs = [pl.BlockSpec((4, None, r, p.shape[3]), lambda i: (0, lax.axis_index("c"), i, 0)) for p, r in zip(parts, rb)]
    quad_specs = [pl.BlockSpec((4, r, p.shape[3]), lambda i: (0, i, 0)) for p, r in zip(parts, rb)]
    own_specs = [pl.BlockSpec((r, p.shape[3]), lambda i: (i, 0)) for p, r in zip(parts, rb)]

    def body(*refs):
        my_chip = 2 * lax.axis_index("x") + lax.axis_index("y")
        for a in range(n_a):
            a_ref, b_ref, own_ref, ob_ref = refs[a], refs[n_a + a], refs[2 * n_a + a], refs[3 * n_a + a]
            ob_ref[...] = (a_ref[...] + b_ref[...]).astype(BF)
            own_ref[...] = a_ref[my_chip] + b_ref[my_chip]

    flat = pl.pallas_call(
        body, name=name, grid=(n_steps,),
        out_shape=tuple(jax.ShapeDtypeStruct(p.shape[2:], F32) for p in parts)
        + tuple(jax.ShapeDtypeStruct((4,) + p.shape[2:], BF) for p in parts),
        in_specs=part_specs + quad_specs, out_specs=tuple(own_specs) + tuple(quad_specs),
        compiler_params=_cparams(("arbitrary",)),
    )(*parts, *theirs)
    return [(flat[a], flat[n_a + a]) for a in range(n_a)]


def _core_sums(parts, theirs):
    out = [None] * len(parts)
    for tag, pick in (("a", lambda p: p.shape[2] % LANES == 0), ("b", lambda p: p.shape[2] % LANES != 0)):
        idx = [i for i, p in enumerate(parts) if pick(p)]
        for i, res in zip(idx, _add_pairs_many([parts[i] for i in idx], [theirs[i] for i in idx], f"core_sum_late_{tag}")):
            out[i] = res
    return out


def _local_step(xs, mems, tgt, w_in_full, shards, sizes, small_w):
    g_mix, b_forget, g_ret_out, g_fox_q, g_fox_k = (small_w[n] for n in ("g_mix", "b_forget", "g_ret_out", "g_fox_q", "g_fox_k"))
    g_xattn, g_mem, g_xq, g_xk, g_ffn = (small_w[n] for n in ("g_xattn", "g_mem", "g_xq", "g_xk", "g_ffn"))
    w_in_t = jnp.pad(w_in_full, ((0, MAIN_W + LANES - IN_W), (0, 0)))
    t_len = xs.shape[0]
    cos_t, sin_t = _rope_tables(t_len)
    tables = _decay_tables(min(RET_BLOCK, t_len))
    gq_t = jnp.concatenate([g_fox_q, g_fox_q], axis=-1)
    gk_t = jnp.concatenate([g_fox_k, g_fox_k], axis=-1)
    b_pad = _pad_row(b_forget, LANES)
    g_ret = g_ret_out.reshape(N_HEADS // 2, 1, LANES)

    n1, proj, rq, rk, q_aug, k_aug, z = _in_proj_fwd(xs, g_mix, w_in_t, b_pad, cos_t, sin_t, gq_t, gk_t)
    raw, mix_r, states = _retention_fwd(rq, rk, proj, g_ret, tables)
    mix_f, o32, lse, *gathered = _fox_fwd(q_aug, k_aug, proj, [shards[n] for n in LATE])
    full = {n: _assemble_weight(n, g, shards[n], sizes[n]) for n, g in zip(LATE, gathered)}
    memn, kraw, kn, vmem = _mem_kv_fwd(mems, g_mem, full["w_xkv"], g_xk)
    h1, hn2, qx, o_x, h2 = _attn_out_xattn_fwd(xs, mix_r, mix_f, full["w_out"], g_xattn, full["w_xq"], g_xq, kn, vmem, full["w_xo"])
    hn3, gate, up, act, dh3, loss_part = _ffn_loss_fwd(h2, g_ffn, full["w_gate"], full["w_up"], full["w_down"], tgt)

    dgate, dup, dh2, dg_ffn = _ffn_bwd(dh3, gate, up, h2, g_ffn, full["w_gate"], full["w_up"], full["w_down"])
    dqx, dh1, dmr, dmf, dkn, dvm, dg_xattn, dg_xq = _attn_out_xattn_bwd(dh2, h1, qx, kn, vmem, full["w_xo"], full["w_xq"],
                                                                      full["w_out"], g_xattn, g_xq)
    dw_xkv, dg_mem, dg_xk = _mem_kv_bwd(dkn, dvm, kraw, mems, memn, g_mem, g_xk, full["w_xkv"])
    dw_gu = _matmul_tn_pair(dgate, dup, hn3, "dw_gate_up")
    dw = {
        "w_out": _matmul_tn_pair(mix_r, mix_f, dh1, "dw_out").reshape(D_MODEL, D_MODEL),
        "w_xq": _matmul_tn(hn2, dqx, "dw_xq"),
        "w_xkv": dw_xkv,
        "w_xo": _matmul_tn(o_x, dh2, "dw_xo"),
        "w_gate": dw_gu[0],
        "w_up": dw_gu[1],
        "w_down": _matmul_tn(act, dh3, "dw_down"),
    }
    late_parts = _shard_parts(LATE, dw)
    dq_r, dk_r, dv_r, drg, dg_ret, *late_theirs = _retention_bwd(dmr, raw, proj, g_ret, rq, rk, states, tables, late_parts)
    late_sums = _core_sums(late_parts, late_theirs)
    dq_f, dk_f, dv_f, df, *late_got = _fox_bwd(q_aug, k_aug, proj, dmf, o32, lse, [s[1] for s in late_sums])
    df_col = jnp.pad(jnp.transpose(df, (1, 0, 2)).reshape(t_len, N_HEADS), ((0, 0), (0, LANES - N_HEADS)))
    dproj, dz, grad_x, dg_mix, dg_fq, dg_fk, db = _in_proj_bwd(xs, g_mix, dh1, dq_r, dk_r, dv_r, drg, dq_f, dk_f, dv_f, df_col,
                                                              proj, z, cos_t, sin_t, gq_t, gk_t, w_in_t)

    dw_in = jnp.concatenate([_matmul_tn(dproj, n1, "dw_in_main"), _matmul_tn(dz, n1, "dw_in_ff")[:IN_W - MAIN_W]], axis=0)
    in_parts = _shard_parts(("w_in",), {"w_in": dw_in})
    sums = {n: s[0] for n, s in zip(LATE, late_sums)}
    got = dict(zip(LATE, late_got))
    small_g = {"g_mix": dg_mix, "b_forget": db[:, :N_HEADS], "g_ret_out": dg_ret, "g_fox_q": dg_fq, "g_fox_k": dg_fk,
               "g_xattn": dg_xattn, "g_mem": dg_mem, "g_xq": dg_xq, "g_xk": dg_xk, "g_ffn": dg_ffn}
    return loss_part, grad_x, sums, got, in_parts, small_g


def _add_received_many(owns, gots, parts):
    n_a, n_w = len(owns), len(parts)
    n_steps = CHIP_SUM_STEPS
    own_specs = [pl.BlockSpec((o.shape[0] // n_steps, o.shape[1]), lambda i: (i, 0)) for o in owns]
    got_specs = [pl.BlockSpec((3, o.shape[0] // n_steps, o.shape[1]), lambda i: (0, i, 0)) for o in owns]

    def body(*refs):
        first_out = 2 * n_a + n_w
        comm = (refs[2 * n_a:first_out], refs[first_out + n_a:first_out + n_a + n_w]) + tuple(refs[first_out + n_a + n_w:])
        step = pl.program_id(0)

        @pl.when(step == 0)
        def _():
            _exchange_phase(0, *comm)

        for a in range(n_a):
            o_ref, g_ref, out_ref = refs[a], refs[n_a + a], refs[first_out + a]
            out_ref[...] = ((o_ref[...] + g_ref[0].astype(F32)) + g_ref[1].astype(F32)) + g_ref[2].astype(F32)

        @pl.when(step == n_steps - 1)
        def _():
            _exchange_phase(1, *comm)

    flat = pl.pallas_call(
        body, name="chip_sum_late", grid=(n_steps,),
        out_shape=tuple(jax.ShapeDtypeStruct(o.shape, F32) for o in owns) + _exchange_out_shapes(parts),
        in_specs=own_specs + got_specs + [ANY] * n_w, out_specs=tuple(own_specs) + (ANY,) * n_w,
        scratch_shapes=_exchange_scratch(n_w), compiler_params=_cparams(("arbitrary",)),
    )(*owns, *gots, *parts)
    return list(flat[:n_a]), list(flat[n_a:])


def _final_grads(names, big, finals, shared):
    my_core = lax.axis_index("c")
    out = {}
    for n, s, fin in zip(names, shared, finals):
        s = lax.dynamic_update_slice(s, fin[None], (my_core, 0, 0))
        s = s.reshape(2 * s.shape[1], s.shape[2])
        rows, width = big[n][0].shape[1:]
        out[n] = s[:rows, :width] if rows % SUBLANES == 0 else lax.dynamic_slice(s, (_shard_row_offset(rows), 0), (rows, width))
    return out


def _reduce_and_update(big, sums, got, in_parts, small_w, small_g, loss_part, grad_x, m_small, v_small):
    small_names = list(small_w)
    pad_rows = SMALL_ROWS - len(small_names) - 1
    stack = lambda d: jnp.concatenate([_pad_row(d[n]) for n in small_names] + [jnp.zeros((pad_rows + 1, D_MODEL), F32)], axis=0)
    g_pack = jnp.concatenate([_pad_row(small_g[n]) for n in small_names] + [_pad_row(loss_part[0:1, 0:1])]
                             + [jnp.zeros((pad_rows, D_MODEL), F32)], axis=0)
    late_finals, in_theirs = _add_received_many([sums[n] for n in LATE], [got[n] for n in LATE], in_parts)
    (in_own, in_bf), late_shared = _add_pairs(in_parts[0], in_theirs[0], "core_sum_w_in", late_finals)
    grads = _final_grads(LATE, big, late_finals, late_shared)
    *late_updates, in_got, g_tot = _adamw_many([big[n][0][0] for n in LATE], [grads[n] for n in LATE], [big[n][1][0] for n in LATE],
                                               [big[n][2][0] for n in LATE], [in_bf], g_pack)
    updates = dict(zip(LATE, late_updates))
    in_final = [_add_received(in_own, in_got, "chip_sum_w_in")]
    grads.update(_final_grads(("w_in",), big, in_final, _share_with_sibling(in_final)))
    updates["w_in"] = _adamw(big["w_in"][0][0], grads["w_in"], big["w_in"][1][0], big["w_in"][2][0], "adamw_w_in")
    deltas, new_m, new_v = {}, {}, {}
    for n in big:
        restore = (lambda a: jnp.swapaxes(a[None], 1, 2)) if n in TRANSPOSED else (lambda a: a[None])
        grads[n] = restore(grads[n])
        deltas[n], new_m[n], new_v[n] = (restore(a) for a in updates[n])

    d_s, m_s, v_s = _adamw(stack(small_w), g_tot, stack(m_small), stack(v_small), "adamw_small")
    for i, n in enumerate(small_names):
        shape = small_w[n].shape
        size = int(np.prod(shape))
        grads[n] = g_tot[i, :size].reshape(shape)
        deltas[n], new_m[n], new_v[n] = d_s[i, :size].reshape(shape), m_s[i, :size].reshape(shape), v_s[i, :size].reshape(shape)
    loss = g_tot[len(small_names), 0]

    order = ["g_mix", "w_in", "b_forget", "g_ret_out", "g_fox_q", "g_fox_k", "w_out", "g_xattn", "w_xq", "w_xkv", "g_mem", "g_xq",
             "g_xk", "w_xo", "g_ffn", "w_gate", "w_up", "w_down"]
    return (loss, grad_x[None], *[grads[n] for n in order], *[deltas[n] for n in order], *[new_m[n] for n in order],
            *[new_v[n] for n in order])
```

```python
import functools

import numpy as np
import jax
import jax.numpy as jnp
from jax import lax
from jax.experimental import pallas as pl
from jax.experimental.pallas import tpu as pltpu

F32 = jnp.float32
BF = jnp.bfloat16

D_MODEL = 1024
HEAD_DIM = 64
N_HEADS = 8
GROUP_W = 512
N_XH = 4
XHD = 256
D_FF = 2816
MAIN_W = 3584
IN_W = 3592
ROPE_BASE = 10000.0
LOG2E = 1.4426950408889634
LN2 = 0.6931471805599453
EPS = 1e-6
NEG = -1e30
LANES = 128
SUBLANES = 8
RET_BLOCK = 256
REF_CHUNK = 64
ROW_TILE = 512
FFN_BWD_TILE = 256
ATT_BLOCK = 256
FWD_GROUP = 4
TN_MAX_ROWS = 1408
SMALL_ROWS = 16
COL_SHARDED = ("w_xkv",)
TRANSPOSED = ("w_in", "w_gate", "w_up")
SHARD_ROW_ALIGN = 32
SHARD_ROW_PAD = 256
LATE = ("w_out", "w_xq", "w_xkv", "w_xo", "w_gate", "w_up", "w_down")
VMEM_LIMIT = 56 * 1024 * 1024

ADAM_LR = 0.001
ADAM_B1 = 0.9
ADAM_B2 = 0.999
ADAM_EPS = 1e-08
ADAM_WD = 0.01
ADAM_STEP = 10
CHIP_SUM_STEPS = 2
ADAM_STEPS = 4

MESH = pl.DeviceIdType.MESH
ANY = pl.BlockSpec(memory_space=pl.ANY)
VMEM_SPEC = pl.BlockSpec(memory_space=pltpu.VMEM)


def _cparams(sem=None, vmem=VMEM_LIMIT):
    return pltpu.CompilerParams(dimension_semantics=sem, vmem_limit_bytes=vmem)


def _dot(a, b):
    return jnp.dot(a.astype(BF), b.astype(BF), preferred_element_type=F32)


def _dot_nt(a, b):
    return lax.dot_general(a.astype(BF), b.astype(BF), (((1,), (1,)), ((), ())), preferred_element_type=F32)


def _dot_tn(a, b):
    return lax.dot_general(a.astype(BF), b.astype(BF), (((0,), (0,)), ((), ())), preferred_element_type=F32)


def _split3(x):
    hi = x.astype(BF)
    r = x - hi.astype(F32)
    mid = r.astype(BF)
    lo = (r - mid.astype(F32)).astype(BF)
    return hi, mid, lo


def _dot_exact(ind, x):
    hi, mid, lo = _split3(x)
    return (jnp.dot(ind, lo, preferred_element_type=F32) + jnp.dot(ind, mid, preferred_element_type=F32)
            + jnp.dot(ind, hi, preferred_element_type=F32))


def _dot_nt_exact(ind, x):
    hi, mid, lo = _split3(x)
    dn = (((1,), (1,)), ((), ()))
    return (lax.dot_general(ind, lo, dn, preferred_element_type=F32) + lax.dot_general(ind, mid, dn, preferred_element_type=F32)
            + lax.dot_general(ind, hi, dn, preferred_element_type=F32))


def _sigmoid(x):
    return 1.0 / (1.0 + jnp.exp(-x))


def _rms_fwd(x, g):
    r = lax.rsqrt(jnp.mean(x * x, axis=-1, keepdims=True) + EPS)
    return x * r * g


def _rms_bwd(x, g, dy):
    r = lax.rsqrt(jnp.mean(x * x, axis=-1, keepdims=True) + EPS)
    xh = x * r
    dg = jnp.sum(dy * xh, axis=0, keepdims=True)
    dxh = dy * g
    dx = r * (dxh - xh * jnp.mean(dxh * xh, axis=-1, keepdims=True))
    return dx, dg


def _group_mean64(x):
    lane = lax.broadcasted_iota(jnp.int32, x.shape, 1)
    lo = lane < HEAD_DIM
    s_lo = jnp.sum(jnp.where(lo, x, 0.0), axis=-1, keepdims=True)
    s_hi = jnp.sum(jnp.where(lo, 0.0, x), axis=-1, keepdims=True)
    return jnp.where(lo, s_lo, s_hi) * (1.0 / HEAD_DIM)


def _swap32(x):
    lane = lax.broadcasted_iota(jnp.int32, x.shape, 1)
    first = (lane % HEAD_DIM) < (HEAD_DIM // 2)
    return jnp.where(first, pltpu.roll(x, LANES - HEAD_DIM // 2, axis=1), pltpu.roll(x, HEAD_DIM // 2, axis=1))


def _chunks(w):
    return [slice(j * LANES, (j + 1) * LANES) for j in range(w // LANES)]


def _aug_pair(qk, f_cols, is_query):
    lane = lax.broadcasted_iota(jnp.int32, qk.shape, 1)
    a = lane - HEAD_DIM
    values = (qk, pltpu.roll(qk, HEAD_DIM, axis=1))
    out = []
    for hh in range(2):
        hi, mid, lo = (p.astype(F32) for p in _split3(f_cols[hh] * LOG2E))
        if is_query:
            aux = jnp.where(a == 0, hi, jnp.where(a == 1, mid, jnp.where(a == 2, lo, jnp.where(a < 6, 1.0, 0.0))))
        else:
            aux = jnp.where(a < 3, 1.0, jnp.where(a == 3, -hi, jnp.where(a == 4, -mid, jnp.where(a == 5, -lo, 0.0))))
        out.append(jnp.where(a < 0, values[hh], aux))
    return jnp.concatenate(out, axis=-1).astype(BF)


def _mem_kv_fwd(mem, g_mem, w_xkv, g_xk):
    m_tok = mem.shape[0]

    def body(mem_ref, gm_ref, w_ref, gk_ref, memn_ref, kraw_ref, kn_ref, v_ref):
        mn = _rms_fwd(mem_ref[...], gm_ref[...]).astype(BF)
        memn_ref[...] = mn
        kv = jnp.dot(mn, w_ref[...], preferred_element_type=F32)
        k = kv[:, :D_MODEL]
        kraw_ref[...] = k
        v_ref[...] = kv[:, D_MODEL:].astype(BF)
        for h in range(N_XH):
            sl = slice(h * XHD, (h + 1) * XHD)
            kn_ref[:, sl] = _rms_fwd(k[:, sl], gk_ref[...]).astype(BF)

    return pl.pallas_call(
        body, name="mem_kv_fwd",
        out_shape=(jax.ShapeDtypeStruct((m_tok, D_MODEL), BF), jax.ShapeDtypeStruct((m_tok, D_MODEL), F32),
                   jax.ShapeDtypeStruct((m_tok, D_MODEL), BF), jax.ShapeDtypeStruct((m_tok, D_MODEL), BF)),
        in_specs=[VMEM_SPEC] * 4, out_specs=(VMEM_SPEC,) * 4, compiler_params=_cparams(),
    )(mem, g_mem, w_xkv, g_xk)


def _in_proj_fwd(x, g_mix, w_in_t, b_pad, cos_t, sin_t, gq_t, gk_t):
    t_len = x.shape[0]
    tm = min(ROW_TILE, t_len)
    n_t = t_len // tm

    def body(x_ref, g_ref, wm_ref, wf_ref, b_ref, cos_ref, sin_ref, gq_ref, gk_ref,
             n1_ref, proj_ref, rq_ref, rk_ref, qa_ref, ka_ref, z_ref, carry):
        i = pl.program_id(0)

        @pl.when(i == 0)
        def _():
            carry[...] = jnp.zeros_like(carry)

        n1 = _rms_fwd(x_ref[...], g_ref[...]).astype(BF)
        n1_ref[...] = n1
        z = _dot_nt(n1, wf_ref[...]) + b_ref[...]
        z_ref[...] = z
        lane = lax.broadcasted_iota(jnp.int32, z.shape, 1)
        lf = jnp.where(lane < N_HEADS, jnp.minimum(z, 0.0) - jnp.log(1.0 + jnp.exp(-jnp.abs(z))), 0.0)
        row = lax.broadcasted_iota(jnp.int32, (tm, tm), 0)
        col = lax.broadcasted_iota(jnp.int32, (tm, tm), 1)
        tri = (row >= col).astype(BF)
        fc = _dot_exact(tri, lf) + carry[0:1, :]
        carry[...] = jnp.broadcast_to(fc[tm - 1:tm, :], carry.shape)
        c, s = cos_ref[...], sin_ref[...]

        def section(n):
            p = _dot_nt(n1, wm_ref[n * GROUP_W:(n + 1) * GROUP_W, :])
            proj_ref[:, n * GROUP_W:(n + 1) * GROUP_W] = p.astype(BF)
            return p

        def rotate(p, out_ref, scale):
            for sl in _chunks(GROUP_W):
                out_ref[:, sl] = ((p[:, sl] * c + _swap32(p[:, sl]) * s) * scale).astype(BF)

        def norm_aug(p, gain, out_ref, scale, is_query):
            for j, sl in enumerate(_chunks(GROUP_W)):
                f = p[:, sl]
                f = f * lax.rsqrt(_group_mean64(f * f) + EPS) * gain * scale
                out_ref[:, 2 * j * LANES:2 * (j + 1) * LANES] = _aug_pair(f, [fc[:, 2 * j:2 * j + 1], fc[:, 2 * j + 1:2 * j + 2]], is_query)

        p_rq, p_rk = section(0), section(1)
        rotate(p_rq, rq_ref, 0.125)
        section(2)
        rotate(p_rk, rk_ref, 1.0)
        section(3)
        p_fq = section(4)
        p_fk = section(5)
        norm_aug(p_fq, gq_ref[...], qa_ref, 0.125 * LOG2E, True)
        section(6)
        norm_aug(p_fk, gk_ref[...], ka_ref, 1.0, False)

    row_spec = lambda w: pl.BlockSpec((tm, w), lambda i: (i, 0))
    full = lambda a: pl.BlockSpec(a.shape, lambda i: (0,) * a.ndim)
    return pl.pallas_call(
        body, name="in_proj_fwd", grid=(n_t,),
        out_shape=(jax.ShapeDtypeStruct((t_len, D_MODEL), BF), jax.ShapeDtypeStruct((t_len, MAIN_W), BF),
                   jax.ShapeDtypeStruct((t_len, GROUP_W), BF), jax.ShapeDtypeStruct((t_len, GROUP_W), BF),
                   jax.ShapeDtypeStruct((t_len, 2 * GROUP_W), BF), jax.ShapeDtypeStruct((t_len, 2 * GROUP_W), BF),
                   jax.ShapeDtypeStruct((t_len, LANES), F32)),
        in_specs=[row_spec(D_MODEL), full(g_mix), *_w_in_specs(), full(b_pad), row_spec(LANES), row_spec(LANES),
                  full(gq_t), full(gk_t)],
        out_specs=(row_spec(D_MODEL), row_spec(MAIN_W), row_spec(GROUP_W), row_spec(GROUP_W), row_spec(2 * GROUP_W),
                   row_spec(2 * GROUP_W), row_spec(LANES)),
        scratch_shapes=[pltpu.VMEM((8, LANES), F32)],
        compiler_params=_cparams(("arbitrary",)),
    )(x, g_mix, w_in_t, w_in_t, b_pad, cos_t, sin_t, gq_t, gk_t)


def _w_in_specs():
    return (pl.BlockSpec((MAIN_W, D_MODEL), lambda i: (0, 0)), pl.BlockSpec((LANES, D_MODEL), lambda i: (MAIN_W // LANES, 0)))


def _matmul_tn_pair(a1, a2, b, name, bk=1024):
    t_len, m = a1.shape
    n = b.shape[1]
    bm = m if m <= TN_MAX_ROWS else m // 2
    bk = min(bk, t_len)

    def body(a1_ref, a2_ref, b_ref, o_ref):
        @pl.when(pl.program_id(1) == 0)
        def _():
            o_ref[...] = jnp.zeros_like(o_ref)

        bv = b_ref[...]
        o_ref[0] += _dot_tn(a1_ref[...], bv)
        o_ref[1] += _dot_tn(a2_ref[...], bv)

    a_spec = pl.BlockSpec((bk, bm), lambda i, k: (k, i))
    return pl.pallas_call(
        body, name=name, grid=(m // bm, t_len // bk),
        out_shape=jax.ShapeDtypeStruct((2, m, n), F32),
        in_specs=[a_spec, a_spec, pl.BlockSpec((bk, n), lambda i, k: (k, 0))],
        out_specs=pl.BlockSpec((2, bm, n), lambda i, k: (0, i, 0)),
        compiler_params=_cparams(("arbitrary", "arbitrary")),
    )(a1, a2, b)


def _decay_tables(c):
    h = np.arange(N_HEADS, dtype=np.float64)
    lg = np.log(1.0 - 2.0 ** (-5.0 - h)).astype(np.float32).astype(np.float64)
    t = np.arange(c)
    same_or_earlier = (t[None, :] // REF_CHUNK) <= (t[:, None] // REF_CHUNK)
    w = np.where(same_or_earlier[None], np.exp(lg[:, None, None] * np.abs(t[:, None] - t[None, :])[None]), 0.0)
    qd = np.exp(lg[:, None] * (t[None, :] + 1.0))
    kd = np.exp(lg[:, None] * (c - 1.0 - t[None, :]))
    cd = np.exp(lg * c)
    ones = np.ones((1, 1, HEAD_DIM))
    return (jnp.asarray(w, F32), jnp.asarray(qd[:, :, None] * ones, F32), jnp.asarray(kd[:, :, None] * ones, F32),
            jnp.asarray(cd[:, None, None] * np.ones((1, HEAD_DIM, HEAD_DIM)), F32))


def _retention_fwd(rq, rk, proj, g_ret, tables):
    t_len = rq.shape[0]
    c = min(RET_BLOCK, t_len)
    n_b = t_len // c
    wdec, qdec, kdec, cdec = tables
    v_col, g_col = 2 * GROUP_W // LANES, 3 * GROUP_W // LANES

    def body(q_ref, k_ref, v_ref, rg_ref, g_ref, w_ref, qd_ref, kd_ref, cd_ref, raw_ref, mix_ref, st_ref, state):
        i = pl.program_id(1)

        @pl.when(i == 0)
        def _():
            state[...] = jnp.zeros_like(state)

        q2, k2, v2 = q_ref[...], k_ref[...], v_ref[...]
        heads = [tuple(t[:, hh * HEAD_DIM:(hh + 1) * HEAD_DIM] for t in (q2, k2, v2)) for hh in range(2)]
        scores = [(_dot_nt(q, k) * w_ref[hh]).astype(BF) for hh, (q, k, _) in enumerate(heads)]
        outs = []
        for hh, (q, k, v) in enumerate(heads):
            sp = state[hh]
            st_ref[0, 0, hh] = sp
            outs.append(jnp.dot(scores[hh], v, preferred_element_type=F32) + _dot(q.astype(F32) * qd_ref[hh], sp))
            state[hh] = sp * cd_ref[hh] + _dot_tn(k.astype(F32) * kd_ref[hh], v)
        o2 = jnp.concatenate(outs, axis=-1)
        raw_ref[...] = o2
        xc = o2 - _group_mean64(o2)
        xh = xc * lax.rsqrt(_group_mean64(xc * xc) + EPS)
        gate = rg_ref[...].astype(F32)
        mix_ref[...] = (gate * _sigmoid(gate) * (xh * g_ref[0])).astype(BF)

    blk = lambda col0: pl.BlockSpec((c, LANES), lambda hp, i: (i, col0 + hp))
    tab = lambda a: pl.BlockSpec((2,) + a.shape[1:], lambda hp, i: (hp, 0, 0))
    return pl.pallas_call(
        body, name="retention_fwd", grid=(N_HEADS // 2, n_b),
        out_shape=(jax.ShapeDtypeStruct((t_len, GROUP_W), F32), jax.ShapeDtypeStruct((t_len, GROUP_W), BF),
                   jax.ShapeDtypeStruct((N_HEADS // 2, n_b, 2, HEAD_DIM, HEAD_DIM), F32)),
        in_specs=[blk(0), blk(0), blk(v_col), blk(g_col), pl.BlockSpec((1, 1, LANES), lambda hp, i: (hp, 0, 0)),
                  tab(wdec), tab(qdec), tab(kdec), tab(cdec)],
        out_specs=(blk(0), blk(0), pl.BlockSpec((1, 1, 2, HEAD_DIM, HEAD_DIM), lambda hp, i: (hp, i, 0, 0, 0))),
        scratch_shapes=[pltpu.VMEM((2, HEAD_DIM, HEAD_DIM), F32)],
        compiler_params=_cparams(("arbitrary", "arbitrary")),
    )(rq, rk, proj, proj, g_ret, wdec, qdec, kdec, cdec)


def _fox_fwd(q_aug, k_aug, proj, shards):
    t_len = q_aug.shape[0]
    tq = min(ATT_BLOCK, t_len)
    nsub = min(FWD_GROUP, t_len // tq)
    tg = nsub * tq
    n_q = t_len // tg
    v_col = 6 * GROUP_W // LANES
    tc = min(512, t_len)
    n_w = len(shards)
    n_steps = (N_HEADS // 2) * n_q

    def body(*refs):
        q_ref, k_ref, v_ref = refs[:3]
        o_ref, o32_ref, lse_ref = refs[3 + n_w:6 + n_w]
        vt = refs[6 + 2 * n_w]
        comm = (refs[3:3 + n_w], refs[6 + n_w:6 + 2 * n_w]) + tuple(refs[7 + 2 * n_w:])
        i = pl.program_id(1)
        step = pl.program_id(0) * n_q + i

        @pl.when(step == 0)
        def _():
            _gather_phase(0, *comm)

        @pl.when(step == (3 * n_steps) // 4)
        def _():
            _gather_phase(1, *comm)

        @pl.when(i == 0)
        def _():
            for c0 in range(0, t_len, tc):
                vt[:, c0:c0 + tc] = v_ref[c0:c0 + tc, :].T

        chains = [(u, hh) for u in range(nsub) for hh in range(2)]
        qs = {(u, hh): q_ref[u * tq:(u + 1) * tq, hh * LANES:(hh + 1) * LANES] for u, hh in chains}
        ones = jnp.ones((HEAD_DIM, tq), BF)

        def scores(j, which):
            k2 = k_ref[pl.ds(pl.multiple_of(j * tq, tq), tq), :]
            return {ch: _dot_nt(k2[:, ch[1] * LANES:(ch[1] + 1) * LANES], qs[ch]) for ch in which}

        def update(j, ss, carry, masked):
            v2 = vt[:, pl.ds(pl.multiple_of(j * tq, tq), tq)]
            ps, stats = {}, {}
            for ch in ss:
                m = carry[ch][0]
                s_t = ss[ch]
                if ch in masked:
                    krow = lax.broadcasted_iota(jnp.int32, (tq, tq), 0)
                    qcol = lax.broadcasted_iota(jnp.int32, (tq, tq), 1)
                    s_t = jnp.where(qcol >= krow, s_t, NEG)
                m_new = jnp.maximum(m, jnp.max(s_t, axis=0, keepdims=True))
                ps[ch] = jnp.exp2(s_t - m_new).astype(BF)
                stats[ch] = (m_new, jnp.exp2(m - m_new))
            out = dict(carry)
            for ch in ss:
                m_new, alpha = stats[ch]
                v_aug = jnp.concatenate([v2[ch[1] * HEAD_DIM:(ch[1] + 1) * HEAD_DIM, :], ones], axis=0)
                out[ch] = (m_new, carry[ch][1] * alpha + jnp.dot(v_aug, ps[ch], preferred_element_type=F32))
            return out

        def advance(j, state):
            ss, carry = state
            return scores(j + 1, chains), update(j, ss, carry, ())

        init = {ch: (jnp.full((1, tq), NEG, F32), jnp.zeros((LANES, tq), F32)) for ch in chains}
        first = nsub * i
        ss, carry = lax.fori_loop(0, first, advance, (scores(0, chains), init))
        carry = update(first, ss, carry, [(0, 0), (0, 1)])
        for u in range(1, nsub):
            rest = [(uu, hh) for uu in range(u, nsub) for hh in range(2)]
            carry = update(first + u, scores(first + u, rest), carry, [(u, 0), (u, 1)])
        for u in range(nsub):
            outs, lses = [], []
            for hh in range(2):
                m, acc = carry[u, hh]
                l = acc[HEAD_DIM:HEAD_DIM + 1, :]
                outs.append(acc[:HEAD_DIM, :] / l)
                lses.append(m + jnp.log2(l))
            o2 = jnp.concatenate(outs, axis=0).T
            o32_ref[u * tq:(u + 1) * tq, :] = o2
            o_ref[u * tq:(u + 1) * tq, :] = o2.astype(BF)
            lse_ref[0, :, u * tq:(u + 1) * tq] = jnp.concatenate(lses, axis=0)

        @pl.when(step == n_steps - 1)
        def _():
            _gather_phase(2, *comm)

    return pl.pallas_call(
        body, name="fox_fwd", grid=(N_HEADS // 2, n_q),
        out_shape=(jax.ShapeDtypeStruct((t_len, GROUP_W), BF), jax.ShapeDtypeStruct((t_len, GROUP_W), F32),
                   jax.ShapeDtypeStruct((N_HEADS // 2, 2, t_len), F32))
        + tuple(jax.ShapeDtypeStruct((4,) + s.shape, s.dtype) for s in shards),
        in_specs=[pl.BlockSpec((tg, 2 * LANES), lambda hp, i: (i, hp)),
                  pl.BlockSpec((t_len, 2 * LANES), lambda hp, i: (0, hp)),
                  pl.BlockSpec((t_len, LANES), lambda hp, i: (0, v_col + hp))] + [ANY] * n_w,
        out_specs=(pl.BlockSpec((tg, LANES), lambda hp, i: (i, hp)), pl.BlockSpec((tg, LANES), lambda hp, i: (i, hp)),
                   pl.BlockSpec((1, 2, tg), lambda hp, i: (hp, 0, i))) + (ANY,) * n_w,
        scratch_shapes=[pltpu.VMEM((LANES, t_len), BF)] + _gather_scratch(n_w),
        compiler_params=_cparams(("arbitrary", "arbitrary")),
    )(q_aug, k_aug, proj, *shards)


def _softmax_rows(s):
    p = jnp.exp(s - jnp.max(s, axis=-1, keepdims=True))
    return p / jnp.sum(p, axis=-1, keepdims=True)


def _attn_out_xattn_fwd(x, mix_r, mix_f, w_out, g_xattn, w_xq, g_xq, kn, v, w_xo):
    t_len = x.shape[0]
    tm = min(ROW_TILE, t_len)

    def body(x_ref, mr_ref, mf_ref, wo_ref, g_ref, wq_ref, gq_ref, kn_ref, v_ref, wxo_ref,
             h1_ref, hn_ref, qx_ref, o_ref, h2_ref):
        h1 = x_ref[...] + jnp.dot(mr_ref[...], wo_ref[:GROUP_W, :], preferred_element_type=F32) \
            + jnp.dot(mf_ref[...], wo_ref[GROUP_W:, :], preferred_element_type=F32)
        h1_ref[...] = h1
        hn = _rms_fwd(h1, g_ref[...]).astype(BF)
        hn_ref[...] = hn
        qx = jnp.dot(hn, wq_ref[...], preferred_element_type=F32).astype(BF)
        qx_ref[...] = qx
        sls = [slice(h * XHD, (h + 1) * XHD) for h in range(N_XH)]
        qns = [_rms_fwd(qx[:, sl].astype(F32), gq_ref[...]).astype(BF) for sl in sls]
        logits = [_dot_nt(qn, kn_ref[:, sl]) * (XHD ** -0.5) for qn, sl in zip(qns, sls)]
        ps = [_softmax_rows(s).astype(BF) for s in logits]
        for p, sl in zip(ps, sls):
            o_ref[:, sl] = jnp.dot(p, v_ref[:, sl], preferred_element_type=F32).astype(BF)
        h2_ref[...] = h1 + jnp.dot(o_ref[...], wxo_ref[...], preferred_element_type=F32)

    row_spec = lambda w: pl.BlockSpec((tm, w), lambda i: (i, 0))
    full = lambda a: pl.BlockSpec(a.shape, lambda i: (0,) * a.ndim)
    return pl.pallas_call(
        body, name="attn_out_xattn_fwd", grid=(t_len // tm,),
        out_shape=(jax.ShapeDtypeStruct((t_len, D_MODEL), F32), jax.ShapeDtypeStruct((t_len, D_MODEL), BF),
                   jax.ShapeDtypeStruct((t_len, D_MODEL), BF), jax.ShapeDtypeStruct((t_len, D_MODEL), BF),
                   jax.ShapeDtypeStruct((t_len, D_MODEL), F32)),
        in_specs=[row_spec(D_MODEL), row_spec(GROUP_W), row_spec(GROUP_W), full(w_out), full(g_xattn), full(w_xq), full(g_xq),
                  full(kn), full(v), full(w_xo)],
        out_specs=(row_spec(D_MODEL),) * 5,
        compiler_params=_cparams(("arbitrary",)),
    )(x, mix_r, mix_f, w_out, g_xattn, w_xq, g_xq, kn, v, w_xo)


def _ffn_loss_fwd(h2, g_ffn, w_gate, w_up, w_down, target):
    t_len = h2.shape[0]
    tm = min(ROW_TILE, t_len)

    def body(h2_ref, g_ref, wg_ref, wu_ref, wd_ref, tgt_ref, hn_ref, gate_ref, up_ref, act_ref, dh3_ref, loss_ref):
        @pl.when(pl.program_id(0) == 0)
        def _():
            loss_ref[...] = jnp.zeros_like(loss_ref)

        h2v = h2_ref[...]
        hn = _rms_fwd(h2v, g_ref[...]).astype(BF)
        hn_ref[...] = hn
        gate = _dot_nt(hn, wg_ref[...])
        up = _dot_nt(hn, wu_ref[...])
        gate_ref[...] = gate.astype(BF)
        up_ref[...] = up.astype(BF)
        act = (gate * _sigmoid(gate) * up).astype(BF)
        act_ref[...] = act
        diff = h2v + jnp.dot(act, wd_ref[...], preferred_element_type=F32) - tgt_ref[...]
        dh3_ref[...] = diff * (1.0 / D_MODEL)
        per_row = jnp.sum(diff * diff, axis=-1, keepdims=True) * (1.0 / D_MODEL)
        loss_ref[...] += 0.5 * jnp.sum(per_row, axis=0, keepdims=True)

    row_spec = lambda w: pl.BlockSpec((tm, w), lambda i: (i, 0))
    full = lambda a: pl.BlockSpec(a.shape, lambda i: (0,) * a.ndim, pipeline_mode=pl.Buffered(1))
    return pl.pallas_call(
        body, name="ffn_loss_fwd", grid=(t_len // tm,),
        out_shape=(jax.ShapeDtypeStruct((t_len, D_MODEL), BF), jax.ShapeDtypeStruct((t_len, D_FF), BF),
                   jax.ShapeDtypeStruct((t_len, D_FF), BF), jax.ShapeDtypeStruct((t_len, D_FF), BF),
                   jax.ShapeDtypeStruct((t_len, D_MODEL), F32), jax.ShapeDtypeStruct((8, LANES), F32)),
        in_specs=[row_spec(D_MODEL), full(g_ffn), full(w_gate), full(w_up), full(w_down), row_spec(D_MODEL)],
        out_specs=(row_spec(D_MODEL), row_spec(D_FF), row_spec(D_FF), row_spec(D_FF), row_spec(D_MODEL),
                   pl.BlockSpec((8, LANES), lambda i: (0, 0))),
        compiler_params=_cparams(("arbitrary",)),
    )(h2, g_ffn, w_gate, w_up, w_down, target)


def _ffn_bwd(dh3, gate, up, h2, g_ffn, w_gate, w_up, w_down):
    t_len = h2.shape[0]
    tm = min(FFN_BWD_TILE, t_len)

    def body(dh3_ref, gate_ref, up_ref, h2_ref, g_ref, wg_ref, wu_ref, wd_ref, dgate_ref, dup_ref, dh2_ref, dg_ref):
        @pl.when(pl.program_id(0) == 0)
        def _():
            dg_ref[...] = jnp.zeros_like(dg_ref)

        dh3v = dh3_ref[...]
        dact = _dot_nt(dh3v, wd_ref[...])
        g = gate_ref[...].astype(F32)
        sg = _sigmoid(g)
        dup = (dact * (g * sg)).astype(BF)
        dgate = (dact * up_ref[...].astype(F32) * (sg * (1.0 + g * (1.0 - sg)))).astype(BF)
        dup_ref[...] = dup
        dgate_ref[...] = dgate
        dhn = jnp.dot(dgate, wg_ref[...], preferred_element_type=F32) + jnp.dot(dup, wu_ref[...], preferred_element_type=F32)
        dx, dg = _rms_bwd(h2_ref[...], g_ref[...], dhn)
        dh2_ref[...] = dh3v + dx
        dg_ref[...] += dg

    row_spec = lambda w: pl.BlockSpec((tm, w), lambda i: (i, 0))
    full = lambda a: pl.BlockSpec(a.shape, lambda i: (0,) * a.ndim, pipeline_mode=pl.Buffered(1))
    return pl.pallas_call(
        body, name="ffn_bwd", grid=(t_len // tm,),
        out_shape=(jax.ShapeDtypeStruct((t_len, D_FF), BF), jax.ShapeDtypeStruct((t_len, D_FF), BF),
                   jax.ShapeDtypeStruct((t_len, D_MODEL), F32), jax.ShapeDtypeStruct((1, D_MODEL), F32)),
        in_specs=[row_spec(D_MODEL), row_spec(D_FF), row_spec(D_FF), row_spec(D_MODEL), full(g_ffn), full(w_gate), full(w_up),
                  full(w_down)],
        out_specs=(row_spec(D_FF), row_spec(D_FF), row_spec(D_MODEL), pl.BlockSpec((1, D_MODEL), lambda i: (0, 0))),
        compiler_params=_cparams(("arbitrary",)),
    )(dh3, gate, up, h2, g_ffn, w_gate, w_up, w_down)


def _attn_out_xattn_bwd(dh2, h1, qx, kn, v, w_xo, w_xq, w_out, g_xattn, g_xq):
    t_len = h1.shape[0]
    tm = min(ROW_TILE, t_len)
    m_tok = kn.shape[0]

    def body(dh2_ref, h1_ref, qx_ref, kn_ref, v_ref, wxo_ref, wq_ref, wo_ref, g_ref, gq_ref,
             dqx_ref, dh1_ref, dmr_ref, dmf_ref, dkn_ref, dv_ref, dg_ref, dgq_ref, dqx_scr):
        @pl.when(pl.program_id(0) == 0)
        def _():
            dkn_ref[...] = jnp.zeros_like(dkn_ref)
            dv_ref[...] = jnp.zeros_like(dv_ref)
            dg_ref[...] = jnp.zeros_like(dg_ref)
            dgq_ref[...] = jnp.zeros_like(dgq_ref)

        dh2v = dh2_ref[...]
        do = _dot_nt(dh2v, wxo_ref[...])
        gq = gq_ref[...]
        sls = [slice(h * XHD, (h + 1) * XHD) for h in range(N_XH)]
        qraws = [qx_ref[:, sl].astype(F32) for sl in sls]
        qns = [_rms_fwd(qraw, gq).astype(BF) for qraw in qraws]
        dohs = [do[:, sl].astype(BF) for sl in sls]
        logits = [_dot_nt(qn, kn_ref[:, sl]) * (XHD ** -0.5) for qn, sl in zip(qns, sls)]
        dps = [_dot_nt(doh, v_ref[:, sl]) for doh, sl in zip(dohs, sls)]
        ps = [_softmax_rows(s) for s in logits]
        dss = [(p * (dp - jnp.sum(dp * p, axis=-1, keepdims=True)) * (XHD ** -0.5)).astype(BF) for p, dp in zip(ps, dps)]
        dqns = []
        for h, sl in enumerate(sls):
            dv_ref[:, sl] += _dot_tn(ps[h], dohs[h])
            dqns.append(jnp.dot(dss[h], kn_ref[:, sl], preferred_element_type=F32))
            dkn_ref[:, sl] += _dot_tn(dss[h], qns[h])
        dgq = jnp.zeros((1, XHD), F32)
        for h, sl in enumerate(sls):
            dx, dg_h = _rms_bwd(qraws[h], gq, dqns[h])
            dgq = dgq + dg_h
            dqx_scr[:, sl] = dx.astype(BF)
        dgq_ref[...] += dgq
        dqx = dqx_scr[...]
        dqx_ref[...] = dqx
        dhn = _dot_nt(dqx, wq_ref[...])
        dx, dg = _rms_bwd(h1_ref[...], g_ref[...], dhn)
        dg_ref[...] += dg
        dh1 = dh2v + dx
        dh1_ref[...] = dh1
        dmix = _dot_nt(dh1, wo_ref[...])
        dmr_ref[...] = dmix[:, :GROUP_W]
        dmf_ref[...] = dmix[:, GROUP_W:].astype(BF)

    row_spec = lambda w: pl.BlockSpec((tm, w), lambda i: (i, 0))
    full = lambda a: pl.BlockSpec(a.shape, lambda i: (0,) * a.ndim)
    acc = lambda r, c: pl.BlockSpec((r, c), lambda i: (0, 0))
    return pl.pallas_call(
        body, name="attn_out_xattn_bwd", grid=(t_len // tm,),
        out_shape=(jax.ShapeDtypeStruct((t_len, D_MODEL), BF), jax.ShapeDtypeStruct((t_len, D_MODEL), F32),
                   jax.ShapeDtypeStruct((t_len, GROUP_W), F32), jax.ShapeDtypeStruct((t_len, GROUP_W), BF),
                   jax.ShapeDtypeStruct((m_tok, D_MODEL), F32), jax.ShapeDtypeStruct((m_tok, D_MODEL), F32),
                   jax.ShapeDtypeStruct((1, D_MODEL), F32), jax.ShapeDtypeStruct((1, XHD), F32)),
        in_specs=[row_spec(D_MODEL), row_spec(D_MODEL), row_spec(D_MODEL), full(kn), full(v), full(w_xo), full(w_xq), full(w_out),
                  full(g_xattn), full(g_xq)],
        out_specs=(row_spec(D_MODEL), row_spec(D_MODEL), row_spec(GROUP_W), row_spec(GROUP_W), acc(m_tok, D_MODEL),
                   acc(m_tok, D_MODEL), acc(1, D_MODEL), acc(1, XHD)),
        scratch_shapes=[pltpu.VMEM((tm, D_MODEL), BF)],
        compiler_params=_cparams(("arbitrary",)),
    )(dh2, h1, qx, kn, v, w_xo, w_xq, w_out, g_xattn, g_xq)


def _mem_kv_bwd(dkn, dv, kraw, mem, memn, g_mem, g_xk, w_xkv):
    m_tok = mem.shape[0]

    def body(dkn_ref, dv_ref, kraw_ref, mem_ref, memn_ref, gm_ref, gk_ref, w_ref, dw_ref, dgm_ref, dgk_ref, dkv_scr):
        gk = gk_ref[...]
        dgk = jnp.zeros((1, XHD), F32)
        for h in range(N_XH):
            sl = slice(h * XHD, (h + 1) * XHD)
            dx, dg_h = _rms_bwd(kraw_ref[:, sl], gk, dkn_ref[:, sl])
            dgk = dgk + dg_h
            dkv_scr[:, sl] = dx.astype(BF)
        dgk_ref[...] = dgk
        dkv_scr[:, D_MODEL:] = dv_ref[...].astype(BF)
        dkv = dkv_scr[...]
        dw_ref[...] = _dot_tn(memn_ref[...], dkv)
        dmemn = _dot_nt(dkv, w_ref[...])
        mem_v = mem_ref[...]
        r = lax.rsqrt(jnp.mean(mem_v * mem_v, axis=-1, keepdims=True) + EPS)
        dgm_ref[...] = jnp.sum(dmemn * mem_v * r, axis=0, keepdims=True)

    return pl.pallas_call(
        body, name="mem_kv_bwd",
        out_shape=(jax.ShapeDtypeStruct((D_MODEL, 2 * D_MODEL), F32), jax.ShapeDtypeStruct((1, D_MODEL), F32),
                   jax.ShapeDtypeStruct((1, XHD), F32)),
        in_specs=[VMEM_SPEC] * 8, out_specs=(VMEM_SPEC,) * 3,
        scratch_shapes=[pltpu.VMEM((m_tok, 2 * D_MODEL), BF)],
        compiler_params=_cparams(),
    )(dkn, dv, kraw, mem, memn, g_mem, g_xk, w_xkv)


def _fox_bwd(q_aug, k_aug, proj, dmf, o32, lse, sums):
    t_len = q_aug.shape[0]
    tb = min(ATT_BLOCK, t_len)
    n_b = t_len // tb
    nsub = 2 if n_b >= 2 else 1
    tg = nsub * tb
    n_g = t_len // tg
    v_col = 6 * GROUP_W // LANES
    n_w = len(sums)
    n_steps = (N_HEADS // 2) * n_g

    def body(*refs):
        k_ref, v_ref, q_ref, do_ref, o_ref, lse_ref = refs[:6]
        dq_ref, dk_ref, dv_ref, df_ref = refs[6 + n_w:10 + n_w]
        delta = refs[10 + 2 * n_w]
        comm = (refs[6:6 + n_w], refs[10 + n_w:10 + 2 * n_w]) + tuple(refs[11 + 2 * n_w:])
        j = pl.program_id(1)
        step = pl.program_id(0) * n_g + j

        @pl.when(step == 0)
        def _():
            _scatter_phase(0, *comm)

        @pl.when(j == 0)
        def _():
            dq_ref[...] = jnp.zeros_like(dq_ref)
            dd = do_ref[...].astype(F32) * o_ref[...]
            hrow = lax.broadcasted_iota(jnp.int32, (8, LANES), 0)
            lane = lax.broadcasted_iota(jnp.int32, (8, LANES), 1)
            ind = ((lane // HEAD_DIM) == hrow).astype(BF)
            delta[...] = _dot_nt_exact(ind, dd)

        k2, v2 = k_ref[...], v_ref[...]
        chains = [(u, hh) for u in range(nsub) for hh in range(2)]
        ks = {(u, hh): k2[u * tb:(u + 1) * tb, hh * LANES:(hh + 1) * LANES] for u, hh in chains}
        vs = {(u, hh): v2[u * tb:(u + 1) * tb, hh * HEAD_DIM:(hh + 1) * HEAD_DIM] for u, hh in chains}

        def block(i, carry, which, masked):
            rows = pl.ds(pl.multiple_of(i * tb, tb), tb)
            q2 = q_ref[rows, :]
            do2 = do_ref[rows, :]
            qs = [q2[:, hh * LANES:(hh + 1) * LANES] for hh in range(2)]
            dos = [do2[:, hh * HEAD_DIM:(hh + 1) * HEAD_DIM] for hh in range(2)]
            ss = {ch: _dot_nt(ks[ch], qs[ch[1]]) for ch in which}
            dps = {ch: _dot_nt(vs[ch], dos[ch[1]]) for ch in which}
            pts, dsts, dfs = {}, {}, {}
            for ch in which:
                hh = ch[1]
                s_t = ss[ch]
                if ch in masked:
                    krow = lax.broadcasted_iota(jnp.int32, (tb, tb), 0)
                    qcol = lax.broadcasted_iota(jnp.int32, (tb, tb), 1)
                    s_t = jnp.where(qcol >= krow, s_t, NEG)
                p_t = jnp.exp2(s_t - lse_ref[0, hh:hh + 1, rows])
                pts[ch] = p_t.astype(BF)
                ds_t = p_t * (dps[ch] - delta[hh:hh + 1, rows])
                dsts[ch] = ds_t.astype(BF)
                dfs[ch] = jnp.sum(ds_t, axis=-1, keepdims=True)
            out = dict(carry)
            for ch in which:
                dk, dv, df = carry[ch]
                dv = dv + jnp.dot(pts[ch], dos[ch[1]], preferred_element_type=F32)
                dk = dk + jnp.dot(dsts[ch], qs[ch[1]], preferred_element_type=F32)
                out[ch] = (dk, dv, df - dfs[ch])
            for hh in range(2):
                parts_dq = [_dot_tn(dsts[ch], ks[ch])[:, :HEAD_DIM] for ch in which if ch[1] == hh]
                dq_ref[rows, hh * HEAD_DIM:(hh + 1) * HEAD_DIM] += sum(parts_dq[1:], parts_dq[0])
            return out

        init = {ch: (jnp.zeros((tb, LANES), F32), jnp.zeros((tb, HEAD_DIM), F32), jnp.zeros((tb, 1), F32)) for ch in chains}
        first = nsub * j
        carry = block(first, init, [(0, 0), (0, 1)], [(0, 0), (0, 1)])
        if nsub == 2:
            carry = block(first + 1, carry, chains, [(1, 0), (1, 1)])
        carry = lax.fori_loop(first + nsub, n_b, lambda i, c: block(i, c, chains, ()), carry)
        for u in range(nsub):
            rs = slice(u * tb, (u + 1) * tb)
            dk_ref[rs, :] = jnp.concatenate([carry[u, hh][0][:, :HEAD_DIM] for hh in range(2)], axis=-1) * LN2
            dv_ref[rs, :] = jnp.concatenate([carry[u, hh][1] for hh in range(2)], axis=-1)
            df_ref[0, rs, :] = jnp.concatenate([carry[u, hh][2] for hh in range(2)], axis=-1)

        @pl.when(step == n_steps - 1)
        def _():
            _scatter_phase(1, *comm)

    blk = lambda w, col0: pl.BlockSpec((tg, w), lambda hp, j: (j, col0 + hp))
    whole = lambda w: pl.BlockSpec((t_len, w), lambda hp, j: (0, hp))
    rows2 = pl.BlockSpec((1, 2, t_len), lambda hp, j: (hp, 0, 0))
    cols2 = pl.BlockSpec((1, tg, 2), lambda hp, j: (hp, j, 0))
    return pl.pallas_call(
        body, name="fox_bwd", grid=(N_HEADS // 2, n_g),
        out_shape=(jax.ShapeDtypeStruct((t_len, GROUP_W), F32), jax.ShapeDtypeStruct((t_len, GROUP_W), F32),
                   jax.ShapeDtypeStruct((t_len, GROUP_W), F32), jax.ShapeDtypeStruct((N_HEADS // 2, t_len, 2), F32))
        + _scatter_out_shapes(sums),
        in_specs=[blk(2 * LANES, 0), blk(LANES, v_col), whole(2 * LANES), whole(LANES), whole(LANES), rows2] + [ANY] * n_w,
        out_specs=(whole(LANES), blk(LANES, 0), blk(LANES, 0), cols2) + (ANY,) * n_w,
        scratch_shapes=[pltpu.VMEM((8, t_len), F32)] + _scatter_scratch(n_w),
        compiler_params=_cparams(("arbitrary", "arbitrary")),
    )(k_aug, proj, q_aug, dmf, o32, lse, *sums)


def _retention_bwd(dmr, raw, proj, g_ret, rq, rk, states, tables, parts):
    t_len = rq.shape[0]
    c = min(RET_BLOCK, t_len)
    n_b = t_len // c
    wdec, qdec, kdec, cdec = tables
    v_col, g_col = 2 * GROUP_W // LANES, 3 * GROUP_W // LANES
    n_w = len(parts)
    n_steps = (N_HEADS // 2) * n_b

    def body(*refs):
        d_ref, raw_ref, rg_ref, g_ref, q_ref, k_ref, v_ref, st_ref, w_ref, wt_ref, qd_ref, kd_ref, cd_ref = refs[:13]
        dq_ref, dk_ref, dv_ref, drg_ref, dg_ref = refs[13 + n_w:18 + n_w]
        gstate = refs[18 + 2 * n_w]
        comm = (refs[13:13 + n_w], refs[18 + n_w:18 + 2 * n_w]) + tuple(refs[19 + 2 * n_w:])
        step = pl.program_id(0) * n_b + pl.program_id(1)

        @pl.when(step == 0)
        def _():
            _exchange_phase(0, *comm)

        @pl.when(pl.program_id(1) == 0)
        def _():
            gstate[...] = jnp.zeros_like(gstate)
            dg_ref[...] = jnp.zeros_like(dg_ref)

        d, raw_v, g = d_ref[...], raw_ref[...], g_ref[0]
        gate = rg_ref[...].astype(F32)
        xc = raw_v - _group_mean64(raw_v)
        r = lax.rsqrt(_group_mean64(xc * xc) + EPS)
        xh = xc * r
        sg = _sigmoid(gate)
        drg_ref[...] = d * (xh * g) * (sg * (1.0 + gate * (1.0 - sg)))
        dy = d * (gate * sg)
        dg_ref[0] += jnp.sum(dy * xh, axis=0, keepdims=True)
        dxh = dy * g
        do2 = r * (dxh - _group_mean64(dxh) - xh * _group_mean64(dxh * xh))
        q2, k2, v2 = q_ref[...], k_ref[...], v_ref[...]
        dqs, dks, dvs = [], [], []
        heads = [tuple(t[:, hh * HEAD_DIM:(hh + 1) * HEAD_DIM] for t in (q2, k2, v2, do2.astype(BF))) for hh in range(2)]
        firsts = [(_dot_nt(k, q) * wt_ref[hh], _dot_nt(do, v) * w_ref[hh], _dot_nt(v, do) * wt_ref[hh])
                  for hh, (q, k, v, do) in enumerate(heads)]
        for hh, (q, k, v, do) in enumerate(heads):
            a_t, dm, dm_t = firsts[hh]
            sp, gs = st_ref[0, 0, hh], gstate[hh]
            qd = q.astype(F32) * qd_ref[hh]
            kd = k.astype(F32) * kd_ref[hh]
            dqs.append(_dot(dm, k) + _dot_nt(do, sp) * qd_ref[hh])
            dks.append(_dot(dm_t, q) + _dot_nt(v, gs) * kd_ref[hh])
            dvs.append(_dot(a_t, do) + _dot(kd, gs))
            gstate[hh] = gs * cd_ref[hh] + _dot_tn(qd, do)
        dq_ref[...] = jnp.concatenate(dqs, axis=-1)
        dk_ref[...] = jnp.concatenate(dks, axis=-1)
        dv_ref[...] = jnp.concatenate(dvs, axis=-1)

        @pl.when(step == n_steps - 1)
        def _():
            _exchange_phase(1, *comm)

    blk = lambda col0: pl.BlockSpec((c, LANES), lambda hp, i: (n_b - 1 - i, col0 + hp))
    tab = lambda a: pl.BlockSpec((2,) + a.shape[1:], lambda hp, i: (hp, 0, 0))
    gspec = pl.BlockSpec((1, 1, LANES), lambda hp, i: (hp, 0, 0))
    return pl.pallas_call(
        body, name="retention_bwd", grid=(N_HEADS // 2, n_b),
        out_shape=(jax.ShapeDtypeStruct((t_len, GROUP_W), F32),) * 4 + (jax.ShapeDtypeStruct((N_HEADS // 2, 1, LANES), F32),)
        + _exchange_out_shapes(parts),
        in_specs=[blk(0), blk(0), blk(g_col), gspec, blk(0), blk(0), blk(v_col),
                  pl.BlockSpec((1, 1, 2, HEAD_DIM, HEAD_DIM), lambda hp, i: (hp, n_b - 1 - i, 0, 0, 0)),
                  tab(wdec), tab(wdec), tab(qdec), tab(kdec), tab(cdec)] + [ANY] * n_w,
        out_specs=(blk(0), blk(0), blk(0), blk(0), gspec) + (ANY,) * n_w,
        scratch_shapes=[pltpu.VMEM((2, HEAD_DIM, HEAD_DIM), F32)] + _exchange_scratch(n_w),
        compiler_params=_cparams(("arbitrary", "arbitrary")),
    )(dmr, raw, proj, g_ret, rq, rk, proj, states, wdec, jnp.transpose(wdec, (0, 2, 1)), qdec, kdec, cdec, *parts)


def _in_proj_bwd(x, g_mix, dh1, dq_r, dk_r, dv_r, drg, dq_f, dk_f, dv_f, df_col, proj, z, cos_t, sin_t, gq_t, gk_t, w_in_t):
    t_len = x.shape[0]
    tm = min(ROW_TILE, t_len)
    n_t = t_len // tm

    def body(x_ref, g_ref, dh1_ref, dqr_ref, dkr_ref, dvr_ref, drg_ref, dqf_ref, dkf_ref, dvf_ref, df_ref, fq_ref, fk_ref, z_ref,
             cos_ref, sin_ref, gq_ref, gk_ref, wm_ref, wf_ref,
             dproj_ref, dz_ref, dx_ref, dg_ref, dgq_ref, dgk_ref, db_ref, carry, gq_acc, gk_acc):
        i = pl.program_id(0)

        @pl.when(i == 0)
        def _():
            carry[...] = jnp.zeros_like(carry)
            gq_acc[...] = jnp.zeros_like(gq_acc)
            gk_acc[...] = jnp.zeros_like(gk_acc)
            dg_ref[...] = jnp.zeros_like(dg_ref)
            db_ref[...] = jnp.zeros_like(db_ref)

        c, s = cos_ref[...], sin_ref[...]
        gq, gk = gq_ref[...], gk_ref[...]
        dgq = jnp.zeros((1, LANES), F32)
        dgk = jnp.zeros((1, LANES), F32)
        for sl in _chunks(GROUP_W):
            dy = dqr_ref[:, sl] * 0.125
            dproj_ref[:, sl] = (dy * c + _swap32(dy * s)).astype(BF)
            dy = dkr_ref[:, sl]
            dproj_ref[:, GROUP_W + sl.start:GROUP_W + sl.stop] = (dy * c + _swap32(dy * s)).astype(BF)
            dproj_ref[:, 2 * GROUP_W + sl.start:2 * GROUP_W + sl.stop] = dvr_ref[:, sl].astype(BF)
            dproj_ref[:, 3 * GROUP_W + sl.start:3 * GROUP_W + sl.stop] = drg_ref[:, sl].astype(BF)
            for src, dsrc, gain, off in ((fq_ref, dqf_ref, gq, 4), (fk_ref, dkf_ref, gk, 5)):
                xr = src[:, sl].astype(F32)
                r = lax.rsqrt(_group_mean64(xr * xr) + EPS)
                xh = xr * r
                dy = dsrc[:, sl] * (0.125 if off == 4 else 1.0)
                dgs = jnp.sum(dy * xh, axis=0, keepdims=True)
                if off == 4:
                    dgq = dgq + dgs
                else:
                    dgk = dgk + dgs
                dxh = dy * gain
                dproj_ref[:, off * GROUP_W + sl.start:off * GROUP_W + sl.stop] = \
                    (r * (dxh - xh * _group_mean64(dxh * xh))).astype(BF)
            dproj_ref[:, 6 * GROUP_W + sl.start:6 * GROUP_W + sl.stop] = dvf_ref[:, sl].astype(BF)
        gq_acc[...] += dgq
        gk_acc[...] += dgk
        row = lax.broadcasted_iota(jnp.int32, (tm, tm), 0)
        col = lax.broadcasted_iota(jnp.int32, (tm, tm), 1)
        dlf = _dot_exact((col >= row).astype(BF), df_ref[...]) + carry[0:1, :]
        carry[...] = jnp.broadcast_to(dlf[0:1, :], carry.shape)
        lane = lax.broadcasted_iota(jnp.int32, (tm, LANES), 1)
        dz = jnp.where(lane < N_HEADS, dlf / (1.0 + jnp.exp(z_ref[...])), 0.0)
        db_ref[...] += jnp.sum(dz, axis=0, keepdims=True)
        dz_bf = dz.astype(BF)
        dz_ref[...] = dz_bf
        dn1 = jnp.dot(dz_bf, wf_ref[...], preferred_element_type=F32)
        for sec in range(MAIN_W // GROUP_W):
            sl = slice(sec * GROUP_W, (sec + 1) * GROUP_W)
            dn1 = dn1 + jnp.dot(dproj_ref[:, sl], wm_ref[sl, :], preferred_element_type=F32)
        dx, dg = _rms_bwd(x_ref[...], g_ref[...], dn1)
        dx_ref[...] = dh1_ref[...] + dx
        dg_ref[...] += dg

        @pl.when(i == n_t - 1)
        def _():
            dgq_ref[...] = gq_acc[:, :HEAD_DIM] + gq_acc[:, HEAD_DIM:]
            dgk_ref[...] = gk_acc[:, :HEAD_DIM] + gk_acc[:, HEAD_DIM:]

    row_spec = lambda w, col=0: pl.BlockSpec((tm, w), lambda i: (n_t - 1 - i, col))
    full = lambda a: pl.BlockSpec(a.shape, lambda i: (0,) * a.ndim)
    acc = lambda r, c: pl.BlockSpec((r, c), lambda i: (0, 0))
    return pl.pallas_call(
        body, name="in_proj_bwd", grid=(n_t,),
        out_shape=(jax.ShapeDtypeStruct((t_len, MAIN_W), BF), jax.ShapeDtypeStruct((t_len, LANES), BF),
                   jax.ShapeDtypeStruct((t_len, D_MODEL), F32), jax.ShapeDtypeStruct((1, D_MODEL), F32),
                   jax.ShapeDtypeStruct((1, HEAD_DIM), F32), jax.ShapeDtypeStruct((1, HEAD_DIM), F32),
                   jax.ShapeDtypeStruct((1, LANES), F32)),
        in_specs=[row_spec(D_MODEL), full(g_mix), row_spec(D_MODEL)] + [row_spec(GROUP_W)] * 7
        + [row_spec(LANES), row_spec(GROUP_W, 4), row_spec(GROUP_W, 5), row_spec(LANES), row_spec(LANES), row_spec(LANES),
           full(gq_t), full(gk_t), *_w_in_specs()],
        out_specs=(row_spec(MAIN_W), row_spec(LANES), row_spec(D_MODEL), acc(1, D_MODEL), acc(1, HEAD_DIM), acc(1, HEAD_DIM),
                   acc(1, LANES)),
        scratch_shapes=[pltpu.VMEM((8, LANES), F32), pltpu.VMEM((1, LANES), F32), pltpu.VMEM((1, LANES), F32)],
        compiler_params=_cparams(("arbitrary",)),
    )(x, g_mix, dh1, dq_r, dk_r, dv_r, drg, dq_f, dk_f, dv_f, df_col, proj, proj, z, cos_t, sin_t, gq_t, gk_t, w_in_t, w_in_t)


def _matmul_tn(a, b, name, bk=1024):
    t_len, m = a.shape
    n = b.shape[1]
    bm = m if m <= TN_MAX_ROWS else m // 2
    bk = min(bk, t_len)

    def body(a_ref, b_ref, o_ref):
        @pl.when(pl.program_id(1) == 0)
        def _():
            o_ref[...] = jnp.zeros_like(o_ref)

        o_ref[...] += _dot_tn(a_ref[...], b_ref[...])

    return pl.pallas_call(
        body, name=name, grid=(m // bm, t_len // bk),
        out_shape=jax.ShapeDtypeStruct((m, n), F32),
        in_specs=[pl.BlockSpec((bk, bm), lambda i, k: (k, i)), pl.BlockSpec((bk, n), lambda i, k: (k, 0))],
        out_specs=pl.BlockSpec((bm, n), lambda i, k: (i, 0)),
        compiler_params=_cparams(("arbitrary", "arbitrary")),
    )(a, b)


def _place():
    x, y, c = lax.axis_index("x"), lax.axis_index("y"), lax.axis_index("c")
    chips = [(1 - x, y), (x, 1 - y), (1 - x, 1 - y)]
    return x, y, c, chips


def _row_chunks(rows, limit):
    step = max(d for d in range(16, min(rows, limit) + 1, 16) if rows % d == 0)
    return [slice(i, i + step) for i in range(0, rows, step)]


ICI_CHUNK_ROWS = 256
D2D_CHUNK_ROWS = 256


def _gather_phase(phase, ins, outs, send_sems, recv_sems):
    x, y, c, chips = _place()
    me_chip = 2 * x + y
    sibling = (x, y, 1 - c)

    def copy(w, k, slot, half, to, rows=slice(None), src=None):
        dst = outs[w].at[slot, half, rows]
        return pltpu.make_async_remote_copy(src_ref=dst if src is None else src, dst_ref=dst,
                                            send_sem=send_sems.at[w, k], recv_sem=recv_sems.at[w, k],
                                            device_id=to, device_id_type=MESH)

    for w in range(len(ins)):
        for j, (px, py) in enumerate(chips):
            if phase == 0:
                for rows in _row_chunks(ins[w].shape[1], ICI_CHUNK_ROWS):
                    copy(w, j, me_chip, c, (px, py, c), rows, src=ins[w].at[c, rows]).start()
            elif phase == 1:
                copy(w, j, 2 * px + py, c, (x, y, c)).wait_recv()
                for rows in _row_chunks(ins[w].shape[1], D2D_CHUNK_ROWS):
                    copy(w, 3 + j, 2 * px + py, c, sibling, rows).start()
            else:
                copy(w, 3 + j, 2 * px + py, 1 - c, (x, y, c)).wait_recv()
                copy(w, j, me_chip, c, (px, py, c), src=ins[w].at[c]).wait_send()
                copy(w, 3 + j, 2 * px + py, c, sibling).wait_send()


def _gather_scratch(n_w):
    return [pltpu.SemaphoreType.DMA((n_w, 6)), pltpu.SemaphoreType.DMA((n_w, 6))]


def _all_gather_weights(shards):
    n_w = len(shards)

    def body(*refs):
        for phase in range(3):
            _gather_phase(phase, refs[:n_w], refs[n_w:2 * n_w], *refs[2 * n_w:])

    return pl.pallas_call(
        body, name="all_gather_weights",
        out_shape=tuple(jax.ShapeDtypeStruct((4,) + s.shape, s.dtype) for s in shards),
        in_specs=[ANY] * n_w, out_specs=(ANY,) * n_w, scratch_shapes=_gather_scratch(n_w),
    )(*shards)


def _exchange_phase(phase, ins, theirs, send_sems, recv_sems):
    x, y, c, _ = _place()

    def remote(w, k=slice(None), rows=slice(None)):
        return pltpu.make_async_remote_copy(src_ref=ins[w].at[k, 1 - c, rows], dst_ref=theirs[w].at[k, rows],
                                            send_sem=send_sems.at[w], recv_sem=recv_sems.at[w], device_id=(x, y, 1 - c),
                                            device_id_type=MESH)

    for w in range(len(ins)):
        if phase == 0:
            for k in range(4):
                for rows in _row_chunks(ins[w].shape[2], D2D_CHUNK_ROWS):
                    remote(w, k, rows).start()
        else:
            remote(w).wait()


def _exchange_scratch(n_w):
    return [pltpu.SemaphoreType.DMA((n_w,)), pltpu.SemaphoreType.DMA((n_w,))]


def _exchange_out_shapes(grads):
    return tuple(jax.ShapeDtypeStruct((4,) + g.shape[2:], g.dtype) for g in grads)


def _add_pairs(part, theirs, name, halves):
    _, _, r, c = part.shape
    rb = 64 if r % 64 == 0 else r
    n_w = len(halves)
    n_steps = r // rb

    def body(*refs):
        a_ref, b_ref = refs[:2]
        own_ref, ob_ref = refs[2 + n_w:4 + n_w]
        comm = (refs[2:2 + n_w], refs[4 + n_w:4 + 2 * n_w]) + tuple(refs[4 + 2 * n_w:])
        step = pl.program_id(0)

        @pl.when(step == 0)
        def _():
            _share_phase(0, *comm)

        my_chip = 2 * lax.axis_index("x") + lax.axis_index("y")
        ob_ref[...] = (a_ref[...] + b_ref[...]).astype(BF)
        own_ref[...] = a_ref[my_chip] + b_ref[my_chip]

        @pl.when(step == n_steps - 1)
        def _():
            _share_phase(1, *comm)

    spec = pl.BlockSpec((4, rb, c), lambda i: (0, i, 0))
    flat = pl.pallas_call(
        body, name=name, grid=(n_steps,),
        out_shape=(jax.ShapeDtypeStruct((r, c), F32), jax.ShapeDtypeStruct((4, r, c), BF)) + _share_out_shapes(halves),
        in_specs=[pl.BlockSpec((4, None, rb, c), lambda i: (0, lax.axis_index("c"), i, 0)), spec] + [ANY] * n_w,
        out_specs=(pl.BlockSpec((rb, c), lambda i: (i, 0)), spec) + (ANY,) * n_w,
        scratch_shapes=_share_scratch(n_w), compiler_params=_cparams(("arbitrary",)),
    )(part, theirs, *halves)
    return (flat[0], flat[1]), list(flat[2:])


def _scatter_phase(phase, bfs, got, send_sems, recv_sems):
    x, y, c, chips = _place()

    def remote(w, j, px, py, rows=slice(None)):
        return pltpu.make_async_remote_copy(src_ref=bfs[w].at[2 * px + py, rows], dst_ref=got[w].at[j, rows],
                                            send_sem=send_sems.at[w, j], recv_sem=recv_sems.at[w, j], device_id=(px, py, c),
                                            device_id_type=MESH)

    for w in range(len(bfs)):
        for j, (px, py) in enumerate(chips):
            if phase == 0:
                for rows in _row_chunks(bfs[w].shape[1], ICI_CHUNK_ROWS):
                    remote(w, j, px, py, rows).start()
            else:
                remote(w, j, px, py).wait()


def _scatter_scratch(n_w):
    return [pltpu.SemaphoreType.DMA((n_w, 3)), pltpu.SemaphoreType.DMA((n_w, 3))]


def _scatter_out_shapes(sums_bf16):
    return tuple(jax.ShapeDtypeStruct((3,) + s.shape[1:], BF) for s in sums_bf16)


def _add_received(own, got, name):
    r, c = own.shape
    rb = 64 if r % 64 == 0 else r

    def body(o_ref, g_ref, out_ref):
        out_ref[...] = ((o_ref[...] + g_ref[0].astype(F32)) + g_ref[1].astype(F32)) + g_ref[2].astype(F32)

    return pl.pallas_call(
        body, name=name, grid=(r // rb,), out_shape=jax.ShapeDtypeStruct((r, c), F32),
        in_specs=[pl.BlockSpec((rb, c), lambda i: (i, 0)), pl.BlockSpec((3, rb, c), lambda i: (0, i, 0))],
        out_specs=pl.BlockSpec((rb, c), lambda i: (i, 0)), compiler_params=_cparams(("arbitrary",)),
    )(own, got)


def _share_phase(phase, ins, outs, send_sems, recv_sems):
    x, y, c, _ = _place()

    def remote(w, rows=slice(None)):
        return pltpu.make_async_remote_copy(src_ref=ins[w].at[rows], dst_ref=outs[w].at[c, rows], send_sem=send_sems.at[w],
                                            recv_sem=recv_sems.at[w], device_id=(x, y, 1 - c), device_id_type=MESH)

    for w in range(len(ins)):
        if phase == 0:
            for rows in _row_chunks(ins[w].shape[0], D2D_CHUNK_ROWS):
                remote(w, rows).start()
        else:
            remote(w).wait()


def _share_scratch(n_w):
    return [pltpu.SemaphoreType.DMA((n_w,)), pltpu.SemaphoreType.DMA((n_w,))]


def _share_out_shapes(halves):
    return tuple(jax.ShapeDtypeStruct((2,) + h.shape, h.dtype) for h in halves)


def _share_with_sibling(halves):
    n_w = len(halves)

    def body(*refs):
        for phase in range(2):
            _share_phase(phase, refs[:n_w], refs[n_w:2 * n_w], *refs[2 * n_w:])

    return pl.pallas_call(
        body, name="share_with_sibling", out_shape=_share_out_shapes(halves),
        in_specs=[ANY] * n_w, out_specs=(ANY,) * n_w, scratch_shapes=_share_scratch(n_w),
    )(*halves)


def _small_phase(phase, p_ref, out_ref, slots, send_sems, recv_sems):
    x, y, cc, _ = _place()
    me = 4 * x + 2 * y + cc
    copies = []
    for k in range(1, 8):
        dx, dy, dc = (k >> 2) & 1, (k >> 1) & 1, k & 1
        to = (1 - x if dx else x, 1 - y if dy else y, 1 - cc if dc else cc)
        copies.append(pltpu.make_async_remote_copy(src_ref=p_ref, dst_ref=slots.at[me], send_sem=send_sems.at[k - 1],
                                                   recv_sem=recv_sems.at[k - 1], device_id=to, device_id_type=MESH))
    if phase == 0:
        slots[me] = p_ref[...]
        for cp in copies:
            cp.start()
    else:
        for cp in copies:
            cp.wait()
        total = slots[0]
        for d in range(1, 8):
            total = total + slots[d]
        out_ref[...] = total


def _adamw_update(w_ref, g_ref, m_ref, v_ref, d_ref, nm_ref, nv_ref):
    gv = g_ref[...]
    nm = ADAM_B1 * m_ref[...] + (1.0 - ADAM_B1) * gv
    nv = ADAM_B2 * v_ref[...] + (1.0 - ADAM_B2) * (gv * gv)
    nm_ref[...] = nm
    nv_ref[...] = nv
    m_hat = nm / (1.0 - ADAM_B1 ** ADAM_STEP)
    v_hat = nv / (1.0 - ADAM_B2 ** ADAM_STEP)
    d_ref[...] = -ADAM_LR * (m_hat / (jnp.sqrt(v_hat) + ADAM_EPS) + ADAM_WD * w_ref[...])


def _adamw_many(ws, gs, ms, vs, sums, pack):
    n_a, n_w = len(ws), len(sums)
    n_steps = ADAM_STEPS
    specs = [pl.BlockSpec((w.shape[0] // n_steps, w.shape[1]), lambda i: (i, 0)) for w in ws]
    pack_spec = pl.BlockSpec(pack.shape, lambda i: (0, 0))

    def body(*refs):
        ins = refs[:4 * n_a]
        p_ref = refs[4 * n_a + n_w]
        first_out = 4 * n_a + n_w + 1
        outs = refs[first_out:first_out + 3 * n_a]
        total_ref = refs[first_out + 3 * n_a + n_w]
        scratch = refs[first_out + 3 * n_a + n_w + 1:]
        scatter = (refs[4 * n_a:4 * n_a + n_w], refs[first_out + 3 * n_a:first_out + 3 * n_a + n_w]) + tuple(scratch[:2])
        small = (p_ref, total_ref) + tuple(scratch[2:])
        step = pl.program_id(0)

        @pl.when(step == 0)
        def _():
            _scatter_phase(0, *scatter)
            _small_phase(0, *small)

        for a in range(n_a):
            _adamw_update(*(ins[k * n_a + a] for k in range(4)), *(outs[3 * a + k] for k in range(3)))

        @pl.when(step == n_steps - 1)
        def _():
            _scatter_phase(1, *scatter)
            _small_phase(1, *small)

    flat = pl.pallas_call(
        body, name="adamw_late", grid=(n_steps,),
        out_shape=tuple(jax.ShapeDtypeStruct(w.shape, F32) for w in ws for _ in range(3)) + _scatter_out_shapes(sums)
        + (jax.ShapeDtypeStruct(pack.shape, F32),),
        in_specs=specs * 4 + [ANY] * n_w + [pack_spec],
        out_specs=tuple(s for s in specs for _ in range(3)) + (ANY,) * n_w + (pack_spec,),
        scratch_shapes=_scatter_scratch(n_w) + [pltpu.VMEM((8,) + pack.shape, F32), pltpu.SemaphoreType.DMA((7,)),
                                                pltpu.SemaphoreType.DMA((7,))],
        compiler_params=_cparams(("arbitrary",)),
    )(*ws, *gs, *ms, *vs, *sums, pack)
    return [tuple(flat[3 * a:3 * a + 3]) for a in range(n_a)] + list(flat[3 * n_a:])


def _adamw(w, g, m, v, name):
    r, c = w.shape
    rb, cb = (128, c) if r % 128 == 0 else (r, LANES if (r % 8 and c % LANES == 0) else c)

    def body(*refs):
        _adamw_update(*refs)

    spec = pl.BlockSpec((rb, cb), lambda i, j: (i, j))
    return pl.pallas_call(
        body, name=name, grid=(r // rb, c // cb), out_shape=(jax.ShapeDtypeStruct((r, c), F32),) * 3,
        in_specs=[spec] * 4, out_specs=(spec,) * 3, compiler_params=_cparams(("arbitrary", "arbitrary")),
    )(w, g, m, v)


def _rope_tables(t_len):
    inv_freq = ROPE_BASE ** (-jnp.arange(0, HEAD_DIM, 2, dtype=F32) / HEAD_DIM)
    ang = jnp.arange(t_len, dtype=F32)[:, None] * inv_freq[None, :]
    cos, sin = jnp.cos(ang), jnp.sin(ang)
    cos_t = jnp.concatenate([cos, cos, cos, cos], axis=-1)
    sin_t = jnp.concatenate([-sin, sin, -sin, sin], axis=-1)
    return cos_t, sin_t


def _cols_to_shards(dw):
    r, n = dw.shape
    return jnp.transpose(dw.reshape(2, r // 2, 4, n // 4), (2, 0, 1, 3))


def _rows_to_shards(dw):
    r, n = dw.shape
    rows = r // 4
    if rows % SUBLANES == 0:
        padded = _pad_rows(dw.reshape(4, rows, n))
    else:
        window = rows + SUBLANES - rows % SUBLANES
        padded = _pad_rows(jnp.stack([dw[rows * k // SUBLANES * SUBLANES:][:window] for k in range(4)]))
    return padded.reshape(4, 2, padded.shape[1] // 2, n)


def _shard_row_offset(rows):
    return (rows * (2 * lax.axis_index("x") + lax.axis_index("y"))) % SUBLANES


def _pad_lanes(a):
    extra = -a.shape[-1] % LANES
    return a if extra == 0 else jnp.pad(a, [(0, 0)] * (a.ndim - 1) + [(0, extra)])


def _pad_rows(a):
    rows = a.shape[-2]
    extra = 0 if rows % SHARD_ROW_ALIGN == 0 else -rows % SHARD_ROW_PAD
    return a if extra == 0 else jnp.pad(a, [(0, 0)] * (a.ndim - 2) + [(0, extra), (0, 0)])


def _pad_row(a, width=D_MODEL):
    a = a.reshape(1, -1)
    return jnp.pad(a, ((0, 0), (0, width - a.shape[1])))


def kernel(x, mem, g_mix, w_in, b_forget, g_ret_out, g_fox_q, g_fox_k, w_out, g_xattn, w_xq, w_xkv, g_mem, g_xq, g_xk, w_xo, g_ffn, w_gate, w_up, w_down, loss_target, m_g_mix, m_w_in, m_b_forget, m_g_ret_out, m_g_fox_q, m_g_fox_k, m_w_out, m_g_xattn, m_w_xq, m_w_xkv, m_g_mem, m_g_xq, m_g_xk, m_w_xo, m_g_ffn, m_w_gate, m_w_up, m_w_down, v_g_mix, v_w_in, v_b_forget, v_g_ret_out, v_g_fox_q, v_g_fox_k, v_w_out, v_g_xattn, v_w_xq, v_w_xkv, v_g_mem, v_g_xq, v_g_xk, v_w_xo, v_g_ffn, v_w_gate, v_w_up, v_w_down):
    big = {"w_in": (w_in, m_w_in, v_w_in), "w_out": (w_out, m_w_out, v_w_out), "w_xq": (w_xq, m_w_xq, v_w_xq),
           "w_xkv": (w_xkv, m_w_xkv, v_w_xkv), "w_xo": (w_xo, m_w_xo, v_w_xo), "w_gate": (w_gate, m_w_gate, v_w_gate),
           "w_up": (w_up, m_w_up, v_w_up), "w_down": (w_down, m_w_down, v_w_down)}
    for n in TRANSPOSED:
        big[n] = tuple(jnp.swapaxes(a, 1, 2) for a in big[n])
    shards = {}
    for n in big:
        w = _pad_rows(_pad_lanes(big[n][0][0].astype(BF)))
        shards[n] = w.reshape(2, w.shape[0] // 2, w.shape[1])
    sizes = {n: big[n][0].shape[1:] for n in big}
    w_in_full = _assemble_weight("w_in", _all_gather_weights([shards["w_in"]])[0], shards["w_in"], sizes["w_in"])
    small_w ={"g_mix": g_mix, "b_forget": b_forget, "g_ret_out": g_ret_out, "g_fox_q": g_fox_q, "g_fox_k": g_fox_k,
               "g_xattn": g_xattn, "g_mem": g_mem, "g_xq": g_xq, "g_xk": g_xk, "g_ffn": g_ffn}
    m_small = {"g_mix": m_g_mix, "b_forget": m_b_forget, "g_ret_out": m_g_ret_out, "g_fox_q": m_g_fox_q, "g_fox_k": m_g_fox_k,
               "g_xattn": m_g_xattn, "g_mem": m_g_mem, "g_xq": m_g_xq, "g_xk": m_g_xk, "g_ffn": m_g_ffn}
    v_small = {"g_mix": v_g_mix, "b_forget": v_b_forget, "g_ret_out": v_g_ret_out, "g_fox_q": v_g_fox_q, "g_fox_k": v_g_fox_k,
               "g_xattn": v_g_xattn, "g_mem": v_g_mem, "g_xq": v_g_xq, "g_xk": v_g_xk, "g_ffn": v_g_ffn}
    loss_part, grad_x, sums, got, in_parts, small_g = _local_step(x[0], mem[0], loss_target[0], w_in_full, shards, sizes, small_w)
    return _reduce_and_update(big, sums, got, in_parts, small_w, small_g, loss_part, grad_x, m_small, v_small)


def _assemble_weight(name, gathered, own, size):
    rows, width = size
    my_chip = 2 * lax.axis_index("x") + lax.axis_index("y")
    g = lax.dynamic_update_slice(gathered, own[None], (my_chip, 0, 0, 0))
    g = g.reshape(4, 2 * g.shape[2], g.shape[3])[:, :rows, :width]
    return jnp.transpose(g, (1, 0, 2)).reshape(rows, 4 * width) if name in COL_SHARDED else g.reshape(4 * rows, width)


def _shard_parts(names, dw):
    return [_pad_lanes(_cols_to_shards(dw[n]) if n in COL_SHARDED else _rows_to_shards(dw[n])) for n in names]


def _add_pairs_many(parts, theirs, name):
    n_a = len(parts)
    n_steps = min(p.shape[2] for p in parts) // 32
    rb = [p.shape[2] // n_steps for p in parts]
    part_specs = [pl.BlockSpec((4, None, r, p.shape[3]), lambda i: (0, lax.axis_index("c"), i, 0)) for p, r in zip(parts, rb)]
    quad_specs = [pl.BlockSpec((4, r, p.shape[3]), lambda i: (0, i, 0)) for p, r in zip(parts, rb)]
    own_specs = [pl.BlockSpec((r, p.shape[3]), lambda i: (i, 0)) for p, r in zip(parts, rb)]

    def body(*refs):
        my_chip = 2 * lax.axis_index("x") + lax.axis_index("y")
        for a in range(n_a):
            a_ref, b_ref, own_ref, ob_ref = refs[a], refs[n_a + a], refs[2 * n_a + a], refs[3 * n_a + a]
            ob_ref[...] = (a_ref[...] + b_ref[...]).astype(BF)
            own_ref[...] = a_ref[my_chip] + b_ref[my_chip]

    flat = pl.pallas_call(
        body, name=name, grid=(n_steps,),
        out_shape=tuple(jax.ShapeDtypeStruct(p.shape[2:], F32) for p in parts)
        + tuple(jax.ShapeDtypeStruct((4,) + p.shape[2:], BF) for p in parts),
        in_specs=part_specs + quad_specs, out_specs=tuple(own_specs) + tuple(quad_specs),
        compiler_params=_cparams(("arbitrary",)),
    )(*parts, *theirs)
    return [(flat[a], flat[n_a + a]) for a in range(n_a)]


def _core_sums(parts, theirs):
    out = [None] * len(parts)
    for tag, pick in (("a", lambda p: p.shape[2] % LANES == 0), ("b", lambda p: p.shape[2] % LANES != 0)):
        idx = [i for i, p in enumerate(parts) if pick(p)]
        for i, res in zip(idx, _add_pairs_many([parts[i] for i in idx], [theirs[i] for i in idx], f"core_sum_late_{tag}")):
            out[i] = res
    return out


def _local_step(xs, mems, tgt, w_in_full, shards, sizes, small_w):
    g_mix, b_forget, g_ret_out, g_fox_q, g_fox_k = (small_w[n] for n in ("g_mix", "b_forget", "g_ret_out", "g_fox_q", "g_fox_k"))
    g_xattn, g_mem, g_xq, g_xk, g_ffn = (small_w[n] for n in ("g_xattn", "g_mem", "g_xq", "g_xk", "g_ffn"))
    w_in_t = jnp.pad(w_in_full, ((0, MAIN_W + LANES - IN_W), (0, 0)))
    t_len = xs.shape[0]
    cos_t, sin_t = _rope_tables(t_len)
    tables = _decay_tables(min(RET_BLOCK, t_len))
    gq_t = jnp.concatenate([g_fox_q, g_fox_q], axis=-1)
    gk_t = jnp.concatenate([g_fox_k, g_fox_k], axis=-1)
    b_pad = _pad_row(b_forget, LANES)
    g_ret = g_ret_out.reshape(N_HEADS // 2, 1, LANES)

    n1, proj, rq, rk, q_aug, k_aug, z = _in_proj_fwd(xs, g_mix, w_in_t, b_pad, cos_t, sin_t, gq_t, gk_t)
    raw, mix_r, states = _retention_fwd(rq, rk, proj, g_ret, tables)
    mix_f, o32, lse, *gathered = _fox_fwd(q_aug, k_aug, proj, [shards[n] for n in LATE])
    full = {n: _assemble_weight(n, g, shards[n], sizes[n]) for n, g in zip(LATE, gathered)}
    memn, kraw, kn, vmem = _mem_kv_fwd(mems, g_mem, full["w_xkv"], g_xk)
    h1, hn2, qx, o_x, h2 = _attn_out_xattn_fwd(xs, mix_r, mix_f, full["w_out"], g_xattn, full["w_xq"], g_xq, kn, vmem, full["w_xo"])
    hn3, gate, up, act, dh3, loss_part = _ffn_loss_fwd(h2, g_ffn, full["w_gate"], full["w_up"], full["w_down"], tgt)

    dgate, dup, dh2, dg_ffn = _ffn_bwd(dh3, gate, up, h2, g_ffn, full["w_gate"], full["w_up"], full["w_down"])
    dqx, dh1, dmr, dmf, dkn, dvm, dg_xattn, dg_xq = _attn_out_xattn_bwd(dh2, h1, qx, kn, vmem, full["w_xo"], full["w_xq"],
                                                                      full["w_out"], g_xattn, g_xq)
    dw_xkv, dg_mem, dg_xk = _mem_kv_bwd(dkn, dvm, kraw, mems, memn, g_mem, g_xk, full["w_xkv"])
    dw_gu = _matmul_tn_pair(dgate, dup, hn3, "dw_gate_up")
    dw = {
        "w_out": _matmul_tn_pair(mix_r, mix_f, dh1, "dw_out").reshape(D_MODEL, D_MODEL),
        "w_xq": _matmul_tn(hn2, dqx, "dw_xq"),
        "w_xkv": dw_xkv,
        "w_xo": _matmul_tn(o_x, dh2, "dw_xo"),
        "w_gate": dw_gu[0],
        "w_up": dw_gu[1],
        "w_down": _matmul_tn(act, dh3, "dw_down"),
    }
    late_parts = _shard_parts(LATE, dw)
    dq_r, dk_r, dv_r, drg, dg_ret, *late_theirs = _retention_bwd(dmr, raw, proj, g_ret, rq, rk, states, tables, late_parts)
    late_sums = _core_sums(late_parts, late_theirs)
    dq_f, dk_f, dv_f, df, *late_got = _fox_bwd(q_aug, k_aug, proj, dmf, o32, lse, [s[1] for s in late_sums])
    df_col = jnp.pad(jnp.transpose(df, (1, 0, 2)).reshape(t_len, N_HEADS), ((0, 0), (0, LANES - N_HEADS)))
    dproj, dz, grad_x, dg_mix, dg_fq, dg_fk, db = _in_proj_bwd(xs, g_mix, dh1, dq_r, dk_r, dv_r, drg, dq_f, dk_f, dv_f, df_col,
                                                              proj, z, cos_t, sin_t, gq_t, gk_t, w_in_t)

    dw_in = jnp.concatenate([_matmul_tn(dproj, n1, "dw_in_main"), _matmul_tn(dz, n1, "dw_in_ff")[:IN_W - MAIN_W]], axis=0)
    in_parts = _shard_parts(("w_in",), {"w_in": dw_in})
    sums = {n: s[0] for n, s in zip(LATE, late_sums)}
    got = dict(zip(LATE, late_got))
    small_g = {"g_mix": dg_mix, "b_forget": db[:, :N_HEADS], "g_ret_out": dg_ret, "g_fox_q": dg_fq, "g_fox_k": dg_fk,
               "g_xattn": dg_xattn, "g_mem": dg_mem, "g_xq": dg_xq, "g_xk": dg_xk, "g_ffn": dg_ffn}
    return loss_part, grad_x, sums, got, in_parts, small_g


def _add_received_many(owns, gots, parts):
    n_a, n_w = len(owns), len(parts)
    n_steps = CHIP_SUM_STEPS
    own_specs = [pl.BlockSpec((o.shape[0] // n_steps, o.shape[1]), lambda i: (i, 0)) for o in owns]
    got_specs = [pl.BlockSpec((3, o.shape[0] // n_steps, o.shape[1]), lambda i: (0, i, 0)) for o in owns]

    def body(*refs):
        first_out = 2 * n_a + n_w
        comm = (refs[2 * n_a:first_out], refs[first_out + n_a:first_out + n_a + n_w]) + tuple(refs[first_out + n_a + n_w:])
        step = pl.program_id(0)

        @pl.when(step == 0)
        def _():
            _exchange_phase(0, *comm)

        for a in range(n_a):
            o_ref, g_ref, out_ref = refs[a], refs[n_a + a], refs[first_out + a]
            out_ref[...] = ((o_ref[...] + g_ref[0].astype(F32)) + g_ref[1].astype(F32)) + g_ref[2].astype(F32)

        @pl.when(step == n_steps - 1)
        def _():
            _exchange_phase(1, *comm)

    flat = pl.pallas_call(
        body, name="chip_sum_late", grid=(n_steps,),
        out_shape=tuple(jax.ShapeDtypeStruct(o.shape, F32) for o in owns) + _exchange_out_shapes(parts),
        in_specs=own_specs + got_specs + [ANY] * n_w, out_specs=tuple(own_specs) + (ANY,) * n_w,
        scratch_shapes=_exchange_scratch(n_w), compiler_params=_cparams(("arbitrary",)),
    )(*owns, *gots, *parts)
    return list(flat[:n_a]), list(flat[n_a:])


def _final_grads(names, big, finals, shared):
    my_core = lax.axis_index("c")
    out = {}
    for n, s, fin in zip(names, shared, finals):
        s = lax.dynamic_update_slice(s, fin[None], (my_core, 0, 0))
        s = s.reshape(2 * s.shape[1], s.shape[2])
        rows, width = big[n][0].shape[1:]
        out[n] = s[:rows, :width] if rows % SUBLANES == 0 else lax.dynamic_slice(s, (_shard_row_offset(rows), 0), (rows, width))
    return out


def _reduce_and_update(big, sums, got, in_parts, small_w, small_g, loss_part, grad_x, m_small, v_small):
    small_names = list(small_w)
    pad_rows = SMALL_ROWS - len(small_names) - 1
    stack = lambda d: jnp.concatenate([_pad_row(d[n]) for n in small_names] + [jnp.zeros((pad_rows + 1, D_MODEL), F32)], axis=0)
    g_pack = jnp.concatenate([_pad_row(small_g[n]) for n in small_names] + [_pad_row(loss_part[0:1, 0:1])]
                             + [jnp.zeros((pad_rows, D_MODEL), F32)], axis=0)
    late_finals, in_theirs = _add_received_many([sums[n] for n in LATE], [got[n] for n in LATE], in_parts)
    (in_own, in_bf), late_shared = _add_pairs(in_parts[0], in_theirs[0], "core_sum_w_in", late_finals)
    grads = _final_grads(LATE, big, late_finals, late_shared)
    *late_updates, in_got, g_tot = _adamw_many([big[n][0][0] for n in LATE], [grads[n] for n in LATE], [big[n][1][0] for n in LATE],
                                               [big[n][2][0] for n in LATE], [in_bf], g_pack)
    updates = dict(zip(LATE, late_updates))
    in_final = [_add_received(in_own, in_got, "chip_sum_w_in")]
    grads.update(_final_grads(("w_in",), big, in_final, _share_with_sibling(in_final)))
    updates["w_in"] = _adamw(big["w_in"][0][0], grads["w_in"], big["w_in"][1][0], big["w_in"][2][0], "adamw_w_in")
    deltas, new_m, new_v = {}, {}, {}
    for n in big:
        restore = (lambda a: jnp.swapaxes(a[None], 1, 2)) if n in TRANSPOSED else (lambda a: a[None])
        grads[n] = restore(grads[n])
        deltas[n], new_m[n], new_v[n] = (restore(a) for a in updates[n])

    d_s, m_s, v_s = _adamw(stack(small_w), g_tot, stack(m_small), stack(v_small), "adamw_small")
    for i, n in enumerate(small_names):
        shape = small_w[n].shape
        size = int(np.prod(shape))
        grads[n] = g_tot[i, :size].reshape(shape)
        deltas[n], new_m[n], new_v[n] = d_s[i, :size].reshape(shape), m_s[i, :size].reshape(shape), v_s[i, :size].reshape(shape)
    loss = g_tot[len(small_names), 0]

    order = ["g_mix", "w_in", "b_forget", "g_ret_out", "g_fox_q", "g_fox_k", "w_out", "g_xattn", "w_xq", "w_xkv", "g_mem", "g_xq",
             "g_xk", "w_xo", "g_ffn", "w_gate", "w_up", "w_down"]
    return (loss, grad_x[None], *[grads[n] for n in order], *[deltas[n] for n in order], *[new_m[n] for n in order],
            *[new_v[n] for n in order])
```

```python
import functools

import numpy as np
import jax
import jax.numpy as jnp
from jax import lax
from jax.experimental import pallas as pl
from jax.experimental.pallas import tpu as pltpu

F32 = jnp.float32
BF = jnp.bfloat16

D_MODEL = 1024
HEAD_DIM = 64
N_HEADS = 8
GROUP_W = 512
N_XH = 4
XHD = 256
D_FF = 2816
MAIN_W = 3584
IN_W = 3592
ROPE_BASE = 10000.0
LOG2E = 1.4426950408889634
LN2 = 0.6931471805599453
EPS = 1e-6
NEG = -1e30
LANES = 128
SUBLANES = 8
RET_BLOCK = 256
REF_CHUNK = 64
ROW_TILE = 512
FFN_BWD_TILE = 256
ATT_BLOCK = 256
FWD_GROUP = 4
TN_MAX_ROWS = 1408
SMALL_ROWS = 16
COL_SHARDED = ("w_xkv",)
TRANSPOSED = ("w_in", "w_gate", "w_up")
SHARD_ROW_ALIGN = 32
SHARD_ROW_PAD = 256
LATE = ("w_out", "w_xq", "w_xkv", "w_xo", "w_gate", "w_up", "w_down")
VMEM_LIMIT = 56 * 1024 * 1024

ADAM_LR = 0.001
ADAM_B1 = 0.9
ADAM_B2 = 0.999
ADAM_EPS = 1e-08
ADAM_WD = 0.01
ADAM_STEP = 10
CHIP_SUM_STEPS = 2
ADAM_STEPS = 8

MESH = pl.DeviceIdType.MESH
ANY = pl.BlockSpec(memory_space=pl.ANY)
VMEM_SPEC = pl.BlockSpec(memory_space=pltpu.VMEM)


def _cparams(sem=None, vmem=VMEM_LIMIT):
    return pltpu.CompilerParams(dimension_semantics=sem, vmem_limit_bytes=vmem)


def _dot(a, b):
    return jnp.dot(a.astype(BF), b.astype(BF), preferred_element_type=F32)


def _dot_nt(a, b):
    return lax.dot_general(a.astype(BF), b.astype(BF), (((1,), (1,)), ((), ())), preferred_element_type=F32)


def _dot_tn(a, b):
    return lax.dot_general(a.astype(BF), b.astype(BF), (((0,), (0,)), ((), ())), preferred_element_type=F32)


def _split3(x):
    hi = x.astype(BF)
    r = x - hi.astype(F32)
    mid = r.astype(BF)
    lo = (r - mid.astype(F32)).astype(BF)
    return hi, mid, lo


def _dot_exact(ind, x):
    hi, mid, lo = _split3(x)
    return (jnp.dot(ind, lo, preferred_element_type=F32) + jnp.dot(ind, mid, preferred_element_type=F32)
            + jnp.dot(ind, hi, preferred_element_type=F32))


def _dot_nt_exact(ind, x):
    hi, mid, lo = _split3(x)
    dn = (((1,), (1,)), ((), ()))
    return (lax.dot_general(ind, lo, dn, preferred_element_type=F32) + lax.dot_general(ind, mid, dn, preferred_element_type=F32)
            + lax.dot_general(ind, hi, dn, preferred_element_type=F32))


def _sigmoid(x):
    return 1.0 / (1.0 + jnp.exp(-x))


def _rms_fwd(x, g):
    r = lax.rsqrt(jnp.mean(x * x, axis=-1, keepdims=True) + EPS)
    return x * r * g


def _rms_bwd(x, g, dy):
    r = lax.rsqrt(jnp.mean(x * x, axis=-1, keepdims=True) + EPS)
    xh = x * r
    dg = jnp.sum(dy * xh, axis=0, keepdims=True)
    dxh = dy * g
    dx = r * (dxh - xh * jnp.mean(dxh * xh, axis=-1, keepdims=True))
    return dx, dg


def _group_mean64(x):
    lane = lax.broadcasted_iota(jnp.int32, x.shape, 1)
    lo = lane < HEAD_DIM
    s_lo = jnp.sum(jnp.where(lo, x, 0.0), axis=-1, keepdims=True)
    s_hi = jnp.sum(jnp.where(lo, 0.0, x), axis=-1, keepdims=True)
    return jnp.where(lo, s_lo, s_hi) * (1.0 / HEAD_DIM)


def _swap32(x):
    lane = lax.broadcasted_iota(jnp.int32, x.shape, 1)
    first = (lane % HEAD_DIM) < (HEAD_DIM // 2)
    return jnp.where(first, pltpu.roll(x, LANES - HEAD_DIM // 2, axis=1), pltpu.roll(x, HEAD_DIM // 2, axis=1))


def _chunks(w):
    return [slice(j * LANES, (j + 1) * LANES) for j in range(w // LANES)]


def _aug_pair(qk, f_cols, is_query):
    lane = lax.broadcasted_iota(jnp.int32, qk.shape, 1)
    a = lane - HEAD_DIM
    values = (qk, pltpu.roll(qk, HEAD_DIM, axis=1))
    out = []
    for hh in range(2):
        hi, mid, lo = (p.astype(F32) for p in _split3(f_cols[hh] * LOG2E))
        if is_query:
            aux = jnp.where(a == 0, hi, jnp.where(a == 1, mid, jnp.where(a == 2, lo, jnp.where(a < 6, 1.0, 0.0))))
        else:
            aux = jnp.where(a < 3, 1.0, jnp.where(a == 3, -hi, jnp.where(a == 4, -mid, jnp.where(a == 5, -lo, 0.0))))
        out.append(jnp.where(a < 0, values[hh], aux))
    return jnp.concatenate(out, axis=-1).astype(BF)


def _mem_kv_fwd(mem, g_mem, w_xkv, g_xk):
    m_tok = mem.shape[0]

    def body(mem_ref, gm_ref, w_ref, gk_ref, memn_ref, kraw_ref, kn_ref, v_ref):
        mn = _rms_fwd(mem_ref[...], gm_ref[...]).astype(BF)
        memn_ref[...] = mn
        kv = jnp.dot(mn, w_ref[...], preferred_element_type=F32)
        k = kv[:, :D_MODEL]
        kraw_ref[...] = k
        v_ref[...] = kv[:, D_MODEL:].astype(BF)
        for h in range(N_XH):
            sl = slice(h * XHD, (h + 1) * XHD)
            kn_ref[:, sl] = _rms_fwd(k[:, sl], gk_ref[...]).astype(BF)

    return pl.pallas_call(
        body, name="mem_kv_fwd",
        out_shape=(jax.ShapeDtypeStruct((m_tok, D_MODEL), BF), jax.ShapeDtypeStruct((m_tok, D_MODEL), F32),
                   jax.ShapeDtypeStruct((m_tok, D_MODEL), BF), jax.ShapeDtypeStruct((m_tok, D_MODEL), BF)),
        in_specs=[VMEM_SPEC] * 4, out_specs=(VMEM_SPEC,) * 4, compiler_params=_cparams(),
    )(mem, g_mem, w_xkv, g_xk)


def _in_proj_fwd(x, g_mix, w_in_t, b_pad, cos_t, sin_t, gq_t, gk_t):
    t_len = x.shape[0]
    tm = min(ROW_TILE, t_len)
    n_t = t_len // tm

    def body(x_ref, g_ref, wm_ref, wf_ref, b_ref, cos_ref, sin_ref, gq_ref, gk_ref,
             n1_ref, proj_ref, rq_ref, rk_ref, qa_ref, ka_ref, z_ref, carry):
        i = pl.program_id(0)

        @pl.when(i == 0)
        def _():
            carry[...] = jnp.zeros_like(carry)

        n1 = _rms_fwd(x_ref[...], g_ref[...]).astype(BF)
        n1_ref[...] = n1
        z = _dot_nt(n1, wf_ref[...]) + b_ref[...]
        z_ref[...] = z
        lane = lax.broadcasted_iota(jnp.int32, z.shape, 1)
        lf = jnp.where(lane < N_HEADS, jnp.minimum(z, 0.0) - jnp.log(1.0 + jnp.exp(-jnp.abs(z))), 0.0)
        row = lax.broadcasted_iota(jnp.int32, (tm, tm), 0)
        col = lax.broadcasted_iota(jnp.int32, (tm, tm), 1)
        tri = (row >= col).astype(BF)
        fc = _dot_exact(tri, lf) + carry[0:1, :]
        carry[...] = jnp.broadcast_to(fc[tm - 1:tm, :], carry.shape)
        c, s = cos_ref[...], sin_ref[...]

        def section(n):
            p = _dot_nt(n1, wm_ref[n * GROUP_W:(n + 1) * GROUP_W, :])
            proj_ref[:, n * GROUP_W:(n + 1) * GROUP_W] = p.astype(BF)
            return p

        def rotate(p, out_ref, scale):
            for sl in _chunks(GROUP_W):
                out_ref[:, sl] = ((p[:, sl] * c + _swap32(p[:, sl]) * s) * scale).astype(BF)

        def norm_aug(p, gain, out_ref, scale, is_query):
            for j, sl in enumerate(_chunks(GROUP_W)):
                f = p[:, sl]
                f = f * lax.rsqrt(_group_mean64(f * f) + EPS) * gain * scale
                out_ref[:, 2 * j * LANES:2 * (j + 1) * LANES] = _aug_pair(f, [fc[:, 2 * j:2 * j + 1], fc[:, 2 * j + 1:2 * j + 2]], is_query)

        p_rq, p_rk = section(0), section(1)
        rotate(p_rq, rq_ref, 0.125)
        section(2)
        rotate(p_rk, rk_ref, 1.0)
        section(3)
        p_fq = section(4)
        p_fk = section(5)
        norm_aug(p_fq, gq_ref[...], qa_ref, 0.125 * LOG2E, True)
        section(6)
        norm_aug(p_fk, gk_ref[...], ka_ref, 1.0, False)

    row_spec = lambda w: pl.BlockSpec((tm, w), lambda i: (i, 0))
    full = lambda a: pl.BlockSpec(a.shape, lambda i: (0,) * a.ndim)
    return pl.pallas_call(
        body, name="in_proj_fwd", grid=(n_t,),
        out_shape=(jax.ShapeDtypeStruct((t_len, D_MODEL), BF), jax.ShapeDtypeStruct((t_len, MAIN_W), BF),
                   jax.ShapeDtypeStruct((t_len, GROUP_W), BF), jax.ShapeDtypeStruct((t_len, GROUP_W), BF),
                   jax.ShapeDtypeStruct((t_len, 2 * GROUP_W), BF), jax.ShapeDtypeStruct((t_len, 2 * GROUP_W), BF),
                   jax.ShapeDtypeStruct((t_len, LANES), F32)),
        in_specs=[row_spec(D_MODEL), full(g_mix), *_w_in_specs(), full(b_pad), row_spec(LANES), row_spec(LANES),
                  full(gq_t), full(gk_t)],
        out_specs=(row_spec(D_MODEL), row_spec(MAIN_W), row_spec(GROUP_W), row_spec(GROUP_W), row_spec(2 * GROUP_W),
                   row_spec(2 * GROUP_W), row_spec(LANES)),
        scratch_shapes=[pltpu.VMEM((8, LANES), F32)],
        compiler_params=_cparams(("arbitrary",)),
    )(x, g_mix, w_in_t, w_in_t, b_pad, cos_t, sin_t, gq_t, gk_t)


def _w_in_specs():
    return (pl.BlockSpec((MAIN_W, D_MODEL), lambda i: (0, 0)), pl.BlockSpec((LANES, D_MODEL), lambda i: (MAIN_W // LANES, 0)))


def _matmul_tn_pair(a1, a2, b, name, bk=1024):
    t_len, m = a1.shape
    n = b.shape[1]
    bm = m if m <= TN_MAX_ROWS else m // 2
    bk = min(bk, t_len)

    def body(a1_ref, a2_ref, b_ref, o_ref):
        @pl.when(pl.program_id(1) == 0)
        def _():
            o_ref[...] = jnp.zeros_like(o_ref)

        bv = b_ref[...]
        o_ref[0] += _dot_tn(a1_ref[...], bv)
        o_ref[1] += _dot_tn(a2_ref[...], bv)

    a_spec = pl.BlockSpec((bk, bm), lambda i, k: (k, i))
    return pl.pallas_call(
        body, name=name, grid=(m // bm, t_len // bk),
        out_shape=jax.ShapeDtypeStruct((2, m, n), F32),
        in_specs=[a_spec, a_spec, pl.BlockSpec((bk, n), lambda i, k: (k, 0))],
        out_specs=pl.BlockSpec((2, bm, n), lambda i, k: (0, i, 0)),
        compiler_params=_cparams(("arbitrary", "arbitrary")),
    )(a1, a2, b)


def _decay_tables(c):
    h = np.arange(N_HEADS, dtype=np.float64)
    lg = np.log(1.0 - 2.0 ** (-5.0 - h)).astype(np.float32).astype(np.float64)
    t = np.arange(c)
    same_or_earlier = (t[None, :] // REF_CHUNK) <= (t[:, None] // REF_CHUNK)
    w = np.where(same_or_earlier[None], np.exp(lg[:, None, None] * np.abs(t[:, None] - t[None, :])[None]), 0.0)
    qd = np.exp(lg[:, None] * (t[None, :] + 1.0))
    kd = np.exp(lg[:, None] * (c - 1.0 - t[None, :]))
    cd = np.exp(lg * c)
    ones = np.ones((1, 1, HEAD_DIM))
    return (jnp.asarray(w, F32), jnp.asarray(qd[:, :, None] * ones, F32), jnp.asarray(kd[:, :, None] * ones, F32),
            jnp.asarray(cd[:, None, None] * np.ones((1, HEAD_DIM, HEAD_DIM)), F32))


def _retention_fwd(rq, rk, proj, g_ret, tables):
    t_len = rq.shape[0]
    c = min(RET_BLOCK, t_len)
    n_b = t_len // c
    wdec, qdec, kdec, cdec = tables
    v_col, g_col = 2 * GROUP_W // LANES, 3 * GROUP_W // LANES

    def body(q_ref, k_ref, v_ref, rg_ref, g_ref, w_ref, qd_ref, kd_ref, cd_ref, raw_ref, mix_ref, st_ref, state):
        i = pl.program_id(1)

        @pl.when(i == 0)
        def _():
            state[...] = jnp.zeros_like(state)

        q2, k2, v2 = q_ref[...], k_ref[...], v_ref[...]
        heads = [tuple(t[:, hh * HEAD_DIM:(hh + 1) * HEAD_DIM] for t in (q2, k2, v2)) for hh in range(2)]
        scores = [(_dot_nt(q, k) * w_ref[hh]).astype(BF) for hh, (q, k, _) in enumerate(heads)]
        outs = []
        for hh, (q, k, v) in enumerate(heads):
            sp = state[hh]
            st_ref[0, 0, hh] = sp
            outs.append(jnp.dot(scores[hh], v, preferred_element_type=F32) + _dot(q.astype(F32) * qd_ref[hh], sp))
            state[hh] = sp * cd_ref[hh] + _dot_tn(k.astype(F32) * kd_ref[hh], v)
        o2 = jnp.concatenate(outs, axis=-1)
        raw_ref[...] = o2
        xc = o2 - _group_mean64(o2)
        xh = xc * lax.rsqrt(_group_mean64(xc * xc) + EPS)
        gate = rg_ref[...].astype(F32)
        mix_ref[...] = (gate * _sigmoid(gate) * (xh * g_ref[0])).astype(BF)

    blk = lambda col0: pl.BlockSpec((c, LANES), lambda hp, i: (i, col0 + hp))
    tab = lambda a: pl.BlockSpec((2,) + a.shape[1:], lambda hp, i: (hp, 0, 0))
    return pl.pallas_call(
        body, name="retention_fwd", grid=(N_HEADS // 2, n_b),
        out_shape=(jax.ShapeDtypeStruct((t_len, GROUP_W), F32), jax.ShapeDtypeStruct((t_len, GROUP_W), BF),
                   jax.ShapeDtypeStruct((N_HEADS // 2, n_b, 2, HEAD_DIM, HEAD_DIM), F32)),
        in_specs=[blk(0), blk(0), blk(v_col), blk(g_col), pl.BlockSpec((1, 1, LANES), lambda hp, i: (hp, 0, 0)),
                  tab(wdec), tab(qdec), tab(kdec), tab(cdec)],
        out_specs=(blk(0), blk(0), pl.BlockSpec((1, 1, 2, HEAD_DIM, HEAD_DIM), lambda hp, i: (hp, i, 0, 0, 0))),
        scratch_shapes=[pltpu.VMEM((2, HEAD_DIM, HEAD_DIM), F32)],
        compiler_params=_cparams(("arbitrary", "arbitrary")),
    )(rq, rk, proj, proj, g_ret, wdec, qdec, kdec, cdec)


def _fox_fwd(q_aug, k_aug, proj, shards):
    t_len = q_aug.shape[0]
    tq = min(ATT_BLOCK, t_len)
    nsub = min(FWD_GROUP, t_len // tq)
    tg = nsub * tq
    n_q = t_len // tg
    v_col = 6 * GROUP_W // LANES
    tc = min(512, t_len)
    n_w = len(shards)
    n_steps = (N_HEADS // 2) * n_q

    def body(*refs):
        q_ref, k_ref, v_ref = refs[:3]
        o_ref, o32_ref, lse_ref = refs[3 + n_w:6 + n_w]
        vt = refs[6 + 2 * n_w]
        comm = (refs[3:3 + n_w], refs[6 + n_w:6 + 2 * n_w]) + tuple(refs[7 + 2 * n_w:])
        i = pl.program_id(1)
        step = pl.program_id(0) * n_q + i

        @pl.when(step == 0)
        def _():
            _gather_phase(0, *comm)

        @pl.when(step == (3 * n_steps) // 4)
        def _():
            _gather_phase(1, *comm)

        @pl.when(i == 0)
        def _():
            for c0 in range(0, t_len, tc):
                vt[:, c0:c0 + tc] = v_ref[c0:c0 + tc, :].T

        chains = [(u, hh) for u in range(nsub) for hh in range(2)]
        qs = {(u, hh): q_ref[u * tq:(u + 1) * tq, hh * LANES:(hh + 1) * LANES] for u, hh in chains}
        ones = jnp.ones((HEAD_DIM, tq), BF)

        def scores(j, which):
            k2 = k_ref[pl.ds(pl.multiple_of(j * tq, tq), tq), :]
            return {ch: _dot_nt(k2[:, ch[1] * LANES:(ch[1] + 1) * LANES], qs[ch]) for ch in which}

        def update(j, ss, carry, masked):
            v2 = vt[:, pl.ds(pl.multiple_of(j * tq, tq), tq)]
            ps, stats = {}, {}
            for ch in ss:
                m = carry[ch][0]
                s_t = ss[ch]
                if ch in masked:
                    krow = lax.broadcasted_iota(jnp.int32, (tq, tq), 0)
                    qcol = lax.broadcasted_iota(jnp.int32, (tq, tq), 1)
                    s_t = jnp.where(qcol >= krow, s_t, NEG)
                m_new = jnp.maximum(m, jnp.max(s_t, axis=0, keepdims=True))
                ps[ch] = jnp.exp2(s_t - m_new).astype(BF)
                stats[ch] = (m_new, jnp.exp2(m - m_new))
            out = dict(carry)
            for ch in ss:
                m_new, alpha = stats[ch]
                v_aug = jnp.concatenate([v2[ch[1] * HEAD_DIM:(ch[1] + 1) * HEAD_DIM, :], ones], axis=0)
                out[ch] = (m_new, carry[ch][1] * alpha + jnp.dot(v_aug, ps[ch], preferred_element_type=F32))
            return out

        def advance(j, state):
            ss, carry = state
            return scores(j + 1, chains), update(j, ss, carry, ())

        init = {ch: (jnp.full((1, tq), NEG, F32), jnp.zeros((LANES, tq), F32)) for ch in chains}
        first = nsub * i
        ss, carry = lax.fori_loop(0, first, advance, (scores(0, chains), init))
        carry = update(first, ss, carry, [(0, 0), (0, 1)])
        for u in range(1, nsub):
            rest = [(uu, hh) for uu in range(u, nsub) for hh in range(2)]
            carry = update(first + u, scores(first + u, rest), carry, [(u, 0), (u, 1)])
        for u in range(nsub):
            outs, lses = [], []
            for hh in range(2):
                m, acc = carry[u, hh]
                l = acc[HEAD_DIM:HEAD_DIM + 1, :]
                outs.append(acc[:HEAD_DIM, :] / l)
                lses.append(m + jnp.log2(l))
            o2 = jnp.concatenate(outs, axis=0).T
            o32_ref[u * tq:(u + 1) * tq, :] = o2
            o_ref[u * tq:(u + 1) * tq, :] = o2.astype(BF)
            lse_ref[0, :, u * tq:(u + 1) * tq] = jnp.concatenate(lses, axis=0)

        @pl.when(step == n_steps - 1)
        def _():
            _gather_phase(2, *comm)

    return pl.pallas_call(
        body, name="fox_fwd", grid=(N_HEADS // 2, n_q),
        out_shape=(jax.ShapeDtypeStruct((t_len, GROUP_W), BF), jax.ShapeDtypeStruct((t_len, GROUP_W), F32),
                   jax.ShapeDtypeStruct((N_HEADS // 2, 2, t_len), F32))
        + tuple(jax.ShapeDtypeStruct((4,) + s.shape, s.dtype) for s in shards),
        in_specs=[pl.BlockSpec((tg, 2 * LANES), lambda hp, i: (i, hp)),
                  pl.BlockSpec((t_len, 2 * LANES), lambda hp, i: (0, hp)),
                  pl.BlockSpec((t_len, LANES), lambda hp, i: (0, v_col + hp))] + [ANY] * n_w,
        out_specs=(pl.BlockSpec((tg, LANES), lambda hp, i: (i, hp)), pl.BlockSpec((tg, LANES), lambda hp, i: (i, hp)),
                   pl.BlockSpec((1, 2, tg), lambda hp, i: (hp, 0, i))) + (ANY,) * n_w,
        scratch_shapes=[pltpu.VMEM((LANES, t_len), BF)] + _gather_scratch(n_w),
        compiler_params=_cparams(("arbitrary", "arbitrary")),
    )(q_aug, k_aug, proj, *shards)


def _softmax_rows(s):
    p = jnp.exp(s - jnp.max(s, axis=-1, keepdims=True))
    return p / jnp.sum(p, axis=-1, keepdims=True)


def _attn_out_xattn_fwd(x, mix_r, mix_f, w_out, g_xattn, w_xq, g_xq, kn, v, w_xo):
    t_len = x.shape[0]
    tm = min(ROW_TILE, t_len)

    def body(x_ref, mr_ref, mf_ref, wo_ref, g_ref, wq_ref, gq_ref, kn_ref, v_ref, wxo_ref,
             h1_ref, hn_ref, qx_ref, o_ref, h2_ref):
        h1 = x_ref[...] + jnp.dot(mr_ref[...], wo_ref[:GROUP_W, :], preferred_element_type=F32) \
            + jnp.dot(mf_ref[...], wo_ref[GROUP_W:, :], preferred_element_type=F32)
        h1_ref[...] = h1
        hn = _rms_fwd(h1, g_ref[...]).astype(BF)
        hn_ref[...] = hn
        qx = jnp.dot(hn, wq_ref[...], preferred_element_type=F32).astype(BF)
        qx_ref[...] = qx
        sls = [slice(h * XHD, (h + 1) * XHD) for h in range(N_XH)]
        qns = [_rms_fwd(qx[:, sl].astype(F32), gq_ref[...]).astype(BF) for sl in sls]
        logits = [_dot_nt(qn, kn_ref[:, sl]) * (XHD ** -0.5) for qn, sl in zip(qns, sls)]
        ps = [_softmax_rows(s).astype(BF) for s in logits]
        for p, sl in zip(ps, sls):
            o_ref[:, sl] = jnp.dot(p, v_ref[:, sl], preferred_element_type=F32).astype(BF)
        h2_ref[...] = h1 + jnp.dot(o_ref[...], wxo_ref[...], preferred_element_type=F32)

    row_spec = lambda w: pl.BlockSpec((tm, w), lambda i: (i, 0))
    full = lambda a: pl.BlockSpec(a.shape, lambda i: (0,) * a.ndim)
    return pl.pallas_call(
        body, name="attn_out_xattn_fwd", grid=(t_len // tm,),
        out_shape=(jax.ShapeDtypeStruct((t_len, D_MODEL), F32), jax.ShapeDtypeStruct((t_len, D_MODEL), BF),
                   jax.ShapeDtypeStruct((t_len, D_MODEL), BF), jax.ShapeDtypeStruct((t_len, D_MODEL), BF),
                   jax.ShapeDtypeStruct((t_len, D_MODEL), F32)),
        in_specs=[row_spec(D_MODEL), row_spec(GROUP_W), row_spec(GROUP_W), full(w_out), full(g_xattn), full(w_xq), full(g_xq),
                  full(kn), full(v), full(w_xo)],
        out_specs=(row_spec(D_MODEL),) * 5,
        compiler_params=_cparams(("arbitrary",)),
    )(x, mix_r, mix_f, w_out, g_xattn, w_xq, g_xq, kn, v, w_xo)


def _ffn_loss_fwd(h2, g_ffn, w_gate, w_up, w_down, target):
    t_len = h2.shape[0]
    tm = min(ROW_TILE, t_len)

    def body(h2_ref, g_ref, wg_ref, wu_ref, wd_ref, tgt_ref, hn_ref, gate_ref, up_ref, act_ref, dh3_ref, loss_ref):
        @pl.when(pl.program_id(0) == 0)
        def _():
            loss_ref[...] = jnp.zeros_like(loss_ref)

        h2v = h2_ref[...]
        hn = _rms_fwd(h2v, g_ref[...]).astype(BF)
        hn_ref[...] = hn
        gate = _dot_nt(hn, wg_ref[...])
        up = _dot_nt(hn, wu_ref[...])
        gate_ref[...] = gate.astype(BF)
        up_ref[...] = up.astype(BF)
        act = (gate * _sigmoid(gate) * up).astype(BF)
        act_ref[...] = act
        diff = h2v + jnp.dot(act, wd_ref[...], preferred_element_type=F32) - tgt_ref[...]
        dh3_ref[...] = diff * (1.0 / D_MODEL)
        per_row = jnp.sum(diff * diff, axis=-1, keepdims=True) * (1.0 / D_MODEL)
        loss_ref[...] += 0.5 * jnp.sum(per_row, axis=0, keepdims=True)

    row_spec = lambda w: pl.BlockSpec((tm, w), lambda i: (i, 0))
    full = lambda a: pl.BlockSpec(a.shape, lambda i: (0,) * a.ndim, pipeline_mode=pl.Buffered(1))
    return pl.pallas_call(
        body, name="ffn_loss_fwd", grid=(t_len // tm,),
        out_shape=(jax.ShapeDtypeStruct((t_len, D_MODEL), BF), jax.ShapeDtypeStruct((t_len, D_FF), BF),
                   jax.ShapeDtypeStruct((t_len, D_FF), BF), jax.ShapeDtypeStruct((t_len, D_FF), BF),
                   jax.ShapeDtypeStruct((t_len, D_MODEL), F32), jax.ShapeDtypeStruct((8, LANES), F32)),
        in_specs=[row_spec(D_MODEL), full(g_ffn), full(w_gate), full(w_up), full(w_down), row_spec(D_MODEL)],
        out_specs=(row_spec(D_MODEL), row_spec(D_FF), row_spec(D_FF), row_spec(D_FF), row_spec(D_MODEL),
                   pl.BlockSpec((8, LANES), lambda i: (0, 0))),
        compiler_params=_cparams(("arbitrary",)),
    )(h2, g_ffn, w_gate, w_up, w_down, target)


def _ffn_bwd(dh3, gate, up, h2, g_ffn, w_gate, w_up, w_down):
    t_len = h2.shape[0]
    tm = min(FFN_BWD_TILE, t_len)

    def body(dh3_ref, gate_ref, up_ref, h2_ref, g_ref, wg_ref, wu_ref, wd_ref, dgate_ref, dup_ref, dh2_ref, dg_ref):
        @pl.when(pl.program_id(0) == 0)
        def _():
            dg_ref[...] = jnp.zeros_like(dg_ref)

        dh3v = dh3_ref[...]
        dact = _dot_nt(dh3v, wd_ref[...])
        g = gate_ref[...].astype(F32)
        sg = _sigmoid(g)
        dup = (dact * (g * sg)).astype(BF)
        dgate = (dact * up_ref[...].astype(F32) * (sg * (1.0 + g * (1.0 - sg)))).astype(BF)
        dup_ref[...] = dup
        dgate_ref[...] = dgate
        dhn = jnp.dot(dgate, wg_ref[...], preferred_element_type=F32) + jnp.dot(dup, wu_ref[...], preferred_element_type=F32)
        dx, dg = _rms_bwd(h2_ref[...], g_ref[...], dhn)
        dh2_ref[...] = dh3v + dx
        dg_ref[...] += dg

    row_spec = lambda w: pl.BlockSpec((tm, w), lambda i: (i, 0))
    full = lambda a: pl.BlockSpec(a.shape, lambda i: (0,) * a.ndim, pipeline_mode=pl.Buffered(1))
    return pl.pallas_call(
        body, name="ffn_bwd", grid=(t_len // tm,),
        out_shape=(jax.ShapeDtypeStruct((t_len, D_FF), BF), jax.ShapeDtypeStruct((t_len, D_FF), BF),
                   jax.ShapeDtypeStruct((t_len, D_MODEL), F32), jax.ShapeDtypeStruct((1, D_MODEL), F32)),
        in_specs=[row_spec(D_MODEL), row_spec(D_FF), row_spec(D_FF), row_spec(D_MODEL), full(g_ffn), full(w_gate), full(w_up),
                  full(w_down)],
        out_specs=(row_spec(D_FF), row_spec(D_FF), row_spec(D_MODEL), pl.BlockSpec((1, D_MODEL), lambda i: (0, 0))),
        compiler_params=_cparams(("arbitrary",)),
    )(dh3, gate, up, h2, g_ffn, w_gate, w_up, w_down)


def _attn_out_xattn_bwd(dh2, h1, qx, kn, v, w_xo, w_xq, w_out, g_xattn, g_xq):
    t_len = h1.shape[0]
    tm = min(ROW_TILE, t_len)
    m_tok = kn.shape[0]

    def body(dh2_ref, h1_ref, qx_ref, kn_ref, v_ref, wxo_ref, wq_ref, wo_ref, g_ref, gq_ref,
             dqx_ref, dh1_ref, dmr_ref, dmf_ref, dkn_ref, dv_ref, dg_ref, dgq_ref, dqx_scr):
        @pl.when(pl.program_id(0) == 0)
        def _():
            dkn_ref[...] = jnp.zeros_like(dkn_ref)
            dv_ref[...] = jnp.zeros_like(dv_ref)
            dg_ref[...] = jnp.zeros_like(dg_ref)
            dgq_ref[...] = jnp.zeros_like(dgq_ref)

        dh2v = dh2_ref[...]
        do = _dot_nt(dh2v, wxo_ref[...])
        gq = gq_ref[...]
        sls = [slice(h * XHD, (h + 1) * XHD) for h in range(N_XH)]
        qraws = [qx_ref[:, sl].astype(F32) for sl in sls]
        qns = [_rms_fwd(qraw, gq).astype(BF) for qraw in qraws]
        dohs = [do[:, sl].astype(BF) for sl in sls]
        logits = [_dot_nt(qn, kn_ref[:, sl]) * (XHD ** -0.5) for qn, sl in zip(qns, sls)]
        dps = [_dot_nt(doh, v_ref[:, sl]) for doh, sl in zip(dohs, sls)]
        ps = [_softmax_rows(s) for s in logits]
        dss = [(p * (dp - jnp.sum(dp * p, axis=-1, keepdims=True)) * (XHD ** -0.5)).astype(BF) for p, dp in zip(ps, dps)]
        dqns = []
        for h, sl in enumerate(sls):
            dv_ref[:, sl] += _dot_tn(ps[h], dohs[h])
            dqns.append(jnp.dot(dss[h], kn_ref[:, sl], preferred_element_type=F32))
            dkn_ref[:, sl] += _dot_tn(dss[h], qns[h])
        dgq = jnp.zeros((1, XHD), F32)
        for h, sl in enumerate(sls):
            dx, dg_h = _rms_bwd(qraws[h], gq, dqns[h])
            dgq = dgq + dg_h
            dqx_scr[:, sl] = dx.astype(BF)
        dgq_ref[...] += dgq
        dqx = dqx_scr[...]
        dqx_ref[...] = dqx
        dhn = _dot_nt(dqx, wq_ref[...])
        dx, dg = _rms_bwd(h1_ref[...], g_ref[...], dhn)
        dg_ref[...] += dg
        dh1 = dh2v + dx
        dh1_ref[...] = dh1
        dmix = _dot_nt(dh1, wo_ref[...])
        dmr_ref[...] = dmix[:, :GROUP_W]
        dmf_ref[...] = dmix[:, GROUP_W:].astype(BF)

    row_spec = lambda w: pl.BlockSpec((tm, w), lambda i: (i, 0))
    full = lambda a: pl.BlockSpec(a.shape, lambda i: (0,) * a.ndim)
    acc = lambda r, c: pl.BlockSpec((r, c), lambda i: (0, 0))
    return pl.pallas_call(
        body, name="attn_out_xattn_bwd", grid=(t_len // tm,),
        out_shape=(jax.ShapeDtypeStruct((t_len, D_MODEL), BF), jax.ShapeDtypeStruct((t_len, D_MODEL), F32),
                   jax.ShapeDtypeStruct((t_len, GROUP_W), F32), jax.ShapeDtypeStruct((t_len, GROUP_W), BF),
                   jax.ShapeDtypeStruct((m_tok, D_MODEL), F32), jax.ShapeDtypeStruct((m_tok, D_MODEL), F32),
                   jax.ShapeDtypeStruct((1, D_MODEL), F32), jax.ShapeDtypeStruct((1, XHD), F32)),
        in_specs=[row_spec(D_MODEL), row_spec(D_MODEL), row_spec(D_MODEL), full(kn), full(v), full(w_xo), full(w_xq), full(w_out),
                  full(g_xattn), full(g_xq)],
        out_specs=(row_spec(D_MODEL), row_spec(D_MODEL), row_spec(GROUP_W), row_spec(GROUP_W), acc(m_tok, D_MODEL),
                   acc(m_tok, D_MODEL), acc(1, D_MODEL), acc(1, XHD)),
        scratch_shapes=[pltpu.VMEM((tm, D_MODEL), BF)],
        compiler_params=_cparams(("arbitrary",)),
    )(dh2, h1, qx, kn, v, w_xo, w_xq, w_out, g_xattn, g_xq)


def _mem_kv_bwd(dkn, dv, kraw, mem, memn, g_mem, g_xk, w_xkv):
    m_tok = mem.shape[0]

    def body(dkn_ref, dv_ref, kraw_ref, mem_ref, memn_ref, gm_ref, gk_ref, w_ref, dw_ref, dgm_ref, dgk_ref, dkv_scr):
        gk = gk_ref[...]
        dgk = jnp.zeros((1, XHD), F32)
        for h in range(N_XH):
            sl = slice(h * XHD, (h + 1) * XHD)
            dx, dg_h = _rms_bwd(kraw_ref[:, sl], gk, dkn_ref[:, sl])
            dgk = dgk + dg_h
            dkv_scr[:, sl] = dx.astype(BF)
        dgk_ref[...] = dgk
        dkv_scr[:, D_MODEL:] = dv_ref[...].astype(BF)
        dkv = dkv_scr[...]
        dw_ref[...] = _dot_tn(memn_ref[...], dkv)
        dmemn = _dot_nt(dkv, w_ref[...])
        mem_v = mem_ref[...]
        r = lax.rsqrt(jnp.mean(mem_v * mem_v, axis=-1, keepdims=True) + EPS)
        dgm_ref[...] = jnp.sum(dmemn * mem_v * r, axis=0, keepdims=True)

    return pl.pallas_call(
        body, name="mem_kv_bwd",
        out_shape=(jax.ShapeDtypeStruct((D_MODEL, 2 * D_MODEL), F32), jax.ShapeDtypeStruct((1, D_MODEL), F32),
                   jax.ShapeDtypeStruct((1, XHD), F32)),
        in_specs=[VMEM_SPEC] * 8, out_specs=(VMEM_SPEC,) * 3,
        scratch_shapes=[pltpu.VMEM((m_tok, 2 * D_MODEL), BF)],
        compiler_params=_cparams(),
    )(dkn, dv, kraw, mem, memn, g_mem, g_xk, w_xkv)


def _fox_bwd(q_aug, k_aug, proj, dmf, o32, lse, sums):
    t_len = q_aug.shape[0]
    tb = min(ATT_BLOCK, t_len)
    n_b = t_len // tb
    nsub = 2 if n_b >= 2 else 1
    tg = nsub * tb
    n_g = t_len // tg
    v_col = 6 * GROUP_W // LANES
    n_w = len(sums)
    n_steps = (N_HEADS // 2) * n_g

    def body(*refs):
        k_ref, v_ref, q_ref, do_ref, o_ref, lse_ref = refs[:6]
        dq_ref, dk_ref, dv_ref, df_ref = refs[6 + n_w:10 + n_w]
        delta = refs[10 + 2 * n_w]
        comm = (refs[6:6 + n_w], refs[10 + n_w:10 + 2 * n_w]) + tuple(refs[11 + 2 * n_w:])
        j = pl.program_id(1)
        step = pl.program_id(0) * n_g + j

        @pl.when(step == 0)
        def _():
            _scatter_phase(0, *comm)

        @pl.when(j == 0)
        def _():
            dq_ref[...] = jnp.zeros_like(dq_ref)
            dd = do_ref[...].astype(F32) * o_ref[...]
            hrow = lax.broadcasted_iota(jnp.int32, (8, LANES), 0)
            lane = lax.broadcasted_iota(jnp.int32, (8, LANES), 1)
            ind = ((lane // HEAD_DIM) == hrow).astype(BF)
            delta[...] = _dot_nt_exact(ind, dd)

        k2, v2 = k_ref[...], v_ref[...]
        chains = [(u, hh) for u in range(nsub) for hh in range(2)]
        ks = {(u, hh): k2[u * tb:(u + 1) * tb, hh * LANES:(hh + 1) * LANES] for u, hh in chains}
        vs = {(u, hh): v2[u * tb:(u + 1) * tb, hh * HEAD_DIM:(hh + 1) * HEAD_DIM] for u, hh in chains}

        def block(i, carry, which, masked):
            rows = pl.ds(pl.multiple_of(i * tb, tb), tb)
            q2 = q_ref[rows, :]
            do2 = do_ref[rows, :]
            qs = [q2[:, hh * LANES:(hh + 1) * LANES] for hh in range(2)]
            dos = [do2[:, hh * HEAD_DIM:(hh + 1) * HEAD_DIM] for hh in range(2)]
            ss = {ch: _dot_nt(ks[ch], qs[ch[1]]) for ch in which}
            dps = {ch: _dot_nt(vs[ch], dos[ch[1]]) for ch in which}
            pts, dsts, dfs = {}, {}, {}
            for ch in which:
                hh = ch[1]
                s_t = ss[ch]
                if ch in masked:
                    krow = lax.broadcasted_iota(jnp.int32, (tb, tb), 0)
                    qcol = lax.broadcasted_iota(jnp.int32, (tb, tb), 1)
                    s_t = jnp.where(qcol >= krow, s_t, NEG)
                p_t = jnp.exp2(s_t - lse_ref[0, hh:hh + 1, rows])
                pts[ch] = p_t.astype(BF)
                ds_t = p_t * (dps[ch] - delta[hh:hh + 1, rows])
                dsts[ch] = ds_t.astype(BF)
                dfs[ch] = jnp.sum(ds_t, axis=-1, keepdims=True)
            out = dict(carry)
            for ch in which:
                dk, dv, df = carry[ch]
                dv = dv + jnp.dot(pts[ch], dos[ch[1]], preferred_element_type=F32)
                dk = dk + jnp.dot(dsts[ch], qs[ch[1]], preferred_element_type=F32)
                out[ch] = (dk, dv, df - dfs[ch])
            for hh in range(2):
                parts_dq = [_dot_tn(dsts[ch], ks[ch])[:, :HEAD_DIM] for ch in which if ch[1] == hh]
                dq_ref[rows, hh * HEAD_DIM:(hh + 1) * HEAD_DIM] += sum(parts_dq[1:], parts_dq[0])
            return out

        init = {ch: (jnp.zeros((tb, LANES), F32), jnp.zeros((tb, HEAD_DIM), F32), jnp.zeros((tb, 1), F32)) for ch in chains}
        first = nsub * j
        carry = block(first, init, [(0, 0), (0, 1)], [(0, 0), (0, 1)])
        if nsub == 2:
            carry = block(first + 1, carry, chains, [(1, 0), (1, 1)])
        carry = lax.fori_loop(first + nsub, n_b, lambda i, c: block(i, c, chains, ()), carry)
        for u in range(nsub):
            rs = slice(u * tb, (u + 1) * tb)
            dk_ref[rs, :] = jnp.concatenate([carry[u, hh][0][:, :HEAD_DIM] for hh in range(2)], axis=-1) * LN2
            dv_ref[rs, :] = jnp.concatenate([carry[u, hh][1] for hh in range(2)], axis=-1)
            df_ref[0, rs, :] = jnp.concatenate([carry[u, hh][2] for hh in range(2)], axis=-1)

        @pl.when(step == n_steps - 1)
        def _():
            _scatter_phase(1, *comm)

    blk = lambda w, col0: pl.BlockSpec((tg, w), lambda hp, j: (j, col0 + hp))
    whole = lambda w: pl.BlockSpec((t_len, w), lambda hp, j: (0, hp))
    rows2 = pl.BlockSpec((1, 2, t_len), lambda hp, j: (hp, 0, 0))
    cols2 = pl.BlockSpec((1, tg, 2), lambda hp, j: (hp, j, 0))
    return pl.pallas_call(
        body, name="fox_bwd", grid=(N_HEADS // 2, n_g),
        out_shape=(jax.ShapeDtypeStruct((t_len, GROUP_W), F32), jax.ShapeDtypeStruct((t_len, GROUP_W), F32),
                   jax.ShapeDtypeStruct((t_len, GROUP_W), F32), jax.ShapeDtypeStruct((N_HEADS // 2, t_len, 2), F32))
        + _scatter_out_shapes(sums),
        in_specs=[blk(2 * LANES, 0), blk(LANES, v_col), whole(2 * LANES), whole(LANES), whole(LANES), rows2] + [ANY] * n_w,
        out_specs=(whole(LANES), blk(LANES, 0), blk(LANES, 0), cols2) + (ANY,) * n_w,
        scratch_shapes=[pltpu.VMEM((8, t_len), F32)] + _scatter_scratch(n_w),
        compiler_params=_cparams(("arbitrary", "arbitrary")),
    )(k_aug, proj, q_aug, dmf, o32, lse, *sums)


def _retention_bwd(dmr, raw, proj, g_ret, rq, rk, states, tables, parts):
    t_len = rq.shape[0]
    c = min(RET_BLOCK, t_len)
    n_b = t_len // c
    wdec, qdec, kdec, cdec = tables
    v_col, g_col = 2 * GROUP_W // LANES, 3 * GROUP_W // LANES
    n_w = len(parts)
    n_steps = (N_HEADS // 2) * n_b

    def body(*refs):
        d_ref, raw_ref, rg_ref, g_ref, q_ref, k_ref, v_ref, st_ref, w_ref, wt_ref, qd_ref, kd_ref, cd_ref = refs[:13]
        dq_ref, dk_ref, dv_ref, drg_ref, dg_ref = refs[13 + n_w:18 + n_w]
        gstate = refs[18 + 2 * n_w]
        comm = (refs[13:13 + n_w], refs[18 + n_w:18 + 2 * n_w]) + tuple(refs[19 + 2 * n_w:])
        step = pl.program_id(0) * n_b + pl.program_id(1)

        @pl.when(step == 0)
        def _():
            _exchange_phase(0, *comm)

        @pl.when(pl.program_id(1) == 0)
        def _():
            gstate[...] = jnp.zeros_like(gstate)
            dg_ref[...] = jnp.zeros_like(dg_ref)

        d, raw_v, g = d_ref[...], raw_ref[...], g_ref[0]
        gate = rg_ref[...].astype(F32)
        xc = raw_v - _group_mean64(raw_v)
        r = lax.rsqrt(_group_mean64(xc * xc) + EPS)
        xh = xc * r
        sg = _sigmoid(gate)
        drg_ref[...] = d * (xh * g) * (sg * (1.0 + gate * (1.0 - sg)))
        dy = d * (gate * sg)
        dg_ref[0] += jnp.sum(dy * xh, axis=0, keepdims=True)
        dxh = dy * g
        do2 = r * (dxh - _group_mean64(dxh) - xh * _group_mean64(dxh * xh))
        q2, k2, v2 = q_ref[...], k_ref[...], v_ref[...]
        dqs, dks, dvs = [], [], []
        heads = [tuple(t[:, hh * HEAD_DIM:(hh + 1) * HEAD_DIM] for t in (q2, k2, v2, do2.astype(BF))) for hh in range(2)]
        firsts = [(_dot_nt(k, q) * wt_ref[hh], _dot_nt(do, v) * w_ref[hh], _dot_nt(v, do) * wt_ref[hh])
                  for hh, (q, k, v, do) in enumerate(heads)]
        for hh, (q, k, v, do) in enumerate(heads):
            a_t, dm, dm_t = firsts[hh]
            sp, gs = st_ref[0, 0, hh], gstate[hh]
            qd = q.astype(F32) * qd_ref[hh]
            kd = k.astype(F32) * kd_ref[hh]
            dqs.append(_dot(dm, k) + _dot_nt(do, sp) * qd_ref[hh])
            dks.append(_dot(dm_t, q) + _dot_nt(v, gs) * kd_ref[hh])
            dvs.append(_dot(a_t, do) + _dot(kd, gs))
            gstate[hh] = gs * cd_ref[hh] + _dot_tn(qd, do)
        dq_ref[...] = jnp.concatenate(dqs, axis=-1)
        dk_ref[...] = jnp.concatenate(dks, axis=-1)
        dv_ref[...] = jnp.concatenate(dvs, axis=-1)

        @pl.when(step == n_steps - 1)
        def _():
            _exchange_phase(1, *comm)

    blk = lambda col0: pl.BlockSpec((c, LANES), lambda hp, i: (n_b - 1 - i, col0 + hp))
    tab = lambda a: pl.BlockSpec((2,) + a.shape[1:], lambda hp, i: (hp, 0, 0))
    gspec = pl.BlockSpec((1, 1, LANES), lambda hp, i: (hp, 0, 0))
    return pl.pallas_call(
        body, name="retention_bwd", grid=(N_HEADS // 2, n_b),
        out_shape=(jax.ShapeDtypeStruct((t_len, GROUP_W), F32),) * 4 + (jax.ShapeDtypeStruct((N_HEADS // 2, 1, LANES), F32),)
        + _exchange_out_shapes(parts),
        in_specs=[blk(0), blk(0), blk(g_col), gspec, blk(0), blk(0), blk(v_col),
                  pl.BlockSpec((1, 1, 2, HEAD_DIM, HEAD_DIM), lambda hp, i: (hp, n_b - 1 - i, 0, 0, 0)),
                  tab(wdec), tab(wdec), tab(qdec), tab(kdec), tab(cdec)] + [ANY] * n_w,
        out_specs=(blk(0), blk(0), blk(0), blk(0), gspec) + (ANY,) * n_w,
        scratch_shapes=[pltpu.VMEM((2, HEAD_DIM, HEAD_DIM), F32)] + _exchange_scratch(n_w),
        compiler_params=_cparams(("arbitrary", "arbitrary")),
    )(dmr, raw, proj, g_ret, rq, rk, proj, states, wdec, jnp.transpose(wdec, (0, 2, 1)), qdec, kdec, cdec, *parts)


def _in_proj_bwd(x, g_mix, dh1, dq_r, dk_r, dv_r, drg, dq_f, dk_f, dv_f, df_col, proj, z, cos_t, sin_t, gq_t, gk_t, w_in_t):
    t_len = x.shape[0]
    tm = min(ROW_TILE, t_len)
    n_t = t_len // tm

    def body(x_ref, g_ref, dh1_ref, dqr_ref, dkr_ref, dvr_ref, drg_ref, dqf_ref, dkf_ref, dvf_ref, df_ref, fq_ref, fk_ref, z_ref,
             cos_ref, sin_ref, gq_ref, gk_ref, wm_ref, wf_ref,
             dproj_ref, dz_ref, dx_ref, dg_ref, dgq_ref, dgk_ref, db_ref, carry, gq_acc, gk_acc):
        i = pl.program_id(0)

        @pl.when(i == 0)
        def _():
            carry[...] = jnp.zeros_like(carry)
            gq_acc[...] = jnp.zeros_like(gq_acc)
            gk_acc[...] = jnp.zeros_like(gk_acc)
            dg_ref[...] = jnp.zeros_like(dg_ref)
            db_ref[...] = jnp.zeros_like(db_ref)

        c, s = cos_ref[...], sin_ref[...]
        gq, gk = gq_ref[...], gk_ref[...]
        dgq = jnp.zeros((1, LANES), F32)
        dgk = jnp.zeros((1, LANES), F32)
        for sl in _chunks(GROUP_W):
            dy = dqr_ref[:, sl] * 0.125
            dproj_ref[:, sl] = (dy * c + _swap32(dy * s)).astype(BF)
            dy = dkr_ref[:, sl]
            dproj_ref[:, GROUP_W + sl.start:GROUP_W + sl.stop] = (dy * c + _swap32(dy * s)).astype(BF)
            dproj_ref[:, 2 * GROUP_W + sl.start:2 * GROUP_W + sl.stop] = dvr_ref[:, sl].astype(BF)
            dproj_ref[:, 3 * GROUP_W + sl.start:3 * GROUP_W + sl.stop] = drg_ref[:, sl].astype(BF)
            for src, dsrc, gain, off in ((fq_ref, dqf_ref, gq, 4), (fk_ref, dkf_ref, gk, 5)):
                xr = src[:, sl].astype(F32)
                r = lax.rsqrt(_group_mean64(xr * xr) + EPS)
                xh = xr * r
                dy = dsrc[:, sl] * (0.125 if off == 4 else 1.0)
                dgs = jnp.sum(dy * xh, axis=0, keepdims=True)
                if off == 4:
                    dgq = dgq + dgs
                else:
                    dgk = dgk + dgs
                dxh = dy * gain
                dproj_ref[:, off * GROUP_W + sl.start:off * GROUP_W + sl.stop] = \
                    (r * (dxh - xh * _group_mean64(dxh * xh))).astype(BF)
            dproj_ref[:, 6 * GROUP_W + sl.start:6 * GROUP_W + sl.stop] = dvf_ref[:, sl].astype(BF)
        gq_acc[...] += dgq
        gk_acc[...] += dgk
        row = lax.broadcasted_iota(jnp.int32, (tm, tm), 0)
        col = lax.broadcasted_iota(jnp.int32, (tm, tm), 1)
        dlf = _dot_exact((col >= row).astype(BF), df_ref[...]) + carry[0:1, :]
        carry[...] = jnp.broadcast_to(dlf[0:1, :], carry.shape)
        lane = lax.broadcasted_iota(jnp.int32, (tm, LANES), 1)
        dz = jnp.where(lane < N_HEADS, dlf / (1.0 + jnp.exp(z_ref[...])), 0.0)
        db_ref[...] += jnp.sum(dz, axis=0, keepdims=True)
        dz_bf = dz.astype(BF)
        dz_ref[...] = dz_bf
        dn1 = jnp.dot(dz_bf, wf_ref[...], preferred_element_type=F32)
        for sec in range(MAIN_W // GROUP_W):
            sl = slice(sec * GROUP_W, (sec + 1) * GROUP_W)
            dn1 = dn1 + jnp.dot(dproj_ref[:, sl], wm_ref[sl, :], preferred_element_type=F32)
        dx, dg = _rms_bwd(x_ref[...], g_ref[...], dn1)
        dx_ref[...] = dh1_ref[...] + dx
        dg_ref[...] += dg

        @pl.when(i == n_t - 1)
        def _():
            dgq_ref[...] = gq_acc[:, :HEAD_DIM] + gq_acc[:, HEAD_DIM:]
            dgk_ref[...] = gk_acc[:, :HEAD_DIM] + gk_acc[:, HEAD_DIM:]

    row_spec = lambda w, col=0: pl.BlockSpec((tm, w), lambda i: (n_t - 1 - i, col))
    full = lambda a: pl.BlockSpec(a.shape, lambda i: (0,) * a.ndim)
    acc = lambda r, c: pl.BlockSpec((r, c), lambda i: (0, 0))
    return pl.pallas_call(
        body, name="in_proj_bwd", grid=(n_t,),
        out_shape=(jax.ShapeDtypeStruct((t_len, MAIN_W), BF), jax.ShapeDtypeStruct((t_len, LANES), BF),
                   jax.ShapeDtypeStruct((t_len, D_MODEL), F32), jax.ShapeDtypeStruct((1, D_MODEL), F32),
                   jax.ShapeDtypeStruct((1, HEAD_DIM), F32), jax.ShapeDtypeStruct((1, HEAD_DIM), F32),
                   jax.ShapeDtypeStruct((1, LANES), F32)),
        in_specs=[row_spec(D_MODEL), full(g_mix), row_spec(D_MODEL)] + [row_spec(GROUP_W)] * 7
        + [row_spec(LANES), row_spec(GROUP_W, 4), row_spec(GROUP_W, 5), row_spec(LANES), row_spec(LANES), row_spec(LANES),
           full(gq_t), full(gk_t), *_w_in_specs()],
        out_specs=(row_spec(MAIN_W), row_spec(LANES), row_spec(D_MODEL), acc(1, D_MODEL), acc(1, HEAD_DIM), acc(1, HEAD_DIM),
                   acc(1, LANES)),
        scratch_shapes=[pltpu.VMEM((8, LANES), F32), pltpu.VMEM((1, LANES), F32), pltpu.VMEM((1, LANES), F32)],
        compiler_params=_cparams(("arbitrary",)),
    )(x, g_mix, dh1, dq_r, dk_r, dv_r, drg, dq_f, dk_f, dv_f, df_col, proj, proj, z, cos_t, sin_t, gq_t, gk_t, w_in_t, w_in_t)


def _matmul_tn(a, b, name, bk=1024):
    t_len, m = a.shape
    n = b.shape[1]
    bm = m if m <= TN_MAX_ROWS else m // 2
    bk = min(bk, t_len)

    def body(a_ref, b_ref, o_ref):
        @pl.when(pl.program_id(1) == 0)
        def _():
            o_ref[...] = jnp.zeros_like(o_ref)

        o_ref[...] += _dot_tn(a_ref[...], b_ref[...])

    return pl.pallas_call(
        body, name=name, grid=(m // bm, t_len // bk),
        out_shape=jax.ShapeDtypeStruct((m, n), F32),
        in_specs=[pl.BlockSpec((bk, bm), lambda i, k: (k, i)), pl.BlockSpec((bk, n), lambda i, k: (k, 0))],
        out_specs=pl.BlockSpec((bm, n), lambda i, k: (i, 0)),
        compiler_params=_cparams(("arbitrary", "arbitrary")),
    )(a, b)


def _place():
    x, y, c = lax.axis_index("x"), lax.axis_index("y"), lax.axis_index("c")
    chips = [(1 - x, y), (x, 1 - y), (1 - x, 1 - y)]
    return x, y, c, chips


def _row_chunks(rows, limit):
    step = max(d for d in range(16, min(rows, limit) + 1, 16) if rows % d == 0)
    return [slice(i, i + step) for i in range(0, rows, step)]


ICI_CHUNK_ROWS = 256
D2D_CHUNK_ROWS = 256


def _gather_phase(phase, ins, outs, send_sems, recv_sems):
    x, y, c, chips = _place()
    me_chip = 2 * x + y
    sibling = (x, y, 1 - c)

    def copy(w, k, slot, half, to, rows=slice(None), src=None):
        dst = outs[w].at[slot, half, rows]
        return pltpu.make_async_remote_copy(src_ref=dst if src is None else src, dst_ref=dst,
                                            send_sem=send_sems.at[w, k], recv_sem=recv_sems.at[w, k],
                                            device_id=to, device_id_type=MESH)

    for w in range(len(ins)):
        for j, (px, py) in enumerate(chips):
            if phase == 0:
                for rows in _row_chunks(ins[w].shape[1], ICI_CHUNK_ROWS):
                    copy(w, j, me_chip, c, (px, py, c), rows, src=ins[w].at[c, rows]).start()
            elif phase == 1:
                copy(w, j, 2 * px + py, c, (x, y, c)).wait_recv()
                for rows in _row_chunks(ins[w].shape[1], D2D_CHUNK_ROWS):
                    copy(w, 3 + j, 2 * px + py, c, sibling, rows).start()
            else:
                copy(w, 3 + j, 2 * px + py, 1 - c, (x, y, c)).wait_recv()
                copy(w, j, me_chip, c, (px, py, c), src=ins[w].at[c]).wait_send()
                copy(w, 3 + j, 2 * px + py, c, sibling).wait_send()


def _gather_scratch(n_w):
    return [pltpu.SemaphoreType.DMA((n_w, 6)), pltpu.SemaphoreType.DMA((n_w, 6))]


def _all_gather_weights(shards):
    n_w = len(shards)

    def body(*refs):
        for phase in range(3):
            _gather_phase(phase, refs[:n_w], refs[n_w:2 * n_w], *refs[2 * n_w:])

    return pl.pallas_call(
        body, name="all_gather_weights",
        out_shape=tuple(jax.ShapeDtypeStruct((4,) + s.shape, s.dtype) for s in shards),
        in_specs=[ANY] * n_w, out_specs=(ANY,) * n_w, scratch_shapes=_gather_scratch(n_w),
    )(*shards)


def _exchange_phase(phase, ins, theirs, send_sems, recv_sems):
    x, y, c, _ = _place()

    def remote(w, k=slice(None), rows=slice(None)):
        return pltpu.make_async_remote_copy(src_ref=ins[w].at[k, 1 - c, rows], dst_ref=theirs[w].at[k, rows],
                                            send_sem=send_sems.at[w], recv_sem=recv_sems.at[w], device_id=(x, y, 1 - c),
                                            device_id_type=MESH)

    for w in range(len(ins)):
        if phase == 0:
            for k in range(4):
                for rows in _row_chunks(ins[w].shape[2], D2D_CHUNK_ROWS):
                    remote(w, k, rows).start()
        else:
            remote(w).wait()


def _exchange_scratch(n_w):
    return [pltpu.SemaphoreType.DMA((n_w,)), pltpu.SemaphoreType.DMA((n_w,))]


def _exchange_out_shapes(grads):
    return tuple(jax.ShapeDtypeStruct((4,) + g.shape[2:], g.dtype) for g in grads)


def _add_pairs(part, theirs, name, halves):
    _, _, r, c = part.shape
    rb = 64 if r % 64 == 0 else r
    n_w = len(halves)
    n_steps = r // rb

    def body(*refs):
        a_ref, b_ref = refs[:2]
        own_ref, ob_ref = refs[2 + n_w:4 + n_w]
        comm = (refs[2:2 + n_w], refs[4 + n_w:4 + 2 * n_w]) + tuple(refs[4 + 2 * n_w:])
        step = pl.program_id(0)

        @pl.when(step == 0)
        def _():
            _share_phase(0, *comm)

        my_chip = 2 * lax.axis_index("x") + lax.axis_index("y")
        ob_ref[...] = (a_ref[...] + b_ref[...]).astype(BF)
        own_ref[...] = a_ref[my_chip] + b_ref[my_chip]

        @pl.when(step == n_steps - 1)
        def _():
            _share_phase(1, *comm)

    spec = pl.BlockSpec((4, rb, c), lambda i: (0, i, 0))
    flat = pl.pallas_call(
        body, name=name, grid=(n_steps,),
        out_shape=(jax.ShapeDtypeStruct((r, c), F32), jax.ShapeDtypeStruct((4, r, c), BF)) + _share_out_shapes(halves),
        in_specs=[pl.BlockSpec((4, None, rb, c), lambda i: (0, lax.axis_index("c"), i, 0)), spec] + [ANY] * n_w,
        out_specs=(pl.BlockSpec((rb, c), lambda i: (i, 0)), spec) + (ANY,) * n_w,
        scratch_shapes=_share_scratch(n_w), compiler_params=_cparams(("arbitrary",)),
    )(part, theirs, *halves)
    return (flat[0], flat[1]), list(flat[2:])


def _scatter_phase(phase, bfs, got, send_sems, recv_sems):
    x, y, c, chips = _place()

    def remote(w, j, px, py, rows=slice(None)):
        return pltpu.make_async_remote_copy(src_ref=bfs[w].at[2 * px + py, rows], dst_ref=got[w].at[j, rows],
                                            send_sem=send_sems.at[w, j], recv_sem=recv_sems.at[w, j], device_id=(px, py, c),
                                            device_id_type=MESH)

    for w in range(len(bfs)):
        for j, (px, py) in enumerate(chips):
            if phase == 0:
                for rows in _row_chunks(bfs[w].shape[1], ICI_CHUNK_ROWS):
                    remote(w, j, px, py, rows).start()
            else:
                remote(w, j, px, py).wait()


def _scatter_scratch(n_w):
    return [pltpu.SemaphoreType.DMA((n_w, 3)), pltpu.SemaphoreType.DMA((n_w, 3))]


def _scatter_out_shapes(sums_bf16):
    return tuple(jax.ShapeDtypeStruct((3,) + s.shape[1:], BF) for s in sums_bf16)


def _add_received(own, got, name):
    r, c = own.shape
    rb = 128 if r % 128 == 0 else r

    def body(o_ref, g_ref, out_ref):
        out_ref[...] = ((o_ref[...] + g_ref[0].astype(F32)) + g_ref[1].astype(F32)) + g_ref[2].astype(F32)

    return pl.pallas_call(
        body, name=name, grid=(r // rb,), out_shape=jax.ShapeDtypeStruct((r, c), F32),
        in_specs=[pl.BlockSpec((rb, c), lambda i: (i, 0)), pl.BlockSpec((3, rb, c), lambda i: (0, i, 0))],
        out_specs=pl.BlockSpec((rb, c), lambda i: (i, 0)), compiler_params=_cparams(("arbitrary",)),
    )(own, got)


def _share_phase(phase, ins, outs, send_sems, recv_sems):
    x, y, c, _ = _place()

    def remote(w, rows=slice(None)):
        return pltpu.make_async_remote_copy(src_ref=ins[w].at[rows], dst_ref=outs[w].at[c, rows], send_sem=send_sems.at[w],
                                            recv_sem=recv_sems.at[w], device_id=(x, y, 1 - c), device_id_type=MESH)

    for w in range(len(ins)):
        if phase == 0:
            for rows in _row_chunks(ins[w].shape[0], D2D_CHUNK_ROWS):
                remote(w, rows).start()
        else:
            remote(w).wait()


def _share_scratch(n_w):
    return [pltpu.SemaphoreType.DMA((n_w,)), pltpu.SemaphoreType.DMA((n_w,))]


def _share_out_shapes(halves):
    return tuple(jax.ShapeDtypeStruct((2,) + h.shape, h.dtype) for h in halves)


def _share_with_sibling(halves):
    n_w = len(halves)

    def body(*refs):
        for phase in range(2):
            _share_phase(phase, refs[:n_w], refs[n_w:2 * n_w], *refs[2 * n_w:])

    return pl.pallas_call(
        body, name="share_with_sibling", out_shape=_share_out_shapes(halves),
        in_specs=[ANY] * n_w, out_specs=(ANY,) * n_w, scratch_shapes=_share_scratch(n_w),
    )(*halves)


def _small_phase(phase, p_ref, out_ref, slots, send_sems, recv_sems):
    x, y, cc, _ = _place()
    me = 4 * x + 2 * y + cc
    copies = []
    for k in range(1, 8):
        dx, dy, dc = (k >> 2) & 1, (k >> 1) & 1, k & 1
        to = (1 - x if dx else x, 1 - y if dy else y, 1 - cc if dc else cc)
        copies.append(pltpu.make_async_remote_copy(src_ref=p_ref, dst_ref=slots.at[me], send_sem=send_sems.at[k - 1],
                                                   recv_sem=recv_sems.at[k - 1], device_id=to, device_id_type=MESH))
    if phase == 0:
        slots[me] = p_ref[...]
        for cp in copies:
            cp.start()
    else:
        for cp in copies:
            cp.wait()
        total = slots[0]
        for d in range(1, 8):
            total = total + slots[d]
        out_ref[...] = total


def _adamw_update(w_ref, g_ref, m_ref, v_ref, d_ref, nm_ref, nv_ref):
    gv = g_ref[...]
    nm = ADAM_B1 * m_ref[...] + (1.0 - ADAM_B1) * gv
    nv = ADAM_B2 * v_ref[...] + (1.0 - ADAM_B2) * (gv * gv)
    nm_ref[...] = nm
    nv_ref[...] = nv
    m_hat = nm / (1.0 - ADAM_B1 ** ADAM_STEP)
    v_hat = nv / (1.0 - ADAM_B2 ** ADAM_STEP)
    d_ref[...] = -ADAM_LR * (m_hat / (jnp.sqrt(v_hat) + ADAM_EPS) + ADAM_WD * w_ref[...])


def _adamw_many(ws, gs, ms, vs, sums, pack):
    n_a, n_w = len(ws), len(sums)
    n_steps = ADAM_STEPS
    specs = [pl.BlockSpec((w.shape[0] // n_steps, w.shape[1]), lambda i: (i, 0)) for w in ws]
    pack_spec = pl.BlockSpec(pack.shape, lambda i: (0, 0))

    def body(*refs):
        ins = refs[:4 * n_a]
        p_ref = refs[4 * n_a + n_w]
        first_out = 4 * n_a + n_w + 1
        outs = refs[first_out:first_out + 3 * n_a]
        total_ref = refs[first_out + 3 * n_a + n_w]
        scratch = refs[first_out + 3 * n_a + n_w + 1:]
        scatter = (refs[4 * n_a:4 * n_a + n_w], refs[first_out + 3 * n_a:first_out + 3 * n_a + n_w]) + tuple(scratch[:2])
        small = (p_ref, total_ref) + tuple(scratch[2:])
        step = pl.program_id(0)

        @pl.when(step == 0)
        def _():
            _scatter_phase(0, *scatter)
            _small_phase(0, *small)

        for a in range(n_a):
            _adamw_update(*(ins[k * n_a + a] for k in range(4)), *(outs[3 * a + k] for k in range(3)))

        @pl.when(step == n_steps - 1)
        def _():
            _scatter_phase(1, *scatter)
            _small_phase(1, *small)

    flat = pl.pallas_call(
        body, name="adamw_late", grid=(n_steps,),
        out_shape=tuple(jax.ShapeDtypeStruct(w.shape, F32) for w in ws for _ in range(3)) + _scatter_out_shapes(sums)
        + (jax.ShapeDtypeStruct(pack.shape, F32),),
        in_specs=specs * 4 + [ANY] * n_w + [pack_spec],
        out_specs=tuple(s for s in specs for _ in range(3)) + (ANY,) * n_w + (pack_spec,),
        scratch_shapes=_scatter_scratch(n_w) + [pltpu.VMEM((8,) + pack.shape, F32), pltpu.SemaphoreType.DMA((7,)),
                                                pltpu.SemaphoreType.DMA((7,))],
        compiler_params=_cparams(("arbitrary",)),
    )(*ws, *gs, *ms, *vs, *sums, pack)
    return [tuple(flat[3 * a:3 * a + 3]) for a in range(n_a)] + list(flat[3 * n_a:])


def _adamw(w, g, m, v, name):
    r, c = w.shape
    rb, cb = (128, c) if r % 128 == 0 else (r, LANES if (r % 8 and c % LANES == 0) else c)

    def body(*refs):
        _adamw_update(*refs)

    spec = pl.BlockSpec((rb, cb), lambda i, j: (i, j))
    return pl.pallas_call(
        body, name=name, grid=(r // rb, c // cb), out_shape=(jax.ShapeDtypeStruct((r, c), F32),) * 3,
        in_specs=[spec] * 4, out_specs=(spec,) * 3, compiler_params=_cparams(("arbitrary", "arbitrary")),
    )(w, g, m, v)


def _rope_tables(t_len):
    inv_freq = ROPE_BASE ** (-jnp.arange(0, HEAD_DIM, 2, dtype=F32) / HEAD_DIM)
    ang = jnp.arange(t_len, dtype=F32)[:, None] * inv_freq[None, :]
    cos, sin = jnp.cos(ang), jnp.sin(ang)
    cos_t = jnp.concatenate([cos, cos, cos, cos], axis=-1)
    sin_t = jnp.concatenate([-sin, sin, -sin, sin], axis=-1)
    return cos_t, sin_t


def _cols_to_shards(dw):
    r, n = dw.shape
    return jnp.transpose(dw.reshape(2, r // 2, 4, n // 4), (2, 0, 1, 3))


def _rows_to_shards(dw):
    r, n = dw.shape
    rows = r // 4
    if rows % SUBLANES == 0:
        padded = _pad_rows(dw.reshape(4, rows, n))
    else:
        window = rows + SUBLANES - rows % SUBLANES
        padded = _pad_rows(jnp.stack([dw[rows * k // SUBLANES * SUBLANES:][:window] for k in range(4)]))
    return padded.reshape(4, 2, padded.shape[1] // 2, n)


def _shard_row_offset(rows):
    return (rows * (2 * lax.axis_index("x") + lax.axis_index("y"))) % SUBLANES


def _pad_lanes(a):
    extra = -a.shape[-1] % LANES
    return a if extra == 0 else jnp.pad(a, [(0, 0)] * (a.ndim - 1) + [(0, extra)])


def _pad_rows(a):
    rows = a.shape[-2]
    extra = 0 if rows % SHARD_ROW_ALIGN == 0 else -rows % SHARD_ROW_PAD
    return a if extra == 0 else jnp.pad(a, [(0, 0)] * (a.ndim - 2) + [(0, extra), (0, 0)])


def _pad_row(a, width=D_MODEL):
    a = a.reshape(1, -1)
    return jnp.pad(a, ((0, 0), (0, width - a.shape[1])))


def kernel(x, mem, g_mix, w_in, b_forget, g_ret_out, g_fox_q, g_fox_k, w_out, g_xattn, w_xq, w_xkv, g_mem, g_xq, g_xk, w_xo, g_ffn, w_gate, w_up, w_down, loss_target, m_g_mix, m_w_in, m_b_forget, m_g_ret_out, m_g_fox_q, m_g_fox_k, m_w_out, m_g_xattn, m_w_xq, m_w_xkv, m_g_mem, m_g_xq, m_g_xk, m_w_xo, m_g_ffn, m_w_gate, m_w_up, m_w_down, v_g_mix, v_w_in, v_b_forget, v_g_ret_out, v_g_fox_q, v_g_fox_k, v_w_out, v_g_xattn, v_w_xq, v_w_xkv, v_g_mem, v_g_xq, v_g_xk, v_w_xo, v_g_ffn, v_w_gate, v_w_up, v_w_down):
    big = {"w_in": (w_in, m_w_in, v_w_in), "w_out": (w_out, m_w_out, v_w_out), "w_xq": (w_xq, m_w_xq, v_w_xq),
           "w_xkv": (w_xkv, m_w_xkv, v_w_xkv), "w_xo": (w_xo, m_w_xo, v_w_xo), "w_gate": (w_gate, m_w_gate, v_w_gate),
           "w_up": (w_up, m_w_up, v_w_up), "w_down": (w_down, m_w_down, v_w_down)}
    for n in TRANSPOSED:
        big[n] = tuple(jnp.swapaxes(a, 1, 2) for a in big[n])
    shards = {}
    for n in big:
        w = _pad_rows(_pad_lanes(big[n][0][0].astype(BF)))
        shards[n] = w.reshape(2, w.shape[0] // 2, w.shape[1])
    sizes = {n: big[n][0].shape[1:] for n in big}
    w_in_full = _assemble_weight("w_in", _all_gather_weights([shards["w_in"]])[0], shards["w_in"], sizes["w_in"])
    small_w ={"g_mix": g_mix, "b_forget": b_forget, "g_ret_out": g_ret_out, "g_fox_q": g_fox_q, "g_fox_k": g_fox_k,
               "g_xattn": g_xattn, "g_mem": g_mem, "g_xq": g_xq, "g_xk": g_xk, "g_ffn": g_ffn}
    m_small = {"g_mix": m_g_mix, "b_forget": m_b_forget, "g_ret_out": m_g_ret_out, "g_fox_q": m_g_fox_q, "g_fox_k": m_g_fox_k,
               "g_xattn": m_g_xattn, "g_mem": m_g_mem, "g_xq": m_g_xq, "g_xk": m_g_xk, "g_ffn": m_g_ffn}
    v_small = {"g_mix": v_g_mix, "b_forget": v_b_forget, "g_ret_out": v_g_ret_out, "g_fox_q": v_g_fox_q, "g_fox_k": v_g_fox_k,
               "g_xattn": v_g_xattn, "g_mem": v_g_mem, "g_xq": v_g_xq, "g_xk": v_g_xk, "g_ffn": v_g_ffn}
    loss_part, grad_x, sums, got, in_parts, small_g = _local_step(x[0], mem[0], loss_target[0], w_in_full, shards, sizes, small_w)
    return _reduce_and_update(big, sums, got, in_parts, small_w, small_g, loss_part, grad_x, m_small, v_small)


def _assemble_weight(name, gathered, own, size):
    rows, width = size
    my_chip = 2 * lax.axis_index("x") + lax.axis_index("y")
    g = lax.dynamic_update_slice(gathered, own[None], (my_chip, 0, 0, 0))
    g = g.reshape(4, 2 * g.shape[2], g.shape[3])[:, :rows, :width]
    return jnp.transpose(g, (1, 0, 2)).reshape(rows, 4 * width) if name in COL_SHARDED else g.reshape(4 * rows, width)


def _shard_parts(names, dw):
    return [_pad_lanes(_cols_to_shards(dw[n]) if n in COL_SHARDED else _rows_to_shards(dw[n])) for n in names]


def _add_pairs_many(parts, theirs, name):
    n_a = len(parts)
    n_steps = min(p.shape[2] for p in parts) // 32
    rb = [p.shape[2] // n_steps for p in parts]
    part_specs = [pl.BlockSpec((4, None, r, p.shape[3]), lambda i: (0, lax.axis_index("c"), i, 0)) for p, r in zip(parts, rb)]
    quad_specs = [pl.BlockSpec((4, r, p.shape[3]), lambda i: (0, i, 0)) for p, r in zip(parts, rb)]
    own_specs = [pl.BlockSpec((r, p.shape[3]), lambda i: (i, 0)) for p, r in zip(parts, rb)]

    def body(*refs):
        my_chip = 2 * lax.axis_index("x") + lax.axis_index("y")
        for a in range(n_a):
            a_ref, b_ref, own_ref, ob_ref = refs[a], refs[n_a + a], refs[2 * n_a + a], refs[3 * n_a + a]
            ob_ref[...] = (a_ref[...] + b_ref[...]).astype(BF)
            own_ref[...] = a_ref[my_chip] + b_ref[my_chip]

    flat = pl.pallas_call(
        body, name=name, grid=(n_steps,),
        out_shape=tuple(jax.ShapeDtypeStruct(p.shape[2:], F32) for p in parts)
        + tuple(jax.ShapeDtypeStruct((4,) + p.shape[2:], BF) for p in parts),
        in_specs=part_specs + quad_specs, out_specs=tuple(own_specs) + tuple(quad_specs),
        compiler_params=_cparams(("arbitrary",)),
    )(*parts, *theirs)
    return [(flat[a], flat[n_a + a]) for a in range(n_a)]


def _core_sums(parts, theirs):
    out = [None] * len(parts)
    for tag, pick in (("a", lambda p: p.shape[2] % LANES == 0), ("b", lambda p: p.shape[2] % LANES != 0)):
        idx = [i for i, p in enumerate(parts) if pick(p)]
        for i, res in zip(idx, _add_pairs_many([parts[i] for i in idx], [theirs[i] for i in idx], f"core_sum_late_{tag}")):
            out[i] = res
    return out


def _local_step(xs, mems, tgt, w_in_full, shards, sizes, small_w):
    g_mix, b_forget, g_ret_out, g_fox_q, g_fox_k = (small_w[n] for n in ("g_mix", "b_forget", "g_ret_out", "g_fox_q", "g_fox_k"))
    g_xattn, g_mem, g_xq, g_xk, g_ffn = (small_w[n] for n in ("g_xattn", "g_mem", "g_xq", "g_xk", "g_ffn"))
    w_in_t = jnp.pad(w_in_full, ((0, MAIN_W + LANES - IN_W), (0, 0)))
    t_len = xs.shape[0]
    cos_t, sin_t = _rope_tables(t_len)
    tables = _decay_tables(min(RET_BLOCK, t_len))
    gq_t = jnp.concatenate([g_fox_q, g_fox_q], axis=-1)
    gk_t = jnp.concatenate([g_fox_k, g_fox_k], axis=-1)
    b_pad = _pad_row(b_forget, LANES)
    g_ret = g_ret_out.reshape(N_HEADS // 2, 1, LANES)

    n1, proj, rq, rk, q_aug, k_aug, z = _in_proj_fwd(xs, g_mix, w_in_t, b_pad, cos_t, sin_t, gq_t, gk_t)
    raw, mix_r, states = _retention_fwd(rq, rk, proj, g_ret, tables)
    mix_f, o32, lse, *gathered = _fox_fwd(q_aug, k_aug, proj, [shards[n] for n in LATE])
    full = {n: _assemble_weight(n, g, shards[n], sizes[n]) for n, g in zip(LATE, gathered)}
    memn, kraw, kn, vmem = _mem_kv_fwd(mems, g_mem, full["w_xkv"], g_xk)
    h1, hn2, qx, o_x, h2 = _attn_out_xattn_fwd(xs, mix_r, mix_f, full["w_out"], g_xattn, full["w_xq"], g_xq, kn, vmem, full["w_xo"])
    hn3, gate, up, act, dh3, loss_part = _ffn_loss_fwd(h2, g_ffn, full["w_gate"], full["w_up"], full["w_down"], tgt)

    dgate, dup, dh2, dg_ffn = _ffn_bwd(dh3, gate, up, h2, g_ffn, full["w_gate"], full["w_up"], full["w_down"])
    dqx, dh1, dmr, dmf, dkn, dvm, dg_xattn, dg_xq = _attn_out_xattn_bwd(dh2, h1, qx, kn, vmem, full["w_xo"], full["w_xq"],
                                                                      full["w_out"], g_xattn, g_xq)
    dw_xkv, dg_mem, dg_xk = _mem_kv_bwd(dkn, dvm, kraw, mems, memn, g_mem, g_xk, full["w_xkv"])
    dw_gu = _matmul_tn_pair(dgate, dup, hn3, "dw_gate_up")
    dw = {
        "w_out": _matmul_tn_pair(mix_r, mix_f, dh1, "dw_out").reshape(D_MODEL, D_MODEL),
        "w_xq": _matmul_tn(hn2, dqx, "dw_xq"),
        "w_xkv": dw_xkv,
        "w_xo": _matmul_tn(o_x, dh2, "dw_xo"),
        "w_gate": dw_gu[0],
        "w_up": dw_gu[1],
        "w_down": _matmul_tn(act, dh3, "dw_down"),
    }
    late_parts = _shard_parts(LATE, dw)
    dq_r, dk_r, dv_r, drg, dg_ret, *late_theirs = _retention_bwd(dmr, raw, proj, g_ret, rq, rk, states, tables, late_parts)
    late_sums = _core_sums(late_parts, late_theirs)
    dq_f, dk_f, dv_f, df, *late_got = _fox_bwd(q_aug, k_aug, proj, dmf, o32, lse, [s[1] for s in late_sums])
    df_col = jnp.pad(jnp.transpose(df, (1, 0, 2)).reshape(t_len, N_HEADS), ((0, 0), (0, LANES - N_HEADS)))
    dproj, dz, grad_x, dg_mix, dg_fq, dg_fk, db = _in_proj_bwd(xs, g_mix, dh1, dq_r, dk_r, dv_r, drg, dq_f, dk_f, dv_f, df_col,
                                                              proj, z, cos_t, sin_t, gq_t, gk_t, w_in_t)

    dw_in = jnp.concatenate([_matmul_tn(dproj, n1, "dw_in_main"), _matmul_tn(dz, n1, "dw_in_ff")[:IN_W - MAIN_W]], axis=0)
    in_parts = _shard_parts(("w_in",), {"w_in": dw_in})
    sums = {n: s[0] for n, s in zip(LATE, late_sums)}
    got = dict(zip(LATE, late_got))
    small_g = {"g_mix": dg_mix, "b_forget": db[:, :N_HEADS], "g_ret_out": dg_ret, "g_fox_q": dg_fq, "g_fox_k": dg_fk,
               "g_xattn": dg_xattn, "g_mem": dg_mem, "g_xq": dg_xq, "g_xk": dg_xk, "g_ffn": dg_ffn}
    return loss_part, grad_x, sums, got, in_parts, small_g


def _add_received_many(owns, gots, parts):
    n_a, n_w = len(owns), len(parts)
    n_steps = CHIP_SUM_STEPS
    own_specs = [pl.BlockSpec((o.shape[0] // n_steps, o.shape[1]), lambda i: (i, 0)) for o in owns]
    got_specs = [pl.BlockSpec((3, o.shape[0] // n_steps, o.shape[1]), lambda i: (0, i, 0)) for o in owns]

    def body(*refs):
        first_out = 2 * n_a + n_w
        comm = (refs[2 * n_a:first_out], refs[first_out + n_a:first_out + n_a + n_w]) + tuple(refs[first_out + n_a + n_w:])
        step = pl.program_id(0)

        @pl.when(step == 0)
        def _():
            _exchange_phase(0, *comm)

        for a in range(n_a):
            o_ref, g_ref, out_ref = refs[a], refs[n_a + a], refs[first_out + a]
            out_ref[...] = ((o_ref[...] + g_ref[0].astype(F32)) + g_ref[1].astype(F32)) + g_ref[2].astype(F32)

        @pl.when(step == n_steps - 1)
        def _():
            _exchange_phase(1, *comm)

    flat = pl.pallas_call(
        body, name="chip_sum_late", grid=(n_steps,),
        out_shape=tuple(jax.ShapeDtypeStruct(o.shape, F32) for o in owns) + _exchange_out_shapes(parts),
        in_specs=own_specs + got_specs + [ANY] * n_w, out_specs=tuple(own_specs) + (ANY,) * n_w,
        scratch_shapes=_exchange_scratch(n_w), compiler_params=_cparams(("arbitrary",)),
    )(*owns, *gots, *parts)
    return list(flat[:n_a]), list(flat[n_a:])


def _final_grads(names, big, finals, shared):
    my_core = lax.axis_index("c")
    out = {}
    for n, s, fin in zip(names, shared, finals):
        s = lax.dynamic_update_slice(s, fin[None], (my_core, 0, 0))
        s = s.reshape(2 * s.shape[1], s.shape[2])
        rows, width = big[n][0].shape[1:]
        out[n] = s[:rows, :width] if rows % SUBLANES == 0 else lax.dynamic_slice(s, (_shard_row_offset(rows), 0), (rows, width))
    return out


def _reduce_and_update(big, sums, got, in_parts, small_w, small_g, loss_part, grad_x, m_small, v_small):
    small_names = list(small_w)
    pad_rows = SMALL_ROWS - len(small_names) - 1
    stack = lambda d: jnp.concatenate([_pad_row(d[n]) for n in small_names] + [jnp.zeros((pad_rows + 1, D_MODEL), F32)], axis=0)
    g_pack = jnp.concatenate([_pad_row(small_g[n]) for n in small_names] + [_pad_row(loss_part[0:1, 0:1])]
                             + [jnp.zeros((pad_rows, D_MODEL), F32)], axis=0)
    late_finals, in_theirs = _add_received_many([sums[n] for n in LATE], [got[n] for n in LATE], in_parts)
    (in_own, in_bf), late_shared = _add_pairs(in_parts[0], in_theirs[0], "core_sum_w_in", late_finals)
    grads = _final_grads(LATE, big, late_finals, late_shared)
    *late_updates, in_got, g_tot = _adamw_many([big[n][0][0] for n in LATE], [grads[n] for n in LATE], [big[n][1][0] for n in LATE],
                                               [big[n][2][0] for n in LATE], [in_bf], g_pack)
    updates = dict(zip(LATE, late_updates))
    in_final = [_add_received(in_own, in_got, "chip_sum_w_in")]
    grads.update(_final_grads(("w_in",), big, in_final, _share_with_sibling(in_final)))
    updates["w_in"] = _adamw(big["w_in"][0][0], grads["w_in"], big["w_in"][1][0], big["w_in"][2][0], "adamw_w_in")
    deltas, new_m, new_v = {}, {}, {}
    for n in big:
        restore = (lambda a: jnp.swapaxes(a[None], 1, 2)) if n in TRANSPOSED else (lambda a: a[None])
        grads[n] = restore(grads[n])
        deltas[n], new_m[n], new_v[n] = (restore(a) for a in updates[n])

    d_s, m_s, v_s = _adamw(stack(small_w), g_tot, stack(m_small), stack(v_small), "adamw_small")
    for i, n in enumerate(small_names):
        shape = small_w[n].shape
        size = int(np.prod(shape))
        grads[n] = g_tot[i, :size].reshape(shape)
        deltas[n], new_m[n], new_v[n] = d_s[i, :size].reshape(shape), m_s[i, :size].reshape(shape), v_s[i, :size].reshape(shape)
    loss = g_tot[len(small_names), 0]

    order = ["g_mix", "w_in", "b_forget", "g_ret_out", "g_fox_q", "g_fox_k", "w_out", "g_xattn", "w_xq", "w_xkv", "g_mem", "g_xq",
             "g_xk", "w_xo", "g_ffn", "w_gate", "w_up", "w_down"]
    return (loss, grad_x[None], *[grads[n] for n in order], *[deltas[n] for n in order], *[new_m[n] for n in order],
            *[new_v[n] for n in order])
```

```python
import functools

import numpy as np
import jax
import jax.numpy as jnp
from jax import lax
from jax.experimental import pallas as pl
from jax.experimental.pallas import tpu as pltpu

F32 = jnp.float32
BF = jnp.bfloat16

D_MODEL = 1024
HEAD_DIM = 64
N_HEADS = 8
GROUP_W = 512
N_XH = 4
XHD = 256
D_FF = 2816
MAIN_W = 3584
IN_W = 3592
ROPE_BASE = 10000.0
LOG2E = 1.4426950408889634
LN2 = 0.6931471805599453
EPS = 1e-6
NEG = -1e30
LANES = 128
SUBLANES = 8
RET_BLOCK = 256
REF_CHUNK = 64
ROW_TILE = 512
FFN_BWD_TILE = 256
ATT_BLOCK = 256
FWD_GROUP = 4
TN_MAX_ROWS = 1408
SMALL_ROWS = 16
COL_SHARDED = ("w_xkv",)
TRANSPOSED = ("w_in", "w_gate", "w_up")
SHARD_ROW_ALIGN = 32
SHARD_ROW_PAD = 256
LATE = ("w_out", "w_xq", "w_xkv", "w_xo", "w_gate", "w_up", "w_down")
VMEM_LIMIT = 56 * 1024 * 1024

ADAM_LR = 0.001
ADAM_B1 = 0.9
ADAM_B2 = 0.999
ADAM_EPS = 1e-08
ADAM_WD = 0.01
ADAM_STEP = 10
CHIP_SUM_STEPS = 2
ADAM_STEPS = 8

MESH = pl.DeviceIdType.MESH
ANY = pl.BlockSpec(memory_space=pl.ANY)
VMEM_SPEC = pl.BlockSpec(memory_space=pltpu.VMEM)


def _cparams(sem=None, vmem=VMEM_LIMIT):
    return pltpu.CompilerParams(dimension_semantics=sem, vmem_limit_bytes=vmem)


def _dot(a, b):
    return jnp.dot(a.astype(BF), b.astype(BF), preferred_element_type=F32)


def _dot_nt(a, b):
    return lax.dot_general(a.astype(BF), b.astype(BF), (((1,), (1,)), ((), ())), preferred_element_type=F32)


def _dot_tn(a, b):
    return lax.dot_general(a.astype(BF), b.astype(BF), (((0,), (0,)), ((), ())), preferred_element_type=F32)


def _split3(x):
    hi = x.astype(BF)
    r = x - hi.astype(F32)
    mid = r.astype(BF)
    lo = (r - mid.astype(F32)).astype(BF)
    return hi, mid, lo


def _dot_exact(ind, x):
    hi, mid, lo = _split3(x)
    return (jnp.dot(ind, lo, preferred_element_type=F32) + jnp.dot(ind, mid, preferred_element_type=F32)
            + jnp.dot(ind, hi, preferred_element_type=F32))


def _dot_nt_exact(ind, x):
    hi, mid, lo = _split3(x)
    dn = (((1,), (1,)), ((), ()))
    return (lax.dot_general(ind, lo, dn, preferred_element_type=F32) + lax.dot_general(ind, mid, dn, preferred_element_type=F32)
            + lax.dot_general(ind, hi, dn, preferred_element_type=F32))


def _sigmoid(x):
    return 1.0 / (1.0 + jnp.exp(-x))


def _rms_fwd(x, g):
    r = lax.rsqrt(jnp.mean(x * x, axis=-1, keepdims=True) + EPS)
    return x * r * g


def _rms_bwd(x, g, dy):
    r = lax.rsqrt(jnp.mean(x * x, axis=-1, keepdims=True) + EPS)
    xh = x * r
    dg = jnp.sum(dy * xh, axis=0, keepdims=True)
    dxh = dy * g
    dx = r * (dxh - xh * jnp.mean(dxh * xh, axis=-1, keepdims=True))
    return dx, dg


def _group_mean64(x):
    lane = lax.broadcasted_iota(jnp.int32, x.shape, 1)
    lo = lane < HEAD_DIM
    s_lo = jnp.sum(jnp.where(lo, x, 0.0), axis=-1, keepdims=True)
    s_hi = jnp.sum(jnp.where(lo, 0.0, x), axis=-1, keepdims=True)
    return jnp.where(lo, s_lo, s_hi) * (1.0 / HEAD_DIM)


def _swap32(x):
    lane = lax.broadcasted_iota(jnp.int32, x.shape, 1)
    first = (lane % HEAD_DIM) < (HEAD_DIM // 2)
    return jnp.where(first, pltpu.roll(x, LANES - HEAD_DIM // 2, axis=1), pltpu.roll(x, HEAD_DIM // 2, axis=1))


def _chunks(w):
    return [slice(j * LANES, (j + 1) * LANES) for j in range(w // LANES)]


def _aug_pair(qk, f_cols, is_query):
    lane = lax.broadcasted_iota(jnp.int32, qk.shape, 1)
    a = lane - HEAD_DIM
    values = (qk, pltpu.roll(qk, HEAD_DIM, axis=1))
    out = []
    for hh in range(2):
        hi, mid, lo = (p.astype(F32) for p in _split3(f_cols[hh] * LOG2E))
        if is_query:
            aux = jnp.where(a == 0, hi, jnp.where(a == 1, mid, jnp.where(a == 2, lo, jnp.where(a < 6, 1.0, 0.0))))
        else:
            aux = jnp.where(a < 3, 1.0, jnp.where(a == 3, -hi, jnp.where(a == 4, -mid, jnp.where(a == 5, -lo, 0.0))))
        out.append(jnp.where(a < 0, values[hh], aux))
    return jnp.concatenate(out, axis=-1).astype(BF)


def _mem_kv_fwd(mem, g_mem, w_xkv, g_xk):
    m_tok = mem.shape[0]

    def body(mem_ref, gm_ref, w_ref, gk_ref, memn_ref, kraw_ref, kn_ref, v_ref):
        mn = _rms_fwd(mem_ref[...], gm_ref[...]).astype(BF)
        memn_ref[...] = mn
        kv = jnp.dot(mn, w_ref[...], preferred_element_type=F32)
        k = kv[:, :D_MODEL]
        kraw_ref[...] = k
        v_ref[...] = kv[:, D_MODEL:].astype(BF)
        for h in range(N_XH):
            sl = slice(h * XHD, (h + 1) * XHD)
            kn_ref[:, sl] = _rms_fwd(k[:, sl], gk_ref[...]).astype(BF)

    return pl.pallas_call(
        body, name="mem_kv_fwd",
        out_shape=(jax.ShapeDtypeStruct((m_tok, D_MODEL), BF), jax.ShapeDtypeStruct((m_tok, D_MODEL), F32),
                   jax.ShapeDtypeStruct((m_tok, D_MODEL), BF), jax.ShapeDtypeStruct((m_tok, D_MODEL), BF)),
        in_specs=[VMEM_SPEC] * 4, out_specs=(VMEM_SPEC,) * 4, compiler_params=_cparams(),
    )(mem, g_mem, w_xkv, g_xk)


def _in_proj_fwd(x, g_mix, w_in_t, b_pad, cos_t, sin_t, gq_t, gk_t):
    t_len = x.shape[0]
    tm = min(ROW_TILE, t_len)
    n_t = t_len // tm

    def body(x_ref, g_ref, wm_ref, wf_ref, b_ref, cos_ref, sin_ref, gq_ref, gk_ref,
             n1_ref, proj_ref, rq_ref, rk_ref, qa_ref, ka_ref, z_ref, carry):
        i = pl.program_id(0)

        @pl.when(i == 0)
        def _():
            carry[...] = jnp.zeros_like(carry)

        n1 = _rms_fwd(x_ref[...], g_ref[...]).astype(BF)
        n1_ref[...] = n1
        z = _dot_nt(n1, wf_ref[...]) + b_ref[...]
        z_ref[...] = z
        lane = lax.broadcasted_iota(jnp.int32, z.shape, 1)
        lf = jnp.where(lane < N_HEADS, jnp.minimum(z, 0.0) - jnp.log(1.0 + jnp.exp(-jnp.abs(z))), 0.0)
        row = lax.broadcasted_iota(jnp.int32, (tm, tm), 0)
        col = lax.broadcasted_iota(jnp.int32, (tm, tm), 1)
        tri = (row >= col).astype(BF)
        fc = _dot_exact(tri, lf) + carry[0:1, :]
        carry[...] = jnp.broadcast_to(fc[tm - 1:tm, :], carry.shape)
        c, s = cos_ref[...], sin_ref[...]

        def section(n):
            p = _dot_nt(n1, wm_ref[n * GROUP_W:(n + 1) * GROUP_W, :])
            proj_ref[:, n * GROUP_W:(n + 1) * GROUP_W] = p.astype(BF)
            return p

        def rotate(p, out_ref, scale):
            for sl in _chunks(GROUP_W):
                out_ref[:, sl] = ((p[:, sl] * c + _swap32(p[:, sl]) * s) * scale).astype(BF)

        def norm_aug(p, gain, out_ref, scale, is_query):
            for j, sl in enumerate(_chunks(GROUP_W)):
                f = p[:, sl]
                f = f * lax.rsqrt(_group_mean64(f * f) + EPS) * gain * scale
                out_ref[:, 2 * j * LANES:2 * (j + 1) * LANES] = _aug_pair(f, [fc[:, 2 * j:2 * j + 1], fc[:, 2 * j + 1:2 * j + 2]], is_query)

        p_rq, p_rk = section(0), section(1)
        rotate(p_rq, rq_ref, 0.125)
        section(2)
        rotate(p_rk, rk_ref, 1.0)
        section(3)
        p_fq = section(4)
        p_fk = section(5)
        norm_aug(p_fq, gq_ref[...], qa_ref, 0.125 * LOG2E, True)
        section(6)
        norm_aug(p_fk, gk_ref[...], ka_ref, 1.0, False)

    row_spec = lambda w: pl.BlockSpec((tm, w), lambda i: (i, 0))
    full = lambda a: pl.BlockSpec(a.shape, lambda i: (0,) * a.ndim)
    return pl.pallas_call(
        body, name="in_proj_fwd", grid=(n_t,),
        out_shape=(jax.ShapeDtypeStruct((t_len, D_MODEL), BF), jax.ShapeDtypeStruct((t_len, MAIN_W), BF),
                   jax.ShapeDtypeStruct((t_len, GROUP_W), BF), jax.ShapeDtypeStruct((t_len, GROUP_W), BF),
                   jax.ShapeDtypeStruct((t_len, 2 * GROUP_W), BF), jax.ShapeDtypeStruct((t_len, 2 * GROUP_W), BF),
                   jax.ShapeDtypeStruct((t_len, LANES), F32)),
        in_specs=[row_spec(D_MODEL), full(g_mix), *_w_in_specs(), full(b_pad), row_spec(LANES), row_spec(LANES),
                  full(gq_t), full(gk_t)],
        out_specs=(row_spec(D_MODEL), row_spec(MAIN_W), row_spec(GROUP_W), row_spec(GROUP_W), row_spec(2 * GROUP_W),
                   row_spec(2 * GROUP_W), row_spec(LANES)),
        scratch_shapes=[pltpu.VMEM((8, LANES), F32)],
        compiler_params=_cparams(("arbitrary",)),
    )(x, g_mix, w_in_t, w_in_t, b_pad, cos_t, sin_t, gq_t, gk_t)


def _w_in_specs():
    return (pl.BlockSpec((MAIN_W, D_MODEL), lambda i: (0, 0)), pl.BlockSpec((LANES, D_MODEL), lambda i: (MAIN_W // LANES, 0)))


def _matmul_tn_pair(a1, a2, b, name, bk=1024):
    t_len, m = a1.shape
    n = b.shape[1]
    bm = m if m <= TN_MAX_ROWS else m // 2
    bk = min(bk, t_len)

    def body(a1_ref, a2_ref, b_ref, o_ref):
        @pl.when(pl.program_id(1) == 0)
        def _():
            o_ref[...] = jnp.zeros_like(o_ref)

        bv = b_ref[...]
        o_ref[0] += _dot_tn(a1_ref[...], bv)
        o_ref[1] += _dot_tn(a2_ref[...], bv)

    a_spec = pl.BlockSpec((bk, bm), lambda i, k: (k, i))
    return pl.pallas_call(
        body, name=name, grid=(m // bm, t_len // bk),
        out_shape=jax.ShapeDtypeStruct((2, m, n), F32),
        in_specs=[a_spec, a_spec, pl.BlockSpec((bk, n), lambda i, k: (k, 0))],
        out_specs=pl.BlockSpec((2, bm, n), lambda i, k: (0, i, 0)),
        compiler_params=_cparams(("arbitrary", "arbitrary")),
    )(a1, a2, b)


def _decay_tables(c):
    h = np.arange(N_HEADS, dtype=np.float64)
    lg = np.log(1.0 - 2.0 ** (-5.0 - h)).astype(np.float32).astype(np.float64)
    t = np.arange(c)
    same_or_earlier = (t[None, :] // REF_CHUNK) <= (t[:, None] // REF_CHUNK)
    w = np.where(same_or_earlier[None], np.exp(lg[:, None, None] * np.abs(t[:, None] - t[None, :])[None]), 0.0)
    qd = np.exp(lg[:, None] * (t[None, :] + 1.0))
    kd = np.exp(lg[:, None] * (c - 1.0 - t[None, :]))
    cd = np.exp(lg * c)
    ones = np.ones((1, 1, HEAD_DIM))
    return (jnp.asarray(w, F32), jnp.asarray(qd[:, :, None] * ones, F32), jnp.asarray(kd[:, :, None] * ones, F32),
            jnp.asarray(cd[:, None, None] * np.ones((1, HEAD_DIM, HEAD_DIM)), F32))


def _retention_fwd(rq, rk, proj, g_ret, tables):
    t_len = rq.shape[0]
    c = min(RET_BLOCK, t_len)
    n_b = t_len // c
    wdec, qdec, kdec, cdec = tables
    v_col, g_col = 2 * GROUP_W // LANES, 3 * GROUP_W // LANES

    def body(q_ref, k_ref, v_ref, rg_ref, g_ref, w_ref, qd_ref, kd_ref, cd_ref, raw_ref, mix_ref, st_ref, state):
        i = pl.program_id(1)

        @pl.when(i == 0)
        def _():
            state[...] = jnp.zeros_like(state)

        q2, k2, v2 = q_ref[...], k_ref[...], v_ref[...]
        heads = [tuple(t[:, hh * HEAD_DIM:(hh + 1) * HEAD_DIM] for t in (q2, k2, v2)) for hh in range(2)]
        scores = [(_dot_nt(q, k) * w_ref[hh]).astype(BF) for hh, (q, k, _) in enumerate(heads)]
        outs = []
        for hh, (q, k, v) in enumerate(heads):
            sp = state[hh]
            st_ref[0, 0, hh] = sp
            outs.append(jnp.dot(scores[hh], v, preferred_element_type=F32) + _dot(q.astype(F32) * qd_ref[hh], sp))
            state[hh] = sp * cd_ref[hh] + _dot_tn(k.astype(F32) * kd_ref[hh], v)
        o2 = jnp.concatenate(outs, axis=-1)
        raw_ref[...] = o2
        xc = o2 - _group_mean64(o2)
        xh = xc * lax.rsqrt(_group_mean64(xc * xc) + EPS)
        gate = rg_ref[...].astype(F32)
        mix_ref[...] = (gate * _sigmoid(gate) * (xh * g_ref[0])).astype(BF)

    blk = lambda col0: pl.BlockSpec((c, LANES), lambda hp, i: (i, col0 + hp))
    tab = lambda a: pl.BlockSpec((2,) + a.shape[1:], lambda hp, i: (hp, 0, 0))
    return pl.pallas_call(
        body, name="retention_fwd", grid=(N_HEADS // 2, n_b),
        out_shape=(jax.ShapeDtypeStruct((t_len, GROUP_W), F32), jax.ShapeDtypeStruct((t_len, GROUP_W), BF),
                   jax.ShapeDtypeStruct((N_HEADS // 2, n_b, 2, HEAD_DIM, HEAD_DIM), F32)),
        in_specs=[blk(0), blk(0), blk(v_col), blk(g_col), pl.BlockSpec((1, 1, LANES), lambda hp, i: (hp, 0, 0)),
                  tab(wdec), tab(qdec), tab(kdec), tab(cdec)],
        out_specs=(blk(0), blk(0), pl.BlockSpec((1, 1, 2, HEAD_DIM, HEAD_DIM), lambda hp, i: (hp, i, 0, 0, 0))),
        scratch_shapes=[pltpu.VMEM((2, HEAD_DIM, HEAD_DIM), F32)],
        compiler_params=_cparams(("arbitrary", "arbitrary")),
    )(rq, rk, proj, proj, g_ret, wdec, qdec, kdec, cdec)


def _fox_fwd(q_aug, k_aug, proj, shards):
    t_len = q_aug.shape[0]
    tq = min(ATT_BLOCK, t_len)
    nsub = min(FWD_GROUP, t_len // tq)
    tg = nsub * tq
    n_q = t_len // tg
    v_col = 6 * GROUP_W // LANES
    tc = min(512, t_len)
    n_w = len(shards)
    n_steps = (N_HEADS // 2) * n_q

    def body(*refs):
        q_ref, k_ref, v_ref = refs[:3]
        o_ref, o32_ref, lse_ref = refs[3 + n_w:6 + n_w]
        vt = refs[6 + 2 * n_w]
        comm = (refs[3:3 + n_w], refs[6 + n_w:6 + 2 * n_w]) + tuple(refs[7 + 2 * n_w:])
        i = pl.program_id(1)
        step = pl.program_id(0) * n_q + i

        @pl.when(step == 0)
        def _():
            _gather_phase(0, *comm)

        @pl.when(step == (3 * n_steps) // 4)
        def _():
            _gather_phase(1, *comm)

        @pl.when(i == 0)
        def _():
            for c0 in range(0, t_len, tc):
                vt[:, c0:c0 + tc] = v_ref[c0:c0 + tc, :].T

        chains = [(u, hh) for u in range(nsub) for hh in range(2)]
        qs = {(u, hh): q_ref[u * tq:(u + 1) * tq, hh * LANES:(hh + 1) * LANES] for u, hh in chains}
        ones = jnp.ones((HEAD_DIM, tq), BF)

        def scores(j, which):
            k2 = k_ref[pl.ds(pl.multiple_of(j * tq, tq), tq), :]
            return {ch: _dot_nt(k2[:, ch[1] * LANES:(ch[1] + 1) * LANES], qs[ch]) for ch in which}

        def update(j, ss, carry, masked):
            v2 = vt[:, pl.ds(pl.multiple_of(j * tq, tq), tq)]
            ps, stats = {}, {}
            for ch in ss:
                m = carry[ch][0]
                s_t = ss[ch]
                if ch in masked:
                    krow = lax.broadcasted_iota(jnp.int32, (tq, tq), 0)
                    qcol = lax.broadcasted_iota(jnp.int32, (tq, tq), 1)
                    s_t = jnp.where(qcol >= krow, s_t, NEG)
                m_new = jnp.maximum(m, jnp.max(s_t, axis=0, keepdims=True))
                ps[ch] = jnp.exp2(s_t - m_new).astype(BF)
                stats[ch] = (m_new, jnp.exp2(m - m_new))
            out = dict(carry)
            for ch in ss:
                m_new, alpha = stats[ch]
                v_aug = jnp.concatenate([v2[ch[1] * HEAD_DIM:(ch[1] + 1) * HEAD_DIM, :], ones], axis=0)
                out[ch] = (m_new, carry[ch][1] * alpha + jnp.dot(v_aug, ps[ch], preferred_element_type=F32))
            return out

        def advance(j, state):
            ss, carry = state
            return scores(j + 1, chains), update(j, ss, carry, ())

        init = {ch: (jnp.full((1, tq), NEG, F32), jnp.zeros((LANES, tq), F32)) for ch in chains}
        first = nsub * i
        ss, carry = lax.fori_loop(0, first, advance, (scores(0, chains), init))
        carry = update(first, ss, carry, [(0, 0), (0, 1)])
        for u in range(1, nsub):
            rest = [(uu, hh) for uu in range(u, nsub) for hh in range(2)]
            carry = update(first + u, scores(first + u, rest), carry, [(u, 0), (u, 1)])
        for u in range(nsub):
            outs, lses = [], []
            for hh in range(2):
                m, acc = carry[u, hh]
                l = acc[HEAD_DIM:HEAD_DIM + 1, :]
                outs.append(acc[:HEAD_DIM, :] / l)
                lses.append(m + jnp.log2(l))
            o2 = jnp.concatenate(outs, axis=0).T
            o32_ref[u * tq:(u + 1) * tq, :] = o2
            o_ref[u * tq:(u + 1) * tq, :] = o2.astype(BF)
            lse_ref[0, :, u * tq:(u + 1) * tq] = jnp.concatenate(lses, axis=0)

        @pl.when(step == n_steps - 1)
        def _():
            _gather_phase(2, *comm)

    return pl.pallas_call(
        body, name="fox_fwd", grid=(N_HEADS // 2, n_q),
        out_shape=(jax.ShapeDtypeStruct((t_len, GROUP_W), BF), jax.ShapeDtypeStruct((t_len, GROUP_W), F32),
                   jax.ShapeDtypeStruct((N_HEADS // 2, 2, t_len), F32))
        + tuple(jax.ShapeDtypeStruct((4,) + s.shape, s.dtype) for s in shards),
        in_specs=[pl.BlockSpec((tg, 2 * LANES), lambda hp, i: (i, hp)),
                  pl.BlockSpec((t_len, 2 * LANES), lambda hp, i: (0, hp)),
                  pl.BlockSpec((t_len, LANES), lambda hp, i: (0, v_col + hp))] + [ANY] * n_w,
        out_specs=(pl.BlockSpec((tg, LANES), lambda hp, i: (i, hp)), pl.BlockSpec((tg, LANES), lambda hp, i: (i, hp)),
                   pl.BlockSpec((1, 2, tg), lambda hp, i: (hp, 0, i))) + (ANY,) * n_w,
        scratch_shapes=[pltpu.VMEM((LANES, t_len), BF)] + _gather_scratch(n_w),
        compiler_params=_cparams(("arbitrary", "arbitrary")),
    )(q_aug, k_aug, proj, *shards)


def _softmax_rows(s):
    p = jnp.exp(s - jnp.max(s, axis=-1, keepdims=True))
    return p / jnp.sum(p, axis=-1, keepdims=True)


def _attn_out_xattn_fwd(x, mix_r, mix_f, w_out, g_xattn, w_xq, g_xq, kn, v, w_xo):
    t_len = x.shape[0]
    tm = min(ROW_TILE, t_len)

    def body(x_ref, mr_ref, mf_ref, wo_ref, g_ref, wq_ref, gq_ref, kn_ref, v_ref, wxo_ref,
             h1_ref, hn_ref, qx_ref, o_ref, h2_ref):
        h1 = x_ref[...] + jnp.dot(mr_ref[...], wo_ref[:GROUP_W, :], preferred_element_type=F32) \
            + jnp.dot(mf_ref[...], wo_ref[GROUP_W:, :], preferred_element_type=F32)
        h1_ref[...] = h1
        hn = _rms_fwd(h1, g_ref[...]).astype(BF)
        hn_ref[...] = hn
        qx = jnp.dot(hn, wq_ref[...], preferred_element_type=F32).astype(BF)
        qx_ref[...] = qx
        sls = [slice(h * XHD, (h + 1) * XHD) for h in range(N_XH)]
        qns = [_rms_fwd(qx[:, sl].astype(F32), gq_ref[...]).astype(BF) for sl in sls]
        logits = [_dot_nt(qn, kn_ref[:, sl]) * (XHD ** -0.5) for qn, sl in zip(qns, sls)]
        ps = [_softmax_rows(s).astype(BF) for s in logits]
        for p, sl in zip(ps, sls):
            o_ref[:, sl] = jnp.dot(p, v_ref[:, sl], preferred_element_type=F32).astype(BF)
        h2_ref[...] = h1 + jnp.dot(o_ref[...], wxo_ref[...], preferred_element_type=F32)

    row_spec = lambda w: pl.BlockSpec((tm, w), lambda i: (i, 0))
    full = lambda a: pl.BlockSpec(a.shape, lambda i: (0,) * a.ndim)
    return pl.pallas_call(
        body, name="attn_out_xattn_fwd", grid=(t_len // tm,),
        out_shape=(jax.ShapeDtypeStruct((t_len, D_MODEL), F32), jax.ShapeDtypeStruct((t_len, D_MODEL), BF),
                   jax.ShapeDtypeStruct((t_len, D_MODEL), BF), jax.ShapeDtypeStruct((t_len, D_MODEL), BF),
                   jax.ShapeDtypeStruct((t_len, D_MODEL), F32)),
        in_specs=[row_spec(D_MODEL), row_spec(GROUP_W), row_spec(GROUP_W), full(w_out), full(g_xattn), full(w_xq), full(g_xq),
                  full(kn), full(v), full(w_xo)],
        out_specs=(row_spec(D_MODEL),) * 5,
        compiler_params=_cparams(("arbitrary",)),
    )(x, mix_r, mix_f, w_out, g_xattn, w_xq, g_xq, kn, v, w_xo)


def _ffn_loss_fwd(h2, g_ffn, w_gate, w_up, w_down, target):
    t_len = h2.shape[0]
    tm = min(ROW_TILE, t_len)

    def body(h2_ref, g_ref, wg_ref, wu_ref, wd_ref, tgt_ref, hn_ref, gate_ref, up_ref, act_ref, dh3_ref, loss_ref):
        @pl.when(pl.program_id(0) == 0)
        def _():
            loss_ref[...] = jnp.zeros_like(loss_ref)

        h2v = h2_ref[...]
        hn = _rms_fwd(h2v, g_ref[...]).astype(BF)
        hn_ref[...] = hn
        gate = _dot_nt(hn, wg_ref[...])
        up = _dot_nt(hn, wu_ref[...])
        gate_ref[...] = gate.astype(BF)
        up_ref[...] = up.astype(BF)
        act = (gate * _sigmoid(gate) * up).astype(BF)
        act_ref[...] = act
        diff = h2v + jnp.dot(act, wd_ref[...], preferred_element_type=F32) - tgt_ref[...]
        dh3_ref[...] = diff * (1.0 / D_MODEL)
        per_row = jnp.sum(diff * diff, axis=-1, keepdims=True) * (1.0 / D_MODEL)
        loss_ref[...] += 0.5 * jnp.sum(per_row, axis=0, keepdims=True)

    row_spec = lambda w: pl.BlockSpec((tm, w), lambda i: (i, 0))
    full = lambda a: pl.BlockSpec(a.shape, lambda i: (0,) * a.ndim, pipeline_mode=pl.Buffered(1))
    return pl.pallas_call(
        body, name="ffn_loss_fwd", grid=(t_len // tm,),
        out_shape=(jax.ShapeDtypeStruct((t_len, D_MODEL), BF), jax.ShapeDtypeStruct((t_len, D_FF), BF),
                   jax.ShapeDtypeStruct((t_len, D_FF), BF), jax.ShapeDtypeStruct((t_len, D_FF), BF),
                   jax.ShapeDtypeStruct((t_len, D_MODEL), F32), jax.ShapeDtypeStruct((8, LANES), F32)),
        in_specs=[row_spec(D_MODEL), full(g_ffn), full(w_gate), full(w_up), full(w_down), row_spec(D_MODEL)],
        out_specs=(row_spec(D_MODEL), row_spec(D_FF), row_spec(D_FF), row_spec(D_FF), row_spec(D_MODEL),
                   pl.BlockSpec((8, LANES), lambda i: (0, 0))),
        compiler_params=_cparams(("arbitrary",)),
    )(h2, g_ffn, w_gate, w_up, w_down, target)


def _ffn_bwd(dh3, gate, up, h2, g_ffn, w_gate, w_up, w_down):
    t_len = h2.shape[0]
    tm = min(FFN_BWD_TILE, t_len)

    def body(dh3_ref, gate_ref, up_ref, h2_ref, g_ref, wg_ref, wu_ref, wd_ref, dgate_ref, dup_ref, dh2_ref, dg_ref):
        @pl.when(pl.program_id(0) == 0)
        def _():
            dg_ref[...] = jnp.zeros_like(dg_ref)

        dh3v = dh3_ref[...]
        dact = _dot_nt(dh3v, wd_ref[...])
        g = gate_ref[...].astype(F32)
        sg = _sigmoid(g)
        dup = (dact * (g * sg)).astype(BF)
        dgate = (dact * up_ref[...].astype(F32) * (sg * (1.0 + g * (1.0 - sg)))).astype(BF)
        dup_ref[...] = dup
        dgate_ref[...] = dgate
        dhn = jnp.dot(dgate, wg_ref[...], preferred_element_type=F32) + jnp.dot(dup, wu_ref[...], preferred_element_type=F32)
        dx, dg = _rms_bwd(h2_ref[...], g_ref[...], dhn)
        dh2_ref[...] = dh3v + dx
        dg_ref[...] += dg

    row_spec = lambda w: pl.BlockSpec((tm, w), lambda i: (i, 0))
    full = lambda a: pl.BlockSpec(a.shape, lambda i: (0,) * a.ndim, pipeline_mode=pl.Buffered(1))
    return pl.pallas_call(
        body, name="ffn_bwd", grid=(t_len // tm,),
        out_shape=(jax.ShapeDtypeStruct((t_len, D_FF), BF), jax.ShapeDtypeStruct((t_len, D_FF), BF),
                   jax.ShapeDtypeStruct((t_len, D_MODEL), F32), jax.ShapeDtypeStruct((1, D_MODEL), F32)),
        in_specs=[row_spec(D_MODEL), row_spec(D_FF), row_spec(D_FF), row_spec(D_MODEL), full(g_ffn), full(w_gate), full(w_up),
                  full(w_down)],
        out_specs=(row_spec(D_FF), row_spec(D_FF), row_spec(D_MODEL), pl.BlockSpec((1, D_MODEL), lambda i: (0, 0))),
        compiler_params=_cparams(("arbitrary",)),
    )(dh3, gate, up, h2, g_ffn, w_gate, w_up, w_down)


def _attn_out_xattn_bwd(dh2, h1, qx, kn, v, w_xo, w_xq, w_out, g_xattn, g_xq):
    t_len = h1.shape[0]
    tm = min(ROW_TILE, t_len)
    m_tok = kn.shape[0]

    def body(dh2_ref, h1_ref, qx_ref, kn_ref, v_ref, wxo_ref, wq_ref, wo_ref, g_ref, gq_ref,
             dqx_ref, dh1_ref, dmr_ref, dmf_ref, dkn_ref, dv_ref, dg_ref, dgq_ref, dqx_scr):
        @pl.when(pl.program_id(0) == 0)
        def _():
            dkn_ref[...] = jnp.zeros_like(dkn_ref)
            dv_ref[...] = jnp.zeros_like(dv_ref)
            dg_ref[...] = jnp.zeros_like(dg_ref)
            dgq_ref[...] = jnp.zeros_like(dgq_ref)

        dh2v = dh2_ref[...]
        do = _dot_nt(dh2v, wxo_ref[...])
        gq = gq_ref[...]
        sls = [slice(h * XHD, (h + 1) * XHD) for h in range(N_XH)]
        qraws = [qx_ref[:, sl].astype(F32) for sl in sls]
        qns = [_rms_fwd(qraw, gq).astype(BF) for qraw in qraws]
        dohs = [do[:, sl].astype(BF) for sl in sls]
        logits = [_dot_nt(qn, kn_ref[:, sl]) * (XHD ** -0.5) for qn, sl in zip(qns, sls)]
        dps = [_dot_nt(doh, v_ref[:, sl]) for doh, sl in zip(dohs, sls)]
        ps = [_softmax_rows(s) for s in logits]
        dss = [(p * (dp - jnp.sum(dp * p, axis=-1, keepdims=True)) * (XHD ** -0.5)).astype(BF) for p, dp in zip(ps, dps)]
        dqns = []
        for h, sl in enumerate(sls):
            dv_ref[:, sl] += _dot_tn(ps[h], dohs[h])
            dqns.append(jnp.dot(dss[h], kn_ref[:, sl], preferred_element_type=F32))
            dkn_ref[:, sl] += _dot_tn(dss[h], qns[h])
        dgq = jnp.zeros((1, XHD), F32)
        for h, sl in enumerate(sls):
            dx, dg_h = _rms_bwd(qraws[h], gq, dqns[h])
            dgq = dgq + dg_h
            dqx_scr[:, sl] = dx.astype(BF)
        dgq_ref[...] += dgq
        dqx = dqx_scr[...]
        dqx_ref[...] = dqx
        dhn = _dot_nt(dqx, wq_ref[...])
        dx, dg = _rms_bwd(h1_ref[...], g_ref[...], dhn)
        dg_ref[...] += dg
        dh1 = dh2v + dx
        dh1_ref[...] = dh1
        dmix = _dot_nt(dh1, wo_ref[...])
        dmr_ref[...] = dmix[:, :GROUP_W]
        dmf_ref[...] = dmix[:, GROUP_W:].astype(BF)

    row_spec = lambda w: pl.BlockSpec((tm, w), lambda i: (i, 0))
    full = lambda a: pl.BlockSpec(a.shape, lambda i: (0,) * a.ndim)
    acc = lambda r, c: pl.BlockSpec((r, c), lambda i: (0, 0))
    return pl.pallas_call(
        body, name="attn_out_xattn_bwd", grid=(t_len // tm,),
        out_shape=(jax.ShapeDtypeStruct((t_len, D_MODEL), BF), jax.ShapeDtypeStruct((t_len, D_MODEL), F32),
                   jax.ShapeDtypeStruct((t_len, GROUP_W), F32), jax.ShapeDtypeStruct((t_len, GROUP_W), BF),
                   jax.ShapeDtypeStruct((m_tok, D_MODEL), F32), jax.ShapeDtypeStruct((m_tok, D_MODEL), F32),
                   jax.ShapeDtypeStruct((1, D_MODEL), F32), jax.ShapeDtypeStruct((1, XHD), F32)),
        in_specs=[row_spec(D_MODEL), row_spec(D_MODEL), row_spec(D_MODEL), full(kn), full(v), full(w_xo), full(w_xq), full(w_out),
                  full(g_xattn), full(g_xq)],
        out_specs=(row_spec(D_MODEL), row_spec(D_MODEL), row_spec(GROUP_W), row_spec(GROUP_W), acc(m_tok, D_MODEL),
                   acc(m_tok, D_MODEL), acc(1, D_MODEL), acc(1, XHD)),
        scratch_shapes=[pltpu.VMEM((tm, D_MODEL), BF)],
        compiler_params=_cparams(("arbitrary",)),
    )(dh2, h1, qx, kn, v, w_xo, w_xq, w_out, g_xattn, g_xq)


def _mem_kv_bwd(dkn, dv, kraw, mem, memn, g_mem, g_xk, w_xkv):
    m_tok = mem.shape[0]

    def body(dkn_ref, dv_ref, kraw_ref, mem_ref, memn_ref, gm_ref, gk_ref, w_ref, dw_ref, dgm_ref, dgk_ref, dkv_scr):
        gk = gk_ref[...]
        dgk = jnp.zeros((1, XHD), F32)
        for h in range(N_XH):
            sl = slice(h * XHD, (h + 1) * XHD)
            dx, dg_h = _rms_bwd(kraw_ref[:, sl], gk, dkn_ref[:, sl])
            dgk = dgk + dg_h
            dkv_scr[:, sl] = dx.astype(BF)
        dgk_ref[...] = dgk
        dkv_scr[:, D_MODEL:] = dv_ref[...].astype(BF)
        dkv = dkv_scr[...]
        dw_ref[...] = _dot_tn(memn_ref[...], dkv)
        dmemn = _dot_nt(dkv, w_ref[...])
        mem_v = mem_ref[...]
        r = lax.rsqrt(jnp.mean(mem_v * mem_v, axis=-1, keepdims=True) + EPS)
        dgm_ref[...] = jnp.sum(dmemn * mem_v * r, axis=0, keepdims=True)

    return pl.pallas_call(
        body, name="mem_kv_bwd",
        out_shape=(jax.ShapeDtypeStruct((D_MODEL, 2 * D_MODEL), F32), jax.ShapeDtypeStruct((1, D_MODEL), F32),
                   jax.ShapeDtypeStruct((1, XHD), F32)),
        in_specs=[VMEM_SPEC] * 8, out_specs=(VMEM_SPEC,) * 3,
        scratch_shapes=[pltpu.VMEM((m_tok, 2 * D_MODEL), BF)],
        compiler_params=_cparams(),
    )(dkn, dv, kraw, mem, memn, g_mem, g_xk, w_xkv)


def _fox_bwd(q_aug, k_aug, proj, dmf, o32, lse, sums):
    t_len = q_aug.shape[0]
    tb = min(ATT_BLOCK, t_len)
    n_b = t_len // tb
    nsub = 2 if n_b >= 2 else 1
    tg = nsub * tb
    n_g = t_len // tg
    v_col = 6 * GROUP_W // LANES
    n_w = len(sums)
    n_steps = (N_HEADS // 2) * n_g

    def body(*refs):
        k_ref, v_ref, q_ref, do_ref, o_ref, lse_ref = refs[:6]
        dq_ref, dk_ref, dv_ref, df_ref = refs[6 + n_w:10 + n_w]
        delta = refs[10 + 2 * n_w]
        comm = (refs[6:6 + n_w], refs[10 + n_w:10 + 2 * n_w]) + tuple(refs[11 + 2 * n_w:])
        j = pl.program_id(1)
        step = pl.program_id(0) * n_g + j

        @pl.when(step == 0)
        def _():
            _scatter_phase(0, *comm)

        @pl.when(j == 0)
        def _():
            dq_ref[...] = jnp.zeros_like(dq_ref)
            dd = do_ref[...].astype(F32) * o_ref[...]
            hrow = lax.broadcasted_iota(jnp.int32, (8, LANES), 0)
            lane = lax.broadcasted_iota(jnp.int32, (8, LANES), 1)
            ind = ((lane // HEAD_DIM) == hrow).astype(BF)
            delta[...] = _dot_nt_exact(ind, dd)

        k2, v2 = k_ref[...], v_ref[...]
        chains = [(u, hh) for u in range(nsub) for hh in range(2)]
        ks = {(u, hh): k2[u * tb:(u + 1) * tb, hh * LANES:(hh + 1) * LANES] for u, hh in chains}
        vs = {(u, hh): v2[u * tb:(u + 1) * tb, hh * HEAD_DIM:(hh + 1) * HEAD_DIM] for u, hh in chains}

        def block(i, carry, which, masked):
            rows = pl.ds(pl.multiple_of(i * tb, tb), tb)
            q2 = q_ref[rows, :]
            do2 = do_ref[rows, :]
            qs = [q2[:, hh * LANES:(hh + 1) * LANES] for hh in range(2)]
            dos = [do2[:, hh * HEAD_DIM:(hh + 1) * HEAD_DIM] for hh in range(2)]
            ss = {ch: _dot_nt(ks[ch], qs[ch[1]]) for ch in which}
            dps = {ch: _dot_nt(vs[ch], dos[ch[1]]) for ch in which}
            pts, dsts, dfs = {}, {}, {}
            for ch in which:
                hh = ch[1]
                s_t = ss[ch]
                if ch in masked:
                    krow = lax.broadcasted_iota(jnp.int32, (tb, tb), 0)
                    qcol = lax.broadcasted_iota(jnp.int32, (tb, tb), 1)
                    s_t = jnp.where(qcol >= krow, s_t, NEG)
                p_t = jnp.exp2(s_t - lse_ref[0, hh:hh + 1, rows])
                pts[ch] = p_t.astype(BF)
                ds_t = p_t * (dps[ch] - delta[hh:hh + 1, rows])
                dsts[ch] = ds_t.astype(BF)
                dfs[ch] = jnp.sum(ds_t, axis=-1, keepdims=True)
            out = dict(carry)
            for ch in which:
                dk, dv, df = carry[ch]
                dv = dv + jnp.dot(pts[ch], dos[ch[1]], preferred_element_type=F32)
                dk = dk + jnp.dot(dsts[ch], qs[ch[1]], preferred_element_type=F32)
                out[ch] = (dk, dv, df - dfs[ch])
            for hh in range(2):
                parts_dq = [_dot_tn(dsts[ch], ks[ch])[:, :HEAD_DIM] for ch in which if ch[1] == hh]
                dq_ref[rows, hh * HEAD_DIM:(hh + 1) * HEAD_DIM] += sum(parts_dq[1:], parts_dq[0])
            return out

        init = {ch: (jnp.zeros((tb, LANES), F32), jnp.zeros((tb, HEAD_DIM), F32), jnp.zeros((tb, 1), F32)) for ch in chains}
        first = nsub * j
        carry = block(first, init, [(0, 0), (0, 1)], [(0, 0), (0, 1)])
        if nsub == 2:
            carry = block(first + 1, carry, chains, [(1, 0), (1, 1)])
        carry = lax.fori_loop(first + nsub, n_b, lambda i, c: block(i, c, chains, ()), carry)
        for u in range(nsub):
            rs = slice(u * tb, (u + 1) * tb)
            dk_ref[rs, :] = jnp.concatenate([carry[u, hh][0][:, :HEAD_DIM] for hh in range(2)], axis=-1) * LN2
            dv_ref[rs, :] = jnp.concatenate([carry[u, hh][1] for hh in range(2)], axis=-1)
            df_ref[0, rs, :] = jnp.concatenate([carry[u, hh][2] for hh in range(2)], axis=-1)

        @pl.when(step == n_steps - 1)
        def _():
            _scatter_phase(1, *comm)

    blk = lambda w, col0: pl.BlockSpec((tg, w), lambda hp, j: (j, col0 + hp))
    whole = lambda w: pl.BlockSpec((t_len, w), lambda hp, j: (0, hp))
    rows2 = pl.BlockSpec((1, 2, t_len), lambda hp, j: (hp, 0, 0))
    cols2 = pl.BlockSpec((1, tg, 2), lambda hp, j: (hp, j, 0))
    return pl.pallas_call(
        body, name="fox_bwd", grid=(N_HEADS // 2, n_g),
        out_shape=(jax.ShapeDtypeStruct((t_len, GROUP_W), F32), jax.ShapeDtypeStruct((t_len, GROUP_W), F32),
                   jax.ShapeDtypeStruct((t_len, GROUP_W), F32), jax.ShapeDtypeStruct((N_HEADS // 2, t_len, 2), F32))
        + _scatter_out_shapes(sums),
        in_specs=[blk(2 * LANES, 0), blk(LANES, v_col), whole(2 * LANES), whole(LANES), whole(LANES), rows2] + [ANY] * n_w,
        out_specs=(whole(LANES), blk(LANES, 0), blk(LANES, 0), cols2) + (ANY,) * n_w,
        scratch_shapes=[pltpu.VMEM((8, t_len), F32)] + _scatter_scratch(n_w),
        compiler_params=_cparams(("arbitrary", "arbitrary")),
    )(k_aug, proj, q_aug, dmf, o32, lse, *sums)


def _retention_bwd(dmr, raw, proj, g_ret, rq, rk, states, tables, parts):
    t_len = rq.shape[0]
    c = min(RET_BLOCK, t_len)
    n_b = t_len // c
    wdec, qdec, kdec, cdec = tables
    v_col, g_col = 2 * GROUP_W // LANES, 3 * GROUP_W // LANES
    n_w = len(parts)
    n_steps = (N_HEADS // 2) * n_b

    def body(*refs):
        d_ref, raw_ref, rg_ref, g_ref, q_ref, k_ref, v_ref, st_ref, w_ref, wt_ref, qd_ref, kd_ref, cd_ref = refs[:13]
        dq_ref, dk_ref, dv_ref, drg_ref, dg_ref = refs[13 + n_w:18 + n_w]
        gstate = refs[18 + 2 * n_w]
        comm = (refs[13:13 + n_w], refs[18 + n_w:18 + 2 * n_w]) + tuple(refs[19 + 2 * n_w:])
        step = pl.program_id(0) * n_b + pl.program_id(1)

        @pl.when(step == 0)
        def _():
            _exchange_phase(0, *comm)

        @pl.when(pl.program_id(1) == 0)
        def _():
            gstate[...] = jnp.zeros_like(gstate)
            dg_ref[...] = jnp.zeros_like(dg_ref)

        d, raw_v, g = d_ref[...], raw_ref[...], g_ref[0]
        gate = rg_ref[...].astype(F32)
        xc = raw_v - _group_mean64(raw_v)
        r = lax.rsqrt(_group_mean64(xc * xc) + EPS)
        xh = xc * r
        sg = _sigmoid(gate)
        drg_ref[...] = d * (xh * g) * (sg * (1.0 + gate * (1.0 - sg)))
        dy = d * (gate * sg)
        dg_ref[0] += jnp.sum(dy * xh, axis=0, keepdims=True)
        dxh = dy * g
        do2 = r * (dxh - _group_mean64(dxh) - xh * _group_mean64(dxh * xh))
        q2, k2, v2 = q_ref[...], k_ref[...], v_ref[...]
        dqs, dks, dvs = [], [], []
        heads = [tuple(t[:, hh * HEAD_DIM:(hh + 1) * HEAD_DIM] for t in (q2, k2, v2, do2.astype(BF))) for hh in range(2)]
        firsts = [(_dot_nt(k, q) * wt_ref[hh], _dot_nt(do, v) * w_ref[hh], _dot_nt(v, do) * wt_ref[hh])
                  for hh, (q, k, v, do) in enumerate(heads)]
        for hh, (q, k, v, do) in enumerate(heads):
            a_t, dm, dm_t = firsts[hh]
            sp, gs = st_ref[0, 0, hh], gstate[hh]
            qd = q.astype(F32) * qd_ref[hh]
            kd = k.astype(F32) * kd_ref[hh]
            dqs.append(_dot(dm, k) + _dot_nt(do, sp) * qd_ref[hh])
            dks.append(_dot(dm_t, q) + _dot_nt(v, gs) * kd_ref[hh])
            dvs.append(_dot(a_t, do) + _dot(kd, gs))
            gstate[hh] = gs * cd_ref[hh] + _dot_tn(qd, do)
        dq_ref[...] = jnp.concatenate(dqs, axis=-1)
        dk_ref[...] = jnp.concatenate(dks, axis=-1)
        dv_ref[...] = jnp.concatenate(dvs, axis=-1)

        @pl.when(step == n_steps - 1)
        def _():
            _exchange_phase(1, *comm)

    blk = lambda col0: pl.BlockSpec((c, LANES), lambda hp, i: (n_b - 1 - i, col0 + hp))
    tab = lambda a: pl.BlockSpec((2,) + a.shape[1:], lambda hp, i: (hp, 0, 0))
    gspec = pl.BlockSpec((1, 1, LANES), lambda hp, i: (hp, 0, 0))
    return pl.pallas_call(
        body, name="retention_bwd", grid=(N_HEADS // 2, n_b),
        out_shape=(jax.ShapeDtypeStruct((t_len, GROUP_W), F32),) * 4 + (jax.ShapeDtypeStruct((N_HEADS // 2, 1, LANES), F32),)
        + _exchange_out_shapes(parts),
        in_specs=[blk(0), blk(0), blk(g_col), gspec, blk(0), blk(0), blk(v_col),
                  pl.BlockSpec((1, 1, 2, HEAD_DIM, HEAD_DIM), lambda hp, i: (hp, n_b - 1 - i, 0, 0, 0)),
                  tab(wdec), tab(wdec), tab(qdec), tab(kdec), tab(cdec)] + [ANY] * n_w,
        out_specs=(blk(0), blk(0), blk(0), blk(0), gspec) + (ANY,) * n_w,
        scratch_shapes=[pltpu.VMEM((2, HEAD_DIM, HEAD_DIM), F32)] + _exchange_scratch(n_w),
        compiler_params=_cparams(("arbitrary", "arbitrary")),
    )(dmr, raw, proj, g_ret, rq, rk, proj, states, wdec, jnp.transpose(wdec, (0, 2, 1)), qdec, kdec, cdec, *parts)


def _in_proj_bwd(x, g_mix, dh1, dq_r, dk_r, dv_r, drg, dq_f, dk_f, dv_f, df_col, proj, z, cos_t, sin_t, gq_t, gk_t, w_in_t):
    t_len = x.shape[0]
    tm = min(ROW_TILE, t_len)
    n_t = t_len // tm

    def body(x_ref, g_ref, dh1_ref, dqr_ref, dkr_ref, dvr_ref, drg_ref, dqf_ref, dkf_ref, dvf_ref, df_ref, fq_ref, fk_ref, z_ref,
             cos_ref, sin_ref, gq_ref, gk_ref, wm_ref, wf_ref,
             dproj_ref, dz_ref, dx_ref, dg_ref, dgq_ref, dgk_ref, db_ref, carry, gq_acc, gk_acc):
        i = pl.program_id(0)

        @pl.when(i == 0)
        def _():
            carry[...] = jnp.zeros_like(carry)
            gq_acc[...] = jnp.zeros_like(gq_acc)
            gk_acc[...] = jnp.zeros_like(gk_acc)
            dg_ref[...] = jnp.zeros_like(dg_ref)
            db_ref[...] = jnp.zeros_like(db_ref)

        c, s = cos_ref[...], sin_ref[...]
        gq, gk = gq_ref[...], gk_ref[...]
        dgq = jnp.zeros((1, LANES), F32)
        dgk = jnp.zeros((1, LANES), F32)
        for sl in _chunks(GROUP_W):
            dy = dqr_ref[:, sl] * 0.125
            dproj_ref[:, sl] = (dy * c + _swap32(dy * s)).astype(BF)
            dy = dkr_ref[:, sl]
            dproj_ref[:, GROUP_W + sl.start:GROUP_W + sl.stop] = (dy * c + _swap32(dy * s)).astype(BF)
            dproj_ref[:, 2 * GROUP_W + sl.start:2 * GROUP_W + sl.stop] = dvr_ref[:, sl].astype(BF)
            dproj_ref[:, 3 * GROUP_W + sl.start:3 * GROUP_W + sl.stop] = drg_ref[:, sl].astype(BF)
            for src, dsrc, gain, off in ((fq_ref, dqf_ref, gq, 4), (fk_ref, dkf_ref, gk, 5)):
                xr = src[:, sl].astype(F32)
                r = lax.rsqrt(_group_mean64(xr * xr) + EPS)
                xh = xr * r
                dy = dsrc[:, sl] * (0.125 if off == 4 else 1.0)
                dgs = jnp.sum(dy * xh, axis=0, keepdims=True)
                if off == 4:
                    dgq = dgq + dgs
                else:
                    dgk = dgk + dgs
                dxh = dy * gain
                dproj_ref[:, off * GROUP_W + sl.start:off * GROUP_W + sl.stop] = \
                    (r * (dxh - xh * _group_mean64(dxh * xh))).astype(BF)
            dproj_ref[:, 6 * GROUP_W + sl.start:6 * GROUP_W + sl.stop] = dvf_ref[:, sl].astype(BF)
        gq_acc[...] += dgq
        gk_acc[...] += dgk
        row = lax.broadcasted_iota(jnp.int32, (tm, tm), 0)
        col = lax.broadcasted_iota(jnp.int32, (tm, tm), 1)
        dlf = _dot_exact((col >= row).astype(BF), df_ref[...]) + carry[0:1, :]
        carry[...] = jnp.broadcast_to(dlf[0:1, :], carry.shape)
        lane = lax.broadcasted_iota(jnp.int32, (tm, LANES), 1)
        dz = jnp.where(lane < N_HEADS, dlf / (1.0 + jnp.exp(z_ref[...])), 0.0)
        db_ref[...] += jnp.sum(dz, axis=0, keepdims=True)
        dz_bf = dz.astype(BF)
        dz_ref[...] = dz_bf
        dn1 = jnp.dot(dz_bf, wf_ref[...], preferred_element_type=F32)
        for sec in range(MAIN_W // GROUP_W):
            sl = slice(sec * GROUP_W, (sec + 1) * GROUP_W)
            dn1 = dn1 + jnp.dot(dproj_ref[:, sl], wm_ref[sl, :], preferred_element_type=F32)
        dx, dg = _rms_bwd(x_ref[...], g_ref[...], dn1)
        dx_ref[...] = dh1_ref[...] + dx
        dg_ref[...] += dg

        @pl.when(i == n_t - 1)
        def _():
            dgq_ref[...] = gq_acc[:, :HEAD_DIM] + gq_acc[:, HEAD_DIM:]
            dgk_ref[...] = gk_acc[:, :HEAD_DIM] + gk_acc[:, HEAD_DIM:]

    row_spec = lambda w, col=0: pl.BlockSpec((tm, w), lambda i: (n_t - 1 - i, col))
    full = lambda a: pl.BlockSpec(a.shape, lambda i: (0,) * a.ndim)
    acc = lambda r, c: pl.BlockSpec((r, c), lambda i: (0, 0))
    return pl.pallas_call(
        body, name="in_proj_bwd", grid=(n_t,),
        out_shape=(jax.ShapeDtypeStruct((t_len, MAIN_W), BF), jax.ShapeDtypeStruct((t_len, LANES), BF),
                   jax.ShapeDtypeStruct((t_len, D_MODEL), F32), jax.ShapeDtypeStruct((1, D_MODEL), F32),
                   jax.ShapeDtypeStruct((1, HEAD_DIM), F32), jax.ShapeDtypeStruct((1, HEAD_DIM), F32),
                   jax.ShapeDtypeStruct((1, LANES), F32)),
        in_specs=[row_spec(D_MODEL), full(g_mix), row_spec(D_MODEL)] + [row_spec(GROUP_W)] * 7
        + [row_spec(LANES), row_spec(GROUP_W, 4), row_spec(GROUP_W, 5), row_spec(LANES), row_spec(LANES), row_spec(LANES),
           full(gq_t), full(gk_t), *_w_in_specs()],
        out_specs=(row_spec(MAIN_W), row_spec(LANES), row_spec(D_MODEL), acc(1, D_MODEL), acc(1, HEAD_DIM), acc(1, HEAD_DIM),
                   acc(1, LANES)),
        scratch_shapes=[pltpu.VMEM((8, LANES), F32), pltpu.VMEM((1, LANES), F32), pltpu.VMEM((1, LANES), F32)],
        compiler_params=_cparams(("arbitrary",)),
    )(x, g_mix, dh1, dq_r, dk_r, dv_r, drg, dq_f, dk_f, dv_f, df_col, proj, proj, z, cos_t, sin_t, gq_t, gk_t, w_in_t, w_in_t)


def _matmul_tn(a, b, name, bk=1024):
    t_len, m = a.shape
    n = b.shape[1]
    bm = m if m <= TN_MAX_ROWS else m // 2
    bk = min(bk, t_len)

    def body(a_ref, b_ref, o_ref):
        @pl.when(pl.program_id(1) == 0)
        def _():
            o_ref[...] = jnp.zeros_like(o_ref)

        o_ref[...] += _dot_tn(a_ref[...], b_ref[...])

    return pl.pallas_call(
        body, name=name, grid=(m // bm, t_len // bk),
        out_shape=jax.ShapeDtypeStruct((m, n), F32),
        in_specs=[pl.BlockSpec((bk, bm), lambda i, k: (k, i)), pl.BlockSpec((bk, n), lambda i, k: (k, 0))],
        out_specs=pl.BlockSpec((bm, n), lambda i, k: (i, 0)),
        compiler_params=_cparams(("arbitrary", "arbitrary")),
    )(a, b)


def _place():
    x, y, c = lax.axis_index("x"), lax.axis_index("y"), lax.axis_index("c")
    chips = [(1 - x, y), (x, 1 - y), (1 - x, 1 - y)]
    return x, y, c, chips


def _row_chunks(rows, limit):
    step = max(d for d in range(16, min(rows, limit) + 1, 16) if rows % d == 0)
    return [slice(i, i + step) for i in range(0, rows, step)]


ICI_CHUNK_ROWS = 256
D2D_CHUNK_ROWS = 256


def _gather_phase(phase, ins, outs, send_sems, recv_sems):
    x, y, c, chips = _place()
    me_chip = 2 * x + y
    sibling = (x, y, 1 - c)

    def copy(w, k, slot, half, to, rows=slice(None), src=None):
        dst = outs[w].at[slot, half, rows]
        return pltpu.make_async_remote_copy(src_ref=dst if src is None else src, dst_ref=dst,
                                            send_sem=send_sems.at[w, k], recv_sem=recv_sems.at[w, k],
                                            device_id=to, device_id_type=MESH)

    for w in range(len(ins)):
        for j, (px, py) in enumerate(chips):
            if phase == 0:
                for rows in _row_chunks(ins[w].shape[1], ICI_CHUNK_ROWS):
                    copy(w, j, me_chip, c, (px, py, c), rows, src=ins[w].at[c, rows]).start()
            elif phase == 1:
                copy(w, j, 2 * px + py, c, (x, y, c)).wait_recv()
                for rows in _row_chunks(ins[w].shape[1], D2D_CHUNK_ROWS):
                    copy(w, 3 + j, 2 * px + py, c, sibling, rows).start()
            else:
                copy(w, 3 + j, 2 * px + py, 1 - c, (x, y, c)).wait_recv()
                copy(w, j, me_chip, c, (px, py, c), src=ins[w].at[c]).wait_send()
                copy(w, 3 + j, 2 * px + py, c, sibling).wait_send()


def _gather_scratch(n_w):
    return [pltpu.SemaphoreType.DMA((n_w, 6)), pltpu.SemaphoreType.DMA((n_w, 6))]


def _all_gather_weights(shards):
    n_w = len(shards)

    def body(*refs):
        for phase in range(3):
            _gather_phase(phase, refs[:n_w], refs[n_w:2 * n_w], *refs[2 * n_w:])

    return pl.pallas_call(
        body, name="all_gather_weights",
        out_shape=tuple(jax.ShapeDtypeStruct((4,) + s.shape, s.dtype) for s in shards),
        in_specs=[ANY] * n_w, out_specs=(ANY,) * n_w, scratch_shapes=_gather_scratch(n_w),
    )(*shards)


def _exchange_phase(phase, ins, theirs, send_sems, recv_sems):
    x, y, c, _ = _place()

    def remote(w, k=slice(None), rows=slice(None)):
        return pltpu.make_async_remote_copy(src_ref=ins[w].at[k, 1 - c, rows], dst_ref=theirs[w].at[k, rows],
                                            send_sem=send_sems.at[w], recv_sem=recv_sems.at[w], device_id=(x, y, 1 - c),
                                            device_id_type=MESH)

    for w in range(len(ins)):
        if phase == 0:
            for k in range(4):
                for rows in _row_chunks(ins[w].shape[2], D2D_CHUNK_ROWS):
                    remote(w, k, rows).start()
        else:
            remote(w).wait()


def _exchange_scratch(n_w):
    return [pltpu.SemaphoreType.DMA((n_w,)), pltpu.SemaphoreType.DMA((n_w,))]


def _exchange_out_shapes(grads):
    return tuple(jax.ShapeDtypeStruct((4,) + g.shape[2:], g.dtype) for g in grads)


def _add_pairs(part, theirs, name, halves):
    _, _, r, c = part.shape
    rb = 64 if r % 64 == 0 else r
    n_w = len(halves)
    n_steps = r // rb

    def body(*refs):
        a_ref, b_ref = refs[:2]
        own_ref, ob_ref = refs[2 + n_w:4 + n_w]
        comm = (refs[2:2 + n_w], refs[4 + n_w:4 + 2 * n_w]) + tuple(refs[4 + 2 * n_w:])
        step = pl.program_id(0)

        @pl.when(step == 0)
        def _():
            _share_phase(0, *comm)

        my_chip = 2 * lax.axis_index("x") + lax.axis_index("y")
        ob_ref[...] = (a_ref[...] + b_ref[...]).astype(BF)
        own_ref[...] = a_ref[my_chip] + b_ref[my_chip]

        @pl.when(step == n_steps - 1)
        def _():
            _share_phase(1, *comm)

    spec = pl.BlockSpec((4, rb, c), lambda i: (0, i, 0))
    flat = pl.pallas_call(
        body, name=name, grid=(n_steps,),
        out_shape=(jax.ShapeDtypeStruct((r, c), F32), jax.ShapeDtypeStruct((4, r, c), BF)) + _share_out_shapes(halves),
        in_specs=[pl.BlockSpec((4, None, rb, c), lambda i: (0, lax.axis_index("c"), i, 0)), spec] + [ANY] * n_w,
        out_specs=(pl.BlockSpec((rb, c), lambda i: (i, 0)), spec) + (ANY,) * n_w,
        scratch_shapes=_share_scratch(n_w), compiler_params=_cparams(("arbitrary",)),
    )(part, theirs, *halves)
    return (flat[0], flat[1]), list(flat[2:])


def _scatter_phase(phase, bfs, got, send_sems, recv_sems):
    x, y, c, chips = _place()

    def remote(w, j, px, py, rows=slice(None)):
        return pltpu.make_async_remote_copy(src_ref=bfs[w].at[2 * px + py, rows], dst_ref=got[w].at[j, rows],
                                            send_sem=send_sems.at[w, j], recv_sem=recv_sems.at[w, j], device_id=(px, py, c),
                                            device_id_type=MESH)

    for w in range(len(bfs)):
        for j, (px, py) in enumerate(chips):
            if phase == 0:
                for rows in _row_chunks(bfs[w].shape[1], ICI_CHUNK_ROWS):
                    remote(w, j, px, py, rows).start()
            else:
                remote(w, j, px, py).wait()


def _scatter_scratch(n_w):
    return [pltpu.SemaphoreType.DMA((n_w, 3)), pltpu.SemaphoreType.DMA((n_w, 3))]


def _scatter_out_shapes(sums_bf16):
    return tuple(jax.ShapeDtypeStruct((3,) + s.shape[1:], BF) for s in sums_bf16)


def _add_received(own, got, name):
    r, c = own.shape
    rb = 128 if r % 128 == 0 else r

    def body(o_ref, g_ref, out_ref):
        out_ref[...] = ((o_ref[...] + g_ref[0].astype(F32)) + g_ref[1].astype(F32)) + g_ref[2].astype(F32)

    return pl.pallas_call(
        body, name=name, grid=(r // rb,), out_shape=jax.ShapeDtypeStruct((r, c), F32),
        in_specs=[pl.BlockSpec((rb, c), lambda i: (i, 0)), pl.BlockSpec((3, rb, c), lambda i: (0, i, 0))],
        out_specs=pl.BlockSpec((rb, c), lambda i: (i, 0)), compiler_params=_cparams(("arbitrary",)),
    )(own, got)


def _share_phase(phase, ins, outs, send_sems, recv_sems):
    x, y, c, _ = _place()

    def remote(w, rows=slice(None)):
        return pltpu.make_async_remote_copy(src_ref=ins[w].at[rows], dst_ref=outs[w].at[c, rows], send_sem=send_sems.at[w],
                                            recv_sem=recv_sems.at[w], device_id=(x, y, 1 - c), device_id_type=MESH)

    for w in range(len(ins)):
        if phase == 0:
            for rows in _row_chunks(ins[w].shape[0], D2D_CHUNK_ROWS):
                remote(w, rows).start()
        else:
            remote(w).wait()


def _share_scratch(n_w):
    return [pltpu.SemaphoreType.DMA((n_w,)), pltpu.SemaphoreType.DMA((n_w,))]


def _share_out_shapes(halves):
    return tuple(jax.ShapeDtypeStruct((2,) + h.shape, h.dtype) for h in halves)


def _share_with_sibling(halves):
    n_w = len(halves)

    def body(*refs):
        for phase in range(2):
            _share_phase(phase, refs[:n_w], refs[n_w:2 * n_w], *refs[2 * n_w:])

    return pl.pallas_call(
        body, name="share_with_sibling", out_shape=_share_out_shapes(halves),
        in_specs=[ANY] * n_w, out_specs=(ANY,) * n_w, scratch_shapes=_share_scratch(n_w),
    )(*halves)


def _small_phase(phase, p_ref, out_ref, slots, send_sems, recv_sems):
    x, y, cc, _ = _place()
    me = 4 * x + 2 * y + cc
    copies = []
    for k in range(1, 8):
        dx, dy, dc = (k >> 2) & 1, (k >> 1) & 1, k & 1
        to = (1 - x if dx else x, 1 - y if dy else y, 1 - cc if dc else cc)
        copies.append(pltpu.make_async_remote_copy(src_ref=p_ref, dst_ref=slots.at[me], send_sem=send_sems.at[k - 1],
                                                   recv_sem=recv_sems.at[k - 1], device_id=to, device_id_type=MESH))
    if phase == 0:
        slots[me] = p_ref[...]
        for cp in copies:
            cp.start()
    else:
        for cp in copies:
            cp.wait()
        total = slots[0]
        for d in range(1, 8):
            total = total + slots[d]
        out_ref[...] = total


def _adamw_update(w_ref, g_ref, m_ref, v_ref, d_ref, nm_ref, nv_ref):
    gv = g_ref[...]
    nm = ADAM_B1 * m_ref[...] + (1.0 - ADAM_B1) * gv
    nv = ADAM_B2 * v_ref[...] + (1.0 - ADAM_B2) * (gv * gv)
    nm_ref[...] = nm
    nv_ref[...] = nv
    m_hat = nm / (1.0 - ADAM_B1 ** ADAM_STEP)
    v_hat = nv / (1.0 - ADAM_B2 ** ADAM_STEP)
    d_ref[...] = -ADAM_LR * (m_hat / (jnp.sqrt(v_hat) + ADAM_EPS) + ADAM_WD * w_ref[...])


def _adamw_many(ws, gs, ms, vs, sums, pack):
    n_a, n_w = len(ws), len(sums)
    n_steps = ADAM_STEPS
    specs = [pl.BlockSpec((w.shape[0] // n_steps, w.shape[1]), lambda i: (i, 0)) for w in ws]
    pack_spec = pl.BlockSpec(pack.shape, lambda i: (0, 0))

    def body(*refs):
        ins = refs[:4 * n_a]
        p_ref = refs[4 * n_a + n_w]
        first_out = 4 * n_a + n_w + 1
        outs = refs[first_out:first_out + 3 * n_a]
        total_ref = refs[first_out + 3 * n_a + n_w]
        scratch = refs[first_out + 3 * n_a + n_w + 1:]
        scatter = (refs[4 * n_a:4 * n_a + n_w], refs[first_out + 3 * n_a:first_out + 3 * n_a + n_w]) + tuple(scratch[:2])
        small = (p_ref, total_ref) + tuple(scratch[2:])
        step = pl.program_id(0)

        @pl.when(step == 0)
        def _():
            _scatter_phase(0, *scatter)
            _small_phase(0, *small)

        for a in range(n_a):
            _adamw_update(*(ins[k * n_a + a] for k in range(4)), *(outs[3 * a + k] for k in range(3)))

        @pl.when(step == n_steps - 1)
        def _():
            _scatter_phase(1, *scatter)
            _small_phase(1, *small)

    flat = pl.pallas_call(
        body, name="adamw_late", grid=(n_steps,),
        out_shape=tuple(jax.ShapeDtypeStruct(w.shape, F32) for w in ws for _ in range(3)) + _scatter_out_shapes(sums)
        + (jax.ShapeDtypeStruct(pack.shape, F32),),
        in_specs=specs * 4 + [ANY] * n_w + [pack_spec],
        out_specs=tuple(s for s in specs for _ in range(3)) + (ANY,) * n_w + (pack_spec,),
        scratch_shapes=_scatter_scratch(n_w) + [pltpu.VMEM((8,) + pack.shape, F32), pltpu.SemaphoreType.DMA((7,)),
                                                pltpu.SemaphoreType.DMA((7,))],
        compiler_params=_cparams(("arbitrary",)),
    )(*ws, *gs, *ms, *vs, *sums, pack)
    return [tuple(flat[3 * a:3 * a + 3]) for a in range(n_a)] + list(flat[3 * n_a:])


def _adamw(w, g, m, v, name):
    r, c = w.shape
    rb, cb = (128, c) if r % 128 == 0 else (r, LANES if (r % 8 and c % LANES == 0) else c)

    def body(*refs):
        _adamw_update(*refs)

    spec = pl.BlockSpec((rb, cb), lambda i, j: (i, j))
    return pl.pallas_call(
        body, name=name, grid=(r // rb, c // cb), out_shape=(jax.ShapeDtypeStruct((r, c), F32),) * 3,
        in_specs=[spec] * 4, out_specs=(spec,) * 3, compiler_params=_cparams(("arbitrary", "arbitrary")),
    )(w, g, m, v)


def _rope_tables(t_len):
    inv_freq = ROPE_BASE ** (-jnp.arange(0, HEAD_DIM, 2, dtype=F32) / HEAD_DIM)
    ang = jnp.arange(t_len, dtype=F32)[:, None] * inv_freq[None, :]
    cos, sin = jnp.cos(ang), jnp.sin(ang)
    cos_t = jnp.concatenate([cos, cos, cos, cos], axis=-1)
    sin_t = jnp.concatenate([-sin, sin, -sin, sin], axis=-1)
    return cos_t, sin_t


def _cols_to_shards(dw):
    r, n = dw.shape
    return jnp.transpose(dw.reshape(2, r // 2, 4, n // 4), (2, 0, 1, 3))


def _rows_to_shards(dw):
    r, n = dw.shape
    rows = r // 4
    if rows % SUBLANES == 0:
        padded = _pad_rows(dw.reshape(4, rows, n))
    else:
        window = rows + SUBLANES - rows % SUBLANES
        padded = _pad_rows(jnp.stack([dw[rows * k // SUBLANES * SUBLANES:][:window] for k in range(4)]))
    return padded.reshape(4, 2, padded.shape[1] // 2, n)


def _shard_row_offset(rows):
    return (rows * (2 * lax.axis_index("x") + lax.axis_index("y"))) % SUBLANES


def _pad_lanes(a):
    extra = -a.shape[-1] % LANES
    return a if extra == 0 else jnp.pad(a, [(0, 0)] * (a.ndim - 1) + [(0, extra)])


def _pad_rows(a):
    rows = a.shape[-2]
    extra = 0 if rows % SHARD_ROW_ALIGN == 0 else -rows % SHARD_ROW_PAD
    return a if extra == 0 else jnp.pad(a, [(0, 0)] * (a.ndim - 2) + [(0, extra), (0, 0)])


def _pad_row(a, width=D_MODEL):
    a = a.reshape(1, -1)
    return jnp.pad(a, ((0, 0), (0, width - a.shape[1])))


def kernel(x, mem, g_mix, w_in, b_forget, g_ret_out, g_fox_q, g_fox_k, w_out, g_xattn, w_xq, w_xkv, g_mem, g_xq, g_xk, w_xo, g_ffn, w_gate, w_up, w_down, loss_target, m_g_mix, m_w_in, m_b_forget, m_g_ret_out, m_g_fox_q, m_g_fox_k, m_w_out, m_g_xattn, m_w_xq, m_w_xkv, m_g_mem, m_g_xq, m_g_xk, m_w_xo, m_g_ffn, m_w_gate, m_w_up, m_w_down, v_g_mix, v_w_in, v_b_forget, v_g_ret_out, v_g_fox_q, v_g_fox_k, v_w_out, v_g_xattn, v_w_xq, v_w_xkv, v_g_mem, v_g_xq, v_g_xk, v_w_xo, v_g_ffn, v_w_gate, v_w_up, v_w_down):
    big = {"w_in": (w_in, m_w_in, v_w_in), "w_out": (w_out, m_w_out, v_w_out), "w_xq": (w_xq, m_w_xq, v_w_xq),
           "w_xkv": (w_xkv, m_w_xkv, v_w_xkv), "w_xo": (w_xo, m_w_xo, v_w_xo), "w_gate": (w_gate, m_w_gate, v_w_gate),
           "w_up": (w_up, m_w_up, v_w_up), "w_down": (w_down, m_w_down, v_w_down)}
    for n in TRANSPOSED:
        big[n] = tuple(jnp.swapaxes(a, 1, 2) for a in big[n])
    shards = {}
    for n in big:
        w = _pad_rows(_pad_lanes(big[n][0][0].astype(BF)))
        shards[n] = w.reshape(2, w.shape[0] // 2, w.shape[1])
    sizes = {n: big[n][0].shape[1:] for n in big}
    w_in_full = _assemble_weight("w_in", _all_gather_weights([shards["w_in"]])[0], shards["w_in"], sizes["w_in"],
                                 tail_rows=MAIN_W + LANES - IN_W)
    small_w ={"g_mix": g_mix, "b_forget": b_forget, "g_ret_out": g_ret_out, "g_fox_q": g_fox_q, "g_fox_k": g_fox_k,
               "g_xattn": g_xattn, "g_mem": g_mem, "g_xq": g_xq, "g_xk": g_xk, "g_ffn": g_ffn}
    m_small = {"g_mix": m_g_mix, "b_forget": m_b_forget, "g_ret_out": m_g_ret_out, "g_fox_q": m_g_fox_q, "g_fox_k": m_g_fox_k,
               "g_xattn": m_g_xattn, "g_mem": m_g_mem, "g_xq": m_g_xq, "g_xk": m_g_xk, "g_ffn": m_g_ffn}
    v_small = {"g_mix": v_g_mix, "b_forget": v_b_forget, "g_ret_out": v_g_ret_out, "g_fox_q": v_g_fox_q, "g_fox_k": v_g_fox_k,
               "g_xattn": v_g_xattn, "g_mem": v_g_mem, "g_xq": v_g_xq, "g_xk": v_g_xk, "g_ffn": v_g_ffn}
    loss_part, grad_x, sums, got, in_parts, small_g = _local_step(x[0], mem[0], loss_target[0], w_in_full, shards, sizes, small_w)
    return _reduce_and_update(big, sums, got, in_parts, small_w, small_g, loss_part, grad_x, m_small, v_small)


def _assemble_weight(name, gathered, own, size, tail_rows=0):
    rows, width = size
    my_chip = 2 * lax.axis_index("x") + lax.axis_index("y")
    g = lax.dynamic_update_slice(gathered, own[None], (my_chip, 0, 0, 0))
    g = g.reshape(4, 2 * g.shape[2], g.shape[3])[:, :rows, :width]
    if tail_rows:
        return jnp.concatenate([g[k] for k in range(4)] + [jnp.zeros((tail_rows, width), g.dtype)])
    return jnp.transpose(g, (1, 0, 2)).reshape(rows, 4 * width) if name in COL_SHARDED else g.reshape(4 * rows, width)


def _shard_parts(names, dw):
    return [_pad_lanes(_cols_to_shards(dw[n]) if n in COL_SHARDED else _rows_to_shards(dw[n])) for n in names]


def _add_pairs_many(parts, theirs, name):
    n_a = len(parts)
    n_steps = min(p.shape[2] for p in parts) // 32
    rb = [p.shape[2] // n_steps for p in parts]
    part_specs = [pl.BlockSpec((4, None, r, p.shape[3]), lambda i: (0, lax.axis_index("c"), i, 0)) for p, r in zip(parts, rb)]
    quad_specs = [pl.BlockSpec((4, r, p.shape[3]), lambda i: (0, i, 0)) for p, r in zip(parts, rb)]
    own_specs = [pl.BlockSpec((r, p.shape[3]), lambda i: (i, 0)) for p, r in zip(parts, rb)]

    def body(*refs):
        my_chip = 2 * lax.axis_index("x") + lax.axis_index("y")
        for a in range(n_a):
            a_ref, b_ref, own_ref, ob_ref = refs[a], refs[n_a + a], refs[2 * n_a + a], refs[3 * n_a + a]
            ob_ref[...] = (a_ref[...] + b_ref[...]).astype(BF)
            own_ref[...] = a_ref[my_chip] + b_ref[my_chip]

    flat = pl.pallas_call(
        body, name=name, grid=(n_steps,),
        out_shape=tuple(jax.ShapeDtypeStruct(p.shape[2:], F32) for p in parts)
        + tuple(jax.ShapeDtypeStruct((4,) + p.shape[2:], BF) for p in parts),
        in_specs=part_specs + quad_specs, out_specs=tuple(own_specs) + tuple(quad_specs),
        compiler_params=_cparams(("arbitrary",)),
    )(*parts, *theirs)
    return [(flat[a], flat[n_a + a]) for a in range(n_a)]


def _core_sums(parts, theirs):
    out = [None] * len(parts)
    for tag, pick in (("a", lambda p: p.shape[2] % LANES == 0), ("b", lambda p: p.shape[2] % LANES != 0)):
        idx = [i for i, p in enumerate(parts) if pick(p)]
        for i, res in zip(idx, _add_pairs_many([parts[i] for i in idx], [theirs[i] for i in idx], f"core_sum_late_{tag}")):
            out[i] = res
    return out


def _local_step(xs, mems, tgt, w_in_full, shards, sizes, small_w):
    g_mix, b_forget, g_ret_out, g_fox_q, g_fox_k = (small_w[n] for n in ("g_mix", "b_forget", "g_ret_out", "g_fox_q", "g_fox_k"))
    g_xattn, g_mem, g_xq, g_xk, g_ffn = (small_w[n] for n in ("g_xattn", "g_mem", "g_xq", "g_xk", "g_ffn"))
    w_in_t = w_in_full
    t_len = xs.shape[0]
    cos_t, sin_t = _rope_tables(t_len)
    tables = _decay_tables(min(RET_BLOCK, t_len))
    gq_t = jnp.concatenate([g_fox_q, g_fox_q], axis=-1)
    gk_t = jnp.concatenate([g_fox_k, g_fox_k], axis=-1)
    b_pad = _pad_row(b_forget, LANES)
    g_ret = g_ret_out.reshape(N_HEADS // 2, 1, LANES)

    n1, proj, rq, rk, q_aug, k_aug, z = _in_proj_fwd(xs, g_mix, w_in_t, b_pad, cos_t, sin_t, gq_t, gk_t)
    raw, mix_r, states = _retention_fwd(rq, rk, proj, g_ret, tables)
    mix_f, o32, lse, *gathered = _fox_fwd(q_aug, k_aug, proj, [shards[n] for n in LATE])
    full = {n: _assemble_weight(n, g, shards[n], sizes[n]) for n, g in zip(LATE, gathered)}
    memn, kraw, kn, vmem = _mem_kv_fwd(mems, g_mem, full["w_xkv"], g_xk)
    h1, hn2, qx, o_x, h2 = _attn_out_xattn_fwd(xs, mix_r, mix_f, full["w_out"], g_xattn, full["w_xq"], g_xq, kn, vmem, full["w_xo"])
    hn3, gate, up, act, dh3, loss_part = _ffn_loss_fwd(h2, g_ffn, full["w_gate"], full["w_up"], full["w_down"], tgt)

    dgate, dup, dh2, dg_ffn = _ffn_bwd(dh3, gate, up, h2, g_ffn, full["w_gate"], full["w_up"], full["w_down"])
    dqx, dh1, dmr, dmf, dkn, dvm, dg_xattn, dg_xq = _attn_out_xattn_bwd(dh2, h1, qx, kn, vmem, full["w_xo"], full["w_xq"],
                                                                      full["w_out"], g_xattn, g_xq)
    dw_xkv, dg_mem, dg_xk = _mem_kv_bwd(dkn, dvm, kraw, mems, memn, g_mem, g_xk, full["w_xkv"])
    dw_gu = _matmul_tn_pair(dgate, dup, hn3, "dw_gate_up")
    dw = {
        "w_out": _matmul_tn_pair(mix_r, mix_f, dh1, "dw_out").reshape(D_MODEL, D_MODEL),
        "w_xq": _matmul_tn(hn2, dqx, "dw_xq"),
        "w_xkv": dw_xkv,
        "w_xo": _matmul_tn(o_x, dh2, "dw_xo"),
        "w_gate": dw_gu[0],
        "w_up": dw_gu[1],
        "w_down": _matmul_tn(act, dh3, "dw_down"),
    }
    late_parts = _shard_parts(LATE, dw)
    dq_r, dk_r, dv_r, drg, dg_ret, *late_theirs = _retention_bwd(dmr, raw, proj, g_ret, rq, rk, states, tables, late_parts)
    late_sums = _core_sums(late_parts, late_theirs)
    dq_f, dk_f, dv_f, df, *late_got = _fox_bwd(q_aug, k_aug, proj, dmf, o32, lse, [s[1] for s in late_sums])
    df_col = jnp.pad(jnp.transpose(df, (1, 0, 2)).reshape(t_len, N_HEADS), ((0, 0), (0, LANES - N_HEADS)))
    dproj, dz, grad_x, dg_mix, dg_fq, dg_fk, db = _in_proj_bwd(xs, g_mix, dh1, dq_r, dk_r, dv_r, drg, dq_f, dk_f, dv_f, df_col,
                                                              proj, z, cos_t, sin_t, gq_t, gk_t, w_in_t)

    dw_in = jnp.concatenate([_matmul_tn(dproj, n1, "dw_in_main"), _matmul_tn(dz, n1, "dw_in_ff")[:IN_W - MAIN_W]], axis=0)
    in_parts = _shard_parts(("w_in",), {"w_in": dw_in})
    sums = {n: s[0] for n, s in zip(LATE, late_sums)}
    got = dict(zip(LATE, late_got))
    small_g = {"g_mix": dg_mix, "b_forget": db[:, :N_HEADS], "g_ret_out": dg_ret, "g_fox_q": dg_fq, "g_fox_k": dg_fk,
               "g_xattn": dg_xattn, "g_mem": dg_mem, "g_xq": dg_xq, "g_xk": dg_xk, "g_ffn": dg_ffn}
    return loss_part, grad_x, sums, got, in_parts, small_g


def _add_received_many(owns, gots, parts):
    n_a, n_w = len(owns), len(parts)
    n_steps = CHIP_SUM_STEPS
    own_specs = [pl.BlockSpec((o.shape[0] // n_steps, o.shape[1]), lambda i: (i, 0)) for o in owns]
    got_specs = [pl.BlockSpec((3, o.shape[0] // n_steps, o.shape[1]), lambda i: (0, i, 0)) for o in owns]

    def body(*refs):
        first_out = 2 * n_a + n_w
        comm = (refs[2 * n_a:first_out], refs[first_out + n_a:first_out + n_a + n_w]) + tuple(refs[first_out + n_a + n_w:])
        step = pl.program_id(0)

        @pl.when(step == 0)
        def _():
            _exchange_phase(0, *comm)

        for a in range(n_a):
            o_ref, g_ref, out_ref = refs[a], refs[n_a + a], refs[first_out + a]
            out_ref[...] = ((o_ref[...] + g_ref[0].astype(F32)) + g_ref[1].astype(F32)) + g_ref[2].astype(F32)

        @pl.when(step == n_steps - 1)
        def _():
            _exchange_phase(1, *comm)

    flat = pl.pallas_call(
        body, name="chip_sum_late", grid=(n_steps,),
        out_shape=tuple(jax.ShapeDtypeStruct(o.shape, F32) for o in owns) + _exchange_out_shapes(parts),
        in_specs=own_specs + got_specs + [ANY] * n_w, out_specs=tuple(own_specs) + (ANY,) * n_w,
        scratch_shapes=_exchange_scratch(n_w), compiler_params=_cparams(("arbitrary",)),
    )(*owns, *gots, *parts)
    return list(flat[:n_a]), list(flat[n_a:])


def _final_grads(names, big, finals, shared):
    my_core = lax.axis_index("c")
    out = {}
    for n, s, fin in zip(names, shared, finals):
        s = lax.dynamic_update_slice(s, fin[None], (my_core, 0, 0))
        s = s.reshape(2 * s.shape[1], s.shape[2])
        rows, width = big[n][0].shape[1:]
        out[n] = s[:rows, :width] if rows % SUBLANES == 0 else lax.dynamic_slice(s, (_shard_row_offset(rows), 0), (rows, width))
    return out


def _reduce_and_update(big, sums, got, in_parts, small_w, small_g, loss_part, grad_x, m_small, v_small):
    small_names = list(small_w)
    pad_rows = SMALL_ROWS - len(small_names) - 1
    stack = lambda d: jnp.concatenate([_pad_row(d[n]) for n in small_names] + [jnp.zeros((pad_rows + 1, D_MODEL), F32)], axis=0)
    g_pack = jnp.concatenate([_pad_row(small_g[n]) for n in small_names] + [_pad_row(loss_part[0:1, 0:1])]
                             + [jnp.zeros((pad_rows, D_MODEL), F32)], axis=0)
    late_finals, in_theirs = _add_received_many([sums[n] for n in LATE], [got[n] for n in LATE], in_parts)
    (in_own, in_bf), late_shared = _add_pairs(in_parts[0], in_theirs[0], "core_sum_w_in", late_finals)
    grads = _final_grads(LATE, big, late_finals, late_shared)
    *late_updates, in_got, g_tot = _adamw_many([big[n][0][0] for n in LATE], [grads[n] for n in LATE], [big[n][1][0] for n in LATE],
                                               [big[n][2][0] for n in LATE], [in_bf], g_pack)
    updates = dict(zip(LATE, late_updates))
    in_final = [_add_received(in_own, in_got, "chip_sum_w_in")]
    grads.update(_final_grads(("w_in",), big, in_final, _share_with_sibling(in_final)))
    updates["w_in"] = _adamw(big["w_in"][0][0], grads["w_in"], big["w_in"][1][0], big["w_in"][2][0], "adamw_w_in")
    deltas, new_m, new_v = {}, {}, {}
    for n in big:
        restore = (lambda a: jnp.swapaxes(a[None], 1, 2)) if n in TRANSPOSED else (lambda a: a[None])
        grads[n] = restore(grads[n])
        deltas[n], new_m[n], new_v[n] = (restore(a) for a in updates[n])

    d_s, m_s, v_s = _adamw(stack(small_w), g_tot, stack(m_small), stack(v_small), "adamw_small")
    for i, n in enumerate(small_names):
        shape = small_w[n].shape
        size = int(np.prod(shape))
        grads[n] = g_tot[i, :size].reshape(shape)
        deltas[n], new_m[n], new_v[n] = d_s[i, :size].reshape(shape), m_s[i, :size].reshape(shape), v_s[i, :size].reshape(shape)
    loss = g_tot[len(small_names), 0]

    order = ["g_mix", "w_in", "b_forget", "g_ret_out", "g_fox_q", "g_fox_k", "w_out", "g_xattn", "w_xq", "w_xkv", "g_mem", "g_xq",
             "g_xk", "w_xo", "g_ffn", "w_gate", "w_up", "w_down"]
    return (loss, grad_x[None], *[grads[n] for n in order], *[deltas[n] for n in order], *[new_m[n] for n in order],
            *[new_v[n] for n in order])
```

```python
import functools

import numpy as np
import jax
import jax.numpy as jnp
from jax import lax
from jax.experimental import pallas as pl
from jax.experimental.pallas import tpu as pltpu

F32 = jnp.float32
BF = jnp.bfloat16

D_MODEL = 1024
HEAD_DIM = 64
N_HEADS = 8
GROUP_W = 512
N_XH = 4
XHD = 256
D_FF = 2816
MAIN_W = 3584
IN_W = 3592
ROPE_BASE = 10000.0
LOG2E = 1.4426950408889634
LN2 = 0.6931471805599453
EPS = 1e-6
NEG = -1e30
LANES = 128
SUBLANES = 8
RET_BLOCK = 256
REF_CHUNK = 64
ROW_TILE = 512
FFN_BWD_TILE = 256
ATT_BLOCK = 256
FWD_GROUP = 4
TN_MAX_ROWS = 1408
SMALL_ROWS = 16
COL_SHARDED = ("w_xkv",)
TRANSPOSED = ("w_in", "w_gate", "w_up")
SHARD_ROW_ALIGN = 32
SHARD_ROW_PAD = 256
LATE = ("w_out", "w_xq", "w_xkv", "w_xo", "w_gate", "w_up", "w_down")
VMEM_LIMIT = 56 * 1024 * 1024

ADAM_LR = 0.001
ADAM_B1 = 0.9
ADAM_B2 = 0.999
ADAM_EPS = 1e-08
ADAM_WD = 0.01
ADAM_STEP = 10
CHIP_SUM_STEPS = 2
ADAM_STEPS = 8

MESH = pl.DeviceIdType.MESH
ANY = pl.BlockSpec(memory_space=pl.ANY)
VMEM_SPEC = pl.BlockSpec(memory_space=pltpu.VMEM)


def _cparams(sem=None, vmem=VMEM_LIMIT):
    return pltpu.CompilerParams(dimension_semantics=sem, vmem_limit_bytes=vmem)


def _dot(a, b):
    return jnp.dot(a.astype(BF), b.astype(BF), preferred_element_type=F32)


def _dot_nt(a, b):
    return lax.dot_general(a.astype(BF), b.astype(BF), (((1,), (1,)), ((), ())), preferred_element_type=F32)


def _dot_tn(a, b):
    return lax.dot_general(a.astype(BF), b.astype(BF), (((0,), (0,)), ((), ())), preferred_element_type=F32)


def _split3(x):
    hi = x.astype(BF)
    r = x - hi.astype(F32)
    mid = r.astype(BF)
    lo = (r - mid.astype(F32)).astype(BF)
    return hi, mid, lo


def _dot_exact(ind, x):
    hi, mid, lo = _split3(x)
    return (jnp.dot(ind, lo, preferred_element_type=F32) + jnp.dot(ind, mid, preferred_element_type=F32)
            + jnp.dot(ind, hi, preferred_element_type=F32))


def _dot_nt_exact(ind, x):
    hi, mid, lo = _split3(x)
    dn = (((1,), (1,)), ((), ()))
    return (lax.dot_general(ind, lo, dn, preferred_element_type=F32) + lax.dot_general(ind, mid, dn, preferred_element_type=F32)
            + lax.dot_general(ind, hi, dn, preferred_element_type=F32))


def _sigmoid(x):
    return 1.0 / (1.0 + jnp.exp(-x))


def _rms_fwd(x, g):
    r = lax.rsqrt(jnp.mean(x * x, axis=-1, keepdims=True) + EPS)
    return x * r * g


def _rms_bwd(x, g, dy):
    r = lax.rsqrt(jnp.mean(x * x, axis=-1, keepdims=True) + EPS)
    xh = x * r
    dg = jnp.sum(dy * xh, axis=0, keepdims=True)
    dxh = dy * g
    dx = r * (dxh - xh * jnp.mean(dxh * xh, axis=-1, keepdims=True))
    return dx, dg


def _group_mean64(x):
    lane = lax.broadcasted_iota(jnp.int32, x.shape, 1)
    lo = lane < HEAD_DIM
    s_lo = jnp.sum(jnp.where(lo, x, 0.0), axis=-1, keepdims=True)
    s_hi = jnp.sum(jnp.where(lo, 0.0, x), axis=-1, keepdims=True)
    return jnp.where(lo, s_lo, s_hi) * (1.0 / HEAD_DIM)


def _swap32(x):
    lane = lax.broadcasted_iota(jnp.int32, x.shape, 1)
    first = (lane % HEAD_DIM) < (HEAD_DIM // 2)
    return jnp.where(first, pltpu.roll(x, LANES - HEAD_DIM // 2, axis=1), pltpu.roll(x, HEAD_DIM // 2, axis=1))


def _chunks(w):
    return [slice(j * LANES, (j + 1) * LANES) for j in range(w // LANES)]


def _aug_pair(qk, f_cols, is_query):
    lane = lax.broadcasted_iota(jnp.int32, qk.shape, 1)
    a = lane - HEAD_DIM
    values = (qk, pltpu.roll(qk, HEAD_DIM, axis=1))
    out = []
    for hh in range(2):
        hi, mid, lo = (p.astype(F32) for p in _split3(f_cols[hh] * LOG2E))
        if is_query:
            aux = jnp.where(a == 0, hi, jnp.where(a == 1, mid, jnp.where(a == 2, lo, jnp.where(a < 6, 1.0, 0.0))))
        else:
            aux = jnp.where(a < 3, 1.0, jnp.where(a == 3, -hi, jnp.where(a == 4, -mid, jnp.where(a == 5, -lo, 0.0))))
        out.append(jnp.where(a < 0, values[hh], aux))
    return jnp.concatenate(out, axis=-1).astype(BF)


def _mem_kv_fwd(mem, g_mem, w_xkv, g_xk):
    m_tok = mem.shape[0]

    def body(mem_ref, gm_ref, w_ref, gk_ref, memn_ref, kraw_ref, kn_ref, v_ref):
        mn = _rms_fwd(mem_ref[...], gm_ref[...]).astype(BF)
        memn_ref[...] = mn
        kv = jnp.dot(mn, w_ref[...], preferred_element_type=F32)
        k = kv[:, :D_MODEL]
        kraw_ref[...] = k
        v_ref[...] = kv[:, D_MODEL:].astype(BF)
        for h in range(N_XH):
            sl = slice(h * XHD, (h + 1) * XHD)
            kn_ref[:, sl] = _rms_fwd(k[:, sl], gk_ref[...]).astype(BF)

    return pl.pallas_call(
        body, name="mem_kv_fwd",
        out_shape=(jax.ShapeDtypeStruct((m_tok, D_MODEL), BF), jax.ShapeDtypeStruct((m_tok, D_MODEL), F32),
                   jax.ShapeDtypeStruct((m_tok, D_MODEL), BF), jax.ShapeDtypeStruct((m_tok, D_MODEL), BF)),
        in_specs=[VMEM_SPEC] * 4, out_specs=(VMEM_SPEC,) * 4, compiler_params=_cparams(),
    )(mem, g_mem, w_xkv, g_xk)


def _in_proj_fwd(x, g_mix, w_in_t, b_pad, cos_t, sin_t, gq_t, gk_t):
    t_len = x.shape[0]
    tm = min(ROW_TILE, t_len)
    n_t = t_len // tm

    def body(x_ref, g_ref, wm_ref, wf_ref, b_ref, cos_ref, sin_ref, gq_ref, gk_ref,
             n1_ref, proj_ref, rq_ref, rk_ref, qa_ref, ka_ref, z_ref, carry):
        i = pl.program_id(0)

        @pl.when(i == 0)
        def _():
            carry[...] = jnp.zeros_like(carry)

        n1 = _rms_fwd(x_ref[...], g_ref[...]).astype(BF)
        n1_ref[...] = n1
        z = _dot_nt(n1, wf_ref[...]) + b_ref[...]
        z_ref[...] = z
        lane = lax.broadcasted_iota(jnp.int32, z.shape, 1)
        lf = jnp.where(lane < N_HEADS, jnp.minimum(z, 0.0) - jnp.log(1.0 + jnp.exp(-jnp.abs(z))), 0.0)
        row = lax.broadcasted_iota(jnp.int32, (tm, tm), 0)
        col = lax.broadcasted_iota(jnp.int32, (tm, tm), 1)
        tri = (row >= col).astype(BF)
        fc = _dot_exact(tri, lf) + carry[0:1, :]
        carry[...] = jnp.broadcast_to(fc[tm - 1:tm, :], carry.shape)
        c, s = cos_ref[...], sin_ref[...]

        def section(n):
            p = _dot_nt(n1, wm_ref[n * GROUP_W:(n + 1) * GROUP_W, :])
            proj_ref[:, n * GROUP_W:(n + 1) * GROUP_W] = p.astype(BF)
            return p

        def rotate(p, out_ref, scale):
            for sl in _chunks(GROUP_W):
                out_ref[:, sl] = ((p[:, sl] * c + _swap32(p[:, sl]) * s) * scale).astype(BF)

        def norm_aug(p, gain, out_ref, scale, is_query):
            for j, sl in enumerate(_chunks(GROUP_W)):
                f = p[:, sl]
                f = f * lax.rsqrt(_group_mean64(f * f) + EPS) * gain * scale
                out_ref[:, 2 * j * LANES:2 * (j + 1) * LANES] = _aug_pair(f, [fc[:, 2 * j:2 * j + 1], fc[:, 2 * j + 1:2 * j + 2]], is_query)

        p_rq, p_rk = section(0), section(1)
        rotate(p_rq, rq_ref, 0.125)
        section(2)
        rotate(p_rk, rk_ref, 1.0)
        section(3)
        p_fq = section(4)
        p_fk = section(5)
        norm_aug(p_fq, gq_ref[...], qa_ref, 0.125 * LOG2E, True)
        section(6)
        norm_aug(p_fk, gk_ref[...], ka_ref, 1.0, False)

    row_spec = lambda w: pl.BlockSpec((tm, w), lambda i: (i, 0))
    full = lambda a: pl.BlockSpec(a.shape, lambda i: (0,) * a.ndim)
    return pl.pallas_call(
        body, name="in_proj_fwd", grid=(n_t,),
        out_shape=(jax.ShapeDtypeStruct((t_len, D_MODEL), BF), jax.ShapeDtypeStruct((t_len, MAIN_W), BF),
                   jax.ShapeDtypeStruct((t_len, GROUP_W), BF), jax.ShapeDtypeStruct((t_len, GROUP_W), BF),
                   jax.ShapeDtypeStruct((t_len, 2 * GROUP_W), BF), jax.ShapeDtypeStruct((t_len, 2 * GROUP_W), BF),
                   jax.ShapeDtypeStruct((t_len, LANES), F32)),
        in_specs=[row_spec(D_MODEL), full(g_mix), *_w_in_specs(), full(b_pad), row_spec(LANES), row_spec(LANES),
                  full(gq_t), full(gk_t)],
        out_specs=(row_spec(D_MODEL), row_spec(MAIN_W), row_spec(GROUP_W), row_spec(GROUP_W), row_spec(2 * GROUP_W),
                   row_spec(2 * GROUP_W), row_spec(LANES)),
        scratch_shapes=[pltpu.VMEM((8, LANES), F32)],
        compiler_params=_cparams(("arbitrary",)),
    )(x, g_mix, w_in_t, w_in_t, b_pad, cos_t, sin_t, gq_t, gk_t)


def _w_in_specs():
    return (pl.BlockSpec((MAIN_W, D_MODEL), lambda i: (0, 0)), pl.BlockSpec((LANES, D_MODEL), lambda i: (MAIN_W // LANES, 0)))


def _matmul_tn_pair(a1, a2, b, name, bk=1024):
    t_len, m = a1.shape
    n = b.shape[1]
    bm = m if m <= TN_MAX_ROWS else m // 2
    bk = min(bk, t_len)

    def body(a1_ref, a2_ref, b_ref, o_ref):
        @pl.when(pl.program_id(1) == 0)
        def _():
            o_ref[...] = jnp.zeros_like(o_ref)

        bv = b_ref[...]
        o_ref[0] += _dot_tn(a1_ref[...], bv)
        o_ref[1] += _dot_tn(a2_ref[...], bv)

    a_spec = pl.BlockSpec((bk, bm), lambda i, k: (k, i))
    return pl.pallas_call(
        body, name=name, grid=(m // bm, t_len // bk),
        out_shape=jax.ShapeDtypeStruct((2, m, n), F32),
        in_specs=[a_spec, a_spec, pl.BlockSpec((bk, n), lambda i, k: (k, 0))],
        out_specs=pl.BlockSpec((2, bm, n), lambda i, k: (0, i, 0)),
        compiler_params=_cparams(("arbitrary", "arbitrary")),
    )(a1, a2, b)


def _decay_tables(c):
    h = np.arange(N_HEADS, dtype=np.float64)
    lg = np.log(1.0 - 2.0 ** (-5.0 - h)).astype(np.float32).astype(np.float64)
    t = np.arange(c)
    same_or_earlier = (t[None, :] // REF_CHUNK) <= (t[:, None] // REF_CHUNK)
    w = np.where(same_or_earlier[None], np.exp(lg[:, None, None] * np.abs(t[:, None] - t[None, :])[None]), 0.0)
    qd = np.exp(lg[:, None] * (t[None, :] + 1.0))
    kd = np.exp(lg[:, None] * (c - 1.0 - t[None, :]))
    cd = np.exp(lg * c)
    ones = np.ones((1, 1, HEAD_DIM))
    return (jnp.asarray(w, F32), jnp.asarray(qd[:, :, None] * ones, F32), jnp.asarray(kd[:, :, None] * ones, F32),
            jnp.asarray(cd[:, None, None] * np.ones((1, HEAD_DIM, HEAD_DIM)), F32))


def _retention_fwd(rq, rk, proj, g_ret, tables):
    t_len = rq.shape[0]
    c = min(RET_BLOCK, t_len)
    n_b = t_len // c
    wdec, qdec, kdec, cdec = tables
    v_col, g_col = 2 * GROUP_W // LANES, 3 * GROUP_W // LANES

    def body(q_ref, k_ref, v_ref, rg_ref, g_ref, w_ref, qd_ref, kd_ref, cd_ref, raw_ref, mix_ref, st_ref, state):
        i = pl.program_id(1)

        @pl.when(i == 0)
        def _():
            state[...] = jnp.zeros_like(state)

        q2, k2, v2 = q_ref[...], k_ref[...], v_ref[...]
        heads = [tuple(t[:, hh * HEAD_DIM:(hh + 1) * HEAD_DIM] for t in (q2, k2, v2)) for hh in range(2)]
        scores = [(_dot_nt(q, k) * w_ref[hh]).astype(BF) for hh, (q, k, _) in enumerate(heads)]
        outs = []
        for hh, (q, k, v) in enumerate(heads):
            sp = state[hh]
            st_ref[0, 0, hh] = sp
            outs.append(jnp.dot(scores[hh], v, preferred_element_type=F32) + _dot(q.astype(F32) * qd_ref[hh], sp))
            state[hh] = sp * cd_ref[hh] + _dot_tn(k.astype(F32) * kd_ref[hh], v)
        o2 = jnp.concatenate(outs, axis=-1)
        raw_ref[...] = o2
        xc = o2 - _group_mean64(o2)
        xh = xc * lax.rsqrt(_group_mean64(xc * xc) + EPS)
        gate = rg_ref[...].astype(F32)
        mix_ref[...] = (gate * _sigmoid(gate) * (xh * g_ref[0])).astype(BF)

    blk = lambda col0: pl.BlockSpec((c, LANES), lambda hp, i: (i, col0 + hp))
    tab = lambda a: pl.BlockSpec((2,) + a.shape[1:], lambda hp, i: (hp, 0, 0))
    return pl.pallas_call(
        body, name="retention_fwd", grid=(N_HEADS // 2, n_b),
        out_shape=(jax.ShapeDtypeStruct((t_len, GROUP_W), F32), jax.ShapeDtypeStruct((t_len, GROUP_W), BF),
                   jax.ShapeDtypeStruct((N_HEADS // 2, n_b, 2, HEAD_DIM, HEAD_DIM), F32)),
        in_specs=[blk(0), blk(0), blk(v_col), blk(g_col), pl.BlockSpec((1, 1, LANES), lambda hp, i: (hp, 0, 0)),
                  tab(wdec), tab(qdec), tab(kdec), tab(cdec)],
        out_specs=(blk(0), blk(0), pl.BlockSpec((1, 1, 2, HEAD_DIM, HEAD_DIM), lambda hp, i: (hp, i, 0, 0, 0))),
        scratch_shapes=[pltpu.VMEM((2, HEAD_DIM, HEAD_DIM), F32)],
        compiler_params=_cparams(("arbitrary", "arbitrary")),
    )(rq, rk, proj, proj, g_ret, wdec, qdec, kdec, cdec)


def _fox_fwd(q_aug, k_aug, proj, shards):
    t_len = q_aug.shape[0]
    tq = min(ATT_BLOCK, t_len)
    nsub = min(FWD_GROUP, t_len // tq)
    tg = nsub * tq
    n_q = t_len // tg
    v_col = 6 * GROUP_W // LANES
    tc = min(512, t_len)
    n_w = len(shards)
    n_steps = (N_HEADS // 2) * n_q

    def body(*refs):
        q_ref, k_ref, v_ref = refs[:3]
        o_ref, o32_ref, lse_ref = refs[3 + n_w:6 + n_w]
        vt = refs[6 + 2 * n_w]
        comm = (refs[3:3 + n_w], refs[6 + n_w:6 + 2 * n_w]) + tuple(refs[7 + 2 * n_w:])
        i = pl.program_id(1)
        step = pl.program_id(0) * n_q + i

        @pl.when(step == 0)
        def _():
            _gather_phase(0, *comm)

        @pl.when(step == (3 * n_steps) // 4)
        def _():
            _gather_phase(1, *comm)

        @pl.when(i == 0)
        def _():
            for c0 in range(0, t_len, tc):
                vt[:, c0:c0 + tc] = v_ref[c0:c0 + tc, :].T

        chains = [(u, hh) for u in range(nsub) for hh in range(2)]
        qs = {(u, hh): q_ref[u * tq:(u + 1) * tq, hh * LANES:(hh + 1) * LANES] for u, hh in chains}
        ones = jnp.ones((HEAD_DIM, tq), BF)

        def scores(j, which):
            k2 = k_ref[pl.ds(pl.multiple_of(j * tq, tq), tq), :]
            return {ch: _dot_nt(k2[:, ch[1] * LANES:(ch[1] + 1) * LANES], qs[ch]) for ch in which}

        def update(j, ss, carry, masked):
            v2 = vt[:, pl.ds(pl.multiple_of(j * tq, tq), tq)]
            ps, stats = {}, {}
            for ch in ss:
                m = carry[ch][0]
                s_t = ss[ch]
                if ch in masked:
                    krow = lax.broadcasted_iota(jnp.int32, (tq, tq), 0)
                    qcol = lax.broadcasted_iota(jnp.int32, (tq, tq), 1)
                    s_t = jnp.where(qcol >= krow, s_t, NEG)
                m_new = jnp.maximum(m, jnp.max(s_t, axis=0, keepdims=True))
                ps[ch] = jnp.exp2(s_t - m_new).astype(BF)
                stats[ch] = (m_new, jnp.exp2(m - m_new))
            out = dict(carry)
            for ch in ss:
                m_new, alpha = stats[ch]
                v_aug = jnp.concatenate([v2[ch[1] * HEAD_DIM:(ch[1] + 1) * HEAD_DIM, :], ones], axis=0)
                out[ch] = (m_new, carry[ch][1] * alpha + jnp.dot(v_aug, ps[ch], preferred_element_type=F32))
            return out

        def advance(j, state):
            ss, carry = state
            return scores(j + 1, chains), update(j, ss, carry, ())

        init = {ch: (jnp.full((1, tq), NEG, F32), jnp.zeros((LANES, tq), F32)) for ch in chains}
        first = nsub * i
        ss, carry = lax.fori_loop(0, first, advance, (scores(0, chains), init))
        carry = update(first, ss, carry, [(0, 0), (0, 1)])
        for u in range(1, nsub):
            rest = [(uu, hh) for uu in range(u, nsub) for hh in range(2)]
            carry = update(first + u, scores(first + u, rest), carry, [(u, 0), (u, 1)])
        for u in range(nsub):
            outs, lses = [], []
            for hh in range(2):
                m, acc = carry[u, hh]
                l = acc[HEAD_DIM:HEAD_DIM + 1, :]
                outs.append(acc[:HEAD_DIM, :] / l)
                lses.append(m + jnp.log2(l))
            o2 = jnp.concatenate(outs, axis=0).T
            o32_ref[u * tq:(u + 1) * tq, :] = o2
            o_ref[u * tq:(u + 1) * tq, :] = o2.astype(BF)
            lse_ref[0, :, u * tq:(u + 1) * tq] = jnp.concatenate(lses, axis=0)

        @pl.when(step == n_steps - 1)
        def _():
            _gather_phase(2, *comm)

    return pl.pallas_call(
        body, name="fox_fwd", grid=(N_HEADS // 2, n_q),
        out_shape=(jax.ShapeDtypeStruct((t_len, GROUP_W), BF), jax.ShapeDtypeStruct((t_len, GROUP_W), F32),
                   jax.ShapeDtypeStruct((N_HEADS // 2, 2, t_len), F32))
        + tuple(jax.ShapeDtypeStruct((4,) + s.shape, s.dtype) for s in shards),
        in_specs=[pl.BlockSpec((tg, 2 * LANES), lambda hp, i: (i, hp)),
                  pl.BlockSpec((t_len, 2 * LANES), lambda hp, i: (0, hp)),
                  pl.BlockSpec((t_len, LANES), lambda hp, i: (0, v_col + hp))] + [ANY] * n_w,
        out_specs=(pl.BlockSpec((tg, LANES), lambda hp, i: (i, hp)), pl.BlockSpec((tg, LANES), lambda hp, i: (i, hp)),
                   pl.BlockSpec((1, 2, tg), lambda hp, i: (hp, 0, i))) + (ANY,) * n_w,
        scratch_shapes=[pltpu.VMEM((LANES, t_len), BF)] + _gather_scratch(n_w),
        compiler_params=_cparams(("arbitrary", "arbitrary")),
    )(q_aug, k_aug, proj, *shards)


def _softmax_rows(s):
    p = jnp.exp(s - jnp.max(s, axis=-1, keepdims=True))
    return p / jnp.sum(p, axis=-1, keepdims=True)


def _attn_out_xattn_fwd(x, mix_r, mix_f, w_out, g_xattn, w_xq, g_xq, kn, v, w_xo):
    t_len = x.shape[0]
    tm = min(ROW_TILE, t_len)

    def body(x_ref, mr_ref, mf_ref, wo_ref, g_ref, wq_ref, gq_ref, kn_ref, v_ref, wxo_ref,
             h1_ref, hn_ref, qx_ref, o_ref, h2_ref):
        h1 = x_ref[...] + jnp.dot(mr_ref[...], wo_ref[:GROUP_W, :], preferred_element_type=F32) \
            + jnp.dot(mf_ref[...], wo_ref[GROUP_W:, :], preferred_element_type=F32)
        h1_ref[...] = h1
        hn = _rms_fwd(h1, g_ref[...]).astype(BF)
        hn_ref[...] = hn
        qx = jnp.dot(hn, wq_ref[...], preferred_element_type=F32).astype(BF)
        qx_ref[...] = qx
        sls = [slice(h * XHD, (h + 1) * XHD) for h in range(N_XH)]
        qns = [_rms_fwd(qx[:, sl].astype(F32), gq_ref[...]).astype(BF) for sl in sls]
        logits = [_dot_nt(qn, kn_ref[:, sl]) * (XHD ** -0.5) for qn, sl in zip(qns, sls)]
        ps = [_softmax_rows(s).astype(BF) for s in logits]
        for p, sl in zip(ps, sls):
            o_ref[:, sl] = jnp.dot(p, v_ref[:, sl], preferred_element_type=F32).astype(BF)
        h2_ref[...] = h1 + jnp.dot(o_ref[...], wxo_ref[...], preferred_element_type=F32)

    row_spec = lambda w: pl.BlockSpec((tm, w), lambda i: (i, 0))
    full = lambda a: pl.BlockSpec(a.shape, lambda i: (0,) * a.ndim)
    return pl.pallas_call(
        body, name="attn_out_xattn_fwd", grid=(t_len // tm,),
        out_shape=(jax.ShapeDtypeStruct((t_len, D_MODEL), F32), jax.ShapeDtypeStruct((t_len, D_MODEL), BF),
                   jax.ShapeDtypeStruct((t_len, D_MODEL), BF), jax.ShapeDtypeStruct((t_len, D_MODEL), BF),
                   jax.ShapeDtypeStruct((t_len, D_MODEL), F32)),
        in_specs=[row_spec(D_MODEL), row_spec(GROUP_W), row_spec(GROUP_W), full(w_out), full(g_xattn), full(w_xq), full(g_xq),
                  full(kn), full(v), full(w_xo)],
        out_specs=(row_spec(D_MODEL),) * 5,
        compiler_params=_cparams(("arbitrary",)),
    )(x, mix_r, mix_f, w_out, g_xattn, w_xq, g_xq, kn, v, w_xo)


def _ffn_loss_fwd(h2, g_ffn, w_gate, w_up, w_down, target):
    t_len = h2.shape[0]
    tm = min(ROW_TILE, t_len)

    def body(h2_ref, g_ref, wg_ref, wu_ref, wd_ref, tgt_ref, hn_ref, gate_ref, up_ref, act_ref, dh3_ref, loss_ref):
        @pl.when(pl.program_id(0) == 0)
        def _():
            loss_ref[...] = jnp.zeros_like(loss_ref)

        h2v = h2_ref[...]
        hn = _rms_fwd(h2v, g_ref[...]).astype(BF)
        hn_ref[...] = hn
        gate = _dot_nt(hn, wg_ref[...])
        up = _dot_nt(hn, wu_ref[...])
        gate_ref[...] = gate.astype(BF)
        up_ref[...] = up.astype(BF)
        act = (gate * _sigmoid(gate) * up).astype(BF)
        act_ref[...] = act
        diff = h2v + jnp.dot(act, wd_ref[...], preferred_element_type=F32) - tgt_ref[...]
        dh3_ref[...] = diff * (1.0 / D_MODEL)
        per_row = jnp.sum(diff * diff, axis=-1, keepdims=True) * (1.0 / D_MODEL)
        loss_ref[...] += 0.5 * jnp.sum(per_row, axis=0, keepdims=True)

    row_spec = lambda w: pl.BlockSpec((tm, w), lambda i: (i, 0))
    full = lambda a: pl.BlockSpec(a.shape, lambda i: (0,) * a.ndim, pipeline_mode=pl.Buffered(1))
    return pl.pallas_call(
        body, name="ffn_loss_fwd", grid=(t_len // tm,),
        out_shape=(jax.ShapeDtypeStruct((t_len, D_MODEL), BF), jax.ShapeDtypeStruct((t_len, D_FF), BF),
                   jax.ShapeDtypeStruct((t_len, D_FF), BF), jax.ShapeDtypeStruct((t_len, D_FF), BF),
                   jax.ShapeDtypeStruct((t_len, D_MODEL), F32), jax.ShapeDtypeStruct((8, LANES), F32)),
        in_specs=[row_spec(D_MODEL), full(g_ffn), full(w_gate), full(w_up), full(w_down), row_spec(D_MODEL)],
        out_specs=(row_spec(D_MODEL), row_spec(D_FF), row_spec(D_FF), row_spec(D_FF), row_spec(D_MODEL),
                   pl.BlockSpec((8, LANES), lambda i: (0, 0))),
        compiler_params=_cparams(("arbitrary",)),
    )(h2, g_ffn, w_gate, w_up, w_down, target)


def _ffn_bwd(dh3, gate, up, h2, g_ffn, w_gate, w_up, w_down):
    t_len = h2.shape[0]
    tm = min(FFN_BWD_TILE, t_len)

    def body(dh3_ref, gate_ref, up_ref, h2_ref, g_ref, wg_ref, wu_ref, wd_ref, dgate_ref, dup_ref, dh2_ref, dg_ref):
        @pl.when(pl.program_id(0) == 0)
        def _():
            dg_ref[...] = jnp.zeros_like(dg_ref)

        dh3v = dh3_ref[...]
        dact = _dot_nt(dh3v, wd_ref[...])
        g = gate_ref[...].astype(F32)
        sg = _sigmoid(g)
        dup = (dact * (g * sg)).astype(BF)
        dgate = (dact * up_ref[...].astype(F32) * (sg * (1.0 + g * (1.0 - sg)))).astype(BF)
        dup_ref[...] = dup
        dgate_ref[...] = dgate
        dhn = jnp.dot(dgate, wg_ref[...], preferred_element_type=F32) + jnp.dot(dup, wu_ref[...], preferred_element_type=F32)
        dx, dg = _rms_bwd(h2_ref[...], g_ref[...], dhn)
        dh2_ref[...] = dh3v + dx
        dg_ref[...] += dg

    row_spec = lambda w: pl.BlockSpec((tm, w), lambda i: (i, 0))
    full = lambda a: pl.BlockSpec(a.shape, lambda i: (0,) * a.ndim, pipeline_mode=pl.Buffered(1))
    return pl.pallas_call(
        body, name="ffn_bwd", grid=(t_len // tm,),
        out_shape=(jax.ShapeDtypeStruct((t_len, D_FF), BF), jax.ShapeDtypeStruct((t_len, D_FF), BF),
                   jax.ShapeDtypeStruct((t_len, D_MODEL), F32), jax.ShapeDtypeStruct((1, D_MODEL), F32)),
        in_specs=[row_spec(D_MODEL), row_spec(D_FF), row_spec(D_FF), row_spec(D_MODEL), full(g_ffn), full(w_gate), full(w_up),
                  full(w_down)],
        out_specs=(row_spec(D_FF), row_spec(D_FF), row_spec(D_MODEL), pl.BlockSpec((1, D_MODEL), lambda i: (0, 0))),
        compiler_params=_cparams(("arbitrary",)),
    )(dh3, gate, up, h2, g_ffn, w_gate, w_up, w_down)


def _attn_out_xattn_bwd(dh2, h1, qx, kn, v, w_xo, w_xq, w_out, g_xattn, g_xq):
    t_len = h1.shape[0]
    tm = min(ROW_TILE, t_len)
    m_tok = kn.shape[0]

    def body(dh2_ref, h1_ref, qx_ref, kn_ref, v_ref, wxo_ref, wq_ref, wo_ref, g_ref, gq_ref,
             dqx_ref, dh1_ref, dmr_ref, dmf_ref, dkn_ref, dv_ref, dg_ref, dgq_ref, dqx_scr):
        @pl.when(pl.program_id(0) == 0)
        def _():
            dkn_ref[...] = jnp.zeros_like(dkn_ref)
            dv_ref[...] = jnp.zeros_like(dv_ref)
            dg_ref[...] = jnp.zeros_like(dg_ref)
            dgq_ref[...] = jnp.zeros_like(dgq_ref)

        dh2v = dh2_ref[...]
        do = _dot_nt(dh2v, wxo_ref[...])
        gq = gq_ref[...]
        sls = [slice(h * XHD, (h + 1) * XHD) for h in range(N_XH)]
        qraws = [qx_ref[:, sl].astype(F32) for sl in sls]
        qns = [_rms_fwd(qraw, gq).astype(BF) for qraw in qraws]
        dohs = [do[:, sl].astype(BF) for sl in sls]
        logits = [_dot_nt(qn, kn_ref[:, sl]) * (XHD ** -0.5) for qn, sl in zip(qns, sls)]
        dps = [_dot_nt(doh, v_ref[:, sl]) for doh, sl in zip(dohs, sls)]
        ps = [_softmax_rows(s) for s in logits]
        dss = [(p * (dp - jnp.sum(dp * p, axis=-1, keepdims=True)) * (XHD ** -0.5)).astype(BF) for p, dp in zip(ps, dps)]
        dqns = []
        for h, sl in enumerate(sls):
            dv_ref[:, sl] += _dot_tn(ps[h], dohs[h])
            dqns.append(jnp.dot(dss[h], kn_ref[:, sl], preferred_element_type=F32))
            dkn_ref[:, sl] += _dot_tn(dss[h], qns[h])
        dgq = jnp.zeros((1, XHD), F32)
        for h, sl in enumerate(sls):
            dx, dg_h = _rms_bwd(qraws[h], gq, dqns[h])
            dgq = dgq + dg_h
            dqx_scr[:, sl] = dx.astype(BF)
        dgq_ref[...] += dgq
        dqx = dqx_scr[...]
        dqx_ref[...] = dqx
        dhn = _dot_nt(dqx, wq_ref[...])
        dx, dg = _rms_bwd(h1_ref[...], g_ref[...], dhn)
        dg_ref[...] += dg
        dh1 = dh2v + dx
        dh1_ref[...] = dh1
        dmix = _dot_nt(dh1, wo_ref[...])
        dmr_ref[...] = dmix[:, :GROUP_W]
        dmf_ref[...] = dmix[:, GROUP_W:].astype(BF)

    row_spec = lambda w: pl.BlockSpec((tm, w), lambda i: (i, 0))
    full = lambda a: pl.BlockSpec(a.shape, lambda i: (0,) * a.ndim)
    acc = lambda r, c: pl.BlockSpec((r, c), lambda i: (0, 0))
    return pl.pallas_call(
        body, name="attn_out_xattn_bwd", grid=(t_len // tm,),
        out_shape=(jax.ShapeDtypeStruct((t_len, D_MODEL), BF), jax.ShapeDtypeStruct((t_len, D_MODEL), F32),
                   jax.ShapeDtypeStruct((t_len, GROUP_W), F32), jax.ShapeDtypeStruct((t_len, GROUP_W), BF),
                   jax.ShapeDtypeStruct((m_tok, D_MODEL), F32), jax.ShapeDtypeStruct((m_tok, D_MODEL), F32),
                   jax.ShapeDtypeStruct((1, D_MODEL), F32), jax.ShapeDtypeStruct((1, XHD), F32)),
        in_specs=[row_spec(D_MODEL), row_spec(D_MODEL), row_spec(D_MODEL), full(kn), full(v), full(w_xo), full(w_xq), full(w_out),
                  full(g_xattn), full(g_xq)],
        out_specs=(row_spec(D_MODEL), row_spec(D_MODEL), row_spec(GROUP_W), row_spec(GROUP_W), acc(m_tok, D_MODEL),
                   acc(m_tok, D_MODEL), acc(1, D_MODEL), acc(1, XHD)),
        scratch_shapes=[pltpu.VMEM((tm, D_MODEL), BF)],
        compiler_params=_cparams(("arbitrary",)),
    )(dh2, h1, qx, kn, v, w_xo, w_xq, w_out, g_xattn, g_xq)


def _mem_kv_bwd(dkn, dv, kraw, mem, memn, g_mem, g_xk, w_xkv):
    m_tok = mem.shape[0]

    def body(dkn_ref, dv_ref, kraw_ref, mem_ref, memn_ref, gm_ref, gk_ref, w_ref, dw_ref, dgm_ref, dgk_ref, dkv_scr):
        gk = gk_ref[...]
        dgk = jnp.zeros((1, XHD), F32)
        for h in range(N_XH):
            sl = slice(h * XHD, (h + 1) * XHD)
            dx, dg_h = _rms_bwd(kraw_ref[:, sl], gk, dkn_ref[:, sl])
            dgk = dgk + dg_h
            dkv_scr[:, sl] = dx.astype(BF)
        dgk_ref[...] = dgk
        dkv_scr[:, D_MODEL:] = dv_ref[...].astype(BF)
        dkv = dkv_scr[...]
        dw_ref[...] = _dot_tn(memn_ref[...], dkv)
        dmemn = _dot_nt(dkv, w_ref[...])
        mem_v = mem_ref[...]
        r = lax.rsqrt(jnp.mean(mem_v * mem_v, axis=-1, keepdims=True) + EPS)
        dgm_ref[...] = jnp.sum(dmemn * mem_v * r, axis=0, keepdims=True)

    return pl.pallas_call(
        body, name="mem_kv_bwd",
        out_shape=(jax.ShapeDtypeStruct((D_MODEL, 2 * D_MODEL), F32), jax.ShapeDtypeStruct((1, D_MODEL), F32),
                   jax.ShapeDtypeStruct((1, XHD), F32)),
        in_specs=[VMEM_SPEC] * 8, out_specs=(VMEM_SPEC,) * 3,
        scratch_shapes=[pltpu.VMEM((m_tok, 2 * D_MODEL), BF)],
        compiler_params=_cparams(),
    )(dkn, dv, kraw, mem, memn, g_mem, g_xk, w_xkv)


def _fox_bwd(q_aug, k_aug, proj, dmf, o32, lse, sums):
    t_len = q_aug.shape[0]
    tb = min(ATT_BLOCK, t_len)
    n_b = t_len // tb
    nsub = 2 if n_b >= 2 else 1
    tg = nsub * tb
    n_g = t_len // tg
    v_col = 6 * GROUP_W // LANES
    n_w = len(sums)
    n_steps = (N_HEADS // 2) * n_g

    def body(*refs):
        k_ref, v_ref, q_ref, do_ref, o_ref, lse_ref = refs[:6]
        dq_ref, dk_ref, dv_ref, df_ref = refs[6 + n_w:10 + n_w]
        delta = refs[10 + 2 * n_w]
        comm = (refs[6:6 + n_w], refs[10 + n_w:10 + 2 * n_w]) + tuple(refs[11 + 2 * n_w:])
        j = pl.program_id(1)
        step = pl.program_id(0) * n_g + j

        @pl.when(step == 0)
        def _():
            _scatter_phase(0, *comm)

        @pl.when(j == 0)
        def _():
            dq_ref[...] = jnp.zeros_like(dq_ref)
            dd = do_ref[...].astype(F32) * o_ref[...]
            hrow = lax.broadcasted_iota(jnp.int32, (8, LANES), 0)
            lane = lax.broadcasted_iota(jnp.int32, (8, LANES), 1)
            ind = ((lane // HEAD_DIM) == hrow).astype(BF)
            delta[...] = _dot_nt_exact(ind, dd)

        k2, v2 = k_ref[...], v_ref[...]
        chains = [(u, hh) for u in range(nsub) for hh in range(2)]
        ks = {(u, hh): k2[u * tb:(u + 1) * tb, hh * LANES:(hh + 1) * LANES] for u, hh in chains}
        vs = {(u, hh): v2[u * tb:(u + 1) * tb, hh * HEAD_DIM:(hh + 1) * HEAD_DIM] for u, hh in chains}

        def block(i, carry, which, masked):
            rows = pl.ds(pl.multiple_of(i * tb, tb), tb)
            q2 = q_ref[rows, :]
            do2 = do_ref[rows, :]
            qs = [q2[:, hh * LANES:(hh + 1) * LANES] for hh in range(2)]
            dos = [do2[:, hh * HEAD_DIM:(hh + 1) * HEAD_DIM] for hh in range(2)]
            ss = {ch: _dot_nt(ks[ch], qs[ch[1]]) for ch in which}
            dps = {ch: _dot_nt(vs[ch], dos[ch[1]]) for ch in which}
            pts, dsts, dfs = {}, {}, {}
            for ch in which:
                hh = ch[1]
                s_t = ss[ch]
                if ch in masked:
                    krow = lax.broadcasted_iota(jnp.int32, (tb, tb), 0)
                    qcol = lax.broadcasted_iota(jnp.int32, (tb, tb), 1)
                    s_t = jnp.where(qcol >= krow, s_t, NEG)
                p_t = jnp.exp2(s_t - lse_ref[0, hh:hh + 1, rows])
                pts[ch] = p_t.astype(BF)
                ds_t = p_t * (dps[ch] - delta[hh:hh + 1, rows])
                dsts[ch] = ds_t.astype(BF)
                dfs[ch] = jnp.sum(ds_t, axis=-1, keepdims=True)
            out = dict(carry)
            for ch in which:
                dk, dv, df = carry[ch]
                dv = dv + jnp.dot(pts[ch], dos[ch[1]], preferred_element_type=F32)
                dk = dk + jnp.dot(dsts[ch], qs[ch[1]], preferred_element_type=F32)
                out[ch] = (dk, dv, df - dfs[ch])
            for hh in range(2):
                parts_dq = [_dot_tn(dsts[ch], ks[ch])[:, :HEAD_DIM] for ch in which if ch[1] == hh]
                dq_ref[rows, hh * HEAD_DIM:(hh + 1) * HEAD_DIM] += sum(parts_dq[1:], parts_dq[0])
            return out

        init = {ch: (jnp.zeros((tb, LANES), F32), jnp.zeros((tb, HEAD_DIM), F32), jnp.zeros((tb, 1), F32)) for ch in chains}
        first = nsub * j
        carry = block(first, init, [(0, 0), (0, 1)], [(0, 0), (0, 1)])
        if nsub == 2:
            carry = block(first + 1, carry, chains, [(1, 0), (1, 1)])
        carry = lax.fori_loop(first + nsub, n_b, lambda i, c: block(i, c, chains, ()), carry)
        for u in range(nsub):
            rs = slice(u * tb, (u + 1) * tb)
            dk_ref[rs, :] = jnp.concatenate([carry[u, hh][0][:, :HEAD_DIM] for hh in range(2)], axis=-1) * LN2
            dv_ref[rs, :] = jnp.concatenate([carry[u, hh][1] for hh in range(2)], axis=-1)
            df_ref[0, rs, :] = jnp.concatenate([carry[u, hh][2] for hh in range(2)], axis=-1)

        @pl.when(step == n_steps - 1)
        def _():
            _scatter_phase(1, *comm)

    blk = lambda w, col0: pl.BlockSpec((tg, w), lambda hp, j: (j, col0 + hp))
    whole = lambda w: pl.BlockSpec((t_len, w), lambda hp, j: (0, hp))
    rows2 = pl.BlockSpec((1, 2, t_len), lambda hp, j: (hp, 0, 0))
    cols2 = pl.BlockSpec((1, tg, 2), lambda hp, j: (hp, j, 0))
    return pl.pallas_call(
        body, name="fox_bwd", grid=(N_HEADS // 2, n_g),
        out_shape=(jax.ShapeDtypeStruct((t_len, GROUP_W), F32), jax.ShapeDtypeStruct((t_len, GROUP_W), F32),
                   jax.ShapeDtypeStruct((t_len, GROUP_W), F32), jax.ShapeDtypeStruct((N_HEADS // 2, t_len, 2), F32))
        + _scatter_out_shapes(sums),
        in_specs=[blk(2 * LANES, 0), blk(LANES, v_col), whole(2 * LANES), whole(LANES), whole(LANES), rows2] + [ANY] * n_w,
        out_specs=(whole(LANES), blk(LANES, 0), blk(LANES, 0), cols2) + (ANY,) * n_w,
        scratch_shapes=[pltpu.VMEM((8, t_len), F32)] + _scatter_scratch(n_w),
        compiler_params=_cparams(("arbitrary", "arbitrary")),
    )(k_aug, proj, q_aug, dmf, o32, lse, *sums)


def _retention_bwd(dmr, raw, proj, g_ret, rq, rk, states, tables, parts):
    t_len = rq.shape[0]
    c = min(RET_BLOCK, t_len)
    n_b = t_len // c
    wdec, qdec, kdec, cdec = tables
    v_col, g_col = 2 * GROUP_W // LANES, 3 * GROUP_W // LANES
    n_w = len(parts)
    n_steps = (N_HEADS // 2) * n_b

    def body(*refs):
        d_ref, raw_ref, rg_ref, g_ref, q_ref, k_ref, v_ref, st_ref, w_ref, wt_ref, qd_ref, kd_ref, cd_ref = refs[:13]
        dq_ref, dk_ref, dv_ref, drg_ref, dg_ref = refs[13 + n_w:18 + n_w]
        gstate = refs[18 + 2 * n_w]
        comm = (refs[13:13 + n_w], refs[18 + n_w:18 + 2 * n_w]) + tuple(refs[19 + 2 * n_w:])
        step = pl.program_id(0) * n_b + pl.program_id(1)

        @pl.when(step == 0)
        def _():
            _exchange_phase(0, *comm)

        @pl.when(pl.program_id(1) == 0)
        def _():
            gstate[...] = jnp.zeros_like(gstate)
            dg_ref[...] = jnp.zeros_like(dg_ref)

        d, raw_v, g = d_ref[...], raw_ref[...], g_ref[0]
        gate = rg_ref[...].astype(F32)
        xc = raw_v - _group_mean64(raw_v)
        r = lax.rsqrt(_group_mean64(xc * xc) + EPS)
        xh = xc * r
        sg = _sigmoid(gate)
        drg_ref[...] = d * (xh * g) * (sg * (1.0 + gate * (1.0 - sg)))
        dy = d * (gate * sg)
        dg_ref[0] += jnp.sum(dy * xh, axis=0, keepdims=True)
        dxh = dy * g
        do2 = r * (dxh - _group_mean64(dxh) - xh * _group_mean64(dxh * xh))
        q2, k2, v2 = q_ref[...], k_ref[...], v_ref[...]
        dqs, dks, dvs = [], [], []
        heads = [tuple(t[:, hh * HEAD_DIM:(hh + 1) * HEAD_DIM] for t in (q2, k2, v2, do2.astype(BF))) for hh in range(2)]
        firsts = [(_dot_nt(k, q) * wt_ref[hh], _dot_nt(do, v) * w_ref[hh], _dot_nt(v, do) * wt_ref[hh])
                  for hh, (q, k, v, do) in enumerate(heads)]
        for hh, (q, k, v, do) in enumerate(heads):
            a_t, dm, dm_t = firsts[hh]
            sp, gs = st_ref[0, 0, hh], gstate[hh]
            qd = q.astype(F32) * qd_ref[hh]
            kd = k.astype(F32) * kd_ref[hh]
            dqs.append(_dot(dm, k) + _dot_nt(do, sp) * qd_ref[hh])
            dks.append(_dot(dm_t, q) + _dot_nt(v, gs) * kd_ref[hh])
            dvs.append(_dot(a_t, do) + _dot(kd, gs))
            gstate[hh] = gs * cd_ref[hh] + _dot_tn(qd, do)
        dq_ref[...] = jnp.concatenate(dqs, axis=-1)
        dk_ref[...] = jnp.concatenate(dks, axis=-1)
        dv_ref[...] = jnp.concatenate(dvs, axis=-1)

        @pl.when(step == n_steps - 1)
        def _():
            _exchange_phase(1, *comm)

    blk = lambda col0: pl.BlockSpec((c, LANES), lambda hp, i: (n_b - 1 - i, col0 + hp))
    tab = lambda a: pl.BlockSpec((2,) + a.shape[1:], lambda hp, i: (hp, 0, 0))
    gspec = pl.BlockSpec((1, 1, LANES), lambda hp, i: (hp, 0, 0))
    return pl.pallas_call(
        body, name="retention_bwd", grid=(N_HEADS // 2, n_b),
        out_shape=(jax.ShapeDtypeStruct((t_len, GROUP_W), F32),) * 4 + (jax.ShapeDtypeStruct((N_HEADS // 2, 1, LANES), F32),)
        + _exchange_out_shapes(parts),
        in_specs=[blk(0), blk(0), blk(g_col), gspec, blk(0), blk(0), blk(v_col),
                  pl.BlockSpec((1, 1, 2, HEAD_DIM, HEAD_DIM), lambda hp, i: (hp, n_b - 1 - i, 0, 0, 0)),
                  tab(wdec), tab(wdec), tab(qdec), tab(kdec), tab(cdec)] + [ANY] * n_w,
        out_specs=(blk(0), blk(0), blk(0), blk(0), gspec) + (ANY,) * n_w,
        scratch_shapes=[pltpu.VMEM((2, HEAD_DIM, HEAD_DIM), F32)] + _exchange_scratch(n_w),
        compiler_params=_cparams(("arbitrary", "arbitrary")),
    )(dmr, raw, proj, g_ret, rq, rk, proj, states, wdec, jnp.transpose(wdec, (0, 2, 1)), qdec, kdec, cdec, *parts)


def _in_proj_bwd(x, g_mix, dh1, dq_r, dk_r, dv_r, drg, dq_f, dk_f, dv_f, df_col, proj, z, cos_t, sin_t, gq_t, gk_t, w_in_t):
    t_len = x.shape[0]
    tm = min(ROW_TILE, t_len)
    n_t = t_len // tm

    def body(x_ref, g_ref, dh1_ref, dqr_ref, dkr_ref, dvr_ref, drg_ref, dqf_ref, dkf_ref, dvf_ref, df_ref, fq_ref, fk_ref, z_ref,
             cos_ref, sin_ref, gq_ref, gk_ref, wm_ref, wf_ref,
             dproj_ref, dz_ref, dx_ref, dg_ref, dgq_ref, dgk_ref, db_ref, carry, gq_acc, gk_acc):
        i = pl.program_id(0)

        @pl.when(i == 0)
        def _():
            carry[...] = jnp.zeros_like(carry)
            gq_acc[...] = jnp.zeros_like(gq_acc)
            gk_acc[...] = jnp.zeros_like(gk_acc)
            dg_ref[...] = jnp.zeros_like(dg_ref)
            db_ref[...] = jnp.zeros_like(db_ref)

        c, s = cos_ref[...], sin_ref[...]
        gq, gk = gq_ref[...], gk_ref[...]
        dgq = jnp.zeros((1, LANES), F32)
        dgk = jnp.zeros((1, LANES), F32)
        for sl in _chunks(GROUP_W):
            dy = dqr_ref[:, sl] * 0.125
            dproj_ref[:, sl] = (dy * c + _swap32(dy * s)).astype(BF)
            dy = dkr_ref[:, sl]
            dproj_ref[:, GROUP_W + sl.start:GROUP_W + sl.stop] = (dy * c + _swap32(dy * s)).astype(BF)
            dproj_ref[:, 2 * GROUP_W + sl.start:2 * GROUP_W + sl.stop] = dvr_ref[:, sl].astype(BF)
            dproj_ref[:, 3 * GROUP_W + sl.start:3 * GROUP_W + sl.stop] = drg_ref[:, sl].astype(BF)
            for src, dsrc, gain, off in ((fq_ref, dqf_ref, gq, 4), (fk_ref, dkf_ref, gk, 5)):
                xr = src[:, sl].astype(F32)
                r = lax.rsqrt(_group_mean64(xr * xr) + EPS)
                xh = xr * r
                dy = dsrc[:, sl] * (0.125 if off == 4 else 1.0)
                dgs = jnp.sum(dy * xh, axis=0, keepdims=True)
                if off == 4:
                    dgq = dgq + dgs
                else:
                    dgk = dgk + dgs
                dxh = dy * gain
                dproj_ref[:, off * GROUP_W + sl.start:off * GROUP_W + sl.stop] = \
                    (r * (dxh - xh * _group_mean64(dxh * xh))).astype(BF)
            dproj_ref[:, 6 * GROUP_W + sl.start:6 * GROUP_W + sl.stop] = dvf_ref[:, sl].astype(BF)
        gq_acc[...] += dgq
        gk_acc[...] += dgk
        row = lax.broadcasted_iota(jnp.int32, (tm, tm), 0)
        col = lax.broadcasted_iota(jnp.int32, (tm, tm), 1)
        dlf = _dot_exact((col >= row).astype(BF), df_ref[...]) + carry[0:1, :]
        carry[...] = jnp.broadcast_to(dlf[0:1, :], carry.shape)
        lane = lax.broadcasted_iota(jnp.int32, (tm, LANES), 1)
        dz = jnp.where(lane < N_HEADS, dlf / (1.0 + jnp.exp(z_ref[...])), 0.0)
        db_ref[...] += jnp.sum(dz, axis=0, keepdims=True)
        dz_bf = dz.astype(BF)
        dz_ref[...] = dz_bf
        dn1 = jnp.dot(dz_bf, wf_ref[...], preferred_element_type=F32)
        for sec in range(MAIN_W // GROUP_W):
            sl = slice(sec * GROUP_W, (sec + 1) * GROUP_W)
            dn1 = dn1 + jnp.dot(dproj_ref[:, sl], wm_ref[sl, :], preferred_element_type=F32)
        dx, dg = _rms_bwd(x_ref[...], g_ref[...], dn1)
        dx_ref[...] = dh1_ref[...] + dx
        dg_ref[...] += dg

        @pl.when(i == n_t - 1)
        def _():
            dgq_ref[...] = gq_acc[:, :HEAD_DIM] + gq_acc[:, HEAD_DIM:]
            dgk_ref[...] = gk_acc[:, :HEAD_DIM] + gk_acc[:, HEAD_DIM:]

    row_spec = lambda w, col=0: pl.BlockSpec((tm, w), lambda i: (n_t - 1 - i, col))
    full = lambda a: pl.BlockSpec(a.shape, lambda i: (0,) * a.ndim)
    acc = lambda r, c: pl.BlockSpec((r, c), lambda i: (0, 0))
    return pl.pallas_call(
        body, name="in_proj_bwd", grid=(n_t,),
        out_shape=(jax.ShapeDtypeStruct((t_len, MAIN_W), BF), jax.ShapeDtypeStruct((t_len, LANES), BF),
                   jax.ShapeDtypeStruct((t_len, D_MODEL), F32), jax.ShapeDtypeStruct((1, D_MODEL), F32),
                   jax.ShapeDtypeStruct((1, HEAD_DIM), F32), jax.ShapeDtypeStruct((1, HEAD_DIM), F32),
                   jax.ShapeDtypeStruct((1, LANES), F32)),
        in_specs=[row_spec(D_MODEL), full(g_mix), row_spec(D_MODEL)] + [row_spec(GROUP_W)] * 7
        + [row_spec(LANES), row_spec(GROUP_W, 4), row_spec(GROUP_W, 5), row_spec(LANES), row_spec(LANES), row_spec(LANES),
           full(gq_t), full(gk_t), *_w_in_specs()],
        out_specs=(row_spec(MAIN_W), row_spec(LANES), row_spec(D_MODEL), acc(1, D_MODEL), acc(1, HEAD_DIM), acc(1, HEAD_DIM),
                   acc(1, LANES)),
        scratch_shapes=[pltpu.VMEM((8, LANES), F32), pltpu.VMEM((1, LANES), F32), pltpu.VMEM((1, LANES), F32)],
        compiler_params=_cparams(("arbitrary",)),
    )(x, g_mix, dh1, dq_r, dk_r, dv_r, drg, dq_f, dk_f, dv_f, df_col, proj, proj, z, cos_t, sin_t, gq_t, gk_t, w_in_t, w_in_t)


def _matmul_tn(a, b, name, bk=1024):
    t_len, m = a.shape
    n = b.shape[1]
    bm = m if m <= TN_MAX_ROWS else m // 2
    bk = min(bk, t_len)

    def body(a_ref, b_ref, o_ref):
        @pl.when(pl.program_id(1) == 0)
        def _():
            o_ref[...] = jnp.zeros_like(o_ref)

        o_ref[...] += _dot_tn(a_ref[...], b_ref[...])

    return pl.pallas_call(
        body, name=name, grid=(m // bm, t_len // bk),
        out_shape=jax.ShapeDtypeStruct((m, n), F32),
        in_specs=[pl.BlockSpec((bk, bm), lambda i, k: (k, i)), pl.BlockSpec((bk, n), lambda i, k: (k, 0))],
        out_specs=pl.BlockSpec((bm, n), lambda i, k: (i, 0)),
        compiler_params=_cparams(("arbitrary", "arbitrary")),
    )(a, b)


def _place():
    x, y, c = lax.axis_index("x"), lax.axis_index("y"), lax.axis_index("c")
    chips = [(1 - x, y), (x, 1 - y), (1 - x, 1 - y)]
    return x, y, c, chips


def _row_chunks(rows, limit):
    step = max(d for d in range(16, min(rows, limit) + 1, 16) if rows % d == 0)
    return [slice(i, i + step) for i in range(0, rows, step)]


ICI_CHUNK_ROWS = 256
D2D_CHUNK_ROWS = 256


def _gather_phase(phase, ins, outs, send_sems, recv_sems):
    x, y, c, chips = _place()
    me_chip = 2 * x + y
    sibling = (x, y, 1 - c)

    def copy(w, k, slot, half, to, rows=slice(None), src=None):
        dst = outs[w].at[slot, half, rows]
        return pltpu.make_async_remote_copy(src_ref=dst if src is None else src, dst_ref=dst,
                                            send_sem=send_sems.at[w, k], recv_sem=recv_sems.at[w, k],
                                            device_id=to, device_id_type=MESH)

    for w in range(len(ins)):
        for j, (px, py) in enumerate(chips):
            if phase == 0:
                for rows in _row_chunks(ins[w].shape[1], ICI_CHUNK_ROWS):
                    copy(w, j, me_chip, c, (px, py, c), rows, src=ins[w].at[c, rows]).start()
            elif phase == 1:
                copy(w, j, 2 * px + py, c, (x, y, c)).wait_recv()
                for rows in _row_chunks(ins[w].shape[1], D2D_CHUNK_ROWS):
                    copy(w, 3 + j, 2 * px + py, c, sibling, rows).start()
            else:
                copy(w, 3 + j, 2 * px + py, 1 - c, (x, y, c)).wait_recv()
                copy(w, j, me_chip, c, (px, py, c), src=ins[w].at[c]).wait_send()
                copy(w, 3 + j, 2 * px + py, c, sibling).wait_send()


def _gather_scratch(n_w):
    return [pltpu.SemaphoreType.DMA((n_w, 6)), pltpu.SemaphoreType.DMA((n_w, 6))]


def _all_gather_weights(shards):
    n_w = len(shards)

    def body(*refs):
        for phase in range(3):
            _gather_phase(phase, refs[:n_w], refs[n_w:2 * n_w], *refs[2 * n_w:])

    return pl.pallas_call(
        body, name="all_gather_weights",
        out_shape=tuple(jax.ShapeDtypeStruct((4,) + s.shape, s.dtype) for s in shards),
        in_specs=[ANY] * n_w, out_specs=(ANY,) * n_w, scratch_shapes=_gather_scratch(n_w),
    )(*shards)


def _exchange_phase(phase, ins, theirs, send_sems, recv_sems):
    x, y, c, _ = _place()

    def remote(w, k=slice(None), rows=slice(None)):
        return pltpu.make_async_remote_copy(src_ref=ins[w].at[k, 1 - c, rows], dst_ref=theirs[w].at[k, rows],
                                            send_sem=send_sems.at[w], recv_sem=recv_sems.at[w], device_id=(x, y, 1 - c),
                                            device_id_type=MESH)

    for w in range(len(ins)):
        if phase == 0:
            for k in range(4):
                for rows in _row_chunks(ins[w].shape[2], D2D_CHUNK_ROWS):
                    remote(w, k, rows).start()
        else:
            remote(w).wait()


def _exchange_scratch(n_w):
    return [pltpu.SemaphoreType.DMA((n_w,)), pltpu.SemaphoreType.DMA((n_w,))]


def _exchange_out_shapes(grads):
    return tuple(jax.ShapeDtypeStruct((4,) + g.shape[2:], g.dtype) for g in grads)


def _add_pairs(part, theirs, name, halves):
    _, _, r, c = part.shape
    rb = 64 if r % 64 == 0 else r
    n_w = len(halves)
    n_steps = r // rb

    def body(*refs):
        a_ref, b_ref = refs[:2]
        own_ref, ob_ref = refs[2 + n_w:4 + n_w]
        comm = (refs[2:2 + n_w], refs[4 + n_w:4 + 2 * n_w]) + tuple(refs[4 + 2 * n_w:])
        step = pl.program_id(0)

        @pl.when(step == 0)
        def _():
            _share_phase(0, *comm)

        my_chip = 2 * lax.axis_index("x") + lax.axis_index("y")
        ob_ref[...] = (a_ref[...] + b_ref[...]).astype(BF)
        own_ref[...] = a_ref[my_chip] + b_ref[my_chip]

        @pl.when(step == n_steps - 1)
        def _():
            _share_phase(1, *comm)

    spec = pl.BlockSpec((4, rb, c), lambda i: (0, i, 0))
    flat = pl.pallas_call(
        body, name=name, grid=(n_steps,),
        out_shape=(jax.ShapeDtypeStruct((r, c), F32), jax.ShapeDtypeStruct((4, r, c), BF)) + _share_out_shapes(halves),
        in_specs=[pl.BlockSpec((4, None, rb, c), lambda i: (0, lax.axis_index("c"), i, 0)), spec] + [ANY] * n_w,
        out_specs=(pl.BlockSpec((rb, c), lambda i: (i, 0)), spec) + (ANY,) * n_w,
        scratch_shapes=_share_scratch(n_w), compiler_params=_cparams(("arbitrary",)),
    )(part, theirs, *halves)
    return (flat[0], flat[1]), list(flat[2:])


def _scatter_phase(phase, bfs, got, send_sems, recv_sems):
    x, y, c, chips = _place()

    def remote(w, j, px, py, rows=slice(None)):
        return pltpu.make_async_remote_copy(src_ref=bfs[w].at[2 * px + py, rows], dst_ref=got[w].at[j, rows],
                                            send_sem=send_sems.at[w, j], recv_sem=recv_sems.at[w, j], device_id=(px, py, c),
                                            device_id_type=MESH)

    for w in range(len(bfs)):
        for j, (px, py) in enumerate(chips):
            if phase == 0:
                for rows in _row_chunks(bfs[w].shape[1], ICI_CHUNK_ROWS):
                    remote(w, j, px, py, rows).start()
            else:
                remote(w, j, px, py).wait()


def _scatter_scratch(n_w):
    return [pltpu.SemaphoreType.DMA((n_w, 3)), pltpu.SemaphoreType.DMA((n_w, 3))]


def _scatter_out_shapes(sums_bf16):
    return tuple(jax.ShapeDtypeStruct((3,) + s.shape[1:], BF) for s in sums_bf16)


def _add_received(own, got, name):
    r, c = own.shape
    rb = 128 if r % 128 == 0 else r

    def body(o_ref, g_ref, out_ref):
        out_ref[...] = ((o_ref[...] + g_ref[0].astype(F32)) + g_ref[1].astype(F32)) + g_ref[2].astype(F32)

    return pl.pallas_call(
        body, name=name, grid=(r // rb,), out_shape=jax.ShapeDtypeStruct((r, c), F32),
        in_specs=[pl.BlockSpec((rb, c), lambda i: (i, 0)), pl.BlockSpec((3, rb, c), lambda i: (0, i, 0))],
        out_specs=pl.BlockSpec((rb, c), lambda i: (i, 0)), compiler_params=_cparams(("arbitrary",)),
    )(own, got)


def _share_phase(phase, ins, outs, send_sems, recv_sems):
    x, y, c, _ = _place()

    def remote(w, rows=slice(None)):
        return pltpu.make_async_remote_copy(src_ref=ins[w].at[rows], dst_ref=outs[w].at[c, rows], send_sem=send_sems.at[w],
                                            recv_sem=recv_sems.at[w], device_id=(x, y, 1 - c), device_id_type=MESH)

    for w in range(len(ins)):
        if phase == 0:
            for rows in _row_chunks(ins[w].shape[0], D2D_CHUNK_ROWS):
                remote(w, rows).start()
        else:
            remote(w).wait()


def _share_scratch(n_w):
    return [pltpu.SemaphoreType.DMA((n_w,)), pltpu.SemaphoreType.DMA((n_w,))]


def _share_out_shapes(halves):
    return tuple(jax.ShapeDtypeStruct((2,) + h.shape, h.dtype) for h in halves)


def _share_with_sibling(halves):
    n_w = len(halves)

    def body(*refs):
        for phase in range(2):
            _share_phase(phase, refs[:n_w], refs[n_w:2 * n_w], *refs[2 * n_w:])

    return pl.pallas_call(
        body, name="share_with_sibling", out_shape=_share_out_shapes(halves),
        in_specs=[ANY] * n_w, out_specs=(ANY,) * n_w, scratch_shapes=_share_scratch(n_w),
    )(*halves)


def _small_phase(phase, p_ref, out_ref, slots, send_sems, recv_sems):
    x, y, cc, _ = _place()
    me = 4 * x + 2 * y + cc
    copies = []
    for k in range(1, 8):
        dx, dy, dc = (k >> 2) & 1, (k >> 1) & 1, k & 1
        to = (1 - x if dx else x, 1 - y if dy else y, 1 - cc if dc else cc)
        copies.append(pltpu.make_async_remote_copy(src_ref=p_ref, dst_ref=slots.at[me], send_sem=send_sems.at[k - 1],
                                                   recv_sem=recv_sems.at[k - 1], device_id=to, device_id_type=MESH))
    if phase == 0:
        slots[me] = p_ref[...]
        for cp in copies:
            cp.start()
    else:
        for cp in copies:
            cp.wait()
        total = slots[0]
        for d in range(1, 8):
            total = total + slots[d]
        out_ref[...] = total


def _adamw_update(w_ref, g_ref, m_ref, v_ref, d_ref, nm_ref, nv_ref):
    gv = g_ref[...]
    nm = ADAM_B1 * m_ref[...] + (1.0 - ADAM_B1) * gv
    nv = ADAM_B2 * v_ref[...] + (1.0 - ADAM_B2) * (gv * gv)
    nm_ref[...] = nm
    nv_ref[...] = nv
    m_hat = nm / (1.0 - ADAM_B1 ** ADAM_STEP)
    v_hat = nv / (1.0 - ADAM_B2 ** ADAM_STEP)
    d_ref[...] = -ADAM_LR * (m_hat / (jnp.sqrt(v_hat) + ADAM_EPS) + ADAM_WD * w_ref[...])


def _adamw_many(ws, gs, ms, vs, sums, pack):
    n_a, n_w = len(ws), len(sums)
    n_steps = ADAM_STEPS
    specs = [pl.BlockSpec((w.shape[0] // n_steps, w.shape[1]), lambda i: (i, 0)) for w in ws]
    pack_spec = pl.BlockSpec(pack.shape, lambda i: (0, 0))

    def body(*refs):
        ins = refs[:4 * n_a]
        p_ref = refs[4 * n_a + n_w]
        first_out = 4 * n_a + n_w + 1
        outs = refs[first_out:first_out + 3 * n_a]
        total_ref = refs[first_out + 3 * n_a + n_w]
        scratch = refs[first_out + 3 * n_a + n_w + 1:]
        scatter = (refs[4 * n_a:4 * n_a + n_w], refs[first_out + 3 * n_a:first_out + 3 * n_a + n_w]) + tuple(scratch[:2])
        small = (p_ref, total_ref) + tuple(scratch[2:])
        step = pl.program_id(0)

        @pl.when(step == 0)
        def _():
            _scatter_phase(0, *scatter)
            _small_phase(0, *small)

        for a in range(n_a):
            _adamw_update(*(ins[k * n_a + a] for k in range(4)), *(outs[3 * a + k] for k in range(3)))

        @pl.when(step == n_steps - 1)
        def _():
            _scatter_phase(1, *scatter)
            _small_phase(1, *small)

    flat = pl.pallas_call(
        body, name="adamw_late", grid=(n_steps,),
        out_shape=tuple(jax.ShapeDtypeStruct(w.shape, F32) for w in ws for _ in range(3)) + _scatter_out_shapes(sums)
        + (jax.ShapeDtypeStruct(pack.shape, F32),),
        in_specs=specs * 4 + [ANY] * n_w + [pack_spec],
        out_specs=tuple(s for s in specs for _ in range(3)) + (ANY,) * n_w + (pack_spec,),
        scratch_shapes=_scatter_scratch(n_w) + [pltpu.VMEM((8,) + pack.shape, F32), pltpu.SemaphoreType.DMA((7,)),
                                                pltpu.SemaphoreType.DMA((7,))],
        compiler_params=_cparams(("arbitrary",)),
    )(*ws, *gs, *ms, *vs, *sums, pack)
    return [tuple(flat[3 * a:3 * a + 3]) for a in range(n_a)] + list(flat[3 * n_a:])


def _adamw(w, g, m, v, name):
    r, c = w.shape
    rb, cb = (128, c) if r % 128 == 0 else (r, LANES if (r % 8 and c % LANES == 0) else c)

    def body(*refs):
        _adamw_update(*refs)

    spec = pl.BlockSpec((rb, cb), lambda i, j: (i, j))
    return pl.pallas_call(
        body, name=name, grid=(r // rb, c // cb), out_shape=(jax.ShapeDtypeStruct((r, c), F32),) * 3,
        in_specs=[spec] * 4, out_specs=(spec,) * 3, compiler_params=_cparams(("arbitrary", "arbitrary")),
    )(w, g, m, v)


def _rope_tables(t_len):
    inv_freq = ROPE_BASE ** (-jnp.arange(0, HEAD_DIM, 2, dtype=F32) / HEAD_DIM)
    ang = jnp.arange(t_len, dtype=F32)[:, None] * inv_freq[None, :]
    cos, sin = jnp.cos(ang), jnp.sin(ang)
    cos_t = jnp.concatenate([cos, cos, cos, cos], axis=-1)
    sin_t = jnp.concatenate([-sin, sin, -sin, sin], axis=-1)
    return cos_t, sin_t


def _cols_to_shards(dw):
    r, n = dw.shape
    return jnp.transpose(dw.reshape(2, r // 2, 4, n // 4), (2, 0, 1, 3))


def _rows_to_shards(dw):
    r, n = dw.shape
    rows = r // 4
    if rows % SUBLANES == 0:
        padded = _pad_rows(dw.reshape(4, rows, n))
    else:
        window = rows + SUBLANES - rows % SUBLANES
        padded = _pad_rows(jnp.stack([dw[rows * k // SUBLANES * SUBLANES:][:window] for k in range(4)]))
    return padded.reshape(4, 2, padded.shape[1] // 2, n)


def _shard_row_offset(rows):
    return (rows * (2 * lax.axis_index("x") + lax.axis_index("y"))) % SUBLANES


def _pad_lanes(a):
    extra = -a.shape[-1] % LANES
    return a if extra == 0 else jnp.pad(a, [(0, 0)] * (a.ndim - 1) + [(0, extra)])


def _pad_rows(a):
    rows = a.shape[-2]
    extra = 0 if rows % SHARD_ROW_ALIGN == 0 else -rows % SHARD_ROW_PAD
    return a if extra == 0 else jnp.pad(a, [(0, 0)] * (a.ndim - 2) + [(0, extra), (0, 0)])


def _pad_row(a, width=D_MODEL):
    a = a.reshape(1, -1)
    return jnp.pad(a, ((0, 0), (0, width - a.shape[1])))


def kernel(x, mem, g_mix, w_in, b_forget, g_ret_out, g_fox_q, g_fox_k, w_out, g_xattn, w_xq, w_xkv, g_mem, g_xq, g_xk, w_xo, g_ffn, w_gate, w_up, w_down, loss_target, m_g_mix, m_w_in, m_b_forget, m_g_ret_out, m_g_fox_q, m_g_fox_k, m_w_out, m_g_xattn, m_w_xq, m_w_xkv, m_g_mem, m_g_xq, m_g_xk, m_w_xo, m_g_ffn, m_w_gate, m_w_up, m_w_down, v_g_mix, v_w_in, v_b_forget, v_g_ret_out, v_g_fox_q, v_g_fox_k, v_w_out, v_g_xattn, v_w_xq, v_w_xkv, v_g_mem, v_g_xq, v_g_xk, v_w_xo, v_g_ffn, v_w_gate, v_w_up, v_w_down):
    big = {"w_in": (w_in, m_w_in, v_w_in), "w_out": (w_out, m_w_out, v_w_out), "w_xq": (w_xq, m_w_xq, v_w_xq),
           "w_xkv": (w_xkv, m_w_xkv, v_w_xkv), "w_xo": (w_xo, m_w_xo, v_w_xo), "w_gate": (w_gate, m_w_gate, v_w_gate),
           "w_up": (w_up, m_w_up, v_w_up), "w_down": (w_down, m_w_down, v_w_down)}
    for n in TRANSPOSED:
        big[n] = tuple(jnp.swapaxes(a, 1, 2) for a in big[n])
    shards = {}
    for n in big:
        w = _pad_rows(_pad_lanes(big[n][0][0].astype(BF)))
        shards[n] = w.reshape(2, w.shape[0] // 2, w.shape[1])
    sizes = {n: big[n][0].shape[1:] for n in big}
    w_in_full = _assemble_weight("w_in", _all_gather_weights([shards["w_in"]])[0], shards["w_in"], sizes["w_in"],
                                 tail_rows=MAIN_W + LANES - IN_W)
    small_w ={"g_mix": g_mix, "b_forget": b_forget, "g_ret_out": g_ret_out, "g_fox_q": g_fox_q, "g_fox_k": g_fox_k,
               "g_xattn": g_xattn, "g_mem": g_mem, "g_xq": g_xq, "g_xk": g_xk, "g_ffn": g_ffn}
    m_small = {"g_mix": m_g_mix, "b_forget": m_b_forget, "g_ret_out": m_g_ret_out, "g_fox_q": m_g_fox_q, "g_fox_k": m_g_fox_k,
               "g_xattn": m_g_xattn, "g_mem": m_g_mem, "g_xq": m_g_xq, "g_xk": m_g_xk, "g_ffn": m_g_ffn}
    v_small = {"g_mix": v_g_mix, "b_forget": v_b_forget, "g_ret_out": v_g_ret_out, "g_fox_q": v_g_fox_q, "g_fox_k": v_g_fox_k,
               "g_xattn": v_g_xattn, "g_mem": v_g_mem, "g_xq": v_g_xq, "g_xk": v_g_xk, "g_ffn": v_g_ffn}
    loss_part, grad_x, sums, got, in_parts, small_g = _local_step(x[0], mem[0], loss_target[0], w_in_full, shards, sizes, small_w)
    return _reduce_and_update(big, sums, got, in_parts, small_w, small_g, loss_part, grad_x, m_small, v_small)


def _assemble_rows(g, rows, out_rows):
    n, padded, c = g.shape

    def body(g_ref, o_ref):
        k = pl.program_id(0)
        for i in range(n):
            @pl.when(k == i)
            def _():
                o_ref[i * rows:(i + 1) * rows, :] = g_ref[0, :rows, :]

        @pl.when(k == n - 1)
        def _():
            o_ref[n * rows:, :] = jnp.zeros((out_rows - n * rows, c), o_ref.dtype)

    return pl.pallas_call(
        body, name="assemble_w_in", grid=(n,), out_shape=jax.ShapeDtypeStruct((out_rows, c), g.dtype),
        in_specs=[pl.BlockSpec((1, padded, c), lambda k: (k, 0, 0))], out_specs=pl.BlockSpec((out_rows, c), lambda k: (0, 0)),
        compiler_params=_cparams(("arbitrary",)),
    )(g)


def _assemble_weight(name, gathered, own, size, tail_rows=0):
    rows, width = size
    my_chip = 2 * lax.axis_index("x") + lax.axis_index("y")
    g = lax.dynamic_update_slice(gathered, own[None], (my_chip, 0, 0, 0))
    g = g.reshape(4, 2 * g.shape[2], g.shape[3])
    if tail_rows:
        assert name not in COL_SHARDED and width == g.shape[2]
        return _assemble_rows(g, rows, 4 * rows + tail_rows)
    g = g[:, :rows, :width]
    return jnp.transpose(g, (1, 0, 2)).reshape(rows, 4 * width) if name in COL_SHARDED else g.reshape(4 * rows, width)


def _shard_parts(names, dw):
    return [_pad_lanes(_cols_to_shards(dw[n]) if n in COL_SHARDED else _rows_to_shards(dw[n])) for n in names]


def _add_pairs_many(parts, theirs, name):
    n_a = len(parts)
    n_steps = min(p.shape[2] for p in parts) // 32
    rb = [p.shape[2] // n_steps for p in parts]
    part_specs = [pl.BlockSpec((4, None, r, p.shape[3]), lambda i: (0, lax.axis_index("c"), i, 0)) for p, r in zip(parts, rb)]
    quad_specs = [pl.BlockSpec((4, r, p.shape[3]), lambda i: (0, i, 0)) for p, r in zip(parts, rb)]
    own_specs = [pl.BlockSpec((r, p.shape[3]), lambda i: (i, 0)) for p, r in zip(parts, rb)]

    def body(*refs):
        my_chip = 2 * lax.axis_index("x") + lax.axis_index("y")
        for a in range(n_a):
            a_ref, b_ref, own_ref, ob_ref = refs[a], refs[n_a + a], refs[2 * n_a + a], refs[3 * n_a + a]
            ob_ref[...] = (a_ref[...] + b_ref[...]).astype(BF)
            own_ref[...] = a_ref[my_chip] + b_ref[my_chip]

    flat = pl.pallas_call(
        body, name=name, grid=(n_steps,),
        out_shape=tuple(jax.ShapeDtypeStruct(p.shape[2:], F32) for p in parts)
        + tuple(jax.ShapeDtypeStruct((4,) + p.shape[2:], BF) for p in parts),
        in_specs=part_specs + quad_specs, out_specs=tuple(own_specs) + tuple(quad_specs),
        compiler_params=_cparams(("arbitrary",)),
    )(*parts, *theirs)
    return [(flat[a], flat[n_a + a]) for a in range(n_a)]


def _core_sums(parts, theirs):
    out = [None] * len(parts)
    for tag, pick in (("a", lambda p: p.shape[2] % LANES == 0), ("b", lambda p: p.shape[2] % LANES != 0)):
        idx = [i for i, p in enumerate(parts) if pick(p)]
        for i, res in zip(idx, _add_pairs_many([parts[i] for i in idx], [theirs[i] for i in idx], f"core_sum_late_{tag}")):
            out[i] = res
    return out


def _local_step(xs, mems, tgt, w_in_full, shards, sizes, small_w):
    g_mix, b_forget, g_ret_out, g_fox_q, g_fox_k = (small_w[n] for n in ("g_mix", "b_forget", "g_ret_out", "g_fox_q", "g_fox_k"))
    g_xattn, g_mem, g_xq, g_xk, g_ffn = (small_w[n] for n in ("g_xattn", "g_mem", "g_xq", "g_xk", "g_ffn"))
    w_in_t = w_in_full
    t_len = xs.shape[0]
    cos_t, sin_t = _rope_tables(t_len)
    tables = _decay_tables(min(RET_BLOCK, t_len))
    gq_t = jnp.concatenate([g_fox_q, g_fox_q], axis=-1)
    gk_t = jnp.concatenate([g_fox_k, g_fox_k], axis=-1)
    b_pad = _pad_row(b_forget, LANES)
    g_ret = g_ret_out.reshape(N_HEADS // 2, 1, LANES)

    n1, proj, rq, rk, q_aug, k_aug, z = _in_proj_fwd(xs, g_mix, w_in_t, b_pad, cos_t, sin_t, gq_t, gk_t)
    raw, mix_r, states = _retention_fwd(rq, rk, proj, g_ret, tables)
    mix_f, o32, lse, *gathered = _fox_fwd(q_aug, k_aug, proj, [shards[n] for n in LATE])
    full = {n: _assemble_weight(n, g, shards[n], sizes[n]) for n, g in zip(LATE, gathered)}
    memn, kraw, kn, vmem = _mem_kv_fwd(mems, g_mem, full["w_xkv"], g_xk)
    h1, hn2, qx, o_x, h2 = _attn_out_xattn_fwd(xs, mix_r, mix_f, full["w_out"], g_xattn, full["w_xq"], g_xq, kn, vmem, full["w_xo"])
    hn3, gate, up, act, dh3, loss_part = _ffn_loss_fwd(h2, g_ffn, full["w_gate"], full["w_up"], full["w_down"], tgt)

    dgate, dup, dh2, dg_ffn = _ffn_bwd(dh3, gate, up, h2, g_ffn, full["w_gate"], full["w_up"], full["w_down"])
    dqx, dh1, dmr, dmf, dkn, dvm, dg_xattn, dg_xq = _attn_out_xattn_bwd(dh2, h1, qx, kn, vmem, full["w_xo"], full["w_xq"],
                                                                      full["w_out"], g_xattn, g_xq)
    dw_xkv, dg_mem, dg_xk = _mem_kv_bwd(dkn, dvm, kraw, mems, memn, g_mem, g_xk, full["w_xkv"])
    dw_gu = _matmul_tn_pair(dgate, dup, hn3, "dw_gate_up")
    dw = {
        "w_out": _matmul_tn_pair(mix_r, mix_f, dh1, "dw_out").reshape(D_MODEL, D_MODEL),
        "w_xq": _matmul_tn(hn2, dqx, "dw_xq"),
        "w_xkv": dw_xkv,
        "w_xo": _matmul_tn(o_x, dh2, "dw_xo"),
        "w_gate": dw_gu[0],
        "w_up": dw_gu[1],
        "w_down": _matmul_tn(act, dh3, "dw_down"),
    }
    late_parts = _shard_parts(LATE, dw)
    dq_r, dk_r, dv_r, drg, dg_ret, *late_theirs = _retention_bwd(dmr, raw, proj, g_ret, rq, rk, states, tables, late_parts)
    late_sums = _core_sums(late_parts, late_theirs)
    dq_f, dk_f, dv_f, df, *late_got = _fox_bwd(q_aug, k_aug, proj, dmf, o32, lse, [s[1] for s in late_sums])
    df_col = jnp.pad(jnp.transpose(df, (1, 0, 2)).reshape(t_len, N_HEADS), ((0, 0), (0, LANES - N_HEADS)))
    dproj, dz, grad_x, dg_mix, dg_fq, dg_fk, db = _in_proj_bwd(xs, g_mix, dh1, dq_r, dk_r, dv_r, drg, dq_f, dk_f, dv_f, df_col,
                                                              proj, z, cos_t, sin_t, gq_t, gk_t, w_in_t)

    dw_in = jnp.concatenate([_matmul_tn(dproj, n1, "dw_in_main"), _matmul_tn(dz, n1, "dw_in_ff")[:IN_W - MAIN_W]], axis=0)
    in_parts = _shard_parts(("w_in",), {"w_in": dw_in})
    sums = {n: s[0] for n, s in zip(LATE, late_sums)}
    got = dict(zip(LATE, late_got))
    small_g = {"g_mix": dg_mix, "b_forget": db[:, :N_HEADS], "g_ret_out": dg_ret, "g_fox_q": dg_fq, "g_fox_k": dg_fk,
               "g_xattn": dg_xattn, "g_mem": dg_mem, "g_xq": dg_xq, "g_xk": dg_xk, "g_ffn": dg_ffn}
    return loss_part, grad_x, sums, got, in_parts, small_g


def _add_received_many(owns, gots, parts):
    n_a, n_w = len(owns), len(parts)
    n_steps = CHIP_SUM_STEPS
    own_specs = [pl.BlockSpec((o.shape[0] // n_steps, o.shape[1]), lambda i: (i, 0)) for o in owns]
    got_specs = [pl.BlockSpec((3, o.shape[0] // n_steps, o.shape[1]), lambda i: (0, i, 0)) for o in owns]

    def body(*refs):
        first_out = 2 * n_a + n_w
        comm = (refs[2 * n_a:first_out], refs[first_out + n_a:first_out + n_a + n_w]) + tuple(refs[first_out + n_a + n_w:])
        step = pl.program_id(0)

        @pl.when(step == 0)
        def _():
            _exchange_phase(0, *comm)

        for a in range(n_a):
            o_ref, g_ref, out_ref = refs[a], refs[n_a + a], refs[first_out + a]
            out_ref[...] = ((o_ref[...] + g_ref[0].astype(F32)) + g_ref[1].astype(F32)) + g_ref[2].astype(F32)

        @pl.when(step == n_steps - 1)
        def _():
            _exchange_phase(1, *comm)

    flat = pl.pallas_call(
        body, name="chip_sum_late", grid=(n_steps,),
        out_shape=tuple(jax.ShapeDtypeStruct(o.shape, F32) for o in owns) + _exchange_out_shapes(parts),
        in_specs=own_specs + got_specs + [ANY] * n_w, out_specs=tuple(own_specs) + (ANY,) * n_w,
        scratch_shapes=_exchange_scratch(n_w), compiler_params=_cparams(("arbitrary",)),
    )(*owns, *gots, *parts)
    return list(flat[:n_a]), list(flat[n_a:])


def _final_grads(names, big, finals, shared):
    my_core = lax.axis_index("c")
    out = {}
    for n, s, fin in zip(names, shared, finals):
        s = lax.dynamic_update_slice(s, fin[None], (my_core, 0, 0))
        s = s.reshape(2 * s.shape[1], s.shape[2])
        rows, width = big[n][0].shape[1:]
        out[n] = s[:rows, :width] if rows % SUBLANES == 0 else lax.dynamic_slice(s, (_shard_row_offset(rows), 0), (rows, width))
    return out


def _reduce_and_update(big, sums, got, in_parts, small_w, small_g, loss_part, grad_x, m_small, v_small):
    small_names = list(small_w)
    pad_rows = SMALL_ROWS - len(small_names) - 1
    stack = lambda d: jnp.concatenate([_pad_row(d[n]) for n in small_names] + [jnp.zeros((pad_rows + 1, D_MODEL), F32)], axis=0)
    g_pack = jnp.concatenate([_pad_row(small_g[n]) for n in small_names] + [_pad_row(loss_part[0:1, 0:1])]
                             + [jnp.zeros((pad_rows, D_MODEL), F32)], axis=0)
    late_finals, in_theirs = _add_received_many([sums[n] for n in LATE], [got[n] for n in LATE], in_parts)
    (in_own, in_bf), late_shared = _add_pairs(in_parts[0], in_theirs[0], "core_sum_w_in", late_finals)
    grads = _final_grads(LATE, big, late_finals, late_shared)
    *late_updates, in_got, g_tot = _adamw_many([big[n][0][0] for n in LATE], [grads[n] for n in LATE], [big[n][1][0] for n in LATE],
                                               [big[n][2][0] for n in LATE], [in_bf], g_pack)
    updates = dict(zip(LATE, late_updates))
    in_final = [_add_received(in_own, in_got, "chip_sum_w_in")]
    grads.update(_final_grads(("w_in",), big, in_final, _share_with_sibling(in_final)))
    updates["w_in"] = _adamw(big["w_in"][0][0], grads["w_in"], big["w_in"][1][0], big["w_in"][2][0], "adamw_w_in")
    deltas, new_m, new_v = {}, {}, {}
    for n in big:
        restore = (lambda a: jnp.swapaxes(a[None], 1, 2)) if n in TRANSPOSED else (lambda a: a[None])
        grads[n] = restore(grads[n])
        deltas[n], new_m[n], new_v[n] = (restore(a) for a in updates[n])

    d_s, m_s, v_s = _adamw(stack(small_w), g_tot, stack(m_small), stack(v_small), "adamw_small")
    for i, n in enumerate(small_names):
        shape = small_w[n].shape
        size = int(np.prod(shape))
        grads[n] = g_tot[i, :size].reshape(shape)
        deltas[n], new_m[n], new_v[n] = d_s[i, :size].reshape(shape), m_s[i, :size].reshape(shape), v_s[i, :size].reshape(shape)
    loss = g_tot[len(small_names), 0]

    order = ["g_mix", "w_in", "b_forget", "g_ret_out", "g_fox_q", "g_fox_k", "w_out", "g_xattn", "w_xq", "w_xkv", "g_mem", "g_xq",
             "g_xk", "w_xo", "g_ffn", "w_gate", "w_up", "w_down"]
    return (loss, grad_x[None], *[grads[n] for n in order], *[deltas[n] for n in order], *[new_m[n] for n in order],
            *[new_v[n] for n in order])
```

```python
import functools

import numpy as np
import jax
import jax.numpy as jnp
from jax import lax
from jax.experimental import pallas as pl
from jax.experimental.pallas import tpu as pltpu

F32 = jnp.float32
BF = jnp.bfloat16

D_MODEL = 1024
HEAD_DIM = 64
N_HEADS = 8
GROUP_W = 512
N_XH = 4
XHD = 256
D_FF = 2816
MAIN_W = 3584
IN_W = 3592
ROPE_BASE = 10000.0
LOG2E = 1.4426950408889634
LN2 = 0.6931471805599453
EPS = 1e-6
NEG = -1e30
LANES = 128
SUBLANES = 8
RET_BLOCK = 256
REF_CHUNK = 64
ROW_TILE = 512
FFN_BWD_TILE = 256
ATT_BLOCK = 256
FWD_GROUP = 4
TN_MAX_ROWS = 1408
SMALL_ROWS = 16
COL_SHARDED = ("w_xkv",)
TRANSPOSED = ("w_in", "w_gate", "w_up")
SHARD_ROW_ALIGN = 32
SHARD_ROW_PAD = 256
LATE = ("w_out", "w_xq", "w_xkv", "w_xo", "w_gate", "w_up", "w_down")
VMEM_LIMIT = 56 * 1024 * 1024

ADAM_LR = 0.001
ADAM_B1 = 0.9
ADAM_B2 = 0.999
ADAM_EPS = 1e-08
ADAM_WD = 0.01
ADAM_STEP = 10
CHIP_SUM_STEPS = 2
ADAM_STEPS = 8

MESH = pl.DeviceIdType.MESH
ANY = pl.BlockSpec(memory_space=pl.ANY)
VMEM_SPEC = pl.BlockSpec(memory_space=pltpu.VMEM)


def _cparams(sem=None, vmem=VMEM_LIMIT):
    return pltpu.CompilerParams(dimension_semantics=sem, vmem_limit_bytes=vmem)


def _dot(a, b):
    return jnp.dot(a.astype(BF), b.astype(BF), preferred_element_type=F32)


def _dot_nt(a, b):
    return lax.dot_general(a.astype(BF), b.astype(BF), (((1,), (1,)), ((), ())), preferred_element_type=F32)


def _dot_tn(a, b):
    return lax.dot_general(a.astype(BF), b.astype(BF), (((0,), (0,)), ((), ())), preferred_element_type=F32)


def _split3(x):
    hi = x.astype(BF)
    r = x - hi.astype(F32)
    mid = r.astype(BF)
    lo = (r - mid.astype(F32)).astype(BF)
    return hi, mid, lo


def _dot_exact(ind, x):
    hi, mid, lo = _split3(x)
    return (jnp.dot(ind, lo, preferred_element_type=F32) + jnp.dot(ind, mid, preferred_element_type=F32)
            + jnp.dot(ind, hi, preferred_element_type=F32))


def _dot_nt_exact(ind, x):
    hi, mid, lo = _split3(x)
    dn = (((1,), (1,)), ((), ()))
    return (lax.dot_general(ind, lo, dn, preferred_element_type=F32) + lax.dot_general(ind, mid, dn, preferred_element_type=F32)
            + lax.dot_general(ind, hi, dn, preferred_element_type=F32))


def _sigmoid(x):
    return 1.0 / (1.0 + jnp.exp(-x))


def _rms_fwd(x, g):
    r = lax.rsqrt(jnp.mean(x * x, axis=-1, keepdims=True) + EPS)
    return x * r * g


def _rms_bwd(x, g, dy):
    r = lax.rsqrt(jnp.mean(x * x, axis=-1, keepdims=True) + EPS)
    xh = x * r
    dg = jnp.sum(dy * xh, axis=0, keepdims=True)
    dxh = dy * g
    dx = r * (dxh - xh * jnp.mean(dxh * xh, axis=-1, keepdims=True))
    return dx, dg


def _group_mean64(x):
    lane = lax.broadcasted_iota(jnp.int32, x.shape, 1)
    lo = lane < HEAD_DIM
    s_lo = jnp.sum(jnp.where(lo, x, 0.0), axis=-1, keepdims=True)
    s_hi = jnp.sum(jnp.where(lo, 0.0, x), axis=-1, keepdims=True)
    return jnp.where(lo, s_lo, s_hi) * (1.0 / HEAD_DIM)


def _swap32(x):
    lane = lax.broadcasted_iota(jnp.int32, x.shape, 1)
    first = (lane % HEAD_DIM) < (HEAD_DIM // 2)
    return jnp.where(first, pltpu.roll(x, LANES - HEAD_DIM // 2, axis=1), pltpu.roll(x, HEAD_DIM // 2, axis=1))


def _chunks(w):
    return [slice(j * LANES, (j + 1) * LANES) for j in range(w // LANES)]


def _aug_pair(qk, f_cols, is_query):
    lane = lax.broadcasted_iota(jnp.int32, qk.shape, 1)
    a = lane - HEAD_DIM
    values = (qk, pltpu.roll(qk, HEAD_DIM, axis=1))
    out = []
    for hh in range(2):
        hi, mid, lo = (p.astype(F32) for p in _split3(f_cols[hh] * LOG2E))
        if is_query:
            aux = jnp.where(a == 0, hi, jnp.where(a == 1, mid, jnp.where(a == 2, lo, jnp.where(a < 6, 1.0, 0.0))))
        else:
            aux = jnp.where(a < 3, 1.0, jnp.where(a == 3, -hi, jnp.where(a == 4, -mid, jnp.where(a == 5, -lo, 0.0))))
        out.append(jnp.where(a < 0, values[hh], aux))
    return jnp.concatenate(out, axis=-1).astype(BF)


def _mem_kv_fwd(mem, g_mem, w_xkv, g_xk):
    m_tok = mem.shape[0]

    def body(mem_ref, gm_ref, w_ref, gk_ref, memn_ref, kraw_ref, kn_ref, v_ref):
        mn = _rms_fwd(mem_ref[...], gm_ref[...]).astype(BF)
        memn_ref[...] = mn
        kv = jnp.dot(mn, w_ref[...], preferred_element_type=F32)
        k = kv[:, :D_MODEL]
        kraw_ref[...] = k
        v_ref[...] = kv[:, D_MODEL:].astype(BF)
        for h in range(N_XH):
            sl = slice(h * XHD, (h + 1) * XHD)
            kn_ref[:, sl] = _rms_fwd(k[:, sl], gk_ref[...]).astype(BF)

    return pl.pallas_call(
        body, name="mem_kv_fwd",
        out_shape=(jax.ShapeDtypeStruct((m_tok, D_MODEL), BF), jax.ShapeDtypeStruct((m_tok, D_MODEL), F32),
                   jax.ShapeDtypeStruct((m_tok, D_MODEL), BF), jax.ShapeDtypeStruct((m_tok, D_MODEL), BF)),
        in_specs=[VMEM_SPEC] * 4, out_specs=(VMEM_SPEC,) * 4, compiler_params=_cparams(),
    )(mem, g_mem, w_xkv, g_xk)


def _in_proj_fwd(x, g_mix, w_in_t, b_pad, cos_t, sin_t, gq_t, gk_t):
    t_len = x.shape[0]
    tm = min(ROW_TILE, t_len)
    n_t = t_len // tm

    def body(x_ref, g_ref, wm_ref, wf_ref, b_ref, cos_ref, sin_ref, gq_ref, gk_ref,
             n1_ref, proj_ref, rq_ref, rk_ref, qa_ref, ka_ref, z_ref, carry):
        i = pl.program_id(0)

        @pl.when(i == 0)
        def _():
            carry[...] = jnp.zeros_like(carry)

        n1 = _rms_fwd(x_ref[...], g_ref[...]).astype(BF)
        n1_ref[...] = n1
        z = _dot_nt(n1, wf_ref[...]) + b_ref[...]
        z_ref[...] = z
        lane = lax.broadcasted_iota(jnp.int32, z.shape, 1)
        lf = jnp.where(lane < N_HEADS, jnp.minimum(z, 0.0) - jnp.log(1.0 + jnp.exp(-jnp.abs(z))), 0.0)
        row = lax.broadcasted_iota(jnp.int32, (tm, tm), 0)
        col = lax.broadcasted_iota(jnp.int32, (tm, tm), 1)
        tri = (row >= col).astype(BF)
        fc = _dot_exact(tri, lf) + carry[0:1, :]
        carry[...] = jnp.broadcast_to(fc[tm - 1:tm, :], carry.shape)
        c, s = cos_ref[...], sin_ref[...]

        def section(n):
            p = _dot_nt(n1, wm_ref[n * GROUP_W:(n + 1) * GROUP_W, :])
            proj_ref[:, n * GROUP_W:(n + 1) * GROUP_W] = p.astype(BF)
            return p

        def rotate(p, out_ref, scale):
            for sl in _chunks(GROUP_W):
                out_ref[:, sl] = ((p[:, sl] * c + _swap32(p[:, sl]) * s) * scale).astype(BF)

        def norm_aug(p, gain, out_ref, scale, is_query):
            for j, sl in enumerate(_chunks(GROUP_W)):
                f = p[:, sl]
                f = f * lax.rsqrt(_group_mean64(f * f) + EPS) * gain * scale
                out_ref[:, 2 * j * LANES:2 * (j + 1) * LANES] = _aug_pair(f, [fc[:, 2 * j:2 * j + 1], fc[:, 2 * j + 1:2 * j + 2]], is_query)

        p_rq, p_rk = section(0), section(1)
        rotate(p_rq, rq_ref, 0.125)
        section(2)
        rotate(p_rk, rk_ref, 1.0)
        section(3)
        p_fq = section(4)
        p_fk = section(5)
        norm_aug(p_fq, gq_ref[...], qa_ref, 0.125 * LOG2E, True)
        section(6)
        norm_aug(p_fk, gk_ref[...], ka_ref, 1.0, False)

    row_spec = lambda w: pl.BlockSpec((tm, w), lambda i: (i, 0))
    full = lambda a: pl.BlockSpec(a.shape, lambda i: (0,) * a.ndim)
    return pl.pallas_call(
        body, name="in_proj_fwd", grid=(n_t,),
        out_shape=(jax.ShapeDtypeStruct((t_len, D_MODEL), BF), jax.ShapeDtypeStruct((t_len, MAIN_W), BF),
                   jax.ShapeDtypeStruct((t_len, GROUP_W), BF), jax.ShapeDtypeStruct((t_len, GROUP_W), BF),
                   jax.ShapeDtypeStruct((t_len, 2 * GROUP_W), BF), jax.ShapeDtypeStruct((t_len, 2 * GROUP_W), BF),
                   jax.ShapeDtypeStruct((t_len, LANES), F32)),
        in_specs=[row_spec(D_MODEL), full(g_mix), *_w_in_specs(), full(b_pad), row_spec(LANES), row_spec(LANES),
                  full(gq_t), full(gk_t)],
        out_specs=(row_spec(D_MODEL), row_spec(MAIN_W), row_spec(GROUP_W), row_spec(GROUP_W), row_spec(2 * GROUP_W),
                   row_spec(2 * GROUP_W), row_spec(LANES)),
        scratch_shapes=[pltpu.VMEM((8, LANES), F32)],
        compiler_params=_cparams(("arbitrary",)),
    )(x, g_mix, w_in_t, w_in_t, b_pad, cos_t, sin_t, gq_t, gk_t)


def _w_in_specs():
    return (pl.BlockSpec((MAIN_W, D_MODEL), lambda i: (0, 0)), pl.BlockSpec((LANES, D_MODEL), lambda i: (MAIN_W // LANES, 0)))


def _matmul_tn_pair(a1, a2, b, name, bk=1024):
    t_len, m = a1.shape
    n = b.shape[1]
    bm = m if m <= TN_MAX_ROWS else m // 2
    bk = min(bk, t_len)

    def body(a1_ref, a2_ref, b_ref, o_ref):
        @pl.when(pl.program_id(1) == 0)
        def _():
            o_ref[...] = jnp.zeros_like(o_ref)

        bv = b_ref[...]
        o_ref[0] += _dot_tn(a1_ref[...], bv)
        o_ref[1] += _dot_tn(a2_ref[...], bv)

    a_spec = pl.BlockSpec((bk, bm), lambda i, k: (k, i))
    return pl.pallas_call(
        body, name=name, grid=(m // bm, t_len // bk),
        out_shape=jax.ShapeDtypeStruct((2, m, n), F32),
        in_specs=[a_spec, a_spec, pl.BlockSpec((bk, n), lambda i, k: (k, 0))],
        out_specs=pl.BlockSpec((2, bm, n), lambda i, k: (0, i, 0)),
        compiler_params=_cparams(("arbitrary", "arbitrary")),
    )(a1, a2, b)


def _decay_tables(c):
    h = np.arange(N_HEADS, dtype=np.float64)
    lg = np.log(1.0 - 2.0 ** (-5.0 - h)).astype(np.float32).astype(np.float64)
    t = np.arange(c)
    same_or_earlier = (t[None, :] // REF_CHUNK) <= (t[:, None] // REF_CHUNK)
    w = np.where(same_or_earlier[None], np.exp(lg[:, None, None] * np.abs(t[:, None] - t[None, :])[None]), 0.0)
    qd = np.exp(lg[:, None] * (t[None, :] + 1.0))
    kd = np.exp(lg[:, None] * (c - 1.0 - t[None, :]))
    cd = np.exp(lg * c)
    ones = np.ones((1, 1, HEAD_DIM))
    return (jnp.asarray(w, F32), jnp.asarray(qd[:, :, None] * ones, F32), jnp.asarray(kd[:, :, None] * ones, F32),
            jnp.asarray(cd[:, None, None] * np.ones((1, HEAD_DIM, HEAD_DIM)), F32))


def _retention_fwd(rq, rk, proj, g_ret, tables):
    t_len = rq.shape[0]
    c = min(RET_BLOCK, t_len)
    n_b = t_len // c
    wdec, qdec, kdec, cdec = tables
    v_col, g_col = 2 * GROUP_W // LANES, 3 * GROUP_W // LANES

    def body(q_ref, k_ref, v_ref, rg_ref, g_ref, w_ref, qd_ref, kd_ref, cd_ref, raw_ref, mix_ref, st_ref, state):
        i = pl.program_id(1)

        @pl.when(i == 0)
        def _():
            state[...] = jnp.zeros_like(state)

        q2, k2, v2 = q_ref[...], k_ref[...], v_ref[...]
        heads = [tuple(t[:, hh * HEAD_DIM:(hh + 1) * HEAD_DIM] for t in (q2, k2, v2)) for hh in range(2)]
        scores = [(_dot_nt(q, k) * w_ref[hh]).astype(BF) for hh, (q, k, _) in enumerate(heads)]
        outs = []
        for hh, (q, k, v) in enumerate(heads):
            sp = state[hh]
            st_ref[0, 0, hh] = sp
            outs.append(jnp.dot(scores[hh], v, preferred_element_type=F32) + _dot(q.astype(F32) * qd_ref[hh], sp))
            state[hh] = sp * cd_ref[hh] + _dot_tn(k.astype(F32) * kd_ref[hh], v)
        o2 = jnp.concatenate(outs, axis=-1)
        raw_ref[...] = o2
        xc = o2 - _group_mean64(o2)
        xh = xc * lax.rsqrt(_group_mean64(xc * xc) + EPS)
        gate = rg_ref[...].astype(F32)
        mix_ref[...] = (gate * _sigmoid(gate) * (xh * g_ref[0])).astype(BF)

    blk = lambda col0: pl.BlockSpec((c, LANES), lambda hp, i: (i, col0 + hp))
    tab = lambda a: pl.BlockSpec((2,) + a.shape[1:], lambda hp, i: (hp, 0, 0))
    return pl.pallas_call(
        body, name="retention_fwd", grid=(N_HEADS // 2, n_b),
        out_shape=(jax.ShapeDtypeStruct((t_len, GROUP_W), F32), jax.ShapeDtypeStruct((t_len, GROUP_W), BF),
                   jax.ShapeDtypeStruct((N_HEADS // 2, n_b, 2, HEAD_DIM, HEAD_DIM), F32)),
        in_specs=[blk(0), blk(0), blk(v_col), blk(g_col), pl.BlockSpec((1, 1, LANES), lambda hp, i: (hp, 0, 0)),
                  tab(wdec), tab(qdec), tab(kdec), tab(cdec)],
        out_specs=(blk(0), blk(0), pl.BlockSpec((1, 1, 2, HEAD_DIM, HEAD_DIM), lambda hp, i: (hp, i, 0, 0, 0))),
        scratch_shapes=[pltpu.VMEM((2, HEAD_DIM, HEAD_DIM), F32)],
        compiler_params=_cparams(("arbitrary", "arbitrary")),
    )(rq, rk, proj, proj, g_ret, wdec, qdec, kdec, cdec)


def _fox_fwd(q_aug, k_aug, proj, shards):
    t_len = q_aug.shape[0]
    tq = min(ATT_BLOCK, t_len)
    nsub = min(FWD_GROUP, t_len // tq)
    tg = nsub * tq
    n_q = t_len // tg
    v_col = 6 * GROUP_W // LANES
    tc = min(512, t_len)
    n_w = len(shards)
    n_steps = (N_HEADS // 2) * n_q

    def body(*refs):
        q_ref, k_ref, v_ref = refs[:3]
        o_ref, o32_ref, lse_ref = refs[3 + n_w:6 + n_w]
        vt = refs[6 + 2 * n_w]
        comm = (refs[3:3 + n_w], refs[6 + n_w:6 + 2 * n_w]) + tuple(refs[7 + 2 * n_w:])
        i = pl.program_id(1)
        step = pl.program_id(0) * n_q + i

        @pl.when(step == 0)
        def _():
            _gather_phase(0, *comm)

        @pl.when(step == (3 * n_steps) // 4)
        def _():
            _gather_phase(1, *comm)

        @pl.when(i == 0)
        def _():
            for c0 in range(0, t_len, tc):
                vt[:, c0:c0 + tc] = v_ref[c0:c0 + tc, :].T

        chains = [(u, hh) for u in range(nsub) for hh in range(2)]
        qs = {(u, hh): q_ref[u * tq:(u + 1) * tq, hh * LANES:(hh + 1) * LANES] for u, hh in chains}
        ones = jnp.ones((HEAD_DIM, tq), BF)

        def scores(j, which):
            k2 = k_ref[pl.ds(pl.multiple_of(j * tq, tq), tq), :]
            return {ch: _dot_nt(k2[:, ch[1] * LANES:(ch[1] + 1) * LANES], qs[ch]) for ch in which}

        def update(j, ss, carry, masked):
            v2 = vt[:, pl.ds(pl.multiple_of(j * tq, tq), tq)]
            ps, stats = {}, {}
            for ch in ss:
                m = carry[ch][0]
                s_t = ss[ch]
                if ch in masked:
                    krow = lax.broadcasted_iota(jnp.int32, (tq, tq), 0)
                    qcol = lax.broadcasted_iota(jnp.int32, (tq, tq), 1)
                    s_t = jnp.where(qcol >= krow, s_t, NEG)
                m_new = jnp.maximum(m, jnp.max(s_t, axis=0, keepdims=True))
                ps[ch] = jnp.exp2(s_t - m_new).astype(BF)
                stats[ch] = (m_new, jnp.exp2(m - m_new))
            out = dict(carry)
            for ch in ss:
                m_new, alpha = stats[ch]
                v_aug = jnp.concatenate([v2[ch[1] * HEAD_DIM:(ch[1] + 1) * HEAD_DIM, :], ones], axis=0)
                out[ch] = (m_new, carry[ch][1] * alpha + jnp.dot(v_aug, ps[ch], preferred_element_type=F32))
            return out

        def advance(j, state):
            ss, carry = state
            return scores(j + 1, chains), update(j, ss, carry, ())

        init = {ch: (jnp.full((1, tq), NEG, F32), jnp.zeros((LANES, tq), F32)) for ch in chains}
        first = nsub * i
        ss, carry = lax.fori_loop(0, first, advance, (scores(0, chains), init))
        carry = update(first, ss, carry, [(0, 0), (0, 1)])
        for u in range(1, nsub):
            rest = [(uu, hh) for uu in range(u, nsub) for hh in range(2)]
            carry = update(first + u, scores(first + u, rest), carry, [(u, 0), (u, 1)])
        for u in range(nsub):
            outs, lses = [], []
            for hh in range(2):
                m, acc = carry[u, hh]
                l = acc[HEAD_DIM:HEAD_DIM + 1, :]
                outs.append(acc[:HEAD_DIM, :] / l)
                lses.append(m + jnp.log2(l))
            o2 = jnp.concatenate(outs, axis=0).T
            o32_ref[u * tq:(u + 1) * tq, :] = o2
            o_ref[u * tq:(u + 1) * tq, :] = o2.astype(BF)
            lse_ref[0, :, u * tq:(u + 1) * tq] = jnp.concatenate(lses, axis=0)

        @pl.when(step == n_steps - 1)
        def _():
            _gather_phase(2, *comm)

    return pl.pallas_call(
        body, name="fox_fwd", grid=(N_HEADS // 2, n_q),
        out_shape=(jax.ShapeDtypeStruct((t_len, GROUP_W), BF), jax.ShapeDtypeStruct((t_len, GROUP_W), F32),
                   jax.ShapeDtypeStruct((N_HEADS // 2, 2, t_len), F32))
        + tuple(jax.ShapeDtypeStruct((4,) + s.shape, s.dtype) for s in shards),
        in_specs=[pl.BlockSpec((tg, 2 * LANES), lambda hp, i: (i, hp)),
                  pl.BlockSpec((t_len, 2 * LANES), lambda hp, i: (0, hp)),
                  pl.BlockSpec((t_len, LANES), lambda hp, i: (0, v_col + hp))] + [ANY] * n_w,
        out_specs=(pl.BlockSpec((tg, LANES), lambda hp, i: (i, hp)), pl.BlockSpec((tg, LANES), lambda hp, i: (i, hp)),
                   pl.BlockSpec((1, 2, tg), lambda hp, i: (hp, 0, i))) + (ANY,) * n_w,
        scratch_shapes=[pltpu.VMEM((LANES, t_len), BF)] + _gather_scratch(n_w),
        compiler_params=_cparams(("arbitrary", "arbitrary")),
    )(q_aug, k_aug, proj, *shards)


def _softmax_rows(s):
    p = jnp.exp(s - jnp.max(s, axis=-1, keepdims=True))
    return p / jnp.sum(p, axis=-1, keepdims=True)


def _attn_out_xattn_fwd(x, mix_r, mix_f, w_out, g_xattn, w_xq, g_xq, kn, v, w_xo):
    t_len = x.shape[0]
    tm = min(ROW_TILE, t_len)

    def body(x_ref, mr_ref, mf_ref, wo_ref, g_ref, wq_ref, gq_ref, kn_ref, v_ref, wxo_ref,
             h1_ref, hn_ref, qx_ref, o_ref, h2_ref):
        h1 = x_ref[...] + jnp.dot(mr_ref[...], wo_ref[:GROUP_W, :], preferred_element_type=F32) \
            + jnp.dot(mf_ref[...], wo_ref[GROUP_W:, :], preferred_element_type=F32)
        h1_ref[...] = h1
        hn = _rms_fwd(h1, g_ref[...]).astype(BF)
        hn_ref[...] = hn
        qx = jnp.dot(hn, wq_ref[...], preferred_element_type=F32).astype(BF)
        qx_ref[...] = qx
        sls = [slice(h * XHD, (h + 1) * XHD) for h in range(N_XH)]
        qns = [_rms_fwd(qx[:, sl].astype(F32), gq_ref[...]).astype(BF) for sl in sls]
        logits = [_dot_nt(qn, kn_ref[:, sl]) * (XHD ** -0.5) for qn, sl in zip(qns, sls)]
        ps = [_softmax_rows(s).astype(BF) for s in logits]
        for p, sl in zip(ps, sls):
            o_ref[:, sl] = jnp.dot(p, v_ref[:, sl], preferred_element_type=F32).astype(BF)
        h2_ref[...] = h1 + jnp.dot(o_ref[...], wxo_ref[...], preferred_element_type=F32)

    row_spec = lambda w: pl.BlockSpec((tm, w), lambda i: (i, 0))
    full = lambda a: pl.BlockSpec(a.shape, lambda i: (0,) * a.ndim)
    return pl.pallas_call(
        body, name="attn_out_xattn_fwd", grid=(t_len // tm,),
        out_shape=(jax.ShapeDtypeStruct((t_len, D_MODEL), F32), jax.ShapeDtypeStruct((t_len, D_MODEL), BF),
                   jax.ShapeDtypeStruct((t_len, D_MODEL), BF), jax.ShapeDtypeStruct((t_len, D_MODEL), BF),
                   jax.ShapeDtypeStruct((t_len, D_MODEL), F32)),
        in_specs=[row_spec(D_MODEL), row_spec(GROUP_W), row_spec(GROUP_W), full(w_out), full(g_xattn), full(w_xq), full(g_xq),
                  full(kn), full(v), full(w_xo)],
        out_specs=(row_spec(D_MODEL),) * 5,
        compiler_params=_cparams(("arbitrary",)),
    )(x, mix_r, mix_f, w_out, g_xattn, w_xq, g_xq, kn, v, w_xo)


def _ffn_loss_fwd(h2, g_ffn, w_gate, w_up, w_down, target):
    t_len = h2.shape[0]
    tm = min(ROW_TILE, t_len)

    def body(h2_ref, g_ref, wg_ref, wu_ref, wd_ref, tgt_ref, hn_ref, gate_ref, up_ref, act_ref, dh3_ref, loss_ref):
        @pl.when(pl.program_id(0) == 0)
        def _():
            loss_ref[...] = jnp.zeros_like(loss_ref)

        h2v = h2_ref[...]
        hn = _rms_fwd(h2v, g_ref[...]).astype(BF)
        hn_ref[...] = hn
        gate = _dot_nt(hn, wg_ref[...])
        up = _dot_nt(hn, wu_ref[...])
        gate_ref[...] = gate.astype(BF)
        up_ref[...] = up.astype(BF)
        act = (gate * _sigmoid(gate) * up).astype(BF)
        act_ref[...] = act
        diff = h2v + jnp.dot(act, wd_ref[...], preferred_element_type=F32) - tgt_ref[...]
        dh3_ref[...] = diff * (1.0 / D_MODEL)
        per_row = jnp.sum(diff * diff, axis=-1, keepdims=True) * (1.0 / D_MODEL)
        loss_ref[...] += 0.5 * jnp.sum(per_row, axis=0, keepdims=True)

    row_spec = lambda w: pl.BlockSpec((tm, w), lambda i: (i, 0))
    full = lambda a: pl.BlockSpec(a.shape, lambda i: (0,) * a.ndim, pipeline_mode=pl.Buffered(1))
    return pl.pallas_call(
        body, name="ffn_loss_fwd", grid=(t_len // tm,),
        out_shape=(jax.ShapeDtypeStruct((t_len, D_MODEL), BF), jax.ShapeDtypeStruct((t_len, D_FF), BF),
                   jax.ShapeDtypeStruct((t_len, D_FF), BF), jax.ShapeDtypeStruct((t_len, D_FF), BF),
                   jax.ShapeDtypeStruct((t_len, D_MODEL), F32), jax.ShapeDtypeStruct((8, LANES), F32)),
        in_specs=[row_spec(D_MODEL), full(g_ffn), full(w_gate), full(w_up), full(w_down), row_spec(D_MODEL)],
        out_specs=(row_spec(D_MODEL), row_spec(D_FF), row_spec(D_FF), row_spec(D_FF), row_spec(D_MODEL),
                   pl.BlockSpec((8, LANES), lambda i: (0, 0))),
        compiler_params=_cparams(("arbitrary",)),
    )(h2, g_ffn, w_gate, w_up, w_down, target)


def _ffn_bwd(dh3, gate, up, h2, g_ffn, w_gate, w_up, w_down):
    t_len = h2.shape[0]
    tm = min(FFN_BWD_TILE, t_len)

    def body(dh3_ref, gate_ref, up_ref, h2_ref, g_ref, wg_ref, wu_ref, wd_ref, dgate_ref, dup_ref, dh2_ref, dg_ref):
        @pl.when(pl.program_id(0) == 0)
        def _():
            dg_ref[...] = jnp.zeros_like(dg_ref)

        dh3v = dh3_ref[...]
        dact = _dot_nt(dh3v, wd_ref[...])
        g = gate_ref[...].astype(F32)
        sg = _sigmoid(g)
        dup = (dact * (g * sg)).astype(BF)
        dgate = (dact * up_ref[...].astype(F32) * (sg * (1.0 + g * (1.0 - sg)))).astype(BF)
        dup_ref[...] = dup
        dgate_ref[...] = dgate
        dhn = jnp.dot(dgate, wg_ref[...], preferred_element_type=F32) + jnp.dot(dup, wu_ref[...], preferred_element_type=F32)
        dx, dg = _rms_bwd(h2_ref[...], g_ref[...], dhn)
        dh2_ref[...] = dh3v + dx
        dg_ref[...] += dg

    row_spec = lambda w: pl.BlockSpec((tm, w), lambda i: (i, 0))
    full = lambda a: pl.BlockSpec(a.shape, lambda i: (0,) * a.ndim, pipeline_mode=pl.Buffered(1))
    return pl.pallas_call(
        body, name="ffn_bwd", grid=(t_len // tm,),
        out_shape=(jax.ShapeDtypeStruct((t_len, D_FF), BF), jax.ShapeDtypeStruct((t_len, D_FF), BF),
                   jax.ShapeDtypeStruct((t_len, D_MODEL), F32), jax.ShapeDtypeStruct((1, D_MODEL), F32)),
        in_specs=[row_spec(D_MODEL), row_spec(D_FF), row_spec(D_FF), row_spec(D_MODEL), full(g_ffn), full(w_gate), full(w_up),
                  full(w_down)],
        out_specs=(row_spec(D_FF), row_spec(D_FF), row_spec(D_MODEL), pl.BlockSpec((1, D_MODEL), lambda i: (0, 0))),
        compiler_params=_cparams(("arbitrary",)),
    )(dh3, gate, up, h2, g_ffn, w_gate, w_up, w_down)


def _attn_out_xattn_bwd(dh2, h1, qx, kn, v, w_xo, w_xq, w_out, g_xattn, g_xq):
    t_len = h1.shape[0]
    tm = min(ROW_TILE, t_len)
    m_tok = kn.shape[0]

    def body(dh2_ref, h1_ref, qx_ref, kn_ref, v_ref, wxo_ref, wq_ref, wo_ref, g_ref, gq_ref,
             dqx_ref, dh1_ref, dmr_ref, dmf_ref, dkn_ref, dv_ref, dg_ref, dgq_ref, dqx_scr):
        @pl.when(pl.program_id(0) == 0)
        def _():
            dkn_ref[...] = jnp.zeros_like(dkn_ref)
            dv_ref[...] = jnp.zeros_like(dv_ref)
            dg_ref[...] = jnp.zeros_like(dg_ref)
            dgq_ref[...] = jnp.zeros_like(dgq_ref)

        dh2v = dh2_ref[...]
        do = _dot_nt(dh2v, wxo_ref[...])
        gq = gq_ref[...]
        sls = [slice(h * XHD, (h + 1) * XHD) for h in range(N_XH)]
        qraws = [qx_ref[:, sl].astype(F32) for sl in sls]
        qns = [_rms_fwd(qraw, gq).astype(BF) for qraw in qraws]
        dohs = [do[:, sl].astype(BF) for sl in sls]
        logits = [_dot_nt(qn, kn_ref[:, sl]) * (XHD ** -0.5) for qn, sl in zip(qns, sls)]
        dps = [_dot_nt(doh, v_ref[:, sl]) for doh, sl in zip(dohs, sls)]
        ps = [_softmax_rows(s) for s in logits]
        dss = [(p * (dp - jnp.sum(dp * p, axis=-1, keepdims=True)) * (XHD ** -0.5)).astype(BF) for p, dp in zip(ps, dps)]
        dqns = []
        for h, sl in enumerate(sls):
            dv_ref[:, sl] += _dot_tn(ps[h], dohs[h])
            dqns.append(jnp.dot(dss[h], kn_ref[:, sl], preferred_element_type=F32))
            dkn_ref[:, sl] += _dot_tn(dss[h], qns[h])
        dgq = jnp.zeros((1, XHD), F32)
        for h, sl in enumerate(sls):
            dx, dg_h = _rms_bwd(qraws[h], gq, dqns[h])
            dgq = dgq + dg_h
            dqx_scr[:, sl] = dx.astype(BF)
        dgq_ref[...] += dgq
        dqx = dqx_scr[...]
        dqx_ref[...] = dqx
        dhn = _dot_nt(dqx, wq_ref[...])
        dx, dg = _rms_bwd(h1_ref[...], g_ref[...], dhn)
        dg_ref[...] += dg
        dh1 = dh2v + dx
        dh1_ref[...] = dh1
        dmix = _dot_nt(dh1, wo_ref[...])
        dmr_ref[...] = dmix[:, :GROUP_W]
        dmf_ref[...] = dmix[:, GROUP_W:].astype(BF)

    row_spec = lambda w: pl.BlockSpec((tm, w), lambda i: (i, 0))
    full = lambda a: pl.BlockSpec(a.shape, lambda i: (0,) * a.ndim)
    acc = lambda r, c: pl.BlockSpec((r, c), lambda i: (0, 0))
    return pl.pallas_call(
        body, name="attn_out_xattn_bwd", grid=(t_len // tm,),
        out_shape=(jax.ShapeDtypeStruct((t_len, D_MODEL), BF), jax.ShapeDtypeStruct((t_len, D_MODEL), F32),
                   jax.ShapeDtypeStruct((t_len, GROUP_W), F32), jax.ShapeDtypeStruct((t_len, GROUP_W), BF),
                   jax.ShapeDtypeStruct((m_tok, D_MODEL), F32), jax.ShapeDtypeStruct((m_tok, D_MODEL), F32),
                   jax.ShapeDtypeStruct((1, D_MODEL), F32), jax.ShapeDtypeStruct((1, XHD), F32)),
        in_specs=[row_spec(D_MODEL), row_spec(D_MODEL), row_spec(D_MODEL), full(kn), full(v), full(w_xo), full(w_xq), full(w_out),
                  full(g_xattn), full(g_xq)],
        out_specs=(row_spec(D_MODEL), row_spec(D_MODEL), row_spec(GROUP_W), row_spec(GROUP_W), acc(m_tok, D_MODEL),
                   acc(m_tok, D_MODEL), acc(1, D_MODEL), acc(1, XHD)),
        scratch_shapes=[pltpu.VMEM((tm, D_MODEL), BF)],
        compiler_params=_cparams(("arbitrary",)),
    )(dh2, h1, qx, kn, v, w_xo, w_xq, w_out, g_xattn, g_xq)


def _mem_kv_bwd(dkn, dv, kraw, mem, memn, g_mem, g_xk, w_xkv):
    m_tok = mem.shape[0]

    def body(dkn_ref, dv_ref, kraw_ref, mem_ref, memn_ref, gm_ref, gk_ref, w_ref, dw_ref, dgm_ref, dgk_ref, dkv_scr):
        gk = gk_ref[...]
        dgk = jnp.zeros((1, XHD), F32)
        for h in range(N_XH):
            sl = slice(h * XHD, (h + 1) * XHD)
            dx, dg_h = _rms_bwd(kraw_ref[:, sl], gk, dkn_ref[:, sl])
            dgk = dgk + dg_h
            dkv_scr[:, sl] = dx.astype(BF)
        dgk_ref[...] = dgk
        dkv_scr[:, D_MODEL:] = dv_ref[...].astype(BF)
        dkv = dkv_scr[...]
        dw_ref[...] = _dot_tn(memn_ref[...], dkv)
        dmemn = _dot_nt(dkv, w_ref[...])
        mem_v = mem_ref[...]
        r = lax.rsqrt(jnp.mean(mem_v * mem_v, axis=-1, keepdims=True) + EPS)
        dgm_ref[...] = jnp.sum(dmemn * mem_v * r, axis=0, keepdims=True)

    return pl.pallas_call(
        body, name="mem_kv_bwd",
        out_shape=(jax.ShapeDtypeStruct((D_MODEL, 2 * D_MODEL), F32), jax.ShapeDtypeStruct((1, D_MODEL), F32),
                   jax.ShapeDtypeStruct((1, XHD), F32)),
        in_specs=[VMEM_SPEC] * 8, out_specs=(VMEM_SPEC,) * 3,
        scratch_shapes=[pltpu.VMEM((m_tok, 2 * D_MODEL), BF)],
        compiler_params=_cparams(),
    )(dkn, dv, kraw, mem, memn, g_mem, g_xk, w_xkv)


def _fox_bwd(q_aug, k_aug, proj, dmf, o32, lse, sums):
    t_len = q_aug.shape[0]
    tb = min(ATT_BLOCK, t_len)
    n_b = t_len // tb
    nsub = 2 if n_b >= 2 else 1
    tg = nsub * tb
    n_g = t_len // tg
    v_col = 6 * GROUP_W // LANES
    n_w = len(sums)
    n_steps = (N_HEADS // 2) * n_g

    def body(*refs):
        k_ref, v_ref, q_ref, do_ref, o_ref, lse_ref = refs[:6]
        dq_ref, dk_ref, dv_ref, df_ref = refs[6 + n_w:10 + n_w]
        delta = refs[10 + 2 * n_w]
        comm = (refs[6:6 + n_w], refs[10 + n_w:10 + 2 * n_w]) + tuple(refs[11 + 2 * n_w:])
        j = pl.program_id(1)
        step = pl.program_id(0) * n_g + j

        @pl.when(step == 0)
        def _():
            _scatter_phase(0, *comm)

        @pl.when(j == 0)
        def _():
            dq_ref[...] = jnp.zeros_like(dq_ref)
            dd = do_ref[...].astype(F32) * o_ref[...]
            hrow = lax.broadcasted_iota(jnp.int32, (8, LANES), 0)
            lane = lax.broadcasted_iota(jnp.int32, (8, LANES), 1)
            ind = ((lane // HEAD_DIM) == hrow).astype(BF)
            delta[...] = _dot_nt_exact(ind, dd)

        k2, v2 = k_ref[...], v_ref[...]
        chains = [(u, hh) for u in range(nsub) for hh in range(2)]
        ks = {(u, hh): k2[u * tb:(u + 1) * tb, hh * LANES:(hh + 1) * LANES] for u, hh in chains}
        vs = {(u, hh): v2[u * tb:(u + 1) * tb, hh * HEAD_DIM:(hh + 1) * HEAD_DIM] for u, hh in chains}

        def block(i, carry, which, masked):
            rows = pl.ds(pl.multiple_of(i * tb, tb), tb)
            q2 = q_ref[rows, :]
            do2 = do_ref[rows, :]
            qs = [q2[:, hh * LANES:(hh + 1) * LANES] for hh in range(2)]
            dos = [do2[:, hh * HEAD_DIM:(hh + 1) * HEAD_DIM] for hh in range(2)]
            ss = {ch: _dot_nt(ks[ch], qs[ch[1]]) for ch in which}
            dps = {ch: _dot_nt(vs[ch], dos[ch[1]]) for ch in which}
            pts, dsts, dfs = {}, {}, {}
            for ch in which:
                hh = ch[1]
                s_t = ss[ch]
                if ch in masked:
                    krow = lax.broadcasted_iota(jnp.int32, (tb, tb), 0)
                    qcol = lax.broadcasted_iota(jnp.int32, (tb, tb), 1)
                    s_t = jnp.where(qcol >= krow, s_t, NEG)
                p_t = jnp.exp2(s_t - lse_ref[0, hh:hh + 1, rows])
                pts[ch] = p_t.astype(BF)
                ds_t = p_t * (dps[ch] - delta[hh:hh + 1, rows])
                dsts[ch] = ds_t.astype(BF)
                dfs[ch] = jnp.sum(ds_t, axis=-1, keepdims=True)
            out = dict(carry)
            for ch in which:
                dk, dv, df = carry[ch]
                dv = dv + jnp.dot(pts[ch], dos[ch[1]], preferred_element_type=F32)
                dk = dk + jnp.dot(dsts[ch], qs[ch[1]], preferred_element_type=F32)
                out[ch] = (dk, dv, df - dfs[ch])
            for hh in range(2):
                parts_dq = [_dot_tn(dsts[ch], ks[ch])[:, :HEAD_DIM] for ch in which if ch[1] == hh]
                dq_ref[rows, hh * HEAD_DIM:(hh + 1) * HEAD_DIM] += sum(parts_dq[1:], parts_dq[0])
            return out

        init = {ch: (jnp.zeros((tb, LANES), F32), jnp.zeros((tb, HEAD_DIM), F32), jnp.zeros((tb, 1), F32)) for ch in chains}
        first = nsub * j
        carry = block(first, init, [(0, 0), (0, 1)], [(0, 0), (0, 1)])
        if nsub == 2:
            carry = block(first + 1, carry, chains, [(1, 0), (1, 1)])
        carry = lax.fori_loop(first + nsub, n_b, lambda i, c: block(i, c, chains, ()), carry)
        for u in range(nsub):
            rs = slice(u * tb, (u + 1) * tb)
            dk_ref[rs, :] = jnp.concatenate([carry[u, hh][0][:, :HEAD_DIM] for hh in range(2)], axis=-1) * LN2
            dv_ref[rs, :] = jnp.concatenate([carry[u, hh][1] for hh in range(2)], axis=-1)
            df_ref[0, rs, :] = jnp.concatenate([carry[u, hh][2] for hh in range(2)], axis=-1)

        @pl.when(step == n_steps - 1)
        def _():
            _scatter_phase(1, *comm)

    blk = lambda w, col0: pl.BlockSpec((tg, w), lambda hp, j: (j, col0 + hp))
    whole = lambda w: pl.BlockSpec((t_len, w), lambda hp, j: (0, hp))
    rows2 = pl.BlockSpec((1, 2, t_len), lambda hp, j: (hp, 0, 0))
    cols2 = pl.BlockSpec((1, tg, 2), lambda hp, j: (hp, j, 0))
    return pl.pallas_call(
        body, name="fox_bwd", grid=(N_HEADS // 2, n_g),
        out_shape=(jax.ShapeDtypeStruct((t_len, GROUP_W), F32), jax.ShapeDtypeStruct((t_len, GROUP_W), F32),
                   jax.ShapeDtypeStruct((t_len, GROUP_W), F32), jax.ShapeDtypeStruct((N_HEADS // 2, t_len, 2), F32))
        + _scatter_out_shapes(sums),
        in_specs=[blk(2 * LANES, 0), blk(LANES, v_col), whole(2 * LANES), whole(LANES), whole(LANES), rows2] + [ANY] * n_w,
        out_specs=(whole(LANES), blk(LANES, 0), blk(LANES, 0), cols2) + (ANY,) * n_w,
        scratch_shapes=[pltpu.VMEM((8, t_len), F32)] + _scatter_scratch(n_w),
        compiler_params=_cparams(("arbitrary", "arbitrary")),
    )(k_aug, proj, q_aug, dmf, o32, lse, *sums)


def _retention_bwd(dmr, raw, proj, g_ret, rq, rk, states, tables, parts):
    t_len = rq.shape[0]
    c = min(RET_BLOCK, t_len)
    n_b = t_len // c
    wdec, qdec, kdec, cdec = tables
    v_col, g_col = 2 * GROUP_W // LANES, 3 * GROUP_W // LANES
    n_w = len(parts)
    n_steps = (N_HEADS // 2) * n_b

    def body(*refs):
        d_ref, raw_ref, rg_ref, g_ref, q_ref, k_ref, v_ref, st_ref, w_ref, wt_ref, qd_ref, kd_ref, cd_ref = refs[:13]
        dq_ref, dk_ref, dv_ref, drg_ref, dg_ref = refs[13 + n_w:18 + n_w]
        gstate = refs[18 + 2 * n_w]
        comm = (refs[13:13 + n_w], refs[18 + n_w:18 + 2 * n_w]) + tuple(refs[19 + 2 * n_w:])
        step = pl.program_id(0) * n_b + pl.program_id(1)

        @pl.when(step == 0)
        def _():
            _exchange_phase(0, *comm)

        @pl.when(pl.program_id(1) == 0)
        def _():
            gstate[...] = jnp.zeros_like(gstate)
            dg_ref[...] = jnp.zeros_like(dg_ref)

        d, raw_v, g = d_ref[...], raw_ref[...], g_ref[0]
        gate = rg_ref[...].astype(F32)
        xc = raw_v - _group_mean64(raw_v)
        r = lax.rsqrt(_group_mean64(xc * xc) + EPS)
        xh = xc * r
        sg = _sigmoid(gate)
        drg_ref[...] = d * (xh * g) * (sg * (1.0 + gate * (1.0 - sg)))
        dy = d * (gate * sg)
        dg_ref[0] += jnp.sum(dy * xh, axis=0, keepdims=True)
        dxh = dy * g
        do2 = r * (dxh - _group_mean64(dxh) - xh * _group_mean64(dxh * xh))
        q2, k2, v2 = q_ref[...], k_ref[...], v_ref[...]
        dqs, dks, dvs = [], [], []
        heads = [tuple(t[:, hh * HEAD_DIM:(hh + 1) * HEAD_DIM] for t in (q2, k2, v2, do2.astype(BF))) for hh in range(2)]
        firsts = [(_dot_nt(k, q) * wt_ref[hh], _dot_nt(do, v) * w_ref[hh], _dot_nt(v, do) * wt_ref[hh])
                  for hh, (q, k, v, do) in enumerate(heads)]
        for hh, (q, k, v, do) in enumerate(heads):
            a_t, dm, dm_t = firsts[hh]
            sp, gs = st_ref[0, 0, hh], gstate[hh]
            qd = q.astype(F32) * qd_ref[hh]
            kd = k.astype(F32) * kd_ref[hh]
            dqs.append(_dot(dm, k) + _dot_nt(do, sp) * qd_ref[hh])
            dks.append(_dot(dm_t, q) + _dot_nt(v, gs) * kd_ref[hh])
            dvs.append(_dot(a_t, do) + _dot(kd, gs))
            gstate[hh] = gs * cd_ref[hh] + _dot_tn(qd, do)
        dq_ref[...] = jnp.concatenate(dqs, axis=-1)
        dk_ref[...] = jnp.concatenate(dks, axis=-1)
        dv_ref[...] = jnp.concatenate(dvs, axis=-1)

        @pl.when(step == n_steps - 1)
        def _():
            _exchange_phase(1, *comm)

    blk = lambda col0: pl.BlockSpec((c, LANES), lambda hp, i: (n_b - 1 - i, col0 + hp))
    tab = lambda a: pl.BlockSpec((2,) + a.shape[1:], lambda hp, i: (hp, 0, 0))
    gspec = pl.BlockSpec((1, 1, LANES), lambda hp, i: (hp, 0, 0))
    return pl.pallas_call(
        body, name="retention_bwd", grid=(N_HEADS // 2, n_b),
        out_shape=(jax.ShapeDtypeStruct((t_len, GROUP_W), F32),) * 4 + (jax.ShapeDtypeStruct((N_HEADS // 2, 1, LANES), F32),)
        + _exchange_out_shapes(parts),
        in_specs=[blk(0), blk(0), blk(g_col), gspec, blk(0), blk(0), blk(v_col),
                  pl.BlockSpec((1, 1, 2, HEAD_DIM, HEAD_DIM), lambda hp, i: (hp, n_b - 1 - i, 0, 0, 0)),
                  tab(wdec), tab(wdec), tab(qdec), tab(kdec), tab(cdec)] + [ANY] * n_w,
        out_specs=(blk(0), blk(0), blk(0), blk(0), gspec) + (ANY,) * n_w,
        scratch_shapes=[pltpu.VMEM((2, HEAD_DIM, HEAD_DIM), F32)] + _exchange_scratch(n_w),
        compiler_params=_cparams(("arbitrary", "arbitrary")),
    )(dmr, raw, proj, g_ret, rq, rk, proj, states, wdec, jnp.transpose(wdec, (0, 2, 1)), qdec, kdec, cdec, *parts)


def _in_proj_bwd(x, g_mix, dh1, dq_r, dk_r, dv_r, drg, dq_f, dk_f, dv_f, df_col, proj, z, cos_t, sin_t, gq_t, gk_t, w_in_t):
    t_len = x.shape[0]
    tm = min(ROW_TILE, t_len)
    n_t = t_len // tm

    def body(x_ref, g_ref, dh1_ref, dqr_ref, dkr_ref, dvr_ref, drg_ref, dqf_ref, dkf_ref, dvf_ref, df_ref, fq_ref, fk_ref, z_ref,
             cos_ref, sin_ref, gq_ref, gk_ref, wm_ref, wf_ref,
             dproj_ref, dz_ref, dx_ref, dg_ref, dgq_ref, dgk_ref, db_ref, carry, gq_acc, gk_acc):
        i = pl.program_id(0)

        @pl.when(i == 0)
        def _():
            carry[...] = jnp.zeros_like(carry)
            gq_acc[...] = jnp.zeros_like(gq_acc)
            gk_acc[...] = jnp.zeros_like(gk_acc)
            dg_ref[...] = jnp.zeros_like(dg_ref)
            db_ref[...] = jnp.zeros_like(db_ref)

        c, s = cos_ref[...], sin_ref[...]
        gq, gk = gq_ref[...], gk_ref[...]
        dgq = jnp.zeros((1, LANES), F32)
        dgk = jnp.zeros((1, LANES), F32)
        for sl in _chunks(GROUP_W):
            dy = dqr_ref[:, sl] * 0.125
            dproj_ref[:, sl] = (dy * c + _swap32(dy * s)).astype(BF)
            dy = dkr_ref[:, sl]
            dproj_ref[:, GROUP_W + sl.start:GROUP_W + sl.stop] = (dy * c + _swap32(dy * s)).astype(BF)
            dproj_ref[:, 2 * GROUP_W + sl.start:2 * GROUP_W + sl.stop] = dvr_ref[:, sl].astype(BF)
            dproj_ref[:, 3 * GROUP_W + sl.start:3 * GROUP_W + sl.stop] = drg_ref[:, sl].astype(BF)
            for src, dsrc, gain, off in ((fq_ref, dqf_ref, gq, 4), (fk_ref, dkf_ref, gk, 5)):
                xr = src[:, sl].astype(F32)
                r = lax.rsqrt(_group_mean64(xr * xr) + EPS)
                xh = xr * r
                dy = dsrc[:, sl] * (0.125 if off == 4 else 1.0)
                dgs = jnp.sum(dy * xh, axis=0, keepdims=True)
                if off == 4:
                    dgq = dgq + dgs
                else:
                    dgk = dgk + dgs
                dxh = dy * gain
                dproj_ref[:, off * GROUP_W + sl.start:off * GROUP_W + sl.stop] = \
                    (r * (dxh - xh * _group_mean64(dxh * xh))).astype(BF)
            dproj_ref[:, 6 * GROUP_W + sl.start:6 * GROUP_W + sl.stop] = dvf_ref[:, sl].astype(BF)
        gq_acc[...] += dgq
        gk_acc[...] += dgk
        row = lax.broadcasted_iota(jnp.int32, (tm, tm), 0)
        col = lax.broadcasted_iota(jnp.int32, (tm, tm), 1)
        dlf = _dot_exact((col >= row).astype(BF), df_ref[...]) + carry[0:1, :]
        carry[...] = jnp.broadcast_to(dlf[0:1, :], carry.shape)
        lane = lax.broadcasted_iota(jnp.int32, (tm, LANES), 1)
        dz = jnp.where(lane < N_HEADS, dlf / (1.0 + jnp.exp(z_ref[...])), 0.0)
        db_ref[...] += jnp.sum(dz, axis=0, keepdims=True)
        dz_bf = dz.astype(BF)
        dz_ref[...] = dz_bf
        dn1 = jnp.dot(dz_bf, wf_ref[...], preferred_element_type=F32)
        for sec in range(MAIN_W // GROUP_W):
            sl = slice(sec * GROUP_W, (sec + 1) * GROUP_W)
            dn1 = dn1 + jnp.dot(dproj_ref[:, sl], wm_ref[sl, :], preferred_element_type=F32)
        dx, dg = _rms_bwd(x_ref[...], g_ref[...], dn1)
        dx_ref[...] = dh1_ref[...] + dx
        dg_ref[...] += dg

        @pl.when(i == n_t - 1)
        def _():
            dgq_ref[...] = gq_acc[:, :HEAD_DIM] + gq_acc[:, HEAD_DIM:]
            dgk_ref[...] = gk_acc[:, :HEAD_DIM] + gk_acc[:, HEAD_DIM:]

    row_spec = lambda w, col=0: pl.BlockSpec((tm, w), lambda i: (n_t - 1 - i, col))
    full = lambda a: pl.BlockSpec(a.shape, lambda i: (0,) * a.ndim)
    acc = lambda r, c: pl.BlockSpec((r, c), lambda i: (0, 0))
    return pl.pallas_call(
        body, name="in_proj_bwd", grid=(n_t,),
        out_shape=(jax.ShapeDtypeStruct((t_len, MAIN_W), BF), jax.ShapeDtypeStruct((t_len, LANES), BF),
                   jax.ShapeDtypeStruct((t_len, D_MODEL), F32), jax.ShapeDtypeStruct((1, D_MODEL), F32),
                   jax.ShapeDtypeStruct((1, HEAD_DIM), F32), jax.ShapeDtypeStruct((1, HEAD_DIM), F32),
                   jax.ShapeDtypeStruct((1, LANES), F32)),
        in_specs=[row_spec(D_MODEL), full(g_mix), row_spec(D_MODEL)] + [row_spec(GROUP_W)] * 7
        + [row_spec(LANES), row_spec(GROUP_W, 4), row_spec(GROUP_W, 5), row_spec(LANES), row_spec(LANES), row_spec(LANES),
           full(gq_t), full(gk_t), *_w_in_specs()],
        out_specs=(row_spec(MAIN_W), row_spec(LANES), row_spec(D_MODEL), acc(1, D_MODEL), acc(1, HEAD_DIM), acc(1, HEAD_DIM),
                   acc(1, LANES)),
        scratch_shapes=[pltpu.VMEM((8, LANES), F32), pltpu.VMEM((1, LANES), F32), pltpu.VMEM((1, LANES), F32)],
        compiler_params=_cparams(("arbitrary",)),
    )(x, g_mix, dh1, dq_r, dk_r, dv_r, drg, dq_f, dk_f, dv_f, df_col, proj, proj, z, cos_t, sin_t, gq_t, gk_t, w_in_t, w_in_t)


def _matmul_tn(a, b, name, bk=1024):
    t_len, m = a.shape
    n = b.shape[1]
    bm = m if m <= TN_MAX_ROWS else m // 2
    bk = min(bk, t_len)

    def body(a_ref, b_ref, o_ref):
        @pl.when(pl.program_id(1) == 0)
        def _():
            o_ref[...] = jnp.zeros_like(o_ref)

        o_ref[...] += _dot_tn(a_ref[...], b_ref[...])

    return pl.pallas_call(
        body, name=name, grid=(m // bm, t_len // bk),
        out_shape=jax.ShapeDtypeStruct((m, n), F32),
        in_specs=[pl.BlockSpec((bk, bm), lambda i, k: (k, i)), pl.BlockSpec((bk, n), lambda i, k: (k, 0))],
        out_specs=pl.BlockSpec((bm, n), lambda i, k: (i, 0)),
        compiler_params=_cparams(("arbitrary", "arbitrary")),
    )(a, b)


def _place():
    x, y, c = lax.axis_index("x"), lax.axis_index("y"), lax.axis_index("c")
    chips = [(1 - x, y), (x, 1 - y), (1 - x, 1 - y)]
    return x, y, c, chips


def _row_chunks(rows, limit):
    step = max(d for d in range(16, min(rows, limit) + 1, 16) if rows % d == 0)
    return [slice(i, i + step) for i in range(0, rows, step)]


ICI_CHUNK_ROWS = 256
D2D_CHUNK_ROWS = 256


def _gather_phase(phase, ins, outs, send_sems, recv_sems):
    x, y, c, chips = _place()
    me_chip = 2 * x + y
    sibling = (x, y, 1 - c)

    def copy(w, k, slot, half, to, rows=slice(None), src=None):
        dst = outs[w].at[slot, half, rows]
        return pltpu.make_async_remote_copy(src_ref=dst if src is None else src, dst_ref=dst,
                                            send_sem=send_sems.at[w, k], recv_sem=recv_sems.at[w, k],
                                            device_id=to, device_id_type=MESH)

    for w in range(len(ins)):
        for j, (px, py) in enumerate(chips):
            if phase == 0:
                for rows in _row_chunks(ins[w].shape[1], ICI_CHUNK_ROWS):
                    copy(w, j, me_chip, c, (px, py, c), rows, src=ins[w].at[c, rows]).start()
            elif phase == 1:
                copy(w, j, 2 * px + py, c, (x, y, c)).wait_recv()
                for rows in _row_chunks(ins[w].shape[1], D2D_CHUNK_ROWS):
                    copy(w, 3 + j, 2 * px + py, c, sibling, rows).start()
            else:
                copy(w, 3 + j, 2 * px + py, 1 - c, (x, y, c)).wait_recv()
                copy(w, j, me_chip, c, (px, py, c), src=ins[w].at[c]).wait_send()
                copy(w, 3 + j, 2 * px + py, c, sibling).wait_send()


def _gather_scratch(n_w):
    return [pltpu.SemaphoreType.DMA((n_w, 6)), pltpu.SemaphoreType.DMA((n_w, 6))]


def _all_gather_weights(shards):
    n_w = len(shards)

    def body(*refs):
        for phase in range(3):
            _gather_phase(phase, refs[:n_w], refs[n_w:2 * n_w], *refs[2 * n_w:])

    return pl.pallas_call(
        body, name="all_gather_weights",
        out_shape=tuple(jax.ShapeDtypeStruct((4,) + s.shape, s.dtype) for s in shards),
        in_specs=[ANY] * n_w, out_specs=(ANY,) * n_w, scratch_shapes=_gather_scratch(n_w),
    )(*shards)


def _exchange_phase(phase, ins, theirs, send_sems, recv_sems):
    x, y, c, _ = _place()

    def remote(w, k=slice(None), rows=slice(None)):
        return pltpu.make_async_remote_copy(src_ref=ins[w].at[k, 1 - c, rows], dst_ref=theirs[w].at[k, rows],
                                            send_sem=send_sems.at[w], recv_sem=recv_sems.at[w], device_id=(x, y, 1 - c),
                                            device_id_type=MESH)

    for w in range(len(ins)):
        if phase == 0:
            for k in range(4):
                for rows in _row_chunks(ins[w].shape[2], D2D_CHUNK_ROWS):
                    remote(w, k, rows).start()
        else:
            remote(w).wait()


def _exchange_scratch(n_w):
    return [pltpu.SemaphoreType.DMA((n_w,)), pltpu.SemaphoreType.DMA((n_w,))]


def _exchange_out_shapes(grads):
    return tuple(jax.ShapeDtypeStruct((4,) + g.shape[2:], g.dtype) for g in grads)


def _add_pairs(part, theirs, name, halves):
    _, _, r, c = part.shape
    rb = 64 if r % 64 == 0 else r
    n_w = len(halves)
    n_steps = r // rb

    def body(*refs):
        a_ref, b_ref = refs[:2]
        own_ref, ob_ref = refs[2 + n_w:4 + n_w]
        comm = (refs[2:2 + n_w], refs[4 + n_w:4 + 2 * n_w]) + tuple(refs[4 + 2 * n_w:])
        step = pl.program_id(0)

        @pl.when(step == 0)
        def _():
            _share_phase(0, *comm)

        my_chip = 2 * lax.axis_index("x") + lax.axis_index("y")
        ob_ref[...] = (a_ref[...] + b_ref[...]).astype(BF)
        own_ref[...] = a_ref[my_chip] + b_ref[my_chip]

        @pl.when(step == n_steps - 1)
        def _():
            _share_phase(1, *comm)

    spec = pl.BlockSpec((4, rb, c), lambda i: (0, i, 0))
    flat = pl.pallas_call(
        body, name=name, grid=(n_steps,),
        out_shape=(jax.ShapeDtypeStruct((r, c), F32), jax.ShapeDtypeStruct((4, r, c), BF)) + _share_out_shapes(halves),
        in_specs=[pl.BlockSpec((4, None, rb, c), lambda i: (0, lax.axis_index("c"), i, 0)), spec] + [ANY] * n_w,
        out_specs=(pl.BlockSpec((rb, c), lambda i: (i, 0)), spec) + (ANY,) * n_w,
        scratch_shapes=_share_scratch(n_w), compiler_params=_cparams(("arbitrary",)),
    )(part, theirs, *halves)
    return (flat[0], flat[1]), list(flat[2:])


def _scatter_phase(phase, bfs, got, send_sems, recv_sems):
    x, y, c, chips = _place()

    def remote(w, j, px, py, rows=slice(None)):
        return pltpu.make_async_remote_copy(src_ref=bfs[w].at[2 * px + py, rows], dst_ref=got[w].at[j, rows],
                                            send_sem=send_sems.at[w, j], recv_sem=recv_sems.at[w, j], device_id=(px, py, c),
                                            device_id_type=MESH)

    for w in range(len(bfs)):
        for j, (px, py) in enumerate(chips):
            if phase == 0:
                for rows in _row_chunks(bfs[w].shape[1], ICI_CHUNK_ROWS):
                    remote(w, j, px, py, rows).start()
            else:
                remote(w, j, px, py).wait()


def _scatter_scratch(n_w):
    return [pltpu.SemaphoreType.DMA((n_w, 3)), pltpu.SemaphoreType.DMA((n_w, 3))]


def _scatter_out_shapes(sums_bf16):
    return tuple(jax.ShapeDtypeStruct((3,) + s.shape[1:], BF) for s in sums_bf16)


def _add_received(own, got, name):
    r, c = own.shape
    rb = 128 if r % 128 == 0 else r

    def body(o_ref, g_ref, out_ref):
        out_ref[...] = ((o_ref[...] + g_ref[0].astype(F32)) + g_ref[1].astype(F32)) + g_ref[2].astype(F32)

    return pl.pallas_call(
        body, name=name, grid=(r // rb,), out_shape=jax.ShapeDtypeStruct((r, c), F32),
        in_specs=[pl.BlockSpec((rb, c), lambda i: (i, 0)), pl.BlockSpec((3, rb, c), lambda i: (0, i, 0))],
        out_specs=pl.BlockSpec((rb, c), lambda i: (i, 0)), compiler_params=_cparams(("arbitrary",)),
    )(own, got)


def _share_phase(phase, ins, outs, send_sems, recv_sems):
    x, y, c, _ = _place()

    def remote(w, rows=slice(None)):
        return pltpu.make_async_remote_copy(src_ref=ins[w].at[rows], dst_ref=outs[w].at[c, rows], send_sem=send_sems.at[w],
                                            recv_sem=recv_sems.at[w], device_id=(x, y, 1 - c), device_id_type=MESH)

    for w in range(len(ins)):
        if phase == 0:
            for rows in _row_chunks(ins[w].shape[0], D2D_CHUNK_ROWS):
                remote(w, rows).start()
        else:
            remote(w).wait()


def _share_scratch(n_w):
    return [pltpu.SemaphoreType.DMA((n_w,)), pltpu.SemaphoreType.DMA((n_w,))]


def _share_out_shapes(halves):
    return tuple(jax.ShapeDtypeStruct((2,) + h.shape, h.dtype) for h in halves)


def _share_with_sibling(halves):
    n_w = len(halves)

    def body(*refs):
        for phase in range(2):
            _share_phase(phase, refs[:n_w], refs[n_w:2 * n_w], *refs[2 * n_w:])

    return pl.pallas_call(
        body, name="share_with_sibling", out_shape=_share_out_shapes(halves),
        in_specs=[ANY] * n_w, out_specs=(ANY,) * n_w, scratch_shapes=_share_scratch(n_w),
    )(*halves)


def _small_phase(phase, p_ref, out_ref, slots, send_sems, recv_sems):
    x, y, cc, _ = _place()
    me = 4 * x + 2 * y + cc
    copies = []
    for k in range(1, 8):
        dx, dy, dc = (k >> 2) & 1, (k >> 1) & 1, k & 1
        to = (1 - x if dx else x, 1 - y if dy else y, 1 - cc if dc else cc)
        copies.append(pltpu.make_async_remote_copy(src_ref=p_ref, dst_ref=slots.at[me], send_sem=send_sems.at[k - 1],
                                                   recv_sem=recv_sems.at[k - 1], device_id=to, device_id_type=MESH))
    if phase == 0:
        slots[me] = p_ref[...]
        for cp in copies:
            cp.start()
    else:
        for cp in copies:
            cp.wait()
        total = slots[0]
        for d in range(1, 8):
            total = total + slots[d]
        out_ref[...] = total


def _adamw_update(w_ref, g_ref, m_ref, v_ref, d_ref, nm_ref, nv_ref):
    gv = g_ref[...]
    nm = ADAM_B1 * m_ref[...] + (1.0 - ADAM_B1) * gv
    nv = ADAM_B2 * v_ref[...] + (1.0 - ADAM_B2) * (gv * gv)
    nm_ref[...] = nm
    nv_ref[...] = nv
    m_hat = nm / (1.0 - ADAM_B1 ** ADAM_STEP)
    v_hat = nv / (1.0 - ADAM_B2 ** ADAM_STEP)
    d_ref[...] = -ADAM_LR * (m_hat / (jnp.sqrt(v_hat) + ADAM_EPS) + ADAM_WD * w_ref[...])


def _adamw_many(ws, gs, ms, vs, sums, pack):
    n_a, n_w = len(ws), len(sums)
    n_steps = ADAM_STEPS
    specs = [pl.BlockSpec((w.shape[0] // n_steps, w.shape[1]), lambda i: (i, 0)) for w in ws]
    pack_spec = pl.BlockSpec(pack.shape, lambda i: (0, 0))

    def body(*refs):
        ins = refs[:4 * n_a]
        p_ref = refs[4 * n_a + n_w]
        first_out = 4 * n_a + n_w + 1
        outs = refs[first_out:first_out + 3 * n_a]
        total_ref = refs[first_out + 3 * n_a + n_w]
        scratch = refs[first_out + 3 * n_a + n_w + 1:]
        scatter = (refs[4 * n_a:4 * n_a + n_w], refs[first_out + 3 * n_a:first_out + 3 * n_a + n_w]) + tuple(scratch[:2])
        small = (p_ref, total_ref) + tuple(scratch[2:])
        step = pl.program_id(0)

        @pl.when(step == 0)
        def _():
            _scatter_phase(0, *scatter)
            _small_phase(0, *small)

        for a in range(n_a):
            _adamw_update(*(ins[k * n_a + a] for k in range(4)), *(outs[3 * a + k] for k in range(3)))

        @pl.when(step == n_steps - 1)
        def _():
            _scatter_phase(1, *scatter)
            _small_phase(1, *small)

    flat = pl.pallas_call(
        body, name="adamw_late", grid=(n_steps,),
        out_shape=tuple(jax.ShapeDtypeStruct(w.shape, F32) for w in ws for _ in range(3)) + _scatter_out_shapes(sums)
        + (jax.ShapeDtypeStruct(pack.shape, F32),),
        in_specs=specs * 4 + [ANY] * n_w + [pack_spec],
        out_specs=tuple(s for s in specs for _ in range(3)) + (ANY,) * n_w + (pack_spec,),
        scratch_shapes=_scatter_scratch(n_w) + [pltpu.VMEM((8,) + pack.shape, F32), pltpu.SemaphoreType.DMA((7,)),
                                                pltpu.SemaphoreType.DMA((7,))],
        compiler_params=_cparams(("arbitrary",)),
    )(*ws, *gs, *ms, *vs, *sums, pack)
    return [tuple(flat[3 * a:3 * a + 3]) for a in range(n_a)] + list(flat[3 * n_a:])


def _adamw(w, g, m, v, name):
    r, c = w.shape
    rb, cb = (128, c) if r % 128 == 0 else (r, LANES if (r % 8 and c % LANES == 0) else c)

    def body(*refs):
        _adamw_update(*refs)

    spec = pl.BlockSpec((rb, cb), lambda i, j: (i, j))
    return pl.pallas_call(
        body, name=name, grid=(r // rb, c // cb), out_shape=(jax.ShapeDtypeStruct((r, c), F32),) * 3,
        in_specs=[spec] * 4, out_specs=(spec,) * 3, compiler_params=_cparams(("arbitrary", "arbitrary")),
    )(w, g, m, v)


def _rope_tables(t_len):
    inv_freq = ROPE_BASE ** (-jnp.arange(0, HEAD_DIM, 2, dtype=F32) / HEAD_DIM)
    ang = jnp.arange(t_len, dtype=F32)[:, None] * inv_freq[None, :]
    cos, sin = jnp.cos(ang), jnp.sin(ang)
    cos_t = jnp.concatenate([cos, cos, cos, cos], axis=-1)
    sin_t = jnp.concatenate([-sin, sin, -sin, sin], axis=-1)
    return cos_t, sin_t


def _cols_to_shards(dw):
    r, n = dw.shape
    return jnp.transpose(dw.reshape(2, r // 2, 4, n // 4), (2, 0, 1, 3))


def _rows_to_shards(dw):
    r, n = dw.shape
    rows = r // 4
    if rows % SUBLANES == 0:
        padded = _pad_rows(dw.reshape(4, rows, n))
    else:
        window = rows + SUBLANES - rows % SUBLANES
        padded = _pad_rows(jnp.stack([dw[rows * k // SUBLANES * SUBLANES:][:window] for k in range(4)]))
    return padded.reshape(4, 2, padded.shape[1] // 2, n)


def _in_grad_windows(dw_main, dw_tail):
    r, n = dw_main.shape
    rows = (r + IN_W - MAIN_W) // 4
    base = rows // SUBLANES * SUBLANES
    assert 3 * (rows % SUBLANES) < SUBLANES and 4 * base == r and IN_W - MAIN_W == SUBLANES
    padded = -(-(base + SUBLANES) // SHARD_ROW_PAD) * SHARD_ROW_PAD
    last_tile = r // SUBLANES - 1

    def body(a_ref, b_ref, t_ref, o_ref):
        k = pl.program_id(0)
        o_ref[0, :base, :] = a_ref[...]
        o_ref[0, base:base + SUBLANES, :] = jnp.where(k == 3, t_ref[...], b_ref[...])
        o_ref[0, base + SUBLANES:, :] = jnp.zeros((padded - base - SUBLANES, n), F32)

    out = pl.pallas_call(
        body, name="dw_in_windows", grid=(4,), out_shape=jax.ShapeDtypeStruct((4, padded, n), F32),
        in_specs=[pl.BlockSpec((base, n), lambda k: (k, 0)),
                  pl.BlockSpec((SUBLANES, n), lambda k: (jnp.minimum((k + 1) * (base // SUBLANES), last_tile), 0)),
                  pl.BlockSpec((SUBLANES, n), lambda k: (0, 0))],
        out_specs=pl.BlockSpec((1, padded, n), lambda k: (k, 0, 0)), compiler_params=_cparams(("arbitrary",)),
    )(dw_main, dw_main, dw_tail)
    return out.reshape(4, 2, padded // 2, n)


def _shard_row_offset(rows):
    return (rows * (2 * lax.axis_index("x") + lax.axis_index("y"))) % SUBLANES


def _pad_lanes(a):
    extra = -a.shape[-1] % LANES
    return a if extra == 0 else jnp.pad(a, [(0, 0)] * (a.ndim - 1) + [(0, extra)])


def _pad_rows(a):
    rows = a.shape[-2]
    extra = 0 if rows % SHARD_ROW_ALIGN == 0 else -rows % SHARD_ROW_PAD
    return a if extra == 0 else jnp.pad(a, [(0, 0)] * (a.ndim - 2) + [(0, extra), (0, 0)])


def _pad_row(a, width=D_MODEL):
    a = a.reshape(1, -1)
    return jnp.pad(a, ((0, 0), (0, width - a.shape[1])))


def kernel(x, mem, g_mix, w_in, b_forget, g_ret_out, g_fox_q, g_fox_k, w_out, g_xattn, w_xq, w_xkv, g_mem, g_xq, g_xk, w_xo, g_ffn, w_gate, w_up, w_down, loss_target, m_g_mix, m_w_in, m_b_forget, m_g_ret_out, m_g_fox_q, m_g_fox_k, m_w_out, m_g_xattn, m_w_xq, m_w_xkv, m_g_mem, m_g_xq, m_g_xk, m_w_xo, m_g_ffn, m_w_gate, m_w_up, m_w_down, v_g_mix, v_w_in, v_b_forget, v_g_ret_out, v_g_fox_q, v_g_fox_k, v_w_out, v_g_xattn, v_w_xq, v_w_xkv, v_g_mem, v_g_xq, v_g_xk, v_w_xo, v_g_ffn, v_w_gate, v_w_up, v_w_down):
    big = {"w_in": (w_in, m_w_in, v_w_in), "w_out": (w_out, m_w_out, v_w_out), "w_xq": (w_xq, m_w_xq, v_w_xq),
           "w_xkv": (w_xkv, m_w_xkv, v_w_xkv), "w_xo": (w_xo, m_w_xo, v_w_xo), "w_gate": (w_gate, m_w_gate, v_w_gate),
           "w_up": (w_up, m_w_up, v_w_up), "w_down": (w_down, m_w_down, v_w_down)}
    for n in TRANSPOSED:
        big[n] = tuple(jnp.swapaxes(a, 1, 2) for a in big[n])
    shards = {}
    for n in big:
        w = _pad_rows(_pad_lanes(big[n][0][0].astype(BF)))
        shards[n] = w.reshape(2, w.shape[0] // 2, w.shape[1])
    sizes = {n: big[n][0].shape[1:] for n in big}
    w_in_full = _assemble_weight("w_in", _all_gather_weights([shards["w_in"]])[0], shards["w_in"], sizes["w_in"],
                                 tail_rows=MAIN_W + LANES - IN_W)
    small_w ={"g_mix": g_mix, "b_forget": b_forget, "g_ret_out": g_ret_out, "g_fox_q": g_fox_q, "g_fox_k": g_fox_k,
               "g_xattn": g_xattn, "g_mem": g_mem, "g_xq": g_xq, "g_xk": g_xk, "g_ffn": g_ffn}
    m_small = {"g_mix": m_g_mix, "b_forget": m_b_forget, "g_ret_out": m_g_ret_out, "g_fox_q": m_g_fox_q, "g_fox_k": m_g_fox_k,
               "g_xattn": m_g_xattn, "g_mem": m_g_mem, "g_xq": m_g_xq, "g_xk": m_g_xk, "g_ffn": m_g_ffn}
    v_small = {"g_mix": v_g_mix, "b_forget": v_b_forget, "g_ret_out": v_g_ret_out, "g_fox_q": v_g_fox_q, "g_fox_k": v_g_fox_k,
               "g_xattn": v_g_xattn, "g_mem": v_g_mem, "g_xq": v_g_xq, "g_xk": v_g_xk, "g_ffn": v_g_ffn}
    loss_part, grad_x, sums, got, in_parts, small_g = _local_step(x[0], mem[0], loss_target[0], w_in_full, shards, sizes, small_w)
    return _reduce_and_update(big, sums, got, in_parts, small_w, small_g, loss_part, grad_x, m_small, v_small)


def _assemble_rows(g, rows, out_rows):
    n, padded, c = g.shape

    def body(g_ref, o_ref):
        k = pl.program_id(0)
        for i in range(n):
            @pl.when(k == i)
            def _():
                o_ref[i * rows:(i + 1) * rows, :] = g_ref[0, :rows, :]

        @pl.when(k == n - 1)
        def _():
            o_ref[n * rows:, :] = jnp.zeros((out_rows - n * rows, c), o_ref.dtype)

    return pl.pallas_call(
        body, name="assemble_w_in", grid=(n,), out_shape=jax.ShapeDtypeStruct((out_rows, c), g.dtype),
        in_specs=[pl.BlockSpec((1, padded, c), lambda k: (k, 0, 0))], out_specs=pl.BlockSpec((out_rows, c), lambda k: (0, 0)),
        compiler_params=_cparams(("arbitrary",)),
    )(g)


def _assemble_weight(name, gathered, own, size, tail_rows=0):
    rows, width = size
    my_chip = 2 * lax.axis_index("x") + lax.axis_index("y")
    g = lax.dynamic_update_slice(gathered, own[None], (my_chip, 0, 0, 0))
    g = g.reshape(4, 2 * g.shape[2], g.shape[3])
    if tail_rows:
        assert name not in COL_SHARDED and width == g.shape[2]
        return _assemble_rows(g, rows, 4 * rows + tail_rows)
    g = g[:, :rows, :width]
    return jnp.transpose(g, (1, 0, 2)).reshape(rows, 4 * width) if name in COL_SHARDED else g.reshape(4 * rows, width)


def _shard_parts(names, dw):
    return [_pad_lanes(_cols_to_shards(dw[n]) if n in COL_SHARDED else _rows_to_shards(dw[n])) for n in names]


def _add_pairs_many(parts, theirs, name):
    n_a = len(parts)
    n_steps = min(p.shape[2] for p in parts) // 32
    rb = [p.shape[2] // n_steps for p in parts]
    part_specs = [pl.BlockSpec((4, None, r, p.shape[3]), lambda i: (0, lax.axis_index("c"), i, 0)) for p, r in zip(parts, rb)]
    quad_specs = [pl.BlockSpec((4, r, p.shape[3]), lambda i: (0, i, 0)) for p, r in zip(parts, rb)]
    own_specs = [pl.BlockSpec((r, p.shape[3]), lambda i: (i, 0)) for p, r in zip(parts, rb)]

    def body(*refs):
        my_chip = 2 * lax.axis_index("x") + lax.axis_index("y")
        for a in range(n_a):
            a_ref, b_ref, own_ref, ob_ref = refs[a], refs[n_a + a], refs[2 * n_a + a], refs[3 * n_a + a]
            ob_ref[...] = (a_ref[...] + b_ref[...]).astype(BF)
            own_ref[...] = a_ref[my_chip] + b_ref[my_chip]

    flat = pl.pallas_call(
        body, name=name, grid=(n_steps,),
        out_shape=tuple(jax.ShapeDtypeStruct(p.shape[2:], F32) for p in parts)
        + tuple(jax.ShapeDtypeStruct((4,) + p.shape[2:], BF) for p in parts),
        in_specs=part_specs + quad_specs, out_specs=tuple(own_specs) + tuple(quad_specs),
        compiler_params=_cparams(("arbitrary",)),
    )(*parts, *theirs)
    return [(flat[a], flat[n_a + a]) for a in range(n_a)]


def _core_sums(parts, theirs):
    out = [None] * len(parts)
    for tag, pick in (("a", lambda p: p.shape[2] % LANES == 0), ("b", lambda p: p.shape[2] % LANES != 0)):
        idx = [i for i, p in enumerate(parts) if pick(p)]
        for i, res in zip(idx, _add_pairs_many([parts[i] for i in idx], [theirs[i] for i in idx], f"core_sum_late_{tag}")):
            out[i] = res
    return out


def _local_step(xs, mems, tgt, w_in_full, shards, sizes, small_w):
    g_mix, b_forget, g_ret_out, g_fox_q, g_fox_k = (small_w[n] for n in ("g_mix", "b_forget", "g_ret_out", "g_fox_q", "g_fox_k"))
    g_xattn, g_mem, g_xq, g_xk, g_ffn = (small_w[n] for n in ("g_xattn", "g_mem", "g_xq", "g_xk", "g_ffn"))
    w_in_t = w_in_full
    t_len = xs.shape[0]
    cos_t, sin_t = _rope_tables(t_len)
    tables = _decay_tables(min(RET_BLOCK, t_len))
    gq_t = jnp.concatenate([g_fox_q, g_fox_q], axis=-1)
    gk_t = jnp.concatenate([g_fox_k, g_fox_k], axis=-1)
    b_pad = _pad_row(b_forget, LANES)
    g_ret = g_ret_out.reshape(N_HEADS // 2, 1, LANES)

    n1, proj, rq, rk, q_aug, k_aug, z = _in_proj_fwd(xs, g_mix, w_in_t, b_pad, cos_t, sin_t, gq_t, gk_t)
    raw, mix_r, states = _retention_fwd(rq, rk, proj, g_ret, tables)
    mix_f, o32, lse, *gathered = _fox_fwd(q_aug, k_aug, proj, [shards[n] for n in LATE])
    full = {n: _assemble_weight(n, g, shards[n], sizes[n]) for n, g in zip(LATE, gathered)}
    memn, kraw, kn, vmem = _mem_kv_fwd(mems, g_mem, full["w_xkv"], g_xk)
    h1, hn2, qx, o_x, h2 = _attn_out_xattn_fwd(xs, mix_r, mix_f, full["w_out"], g_xattn, full["w_xq"], g_xq, kn, vmem, full["w_xo"])
    hn3, gate, up, act, dh3, loss_part = _ffn_loss_fwd(h2, g_ffn, full["w_gate"], full["w_up"], full["w_down"], tgt)

    dgate, dup, dh2, dg_ffn = _ffn_bwd(dh3, gate, up, h2, g_ffn, full["w_gate"], full["w_up"], full["w_down"])
    dqx, dh1, dmr, dmf, dkn, dvm, dg_xattn, dg_xq = _attn_out_xattn_bwd(dh2, h1, qx, kn, vmem, full["w_xo"], full["w_xq"],
                                                                      full["w_out"], g_xattn, g_xq)
    dw_xkv, dg_mem, dg_xk = _mem_kv_bwd(dkn, dvm, kraw, mems, memn, g_mem, g_xk, full["w_xkv"])
    dw_gu = _matmul_tn_pair(dgate, dup, hn3, "dw_gate_up")
    dw = {
        "w_out": _matmul_tn_pair(mix_r, mix_f, dh1, "dw_out").reshape(D_MODEL, D_MODEL),
        "w_xq": _matmul_tn(hn2, dqx, "dw_xq"),
        "w_xkv": dw_xkv,
        "w_xo": _matmul_tn(o_x, dh2, "dw_xo"),
        "w_gate": dw_gu[0],
        "w_up": dw_gu[1],
        "w_down": _matmul_tn(act, dh3, "dw_down"),
    }
    late_parts = _shard_parts(LATE, dw)
    dq_r, dk_r, dv_r, drg, dg_ret, *late_theirs = _retention_bwd(dmr, raw, proj, g_ret, rq, rk, states, tables, late_parts)
    late_sums = _core_sums(late_parts, late_theirs)
    dq_f, dk_f, dv_f, df, *late_got = _fox_bwd(q_aug, k_aug, proj, dmf, o32, lse, [s[1] for s in late_sums])
    df_col = jnp.pad(jnp.transpose(df, (1, 0, 2)).reshape(t_len, N_HEADS), ((0, 0), (0, LANES - N_HEADS)))
    dproj, dz, grad_x, dg_mix, dg_fq, dg_fk, db = _in_proj_bwd(xs, g_mix, dh1, dq_r, dk_r, dv_r, drg, dq_f, dk_f, dv_f, df_col,
                                                              proj, z, cos_t, sin_t, gq_t, gk_t, w_in_t)

    in_parts = [_in_grad_windows(_matmul_tn(dproj, n1, "dw_in_main"), _matmul_tn(dz, n1, "dw_in_ff"))]
    sums = {n: s[0] for n, s in zip(LATE, late_sums)}
    got = dict(zip(LATE, late_got))
    small_g = {"g_mix": dg_mix, "b_forget": db[:, :N_HEADS], "g_ret_out": dg_ret, "g_fox_q": dg_fq, "g_fox_k": dg_fk,
               "g_xattn": dg_xattn, "g_mem": dg_mem, "g_xq": dg_xq, "g_xk": dg_xk, "g_ffn": dg_ffn}
    return loss_part, grad_x, sums, got, in_parts, small_g


def _add_received_many(owns, gots, parts):
    n_a, n_w = len(owns), len(parts)
    n_steps = CHIP_SUM_STEPS
    own_specs = [pl.BlockSpec((o.shape[0] // n_steps, o.shape[1]), lambda i: (i, 0)) for o in owns]
    got_specs = [pl.BlockSpec((3, o.shape[0] // n_steps, o.shape[1]), lambda i: (0, i, 0)) for o in owns]

    def body(*refs):
        first_out = 2 * n_a + n_w
        comm = (refs[2 * n_a:first_out], refs[first_out + n_a:first_out + n_a + n_w]) + tuple(refs[first_out + n_a + n_w:])
        step = pl.program_id(0)

        @pl.when(step == 0)
        def _():
            _exchange_phase(0, *comm)

        for a in range(n_a):
            o_ref, g_ref, out_ref = refs[a], refs[n_a + a], refs[first_out + a]
            out_ref[...] = ((o_ref[...] + g_ref[0].astype(F32)) + g_ref[1].astype(F32)) + g_ref[2].astype(F32)

        @pl.when(step == n_steps - 1)
        def _():
            _exchange_phase(1, *comm)

    flat = pl.pallas_call(
        body, name="chip_sum_late", grid=(n_steps,),
        out_shape=tuple(jax.ShapeDtypeStruct(o.shape, F32) for o in owns) + _exchange_out_shapes(parts),
        in_specs=own_specs + got_specs + [ANY] * n_w, out_specs=tuple(own_specs) + (ANY,) * n_w,
        scratch_shapes=_exchange_scratch(n_w), compiler_params=_cparams(("arbitrary",)),
    )(*owns, *gots, *parts)
    return list(flat[:n_a]), list(flat[n_a:])


def _final_grads(names, big, finals, shared):
    my_core = lax.axis_index("c")
    out = {}
    for n, s, fin in zip(names, shared, finals):
        s = lax.dynamic_update_slice(s, fin[None], (my_core, 0, 0))
        s = s.reshape(2 * s.shape[1], s.shape[2])
        rows, width = big[n][0].shape[1:]
        out[n] = s[:rows, :width] if rows % SUBLANES == 0 else lax.dynamic_slice(s, (_shard_row_offset(rows), 0), (rows, width))
    return out


def _reduce_and_update(big, sums, got, in_parts, small_w, small_g, loss_part, grad_x, m_small, v_small):
    small_names = list(small_w)
    pad_rows = SMALL_ROWS - len(small_names) - 1
    stack = lambda d: jnp.concatenate([_pad_row(d[n]) for n in small_names] + [jnp.zeros((pad_rows + 1, D_MODEL), F32)], axis=0)
    g_pack = jnp.concatenate([_pad_row(small_g[n]) for n in small_names] + [_pad_row(loss_part[0:1, 0:1])]
                             + [jnp.zeros((pad_rows, D_MODEL), F32)], axis=0)
    late_finals, in_theirs = _add_received_many([sums[n] for n in LATE], [got[n] for n in LATE], in_parts)
    (in_own, in_bf), late_shared = _add_pairs(in_parts[0], in_theirs[0], "core_sum_w_in", late_finals)
    grads = _final_grads(LATE, big, late_finals, late_shared)
    *late_updates, in_got, g_tot = _adamw_many([big[n][0][0] for n in LATE], [grads[n] for n in LATE], [big[n][1][0] for n in LATE],
                                               [big[n][2][0] for n in LATE], [in_bf], g_pack)
    updates = dict(zip(LATE, late_updates))
    in_final = [_add_received(in_own, in_got, "chip_sum_w_in")]
    grads.update(_final_grads(("w_in",), big, in_final, _share_with_sibling(in_final)))
    updates["w_in"] = _adamw(big["w_in"][0][0], grads["w_in"], big["w_in"][1][0], big["w_in"][2][0], "adamw_w_in")
    deltas, new_m, new_v = {}, {}, {}
    for n in big:
        restore = (lambda a: jnp.swapaxes(a[None], 1, 2)) if n in TRANSPOSED else (lambda a: a[None])
        grads[n] = restore(grads[n])
        deltas[n], new_m[n], new_v[n] = (restore(a) for a in updates[n])

    d_s, m_s, v_s = _adamw(stack(small_w), g_tot, stack(m_small), stack(v_small), "adamw_small")
    for i, n in enumerate(small_names):
        shape = small_w[n].shape
        size = int(np.prod(shape))
        grads[n] = g_tot[i, :size].reshape(shape)
        deltas[n], new_m[n], new_v[n] = d_s[i, :size].reshape(shape), m_s[i, :size].reshape(shape), v_s[i, :size].reshape(shape)
    loss = g_tot[len(small_names), 0]

    order = ["g_mix", "w_in", "b_forget", "g_ret_out", "g_fox_q", "g_fox_k", "w_out", "g_xattn", "w_xq", "w_xkv", "g_mem", "g_xq",
             "g_xk", "w_xo", "g_ffn", "w_gate", "w_up", "w_down"]
    return (loss, grad_x[None], *[grads[n] for n in order], *[deltas[n] for n in order], *[new_m[n] for n in order],
            *[new_v[n] for n in order])
```
